```python
import math
import jax, jax.numpy as jnp
from jax import lax
import numpy as np

D_MODEL = 1024
BATCH = 8
SEQ = 4096
DEPTH = 1

ATTN_HEAD_DIM = 64
ATTN_Q_HEADS = 16
ATTN_KV_HEADS = 2
ATTN_GROUP = ATTN_Q_HEADS // ATTN_KV_HEADS
ATTN_WIDTH = ATTN_Q_HEADS * ATTN_HEAD_DIM
ATTN_KV_WIDTH = ATTN_KV_HEADS * ATTN_HEAD_DIM
WINDOW = 128
ATTN_BLOCK = 128
HGRN_EXPAND = 128
HGRN_HEADS = D_MODEL // HGRN_EXPAND
HGRN_K = HGRN_EXPAND
HGRN_V = D_MODEL // HGRN_HEADS
HGRN_KEY_WIDTH = HGRN_HEADS * HGRN_K
HGRN_WIDTH = HGRN_HEADS * HGRN_V
CHUNK = 64
FFN_HIDDEN = -(-(8 * D_MODEL) // (3 * 256)) * 256
EPS = 1e-6
NEG_INF = -1e30
IN_SPLITS = (ATTN_WIDTH, ATTN_KV_WIDTH, ATTN_KV_WIDTH,
             HGRN_KEY_WIDTH, HGRN_KEY_WIDTH, HGRN_WIDTH, HGRN_WIDTH,
             D_MODEL, D_MODEL)
IN_WIDTH = sum(IN_SPLITS)

kernel_name = "hybrid_swa_sink_hgrn2_gated_block"


def rmsnorm(x, g):
    xf = x.astype(jnp.float32)
    y = xf * lax.rsqrt(jnp.mean(xf * xf, axis=-1, keepdims=True) + EPS)
    return (y * g.astype(jnp.float32)).astype(x.dtype)


def split_columns(p):
    idx = np.cumsum(np.array(IN_SPLITS))[:-1].tolist()
    return jnp.split(p, idx, axis=-1)


def sliding_window_sink_attention(q, k, v, sinks):
    B, S = q.shape[0], q.shape[1]
    nb = S // ATTN_BLOCK
    qb = q.reshape(B, nb, ATTN_BLOCK, ATTN_KV_HEADS, ATTN_GROUP, ATTN_HEAD_DIM)
    kb = k.reshape(B, nb, ATTN_BLOCK, ATTN_KV_HEADS, ATTN_HEAD_DIM)
    vb = v.reshape(B, nb, ATTN_BLOCK, ATTN_KV_HEADS, ATTN_HEAD_DIM)

    def with_prev(t):
        prev = jnp.concatenate([jnp.zeros_like(t[:, :1]), t[:, :-1]], axis=1)
        return jnp.concatenate([prev, t], axis=2)

    kw, vw = with_prev(kb), with_prev(vb)
    scale = 1.0 / math.sqrt(ATTN_HEAD_DIM)
    scores = jnp.einsum('bnqhgd,bnkhd->bnhgqk', qb, kw).astype(jnp.float32) * scale
    qi = jnp.arange(ATTN_BLOCK)[:, None]
    kj = jnp.arange(2 * ATTN_BLOCK)[None, :]
    rel = qi + ATTN_BLOCK - kj
    key_pos = jnp.arange(nb)[:, None, None] * ATTN_BLOCK - ATTN_BLOCK + kj[None]
    valid = (rel >= 0)[None] & (rel < WINDOW)[None] & (key_pos >= 0)
    scores = jnp.where(valid[None, :, None, None], scores, NEG_INF)
    sink = jnp.broadcast_to(
        sinks.astype(jnp.float32).reshape(1, 1, ATTN_KV_HEADS, ATTN_GROUP, 1, 1),
        scores.shape[:-1] + (1,))
    probs = jax.nn.softmax(jnp.concatenate([scores, sink], axis=-1), axis=-1)[..., :-1]
    out = jnp.einsum('bnhgqk,bnkhd->bnqhgd', probs.astype(v.dtype), vw)
    return out.reshape(B, S, ATTN_WIDTH)


def hgrn2_chunkwise(q, k, v, log_f):
    B, S, H, K = q.shape
    V = v.shape[-1]
    n = S // CHUNK

    def to_chunks(t):
        return t.reshape(B, n, CHUNK, H, t.shape[-1]).transpose(1, 0, 3, 2, 4)

    causal = jnp.tril(jnp.ones((CHUNK, CHUNK), dtype=bool))

    def step(state, inp):
        qc, kc, vc, gc = inp
        b = jnp.cumsum(gc, axis=2)
        b_mid = b[:, :, CHUNK // 2 - 1:CHUNK // 2]
        b_last = b[:, :, -1:]
        a = jnp.einsum('bhck,bhsk->bhcs', qc * jnp.exp(b - b_mid), kc * jnp.exp(b_mid - b))
        a = jnp.where(causal, a, 0.0)
        o = (jnp.einsum('bhcs,bhsv->bhcv', a, vc)
             + jnp.einsum('bhck,bhkv->bhcv', qc * jnp.exp(b), state))
        state = (jnp.exp(b_last)[:, :, 0, :, None] * state
                 + jnp.einsum('bhsk,bhsv->bhkv', kc * jnp.exp(b_last - b), vc))
        return state, o

    s0 = jnp.zeros((B, H, K, V), jnp.float32)
    _, o = lax.scan(step, s0, (to_chunks(q), to_chunks(k), to_chunks(v), to_chunks(log_f)))
    return o.transpose(1, 0, 3, 2, 4).reshape(B, S, H, V)


def hgrn2_branch(hq, hf, hi, hg, lb, norm_g):
    B, S = hq.shape[0], hq.shape[1]
    fp = hf.astype(jnp.float32)
    log_f = jnp.log(lb + (1.0 - lb) * jax.nn.sigmoid(fp))
    k = (1.0 - lb) * jax.nn.sigmoid(-fp)
    q = jax.nn.silu(hq.astype(jnp.float32))
    shp_k = (B, S, HGRN_HEADS, HGRN_K)
    o = hgrn2_chunkwise(q.reshape(shp_k), k.reshape(shp_k),
                        hi.astype(jnp.float32).reshape(B, S, HGRN_HEADS, HGRN_V),
                        log_f.reshape(shp_k))
    o = o * lax.rsqrt(jnp.mean(o * o, axis=-1, keepdims=True) + EPS)
    o = o.reshape(B, S, HGRN_WIDTH) * norm_g.astype(jnp.float32)
    o = o * jax.nn.sigmoid(hg.astype(jnp.float32))
    return o.astype(hq.dtype)


def _fwd_setup_inputs(seed: int = 0) -> dict:
    key = jax.random.key(seed)
    ks = jax.random.split(key, 20)
    D, F = D_MODEL, FFN_HIDDEN
    nrm = lambda k, shape, fan_in: jax.random.normal(k, shape, jnp.float32) * fan_in ** -0.5
    gain = lambda k, shape: 1.0 + 0.05 * jax.random.normal(k, shape, jnp.float32)
    return {
        "x": jax.random.normal(ks[0], (BATCH, SEQ, D), jnp.float32),
        "norm_mix_g": gain(ks[1], (DEPTH, D)),
        "w_in": nrm(ks[2], (DEPTH, D, IN_WIDTH), D),
        "b_in": 0.01 * jax.random.normal(ks[3], (DEPTH, IN_WIDTH), jnp.float32),
        "attn_sinks": 0.5 * jax.random.normal(ks[4], (DEPTH, ATTN_Q_HEADS), jnp.float32),
        "hgrn_lb_logits": gain(ks[5], (DEPTH + 1, HGRN_KEY_WIDTH)),
        "hgrn_norm_g": gain(ks[6], (DEPTH, HGRN_WIDTH)),
        "w_branch_attn": nrm(ks[7], (DEPTH, ATTN_WIDTH, D), ATTN_WIDTH),
        "w_branch_hgrn": nrm(ks[8], (DEPTH, HGRN_WIDTH, D), HGRN_WIDTH),
        "w_out": nrm(ks[9], (DEPTH, D, D), D),
        "norm_ffn_g": gain(ks[10], (DEPTH, D)),
        "w_ffn_gate": nrm(ks[11], (DEPTH, D, F), D),
        "w_ffn_up": nrm(ks[12], (DEPTH, D, F), D),
        "w_ffn_down": nrm(ks[13], (DEPTH, F, D), F),
        "norm_final_g": gain(ks[14], (D,)),
    }


def _fwd_reference(x, norm_mix_g, w_in, b_in, attn_sinks, hgrn_lb_logits, hgrn_norm_g,
              w_branch_attn, w_branch_hgrn, w_out, norm_ffn_g, w_ffn_gate, w_ffn_up,
              w_ffn_down, norm_final_g):
    lb_all = jnp.cumsum(jax.nn.softmax(hgrn_lb_logits.astype(jnp.float32), axis=0), axis=0)
    h = x
    for layer in range(DEPTH):
        u = rmsnorm(h, norm_mix_g[layer])
        p = jnp.einsum('bsd,de->bse', u, w_in[layer]) + b_in[layer]
        aq, ak, av, hq, hf, hi, hg, gate_a, gate_b = split_columns(p)
        y_attn = sliding_window_sink_attention(aq, ak, av, attn_sinks[layer])
        y_hgrn = hgrn2_branch(hq, hf, hi, hg, lb_all[layer], hgrn_norm_g[layer])
        ya = jnp.einsum('bse,ed->bsd', y_attn, w_branch_attn[layer])
        yb = jnp.einsum('bse,ed->bsd', y_hgrn, w_branch_hgrn[layer])
        merged = jax.nn.sigmoid(gate_a) * ya + jax.nn.sigmoid(gate_b) * yb
        h = h + jnp.einsum('bsd,de->bse', merged, w_out[layer])
        u = rmsnorm(h, norm_ffn_g[layer])
        z = (jax.nn.silu(jnp.einsum('bsd,df->bsf', u, w_ffn_gate[layer]))
             * jnp.einsum('bsd,df->bsf', u, w_ffn_up[layer]))
        h = h + jnp.einsum('bsf,fd->bsd', z, w_ffn_down[layer])
    return rmsnorm(h, norm_final_g)


import jax as _jax
import jax.numpy as _jnp

TWIN_FORMAT = 'train_step'
FWD_PARAMS = ['x', 'norm_mix_g', 'w_in', 'b_in', 'attn_sinks', 'hgrn_lb_logits', 'hgrn_norm_g', 'w_branch_attn', 'w_branch_hgrn', 'w_out', 'norm_ffn_g', 'w_ffn_gate', 'w_ffn_up', 'w_ffn_down', 'norm_final_g']
TWIN_WEIGHTS = ['norm_mix_g', 'w_in', 'b_in', 'attn_sinks', 'hgrn_lb_logits', 'hgrn_norm_g', 'w_branch_attn', 'w_branch_hgrn', 'w_out', 'norm_ffn_g', 'w_ffn_gate', 'w_ffn_up', 'w_ffn_down', 'norm_final_g']
TWIN_DIFF_INPUT = 'x'
TWIN_INPUTS = ['x', 'norm_mix_g', 'w_in', 'b_in', 'attn_sinks', 'hgrn_lb_logits', 'hgrn_norm_g', 'w_branch_attn', 'w_branch_hgrn', 'w_out', 'norm_ffn_g', 'w_ffn_gate', 'w_ffn_up', 'w_ffn_down', 'norm_final_g', 'loss_target', 'm_norm_mix_g', 'm_w_in', 'm_b_in', 'm_attn_sinks', 'm_hgrn_lb_logits', 'm_hgrn_norm_g', 'm_w_branch_attn', 'm_w_branch_hgrn', 'm_w_out', 'm_norm_ffn_g', 'm_w_ffn_gate', 'm_w_ffn_up', 'm_w_ffn_down', 'm_norm_final_g', 'v_norm_mix_g', 'v_w_in', 'v_b_in', 'v_attn_sinks', 'v_hgrn_lb_logits', 'v_hgrn_norm_g', 'v_w_branch_attn', 'v_w_branch_hgrn', 'v_w_out', 'v_norm_ffn_g', 'v_w_ffn_gate', 'v_w_ffn_up', 'v_w_ffn_down', 'v_norm_final_g']
TWIN_OUTPUTS = ['loss', 'grad_x', 'grad_norm_mix_g', 'grad_w_in', 'grad_b_in', 'grad_attn_sinks', 'grad_hgrn_lb_logits', 'grad_hgrn_norm_g', 'grad_w_branch_attn', 'grad_w_branch_hgrn', 'grad_w_out', 'grad_norm_ffn_g', 'grad_w_ffn_gate', 'grad_w_ffn_up', 'grad_w_ffn_down', 'grad_norm_final_g', 'delta_norm_mix_g', 'delta_w_in', 'delta_b_in', 'delta_attn_sinks', 'delta_hgrn_lb_logits', 'delta_hgrn_norm_g', 'delta_w_branch_attn', 'delta_w_branch_hgrn', 'delta_w_out', 'delta_norm_ffn_g', 'delta_w_ffn_gate', 'delta_w_ffn_up', 'delta_w_ffn_down', 'delta_norm_final_g', 'new_m_norm_mix_g', 'new_m_w_in', 'new_m_b_in', 'new_m_attn_sinks', 'new_m_hgrn_lb_logits', 'new_m_hgrn_norm_g', 'new_m_w_branch_attn', 'new_m_w_branch_hgrn', 'new_m_w_out', 'new_m_norm_ffn_g', 'new_m_w_ffn_gate', 'new_m_w_ffn_up', 'new_m_w_ffn_down', 'new_m_norm_final_g', 'new_v_norm_mix_g', 'new_v_w_in', 'new_v_b_in', 'new_v_attn_sinks', 'new_v_hgrn_lb_logits', 'new_v_hgrn_norm_g', 'new_v_w_branch_attn', 'new_v_w_branch_hgrn', 'new_v_w_out', 'new_v_norm_ffn_g', 'new_v_w_ffn_gate', 'new_v_w_ffn_up', 'new_v_w_ffn_down', 'new_v_norm_final_g']
TWIN_LEAF_KINDS = {'loss': 'loss', 'grad_x': 'grad_x', 'grad_norm_mix_g': 'grad_w', 'grad_w_in': 'grad_w', 'grad_b_in': 'grad_w', 'grad_attn_sinks': 'grad_w', 'grad_hgrn_lb_logits': 'grad_w', 'grad_hgrn_norm_g': 'grad_w', 'grad_w_branch_attn': 'grad_w', 'grad_w_branch_hgrn': 'grad_w', 'grad_w_out': 'grad_w', 'grad_norm_ffn_g': 'grad_w', 'grad_w_ffn_gate': 'grad_w', 'grad_w_ffn_up': 'grad_w', 'grad_w_ffn_down': 'grad_w', 'grad_norm_final_g': 'grad_w', 'delta_norm_mix_g': 'delta_w', 'delta_w_in': 'delta_w', 'delta_b_in': 'delta_w', 'delta_attn_sinks': 'delta_w', 'delta_hgrn_lb_logits': 'delta_w', 'delta_hgrn_norm_g': 'delta_w', 'delta_w_branch_attn': 'delta_w', 'delta_w_branch_hgrn': 'delta_w', 'delta_w_out': 'delta_w', 'delta_norm_ffn_g': 'delta_w', 'delta_w_ffn_gate': 'delta_w', 'delta_w_ffn_up': 'delta_w', 'delta_w_ffn_down': 'delta_w', 'delta_norm_final_g': 'delta_w', 'new_m_norm_mix_g': 'new_m', 'new_m_w_in': 'new_m', 'new_m_b_in': 'new_m', 'new_m_attn_sinks': 'new_m', 'new_m_hgrn_lb_logits': 'new_m', 'new_m_hgrn_norm_g': 'new_m', 'new_m_w_branch_attn': 'new_m', 'new_m_w_branch_hgrn': 'new_m', 'new_m_w_out': 'new_m', 'new_m_norm_ffn_g': 'new_m', 'new_m_w_ffn_gate': 'new_m', 'new_m_w_ffn_up': 'new_m', 'new_m_w_ffn_down': 'new_m', 'new_m_norm_final_g': 'new_m', 'new_v_norm_mix_g': 'new_v', 'new_v_w_in': 'new_v', 'new_v_b_in': 'new_v', 'new_v_attn_sinks': 'new_v', 'new_v_hgrn_lb_logits': 'new_v', 'new_v_hgrn_norm_g': 'new_v', 'new_v_w_branch_attn': 'new_v', 'new_v_w_branch_hgrn': 'new_v', 'new_v_w_out': 'new_v', 'new_v_norm_ffn_g': 'new_v', 'new_v_w_ffn_gate': 'new_v', 'new_v_w_ffn_up': 'new_v', 'new_v_w_ffn_down': 'new_v', 'new_v_norm_final_g': 'new_v'}


def _forward(args):
    return _fwd_reference(*[args[k] for k in FWD_PARAMS])


def _output_shape():
    def fwd():
        inp = _fwd_setup_inputs(0)
        return _fwd_reference(*[inp[k] for k in FWD_PARAMS])
    out = _jax.eval_shape(fwd)
    return out.shape, out.dtype

N_MICROBATCH = 1
ADAM_LR = 0.001
ADAM_B1 = 0.9
ADAM_B2 = 0.999
ADAM_EPS = 1e-08
ADAM_WD = 0.01
ADAM_STEP = 10
PER_EXAMPLE_BATCH_AXIS = {'x': 0, 'loss_target': 0}
SHARED_INPUTS = []
_WEIGHT_DTYPES = {'norm_mix_g': _jnp.float32, 'w_in': _jnp.float32, 'b_in': _jnp.float32, 'attn_sinks': _jnp.float32, 'hgrn_lb_logits': _jnp.float32, 'hgrn_norm_g': _jnp.float32, 'w_branch_attn': _jnp.float32, 'w_branch_hgrn': _jnp.float32, 'w_out': _jnp.float32, 'norm_ffn_g': _jnp.float32, 'w_ffn_gate': _jnp.float32, 'w_ffn_up': _jnp.float32, 'w_ffn_down': _jnp.float32, 'norm_final_g': _jnp.float32}
MOMENT_SCALE = {'norm_mix_g': 7.240073e-02, 'w_in': 2.742747e-02, 'b_in': 8.548568e-02, 'attn_sinks': 1.687718e-02, 'hgrn_lb_logits': 5.504632e-03, 'hgrn_norm_g': 5.577721e-02, 'w_branch_attn': 2.108969e-02, 'w_branch_hgrn': 5.727486e-02, 'w_out': 5.999332e-02, 'norm_ffn_g': 1.305579e-01, 'w_ffn_gate': 5.520579e-02, 'w_ffn_up': 5.380992e-02, 'w_ffn_down': 8.976343e-02, 'norm_final_g': 3.207146e+01}


def _to_microbatches(a, axis):
    t = _jnp.moveaxis(a, axis, 0)
    t = t.reshape((N_MICROBATCH, t.shape[0] // N_MICROBATCH) + t.shape[1:])
    return _jnp.moveaxis(t, 1, axis + 1)


def setup_inputs(seed: int = 0) -> dict:
    inp = _fwd_setup_inputs(seed)
    key = _jax.random.fold_in(_jax.random.key(seed), 7919)
    shape, _ = _output_shape()
    out = dict(inp)
    out["loss_target"] = _jax.random.normal(_jax.random.fold_in(key, 0), shape, _jnp.float32)
    for i, name in enumerate(TWIN_WEIGHTS):
        w = inp[name].astype(_jnp.float32)
        if MOMENT_SCALE is None:
            s = _jnp.sqrt(_jnp.mean(_jnp.square(w)) + 1e-30)
        else:
            s = MOMENT_SCALE[name]
        km, kv = _jax.random.split(_jax.random.fold_in(key, i + 1))
        out[name] = w
        out["m_" + name] = s * _jax.random.normal(km, w.shape, _jnp.float32)
        out["v_" + name] = (s * s) * _jax.random.uniform(kv, w.shape, _jnp.float32, 0.5, 1.5)
    if N_MICROBATCH > 1:
        for name, axis in PER_EXAMPLE_BATCH_AXIS.items():
            out[name] = _to_microbatches(out[name], axis)
    return {'x': out['x'], 'norm_mix_g': out['norm_mix_g'], 'w_in': out['w_in'], 'b_in': out['b_in'], 'attn_sinks': out['attn_sinks'], 'hgrn_lb_logits': out['hgrn_lb_logits'], 'hgrn_norm_g': out['hgrn_norm_g'], 'w_branch_attn': out['w_branch_attn'], 'w_branch_hgrn': out['w_branch_hgrn'], 'w_out': out['w_out'], 'norm_ffn_g': out['norm_ffn_g'], 'w_ffn_gate': out['w_ffn_gate'], 'w_ffn_up': out['w_ffn_up'], 'w_ffn_down': out['w_ffn_down'], 'norm_final_g': out['norm_final_g'], 'loss_target': out['loss_target'], 'm_norm_mix_g': out['m_norm_mix_g'], 'm_w_in': out['m_w_in'], 'm_b_in': out['m_b_in'], 'm_attn_sinks': out['m_attn_sinks'], 'm_hgrn_lb_logits': out['m_hgrn_lb_logits'], 'm_hgrn_norm_g': out['m_hgrn_norm_g'], 'm_w_branch_attn': out['m_w_branch_attn'], 'm_w_branch_hgrn': out['m_w_branch_hgrn'], 'm_w_out': out['m_w_out'], 'm_norm_ffn_g': out['m_norm_ffn_g'], 'm_w_ffn_gate': out['m_w_ffn_gate'], 'm_w_ffn_up': out['m_w_ffn_up'], 'm_w_ffn_down': out['m_w_ffn_down'], 'm_norm_final_g': out['m_norm_final_g'], 'v_norm_mix_g': out['v_norm_mix_g'], 'v_w_in': out['v_w_in'], 'v_b_in': out['v_b_in'], 'v_attn_sinks': out['v_attn_sinks'], 'v_hgrn_lb_logits': out['v_hgrn_lb_logits'], 'v_hgrn_norm_g': out['v_hgrn_norm_g'], 'v_w_branch_attn': out['v_w_branch_attn'], 'v_w_branch_hgrn': out['v_w_branch_hgrn'], 'v_w_out': out['v_w_out'], 'v_norm_ffn_g': out['v_norm_ffn_g'], 'v_w_ffn_gate': out['v_w_ffn_gate'], 'v_w_ffn_up': out['v_w_ffn_up'], 'v_w_ffn_down': out['v_w_ffn_down'], 'v_norm_final_g': out['v_norm_final_g']}


def _loss(weights, diff, rest, loss_target):
    with _jax.named_scope("forward"):
        args = {**rest, TWIN_DIFF_INPUT: diff, **{k: w.astype(_WEIGHT_DTYPES[k]) for k, w in weights.items()}}
        y = _forward(args)
    with _jax.named_scope("loss_head"):
        err = _jnp.square(y.astype(_jnp.float32) - loss_target)
        return 0.5 * _jnp.sum(_jnp.mean(err, axis=-1)) if err.ndim else 0.5 * err


def _adamw(w, g, m, v):
    m = ADAM_B1 * m + (1.0 - ADAM_B1) * g
    v = ADAM_B2 * v + (1.0 - ADAM_B2) * _jnp.square(g)
    m_hat = m / (1.0 - ADAM_B1 ** ADAM_STEP)
    v_hat = v / (1.0 - ADAM_B2 ** ADAM_STEP)
    delta = -ADAM_LR * (m_hat / (_jnp.sqrt(v_hat) + ADAM_EPS) + ADAM_WD * w)
    return delta, m, v


def reference(x, norm_mix_g, w_in, b_in, attn_sinks, hgrn_lb_logits, hgrn_norm_g, w_branch_attn, w_branch_hgrn, w_out, norm_ffn_g, w_ffn_gate, w_ffn_up, w_ffn_down, norm_final_g, loss_target, m_norm_mix_g, m_w_in, m_b_in, m_attn_sinks, m_hgrn_lb_logits, m_hgrn_norm_g, m_w_branch_attn, m_w_branch_hgrn, m_w_out, m_norm_ffn_g, m_w_ffn_gate, m_w_ffn_up, m_w_ffn_down, m_norm_final_g, v_norm_mix_g, v_w_in, v_b_in, v_attn_sinks, v_hgrn_lb_logits, v_hgrn_norm_g, v_w_branch_attn, v_w_branch_hgrn, v_w_out, v_norm_ffn_g, v_w_ffn_gate, v_w_ffn_up, v_w_ffn_down, v_norm_final_g):
    given = dict(x=x, norm_mix_g=norm_mix_g, w_in=w_in, b_in=b_in, attn_sinks=attn_sinks, hgrn_lb_logits=hgrn_lb_logits, hgrn_norm_g=hgrn_norm_g, w_branch_attn=w_branch_attn, w_branch_hgrn=w_branch_hgrn, w_out=w_out, norm_ffn_g=norm_ffn_g, w_ffn_gate=w_ffn_gate, w_ffn_up=w_ffn_up, w_ffn_down=w_ffn_down, norm_final_g=norm_final_g, loss_target=loss_target, m_norm_mix_g=m_norm_mix_g, m_w_in=m_w_in, m_b_in=m_b_in, m_attn_sinks=m_attn_sinks, m_hgrn_lb_logits=m_hgrn_lb_logits, m_hgrn_norm_g=m_hgrn_norm_g, m_w_branch_attn=m_w_branch_attn, m_w_branch_hgrn=m_w_branch_hgrn, m_w_out=m_w_out, m_norm_ffn_g=m_norm_ffn_g, m_w_ffn_gate=m_w_ffn_gate, m_w_ffn_up=m_w_ffn_up, m_w_ffn_down=m_w_ffn_down, m_norm_final_g=m_norm_final_g, v_norm_mix_g=v_norm_mix_g, v_w_in=v_w_in, v_b_in=v_b_in, v_attn_sinks=v_attn_sinks, v_hgrn_lb_logits=v_hgrn_lb_logits, v_hgrn_norm_g=v_hgrn_norm_g, v_w_branch_attn=v_w_branch_attn, v_w_branch_hgrn=v_w_branch_hgrn, v_w_out=v_w_out, v_norm_ffn_g=v_norm_ffn_g, v_w_ffn_gate=v_w_ffn_gate, v_w_ffn_up=v_w_ffn_up, v_w_ffn_down=v_w_ffn_down, v_norm_final_g=v_norm_final_g)
    weights = {n: given[n] for n in TWIN_WEIGHTS}
    shared = {n: given[n] for n in SHARED_INPUTS}
    per_example = {n: given[n] for n in ['x']}
    grad_fn = _jax.value_and_grad(_loss, argnums=(0, 1))

    def one_microbatch(ex, loss_target):
        ex = dict(ex)
        diff = ex.pop(TWIN_DIFF_INPUT)
        return grad_fn(weights, diff, {**shared, **ex}, loss_target)

    if N_MICROBATCH == 1:
        loss, (grad_w, grad_x) = one_microbatch(per_example, given["loss_target"])
    else:
        def body(carry, xs):
            loss_sum, grad_sum = carry
            l_k, (gw_k, gx_k) = one_microbatch(xs[0], xs[1])
            with _jax.named_scope("update"):
                return (loss_sum + l_k, _jax.tree.map(_jnp.add, grad_sum, gw_k)), gx_k

        init = (_jnp.zeros((), _jnp.float32), _jax.tree.map(_jnp.zeros_like, weights))
        (loss, grad_w), grad_x = _jax.lax.scan(body, init, (per_example, given["loss_target"]))
    with _jax.named_scope("update"):
        delta_w, new_m, new_v = {}, {}, {}
        for n in TWIN_WEIGHTS:
            delta_w[n], new_m[n], new_v[n] = _adamw(weights[n], grad_w[n], given["m_" + n], given["v_" + n])
    return (loss, grad_x, *[grad_w[n] for n in TWIN_WEIGHTS], *[delta_w[n] for n in TWIN_WEIGHTS],
            *[new_m[n] for n in TWIN_WEIGHTS], *[new_v[n] for n in TWIN_WEIGHTS])
```

```python
import functools
import math

import jax
import jax.numpy as jnp
from jax import lax
from jax.experimental import pallas as pl
from jax.experimental.pallas import tpu as pltpu

F32 = jnp.float32
BF16 = jnp.bfloat16
MXU_DTYPE = jnp.bfloat16
MESH_ID = pl.DeviceIdType.MESH

D_MODEL = 1024
HEAD_DIM = 64
Q_HEADS = 16
KV_HEADS = 2
GROUP = Q_HEADS // KV_HEADS
KV_WIDTH = KV_HEADS * HEAD_DIM
ATTN_BLOCK = 128
HGRN_HEADS = 8
HGRN_K = 128
CHUNK = 64
HGRN_TOKENS = 256
FFN = 2816
IN_SPLITS = (1024, 256, 4096, 2048)
EPS = 1e-6
NEG_INF = -1e30
ADAM_LR, ADAM_B1, ADAM_B2, ADAM_EPS, ADAM_WD, ADAM_STEP = 0.001, 0.9, 0.999, 1e-08, 0.01, 10
N_CHIPS = 4
VMEM_LIMIT = 56 * 1024 * 1024
SMALL_ROWS = 16


def _params(sem=None):
    return pltpu.CompilerParams(dimension_semantics=sem, vmem_limit_bytes=VMEM_LIMIT)


def _sigmoid(v):
    return 1.0 / (1.0 + jnp.exp(-v))


def _dot(a, b, dims):
    return lax.dot_general(a.astype(MXU_DTYPE), b.astype(MXU_DTYPE), (dims, ((), ())),
                           preferred_element_type=F32)


def _nn(a, b):
    return _dot(a, b, ((1,), (0,)))


def _nt(a, b):
    return _dot(a, b, ((1,), (1,)))


def _tn(a, b):
    return _dot(a, b, ((0,), (0,)))


def _matmul(name, a, b, mode, *, tm, tn, tk, outs, extras=(), epilogue=None):
    if mode == "nn":
        (M, K), N = a.shape, b.shape[1]
        a_spec = pl.BlockSpec((tm, tk), lambda i, j, k: (i, k))
        b_spec = pl.BlockSpec((tk, tn), lambda i, j, k: (k, j))
        prod = _nn
    elif mode == "nt":
        (M, K), N = a.shape, b.shape[0]
        a_spec = pl.BlockSpec((tm, tk), lambda i, j, k: (i, k))
        b_spec = pl.BlockSpec((tn, tk), lambda i, j, k: (j, k))
        prod = _nt
    else:
        (K, M), N = a.shape, b.shape[1]
        a_spec = pl.BlockSpec((tk, tm), lambda i, j, k: (k, i))
        b_spec = pl.BlockSpec((tk, tn), lambda i, j, k: (k, j))
        prod = _tn
    assert M % tm == 0 and N % tn == 0 and K % tk == 0, (name, M, N, K, tm, tn, tk)
    ni, nj, nk = M // tm, N // tn, K // tk
    ne, no = len(extras), len(outs)
    if epilogue is None:
        epilogue = lambda acc: (acc,)

    def finish(acc, extra_refs, out_refs):
        vals = epilogue(acc, *[r[...] for r in extra_refs])
        for (kind, _), o_ref, val in zip(outs, out_refs, vals):
            if kind == "pn":
                val = jnp.broadcast_to(val, o_ref.shape)
            o_ref[...] = val.astype(o_ref.dtype)

    def body(a_ref, b_ref, *rest):
        extra_refs, out_refs = rest[:ne], rest[ne:ne + no]
        if nk == 1:
            finish(prod(a_ref[...], b_ref[...]), extra_refs, out_refs)
            return
        acc_ref = rest[ne + no]
        k = pl.program_id(2)

        @pl.when(k == 0)
        def _():
            acc_ref[...] = jnp.zeros_like(acc_ref)

        acc_ref[...] += prod(a_ref[...], b_ref[...])

        @pl.when(k == nk - 1)
        def _():
            finish(acc_ref[...], extra_refs, out_refs)

    in_specs = [a_spec, b_spec]
    for _, shape, im in extras:
        in_specs.append(pl.BlockSpec(shape, functools.partial(lambda i, j, k, im: im(i, j), im=im)))
    out_shape, out_specs = [], []
    for kind, dt in outs:
        if kind == "mn":
            out_shape.append(jax.ShapeDtypeStruct((M, N), dt))
            out_specs.append(pl.BlockSpec((tm, tn), lambda i, j, k: (i, j)))
        elif kind == "m":
            assert nj == 1
            out_shape.append(jax.ShapeDtypeStruct((M, 1), dt))
            out_specs.append(pl.BlockSpec((tm, 1), lambda i, j, k: (i, 0)))
        else:
            out_shape.append(jax.ShapeDtypeStruct((8 * ni, N), dt))
            out_specs.append(pl.BlockSpec((8, tn), lambda i, j, k: (i, j)))
    res = pl.pallas_call(
        body, name=name, grid=(ni, nj, nk), in_specs=in_specs, out_specs=out_specs, out_shape=out_shape,
        scratch_shapes=[pltpu.VMEM((tm, tn), F32)] if nk > 1 else [],
        compiler_params=_params(("parallel", "parallel", "arbitrary")),
    )(a, b, *[e[0] for e in extras])
    return res


def _rows(shape):
    return shape, (lambda i, j: (i, 0))


def _colsum_partials(p):
    return jnp.sum(p.reshape(-1, 8, p.shape[-1])[:, 0, :], axis=0, keepdims=True)


def _rmsnorm_fwd(name, x, g, tr=512):
    T, D = x.shape

    def body(x_ref, g_ref, u_ref, r_ref):
        xv = x_ref[...]
        r = lax.rsqrt(jnp.mean(xv * xv, axis=-1, keepdims=True) + EPS)
        u_ref[...] = (xv * r * g_ref[...]).astype(u_ref.dtype)
        r_ref[...] = r

    return pl.pallas_call(
        body, name=name, grid=(T // tr,),
        in_specs=[pl.BlockSpec((tr, D), lambda i: (i, 0)), pl.BlockSpec((1, D), lambda i: (0, 0))],
        out_specs=[pl.BlockSpec((tr, D), lambda i: (i, 0)), pl.BlockSpec((tr, 1), lambda i: (i, 0))],
        out_shape=[jax.ShapeDtypeStruct((T, D), MXU_DTYPE), jax.ShapeDtypeStruct((T, 1), F32)],
        compiler_params=_params(("parallel",)),
    )(x, g)


def _rmsnorm_bwd_vals(dy, xin, rstd, g):
    xhat = xin * rstd
    dg = jnp.sum(dy * xhat, axis=0, keepdims=True)
    dxh = dy * g
    dx = rstd * (dxh - xhat * jnp.mean(dxh * xhat, axis=-1, keepdims=True))
    return dx, dg


def _colsum(name, a, tr=512):
    T, N = a.shape

    def body(a_ref, o_ref):
        @pl.when(pl.program_id(0) == 0)
        def _():
            o_ref[...] = jnp.zeros_like(o_ref)

        o_ref[...] += jnp.sum(a_ref[...].astype(F32), axis=0, keepdims=True)

    return pl.pallas_call(
        body, name=name, grid=(T // tr,),
        in_specs=[pl.BlockSpec((tr, N), lambda i: (i, 0))],
        out_specs=pl.BlockSpec((1, N), lambda i: (0, 0)),
        out_shape=jax.ShapeDtypeStruct((1, N), F32),
        compiler_params=_params(("arbitrary",)),
    )(a)


def _attn_masks(n):
    qi = lax.broadcasted_iota(jnp.int32, (ATTN_BLOCK, ATTN_BLOCK), 0)
    kj = lax.broadcasted_iota(jnp.int32, (ATTN_BLOCK, ATTN_BLOCK), 1)
    return kj <= qi, jnp.logical_and(kj > qi, n > 0)


def _attn_fwd(pq, pkv, sinks):
    T = pq.shape[0]
    nb = T // ATTN_BLOCK
    scale = 1.0 / math.sqrt(HEAD_DIM)

    def body(q_ref, kvc_ref, kvp_ref, s_ref, y_ref, lse_ref):
        mask_c, mask_p = _attn_masks(pl.program_id(0))
        for g in range(KV_HEADS):
            ks = slice(g * HEAD_DIM, (g + 1) * HEAD_DIM)
            vs = slice(KV_WIDTH + g * HEAD_DIM, KV_WIDTH + (g + 1) * HEAD_DIM)
            kc, vc = kvc_ref[:, ks].astype(MXU_DTYPE), kvc_ref[:, vs].astype(MXU_DTYPE)
            kp, vp = kvp_ref[:, ks].astype(MXU_DTYPE), kvp_ref[:, vs].astype(MXU_DTYPE)
            for h in range(g * GROUP, (g + 1) * GROUP):
                hs = slice(h * HEAD_DIM, (h + 1) * HEAD_DIM)
                q = q_ref[:, hs].astype(MXU_DTYPE)
                sc = jnp.where(mask_c, _nt(q, kc) * scale, NEG_INF)
                sp = jnp.where(mask_p, _nt(q, kp) * scale, NEG_INF)
                sink = s_ref[0, h]
                m = jnp.maximum(jnp.maximum(jnp.max(sc, axis=-1, keepdims=True),
                                            jnp.max(sp, axis=-1, keepdims=True)), sink)
                pc, pp = jnp.exp(sc - m), jnp.exp(sp - m)
                den = (jnp.sum(pc, axis=-1, keepdims=True) + jnp.sum(pp, axis=-1, keepdims=True)
                       + jnp.exp(sink - m))
                o = (_nn(pc, vc) + _nn(pp, vp)) / den
                y_ref[:, hs] = o.astype(y_ref.dtype)
                lse_ref[:, h:h + 1] = m + jnp.log(den)

    return pl.pallas_call(
        body, name="attn_fwd", grid=(nb,),
        in_specs=[pl.BlockSpec((ATTN_BLOCK, D_MODEL), lambda n: (n, 0)),
                  pl.BlockSpec((ATTN_BLOCK, 2 * KV_WIDTH), lambda n: (n, 0)),
                  pl.BlockSpec((ATTN_BLOCK, 2 * KV_WIDTH), lambda n: (jnp.maximum(n - 1, 0), 0)),
                  pl.BlockSpec(memory_space=pltpu.SMEM)],
        out_specs=[pl.BlockSpec((ATTN_BLOCK, D_MODEL), lambda n: (n, 0)),
                   pl.BlockSpec((ATTN_BLOCK, Q_HEADS), lambda n: (n, 0))],
        out_shape=[jax.ShapeDtypeStruct((T, D_MODEL), MXU_DTYPE), jax.ShapeDtypeStruct((T, Q_HEADS), F32)],
        compiler_params=_params(("parallel",)),
    )(pq, pkv, pkv, sinks)


def _attn_bwd(pq, pkv, sinks, lse, dy):
    T = pq.shape[0]
    nb = T // ATTN_BLOCK
    scale = 1.0 / math.sqrt(HEAD_DIM)
    cur = lambda n: (jnp.minimum(n, nb - 1), 0)

    def body(q_ref, kvc_ref, kvp_ref, s_ref, lse_ref, dy_ref, dq_ref, dkv_ref, ds_ref, carry, top, bot):
        n = pl.program_id(0)

        @pl.when(n == 0)
        def _():
            carry[...] = jnp.zeros_like(carry)
            ds_ref[...] = jnp.zeros_like(ds_ref)

        @pl.when(n < nb)
        def _():
            mask_c, mask_p = _attn_masks(n)
            row = lax.broadcasted_iota(jnp.int32, (8, 128), 0)
            col = lax.broadcasted_iota(jnp.int32, (8, 128), 1)
            dsink = jnp.zeros((8, 128), F32)
            for g in range(KV_HEADS):
                ks = slice(g * HEAD_DIM, (g + 1) * HEAD_DIM)
                vs = slice(KV_WIDTH + g * HEAD_DIM, KV_WIDTH + (g + 1) * HEAD_DIM)
                kc, vc = kvc_ref[:, ks].astype(MXU_DTYPE), kvc_ref[:, vs].astype(MXU_DTYPE)
                kp, vp = kvp_ref[:, ks].astype(MXU_DTYPE), kvp_ref[:, vs].astype(MXU_DTYPE)
                dkc = jnp.zeros((ATTN_BLOCK, HEAD_DIM), F32)
                dkp, dvc, dvp = dkc, dkc, dkc
                for h in range(g * GROUP, (g + 1) * GROUP):
                    hs = slice(h * HEAD_DIM, (h + 1) * HEAD_DIM)
                    q = q_ref[:, hs].astype(MXU_DTYPE)
                    do = dy_ref[:, hs].astype(MXU_DTYPE)
                    lse_h = lse_ref[:, h:h + 1]
                    pc = jnp.where(mask_c, jnp.exp(_nt(q, kc) * scale - lse_h), 0.0)
                    pp = jnp.where(mask_p, jnp.exp(_nt(q, kp) * scale - lse_h), 0.0)
                    dpc, dpp = _nt(do, vc), _nt(do, vp)
                    delta = (jnp.sum(pc * dpc, axis=-1, keepdims=True)
                             + jnp.sum(pp * dpp, axis=-1, keepdims=True))
                    dsc = pc * (dpc - delta) * scale
                    dsp = pp * (dpp - delta) * scale
                    dq_ref[:, hs] = (_nn(dsc, kc) + _nn(dsp, kp)).astype(dq_ref.dtype)
                    dkc = dkc + _tn(dsc, q)
                    dkp = dkp + _tn(dsp, q)
                    dvc = dvc + _tn(pc, do)
                    dvp = dvp + _tn(pp, do)
                    p_sink = jnp.exp(s_ref[0, h] - lse_h)
                    val = -jnp.sum(p_sink * delta, axis=0, keepdims=True)
                    dsink = dsink + jnp.where(jnp.logical_and(row == 0, col == h), val, 0.0)
                bot[:, ks], bot[:, vs] = dkc, dvc
                top[:, ks], top[:, vs] = dkp, dvp
            ds_ref[...] += dsink
            dkv_ref[...] = (carry[...] + top[...]).astype(dkv_ref.dtype)
            carry[...] = bot[...]

        @pl.when(n == nb)
        def _():
            dkv_ref[...] = carry[...].astype(dkv_ref.dtype)

    return pl.pallas_call(
        body, name="attn_bwd", grid=(nb + 1,),
        in_specs=[pl.BlockSpec((ATTN_BLOCK, D_MODEL), cur),
                  pl.BlockSpec((ATTN_BLOCK, 2 * KV_WIDTH), cur),
                  pl.BlockSpec((ATTN_BLOCK, 2 * KV_WIDTH), lambda n: (jnp.maximum(jnp.minimum(n, nb - 1) - 1, 0), 0)),
                  pl.BlockSpec(memory_space=pltpu.SMEM),
                  pl.BlockSpec((ATTN_BLOCK, Q_HEADS), cur),
                  pl.BlockSpec((ATTN_BLOCK, D_MODEL), cur)],
        out_specs=[pl.BlockSpec((ATTN_BLOCK, D_MODEL), cur),
                   pl.BlockSpec((ATTN_BLOCK, 2 * KV_WIDTH), lambda n: (jnp.maximum(n - 1, 0), 0)),
                   pl.BlockSpec((8, 128), lambda n: (0, 0))],
        out_shape=[jax.ShapeDtypeStruct((T, D_MODEL), MXU_DTYPE),
                   jax.ShapeDtypeStruct((T, 2 * KV_WIDTH), MXU_DTYPE),
                   jax.ShapeDtypeStruct((8, 128), F32)],
        scratch_shapes=[pltpu.VMEM((ATTN_BLOCK, 2 * KV_WIDTH), F32)] * 3,
        compiler_params=_params(("arbitrary",)),
    )(pq, pkv, pkv, sinks, lse, dy)


def _lower_bound(l):
    m = jnp.maximum(l[0:1], l[1:2])
    e0, e1 = jnp.exp(l[0:1] - m), jnp.exp(l[1:2] - m)
    return e0 / (e0 + e1)


def _tri(lower):
    r = lax.broadcasted_iota(jnp.int32, (CHUNK, CHUNK), 0)
    c = lax.broadcasted_iota(jnp.int32, (CHUNK, CHUNK), 1)
    return (r >= c) if lower else (c >= r)


def _chunk_sum(mask, v):
    return lax.dot_general(mask.astype(F32), v, (((1,), (0,)), ((), ())),
                           precision=lax.Precision.HIGHEST, preferred_element_type=F32)


def _hgrn_chunk_inputs(hq, hf, lb, causal):
    sg, sgn = _sigmoid(hf), _sigmoid(-hf)
    f = lb + (1.0 - lb) * sg
    kk = (1.0 - lb) * sgn
    sq = _sigmoid(hq)
    q = hq * sq
    b = _chunk_sum(causal, jnp.log(f))
    bm, bl = b[CHUNK // 2 - 1:CHUNK // 2, :], b[CHUNK - 1:CHUNK, :]
    e_qm, e_km, e_qs, e_kl = jnp.exp(b - bm), jnp.exp(bm - b), jnp.exp(b), jnp.exp(bl - b)
    return dict(sg=sg, sgn=sgn, f=f, kk=kk, sq=sq, q=q, e_qm=e_qm, e_km=e_km, e_qs=e_qs, e_kl=e_kl,
                qm=q * e_qm, km=kk * e_km, qs=q * e_qs, kl=kk * e_kl, el=jnp.exp(bl))


def _hgrn_fwd(ph, lb_logits, norm_g):
    T = ph.shape[0]
    nblk, cpb = T // HGRN_TOKENS, HGRN_TOKENS // CHUNK
    col = lambda c: pl.BlockSpec((HGRN_TOKENS, D_MODEL), functools.partial(lambda i, c: (i, c), c=c))

    def body(hq_ref, hf_ref, hi_ref, hg_ref, l_ref, ng_ref, y_ref, o_ref, st_ref, s_ref):
        @pl.when(pl.program_id(0) == 0)
        def _():
            s_ref[...] = jnp.zeros_like(s_ref)

        lb = _lower_bound(l_ref[...])
        causal = _tri(True)
        for c in range(cpb):
            rows = slice(c * CHUNK, (c + 1) * CHUNK)
            t = _hgrn_chunk_inputs(hq_ref[rows, :], hf_ref[rows, :], lb, causal)
            qm, km, qs, kl = (t[n].astype(MXU_DTYPE) for n in ("qm", "km", "qs", "kl"))
            v = hi_ref[rows, :].astype(MXU_DTYPE)
            for h in range(HGRN_HEADS):
                ls = slice(h * HGRN_K, (h + 1) * HGRN_K)
                st = s_ref[h]
                st_ref[c, ls, :] = st
                a = jnp.where(causal, _nt(qm[:, ls], km[:, ls]), 0.0)
                o_ref[rows, ls] = _nn(a, v[:, ls]) + _nt(qs[:, ls], st)
                s_ref[h] = t["el"][:, ls] * st + _tn(v[:, ls], kl[:, ls])
        for h in range(HGRN_HEADS):
            ls = slice(h * HGRN_K, (h + 1) * HGRN_K)
            o = o_ref[:, ls]
            r = lax.rsqrt(jnp.mean(o * o, axis=-1, keepdims=True) + EPS)
            y_ref[:, ls] = (o * r * ng_ref[:, ls] * _sigmoid(hg_ref[:, ls])).astype(y_ref.dtype)

    return pl.pallas_call(
        body, name="hgrn_fwd", grid=(nblk,),
        in_specs=[col(0), col(1), col(2), col(3),
                  pl.BlockSpec((2, D_MODEL), lambda i: (0, 0)), pl.BlockSpec((1, D_MODEL), lambda i: (0, 0))],
        out_specs=[pl.BlockSpec((HGRN_TOKENS, D_MODEL), lambda i: (i, 0)),
                   pl.BlockSpec((HGRN_TOKENS, D_MODEL), lambda i: (i, 0)),
                   pl.BlockSpec((cpb, D_MODEL, HGRN_K), lambda i: (i, 0, 0))],
        out_shape=[jax.ShapeDtypeStruct((T, D_MODEL), MXU_DTYPE), jax.ShapeDtypeStruct((T, D_MODEL), F32),
                   jax.ShapeDtypeStruct((T // CHUNK, D_MODEL, HGRN_K), F32)],
        scratch_shapes=[pltpu.VMEM((HGRN_HEADS, HGRN_K, HGRN_K), F32)],
        compiler_params=_params(("arbitrary",)),
    )(ph, ph, ph, ph, lb_logits, norm_g)


def _hgrn_bwd(ph, o_raw, states, dy, lb_logits, norm_g):
    T = ph.shape[0]
    nblk, cpb = T // HGRN_TOKENS, HGRN_TOKENS // CHUNK
    rev = lambda i: nblk - 1 - i
    col = lambda c: pl.BlockSpec((HGRN_TOKENS, D_MODEL), functools.partial(lambda i, c: (rev(i), c), c=c))
    tok = pl.BlockSpec((HGRN_TOKENS, D_MODEL), lambda i: (rev(i), 0))

    def body(hq_ref, hf_ref, hi_ref, hg_ref, o_ref, st_ref, dy_ref, l_ref, ng_ref,
             dph_ref, dng_ref, dl_ref, dst_ref, dlb_ref, do_s, dqm_s, dkm_s, dqs_s, dkl_s, dv_s, del_s):
        i = pl.program_id(0)

        @pl.when(i == 0)
        def _():
            dst_ref[...] = jnp.zeros_like(dst_ref)
            dlb_ref[...] = jnp.zeros_like(dlb_ref)
            dng_ref[...] = jnp.zeros_like(dng_ref)

        lb = _lower_bound(l_ref[...])
        causal, anti = _tri(True), _tri(False)
        row = lax.broadcasted_iota(jnp.int32, (CHUNK, D_MODEL), 0)
        for c in reversed(range(cpb)):
            rows = slice(c * CHUNK, (c + 1) * CHUNK)
            hq = hq_ref[rows, :]
            t = _hgrn_chunk_inputs(hq, hf_ref[rows, :], lb, causal)
            sgg = _sigmoid(hg_ref[rows, :])
            dyv = dy_ref[rows, :]
            for h in range(HGRN_HEADS):
                ls = slice(h * HGRN_K, (h + 1) * HGRN_K)
                o = o_ref[rows, ls]
                r = lax.rsqrt(jnp.mean(o * o, axis=-1, keepdims=True) + EPS)
                nrm = o * r
                g_h = sgg[:, ls]
                dph_ref[rows, 3 * D_MODEL + h * HGRN_K:3 * D_MODEL + (h + 1) * HGRN_K] = (
                    dyv[:, ls] * nrm * ng_ref[:, ls] * g_h * (1.0 - g_h)).astype(dph_ref.dtype)
                dyg = dyv[:, ls] * g_h
                dng_ref[:, ls] += jnp.sum(dyg * nrm, axis=0, keepdims=True)
                dn = dyg * ng_ref[:, ls]
                do_s[:, ls] = r * (dn - nrm * jnp.mean(dn * nrm, axis=-1, keepdims=True))
            qm, km, qs, kl = (t[n].astype(MXU_DTYPE) for n in ("qm", "km", "qs", "kl"))
            v = hi_ref[rows, :].astype(MXU_DTYPE)
            do = do_s[...].astype(MXU_DTYPE)
            for h in range(HGRN_HEADS):
                ls = slice(h * HGRN_K, (h + 1) * HGRN_K)
                st = st_ref[c, ls, :]
                dst = dst_ref[h]
                a = jnp.where(causal, _nt(qm[:, ls], km[:, ls]), 0.0)
                da = jnp.where(causal, _nt(do[:, ls], v[:, ls]), 0.0)
                dv_s[:, ls] = _tn(a, do[:, ls]) + _nt(kl[:, ls], dst)
                dkl_s[:, ls] = _nn(v[:, ls], dst)
                dqs_s[:, ls] = _nn(do[:, ls], st)
                del_s[:, ls] = jnp.sum(dst * st, axis=0, keepdims=True)
                dst_ref[h] = _tn(do[:, ls], qs[:, ls]) + t["el"][:, ls] * dst
                dqm_s[:, ls] = _nn(da, km[:, ls])
                dkm_s[:, ls] = _tn(da, qm[:, ls])
            dqm, dkm, dqs, dkl = dqm_s[...], dkm_s[...], dqs_s[...], dkl_s[...]
            dq = dqm * t["e_qm"] + dqs * t["e_qs"]
            dk = dkm * t["e_km"] + dkl * t["e_kl"]
            t_qm, t_km, t_kl = dqm * t["qm"], dkm * t["km"], dkl * t["kl"]
            db = t_qm - t_km + dqs * t["qs"] - t_kl
            db_mid = jnp.sum(t_km - t_qm, axis=0, keepdims=True)
            db_last = jnp.sum(t_kl, axis=0, keepdims=True) + del_s[...] * t["el"]
            db = db + jnp.where(row == CHUNK // 2 - 1, db_mid, 0.0) + jnp.where(row == CHUNK - 1, db_last, 0.0)
            dlogf = _chunk_sum(anti, db)
            sq, sg, sgn, f = t["sq"], t["sg"], t["sgn"], t["f"]
            dph_ref[rows, 0:D_MODEL] = (dq * (sq * (1.0 + hq * (1.0 - sq)))).astype(dph_ref.dtype)
            dph_ref[rows, D_MODEL:2 * D_MODEL] = (
                dlogf * (1.0 - lb) * sg * (1.0 - sg) / f - dk * (1.0 - lb) * sgn * (1.0 - sgn)).astype(dph_ref.dtype)
            dph_ref[rows, 2 * D_MODEL:3 * D_MODEL] = dv_s[...].astype(dph_ref.dtype)
            dlb_ref[...] += jnp.sum(dlogf * (1.0 - sg) / f - dk * sgn, axis=0, keepdims=True)

        @pl.when(i == nblk - 1)
        def _():
            dl0 = dlb_ref[...] * lb * (1.0 - lb)
            dl_ref[0:1, :] = dl0
            dl_ref[1:2, :] = -dl0

    wide = pltpu.VMEM((CHUNK, D_MODEL), F32)
    return pl.pallas_call(
        body, name="hgrn_bwd", grid=(nblk,),
        in_specs=[col(0), col(1), col(2), col(3), tok,
                  pl.BlockSpec((cpb, D_MODEL, HGRN_K), lambda i: (rev(i), 0, 0)), tok,
                  pl.BlockSpec((2, D_MODEL), lambda i: (0, 0)), pl.BlockSpec((1, D_MODEL), lambda i: (0, 0))],
        out_specs=[pl.BlockSpec((HGRN_TOKENS, 4 * D_MODEL), lambda i: (rev(i), 0)),
                   pl.BlockSpec((1, D_MODEL), lambda i: (0, 0)), pl.BlockSpec((2, D_MODEL), lambda i: (0, 0))],
        out_shape=[jax.ShapeDtypeStruct((T, 4 * D_MODEL), MXU_DTYPE), jax.ShapeDtypeStruct((1, D_MODEL), F32),
                   jax.ShapeDtypeStruct((2, D_MODEL), F32)],
        scratch_shapes=[pltpu.VMEM((HGRN_HEADS, HGRN_K, HGRN_K), F32), pltpu.VMEM((1, D_MODEL), F32),
                        wide, wide, wide, wide, wide, wide, pltpu.VMEM((1, D_MODEL), F32)],
        compiler_params=_params(("arbitrary",)),
    )(ph, ph, ph, ph, o_raw, states, dy, lb_logits, norm_g)


def _local_step(x, target, norm_mix_g, w_in, b_in, sinks, lb_logits, hgrn_norm_g, w_ba, w_bh, w_out,
                norm_ffn_g, w_gate, w_up, w_down, norm_final_g):
    T, D = x.shape
    o_q, o_kv, o_h, o_g = (sum(IN_SPLITS[:i]) for i in range(4))
    w_q, w_kv, w_h, w_g = (w_in[:, o:o + n] for o, n in zip((o_q, o_kv, o_h, o_g), IN_SPLITS))
    b_q, b_kv, b_h, b_g = (b_in[:, o:o + n] for o, n in zip((o_q, o_kv, o_h, o_g), IN_SPLITS))
    bias = lambda acc, b: (acc + b,)
    row_vec = lambda n: ((1, n), lambda i, j: (0, j))
    tile = lambda tm, tn: ((tm, tn), lambda i, j: (i, j))
    TM = 512

    u, rstd1 = _rmsnorm_fwd("norm_mix", x, norm_mix_g)
    pq, = _matmul("in_q", u, w_q, "nn", tm=TM, tn=1024, tk=1024, outs=[("mn", F32)],
                  extras=[(b_q, *row_vec(1024))], epilogue=bias)
    pkv, = _matmul("in_kv", u, w_kv, "nn", tm=TM, tn=256, tk=1024, outs=[("mn", F32)],
                   extras=[(b_kv, *row_vec(256))], epilogue=bias)
    ph, = _matmul("in_h", u, w_h, "nn", tm=TM, tn=1024, tk=1024, outs=[("mn", F32)],
                  extras=[(b_h, *row_vec(1024))], epilogue=bias)
    pg, = _matmul("in_g", u, w_g, "nn", tm=TM, tn=1024, tk=1024, outs=[("mn", F32)],
                  extras=[(b_g, *row_vec(1024))], epilogue=bias)
    y_attn, lse = _attn_fwd(pq, pkv, sinks)
    y_hgrn, o_raw, states = _hgrn_fwd(ph, lb_logits, hgrn_norm_g)
    ya, = _matmul("branch_a", y_attn, w_ba, "nn", tm=TM, tn=1024, tk=1024, outs=[("mn", F32)])
    gate_a = (pg, (TM, 1024), lambda i, j: (i, 0))
    gate_b = (pg, (TM, 1024), lambda i, j: (i, 1))

    def merge(acc, ya_v, ga, gb):
        return acc, _sigmoid(ga) * ya_v + _sigmoid(gb) * acc

    yb, merged = _matmul("branch_b", y_hgrn, w_bh, "nn", tm=TM, tn=1024, tk=1024,
                         outs=[("mn", F32), ("mn", MXU_DTYPE)],
                         extras=[(ya, *tile(TM, 1024)), gate_a, gate_b], epilogue=merge)

    def resid_norm(acc, xin, g):
        h = xin + acc
        r = lax.rsqrt(jnp.mean(h * h, axis=-1, keepdims=True) + EPS)
        return h, h * r * g, r

    h1, u2, rstd2 = _matmul("out_proj", merged, w_out, "nn", tm=TM, tn=1024, tk=1024,
                            outs=[("mn", F32), ("mn", MXU_DTYPE), ("m", F32)],
                            extras=[(x, *tile(TM, 1024)), (norm_ffn_g, *row_vec(1024))], epilogue=resid_norm)
    FT = FFN // 2
    gpre, = _matmul("ffn_gate", u2, w_gate, "nn", tm=TM, tn=FT, tk=1024, outs=[("mn", F32)])

    def swiglu(acc, gv):
        return acc, gv * _sigmoid(gv) * acc

    up, z = _matmul("ffn_up", u2, w_up, "nn", tm=TM, tn=FT, tk=1024, outs=[("mn", F32), ("mn", MXU_DTYPE)],
                    extras=[(gpre, *tile(TM, FT))], epilogue=swiglu)

    def loss_head(acc, h1_v, tgt, g):
        h2 = h1_v + acc
        r = lax.rsqrt(jnp.mean(h2 * h2, axis=-1, keepdims=True) + EPS)
        xhat = h2 * r
        err = xhat * g - tgt
        part = 0.5 * jnp.sum(jnp.sum(err * err, axis=-1, keepdims=True), axis=0, keepdims=True) / D
        dyv = err / D
        dxh = dyv * g
        dh2 = r * (dxh - xhat * jnp.mean(dxh * xhat, axis=-1, keepdims=True))
        return dh2, dh2, jnp.sum(dyv * xhat, axis=0, keepdims=True), jnp.broadcast_to(part, (1, D))

    dh2, dh2b, dgf_p, loss_p = _matmul(
        "ffn_down", z, w_down, "nn", tm=TM, tn=1024, tk=FFN,
        outs=[("mn", F32), ("mn", MXU_DTYPE), ("pn", F32), ("pn", F32)],
        extras=[(h1, *tile(TM, 1024)), (target, *tile(TM, 1024)), (norm_final_g, *row_vec(1024))],
        epilogue=loss_head)
    loss = jnp.sum(loss_p.reshape(-1, 8, D)[:, 0, 0])
    d_norm_final = _colsum_partials(dgf_p)

    def swiglu_bwd(acc, gv, upv):
        s = _sigmoid(gv)
        return acc * upv * (s * (1.0 + gv * (1.0 - s))), acc * (gv * s)

    dgp, dup = _matmul("d_ffn_hidden", dh2b, w_down, "nt", tm=TM, tn=FT, tk=1024,
                       outs=[("mn", MXU_DTYPE), ("mn", MXU_DTYPE)],
                       extras=[(gpre, *tile(TM, FT)), (up, *tile(TM, FT))], epilogue=swiglu_bwd)
    d_w_down, = _matmul("dw_down", z, dh2b, "tn", tm=FT, tn=1024, tk=512, outs=[("mn", F32)])
    du2a, = _matmul("d_ffn_in_gate", dgp, w_gate, "nt", tm=TM, tn=1024, tk=FT, outs=[("mn", F32)])

    def norm_ffn_bwd(acc, part, h1_v, r, g, dres):
        dx, dg = _rmsnorm_bwd_vals(acc + part, h1_v, r, g)
        dh = dres + dx
        return dh, dh, dg

    dh1, dh1b, dg2_p = _matmul(
        "d_ffn_in_up", dup, w_up, "nt", tm=TM, tn=1024, tk=FT,
        outs=[("mn", F32), ("mn", MXU_DTYPE), ("pn", F32)],
        extras=[(du2a, *tile(TM, 1024)), (h1, *tile(TM, 1024)), (rstd2, (TM, 1), lambda i, j: (i, 0)),
                (norm_ffn_g, *row_vec(1024)), (dh2, *tile(TM, 1024))], epilogue=norm_ffn_bwd)
    d_norm_ffn = _colsum_partials(dg2_p)
    d_w_gate, = _matmul("dw_gate", u2, dgp, "tn", tm=1024, tn=FT, tk=512, outs=[("mn", F32)])
    d_w_up, = _matmul("dw_up", u2, dup, "tn", tm=1024, tn=FT, tk=512, outs=[("mn", F32)])

    def merge_bwd(acc, ya_v, yb_v, ga, gb):
        sa, sb = _sigmoid(ga), _sigmoid(gb)
        return acc * sa, acc * sb, acc * ya_v * sa * (1.0 - sa), acc * yb_v * sb * (1.0 - sb)

    dya, dyb, dga, dgb = _matmul(
        "d_merged", dh1b, w_out, "nt", tm=TM, tn=1024, tk=1024, outs=[("mn", MXU_DTYPE)] * 4,
        extras=[(ya, *tile(TM, 1024)), (yb, *tile(TM, 1024)), gate_a, gate_b], epilogue=merge_bwd)
    d_w_out, = _matmul("dw_out", merged, dh1b, "tn", tm=1024, tn=1024, tk=512, outs=[("mn", F32)])
    dy_attn, = _matmul("d_y_attn", dya, w_ba, "nt", tm=TM, tn=1024, tk=1024, outs=[("mn", F32)])
    dy_hgrn, = _matmul("d_y_hgrn", dyb, w_bh, "nt", tm=TM, tn=1024, tk=1024, outs=[("mn", F32)])
    d_w_ba, = _matmul("dw_branch_a", y_attn, dya, "tn", tm=1024, tn=1024, tk=512, outs=[("mn", F32)])
    d_w_bh, = _matmul("dw_branch_b", y_hgrn, dyb, "tn", tm=1024, tn=1024, tk=512, outs=[("mn", F32)])

    dq, dkv, dsink = _attn_bwd(pq, pkv, sinks, lse, dy_attn)
    dph, d_hgrn_norm, d_lb_logits = _hgrn_bwd(ph, o_raw, states, dy_hgrn, lb_logits, hgrn_norm_g)

    acc_in = lambda acc, part: (acc + part,)
    du, = _matmul("d_u_q", dq, w_q, "nt", tm=TM, tn=1024, tk=1024, outs=[("mn", F32)])
    du, = _matmul("d_u_kv", dkv, w_kv, "nt", tm=TM, tn=1024, tk=256, outs=[("mn", F32)],
                  extras=[(du, *tile(TM, 1024))], epilogue=acc_in)
    du, = _matmul("d_u_h", dph, w_h, "nt", tm=TM, tn=1024, tk=1024, outs=[("mn", F32)],
                  extras=[(du, *tile(TM, 1024))], epilogue=acc_in)
    du, = _matmul("d_u_ga", dga, w_g[:, :D], "nt", tm=TM, tn=1024, tk=1024, outs=[("mn", F32)],
                  extras=[(du, *tile(TM, 1024))], epilogue=acc_in)

    def norm_mix_bwd(acc, part, xin, r, g, dres):
        dx, dg = _rmsnorm_bwd_vals(acc + part, xin, r, g)
        return dres + dx, dg

    dx, dg1_p = _matmul(
        "d_u_gb", dgb, w_g[:, D:], "nt", tm=TM, tn=1024, tk=1024, outs=[("mn", F32), ("pn", F32)],
        extras=[(du, *tile(TM, 1024)), (x, *tile(TM, 1024)), (rstd1, (TM, 1), lambda i, j: (i, 0)),
                (norm_mix_g, *row_vec(1024)), (dh1, *tile(TM, 1024))], epilogue=norm_mix_bwd)
    d_norm_mix = _colsum_partials(dg1_p)
    dw_q, = _matmul("dw_in_q", u, dq, "tn", tm=1024, tn=1024, tk=512, outs=[("mn", F32)])
    dw_kv, = _matmul("dw_in_kv", u, dkv, "tn", tm=1024, tn=256, tk=512, outs=[("mn", F32)])
    dw_h, = _matmul("dw_in_h", u, dph, "tn", tm=1024, tn=1024, tk=512, outs=[("mn", F32)])
    dw_ga, = _matmul("dw_in_ga", u, dga, "tn", tm=1024, tn=1024, tk=512, outs=[("mn", F32)])
    dw_gb, = _matmul("dw_in_gb", u, dgb, "tn", tm=1024, tn=1024, tk=512, outs=[("mn", F32)])
    d_w_in = jnp.concatenate([dw_q, dw_kv, dw_h, dw_ga, dw_gb], axis=1)
    d_b_in = jnp.concatenate([_colsum("db_q", dq), _colsum("db_kv", dkv), _colsum("db_h", dph),
                              _colsum("db_ga", dga), _colsum("db_gb", dgb)], axis=1)
    mats = dict(w_in=d_w_in, w_branch_attn=d_w_ba, w_branch_hgrn=d_w_bh, w_out=d_w_out,
                w_ffn_gate=d_w_gate, w_ffn_up=d_w_up, w_ffn_down=d_w_down)
    vecs = dict(norm_mix_g=d_norm_mix, b_in=d_b_in, attn_sinks=dsink[0:1, 0:Q_HEADS], hgrn_lb_logits=d_lb_logits,
                hgrn_norm_g=d_hgrn_norm, norm_ffn_g=d_norm_ffn, norm_final_g=d_norm_final)
    return loss, dx, mats, vecs


HBM_SPEC = pl.BlockSpec(memory_space=pl.ANY)


def _place():
    return lax.axis_index("x"), lax.axis_index("y"), lax.axis_index("c")


def _other_chips(x, y):
    return [(1 - x, y), (x, 1 - y), (1 - x, 1 - y)]


def _all_gather(shards):
    n = len(shards)

    def body(*refs):
        ins, outs = refs[:n], refs[n:2 * n]
        send_sems, recv_sems, local_sems = refs[2 * n:]
        x, y, c = _place()
        mine = 2 * x + y
        started = []
        for w in range(n):
            cp = pltpu.make_async_copy(ins[w], outs[w].at[mine], local_sems.at[w])
            cp.start()
            started.append(cp)
        sends, recvs = [], []
        for w in range(n):
            for k, (px, py) in enumerate(_other_chips(x, y)):
                sem = 3 * w + k
                sends.append(pltpu.make_async_remote_copy(
                    src_ref=ins[w], dst_ref=outs[w].at[mine], send_sem=send_sems.at[sem], recv_sem=recv_sems.at[sem],
                    device_id=(px, py, c), device_id_type=MESH_ID))
                recvs.append(pltpu.make_async_remote_copy(
                    src_ref=ins[w], dst_ref=outs[w].at[2 * px + py], send_sem=send_sems.at[sem],
                    recv_sem=recv_sems.at[sem], device_id=(px, py, c), device_id_type=MESH_ID))
        for cp in sends:
            cp.start()
        for cp in recvs:
            cp.wait_recv()
        for cp in sends:
            cp.wait_send()
        for cp in started:
            cp.wait()

    return pl.pallas_call(
        body, name="gather_weights",
        in_specs=[HBM_SPEC] * n, out_specs=[HBM_SPEC] * n,
        out_shape=[jax.ShapeDtypeStruct((N_CHIPS,) + s.shape, s.dtype) for s in shards],
        scratch_shapes=[pltpu.SemaphoreType.DMA((3 * n,)), pltpu.SemaphoreType.DMA((3 * n,)),
                        pltpu.SemaphoreType.DMA((n,))],
        compiler_params=pltpu.CompilerParams(has_side_effects=True),
    )(*shards)


def _exchange_grads(stacked, small):
    n = len(stacked)

    def body(*refs):
        ins, small_ref = refs[:n], refs[n]
        outs, all_ref = refs[n + 1:2 * n + 1], refs[2 * n + 1]
        send_sems, recv_sems, small_send, small_recv, local_sem = refs[2 * n + 2:]
        x, y, c = _place()
        me = 4 * x + 2 * y + c
        own = pltpu.make_async_copy(small_ref, all_ref.at[me], local_sem)
        own.start()
        sends, recvs = [], []
        for r in range(1, 8):
            px = 1 - x if r & 4 else x
            py = 1 - y if r & 2 else y
            pc = 1 - c if r & 1 else c
            sends.append(pltpu.make_async_remote_copy(
                src_ref=small_ref, dst_ref=all_ref.at[me], send_sem=small_send.at[r - 1], recv_sem=small_recv.at[r - 1],
                device_id=(px, py, pc), device_id_type=MESH_ID))
            recvs.append(pltpu.make_async_remote_copy(
                src_ref=small_ref, dst_ref=all_ref.at[4 * px + 2 * py + pc], send_sem=small_send.at[r - 1],
                recv_sem=small_recv.at[r - 1], device_id=(px, py, pc), device_id_type=MESH_ID))
        for w in range(n):
            for k, (px, py) in enumerate(_other_chips(x, y)):
                sem = 3 * w + k
                sends.append(pltpu.make_async_remote_copy(
                    src_ref=ins[w].at[2 * px + py], dst_ref=outs[w].at[k], send_sem=send_sems.at[sem],
                    recv_sem=recv_sems.at[sem], device_id=(px, py, c), device_id_type=MESH_ID))
                recvs.append(sends[-1])
        for cp in sends:
            cp.start()
        for cp in recvs:
            cp.wait_recv()
        for cp in sends:
            cp.wait_send()
        own.wait()

    res = pl.pallas_call(
        body, name="exchange_grads",
        in_specs=[HBM_SPEC] * (n + 1), out_specs=[HBM_SPEC] * (n + 1),
        out_shape=[jax.ShapeDtypeStruct((3,) + s.shape[1:], s.dtype) for s in stacked]
        + [jax.ShapeDtypeStruct((8,) + small.shape, small.dtype)],
        scratch_shapes=[pltpu.SemaphoreType.DMA((3 * n,)), pltpu.SemaphoreType.DMA((3 * n,)),
                        pltpu.SemaphoreType.DMA((7,)), pltpu.SemaphoreType.DMA((7,)), pltpu.SemaphoreType.DMA(())],
        compiler_params=pltpu.CompilerParams(has_side_effects=True),
    )(*stacked, small)
    return res[:n], res[n]


def _exchange_sibling(parts):
    n = len(parts)

    def body(*refs):
        ins, outs = refs[:n], refs[n:2 * n]
        send_sems, recv_sems = refs[2 * n:]
        x, y, c = _place()
        copies = [pltpu.make_async_remote_copy(
            src_ref=ins[w], dst_ref=outs[w], send_sem=send_sems.at[w], recv_sem=recv_sems.at[w],
            device_id=(x, y, 1 - c), device_id_type=MESH_ID) for w in range(n)]
        for cp in copies:
            cp.start()
        for cp in copies:
            cp.wait_recv()
        for cp in copies:
            cp.wait_send()

    return pl.pallas_call(
        body, name="exchange_sibling",
        in_specs=[HBM_SPEC] * n, out_specs=[HBM_SPEC] * n,
        out_shape=[jax.ShapeDtypeStruct(p.shape, p.dtype) for p in parts],
        scratch_shapes=[pltpu.SemaphoreType.DMA((n,)), pltpu.SemaphoreType.DMA((n,))],
        compiler_params=pltpu.CompilerParams(has_side_effects=True),
    )(*parts)


def _row_tile(rows):
    return 128 if rows % 128 == 0 else 176


def _partial_sum(name, own, recv):
    R, C = own.shape
    tr = _row_tile(R)

    def body(o_ref, r_ref, p_ref):
        p_ref[...] = ((o_ref[...] + r_ref[0].astype(F32)) + r_ref[1].astype(F32)) + r_ref[2].astype(F32)

    return pl.pallas_call(
        body, name=name, grid=(R // tr,),
        in_specs=[pl.BlockSpec((tr, C), lambda i: (i, 0)), pl.BlockSpec((3, tr, C), lambda i: (0, i, 0))],
        out_specs=pl.BlockSpec((tr, C), lambda i: (i, 0)),
        out_shape=jax.ShapeDtypeStruct((R, C), F32),
        compiler_params=_params(("parallel",)),
    )(own, recv)


def _adam_vals(w, g, m, v):
    m = ADAM_B1 * m + (1.0 - ADAM_B1) * g
    v = ADAM_B2 * v + (1.0 - ADAM_B2) * (g * g)
    m_hat = m / (1.0 - ADAM_B1 ** ADAM_STEP)
    v_hat = v / (1.0 - ADAM_B2 ** ADAM_STEP)
    delta = -ADAM_LR * (m_hat / (jnp.sqrt(v_hat) + ADAM_EPS) + ADAM_WD * w)
    return delta, m, v


def _adamw(name, w, m, v, p_south, p_north):
    R, C = w.shape
    tr = _row_tile(R)

    def body(w_ref, m_ref, v_ref, s_ref, n_ref, g_ref, d_ref, nm_ref, nv_ref):
        g = s_ref[...] + n_ref[...]
        d, nm, nv = _adam_vals(w_ref[...], g, m_ref[...], v_ref[...])
        g_ref[...], d_ref[...], nm_ref[...], nv_ref[...] = g, d, nm, nv

    spec = pl.BlockSpec((tr, C), lambda i: (i, 0))
    return pl.pallas_call(
        body, name=name, grid=(R // tr,), in_specs=[spec] * 5, out_specs=[spec] * 4,
        out_shape=[jax.ShapeDtypeStruct((R, C), F32)] * 4,
        compiler_params=_params(("parallel",)),
    )(w, m, v, p_south, p_north)


def _adamw_small(w, m, v, g_all):
    def body(w_ref, m_ref, v_ref, a_ref, g_ref, d_ref, nm_ref, nv_ref):
        g = a_ref[0]
        for dev in range(1, 8):
            g = g + a_ref[dev]
        d, nm, nv = _adam_vals(w_ref[...], g, m_ref[...], v_ref[...])
        g_ref[...], d_ref[...], nm_ref[...], nv_ref[...] = g, d, nm, nv

    return pl.pallas_call(
        body, name="adamw_small", out_shape=[jax.ShapeDtypeStruct(w.shape, F32)] * 4,
    )(w, m, v, g_all)


SMALL_LAYOUT = dict(norm_mix_g=(0, 1, 1024), b_in=(1, 8, 7424), hgrn_norm_g=(9, 1, 1024), norm_ffn_g=(10, 1, 1024),
                    norm_final_g=(11, 1, 1024), hgrn_lb_logits=(12, 2, 2048), attn_sinks=(14, 1, 16))


def _pack_small(vals):
    rows = []
    for name, (_, nrows, nvalid) in SMALL_LAYOUT.items():
        flat = vals[name].astype(F32).reshape(-1)
        rows.append(jnp.pad(flat, (0, nrows * D_MODEL - nvalid)).reshape(nrows, D_MODEL))
    rows.append(jnp.zeros((SMALL_ROWS - 15, D_MODEL), F32))
    return jnp.concatenate(rows, axis=0)


def _unpack_small(packed, shapes):
    return {name: packed[r0:r0 + nrows].reshape(-1)[:nvalid].reshape(shapes[name])
            for name, (r0, nrows, nvalid) in SMALL_LAYOUT.items()}


MATRICES = ("w_in", "w_branch_attn", "w_branch_hgrn", "w_out", "w_ffn_gate", "w_ffn_up", "w_ffn_down")
COLUMN_SHARDED = ("w_in", "w_ffn_gate", "w_ffn_up")
WEIGHTS = ("norm_mix_g", "w_in", "b_in", "attn_sinks", "hgrn_lb_logits", "hgrn_norm_g", "w_branch_attn",
           "w_branch_hgrn", "w_out", "norm_ffn_g", "w_ffn_gate", "w_ffn_up", "w_ffn_down", "norm_final_g")


def kernel(x, norm_mix_g, w_in, b_in, attn_sinks, hgrn_lb_logits, hgrn_norm_g, w_branch_attn, w_branch_hgrn, w_out, norm_ffn_g, w_ffn_gate, w_ffn_up, w_ffn_down, norm_final_g, loss_target, m_norm_mix_g, m_w_in, m_b_in, m_attn_sinks, m_hgrn_lb_logits, m_hgrn_norm_g, m_w_branch_attn, m_w_branch_hgrn, m_w_out, m_norm_ffn_g, m_w_ffn_gate, m_w_ffn_up, m_w_ffn_down, m_norm_final_g, v_norm_mix_g, v_w_in, v_b_in, v_attn_sinks, v_hgrn_lb_logits, v_hgrn_norm_g, v_w_branch_attn, v_w_branch_hgrn, v_w_out, v_norm_ffn_g, v_w_ffn_gate, v_w_ffn_up, v_w_ffn_down, v_norm_final_g):
    given = dict(locals())
    w = {n: given[n] for n in WEIGHTS}
    m = {n: given["m_" + n] for n in WEIGHTS}
    v = {n: given["v_" + n] for n in WEIGHTS}
    xi, yi, ci = _place()
    chip = 2 * xi + yi

    gathered = _all_gather([w[n][0].astype(MXU_DTYPE) for n in MATRICES])
    full = {}
    for n, g in zip(MATRICES, gathered):
        if n in COLUMN_SHARDED:
            full[n] = jnp.concatenate([g[s] for s in range(N_CHIPS)], axis=1)
        else:
            full[n] = g.reshape(-1, g.shape[-1])

    loss_part, dx, d_mats, d_vecs = _local_step(
        x[0], loss_target[0], norm_mix_g, full["w_in"], b_in, attn_sinks, hgrn_lb_logits, hgrn_norm_g,
        full["w_branch_attn"], full["w_branch_hgrn"], full["w_out"], norm_ffn_g, full["w_ffn_gate"],
        full["w_ffn_up"], full["w_ffn_down"], norm_final_g.reshape(1, D_MODEL))

    stacked32 = []
    for n in MATRICES:
        d = d_mats[n]
        if n in COLUMN_SHARDED:
            stacked32.append(jnp.stack(jnp.split(d, N_CHIPS, axis=1)))
        else:
            stacked32.append(d.reshape(N_CHIPS, -1, d.shape[-1]))
    received, small_all = _exchange_grads([s.astype(MXU_DTYPE) for s in stacked32], _pack_small(d_vecs))
    sums = [_partial_sum("sum_" + n, lax.dynamic_index_in_dim(s, chip, 0, keepdims=False), r)
            for n, s, r in zip(MATRICES, stacked32, received)]
    theirs = _exchange_sibling(sums)

    grads, deltas, new_m, new_v = {}, {}, {}, {}
    for n, mine, other in zip(MATRICES, sums, theirs):
        south = jnp.where(ci == 0, mine, other)
        north = jnp.where(ci == 0, other, mine)
        res = _adamw("adamw_" + n, w[n][0], m[n][0], v[n][0], south, north)
        grads[n], deltas[n], new_m[n], new_v[n] = (r[None] for r in res)
    shapes = {n: w[n].shape for n in SMALL_LAYOUT}
    res = _adamw_small(_pack_small(w), _pack_small(m), _pack_small(v), small_all)
    for dst, packed in zip((grads, deltas, new_m, new_v), res):
        dst.update(_unpack_small(packed, shapes))

    loss = lax.psum(loss_part, ("x", "y", "c"))
    return (loss, dx[None], *[grads[n] for n in WEIGHTS], *[deltas[n] for n in WEIGHTS],
            *[new_m[n] for n in WEIGHTS], *[new_v[n] for n in WEIGHTS])
```

```python
import functools
import math

import jax
import jax.numpy as jnp
from jax import lax
from jax.experimental import pallas as pl
from jax.experimental.pallas import tpu as pltpu

F32 = jnp.float32
BF16 = jnp.bfloat16
MXU_DTYPE = jnp.bfloat16
MESH_ID = pl.DeviceIdType.MESH

D_MODEL = 1024
HEAD_DIM = 64
Q_HEADS = 16
KV_HEADS = 2
GROUP = Q_HEADS // KV_HEADS
KV_WIDTH = KV_HEADS * HEAD_DIM
ATTN_BLOCK = 128
HGRN_HEADS = 8
HGRN_K = 128
CHUNK = 64
HGRN_TOKENS = 256
FFN = 2816
IN_SPLITS = (1024, 256, 4096, 2048)
EPS = 1e-6
NEG_INF = -1e30
ADAM_LR, ADAM_B1, ADAM_B2, ADAM_EPS, ADAM_WD, ADAM_STEP = 0.001, 0.9, 0.999, 1e-08, 0.01, 10
N_CHIPS = 4
VMEM_LIMIT = 56 * 1024 * 1024


def _params(sem=None):
    return pltpu.CompilerParams(dimension_semantics=sem, vmem_limit_bytes=VMEM_LIMIT)


def _sigmoid(v):
    return 1.0 / (1.0 + jnp.exp(-v))


def _dot(a, b, dims):
    return lax.dot_general(a.astype(MXU_DTYPE), b.astype(MXU_DTYPE), (dims, ((), ())),
                           preferred_element_type=F32)


def _nn(a, b):
    return _dot(a, b, ((1,), (0,)))


def _nt(a, b):
    return _dot(a, b, ((1,), (1,)))


def _tn(a, b):
    return _dot(a, b, ((0,), (0,)))


HBM_SPEC = pl.BlockSpec(memory_space=pl.ANY)


class _Carried:
    def __init__(self, arrays, out_shapes, n_remote, n_local, build):
        self.arrays, self.out_shapes, self.build = list(arrays), list(out_shapes), build
        self.scratch = [pltpu.SemaphoreType.DMA((n_remote,)), pltpu.SemaphoreType.DMA((n_remote,)),
                        pltpu.SemaphoreType.DMA((max(n_local, 1),))]

    def start(self, ins, outs, sems):
        sends, _, local = self.build(ins, outs, *sems)
        for cp in local + sends:
            cp.start()

    def wait(self, ins, outs, sems):
        sends, recvs, local = self.build(ins, outs, *sems)
        for cp in recvs:
            cp.wait_recv()
        for cp in sends:
            cp.wait_send()
        for cp in local:
            cp.wait()


def _call(name, body, *, grid, in_specs, out_specs, out_shape, args, scratch=(), semantics=None, comm=None):
    n_in, n_out, n_scr = len(in_specs), len(out_specs), len(scratch)
    if comm is None:
        res = pl.pallas_call(body, name=name, grid=grid, in_specs=in_specs, out_specs=out_specs, out_shape=out_shape,
                             scratch_shapes=list(scratch), compiler_params=_params(semantics))(*args)
        return list(res), []
    ci, co = len(comm.arrays), len(comm.out_shapes)

    def carrying(*refs):
        ins, refs = refs[:n_in], refs[n_in:]
        c_ins, refs = refs[:ci], refs[ci:]
        outs, refs = refs[:n_out], refs[n_out:]
        c_outs, refs = refs[:co], refs[co:]
        scr, sems = refs[:n_scr], refs[n_scr:]
        if not grid:
            comm.start(c_ins, c_outs, sems)
            body(*ins, *outs, *scr)
            comm.wait(c_ins, c_outs, sems)
            return
        first = functools.reduce(jnp.logical_and, [pl.program_id(a) == 0 for a in range(len(grid))])
        last = functools.reduce(jnp.logical_and, [pl.program_id(a) == g - 1 for a, g in enumerate(grid)])

        @pl.when(first)
        def _():
            comm.start(c_ins, c_outs, sems)

        body(*ins, *outs, *scr)

        @pl.when(last)
        def _():
            comm.wait(c_ins, c_outs, sems)

    res = pl.pallas_call(
        carrying, name=name, grid=grid, in_specs=list(in_specs) + [HBM_SPEC] * ci,
        out_specs=list(out_specs) + [HBM_SPEC] * co, out_shape=list(out_shape) + comm.out_shapes,
        scratch_shapes=list(scratch) + comm.scratch,
        compiler_params=_params(("arbitrary",) * len(grid) if grid else None),
    )(*args, *comm.arrays)
    return list(res[:n_out]), list(res[n_out:])


_NO_COPIES = object()


def _matmul(name, pairs, mode, *, tm, tn, tk, outs, extras=(), epilogue=None, comm=_NO_COPIES):
    prod = dict(nn=_nn, nt=_nt, tn=_tn)[mode]
    a0, b0 = pairs[0]
    M = a0.shape[1] if mode == "tn" else a0.shape[0]
    N = b0.shape[0] if mode == "nt" else b0.shape[1]
    steps, in_specs, offset = [], [], 0
    for a, b in pairs:
        K = a.shape[0] if mode == "tn" else a.shape[1]
        t = min(tk, K)
        assert K % t == 0, (name, K, t)
        kmap = functools.partial(lambda k, off, n: jnp.clip(k - off, 0, n - 1), off=offset, n=K // t)
        if mode == "tn":
            in_specs.append(pl.BlockSpec((t, tm), functools.partial(lambda i, j, k, f: (f(k), i), f=kmap)))
        else:
            in_specs.append(pl.BlockSpec((tm, t), functools.partial(lambda i, j, k, f: (i, f(k)), f=kmap)))
        if mode == "nt":
            in_specs.append(pl.BlockSpec((tn, t), functools.partial(lambda i, j, k, f: (j, f(k)), f=kmap)))
        else:
            in_specs.append(pl.BlockSpec((t, tn), functools.partial(lambda i, j, k, f: (f(k), j), f=kmap)))
        steps.append((offset, offset + K // t))
        offset += K // t
    assert M % tm == 0 and N % tn == 0, (name, M, N, tm, tn)
    ni, nj, nk = M // tm, N // tn, offset
    npair, ne, no = len(pairs), len(extras), len(outs)
    if epilogue is None:
        epilogue = lambda acc: (acc,)

    def finish(acc, extra_refs, out_refs):
        vals = epilogue(acc, *[r[...] for r in extra_refs])
        for (kind, _), o_ref, val in zip(outs, out_refs, vals):
            if kind == "pn":
                val = jnp.broadcast_to(val, o_ref.shape)
            o_ref[...] = val.astype(o_ref.dtype)

    def body(*refs):
        ab, rest = refs[:2 * npair], refs[2 * npair:]
        extra_refs, out_refs = rest[:ne], rest[ne:ne + no]
        if nk == 1:
            finish(prod(ab[0][...], ab[1][...]), extra_refs, out_refs)
            return
        acc_ref = rest[ne + no]
        k = pl.program_id(2)

        @pl.when(k == 0)
        def _():
            acc_ref[...] = jnp.zeros_like(acc_ref)

        for p, (lo, hi) in enumerate(steps):
            @pl.when(jnp.logical_and(k >= lo, k < hi))
            def _():
                acc_ref[...] += prod(ab[2 * p][...], ab[2 * p + 1][...])

        @pl.when(k == nk - 1)
        def _():
            finish(acc_ref[...], extra_refs, out_refs)

    for _, shape, im in extras:
        in_specs.append(pl.BlockSpec(shape, functools.partial(lambda i, j, k, im: im(i, j), im=im)))
    out_shape, out_specs = [], []
    for kind, dt in outs:
        if kind == "mn":
            out_shape.append(jax.ShapeDtypeStruct((M, N), dt))
            out_specs.append(pl.BlockSpec((tm, tn), lambda i, j, k: (i, j)))
        elif kind == "m":
            assert nj == 1
            out_shape.append(jax.ShapeDtypeStruct((M, 1), dt))
            out_specs.append(pl.BlockSpec((tm, 1), lambda i, j, k: (i, 0)))
        else:
            out_shape.append(jax.ShapeDtypeStruct((8 * ni, N), dt))
            out_specs.append(pl.BlockSpec((8, tn), lambda i, j, k: (i, j)))
    res, got = _call(name, body, grid=(ni, nj, nk), in_specs=in_specs, out_specs=out_specs, out_shape=out_shape,
                     args=[t for pair in pairs for t in pair] + [e[0] for e in extras],
                     scratch=[pltpu.VMEM((tm, tn), F32)] if nk > 1 else [],
                     semantics=("parallel", "parallel", "arbitrary"), comm=None if comm is _NO_COPIES else comm)
    return res if comm is _NO_COPIES else (res, got)


def _colsum_partials(p):
    return jnp.sum(p.reshape(-1, 8, p.shape[-1])[:, 0, :], axis=0, keepdims=True)


def _rmsnorm_fwd(name, x, g, tr=512):
    T, D = x.shape

    def body(x_ref, g_ref, u_ref, r_ref):
        xv = x_ref[...]
        r = lax.rsqrt(jnp.mean(xv * xv, axis=-1, keepdims=True) + EPS)
        u_ref[...] = (xv * r * g_ref[...]).astype(u_ref.dtype)
        r_ref[...] = r

    return pl.pallas_call(
        body, name=name, grid=(T // tr,),
        in_specs=[pl.BlockSpec((tr, D), lambda i: (i, 0)), pl.BlockSpec((1, D), lambda i: (0, 0))],
        out_specs=[pl.BlockSpec((tr, D), lambda i: (i, 0)), pl.BlockSpec((tr, 1), lambda i: (i, 0))],
        out_shape=[jax.ShapeDtypeStruct((T, D), MXU_DTYPE), jax.ShapeDtypeStruct((T, 1), F32)],
        compiler_params=_params(("parallel",)),
    )(x, g)


def _rmsnorm_bwd_vals(dy, xin, rstd, g):
    xhat = xin * rstd
    dg = jnp.sum(dy * xhat, axis=0, keepdims=True)
    dxh = dy * g
    dx = rstd * (dxh - xhat * jnp.mean(dxh * xhat, axis=-1, keepdims=True))
    return dx, dg


def _colsum(name, a, tr=512):
    T, N = a.shape

    def body(a_ref, o_ref):
        @pl.when(pl.program_id(0) == 0)
        def _():
            o_ref[...] = jnp.zeros_like(o_ref)

        o_ref[...] += jnp.sum(a_ref[...].astype(F32), axis=0, keepdims=True)

    return pl.pallas_call(
        body, name=name, grid=(T // tr,),
        in_specs=[pl.BlockSpec((tr, N), lambda i: (i, 0))],
        out_specs=pl.BlockSpec((1, N), lambda i: (0, 0)),
        out_shape=jax.ShapeDtypeStruct((1, N), F32),
        compiler_params=_params(("arbitrary",)),
    )(a)


def _attn_masks(n):
    qi = lax.broadcasted_iota(jnp.int32, (ATTN_BLOCK, ATTN_BLOCK), 0)
    kj = lax.broadcasted_iota(jnp.int32, (ATTN_BLOCK, ATTN_BLOCK), 1)
    return kj <= qi, jnp.logical_and(kj > qi, n > 0)


def _attn_fwd(pq, pkv, sinks, comm=None):
    T = pq.shape[0]
    nb = T // ATTN_BLOCK
    scale = 1.0 / math.sqrt(HEAD_DIM)

    def body(q_ref, kvc_ref, kvp_ref, s_ref, y_ref, lse_ref):
        mask_c, mask_p = _attn_masks(pl.program_id(0))
        for g in range(KV_HEADS):
            ks = slice(g * HEAD_DIM, (g + 1) * HEAD_DIM)
            vs = slice(KV_WIDTH + g * HEAD_DIM, KV_WIDTH + (g + 1) * HEAD_DIM)
            kc, vc = kvc_ref[:, ks].astype(MXU_DTYPE), kvc_ref[:, vs].astype(MXU_DTYPE)
            kp, vp = kvp_ref[:, ks].astype(MXU_DTYPE), kvp_ref[:, vs].astype(MXU_DTYPE)
            for h in range(g * GROUP, (g + 1) * GROUP):
                hs = slice(h * HEAD_DIM, (h + 1) * HEAD_DIM)
                q = q_ref[:, hs].astype(MXU_DTYPE)
                sc = jnp.where(mask_c, _nt(q, kc) * scale, NEG_INF)
                sp = jnp.where(mask_p, _nt(q, kp) * scale, NEG_INF)
                sink = s_ref[0, h]
                m = jnp.maximum(jnp.maximum(jnp.max(sc, axis=-1, keepdims=True),
                                            jnp.max(sp, axis=-1, keepdims=True)), sink)
                pc, pp = jnp.exp(sc - m), jnp.exp(sp - m)
                den = (jnp.sum(pc, axis=-1, keepdims=True) + jnp.sum(pp, axis=-1, keepdims=True)
                       + jnp.exp(sink - m))
                o = (_nn(pc, vc) + _nn(pp, vp)) / den
                y_ref[:, hs] = o.astype(y_ref.dtype)
                lse_ref[:, h:h + 1] = m + jnp.log(den)

    return _call(
        "attn_fwd", body, grid=(nb,),
        in_specs=[pl.BlockSpec((ATTN_BLOCK, D_MODEL), lambda n: (n, 0)),
                  pl.BlockSpec((ATTN_BLOCK, 2 * KV_WIDTH), lambda n: (n, 0)),
                  pl.BlockSpec((ATTN_BLOCK, 2 * KV_WIDTH), lambda n: (jnp.maximum(n - 1, 0), 0)),
                  pl.BlockSpec(memory_space=pltpu.SMEM)],
        out_specs=[pl.BlockSpec((ATTN_BLOCK, D_MODEL), lambda n: (n, 0)),
                   pl.BlockSpec((ATTN_BLOCK, Q_HEADS), lambda n: (n, 0))],
        out_shape=[jax.ShapeDtypeStruct((T, D_MODEL), MXU_DTYPE), jax.ShapeDtypeStruct((T, Q_HEADS), F32)],
        args=[pq, pkv, pkv, sinks], semantics=("parallel",), comm=comm)


def _attn_bwd(pq, pkv, sinks, lse, dy, comm=None):
    T = pq.shape[0]
    nb = T // ATTN_BLOCK
    scale = 1.0 / math.sqrt(HEAD_DIM)
    cur = lambda n: (jnp.minimum(n, nb - 1), 0)

    def body(q_ref, kvc_ref, kvp_ref, s_ref, lse_ref, dy_ref, dq_ref, dkv_ref, ds_ref, carry, top, bot):
        n = pl.program_id(0)

        @pl.when(n == 0)
        def _():
            carry[...] = jnp.zeros_like(carry)
            ds_ref[...] = jnp.zeros_like(ds_ref)

        @pl.when(n < nb)
        def _():
            mask_c, mask_p = _attn_masks(n)
            row = lax.broadcasted_iota(jnp.int32, (8, 128), 0)
            col = lax.broadcasted_iota(jnp.int32, (8, 128), 1)
            dsink = jnp.zeros((8, 128), F32)
            for g in range(KV_HEADS):
                ks = slice(g * HEAD_DIM, (g + 1) * HEAD_DIM)
                vs = slice(KV_WIDTH + g * HEAD_DIM, KV_WIDTH + (g + 1) * HEAD_DIM)
                kc, vc = kvc_ref[:, ks].astype(MXU_DTYPE), kvc_ref[:, vs].astype(MXU_DTYPE)
                kp, vp = kvp_ref[:, ks].astype(MXU_DTYPE), kvp_ref[:, vs].astype(MXU_DTYPE)
                dkc = jnp.zeros((ATTN_BLOCK, HEAD_DIM), F32)
                dkp, dvc, dvp = dkc, dkc, dkc
                for h in range(g * GROUP, (g + 1) * GROUP):
                    hs = slice(h * HEAD_DIM, (h + 1) * HEAD_DIM)
                    q = q_ref[:, hs].astype(MXU_DTYPE)
                    do = dy_ref[:, hs].astype(MXU_DTYPE)
                    lse_h = lse_ref[:, h:h + 1]
                    pc = jnp.where(mask_c, jnp.exp(_nt(q, kc) * scale - lse_h), 0.0)
                    pp = jnp.where(mask_p, jnp.exp(_nt(q, kp) * scale - lse_h), 0.0)
                    dpc, dpp = _nt(do, vc), _nt(do, vp)
                    delta = (jnp.sum(pc * dpc, axis=-1, keepdims=True)
                             + jnp.sum(pp * dpp, axis=-1, keepdims=True))
                    dsc = pc * (dpc - delta) * scale
                    dsp = pp * (dpp - delta) * scale
                    dq_ref[:, hs] = (_nn(dsc, kc) + _nn(dsp, kp)).astype(dq_ref.dtype)
                    dkc = dkc + _tn(dsc, q)
                    dkp = dkp + _tn(dsp, q)
                    dvc = dvc + _tn(pc, do)
                    dvp = dvp + _tn(pp, do)
                    p_sink = jnp.exp(s_ref[0, h] - lse_h)
                    val = -jnp.sum(p_sink * delta, axis=0, keepdims=True)
                    dsink = dsink + jnp.where(jnp.logical_and(row == 0, col == h), val, 0.0)
                bot[:, ks], bot[:, vs] = dkc, dvc
                top[:, ks], top[:, vs] = dkp, dvp
            ds_ref[...] += dsink
            dkv_ref[...] = (carry[...] + top[...]).astype(dkv_ref.dtype)
            carry[...] = bot[...]

        @pl.when(n == nb)
        def _():
            dkv_ref[...] = carry[...].astype(dkv_ref.dtype)

    return _call(
        "attn_bwd", body, grid=(nb + 1,),
        in_specs=[pl.BlockSpec((ATTN_BLOCK, D_MODEL), cur),
                  pl.BlockSpec((ATTN_BLOCK, 2 * KV_WIDTH), cur),
                  pl.BlockSpec((ATTN_BLOCK, 2 * KV_WIDTH), lambda n: (jnp.maximum(jnp.minimum(n, nb - 1) - 1, 0), 0)),
                  pl.BlockSpec(memory_space=pltpu.SMEM),
                  pl.BlockSpec((ATTN_BLOCK, Q_HEADS), cur),
                  pl.BlockSpec((ATTN_BLOCK, D_MODEL), cur)],
        out_specs=[pl.BlockSpec((ATTN_BLOCK, D_MODEL), cur),
                   pl.BlockSpec((ATTN_BLOCK, 2 * KV_WIDTH), lambda n: (jnp.maximum(n - 1, 0), 0)),
                   pl.BlockSpec((8, 128), lambda n: (0, 0))],
        out_shape=[jax.ShapeDtypeStruct((T, D_MODEL), MXU_DTYPE),
                   jax.ShapeDtypeStruct((T, 2 * KV_WIDTH), MXU_DTYPE),
                   jax.ShapeDtypeStruct((8, 128), F32)],
        scratch=[pltpu.VMEM((ATTN_BLOCK, 2 * KV_WIDTH), F32)] * 3,
        args=[pq, pkv, pkv, sinks, lse, dy], semantics=("arbitrary",), comm=comm)


def _lower_bound(l):
    m = jnp.maximum(l[0:1], l[1:2])
    e0, e1 = jnp.exp(l[0:1] - m), jnp.exp(l[1:2] - m)
    return e0 / (e0 + e1)


def _tri(lower):
    r = lax.broadcasted_iota(jnp.int32, (CHUNK, CHUNK), 0)
    c = lax.broadcasted_iota(jnp.int32, (CHUNK, CHUNK), 1)
    return (r >= c) if lower else (c >= r)


def _chunk_sum(mask, v):
    return lax.dot_general(mask.astype(F32), v, (((1,), (0,)), ((), ())),
                           precision=lax.Precision.HIGHEST, preferred_element_type=F32)


def _hgrn_chunk_inputs(hq, hf, lb, causal):
    sg, sgn = _sigmoid(hf), _sigmoid(-hf)
    f = lb + (1.0 - lb) * sg
    kk = (1.0 - lb) * sgn
    sq = _sigmoid(hq)
    q = hq * sq
    b = _chunk_sum(causal, jnp.log(f))
    bm, bl = b[CHUNK // 2 - 1:CHUNK // 2, :], b[CHUNK - 1:CHUNK, :]
    e_qm, e_km, e_qs, e_kl = jnp.exp(b - bm), jnp.exp(bm - b), jnp.exp(b), jnp.exp(bl - b)
    return dict(sg=sg, sgn=sgn, f=f, kk=kk, sq=sq, q=q, e_qm=e_qm, e_km=e_km, e_qs=e_qs, e_kl=e_kl,
                qm=q * e_qm, km=kk * e_km, qs=q * e_qs, kl=kk * e_kl, el=jnp.exp(bl))


def _hgrn_fwd(ph, lb_logits, norm_g, comm=None):
    T = ph.shape[0]
    nblk, cpb = T // HGRN_TOKENS, HGRN_TOKENS // CHUNK
    col = lambda c: pl.BlockSpec((HGRN_TOKENS, D_MODEL), functools.partial(lambda i, c: (i, c), c=c))

    def body(hq_ref, hf_ref, hi_ref, hg_ref, l_ref, ng_ref, y_ref, o_ref, st_ref, s_ref):
        @pl.when(pl.program_id(0) == 0)
        def _():
            s_ref[...] = jnp.zeros_like(s_ref)

        lb = _lower_bound(l_ref[...])
        causal = _tri(True)
        for c in range(cpb):
            rows = slice(c * CHUNK, (c + 1) * CHUNK)
            t = _hgrn_chunk_inputs(hq_ref[rows, :], hf_ref[rows, :], lb, causal)
            qm, km, qs, kl = (t[n].astype(MXU_DTYPE) for n in ("qm", "km", "qs", "kl"))
            v = hi_ref[rows, :].astype(MXU_DTYPE)
            for h in range(HGRN_HEADS):
                ls = slice(h * HGRN_K, (h + 1) * HGRN_K)
                st = s_ref[h]
                st_ref[c, ls, :] = st
                a = jnp.where(causal, _nt(qm[:, ls], km[:, ls]), 0.0)
                o_ref[rows, ls] = _nn(a, v[:, ls]) + _nt(qs[:, ls], st)
                s_ref[h] = t["el"][:, ls] * st + _tn(v[:, ls], kl[:, ls])
        for h in range(HGRN_HEADS):
            ls = slice(h * HGRN_K, (h + 1) * HGRN_K)
            o = o_ref[:, ls]
            r = lax.rsqrt(jnp.mean(o * o, axis=-1, keepdims=True) + EPS)
            y_ref[:, ls] = (o * r * ng_ref[:, ls] * _sigmoid(hg_ref[:, ls])).astype(y_ref.dtype)

    return _call(
        "hgrn_fwd", body, grid=(nblk,),
        in_specs=[col(0), col(1), col(2), col(3),
                  pl.BlockSpec((2, D_MODEL), lambda i: (0, 0)), pl.BlockSpec((1, D_MODEL), lambda i: (0, 0))],
        out_specs=[pl.BlockSpec((HGRN_TOKENS, D_MODEL), lambda i: (i, 0)),
                   pl.BlockSpec((HGRN_TOKENS, D_MODEL), lambda i: (i, 0)),
                   pl.BlockSpec((cpb, D_MODEL, HGRN_K), lambda i: (i, 0, 0))],
        out_shape=[jax.ShapeDtypeStruct((T, D_MODEL), MXU_DTYPE), jax.ShapeDtypeStruct((T, D_MODEL), F32),
                   jax.ShapeDtypeStruct((T // CHUNK, D_MODEL, HGRN_K), F32)],
        scratch=[pltpu.VMEM((HGRN_HEADS, HGRN_K, HGRN_K), F32)],
        args=[ph, ph, ph, ph, lb_logits, norm_g], semantics=("arbitrary",), comm=comm)


def _hgrn_bwd(ph, o_raw, states, dy, lb_logits, norm_g, comm=None):
    T = ph.shape[0]
    nblk, cpb = T // HGRN_TOKENS, HGRN_TOKENS // CHUNK
    rev = lambda i: nblk - 1 - i
    col = lambda c: pl.BlockSpec((HGRN_TOKENS, D_MODEL), functools.partial(lambda i, c: (rev(i), c), c=c))
    tok = pl.BlockSpec((HGRN_TOKENS, D_MODEL), lambda i: (rev(i), 0))

    def body(hq_ref, hf_ref, hi_ref, hg_ref, o_ref, st_ref, dy_ref, l_ref, ng_ref,
             dph_ref, dng_ref, dl_ref, dst_ref, dlb_ref, do_s, dqm_s, dkm_s, dqs_s, dkl_s, dv_s, del_s):
        i = pl.program_id(0)

        @pl.when(i == 0)
        def _():
            dst_ref[...] = jnp.zeros_like(dst_ref)
            dlb_ref[...] = jnp.zeros_like(dlb_ref)
            dng_ref[...] = jnp.zeros_like(dng_ref)

        lb = _lower_bound(l_ref[...])
        causal, anti = _tri(True), _tri(False)
        row = lax.broadcasted_iota(jnp.int32, (CHUNK, D_MODEL), 0)
        for c in reversed(range(cpb)):
            rows = slice(c * CHUNK, (c + 1) * CHUNK)
            hq = hq_ref[rows, :]
            t = _hgrn_chunk_inputs(hq, hf_ref[rows, :], lb, causal)
            sgg = _sigmoid(hg_ref[rows, :])
            dyv = dy_ref[rows, :]
            for h in range(HGRN_HEADS):
                ls = slice(h * HGRN_K, (h + 1) * HGRN_K)
                o = o_ref[rows, ls]
                r = lax.rsqrt(jnp.mean(o * o, axis=-1, keepdims=True) + EPS)
                nrm = o * r
                g_h = sgg[:, ls]
                dph_ref[rows, 3 * D_MODEL + h * HGRN_K:3 * D_MODEL + (h + 1) * HGRN_K] = (
                    dyv[:, ls] * nrm * ng_ref[:, ls] * g_h * (1.0 - g_h)).astype(dph_ref.dtype)
                dyg = dyv[:, ls] * g_h
                dng_ref[:, ls] += jnp.sum(dyg * nrm, axis=0, keepdims=True)
                dn = dyg * ng_ref[:, ls]
                do_s[:, ls] = r * (dn - nrm * jnp.mean(dn * nrm, axis=-1, keepdims=True))
            qm, km, qs, kl = (t[n].astype(MXU_DTYPE) for n in ("qm", "km", "qs", "kl"))
            v = hi_ref[rows, :].astype(MXU_DTYPE)
            do = do_s[...].astype(MXU_DTYPE)
            for h in range(HGRN_HEADS):
                ls = slice(h * HGRN_K, (h + 1) * HGRN_K)
                st = st_ref[c, ls, :]
                dst = dst_ref[h]
                a = jnp.where(causal, _nt(qm[:, ls], km[:, ls]), 0.0)
                da = jnp.where(causal, _nt(do[:, ls], v[:, ls]), 0.0)
                dv_s[:, ls] = _tn(a, do[:, ls]) + _nt(kl[:, ls], dst)
                dkl_s[:, ls] = _nn(v[:, ls], dst)
                dqs_s[:, ls] = _nn(do[:, ls], st)
                del_s[:, ls] = jnp.sum(dst * st, axis=0, keepdims=True)
                dst_ref[h] = _tn(do[:, ls], qs[:, ls]) + t["el"][:, ls] * dst
                dqm_s[:, ls] = _nn(da, km[:, ls])
                dkm_s[:, ls] = _tn(da, qm[:, ls])
            dqm, dkm, dqs, dkl = dqm_s[...], dkm_s[...], dqs_s[...], dkl_s[...]
            dq = dqm * t["e_qm"] + dqs * t["e_qs"]
            dk = dkm * t["e_km"] + dkl * t["e_kl"]
            t_qm, t_km, t_kl = dqm * t["qm"], dkm * t["km"], dkl * t["kl"]
            db = t_qm - t_km + dqs * t["qs"] - t_kl
            db_mid = jnp.sum(t_km - t_qm, axis=0, keepdims=True)
            db_last = jnp.sum(t_kl, axis=0, keepdims=True) + del_s[...] * t["el"]
            db = db + jnp.where(row == CHUNK // 2 - 1, db_mid, 0.0) + jnp.where(row == CHUNK - 1, db_last, 0.0)
            dlogf = _chunk_sum(anti, db)
            sq, sg, sgn, f = t["sq"], t["sg"], t["sgn"], t["f"]
            dph_ref[rows, 0:D_MODEL] = (dq * (sq * (1.0 + hq * (1.0 - sq)))).astype(dph_ref.dtype)
            dph_ref[rows, D_MODEL:2 * D_MODEL] = (
                dlogf * (1.0 - lb) * sg * (1.0 - sg) / f - dk * (1.0 - lb) * sgn * (1.0 - sgn)).astype(dph_ref.dtype)
            dph_ref[rows, 2 * D_MODEL:3 * D_MODEL] = dv_s[...].astype(dph_ref.dtype)
            dlb_ref[...] += jnp.sum(dlogf * (1.0 - sg) / f - dk * sgn, axis=0, keepdims=True)

        @pl.when(i == nblk - 1)
        def _():
            dl0 = dlb_ref[...] * lb * (1.0 - lb)
            dl_ref[0:1, :] = dl0
            dl_ref[1:2, :] = -dl0

    wide = pltpu.VMEM((CHUNK, D_MODEL), F32)
    return _call(
        "hgrn_bwd", body, grid=(nblk,),
        in_specs=[col(0), col(1), col(2), col(3), tok,
                  pl.BlockSpec((cpb, D_MODEL, HGRN_K), lambda i: (rev(i), 0, 0)), tok,
                  pl.BlockSpec((2, D_MODEL), lambda i: (0, 0)), pl.BlockSpec((1, D_MODEL), lambda i: (0, 0))],
        out_specs=[pl.BlockSpec((HGRN_TOKENS, 4 * D_MODEL), lambda i: (rev(i), 0)),
                   pl.BlockSpec((1, D_MODEL), lambda i: (0, 0)), pl.BlockSpec((2, D_MODEL), lambda i: (0, 0))],
        out_shape=[jax.ShapeDtypeStruct((T, 4 * D_MODEL), MXU_DTYPE), jax.ShapeDtypeStruct((1, D_MODEL), F32),
                   jax.ShapeDtypeStruct((2, D_MODEL), F32)],
        scratch=[pltpu.VMEM((HGRN_HEADS, HGRN_K, HGRN_K), F32), pltpu.VMEM((1, D_MODEL), F32),
                 wide, wide, wide, wide, wide, wide, pltpu.VMEM((1, D_MODEL), F32)],
        args=[ph, ph, ph, ph, o_raw, states, dy, lb_logits, norm_g], semantics=("arbitrary",), comm=comm)


def _local_step(x, target, vec, net):
    T, D = x.shape
    norm_mix_g, b_in, sinks, lb_logits = vec["norm_mix_g"], vec["b_in"], vec["attn_sinks"], vec["hgrn_lb_logits"]
    hgrn_norm_g, norm_ffn_g, norm_final_g = vec["hgrn_norm_g"], vec["norm_ffn_g"], vec["norm_final_g"]
    w_in = net.full("w_in")
    o_q, o_kv, o_h, o_g = (sum(IN_SPLITS[:i]) for i in range(4))
    w_q, w_kv, w_h, w_g = (w_in[:, o:o + n] for o, n in zip((o_q, o_kv, o_h, o_g), IN_SPLITS))
    b_q, b_kv, b_h, b_g = (b_in[:, o:o + n] for o, n in zip((o_q, o_kv, o_h, o_g), IN_SPLITS))
    bias = lambda acc, b: (acc + b,)
    row_vec = lambda n: ((1, n), lambda i, j: (0, j))
    tile = lambda tm, tn: ((tm, tn), lambda i, j: (i, j))
    TM = 512

    u, rstd1 = _rmsnorm_fwd("norm_mix", x, norm_mix_g)
    pq, = _matmul("in_q", [(u, w_q)], "nn", tm=TM, tn=1024, tk=1024, outs=[("mn", F32)],
                  extras=[(b_q, *row_vec(1024))], epilogue=bias)
    pkv, = _matmul("in_kv", [(u, w_kv)], "nn", tm=TM, tn=256, tk=1024, outs=[("mn", F32)],
                   extras=[(b_kv, *row_vec(256))], epilogue=bias)
    names = ("w_branch_attn", "w_branch_hgrn", "w_out")
    (ph,), got = _matmul("in_h", [(u, w_h)], "nn", tm=TM, tn=1024, tk=1024, outs=[("mn", F32)],
                         extras=[(b_h, *row_vec(1024))], epilogue=bias, comm=net.gather(names))
    net.gathered(names, got)
    pg, = _matmul("in_g", [(u, w_g)], "nn", tm=TM, tn=1024, tk=1024, outs=[("mn", F32)],
                  extras=[(b_g, *row_vec(1024))], epilogue=bias)
    names = ("w_ffn_gate", "w_ffn_up")
    (y_attn, lse), got = _attn_fwd(pq, pkv, sinks, comm=net.gather(names))
    net.gathered(names, got)
    names = ("w_ffn_down",)
    (y_hgrn, o_raw, states), got = _hgrn_fwd(ph, lb_logits, hgrn_norm_g, comm=net.gather(names))
    net.gathered(names, got)
    w_ba, w_bh, w_out = net.full("w_branch_attn"), net.full("w_branch_hgrn"), net.full("w_out")
    w_gate, w_up, w_down = net.full("w_ffn_gate"), net.full("w_ffn_up"), net.full("w_ffn_down")
    ya, = _matmul("branch_a", [(y_attn, w_ba)], "nn", tm=TM, tn=1024, tk=1024, outs=[("mn", F32)])
    gate_a = (pg, (TM, 1024), lambda i, j: (i, 0))
    gate_b = (pg, (TM, 1024), lambda i, j: (i, 1))

    def merge(acc, ya_v, ga, gb):
        return acc, _sigmoid(ga) * ya_v + _sigmoid(gb) * acc

    yb, merged = _matmul("branch_b", [(y_hgrn, w_bh)], "nn", tm=TM, tn=1024, tk=1024,
                         outs=[("mn", F32), ("mn", MXU_DTYPE)],
                         extras=[(ya, *tile(TM, 1024)), gate_a, gate_b], epilogue=merge)

    def resid_norm(acc, xin, g):
        h = xin + acc
        r = lax.rsqrt(jnp.mean(h * h, axis=-1, keepdims=True) + EPS)
        return h, h * r * g, r

    h1, u2, rstd2 = _matmul("out_proj", [(merged, w_out)], "nn", tm=TM, tn=1024, tk=1024,
                            outs=[("mn", F32), ("mn", MXU_DTYPE), ("m", F32)],
                            extras=[(x, *tile(TM, 1024)), (norm_ffn_g, *row_vec(1024))], epilogue=resid_norm)
    FT = FFN // 2
    gpre, = _matmul("ffn_gate", [(u2, w_gate)], "nn", tm=TM, tn=FT, tk=1024, outs=[("mn", F32)])

    def swiglu(acc, gv):
        return acc, gv * _sigmoid(gv) * acc

    up, z = _matmul("ffn_up", [(u2, w_up)], "nn", tm=TM, tn=FT, tk=1024, outs=[("mn", F32), ("mn", MXU_DTYPE)],
                    extras=[(gpre, *tile(TM, FT))], epilogue=swiglu)

    def loss_head(acc, h1_v, tgt, g):
        h2 = h1_v + acc
        r = lax.rsqrt(jnp.mean(h2 * h2, axis=-1, keepdims=True) + EPS)
        xhat = h2 * r
        err = xhat * g - tgt
        part = 0.5 * jnp.sum(jnp.sum(err * err, axis=-1, keepdims=True), axis=0, keepdims=True) / D
        dyv = err / D
        dxh = dyv * g
        dh2 = r * (dxh - xhat * jnp.mean(dxh * xhat, axis=-1, keepdims=True))
        return dh2, dh2, jnp.sum(dyv * xhat, axis=0, keepdims=True), jnp.broadcast_to(part, (1, D))

    dh2, dh2b, dgf_p, loss_p = _matmul(
        "ffn_down", [(z, w_down)], "nn", tm=TM, tn=1024, tk=FFN,
        outs=[("mn", F32), ("mn", MXU_DTYPE), ("pn", F32), ("pn", F32)],
        extras=[(h1, *tile(TM, 1024)), (target, *tile(TM, 1024)), (norm_final_g, *row_vec(1024))],
        epilogue=loss_head)
    loss = jnp.sum(loss_p.reshape(-1, 8, D)[:, 0, 0])
    d_norm_final = _colsum_partials(dgf_p)

    def swiglu_bwd(acc, gv, upv):
        s = _sigmoid(gv)
        return acc * upv * (s * (1.0 + gv * (1.0 - s))), acc * (gv * s)

    dgp, dup = _matmul("d_ffn_hidden", [(dh2b, w_down)], "nt", tm=TM, tn=FT, tk=1024,
                       outs=[("mn", MXU_DTYPE), ("mn", MXU_DTYPE)],
                       extras=[(gpre, *tile(TM, FT)), (up, *tile(TM, FT))], epilogue=swiglu_bwd)
    d_w_down, = _matmul("dw_down", [(z, dh2b)], "tn", tm=FT, tn=1024, tk=512, outs=[("mn", F32)])

    def norm_ffn_bwd(acc, h1_v, r, g, dres):
        dx, dg = _rmsnorm_bwd_vals(acc, h1_v, r, g)
        dh = dres + dx
        return dh, dh, dg

    names = ("w_ffn_down",)
    (dh1, dh1b, dg2_p), got = _matmul(
        "d_ffn_in", [(dgp, w_gate), (dup, w_up)], "nt", tm=TM, tn=1024, tk=FT,
        outs=[("mn", F32), ("mn", MXU_DTYPE), ("pn", F32)],
        extras=[(h1, *tile(TM, 1024)), (rstd2, (TM, 1), lambda i, j: (i, 0)),
                (norm_ffn_g, *row_vec(1024)), (dh2, *tile(TM, 1024))], epilogue=norm_ffn_bwd,
        comm=net.exchange(dict(w_ffn_down=d_w_down)))
    net.received(names, got)
    d_norm_ffn = _colsum_partials(dg2_p)
    d_w_gate, = _matmul("dw_gate", [(u2, dgp)], "tn", tm=1024, tn=FT, tk=512, outs=[("mn", F32)])
    d_w_up, = _matmul("dw_up", [(u2, dup)], "tn", tm=1024, tn=FT, tk=512, outs=[("mn", F32)])

    def merge_bwd(acc, ya_v, yb_v, ga, gb):
        sa, sb = _sigmoid(ga), _sigmoid(gb)
        return acc * sa, acc * sb, acc * ya_v * sa * (1.0 - sa), acc * yb_v * sb * (1.0 - sb)

    dya, dyb, dga, dgb = _matmul(
        "d_merged", [(dh1b, w_out)], "nt", tm=TM, tn=1024, tk=1024, outs=[("mn", MXU_DTYPE)] * 4,
        extras=[(ya, *tile(TM, 1024)), (yb, *tile(TM, 1024)), gate_a, gate_b], epilogue=merge_bwd)
    d_w_out, = _matmul("dw_out", [(merged, dh1b)], "tn", tm=1024, tn=1024, tk=512, outs=[("mn", F32)])
    dy_attn, = _matmul("d_y_attn", [(dya, w_ba)], "nt", tm=TM, tn=1024, tk=1024, outs=[("mn", F32)])
    dy_hgrn, = _matmul("d_y_hgrn", [(dyb, w_bh)], "nt", tm=TM, tn=1024, tk=1024, outs=[("mn", F32)])
    d_w_ba, = _matmul("dw_branch_a", [(y_attn, dya)], "tn", tm=1024, tn=1024, tk=512, outs=[("mn", F32)])
    d_w_bh, = _matmul("dw_branch_b", [(y_hgrn, dyb)], "tn", tm=1024, tn=1024, tk=512, outs=[("mn", F32)])

    names = ("w_ffn_gate", "w_ffn_up")
    (dq, dkv, dsink), got = _attn_bwd(pq, pkv, sinks, lse, dy_attn,
                                      comm=net.exchange(dict(w_ffn_gate=d_w_gate, w_ffn_up=d_w_up)))
    net.received(names, got)
    names = ("w_out", "w_branch_attn", "w_branch_hgrn")
    (dph, d_hgrn_norm, d_lb_logits), got = _hgrn_bwd(
        ph, o_raw, states, dy_hgrn, lb_logits, hgrn_norm_g,
        comm=net.exchange(dict(w_out=d_w_out, w_branch_attn=d_w_ba, w_branch_hgrn=d_w_bh)))
    net.received(names, got)

    dw_q, = _matmul("dw_in_q", [(u, dq)], "tn", tm=1024, tn=1024, tk=512, outs=[("mn", F32)])
    dw_kv, = _matmul("dw_in_kv", [(u, dkv)], "tn", tm=1024, tn=256, tk=512, outs=[("mn", F32)])
    dw_h, = _matmul("dw_in_h", [(u, dph)], "tn", tm=1024, tn=1024, tk=512, outs=[("mn", F32)])
    dw_ga, = _matmul("dw_in_ga", [(u, dga)], "tn", tm=1024, tn=1024, tk=512, outs=[("mn", F32)])
    dw_gb, = _matmul("dw_in_gb", [(u, dgb)], "tn", tm=1024, tn=1024, tk=512, outs=[("mn", F32)])
    d_w_in = jnp.concatenate([dw_q, dw_kv, dw_h, dw_ga, dw_gb], axis=1)

    def norm_mix_bwd(acc, xin, r, g, dres):
        dx, dg = _rmsnorm_bwd_vals(acc, xin, r, g)
        return dres + dx, dg

    names = ("w_in",)
    (dx, dg1_p), got = _matmul(
        "d_u", [(dq, w_q), (dkv, w_kv), (dph, w_h), (dga, w_g[:, :D]), (dgb, w_g[:, D:])], "nt",
        tm=TM, tn=1024, tk=512, outs=[("mn", F32), ("pn", F32)],
        extras=[(x, *tile(TM, 1024)), (rstd1, (TM, 1), lambda i, j: (i, 0)),
                (norm_mix_g, *row_vec(1024)), (dh1, *tile(TM, 1024))], epilogue=norm_mix_bwd,
        comm=net.exchange(dict(w_in=d_w_in)))
    net.received(names, got)
    d_norm_mix = _colsum_partials(dg1_p)
    d_b_in = jnp.concatenate([_colsum("db_q", dq), _colsum("db_kv", dkv), _colsum("db_h", dph),
                              _colsum("db_ga", dga), _colsum("db_gb", dgb)], axis=1)
    vecs = dict(norm_mix_g=d_norm_mix, b_in=d_b_in, attn_sinks=dsink[0:1, 0:Q_HEADS], hgrn_lb_logits=d_lb_logits,
                hgrn_norm_g=d_hgrn_norm, norm_ffn_g=d_norm_ffn, norm_final_g=d_norm_final)
    return loss, dx, vecs


def _place():
    return lax.axis_index("x"), lax.axis_index("y"), lax.axis_index("c")


def _other_chips(x, y):
    return [(1 - x, y), (x, 1 - y), (1 - x, 1 - y)]


def _gather_copies(shards):
    n = len(shards)

    def build(ins, outs, send_sems, recv_sems, local_sems):
        x, y, c = _place()
        mine = 2 * x + y
        local = [pltpu.make_async_copy(ins[w], outs[w].at[mine], local_sems.at[w]) for w in range(n)]
        sends, recvs = [], []
        for w in range(n):
            for k, (px, py) in enumerate(_other_chips(x, y)):
                sem = 3 * w + k
                sends.append(pltpu.make_async_remote_copy(
                    src_ref=ins[w], dst_ref=outs[w].at[mine], send_sem=send_sems.at[sem], recv_sem=recv_sems.at[sem],
                    device_id=(px, py, c), device_id_type=MESH_ID))
                recvs.append(pltpu.make_async_remote_copy(
                    src_ref=ins[w], dst_ref=outs[w].at[2 * px + py], send_sem=send_sems.at[sem],
                    recv_sem=recv_sems.at[sem], device_id=(px, py, c), device_id_type=MESH_ID))
        return sends, recvs, local

    return _Carried(shards, [jax.ShapeDtypeStruct((N_CHIPS,) + s.shape, s.dtype) for s in shards], 3 * n, n, build)


def _grad_copies(stacked):
    n = len(stacked)

    def build(ins, outs, send_sems, recv_sems, local_sems):
        x, y, c = _place()
        sends = []
        for w in range(n):
            for k, (px, py) in enumerate(_other_chips(x, y)):
                sem = 3 * w + k
                sends.append(pltpu.make_async_remote_copy(
                    src_ref=ins[w].at[2 * px + py], dst_ref=outs[w].at[k], send_sem=send_sems.at[sem],
                    recv_sem=recv_sems.at[sem], device_id=(px, py, c), device_id_type=MESH_ID))
        return sends, sends, []

    return _Carried(stacked, [jax.ShapeDtypeStruct((3,) + s.shape[1:], s.dtype) for s in stacked], 3 * n, 0, build)


def _small_copies(small):
    def build(ins, outs, send_sems, recv_sems, local_sems):
        small_ref, all_ref = ins[0], outs[0]
        x, y, c = _place()
        me = 4 * x + 2 * y + c
        sends, recvs = [], []
        for r in range(1, 8):
            px = 1 - x if r & 4 else x
            py = 1 - y if r & 2 else y
            pc = 1 - c if r & 1 else c
            sends.append(pltpu.make_async_remote_copy(
                src_ref=small_ref, dst_ref=all_ref.at[me], send_sem=send_sems.at[r - 1], recv_sem=recv_sems.at[r - 1],
                device_id=(px, py, pc), device_id_type=MESH_ID))
            recvs.append(pltpu.make_async_remote_copy(
                src_ref=small_ref, dst_ref=all_ref.at[4 * px + 2 * py + pc], send_sem=send_sems.at[r - 1],
                recv_sem=recv_sems.at[r - 1], device_id=(px, py, pc), device_id_type=MESH_ID))
        return sends, recvs, [pltpu.make_async_copy(small_ref, all_ref.at[me], local_sems.at[0])]

    return _Carried([small], [jax.ShapeDtypeStruct((8,) + small.shape, small.dtype)], 7, 1, build)


def _copies_alone(name, comm):
    return _call(name, lambda: None, grid=(), in_specs=[], out_specs=[], out_shape=[], args=[], comm=comm)[1]


class _Net:
    def __init__(self, shards):
        self.shards = shards
        self.whole, self.own, self.theirs = {}, {}, {}
        x, y, _ = _place()
        self.chip = 2 * x + y

    def gather(self, names):
        return _gather_copies([self.shards[n] for n in names])

    def gathered(self, names, got):
        for n, g in zip(names, got):
            if n in COLUMN_SHARDED:
                self.whole[n] = jnp.concatenate([g[s] for s in range(N_CHIPS)], axis=1)
            else:
                self.whole[n] = g.reshape(-1, g.shape[-1])

    def full(self, name):
        return self.whole[name]

    def exchange(self, grads):
        stacked = []
        for n, d in grads.items():
            s = (jnp.stack(jnp.split(d, N_CHIPS, axis=1)) if n in COLUMN_SHARDED
                 else d.reshape(N_CHIPS, -1, d.shape[-1]))
            self.own[n] = lax.dynamic_index_in_dim(s, self.chip, 0, keepdims=False)
            stacked.append(s.astype(MXU_DTYPE))
        return _grad_copies(stacked)

    def received(self, names, got):
        self.theirs.update(zip(names, got))


def _exchange_sibling(parts):
    n = len(parts)

    def body(*refs):
        ins, outs = refs[:n], refs[n:2 * n]
        send_sems, recv_sems = refs[2 * n:]
        x, y, c = _place()
        copies = [pltpu.make_async_remote_copy(
            src_ref=ins[w], dst_ref=outs[w], send_sem=send_sems.at[w], recv_sem=recv_sems.at[w],
            device_id=(x, y, 1 - c), device_id_type=MESH_ID) for w in range(n)]
        for cp in copies:
            cp.start()
        for cp in copies:
            cp.wait_recv()
        for cp in copies:
            cp.wait_send()

    return pl.pallas_call(
        body, name="exchange_sibling",
        in_specs=[HBM_SPEC] * n, out_specs=[HBM_SPEC] * n,
        out_shape=[jax.ShapeDtypeStruct(p.shape, p.dtype) for p in parts],
        scratch_shapes=[pltpu.SemaphoreType.DMA((n,)), pltpu.SemaphoreType.DMA((n,))],
        compiler_params=pltpu.CompilerParams(has_side_effects=True),
    )(*parts)


def _row_tile(rows):
    return 128 if rows % 128 == 0 else 176


def _partial_sum(name, own, recv):
    R, C = own.shape
    tr = _row_tile(R)

    def body(o_ref, r_ref, p_ref):
        p_ref[...] = ((o_ref[...] + r_ref[0].astype(F32)) + r_ref[1].astype(F32)) + r_ref[2].astype(F32)

    return pl.pallas_call(
        body, name=name, grid=(R // tr,),
        in_specs=[pl.BlockSpec((tr, C), lambda i: (i, 0)), pl.BlockSpec((3, tr, C), lambda i: (0, i, 0))],
        out_specs=pl.BlockSpec((tr, C), lambda i: (i, 0)),
        out_shape=jax.ShapeDtypeStruct((R, C), F32),
        compiler_params=_params(("parallel",)),
    )(own, recv)


def _adam_vals(w, g, m, v):
    m = ADAM_B1 * m + (1.0 - ADAM_B1) * g
    v = ADAM_B2 * v + (1.0 - ADAM_B2) * (g * g)
    m_hat = m / (1.0 - ADAM_B1 ** ADAM_STEP)
    v_hat = v / (1.0 - ADAM_B2 ** ADAM_STEP)
    delta = -ADAM_LR * (m_hat / (jnp.sqrt(v_hat) + ADAM_EPS) + ADAM_WD * w)
    return delta, m, v


def _adamw(name, w, m, v, p_south, p_north):
    R, C = w.shape
    tr = _row_tile(R)

    def body(w_ref, m_ref, v_ref, s_ref, n_ref, g_ref, d_ref, nm_ref, nv_ref):
        g = s_ref[...] + n_ref[...]
        d, nm, nv = _adam_vals(w_ref[...], g, m_ref[...], v_ref[...])
        g_ref[...], d_ref[...], nm_ref[...], nv_ref[...] = g, d, nm, nv

    spec = pl.BlockSpec((tr, C), lambda i: (i, 0))
    return pl.pallas_call(
        body, name=name, grid=(R // tr,), in_specs=[spec] * 5, out_specs=[spec] * 4,
        out_shape=[jax.ShapeDtypeStruct((R, C), F32)] * 4,
        compiler_params=_params(("parallel",)),
    )(w, m, v, p_south, p_north)


def _adamw_small(w, m, v, g_all):
    def body(w_ref, m_ref, v_ref, a_ref, g_ref, d_ref, nm_ref, nv_ref):
        g = a_ref[0]
        for dev in range(1, 8):
            g = g + a_ref[dev]
        d, nm, nv = _adam_vals(w_ref[...], g, m_ref[...], v_ref[...])
        g_ref[...], d_ref[...], nm_ref[...], nv_ref[...] = g, d, nm, nv

    return pl.pallas_call(
        body, name="adamw_small", out_shape=[jax.ShapeDtypeStruct(w.shape, F32)] * 4,
    )(w, m, v, g_all)


SMALL_LAYOUT = dict(norm_mix_g=(0, 1024), b_in=(8, 7424), hgrn_norm_g=(16, 1024), norm_ffn_g=(24, 1024),
                    norm_final_g=(32, 1024), hgrn_lb_logits=(40, 2048), attn_sinks=(48, 16))


def _pack_small(vals):
    parts = []
    for name, (_, nvalid) in SMALL_LAYOUT.items():
        flat = vals[name].astype(F32).reshape(-1)
        parts.append(jnp.pad(flat, (0, 8 * D_MODEL - nvalid)).reshape(8, D_MODEL))
    return jnp.concatenate(parts, axis=0)


def _unpack_small(packed, shapes):
    return {name: packed[r0:r0 + 8].reshape(-1)[:nvalid].reshape(shapes[name])
            for name, (r0, nvalid) in SMALL_LAYOUT.items()}


MATRICES = ("w_in", "w_branch_attn", "w_branch_hgrn", "w_out", "w_ffn_gate", "w_ffn_up", "w_ffn_down")
COLUMN_SHARDED = ("w_in", "w_ffn_gate", "w_ffn_up")
WEIGHTS = ("norm_mix_g", "w_in", "b_in", "attn_sinks", "hgrn_lb_logits", "hgrn_norm_g", "w_branch_attn",
           "w_branch_hgrn", "w_out", "norm_ffn_g", "w_ffn_gate", "w_ffn_up", "w_ffn_down", "norm_final_g")


def kernel(x, norm_mix_g, w_in, b_in, attn_sinks, hgrn_lb_logits, hgrn_norm_g, w_branch_attn, w_branch_hgrn, w_out, norm_ffn_g, w_ffn_gate, w_ffn_up, w_ffn_down, norm_final_g, loss_target, m_norm_mix_g, m_w_in, m_b_in, m_attn_sinks, m_hgrn_lb_logits, m_hgrn_norm_g, m_w_branch_attn, m_w_branch_hgrn, m_w_out, m_norm_ffn_g, m_w_ffn_gate, m_w_ffn_up, m_w_ffn_down, m_norm_final_g, v_norm_mix_g, v_w_in, v_b_in, v_attn_sinks, v_hgrn_lb_logits, v_hgrn_norm_g, v_w_branch_attn, v_w_branch_hgrn, v_w_out, v_norm_ffn_g, v_w_ffn_gate, v_w_ffn_up, v_w_ffn_down, v_norm_final_g):
    given = dict(locals())
    w = {n: given[n] for n in WEIGHTS}
    m = {n: given["m_" + n] for n in WEIGHTS}
    v = {n: given["v_" + n] for n in WEIGHTS}

    net = _Net({n: w[n][0].astype(MXU_DTYPE) for n in MATRICES})
    net.gathered(("w_in",), _copies_alone("gather_w_in", net.gather(("w_in",))))
    vec = dict(norm_mix_g=norm_mix_g, b_in=b_in, attn_sinks=attn_sinks, hgrn_lb_logits=hgrn_lb_logits,
               hgrn_norm_g=hgrn_norm_g, norm_ffn_g=norm_ffn_g, norm_final_g=norm_final_g.reshape(1, D_MODEL))
    loss_part, dx, d_vecs = _local_step(x[0], loss_target[0], vec, net)

    small_all, = _copies_alone("exchange_small", _small_copies(_pack_small(d_vecs)))
    sums = [_partial_sum("sum_" + n, net.own[n], net.theirs[n]) for n in MATRICES]
    theirs = _exchange_sibling(sums)

    grads, deltas, new_m, new_v = {}, {}, {}, {}
    for n, mine, other in zip(MATRICES, sums, theirs):
        res = _adamw("adamw_" + n, w[n][0], m[n][0], v[n][0], mine, other)
        grads[n], deltas[n], new_m[n], new_v[n] = (r[None] for r in res)
    shapes = {n: w[n].shape for n in SMALL_LAYOUT}
    res = _adamw_small(_pack_small(w), _pack_small(m), _pack_small(v), small_all)
    for dst, packed in zip((grads, deltas, new_m, new_v), res):
        dst.update(_unpack_small(packed, shapes))

    loss = lax.psum(loss_part, ("x", "y", "c"))
    return (loss, dx[None], *[grads[n] for n in WEIGHTS], *[deltas[n] for n in WEIGHTS],
            *[new_m[n] for n in WEIGHTS], *[new_v[n] for n in WEIGHTS])
```

```python
import functools
import math

import jax
import jax.numpy as jnp
from jax import lax
from jax.experimental import pallas as pl
from jax.experimental.pallas import tpu as pltpu

F32 = jnp.float32
BF16 = jnp.bfloat16
MXU_DTYPE = jnp.bfloat16
MESH_ID = pl.DeviceIdType.MESH

D_MODEL = 1024
HEAD_DIM = 64
Q_HEADS = 16
KV_HEADS = 2
GROUP = Q_HEADS // KV_HEADS
KV_WIDTH = KV_HEADS * HEAD_DIM
ATTN_BLOCK = 128
HGRN_HEADS = 8
HGRN_K = 128
CHUNK = 64
HGRN_TOKENS = 256
FFN = 2816
IN_SPLITS = (1024, 256, 4096, 2048)
EPS = 1e-6
NEG_INF = -1e30
ADAM_LR, ADAM_B1, ADAM_B2, ADAM_EPS, ADAM_WD, ADAM_STEP = 0.001, 0.9, 0.999, 1e-08, 0.01, 10
N_CHIPS = 4
VMEM_LIMIT = 56 * 1024 * 1024


def _params(sem=None):
    return pltpu.CompilerParams(dimension_semantics=sem, vmem_limit_bytes=VMEM_LIMIT)


def _sigmoid(v):
    return 1.0 / (1.0 + jnp.exp(-v))


def _dot(a, b, dims):
    return lax.dot_general(a.astype(MXU_DTYPE), b.astype(MXU_DTYPE), (dims, ((), ())),
                           preferred_element_type=F32)


def _nn(a, b):
    return _dot(a, b, ((1,), (0,)))


def _nt(a, b):
    return _dot(a, b, ((1,), (1,)))


def _tn(a, b):
    return _dot(a, b, ((0,), (0,)))


HBM_SPEC = pl.BlockSpec(memory_space=pl.ANY)


class _Carried:
    def __init__(self, arrays, out_shapes, n_remote, n_local, build):
        self.arrays, self.out_shapes, self.build = list(arrays), list(out_shapes), build
        self.scratch = [pltpu.SemaphoreType.DMA((n_remote,)), pltpu.SemaphoreType.DMA((n_remote,)),
                        pltpu.SemaphoreType.DMA((max(n_local, 1),))]

    def start(self, ins, outs, sems):
        sends, _, local = self.build(ins, outs, *sems)
        for cp in local + sends:
            cp.start()

    def wait(self, ins, outs, sems):
        sends, recvs, local = self.build(ins, outs, *sems)
        for cp in recvs:
            cp.wait_recv()
        for cp in sends:
            cp.wait_send()
        for cp in local:
            cp.wait()


def _call(name, body, *, grid, in_specs, out_specs, out_shape, args, scratch=(), semantics=None, comm=None):
    n_in, n_out, n_scr = len(in_specs), len(out_specs), len(scratch)
    if comm is None:
        res = pl.pallas_call(body, name=name, grid=grid, in_specs=in_specs, out_specs=out_specs, out_shape=out_shape,
                             scratch_shapes=list(scratch), compiler_params=_params(semantics))(*args)
        return list(res), []
    ci, co = len(comm.arrays), len(comm.out_shapes)

    def carrying(*refs):
        ins, refs = refs[:n_in], refs[n_in:]
        c_ins, refs = refs[:ci], refs[ci:]
        outs, refs = refs[:n_out], refs[n_out:]
        c_outs, refs = refs[:co], refs[co:]
        scr, sems = refs[:n_scr], refs[n_scr:]
        if not grid:
            comm.start(c_ins, c_outs, sems)
            body(*ins, *outs, *scr)
            comm.wait(c_ins, c_outs, sems)
            return
        first = functools.reduce(jnp.logical_and, [pl.program_id(a) == 0 for a in range(len(grid))])
        last = functools.reduce(jnp.logical_and, [pl.program_id(a) == g - 1 for a, g in enumerate(grid)])

        @pl.when(first)
        def _():
            comm.start(c_ins, c_outs, sems)

        body(*ins, *outs, *scr)

        @pl.when(last)
        def _():
            comm.wait(c_ins, c_outs, sems)

    res = pl.pallas_call(
        carrying, name=name, grid=grid, in_specs=list(in_specs) + [HBM_SPEC] * ci,
        out_specs=list(out_specs) + [HBM_SPEC] * co, out_shape=list(out_shape) + comm.out_shapes,
        scratch_shapes=list(scratch) + comm.scratch,
        compiler_params=_params(("arbitrary",) * len(grid) if grid else None),
    )(*args, *comm.arrays)
    return list(res[:n_out]), list(res[n_out:])


_NO_COPIES = object()


def _matmul(name, pairs, mode, *, tm, tn, tk, outs, extras=(), epilogue=None, comm=_NO_COPIES):
    prod = dict(nn=_nn, nt=_nt, tn=_tn)[mode]
    a0, b0 = pairs[0]
    M = a0.shape[1] if mode == "tn" else a0.shape[0]
    N = b0.shape[0] if mode == "nt" else b0.shape[1]
    steps, in_specs, offset = [], [], 0
    for a, b in pairs:
        K = a.shape[0] if mode == "tn" else a.shape[1]
        t = min(tk, K)
        assert K % t == 0, (name, K, t)
        kmap = functools.partial(lambda k, off, n: jnp.clip(k - off, 0, n - 1), off=offset, n=K // t)
        if mode == "tn":
            in_specs.append(pl.BlockSpec((t, tm), functools.partial(lambda i, j, k, f: (f(k), i), f=kmap)))
        else:
            in_specs.append(pl.BlockSpec((tm, t), functools.partial(lambda i, j, k, f: (i, f(k)), f=kmap)))
        if mode == "nt":
            in_specs.append(pl.BlockSpec((tn, t), functools.partial(lambda i, j, k, f: (j, f(k)), f=kmap)))
        else:
            in_specs.append(pl.BlockSpec((t, tn), functools.partial(lambda i, j, k, f: (f(k), j), f=kmap)))
        steps.append((offset, offset + K // t))
        offset += K // t
    assert M % tm == 0 and N % tn == 0, (name, M, N, tm, tn)
    ni, nj, nk = M // tm, N // tn, offset
    npair, ne, no = len(pairs), len(extras), len(outs)
    if epilogue is None:
        epilogue = lambda acc: (acc,)

    def finish(acc, extra_refs, out_refs):
        vals = epilogue(acc, *[r[...] for r in extra_refs])
        for (kind, _), o_ref, val in zip(outs, out_refs, vals):
            if kind == "pn":
                val = jnp.broadcast_to(val, o_ref.shape)
            o_ref[...] = val.astype(o_ref.dtype)

    def body(*refs):
        ab, rest = refs[:2 * npair], refs[2 * npair:]
        extra_refs, out_refs = rest[:ne], rest[ne:ne + no]
        if nk == 1:
            finish(prod(ab[0][...], ab[1][...]), extra_refs, out_refs)
            return
        acc_ref = rest[ne + no]
        k = pl.program_id(2)

        @pl.when(k == 0)
        def _():
            acc_ref[...] = jnp.zeros_like(acc_ref)

        for p, (lo, hi) in enumerate(steps):
            @pl.when(jnp.logical_and(k >= lo, k < hi))
            def _():
                acc_ref[...] += prod(ab[2 * p][...], ab[2 * p + 1][...])

        @pl.when(k == nk - 1)
        def _():
            finish(acc_ref[...], extra_refs, out_refs)

    for _, shape, im in extras:
        in_specs.append(pl.BlockSpec(shape, functools.partial(lambda i, j, k, im: im(i, j), im=im)))
    out_shape, out_specs = [], []
    for kind, dt in outs:
        if kind == "mn":
            out_shape.append(jax.ShapeDtypeStruct((M, N), dt))
            out_specs.append(pl.BlockSpec((tm, tn), lambda i, j, k: (i, j)))
        elif kind == "m":
            assert nj == 1
            out_shape.append(jax.ShapeDtypeStruct((M, 1), dt))
            out_specs.append(pl.BlockSpec((tm, 1), lambda i, j, k: (i, 0)))
        else:
            out_shape.append(jax.ShapeDtypeStruct((8 * ni, N), dt))
            out_specs.append(pl.BlockSpec((8, tn), lambda i, j, k: (i, j)))
    res, got = _call(name, body, grid=(ni, nj, nk), in_specs=in_specs, out_specs=out_specs, out_shape=out_shape,
                     args=[t for pair in pairs for t in pair] + [e[0] for e in extras],
                     scratch=[pltpu.VMEM((tm, tn), F32)] if nk > 1 else [],
                     semantics=("parallel", "parallel", "arbitrary"), comm=None if comm is _NO_COPIES else comm)
    return res if comm is _NO_COPIES else (res, got)


def _colsum_partials(p):
    return jnp.sum(p.reshape(-1, 8, p.shape[-1])[:, 0, :], axis=0, keepdims=True)


def _rmsnorm_fwd(name, x, g, tr=512):
    T, D = x.shape

    def body(x_ref, g_ref, u_ref, r_ref):
        xv = x_ref[...]
        r = lax.rsqrt(jnp.mean(xv * xv, axis=-1, keepdims=True) + EPS)
        u_ref[...] = (xv * r * g_ref[...]).astype(u_ref.dtype)
        r_ref[...] = r

    return pl.pallas_call(
        body, name=name, grid=(T // tr,),
        in_specs=[pl.BlockSpec((tr, D), lambda i: (i, 0)), pl.BlockSpec((1, D), lambda i: (0, 0))],
        out_specs=[pl.BlockSpec((tr, D), lambda i: (i, 0)), pl.BlockSpec((tr, 1), lambda i: (i, 0))],
        out_shape=[jax.ShapeDtypeStruct((T, D), MXU_DTYPE), jax.ShapeDtypeStruct((T, 1), F32)],
        compiler_params=_params(("parallel",)),
    )(x, g)


def _rmsnorm_bwd_vals(dy, xin, rstd, g):
    xhat = xin * rstd
    dg = jnp.sum(dy * xhat, axis=0, keepdims=True)
    dxh = dy * g
    dx = rstd * (dxh - xhat * jnp.mean(dxh * xhat, axis=-1, keepdims=True))
    return dx, dg


def _colsum(name, a, tr=512):
    T, N = a.shape

    def body(a_ref, o_ref):
        @pl.when(pl.program_id(0) == 0)
        def _():
            o_ref[...] = jnp.zeros_like(o_ref)

        o_ref[...] += jnp.sum(a_ref[...].astype(F32), axis=0, keepdims=True)

    return pl.pallas_call(
        body, name=name, grid=(T // tr,),
        in_specs=[pl.BlockSpec((tr, N), lambda i: (i, 0))],
        out_specs=pl.BlockSpec((1, N), lambda i: (0, 0)),
        out_shape=jax.ShapeDtypeStruct((1, N), F32),
        compiler_params=_params(("arbitrary",)),
    )(a)


ATTN_SCALE = 1.0 / math.sqrt(HEAD_DIM)
GROUP_LANES = GROUP * ATTN_BLOCK
PAIR = 2 * HEAD_DIM


def _attn_mask():
    kj = lax.broadcasted_iota(jnp.int32, (ATTN_BLOCK, GROUP_LANES), 0)
    qi = lax.broadcasted_iota(jnp.int32, (ATTN_BLOCK, GROUP_LANES), 1) & (ATTN_BLOCK - 1)
    return kj <= qi


def _heads_transposed(ref, g, scale=None):
    parts = []
    for a in range(GROUP // 2):
        lo = (g * GROUP // 2 + a) * PAIR
        pair = ref[:, lo:lo + PAIR]
        pair = (pair if scale is None else pair * scale).T
        parts += [pair[:HEAD_DIM], pair[HEAD_DIM:]]
    return jnp.concatenate(parts, axis=1).astype(MXU_DTYPE)


def _heads_back(ref, g, vt):
    for a in range(GROUP // 2):
        lo = (g * GROUP // 2 + a) * PAIR
        pair = jnp.concatenate([vt[:, (2 * a) * ATTN_BLOCK:(2 * a + 1) * ATTN_BLOCK],
                                vt[:, (2 * a + 1) * ATTN_BLOCK:(2 * a + 2) * ATTN_BLOCK]], axis=0)
        ref[:, lo:lo + PAIR] = pair.T.astype(ref.dtype)


def _kv_parts(kv_ref, g):
    ks = slice(g * HEAD_DIM, (g + 1) * HEAD_DIM)
    vs = slice(KV_WIDTH + g * HEAD_DIM, KV_WIDTH + (g + 1) * HEAD_DIM)
    return kv_ref[:, ks].astype(MXU_DTYPE), kv_ref[:, vs].astype(MXU_DTYPE)


def _sink_rows(sinks):
    return jnp.repeat(sinks.reshape(KV_HEADS, GROUP), ATTN_BLOCK, axis=1)


def _attn_fwd(pq, pkv, sinks, comm=None):
    T = pq.shape[0]
    nb = T // ATTN_BLOCK

    def body(q_ref, kvc_ref, kvp_ref, s_ref, y_ref, lse_ref):
        mask_c = _attn_mask()
        has_prev = pl.program_id(0) > 0
        for g in range(KV_HEADS):
            (kc, vc), (kp, vp) = _kv_parts(kvc_ref, g), _kv_parts(kvp_ref, g)
            qt = _heads_transposed(q_ref, g, ATTN_SCALE)
            s = jnp.where(mask_c, _nn(kc, qt), jnp.where(has_prev, _nn(kp, qt), NEG_INF))
            sink = s_ref[g:g + 1, :]
            m = jnp.maximum(jnp.max(s, axis=0, keepdims=True), sink)
            p = jnp.exp(s - m)
            den = jnp.sum(p, axis=0, keepdims=True) + jnp.exp(sink - m)
            pc = jnp.where(mask_c, p, 0.0)
            _heads_back(y_ref, g, (_tn(vc, pc) + _tn(vp, p - pc)) / den)
            lse = m + jnp.log(den)
            for i in range(GROUP):
                lse_ref[g * GROUP + i:g * GROUP + i + 1, :] = lse[:, i * ATTN_BLOCK:(i + 1) * ATTN_BLOCK]

    return _call(
        "attn_fwd", body, grid=(nb,),
        in_specs=[pl.BlockSpec((ATTN_BLOCK, D_MODEL), lambda n: (n, 0)),
                  pl.BlockSpec((ATTN_BLOCK, 2 * KV_WIDTH), lambda n: (n, 0)),
                  pl.BlockSpec((ATTN_BLOCK, 2 * KV_WIDTH), lambda n: (jnp.maximum(n - 1, 0), 0)),
                  pl.BlockSpec((KV_HEADS, GROUP_LANES), lambda n: (0, 0))],
        out_specs=[pl.BlockSpec((ATTN_BLOCK, D_MODEL), lambda n: (n, 0)),
                   pl.BlockSpec((Q_HEADS, ATTN_BLOCK), lambda n: (0, n))],
        out_shape=[jax.ShapeDtypeStruct((T, D_MODEL), MXU_DTYPE), jax.ShapeDtypeStruct((Q_HEADS, T), F32)],
        args=[pq, pkv, pkv, _sink_rows(sinks)], semantics=("parallel",), comm=comm)


def _attn_bwd(pq, pkv, sinks, lse, dy, comm=None):
    T = pq.shape[0]
    nb = T // ATTN_BLOCK
    cur = lambda n: (jnp.minimum(n, nb - 1), 0)

    def body(q_ref, kvc_ref, kvp_ref, s_ref, lse_ref, dy_ref, dq_ref, dkv_ref, ds_ref, carry, top, bot):
        n = pl.program_id(0)

        @pl.when(n == 0)
        def _():
            carry[...] = jnp.zeros_like(carry)
            ds_ref[...] = jnp.zeros_like(ds_ref)

        @pl.when(n < nb)
        def _():
            mask_c = _attn_mask()
            valid = jnp.logical_or(mask_c, n > 0)
            for g in range(KV_HEADS):
                ks = slice(g * HEAD_DIM, (g + 1) * HEAD_DIM)
                vs = slice(KV_WIDTH + g * HEAD_DIM, KV_WIDTH + (g + 1) * HEAD_DIM)
                (kc, vc), (kp, vp) = _kv_parts(kvc_ref, g), _kv_parts(kvp_ref, g)
                qt = _heads_transposed(q_ref, g, ATTN_SCALE)
                dot = _heads_transposed(dy_ref, g)
                lse = jnp.concatenate([lse_ref[g * GROUP + i:g * GROUP + i + 1, :] for i in range(GROUP)], axis=1)
                p = jnp.where(valid, jnp.exp(jnp.where(mask_c, _nn(kc, qt), _nn(kp, qt)) - lse), 0.0)
                dp = jnp.where(mask_c, _nn(vc, dot), _nn(vp, dot))
                delta = jnp.sum(p * dp, axis=0, keepdims=True)
                ds = p * (dp - delta)
                ds_c, p_c = jnp.where(mask_c, ds, 0.0), jnp.where(mask_c, p, 0.0)
                ds_p, p_p = ds - ds_c, p - p_c
                _heads_back(dq_ref, g, (_tn(kc, ds_c) + _tn(kp, ds_p)) * ATTN_SCALE)
                bot[:, ks], bot[:, vs] = _nt(ds_c, qt), _nt(p_c, dot)
                top[:, ks], top[:, vs] = _nt(ds_p, qt), _nt(p_p, dot)
                ds_ref[g:g + 1, :] -= jnp.exp(s_ref[g:g + 1, :] - lse) * delta
            dkv_ref[...] = (carry[...] + top[...]).astype(dkv_ref.dtype)
            carry[...] = bot[...]

        @pl.when(n == nb)
        def _():
            dkv_ref[...] = carry[...].astype(dkv_ref.dtype)

    return _call(
        "attn_bwd", body, grid=(nb + 1,),
        in_specs=[pl.BlockSpec((ATTN_BLOCK, D_MODEL), cur),
                  pl.BlockSpec((ATTN_BLOCK, 2 * KV_WIDTH), cur),
                  pl.BlockSpec((ATTN_BLOCK, 2 * KV_WIDTH), lambda n: (jnp.maximum(jnp.minimum(n, nb - 1) - 1, 0), 0)),
                  pl.BlockSpec((KV_HEADS, GROUP_LANES), lambda n: (0, 0)),
                  pl.BlockSpec((Q_HEADS, ATTN_BLOCK), lambda n: (0, jnp.minimum(n, nb - 1))),
                  pl.BlockSpec((ATTN_BLOCK, D_MODEL), cur)],
        out_specs=[pl.BlockSpec((ATTN_BLOCK, D_MODEL), cur),
                   pl.BlockSpec((ATTN_BLOCK, 2 * KV_WIDTH), lambda n: (jnp.maximum(n - 1, 0), 0)),
                   pl.BlockSpec((KV_HEADS, GROUP_LANES), lambda n: (0, 0))],
        out_shape=[jax.ShapeDtypeStruct((T, D_MODEL), MXU_DTYPE),
                   jax.ShapeDtypeStruct((T, 2 * KV_WIDTH), MXU_DTYPE),
                   jax.ShapeDtypeStruct((KV_HEADS, GROUP_LANES), F32)],
        scratch=[pltpu.VMEM((ATTN_BLOCK, 2 * KV_WIDTH), F32)] * 3,
        args=[pq, pkv, pkv, _sink_rows(sinks), lse, dy], semantics=("arbitrary",), comm=comm)


def _lower_bound(l):
    m = jnp.maximum(l[0:1], l[1:2])
    e0, e1 = jnp.exp(l[0:1] - m), jnp.exp(l[1:2] - m)
    return e0 / (e0 + e1)


def _tri(lower):
    r = lax.broadcasted_iota(jnp.int32, (CHUNK, CHUNK), 0)
    c = lax.broadcasted_iota(jnp.int32, (CHUNK, CHUNK), 1)
    return (r >= c) if lower else (c >= r)


def _chunk_sum(mask, v):
    return lax.dot_general(mask.astype(F32), v, (((1,), (0,)), ((), ())),
                           precision=lax.Precision.HIGHEST, preferred_element_type=F32)


def _hgrn_chunk_inputs(hq, hf, lb, causal):
    sg, sgn = _sigmoid(hf), _sigmoid(-hf)
    f = lb + (1.0 - lb) * sg
    kk = (1.0 - lb) * sgn
    sq = _sigmoid(hq)
    q = hq * sq
    b = _chunk_sum(causal, jnp.log(f))
    bm, bl = b[CHUNK // 2 - 1:CHUNK // 2, :], b[CHUNK - 1:CHUNK, :]
    e_qm, e_km, e_qs, e_kl = jnp.exp(b - bm), jnp.exp(bm - b), jnp.exp(b), jnp.exp(bl - b)
    return dict(sg=sg, sgn=sgn, f=f, kk=kk, sq=sq, q=q, e_qm=e_qm, e_km=e_km, e_qs=e_qs, e_kl=e_kl,
                qm=q * e_qm, km=kk * e_km, qs=q * e_qs, kl=kk * e_kl, el=jnp.exp(bl))


def _hgrn_fwd(ph, lb_logits, norm_g, comm=None):
    T = ph.shape[0]
    nblk, cpb = T // HGRN_TOKENS, HGRN_TOKENS // CHUNK
    col = lambda c: pl.BlockSpec((HGRN_TOKENS, D_MODEL), functools.partial(lambda i, c: (i, c), c=c))

    def body(hq_ref, hf_ref, hi_ref, hg_ref, l_ref, ng_ref, y_ref, o_ref, st_ref, s_ref):
        @pl.when(pl.program_id(0) == 0)
        def _():
            s_ref[...] = jnp.zeros_like(s_ref)

        lb = _lower_bound(l_ref[...])
        causal = _tri(True)
        for c in range(cpb):
            rows = slice(c * CHUNK, (c + 1) * CHUNK)
            t = _hgrn_chunk_inputs(hq_ref[rows, :], hf_ref[rows, :], lb, causal)
            qm, km, qs, kl = (t[n].astype(MXU_DTYPE) for n in ("qm", "km", "qs", "kl"))
            v = hi_ref[rows, :].astype(MXU_DTYPE)
            for h in range(HGRN_HEADS):
                ls = slice(h * HGRN_K, (h + 1) * HGRN_K)
                st = s_ref[h]
                st_ref[c, ls, :] = st
                a = jnp.where(causal, _nt(qm[:, ls], km[:, ls]), 0.0)
                o_ref[rows, ls] = _nn(a, v[:, ls]) + _nt(qs[:, ls], st)
                s_ref[h] = t["el"][:, ls] * st + _tn(v[:, ls], kl[:, ls])
        for h in range(HGRN_HEADS):
            ls = slice(h * HGRN_K, (h + 1) * HGRN_K)
            o = o_ref[:, ls]
            r = lax.rsqrt(jnp.mean(o * o, axis=-1, keepdims=True) + EPS)
            y_ref[:, ls] = (o * r * ng_ref[:, ls] * _sigmoid(hg_ref[:, ls])).astype(y_ref.dtype)

    return _call(
        "hgrn_fwd", body, grid=(nblk,),
        in_specs=[col(0), col(1), col(2), col(3),
                  pl.BlockSpec((2, D_MODEL), lambda i: (0, 0)), pl.BlockSpec((1, D_MODEL), lambda i: (0, 0))],
        out_specs=[pl.BlockSpec((HGRN_TOKENS, D_MODEL), lambda i: (i, 0)),
                   pl.BlockSpec((HGRN_TOKENS, D_MODEL), lambda i: (i, 0)),
                   pl.BlockSpec((cpb, D_MODEL, HGRN_K), lambda i: (i, 0, 0))],
        out_shape=[jax.ShapeDtypeStruct((T, D_MODEL), MXU_DTYPE), jax.ShapeDtypeStruct((T, D_MODEL), F32),
                   jax.ShapeDtypeStruct((T // CHUNK, D_MODEL, HGRN_K), F32)],
        scratch=[pltpu.VMEM((HGRN_HEADS, HGRN_K, HGRN_K), F32)],
        args=[ph, ph, ph, ph, lb_logits, norm_g], semantics=("arbitrary",), comm=comm)


def _hgrn_bwd(ph, o_raw, states, dy, lb_logits, norm_g, comm=None):
    T = ph.shape[0]
    nblk, cpb = T // HGRN_TOKENS, HGRN_TOKENS // CHUNK
    rev = lambda i: nblk - 1 - i
    col = lambda c: pl.BlockSpec((HGRN_TOKENS, D_MODEL), functools.partial(lambda i, c: (rev(i), c), c=c))
    tok = pl.BlockSpec((HGRN_TOKENS, D_MODEL), lambda i: (rev(i), 0))

    def body(hq_ref, hf_ref, hi_ref, hg_ref, o_ref, st_ref, dy_ref, l_ref, ng_ref,
             dph_ref, dng_ref, dl_ref, dst_ref, dlb_ref, do_s, dqm_s, dkm_s, dqs_s, dkl_s, dv_s, del_s):
        i = pl.program_id(0)

        @pl.when(i == 0)
        def _():
            dst_ref[...] = jnp.zeros_like(dst_ref)
            dlb_ref[...] = jnp.zeros_like(dlb_ref)
            dng_ref[...] = jnp.zeros_like(dng_ref)

        lb = _lower_bound(l_ref[...])
        causal, anti = _tri(True), _tri(False)
        row = lax.broadcasted_iota(jnp.int32, (CHUNK, D_MODEL), 0)
        for c in reversed(range(cpb)):
            rows = slice(c * CHUNK, (c + 1) * CHUNK)
            hq = hq_ref[rows, :]
            t = _hgrn_chunk_inputs(hq, hf_ref[rows, :], lb, causal)
            sgg = _sigmoid(hg_ref[rows, :])
            dyv = dy_ref[rows, :]
            for h in range(HGRN_HEADS):
                ls = slice(h * HGRN_K, (h + 1) * HGRN_K)
                o = o_ref[rows, ls]
                r = lax.rsqrt(jnp.mean(o * o, axis=-1, keepdims=True) + EPS)
                nrm = o * r
                g_h = sgg[:, ls]
                dph_ref[rows, 3 * D_MODEL + h * HGRN_K:3 * D_MODEL + (h + 1) * HGRN_K] = (
                    dyv[:, ls] * nrm * ng_ref[:, ls] * g_h * (1.0 - g_h)).astype(dph_ref.dtype)
                dyg = dyv[:, ls] * g_h
                dng_ref[:, ls] += jnp.sum(dyg * nrm, axis=0, keepdims=True)
                dn = dyg * ng_ref[:, ls]
                do_s[:, ls] = r * (dn - nrm * jnp.mean(dn * nrm, axis=-1, keepdims=True))
            qm, km, qs, kl = (t[n].astype(MXU_DTYPE) for n in ("qm", "km", "qs", "kl"))
            v = hi_ref[rows, :].astype(MXU_DTYPE)
            do = do_s[...].astype(MXU_DTYPE)
            for h in range(HGRN_HEADS):
                ls = slice(h * HGRN_K, (h + 1) * HGRN_K)
                st = st_ref[c, ls, :]
                dst = dst_ref[h]
                a = jnp.where(causal, _nt(qm[:, ls], km[:, ls]), 0.0)
                da = jnp.where(causal, _nt(do[:, ls], v[:, ls]), 0.0)
                dv_s[:, ls] = _tn(a, do[:, ls]) + _nt(kl[:, ls], dst)
                dkl_s[:, ls] = _nn(v[:, ls], dst)
                dqs_s[:, ls] = _nn(do[:, ls], st)
                del_s[:, ls] = jnp.sum(dst * st, axis=0, keepdims=True)
                dst_ref[h] = _tn(do[:, ls], qs[:, ls]) + t["el"][:, ls] * dst
                dqm_s[:, ls] = _nn(da, km[:, ls])
                dkm_s[:, ls] = _tn(da, qm[:, ls])
            dqm, dkm, dqs, dkl = dqm_s[...], dkm_s[...], dqs_s[...], dkl_s[...]
            dq = dqm * t["e_qm"] + dqs * t["e_qs"]
            dk = dkm * t["e_km"] + dkl * t["e_kl"]
            t_qm, t_km, t_kl = dqm * t["qm"], dkm * t["km"], dkl * t["kl"]
            db = t_qm - t_km + dqs * t["qs"] - t_kl
            db_mid = jnp.sum(t_km - t_qm, axis=0, keepdims=True)
            db_last = jnp.sum(t_kl, axis=0, keepdims=True) + del_s[...] * t["el"]
            db = db + jnp.where(row == CHUNK // 2 - 1, db_mid, 0.0) + jnp.where(row == CHUNK - 1, db_last, 0.0)
            dlogf = _chunk_sum(anti, db)
            sq, sg, sgn, f = t["sq"], t["sg"], t["sgn"], t["f"]
            dph_ref[rows, 0:D_MODEL] = (dq * (sq * (1.0 + hq * (1.0 - sq)))).astype(dph_ref.dtype)
            dph_ref[rows, D_MODEL:2 * D_MODEL] = (
                dlogf * (1.0 - lb) * sg * (1.0 - sg) / f - dk * (1.0 - lb) * sgn * (1.0 - sgn)).astype(dph_ref.dtype)
            dph_ref[rows, 2 * D_MODEL:3 * D_MODEL] = dv_s[...].astype(dph_ref.dtype)
            dlb_ref[...] += jnp.sum(dlogf * (1.0 - sg) / f - dk * sgn, axis=0, keepdims=True)

        @pl.when(i == nblk - 1)
        def _():
            dl0 = dlb_ref[...] * lb * (1.0 - lb)
            dl_ref[0:1, :] = dl0
            dl_ref[1:2, :] = -dl0

    wide = pltpu.VMEM((CHUNK, D_MODEL), F32)
    return _call(
        "hgrn_bwd", body, grid=(nblk,),
        in_specs=[col(0), col(1), col(2), col(3), tok,
                  pl.BlockSpec((cpb, D_MODEL, HGRN_K), lambda i: (rev(i), 0, 0)), tok,
                  pl.BlockSpec((2, D_MODEL), lambda i: (0, 0)), pl.BlockSpec((1, D_MODEL), lambda i: (0, 0))],
        out_specs=[pl.BlockSpec((HGRN_TOKENS, 4 * D_MODEL), lambda i: (rev(i), 0)),
                   pl.BlockSpec((1, D_MODEL), lambda i: (0, 0)), pl.BlockSpec((2, D_MODEL), lambda i: (0, 0))],
        out_shape=[jax.ShapeDtypeStruct((T, 4 * D_MODEL), MXU_DTYPE), jax.ShapeDtypeStruct((1, D_MODEL), F32),
                   jax.ShapeDtypeStruct((2, D_MODEL), F32)],
        scratch=[pltpu.VMEM((HGRN_HEADS, HGRN_K, HGRN_K), F32), pltpu.VMEM((1, D_MODEL), F32),
                 wide, wide, wide, wide, wide, wide, pltpu.VMEM((1, D_MODEL), F32)],
        args=[ph, ph, ph, ph, o_raw, states, dy, lb_logits, norm_g], semantics=("arbitrary",), comm=comm)


def _local_step(x, target, vec, net):
    T, D = x.shape
    norm_mix_g, b_in, sinks, lb_logits = vec["norm_mix_g"], vec["b_in"], vec["attn_sinks"], vec["hgrn_lb_logits"]
    hgrn_norm_g, norm_ffn_g, norm_final_g = vec["hgrn_norm_g"], vec["norm_ffn_g"], vec["norm_final_g"]
    w_in = net.full("w_in")
    o_q, o_kv, o_h, o_g = (sum(IN_SPLITS[:i]) for i in range(4))
    w_q, w_kv, w_h, w_g = (w_in[:, o:o + n] for o, n in zip((o_q, o_kv, o_h, o_g), IN_SPLITS))
    b_q, b_kv, b_h, b_g = (b_in[:, o:o + n] for o, n in zip((o_q, o_kv, o_h, o_g), IN_SPLITS))
    bias = lambda acc, b: (acc + b,)
    row_vec = lambda n: ((1, n), lambda i, j: (0, j))
    tile = lambda tm, tn: ((tm, tn), lambda i, j: (i, j))
    TM = 512

    u, rstd1 = _rmsnorm_fwd("norm_mix", x, norm_mix_g)
    pq, = _matmul("in_q", [(u, w_q)], "nn", tm=TM, tn=1024, tk=1024, outs=[("mn", F32)],
                  extras=[(b_q, *row_vec(1024))], epilogue=bias)
    pkv, = _matmul("in_kv", [(u, w_kv)], "nn", tm=TM, tn=256, tk=1024, outs=[("mn", F32)],
                   extras=[(b_kv, *row_vec(256))], epilogue=bias)
    names = ("w_branch_attn", "w_branch_hgrn", "w_out")
    (ph,), got = _matmul("in_h", [(u, w_h)], "nn", tm=TM, tn=1024, tk=1024, outs=[("mn", F32)],
                         extras=[(b_h, *row_vec(1024))], epilogue=bias, comm=net.gather(names))
    net.gathered(names, got)
    pg, = _matmul("in_g", [(u, w_g)], "nn", tm=TM, tn=1024, tk=1024, outs=[("mn", F32)],
                  extras=[(b_g, *row_vec(1024))], epilogue=bias)
    names = ("w_ffn_gate", "w_ffn_up")
    (y_attn, lse), got = _attn_fwd(pq, pkv, sinks, comm=net.gather(names))
    net.gathered(names, got)
    names = ("w_ffn_down",)
    (y_hgrn, o_raw, states), got = _hgrn_fwd(ph, lb_logits, hgrn_norm_g, comm=net.gather(names))
    net.gathered(names, got)
    w_ba, w_bh, w_out = net.full("w_branch_attn"), net.full("w_branch_hgrn"), net.full("w_out")
    w_gate, w_up, w_down = net.full("w_ffn_gate"), net.full("w_ffn_up"), net.full("w_ffn_down")
    ya, = _matmul("branch_a", [(y_attn, w_ba)], "nn", tm=TM, tn=1024, tk=1024, outs=[("mn", F32)])
    gate_a = (pg, (TM, 1024), lambda i, j: (i, 0))
    gate_b = (pg, (TM, 1024), lambda i, j: (i, 1))

    def merge(acc, ya_v, ga, gb):
        return acc, _sigmoid(ga) * ya_v + _sigmoid(gb) * acc

    yb, merged = _matmul("branch_b", [(y_hgrn, w_bh)], "nn", tm=TM, tn=1024, tk=1024,
                         outs=[("mn", F32), ("mn", MXU_DTYPE)],
                         extras=[(ya, *tile(TM, 1024)), gate_a, gate_b], epilogue=merge)

    def resid_norm(acc, xin, g):
        h = xin + acc
        r = lax.rsqrt(jnp.mean(h * h, axis=-1, keepdims=True) + EPS)
        return h, h * r * g, r

    h1, u2, rstd2 = _matmul("out_proj", [(merged, w_out)], "nn", tm=TM, tn=1024, tk=1024,
                            outs=[("mn", F32), ("mn", MXU_DTYPE), ("m", F32)],
                            extras=[(x, *tile(TM, 1024)), (norm_ffn_g, *row_vec(1024))], epilogue=resid_norm)
    FT = FFN // 2
    gpre, = _matmul("ffn_gate", [(u2, w_gate)], "nn", tm=TM, tn=FT, tk=1024, outs=[("mn", F32)])

    def swiglu(acc, gv):
        return acc, gv * _sigmoid(gv) * acc

    up, z = _matmul("ffn_up", [(u2, w_up)], "nn", tm=TM, tn=FT, tk=1024, outs=[("mn", F32), ("mn", MXU_DTYPE)],
                    extras=[(gpre, *tile(TM, FT))], epilogue=swiglu)

    def loss_head(acc, h1_v, tgt, g):
        h2 = h1_v + acc
        r = lax.rsqrt(jnp.mean(h2 * h2, axis=-1, keepdims=True) + EPS)
        xhat = h2 * r
        err = xhat * g - tgt
        part = 0.5 * jnp.sum(jnp.sum(err * err, axis=-1, keepdims=True), axis=0, keepdims=True) / D
        dyv = err / D
        dxh = dyv * g
        dh2 = r * (dxh - xhat * jnp.mean(dxh * xhat, axis=-1, keepdims=True))
        return dh2, dh2, jnp.sum(dyv * xhat, axis=0, keepdims=True), jnp.broadcast_to(part, (1, D))

    dh2, dh2b, dgf_p, loss_p = _matmul(
        "ffn_down", [(z, w_down)], "nn", tm=TM, tn=1024, tk=FFN,
        outs=[("mn", F32), ("mn", MXU_DTYPE), ("pn", F32), ("pn", F32)],
        extras=[(h1, *tile(TM, 1024)), (target, *tile(TM, 1024)), (norm_final_g, *row_vec(1024))],
        epilogue=loss_head)
    loss = jnp.sum(loss_p.reshape(-1, 8, D)[:, 0, 0])
    d_norm_final = _colsum_partials(dgf_p)

    def swiglu_bwd(acc, gv, upv):
        s = _sigmoid(gv)
        return acc * upv * (s * (1.0 + gv * (1.0 - s))), acc * (gv * s)

    dgp, dup = _matmul("d_ffn_hidden", [(dh2b, w_down)], "nt", tm=TM, tn=FT, tk=1024,
                       outs=[("mn", MXU_DTYPE), ("mn", MXU_DTYPE)],
                       extras=[(gpre, *tile(TM, FT)), (up, *tile(TM, FT))], epilogue=swiglu_bwd)
    d_w_down, = _matmul("dw_down", [(z, dh2b)], "tn", tm=FT, tn=1024, tk=512, outs=[("mn", F32)])

    def norm_ffn_bwd(acc, h1_v, r, g, dres):
        dx, dg = _rmsnorm_bwd_vals(acc, h1_v, r, g)
        dh = dres + dx
        return dh, dh, dg

    names = ("w_ffn_down",)
    (dh1, dh1b, dg2_p), got = _matmul(
        "d_ffn_in", [(dgp, w_gate), (dup, w_up)], "nt", tm=TM, tn=1024, tk=FT,
        outs=[("mn", F32), ("mn", MXU_DTYPE), ("pn", F32)],
        extras=[(h1, *tile(TM, 1024)), (rstd2, (TM, 1), lambda i, j: (i, 0)),
                (norm_ffn_g, *row_vec(1024)), (dh2, *tile(TM, 1024))], epilogue=norm_ffn_bwd,
        comm=net.exchange(dict(w_ffn_down=d_w_down)))
    net.received(names, got)
    d_norm_ffn = _colsum_partials(dg2_p)
    d_w_gate, = _matmul("dw_gate", [(u2, dgp)], "tn", tm=1024, tn=FT, tk=512, outs=[("mn", F32)])
    d_w_up, = _matmul("dw_up", [(u2, dup)], "tn", tm=1024, tn=FT, tk=512, outs=[("mn", F32)])

    def merge_bwd(acc, ya_v, yb_v, ga, gb):
        sa, sb = _sigmoid(ga), _sigmoid(gb)
        return acc * sa, acc * sb, acc * ya_v * sa * (1.0 - sa), acc * yb_v * sb * (1.0 - sb)

    dya, dyb, dga, dgb = _matmul(
        "d_merged", [(dh1b, w_out)], "nt", tm=TM, tn=1024, tk=1024, outs=[("mn", MXU_DTYPE)] * 4,
        extras=[(ya, *tile(TM, 1024)), (yb, *tile(TM, 1024)), gate_a, gate_b], epilogue=merge_bwd)
    d_w_out, = _matmul("dw_out", [(merged, dh1b)], "tn", tm=1024, tn=1024, tk=512, outs=[("mn", F32)])
    dy_attn, = _matmul("d_y_attn", [(dya, w_ba)], "nt", tm=TM, tn=1024, tk=1024, outs=[("mn", F32)])
    dy_hgrn, = _matmul("d_y_hgrn", [(dyb, w_bh)], "nt", tm=TM, tn=1024, tk=1024, outs=[("mn", F32)])
    d_w_ba, = _matmul("dw_branch_a", [(y_attn, dya)], "tn", tm=1024, tn=1024, tk=512, outs=[("mn", F32)])
    d_w_bh, = _matmul("dw_branch_b", [(y_hgrn, dyb)], "tn", tm=1024, tn=1024, tk=512, outs=[("mn", F32)])

    names = ("w_ffn_gate", "w_ffn_up")
    (dq, dkv, dsink), got = _attn_bwd(pq, pkv, sinks, lse, dy_attn,
                                      comm=net.exchange(dict(w_ffn_gate=d_w_gate, w_ffn_up=d_w_up)))
    net.received(names, got)
    names = ("w_out", "w_branch_attn", "w_branch_hgrn")
    (dph, d_hgrn_norm, d_lb_logits), got = _hgrn_bwd(
        ph, o_raw, states, dy_hgrn, lb_logits, hgrn_norm_g,
        comm=net.exchange(dict(w_out=d_w_out, w_branch_attn=d_w_ba, w_branch_hgrn=d_w_bh)))
    net.received(names, got)

    dw_q, = _matmul("dw_in_q", [(u, dq)], "tn", tm=1024, tn=1024, tk=512, outs=[("mn", F32)])
    dw_kv, = _matmul("dw_in_kv", [(u, dkv)], "tn", tm=1024, tn=256, tk=512, outs=[("mn", F32)])
    dw_h, = _matmul("dw_in_h", [(u, dph)], "tn", tm=1024, tn=1024, tk=512, outs=[("mn", F32)])
    dw_ga, = _matmul("dw_in_ga", [(u, dga)], "tn", tm=1024, tn=1024, tk=512, outs=[("mn", F32)])
    dw_gb, = _matmul("dw_in_gb", [(u, dgb)], "tn", tm=1024, tn=1024, tk=512, outs=[("mn", F32)])
    d_w_in = jnp.concatenate([dw_q, dw_kv, dw_h, dw_ga, dw_gb], axis=1)

    def norm_mix_bwd(acc, xin, r, g, dres):
        dx, dg = _rmsnorm_bwd_vals(acc, xin, r, g)
        return dres + dx, dg

    names = ("w_in",)
    (dx, dg1_p), got = _matmul(
        "d_u", [(dq, w_q), (dkv, w_kv), (dph, w_h), (dga, w_g[:, :D]), (dgb, w_g[:, D:])], "nt",
        tm=TM, tn=1024, tk=512, outs=[("mn", F32), ("pn", F32)],
        extras=[(x, *tile(TM, 1024)), (rstd1, (TM, 1), lambda i, j: (i, 0)),
                (norm_mix_g, *row_vec(1024)), (dh1, *tile(TM, 1024))], epilogue=norm_mix_bwd,
        comm=net.exchange(dict(w_in=d_w_in)))
    net.received(names, got)
    d_norm_mix = _colsum_partials(dg1_p)
    d_b_in = jnp.concatenate([_colsum("db_q", dq), _colsum("db_kv", dkv), _colsum("db_h", dph),
                              _colsum("db_ga", dga), _colsum("db_gb", dgb)], axis=1)
    vecs = dict(norm_mix_g=d_norm_mix, b_in=d_b_in, attn_sinks=jnp.sum(dsink.reshape(Q_HEADS, ATTN_BLOCK), axis=1).reshape(1, Q_HEADS),
                hgrn_lb_logits=d_lb_logits,
                hgrn_norm_g=d_hgrn_norm, norm_ffn_g=d_norm_ffn, norm_final_g=d_norm_final)
    return loss, dx, vecs


def _place():
    return lax.axis_index("x"), lax.axis_index("y"), lax.axis_index("c")


def _other_chips(x, y):
    return [(1 - x, y), (x, 1 - y), (1 - x, 1 - y)]


def _gather_copies(shards):
    n = len(shards)

    def build(ins, outs, send_sems, recv_sems, local_sems):
        x, y, c = _place()
        mine = 2 * x + y
        local = [pltpu.make_async_copy(ins[w], outs[w].at[mine], local_sems.at[w]) for w in range(n)]
        sends, recvs = [], []
        for w in range(n):
            for k, (px, py) in enumerate(_other_chips(x, y)):
                sem = 3 * w + k
                sends.append(pltpu.make_async_remote_copy(
                    src_ref=ins[w], dst_ref=outs[w].at[mine], send_sem=send_sems.at[sem], recv_sem=recv_sems.at[sem],
                    device_id=(px, py, c), device_id_type=MESH_ID))
                recvs.append(pltpu.make_async_remote_copy(
                    src_ref=ins[w], dst_ref=outs[w].at[2 * px + py], send_sem=send_sems.at[sem],
                    recv_sem=recv_sems.at[sem], device_id=(px, py, c), device_id_type=MESH_ID))
        return sends, recvs, local

    return _Carried(shards, [jax.ShapeDtypeStruct((N_CHIPS,) + s.shape, s.dtype) for s in shards], 3 * n, n, build)


def _grad_copies(stacked):
    n = len(stacked)

    def build(ins, outs, send_sems, recv_sems, local_sems):
        x, y, c = _place()
        sends = []
        for w in range(n):
            for k, (px, py) in enumerate(_other_chips(x, y)):
                sem = 3 * w + k
                sends.append(pltpu.make_async_remote_copy(
                    src_ref=ins[w].at[2 * px + py], dst_ref=outs[w].at[k], send_sem=send_sems.at[sem],
                    recv_sem=recv_sems.at[sem], device_id=(px, py, c), device_id_type=MESH_ID))
        return sends, sends, []

    return _Carried(stacked, [jax.ShapeDtypeStruct((3,) + s.shape[1:], s.dtype) for s in stacked], 3 * n, 0, build)


def _small_copies(small):
    def build(ins, outs, send_sems, recv_sems, local_sems):
        small_ref, all_ref = ins[0], outs[0]
        x, y, c = _place()
        me = 4 * x + 2 * y + c
        sends, recvs = [], []
        for r in range(1, 8):
            px = 1 - x if r & 4 else x
            py = 1 - y if r & 2 else y
            pc = 1 - c if r & 1 else c
            sends.append(pltpu.make_async_remote_copy(
                src_ref=small_ref, dst_ref=all_ref.at[me], send_sem=send_sems.at[r - 1], recv_sem=recv_sems.at[r - 1],
                device_id=(px, py, pc), device_id_type=MESH_ID))
            recvs.append(pltpu.make_async_remote_copy(
                src_ref=small_ref, dst_ref=all_ref.at[4 * px + 2 * py + pc], send_sem=send_sems.at[r - 1],
                recv_sem=recv_sems.at[r - 1], device_id=(px, py, pc), device_id_type=MESH_ID))
        return sends, recvs, [pltpu.make_async_copy(small_ref, all_ref.at[me], local_sems.at[0])]

    return _Carried([small], [jax.ShapeDtypeStruct((8,) + small.shape, small.dtype)], 7, 1, build)


def _copies_alone(name, comm):
    return _call(name, lambda: None, grid=(), in_specs=[], out_specs=[], out_shape=[], args=[], comm=comm)[1]


class _Net:
    def __init__(self, shards):
        self.shards = shards
        self.whole, self.own, self.theirs = {}, {}, {}
        x, y, _ = _place()
        self.chip = 2 * x + y

    def gather(self, names):
        return _gather_copies([self.shards[n] for n in names])

    def gathered(self, names, got):
        for n, g in zip(names, got):
            if n in COLUMN_SHARDED:
                self.whole[n] = jnp.concatenate([g[s] for s in range(N_CHIPS)], axis=1)
            else:
                self.whole[n] = g.reshape(-1, g.shape[-1])

    def full(self, name):
        return self.whole[name]

    def exchange(self, grads):
        stacked = []
        for n, d in grads.items():
            s = (jnp.stack(jnp.split(d, N_CHIPS, axis=1)) if n in COLUMN_SHARDED
                 else d.reshape(N_CHIPS, -1, d.shape[-1]))
            self.own[n] = lax.dynamic_index_in_dim(s, self.chip, 0, keepdims=False)
            stacked.append(s.astype(MXU_DTYPE))
        return _grad_copies(stacked)

    def received(self, names, got):
        self.theirs.update(zip(names, got))


def _exchange_sibling(parts):
    n = len(parts)

    def body(*refs):
        ins, outs = refs[:n], refs[n:2 * n]
        send_sems, recv_sems = refs[2 * n:]
        x, y, c = _place()
        copies = [pltpu.make_async_remote_copy(
            src_ref=ins[w], dst_ref=outs[w], send_sem=send_sems.at[w], recv_sem=recv_sems.at[w],
            device_id=(x, y, 1 - c), device_id_type=MESH_ID) for w in range(n)]
        for cp in copies:
            cp.start()
        for cp in copies:
            cp.wait_recv()
        for cp in copies:
            cp.wait_send()

    return pl.pallas_call(
        body, name="exchange_sibling",
        in_specs=[HBM_SPEC] * n, out_specs=[HBM_SPEC] * n,
        out_shape=[jax.ShapeDtypeStruct(p.shape, p.dtype) for p in parts],
        scratch_shapes=[pltpu.SemaphoreType.DMA((n,)), pltpu.SemaphoreType.DMA((n,))],
        compiler_params=pltpu.CompilerParams(has_side_effects=True),
    )(*parts)


def _row_tile(rows):
    return 128 if rows % 128 == 0 else 176


def _partial_sum(name, own, recv):
    R, C = own.shape
    tr = _row_tile(R)

    def body(o_ref, r_ref, p_ref):
        p_ref[...] = ((o_ref[...] + r_ref[0].astype(F32)) + r_ref[1].astype(F32)) + r_ref[2].astype(F32)

    return pl.pallas_call(
        body, name=name, grid=(R // tr,),
        in_specs=[pl.BlockSpec((tr, C), lambda i: (i, 0)), pl.BlockSpec((3, tr, C), lambda i: (0, i, 0))],
        out_specs=pl.BlockSpec((tr, C), lambda i: (i, 0)),
        out_shape=jax.ShapeDtypeStruct((R, C), F32),
        compiler_params=_params(("parallel",)),
    )(own, recv)


def _adam_vals(w, g, m, v):
    m = ADAM_B1 * m + (1.0 - ADAM_B1) * g
    v = ADAM_B2 * v + (1.0 - ADAM_B2) * (g * g)
    m_hat = m / (1.0 - ADAM_B1 ** ADAM_STEP)
    v_hat = v / (1.0 - ADAM_B2 ** ADAM_STEP)
    delta = -ADAM_LR * (m_hat / (jnp.sqrt(v_hat) + ADAM_EPS) + ADAM_WD * w)
    return delta, m, v


def _adamw(name, w, m, v, p_south, p_north):
    R, C = w.shape
    tr = _row_tile(R)

    def body(w_ref, m_ref, v_ref, s_ref, n_ref, g_ref, d_ref, nm_ref, nv_ref):
        g = s_ref[...] + n_ref[...]
        d, nm, nv = _adam_vals(w_ref[...], g, m_ref[...], v_ref[...])
        g_ref[...], d_ref[...], nm_ref[...], nv_ref[...] = g, d, nm, nv

    spec = pl.BlockSpec((tr, C), lambda i: (i, 0))
    return pl.pallas_call(
        body, name=name, grid=(R // tr,), in_specs=[spec] * 5, out_specs=[spec] * 4,
        out_shape=[jax.ShapeDtypeStruct((R, C), F32)] * 4,
        compiler_params=_params(("parallel",)),
    )(w, m, v, p_south, p_north)


def _adamw_small(w, m, v, g_all):
    def body(w_ref, m_ref, v_ref, a_ref, g_ref, d_ref, nm_ref, nv_ref):
        g = a_ref[0]
        for dev in range(1, 8):
            g = g + a_ref[dev]
        d, nm, nv = _adam_vals(w_ref[...], g, m_ref[...], v_ref[...])
        g_ref[...], d_ref[...], nm_ref[...], nv_ref[...] = g, d, nm, nv

    return pl.pallas_call(
        body, name="adamw_small", out_shape=[jax.ShapeDtypeStruct(w.shape, F32)] * 4,
    )(w, m, v, g_all)


SMALL_LAYOUT = dict(norm_mix_g=(0, 1024), b_in=(8, 7424), hgrn_norm_g=(16, 1024), norm_ffn_g=(24, 1024),
                    norm_final_g=(32, 1024), hgrn_lb_logits=(40, 2048), attn_sinks=(48, 16))


def _pack_small(vals):
    parts = []
    for name, (_, nvalid) in SMALL_LAYOUT.items():
        flat = vals[name].astype(F32).reshape(-1)
        parts.append(jnp.pad(flat, (0, 8 * D_MODEL - nvalid)).reshape(8, D_MODEL))
    return jnp.concatenate(parts, axis=0)


def _unpack_small(packed, shapes):
    return {name: packed[r0:r0 + 8].reshape(-1)[:nvalid].reshape(shapes[name])
            for name, (r0, nvalid) in SMALL_LAYOUT.items()}


MATRICES = ("w_in", "w_branch_attn", "w_branch_hgrn", "w_out", "w_ffn_gate", "w_ffn_up", "w_ffn_down")
COLUMN_SHARDED = ("w_in", "w_ffn_gate", "w_ffn_up")
WEIGHTS = ("norm_mix_g", "w_in", "b_in", "attn_sinks", "hgrn_lb_logits", "hgrn_norm_g", "w_branch_attn",
           "w_branch_hgrn", "w_out", "norm_ffn_g", "w_ffn_gate", "w_ffn_up", "w_ffn_down", "norm_final_g")


def kernel(x, norm_mix_g, w_in, b_in, attn_sinks, hgrn_lb_logits, hgrn_norm_g, w_branch_attn, w_branch_hgrn, w_out, norm_ffn_g, w_ffn_gate, w_ffn_up, w_ffn_down, norm_final_g, loss_target, m_norm_mix_g, m_w_in, m_b_in, m_attn_sinks, m_hgrn_lb_logits, m_hgrn_norm_g, m_w_branch_attn, m_w_branch_hgrn, m_w_out, m_norm_ffn_g, m_w_ffn_gate, m_w_ffn_up, m_w_ffn_down, m_norm_final_g, v_norm_mix_g, v_w_in, v_b_in, v_attn_sinks, v_hgrn_lb_logits, v_hgrn_norm_g, v_w_branch_attn, v_w_branch_hgrn, v_w_out, v_norm_ffn_g, v_w_ffn_gate, v_w_ffn_up, v_w_ffn_down, v_norm_final_g):
    given = dict(locals())
    w = {n: given[n] for n in WEIGHTS}
    m = {n: given["m_" + n] for n in WEIGHTS}
    v = {n: given["v_" + n] for n in WEIGHTS}

    net = _Net({n: w[n][0].astype(MXU_DTYPE) for n in MATRICES})
    net.gathered(("w_in",), _copies_alone("gather_w_in", net.gather(("w_in",))))
    vec = dict(norm_mix_g=norm_mix_g, b_in=b_in, attn_sinks=attn_sinks, hgrn_lb_logits=hgrn_lb_logits,
               hgrn_norm_g=hgrn_norm_g, norm_ffn_g=norm_ffn_g, norm_final_g=norm_final_g.reshape(1, D_MODEL))
    loss_part, dx, d_vecs = _local_step(x[0], loss_target[0], vec, net)

    small_all, = _copies_alone("exchange_small", _small_copies(_pack_small(d_vecs)))
    sums = [_partial_sum("sum_" + n, net.own[n], net.theirs[n]) for n in MATRICES]
    theirs = _exchange_sibling(sums)

    grads, deltas, new_m, new_v = {}, {}, {}, {}
    for n, mine, other in zip(MATRICES, sums, theirs):
        res = _adamw("adamw_" + n, w[n][0], m[n][0], v[n][0], mine, other)
        grads[n], deltas[n], new_m[n], new_v[n] = (r[None] for r in res)
    shapes = {n: w[n].shape for n in SMALL_LAYOUT}
    res = _adamw_small(_pack_small(w), _pack_small(m), _pack_small(v), small_all)
    for dst, packed in zip((grads, deltas, new_m, new_v), res):
        dst.update(_unpack_small(packed, shapes))

    loss = lax.psum(loss_part, ("x", "y", "c"))
    return (loss, dx[None], *[grads[n] for n in WEIGHTS], *[deltas[n] for n in WEIGHTS],
            *[new_m[n] for n in WEIGHTS], *[new_v[n] for n in WEIGHTS])
```

```python
import functools
import math

import jax
import jax.numpy as jnp
from jax import lax
from jax.experimental import pallas as pl
from jax.experimental.pallas import tpu as pltpu

F32 = jnp.float32
BF16 = jnp.bfloat16
MXU_DTYPE = jnp.bfloat16
MESH_ID = pl.DeviceIdType.MESH

D_MODEL = 1024
HEAD_DIM = 64
Q_HEADS = 16
KV_HEADS = 2
GROUP = Q_HEADS // KV_HEADS
KV_WIDTH = KV_HEADS * HEAD_DIM
ATTN_BLOCK = 128
HGRN_HEADS = 8
HGRN_K = 128
CHUNK = 64
HGRN_TOKENS = 256
FFN = 2816
IN_SPLITS = (1024, 256, 4096, 2048)
EPS = 1e-6
NEG_INF = -1e30
ADAM_LR, ADAM_B1, ADAM_B2, ADAM_EPS, ADAM_WD, ADAM_STEP = 0.001, 0.9, 0.999, 1e-08, 0.01, 10
N_CHIPS = 4
VMEM_LIMIT = 56 * 1024 * 1024


def _params(sem=None):
    return pltpu.CompilerParams(dimension_semantics=sem, vmem_limit_bytes=VMEM_LIMIT)


def _sigmoid(v):
    return 1.0 / (1.0 + jnp.exp(-v))


def _dot(a, b, dims):
    return lax.dot_general(a.astype(MXU_DTYPE), b.astype(MXU_DTYPE), (dims, ((), ())),
                           preferred_element_type=F32)


def _nn(a, b):
    return _dot(a, b, ((1,), (0,)))


def _nt(a, b):
    return _dot(a, b, ((1,), (1,)))


def _tn(a, b):
    return _dot(a, b, ((0,), (0,)))


HBM_SPEC = pl.BlockSpec(memory_space=pl.ANY)


class _Carried:
    def __init__(self, arrays, out_shapes, n_remote, n_local, build):
        self.arrays, self.out_shapes, self.build = list(arrays), list(out_shapes), build
        self.scratch = [pltpu.SemaphoreType.DMA((n_remote,)), pltpu.SemaphoreType.DMA((n_remote,)),
                        pltpu.SemaphoreType.DMA((max(n_local, 1),))]

    def start(self, ins, outs, sems):
        sends, _, local = self.build(ins, outs, *sems)
        for cp in local + sends:
            cp.start()

    def wait(self, ins, outs, sems):
        sends, recvs, local = self.build(ins, outs, *sems)
        for cp in recvs:
            cp.wait_recv()
        for cp in sends:
            cp.wait_send()
        for cp in local:
            cp.wait()


def _call(name, body, *, grid, in_specs, out_specs, out_shape, args, scratch=(), semantics=None, comm=None):
    n_in, n_out, n_scr = len(in_specs), len(out_specs), len(scratch)
    if comm is None:
        res = pl.pallas_call(body, name=name, grid=grid, in_specs=in_specs, out_specs=out_specs, out_shape=out_shape,
                             scratch_shapes=list(scratch), compiler_params=_params(semantics))(*args)
        return list(res), []
    ci, co = len(comm.arrays), len(comm.out_shapes)

    def carrying(*refs):
        ins, refs = refs[:n_in], refs[n_in:]
        c_ins, refs = refs[:ci], refs[ci:]
        outs, refs = refs[:n_out], refs[n_out:]
        c_outs, refs = refs[:co], refs[co:]
        scr, sems = refs[:n_scr], refs[n_scr:]
        if not grid:
            comm.start(c_ins, c_outs, sems)
            body(*ins, *outs, *scr)
            comm.wait(c_ins, c_outs, sems)
            return
        first = functools.reduce(jnp.logical_and, [pl.program_id(a) == 0 for a in range(len(grid))])
        last = functools.reduce(jnp.logical_and, [pl.program_id(a) == g - 1 for a, g in enumerate(grid)])

        @pl.when(first)
        def _():
            comm.start(c_ins, c_outs, sems)

        body(*ins, *outs, *scr)

        @pl.when(last)
        def _():
            comm.wait(c_ins, c_outs, sems)

    res = pl.pallas_call(
        carrying, name=name, grid=grid, in_specs=list(in_specs) + [HBM_SPEC] * ci,
        out_specs=list(out_specs) + [HBM_SPEC] * co, out_shape=list(out_shape) + comm.out_shapes,
        scratch_shapes=list(scratch) + comm.scratch,
        compiler_params=_params(("arbitrary",) * len(grid) if grid else None),
    )(*args, *comm.arrays)
    return list(res[:n_out]), list(res[n_out:])


_NO_COPIES = object()


def _matmul(name, pairs, mode, *, tm, tn, tk, outs, extras=(), epilogue=None, comm=_NO_COPIES):
    prod = dict(nn=_nn, nt=_nt, tn=_tn)[mode]
    a0, b0 = pairs[0]
    M = a0.shape[1] if mode == "tn" else a0.shape[0]
    N = b0.shape[0] if mode == "nt" else b0.shape[1]
    steps, in_specs, offset = [], [], 0
    for a, b in pairs:
        K = a.shape[0] if mode == "tn" else a.shape[1]
        t = min(tk, K)
        assert K % t == 0, (name, K, t)
        kmap = functools.partial(lambda k, off, n: jnp.clip(k - off, 0, n - 1), off=offset, n=K // t)
        if mode == "tn":
            in_specs.append(pl.BlockSpec((t, tm), functools.partial(lambda i, j, k, f: (f(k), i), f=kmap)))
        else:
            in_specs.append(pl.BlockSpec((tm, t), functools.partial(lambda i, j, k, f: (i, f(k)), f=kmap)))
        if mode == "nt":
            in_specs.append(pl.BlockSpec((tn, t), functools.partial(lambda i, j, k, f: (j, f(k)), f=kmap)))
        else:
            in_specs.append(pl.BlockSpec((t, tn), functools.partial(lambda i, j, k, f: (f(k), j), f=kmap)))
        steps.append((offset, offset + K // t))
        offset += K // t
    assert M % tm == 0 and N % tn == 0, (name, M, N, tm, tn)
    ni, nj, nk = M // tm, N // tn, offset
    npair, ne, no = len(pairs), len(extras), len(outs)
    if epilogue is None:
        epilogue = lambda acc: (acc,)

    def finish(acc, extra_refs, out_refs):
        vals = epilogue(acc, *[r[...] for r in extra_refs])
        for (kind, _), o_ref, val in zip(outs, out_refs, vals):
            if kind == "pn":
                val = jnp.broadcast_to(val, o_ref.shape)
            o_ref[...] = val.astype(o_ref.dtype)

    def body(*refs):
        ab, rest = refs[:2 * npair], refs[2 * npair:]
        extra_refs, out_refs = rest[:ne], rest[ne:ne + no]
        if nk == 1:
            finish(prod(ab[0][...], ab[1][...]), extra_refs, out_refs)
            return
        acc_ref = rest[ne + no]
        k = pl.program_id(2)

        @pl.when(k == 0)
        def _():
            acc_ref[...] = jnp.zeros_like(acc_ref)

        for p, (lo, hi) in enumerate(steps):
            @pl.when(jnp.logical_and(k >= lo, k < hi))
            def _():
                acc_ref[...] += prod(ab[2 * p][...], ab[2 * p + 1][...])

        @pl.when(k == nk - 1)
        def _():
            finish(acc_ref[...], extra_refs, out_refs)

    for _, shape, im in extras:
        in_specs.append(pl.BlockSpec(shape, functools.partial(lambda i, j, k, im: im(i, j), im=im)))
    out_shape, out_specs = [], []
    for kind, dt in outs:
        if kind == "mn":
            out_shape.append(jax.ShapeDtypeStruct((M, N), dt))
            out_specs.append(pl.BlockSpec((tm, tn), lambda i, j, k: (i, j)))
        elif kind == "m":
            assert nj == 1
            out_shape.append(jax.ShapeDtypeStruct((M, 1), dt))
            out_specs.append(pl.BlockSpec((tm, 1), lambda i, j, k: (i, 0)))
        else:
            out_shape.append(jax.ShapeDtypeStruct((8 * ni, N), dt))
            out_specs.append(pl.BlockSpec((8, tn), lambda i, j, k: (i, j)))
    res, got = _call(name, body, grid=(ni, nj, nk), in_specs=in_specs, out_specs=out_specs, out_shape=out_shape,
                     args=[t for pair in pairs for t in pair] + [e[0] for e in extras],
                     scratch=[pltpu.VMEM((tm, tn), F32)] if nk > 1 else [],
                     semantics=("parallel", "parallel", "arbitrary"), comm=None if comm is _NO_COPIES else comm)
    return res if comm is _NO_COPIES else (res, got)


def _colsum_partials(p):
    return jnp.sum(p.reshape(-1, 8, p.shape[-1])[:, 0, :], axis=0, keepdims=True)


def _rmsnorm_fwd(name, x, g, tr=512):
    T, D = x.shape

    def body(x_ref, g_ref, u_ref, r_ref):
        xv = x_ref[...]
        r = lax.rsqrt(jnp.mean(xv * xv, axis=-1, keepdims=True) + EPS)
        u_ref[...] = (xv * r * g_ref[...]).astype(u_ref.dtype)
        r_ref[...] = r

    return pl.pallas_call(
        body, name=name, grid=(T // tr,),
        in_specs=[pl.BlockSpec((tr, D), lambda i: (i, 0)), pl.BlockSpec((1, D), lambda i: (0, 0))],
        out_specs=[pl.BlockSpec((tr, D), lambda i: (i, 0)), pl.BlockSpec((tr, 1), lambda i: (i, 0))],
        out_shape=[jax.ShapeDtypeStruct((T, D), MXU_DTYPE), jax.ShapeDtypeStruct((T, 1), F32)],
        compiler_params=_params(("parallel",)),
    )(x, g)


def _rmsnorm_bwd_vals(dy, xin, rstd, g):
    xhat = xin * rstd
    dg = jnp.sum(dy * xhat, axis=0, keepdims=True)
    dxh = dy * g
    dx = rstd * (dxh - xhat * jnp.mean(dxh * xhat, axis=-1, keepdims=True))
    return dx, dg


def _colsum(name, a, tr=512):
    T, N = a.shape

    def body(a_ref, o_ref):
        @pl.when(pl.program_id(0) == 0)
        def _():
            o_ref[...] = jnp.zeros_like(o_ref)

        o_ref[...] += jnp.sum(a_ref[...].astype(F32), axis=0, keepdims=True)

    return pl.pallas_call(
        body, name=name, grid=(T // tr,),
        in_specs=[pl.BlockSpec((tr, N), lambda i: (i, 0))],
        out_specs=pl.BlockSpec((1, N), lambda i: (0, 0)),
        out_shape=jax.ShapeDtypeStruct((1, N), F32),
        compiler_params=_params(("arbitrary",)),
    )(a)


ATTN_SCALE = 1.0 / math.sqrt(HEAD_DIM)
GROUP_LANES = GROUP * ATTN_BLOCK
PAIR = 2 * HEAD_DIM


def _attn_mask():
    kj = lax.broadcasted_iota(jnp.int32, (ATTN_BLOCK, GROUP_LANES), 0)
    qi = lax.broadcasted_iota(jnp.int32, (ATTN_BLOCK, GROUP_LANES), 1) & (ATTN_BLOCK - 1)
    return kj <= qi


def _heads_transposed(ref, g, scale=None):
    parts = []
    for a in range(GROUP // 2):
        lo = (g * GROUP // 2 + a) * PAIR
        pair = ref[:, lo:lo + PAIR]
        pair = (pair if scale is None else pair * scale).T
        parts += [pair[:HEAD_DIM], pair[HEAD_DIM:]]
    return jnp.concatenate(parts, axis=1).astype(MXU_DTYPE)


def _heads_back(ref, g, vt):
    for a in range(GROUP // 2):
        lo = (g * GROUP // 2 + a) * PAIR
        pair = jnp.concatenate([vt[:, (2 * a) * ATTN_BLOCK:(2 * a + 1) * ATTN_BLOCK],
                                vt[:, (2 * a + 1) * ATTN_BLOCK:(2 * a + 2) * ATTN_BLOCK]], axis=0)
        ref[:, lo:lo + PAIR] = pair.T.astype(ref.dtype)


def _kv_parts(kv_ref, g):
    ks = slice(g * HEAD_DIM, (g + 1) * HEAD_DIM)
    vs = slice(KV_WIDTH + g * HEAD_DIM, KV_WIDTH + (g + 1) * HEAD_DIM)
    return kv_ref[:, ks].astype(MXU_DTYPE), kv_ref[:, vs].astype(MXU_DTYPE)


def _sink_rows(sinks):
    return jnp.repeat(sinks.reshape(KV_HEADS, GROUP), ATTN_BLOCK, axis=1)


def _attn_fwd(pq, pkv, sinks, comm=None):
    T = pq.shape[0]
    nb = T // ATTN_BLOCK

    def body(q_ref, kvc_ref, kvp_ref, s_ref, y_ref, lse_ref):
        mask_c = _attn_mask()
        has_prev = pl.program_id(0) > 0
        for g in range(KV_HEADS):
            (kc, vc), (kp, vp) = _kv_parts(kvc_ref, g), _kv_parts(kvp_ref, g)
            qt = _heads_transposed(q_ref, g, ATTN_SCALE)
            s = jnp.where(mask_c, _nn(kc, qt), jnp.where(has_prev, _nn(kp, qt), NEG_INF))
            sink = s_ref[g:g + 1, :]
            m = jnp.maximum(jnp.max(s, axis=0, keepdims=True), sink)
            p = jnp.exp(s - m)
            den = jnp.sum(p, axis=0, keepdims=True) + jnp.exp(sink - m)
            pc = jnp.where(mask_c, p, 0.0)
            _heads_back(y_ref, g, (_tn(vc, pc) + _tn(vp, p - pc)) / den)
            lse = m + jnp.log(den)
            for i in range(GROUP):
                lse_ref[g * GROUP + i:g * GROUP + i + 1, :] = lse[:, i * ATTN_BLOCK:(i + 1) * ATTN_BLOCK]

    return _call(
        "attn_fwd", body, grid=(nb,),
        in_specs=[pl.BlockSpec((ATTN_BLOCK, D_MODEL), lambda n: (n, 0)),
                  pl.BlockSpec((ATTN_BLOCK, 2 * KV_WIDTH), lambda n: (n, 0)),
                  pl.BlockSpec((ATTN_BLOCK, 2 * KV_WIDTH), lambda n: (jnp.maximum(n - 1, 0), 0)),
                  pl.BlockSpec((KV_HEADS, GROUP_LANES), lambda n: (0, 0))],
        out_specs=[pl.BlockSpec((ATTN_BLOCK, D_MODEL), lambda n: (n, 0)),
                   pl.BlockSpec((Q_HEADS, ATTN_BLOCK), lambda n: (0, n))],
        out_shape=[jax.ShapeDtypeStruct((T, D_MODEL), MXU_DTYPE), jax.ShapeDtypeStruct((Q_HEADS, T), F32)],
        args=[pq, pkv, pkv, _sink_rows(sinks)], semantics=("parallel",), comm=comm)


def _attn_bwd(pq, pkv, sinks, lse, dy, comm=None):
    T = pq.shape[0]
    nb = T // ATTN_BLOCK
    cur = lambda n: (jnp.minimum(n, nb - 1), 0)

    def body(q_ref, kvc_ref, kvp_ref, s_ref, lse_ref, dy_ref, dq_ref, dkv_ref, ds_ref, carry, top, bot):
        n = pl.program_id(0)

        @pl.when(n == 0)
        def _():
            carry[...] = jnp.zeros_like(carry)
            ds_ref[...] = jnp.zeros_like(ds_ref)

        @pl.when(n < nb)
        def _():
            mask_c = _attn_mask()
            valid = jnp.logical_or(mask_c, n > 0)
            for g in range(KV_HEADS):
                ks = slice(g * HEAD_DIM, (g + 1) * HEAD_DIM)
                vs = slice(KV_WIDTH + g * HEAD_DIM, KV_WIDTH + (g + 1) * HEAD_DIM)
                (kc, vc), (kp, vp) = _kv_parts(kvc_ref, g), _kv_parts(kvp_ref, g)
                qt = _heads_transposed(q_ref, g, ATTN_SCALE)
                dot = _heads_transposed(dy_ref, g)
                lse = jnp.concatenate([lse_ref[g * GROUP + i:g * GROUP + i + 1, :] for i in range(GROUP)], axis=1)
                p = jnp.where(valid, jnp.exp(jnp.where(mask_c, _nn(kc, qt), _nn(kp, qt)) - lse), 0.0)
                dp = jnp.where(mask_c, _nn(vc, dot), _nn(vp, dot))
                delta = jnp.sum(p * dp, axis=0, keepdims=True)
                ds = p * (dp - delta)
                ds_c, p_c = jnp.where(mask_c, ds, 0.0), jnp.where(mask_c, p, 0.0)
                ds_p, p_p = ds - ds_c, p - p_c
                _heads_back(dq_ref, g, (_tn(kc, ds_c) + _tn(kp, ds_p)) * ATTN_SCALE)
                bot[:, ks], bot[:, vs] = _nt(ds_c, qt), _nt(p_c, dot)
                top[:, ks], top[:, vs] = _nt(ds_p, qt), _nt(p_p, dot)
                ds_ref[g:g + 1, :] -= jnp.exp(s_ref[g:g + 1, :] - lse) * delta
            dkv_ref[...] = (carry[...] + top[...]).astype(dkv_ref.dtype)
            carry[...] = bot[...]

        @pl.when(n == nb)
        def _():
            dkv_ref[...] = carry[...].astype(dkv_ref.dtype)

    return _call(
        "attn_bwd", body, grid=(nb + 1,),
        in_specs=[pl.BlockSpec((ATTN_BLOCK, D_MODEL), cur),
                  pl.BlockSpec((ATTN_BLOCK, 2 * KV_WIDTH), cur),
                  pl.BlockSpec((ATTN_BLOCK, 2 * KV_WIDTH), lambda n: (jnp.maximum(jnp.minimum(n, nb - 1) - 1, 0), 0)),
                  pl.BlockSpec((KV_HEADS, GROUP_LANES), lambda n: (0, 0)),
                  pl.BlockSpec((Q_HEADS, ATTN_BLOCK), lambda n: (0, jnp.minimum(n, nb - 1))),
                  pl.BlockSpec((ATTN_BLOCK, D_MODEL), cur)],
        out_specs=[pl.BlockSpec((ATTN_BLOCK, D_MODEL), cur),
                   pl.BlockSpec((ATTN_BLOCK, 2 * KV_WIDTH), lambda n: (jnp.maximum(n - 1, 0), 0)),
                   pl.BlockSpec((KV_HEADS, GROUP_LANES), lambda n: (0, 0))],
        out_shape=[jax.ShapeDtypeStruct((T, D_MODEL), MXU_DTYPE),
                   jax.ShapeDtypeStruct((T, 2 * KV_WIDTH), MXU_DTYPE),
                   jax.ShapeDtypeStruct((KV_HEADS, GROUP_LANES), F32)],
        scratch=[pltpu.VMEM((ATTN_BLOCK, 2 * KV_WIDTH), F32)] * 3,
        args=[pq, pkv, pkv, _sink_rows(sinks), lse, dy], semantics=("arbitrary",), comm=comm)


def _lower_bound(l):
    m = jnp.maximum(l[0:1], l[1:2])
    e0, e1 = jnp.exp(l[0:1] - m), jnp.exp(l[1:2] - m)
    return e0 / (e0 + e1)


def _tri(lower):
    r = lax.broadcasted_iota(jnp.int32, (CHUNK, CHUNK), 0)
    c = lax.broadcasted_iota(jnp.int32, (CHUNK, CHUNK), 1)
    return (r >= c) if lower else (c >= r)


def _chunk_sum(mask, v):
    return lax.dot_general(mask.astype(F32), v, (((1,), (0,)), ((), ())),
                           precision=lax.Precision.HIGHEST, preferred_element_type=F32)


def _hgrn_chunk_inputs(hq, hf, lb, causal):
    sg, sgn = _sigmoid(hf), _sigmoid(-hf)
    f = lb + (1.0 - lb) * sg
    kk = (1.0 - lb) * sgn
    sq = _sigmoid(hq)
    q = hq * sq
    b = _chunk_sum(causal, jnp.log(f))
    bm, bl = b[CHUNK // 2 - 1:CHUNK // 2, :], b[CHUNK - 1:CHUNK, :]
    e_qm, e_km, e_qs, e_kl = jnp.exp(b - bm), jnp.exp(bm - b), jnp.exp(b), jnp.exp(bl - b)
    return dict(sg=sg, sgn=sgn, f=f, kk=kk, sq=sq, q=q, e_qm=e_qm, e_km=e_km, e_qs=e_qs, e_kl=e_kl,
                qm=q * e_qm, km=kk * e_km, qs=q * e_qs, kl=kk * e_kl, el=jnp.exp(bl))


def _hgrn_fwd(ph, lb_logits, norm_g, comm=None):
    T = ph.shape[0]
    nblk, cpb = T // HGRN_TOKENS, HGRN_TOKENS // CHUNK
    col = lambda c: pl.BlockSpec((HGRN_TOKENS, D_MODEL), functools.partial(lambda i, c: (i, c), c=c))

    def body(hq_ref, hf_ref, hi_ref, hg_ref, l_ref, ng_ref, y_ref, o_ref, st_ref, s_ref):
        @pl.when(pl.program_id(0) == 0)
        def _():
            s_ref[...] = jnp.zeros_like(s_ref)

        lb = _lower_bound(l_ref[...])
        causal = _tri(True)
        for c in range(cpb):
            rows = slice(c * CHUNK, (c + 1) * CHUNK)
            t = _hgrn_chunk_inputs(hq_ref[rows, :], hf_ref[rows, :], lb, causal)
            qm, km, qs, kl = (t[n].astype(MXU_DTYPE) for n in ("qm", "km", "qs", "kl"))
            v = hi_ref[rows, :].astype(MXU_DTYPE)
            for h in range(HGRN_HEADS):
                ls = slice(h * HGRN_K, (h + 1) * HGRN_K)
                st = s_ref[h]
                st_ref[c, ls, :] = st
                a = jnp.where(causal, _nt(qm[:, ls], km[:, ls]), 0.0)
                o_ref[rows, ls] = _nn(a, v[:, ls]) + _nt(qs[:, ls], st)
                s_ref[h] = t["el"][:, ls] * st + _tn(v[:, ls], kl[:, ls])
        for h in range(HGRN_HEADS):
            ls = slice(h * HGRN_K, (h + 1) * HGRN_K)
            o = o_ref[:, ls]
            r = lax.rsqrt(jnp.mean(o * o, axis=-1, keepdims=True) + EPS)
            y_ref[:, ls] = (o * r * ng_ref[:, ls] * _sigmoid(hg_ref[:, ls])).astype(y_ref.dtype)

    return _call(
        "hgrn_fwd", body, grid=(nblk,),
        in_specs=[col(0), col(1), col(2), col(3),
                  pl.BlockSpec((2, D_MODEL), lambda i: (0, 0)), pl.BlockSpec((1, D_MODEL), lambda i: (0, 0))],
        out_specs=[pl.BlockSpec((HGRN_TOKENS, D_MODEL), lambda i: (i, 0)),
                   pl.BlockSpec((HGRN_TOKENS, D_MODEL), lambda i: (i, 0)),
                   pl.BlockSpec((cpb, D_MODEL, HGRN_K), lambda i: (i, 0, 0))],
        out_shape=[jax.ShapeDtypeStruct((T, D_MODEL), MXU_DTYPE), jax.ShapeDtypeStruct((T, D_MODEL), F32),
                   jax.ShapeDtypeStruct((T // CHUNK, D_MODEL, HGRN_K), F32)],
        scratch=[pltpu.VMEM((HGRN_HEADS, HGRN_K, HGRN_K), F32)],
        args=[ph, ph, ph, ph, lb_logits, norm_g], semantics=("arbitrary",), comm=comm)


def _hgrn_bwd(ph, o_raw, states, dy, lb_logits, norm_g, comm=None):
    T = ph.shape[0]
    nblk, cpb = T // HGRN_TOKENS, HGRN_TOKENS // CHUNK
    rev = lambda i: nblk - 1 - i
    col = lambda c: pl.BlockSpec((HGRN_TOKENS, D_MODEL), functools.partial(lambda i, c: (rev(i), c), c=c))
    tok = pl.BlockSpec((HGRN_TOKENS, D_MODEL), lambda i: (rev(i), 0))

    def body(hq_ref, hf_ref, hi_ref, hg_ref, o_ref, st_ref, dy_ref, l_ref, ng_ref,
             dph_ref, dng_ref, dl_ref, dst_ref, dlb_ref, do_s, dqm_s, dkm_s, dqs_s, dkl_s, dv_s, del_s):
        i = pl.program_id(0)

        @pl.when(i == 0)
        def _():
            dst_ref[...] = jnp.zeros_like(dst_ref)
            dlb_ref[...] = jnp.zeros_like(dlb_ref)
            dng_ref[...] = jnp.zeros_like(dng_ref)

        lb = _lower_bound(l_ref[...])
        causal, anti = _tri(True), _tri(False)
        row = lax.broadcasted_iota(jnp.int32, (CHUNK, D_MODEL), 0)
        for c in reversed(range(cpb)):
            rows = slice(c * CHUNK, (c + 1) * CHUNK)
            hq = hq_ref[rows, :]
            t = _hgrn_chunk_inputs(hq, hf_ref[rows, :], lb, causal)
            sgg = _sigmoid(hg_ref[rows, :])
            dyv = dy_ref[rows, :]
            for h in range(HGRN_HEADS):
                ls = slice(h * HGRN_K, (h + 1) * HGRN_K)
                o = o_ref[rows, ls]
                r = lax.rsqrt(jnp.mean(o * o, axis=-1, keepdims=True) + EPS)
                nrm = o * r
                g_h = sgg[:, ls]
                dph_ref[rows, 3 * D_MODEL + h * HGRN_K:3 * D_MODEL + (h + 1) * HGRN_K] = (
                    dyv[:, ls] * nrm * ng_ref[:, ls] * g_h * (1.0 - g_h)).astype(dph_ref.dtype)
                dyg = dyv[:, ls] * g_h
                dng_ref[:, ls] += jnp.sum(dyg * nrm, axis=0, keepdims=True)
                dn = dyg * ng_ref[:, ls]
                do_s[:, ls] = r * (dn - nrm * jnp.mean(dn * nrm, axis=-1, keepdims=True))
            qm, km, qs, kl = (t[n].astype(MXU_DTYPE) for n in ("qm", "km", "qs", "kl"))
            v = hi_ref[rows, :].astype(MXU_DTYPE)
            do = do_s[...].astype(MXU_DTYPE)
            for h in range(HGRN_HEADS):
                ls = slice(h * HGRN_K, (h + 1) * HGRN_K)
                st = st_ref[c, ls, :]
                dst = dst_ref[h]
                a = jnp.where(causal, _nt(qm[:, ls], km[:, ls]), 0.0)
                da = jnp.where(causal, _nt(do[:, ls], v[:, ls]), 0.0)
                dv_s[:, ls] = _tn(a, do[:, ls]) + _nt(kl[:, ls], dst)
                dkl_s[:, ls] = _nn(v[:, ls], dst)
                dqs_s[:, ls] = _nn(do[:, ls], st)
                del_s[:, ls] = jnp.sum(dst * st, axis=0, keepdims=True)
                dst_ref[h] = _tn(do[:, ls], qs[:, ls]) + t["el"][:, ls] * dst
                dqm_s[:, ls] = _nn(da, km[:, ls])
                dkm_s[:, ls] = _tn(da, qm[:, ls])
            dqm, dkm, dqs, dkl = dqm_s[...], dkm_s[...], dqs_s[...], dkl_s[...]
            dq = dqm * t["e_qm"] + dqs * t["e_qs"]
            dk = dkm * t["e_km"] + dkl * t["e_kl"]
            t_qm, t_km, t_kl = dqm * t["qm"], dkm * t["km"], dkl * t["kl"]
            db = t_qm - t_km + dqs * t["qs"] - t_kl
            db_mid = jnp.sum(t_km - t_qm, axis=0, keepdims=True)
            db_last = jnp.sum(t_kl, axis=0, keepdims=True) + del_s[...] * t["el"]
            db = db + jnp.where(row == CHUNK // 2 - 1, db_mid, 0.0) + jnp.where(row == CHUNK - 1, db_last, 0.0)
            dlogf = _chunk_sum(anti, db)
            sq, sg, sgn, f = t["sq"], t["sg"], t["sgn"], t["f"]
            dph_ref[rows, 0:D_MODEL] = (dq * (sq * (1.0 + hq * (1.0 - sq)))).astype(dph_ref.dtype)
            dph_ref[rows, D_MODEL:2 * D_MODEL] = (
                dlogf * (1.0 - lb) * sg * (1.0 - sg) / f - dk * (1.0 - lb) * sgn * (1.0 - sgn)).astype(dph_ref.dtype)
            dph_ref[rows, 2 * D_MODEL:3 * D_MODEL] = dv_s[...].astype(dph_ref.dtype)
            dlb_ref[...] += jnp.sum(dlogf * (1.0 - sg) / f - dk * sgn, axis=0, keepdims=True)

        @pl.when(i == nblk - 1)
        def _():
            dl0 = dlb_ref[...] * lb * (1.0 - lb)
            dl_ref[0:1, :] = dl0
            dl_ref[1:2, :] = -dl0

    wide = pltpu.VMEM((CHUNK, D_MODEL), F32)
    return _call(
        "hgrn_bwd", body, grid=(nblk,),
        in_specs=[col(0), col(1), col(2), col(3), tok,
                  pl.BlockSpec((cpb, D_MODEL, HGRN_K), lambda i: (rev(i), 0, 0)), tok,
                  pl.BlockSpec((2, D_MODEL), lambda i: (0, 0)), pl.BlockSpec((1, D_MODEL), lambda i: (0, 0))],
        out_specs=[pl.BlockSpec((HGRN_TOKENS, 4 * D_MODEL), lambda i: (rev(i), 0)),
                   pl.BlockSpec((1, D_MODEL), lambda i: (0, 0)), pl.BlockSpec((2, D_MODEL), lambda i: (0, 0))],
        out_shape=[jax.ShapeDtypeStruct((T, 4 * D_MODEL), MXU_DTYPE), jax.ShapeDtypeStruct((1, D_MODEL), F32),
                   jax.ShapeDtypeStruct((2, D_MODEL), F32)],
        scratch=[pltpu.VMEM((HGRN_HEADS, HGRN_K, HGRN_K), F32), pltpu.VMEM((1, D_MODEL), F32),
                 wide, wide, wide, wide, wide, wide, pltpu.VMEM((1, D_MODEL), F32)],
        args=[ph, ph, ph, ph, o_raw, states, dy, lb_logits, norm_g], semantics=("arbitrary",), comm=comm)


def _local_step(x, target, vec, net):
    T, D = x.shape
    norm_mix_g, b_in, sinks, lb_logits = vec["norm_mix_g"], vec["b_in"], vec["attn_sinks"], vec["hgrn_lb_logits"]
    hgrn_norm_g, norm_ffn_g, norm_final_g = vec["hgrn_norm_g"], vec["norm_ffn_g"], vec["norm_final_g"]
    w_in = net.full("w_in")
    o_q, o_kv, o_h, o_g = (sum(IN_SPLITS[:i]) for i in range(4))
    w_q, w_kv, w_h, w_g = (w_in[o:o + n] for o, n in zip((o_q, o_kv, o_h, o_g), IN_SPLITS))
    b_q, b_kv, b_h, b_g = (b_in[:, o:o + n] for o, n in zip((o_q, o_kv, o_h, o_g), IN_SPLITS))
    bias = lambda acc, b: (acc + b,)
    both = lambda acc: (acc, acc)
    grad_outs = [("mn", F32), ("mn", MXU_DTYPE)]
    row_vec = lambda n: ((1, n), lambda i, j: (0, j))
    tile = lambda tm, tn: ((tm, tn), lambda i, j: (i, j))
    TM = 512

    u, rstd1 = _rmsnorm_fwd("norm_mix", x, norm_mix_g)
    pq, = _matmul("in_q", [(u, w_q)], "nt", tm=TM, tn=1024, tk=1024, outs=[("mn", F32)],
                  extras=[(b_q, *row_vec(1024))], epilogue=bias)
    pkv, = _matmul("in_kv", [(u, w_kv)], "nt", tm=TM, tn=256, tk=1024, outs=[("mn", F32)],
                   extras=[(b_kv, *row_vec(256))], epilogue=bias)
    names = ("w_branch_attn", "w_branch_hgrn", "w_out")
    (ph,), got = _matmul("in_h", [(u, w_h)], "nt", tm=TM, tn=1024, tk=1024, outs=[("mn", F32)],
                         extras=[(b_h, *row_vec(1024))], epilogue=bias, comm=net.gather(names))
    net.gathered(names, got)
    pg, = _matmul("in_g", [(u, w_g)], "nt", tm=TM, tn=1024, tk=1024, outs=[("mn", F32)],
                  extras=[(b_g, *row_vec(1024))], epilogue=bias)
    names = ("w_ffn_gate", "w_ffn_up")
    (y_attn, lse), got = _attn_fwd(pq, pkv, sinks, comm=net.gather(names))
    net.gathered(names, got)
    names = ("w_ffn_down",)
    (y_hgrn, o_raw, states), got = _hgrn_fwd(ph, lb_logits, hgrn_norm_g, comm=net.gather(names))
    net.gathered(names, got)
    w_ba, w_bh, w_out = net.full("w_branch_attn"), net.full("w_branch_hgrn"), net.full("w_out")
    w_gate, w_up, w_down = net.full("w_ffn_gate"), net.full("w_ffn_up"), net.full("w_ffn_down")
    ya, = _matmul("branch_a", [(y_attn, w_ba)], "nn", tm=TM, tn=1024, tk=1024, outs=[("mn", F32)])
    gate_a = (pg, (TM, 1024), lambda i, j: (i, 0))
    gate_b = (pg, (TM, 1024), lambda i, j: (i, 1))

    def merge(acc, ya_v, ga, gb):
        return acc, _sigmoid(ga) * ya_v + _sigmoid(gb) * acc

    yb, merged = _matmul("branch_b", [(y_hgrn, w_bh)], "nn", tm=TM, tn=1024, tk=1024,
                         outs=[("mn", F32), ("mn", MXU_DTYPE)],
                         extras=[(ya, *tile(TM, 1024)), gate_a, gate_b], epilogue=merge)

    def resid_norm(acc, xin, g):
        h = xin + acc
        r = lax.rsqrt(jnp.mean(h * h, axis=-1, keepdims=True) + EPS)
        return h, h * r * g, r

    h1, u2, rstd2 = _matmul("out_proj", [(merged, w_out)], "nn", tm=TM, tn=1024, tk=1024,
                            outs=[("mn", F32), ("mn", MXU_DTYPE), ("m", F32)],
                            extras=[(x, *tile(TM, 1024)), (norm_ffn_g, *row_vec(1024))], epilogue=resid_norm)
    FT = FFN // 2
    gpre, = _matmul("ffn_gate", [(u2, w_gate)], "nt", tm=TM, tn=FT, tk=1024, outs=[("mn", F32)])

    def swiglu(acc, gv):
        return acc, gv * _sigmoid(gv) * acc

    up, z = _matmul("ffn_up", [(u2, w_up)], "nt", tm=TM, tn=FT, tk=1024, outs=[("mn", F32), ("mn", MXU_DTYPE)],
                    extras=[(gpre, *tile(TM, FT))], epilogue=swiglu)

    def loss_head(acc, h1_v, tgt, g):
        h2 = h1_v + acc
        r = lax.rsqrt(jnp.mean(h2 * h2, axis=-1, keepdims=True) + EPS)
        xhat = h2 * r
        err = xhat * g - tgt
        part = 0.5 * jnp.sum(jnp.sum(err * err, axis=-1, keepdims=True), axis=0, keepdims=True) / D
        dyv = err / D
        dxh = dyv * g
        dh2 = r * (dxh - xhat * jnp.mean(dxh * xhat, axis=-1, keepdims=True))
        return dh2, dh2, jnp.sum(dyv * xhat, axis=0, keepdims=True), jnp.broadcast_to(part, (1, D))

    dh2, dh2b, dgf_p, loss_p = _matmul(
        "ffn_down", [(z, w_down)], "nn", tm=TM, tn=1024, tk=FFN,
        outs=[("mn", F32), ("mn", MXU_DTYPE), ("pn", F32), ("pn", F32)],
        extras=[(h1, *tile(TM, 1024)), (target, *tile(TM, 1024)), (norm_final_g, *row_vec(1024))],
        epilogue=loss_head)
    loss = jnp.sum(loss_p.reshape(-1, 8, D)[:, 0, 0])
    d_norm_final = _colsum_partials(dgf_p)

    def swiglu_bwd(acc, gv, upv):
        s = _sigmoid(gv)
        return acc * upv * (s * (1.0 + gv * (1.0 - s))), acc * (gv * s)

    dgp, dup = _matmul("d_ffn_hidden", [(dh2b, w_down)], "nt", tm=TM, tn=FT, tk=1024,
                       outs=[("mn", MXU_DTYPE), ("mn", MXU_DTYPE)],
                       extras=[(gpre, *tile(TM, FT)), (up, *tile(TM, FT))], epilogue=swiglu_bwd)
    d_w_down = _matmul("dw_down", [(z, dh2b)], "tn", tm=FT, tn=1024, tk=512, outs=grad_outs, epilogue=both)

    def norm_ffn_bwd(acc, h1_v, r, g, dres):
        dx, dg = _rmsnorm_bwd_vals(acc, h1_v, r, g)
        dh = dres + dx
        return dh, dh, dg

    names = ("w_ffn_down",)
    (dh1, dh1b, dg2_p), got = _matmul(
        "d_ffn_in", [(dgp, w_gate), (dup, w_up)], "nn", tm=TM, tn=1024, tk=FT,
        outs=[("mn", F32), ("mn", MXU_DTYPE), ("pn", F32)],
        extras=[(h1, *tile(TM, 1024)), (rstd2, (TM, 1), lambda i, j: (i, 0)),
                (norm_ffn_g, *row_vec(1024)), (dh2, *tile(TM, 1024))], epilogue=norm_ffn_bwd,
        comm=net.exchange(dict(w_ffn_down=[d_w_down])))
    net.received(names, got)
    d_norm_ffn = _colsum_partials(dg2_p)
    d_w_gate = _matmul("dw_gate", [(dgp, u2)], "tn", tm=FT, tn=1024, tk=512, outs=grad_outs, epilogue=both)
    d_w_up = _matmul("dw_up", [(dup, u2)], "tn", tm=FT, tn=1024, tk=512, outs=grad_outs, epilogue=both)

    def merge_bwd(acc, ya_v, yb_v, ga, gb):
        sa, sb = _sigmoid(ga), _sigmoid(gb)
        return acc * sa, acc * sb, acc * ya_v * sa * (1.0 - sa), acc * yb_v * sb * (1.0 - sb)

    dya, dyb, dga, dgb = _matmul(
        "d_merged", [(dh1b, w_out)], "nt", tm=TM, tn=1024, tk=1024, outs=[("mn", MXU_DTYPE)] * 4,
        extras=[(ya, *tile(TM, 1024)), (yb, *tile(TM, 1024)), gate_a, gate_b], epilogue=merge_bwd)
    tn_grad = functools.partial(_matmul, mode="tn", tn=1024, tk=512, outs=grad_outs, epilogue=both)
    d_w_out = tn_grad("dw_out", [(merged, dh1b)], tm=1024)
    dy_attn, = _matmul("d_y_attn", [(dya, w_ba)], "nt", tm=TM, tn=1024, tk=1024, outs=[("mn", F32)])
    dy_hgrn, = _matmul("d_y_hgrn", [(dyb, w_bh)], "nt", tm=TM, tn=1024, tk=1024, outs=[("mn", F32)])
    d_w_ba = tn_grad("dw_branch_a", [(y_attn, dya)], tm=1024)
    d_w_bh = tn_grad("dw_branch_b", [(y_hgrn, dyb)], tm=1024)

    names = ("w_ffn_gate", "w_ffn_up")
    (dq, dkv, dsink), got = _attn_bwd(pq, pkv, sinks, lse, dy_attn,
                                      comm=net.exchange(dict(w_ffn_gate=[d_w_gate], w_ffn_up=[d_w_up])))
    net.received(names, got)
    names = ("w_out", "w_branch_attn", "w_branch_hgrn")
    (dph, d_hgrn_norm, d_lb_logits), got = _hgrn_bwd(
        ph, o_raw, states, dy_hgrn, lb_logits, hgrn_norm_g,
        comm=net.exchange(dict(w_out=[d_w_out], w_branch_attn=[d_w_ba], w_branch_hgrn=[d_w_bh])))
    net.received(names, got)

    d_w_in = [tn_grad("dw_in_q", [(dq, u)], tm=1024), tn_grad("dw_in_kv", [(dkv, u)], tm=256),
              tn_grad("dw_in_h", [(dph, u)], tm=1024), tn_grad("dw_in_ga", [(dga, u)], tm=1024),
              tn_grad("dw_in_gb", [(dgb, u)], tm=1024)]

    def norm_mix_bwd(acc, xin, r, g, dres):
        dx, dg = _rmsnorm_bwd_vals(acc, xin, r, g)
        return dres + dx, dg

    names = ("w_in",)
    (dx, dg1_p), got = _matmul(
        "d_u", [(dq, w_q), (dkv, w_kv), (dph, w_h), (dga, w_g[:D]), (dgb, w_g[D:])], "nn",
        tm=TM, tn=1024, tk=512, outs=[("mn", F32), ("pn", F32)],
        extras=[(x, *tile(TM, 1024)), (rstd1, (TM, 1), lambda i, j: (i, 0)),
                (norm_mix_g, *row_vec(1024)), (dh1, *tile(TM, 1024))], epilogue=norm_mix_bwd,
        comm=net.exchange(dict(w_in=d_w_in)))
    net.received(names, got)
    d_norm_mix = _colsum_partials(dg1_p)
    d_b_in = jnp.concatenate([_colsum("db_q", dq), _colsum("db_kv", dkv), _colsum("db_h", dph),
                              _colsum("db_ga", dga), _colsum("db_gb", dgb)], axis=1)
    vecs = dict(norm_mix_g=d_norm_mix, b_in=d_b_in, attn_sinks=jnp.sum(dsink.reshape(Q_HEADS, ATTN_BLOCK), axis=1).reshape(1, Q_HEADS),
                hgrn_lb_logits=d_lb_logits,
                hgrn_norm_g=d_hgrn_norm, norm_ffn_g=d_norm_ffn, norm_final_g=d_norm_final)
    return loss, dx, vecs


def _place():
    return lax.axis_index("x"), lax.axis_index("y"), lax.axis_index("c")


def _other_chips(x, y):
    return [(1 - x, y), (x, 1 - y), (1 - x, 1 - y)]


def _gather_copies(shards):
    n = len(shards)

    def build(ins, outs, send_sems, recv_sems, local_sems):
        x, y, c = _place()
        mine = 2 * x + y
        local = [pltpu.make_async_copy(ins[w], outs[w].at[mine], local_sems.at[w]) for w in range(n)]
        sends, recvs = [], []
        for w in range(n):
            for k, (px, py) in enumerate(_other_chips(x, y)):
                sem = 3 * w + k
                sends.append(pltpu.make_async_remote_copy(
                    src_ref=ins[w], dst_ref=outs[w].at[mine], send_sem=send_sems.at[sem], recv_sem=recv_sems.at[sem],
                    device_id=(px, py, c), device_id_type=MESH_ID))
                recvs.append(pltpu.make_async_remote_copy(
                    src_ref=ins[w], dst_ref=outs[w].at[2 * px + py], send_sem=send_sems.at[sem],
                    recv_sem=recv_sems.at[sem], device_id=(px, py, c), device_id_type=MESH_ID))
        return sends, recvs, local

    return _Carried(shards, [jax.ShapeDtypeStruct((N_CHIPS,) + s.shape, s.dtype) for s in shards], 3 * n, n, build)


def _grad_copies(stacked):
    n = len(stacked)

    def build(ins, outs, send_sems, recv_sems, local_sems):
        x, y, c = _place()
        sends = []
        for w in range(n):
            for k, (px, py) in enumerate(_other_chips(x, y)):
                sem = 3 * w + k
                sends.append(pltpu.make_async_remote_copy(
                    src_ref=ins[w].at[2 * px + py], dst_ref=outs[w].at[k], send_sem=send_sems.at[sem],
                    recv_sem=recv_sems.at[sem], device_id=(px, py, c), device_id_type=MESH_ID))
        return sends, sends, []

    return _Carried(stacked, [jax.ShapeDtypeStruct((3,) + s.shape[1:], s.dtype) for s in stacked], 3 * n, 0, build)


def _small_copies(small):
    def build(ins, outs, send_sems, recv_sems, local_sems):
        small_ref, all_ref = ins[0], outs[0]
        x, y, c = _place()
        me = 4 * x + 2 * y + c
        sends, recvs = [], []
        for r in range(1, 8):
            px = 1 - x if r & 4 else x
            py = 1 - y if r & 2 else y
            pc = 1 - c if r & 1 else c
            sends.append(pltpu.make_async_remote_copy(
                src_ref=small_ref, dst_ref=all_ref.at[me], send_sem=send_sems.at[r - 1], recv_sem=recv_sems.at[r - 1],
                device_id=(px, py, pc), device_id_type=MESH_ID))
            recvs.append(pltpu.make_async_remote_copy(
                src_ref=small_ref, dst_ref=all_ref.at[4 * px + 2 * py + pc], send_sem=send_sems.at[r - 1],
                recv_sem=recv_sems.at[r - 1], device_id=(px, py, pc), device_id_type=MESH_ID))
        return sends, recvs, [pltpu.make_async_copy(small_ref, all_ref.at[me], local_sems.at[0])]

    return _Carried([small], [jax.ShapeDtypeStruct((8,) + small.shape, small.dtype)], 7, 1, build)


def _copies_alone(name, comm):
    return _call(name, lambda: None, grid=(), in_specs=[], out_specs=[], out_shape=[], args=[], comm=comm)[1]


class _Net:
    def __init__(self, shards):
        self.shards = shards
        self.whole, self.own, self.theirs = {}, {}, {}
        x, y, _ = _place()
        self.chip = 2 * x + y

    def gather(self, names):
        return _gather_copies([self.shards[n] for n in names])

    def gathered(self, names, got):
        for n, g in zip(names, got):
            self.whole[n] = g.reshape(-1, g.shape[-1])

    def full(self, name):
        return self.whole[name]

    def exchange(self, grads):
        stacked = []
        for n, pieces in grads.items():
            keep = jnp.concatenate([p[0] for p in pieces], axis=0) if len(pieces) > 1 else pieces[0][0]
            send = jnp.concatenate([p[1] for p in pieces], axis=0) if len(pieces) > 1 else pieces[0][1]
            rows = keep.shape[0] // N_CHIPS
            self.own[n] = lax.dynamic_slice_in_dim(keep, self.chip * rows, rows, axis=0)
            stacked.append(send.reshape(N_CHIPS, rows, send.shape[-1]))
        return _grad_copies(stacked)

    def received(self, names, got):
        self.theirs.update(zip(names, got))


def _exchange_sibling(parts):
    n = len(parts)

    def body(*refs):
        ins, outs = refs[:n], refs[n:2 * n]
        send_sems, recv_sems = refs[2 * n:]
        x, y, c = _place()
        copies = [pltpu.make_async_remote_copy(
            src_ref=ins[w], dst_ref=outs[w], send_sem=send_sems.at[w], recv_sem=recv_sems.at[w],
            device_id=(x, y, 1 - c), device_id_type=MESH_ID) for w in range(n)]
        for cp in copies:
            cp.start()
        for cp in copies:
            cp.wait_recv()
        for cp in copies:
            cp.wait_send()

    return pl.pallas_call(
        body, name="exchange_sibling",
        in_specs=[HBM_SPEC] * n, out_specs=[HBM_SPEC] * n,
        out_shape=[jax.ShapeDtypeStruct(p.shape, p.dtype) for p in parts],
        scratch_shapes=[pltpu.SemaphoreType.DMA((n,)), pltpu.SemaphoreType.DMA((n,))],
        compiler_params=pltpu.CompilerParams(has_side_effects=True),
    )(*parts)


def _row_tile(rows, most=512, sublanes=16):
    return max(t for t in range(sublanes, most + 1, sublanes) if rows % t == 0)


def _partial_sum(name, own, recv):
    R, C = own.shape
    tr = _row_tile(R)

    def body(o_ref, r_ref, p_ref):
        p_ref[...] = ((o_ref[...] + r_ref[0].astype(F32)) + r_ref[1].astype(F32)) + r_ref[2].astype(F32)

    return pl.pallas_call(
        body, name=name, grid=(R // tr,),
        in_specs=[pl.BlockSpec((tr, C), lambda i: (i, 0)), pl.BlockSpec((3, tr, C), lambda i: (0, i, 0))],
        out_specs=pl.BlockSpec((tr, C), lambda i: (i, 0)),
        out_shape=jax.ShapeDtypeStruct((R, C), F32),
        compiler_params=_params(("parallel",)),
    )(own, recv)


def _adam_vals(w, g, m, v):
    m = ADAM_B1 * m + (1.0 - ADAM_B1) * g
    v = ADAM_B2 * v + (1.0 - ADAM_B2) * (g * g)
    m_hat = m / (1.0 - ADAM_B1 ** ADAM_STEP)
    v_hat = v / (1.0 - ADAM_B2 ** ADAM_STEP)
    delta = -ADAM_LR * (m_hat / (jnp.sqrt(v_hat) + ADAM_EPS) + ADAM_WD * w)
    return delta, m, v


def _adamw(name, w, m, v, p_south, p_north):
    R, C = w.shape
    tr = _row_tile(R)

    def body(w_ref, m_ref, v_ref, s_ref, n_ref, g_ref, d_ref, nm_ref, nv_ref):
        g = s_ref[...] + n_ref[...]
        d, nm, nv = _adam_vals(w_ref[...], g, m_ref[...], v_ref[...])
        g_ref[...], d_ref[...], nm_ref[...], nv_ref[...] = g, d, nm, nv

    spec = pl.BlockSpec((tr, C), lambda i: (i, 0))
    return pl.pallas_call(
        body, name=name, grid=(R // tr,), in_specs=[spec] * 5, out_specs=[spec] * 4,
        out_shape=[jax.ShapeDtypeStruct((R, C), F32)] * 4,
        compiler_params=_params(("parallel",)),
    )(w, m, v, p_south, p_north)


def _adamw_small(w, m, v, g_all):
    def body(w_ref, m_ref, v_ref, a_ref, g_ref, d_ref, nm_ref, nv_ref):
        g = a_ref[0]
        for dev in range(1, 8):
            g = g + a_ref[dev]
        d, nm, nv = _adam_vals(w_ref[...], g, m_ref[...], v_ref[...])
        g_ref[...], d_ref[...], nm_ref[...], nv_ref[...] = g, d, nm, nv

    return pl.pallas_call(
        body, name="adamw_small", out_shape=[jax.ShapeDtypeStruct(w.shape, F32)] * 4,
    )(w, m, v, g_all)


SMALL_LAYOUT = dict(norm_mix_g=(0, 1024), b_in=(8, 7424), hgrn_norm_g=(16, 1024), norm_ffn_g=(24, 1024),
                    norm_final_g=(32, 1024), hgrn_lb_logits=(40, 2048), attn_sinks=(48, 16))


def _pack_small(vals):
    parts = []
    for name, (_, nvalid) in SMALL_LAYOUT.items():
        flat = vals[name].astype(F32).reshape(-1)
        parts.append(jnp.pad(flat, (0, 8 * D_MODEL - nvalid)).reshape(8, D_MODEL))
    return jnp.concatenate(parts, axis=0)


def _unpack_small(packed, shapes):
    return {name: packed[r0:r0 + 8].reshape(-1)[:nvalid].reshape(shapes[name])
            for name, (r0, nvalid) in SMALL_LAYOUT.items()}


MATRICES = ("w_in", "w_branch_attn", "w_branch_hgrn", "w_out", "w_ffn_gate", "w_ffn_up", "w_ffn_down")
COLUMN_SHARDED = ("w_in", "w_ffn_gate", "w_ffn_up")
WEIGHTS = ("norm_mix_g", "w_in", "b_in", "attn_sinks", "hgrn_lb_logits", "hgrn_norm_g", "w_branch_attn",
           "w_branch_hgrn", "w_out", "norm_ffn_g", "w_ffn_gate", "w_ffn_up", "w_ffn_down", "norm_final_g")


def kernel(x, norm_mix_g, w_in, b_in, attn_sinks, hgrn_lb_logits, hgrn_norm_g, w_branch_attn, w_branch_hgrn, w_out, norm_ffn_g, w_ffn_gate, w_ffn_up, w_ffn_down, norm_final_g, loss_target, m_norm_mix_g, m_w_in, m_b_in, m_attn_sinks, m_hgrn_lb_logits, m_hgrn_norm_g, m_w_branch_attn, m_w_branch_hgrn, m_w_out, m_norm_ffn_g, m_w_ffn_gate, m_w_ffn_up, m_w_ffn_down, m_norm_final_g, v_norm_mix_g, v_w_in, v_b_in, v_attn_sinks, v_hgrn_lb_logits, v_hgrn_norm_g, v_w_branch_attn, v_w_branch_hgrn, v_w_out, v_norm_ffn_g, v_w_ffn_gate, v_w_ffn_up, v_w_ffn_down, v_norm_final_g):
    given = dict(locals())
    w = {n: given[n] for n in WEIGHTS}
    m = {n: given["m_" + n] for n in WEIGHTS}
    v = {n: given["v_" + n] for n in WEIGHTS}

    block = lambda a, n: jnp.transpose(a[0]) if n in COLUMN_SHARDED else a[0]
    unblock = lambda a, n: (jnp.transpose(a) if n in COLUMN_SHARDED else a)[None]
    net = _Net({n: block(w[n], n).astype(MXU_DTYPE) for n in MATRICES})
    net.gathered(("w_in",), _copies_alone("gather_w_in", net.gather(("w_in",))))
    vec = dict(norm_mix_g=norm_mix_g, b_in=b_in, attn_sinks=attn_sinks, hgrn_lb_logits=hgrn_lb_logits,
               hgrn_norm_g=hgrn_norm_g, norm_ffn_g=norm_ffn_g, norm_final_g=norm_final_g.reshape(1, D_MODEL))
    loss_part, dx, d_vecs = _local_step(x[0], loss_target[0], vec, net)

    small_all, = _copies_alone("exchange_small", _small_copies(_pack_small(d_vecs)))
    sums = [_partial_sum("sum_" + n, net.own[n], net.theirs[n]) for n in MATRICES]
    theirs = _exchange_sibling(sums)

    grads, deltas, new_m, new_v = {}, {}, {}, {}
    for n, mine, other in zip(MATRICES, sums, theirs):
        res = _adamw("adamw_" + n, block(w[n], n), block(m[n], n), block(v[n], n), mine, other)
        grads[n], deltas[n], new_m[n], new_v[n] = (unblock(r, n) for r in res)
    shapes = {n: w[n].shape for n in SMALL_LAYOUT}
    res = _adamw_small(_pack_small(w), _pack_small(m), _pack_small(v), small_all)
    for dst, packed in zip((grads, deltas, new_m, new_v), res):
        dst.update(_unpack_small(packed, shapes))

    loss = lax.psum(loss_part, ("x", "y", "c"))
    return (loss, dx[None], *[grads[n] for n in WEIGHTS], *[deltas[n] for n in WEIGHTS],
            *[new_m[n] for n in WEIGHTS], *[new_v[n] for n in WEIGHTS])
```

```python
import functools
import math

import jax
import jax.numpy as jnp
from jax import lax
from jax.experimental import pallas as pl
from jax.experimental.pallas import tpu as pltpu

F32 = jnp.float32
BF16 = jnp.bfloat16
MXU_DTYPE = jnp.bfloat16
MESH_ID = pl.DeviceIdType.MESH

D_MODEL = 1024
HEAD_DIM = 64
Q_HEADS = 16
KV_HEADS = 2
GROUP = Q_HEADS // KV_HEADS
KV_WIDTH = KV_HEADS * HEAD_DIM
ATTN_BLOCK = 128
HGRN_HEADS = 8
HGRN_K = 128
CHUNK = 64
HGRN_TOKENS = 256
FFN = 2816
IN_SPLITS = (1024, 256, 4096, 2048)
EPS = 1e-6
NEG_INF = -1e30
ADAM_LR, ADAM_B1, ADAM_B2, ADAM_EPS, ADAM_WD, ADAM_STEP = 0.001, 0.9, 0.999, 1e-08, 0.01, 10
N_CHIPS = 4
VMEM_LIMIT = 56 * 1024 * 1024


def _params(sem=None):
    return pltpu.CompilerParams(dimension_semantics=sem, vmem_limit_bytes=VMEM_LIMIT)


def _sigmoid(v):
    return 0.5 * jnp.tanh(0.5 * v) + 0.5


def _dot(a, b, dims):
    return lax.dot_general(a.astype(MXU_DTYPE), b.astype(MXU_DTYPE), (dims, ((), ())),
                           preferred_element_type=F32)


def _nn(a, b):
    return _dot(a, b, ((1,), (0,)))


def _nt(a, b):
    return _dot(a, b, ((1,), (1,)))


def _tn(a, b):
    return _dot(a, b, ((0,), (0,)))


HBM_SPEC = pl.BlockSpec(memory_space=pl.ANY)


class _Carried:
    def __init__(self, arrays, out_shapes, n_remote, n_local, build):
        self.arrays, self.out_shapes, self.build = list(arrays), list(out_shapes), build
        self.scratch = [pltpu.SemaphoreType.DMA((n_remote,)), pltpu.SemaphoreType.DMA((n_remote,)),
                        pltpu.SemaphoreType.DMA((max(n_local, 1),))]

    def start(self, ins, outs, sems):
        sends, _, local = self.build(ins, outs, *sems)
        for cp in local + sends:
            cp.start()

    def wait(self, ins, outs, sems):
        sends, recvs, local = self.build(ins, outs, *sems)
        for cp in recvs:
            cp.wait_recv()
        for cp in sends:
            cp.wait_send()
        for cp in local:
            cp.wait()


def _call(name, body, *, grid, in_specs, out_specs, out_shape, args, scratch=(), semantics=None, comm=None):
    n_in, n_out, n_scr = len(in_specs), len(out_specs), len(scratch)
    if comm is None:
        res = pl.pallas_call(body, name=name, grid=grid, in_specs=in_specs, out_specs=out_specs, out_shape=out_shape,
                             scratch_shapes=list(scratch), compiler_params=_params(semantics))(*args)
        return list(res), []
    ci, co = len(comm.arrays), len(comm.out_shapes)

    def carrying(*refs):
        ins, refs = refs[:n_in], refs[n_in:]
        c_ins, refs = refs[:ci], refs[ci:]
        outs, refs = refs[:n_out], refs[n_out:]
        c_outs, refs = refs[:co], refs[co:]
        scr, sems = refs[:n_scr], refs[n_scr:]
        if not grid:
            comm.start(c_ins, c_outs, sems)
            body(*ins, *outs, *scr)
            comm.wait(c_ins, c_outs, sems)
            return
        first = functools.reduce(jnp.logical_and, [pl.program_id(a) == 0 for a in range(len(grid))])
        last = functools.reduce(jnp.logical_and, [pl.program_id(a) == g - 1 for a, g in enumerate(grid)])

        @pl.when(first)
        def _():
            comm.start(c_ins, c_outs, sems)

        body(*ins, *outs, *scr)

        @pl.when(last)
        def _():
            comm.wait(c_ins, c_outs, sems)

    res = pl.pallas_call(
        carrying, name=name, grid=grid, in_specs=list(in_specs) + [HBM_SPEC] * ci,
        out_specs=list(out_specs) + [HBM_SPEC] * co, out_shape=list(out_shape) + comm.out_shapes,
        scratch_shapes=list(scratch) + comm.scratch,
        compiler_params=_params(("arbitrary",) * len(grid) if grid else None),
    )(*args, *comm.arrays)
    return list(res[:n_out]), list(res[n_out:])


_NO_COPIES = object()


def _matmul(name, pairs, mode, *, tm, tn, tk, outs, extras=(), epilogue=None, comm=_NO_COPIES):
    prod = dict(nn=_nn, nt=_nt, tn=_tn)[mode]
    a0, b0 = pairs[0]
    M = a0.shape[1] if mode == "tn" else a0.shape[0]
    N = b0.shape[0] if mode == "nt" else b0.shape[1]
    steps, in_specs, offset = [], [], 0
    for a, b in pairs:
        K = a.shape[0] if mode == "tn" else a.shape[1]
        t = min(tk, K)
        assert K % t == 0, (name, K, t)
        kmap = functools.partial(lambda k, off, n: jnp.clip(k - off, 0, n - 1), off=offset, n=K // t)
        if mode == "tn":
            in_specs.append(pl.BlockSpec((t, tm), functools.partial(lambda i, j, k, f: (f(k), i), f=kmap)))
        else:
            in_specs.append(pl.BlockSpec((tm, t), functools.partial(lambda i, j, k, f: (i, f(k)), f=kmap)))
        if mode == "nt":
            in_specs.append(pl.BlockSpec((tn, t), functools.partial(lambda i, j, k, f: (j, f(k)), f=kmap)))
        else:
            in_specs.append(pl.BlockSpec((t, tn), functools.partial(lambda i, j, k, f: (f(k), j), f=kmap)))
        steps.append((offset, offset + K // t))
        offset += K // t
    assert M % tm == 0 and N % tn == 0, (name, M, N, tm, tn)
    ni, nj, nk = M // tm, N // tn, offset
    npair, ne, no = len(pairs), len(extras), len(outs)
    if epilogue is None:
        epilogue = lambda acc: (acc,)

    def finish(acc, extra_refs, out_refs):
        vals = epilogue(acc, *[r[...] for r in extra_refs])
        for (kind, _), o_ref, val in zip(outs, out_refs, vals):
            if kind == "pn":
                val = jnp.broadcast_to(val, o_ref.shape)
            o_ref[...] = val.astype(o_ref.dtype)

    def body(*refs):
        ab, rest = refs[:2 * npair], refs[2 * npair:]
        extra_refs, out_refs = rest[:ne], rest[ne:ne + no]
        if nk == 1:
            finish(prod(ab[0][...], ab[1][...]), extra_refs, out_refs)
            return
        acc_ref = rest[ne + no]
        k = pl.program_id(2)

        @pl.when(k == 0)
        def _():
            acc_ref[...] = jnp.zeros_like(acc_ref)

        for p, (lo, hi) in enumerate(steps):
            @pl.when(jnp.logical_and(k >= lo, k < hi))
            def _():
                acc_ref[...] += prod(ab[2 * p][...], ab[2 * p + 1][...])

        @pl.when(k == nk - 1)
        def _():
            finish(acc_ref[...], extra_refs, out_refs)

    for _, shape, im in extras:
        in_specs.append(pl.BlockSpec(shape, functools.partial(lambda i, j, k, im: im(i, j), im=im)))
    out_shape, out_specs = [], []
    for kind, dt in outs:
        if kind == "mn":
            out_shape.append(jax.ShapeDtypeStruct((M, N), dt))
            out_specs.append(pl.BlockSpec((tm, tn), lambda i, j, k: (i, j)))
        elif kind == "m":
            assert nj == 1
            out_shape.append(jax.ShapeDtypeStruct((M, 1), dt))
            out_specs.append(pl.BlockSpec((tm, 1), lambda i, j, k: (i, 0)))
        else:
            out_shape.append(jax.ShapeDtypeStruct((8 * ni, N), dt))
            out_specs.append(pl.BlockSpec((8, tn), lambda i, j, k: (i, j)))
    res, got = _call(name, body, grid=(ni, nj, nk), in_specs=in_specs, out_specs=out_specs, out_shape=out_shape,
                     args=[t for pair in pairs for t in pair] + [e[0] for e in extras],
                     scratch=[pltpu.VMEM((tm, tn), F32)] if nk > 1 else [],
                     semantics=("parallel", "parallel", "arbitrary"), comm=None if comm is _NO_COPIES else comm)
    return res if comm is _NO_COPIES else (res, got)


def _colsum_partials(p):
    return jnp.sum(p.reshape(-1, 8, p.shape[-1])[:, 0, :], axis=0, keepdims=True)


def _rmsnorm_fwd(name, x, g, tr=512):
    T, D = x.shape

    def body(x_ref, g_ref, u_ref, r_ref):
        xv = x_ref[...]
        r = lax.rsqrt(jnp.mean(xv * xv, axis=-1, keepdims=True) + EPS)
        u_ref[...] = (xv * r * g_ref[...]).astype(u_ref.dtype)
        r_ref[...] = r

    return pl.pallas_call(
        body, name=name, grid=(T // tr,),
        in_specs=[pl.BlockSpec((tr, D), lambda i: (i, 0)), pl.BlockSpec((1, D), lambda i: (0, 0))],
        out_specs=[pl.BlockSpec((tr, D), lambda i: (i, 0)), pl.BlockSpec((tr, 1), lambda i: (i, 0))],
        out_shape=[jax.ShapeDtypeStruct((T, D), MXU_DTYPE), jax.ShapeDtypeStruct((T, 1), F32)],
        compiler_params=_params(("parallel",)),
    )(x, g)


def _rmsnorm_bwd_vals(dy, xin, rstd, g):
    xhat = xin * rstd
    dg = jnp.sum(dy * xhat, axis=0, keepdims=True)
    dxh = dy * g
    dx = rstd * (dxh - xhat * jnp.mean(dxh * xhat, axis=-1, keepdims=True))
    return dx, dg


def _colsum(name, a, tr=512):
    T, N = a.shape

    def body(a_ref, o_ref):
        @pl.when(pl.program_id(0) == 0)
        def _():
            o_ref[...] = jnp.zeros_like(o_ref)

        o_ref[...] += jnp.sum(a_ref[...].astype(F32), axis=0, keepdims=True)

    return pl.pallas_call(
        body, name=name, grid=(T // tr,),
        in_specs=[pl.BlockSpec((tr, N), lambda i: (i, 0))],
        out_specs=pl.BlockSpec((1, N), lambda i: (0, 0)),
        out_shape=jax.ShapeDtypeStruct((1, N), F32),
        compiler_params=_params(("arbitrary",)),
    )(a)


ATTN_SCALE = 1.0 / math.sqrt(HEAD_DIM)
GROUP_LANES = GROUP * ATTN_BLOCK
PAIR = 2 * HEAD_DIM


def _attn_mask():
    kj = lax.broadcasted_iota(jnp.int32, (ATTN_BLOCK, GROUP_LANES), 0)
    qi = lax.broadcasted_iota(jnp.int32, (ATTN_BLOCK, GROUP_LANES), 1) & (ATTN_BLOCK - 1)
    return kj <= qi


def _heads_transposed(ref, g, scale=None):
    parts = []
    for a in range(GROUP // 2):
        lo = (g * GROUP // 2 + a) * PAIR
        pair = ref[:, lo:lo + PAIR]
        pair = (pair if scale is None else pair * scale).T
        parts += [pair[:HEAD_DIM], pair[HEAD_DIM:]]
    return jnp.concatenate(parts, axis=1).astype(MXU_DTYPE)


def _heads_back(ref, g, vt):
    for a in range(GROUP // 2):
        lo = (g * GROUP // 2 + a) * PAIR
        pair = jnp.concatenate([vt[:, (2 * a) * ATTN_BLOCK:(2 * a + 1) * ATTN_BLOCK],
                                vt[:, (2 * a + 1) * ATTN_BLOCK:(2 * a + 2) * ATTN_BLOCK]], axis=0)
        ref[:, lo:lo + PAIR] = pair.T.astype(ref.dtype)


def _kv_parts(kv_ref, g):
    ks = slice(g * HEAD_DIM, (g + 1) * HEAD_DIM)
    vs = slice(KV_WIDTH + g * HEAD_DIM, KV_WIDTH + (g + 1) * HEAD_DIM)
    return kv_ref[:, ks].astype(MXU_DTYPE), kv_ref[:, vs].astype(MXU_DTYPE)


def _sink_rows(sinks):
    return jnp.repeat(sinks.reshape(KV_HEADS, GROUP), ATTN_BLOCK, axis=1)


def _attn_fwd(pq, pkv, sinks, comm=None):
    T = pq.shape[0]
    nb = T // ATTN_BLOCK

    def body(q_ref, kvc_ref, kvp_ref, s_ref, y_ref, lse_ref):
        mask_c = _attn_mask()
        has_prev = pl.program_id(0) > 0
        for g in range(KV_HEADS):
            (kc, vc), (kp, vp) = _kv_parts(kvc_ref, g), _kv_parts(kvp_ref, g)
            qt = _heads_transposed(q_ref, g, ATTN_SCALE)
            s = jnp.where(mask_c, _nn(kc, qt), jnp.where(has_prev, _nn(kp, qt), NEG_INF))
            sink = s_ref[g:g + 1, :]
            m = jnp.maximum(jnp.max(s, axis=0, keepdims=True), sink)
            p = jnp.exp(s - m)
            den = jnp.sum(p, axis=0, keepdims=True) + jnp.exp(sink - m)
            pc = jnp.where(mask_c, p, 0.0)
            _heads_back(y_ref, g, (_tn(vc, pc) + _tn(vp, p - pc)) / den)
            lse = m + jnp.log(den)
            for i in range(GROUP):
                lse_ref[g * GROUP + i:g * GROUP + i + 1, :] = lse[:, i * ATTN_BLOCK:(i + 1) * ATTN_BLOCK]

    return _call(
        "attn_fwd", body, grid=(nb,),
        in_specs=[pl.BlockSpec((ATTN_BLOCK, D_MODEL), lambda n: (n, 0)),
                  pl.BlockSpec((ATTN_BLOCK, 2 * KV_WIDTH), lambda n: (n, 0)),
                  pl.BlockSpec((ATTN_BLOCK, 2 * KV_WIDTH), lambda n: (jnp.maximum(n - 1, 0), 0)),
                  pl.BlockSpec((KV_HEADS, GROUP_LANES), lambda n: (0, 0))],
        out_specs=[pl.BlockSpec((ATTN_BLOCK, D_MODEL), lambda n: (n, 0)),
                   pl.BlockSpec((Q_HEADS, ATTN_BLOCK), lambda n: (0, n))],
        out_shape=[jax.ShapeDtypeStruct((T, D_MODEL), MXU_DTYPE), jax.ShapeDtypeStruct((Q_HEADS, T), F32)],
        args=[pq, pkv, pkv, _sink_rows(sinks)], semantics=("parallel",), comm=comm)


def _attn_bwd(pq, pkv, sinks, lse, dy, comm=None):
    T = pq.shape[0]
    nb = T // ATTN_BLOCK
    cur = lambda n: (jnp.minimum(n, nb - 1), 0)

    def body(q_ref, kvc_ref, kvp_ref, s_ref, lse_ref, dy_ref, dq_ref, dkv_ref, ds_ref, carry, top, bot):
        n = pl.program_id(0)

        @pl.when(n == 0)
        def _():
            carry[...] = jnp.zeros_like(carry)
            ds_ref[...] = jnp.zeros_like(ds_ref)

        @pl.when(n < nb)
        def _():
            mask_c = _attn_mask()
            valid = jnp.logical_or(mask_c, n > 0)
            for g in range(KV_HEADS):
                ks = slice(g * HEAD_DIM, (g + 1) * HEAD_DIM)
                vs = slice(KV_WIDTH + g * HEAD_DIM, KV_WIDTH + (g + 1) * HEAD_DIM)
                (kc, vc), (kp, vp) = _kv_parts(kvc_ref, g), _kv_parts(kvp_ref, g)
                qt = _heads_transposed(q_ref, g, ATTN_SCALE)
                dot = _heads_transposed(dy_ref, g)
                lse = jnp.concatenate([lse_ref[g * GROUP + i:g * GROUP + i + 1, :] for i in range(GROUP)], axis=1)
                p = jnp.where(valid, jnp.exp(jnp.where(mask_c, _nn(kc, qt), _nn(kp, qt)) - lse), 0.0)
                dp = jnp.where(mask_c, _nn(vc, dot), _nn(vp, dot))
                delta = jnp.sum(p * dp, axis=0, keepdims=True)
                ds = p * (dp - delta)
                ds_c, p_c = jnp.where(mask_c, ds, 0.0), jnp.where(mask_c, p, 0.0)
                ds_p, p_p = ds - ds_c, p - p_c
                _heads_back(dq_ref, g, (_tn(kc, ds_c) + _tn(kp, ds_p)) * ATTN_SCALE)
                bot[:, ks], bot[:, vs] = _nt(ds_c, qt), _nt(p_c, dot)
                top[:, ks], top[:, vs] = _nt(ds_p, qt), _nt(p_p, dot)
                ds_ref[g:g + 1, :] -= jnp.exp(s_ref[g:g + 1, :] - lse) * delta
            dkv_ref[...] = (carry[...] + top[...]).astype(dkv_ref.dtype)
            carry[...] = bot[...]

        @pl.when(n == nb)
        def _():
            dkv_ref[...] = carry[...].astype(dkv_ref.dtype)

    return _call(
        "attn_bwd", body, grid=(nb + 1,),
        in_specs=[pl.BlockSpec((ATTN_BLOCK, D_MODEL), cur),
                  pl.BlockSpec((ATTN_BLOCK, 2 * KV_WIDTH), cur),
                  pl.BlockSpec((ATTN_BLOCK, 2 * KV_WIDTH), lambda n: (jnp.maximum(jnp.minimum(n, nb - 1) - 1, 0), 0)),
                  pl.BlockSpec((KV_HEADS, GROUP_LANES), lambda n: (0, 0)),
                  pl.BlockSpec((Q_HEADS, ATTN_BLOCK), lambda n: (0, jnp.minimum(n, nb - 1))),
                  pl.BlockSpec((ATTN_BLOCK, D_MODEL), cur)],
        out_specs=[pl.BlockSpec((ATTN_BLOCK, D_MODEL), cur),
                   pl.BlockSpec((ATTN_BLOCK, 2 * KV_WIDTH), lambda n: (jnp.maximum(n - 1, 0), 0)),
                   pl.BlockSpec((KV_HEADS, GROUP_LANES), lambda n: (0, 0))],
        out_shape=[jax.ShapeDtypeStruct((T, D_MODEL), MXU_DTYPE),
                   jax.ShapeDtypeStruct((T, 2 * KV_WIDTH), MXU_DTYPE),
                   jax.ShapeDtypeStruct((KV_HEADS, GROUP_LANES), F32)],
        scratch=[pltpu.VMEM((ATTN_BLOCK, 2 * KV_WIDTH), F32)] * 3,
        args=[pq, pkv, pkv, _sink_rows(sinks), lse, dy], semantics=("arbitrary",), comm=comm)


def _lower_bound(l):
    m = jnp.maximum(l[0:1], l[1:2])
    e0, e1 = jnp.exp(l[0:1] - m), jnp.exp(l[1:2] - m)
    return e0 / (e0 + e1)


def _tri(lower):
    r = lax.broadcasted_iota(jnp.int32, (CHUNK, CHUNK), 0)
    c = lax.broadcasted_iota(jnp.int32, (CHUNK, CHUNK), 1)
    return (r >= c) if lower else (c >= r)


def _chunk_sum(mask, v):
    return lax.dot_general(mask.astype(F32), v, (((1,), (0,)), ((), ())),
                           precision=lax.Precision.HIGHEST, preferred_element_type=F32)


def _hgrn_chunk_inputs(hq, hf, lb, causal):
    sg, sgn = _sigmoid(hf), _sigmoid(-hf)
    f = lb + (1.0 - lb) * sg
    kk = (1.0 - lb) * sgn
    sq = _sigmoid(hq)
    q = hq * sq
    b = _chunk_sum(causal, jnp.log(f))
    bm, bl = b[CHUNK // 2 - 1:CHUNK // 2, :], b[CHUNK - 1:CHUNK, :]
    e_qm, e_km, e_qs, e_kl = jnp.exp(b - bm), jnp.exp(bm - b), jnp.exp(b), jnp.exp(bl - b)
    return dict(sg=sg, sgn=sgn, f=f, kk=kk, sq=sq, q=q, e_qm=e_qm, e_km=e_km, e_qs=e_qs, e_kl=e_kl,
                qm=q * e_qm, km=kk * e_km, qs=q * e_qs, kl=kk * e_kl, el=jnp.exp(bl))


def _hgrn_fwd(ph, lb_logits, norm_g, comm=None):
    T = ph.shape[0]
    nblk, cpb = T // HGRN_TOKENS, HGRN_TOKENS // CHUNK
    col = lambda c: pl.BlockSpec((HGRN_TOKENS, D_MODEL), functools.partial(lambda i, c: (i, c), c=c))

    def body(hq_ref, hf_ref, hi_ref, hg_ref, l_ref, ng_ref, y_ref, o_ref, st_ref, s_ref):
        @pl.when(pl.program_id(0) == 0)
        def _():
            s_ref[...] = jnp.zeros_like(s_ref)

        lb = _lower_bound(l_ref[...])
        causal = _tri(True)
        for c in range(cpb):
            rows = slice(c * CHUNK, (c + 1) * CHUNK)
            t = _hgrn_chunk_inputs(hq_ref[rows, :], hf_ref[rows, :], lb, causal)
            qm, km, qs, kl = (t[n].astype(MXU_DTYPE) for n in ("qm", "km", "qs", "kl"))
            v = hi_ref[rows, :].astype(MXU_DTYPE)
            for h in range(HGRN_HEADS):
                ls = slice(h * HGRN_K, (h + 1) * HGRN_K)
                st = s_ref[h]
                st_ref[c, ls, :] = st
                a = jnp.where(causal, _nt(qm[:, ls], km[:, ls]), 0.0)
                o_ref[rows, ls] = _nn(a, v[:, ls]) + _nt(qs[:, ls], st)
                s_ref[h] = t["el"][:, ls] * st + _tn(v[:, ls], kl[:, ls])
        for h in range(HGRN_HEADS):
            ls = slice(h * HGRN_K, (h + 1) * HGRN_K)
            o = o_ref[:, ls]
            r = lax.rsqrt(jnp.mean(o * o, axis=-1, keepdims=True) + EPS)
            y_ref[:, ls] = (o * r * ng_ref[:, ls] * _sigmoid(hg_ref[:, ls])).astype(y_ref.dtype)

    return _call(
        "hgrn_fwd", body, grid=(nblk,),
        in_specs=[col(0), col(1), col(2), col(3),
                  pl.BlockSpec((2, D_MODEL), lambda i: (0, 0)), pl.BlockSpec((1, D_MODEL), lambda i: (0, 0))],
        out_specs=[pl.BlockSpec((HGRN_TOKENS, D_MODEL), lambda i: (i, 0)),
                   pl.BlockSpec((HGRN_TOKENS, D_MODEL), lambda i: (i, 0)),
                   pl.BlockSpec((cpb, D_MODEL, HGRN_K), lambda i: (i, 0, 0))],
        out_shape=[jax.ShapeDtypeStruct((T, D_MODEL), MXU_DTYPE), jax.ShapeDtypeStruct((T, D_MODEL), F32),
                   jax.ShapeDtypeStruct((T // CHUNK, D_MODEL, HGRN_K), F32)],
        scratch=[pltpu.VMEM((HGRN_HEADS, HGRN_K, HGRN_K), F32)],
        args=[ph, ph, ph, ph, lb_logits, norm_g], semantics=("arbitrary",), comm=comm)


def _hgrn_bwd(ph, o_raw, states, dy, lb_logits, norm_g, comm=None):
    T = ph.shape[0]
    nblk, cpb = T // HGRN_TOKENS, HGRN_TOKENS // CHUNK
    rev = lambda i: nblk - 1 - i
    col = lambda c: pl.BlockSpec((HGRN_TOKENS, D_MODEL), functools.partial(lambda i, c: (rev(i), c), c=c))
    tok = pl.BlockSpec((HGRN_TOKENS, D_MODEL), lambda i: (rev(i), 0))

    def body(hq_ref, hf_ref, hi_ref, hg_ref, o_ref, st_ref, dy_ref, l_ref, ng_ref,
             dph_ref, dng_ref, dl_ref, dst_ref, dlb_ref, do_s, dqm_s, dkm_s, dqs_s, dkl_s, dv_s, del_s):
        i = pl.program_id(0)

        @pl.when(i == 0)
        def _():
            dst_ref[...] = jnp.zeros_like(dst_ref)
            dlb_ref[...] = jnp.zeros_like(dlb_ref)
            dng_ref[...] = jnp.zeros_like(dng_ref)

        lb = _lower_bound(l_ref[...])
        causal, anti = _tri(True), _tri(False)
        row = lax.broadcasted_iota(jnp.int32, (CHUNK, D_MODEL), 0)
        for c in reversed(range(cpb)):
            rows = slice(c * CHUNK, (c + 1) * CHUNK)
            hq = hq_ref[rows, :]
            t = _hgrn_chunk_inputs(hq, hf_ref[rows, :], lb, causal)
            sgg = _sigmoid(hg_ref[rows, :])
            dyv = dy_ref[rows, :]
            for h in range(HGRN_HEADS):
                ls = slice(h * HGRN_K, (h + 1) * HGRN_K)
                o = o_ref[rows, ls]
                r = lax.rsqrt(jnp.mean(o * o, axis=-1, keepdims=True) + EPS)
                nrm = o * r
                g_h = sgg[:, ls]
                dph_ref[rows, 3 * D_MODEL + h * HGRN_K:3 * D_MODEL + (h + 1) * HGRN_K] = (
                    dyv[:, ls] * nrm * ng_ref[:, ls] * g_h * (1.0 - g_h)).astype(dph_ref.dtype)
                dyg = dyv[:, ls] * g_h
                dng_ref[:, ls] += jnp.sum(dyg * nrm, axis=0, keepdims=True)
                dn = dyg * ng_ref[:, ls]
                do_s[:, ls] = r * (dn - nrm * jnp.mean(dn * nrm, axis=-1, keepdims=True))
            qm, km, qs, kl = (t[n].astype(MXU_DTYPE) for n in ("qm", "km", "qs", "kl"))
            v = hi_ref[rows, :].astype(MXU_DTYPE)
            do = do_s[...].astype(MXU_DTYPE)
            for h in range(HGRN_HEADS):
                ls = slice(h * HGRN_K, (h + 1) * HGRN_K)
                st = st_ref[c, ls, :]
                dst = dst_ref[h]
                a = jnp.where(causal, _nt(qm[:, ls], km[:, ls]), 0.0)
                da = jnp.where(causal, _nt(do[:, ls], v[:, ls]), 0.0)
                dv_s[:, ls] = _tn(a, do[:, ls]) + _nt(kl[:, ls], dst)
                dkl_s[:, ls] = _nn(v[:, ls], dst)
                dqs_s[:, ls] = _nn(do[:, ls], st)
                del_s[:, ls] = jnp.sum(dst * st, axis=0, keepdims=True)
                dst_ref[h] = _tn(do[:, ls], qs[:, ls]) + t["el"][:, ls] * dst
                dqm_s[:, ls] = _nn(da, km[:, ls])
                dkm_s[:, ls] = _tn(da, qm[:, ls])
            dqm, dkm, dqs, dkl = dqm_s[...], dkm_s[...], dqs_s[...], dkl_s[...]
            dq = dqm * t["e_qm"] + dqs * t["e_qs"]
            dk = dkm * t["e_km"] + dkl * t["e_kl"]
            t_qm, t_km, t_kl = dqm * t["qm"], dkm * t["km"], dkl * t["kl"]
            db = t_qm - t_km + dqs * t["qs"] - t_kl
            db_mid = jnp.sum(t_km - t_qm, axis=0, keepdims=True)
            db_last = jnp.sum(t_kl, axis=0, keepdims=True) + del_s[...] * t["el"]
            db = db + jnp.where(row == CHUNK // 2 - 1, db_mid, 0.0) + jnp.where(row == CHUNK - 1, db_last, 0.0)
            dlogf = _chunk_sum(anti, db)
            sq, sg, sgn, f = t["sq"], t["sg"], t["sgn"], t["f"]
            dph_ref[rows, 0:D_MODEL] = (dq * (sq * (1.0 + hq * (1.0 - sq)))).astype(dph_ref.dtype)
            dph_ref[rows, D_MODEL:2 * D_MODEL] = (
                dlogf * (1.0 - lb) * sg * (1.0 - sg) / f - dk * (1.0 - lb) * sgn * (1.0 - sgn)).astype(dph_ref.dtype)
            dph_ref[rows, 2 * D_MODEL:3 * D_MODEL] = dv_s[...].astype(dph_ref.dtype)
            dlb_ref[...] += jnp.sum(dlogf * (1.0 - sg) / f - dk * sgn, axis=0, keepdims=True)

        @pl.when(i == nblk - 1)
        def _():
            dl0 = dlb_ref[...] * lb * (1.0 - lb)
            dl_ref[0:1, :] = dl0
            dl_ref[1:2, :] = -dl0

    wide = pltpu.VMEM((CHUNK, D_MODEL), F32)
    return _call(
        "hgrn_bwd", body, grid=(nblk,),
        in_specs=[col(0), col(1), col(2), col(3), tok,
                  pl.BlockSpec((cpb, D_MODEL, HGRN_K), lambda i: (rev(i), 0, 0)), tok,
                  pl.BlockSpec((2, D_MODEL), lambda i: (0, 0)), pl.BlockSpec((1, D_MODEL), lambda i: (0, 0))],
        out_specs=[pl.BlockSpec((HGRN_TOKENS, 4 * D_MODEL), lambda i: (rev(i), 0)),
                   pl.BlockSpec((1, D_MODEL), lambda i: (0, 0)), pl.BlockSpec((2, D_MODEL), lambda i: (0, 0))],
        out_shape=[jax.ShapeDtypeStruct((T, 4 * D_MODEL), MXU_DTYPE), jax.ShapeDtypeStruct((1, D_MODEL), F32),
                   jax.ShapeDtypeStruct((2, D_MODEL), F32)],
        scratch=[pltpu.VMEM((HGRN_HEADS, HGRN_K, HGRN_K), F32), pltpu.VMEM((1, D_MODEL), F32),
                 wide, wide, wide, wide, wide, wide, pltpu.VMEM((1, D_MODEL), F32)],
        args=[ph, ph, ph, ph, o_raw, states, dy, lb_logits, norm_g], semantics=("arbitrary",), comm=comm)


def _local_step(x, target, vec, net):
    T, D = x.shape
    norm_mix_g, b_in, sinks, lb_logits = vec["norm_mix_g"], vec["b_in"], vec["attn_sinks"], vec["hgrn_lb_logits"]
    hgrn_norm_g, norm_ffn_g, norm_final_g = vec["hgrn_norm_g"], vec["norm_ffn_g"], vec["norm_final_g"]
    w_in = net.full("w_in")
    o_q, o_kv, o_h, o_g = (sum(IN_SPLITS[:i]) for i in range(4))
    w_q, w_kv, w_h, w_g = (w_in[o:o + n] for o, n in zip((o_q, o_kv, o_h, o_g), IN_SPLITS))
    b_q, b_kv, b_h, b_g = (b_in[:, o:o + n] for o, n in zip((o_q, o_kv, o_h, o_g), IN_SPLITS))
    bias = lambda acc, b: (acc + b,)
    both = lambda acc: (acc, acc)
    grad_outs = [("mn", F32), ("mn", MXU_DTYPE)]
    row_vec = lambda n: ((1, n), lambda i, j: (0, j))
    tile = lambda tm, tn: ((tm, tn), lambda i, j: (i, j))
    TM = 512

    u, rstd1 = _rmsnorm_fwd("norm_mix", x, norm_mix_g)
    pq, = _matmul("in_q", [(u, w_q)], "nt", tm=TM, tn=1024, tk=1024, outs=[("mn", F32)],
                  extras=[(b_q, *row_vec(1024))], epilogue=bias)
    pkv, = _matmul("in_kv", [(u, w_kv)], "nt", tm=TM, tn=256, tk=1024, outs=[("mn", F32)],
                   extras=[(b_kv, *row_vec(256))], epilogue=bias)
    names = ("w_branch_attn", "w_branch_hgrn", "w_out")
    (ph,), got = _matmul("in_h", [(u, w_h)], "nt", tm=TM, tn=1024, tk=1024, outs=[("mn", F32)],
                         extras=[(b_h, *row_vec(1024))], epilogue=bias, comm=net.gather(names))
    net.gathered(names, got)
    pg, = _matmul("in_g", [(u, w_g)], "nt", tm=TM, tn=1024, tk=1024, outs=[("mn", F32)],
                  extras=[(b_g, *row_vec(1024))], epilogue=bias)
    names = ("w_ffn_gate",)
    (y_attn, lse), got = _attn_fwd(pq, pkv, sinks, comm=net.gather(names))
    net.gathered(names, got)
    names = ("w_ffn_up",)
    (y_hgrn, o_raw, states), got = _hgrn_fwd(ph, lb_logits, hgrn_norm_g, comm=net.gather(names))
    net.gathered(names, got)
    w_ba, w_bh, w_out = net.full("w_branch_attn"), net.full("w_branch_hgrn"), net.full("w_out")
    w_gate, w_up = net.full("w_ffn_gate"), net.full("w_ffn_up")
    ya, = _matmul("branch_a", [(y_attn, w_ba)], "nn", tm=TM, tn=1024, tk=1024, outs=[("mn", F32)])
    gate_a = (pg, (TM, 1024), lambda i, j: (i, 0))
    gate_b = (pg, (TM, 1024), lambda i, j: (i, 1))

    def merge(acc, ya_v, ga, gb):
        return acc, _sigmoid(ga) * ya_v + _sigmoid(gb) * acc

    yb, merged = _matmul("branch_b", [(y_hgrn, w_bh)], "nn", tm=TM, tn=1024, tk=1024,
                         outs=[("mn", F32), ("mn", MXU_DTYPE)],
                         extras=[(ya, *tile(TM, 1024)), gate_a, gate_b], epilogue=merge)

    def resid_norm(acc, xin, g):
        h = xin + acc
        r = lax.rsqrt(jnp.mean(h * h, axis=-1, keepdims=True) + EPS)
        return h, h * r * g, r

    h1, u2, rstd2 = _matmul("out_proj", [(merged, w_out)], "nn", tm=TM, tn=1024, tk=1024,
                            outs=[("mn", F32), ("mn", MXU_DTYPE), ("m", F32)],
                            extras=[(x, *tile(TM, 1024)), (norm_ffn_g, *row_vec(1024))], epilogue=resid_norm)
    FT = FFN // 2
    gpre, = _matmul("ffn_gate", [(u2, w_gate)], "nt", tm=TM, tn=FT, tk=1024, outs=[("mn", F32)])

    def swiglu(acc, gv):
        return acc, gv * _sigmoid(gv) * acc

    names = ("w_ffn_down",)
    (up, z), got = _matmul("ffn_up", [(u2, w_up)], "nt", tm=TM, tn=FT, tk=1024,
                           outs=[("mn", F32), ("mn", MXU_DTYPE)],
                           extras=[(gpre, *tile(TM, FT))], epilogue=swiglu, comm=net.gather(names))
    net.gathered(names, got)
    w_down = net.full("w_ffn_down")

    def loss_head(acc, h1_v, tgt, g):
        h2 = h1_v + acc
        r = lax.rsqrt(jnp.mean(h2 * h2, axis=-1, keepdims=True) + EPS)
        xhat = h2 * r
        err = xhat * g - tgt
        part = 0.5 * jnp.sum(jnp.sum(err * err, axis=-1, keepdims=True), axis=0, keepdims=True) / D
        dyv = err / D
        dxh = dyv * g
        dh2 = r * (dxh - xhat * jnp.mean(dxh * xhat, axis=-1, keepdims=True))
        return dh2, dh2, jnp.sum(dyv * xhat, axis=0, keepdims=True), jnp.broadcast_to(part, (1, D))

    dh2, dh2b, dgf_p, loss_p = _matmul(
        "ffn_down", [(z, w_down)], "nn", tm=TM, tn=1024, tk=FFN,
        outs=[("mn", F32), ("mn", MXU_DTYPE), ("pn", F32), ("pn", F32)],
        extras=[(h1, *tile(TM, 1024)), (target, *tile(TM, 1024)), (norm_final_g, *row_vec(1024))],
        epilogue=loss_head)
    loss = jnp.sum(loss_p.reshape(-1, 8, D)[:, 0, 0])
    d_norm_final = _colsum_partials(dgf_p)

    def swiglu_bwd(acc, gv, upv):
        s = _sigmoid(gv)
        return acc * upv * (s * (1.0 + gv * (1.0 - s))), acc * (gv * s)

    dgp, dup = _matmul("d_ffn_hidden", [(dh2b, w_down)], "nt", tm=TM, tn=FT, tk=1024,
                       outs=[("mn", MXU_DTYPE), ("mn", MXU_DTYPE)],
                       extras=[(gpre, *tile(TM, FT)), (up, *tile(TM, FT))], epilogue=swiglu_bwd)
    d_w_down = _matmul("dw_down", [(z, dh2b)], "tn", tm=FT, tn=1024, tk=1024, outs=grad_outs, epilogue=both)

    def norm_ffn_bwd(acc, h1_v, r, g, dres):
        dx, dg = _rmsnorm_bwd_vals(acc, h1_v, r, g)
        dh = dres + dx
        return dh, dh, dg

    names = ("w_ffn_down",)
    (dh1, dh1b, dg2_p), got = _matmul(
        "d_ffn_in", [(dgp, w_gate), (dup, w_up)], "nn", tm=TM, tn=1024, tk=FT,
        outs=[("mn", F32), ("mn", MXU_DTYPE), ("pn", F32)],
        extras=[(h1, *tile(TM, 1024)), (rstd2, (TM, 1), lambda i, j: (i, 0)),
                (norm_ffn_g, *row_vec(1024)), (dh2, *tile(TM, 1024))], epilogue=norm_ffn_bwd,
        comm=net.exchange(dict(w_ffn_down=[d_w_down])))
    net.received(names, got)
    d_norm_ffn = _colsum_partials(dg2_p)
    d_w_gate = _matmul("dw_gate", [(dgp, u2)], "tn", tm=FT, tn=1024, tk=1024, outs=grad_outs, epilogue=both)
    d_w_up = _matmul("dw_up", [(dup, u2)], "tn", tm=FT, tn=1024, tk=1024, outs=grad_outs, epilogue=both)

    def merge_bwd(acc, ya_v, yb_v, ga, gb):
        sa, sb = _sigmoid(ga), _sigmoid(gb)
        return acc * sa, acc * sb, acc * ya_v * sa * (1.0 - sa), acc * yb_v * sb * (1.0 - sb)

    dya, dyb, dga, dgb = _matmul(
        "d_merged", [(dh1b, w_out)], "nt", tm=TM, tn=1024, tk=1024, outs=[("mn", MXU_DTYPE)] * 4,
        extras=[(ya, *tile(TM, 1024)), (yb, *tile(TM, 1024)), gate_a, gate_b], epilogue=merge_bwd)
    tn_grad = functools.partial(_matmul, mode="tn", tn=1024, tk=1024, outs=grad_outs, epilogue=both)
    d_w_out = tn_grad("dw_out", [(merged, dh1b)], tm=1024)
    dy_attn, = _matmul("d_y_attn", [(dya, w_ba)], "nt", tm=TM, tn=1024, tk=1024, outs=[("mn", F32)])
    dy_hgrn, = _matmul("d_y_hgrn", [(dyb, w_bh)], "nt", tm=TM, tn=1024, tk=1024, outs=[("mn", F32)])
    d_w_ba = tn_grad("dw_branch_a", [(y_attn, dya)], tm=1024)
    d_w_bh = tn_grad("dw_branch_b", [(y_hgrn, dyb)], tm=1024)

    names = ("w_ffn_gate",)
    (dq, dkv, dsink), got = _attn_bwd(pq, pkv, sinks, lse, dy_attn, comm=net.exchange(dict(w_ffn_gate=[d_w_gate])))
    net.received(names, got)
    names = ("w_ffn_up", "w_out", "w_branch_attn", "w_branch_hgrn")
    (dph, d_hgrn_norm, d_lb_logits), got = _hgrn_bwd(
        ph, o_raw, states, dy_hgrn, lb_logits, hgrn_norm_g,
        comm=net.exchange(dict(w_ffn_up=[d_w_up], w_out=[d_w_out], w_branch_attn=[d_w_ba],
                               w_branch_hgrn=[d_w_bh])))
    net.received(names, got)

    d_w_in = [tn_grad("dw_in_q", [(dq, u)], tm=1024), tn_grad("dw_in_kv", [(dkv, u)], tm=256),
              tn_grad("dw_in_h", [(dph, u)], tm=1024), tn_grad("dw_in_ga", [(dga, u)], tm=1024),
              tn_grad("dw_in_gb", [(dgb, u)], tm=1024)]

    def norm_mix_bwd(acc, xin, r, g, dres):
        dx, dg = _rmsnorm_bwd_vals(acc, xin, r, g)
        return dres + dx, dg

    names = ("w_in",)
    (dx, dg1_p), got = _matmul(
        "d_u", [(dq, w_q), (dkv, w_kv), (dph, w_h), (dga, w_g[:D]), (dgb, w_g[D:])], "nn",
        tm=TM, tn=1024, tk=512, outs=[("mn", F32), ("pn", F32)],
        extras=[(x, *tile(TM, 1024)), (rstd1, (TM, 1), lambda i, j: (i, 0)),
                (norm_mix_g, *row_vec(1024)), (dh1, *tile(TM, 1024))], epilogue=norm_mix_bwd,
        comm=net.exchange(dict(w_in=d_w_in)))
    net.received(names, got)
    d_norm_mix = _colsum_partials(dg1_p)
    d_b_in = jnp.concatenate([_colsum("db_q", dq), _colsum("db_kv", dkv), _colsum("db_h", dph),
                              _colsum("db_ga", dga), _colsum("db_gb", dgb)], axis=1)
    vecs = dict(norm_mix_g=d_norm_mix, b_in=d_b_in, attn_sinks=jnp.sum(dsink.reshape(Q_HEADS, ATTN_BLOCK), axis=1).reshape(1, Q_HEADS),
                hgrn_lb_logits=d_lb_logits,
                hgrn_norm_g=d_hgrn_norm, norm_ffn_g=d_norm_ffn, norm_final_g=d_norm_final)
    return loss, dx, vecs


def _place():
    return lax.axis_index("x"), lax.axis_index("y"), lax.axis_index("c")


def _other_chips(x, y):
    return [(1 - x, y), (x, 1 - y), (1 - x, 1 - y)]


def _gather_copies(shards):
    n = len(shards)

    def build(ins, outs, send_sems, recv_sems, local_sems):
        x, y, c = _place()
        mine = 2 * x + y
        local = [pltpu.make_async_copy(ins[w], outs[w].at[mine], local_sems.at[w]) for w in range(n)]
        sends, recvs = [], []
        for w in range(n):
            for k, (px, py) in enumerate(_other_chips(x, y)):
                sem = 3 * w + k
                sends.append(pltpu.make_async_remote_copy(
                    src_ref=ins[w], dst_ref=outs[w].at[mine], send_sem=send_sems.at[sem], recv_sem=recv_sems.at[sem],
                    device_id=(px, py, c), device_id_type=MESH_ID))
                recvs.append(pltpu.make_async_remote_copy(
                    src_ref=ins[w], dst_ref=outs[w].at[2 * px + py], send_sem=send_sems.at[sem],
                    recv_sem=recv_sems.at[sem], device_id=(px, py, c), device_id_type=MESH_ID))
        return sends, recvs, local

    return _Carried(shards, [jax.ShapeDtypeStruct((N_CHIPS,) + s.shape, s.dtype) for s in shards], 3 * n, n, build)


def _grad_copies(stacked):
    n = len(stacked)

    def build(ins, outs, send_sems, recv_sems, local_sems):
        x, y, c = _place()
        sends = []
        for w in range(n):
            for k, (px, py) in enumerate(_other_chips(x, y)):
                sem = 3 * w + k
                sends.append(pltpu.make_async_remote_copy(
                    src_ref=ins[w].at[2 * px + py], dst_ref=outs[w].at[k], send_sem=send_sems.at[sem],
                    recv_sem=recv_sems.at[sem], device_id=(px, py, c), device_id_type=MESH_ID))
        return sends, sends, []

    return _Carried(stacked, [jax.ShapeDtypeStruct((3,) + s.shape[1:], s.dtype) for s in stacked], 3 * n, 0, build)


def _small_copies(small):
    def build(ins, outs, send_sems, recv_sems, local_sems):
        small_ref, all_ref = ins[0], outs[0]
        x, y, c = _place()
        me = 4 * x + 2 * y + c
        sends, recvs = [], []
        for r in range(1, 8):
            px = 1 - x if r & 4 else x
            py = 1 - y if r & 2 else y
            pc = 1 - c if r & 1 else c
            sends.append(pltpu.make_async_remote_copy(
                src_ref=small_ref, dst_ref=all_ref.at[me], send_sem=send_sems.at[r - 1], recv_sem=recv_sems.at[r - 1],
                device_id=(px, py, pc), device_id_type=MESH_ID))
            recvs.append(pltpu.make_async_remote_copy(
                src_ref=small_ref, dst_ref=all_ref.at[4 * px + 2 * py + pc], send_sem=send_sems.at[r - 1],
                recv_sem=recv_sems.at[r - 1], device_id=(px, py, pc), device_id_type=MESH_ID))
        return sends, recvs, [pltpu.make_async_copy(small_ref, all_ref.at[me], local_sems.at[0])]

    return _Carried([small], [jax.ShapeDtypeStruct((8,) + small.shape, small.dtype)], 7, 1, build)


def _gather_two_level(name, shards):
    n = len(shards)

    def body(*refs):
        ins, outs = refs[:n], refs[n:2 * n]
        send1, recv1, send2, recv2, local_sems = refs[2 * n:]
        x, y, c = _place()
        mine = 2 * x + y
        local = [pltpu.make_async_copy(ins[w], outs[w].at[mine], local_sems.at[w]) for w in range(n)]
        first, landed, relay, relayed = [], [], [], []
        for w in range(n):
            half = shards[w].shape[0] // 2
            my_half, other_half = pl.ds(c * half, half), pl.ds((1 - c) * half, half)
            for k, (px, py) in enumerate(_other_chips(x, y)):
                sem, chip = 3 * w + k, 2 * px + py
                first.append(pltpu.make_async_remote_copy(
                    src_ref=ins[w].at[my_half], dst_ref=outs[w].at[mine, my_half], send_sem=send1.at[sem],
                    recv_sem=recv1.at[sem], device_id=(px, py, c), device_id_type=MESH_ID))
                landed.append(pltpu.make_async_remote_copy(
                    src_ref=ins[w].at[my_half], dst_ref=outs[w].at[chip, my_half], send_sem=send1.at[sem],
                    recv_sem=recv1.at[sem], device_id=(px, py, c), device_id_type=MESH_ID))
                relay.append(pltpu.make_async_remote_copy(
                    src_ref=outs[w].at[chip, my_half], dst_ref=outs[w].at[chip, my_half], send_sem=send2.at[sem],
                    recv_sem=recv2.at[sem], device_id=(x, y, 1 - c), device_id_type=MESH_ID))
                relayed.append(pltpu.make_async_remote_copy(
                    src_ref=outs[w].at[chip, other_half], dst_ref=outs[w].at[chip, other_half], send_sem=send2.at[sem],
                    recv_sem=recv2.at[sem], device_id=(x, y, 1 - c), device_id_type=MESH_ID))
        for cp in local + first:
            cp.start()
        for arrived, onward in zip(landed, relay):
            arrived.wait_recv()
            onward.start()
        for cp in relayed:
            cp.wait_recv()
        for cp in first + relay:
            cp.wait_send()
        for cp in local:
            cp.wait()

    sems = pltpu.SemaphoreType.DMA((3 * n,))
    return pl.pallas_call(
        body, name=name, in_specs=[HBM_SPEC] * n, out_specs=[HBM_SPEC] * n,
        out_shape=[jax.ShapeDtypeStruct((N_CHIPS,) + s.shape, s.dtype) for s in shards],
        scratch_shapes=[sems, sems, sems, sems, pltpu.SemaphoreType.DMA((n,))],
    )(*shards)


def _copies_alone(name, comm):
    return _call(name, lambda: None, grid=(), in_specs=[], out_specs=[], out_shape=[], args=[], comm=comm)[1]


class _Net:
    def __init__(self, shards):
        self.shards = shards
        self.whole, self.own, self.theirs = {}, {}, {}
        x, y, _ = _place()
        self.chip = 2 * x + y

    def gather(self, names):
        return _gather_copies([self.shards[n] for n in names])

    def gathered(self, names, got):
        for n, g in zip(names, got):
            self.whole[n] = g.reshape(-1, g.shape[-1])

    def full(self, name):
        return self.whole[name]

    def exchange(self, grads):
        stacked = []
        for n, pieces in grads.items():
            keep = jnp.concatenate([p[0] for p in pieces], axis=0) if len(pieces) > 1 else pieces[0][0]
            send = jnp.concatenate([p[1] for p in pieces], axis=0) if len(pieces) > 1 else pieces[0][1]
            rows = keep.shape[0] // N_CHIPS
            self.own[n] = lax.dynamic_slice_in_dim(keep, self.chip * rows, rows, axis=0)
            stacked.append(send.reshape(N_CHIPS, rows, send.shape[-1]))
        return _grad_copies(stacked)

    def received(self, names, got):
        self.theirs.update(zip(names, got))


def _exchange_sibling(parts):
    n = len(parts)

    def body(*refs):
        ins, outs = refs[:n], refs[n:2 * n]
        send_sems, recv_sems = refs[2 * n:]
        x, y, c = _place()
        copies = [pltpu.make_async_remote_copy(
            src_ref=ins[w], dst_ref=outs[w], send_sem=send_sems.at[w], recv_sem=recv_sems.at[w],
            device_id=(x, y, 1 - c), device_id_type=MESH_ID) for w in range(n)]
        for cp in copies:
            cp.start()
        for cp in copies:
            cp.wait_recv()
        for cp in copies:
            cp.wait_send()

    return pl.pallas_call(
        body, name="exchange_sibling",
        in_specs=[HBM_SPEC] * n, out_specs=[HBM_SPEC] * n,
        out_shape=[jax.ShapeDtypeStruct(p.shape, p.dtype) for p in parts],
        scratch_shapes=[pltpu.SemaphoreType.DMA((n,)), pltpu.SemaphoreType.DMA((n,))],
        compiler_params=pltpu.CompilerParams(has_side_effects=True),
    )(*parts)


def _row_tile(rows, most=512, sublanes=16):
    return max(t for t in range(sublanes, most + 1, sublanes) if rows % t == 0)


def _partial_sum(name, own, recv):
    R, C = own.shape
    tr = _row_tile(R)

    def body(o_ref, r_ref, p_ref):
        p_ref[...] = ((o_ref[...] + r_ref[0].astype(F32)) + r_ref[1].astype(F32)) + r_ref[2].astype(F32)

    return pl.pallas_call(
        body, name=name, grid=(R // tr,),
        in_specs=[pl.BlockSpec((tr, C), lambda i: (i, 0)), pl.BlockSpec((3, tr, C), lambda i: (0, i, 0))],
        out_specs=pl.BlockSpec((tr, C), lambda i: (i, 0)),
        out_shape=jax.ShapeDtypeStruct((R, C), F32),
        compiler_params=_params(("parallel",)),
    )(own, recv)


def _adam_vals(w, g, m, v):
    m = ADAM_B1 * m + (1.0 - ADAM_B1) * g
    v = ADAM_B2 * v + (1.0 - ADAM_B2) * (g * g)
    m_hat = m / (1.0 - ADAM_B1 ** ADAM_STEP)
    v_hat = v / (1.0 - ADAM_B2 ** ADAM_STEP)
    delta = -ADAM_LR * (m_hat / (jnp.sqrt(v_hat) + ADAM_EPS) + ADAM_WD * w)
    return delta, m, v


def _adamw(name, w, m, v, p_south, p_north):
    R, C = w.shape
    tr = _row_tile(R)

    def body(w_ref, m_ref, v_ref, s_ref, n_ref, g_ref, d_ref, nm_ref, nv_ref):
        g = s_ref[...] + n_ref[...]
        d, nm, nv = _adam_vals(w_ref[...], g, m_ref[...], v_ref[...])
        g_ref[...], d_ref[...], nm_ref[...], nv_ref[...] = g, d, nm, nv

    spec = pl.BlockSpec((tr, C), lambda i: (i, 0))
    return pl.pallas_call(
        body, name=name, grid=(R // tr,), in_specs=[spec] * 5, out_specs=[spec] * 4,
        out_shape=[jax.ShapeDtypeStruct((R, C), F32)] * 4,
        compiler_params=_params(("parallel",)),
    )(w, m, v, p_south, p_north)


def _adamw_small(w, m, v, g_all):
    def body(w_ref, m_ref, v_ref, a_ref, g_ref, d_ref, nm_ref, nv_ref):
        g = a_ref[0]
        for dev in range(1, 8):
            g = g + a_ref[dev]
        d, nm, nv = _adam_vals(w_ref[...], g, m_ref[...], v_ref[...])
        g_ref[...], d_ref[...], nm_ref[...], nv_ref[...] = g, d, nm, nv

    return pl.pallas_call(
        body, name="adamw_small", out_shape=[jax.ShapeDtypeStruct(w.shape, F32)] * 4,
    )(w, m, v, g_all)


SMALL_LAYOUT = dict(norm_mix_g=(0, 1024), b_in=(8, 7424), hgrn_norm_g=(16, 1024), norm_ffn_g=(24, 1024),
                    norm_final_g=(32, 1024), hgrn_lb_logits=(40, 2048), attn_sinks=(48, 16))


def _pack_small(vals):
    parts = []
    for name, (_, nvalid) in SMALL_LAYOUT.items():
        flat = vals[name].astype(F32).reshape(-1)
        parts.append(jnp.pad(flat, (0, 8 * D_MODEL - nvalid)).reshape(8, D_MODEL))
    return jnp.concatenate(parts, axis=0)


def _unpack_small(packed, shapes):
    return {name: packed[r0:r0 + 8].reshape(-1)[:nvalid].reshape(shapes[name])
            for name, (r0, nvalid) in SMALL_LAYOUT.items()}


MATRICES = ("w_in", "w_branch_attn", "w_branch_hgrn", "w_out", "w_ffn_gate", "w_ffn_up", "w_ffn_down")
COLUMN_SHARDED = ("w_in", "w_ffn_gate", "w_ffn_up")
WEIGHTS = ("norm_mix_g", "w_in", "b_in", "attn_sinks", "hgrn_lb_logits", "hgrn_norm_g", "w_branch_attn",
           "w_branch_hgrn", "w_out", "norm_ffn_g", "w_ffn_gate", "w_ffn_up", "w_ffn_down", "norm_final_g")


def kernel(x, norm_mix_g, w_in, b_in, attn_sinks, hgrn_lb_logits, hgrn_norm_g, w_branch_attn, w_branch_hgrn, w_out, norm_ffn_g, w_ffn_gate, w_ffn_up, w_ffn_down, norm_final_g, loss_target, m_norm_mix_g, m_w_in, m_b_in, m_attn_sinks, m_hgrn_lb_logits, m_hgrn_norm_g, m_w_branch_attn, m_w_branch_hgrn, m_w_out, m_norm_ffn_g, m_w_ffn_gate, m_w_ffn_up, m_w_ffn_down, m_norm_final_g, v_norm_mix_g, v_w_in, v_b_in, v_attn_sinks, v_hgrn_lb_logits, v_hgrn_norm_g, v_w_branch_attn, v_w_branch_hgrn, v_w_out, v_norm_ffn_g, v_w_ffn_gate, v_w_ffn_up, v_w_ffn_down, v_norm_final_g):
    given = dict(locals())
    w = {n: given[n] for n in WEIGHTS}
    m = {n: given["m_" + n] for n in WEIGHTS}
    v = {n: given["v_" + n] for n in WEIGHTS}

    block = lambda a, n: jnp.transpose(a[0]) if n in COLUMN_SHARDED else a[0]
    unblock = lambda a, n: (jnp.transpose(a) if n in COLUMN_SHARDED else a)[None]
    net = _Net({n: block(w[n], n).astype(MXU_DTYPE) for n in MATRICES})
    net.gathered(("w_in",), _gather_two_level("gather_w_in", [net.shards["w_in"]]))
    vec = dict(norm_mix_g=norm_mix_g, b_in=b_in, attn_sinks=attn_sinks, hgrn_lb_logits=hgrn_lb_logits,
               hgrn_norm_g=hgrn_norm_g, norm_ffn_g=norm_ffn_g, norm_final_g=norm_final_g.reshape(1, D_MODEL))
    loss_part, dx, d_vecs = _local_step(x[0], loss_target[0], vec, net)

    small_all, = _copies_alone("exchange_small", _small_copies(_pack_small(d_vecs)))
    sums = [_partial_sum("sum_" + n, net.own[n], net.theirs[n]) for n in MATRICES]
    theirs = _exchange_sibling(sums)

    grads, deltas, new_m, new_v = {}, {}, {}, {}
    for n, mine, other in zip(MATRICES, sums, theirs):
        res = _adamw("adamw_" + n, block(w[n], n), block(m[n], n), block(v[n], n), mine, other)
        grads[n], deltas[n], new_m[n], new_v[n] = (unblock(r, n) for r in res)
    shapes = {n: w[n].shape for n in SMALL_LAYOUT}
    res = _adamw_small(_pack_small(w), _pack_small(m), _pack_small(v), small_all)
    for dst, packed in zip((grads, deltas, new_m, new_v), res):
        dst.update(_unpack_small(packed, shapes))

    loss = lax.psum(loss_part, ("x", "y", "c"))
    return (loss, dx[None], *[grads[n] for n in WEIGHTS], *[deltas[n] for n in WEIGHTS],
            *[new_m[n] for n in WEIGHTS], *[new_v[n] for n in WEIGHTS])
```

```python
import functools
import math

import jax
import jax.numpy as jnp
from jax import lax
from jax.experimental import pallas as pl
from jax.experimental.pallas import tpu as pltpu

F32 = jnp.float32
BF16 = jnp.bfloat16
MXU_DTYPE = jnp.bfloat16
MESH_ID = pl.DeviceIdType.MESH

D_MODEL = 1024
HEAD_DIM = 64
Q_HEADS = 16
KV_HEADS = 2
GROUP = Q_HEADS // KV_HEADS
KV_WIDTH = KV_HEADS * HEAD_DIM
ATTN_BLOCK = 128
HGRN_HEADS = 8
HGRN_K = 128
CHUNK = 64
HGRN_TOKENS = 256
FFN = 2816
IN_SPLITS = (1024, 256, 4096, 2048)
EPS = 1e-6
NEG_INF = -1e30
ADAM_LR, ADAM_B1, ADAM_B2, ADAM_EPS, ADAM_WD, ADAM_STEP = 0.001, 0.9, 0.999, 1e-08, 0.01, 10
N_CHIPS = 4
VMEM_LIMIT = 56 * 1024 * 1024


def _params(sem=None):
    return pltpu.CompilerParams(dimension_semantics=sem, vmem_limit_bytes=VMEM_LIMIT)


def _sigmoid(v):
    return 0.5 * jnp.tanh(0.5 * v) + 0.5


def _dot(a, b, dims):
    return lax.dot_general(a.astype(MXU_DTYPE), b.astype(MXU_DTYPE), (dims, ((), ())),
                           preferred_element_type=F32)


def _nn(a, b):
    return _dot(a, b, ((1,), (0,)))


def _nt(a, b):
    return _dot(a, b, ((1,), (1,)))


def _tn(a, b):
    return _dot(a, b, ((0,), (0,)))


HBM_SPEC = pl.BlockSpec(memory_space=pl.ANY)


class _Carried:
    def __init__(self, arrays, out_shapes, n_remote, n_local, build):
        self.parts = [(len(arrays), len(out_shapes), build)]
        self.arrays, self.out_shapes = list(arrays), list(out_shapes)
        self.scratch = [pltpu.SemaphoreType.DMA((n_remote,)), pltpu.SemaphoreType.DMA((n_remote,)),
                        pltpu.SemaphoreType.DMA((max(n_local, 1),))]

    def __add__(self, other):
        both = _Carried([], [], 1, 0, None)
        both.parts = self.parts + other.parts
        both.arrays, both.out_shapes = self.arrays + other.arrays, self.out_shapes + other.out_shapes
        both.scratch = self.scratch + other.scratch
        return both

    def _built(self, ins, outs, sems):
        for p, (ni, no, build) in enumerate(self.parts):
            yield build(ins[:ni], outs[:no], *sems[3 * p:3 * p + 3])
            ins, outs = ins[ni:], outs[no:]

    def start(self, ins, outs, sems):
        for sends, _, local in self._built(ins, outs, sems):
            for cp in local + sends:
                cp.start()

    def wait(self, ins, outs, sems):
        for sends, recvs, local in self._built(ins, outs, sems):
            for cp in recvs:
                cp.wait_recv()
            for cp in sends:
                cp.wait_send()
            for cp in local:
                cp.wait()


def _join(*comms):
    comms = [c for c in comms if c is not None]
    return functools.reduce(lambda a, b: a + b, comms) if comms else None


def _call(name, body, *, grid, in_specs, out_specs, out_shape, args, scratch=(), semantics=None, comm=None):
    n_in, n_out, n_scr = len(in_specs), len(out_specs), len(scratch)
    if comm is None:
        res = pl.pallas_call(body, name=name, grid=grid, in_specs=in_specs, out_specs=out_specs, out_shape=out_shape,
                             scratch_shapes=list(scratch), compiler_params=_params(semantics))(*args)
        return list(res), []
    ci, co = len(comm.arrays), len(comm.out_shapes)

    def carrying(*refs):
        ins, refs = refs[:n_in], refs[n_in:]
        c_ins, refs = refs[:ci], refs[ci:]
        outs, refs = refs[:n_out], refs[n_out:]
        c_outs, refs = refs[:co], refs[co:]
        scr, sems = refs[:n_scr], refs[n_scr:]
        if not grid:
            comm.start(c_ins, c_outs, sems)
            body(*ins, *outs, *scr)
            comm.wait(c_ins, c_outs, sems)
            return
        first = functools.reduce(jnp.logical_and, [pl.program_id(a) == 0 for a in range(len(grid))])
        last = functools.reduce(jnp.logical_and, [pl.program_id(a) == g - 1 for a, g in enumerate(grid)])

        @pl.when(first)
        def _():
            comm.start(c_ins, c_outs, sems)

        body(*ins, *outs, *scr)

        @pl.when(last)
        def _():
            comm.wait(c_ins, c_outs, sems)

    res = pl.pallas_call(
        carrying, name=name, grid=grid, in_specs=list(in_specs) + [HBM_SPEC] * ci,
        out_specs=list(out_specs) + [HBM_SPEC] * co, out_shape=list(out_shape) + comm.out_shapes,
        scratch_shapes=list(scratch) + comm.scratch,
        compiler_params=_params(("arbitrary",) * len(grid) if grid else None),
    )(*args, *comm.arrays)
    return list(res[:n_out]), list(res[n_out:])


_NO_COPIES = object()


def _matmul(name, pairs, mode, *, tm, tn, tk, outs, extras=(), epilogue=None, comm=_NO_COPIES):
    prod = dict(nn=_nn, nt=_nt, tn=_tn)[mode]
    a0, b0 = pairs[0]
    M = a0.shape[1] if mode == "tn" else a0.shape[0]
    N = b0.shape[0] if mode == "nt" else b0.shape[1]
    steps, in_specs, offset = [], [], 0
    for a, b in pairs:
        K = a.shape[0] if mode == "tn" else a.shape[1]
        t = min(tk, K)
        assert K % t == 0, (name, K, t)
        kmap = functools.partial(lambda k, off, n: jnp.clip(k - off, 0, n - 1), off=offset, n=K // t)
        if mode == "tn":
            in_specs.append(pl.BlockSpec((t, tm), functools.partial(lambda i, j, k, f: (f(k), i), f=kmap)))
        else:
            in_specs.append(pl.BlockSpec((tm, t), functools.partial(lambda i, j, k, f: (i, f(k)), f=kmap)))
        if mode == "nt":
            in_specs.append(pl.BlockSpec((tn, t), functools.partial(lambda i, j, k, f: (j, f(k)), f=kmap)))
        else:
            in_specs.append(pl.BlockSpec((t, tn), functools.partial(lambda i, j, k, f: (f(k), j), f=kmap)))
        steps.append((offset, offset + K // t))
        offset += K // t
    assert M % tm == 0 and N % tn == 0, (name, M, N, tm, tn)
    ni, nj, nk = M // tm, N // tn, offset
    npair, ne, no = len(pairs), len(extras), len(outs)
    if epilogue is None:
        epilogue = lambda acc: (acc,)

    def finish(acc, extra_refs, out_refs):
        vals = epilogue(acc, *[r[...] for r in extra_refs])
        for (kind, _), o_ref, val in zip(outs, out_refs, vals):
            if kind == "pn":
                val = jnp.broadcast_to(val, o_ref.shape)
            o_ref[...] = val.astype(o_ref.dtype)

    def body(*refs):
        ab, rest = refs[:2 * npair], refs[2 * npair:]
        extra_refs, out_refs = rest[:ne], rest[ne:ne + no]
        if nk == 1:
            finish(prod(ab[0][...], ab[1][...]), extra_refs, out_refs)
            return
        acc_ref = rest[ne + no]
        k = pl.program_id(2)

        @pl.when(k == 0)
        def _():
            acc_ref[...] = jnp.zeros_like(acc_ref)

        for p, (lo, hi) in enumerate(steps):
            @pl.when(jnp.logical_and(k >= lo, k < hi))
            def _():
                acc_ref[...] += prod(ab[2 * p][...], ab[2 * p + 1][...])

        @pl.when(k == nk - 1)
        def _():
            finish(acc_ref[...], extra_refs, out_refs)

    for _, shape, im in extras:
        in_specs.append(pl.BlockSpec(shape, functools.partial(lambda i, j, k, im: im(i, j), im=im)))
    out_shape, out_specs = [], []
    for kind, dt in outs:
        if kind == "mn":
            out_shape.append(jax.ShapeDtypeStruct((M, N), dt))
            out_specs.append(pl.BlockSpec((tm, tn), lambda i, j, k: (i, j)))
        elif kind == "m":
            assert nj == 1
            out_shape.append(jax.ShapeDtypeStruct((M, 1), dt))
            out_specs.append(pl.BlockSpec((tm, 1), lambda i, j, k: (i, 0)))
        else:
            out_shape.append(jax.ShapeDtypeStruct((8 * ni, N), dt))
            out_specs.append(pl.BlockSpec((8, tn), lambda i, j, k: (i, j)))
    grid = (ni, nj, nk)
    if nj > 1 and nk == 1:
        turned = lambda spec: pl.BlockSpec(spec.block_shape, functools.partial(
            lambda j, i, k, im: im(i, j, k), im=spec.index_map))
        in_specs, out_specs, grid = [turned(s) for s in in_specs], [turned(s) for s in out_specs], (nj, ni, nk)
    res, got = _call(name, body, grid=grid, in_specs=in_specs, out_specs=out_specs, out_shape=out_shape,
                     args=[t for pair in pairs for t in pair] + [e[0] for e in extras],
                     scratch=[pltpu.VMEM((tm, tn), F32)] if nk > 1 else [],
                     semantics=("parallel", "parallel", "arbitrary"), comm=None if comm is _NO_COPIES else comm)
    return res if comm is _NO_COPIES else (res, got)


def _colsum_partials(p):
    return jnp.sum(p.reshape(-1, 8, p.shape[-1])[:, 0, :], axis=0, keepdims=True)


def _rmsnorm_fwd(name, x, g, tr=512):
    T, D = x.shape

    def body(x_ref, g_ref, u_ref, r_ref):
        xv = x_ref[...]
        r = lax.rsqrt(jnp.mean(xv * xv, axis=-1, keepdims=True) + EPS)
        u_ref[...] = (xv * r * g_ref[...]).astype(u_ref.dtype)
        r_ref[...] = r

    return pl.pallas_call(
        body, name=name, grid=(T // tr,),
        in_specs=[pl.BlockSpec((tr, D), lambda i: (i, 0)), pl.BlockSpec((1, D), lambda i: (0, 0))],
        out_specs=[pl.BlockSpec((tr, D), lambda i: (i, 0)), pl.BlockSpec((tr, 1), lambda i: (i, 0))],
        out_shape=[jax.ShapeDtypeStruct((T, D), MXU_DTYPE), jax.ShapeDtypeStruct((T, 1), F32)],
        compiler_params=_params(("parallel",)),
    )(x, g)


def _rmsnorm_bwd_vals(dy, xin, rstd, g):
    xhat = xin * rstd
    dg = jnp.sum(dy * xhat, axis=0, keepdims=True)
    dxh = dy * g
    dx = rstd * (dxh - xhat * jnp.mean(dxh * xhat, axis=-1, keepdims=True))
    return dx, dg


def _colsum(name, a, tr=512):
    T, N = a.shape

    def body(a_ref, o_ref):
        @pl.when(pl.program_id(0) == 0)
        def _():
            o_ref[...] = jnp.zeros_like(o_ref)

        o_ref[...] += jnp.sum(a_ref[...].astype(F32), axis=0, keepdims=True)

    return pl.pallas_call(
        body, name=name, grid=(T // tr,),
        in_specs=[pl.BlockSpec((tr, N), lambda i: (i, 0))],
        out_specs=pl.BlockSpec((1, N), lambda i: (0, 0)),
        out_shape=jax.ShapeDtypeStruct((1, N), F32),
        compiler_params=_params(("arbitrary",)),
    )(a)


ATTN_SCALE = 1.0 / math.sqrt(HEAD_DIM)
GROUP_LANES = GROUP * ATTN_BLOCK
PAIR = 2 * HEAD_DIM


def _attn_mask():
    kj = lax.broadcasted_iota(jnp.int32, (ATTN_BLOCK, GROUP_LANES), 0)
    qi = lax.broadcasted_iota(jnp.int32, (ATTN_BLOCK, GROUP_LANES), 1) & (ATTN_BLOCK - 1)
    return kj <= qi


def _heads_transposed(ref, g, scale=None):
    parts = []
    for a in range(GROUP // 2):
        lo = (g * GROUP // 2 + a) * PAIR
        pair = ref[:, lo:lo + PAIR]
        pair = (pair if scale is None else pair * scale).T
        parts += [pair[:HEAD_DIM], pair[HEAD_DIM:]]
    return jnp.concatenate(parts, axis=1).astype(MXU_DTYPE)


def _heads_back(ref, g, vt):
    for a in range(GROUP // 2):
        lo = (g * GROUP // 2 + a) * PAIR
        pair = jnp.concatenate([vt[:, (2 * a) * ATTN_BLOCK:(2 * a + 1) * ATTN_BLOCK],
                                vt[:, (2 * a + 1) * ATTN_BLOCK:(2 * a + 2) * ATTN_BLOCK]], axis=0)
        ref[:, lo:lo + PAIR] = pair.T.astype(ref.dtype)


def _kv_parts(kv_ref, g):
    ks = slice(g * HEAD_DIM, (g + 1) * HEAD_DIM)
    vs = slice(KV_WIDTH + g * HEAD_DIM, KV_WIDTH + (g + 1) * HEAD_DIM)
    return kv_ref[:, ks].astype(MXU_DTYPE), kv_ref[:, vs].astype(MXU_DTYPE)


def _sink_rows(sinks):
    return jnp.repeat(sinks.reshape(KV_HEADS, GROUP), ATTN_BLOCK, axis=1)


def _attn_fwd(pq, pkv, sinks, comm=None):
    T = pq.shape[0]
    nb = T // ATTN_BLOCK

    def body(q_ref, kvc_ref, kvp_ref, s_ref, y_ref, lse_ref):
        mask_c = _attn_mask()
        has_prev = pl.program_id(0) > 0
        for g in range(KV_HEADS):
            (kc, vc), (kp, vp) = _kv_parts(kvc_ref, g), _kv_parts(kvp_ref, g)
            qt = _heads_transposed(q_ref, g, ATTN_SCALE)
            s = jnp.where(mask_c, _nn(kc, qt), jnp.where(has_prev, _nn(kp, qt), NEG_INF))
            sink = s_ref[g:g + 1, :]
            m = jnp.maximum(jnp.max(s, axis=0, keepdims=True), sink)
            p = jnp.exp(s - m)
            den = jnp.sum(p, axis=0, keepdims=True) + jnp.exp(sink - m)
            pc = jnp.where(mask_c, p, 0.0)
            _heads_back(y_ref, g, (_tn(vc, pc) + _tn(vp, p - pc)) / den)
            lse = m + jnp.log(den)
            for i in range(GROUP):
                lse_ref[g * GROUP + i:g * GROUP + i + 1, :] = lse[:, i * ATTN_BLOCK:(i + 1) * ATTN_BLOCK]

    return _call(
        "attn_fwd", body, grid=(nb,),
        in_specs=[pl.BlockSpec((ATTN_BLOCK, D_MODEL), lambda n: (n, 0)),
                  pl.BlockSpec((ATTN_BLOCK, 2 * KV_WIDTH), lambda n: (n, 0)),
                  pl.BlockSpec((ATTN_BLOCK, 2 * KV_WIDTH), lambda n: (jnp.maximum(n - 1, 0), 0)),
                  pl.BlockSpec((KV_HEADS, GROUP_LANES), lambda n: (0, 0))],
        out_specs=[pl.BlockSpec((ATTN_BLOCK, D_MODEL), lambda n: (n, 0)),
                   pl.BlockSpec((Q_HEADS, ATTN_BLOCK), lambda n: (0, n))],
        out_shape=[jax.ShapeDtypeStruct((T, D_MODEL), MXU_DTYPE), jax.ShapeDtypeStruct((Q_HEADS, T), F32)],
        args=[pq, pkv, pkv, _sink_rows(sinks)], semantics=("parallel",), comm=comm)


def _attn_bwd(pq, pkv, sinks, lse, dy, comm=None):
    T = pq.shape[0]
    nb = T // ATTN_BLOCK
    cur = lambda n: (jnp.minimum(n, nb - 1), 0)

    def body(q_ref, kvc_ref, kvp_ref, s_ref, lse_ref, dy_ref, dq_ref, dkv_ref, ds_ref, carry, top, bot):
        n = pl.program_id(0)

        @pl.when(n == 0)
        def _():
            carry[...] = jnp.zeros_like(carry)
            ds_ref[...] = jnp.zeros_like(ds_ref)

        @pl.when(n < nb)
        def _():
            mask_c = _attn_mask()
            valid = jnp.logical_or(mask_c, n > 0)
            for g in range(KV_HEADS):
                ks = slice(g * HEAD_DIM, (g + 1) * HEAD_DIM)
                vs = slice(KV_WIDTH + g * HEAD_DIM, KV_WIDTH + (g + 1) * HEAD_DIM)
                (kc, vc), (kp, vp) = _kv_parts(kvc_ref, g), _kv_parts(kvp_ref, g)
                qt = _heads_transposed(q_ref, g, ATTN_SCALE)
                dot = _heads_transposed(dy_ref, g)
                lse = jnp.concatenate([lse_ref[g * GROUP + i:g * GROUP + i + 1, :] for i in range(GROUP)], axis=1)
                p = jnp.where(valid, jnp.exp(jnp.where(mask_c, _nn(kc, qt), _nn(kp, qt)) - lse), 0.0)
                dp = jnp.where(mask_c, _nn(vc, dot), _nn(vp, dot))
                delta = jnp.sum(p * dp, axis=0, keepdims=True)
                ds = p * (dp - delta)
                ds_c, p_c = jnp.where(mask_c, ds, 0.0), jnp.where(mask_c, p, 0.0)
                ds_p, p_p = ds - ds_c, p - p_c
                _heads_back(dq_ref, g, (_tn(kc, ds_c) + _tn(kp, ds_p)) * ATTN_SCALE)
                bot[:, ks], bot[:, vs] = _nt(ds_c, qt), _nt(p_c, dot)
                top[:, ks], top[:, vs] = _nt(ds_p, qt), _nt(p_p, dot)
                ds_ref[g:g + 1, :] -= jnp.exp(s_ref[g:g + 1, :] - lse) * delta
            dkv_ref[...] = (carry[...] + top[...]).astype(dkv_ref.dtype)
            carry[...] = bot[...]

        @pl.when(n == nb)
        def _():
            dkv_ref[...] = carry[...].astype(dkv_ref.dtype)

    return _call(
        "attn_bwd", body, grid=(nb + 1,),
        in_specs=[pl.BlockSpec((ATTN_BLOCK, D_MODEL), cur),
                  pl.BlockSpec((ATTN_BLOCK, 2 * KV_WIDTH), cur),
                  pl.BlockSpec((ATTN_BLOCK, 2 * KV_WIDTH), lambda n: (jnp.maximum(jnp.minimum(n, nb - 1) - 1, 0), 0)),
                  pl.BlockSpec((KV_HEADS, GROUP_LANES), lambda n: (0, 0)),
                  pl.BlockSpec((Q_HEADS, ATTN_BLOCK), lambda n: (0, jnp.minimum(n, nb - 1))),
                  pl.BlockSpec((ATTN_BLOCK, D_MODEL), cur)],
        out_specs=[pl.BlockSpec((ATTN_BLOCK, D_MODEL), cur),
                   pl.BlockSpec((ATTN_BLOCK, 2 * KV_WIDTH), lambda n: (jnp.maximum(n - 1, 0), 0)),
                   pl.BlockSpec((KV_HEADS, GROUP_LANES), lambda n: (0, 0))],
        out_shape=[jax.ShapeDtypeStruct((T, D_MODEL), MXU_DTYPE),
                   jax.ShapeDtypeStruct((T, 2 * KV_WIDTH), MXU_DTYPE),
                   jax.ShapeDtypeStruct((KV_HEADS, GROUP_LANES), F32)],
        scratch=[pltpu.VMEM((ATTN_BLOCK, 2 * KV_WIDTH), F32)] * 3,
        args=[pq, pkv, pkv, _sink_rows(sinks), lse, dy], semantics=("arbitrary",), comm=comm)


def _lower_bound(l):
    m = jnp.maximum(l[0:1], l[1:2])
    e0, e1 = jnp.exp(l[0:1] - m), jnp.exp(l[1:2] - m)
    return e0 / (e0 + e1)


def _tri(lower):
    r = lax.broadcasted_iota(jnp.int32, (CHUNK, CHUNK), 0)
    c = lax.broadcasted_iota(jnp.int32, (CHUNK, CHUNK), 1)
    return (r >= c) if lower else (c >= r)


def _chunk_sum(mask, v):
    return lax.dot_general(mask.astype(F32), v, (((1,), (0,)), ((), ())),
                           precision=lax.Precision.HIGHEST, preferred_element_type=F32)


def _hgrn_chunk_inputs(hq, hf, lb, causal):
    sg, sgn = _sigmoid(hf), _sigmoid(-hf)
    f = lb + (1.0 - lb) * sg
    kk = (1.0 - lb) * sgn
    sq = _sigmoid(hq)
    q = hq * sq
    b = _chunk_sum(causal, jnp.log(f))
    bm, bl = b[CHUNK // 2 - 1:CHUNK // 2, :], b[CHUNK - 1:CHUNK, :]
    e_qm, e_km, e_qs, e_kl = jnp.exp(b - bm), jnp.exp(bm - b), jnp.exp(b), jnp.exp(bl - b)
    return dict(sg=sg, sgn=sgn, f=f, kk=kk, sq=sq, q=q, e_qm=e_qm, e_km=e_km, e_qs=e_qs, e_kl=e_kl,
                qm=q * e_qm, km=kk * e_km, qs=q * e_qs, kl=kk * e_kl, el=jnp.exp(bl))


def _hgrn_fwd(ph, lb_logits, norm_g, comm=None):
    T = ph.shape[0]
    nblk, cpb = T // HGRN_TOKENS, HGRN_TOKENS // CHUNK
    col = lambda c: pl.BlockSpec((HGRN_TOKENS, D_MODEL), functools.partial(lambda i, c: (i, c), c=c))

    def body(hq_ref, hf_ref, hi_ref, hg_ref, l_ref, ng_ref, y_ref, o_ref, st_ref, s_ref):
        @pl.when(pl.program_id(0) == 0)
        def _():
            s_ref[...] = jnp.zeros_like(s_ref)

        lb = _lower_bound(l_ref[...])
        causal = _tri(True)
        for c in range(cpb):
            rows = slice(c * CHUNK, (c + 1) * CHUNK)
            t = _hgrn_chunk_inputs(hq_ref[rows, :], hf_ref[rows, :], lb, causal)
            qm, km, qs, kl = (t[n].astype(MXU_DTYPE) for n in ("qm", "km", "qs", "kl"))
            v = hi_ref[rows, :].astype(MXU_DTYPE)
            for h in range(HGRN_HEADS):
                ls = slice(h * HGRN_K, (h + 1) * HGRN_K)
                st = s_ref[h]
                st_ref[c, ls, :] = st
                a = jnp.where(causal, _nt(qm[:, ls], km[:, ls]), 0.0)
                o_ref[rows, ls] = _nn(a, v[:, ls]) + _nt(qs[:, ls], st)
                s_ref[h] = t["el"][:, ls] * st + _tn(v[:, ls], kl[:, ls])
        for h in range(HGRN_HEADS):
            ls = slice(h * HGRN_K, (h + 1) * HGRN_K)
            o = o_ref[:, ls]
            r = lax.rsqrt(jnp.mean(o * o, axis=-1, keepdims=True) + EPS)
            y_ref[:, ls] = (o * r * ng_ref[:, ls] * _sigmoid(hg_ref[:, ls])).astype(y_ref.dtype)

    return _call(
        "hgrn_fwd", body, grid=(nblk,),
        in_specs=[col(0), col(1), col(2), col(3),
                  pl.BlockSpec((2, D_MODEL), lambda i: (0, 0)), pl.BlockSpec((1, D_MODEL), lambda i: (0, 0))],
        out_specs=[pl.BlockSpec((HGRN_TOKENS, D_MODEL), lambda i: (i, 0)),
                   pl.BlockSpec((HGRN_TOKENS, D_MODEL), lambda i: (i, 0)),
                   pl.BlockSpec((cpb, D_MODEL, HGRN_K), lambda i: (i, 0, 0))],
        out_shape=[jax.ShapeDtypeStruct((T, D_MODEL), MXU_DTYPE), jax.ShapeDtypeStruct((T, D_MODEL), F32),
                   jax.ShapeDtypeStruct((T // CHUNK, D_MODEL, HGRN_K), F32)],
        scratch=[pltpu.VMEM((HGRN_HEADS, HGRN_K, HGRN_K), F32)],
        args=[ph, ph, ph, ph, lb_logits, norm_g], semantics=("arbitrary",), comm=comm)


def _hgrn_bwd(ph, o_raw, states, dy, lb_logits, norm_g, comm=None):
    T = ph.shape[0]
    nblk, cpb = T // HGRN_TOKENS, HGRN_TOKENS // CHUNK
    rev = lambda i: nblk - 1 - i
    col = lambda c: pl.BlockSpec((HGRN_TOKENS, D_MODEL), functools.partial(lambda i, c: (rev(i), c), c=c))
    tok = pl.BlockSpec((HGRN_TOKENS, D_MODEL), lambda i: (rev(i), 0))

    def body(hq_ref, hf_ref, hi_ref, hg_ref, o_ref, st_ref, dy_ref, l_ref, ng_ref,
             dph_ref, dng_ref, dl_ref, dst_ref, dlb_ref, do_s, dqm_s, dkm_s, dqs_s, dkl_s, dv_s, del_s):
        i = pl.program_id(0)

        @pl.when(i == 0)
        def _():
            dst_ref[...] = jnp.zeros_like(dst_ref)
            dlb_ref[...] = jnp.zeros_like(dlb_ref)
            dng_ref[...] = jnp.zeros_like(dng_ref)

        lb = _lower_bound(l_ref[...])
        causal, anti = _tri(True), _tri(False)
        row = lax.broadcasted_iota(jnp.int32, (CHUNK, D_MODEL), 0)
        for c in reversed(range(cpb)):
            rows = slice(c * CHUNK, (c + 1) * CHUNK)
            hq = hq_ref[rows, :]
            t = _hgrn_chunk_inputs(hq, hf_ref[rows, :], lb, causal)
            sgg = _sigmoid(hg_ref[rows, :])
            dyv = dy_ref[rows, :]
            for h in range(HGRN_HEADS):
                ls = slice(h * HGRN_K, (h + 1) * HGRN_K)
                o = o_ref[rows, ls]
                r = lax.rsqrt(jnp.mean(o * o, axis=-1, keepdims=True) + EPS)
                nrm = o * r
                g_h = sgg[:, ls]
                dph_ref[rows, 3 * D_MODEL + h * HGRN_K:3 * D_MODEL + (h + 1) * HGRN_K] = (
                    dyv[:, ls] * nrm * ng_ref[:, ls] * g_h * (1.0 - g_h)).astype(dph_ref.dtype)
                dyg = dyv[:, ls] * g_h
                dng_ref[:, ls] += jnp.sum(dyg * nrm, axis=0, keepdims=True)
                dn = dyg * ng_ref[:, ls]
                do_s[:, ls] = r * (dn - nrm * jnp.mean(dn * nrm, axis=-1, keepdims=True))
            qm, km, qs, kl = (t[n].astype(MXU_DTYPE) for n in ("qm", "km", "qs", "kl"))
            v = hi_ref[rows, :].astype(MXU_DTYPE)
            do = do_s[...].astype(MXU_DTYPE)
            for h in range(HGRN_HEADS):
                ls = slice(h * HGRN_K, (h + 1) * HGRN_K)
                st = st_ref[c, ls, :]
                dst = dst_ref[h]
                a = jnp.where(causal, _nt(qm[:, ls], km[:, ls]), 0.0)
                da = jnp.where(causal, _nt(do[:, ls], v[:, ls]), 0.0)
                dv_s[:, ls] = _tn(a, do[:, ls]) + _nt(kl[:, ls], dst)
                dkl_s[:, ls] = _nn(v[:, ls], dst)
                dqs_s[:, ls] = _nn(do[:, ls], st)
                del_s[:, ls] = jnp.sum(dst * st, axis=0, keepdims=True)
                dst_ref[h] = _tn(do[:, ls], qs[:, ls]) + t["el"][:, ls] * dst
                dqm_s[:, ls] = _nn(da, km[:, ls])
                dkm_s[:, ls] = _tn(da, qm[:, ls])
            dqm, dkm, dqs, dkl = dqm_s[...], dkm_s[...], dqs_s[...], dkl_s[...]
            dq = dqm * t["e_qm"] + dqs * t["e_qs"]
            dk = dkm * t["e_km"] + dkl * t["e_kl"]
            t_qm, t_km, t_kl = dqm * t["qm"], dkm * t["km"], dkl * t["kl"]
            db = t_qm - t_km + dqs * t["qs"] - t_kl
            db_mid = jnp.sum(t_km - t_qm, axis=0, keepdims=True)
            db_last = jnp.sum(t_kl, axis=0, keepdims=True) + del_s[...] * t["el"]
            db = db + jnp.where(row == CHUNK // 2 - 1, db_mid, 0.0) + jnp.where(row == CHUNK - 1, db_last, 0.0)
            dlogf = _chunk_sum(anti, db)
            sq, sg, sgn, f = t["sq"], t["sg"], t["sgn"], t["f"]
            dph_ref[rows, 0:D_MODEL] = (dq * (sq * (1.0 + hq * (1.0 - sq)))).astype(dph_ref.dtype)
            dph_ref[rows, D_MODEL:2 * D_MODEL] = (
                dlogf * (1.0 - lb) * sg * (1.0 - sg) / f - dk * (1.0 - lb) * sgn * (1.0 - sgn)).astype(dph_ref.dtype)
            dph_ref[rows, 2 * D_MODEL:3 * D_MODEL] = dv_s[...].astype(dph_ref.dtype)
            dlb_ref[...] += jnp.sum(dlogf * (1.0 - sg) / f - dk * sgn, axis=0, keepdims=True)

        @pl.when(i == nblk - 1)
        def _():
            dl0 = dlb_ref[...] * lb * (1.0 - lb)
            dl_ref[0:1, :] = dl0
            dl_ref[1:2, :] = -dl0

    wide = pltpu.VMEM((CHUNK, D_MODEL), F32)
    return _call(
        "hgrn_bwd", body, grid=(nblk,),
        in_specs=[col(0), col(1), col(2), col(3), tok,
                  pl.BlockSpec((cpb, D_MODEL, HGRN_K), lambda i: (rev(i), 0, 0)), tok,
                  pl.BlockSpec((2, D_MODEL), lambda i: (0, 0)), pl.BlockSpec((1, D_MODEL), lambda i: (0, 0))],
        out_specs=[pl.BlockSpec((HGRN_TOKENS, 4 * D_MODEL), lambda i: (rev(i), 0)),
                   pl.BlockSpec((1, D_MODEL), lambda i: (0, 0)), pl.BlockSpec((2, D_MODEL), lambda i: (0, 0))],
        out_shape=[jax.ShapeDtypeStruct((T, 4 * D_MODEL), MXU_DTYPE), jax.ShapeDtypeStruct((1, D_MODEL), F32),
                   jax.ShapeDtypeStruct((2, D_MODEL), F32)],
        scratch=[pltpu.VMEM((HGRN_HEADS, HGRN_K, HGRN_K), F32), pltpu.VMEM((1, D_MODEL), F32),
                 wide, wide, wide, wide, wide, wide, pltpu.VMEM((1, D_MODEL), F32)],
        args=[ph, ph, ph, ph, o_raw, states, dy, lb_logits, norm_g], semantics=("arbitrary",), comm=comm)


def _local_step(x, target, vec, net):
    T, D = x.shape
    norm_mix_g, b_in, sinks, lb_logits = vec["norm_mix_g"], vec["b_in"], vec["attn_sinks"], vec["hgrn_lb_logits"]
    hgrn_norm_g, norm_ffn_g, norm_final_g = vec["hgrn_norm_g"], vec["norm_ffn_g"], vec["norm_final_g"]
    w_in = net.full("w_in")
    o_q, o_kv, o_h, o_g = (sum(IN_SPLITS[:i]) for i in range(4))
    w_q, w_kv, w_h, w_g = (w_in[o:o + n] for o, n in zip((o_q, o_kv, o_h, o_g), IN_SPLITS))
    b_q, b_kv, b_h, b_g = (b_in[:, o:o + n] for o, n in zip((o_q, o_kv, o_h, o_g), IN_SPLITS))
    bias = lambda acc, b: (acc + b,)
    both = lambda acc: (acc, acc)
    grad_outs = [("mn", F32), ("mn", MXU_DTYPE)]
    row_vec = lambda n: ((1, n), lambda i, j: (0, j))
    tile = lambda tm, tn: ((tm, tn), lambda i, j: (i, j))
    TM = 512

    u, rstd1 = _rmsnorm_fwd("norm_mix", x, norm_mix_g)
    pq, = _matmul("in_q", [(u, w_q)], "nt", tm=TM, tn=1024, tk=1024, outs=[("mn", F32)],
                  extras=[(b_q, *row_vec(1024))], epilogue=bias)
    pkv, = _matmul("in_kv", [(u, w_kv)], "nt", tm=TM, tn=256, tk=1024, outs=[("mn", F32)],
                   extras=[(b_kv, *row_vec(256))], epilogue=bias)
    names = ("w_branch_attn", "w_branch_hgrn", "w_out")
    (ph,), got = _matmul("in_h", [(u, w_h)], "nt", tm=TM, tn=1024, tk=1024, outs=[("mn", F32)],
                         extras=[(b_h, *row_vec(1024))], epilogue=bias, comm=net.gather(names))
    net.gathered(names, got)
    pg, = _matmul("in_g", [(u, w_g)], "nt", tm=TM, tn=1024, tk=1024, outs=[("mn", F32)],
                  extras=[(b_g, *row_vec(1024))], epilogue=bias)
    names = ("w_ffn_gate",)
    (y_attn, lse), got = _attn_fwd(pq, pkv, sinks, comm=net.gather(names))
    net.gathered(names, got)
    names = ("w_ffn_up",)
    (y_hgrn, o_raw, states), got = _hgrn_fwd(ph, lb_logits, hgrn_norm_g, comm=net.gather(names))
    net.gathered(names, got)
    w_ba, w_bh, w_out = net.full("w_branch_attn"), net.full("w_branch_hgrn"), net.full("w_out")
    w_gate, w_up = net.full("w_ffn_gate"), net.full("w_ffn_up")
    ya, = _matmul("branch_a", [(y_attn, w_ba)], "nn", tm=TM, tn=1024, tk=1024, outs=[("mn", F32)])
    gate_a = (pg, (TM, 1024), lambda i, j: (i, 0))
    gate_b = (pg, (TM, 1024), lambda i, j: (i, 1))

    def merge(acc, ya_v, ga, gb):
        return acc, _sigmoid(ga) * ya_v + _sigmoid(gb) * acc

    yb, merged = _matmul("branch_b", [(y_hgrn, w_bh)], "nn", tm=TM, tn=1024, tk=1024,
                         outs=[("mn", F32), ("mn", MXU_DTYPE)],
                         extras=[(ya, *tile(TM, 1024)), gate_a, gate_b], epilogue=merge)

    def resid_norm(acc, xin, g):
        h = xin + acc
        r = lax.rsqrt(jnp.mean(h * h, axis=-1, keepdims=True) + EPS)
        return h, h * r * g, r

    h1, u2, rstd2 = _matmul("out_proj", [(merged, w_out)], "nn", tm=TM, tn=1024, tk=1024,
                            outs=[("mn", F32), ("mn", MXU_DTYPE), ("m", F32)],
                            extras=[(x, *tile(TM, 1024)), (norm_ffn_g, *row_vec(1024))], epilogue=resid_norm)
    FT = FFN // 2
    gpre, = _matmul("ffn_gate", [(u2, w_gate)], "nt", tm=TM, tn=FT, tk=1024, outs=[("mn", F32)])

    def swiglu(acc, gv):
        return acc, gv * _sigmoid(gv) * acc

    names = ("w_ffn_down",)
    (up, z), got = _matmul("ffn_up", [(u2, w_up)], "nt", tm=TM, tn=FT, tk=1024,
                           outs=[("mn", F32), ("mn", MXU_DTYPE)],
                           extras=[(gpre, *tile(TM, FT))], epilogue=swiglu, comm=net.gather(names))
    net.gathered(names, got)
    w_down = net.full("w_ffn_down")

    def loss_head(acc, h1_v, tgt, g):
        h2 = h1_v + acc
        r = lax.rsqrt(jnp.mean(h2 * h2, axis=-1, keepdims=True) + EPS)
        xhat = h2 * r
        err = xhat * g - tgt
        part = 0.5 * jnp.sum(jnp.sum(err * err, axis=-1, keepdims=True), axis=0, keepdims=True) / D
        dyv = err / D
        dxh = dyv * g
        dh2 = r * (dxh - xhat * jnp.mean(dxh * xhat, axis=-1, keepdims=True))
        return dh2, dh2, jnp.sum(dyv * xhat, axis=0, keepdims=True), jnp.broadcast_to(part, (1, D))

    dh2, dh2b, dgf_p, loss_p = _matmul(
        "ffn_down", [(z, w_down)], "nn", tm=TM, tn=1024, tk=FFN,
        outs=[("mn", F32), ("mn", MXU_DTYPE), ("pn", F32), ("pn", F32)],
        extras=[(h1, *tile(TM, 1024)), (target, *tile(TM, 1024)), (norm_final_g, *row_vec(1024))],
        epilogue=loss_head)
    loss = jnp.sum(loss_p.reshape(-1, 8, D)[:, 0, 0])
    d_norm_final = _colsum_partials(dgf_p)

    def swiglu_bwd(acc, gv, upv):
        s = _sigmoid(gv)
        return acc * upv * (s * (1.0 + gv * (1.0 - s))), acc * (gv * s)

    dgp, dup = _matmul("d_ffn_hidden", [(dh2b, w_down)], "nt", tm=TM, tn=FT, tk=1024,
                       outs=[("mn", MXU_DTYPE), ("mn", MXU_DTYPE)],
                       extras=[(gpre, *tile(TM, FT)), (up, *tile(TM, FT))], epilogue=swiglu_bwd)
    d_w_down = _matmul("dw_down", [(z, dh2b)], "tn", tm=FT, tn=1024, tk=1024, outs=grad_outs, epilogue=both)

    def norm_ffn_bwd(acc, h1_v, r, g, dres):
        dx, dg = _rmsnorm_bwd_vals(acc, h1_v, r, g)
        dh = dres + dx
        return dh, dh, dg

    names = ("w_ffn_down",)
    (dh1, dh1b, dg2_p), got = _matmul(
        "d_ffn_in", [(dgp, w_gate), (dup, w_up)], "nn", tm=TM, tn=1024, tk=FT,
        outs=[("mn", F32), ("mn", MXU_DTYPE), ("pn", F32)],
        extras=[(h1, *tile(TM, 1024)), (rstd2, (TM, 1), lambda i, j: (i, 0)),
                (norm_ffn_g, *row_vec(1024)), (dh2, *tile(TM, 1024))], epilogue=norm_ffn_bwd,
        comm=net.exchange(dict(w_ffn_down=[d_w_down])))
    net.received(names, (), got)
    d_norm_ffn = _colsum_partials(dg2_p)
    d_w_gate = _matmul("dw_gate", [(dgp, u2)], "tn", tm=FT, tn=1024, tk=1024, outs=grad_outs, epilogue=both)
    d_w_up = _matmul("dw_up", [(dup, u2)], "tn", tm=FT, tn=1024, tk=1024, outs=grad_outs, epilogue=both)

    def merge_bwd(acc, ya_v, yb_v, ga, gb):
        sa, sb = _sigmoid(ga), _sigmoid(gb)
        return acc * sa, acc * sb, acc * ya_v * sa * (1.0 - sa), acc * yb_v * sb * (1.0 - sb)

    dya, dyb, dga, dgb = _matmul(
        "d_merged", [(dh1b, w_out)], "nt", tm=TM, tn=1024, tk=1024, outs=[("mn", MXU_DTYPE)] * 4,
        extras=[(ya, *tile(TM, 1024)), (yb, *tile(TM, 1024)), gate_a, gate_b], epilogue=merge_bwd)
    tn_grad = functools.partial(_matmul, mode="tn", tn=1024, tk=1024, outs=grad_outs, epilogue=both)
    d_w_out = tn_grad("dw_out", [(merged, dh1b)], tm=1024)
    dy_attn, = _matmul("d_y_attn", [(dya, w_ba)], "nt", tm=TM, tn=1024, tk=1024, outs=[("mn", F32)])
    dy_hgrn, = _matmul("d_y_hgrn", [(dyb, w_bh)], "nt", tm=TM, tn=1024, tk=1024, outs=[("mn", F32)])
    d_w_ba = tn_grad("dw_branch_a", [(y_attn, dya)], tm=1024)
    d_w_bh = tn_grad("dw_branch_b", [(y_hgrn, dyb)], tm=1024)

    names, swap = ("w_ffn_gate",), ("w_ffn_down",)
    (dq, dkv, dsink), got = _attn_bwd(pq, pkv, sinks, lse, dy_attn,
                                      comm=net.exchange(dict(w_ffn_gate=[d_w_gate]), swap))
    net.received(names, swap, got)
    names, swap = ("w_ffn_up", "w_out", "w_branch_attn", "w_branch_hgrn"), ("w_ffn_gate",)
    (dph, d_hgrn_norm, d_lb_logits), got = _hgrn_bwd(
        ph, o_raw, states, dy_hgrn, lb_logits, hgrn_norm_g,
        comm=net.exchange(dict(w_ffn_up=[d_w_up], w_out=[d_w_out], w_branch_attn=[d_w_ba],
                               w_branch_hgrn=[d_w_bh]), swap))
    net.received(names, swap, got)

    d_w_in = [tn_grad("dw_in_q", [(dq, u)], tm=1024), tn_grad("dw_in_kv", [(dkv, u)], tm=256),
              tn_grad("dw_in_h", [(dph, u)], tm=1024), tn_grad("dw_in_ga", [(dga, u)], tm=1024),
              tn_grad("dw_in_gb", [(dgb, u)], tm=1024)]

    def norm_mix_bwd(acc, xin, r, g, dres):
        dx, dg = _rmsnorm_bwd_vals(acc, xin, r, g)
        return dres + dx, dg

    names, swap = ("w_in",), ("w_ffn_up", "w_out", "w_branch_attn", "w_branch_hgrn")
    (dx, dg1_p), got = _matmul(
        "d_u", [(dq, w_q), (dkv, w_kv), (dph, w_h), (dga, w_g[:D]), (dgb, w_g[D:])], "nn",
        tm=TM, tn=1024, tk=512, outs=[("mn", F32), ("pn", F32)],
        extras=[(x, *tile(TM, 1024)), (rstd1, (TM, 1), lambda i, j: (i, 0)),
                (norm_mix_g, *row_vec(1024)), (dh1, *tile(TM, 1024))], epilogue=norm_mix_bwd,
        comm=net.exchange(dict(w_in=d_w_in), swap))
    net.received(names, swap, got)
    d_norm_mix = _colsum_partials(dg1_p)
    d_b_in = jnp.concatenate([_colsum("db_q", dq), _colsum("db_kv", dkv), _colsum("db_h", dph),
                              _colsum("db_ga", dga), _colsum("db_gb", dgb)], axis=1)
    vecs = dict(norm_mix_g=d_norm_mix, b_in=d_b_in, attn_sinks=jnp.sum(dsink.reshape(Q_HEADS, ATTN_BLOCK), axis=1).reshape(1, Q_HEADS),
                hgrn_lb_logits=d_lb_logits,
                hgrn_norm_g=d_hgrn_norm, norm_ffn_g=d_norm_ffn, norm_final_g=d_norm_final)
    return loss, dx, vecs


def _place():
    return lax.axis_index("x"), lax.axis_index("y"), lax.axis_index("c")


def _other_chips(x, y):
    return [(1 - x, y), (x, 1 - y), (1 - x, 1 - y)]


def _gather_copies(shards):
    n = len(shards)

    def build(ins, outs, send_sems, recv_sems, local_sems):
        x, y, c = _place()
        mine = 2 * x + y
        local = [pltpu.make_async_copy(ins[w], outs[w].at[mine], local_sems.at[w]) for w in range(n)]
        sends, recvs = [], []
        for w in range(n):
            for k, (px, py) in enumerate(_other_chips(x, y)):
                sem = 3 * w + k
                sends.append(pltpu.make_async_remote_copy(
                    src_ref=ins[w], dst_ref=outs[w].at[mine], send_sem=send_sems.at[sem], recv_sem=recv_sems.at[sem],
                    device_id=(px, py, c), device_id_type=MESH_ID))
                recvs.append(pltpu.make_async_remote_copy(
                    src_ref=ins[w], dst_ref=outs[w].at[2 * px + py], send_sem=send_sems.at[sem],
                    recv_sem=recv_sems.at[sem], device_id=(px, py, c), device_id_type=MESH_ID))
        return sends, recvs, local

    return _Carried(shards, [jax.ShapeDtypeStruct((N_CHIPS,) + s.shape, s.dtype) for s in shards], 3 * n, n, build)


def _grad_copies(stacked):
    n = len(stacked)

    def build(ins, outs, send_sems, recv_sems, local_sems):
        x, y, c = _place()
        sends = []
        for w in range(n):
            for k, (px, py) in enumerate(_other_chips(x, y)):
                sem = 3 * w + k
                sends.append(pltpu.make_async_remote_copy(
                    src_ref=ins[w].at[2 * px + py], dst_ref=outs[w].at[k], send_sem=send_sems.at[sem],
                    recv_sem=recv_sems.at[sem], device_id=(px, py, c), device_id_type=MESH_ID))
        return sends, sends, []

    return _Carried(stacked, [jax.ShapeDtypeStruct((3,) + s.shape[1:], s.dtype) for s in stacked], 3 * n, 0, build)


def _small_copies(small):
    def build(ins, outs, send_sems, recv_sems, local_sems):
        small_ref, all_ref = ins[0], outs[0]
        x, y, c = _place()
        me = 4 * x + 2 * y + c
        sends, recvs = [], []
        for r in range(1, 8):
            px = 1 - x if r & 4 else x
            py = 1 - y if r & 2 else y
            pc = 1 - c if r & 1 else c
            sends.append(pltpu.make_async_remote_copy(
                src_ref=small_ref, dst_ref=all_ref.at[me], send_sem=send_sems.at[r - 1], recv_sem=recv_sems.at[r - 1],
                device_id=(px, py, pc), device_id_type=MESH_ID))
            recvs.append(pltpu.make_async_remote_copy(
                src_ref=small_ref, dst_ref=all_ref.at[4 * px + 2 * py + pc], send_sem=send_sems.at[r - 1],
                recv_sem=recv_sems.at[r - 1], device_id=(px, py, pc), device_id_type=MESH_ID))
        return sends, recvs, [pltpu.make_async_copy(small_ref, all_ref.at[me], local_sems.at[0])]

    return _Carried([small], [jax.ShapeDtypeStruct((8,) + small.shape, small.dtype)], 7, 1, build)


def _gather_two_level(name, shards):
    n = len(shards)

    def body(*refs):
        ins, outs = refs[:n], refs[n:2 * n]
        send1, recv1, send2, recv2, local_sems = refs[2 * n:]
        x, y, c = _place()
        mine = 2 * x + y
        local = [pltpu.make_async_copy(ins[w], outs[w].at[mine], local_sems.at[w]) for w in range(n)]
        first, landed, relay, relayed = [], [], [], []
        for w in range(n):
            half = shards[w].shape[0] // 2
            my_half, other_half = pl.ds(c * half, half), pl.ds((1 - c) * half, half)
            for k, (px, py) in enumerate(_other_chips(x, y)):
                sem, chip = 3 * w + k, 2 * px + py
                first.append(pltpu.make_async_remote_copy(
                    src_ref=ins[w].at[my_half], dst_ref=outs[w].at[mine, my_half], send_sem=send1.at[sem],
                    recv_sem=recv1.at[sem], device_id=(px, py, c), device_id_type=MESH_ID))
                landed.append(pltpu.make_async_remote_copy(
                    src_ref=ins[w].at[my_half], dst_ref=outs[w].at[chip, my_half], send_sem=send1.at[sem],
                    recv_sem=recv1.at[sem], device_id=(px, py, c), device_id_type=MESH_ID))
                relay.append(pltpu.make_async_remote_copy(
                    src_ref=outs[w].at[chip, my_half], dst_ref=outs[w].at[chip, my_half], send_sem=send2.at[sem],
                    recv_sem=recv2.at[sem], device_id=(x, y, 1 - c), device_id_type=MESH_ID))
                relayed.append(pltpu.make_async_remote_copy(
                    src_ref=outs[w].at[chip, other_half], dst_ref=outs[w].at[chip, other_half], send_sem=send2.at[sem],
                    recv_sem=recv2.at[sem], device_id=(x, y, 1 - c), device_id_type=MESH_ID))
        for cp in local + first:
            cp.start()
        for arrived, onward in zip(landed, relay):
            arrived.wait_recv()
            onward.start()
        for cp in relayed:
            cp.wait_recv()
        for cp in first + relay:
            cp.wait_send()
        for cp in local:
            cp.wait()

    sems = pltpu.SemaphoreType.DMA((3 * n,))
    return pl.pallas_call(
        body, name=name, in_specs=[HBM_SPEC] * n, out_specs=[HBM_SPEC] * n,
        out_shape=[jax.ShapeDtypeStruct((N_CHIPS,) + s.shape, s.dtype) for s in shards],
        scratch_shapes=[sems, sems, sems, sems, pltpu.SemaphoreType.DMA((n,))],
    )(*shards)


def _copies_alone(name, comm):
    return _call(name, lambda: None, grid=(), in_specs=[], out_specs=[], out_shape=[], args=[], comm=comm)[1]


class _Net:
    def __init__(self, shards):
        self.shards = shards
        self.whole, self.own, self.theirs, self.sums, self.other = {}, {}, {}, {}, {}
        x, y, _ = _place()
        self.chip = 2 * x + y

    def gather(self, names):
        return _gather_copies([self.shards[n] for n in names])

    def gathered(self, names, got):
        for n, g in zip(names, got):
            self.whole[n] = g.reshape(-1, g.shape[-1])

    def full(self, name):
        return self.whole[name]

    def exchange(self, grads, swap=()):
        stacked = []
        for n, pieces in grads.items():
            keep = jnp.concatenate([p[0] for p in pieces], axis=0) if len(pieces) > 1 else pieces[0][0]
            send = jnp.concatenate([p[1] for p in pieces], axis=0) if len(pieces) > 1 else pieces[0][1]
            rows = keep.shape[0] // N_CHIPS
            self.own[n] = lax.dynamic_slice_in_dim(keep, self.chip * rows, rows, axis=0)
            stacked.append(send.reshape(N_CHIPS, rows, send.shape[-1]))
        return _join(_grad_copies(stacked), self.swap(swap))

    def swap(self, names):
        return _sibling_copies([self.sums[n] for n in names]) if names else None

    def received(self, names, swap, got):
        self.theirs.update(zip(names, got[:len(names)]))
        self.other.update(zip(swap, got[len(names):]))
        for n in names:
            self.sums[n] = _partial_sum("sum_" + n, self.own[n], self.theirs[n])


def _sibling_copies(parts):
    n = len(parts)

    def build(ins, outs, send_sems, recv_sems, local_sems):
        x, y, c = _place()
        copies = [pltpu.make_async_remote_copy(
            src_ref=ins[w], dst_ref=outs[w], send_sem=send_sems.at[w], recv_sem=recv_sems.at[w],
            device_id=(x, y, 1 - c), device_id_type=MESH_ID) for w in range(n)]
        return copies, copies, []

    return _Carried(parts, [jax.ShapeDtypeStruct(p.shape, p.dtype) for p in parts], n, 0, build)


def _row_tile(rows, most=512, sublanes=16):
    return max(t for t in range(sublanes, most + 1, sublanes) if rows % t == 0)


def _partial_sum(name, own, recv):
    R, C = own.shape
    tr = _row_tile(R)

    def body(o_ref, r_ref, p_ref):
        p_ref[...] = ((o_ref[...] + r_ref[0].astype(F32)) + r_ref[1].astype(F32)) + r_ref[2].astype(F32)

    return pl.pallas_call(
        body, name=name, grid=(R // tr,),
        in_specs=[pl.BlockSpec((tr, C), lambda i: (i, 0)), pl.BlockSpec((3, tr, C), lambda i: (0, i, 0))],
        out_specs=pl.BlockSpec((tr, C), lambda i: (i, 0)),
        out_shape=jax.ShapeDtypeStruct((R, C), F32),
        compiler_params=_params(("parallel",)),
    )(own, recv)


def _adam_vals(w, g, m, v):
    m = ADAM_B1 * m + (1.0 - ADAM_B1) * g
    v = ADAM_B2 * v + (1.0 - ADAM_B2) * (g * g)
    m_hat = m / (1.0 - ADAM_B1 ** ADAM_STEP)
    v_hat = v / (1.0 - ADAM_B2 ** ADAM_STEP)
    delta = -ADAM_LR * (m_hat / (jnp.sqrt(v_hat) + ADAM_EPS) + ADAM_WD * w)
    return delta, m, v


def _adamw(name, w, m, v, mine, other, comm=None):
    R, C = w.shape
    tr = _row_tile(R)

    def body(w_ref, m_ref, v_ref, s_ref, n_ref, g_ref, d_ref, nm_ref, nv_ref):
        g = s_ref[...] + n_ref[...]
        d, nm, nv = _adam_vals(w_ref[...], g, m_ref[...], v_ref[...])
        g_ref[...], d_ref[...], nm_ref[...], nv_ref[...] = g, d, nm, nv

    spec = pl.BlockSpec((tr, C), lambda i: (i, 0))
    return _call(name, body, grid=(R // tr,), in_specs=[spec] * 5, out_specs=[spec] * 4,
                 out_shape=[jax.ShapeDtypeStruct((R, C), F32)] * 4, args=[w, m, v, mine, other],
                 semantics=("parallel",), comm=comm)


def _adamw_small(w, m, v, g_all):
    def body(w_ref, m_ref, v_ref, a_ref, g_ref, d_ref, nm_ref, nv_ref):
        g = a_ref[0]
        for dev in range(1, 8):
            g = g + a_ref[dev]
        d, nm, nv = _adam_vals(w_ref[...], g, m_ref[...], v_ref[...])
        g_ref[...], d_ref[...], nm_ref[...], nv_ref[...] = g, d, nm, nv

    return pl.pallas_call(
        body, name="adamw_small", out_shape=[jax.ShapeDtypeStruct(w.shape, F32)] * 4,
    )(w, m, v, g_all)


SMALL_LAYOUT = dict(norm_mix_g=(0, 1024), b_in=(8, 7424), hgrn_norm_g=(16, 1024), norm_ffn_g=(24, 1024),
                    norm_final_g=(32, 1024), hgrn_lb_logits=(40, 2048), attn_sinks=(48, 16))


def _pack_small(vals):
    parts = []
    for name, (_, nvalid) in SMALL_LAYOUT.items():
        flat = vals[name].astype(F32).reshape(-1)
        parts.append(jnp.pad(flat, (0, 8 * D_MODEL - nvalid)).reshape(8, D_MODEL))
    return jnp.concatenate(parts, axis=0)


def _unpack_small(packed, shapes):
    return {name: packed[r0:r0 + 8].reshape(-1)[:nvalid].reshape(shapes[name])
            for name, (r0, nvalid) in SMALL_LAYOUT.items()}


MATRICES = ("w_in", "w_branch_attn", "w_branch_hgrn", "w_out", "w_ffn_gate", "w_ffn_up", "w_ffn_down")
COLUMN_SHARDED = ("w_in", "w_ffn_gate", "w_ffn_up")
WEIGHTS = ("norm_mix_g", "w_in", "b_in", "attn_sinks", "hgrn_lb_logits", "hgrn_norm_g", "w_branch_attn",
           "w_branch_hgrn", "w_out", "norm_ffn_g", "w_ffn_gate", "w_ffn_up", "w_ffn_down", "norm_final_g")


def kernel(x, norm_mix_g, w_in, b_in, attn_sinks, hgrn_lb_logits, hgrn_norm_g, w_branch_attn, w_branch_hgrn, w_out, norm_ffn_g, w_ffn_gate, w_ffn_up, w_ffn_down, norm_final_g, loss_target, m_norm_mix_g, m_w_in, m_b_in, m_attn_sinks, m_hgrn_lb_logits, m_hgrn_norm_g, m_w_branch_attn, m_w_branch_hgrn, m_w_out, m_norm_ffn_g, m_w_ffn_gate, m_w_ffn_up, m_w_ffn_down, m_norm_final_g, v_norm_mix_g, v_w_in, v_b_in, v_attn_sinks, v_hgrn_lb_logits, v_hgrn_norm_g, v_w_branch_attn, v_w_branch_hgrn, v_w_out, v_norm_ffn_g, v_w_ffn_gate, v_w_ffn_up, v_w_ffn_down, v_norm_final_g):
    given = dict(locals())
    w = {n: given[n] for n in WEIGHTS}
    m = {n: given["m_" + n] for n in WEIGHTS}
    v = {n: given["v_" + n] for n in WEIGHTS}

    block = lambda a, n: jnp.transpose(a[0]) if n in COLUMN_SHARDED else a[0]
    unblock = lambda a, n: (jnp.transpose(a) if n in COLUMN_SHARDED else a)[None]
    net = _Net({n: block(w[n], n).astype(MXU_DTYPE) for n in MATRICES})
    net.gathered(("w_in",), _gather_two_level("gather_w_in", [net.shards["w_in"]]))
    vec = dict(norm_mix_g=norm_mix_g, b_in=b_in, attn_sinks=attn_sinks, hgrn_lb_logits=hgrn_lb_logits,
               hgrn_norm_g=hgrn_norm_g, norm_ffn_g=norm_ffn_g, norm_final_g=norm_final_g.reshape(1, D_MODEL))
    loss_part, dx, d_vecs = _local_step(x[0], loss_target[0], vec, net)

    grads, deltas, new_m, new_v = {}, {}, {}, {}
    carried = dict(w_ffn_down=net.swap(("w_in",)), w_ffn_gate=_small_copies(_pack_small(d_vecs)))
    for n in ("w_ffn_down", "w_ffn_gate", "w_ffn_up", "w_out", "w_branch_attn", "w_branch_hgrn", "w_in"):
        res, got = _adamw("adamw_" + n, block(w[n], n), block(m[n], n), block(v[n], n), net.sums[n], net.other[n],
                          comm=carried.get(n))
        if n == "w_ffn_down":
            net.other["w_in"], = got
        if n == "w_ffn_gate":
            small_all, = got
        grads[n], deltas[n], new_m[n], new_v[n] = (unblock(r, n) for r in res)
    shapes = {n: w[n].shape for n in SMALL_LAYOUT}
    res = _adamw_small(_pack_small(w), _pack_small(m), _pack_small(v), small_all)
    for dst, packed in zip((grads, deltas, new_m, new_v), res):
        dst.update(_unpack_small(packed, shapes))

    loss = lax.psum(loss_part, ("x", "y", "c"))
    return (loss, dx[None], *[grads[n] for n in WEIGHTS], *[deltas[n] for n in WEIGHTS],
            *[new_m[n] for n in WEIGHTS], *[new_v[n] for n in WEIGHTS])
```

```python
import functools
import math

import jax
import jax.numpy as jnp
from jax import lax
from jax.experimental import pallas as pl
from jax.experimental.pallas import tpu as pltpu

F32 = jnp.float32
BF16 = jnp.bfloat16
MXU_DTYPE = jnp.bfloat16
MESH_ID = pl.DeviceIdType.MESH

D_MODEL = 1024
HEAD_DIM = 64
Q_HEADS = 16
KV_HEADS = 2
GROUP = Q_HEADS // KV_HEADS
KV_WIDTH = KV_HEADS * HEAD_DIM
ATTN_BLOCK = 128
HGRN_HEADS = 8
HGRN_K = 128
CHUNK = 64
HGRN_TOKENS = 256
FFN = 2816
IN_SPLITS = (1024, 256, 4096, 2048)
EPS = 1e-6
NEG_INF = -1e30
ADAM_LR, ADAM_B1, ADAM_B2, ADAM_EPS, ADAM_WD, ADAM_STEP = 0.001, 0.9, 0.999, 1e-08, 0.01, 10
N_CHIPS = 4
VMEM_LIMIT = 60 * 1024 * 1024


def _params(sem=None):
    return pltpu.CompilerParams(dimension_semantics=sem, vmem_limit_bytes=VMEM_LIMIT)


def _sigmoid(v):
    return 0.5 * jnp.tanh(0.5 * v) + 0.5


def _dot(a, b, dims):
    return lax.dot_general(a.astype(MXU_DTYPE), b.astype(MXU_DTYPE), (dims, ((), ())),
                           preferred_element_type=F32)


def _nn(a, b):
    return _dot(a, b, ((1,), (0,)))


def _nt(a, b):
    return _dot(a, b, ((1,), (1,)))


def _tn(a, b):
    return _dot(a, b, ((0,), (0,)))


HBM_SPEC = pl.BlockSpec(memory_space=pl.ANY)


class _Carried:
    def __init__(self, arrays, out_shapes, n_remote, n_local, build):
        self.parts = [(len(arrays), len(out_shapes), build)]
        self.arrays, self.out_shapes = list(arrays), list(out_shapes)
        self.scratch = [pltpu.SemaphoreType.DMA((n_remote,)), pltpu.SemaphoreType.DMA((n_remote,)),
                        pltpu.SemaphoreType.DMA((max(n_local, 1),))]

    def __add__(self, other):
        both = _Carried([], [], 1, 0, None)
        both.parts = self.parts + other.parts
        both.arrays, both.out_shapes = self.arrays + other.arrays, self.out_shapes + other.out_shapes
        both.scratch = self.scratch + other.scratch
        return both

    def _built(self, ins, outs, sems):
        for p, (ni, no, build) in enumerate(self.parts):
            yield build(ins[:ni], outs[:no], *sems[3 * p:3 * p + 3])
            ins, outs = ins[ni:], outs[no:]

    def start(self, ins, outs, sems):
        for sends, _, local in self._built(ins, outs, sems):
            for cp in local + sends:
                cp.start()

    def wait(self, ins, outs, sems):
        for sends, recvs, local in self._built(ins, outs, sems):
            for cp in recvs:
                cp.wait_recv()
            for cp in sends:
                cp.wait_send()
            for cp in local:
                cp.wait()


def _join(*comms):
    comms = [c for c in comms if c is not None]
    return functools.reduce(lambda a, b: a + b, comms) if comms else None


def _call(name, body, *, grid, in_specs, out_specs, out_shape, args, scratch=(), semantics=None, comm=None):
    n_in, n_out, n_scr = len(in_specs), len(out_specs), len(scratch)
    if comm is None:
        res = pl.pallas_call(body, name=name, grid=grid, in_specs=in_specs, out_specs=out_specs, out_shape=out_shape,
                             scratch_shapes=list(scratch), compiler_params=_params(semantics))(*args)
        return list(res), []
    ci, co = len(comm.arrays), len(comm.out_shapes)

    def carrying(*refs):
        ins, refs = refs[:n_in], refs[n_in:]
        c_ins, refs = refs[:ci], refs[ci:]
        outs, refs = refs[:n_out], refs[n_out:]
        c_outs, refs = refs[:co], refs[co:]
        scr, sems = refs[:n_scr], refs[n_scr:]
        if not grid:
            comm.start(c_ins, c_outs, sems)
            body(*ins, *outs, *scr)
            comm.wait(c_ins, c_outs, sems)
            return
        first = functools.reduce(jnp.logical_and, [pl.program_id(a) == 0 for a in range(len(grid))])
        last = functools.reduce(jnp.logical_and, [pl.program_id(a) == g - 1 for a, g in enumerate(grid)])

        @pl.when(first)
        def _():
            comm.start(c_ins, c_outs, sems)

        body(*ins, *outs, *scr)

        @pl.when(last)
        def _():
            comm.wait(c_ins, c_outs, sems)

    res = pl.pallas_call(
        carrying, name=name, grid=grid, in_specs=list(in_specs) + [HBM_SPEC] * ci,
        out_specs=list(out_specs) + [HBM_SPEC] * co, out_shape=list(out_shape) + comm.out_shapes,
        scratch_shapes=list(scratch) + comm.scratch,
        compiler_params=_params(("arbitrary",) * len(grid) if grid else None),
    )(*args, *comm.arrays)
    return list(res[:n_out]), list(res[n_out:])


_NO_COPIES = object()


def _matmul(name, pairs, mode, *, tm, tn, tk, outs, extras=(), epilogue=None, comm=_NO_COPIES):
    prod = dict(nn=_nn, nt=_nt, tn=_tn)[mode]
    a0, b0 = pairs[0]
    M = a0.shape[1] if mode == "tn" else a0.shape[0]
    N = b0.shape[0] if mode == "nt" else b0.shape[1]
    steps, in_specs, offset = [], [], 0
    for a, b in pairs:
        K = a.shape[0] if mode == "tn" else a.shape[1]
        t = min(tk, K)
        assert K % t == 0, (name, K, t)
        kmap = functools.partial(lambda k, off, n: jnp.clip(k - off, 0, n - 1), off=offset, n=K // t)
        if mode == "tn":
            in_specs.append(pl.BlockSpec((t, tm), functools.partial(lambda i, j, k, f: (f(k), i), f=kmap)))
        else:
            in_specs.append(pl.BlockSpec((tm, t), functools.partial(lambda i, j, k, f: (i, f(k)), f=kmap)))
        if mode == "nt":
            in_specs.append(pl.BlockSpec((tn, t), functools.partial(lambda i, j, k, f: (j, f(k)), f=kmap)))
        else:
            in_specs.append(pl.BlockSpec((t, tn), functools.partial(lambda i, j, k, f: (f(k), j), f=kmap)))
        steps.append((offset, offset + K // t))
        offset += K // t
    assert M % tm == 0 and N % tn == 0, (name, M, N, tm, tn)
    ni, nj, nk = M // tm, N // tn, offset
    npair, ne, no = len(pairs), len(extras), len(outs)
    if epilogue is None:
        epilogue = lambda acc: (acc,)

    def finish(acc, extra_refs, out_refs):
        vals = epilogue(acc, *[r[...] for r in extra_refs])
        for (kind, _), o_ref, val in zip(outs, out_refs, vals):
            if kind == "pn":
                val = jnp.broadcast_to(val, o_ref.shape)
            o_ref[...] = val.astype(o_ref.dtype)

    def body(*refs):
        ab, rest = refs[:2 * npair], refs[2 * npair:]
        extra_refs, out_refs = rest[:ne], rest[ne:ne + no]
        if nk == 1:
            finish(prod(ab[0][...], ab[1][...]), extra_refs, out_refs)
            return
        acc_ref = rest[ne + no]
        k = pl.program_id(2)

        @pl.when(k == 0)
        def _():
            acc_ref[...] = jnp.zeros_like(acc_ref)

        for p, (lo, hi) in enumerate(steps):
            @pl.when(jnp.logical_and(k >= lo, k < hi))
            def _():
                acc_ref[...] += prod(ab[2 * p][...], ab[2 * p + 1][...])

        @pl.when(k == nk - 1)
        def _():
            finish(acc_ref[...], extra_refs, out_refs)

    for _, shape, im in extras:
        in_specs.append(pl.BlockSpec(shape, functools.partial(lambda i, j, k, im: im(i, j), im=im)))
    out_shape, out_specs = [], []
    for kind, dt in outs:
        if kind == "mn":
            out_shape.append(jax.ShapeDtypeStruct((M, N), dt))
            out_specs.append(pl.BlockSpec((tm, tn), lambda i, j, k: (i, j)))
        else:
            out_shape.append(jax.ShapeDtypeStruct((8 * ni, N), dt))
            out_specs.append(pl.BlockSpec((8, tn), lambda i, j, k: (i, j)))
    grid = (ni, nj, nk)
    if nj > 1 and nk == 1:
        turned = lambda spec: pl.BlockSpec(spec.block_shape, functools.partial(
            lambda j, i, k, im: im(i, j, k), im=spec.index_map))
        in_specs, out_specs, grid = [turned(s) for s in in_specs], [turned(s) for s in out_specs], (nj, ni, nk)
    res, got = _call(name, body, grid=grid, in_specs=in_specs, out_specs=out_specs, out_shape=out_shape,
                     args=[t for pair in pairs for t in pair] + [e[0] for e in extras],
                     scratch=[pltpu.VMEM((tm, tn), F32)] if nk > 1 else [],
                     semantics=("parallel", "parallel", "arbitrary"), comm=None if comm is _NO_COPIES else comm)
    return res if comm is _NO_COPIES else (res, got)


def _swiglu_fwd(u, w_gate_t, w_up_t, *, tm, tn, comm=None):
    (T, D), F = u.shape, w_gate_t.shape[0]

    def body(u_ref, wg_ref, wu_ref, g_ref, up_ref, z_ref):
        g, up = _nt(u_ref[...], wg_ref[...]), _nt(u_ref[...], wu_ref[...])
        g_ref[...], up_ref[...] = g, up
        z_ref[...] = (g * _sigmoid(g) * up).astype(z_ref.dtype)

    w_spec = pl.BlockSpec((tn, D), lambda j, i: (j, 0))
    o_spec = pl.BlockSpec((tm, tn), lambda j, i: (i, j))
    return _call("ffn_hidden", body, grid=(F // tn, T // tm),
                 in_specs=[pl.BlockSpec((tm, D), lambda j, i: (i, 0)), w_spec, w_spec], out_specs=[o_spec] * 3,
                 out_shape=[jax.ShapeDtypeStruct((T, F), F32)] * 2 + [jax.ShapeDtypeStruct((T, F), MXU_DTYPE)],
                 args=[u, w_gate_t, w_up_t], semantics=("parallel", "parallel"), comm=comm)


def _colsum_partials(p):
    return jnp.sum(p.reshape(-1, 8, p.shape[-1])[:, 0, :], axis=0, keepdims=True)


def _rmsnorm_fwd(name, x, g, tr=512):
    T, D = x.shape

    def body(x_ref, g_ref, u_ref):
        xv = x_ref[...]
        r = lax.rsqrt(jnp.mean(xv * xv, axis=-1, keepdims=True) + EPS)
        u_ref[...] = (xv * r * g_ref[...]).astype(u_ref.dtype)

    return pl.pallas_call(
        body, name=name, grid=(T // tr,),
        in_specs=[pl.BlockSpec((tr, D), lambda i: (i, 0)), pl.BlockSpec((1, D), lambda i: (0, 0))],
        out_specs=pl.BlockSpec((tr, D), lambda i: (i, 0)),
        out_shape=jax.ShapeDtypeStruct((T, D), MXU_DTYPE),
        compiler_params=_params(("parallel",)),
    )(x, g)


def _rmsnorm_bwd_vals(dy, xin, g):
    rstd = lax.rsqrt(jnp.mean(xin * xin, axis=-1, keepdims=True) + EPS)
    xhat = xin * rstd
    dg = jnp.sum(dy * xhat, axis=0, keepdims=True)
    dxh = dy * g
    dx = rstd * (dxh - xhat * jnp.mean(dxh * xhat, axis=-1, keepdims=True))
    return dx, dg


def _colsum(name, a, tr=512):
    T, N = a.shape

    def body(a_ref, o_ref):
        @pl.when(pl.program_id(0) == 0)
        def _():
            o_ref[...] = jnp.zeros_like(o_ref)

        o_ref[...] += jnp.sum(a_ref[...].astype(F32), axis=0, keepdims=True)

    return pl.pallas_call(
        body, name=name, grid=(T // tr,),
        in_specs=[pl.BlockSpec((tr, N), lambda i: (i, 0))],
        out_specs=pl.BlockSpec((1, N), lambda i: (0, 0)),
        out_shape=jax.ShapeDtypeStruct((1, N), F32),
        compiler_params=_params(("arbitrary",)),
    )(a)


ATTN_SCALE = 1.0 / math.sqrt(HEAD_DIM)
GROUP_LANES = GROUP * ATTN_BLOCK
PAIR = 2 * HEAD_DIM


def _attn_mask():
    kj = lax.broadcasted_iota(jnp.int32, (ATTN_BLOCK, GROUP_LANES), 0)
    qi = lax.broadcasted_iota(jnp.int32, (ATTN_BLOCK, GROUP_LANES), 1) & (ATTN_BLOCK - 1)
    return kj <= qi


def _heads_transposed(ref, g, scale=None):
    parts = []
    for a in range(GROUP // 2):
        lo = (g * GROUP // 2 + a) * PAIR
        pair = ref[:, lo:lo + PAIR]
        pair = (pair if scale is None else pair * scale).T
        parts += [pair[:HEAD_DIM], pair[HEAD_DIM:]]
    return jnp.concatenate(parts, axis=1).astype(MXU_DTYPE)


def _heads_back(ref, g, vt):
    for a in range(GROUP // 2):
        lo = (g * GROUP // 2 + a) * PAIR
        pair = jnp.concatenate([vt[:, (2 * a) * ATTN_BLOCK:(2 * a + 1) * ATTN_BLOCK],
                                vt[:, (2 * a + 1) * ATTN_BLOCK:(2 * a + 2) * ATTN_BLOCK]], axis=0)
        ref[:, lo:lo + PAIR] = pair.T.astype(ref.dtype)


def _kv_parts(kv_ref, g):
    ks = slice(g * HEAD_DIM, (g + 1) * HEAD_DIM)
    vs = slice(KV_WIDTH + g * HEAD_DIM, KV_WIDTH + (g + 1) * HEAD_DIM)
    return kv_ref[:, ks].astype(MXU_DTYPE), kv_ref[:, vs].astype(MXU_DTYPE)


def _sink_rows(sinks):
    return jnp.repeat(sinks.reshape(KV_HEADS, GROUP), ATTN_BLOCK, axis=1)


def _attn_fwd(pq, pkv, sinks, comm=None):
    T = pq.shape[0]
    nb = T // ATTN_BLOCK

    def body(q_ref, kvc_ref, kvp_ref, s_ref, y_ref, lse_ref):
        mask_c = _attn_mask()
        has_prev = pl.program_id(0) > 0
        for g in range(KV_HEADS):
            (kc, vc), (kp, vp) = _kv_parts(kvc_ref, g), _kv_parts(kvp_ref, g)
            qt = _heads_transposed(q_ref, g, ATTN_SCALE)
            s = jnp.where(mask_c, _nn(kc, qt), jnp.where(has_prev, _nn(kp, qt), NEG_INF))
            sink = s_ref[g:g + 1, :]
            m = jnp.maximum(jnp.max(s, axis=0, keepdims=True), sink)
            p = jnp.exp(s - m)
            den = jnp.sum(p, axis=0, keepdims=True) + jnp.exp(sink - m)
            pc = jnp.where(mask_c, p, 0.0)
            _heads_back(y_ref, g, (_tn(vc, pc) + _tn(vp, p - pc)) / den)
            lse = m + jnp.log(den)
            for i in range(GROUP):
                lse_ref[g * GROUP + i:g * GROUP + i + 1, :] = lse[:, i * ATTN_BLOCK:(i + 1) * ATTN_BLOCK]

    return _call(
        "attn_fwd", body, grid=(nb,),
        in_specs=[pl.BlockSpec((ATTN_BLOCK, D_MODEL), lambda n: (n, 0)),
                  pl.BlockSpec((ATTN_BLOCK, 2 * KV_WIDTH), lambda n: (n, 0)),
                  pl.BlockSpec((ATTN_BLOCK, 2 * KV_WIDTH), lambda n: (jnp.maximum(n - 1, 0), 0)),
                  pl.BlockSpec((KV_HEADS, GROUP_LANES), lambda n: (0, 0))],
        out_specs=[pl.BlockSpec((ATTN_BLOCK, D_MODEL), lambda n: (n, 0)),
                   pl.BlockSpec((Q_HEADS, ATTN_BLOCK), lambda n: (0, n))],
        out_shape=[jax.ShapeDtypeStruct((T, D_MODEL), MXU_DTYPE), jax.ShapeDtypeStruct((Q_HEADS, T), F32)],
        args=[pq, pkv, pkv, _sink_rows(sinks)], semantics=("parallel",), comm=comm)


def _attn_bwd(pq, pkv, sinks, lse, dy, comm=None):
    T = pq.shape[0]
    nb = T // ATTN_BLOCK
    cur = lambda n: (jnp.minimum(n, nb - 1), 0)

    def body(q_ref, kvc_ref, kvp_ref, s_ref, lse_ref, dy_ref, dq_ref, dkv_ref, ds_ref, carry, top, bot):
        n = pl.program_id(0)

        @pl.when(n == 0)
        def _():
            carry[...] = jnp.zeros_like(carry)
            ds_ref[...] = jnp.zeros_like(ds_ref)

        @pl.when(n < nb)
        def _():
            mask_c = _attn_mask()
            valid = jnp.logical_or(mask_c, n > 0)
            for g in range(KV_HEADS):
                ks = slice(g * HEAD_DIM, (g + 1) * HEAD_DIM)
                vs = slice(KV_WIDTH + g * HEAD_DIM, KV_WIDTH + (g + 1) * HEAD_DIM)
                (kc, vc), (kp, vp) = _kv_parts(kvc_ref, g), _kv_parts(kvp_ref, g)
                qt = _heads_transposed(q_ref, g, ATTN_SCALE)
                dot = _heads_transposed(dy_ref, g)
                lse = jnp.concatenate([lse_ref[g * GROUP + i:g * GROUP + i + 1, :] for i in range(GROUP)], axis=1)
                p = jnp.where(valid, jnp.exp(jnp.where(mask_c, _nn(kc, qt), _nn(kp, qt)) - lse), 0.0)
                dp = jnp.where(mask_c, _nn(vc, dot), _nn(vp, dot))
                delta = jnp.sum(p * dp, axis=0, keepdims=True)
                ds = p * (dp - delta)
                ds_c, p_c = jnp.where(mask_c, ds, 0.0), jnp.where(mask_c, p, 0.0)
                ds_p, p_p = ds - ds_c, p - p_c
                _heads_back(dq_ref, g, (_tn(kc, ds_c) + _tn(kp, ds_p)) * ATTN_SCALE)
                bot[:, ks], bot[:, vs] = _nt(ds_c, qt), _nt(p_c, dot)
                top[:, ks], top[:, vs] = _nt(ds_p, qt), _nt(p_p, dot)
                ds_ref[g:g + 1, :] -= jnp.exp(s_ref[g:g + 1, :] - lse) * delta
            dkv_ref[...] = (carry[...] + top[...]).astype(dkv_ref.dtype)
            carry[...] = bot[...]

        @pl.when(n == nb)
        def _():
            dkv_ref[...] = carry[...].astype(dkv_ref.dtype)

    return _call(
        "attn_bwd", body, grid=(nb + 1,),
        in_specs=[pl.BlockSpec((ATTN_BLOCK, D_MODEL), cur),
                  pl.BlockSpec((ATTN_BLOCK, 2 * KV_WIDTH), cur),
                  pl.BlockSpec((ATTN_BLOCK, 2 * KV_WIDTH), lambda n: (jnp.maximum(jnp.minimum(n, nb - 1) - 1, 0), 0)),
                  pl.BlockSpec((KV_HEADS, GROUP_LANES), lambda n: (0, 0)),
                  pl.BlockSpec((Q_HEADS, ATTN_BLOCK), lambda n: (0, jnp.minimum(n, nb - 1))),
                  pl.BlockSpec((ATTN_BLOCK, D_MODEL), cur)],
        out_specs=[pl.BlockSpec((ATTN_BLOCK, D_MODEL), cur),
                   pl.BlockSpec((ATTN_BLOCK, 2 * KV_WIDTH), lambda n: (jnp.maximum(n - 1, 0), 0)),
                   pl.BlockSpec((KV_HEADS, GROUP_LANES), lambda n: (0, 0))],
        out_shape=[jax.ShapeDtypeStruct((T, D_MODEL), MXU_DTYPE),
                   jax.ShapeDtypeStruct((T, 2 * KV_WIDTH), MXU_DTYPE),
                   jax.ShapeDtypeStruct((KV_HEADS, GROUP_LANES), F32)],
        scratch=[pltpu.VMEM((ATTN_BLOCK, 2 * KV_WIDTH), F32)] * 3,
        args=[pq, pkv, pkv, _sink_rows(sinks), lse, dy], semantics=("arbitrary",), comm=comm)


def _lower_bound(l):
    m = jnp.maximum(l[0:1], l[1:2])
    e0, e1 = jnp.exp(l[0:1] - m), jnp.exp(l[1:2] - m)
    return e0 / (e0 + e1)


def _tri(lower):
    r = lax.broadcasted_iota(jnp.int32, (CHUNK, CHUNK), 0)
    c = lax.broadcasted_iota(jnp.int32, (CHUNK, CHUNK), 1)
    return (r >= c) if lower else (c >= r)


def _chunk_sum(mask, v):
    return lax.dot_general(mask.astype(F32), v, (((1,), (0,)), ((), ())),
                           precision=lax.Precision.HIGHEST, preferred_element_type=F32)


def _hgrn_chunk_inputs(hq, hf, lb, causal):
    sg, sgn = _sigmoid(hf), _sigmoid(-hf)
    f = lb + (1.0 - lb) * sg
    kk = (1.0 - lb) * sgn
    sq = _sigmoid(hq)
    q = hq * sq
    b = _chunk_sum(causal, jnp.log(f))
    bm, bl = b[CHUNK // 2 - 1:CHUNK // 2, :], b[CHUNK - 1:CHUNK, :]
    e_qm, e_km, e_qs, e_kl = jnp.exp(b - bm), jnp.exp(bm - b), jnp.exp(b), jnp.exp(bl - b)
    return dict(sg=sg, sgn=sgn, f=f, kk=kk, sq=sq, q=q, e_qm=e_qm, e_km=e_km, e_qs=e_qs, e_kl=e_kl,
                qm=q * e_qm, km=kk * e_km, qs=q * e_qs, kl=kk * e_kl, el=jnp.exp(bl))


def _hgrn_fwd(ph, lb_logits, norm_g, comm=None):
    T = ph.shape[0]
    nblk, cpb = T // HGRN_TOKENS, HGRN_TOKENS // CHUNK
    col = lambda c: pl.BlockSpec((HGRN_TOKENS, D_MODEL), functools.partial(lambda i, c: (i, c), c=c))

    def body(hq_ref, hf_ref, hi_ref, hg_ref, l_ref, ng_ref, y_ref, o_ref, st_ref, s_ref):
        @pl.when(pl.program_id(0) == 0)
        def _():
            s_ref[...] = jnp.zeros_like(s_ref)

        lb = _lower_bound(l_ref[...])
        causal = _tri(True)
        for c in range(cpb):
            rows = slice(c * CHUNK, (c + 1) * CHUNK)
            t = _hgrn_chunk_inputs(hq_ref[rows, :], hf_ref[rows, :], lb, causal)
            qm, km, qs, kl = (t[n].astype(MXU_DTYPE) for n in ("qm", "km", "qs", "kl"))
            v = hi_ref[rows, :].astype(MXU_DTYPE)
            for h in range(HGRN_HEADS):
                ls = slice(h * HGRN_K, (h + 1) * HGRN_K)
                st = s_ref[h]
                st_ref[c, ls, :] = st
                a = jnp.where(causal, _nt(qm[:, ls], km[:, ls]), 0.0)
                o_ref[rows, ls] = _nn(a, v[:, ls]) + _nt(qs[:, ls], st)
                s_ref[h] = t["el"][:, ls] * st + _tn(v[:, ls], kl[:, ls])
        for h in range(HGRN_HEADS):
            ls = slice(h * HGRN_K, (h + 1) * HGRN_K)
            o = o_ref[:, ls]
            r = lax.rsqrt(jnp.mean(o * o, axis=-1, keepdims=True) + EPS)
            y_ref[:, ls] = (o * r * ng_ref[:, ls] * _sigmoid(hg_ref[:, ls])).astype(y_ref.dtype)

    return _call(
        "hgrn_fwd", body, grid=(nblk,),
        in_specs=[col(0), col(1), col(2), col(3),
                  pl.BlockSpec((2, D_MODEL), lambda i: (0, 0)), pl.BlockSpec((1, D_MODEL), lambda i: (0, 0))],
        out_specs=[pl.BlockSpec((HGRN_TOKENS, D_MODEL), lambda i: (i, 0)),
                   pl.BlockSpec((HGRN_TOKENS, D_MODEL), lambda i: (i, 0)),
                   pl.BlockSpec((cpb, D_MODEL, HGRN_K), lambda i: (i, 0, 0))],
        out_shape=[jax.ShapeDtypeStruct((T, D_MODEL), MXU_DTYPE), jax.ShapeDtypeStruct((T, D_MODEL), F32),
                   jax.ShapeDtypeStruct((T // CHUNK, D_MODEL, HGRN_K), F32)],
        scratch=[pltpu.VMEM((HGRN_HEADS, HGRN_K, HGRN_K), F32)],
        args=[ph, ph, ph, ph, lb_logits, norm_g], semantics=("arbitrary",), comm=comm)


def _hgrn_bwd(ph, o_raw, states, dy, lb_logits, norm_g, comm=None):
    T = ph.shape[0]
    nblk, cpb = T // HGRN_TOKENS, HGRN_TOKENS // CHUNK
    rev = lambda i: nblk - 1 - i
    col = lambda c: pl.BlockSpec((HGRN_TOKENS, D_MODEL), functools.partial(lambda i, c: (rev(i), c), c=c))
    tok = pl.BlockSpec((HGRN_TOKENS, D_MODEL), lambda i: (rev(i), 0))

    def body(hq_ref, hf_ref, hi_ref, hg_ref, o_ref, st_ref, dy_ref, l_ref, ng_ref,
             dph_ref, dng_ref, dl_ref, dst_ref, dlb_ref, do_s, dqm_s, dkm_s, dqs_s, dkl_s, dv_s, del_s):
        i = pl.program_id(0)

        @pl.when(i == 0)
        def _():
            dst_ref[...] = jnp.zeros_like(dst_ref)
            dlb_ref[...] = jnp.zeros_like(dlb_ref)
            dng_ref[...] = jnp.zeros_like(dng_ref)

        lb = _lower_bound(l_ref[...])
        causal, anti = _tri(True), _tri(False)
        row = lax.broadcasted_iota(jnp.int32, (CHUNK, D_MODEL), 0)
        for c in reversed(range(cpb)):
            rows = slice(c * CHUNK, (c + 1) * CHUNK)
            hq = hq_ref[rows, :]
            t = _hgrn_chunk_inputs(hq, hf_ref[rows, :], lb, causal)
            sgg = _sigmoid(hg_ref[rows, :])
            dyv = dy_ref[rows, :]
            for h in range(HGRN_HEADS):
                ls = slice(h * HGRN_K, (h + 1) * HGRN_K)
                o = o_ref[rows, ls]
                r = lax.rsqrt(jnp.mean(o * o, axis=-1, keepdims=True) + EPS)
                nrm = o * r
                g_h = sgg[:, ls]
                dph_ref[rows, 3 * D_MODEL + h * HGRN_K:3 * D_MODEL + (h + 1) * HGRN_K] = (
                    dyv[:, ls] * nrm * ng_ref[:, ls] * g_h * (1.0 - g_h)).astype(dph_ref.dtype)
                dyg = dyv[:, ls] * g_h
                dng_ref[:, ls] += jnp.sum(dyg * nrm, axis=0, keepdims=True)
                dn = dyg * ng_ref[:, ls]
                do_s[:, ls] = r * (dn - nrm * jnp.mean(dn * nrm, axis=-1, keepdims=True))
            qm, km, qs, kl = (t[n].astype(MXU_DTYPE) for n in ("qm", "km", "qs", "kl"))
            v = hi_ref[rows, :].astype(MXU_DTYPE)
            do = do_s[...].astype(MXU_DTYPE)
            for h in range(HGRN_HEADS):
                ls = slice(h * HGRN_K, (h + 1) * HGRN_K)
                st = st_ref[c, ls, :]
                dst = dst_ref[h]
                a = jnp.where(causal, _nt(qm[:, ls], km[:, ls]), 0.0)
                da = jnp.where(causal, _nt(do[:, ls], v[:, ls]), 0.0)
                dv_s[:, ls] = _tn(a, do[:, ls]) + _nt(kl[:, ls], dst)
                dkl_s[:, ls] = _nn(v[:, ls], dst)
                dqs_s[:, ls] = _nn(do[:, ls], st)
                del_s[:, ls] = jnp.sum(dst * st, axis=0, keepdims=True)
                dst_ref[h] = _tn(do[:, ls], qs[:, ls]) + t["el"][:, ls] * dst
                dqm_s[:, ls] = _nn(da, km[:, ls])
                dkm_s[:, ls] = _tn(da, qm[:, ls])
            dqm, dkm, dqs, dkl = dqm_s[...], dkm_s[...], dqs_s[...], dkl_s[...]
            dq = dqm * t["e_qm"] + dqs * t["e_qs"]
            dk = dkm * t["e_km"] + dkl * t["e_kl"]
            t_qm, t_km, t_kl = dqm * t["qm"], dkm * t["km"], dkl * t["kl"]
            db = t_qm - t_km + dqs * t["qs"] - t_kl
            db_mid = jnp.sum(t_km - t_qm, axis=0, keepdims=True)
            db_last = jnp.sum(t_kl, axis=0, keepdims=True) + del_s[...] * t["el"]
            db = db + jnp.where(row == CHUNK // 2 - 1, db_mid, 0.0) + jnp.where(row == CHUNK - 1, db_last, 0.0)
            dlogf = _chunk_sum(anti, db)
            sq, sg, sgn, f = t["sq"], t["sg"], t["sgn"], t["f"]
            dph_ref[rows, 0:D_MODEL] = (dq * (sq * (1.0 + hq * (1.0 - sq)))).astype(dph_ref.dtype)
            dph_ref[rows, D_MODEL:2 * D_MODEL] = (
                dlogf * (1.0 - lb) * sg * (1.0 - sg) / f - dk * (1.0 - lb) * sgn * (1.0 - sgn)).astype(dph_ref.dtype)
            dph_ref[rows, 2 * D_MODEL:3 * D_MODEL] = dv_s[...].astype(dph_ref.dtype)
            dlb_ref[...] += jnp.sum(dlogf * (1.0 - sg) / f - dk * sgn, axis=0, keepdims=True)

        @pl.when(i == nblk - 1)
        def _():
            dl0 = dlb_ref[...] * lb * (1.0 - lb)
            dl_ref[0:1, :] = dl0
            dl_ref[1:2, :] = -dl0

    wide = pltpu.VMEM((CHUNK, D_MODEL), F32)
    return _call(
        "hgrn_bwd", body, grid=(nblk,),
        in_specs=[col(0), col(1), col(2), col(3), tok,
                  pl.BlockSpec((cpb, D_MODEL, HGRN_K), lambda i: (rev(i), 0, 0)), tok,
                  pl.BlockSpec((2, D_MODEL), lambda i: (0, 0)), pl.BlockSpec((1, D_MODEL), lambda i: (0, 0))],
        out_specs=[pl.BlockSpec((HGRN_TOKENS, 4 * D_MODEL), lambda i: (rev(i), 0)),
                   pl.BlockSpec((1, D_MODEL), lambda i: (0, 0)), pl.BlockSpec((2, D_MODEL), lambda i: (0, 0))],
        out_shape=[jax.ShapeDtypeStruct((T, 4 * D_MODEL), MXU_DTYPE), jax.ShapeDtypeStruct((1, D_MODEL), F32),
                   jax.ShapeDtypeStruct((2, D_MODEL), F32)],
        scratch=[pltpu.VMEM((HGRN_HEADS, HGRN_K, HGRN_K), F32), pltpu.VMEM((1, D_MODEL), F32),
                 wide, wide, wide, wide, wide, wide, pltpu.VMEM((1, D_MODEL), F32)],
        args=[ph, ph, ph, ph, o_raw, states, dy, lb_logits, norm_g], semantics=("arbitrary",), comm=comm)


def _local_step(x, target, vec, net):
    T, D = x.shape
    norm_mix_g, b_in, sinks, lb_logits = vec["norm_mix_g"], vec["b_in"], vec["attn_sinks"], vec["hgrn_lb_logits"]
    hgrn_norm_g, norm_ffn_g, norm_final_g = vec["hgrn_norm_g"], vec["norm_ffn_g"], vec["norm_final_g"]
    w_in = net.full("w_in")
    o_q, o_kv, o_h, o_g = (sum(IN_SPLITS[:i]) for i in range(4))
    w_q, w_kv, w_h, w_g = (w_in[o:o + n] for o, n in zip((o_q, o_kv, o_h, o_g), IN_SPLITS))
    b_q, b_kv, b_h, b_g = (b_in[:, o:o + n] for o, n in zip((o_q, o_kv, o_h, o_g), IN_SPLITS))
    bias = lambda acc, b: (acc + b,)
    both = lambda acc: (acc, acc)
    grad_outs = [("mn", F32), ("mn", MXU_DTYPE)]
    row_vec = lambda n: ((1, n), lambda i, j: (0, j))
    tile = lambda tm, tn: ((tm, tn), lambda i, j: (i, j))
    TM = 512
    BIG = min(T, 1024)

    u = _rmsnorm_fwd("norm_mix", x, norm_mix_g)
    pq, = _matmul("in_q", [(u, w_q)], "nt", tm=TM, tn=1024, tk=1024, outs=[("mn", F32)],
                  extras=[(b_q, *row_vec(1024))], epilogue=bias)
    pkv, = _matmul("in_kv", [(u, w_kv)], "nt", tm=TM, tn=256, tk=1024, outs=[("mn", F32)],
                   extras=[(b_kv, *row_vec(256))], epilogue=bias)
    names = ("w_branch_attn", "w_branch_hgrn", "w_out")
    (ph,), got = _matmul("in_h", [(u, w_h)], "nt", tm=TM, tn=1024, tk=1024, outs=[("mn", F32)],
                         extras=[(b_h, *row_vec(1024))], epilogue=bias, comm=net.gather(names))
    net.gathered(names, got)
    pg, = _matmul("in_g", [(u, w_g)], "nt", tm=TM, tn=1024, tk=1024, outs=[("mn", F32)],
                  extras=[(b_g, *row_vec(1024))], epilogue=bias)
    names = ("w_ffn_gate",)
    (y_attn, lse), got = _attn_fwd(pq, pkv, sinks, comm=net.gather(names))
    net.gathered(names, got)
    names = ("w_ffn_up",)
    (y_hgrn, o_raw, states), got = _hgrn_fwd(ph, lb_logits, hgrn_norm_g, comm=net.gather(names))
    net.gathered(names, got)
    w_ba, w_bh, w_out = net.full("w_branch_attn"), net.full("w_branch_hgrn"), net.full("w_out")
    w_gate, w_up = net.full("w_ffn_gate"), net.full("w_ffn_up")
    ya, = _matmul("branch_a", [(y_attn, w_ba)], "nn", tm=TM, tn=1024, tk=1024, outs=[("mn", F32)])
    gate_a = (pg, (TM, 1024), lambda i, j: (i, 0))
    gate_b = (pg, (TM, 1024), lambda i, j: (i, 1))

    def merge(acc, ya_v, ga, gb):
        return acc, _sigmoid(ga) * ya_v + _sigmoid(gb) * acc

    yb, merged = _matmul("branch_b", [(y_hgrn, w_bh)], "nn", tm=TM, tn=1024, tk=1024,
                         outs=[("mn", F32), ("mn", MXU_DTYPE)],
                         extras=[(ya, *tile(TM, 1024)), gate_a, gate_b], epilogue=merge)

    def resid_norm(acc, xin, g):
        h = xin + acc
        r = lax.rsqrt(jnp.mean(h * h, axis=-1, keepdims=True) + EPS)
        return h, h * r * g

    h1, u2 = _matmul("out_proj", [(merged, w_out)], "nn", tm=TM, tn=1024, tk=1024,
                     outs=[("mn", F32), ("mn", MXU_DTYPE)],
                     extras=[(x, *tile(TM, 1024)), (norm_ffn_g, *row_vec(1024))], epilogue=resid_norm)
    FT = FFN // 2
    names = ("w_ffn_down",)
    (gpre, up, z), got = _swiglu_fwd(u2, w_gate, w_up, tm=TM, tn=FT, comm=net.gather(names))
    net.gathered(names, got)
    w_down = net.full("w_ffn_down")

    def loss_head(acc, h1_v, tgt, g):
        h2 = h1_v + acc
        r = lax.rsqrt(jnp.mean(h2 * h2, axis=-1, keepdims=True) + EPS)
        xhat = h2 * r
        err = xhat * g - tgt
        part = 0.5 * jnp.sum(jnp.sum(err * err, axis=-1, keepdims=True), axis=0, keepdims=True) / D
        dyv = err / D
        dxh = dyv * g
        dh2 = r * (dxh - xhat * jnp.mean(dxh * xhat, axis=-1, keepdims=True))
        return dh2, dh2, jnp.sum(dyv * xhat, axis=0, keepdims=True), jnp.broadcast_to(part, (1, D))

    dh2, dh2b, dgf_p, loss_p = _matmul(
        "ffn_down", [(z, w_down)], "nn", tm=TM, tn=1024, tk=FFN,
        outs=[("mn", F32), ("mn", MXU_DTYPE), ("pn", F32), ("pn", F32)],
        extras=[(h1, *tile(TM, 1024)), (target, *tile(TM, 1024)), (norm_final_g, *row_vec(1024))],
        epilogue=loss_head)
    loss = jnp.sum(loss_p.reshape(-1, 8, D)[:, 0, 0])
    d_norm_final = _colsum_partials(dgf_p)

    def swiglu_bwd(acc, gv, upv):
        s = _sigmoid(gv)
        return acc * upv * (s * (1.0 + gv * (1.0 - s))), acc * (gv * s)

    dgp, dup = _matmul("d_ffn_hidden", [(dh2b, w_down)], "nt", tm=TM, tn=FT, tk=1024,
                       outs=[("mn", MXU_DTYPE), ("mn", MXU_DTYPE)],
                       extras=[(gpre, *tile(TM, FT)), (up, *tile(TM, FT))], epilogue=swiglu_bwd)
    d_w_down = _matmul("dw_down", [(z, dh2b)], "tn", tm=FT, tn=1024, tk=1024, outs=grad_outs, epilogue=both)

    def norm_ffn_bwd(acc, h1_v, g, dres):
        dx, dg = _rmsnorm_bwd_vals(acc, h1_v, g)
        dh = dres + dx
        return dh, dh, dg

    names = ("w_ffn_down",)
    (dh1, dh1b, dg2_p), got = _matmul(
        "d_ffn_in", [(dgp, w_gate), (dup, w_up)], "nn", tm=BIG, tn=1024, tk=FT,
        outs=[("mn", F32), ("mn", MXU_DTYPE), ("pn", F32)],
        extras=[(h1, *tile(BIG, 1024)), (norm_ffn_g, *row_vec(1024)), (dh2, *tile(BIG, 1024))],
        epilogue=norm_ffn_bwd,
        comm=net.exchange(dict(w_ffn_down=[d_w_down])))
    net.received(names, (), got)
    d_norm_ffn = _colsum_partials(dg2_p)
    d_w_gate = _matmul("dw_gate", [(dgp, u2)], "tn", tm=FT, tn=1024, tk=1024, outs=grad_outs, epilogue=both)
    d_w_up = _matmul("dw_up", [(dup, u2)], "tn", tm=FT, tn=1024, tk=1024, outs=grad_outs, epilogue=both)

    def merge_bwd(acc, ya_v, yb_v, ga, gb):
        sa, sb = _sigmoid(ga), _sigmoid(gb)
        return acc * sa, acc * sb, acc * ya_v * sa * (1.0 - sa), acc * yb_v * sb * (1.0 - sb)

    dya, dyb, dga, dgb = _matmul(
        "d_merged", [(dh1b, w_out)], "nt", tm=TM, tn=1024, tk=1024, outs=[("mn", MXU_DTYPE)] * 4,
        extras=[(ya, *tile(TM, 1024)), (yb, *tile(TM, 1024)), gate_a, gate_b], epilogue=merge_bwd)
    tn_grad = functools.partial(_matmul, mode="tn", tn=1024, tk=1024, outs=grad_outs, epilogue=both)
    d_w_out = tn_grad("dw_out", [(merged, dh1b)], tm=1024)
    dy_attn, = _matmul("d_y_attn", [(dya, w_ba)], "nt", tm=TM, tn=1024, tk=1024, outs=[("mn", F32)])
    dy_hgrn, = _matmul("d_y_hgrn", [(dyb, w_bh)], "nt", tm=TM, tn=1024, tk=1024, outs=[("mn", F32)])
    d_w_ba = tn_grad("dw_branch_a", [(y_attn, dya)], tm=1024)
    d_w_bh = tn_grad("dw_branch_b", [(y_hgrn, dyb)], tm=1024)

    names, swap = ("w_ffn_gate",), ("w_ffn_down",)
    (dq, dkv, dsink), got = _attn_bwd(pq, pkv, sinks, lse, dy_attn,
                                      comm=net.exchange(dict(w_ffn_gate=[d_w_gate]), swap))
    net.received(names, swap, got)
    names, swap = ("w_ffn_up", "w_out", "w_branch_attn", "w_branch_hgrn"), ("w_ffn_gate",)
    (dph, d_hgrn_norm, d_lb_logits), got = _hgrn_bwd(
        ph, o_raw, states, dy_hgrn, lb_logits, hgrn_norm_g,
        comm=net.exchange(dict(w_ffn_up=[d_w_up], w_out=[d_w_out], w_branch_attn=[d_w_ba],
                               w_branch_hgrn=[d_w_bh]), swap))
    net.received(names, swap, got)

    d_w_in = [tn_grad("dw_in_q", [(dq, u)], tm=1024), tn_grad("dw_in_kv", [(dkv, u)], tm=256),
              tn_grad("dw_in_h", [(dph, u)], tm=1024), tn_grad("dw_in_ga", [(dga, u)], tm=1024),
              tn_grad("dw_in_gb", [(dgb, u)], tm=1024)]

    def norm_mix_bwd(acc, xin, g, dres):
        dx, dg = _rmsnorm_bwd_vals(acc, xin, g)
        return dres + dx, dg

    names, swap = ("w_in",), ("w_ffn_up", "w_out", "w_branch_attn", "w_branch_hgrn")
    (dx, dg1_p), got = _matmul(
        "d_u", [(dq, w_q), (dkv, w_kv), (dph, w_h), (dga, w_g[:D]), (dgb, w_g[D:])], "nn",
        tm=BIG, tn=1024, tk=512, outs=[("mn", F32), ("pn", F32)],
        extras=[(x, *tile(BIG, 1024)), (norm_mix_g, *row_vec(1024)), (dh1, *tile(BIG, 1024))],
        epilogue=norm_mix_bwd,
        comm=net.exchange(dict(w_in=d_w_in), swap))
    net.last = (names, swap, got)
    d_norm_mix = _colsum_partials(dg1_p)
    d_b_in = jnp.concatenate([_colsum("db_q", dq), _colsum("db_kv", dkv), _colsum("db_h", dph),
                              _colsum("db_ga", dga), _colsum("db_gb", dgb)], axis=1)
    vecs = dict(norm_mix_g=d_norm_mix, b_in=d_b_in, attn_sinks=jnp.sum(dsink.reshape(Q_HEADS, ATTN_BLOCK), axis=1).reshape(1, Q_HEADS),
                hgrn_lb_logits=d_lb_logits,
                hgrn_norm_g=d_hgrn_norm, norm_ffn_g=d_norm_ffn, norm_final_g=d_norm_final)
    return loss, dx, vecs


def _place():
    return lax.axis_index("x"), lax.axis_index("y"), lax.axis_index("c")


def _other_chips(x, y):
    return [(1 - x, y), (x, 1 - y), (1 - x, 1 - y)]


def _gather_copies(shards):
    n = len(shards)

    def build(ins, outs, send_sems, recv_sems, local_sems):
        x, y, c = _place()
        mine = 2 * x + y
        local = [pltpu.make_async_copy(ins[w], outs[w].at[mine], local_sems.at[w]) for w in range(n)]
        sends, recvs = [], []
        for w in range(n):
            for k, (px, py) in enumerate(_other_chips(x, y)):
                sem = 3 * w + k
                sends.append(pltpu.make_async_remote_copy(
                    src_ref=ins[w], dst_ref=outs[w].at[mine], send_sem=send_sems.at[sem], recv_sem=recv_sems.at[sem],
                    device_id=(px, py, c), device_id_type=MESH_ID))
                recvs.append(pltpu.make_async_remote_copy(
                    src_ref=ins[w], dst_ref=outs[w].at[2 * px + py], send_sem=send_sems.at[sem],
                    recv_sem=recv_sems.at[sem], device_id=(px, py, c), device_id_type=MESH_ID))
        return sends, recvs, local

    return _Carried(shards, [jax.ShapeDtypeStruct((N_CHIPS,) + s.shape, s.dtype) for s in shards], 3 * n, n, build)


def _grad_copies(stacked):
    n = len(stacked)

    def build(ins, outs, send_sems, recv_sems, local_sems):
        x, y, c = _place()
        sends = []
        for w in range(n):
            for k, (px, py) in enumerate(_other_chips(x, y)):
                sem = 3 * w + k
                sends.append(pltpu.make_async_remote_copy(
                    src_ref=ins[w].at[2 * px + py], dst_ref=outs[w].at[k], send_sem=send_sems.at[sem],
                    recv_sem=recv_sems.at[sem], device_id=(px, py, c), device_id_type=MESH_ID))
        return sends, sends, []

    return _Carried(stacked, [jax.ShapeDtypeStruct((3,) + s.shape[1:], s.dtype) for s in stacked], 3 * n, 0, build)


def _small_copies(small):
    def build(ins, outs, send_sems, recv_sems, local_sems):
        small_ref, all_ref = ins[0], outs[0]
        x, y, c = _place()
        me = 4 * x + 2 * y + c
        sends, recvs = [], []
        for r in range(1, 8):
            px = 1 - x if r & 4 else x
            py = 1 - y if r & 2 else y
            pc = 1 - c if r & 1 else c
            sends.append(pltpu.make_async_remote_copy(
                src_ref=small_ref, dst_ref=all_ref.at[me], send_sem=send_sems.at[r - 1], recv_sem=recv_sems.at[r - 1],
                device_id=(px, py, pc), device_id_type=MESH_ID))
            recvs.append(pltpu.make_async_remote_copy(
                src_ref=small_ref, dst_ref=all_ref.at[4 * px + 2 * py + pc], send_sem=send_sems.at[r - 1],
                recv_sem=recv_sems.at[r - 1], device_id=(px, py, pc), device_id_type=MESH_ID))
        return sends, recvs, [pltpu.make_async_copy(small_ref, all_ref.at[me], local_sems.at[0])]

    return _Carried([small], [jax.ShapeDtypeStruct((8,) + small.shape, small.dtype)], 7, 1, build)


def _gather_two_level(name, shards):
    n = len(shards)

    def body(*refs):
        ins, outs = refs[:n], refs[n:2 * n]
        send1, recv1, send2, recv2, local_sems = refs[2 * n:]
        x, y, c = _place()
        mine = 2 * x + y
        local = [pltpu.make_async_copy(ins[w], outs[w].at[mine], local_sems.at[w]) for w in range(n)]
        first, landed, relay, relayed = [], [], [], []
        for w in range(n):
            half = shards[w].shape[0] // 2
            my_half, other_half = pl.ds(c * half, half), pl.ds((1 - c) * half, half)
            for k, (px, py) in enumerate(_other_chips(x, y)):
                sem, chip = 3 * w + k, 2 * px + py
                first.append(pltpu.make_async_remote_copy(
                    src_ref=ins[w].at[my_half], dst_ref=outs[w].at[mine, my_half], send_sem=send1.at[sem],
                    recv_sem=recv1.at[sem], device_id=(px, py, c), device_id_type=MESH_ID))
                landed.append(pltpu.make_async_remote_copy(
                    src_ref=ins[w].at[my_half], dst_ref=outs[w].at[chip, my_half], send_sem=send1.at[sem],
                    recv_sem=recv1.at[sem], device_id=(px, py, c), device_id_type=MESH_ID))
                relay.append(pltpu.make_async_remote_copy(
                    src_ref=outs[w].at[chip, my_half], dst_ref=outs[w].at[chip, my_half], send_sem=send2.at[sem],
                    recv_sem=recv2.at[sem], device_id=(x, y, 1 - c), device_id_type=MESH_ID))
                relayed.append(pltpu.make_async_remote_copy(
                    src_ref=outs[w].at[chip, other_half], dst_ref=outs[w].at[chip, other_half], send_sem=send2.at[sem],
                    recv_sem=recv2.at[sem], device_id=(x, y, 1 - c), device_id_type=MESH_ID))
        for cp in local + first:
            cp.start()
        for arrived, onward in zip(landed, relay):
            arrived.wait_recv()
            onward.start()
        for cp in relayed:
            cp.wait_recv()
        for cp in first + relay:
            cp.wait_send()
        for cp in local:
            cp.wait()

    sems = pltpu.SemaphoreType.DMA((3 * n,))
    return pl.pallas_call(
        body, name=name, in_specs=[HBM_SPEC] * n, out_specs=[HBM_SPEC] * n,
        out_shape=[jax.ShapeDtypeStruct((N_CHIPS,) + s.shape, s.dtype) for s in shards],
        scratch_shapes=[sems, sems, sems, sems, pltpu.SemaphoreType.DMA((n,))],
    )(*shards)


def _copies_alone(name, comm):
    return _call(name, lambda: None, grid=(), in_specs=[], out_specs=[], out_shape=[], args=[], comm=comm)[1]


class _Net:
    def __init__(self, shards):
        self.shards = shards
        self.whole, self.own, self.theirs, self.sums, self.other = {}, {}, {}, {}, {}
        x, y, _ = _place()
        self.chip = 2 * x + y

    def gather(self, names):
        return _gather_copies([self.shards[n] for n in names])

    def gathered(self, names, got):
        for n, g in zip(names, got):
            self.whole[n] = g.reshape(-1, g.shape[-1])

    def full(self, name):
        return self.whole[name]

    def exchange(self, grads, swap=()):
        stacked = []
        for n, pieces in grads.items():
            keep = jnp.concatenate([p[0] for p in pieces], axis=0) if len(pieces) > 1 else pieces[0][0]
            send = jnp.concatenate([p[1] for p in pieces], axis=0) if len(pieces) > 1 else pieces[0][1]
            rows = keep.shape[0] // N_CHIPS
            self.own[n] = lax.dynamic_slice_in_dim(keep, self.chip * rows, rows, axis=0)
            stacked.append(send.reshape(N_CHIPS, rows, send.shape[-1]))
        return _join(_grad_copies(stacked), self.swap(swap))

    def swap(self, names):
        return _sibling_copies([self.sums[n] for n in names]) if names else None

    def received(self, names, swap, got, carried=None):
        self.theirs.update(zip(names, got[:len(names)]))
        self.other.update(zip(swap, got[len(names):]))
        for n in names:
            (self.sums[n],), more = _partial_sum("sum_" + n, self.own[n], self.theirs[n], comm=carried)
        return more


def _sibling_copies(parts):
    n = len(parts)

    def build(ins, outs, send_sems, recv_sems, local_sems):
        x, y, c = _place()
        copies = [pltpu.make_async_remote_copy(
            src_ref=ins[w], dst_ref=outs[w], send_sem=send_sems.at[w], recv_sem=recv_sems.at[w],
            device_id=(x, y, 1 - c), device_id_type=MESH_ID) for w in range(n)]
        return copies, copies, []

    return _Carried(parts, [jax.ShapeDtypeStruct(p.shape, p.dtype) for p in parts], n, 0, build)


def _row_tile(rows, most=512, sublanes=16):
    return max(t for t in range(sublanes, most + 1, sublanes) if rows % t == 0)


def _partial_sum(name, own, recv, comm=None):
    R, C = own.shape
    tr = _row_tile(R, most=128)

    def body(o_ref, r_ref, p_ref):
        p_ref[...] = ((o_ref[...] + r_ref[0].astype(F32)) + r_ref[1].astype(F32)) + r_ref[2].astype(F32)

    return _call(name, body, grid=(R // tr,),
                 in_specs=[pl.BlockSpec((tr, C), lambda i: (i, 0)), pl.BlockSpec((3, tr, C), lambda i: (0, i, 0))],
                 out_specs=[pl.BlockSpec((tr, C), lambda i: (i, 0))], out_shape=[jax.ShapeDtypeStruct((R, C), F32)],
                 args=[own, recv], semantics=("parallel",), comm=comm)


def _adam_vals(w, g, m, v):
    m = ADAM_B1 * m + (1.0 - ADAM_B1) * g
    v = ADAM_B2 * v + (1.0 - ADAM_B2) * (g * g)
    m_hat = m / (1.0 - ADAM_B1 ** ADAM_STEP)
    v_hat = v / (1.0 - ADAM_B2 ** ADAM_STEP)
    delta = -ADAM_LR * (m_hat / (jnp.sqrt(v_hat) + ADAM_EPS) + ADAM_WD * w)
    return delta, m, v


def _adamw(name, w, m, v, mine, other, comm=None):
    R, C = w.shape
    tr = _row_tile(R)

    def body(w_ref, m_ref, v_ref, s_ref, n_ref, g_ref, d_ref, nm_ref, nv_ref):
        g = s_ref[...] + n_ref[...]
        d, nm, nv = _adam_vals(w_ref[...], g, m_ref[...], v_ref[...])
        g_ref[...], d_ref[...], nm_ref[...], nv_ref[...] = g, d, nm, nv

    spec = pl.BlockSpec((tr, C), lambda i: (i, 0))
    return _call(name, body, grid=(R // tr,), in_specs=[spec] * 5, out_specs=[spec] * 4,
                 out_shape=[jax.ShapeDtypeStruct((R, C), F32)] * 4, args=[w, m, v, mine, other],
                 semantics=("parallel",), comm=comm)


def _adamw_small(w, m, v, g_all):
    def body(w_ref, m_ref, v_ref, a_ref, g_ref, d_ref, nm_ref, nv_ref):
        g = a_ref[0]
        for dev in range(1, 8):
            g = g + a_ref[dev]
        d, nm, nv = _adam_vals(w_ref[...], g, m_ref[...], v_ref[...])
        g_ref[...], d_ref[...], nm_ref[...], nv_ref[...] = g, d, nm, nv

    return pl.pallas_call(
        body, name="adamw_small", out_shape=[jax.ShapeDtypeStruct(w.shape, F32)] * 4,
    )(w, m, v, g_all)


SMALL_LAYOUT = dict(norm_mix_g=(0, 1024), b_in=(1024, 7424), hgrn_norm_g=(8448, 1024), norm_ffn_g=(9472, 1024),
                    norm_final_g=(10496, 1024), hgrn_lb_logits=(11520, 2048), attn_sinks=(13568, 16))
SMALL_LOSS, SMALL_SIZE = 13584, 16 * D_MODEL


def _pack_small(vals, loss=None):
    names = list(SMALL_LAYOUT)
    tail = jnp.zeros((SMALL_SIZE - SMALL_LAYOUT[names[-1]][0],), F32)
    tail = lax.dynamic_update_slice(tail, vals[names[-1]].astype(F32).reshape(-1), (0,))
    if loss is not None:
        tail = lax.dynamic_update_slice(tail, loss.astype(F32).reshape(1), (SMALL_LOSS - SMALL_LAYOUT[names[-1]][0],))
    flat = jnp.concatenate([vals[n].astype(F32).reshape(-1) for n in names[:-1]] + [tail])
    return flat.reshape(-1, D_MODEL)


def _unpack_small(packed, shapes):
    flat = packed.reshape(-1)
    return {name: flat[lo:lo + n].reshape(shapes[name]) for name, (lo, n) in SMALL_LAYOUT.items()}


MATRICES = ("w_in", "w_branch_attn", "w_branch_hgrn", "w_out", "w_ffn_gate", "w_ffn_up", "w_ffn_down")
COLUMN_SHARDED = ("w_in", "w_ffn_gate", "w_ffn_up")
WEIGHTS = ("norm_mix_g", "w_in", "b_in", "attn_sinks", "hgrn_lb_logits", "hgrn_norm_g", "w_branch_attn",
           "w_branch_hgrn", "w_out", "norm_ffn_g", "w_ffn_gate", "w_ffn_up", "w_ffn_down", "norm_final_g")


def kernel(x, norm_mix_g, w_in, b_in, attn_sinks, hgrn_lb_logits, hgrn_norm_g, w_branch_attn, w_branch_hgrn, w_out, norm_ffn_g, w_ffn_gate, w_ffn_up, w_ffn_down, norm_final_g, loss_target, m_norm_mix_g, m_w_in, m_b_in, m_attn_sinks, m_hgrn_lb_logits, m_hgrn_norm_g, m_w_branch_attn, m_w_branch_hgrn, m_w_out, m_norm_ffn_g, m_w_ffn_gate, m_w_ffn_up, m_w_ffn_down, m_norm_final_g, v_norm_mix_g, v_w_in, v_b_in, v_attn_sinks, v_hgrn_lb_logits, v_hgrn_norm_g, v_w_branch_attn, v_w_branch_hgrn, v_w_out, v_norm_ffn_g, v_w_ffn_gate, v_w_ffn_up, v_w_ffn_down, v_norm_final_g):
    given = dict(locals())
    w = {n: given[n] for n in WEIGHTS}
    m = {n: given["m_" + n] for n in WEIGHTS}
    v = {n: given["v_" + n] for n in WEIGHTS}

    block = lambda a, n: jnp.transpose(a[0]) if n in COLUMN_SHARDED else a[0]
    unblock = lambda a, n: (jnp.transpose(a) if n in COLUMN_SHARDED else a)[None]
    net = _Net({n: block(w[n], n).astype(MXU_DTYPE) for n in MATRICES})
    net.gathered(("w_in",), _gather_two_level("gather_w_in", [net.shards["w_in"]]))
    vec = dict(norm_mix_g=norm_mix_g, b_in=b_in, attn_sinks=attn_sinks, hgrn_lb_logits=hgrn_lb_logits,
               hgrn_norm_g=hgrn_norm_g, norm_ffn_g=norm_ffn_g, norm_final_g=norm_final_g.reshape(1, D_MODEL))
    loss_part, dx, d_vecs = _local_step(x[0], loss_target[0], vec, net)

    small_all, = net.received(*net.last, carried=_small_copies(_pack_small(d_vecs, loss_part)))
    grads, deltas, new_m, new_v = {}, {}, {}, {}
    for n in ("w_ffn_down", "w_ffn_gate", "w_ffn_up", "w_out", "w_branch_attn", "w_branch_hgrn", "w_in"):
        res, got = _adamw("adamw_" + n, block(w[n], n), block(m[n], n), block(v[n], n), net.sums[n], net.other[n],
                          comm=net.swap(("w_in",)) if n == "w_ffn_down" else None)
        if n == "w_ffn_down":
            net.other["w_in"], = got
        grads[n], deltas[n], new_m[n], new_v[n] = (unblock(r, n) for r in res)
    shapes = {n: w[n].shape for n in SMALL_LAYOUT}
    res = _adamw_small(_pack_small(w), _pack_small(m), _pack_small(v), small_all)
    for dst, packed in zip((grads, deltas, new_m, new_v), res):
        dst.update(_unpack_small(packed, shapes))

    loss = res[0].reshape(-1)[SMALL_LOSS]
    return (loss, dx[None], *[grads[n] for n in WEIGHTS], *[deltas[n] for n in WEIGHTS],
            *[new_m[n] for n in WEIGHTS], *[new_v[n] for n in WEIGHTS])
```

```python
import functools
import math

import jax
import jax.numpy as jnp
from jax import lax
from jax.experimental import pallas as pl
from jax.experimental.pallas import tpu as pltpu

F32 = jnp.float32
BF16 = jnp.bfloat16
MXU_DTYPE = jnp.bfloat16
MESH_ID = pl.DeviceIdType.MESH

D_MODEL = 1024
HEAD_DIM = 64
Q_HEADS = 16
KV_HEADS = 2
GROUP = Q_HEADS // KV_HEADS
KV_WIDTH = KV_HEADS * HEAD_DIM
ATTN_BLOCK = 128
HGRN_HEADS = 8
HGRN_K = 128
CHUNK = 64
HGRN_TOKENS = 256
FFN = 2816
IN_SPLITS = (1024, 256, 4096, 2048)
EPS = 1e-6
NEG_INF = -1e30
ADAM_LR, ADAM_B1, ADAM_B2, ADAM_EPS, ADAM_WD, ADAM_STEP = 0.001, 0.9, 0.999, 1e-08, 0.01, 10
N_CHIPS = 4
VMEM_LIMIT = 60 * 1024 * 1024


def _params(sem=None):
    return pltpu.CompilerParams(dimension_semantics=sem, vmem_limit_bytes=VMEM_LIMIT)


def _sigmoid(v):
    return 0.5 * jnp.tanh(0.5 * v) + 0.5


def _dot(a, b, dims):
    return lax.dot_general(a.astype(MXU_DTYPE), b.astype(MXU_DTYPE), (dims, ((), ())),
                           preferred_element_type=F32)


def _nn(a, b):
    return _dot(a, b, ((1,), (0,)))


def _nt(a, b):
    return _dot(a, b, ((1,), (1,)))


def _tn(a, b):
    return _dot(a, b, ((0,), (0,)))


HBM_SPEC = pl.BlockSpec(memory_space=pl.ANY)


class _Carried:
    def __init__(self, arrays, out_shapes, n_remote, n_local, build):
        self.parts = [(len(arrays), len(out_shapes), build)]
        self.arrays, self.out_shapes = list(arrays), list(out_shapes)
        self.scratch = [pltpu.SemaphoreType.DMA((n_remote,)), pltpu.SemaphoreType.DMA((n_remote,)),
                        pltpu.SemaphoreType.DMA((max(n_local, 1),))]

    def __add__(self, other):
        both = _Carried([], [], 1, 0, None)
        both.parts = self.parts + other.parts
        both.arrays, both.out_shapes = self.arrays + other.arrays, self.out_shapes + other.out_shapes
        both.scratch = self.scratch + other.scratch
        return both

    def _built(self, ins, outs, sems):
        for p, (ni, no, build) in enumerate(self.parts):
            yield build(ins[:ni], outs[:no], *sems[3 * p:3 * p + 3])
            ins, outs = ins[ni:], outs[no:]

    def start(self, ins, outs, sems):
        for sends, _, local in self._built(ins, outs, sems):
            for cp in local + sends:
                cp.start()

    def wait(self, ins, outs, sems):
        for sends, recvs, local in self._built(ins, outs, sems):
            for cp in recvs:
                cp.wait_recv()
            for cp in sends:
                cp.wait_send()
            for cp in local:
                cp.wait()


def _join(*comms):
    comms = [c for c in comms if c is not None]
    return functools.reduce(lambda a, b: a + b, comms) if comms else None


def _call(name, body, *, grid, in_specs, out_specs, out_shape, args, scratch=(), semantics=None, comm=None):
    n_in, n_out, n_scr = len(in_specs), len(out_specs), len(scratch)
    if comm is None:
        res = pl.pallas_call(body, name=name, grid=grid, in_specs=in_specs, out_specs=out_specs, out_shape=out_shape,
                             scratch_shapes=list(scratch), compiler_params=_params(semantics))(*args)
        return list(res), []
    ci, co = len(comm.arrays), len(comm.out_shapes)

    def carrying(*refs):
        ins, refs = refs[:n_in], refs[n_in:]
        c_ins, refs = refs[:ci], refs[ci:]
        outs, refs = refs[:n_out], refs[n_out:]
        c_outs, refs = refs[:co], refs[co:]
        scr, sems = refs[:n_scr], refs[n_scr:]
        if not grid:
            comm.start(c_ins, c_outs, sems)
            body(*ins, *outs, *scr)
            comm.wait(c_ins, c_outs, sems)
            return
        first = functools.reduce(jnp.logical_and, [pl.program_id(a) == 0 for a in range(len(grid))])
        last = functools.reduce(jnp.logical_and, [pl.program_id(a) == g - 1 for a, g in enumerate(grid)])

        @pl.when(first)
        def _():
            comm.start(c_ins, c_outs, sems)

        body(*ins, *outs, *scr)

        @pl.when(last)
        def _():
            comm.wait(c_ins, c_outs, sems)

    res = pl.pallas_call(
        carrying, name=name, grid=grid, in_specs=list(in_specs) + [HBM_SPEC] * ci,
        out_specs=list(out_specs) + [HBM_SPEC] * co, out_shape=list(out_shape) + comm.out_shapes,
        scratch_shapes=list(scratch) + comm.scratch,
        compiler_params=_params(("arbitrary",) * len(grid) if grid else None),
    )(*args, *comm.arrays)
    return list(res[:n_out]), list(res[n_out:])


_NO_COPIES = object()


def _matmul(name, pairs, mode, *, tm, tn, tk, outs, extras=(), epilogue=None, comm=_NO_COPIES):
    prod = dict(nn=_nn, nt=_nt, tn=_tn)[mode]
    a0, b0 = pairs[0]
    M = a0.shape[1] if mode == "tn" else a0.shape[0]
    N = b0.shape[0] if mode == "nt" else b0.shape[1]
    steps, in_specs, offset = [], [], 0
    for a, b in pairs:
        K = a.shape[0] if mode == "tn" else a.shape[1]
        t = min(tk, K)
        assert K % t == 0, (name, K, t)
        kmap = functools.partial(lambda k, off, n: jnp.clip(k - off, 0, n - 1), off=offset, n=K // t)
        if mode == "tn":
            in_specs.append(pl.BlockSpec((t, tm), functools.partial(lambda i, j, k, f: (f(k), i), f=kmap)))
        else:
            in_specs.append(pl.BlockSpec((tm, t), functools.partial(lambda i, j, k, f: (i, f(k)), f=kmap)))
        if mode == "nt":
            in_specs.append(pl.BlockSpec((tn, t), functools.partial(lambda i, j, k, f: (j, f(k)), f=kmap)))
        else:
            in_specs.append(pl.BlockSpec((t, tn), functools.partial(lambda i, j, k, f: (f(k), j), f=kmap)))
        steps.append((offset, offset + K // t))
        offset += K // t
    assert M % tm == 0 and N % tn == 0, (name, M, N, tm, tn)
    ni, nj, nk = M // tm, N // tn, offset
    npair, ne, no = len(pairs), len(extras), len(outs)
    if epilogue is None:
        epilogue = lambda acc: (acc,)

    def finish(acc, extra_refs, out_refs):
        vals = epilogue(acc, *[r[...] for r in extra_refs])
        for (kind, _), o_ref, val in zip(outs, out_refs, vals):
            if kind == "pn":
                val = jnp.broadcast_to(val, o_ref.shape)
            o_ref[...] = val.astype(o_ref.dtype)

    def body(*refs):
        ab, rest = refs[:2 * npair], refs[2 * npair:]
        extra_refs, out_refs = rest[:ne], rest[ne:ne + no]
        if nk == 1:
            finish(prod(ab[0][...], ab[1][...]), extra_refs, out_refs)
            return
        acc_ref = rest[ne + no]
        k = pl.program_id(2)

        @pl.when(k == 0)
        def _():
            acc_ref[...] = jnp.zeros_like(acc_ref)

        for p, (lo, hi) in enumerate(steps):
            @pl.when(jnp.logical_and(k >= lo, k < hi))
            def _():
                acc_ref[...] += prod(ab[2 * p][...], ab[2 * p + 1][...])

        @pl.when(k == nk - 1)
        def _():
            finish(acc_ref[...], extra_refs, out_refs)

    for _, shape, im in extras:
        in_specs.append(pl.BlockSpec(shape, functools.partial(lambda i, j, k, im: im(i, j), im=im)))
    out_shape, out_specs = [], []
    for kind, dt in outs:
        if kind == "mn":
            out_shape.append(jax.ShapeDtypeStruct((M, N), dt))
            out_specs.append(pl.BlockSpec((tm, tn), lambda i, j, k: (i, j)))
        else:
            out_shape.append(jax.ShapeDtypeStruct((8 * ni, N), dt))
            out_specs.append(pl.BlockSpec((8, tn), lambda i, j, k: (i, j)))
    grid = (ni, nj, nk)
    if nj > 1 and nk == 1:
        turned = lambda spec: pl.BlockSpec(spec.block_shape, functools.partial(
            lambda j, i, k, im: im(i, j, k), im=spec.index_map))
        in_specs, out_specs, grid = [turned(s) for s in in_specs], [turned(s) for s in out_specs], (nj, ni, nk)
    res, got = _call(name, body, grid=grid, in_specs=in_specs, out_specs=out_specs, out_shape=out_shape,
                     args=[t for pair in pairs for t in pair] + [e[0] for e in extras],
                     scratch=[pltpu.VMEM((tm, tn), F32)] if nk > 1 else [],
                     semantics=("parallel", "parallel", "arbitrary"), comm=None if comm is _NO_COPIES else comm)
    return res if comm is _NO_COPIES else (res, got)


def _swiglu_fwd(u, w_gate_t, w_up_t, *, tm, tn, comm=None):
    (T, D), F = u.shape, w_gate_t.shape[0]

    def body(u_ref, wg_ref, wu_ref, g_ref, up_ref, z_ref):
        g, up = _nt(u_ref[...], wg_ref[...]), _nt(u_ref[...], wu_ref[...])
        g_ref[...], up_ref[...] = g, up
        z_ref[...] = (g * _sigmoid(g) * up).astype(z_ref.dtype)

    w_spec = pl.BlockSpec((tn, D), lambda j, i: (j, 0))
    o_spec = pl.BlockSpec((tm, tn), lambda j, i: (i, j))
    return _call("ffn_hidden", body, grid=(F // tn, T // tm),
                 in_specs=[pl.BlockSpec((tm, D), lambda j, i: (i, 0)), w_spec, w_spec], out_specs=[o_spec] * 3,
                 out_shape=[jax.ShapeDtypeStruct((T, F), F32)] * 2 + [jax.ShapeDtypeStruct((T, F), MXU_DTYPE)],
                 args=[u, w_gate_t, w_up_t], semantics=("parallel", "parallel"), comm=comm)


def _colsum_partials(p):
    return jnp.sum(p.reshape(-1, 8, p.shape[-1])[:, 0, :], axis=0, keepdims=True)


def _rmsnorm_fwd(name, x, g, tr=512):
    T, D = x.shape

    def body(x_ref, g_ref, u_ref):
        xv = x_ref[...]
        r = lax.rsqrt(jnp.mean(xv * xv, axis=-1, keepdims=True) + EPS)
        u_ref[...] = (xv * r * g_ref[...]).astype(u_ref.dtype)

    return pl.pallas_call(
        body, name=name, grid=(T // tr,),
        in_specs=[pl.BlockSpec((tr, D), lambda i: (i, 0)), pl.BlockSpec((1, D), lambda i: (0, 0))],
        out_specs=pl.BlockSpec((tr, D), lambda i: (i, 0)),
        out_shape=jax.ShapeDtypeStruct((T, D), MXU_DTYPE),
        compiler_params=_params(("parallel",)),
    )(x, g)


def _rmsnorm_bwd_vals(dy, xin, g):
    rstd = lax.rsqrt(jnp.mean(xin * xin, axis=-1, keepdims=True) + EPS)
    xhat = xin * rstd
    dg = jnp.sum(dy * xhat, axis=0, keepdims=True)
    dxh = dy * g
    dx = rstd * (dxh - xhat * jnp.mean(dxh * xhat, axis=-1, keepdims=True))
    return dx, dg


def _colsum(name, a, tr=512, comm=_NO_COPIES):
    T, N = a.shape

    def body(a_ref, o_ref):
        @pl.when(pl.program_id(0) == 0)
        def _():
            o_ref[...] = jnp.zeros_like(o_ref)

        o_ref[...] += jnp.sum(a_ref[...].astype(F32), axis=0, keepdims=True)

    (res,), got = _call(name, body, grid=(T // tr,), in_specs=[pl.BlockSpec((tr, N), lambda i: (i, 0))],
                        out_specs=[pl.BlockSpec((1, N), lambda i: (0, 0))],
                        out_shape=[jax.ShapeDtypeStruct((1, N), F32)], args=[a], semantics=("arbitrary",),
                        comm=None if comm is _NO_COPIES else comm)
    return res if comm is _NO_COPIES else (res, got)


ATTN_SCALE = 1.0 / math.sqrt(HEAD_DIM)
GROUP_LANES = GROUP * ATTN_BLOCK
PAIR = 2 * HEAD_DIM


def _attn_mask():
    kj = lax.broadcasted_iota(jnp.int32, (ATTN_BLOCK, GROUP_LANES), 0)
    qi = lax.broadcasted_iota(jnp.int32, (ATTN_BLOCK, GROUP_LANES), 1) & (ATTN_BLOCK - 1)
    return kj <= qi


def _heads_transposed(ref, g, scale=None):
    parts = []
    for a in range(GROUP // 2):
        lo = (g * GROUP // 2 + a) * PAIR
        pair = ref[:, lo:lo + PAIR]
        pair = (pair if scale is None else pair * scale).T
        parts += [pair[:HEAD_DIM], pair[HEAD_DIM:]]
    return jnp.concatenate(parts, axis=1).astype(MXU_DTYPE)


def _heads_back(ref, g, vt):
    for a in range(GROUP // 2):
        lo = (g * GROUP // 2 + a) * PAIR
        pair = jnp.concatenate([vt[:, (2 * a) * ATTN_BLOCK:(2 * a + 1) * ATTN_BLOCK],
                                vt[:, (2 * a + 1) * ATTN_BLOCK:(2 * a + 2) * ATTN_BLOCK]], axis=0)
        ref[:, lo:lo + PAIR] = pair.T.astype(ref.dtype)


def _kv_parts(kv_ref, g):
    ks = slice(g * HEAD_DIM, (g + 1) * HEAD_DIM)
    vs = slice(KV_WIDTH + g * HEAD_DIM, KV_WIDTH + (g + 1) * HEAD_DIM)
    return kv_ref[:, ks].astype(MXU_DTYPE), kv_ref[:, vs].astype(MXU_DTYPE)


def _sink_rows(sinks):
    return jnp.repeat(sinks.reshape(KV_HEADS, GROUP), ATTN_BLOCK, axis=1)


def _attn_fwd(pq, pkv, sinks, comm=None):
    T = pq.shape[0]
    nb = T // ATTN_BLOCK

    def body(q_ref, kvc_ref, kvp_ref, s_ref, y_ref, lse_ref):
        mask_c = _attn_mask()
        has_prev = pl.program_id(0) > 0
        for g in range(KV_HEADS):
            (kc, vc), (kp, vp) = _kv_parts(kvc_ref, g), _kv_parts(kvp_ref, g)
            qt = _heads_transposed(q_ref, g, ATTN_SCALE)
            s = jnp.where(mask_c, _nn(kc, qt), jnp.where(has_prev, _nn(kp, qt), NEG_INF))
            sink = s_ref[g:g + 1, :]
            m = jnp.maximum(jnp.max(s, axis=0, keepdims=True), sink)
            p = jnp.exp(s - m)
            den = jnp.sum(p, axis=0, keepdims=True) + jnp.exp(sink - m)
            pc = jnp.where(mask_c, p, 0.0)
            _heads_back(y_ref, g, (_tn(vc, pc) + _tn(vp, p - pc)) / den)
            lse = m + jnp.log(den)
            for i in range(GROUP):
                lse_ref[g * GROUP + i:g * GROUP + i + 1, :] = lse[:, i * ATTN_BLOCK:(i + 1) * ATTN_BLOCK]

    return _call(
        "attn_fwd", body, grid=(nb,),
        in_specs=[pl.BlockSpec((ATTN_BLOCK, D_MODEL), lambda n: (n, 0)),
                  pl.BlockSpec((ATTN_BLOCK, 2 * KV_WIDTH), lambda n: (n, 0)),
                  pl.BlockSpec((ATTN_BLOCK, 2 * KV_WIDTH), lambda n: (jnp.maximum(n - 1, 0), 0)),
                  pl.BlockSpec((KV_HEADS, GROUP_LANES), lambda n: (0, 0))],
        out_specs=[pl.BlockSpec((ATTN_BLOCK, D_MODEL), lambda n: (n, 0)),
                   pl.BlockSpec((Q_HEADS, ATTN_BLOCK), lambda n: (0, n))],
        out_shape=[jax.ShapeDtypeStruct((T, D_MODEL), MXU_DTYPE), jax.ShapeDtypeStruct((Q_HEADS, T), F32)],
        args=[pq, pkv, pkv, _sink_rows(sinks)], semantics=("parallel",), comm=comm)


def _attn_bwd(pq, pkv, sinks, lse, dy, comm=None):
    T = pq.shape[0]
    nb = T // ATTN_BLOCK
    cur = lambda n: (jnp.minimum(n, nb - 1), 0)

    def body(q_ref, kvc_ref, kvp_ref, s_ref, lse_ref, dy_ref, dq_ref, dkv_ref, ds_ref, carry, top, bot):
        n = pl.program_id(0)

        @pl.when(n == 0)
        def _():
            carry[...] = jnp.zeros_like(carry)
            ds_ref[...] = jnp.zeros_like(ds_ref)

        @pl.when(n < nb)
        def _():
            mask_c = _attn_mask()
            valid = jnp.logical_or(mask_c, n > 0)
            for g in range(KV_HEADS):
                ks = slice(g * HEAD_DIM, (g + 1) * HEAD_DIM)
                vs = slice(KV_WIDTH + g * HEAD_DIM, KV_WIDTH + (g + 1) * HEAD_DIM)
                (kc, vc), (kp, vp) = _kv_parts(kvc_ref, g), _kv_parts(kvp_ref, g)
                qt = _heads_transposed(q_ref, g, ATTN_SCALE)
                dot = _heads_transposed(dy_ref, g)
                lse = jnp.concatenate([lse_ref[g * GROUP + i:g * GROUP + i + 1, :] for i in range(GROUP)], axis=1)
                p = jnp.where(valid, jnp.exp(jnp.where(mask_c, _nn(kc, qt), _nn(kp, qt)) - lse), 0.0)
                dp = jnp.where(mask_c, _nn(vc, dot), _nn(vp, dot))
                delta = jnp.sum(p * dp, axis=0, keepdims=True)
                ds = p * (dp - delta)
                ds_c, p_c = jnp.where(mask_c, ds, 0.0), jnp.where(mask_c, p, 0.0)
                ds_p, p_p = ds - ds_c, p - p_c
                _heads_back(dq_ref, g, (_tn(kc, ds_c) + _tn(kp, ds_p)) * ATTN_SCALE)
                bot[:, ks], bot[:, vs] = _nt(ds_c, qt), _nt(p_c, dot)
                top[:, ks], top[:, vs] = _nt(ds_p, qt), _nt(p_p, dot)
                ds_ref[g:g + 1, :] -= jnp.exp(s_ref[g:g + 1, :] - lse) * delta
            dkv_ref[...] = (carry[...] + top[...]).astype(dkv_ref.dtype)
            carry[...] = bot[...]

        @pl.when(n == nb)
        def _():
            dkv_ref[...] = carry[...].astype(dkv_ref.dtype)

    return _call(
        "attn_bwd", body, grid=(nb + 1,),
        in_specs=[pl.BlockSpec((ATTN_BLOCK, D_MODEL), cur),
                  pl.BlockSpec((ATTN_BLOCK, 2 * KV_WIDTH), cur),
                  pl.BlockSpec((ATTN_BLOCK, 2 * KV_WIDTH), lambda n: (jnp.maximum(jnp.minimum(n, nb - 1) - 1, 0), 0)),
                  pl.BlockSpec((KV_HEADS, GROUP_LANES), lambda n: (0, 0)),
                  pl.BlockSpec((Q_HEADS, ATTN_BLOCK), lambda n: (0, jnp.minimum(n, nb - 1))),
                  pl.BlockSpec((ATTN_BLOCK, D_MODEL), cur)],
        out_specs=[pl.BlockSpec((ATTN_BLOCK, D_MODEL), cur),
                   pl.BlockSpec((ATTN_BLOCK, 2 * KV_WIDTH), lambda n: (jnp.maximum(n - 1, 0), 0)),
                   pl.BlockSpec((KV_HEADS, GROUP_LANES), lambda n: (0, 0))],
        out_shape=[jax.ShapeDtypeStruct((T, D_MODEL), MXU_DTYPE),
                   jax.ShapeDtypeStruct((T, 2 * KV_WIDTH), MXU_DTYPE),
                   jax.ShapeDtypeStruct((KV_HEADS, GROUP_LANES), F32)],
        scratch=[pltpu.VMEM((ATTN_BLOCK, 2 * KV_WIDTH), F32)] * 3,
        args=[pq, pkv, pkv, _sink_rows(sinks), lse, dy], semantics=("arbitrary",), comm=comm)


def _lower_bound(l):
    m = jnp.maximum(l[0:1], l[1:2])
    e0, e1 = jnp.exp(l[0:1] - m), jnp.exp(l[1:2] - m)
    return e0 / (e0 + e1)


def _tri(lower):
    r = lax.broadcasted_iota(jnp.int32, (CHUNK, CHUNK), 0)
    c = lax.broadcasted_iota(jnp.int32, (CHUNK, CHUNK), 1)
    return (r >= c) if lower else (c >= r)


def _chunk_sum(mask, v):
    return lax.dot_general(mask.astype(F32), v, (((1,), (0,)), ((), ())),
                           precision=lax.Precision.HIGHEST, preferred_element_type=F32)


def _hgrn_chunk_inputs(hq, hf, lb, causal):
    sg, sgn = _sigmoid(hf), _sigmoid(-hf)
    f = lb + (1.0 - lb) * sg
    kk = (1.0 - lb) * sgn
    sq = _sigmoid(hq)
    q = hq * sq
    b = _chunk_sum(causal, jnp.log(f))
    bm, bl = b[CHUNK // 2 - 1:CHUNK // 2, :], b[CHUNK - 1:CHUNK, :]
    e_qm, e_km, e_qs, e_kl = jnp.exp(b - bm), jnp.exp(bm - b), jnp.exp(b), jnp.exp(bl - b)
    return dict(sg=sg, sgn=sgn, f=f, kk=kk, sq=sq, q=q, e_qm=e_qm, e_km=e_km, e_qs=e_qs, e_kl=e_kl,
                qm=q * e_qm, km=kk * e_km, qs=q * e_qs, kl=kk * e_kl, el=jnp.exp(bl))


def _hgrn_fwd(ph, lb_logits, norm_g, comm=None):
    T = ph.shape[0]
    nblk, cpb = T // HGRN_TOKENS, HGRN_TOKENS // CHUNK
    col = lambda c: pl.BlockSpec((HGRN_TOKENS, D_MODEL), functools.partial(lambda i, c: (i, c), c=c))

    def body(hq_ref, hf_ref, hi_ref, hg_ref, l_ref, ng_ref, y_ref, o_ref, st_ref, s_ref):
        @pl.when(pl.program_id(0) == 0)
        def _():
            s_ref[...] = jnp.zeros_like(s_ref)

        lb = _lower_bound(l_ref[...])
        causal = _tri(True)
        for c in range(cpb):
            rows = slice(c * CHUNK, (c + 1) * CHUNK)
            t = _hgrn_chunk_inputs(hq_ref[rows, :], hf_ref[rows, :], lb, causal)
            qm, km, qs, kl = (t[n].astype(MXU_DTYPE) for n in ("qm", "km", "qs", "kl"))
            v = hi_ref[rows, :].astype(MXU_DTYPE)
            for h in range(HGRN_HEADS):
                ls = slice(h * HGRN_K, (h + 1) * HGRN_K)
                st = s_ref[h]
                st_ref[c, ls, :] = st
                a = jnp.where(causal, _nt(qm[:, ls], km[:, ls]), 0.0)
                o_ref[rows, ls] = _nn(a, v[:, ls]) + _nt(qs[:, ls], st)
                s_ref[h] = t["el"][:, ls] * st + _tn(v[:, ls], kl[:, ls])
        for h in range(HGRN_HEADS):
            ls = slice(h * HGRN_K, (h + 1) * HGRN_K)
            o = o_ref[:, ls]
            r = lax.rsqrt(jnp.mean(o * o, axis=-1, keepdims=True) + EPS)
            y_ref[:, ls] = (o * r * ng_ref[:, ls] * _sigmoid(hg_ref[:, ls])).astype(y_ref.dtype)

    return _call(
        "hgrn_fwd", body, grid=(nblk,),
        in_specs=[col(0), col(1), col(2), col(3),
                  pl.BlockSpec((2, D_MODEL), lambda i: (0, 0)), pl.BlockSpec((1, D_MODEL), lambda i: (0, 0))],
        out_specs=[pl.BlockSpec((HGRN_TOKENS, D_MODEL), lambda i: (i, 0)),
                   pl.BlockSpec((HGRN_TOKENS, D_MODEL), lambda i: (i, 0)),
                   pl.BlockSpec((cpb, D_MODEL, HGRN_K), lambda i: (i, 0, 0))],
        out_shape=[jax.ShapeDtypeStruct((T, D_MODEL), MXU_DTYPE), jax.ShapeDtypeStruct((T, D_MODEL), F32),
                   jax.ShapeDtypeStruct((T // CHUNK, D_MODEL, HGRN_K), F32)],
        scratch=[pltpu.VMEM((HGRN_HEADS, HGRN_K, HGRN_K), F32)],
        args=[ph, ph, ph, ph, lb_logits, norm_g], semantics=("arbitrary",), comm=comm)


def _hgrn_bwd(ph, o_raw, states, dy, lb_logits, norm_g, comm=None):
    T = ph.shape[0]
    nblk, cpb = T // HGRN_TOKENS, HGRN_TOKENS // CHUNK
    rev = lambda i: nblk - 1 - i
    col = lambda c: pl.BlockSpec((HGRN_TOKENS, D_MODEL), functools.partial(lambda i, c: (rev(i), c), c=c))
    tok = pl.BlockSpec((HGRN_TOKENS, D_MODEL), lambda i: (rev(i), 0))

    def body(hq_ref, hf_ref, hi_ref, hg_ref, o_ref, st_ref, dy_ref, l_ref, ng_ref,
             dph_ref, dng_ref, dl_ref, dst_ref, dlb_ref, do_s, dqm_s, dkm_s, dqs_s, dkl_s, dv_s, del_s):
        i = pl.program_id(0)

        @pl.when(i == 0)
        def _():
            dst_ref[...] = jnp.zeros_like(dst_ref)
            dlb_ref[...] = jnp.zeros_like(dlb_ref)
            dng_ref[...] = jnp.zeros_like(dng_ref)

        lb = _lower_bound(l_ref[...])
        causal, anti = _tri(True), _tri(False)
        row = lax.broadcasted_iota(jnp.int32, (CHUNK, D_MODEL), 0)
        for c in reversed(range(cpb)):
            rows = slice(c * CHUNK, (c + 1) * CHUNK)
            hq = hq_ref[rows, :]
            t = _hgrn_chunk_inputs(hq, hf_ref[rows, :], lb, causal)
            sgg = _sigmoid(hg_ref[rows, :])
            dyv = dy_ref[rows, :]
            for h in range(HGRN_HEADS):
                ls = slice(h * HGRN_K, (h + 1) * HGRN_K)
                o = o_ref[rows, ls]
                r = lax.rsqrt(jnp.mean(o * o, axis=-1, keepdims=True) + EPS)
                nrm = o * r
                g_h = sgg[:, ls]
                dph_ref[rows, 3 * D_MODEL + h * HGRN_K:3 * D_MODEL + (h + 1) * HGRN_K] = (
                    dyv[:, ls] * nrm * ng_ref[:, ls] * g_h * (1.0 - g_h)).astype(dph_ref.dtype)
                dyg = dyv[:, ls] * g_h
                dng_ref[:, ls] += jnp.sum(dyg * nrm, axis=0, keepdims=True)
                dn = dyg * ng_ref[:, ls]
                do_s[:, ls] = r * (dn - nrm * jnp.mean(dn * nrm, axis=-1, keepdims=True))
            qm, km, qs, kl = (t[n].astype(MXU_DTYPE) for n in ("qm", "km", "qs", "kl"))
            v = hi_ref[rows, :].astype(MXU_DTYPE)
            do = do_s[...].astype(MXU_DTYPE)
            for h in range(HGRN_HEADS):
                ls = slice(h * HGRN_K, (h + 1) * HGRN_K)
                st = st_ref[c, ls, :]
                dst = dst_ref[h]
                a = jnp.where(causal, _nt(qm[:, ls], km[:, ls]), 0.0)
                da = jnp.where(causal, _nt(do[:, ls], v[:, ls]), 0.0)
                dv_s[:, ls] = _tn(a, do[:, ls]) + _nt(kl[:, ls], dst)
                dkl_s[:, ls] = _nn(v[:, ls], dst)
                dqs_s[:, ls] = _nn(do[:, ls], st)
                del_s[:, ls] = jnp.sum(dst * st, axis=0, keepdims=True)
                dst_ref[h] = _tn(do[:, ls], qs[:, ls]) + t["el"][:, ls] * dst
                dqm_s[:, ls] = _nn(da, km[:, ls])
                dkm_s[:, ls] = _tn(da, qm[:, ls])
            dqm, dkm, dqs, dkl = dqm_s[...], dkm_s[...], dqs_s[...], dkl_s[...]
            dq = dqm * t["e_qm"] + dqs * t["e_qs"]
            dk = dkm * t["e_km"] + dkl * t["e_kl"]
            t_qm, t_km, t_kl = dqm * t["qm"], dkm * t["km"], dkl * t["kl"]
            db = t_qm - t_km + dqs * t["qs"] - t_kl
            db_mid = jnp.sum(t_km - t_qm, axis=0, keepdims=True)
            db_last = jnp.sum(t_kl, axis=0, keepdims=True) + del_s[...] * t["el"]
            db = db + jnp.where(row == CHUNK // 2 - 1, db_mid, 0.0) + jnp.where(row == CHUNK - 1, db_last, 0.0)
            dlogf = _chunk_sum(anti, db)
            sq, sg, sgn, f = t["sq"], t["sg"], t["sgn"], t["f"]
            dph_ref[rows, 0:D_MODEL] = (dq * (sq * (1.0 + hq * (1.0 - sq)))).astype(dph_ref.dtype)
            dph_ref[rows, D_MODEL:2 * D_MODEL] = (
                dlogf * (1.0 - lb) * sg * (1.0 - sg) / f - dk * (1.0 - lb) * sgn * (1.0 - sgn)).astype(dph_ref.dtype)
            dph_ref[rows, 2 * D_MODEL:3 * D_MODEL] = dv_s[...].astype(dph_ref.dtype)
            dlb_ref[...] += jnp.sum(dlogf * (1.0 - sg) / f - dk * sgn, axis=0, keepdims=True)

        @pl.when(i == nblk - 1)
        def _():
            dl0 = dlb_ref[...] * lb * (1.0 - lb)
            dl_ref[0:1, :] = dl0
            dl_ref[1:2, :] = -dl0

    wide = pltpu.VMEM((CHUNK, D_MODEL), F32)
    return _call(
        "hgrn_bwd", body, grid=(nblk,),
        in_specs=[col(0), col(1), col(2), col(3), tok,
                  pl.BlockSpec((cpb, D_MODEL, HGRN_K), lambda i: (rev(i), 0, 0)), tok,
                  pl.BlockSpec((2, D_MODEL), lambda i: (0, 0)), pl.BlockSpec((1, D_MODEL), lambda i: (0, 0))],
        out_specs=[pl.BlockSpec((HGRN_TOKENS, 4 * D_MODEL), lambda i: (rev(i), 0)),
                   pl.BlockSpec((1, D_MODEL), lambda i: (0, 0)), pl.BlockSpec((2, D_MODEL), lambda i: (0, 0))],
        out_shape=[jax.ShapeDtypeStruct((T, 4 * D_MODEL), MXU_DTYPE), jax.ShapeDtypeStruct((1, D_MODEL), F32),
                   jax.ShapeDtypeStruct((2, D_MODEL), F32)],
        scratch=[pltpu.VMEM((HGRN_HEADS, HGRN_K, HGRN_K), F32), pltpu.VMEM((1, D_MODEL), F32),
                 wide, wide, wide, wide, wide, wide, pltpu.VMEM((1, D_MODEL), F32)],
        args=[ph, ph, ph, ph, o_raw, states, dy, lb_logits, norm_g], semantics=("arbitrary",), comm=comm)


def _local_step(x, target, vec, net):
    T, D = x.shape
    norm_mix_g, b_in, sinks, lb_logits = vec["norm_mix_g"], vec["b_in"], vec["attn_sinks"], vec["hgrn_lb_logits"]
    hgrn_norm_g, norm_ffn_g, norm_final_g = vec["hgrn_norm_g"], vec["norm_ffn_g"], vec["norm_final_g"]
    w_in = net.full("w_in")
    o_q, o_kv, o_h, o_g = (sum(IN_SPLITS[:i]) for i in range(4))
    w_q, w_kv, w_h, w_g = (w_in[o:o + n] for o, n in zip((o_q, o_kv, o_h, o_g), IN_SPLITS))
    b_q, b_kv, b_h, b_g = (b_in[:, o:o + n] for o, n in zip((o_q, o_kv, o_h, o_g), IN_SPLITS))
    bias = lambda acc, b: (acc + b,)
    both = lambda acc: (acc, acc)
    grad_outs = [("mn", F32), ("mn", MXU_DTYPE)]
    row_vec = lambda n: ((1, n), lambda i, j: (0, j))
    tile = lambda tm, tn: ((tm, tn), lambda i, j: (i, j))
    TM = 512
    BIG = min(T, 1024)

    u = _rmsnorm_fwd("norm_mix", x, norm_mix_g)
    pq, = _matmul("in_q", [(u, w_q)], "nt", tm=TM, tn=1024, tk=1024, outs=[("mn", F32)],
                  extras=[(b_q, *row_vec(1024))], epilogue=bias)
    pkv, = _matmul("in_kv", [(u, w_kv)], "nt", tm=TM, tn=256, tk=1024, outs=[("mn", F32)],
                   extras=[(b_kv, *row_vec(256))], epilogue=bias)
    names = ("w_branch_attn", "w_branch_hgrn", "w_out")
    (ph,), got = _matmul("in_h", [(u, w_h)], "nt", tm=TM, tn=1024, tk=1024, outs=[("mn", F32)],
                         extras=[(b_h, *row_vec(1024))], epilogue=bias, comm=net.gather(names))
    net.gathered(names, got)
    pg, = _matmul("in_g", [(u, w_g)], "nt", tm=TM, tn=1024, tk=1024, outs=[("mn", F32)],
                  extras=[(b_g, *row_vec(1024))], epilogue=bias)
    names = ("w_ffn_gate",)
    (y_attn, lse), got = _attn_fwd(pq, pkv, sinks, comm=net.gather(names))
    net.gathered(names, got)
    names = ("w_ffn_up",)
    (y_hgrn, o_raw, states), got = _hgrn_fwd(ph, lb_logits, hgrn_norm_g, comm=net.gather(names))
    net.gathered(names, got)
    w_ba, w_bh, w_out = net.full("w_branch_attn"), net.full("w_branch_hgrn"), net.full("w_out")
    w_gate, w_up = net.full("w_ffn_gate"), net.full("w_ffn_up")
    ya, = _matmul("branch_a", [(y_attn, w_ba)], "nn", tm=TM, tn=1024, tk=1024, outs=[("mn", F32)])
    gate_a = (pg, (TM, 1024), lambda i, j: (i, 0))
    gate_b = (pg, (TM, 1024), lambda i, j: (i, 1))

    def merge(acc, ya_v, ga, gb):
        return acc, _sigmoid(ga) * ya_v + _sigmoid(gb) * acc

    yb, merged = _matmul("branch_b", [(y_hgrn, w_bh)], "nn", tm=TM, tn=1024, tk=1024,
                         outs=[("mn", F32), ("mn", MXU_DTYPE)],
                         extras=[(ya, *tile(TM, 1024)), gate_a, gate_b], epilogue=merge)

    def resid_norm(acc, xin, g):
        h = xin + acc
        r = lax.rsqrt(jnp.mean(h * h, axis=-1, keepdims=True) + EPS)
        return h, h * r * g

    h1, u2 = _matmul("out_proj", [(merged, w_out)], "nn", tm=TM, tn=1024, tk=1024,
                     outs=[("mn", F32), ("mn", MXU_DTYPE)],
                     extras=[(x, *tile(TM, 1024)), (norm_ffn_g, *row_vec(1024))], epilogue=resid_norm)
    FT = FFN // 2
    names = ("w_ffn_down",)
    (gpre, up, z), got = _swiglu_fwd(u2, w_gate, w_up, tm=TM, tn=FT, comm=net.gather(names))
    net.gathered(names, got)
    w_down = net.full("w_ffn_down")

    def loss_head(acc, h1_v, tgt, g):
        h2 = h1_v + acc
        r = lax.rsqrt(jnp.mean(h2 * h2, axis=-1, keepdims=True) + EPS)
        xhat = h2 * r
        err = xhat * g - tgt
        part = 0.5 * jnp.sum(jnp.sum(err * err, axis=-1, keepdims=True), axis=0, keepdims=True) / D
        dyv = err / D
        dxh = dyv * g
        dh2 = r * (dxh - xhat * jnp.mean(dxh * xhat, axis=-1, keepdims=True))
        return dh2, dh2, jnp.sum(dyv * xhat, axis=0, keepdims=True), jnp.broadcast_to(part, (1, D))

    dh2, dh2b, dgf_p, loss_p = _matmul(
        "ffn_down", [(z, w_down)], "nn", tm=TM, tn=1024, tk=FFN,
        outs=[("mn", F32), ("mn", MXU_DTYPE), ("pn", F32), ("pn", F32)],
        extras=[(h1, *tile(TM, 1024)), (target, *tile(TM, 1024)), (norm_final_g, *row_vec(1024))],
        epilogue=loss_head)
    loss = jnp.sum(loss_p.reshape(-1, 8, D)[:, 0, 0])
    d_norm_final = _colsum_partials(dgf_p)

    def swiglu_bwd(acc, gv, upv):
        s = _sigmoid(gv)
        return acc * upv * (s * (1.0 + gv * (1.0 - s))), acc * (gv * s)

    dgp, dup = _matmul("d_ffn_hidden", [(dh2b, w_down)], "nt", tm=TM, tn=FT, tk=1024,
                       outs=[("mn", MXU_DTYPE), ("mn", MXU_DTYPE)],
                       extras=[(gpre, *tile(TM, FT)), (up, *tile(TM, FT))], epilogue=swiglu_bwd)
    d_w_down = _matmul("dw_down", [(z, dh2b)], "tn", tm=FT, tn=1024, tk=1024, outs=grad_outs, epilogue=both)

    def norm_ffn_bwd(acc, h1_v, g, dres):
        dx, dg = _rmsnorm_bwd_vals(acc, h1_v, g)
        dh = dres + dx
        return dh, dh, dg

    names = ("w_ffn_down",)
    (dh1, dh1b, dg2_p), got = _matmul(
        "d_ffn_in", [(dgp, w_gate), (dup, w_up)], "nn", tm=BIG, tn=1024, tk=FT,
        outs=[("mn", F32), ("mn", MXU_DTYPE), ("pn", F32)],
        extras=[(h1, *tile(BIG, 1024)), (norm_ffn_g, *row_vec(1024)), (dh2, *tile(BIG, 1024))],
        epilogue=norm_ffn_bwd,
        comm=net.exchange(dict(w_ffn_down=[d_w_down])))
    net.received(names, (), got)
    d_norm_ffn = _colsum_partials(dg2_p)
    d_w_gate = _matmul("dw_gate", [(dgp, u2)], "tn", tm=FT, tn=1024, tk=1024, outs=grad_outs, epilogue=both)
    d_w_up = _matmul("dw_up", [(dup, u2)], "tn", tm=FT, tn=1024, tk=1024, outs=grad_outs, epilogue=both)

    def merge_bwd(acc, ya_v, yb_v, ga, gb):
        sa, sb = _sigmoid(ga), _sigmoid(gb)
        return acc * sa, acc * sb, acc * ya_v * sa * (1.0 - sa), acc * yb_v * sb * (1.0 - sb)

    dya, dyb, dga, dgb = _matmul(
        "d_merged", [(dh1b, w_out)], "nt", tm=TM, tn=1024, tk=1024, outs=[("mn", MXU_DTYPE)] * 4,
        extras=[(ya, *tile(TM, 1024)), (yb, *tile(TM, 1024)), gate_a, gate_b], epilogue=merge_bwd)
    tn_grad = functools.partial(_matmul, mode="tn", tn=1024, tk=1024, outs=grad_outs, epilogue=both)
    d_w_out = tn_grad("dw_out", [(merged, dh1b)], tm=1024)
    dy_attn, = _matmul("d_y_attn", [(dya, w_ba)], "nt", tm=TM, tn=1024, tk=1024, outs=[("mn", F32)])
    dy_hgrn, = _matmul("d_y_hgrn", [(dyb, w_bh)], "nt", tm=TM, tn=1024, tk=1024, outs=[("mn", F32)])
    d_w_ba = tn_grad("dw_branch_a", [(y_attn, dya)], tm=1024)
    d_w_bh = tn_grad("dw_branch_b", [(y_hgrn, dyb)], tm=1024)

    names, swap = ("w_ffn_gate",), ("w_ffn_down",)
    (dq, dkv, dsink), got = _attn_bwd(pq, pkv, sinks, lse, dy_attn,
                                      comm=net.exchange(dict(w_ffn_gate=[d_w_gate]), swap))
    net.received(names, swap, got)
    names, swap = ("w_ffn_up", "w_out", "w_branch_attn", "w_branch_hgrn"), ("w_ffn_gate",)
    (dph, d_hgrn_norm, d_lb_logits), got = _hgrn_bwd(
        ph, o_raw, states, dy_hgrn, lb_logits, hgrn_norm_g,
        comm=net.exchange(dict(w_ffn_up=[d_w_up], w_out=[d_w_out], w_branch_attn=[d_w_ba],
                               w_branch_hgrn=[d_w_bh]), swap))
    net.received(names, swap, got)

    d_w_in = [tn_grad("dw_in_q", [(dq, u)], tm=1024), tn_grad("dw_in_kv", [(dkv, u)], tm=256),
              tn_grad("dw_in_h", [(dph, u)], tm=1024), tn_grad("dw_in_ga", [(dga, u)], tm=1024),
              tn_grad("dw_in_gb", [(dgb, u)], tm=1024)]

    def norm_mix_bwd(acc, xin, g, dres):
        dx, dg = _rmsnorm_bwd_vals(acc, xin, g)
        return dres + dx, dg

    db_h, got = _colsum("db_h", dph, comm=net.presum_begin("w_in", d_w_in))
    halves = net.presum_end("w_in", got)
    names, swap = ("w_in",), ("w_ffn_up", "w_out", "w_branch_attn", "w_branch_hgrn")
    (dx, dg1_p), got = _matmul(
        "d_u", [(dq, w_q), (dkv, w_kv), (dph, w_h), (dga, w_g[:D]), (dgb, w_g[D:])], "nn",
        tm=BIG, tn=1024, tk=512, outs=[("mn", F32), ("pn", F32)],
        extras=[(x, *tile(BIG, 1024)), (norm_mix_g, *row_vec(1024)), (dh1, *tile(BIG, 1024))],
        epilogue=norm_mix_bwd,
        comm=_join(halves, net.swap(swap)))
    net.last = (names, swap, got)
    d_norm_mix = _colsum_partials(dg1_p)
    d_b_in = jnp.concatenate([_colsum("db_q", dq), _colsum("db_kv", dkv), db_h,
                              _colsum("db_ga", dga), _colsum("db_gb", dgb)], axis=1)
    vecs = dict(norm_mix_g=d_norm_mix, b_in=d_b_in, attn_sinks=jnp.sum(dsink.reshape(Q_HEADS, ATTN_BLOCK), axis=1).reshape(1, Q_HEADS),
                hgrn_lb_logits=d_lb_logits,
                hgrn_norm_g=d_hgrn_norm, norm_ffn_g=d_norm_ffn, norm_final_g=d_norm_final)
    return loss, dx, vecs


def _place():
    return lax.axis_index("x"), lax.axis_index("y"), lax.axis_index("c")


def _other_chips(x, y):
    return [(1 - x, y), (x, 1 - y), (1 - x, 1 - y)]


def _gather_copies(shards):
    n = len(shards)

    def build(ins, outs, send_sems, recv_sems, local_sems):
        x, y, c = _place()
        mine = 2 * x + y
        local = [pltpu.make_async_copy(ins[w], outs[w].at[mine], local_sems.at[w]) for w in range(n)]
        sends, recvs = [], []
        for w in range(n):
            for k, (px, py) in enumerate(_other_chips(x, y)):
                sem = 3 * w + k
                sends.append(pltpu.make_async_remote_copy(
                    src_ref=ins[w], dst_ref=outs[w].at[mine], send_sem=send_sems.at[sem], recv_sem=recv_sems.at[sem],
                    device_id=(px, py, c), device_id_type=MESH_ID))
                recvs.append(pltpu.make_async_remote_copy(
                    src_ref=ins[w], dst_ref=outs[w].at[2 * px + py], send_sem=send_sems.at[sem],
                    recv_sem=recv_sems.at[sem], device_id=(px, py, c), device_id_type=MESH_ID))
        return sends, recvs, local

    return _Carried(shards, [jax.ShapeDtypeStruct((N_CHIPS,) + s.shape, s.dtype) for s in shards], 3 * n, n, build)


def _grad_copies(stacked):
    n = len(stacked)

    def build(ins, outs, send_sems, recv_sems, local_sems):
        x, y, c = _place()
        sends = []
        for w in range(n):
            for k, (px, py) in enumerate(_other_chips(x, y)):
                sem = 3 * w + k
                sends.append(pltpu.make_async_remote_copy(
                    src_ref=ins[w].at[2 * px + py], dst_ref=outs[w].at[k], send_sem=send_sems.at[sem],
                    recv_sem=recv_sems.at[sem], device_id=(px, py, c), device_id_type=MESH_ID))
        return sends, sends, []

    return _Carried(stacked, [jax.ShapeDtypeStruct((3,) + s.shape[1:], s.dtype) for s in stacked], 3 * n, 0, build)


def _small_copies(small):
    def build(ins, outs, send_sems, recv_sems, local_sems):
        small_ref, all_ref = ins[0], outs[0]
        x, y, c = _place()
        me = 4 * x + 2 * y + c
        sends, recvs = [], []
        for r in range(1, 8):
            px = 1 - x if r & 4 else x
            py = 1 - y if r & 2 else y
            pc = 1 - c if r & 1 else c
            sends.append(pltpu.make_async_remote_copy(
                src_ref=small_ref, dst_ref=all_ref.at[me], send_sem=send_sems.at[r - 1], recv_sem=recv_sems.at[r - 1],
                device_id=(px, py, pc), device_id_type=MESH_ID))
            recvs.append(pltpu.make_async_remote_copy(
                src_ref=small_ref, dst_ref=all_ref.at[4 * px + 2 * py + pc], send_sem=send_sems.at[r - 1],
                recv_sem=recv_sems.at[r - 1], device_id=(px, py, pc), device_id_type=MESH_ID))
        return sends, recvs, [pltpu.make_async_copy(small_ref, all_ref.at[me], local_sems.at[0])]

    return _Carried([small], [jax.ShapeDtypeStruct((8,) + small.shape, small.dtype)], 7, 1, build)


def _gather_two_level(name, shards):
    n = len(shards)

    def body(*refs):
        ins, outs = refs[:n], refs[n:2 * n]
        send1, recv1, send2, recv2, local_sems = refs[2 * n:]
        x, y, c = _place()
        mine = 2 * x + y
        local = [pltpu.make_async_copy(ins[w], outs[w].at[mine], local_sems.at[w]) for w in range(n)]
        first, landed, relay, relayed = [], [], [], []
        for w in range(n):
            half = shards[w].shape[0] // 2
            my_half, other_half = pl.ds(c * half, half), pl.ds((1 - c) * half, half)
            for k, (px, py) in enumerate(_other_chips(x, y)):
                sem, chip = 3 * w + k, 2 * px + py
                first.append(pltpu.make_async_remote_copy(
                    src_ref=ins[w].at[my_half], dst_ref=outs[w].at[mine, my_half], send_sem=send1.at[sem],
                    recv_sem=recv1.at[sem], device_id=(px, py, c), device_id_type=MESH_ID))
                landed.append(pltpu.make_async_remote_copy(
                    src_ref=ins[w].at[my_half], dst_ref=outs[w].at[chip, my_half], send_sem=send1.at[sem],
                    recv_sem=recv1.at[sem], device_id=(px, py, c), device_id_type=MESH_ID))
                relay.append(pltpu.make_async_remote_copy(
                    src_ref=outs[w].at[chip, my_half], dst_ref=outs[w].at[chip, my_half], send_sem=send2.at[sem],
                    recv_sem=recv2.at[sem], device_id=(x, y, 1 - c), device_id_type=MESH_ID))
                relayed.append(pltpu.make_async_remote_copy(
                    src_ref=outs[w].at[chip, other_half], dst_ref=outs[w].at[chip, other_half], send_sem=send2.at[sem],
                    recv_sem=recv2.at[sem], device_id=(x, y, 1 - c), device_id_type=MESH_ID))
        for cp in local + first:
            cp.start()
        for arrived, onward in zip(landed, relay):
            arrived.wait_recv()
            onward.start()
        for cp in relayed:
            cp.wait_recv()
        for cp in first + relay:
            cp.wait_send()
        for cp in local:
            cp.wait()

    sems = pltpu.SemaphoreType.DMA((3 * n,))
    return pl.pallas_call(
        body, name=name, in_specs=[HBM_SPEC] * n, out_specs=[HBM_SPEC] * n,
        out_shape=[jax.ShapeDtypeStruct((N_CHIPS,) + s.shape, s.dtype) for s in shards],
        scratch_shapes=[sems, sems, sems, sems, pltpu.SemaphoreType.DMA((n,))],
    )(*shards)


def _copies_alone(name, comm):
    return _call(name, lambda: None, grid=(), in_specs=[], out_specs=[], out_shape=[], args=[], comm=comm)[1]


class _Net:
    def __init__(self, shards):
        self.shards = shards
        self.whole, self.own, self.theirs, self.sums, self.other = {}, {}, {}, {}, {}
        x, y, _ = _place()
        self.chip = 2 * x + y

    def gather(self, names):
        return _gather_copies([self.shards[n] for n in names])

    def gathered(self, names, got):
        for n, g in zip(names, got):
            self.whole[n] = g.reshape(-1, g.shape[-1])

    def full(self, name):
        return self.whole[name]

    def exchange(self, grads, swap=()):
        stacked = []
        for n, pieces in grads.items():
            keep = jnp.concatenate([p[0] for p in pieces], axis=0) if len(pieces) > 1 else pieces[0][0]
            send = jnp.concatenate([p[1] for p in pieces], axis=0) if len(pieces) > 1 else pieces[0][1]
            rows = keep.shape[0] // N_CHIPS
            self.own[n] = lax.dynamic_slice_in_dim(keep, self.chip * rows, rows, axis=0)
            stacked.append(send.reshape(N_CHIPS, rows, send.shape[-1]))
        return _join(_grad_copies(stacked), self.swap(swap))

    def swap(self, names):
        return _sibling_copies([self.sums[n] for n in names]) if names else None

    def presum_begin(self, name, pieces):
        keep = jnp.concatenate([p[0] for p in pieces], axis=0)
        send = jnp.concatenate([p[1] for p in pieces], axis=0)
        rows = keep.shape[0] // N_CHIPS
        self.held = keep.reshape(N_CHIPS, rows, keep.shape[-1])
        return _half_rows_copies(send.reshape(N_CHIPS, rows, send.shape[-1]))

    def presum_end(self, name, got):
        x, y, c = _place()
        to_send, self.own[name] = _pre_sum("presum_" + name, self.held, got[0], jnp.stack([c, self.chip]))
        return _grad_copies([to_send])

    def received(self, names, swap, got, carried=None):
        self.theirs.update(zip(names, got[:len(names)]))
        self.other.update(zip(swap, got[len(names):]))
        for n in names:
            (total,), more = _partial_sum("sum_" + n, self.own[n], self.theirs[n], comm=carried)
            rows = self.shards[n].shape[0]
            if total.shape[0] != rows:
                _, _, c = _place()
                total = lax.dynamic_update_slice(jnp.zeros((rows, total.shape[1]), F32), total, (c * (rows // 2), 0))
            self.sums[n] = total
        return more


def _half_rows_copies(stacked):
    n, rows = stacked.shape[0], stacked.shape[1] // 2

    def build(ins, outs, send_sems, recv_sems, local_sems):
        x, y, c = _place()
        copies = [pltpu.make_async_remote_copy(
            src_ref=ins[0].at[s, pl.ds((1 - c) * rows, rows)], dst_ref=outs[0].at[s], send_sem=send_sems.at[s],
            recv_sem=recv_sems.at[s], device_id=(x, y, 1 - c), device_id_type=MESH_ID) for s in range(n)]
        return copies, copies, []

    return _Carried([stacked], [jax.ShapeDtypeStruct((n, rows, stacked.shape[2]), stacked.dtype)], n, 0, build)


def _pre_sum(name, held, theirs, core_and_chip):
    n, R, C = held.shape
    half = R // 2
    tr = _row_tile(half)
    per_half = half // tr

    def body(place_ref, h_ref, t_ref, send_ref, own_ref):
        total = h_ref[0] + t_ref[0].astype(F32)
        send_ref[0] = total.astype(send_ref.dtype)

        @pl.when(pl.program_id(1) == place_ref[1])
        def _():
            own_ref[...] = total

    return pl.pallas_call(
        body, name=name,
        grid_spec=pltpu.PrefetchScalarGridSpec(
            num_scalar_prefetch=1, grid=(per_half, n),
            in_specs=[pl.BlockSpec((1, tr, C), lambda i, s, place: (s, place[0] * per_half + i, 0)),
                      pl.BlockSpec((1, tr, C), lambda i, s, place: (s, i, 0))],
            out_specs=[pl.BlockSpec((1, tr, C), lambda i, s, place: (s, i, 0)),
                       pl.BlockSpec((tr, C), lambda i, s, place: (i, 0))]),
        out_shape=[jax.ShapeDtypeStruct((n, half, C), MXU_DTYPE), jax.ShapeDtypeStruct((half, C), F32)],
        compiler_params=_params(("arbitrary", "arbitrary")),
    )(core_and_chip, held, theirs)


def _sibling_copies(parts):
    n = len(parts)

    def build(ins, outs, send_sems, recv_sems, local_sems):
        x, y, c = _place()
        copies = [pltpu.make_async_remote_copy(
            src_ref=ins[w], dst_ref=outs[w], send_sem=send_sems.at[w], recv_sem=recv_sems.at[w],
            device_id=(x, y, 1 - c), device_id_type=MESH_ID) for w in range(n)]
        return copies, copies, []

    return _Carried(parts, [jax.ShapeDtypeStruct(p.shape, p.dtype) for p in parts], n, 0, build)


def _row_tile(rows, most=512, sublanes=16):
    return max(t for t in range(sublanes, most + 1, sublanes) if rows % t == 0)


def _partial_sum(name, own, recv, comm=None):
    R, C = own.shape
    tr = _row_tile(R, most=128)

    def body(o_ref, r_ref, p_ref):
        p_ref[...] = ((o_ref[...] + r_ref[0].astype(F32)) + r_ref[1].astype(F32)) + r_ref[2].astype(F32)

    return _call(name, body, grid=(R // tr,),
                 in_specs=[pl.BlockSpec((tr, C), lambda i: (i, 0)), pl.BlockSpec((3, tr, C), lambda i: (0, i, 0))],
                 out_specs=[pl.BlockSpec((tr, C), lambda i: (i, 0))], out_shape=[jax.ShapeDtypeStruct((R, C), F32)],
                 args=[own, recv], semantics=("parallel",), comm=comm)


def _adam_vals(w, g, m, v):
    m = ADAM_B1 * m + (1.0 - ADAM_B1) * g
    v = ADAM_B2 * v + (1.0 - ADAM_B2) * (g * g)
    m_hat = m / (1.0 - ADAM_B1 ** ADAM_STEP)
    v_hat = v / (1.0 - ADAM_B2 ** ADAM_STEP)
    delta = -ADAM_LR * (m_hat / (jnp.sqrt(v_hat) + ADAM_EPS) + ADAM_WD * w)
    return delta, m, v


def _adamw(name, w, m, v, mine, other, comm=None):
    R, C = w.shape
    tr = _row_tile(R)

    def body(w_ref, m_ref, v_ref, s_ref, n_ref, g_ref, d_ref, nm_ref, nv_ref):
        g = s_ref[...] + n_ref[...]
        d, nm, nv = _adam_vals(w_ref[...], g, m_ref[...], v_ref[...])
        g_ref[...], d_ref[...], nm_ref[...], nv_ref[...] = g, d, nm, nv

    spec = pl.BlockSpec((tr, C), lambda i: (i, 0))
    return _call(name, body, grid=(R // tr,), in_specs=[spec] * 5, out_specs=[spec] * 4,
                 out_shape=[jax.ShapeDtypeStruct((R, C), F32)] * 4, args=[w, m, v, mine, other],
                 semantics=("parallel",), comm=comm)


def _adamw_small(w, m, v, g_all):
    def body(w_ref, m_ref, v_ref, a_ref, g_ref, d_ref, nm_ref, nv_ref):
        g = a_ref[0]
        for dev in range(1, 8):
            g = g + a_ref[dev]
        d, nm, nv = _adam_vals(w_ref[...], g, m_ref[...], v_ref[...])
        g_ref[...], d_ref[...], nm_ref[...], nv_ref[...] = g, d, nm, nv

    return pl.pallas_call(
        body, name="adamw_small", out_shape=[jax.ShapeDtypeStruct(w.shape, F32)] * 4,
    )(w, m, v, g_all)


SMALL_LAYOUT = dict(norm_mix_g=(0, 1024), b_in=(1024, 7424), hgrn_norm_g=(8448, 1024), norm_ffn_g=(9472, 1024),
                    norm_final_g=(10496, 1024), hgrn_lb_logits=(11520, 2048), attn_sinks=(13568, 16))
SMALL_LOSS, SMALL_SIZE = 13584, 16 * D_MODEL


def _pack_small(vals, loss=None):
    names = list(SMALL_LAYOUT)
    tail = jnp.zeros((SMALL_SIZE - SMALL_LAYOUT[names[-1]][0],), F32)
    tail = lax.dynamic_update_slice(tail, vals[names[-1]].astype(F32).reshape(-1), (0,))
    if loss is not None:
        tail = lax.dynamic_update_slice(tail, loss.astype(F32).reshape(1), (SMALL_LOSS - SMALL_LAYOUT[names[-1]][0],))
    flat = jnp.concatenate([vals[n].astype(F32).reshape(-1) for n in names[:-1]] + [tail])
    return flat.reshape(-1, D_MODEL)


def _unpack_small(packed, shapes):
    flat = packed.reshape(-1)
    return {name: flat[lo:lo + n].reshape(shapes[name]) for name, (lo, n) in SMALL_LAYOUT.items()}


MATRICES = ("w_in", "w_branch_attn", "w_branch_hgrn", "w_out", "w_ffn_gate", "w_ffn_up", "w_ffn_down")
COLUMN_SHARDED = ("w_in", "w_ffn_gate", "w_ffn_up")
WEIGHTS = ("norm_mix_g", "w_in", "b_in", "attn_sinks", "hgrn_lb_logits", "hgrn_norm_g", "w_branch_attn",
           "w_branch_hgrn", "w_out", "norm_ffn_g", "w_ffn_gate", "w_ffn_up", "w_ffn_down", "norm_final_g")


def kernel(x, norm_mix_g, w_in, b_in, attn_sinks, hgrn_lb_logits, hgrn_norm_g, w_branch_attn, w_branch_hgrn, w_out, norm_ffn_g, w_ffn_gate, w_ffn_up, w_ffn_down, norm_final_g, loss_target, m_norm_mix_g, m_w_in, m_b_in, m_attn_sinks, m_hgrn_lb_logits, m_hgrn_norm_g, m_w_branch_attn, m_w_branch_hgrn, m_w_out, m_norm_ffn_g, m_w_ffn_gate, m_w_ffn_up, m_w_ffn_down, m_norm_final_g, v_norm_mix_g, v_w_in, v_b_in, v_attn_sinks, v_hgrn_lb_logits, v_hgrn_norm_g, v_w_branch_attn, v_w_branch_hgrn, v_w_out, v_norm_ffn_g, v_w_ffn_gate, v_w_ffn_up, v_w_ffn_down, v_norm_final_g):
    given = dict(locals())
    w = {n: given[n] for n in WEIGHTS}
    m = {n: given["m_" + n] for n in WEIGHTS}
    v = {n: given["v_" + n] for n in WEIGHTS}

    block = lambda a, n: jnp.transpose(a[0]) if n in COLUMN_SHARDED else a[0]
    unblock = lambda a, n: (jnp.transpose(a) if n in COLUMN_SHARDED else a)[None]
    net = _Net({n: block(w[n], n).astype(MXU_DTYPE) for n in MATRICES})
    net.gathered(("w_in",), _gather_two_level("gather_w_in", [net.shards["w_in"]]))
    vec = dict(norm_mix_g=norm_mix_g, b_in=b_in, attn_sinks=attn_sinks, hgrn_lb_logits=hgrn_lb_logits,
               hgrn_norm_g=hgrn_norm_g, norm_ffn_g=norm_ffn_g, norm_final_g=norm_final_g.reshape(1, D_MODEL))
    loss_part, dx, d_vecs = _local_step(x[0], loss_target[0], vec, net)

    small_all, = net.received(*net.last, carried=_small_copies(_pack_small(d_vecs, loss_part)))
    grads, deltas, new_m, new_v = {}, {}, {}, {}
    for n in ("w_ffn_down", "w_ffn_gate", "w_ffn_up", "w_out", "w_branch_attn", "w_branch_hgrn", "w_in"):
        res, got = _adamw("adamw_" + n, block(w[n], n), block(m[n], n), block(v[n], n), net.sums[n], net.other[n],
                          comm=net.swap(("w_in",)) if n == "w_ffn_down" else None)
        if n == "w_ffn_down":
            net.other["w_in"], = got
        grads[n], deltas[n], new_m[n], new_v[n] = (unblock(r, n) for r in res)
    shapes = {n: w[n].shape for n in SMALL_LAYOUT}
    res = _adamw_small(_pack_small(w), _pack_small(m), _pack_small(v), small_all)
    for dst, packed in zip((grads, deltas, new_m, new_v), res):
        dst.update(_unpack_small(packed, shapes))

    loss = res[0].reshape(-1)[SMALL_LOSS]
    return (loss, dx[None], *[grads[n] for n in WEIGHTS], *[deltas[n] for n in WEIGHTS],
            *[new_m[n] for n in WEIGHTS], *[new_v[n] for n in WEIGHTS])
```

```python
import functools
import math

import jax
import jax.numpy as jnp
from jax import lax
from jax.experimental import pallas as pl
from jax.experimental.pallas import tpu as pltpu

F32 = jnp.float32
BF16 = jnp.bfloat16
MXU_DTYPE = jnp.bfloat16
MESH_ID = pl.DeviceIdType.MESH

D_MODEL = 1024
HEAD_DIM = 64
Q_HEADS = 16
KV_HEADS = 2
GROUP = Q_HEADS // KV_HEADS
KV_WIDTH = KV_HEADS * HEAD_DIM
ATTN_BLOCK = 128
HGRN_HEADS = 8
HGRN_K = 128
CHUNK = 64
HGRN_TOKENS = 256
FFN = 2816
IN_SPLITS = (1024, 256, 4096, 2048)
EPS = 1e-6
NEG_INF = -1e30
ADAM_LR, ADAM_B1, ADAM_B2, ADAM_EPS, ADAM_WD, ADAM_STEP = 0.001, 0.9, 0.999, 1e-08, 0.01, 10
N_CHIPS = 4
VMEM_LIMIT = 60 * 1024 * 1024


def _params(sem=None):
    return pltpu.CompilerParams(dimension_semantics=sem, vmem_limit_bytes=VMEM_LIMIT)


def _sigmoid(v):
    return 0.5 * jnp.tanh(0.5 * v) + 0.5


def _dot(a, b, dims):
    return lax.dot_general(a.astype(MXU_DTYPE), b.astype(MXU_DTYPE), (dims, ((), ())),
                           preferred_element_type=F32)


def _nn(a, b):
    return _dot(a, b, ((1,), (0,)))


def _nt(a, b):
    return _dot(a, b, ((1,), (1,)))


def _tn(a, b):
    return _dot(a, b, ((0,), (0,)))


HBM_SPEC = pl.BlockSpec(memory_space=pl.ANY)


class _Carried:
    def __init__(self, arrays, out_shapes, n_remote, n_local, build):
        self.parts = [(len(arrays), len(out_shapes), build)]
        self.arrays, self.out_shapes = list(arrays), list(out_shapes)
        self.scratch = [pltpu.SemaphoreType.DMA((n_remote,)), pltpu.SemaphoreType.DMA((n_remote,)),
                        pltpu.SemaphoreType.DMA((max(n_local, 1),))]

    def __add__(self, other):
        both = _Carried([], [], 1, 0, None)
        both.parts = self.parts + other.parts
        both.arrays, both.out_shapes = self.arrays + other.arrays, self.out_shapes + other.out_shapes
        both.scratch = self.scratch + other.scratch
        return both

    def _built(self, ins, outs, sems):
        for p, (ni, no, build) in enumerate(self.parts):
            yield build(ins[:ni], outs[:no], *sems[3 * p:3 * p + 3])
            ins, outs = ins[ni:], outs[no:]

    def start(self, ins, outs, sems):
        for sends, _, local in self._built(ins, outs, sems):
            for cp in local + sends:
                cp.start()

    def wait(self, ins, outs, sems):
        for sends, recvs, local in self._built(ins, outs, sems):
            for cp in recvs:
                cp.wait_recv()
            for cp in sends:
                cp.wait_send()
            for cp in local:
                cp.wait()


def _join(*comms):
    comms = [c for c in comms if c is not None]
    return functools.reduce(lambda a, b: a + b, comms) if comms else None


def _call(name, body, *, grid, in_specs, out_specs, out_shape, args, scratch=(), semantics=None, comm=None):
    n_in, n_out, n_scr = len(in_specs), len(out_specs), len(scratch)
    if comm is None:
        res = pl.pallas_call(body, name=name, grid=grid, in_specs=in_specs, out_specs=out_specs, out_shape=out_shape,
                             scratch_shapes=list(scratch), compiler_params=_params(semantics))(*args)
        return list(res), []
    ci, co = len(comm.arrays), len(comm.out_shapes)

    def carrying(*refs):
        ins, refs = refs[:n_in], refs[n_in:]
        c_ins, refs = refs[:ci], refs[ci:]
        outs, refs = refs[:n_out], refs[n_out:]
        c_outs, refs = refs[:co], refs[co:]
        scr, sems = refs[:n_scr], refs[n_scr:]
        if not grid:
            comm.start(c_ins, c_outs, sems)
            body(*ins, *outs, *scr)
            comm.wait(c_ins, c_outs, sems)
            return
        first = functools.reduce(jnp.logical_and, [pl.program_id(a) == 0 for a in range(len(grid))])
        last = functools.reduce(jnp.logical_and, [pl.program_id(a) == g - 1 for a, g in enumerate(grid)])

        @pl.when(first)
        def _():
            comm.start(c_ins, c_outs, sems)

        body(*ins, *outs, *scr)

        @pl.when(last)
        def _():
            comm.wait(c_ins, c_outs, sems)

    res = pl.pallas_call(
        carrying, name=name, grid=grid, in_specs=list(in_specs) + [HBM_SPEC] * ci,
        out_specs=list(out_specs) + [HBM_SPEC] * co, out_shape=list(out_shape) + comm.out_shapes,
        scratch_shapes=list(scratch) + comm.scratch,
        compiler_params=_params(("arbitrary",) * len(grid) if grid else None),
    )(*args, *comm.arrays)
    return list(res[:n_out]), list(res[n_out:])


_NO_COPIES = object()


def _matmul(name, pairs, mode, *, tm, tn, tk, outs, extras=(), epilogue=None, a_colsum=False, comm=_NO_COPIES):
    prod = dict(nn=_nn, nt=_nt, tn=_tn)[mode]
    a0, b0 = pairs[0]
    M = a0.shape[1] if mode == "tn" else a0.shape[0]
    N = b0.shape[0] if mode == "nt" else b0.shape[1]
    steps, in_specs, offset = [], [], 0
    for a, b in pairs:
        K = a.shape[0] if mode == "tn" else a.shape[1]
        t = min(tk, K)
        assert K % t == 0, (name, K, t)
        kmap = functools.partial(lambda k, off, n: jnp.clip(k - off, 0, n - 1), off=offset, n=K // t)
        if mode == "tn":
            in_specs.append(pl.BlockSpec((t, tm), functools.partial(lambda i, j, k, f: (f(k), i), f=kmap)))
        else:
            in_specs.append(pl.BlockSpec((tm, t), functools.partial(lambda i, j, k, f: (i, f(k)), f=kmap)))
        if mode == "nt":
            in_specs.append(pl.BlockSpec((tn, t), functools.partial(lambda i, j, k, f: (j, f(k)), f=kmap)))
        else:
            in_specs.append(pl.BlockSpec((t, tn), functools.partial(lambda i, j, k, f: (f(k), j), f=kmap)))
        steps.append((offset, offset + K // t))
        offset += K // t
    assert M % tm == 0 and N % tn == 0, (name, M, N, tm, tn)
    ni, nj, nk = M // tm, N // tn, offset
    npair, ne, no = len(pairs), len(extras), len(outs)
    if epilogue is None:
        epilogue = lambda acc: (acc,)

    def finish(acc, extra_refs, out_refs):
        vals = epilogue(acc, *[r[...] for r in extra_refs])
        for (kind, _), o_ref, val in zip(outs, out_refs, vals):
            if kind == "pn":
                val = jnp.broadcast_to(val, o_ref.shape)
            o_ref[...] = val.astype(o_ref.dtype)

    def body(*refs):
        ab, rest = refs[:2 * npair], refs[2 * npair:]
        extra_refs, out_refs = rest[:ne], rest[ne:ne + no]
        if nk == 1:
            finish(prod(ab[0][...], ab[1][...]), extra_refs, out_refs)
            return
        acc_ref = rest[-1]
        k = pl.program_id(2)

        @pl.when(k == 0)
        def _():
            acc_ref[...] = jnp.zeros_like(acc_ref)
            if a_colsum:
                rest[ne + no][...] = jnp.zeros((1, tm), F32)

        if a_colsum:
            rest[ne + no][...] += jnp.sum(ab[0][...].astype(F32), axis=0, keepdims=True)
        for p, (lo, hi) in enumerate(steps):
            @pl.when(jnp.logical_and(k >= lo, k < hi))
            def _():
                acc_ref[...] += prod(ab[2 * p][...], ab[2 * p + 1][...])

        @pl.when(k == nk - 1)
        def _():
            finish(acc_ref[...], extra_refs, out_refs)

    for _, shape, im in extras:
        in_specs.append(pl.BlockSpec(shape, functools.partial(lambda i, j, k, im: im(i, j), im=im)))
    out_shape, out_specs = [], []
    for kind, dt in outs:
        if kind == "mn":
            out_shape.append(jax.ShapeDtypeStruct((M, N), dt))
            out_specs.append(pl.BlockSpec((tm, tn), lambda i, j, k: (i, j)))
        else:
            out_shape.append(jax.ShapeDtypeStruct((8 * ni, N), dt))
            out_specs.append(pl.BlockSpec((8, tn), lambda i, j, k: (i, j)))
    if a_colsum:
        assert mode == "tn" and npair == 1 and nj == 1 and nk > 1, name
        out_shape.append(jax.ShapeDtypeStruct((1, M), F32))
        out_specs.append(pl.BlockSpec((1, tm), lambda i, j, k: (0, i)))
    grid = (ni, nj, nk)
    if nj > 1 and nk == 1:
        turned = lambda spec: pl.BlockSpec(spec.block_shape, functools.partial(
            lambda j, i, k, im: im(i, j, k), im=spec.index_map))
        in_specs, out_specs, grid = [turned(s) for s in in_specs], [turned(s) for s in out_specs], (nj, ni, nk)
    res, got = _call(name, body, grid=grid, in_specs=in_specs, out_specs=out_specs, out_shape=out_shape,
                     args=[t for pair in pairs for t in pair] + [e[0] for e in extras],
                     scratch=[pltpu.VMEM((tm, tn), F32)] if nk > 1 else [],
                     semantics=("parallel", "parallel", "arbitrary"), comm=None if comm is _NO_COPIES else comm)
    return res if comm is _NO_COPIES else (res, got)


def _swiglu_fwd(u, w_gate_t, w_up_t, *, tm, tn, comm=None):
    (T, D), F = u.shape, w_gate_t.shape[0]

    def body(u_ref, wg_ref, wu_ref, g_ref, up_ref, z_ref):
        g, up = _nt(u_ref[...], wg_ref[...]), _nt(u_ref[...], wu_ref[...])
        g_ref[...], up_ref[...] = g, up
        z_ref[...] = (g * _sigmoid(g) * up).astype(z_ref.dtype)

    w_spec = pl.BlockSpec((tn, D), lambda j, i: (j, 0))
    o_spec = pl.BlockSpec((tm, tn), lambda j, i: (i, j))
    return _call("ffn_hidden", body, grid=(F // tn, T // tm),
                 in_specs=[pl.BlockSpec((tm, D), lambda j, i: (i, 0)), w_spec, w_spec], out_specs=[o_spec] * 3,
                 out_shape=[jax.ShapeDtypeStruct((T, F), F32)] * 2 + [jax.ShapeDtypeStruct((T, F), MXU_DTYPE)],
                 args=[u, w_gate_t, w_up_t], semantics=("parallel", "parallel"), comm=comm)


def _colsum_partials(p):
    return jnp.sum(p.reshape(-1, 8, p.shape[-1])[:, 0, :], axis=0, keepdims=True)


def _rmsnorm_fwd(name, x, g, tr=512):
    T, D = x.shape

    def body(x_ref, g_ref, u_ref):
        xv = x_ref[...]
        r = lax.rsqrt(jnp.mean(xv * xv, axis=-1, keepdims=True) + EPS)
        u_ref[...] = (xv * r * g_ref[...]).astype(u_ref.dtype)

    return pl.pallas_call(
        body, name=name, grid=(T // tr,),
        in_specs=[pl.BlockSpec((tr, D), lambda i: (i, 0)), pl.BlockSpec((1, D), lambda i: (0, 0))],
        out_specs=pl.BlockSpec((tr, D), lambda i: (i, 0)),
        out_shape=jax.ShapeDtypeStruct((T, D), MXU_DTYPE),
        compiler_params=_params(("parallel",)),
    )(x, g)


def _rmsnorm_bwd_vals(dy, xin, g):
    rstd = lax.rsqrt(jnp.mean(xin * xin, axis=-1, keepdims=True) + EPS)
    xhat = xin * rstd
    dg = jnp.sum(dy * xhat, axis=0, keepdims=True)
    dxh = dy * g
    dx = rstd * (dxh - xhat * jnp.mean(dxh * xhat, axis=-1, keepdims=True))
    return dx, dg


def _colsum(name, a, tr=512, comm=_NO_COPIES):
    T, N = a.shape

    def body(a_ref, o_ref):
        @pl.when(pl.program_id(0) == 0)
        def _():
            o_ref[...] = jnp.zeros_like(o_ref)

        o_ref[...] += jnp.sum(a_ref[...].astype(F32), axis=0, keepdims=True)

    (res,), got = _call(name, body, grid=(T // tr,), in_specs=[pl.BlockSpec((tr, N), lambda i: (i, 0))],
                        out_specs=[pl.BlockSpec((1, N), lambda i: (0, 0))],
                        out_shape=[jax.ShapeDtypeStruct((1, N), F32)], args=[a], semantics=("arbitrary",),
                        comm=None if comm is _NO_COPIES else comm)
    return res if comm is _NO_COPIES else (res, got)


ATTN_SCALE = 1.0 / math.sqrt(HEAD_DIM)
GROUP_LANES = GROUP * ATTN_BLOCK
PAIR = 2 * HEAD_DIM


def _attn_mask():
    kj = lax.broadcasted_iota(jnp.int32, (ATTN_BLOCK, GROUP_LANES), 0)
    qi = lax.broadcasted_iota(jnp.int32, (ATTN_BLOCK, GROUP_LANES), 1) & (ATTN_BLOCK - 1)
    return kj <= qi


def _heads_transposed(ref, g, scale=None):
    parts = []
    for a in range(GROUP // 2):
        lo = (g * GROUP // 2 + a) * PAIR
        pair = ref[:, lo:lo + PAIR].astype(F32)
        pair = (pair if scale is None else pair * scale).T
        parts += [pair[:HEAD_DIM], pair[HEAD_DIM:]]
    return jnp.concatenate(parts, axis=1).astype(MXU_DTYPE)


def _heads_back(ref, g, vt):
    for a in range(GROUP // 2):
        lo = (g * GROUP // 2 + a) * PAIR
        pair = jnp.concatenate([vt[:, (2 * a) * ATTN_BLOCK:(2 * a + 1) * ATTN_BLOCK],
                                vt[:, (2 * a + 1) * ATTN_BLOCK:(2 * a + 2) * ATTN_BLOCK]], axis=0)
        ref[:, lo:lo + PAIR] = pair.T.astype(ref.dtype)


def _kv_parts(kv_ref, g):
    ks = slice(g * HEAD_DIM, (g + 1) * HEAD_DIM)
    vs = slice(KV_WIDTH + g * HEAD_DIM, KV_WIDTH + (g + 1) * HEAD_DIM)
    return kv_ref[:, ks].astype(MXU_DTYPE), kv_ref[:, vs].astype(MXU_DTYPE)


def _sink_rows(sinks):
    return jnp.repeat(sinks.reshape(KV_HEADS, GROUP), ATTN_BLOCK, axis=1)


def _attn_fwd(pq, pkv, sinks, comm=None):
    T = pq.shape[0]
    nb = T // ATTN_BLOCK

    def body(q_ref, kvc_ref, kvp_ref, s_ref, y_ref, lse_ref):
        mask_c = _attn_mask()
        has_prev = pl.program_id(0) > 0
        for g in range(KV_HEADS):
            (kc, vc), (kp, vp) = _kv_parts(kvc_ref, g), _kv_parts(kvp_ref, g)
            qt = _heads_transposed(q_ref, g, ATTN_SCALE)
            s = jnp.where(mask_c, _nn(kc, qt), jnp.where(has_prev, _nn(kp, qt), NEG_INF))
            sink = s_ref[g:g + 1, :]
            m = jnp.maximum(jnp.max(s, axis=0, keepdims=True), sink)
            p = jnp.exp(s - m)
            den = jnp.sum(p, axis=0, keepdims=True) + jnp.exp(sink - m)
            pc = jnp.where(mask_c, p, 0.0)
            _heads_back(y_ref, g, (_tn(vc, pc) + _tn(vp, p - pc)) / den)
            lse = m + jnp.log(den)
            for i in range(GROUP):
                lse_ref[g * GROUP + i:g * GROUP + i + 1, :] = lse[:, i * ATTN_BLOCK:(i + 1) * ATTN_BLOCK]

    return _call(
        "attn_fwd", body, grid=(nb,),
        in_specs=[pl.BlockSpec((ATTN_BLOCK, D_MODEL), lambda n: (n, 0)),
                  pl.BlockSpec((ATTN_BLOCK, 2 * KV_WIDTH), lambda n: (n, 0)),
                  pl.BlockSpec((ATTN_BLOCK, 2 * KV_WIDTH), lambda n: (jnp.maximum(n - 1, 0), 0)),
                  pl.BlockSpec((KV_HEADS, GROUP_LANES), lambda n: (0, 0))],
        out_specs=[pl.BlockSpec((ATTN_BLOCK, D_MODEL), lambda n: (n, 0)),
                   pl.BlockSpec((Q_HEADS, ATTN_BLOCK), lambda n: (0, n))],
        out_shape=[jax.ShapeDtypeStruct((T, D_MODEL), MXU_DTYPE), jax.ShapeDtypeStruct((Q_HEADS, T), F32)],
        args=[pq, pkv, pkv, _sink_rows(sinks)], semantics=("parallel",), comm=comm)


def _attn_bwd(pq, pkv, sinks, lse, dy, comm=None):
    T = pq.shape[0]
    nb = T // ATTN_BLOCK
    cur = lambda n: (jnp.minimum(n, nb - 1), 0)

    def body(q_ref, kvc_ref, kvp_ref, s_ref, lse_ref, dy_ref, dq_ref, dkv_ref, ds_ref, carry, top, bot):
        n = pl.program_id(0)

        @pl.when(n == 0)
        def _():
            carry[...] = jnp.zeros_like(carry)
            ds_ref[...] = jnp.zeros_like(ds_ref)

        @pl.when(n < nb)
        def _():
            mask_c = _attn_mask()
            valid = jnp.logical_or(mask_c, n > 0)
            for g in range(KV_HEADS):
                ks = slice(g * HEAD_DIM, (g + 1) * HEAD_DIM)
                vs = slice(KV_WIDTH + g * HEAD_DIM, KV_WIDTH + (g + 1) * HEAD_DIM)
                (kc, vc), (kp, vp) = _kv_parts(kvc_ref, g), _kv_parts(kvp_ref, g)
                qt = _heads_transposed(q_ref, g, ATTN_SCALE)
                dot = _heads_transposed(dy_ref, g)
                lse = jnp.concatenate([lse_ref[g * GROUP + i:g * GROUP + i + 1, :] for i in range(GROUP)], axis=1)
                p = jnp.where(valid, jnp.exp(jnp.where(mask_c, _nn(kc, qt), _nn(kp, qt)) - lse), 0.0)
                dp = jnp.where(mask_c, _nn(vc, dot), _nn(vp, dot))
                delta = jnp.sum(p * dp, axis=0, keepdims=True)
                ds = p * (dp - delta)
                ds_c, p_c = jnp.where(mask_c, ds, 0.0), jnp.where(mask_c, p, 0.0)
                ds_p, p_p = ds - ds_c, p - p_c
                _heads_back(dq_ref, g, (_tn(kc, ds_c) + _tn(kp, ds_p)) * ATTN_SCALE)
                bot[:, ks], bot[:, vs] = _nt(ds_c, qt), _nt(p_c, dot)
                top[:, ks], top[:, vs] = _nt(ds_p, qt), _nt(p_p, dot)
                ds_ref[g:g + 1, :] -= jnp.exp(s_ref[g:g + 1, :] - lse) * delta
            dkv_ref[...] = (carry[...] + top[...]).astype(dkv_ref.dtype)
            carry[...] = bot[...]

        @pl.when(n == nb)
        def _():
            dkv_ref[...] = carry[...].astype(dkv_ref.dtype)

    return _call(
        "attn_bwd", body, grid=(nb + 1,),
        in_specs=[pl.BlockSpec((ATTN_BLOCK, D_MODEL), cur),
                  pl.BlockSpec((ATTN_BLOCK, 2 * KV_WIDTH), cur),
                  pl.BlockSpec((ATTN_BLOCK, 2 * KV_WIDTH), lambda n: (jnp.maximum(jnp.minimum(n, nb - 1) - 1, 0), 0)),
                  pl.BlockSpec((KV_HEADS, GROUP_LANES), lambda n: (0, 0)),
                  pl.BlockSpec((Q_HEADS, ATTN_BLOCK), lambda n: (0, jnp.minimum(n, nb - 1))),
                  pl.BlockSpec((ATTN_BLOCK, D_MODEL), cur)],
        out_specs=[pl.BlockSpec((ATTN_BLOCK, D_MODEL), cur),
                   pl.BlockSpec((ATTN_BLOCK, 2 * KV_WIDTH), lambda n: (jnp.maximum(n - 1, 0), 0)),
                   pl.BlockSpec((KV_HEADS, GROUP_LANES), lambda n: (0, 0))],
        out_shape=[jax.ShapeDtypeStruct((T, D_MODEL), MXU_DTYPE),
                   jax.ShapeDtypeStruct((T, 2 * KV_WIDTH), MXU_DTYPE),
                   jax.ShapeDtypeStruct((KV_HEADS, GROUP_LANES), F32)],
        scratch=[pltpu.VMEM((ATTN_BLOCK, 2 * KV_WIDTH), F32)] * 3,
        args=[pq, pkv, pkv, _sink_rows(sinks), lse, dy], semantics=("arbitrary",), comm=comm)


def _lower_bound(l):
    m = jnp.maximum(l[0:1], l[1:2])
    e0, e1 = jnp.exp(l[0:1] - m), jnp.exp(l[1:2] - m)
    return e0 / (e0 + e1)


def _tri(lower):
    r = lax.broadcasted_iota(jnp.int32, (CHUNK, CHUNK), 0)
    c = lax.broadcasted_iota(jnp.int32, (CHUNK, CHUNK), 1)
    return (r >= c) if lower else (c >= r)


def _chunk_sum(mask, v):
    ones = mask.astype(BF16)
    hi = v.astype(BF16)
    rest = v - hi.astype(F32)
    mid = rest.astype(BF16)
    lo = (rest - mid.astype(F32)).astype(BF16)
    part = lambda t: lax.dot_general(ones, t, (((1,), (0,)), ((), ())), preferred_element_type=F32)
    return part(hi) + part(mid) + part(lo)


def _hgrn_chunk_inputs(hq, hf, lb, causal):
    sg, sgn = _sigmoid(hf), _sigmoid(-hf)
    f = lb + (1.0 - lb) * sg
    kk = (1.0 - lb) * sgn
    sq = _sigmoid(hq)
    q = hq * sq
    b = _chunk_sum(causal, jnp.log(f))
    bm, bl = b[CHUNK // 2 - 1:CHUNK // 2, :], b[CHUNK - 1:CHUNK, :]
    e_qm, e_km, e_qs, e_kl = jnp.exp(b - bm), jnp.exp(bm - b), jnp.exp(b), jnp.exp(bl - b)
    return dict(sg=sg, sgn=sgn, f=f, kk=kk, sq=sq, q=q, e_qm=e_qm, e_km=e_km, e_qs=e_qs, e_kl=e_kl,
                qm=q * e_qm, km=kk * e_km, qs=q * e_qs, kl=kk * e_kl, el=jnp.exp(bl))


def _hgrn_fwd(ph, lb_logits, norm_g, comm=None):
    T = ph.shape[0]
    nblk, cpb = T // HGRN_TOKENS, HGRN_TOKENS // CHUNK
    col = lambda c: pl.BlockSpec((HGRN_TOKENS, D_MODEL), functools.partial(lambda i, c: (i, c), c=c))

    def body(hq_ref, hf_ref, hi_ref, hg_ref, l_ref, ng_ref, y_ref, o_ref, st_ref, s_ref):
        @pl.when(pl.program_id(0) == 0)
        def _():
            s_ref[...] = jnp.zeros_like(s_ref)

        lb = _lower_bound(l_ref[...])
        causal = _tri(True)
        for c in range(cpb):
            rows = slice(c * CHUNK, (c + 1) * CHUNK)
            t = _hgrn_chunk_inputs(hq_ref[rows, :], hf_ref[rows, :], lb, causal)
            qm, km, qs, kl = (t[n].astype(MXU_DTYPE) for n in ("qm", "km", "qs", "kl"))
            v = hi_ref[rows, :].astype(MXU_DTYPE)
            for h in range(HGRN_HEADS):
                ls = slice(h * HGRN_K, (h + 1) * HGRN_K)
                st = s_ref[h]
                st_ref[c, ls, :] = st
                a = jnp.where(causal, _nt(qm[:, ls], km[:, ls]), 0.0)
                o_ref[rows, ls] = _nn(a, v[:, ls]) + _nt(qs[:, ls], st)
                s_ref[h] = t["el"][:, ls] * st + _tn(v[:, ls], kl[:, ls])
        for h in range(HGRN_HEADS):
            ls = slice(h * HGRN_K, (h + 1) * HGRN_K)
            o = o_ref[:, ls]
            r = lax.rsqrt(jnp.mean(o * o, axis=-1, keepdims=True) + EPS)
            y_ref[:, ls] = (o * r * ng_ref[:, ls] * _sigmoid(hg_ref[:, ls])).astype(y_ref.dtype)

    return _call(
        "hgrn_fwd", body, grid=(nblk,),
        in_specs=[col(0), col(1), col(2), col(3),
                  pl.BlockSpec((2, D_MODEL), lambda i: (0, 0)), pl.BlockSpec((1, D_MODEL), lambda i: (0, 0))],
        out_specs=[pl.BlockSpec((HGRN_TOKENS, D_MODEL), lambda i: (i, 0)),
                   pl.BlockSpec((HGRN_TOKENS, D_MODEL), lambda i: (i, 0)),
                   pl.BlockSpec((cpb, D_MODEL, HGRN_K), lambda i: (i, 0, 0))],
        out_shape=[jax.ShapeDtypeStruct((T, D_MODEL), MXU_DTYPE), jax.ShapeDtypeStruct((T, D_MODEL), F32),
                   jax.ShapeDtypeStruct((T // CHUNK, D_MODEL, HGRN_K), F32)],
        scratch=[pltpu.VMEM((HGRN_HEADS, HGRN_K, HGRN_K), F32)],
        args=[ph, ph, ph, ph, lb_logits, norm_g], semantics=("arbitrary",), comm=comm)


def _hgrn_bwd(ph, o_raw, states, dy, lb_logits, norm_g, comm=None):
    T = ph.shape[0]
    nblk, cpb = T // HGRN_TOKENS, HGRN_TOKENS // CHUNK
    rev = lambda i: nblk - 1 - i
    col = lambda c: pl.BlockSpec((HGRN_TOKENS, D_MODEL), functools.partial(lambda i, c: (rev(i), c), c=c))
    tok = pl.BlockSpec((HGRN_TOKENS, D_MODEL), lambda i: (rev(i), 0))

    def body(hq_ref, hf_ref, hi_ref, hg_ref, o_ref, st_ref, dy_ref, l_ref, ng_ref,
             dph_ref, dng_ref, dl_ref, dst_ref, dlb_ref, do_s, dqm_s, dkm_s, dqs_s, dkl_s, dv_s, del_s):
        i = pl.program_id(0)

        @pl.when(i == 0)
        def _():
            dst_ref[...] = jnp.zeros_like(dst_ref)
            dlb_ref[...] = jnp.zeros_like(dlb_ref)
            dng_ref[...] = jnp.zeros_like(dng_ref)

        lb = _lower_bound(l_ref[...])
        causal, anti = _tri(True), _tri(False)
        row = lax.broadcasted_iota(jnp.int32, (CHUNK, D_MODEL), 0)
        for c in reversed(range(cpb)):
            rows = slice(c * CHUNK, (c + 1) * CHUNK)
            hq = hq_ref[rows, :]
            t = _hgrn_chunk_inputs(hq, hf_ref[rows, :], lb, causal)
            sgg = _sigmoid(hg_ref[rows, :])
            dyv = dy_ref[rows, :]
            for h in range(HGRN_HEADS):
                ls = slice(h * HGRN_K, (h + 1) * HGRN_K)
                o = o_ref[rows, ls]
                r = lax.rsqrt(jnp.mean(o * o, axis=-1, keepdims=True) + EPS)
                nrm = o * r
                g_h = sgg[:, ls]
                dph_ref[rows, 3 * D_MODEL + h * HGRN_K:3 * D_MODEL + (h + 1) * HGRN_K] = (
                    dyv[:, ls] * nrm * ng_ref[:, ls] * g_h * (1.0 - g_h)).astype(dph_ref.dtype)
                dyg = dyv[:, ls] * g_h
                dng_ref[:, ls] += jnp.sum(dyg * nrm, axis=0, keepdims=True)
                dn = dyg * ng_ref[:, ls]
                do_s[:, ls] = r * (dn - nrm * jnp.mean(dn * nrm, axis=-1, keepdims=True))
            qm, km, qs, kl = (t[n].astype(MXU_DTYPE) for n in ("qm", "km", "qs", "kl"))
            v = hi_ref[rows, :].astype(MXU_DTYPE)
            do = do_s[...].astype(MXU_DTYPE)
            for h in range(HGRN_HEADS):
                ls = slice(h * HGRN_K, (h + 1) * HGRN_K)
                st = st_ref[c, ls, :]
                dst = dst_ref[h]
                a = jnp.where(causal, _nt(qm[:, ls], km[:, ls]), 0.0)
                da = jnp.where(causal, _nt(do[:, ls], v[:, ls]), 0.0)
                dv_s[:, ls] = _tn(a, do[:, ls]) + _nt(kl[:, ls], dst)
                dkl_s[:, ls] = _nn(v[:, ls], dst)
                dqs_s[:, ls] = _nn(do[:, ls], st)
                del_s[:, ls] = jnp.sum(dst * st, axis=0, keepdims=True)
                dst_ref[h] = _tn(do[:, ls], qs[:, ls]) + t["el"][:, ls] * dst
                dqm_s[:, ls] = _nn(da, km[:, ls])
                dkm_s[:, ls] = _tn(da, qm[:, ls])
            dqm, dkm, dqs, dkl = dqm_s[...], dkm_s[...], dqs_s[...], dkl_s[...]
            dq = dqm * t["e_qm"] + dqs * t["e_qs"]
            dk = dkm * t["e_km"] + dkl * t["e_kl"]
            t_qm, t_km, t_kl = dqm * t["qm"], dkm * t["km"], dkl * t["kl"]
            db = t_qm - t_km + dqs * t["qs"] - t_kl
            db_mid = jnp.sum(t_km - t_qm, axis=0, keepdims=True)
            db_last = jnp.sum(t_kl, axis=0, keepdims=True) + del_s[...] * t["el"]
            db = db + jnp.where(row == CHUNK // 2 - 1, db_mid, 0.0) + jnp.where(row == CHUNK - 1, db_last, 0.0)
            dlogf = _chunk_sum(anti, db)
            sq, sg, sgn, f = t["sq"], t["sg"], t["sgn"], t["f"]
            dph_ref[rows, 0:D_MODEL] = (dq * (sq * (1.0 + hq * (1.0 - sq)))).astype(dph_ref.dtype)
            dph_ref[rows, D_MODEL:2 * D_MODEL] = (
                dlogf * (1.0 - lb) * sg * (1.0 - sg) / f - dk * (1.0 - lb) * sgn * (1.0 - sgn)).astype(dph_ref.dtype)
            dph_ref[rows, 2 * D_MODEL:3 * D_MODEL] = dv_s[...].astype(dph_ref.dtype)
            dlb_ref[...] += jnp.sum(dlogf * (1.0 - sg) / f - dk * sgn, axis=0, keepdims=True)

        @pl.when(i == nblk - 1)
        def _():
            dl0 = dlb_ref[...] * lb * (1.0 - lb)
            dl_ref[0:1, :] = dl0
            dl_ref[1:2, :] = -dl0

    wide = pltpu.VMEM((CHUNK, D_MODEL), F32)
    return _call(
        "hgrn_bwd", body, grid=(nblk,),
        in_specs=[col(0), col(1), col(2), col(3), tok,
                  pl.BlockSpec((cpb, D_MODEL, HGRN_K), lambda i: (rev(i), 0, 0)), tok,
                  pl.BlockSpec((2, D_MODEL), lambda i: (0, 0)), pl.BlockSpec((1, D_MODEL), lambda i: (0, 0))],
        out_specs=[pl.BlockSpec((HGRN_TOKENS, 4 * D_MODEL), lambda i: (rev(i), 0)),
                   pl.BlockSpec((1, D_MODEL), lambda i: (0, 0)), pl.BlockSpec((2, D_MODEL), lambda i: (0, 0))],
        out_shape=[jax.ShapeDtypeStruct((T, 4 * D_MODEL), MXU_DTYPE), jax.ShapeDtypeStruct((1, D_MODEL), F32),
                   jax.ShapeDtypeStruct((2, D_MODEL), F32)],
        scratch=[pltpu.VMEM((HGRN_HEADS, HGRN_K, HGRN_K), F32), pltpu.VMEM((1, D_MODEL), F32),
                 wide, wide, wide, wide, wide, wide, pltpu.VMEM((1, D_MODEL), F32)],
        args=[ph, ph, ph, ph, o_raw, states, dy, lb_logits, norm_g], semantics=("arbitrary",), comm=comm)


def _local_step(x, target, vec, net):
    T, D = x.shape
    norm_mix_g, b_in, sinks, lb_logits = vec["norm_mix_g"], vec["b_in"], vec["attn_sinks"], vec["hgrn_lb_logits"]
    hgrn_norm_g, norm_ffn_g, norm_final_g = vec["hgrn_norm_g"], vec["norm_ffn_g"], vec["norm_final_g"]
    w_in = net.full("w_in")
    o_q, o_kv, o_h, o_g = (sum(IN_SPLITS[:i]) for i in range(4))
    w_q, w_kv, w_h, w_g = (w_in[o:o + n] for o, n in zip((o_q, o_kv, o_h, o_g), IN_SPLITS))
    b_q, b_kv, b_h, b_g = (b_in[:, o:o + n] for o, n in zip((o_q, o_kv, o_h, o_g), IN_SPLITS))
    bias = lambda acc, b: (acc + b,)
    both = lambda acc: (acc, acc)
    grad_outs = [("mn", F32), ("mn", MXU_DTYPE)]
    row_vec = lambda n: ((1, n), lambda i, j: (0, j))
    tile = lambda tm, tn: ((tm, tn), lambda i, j: (i, j))
    TM = 512
    BIG = min(T, 1024)

    u = _rmsnorm_fwd("norm_mix", x, norm_mix_g)
    pq, = _matmul("in_q", [(u, w_q)], "nt", tm=TM, tn=1024, tk=1024, outs=[("mn", MXU_DTYPE)],
                  extras=[(b_q, *row_vec(1024))], epilogue=bias)
    pkv, = _matmul("in_kv", [(u, w_kv)], "nt", tm=TM, tn=256, tk=1024, outs=[("mn", MXU_DTYPE)],
                   extras=[(b_kv, *row_vec(256))], epilogue=bias)
    names = ("w_branch_attn", "w_branch_hgrn", "w_out")
    (ph,), got = _matmul("in_h", [(u, w_h)], "nt", tm=TM, tn=1024, tk=1024, outs=[("mn", F32)],
                         extras=[(b_h, *row_vec(1024))], epilogue=bias, comm=net.gather(names))
    net.gathered(names, got)
    pg, = _matmul("in_g", [(u, w_g)], "nt", tm=TM, tn=1024, tk=1024, outs=[("mn", F32)],
                  extras=[(b_g, *row_vec(1024))], epilogue=bias)
    names = ("w_ffn_gate",)
    (y_attn, lse), got = _attn_fwd(pq, pkv, sinks, comm=net.gather(names))
    net.gathered(names, got)
    names = ("w_ffn_up",)
    (y_hgrn, o_raw, states), got = _hgrn_fwd(ph, lb_logits, hgrn_norm_g, comm=net.gather(names))
    net.gathered(names, got)
    w_ba, w_bh, w_out = net.full("w_branch_attn"), net.full("w_branch_hgrn"), net.full("w_out")
    w_gate, w_up = net.full("w_ffn_gate"), net.full("w_ffn_up")
    ya, = _matmul("branch_a", [(y_attn, w_ba)], "nn", tm=TM, tn=1024, tk=1024, outs=[("mn", F32)])
    gate_a = (pg, (TM, 1024), lambda i, j: (i, 0))
    gate_b = (pg, (TM, 1024), lambda i, j: (i, 1))

    def merge(acc, ya_v, ga, gb):
        return acc, _sigmoid(ga) * ya_v + _sigmoid(gb) * acc

    yb, merged = _matmul("branch_b", [(y_hgrn, w_bh)], "nn", tm=TM, tn=1024, tk=1024,
                         outs=[("mn", F32), ("mn", MXU_DTYPE)],
                         extras=[(ya, *tile(TM, 1024)), gate_a, gate_b], epilogue=merge)

    def resid_norm(acc, xin, g):
        h = xin + acc
        r = lax.rsqrt(jnp.mean(h * h, axis=-1, keepdims=True) + EPS)
        return h, h * r * g

    h1, u2 = _matmul("out_proj", [(merged, w_out)], "nn", tm=TM, tn=1024, tk=1024,
                     outs=[("mn", F32), ("mn", MXU_DTYPE)],
                     extras=[(x, *tile(TM, 1024)), (norm_ffn_g, *row_vec(1024))], epilogue=resid_norm)
    FT = FFN // 2
    names = ("w_ffn_down",)
    (gpre, up, z), got = _swiglu_fwd(u2, w_gate, w_up, tm=TM, tn=FT, comm=net.gather(names))
    net.gathered(names, got)
    w_down = net.full("w_ffn_down")

    def loss_head(acc, h1_v, tgt, g):
        h2 = h1_v + acc
        r = lax.rsqrt(jnp.mean(h2 * h2, axis=-1, keepdims=True) + EPS)
        xhat = h2 * r
        err = xhat * g - tgt
        part = 0.5 * jnp.sum(jnp.sum(err * err, axis=-1, keepdims=True), axis=0, keepdims=True) / D
        dyv = err / D
        dxh = dyv * g
        dh2 = r * (dxh - xhat * jnp.mean(dxh * xhat, axis=-1, keepdims=True))
        return dh2, dh2, jnp.sum(dyv * xhat, axis=0, keepdims=True), jnp.broadcast_to(part, (1, D))

    dh2, dh2b, dgf_p, loss_p = _matmul(
        "ffn_down", [(z, w_down)], "nn", tm=TM, tn=1024, tk=FFN,
        outs=[("mn", F32), ("mn", MXU_DTYPE), ("pn", F32), ("pn", F32)],
        extras=[(h1, *tile(TM, 1024)), (target, *tile(TM, 1024)), (norm_final_g, *row_vec(1024))],
        epilogue=loss_head)
    loss = jnp.sum(loss_p.reshape(-1, 8, D)[:, 0, 0])
    d_norm_final = _colsum_partials(dgf_p)

    def swiglu_bwd(acc, gv, upv):
        s = _sigmoid(gv)
        return acc * upv * (s * (1.0 + gv * (1.0 - s))), acc * (gv * s)

    dgp, dup = _matmul("d_ffn_hidden", [(dh2b, w_down)], "nt", tm=TM, tn=FT, tk=1024,
                       outs=[("mn", MXU_DTYPE), ("mn", MXU_DTYPE)],
                       extras=[(gpre, *tile(TM, FT)), (up, *tile(TM, FT))], epilogue=swiglu_bwd)
    d_w_down = _matmul("dw_down", [(z, dh2b)], "tn", tm=FT, tn=1024, tk=1024, outs=grad_outs, epilogue=both)

    def norm_ffn_bwd(acc, h1_v, g, dres):
        dx, dg = _rmsnorm_bwd_vals(acc, h1_v, g)
        dh = dres + dx
        return dh, dh, dg

    names = ("w_ffn_down",)
    (dh1, dh1b, dg2_p), got = _matmul(
        "d_ffn_in", [(dgp, w_gate), (dup, w_up)], "nn", tm=BIG, tn=1024, tk=FT,
        outs=[("mn", F32), ("mn", MXU_DTYPE), ("pn", F32)],
        extras=[(h1, *tile(BIG, 1024)), (norm_ffn_g, *row_vec(1024)), (dh2, *tile(BIG, 1024))],
        epilogue=norm_ffn_bwd,
        comm=net.exchange(dict(w_ffn_down=[d_w_down])))
    net.received(names, (), got)
    d_norm_ffn = _colsum_partials(dg2_p)
    d_w_gate = _matmul("dw_gate", [(dgp, u2)], "tn", tm=FT, tn=1024, tk=1024, outs=grad_outs, epilogue=both)
    d_w_up = _matmul("dw_up", [(dup, u2)], "tn", tm=FT, tn=1024, tk=1024, outs=grad_outs, epilogue=both)

    def merge_bwd(acc, ya_v, yb_v, ga, gb):
        sa, sb = _sigmoid(ga), _sigmoid(gb)
        return acc * sa, acc * sb, acc * ya_v * sa * (1.0 - sa), acc * yb_v * sb * (1.0 - sb)

    dya, dyb, dga, dgb = _matmul(
        "d_merged", [(dh1b, w_out)], "nt", tm=TM, tn=1024, tk=1024, outs=[("mn", MXU_DTYPE)] * 4,
        extras=[(ya, *tile(TM, 1024)), (yb, *tile(TM, 1024)), gate_a, gate_b], epilogue=merge_bwd)
    tn_grad = functools.partial(_matmul, mode="tn", tn=1024, tk=1024, outs=grad_outs, epilogue=both)
    d_w_out = tn_grad("dw_out", [(merged, dh1b)], tm=1024)
    dy_attn, = _matmul("d_y_attn", [(dya, w_ba)], "nt", tm=TM, tn=1024, tk=1024, outs=[("mn", MXU_DTYPE)])
    dy_hgrn, = _matmul("d_y_hgrn", [(dyb, w_bh)], "nt", tm=TM, tn=1024, tk=1024, outs=[("mn", F32)])
    d_w_ba = tn_grad("dw_branch_a", [(y_attn, dya)], tm=1024)
    d_w_bh = tn_grad("dw_branch_b", [(y_hgrn, dyb)], tm=1024)

    names, swap = ("w_ffn_gate",), ("w_ffn_down",)
    (dq, dkv, dsink), got = _attn_bwd(pq, pkv, sinks, lse, dy_attn,
                                      comm=net.exchange(dict(w_ffn_gate=[d_w_gate]), swap))
    net.received(names, swap, got)
    names, swap = ("w_ffn_up", "w_out", "w_branch_attn", "w_branch_hgrn"), ("w_ffn_gate",)
    (dph, d_hgrn_norm, d_lb_logits), got = _hgrn_bwd(
        ph, o_raw, states, dy_hgrn, lb_logits, hgrn_norm_g,
        comm=net.exchange(dict(w_ffn_up=[d_w_up], w_out=[d_w_out], w_branch_attn=[d_w_ba],
                               w_branch_hgrn=[d_w_bh]), swap))
    net.received(names, swap, got)

    d_w_in = [tn_grad("dw_in_q", [(dq, u)], tm=1024, a_colsum=True),
              tn_grad("dw_in_kv", [(dkv, u)], tm=256, a_colsum=True),
              tn_grad("dw_in_h", [(dph, u)], tm=1024),
              tn_grad("dw_in_ga", [(dga, u)], tm=1024, a_colsum=True),
              tn_grad("dw_in_gb", [(dgb, u)], tm=1024, a_colsum=True)]

    def norm_mix_bwd(acc, xin, g, dres):
        dx, dg = _rmsnorm_bwd_vals(acc, xin, g)
        return dres + dx, dg

    db_h, got = _colsum("db_h", dph, comm=net.presum_begin("w_in", d_w_in))
    halves = net.presum_end("w_in", got)
    names, swap = ("w_in",), ("w_ffn_up", "w_out", "w_branch_attn", "w_branch_hgrn")
    (dx, dg1_p), got = _matmul(
        "d_u", [(dq, w_q), (dkv, w_kv), (dph, w_h), (dga, w_g[:D]), (dgb, w_g[D:])], "nn",
        tm=BIG, tn=1024, tk=512, outs=[("mn", F32), ("pn", F32)],
        extras=[(x, *tile(BIG, 1024)), (norm_mix_g, *row_vec(1024)), (dh1, *tile(BIG, 1024))],
        epilogue=norm_mix_bwd,
        comm=_join(halves, net.swap(swap)))
    net.last = (names, swap, got)
    d_norm_mix = _colsum_partials(dg1_p)
    d_b_in = jnp.concatenate([d_w_in[0][2], d_w_in[1][2], db_h, d_w_in[3][2], d_w_in[4][2]], axis=1)
    vecs = dict(norm_mix_g=d_norm_mix, b_in=d_b_in, attn_sinks=jnp.sum(dsink.reshape(Q_HEADS, ATTN_BLOCK), axis=1).reshape(1, Q_HEADS),
                hgrn_lb_logits=d_lb_logits,
                hgrn_norm_g=d_hgrn_norm, norm_ffn_g=d_norm_ffn, norm_final_g=d_norm_final)
    return loss, dx, vecs


def _place():
    return lax.axis_index("x"), lax.axis_index("y"), lax.axis_index("c")


def _other_chips(x, y):
    return [(1 - x, y), (x, 1 - y), (1 - x, 1 - y)]


def _gather_copies(shards):
    n = len(shards)

    def build(ins, outs, send_sems, recv_sems, local_sems):
        x, y, c = _place()
        mine = 2 * x + y
        local = [pltpu.make_async_copy(ins[w], outs[w].at[mine], local_sems.at[w]) for w in range(n)]
        sends, recvs = [], []
        for w in range(n):
            for k, (px, py) in enumerate(_other_chips(x, y)):
                sem = 3 * w + k
                sends.append(pltpu.make_async_remote_copy(
                    src_ref=ins[w], dst_ref=outs[w].at[mine], send_sem=send_sems.at[sem], recv_sem=recv_sems.at[sem],
                    device_id=(px, py, c), device_id_type=MESH_ID))
                recvs.append(pltpu.make_async_remote_copy(
                    src_ref=ins[w], dst_ref=outs[w].at[2 * px + py], send_sem=send_sems.at[sem],
                    recv_sem=recv_sems.at[sem], device_id=(px, py, c), device_id_type=MESH_ID))
        return sends, recvs, local

    return _Carried(shards, [jax.ShapeDtypeStruct((N_CHIPS,) + s.shape, s.dtype) for s in shards], 3 * n, n, build)


def _grad_copies(stacked):
    n = len(stacked)

    def build(ins, outs, send_sems, recv_sems, local_sems):
        x, y, c = _place()
        sends = []
        for w in range(n):
            for k, (px, py) in enumerate(_other_chips(x, y)):
                sem = 3 * w + k
                sends.append(pltpu.make_async_remote_copy(
                    src_ref=ins[w].at[2 * px + py], dst_ref=outs[w].at[k], send_sem=send_sems.at[sem],
                    recv_sem=recv_sems.at[sem], device_id=(px, py, c), device_id_type=MESH_ID))
        return sends, sends, []

    return _Carried(stacked, [jax.ShapeDtypeStruct((3,) + s.shape[1:], s.dtype) for s in stacked], 3 * n, 0, build)


def _small_copies(small):
    def build(ins, outs, send_sems, recv_sems, local_sems):
        small_ref, all_ref = ins[0], outs[0]
        x, y, c = _place()
        me = 4 * x + 2 * y + c
        sends, recvs = [], []
        for r in range(1, 8):
            px = 1 - x if r & 4 else x
            py = 1 - y if r & 2 else y
            pc = 1 - c if r & 1 else c
            sends.append(pltpu.make_async_remote_copy(
                src_ref=small_ref, dst_ref=all_ref.at[me], send_sem=send_sems.at[r - 1], recv_sem=recv_sems.at[r - 1],
                device_id=(px, py, pc), device_id_type=MESH_ID))
            recvs.append(pltpu.make_async_remote_copy(
                src_ref=small_ref, dst_ref=all_ref.at[4 * px + 2 * py + pc], send_sem=send_sems.at[r - 1],
                recv_sem=recv_sems.at[r - 1], device_id=(px, py, pc), device_id_type=MESH_ID))
        return sends, recvs, [pltpu.make_async_copy(small_ref, all_ref.at[me], local_sems.at[0])]

    return _Carried([small], [jax.ShapeDtypeStruct((8,) + small.shape, small.dtype)], 7, 1, build)


def _gather_two_level(name, shards):
    n = len(shards)

    def body(*refs):
        ins, outs = refs[:n], refs[n:2 * n]
        send1, recv1, send2, recv2, local_sems = refs[2 * n:]
        x, y, c = _place()
        mine = 2 * x + y
        local = [pltpu.make_async_copy(ins[w], outs[w].at[mine], local_sems.at[w]) for w in range(n)]
        first, landed, relay, relayed = [], [], [], []
        for w in range(n):
            half = shards[w].shape[0] // 2
            my_half, other_half = pl.ds(c * half, half), pl.ds((1 - c) * half, half)
            for k, (px, py) in enumerate(_other_chips(x, y)):
                sem, chip = 3 * w + k, 2 * px + py
                first.append(pltpu.make_async_remote_copy(
                    src_ref=ins[w].at[my_half], dst_ref=outs[w].at[mine, my_half], send_sem=send1.at[sem],
                    recv_sem=recv1.at[sem], device_id=(px, py, c), device_id_type=MESH_ID))
                landed.append(pltpu.make_async_remote_copy(
                    src_ref=ins[w].at[my_half], dst_ref=outs[w].at[chip, my_half], send_sem=send1.at[sem],
                    recv_sem=recv1.at[sem], device_id=(px, py, c), device_id_type=MESH_ID))
                relay.append(pltpu.make_async_remote_copy(
                    src_ref=outs[w].at[chip, my_half], dst_ref=outs[w].at[chip, my_half], send_sem=send2.at[sem],
                    recv_sem=recv2.at[sem], device_id=(x, y, 1 - c), device_id_type=MESH_ID))
                relayed.append(pltpu.make_async_remote_copy(
                    src_ref=outs[w].at[chip, other_half], dst_ref=outs[w].at[chip, other_half], send_sem=send2.at[sem],
                    recv_sem=recv2.at[sem], device_id=(x, y, 1 - c), device_id_type=MESH_ID))
        for cp in local + first:
            cp.start()
        for arrived, onward in zip(landed, relay):
            arrived.wait_recv()
            onward.start()
        for cp in relayed:
            cp.wait_recv()
        for cp in first + relay:
            cp.wait_send()
        for cp in local:
            cp.wait()

    sems = pltpu.SemaphoreType.DMA((3 * n,))
    return pl.pallas_call(
        body, name=name, in_specs=[HBM_SPEC] * n, out_specs=[HBM_SPEC] * n,
        out_shape=[jax.ShapeDtypeStruct((N_CHIPS,) + s.shape, s.dtype) for s in shards],
        scratch_shapes=[sems, sems, sems, sems, pltpu.SemaphoreType.DMA((n,))],
    )(*shards)


def _copies_alone(name, comm):
    return _call(name, lambda: None, grid=(), in_specs=[], out_specs=[], out_shape=[], args=[], comm=comm)[1]


class _Net:
    def __init__(self, shards):
        self.shards = shards
        self.whole, self.own, self.theirs, self.sums, self.other = {}, {}, {}, {}, {}
        x, y, _ = _place()
        self.chip = 2 * x + y

    def gather(self, names):
        return _gather_copies([self.shards[n] for n in names])

    def gathered(self, names, got):
        for n, g in zip(names, got):
            self.whole[n] = g.reshape(-1, g.shape[-1])

    def full(self, name):
        return self.whole[name]

    def exchange(self, grads, swap=()):
        stacked = []
        for n, pieces in grads.items():
            keep = jnp.concatenate([p[0] for p in pieces], axis=0) if len(pieces) > 1 else pieces[0][0]
            send = jnp.concatenate([p[1] for p in pieces], axis=0) if len(pieces) > 1 else pieces[0][1]
            rows = keep.shape[0] // N_CHIPS
            self.own[n] = lax.dynamic_slice_in_dim(keep, self.chip * rows, rows, axis=0)
            stacked.append(send.reshape(N_CHIPS, rows, send.shape[-1]))
        return _join(_grad_copies(stacked), self.swap(swap))

    def swap(self, names):
        return _sibling_copies([self.sums[n] for n in names]) if names else None

    def presum_begin(self, name, pieces):
        keep = jnp.concatenate([p[0] for p in pieces], axis=0)
        send = jnp.concatenate([p[1] for p in pieces], axis=0)
        rows = keep.shape[0] // N_CHIPS
        self.held = keep.reshape(N_CHIPS, rows, keep.shape[-1])
        return _half_rows_copies(send.reshape(N_CHIPS, rows, send.shape[-1]))

    def presum_end(self, name, got):
        x, y, c = _place()
        to_send, self.own[name] = _pre_sum("presum_" + name, self.held, got[0], jnp.stack([c, self.chip]))
        return _grad_copies([to_send])

    def received(self, names, swap, got, carried=None):
        self.theirs.update(zip(names, got[:len(names)]))
        self.other.update(zip(swap, got[len(names):]))
        for n in names:
            (total,), more = _partial_sum("sum_" + n, self.own[n], self.theirs[n], comm=carried)
            rows = self.shards[n].shape[0]
            if total.shape[0] != rows:
                _, _, c = _place()
                total = lax.dynamic_update_slice(jnp.zeros((rows, total.shape[1]), F32), total, (c * (rows // 2), 0))
            self.sums[n] = total
        return more


def _half_rows_copies(stacked):
    n, rows = stacked.shape[0], stacked.shape[1] // 2

    def build(ins, outs, send_sems, recv_sems, local_sems):
        x, y, c = _place()
        copies = [pltpu.make_async_remote_copy(
            src_ref=ins[0].at[s, pl.ds((1 - c) * rows, rows)], dst_ref=outs[0].at[s], send_sem=send_sems.at[s],
            recv_sem=recv_sems.at[s], device_id=(x, y, 1 - c), device_id_type=MESH_ID) for s in range(n)]
        return copies, copies, []

    return _Carried([stacked], [jax.ShapeDtypeStruct((n, rows, stacked.shape[2]), stacked.dtype)], n, 0, build)


def _pre_sum(name, held, theirs, core_and_chip):
    n, R, C = held.shape
    half = R // 2
    tr = _row_tile(half)
    per_half = half // tr

    def body(place_ref, h_ref, t_ref, send_ref, own_ref):
        total = h_ref[0] + t_ref[0].astype(F32)
        send_ref[0] = total.astype(send_ref.dtype)

        @pl.when(pl.program_id(1) == place_ref[1])
        def _():
            own_ref[...] = total

    return pl.pallas_call(
        body, name=name,
        grid_spec=pltpu.PrefetchScalarGridSpec(
            num_scalar_prefetch=1, grid=(per_half, n),
            in_specs=[pl.BlockSpec((1, tr, C), lambda i, s, place: (s, place[0] * per_half + i, 0)),
                      pl.BlockSpec((1, tr, C), lambda i, s, place: (s, i, 0))],
            out_specs=[pl.BlockSpec((1, tr, C), lambda i, s, place: (s, i, 0)),
                       pl.BlockSpec((tr, C), lambda i, s, place: (i, 0))]),
        out_shape=[jax.ShapeDtypeStruct((n, half, C), MXU_DTYPE), jax.ShapeDtypeStruct((half, C), F32)],
        compiler_params=_params(("arbitrary", "arbitrary")),
    )(core_and_chip, held, theirs)


def _sibling_copies(parts):
    n = len(parts)

    def build(ins, outs, send_sems, recv_sems, local_sems):
        x, y, c = _place()
        copies = [pltpu.make_async_remote_copy(
            src_ref=ins[w], dst_ref=outs[w], send_sem=send_sems.at[w], recv_sem=recv_sems.at[w],
            device_id=(x, y, 1 - c), device_id_type=MESH_ID) for w in range(n)]
        return copies, copies, []

    return _Carried(parts, [jax.ShapeDtypeStruct(p.shape, p.dtype) for p in parts], n, 0, build)


def _row_tile(rows, most=512, sublanes=16):
    return max(t for t in range(sublanes, most + 1, sublanes) if rows % t == 0)


def _partial_sum(name, own, recv, comm=None):
    R, C = own.shape
    tr = _row_tile(R, most=128)

    def body(o_ref, r_ref, p_ref):
        p_ref[...] = ((o_ref[...] + r_ref[0].astype(F32)) + r_ref[1].astype(F32)) + r_ref[2].astype(F32)

    return _call(name, body, grid=(R // tr,),
                 in_specs=[pl.BlockSpec((tr, C), lambda i: (i, 0)), pl.BlockSpec((3, tr, C), lambda i: (0, i, 0))],
                 out_specs=[pl.BlockSpec((tr, C), lambda i: (i, 0))], out_shape=[jax.ShapeDtypeStruct((R, C), F32)],
                 args=[own, recv], semantics=("parallel",), comm=comm)


def _adam_vals(w, g, m, v):
    m = ADAM_B1 * m + (1.0 - ADAM_B1) * g
    v = ADAM_B2 * v + (1.0 - ADAM_B2) * (g * g)
    m_hat = m / (1.0 - ADAM_B1 ** ADAM_STEP)
    v_hat = v / (1.0 - ADAM_B2 ** ADAM_STEP)
    delta = -ADAM_LR * (m_hat / (jnp.sqrt(v_hat) + ADAM_EPS) + ADAM_WD * w)
    return delta, m, v


def _adamw(name, w, m, v, mine, other, comm=None):
    R, C = w.shape
    tr = _row_tile(R)

    def body(w_ref, m_ref, v_ref, s_ref, n_ref, g_ref, d_ref, nm_ref, nv_ref):
        g = s_ref[...] + n_ref[...]
        d, nm, nv = _adam_vals(w_ref[...], g, m_ref[...], v_ref[...])
        g_ref[...], d_ref[...], nm_ref[...], nv_ref[...] = g, d, nm, nv

    spec = pl.BlockSpec((tr, C), lambda i: (i, 0))
    return _call(name, body, grid=(R // tr,), in_specs=[spec] * 5, out_specs=[spec] * 4,
                 out_shape=[jax.ShapeDtypeStruct((R, C), F32)] * 4, args=[w, m, v, mine, other],
                 semantics=("parallel",), comm=comm)


def _adamw_small(w, m, v, g_all):
    def body(w_ref, m_ref, v_ref, a_ref, g_ref, d_ref, nm_ref, nv_ref):
        g = a_ref[0]
        for dev in range(1, 8):
            g = g + a_ref[dev]
        d, nm, nv = _adam_vals(w_ref[...], g, m_ref[...], v_ref[...])
        g_ref[...], d_ref[...], nm_ref[...], nv_ref[...] = g, d, nm, nv

    return pl.pallas_call(
        body, name="adamw_small", out_shape=[jax.ShapeDtypeStruct(w.shape, F32)] * 4,
    )(w, m, v, g_all)


SMALL_LAYOUT = dict(norm_mix_g=(0, 1024), b_in=(1024, 7424), hgrn_norm_g=(8448, 1024), norm_ffn_g=(9472, 1024),
                    norm_final_g=(10496, 1024), hgrn_lb_logits=(11520, 2048), attn_sinks=(13568, 16))
SMALL_LOSS, SMALL_SIZE = 13584, 16 * D_MODEL


def _pack_small(vals, loss=None):
    names = list(SMALL_LAYOUT)
    tail = jnp.zeros((SMALL_SIZE - SMALL_LAYOUT[names[-1]][0],), F32)
    tail = lax.dynamic_update_slice(tail, vals[names[-1]].astype(F32).reshape(-1), (0,))
    if loss is not None:
        tail = lax.dynamic_update_slice(tail, loss.astype(F32).reshape(1), (SMALL_LOSS - SMALL_LAYOUT[names[-1]][0],))
    flat = jnp.concatenate([vals[n].astype(F32).reshape(-1) for n in names[:-1]] + [tail])
    return flat.reshape(-1, D_MODEL)


def _unpack_small(packed, shapes):
    flat = packed.reshape(-1)
    return {name: flat[lo:lo + n].reshape(shapes[name]) for name, (lo, n) in SMALL_LAYOUT.items()}


MATRICES = ("w_in", "w_branch_attn", "w_branch_hgrn", "w_out", "w_ffn_gate", "w_ffn_up", "w_ffn_down")
COLUMN_SHARDED = ("w_in", "w_ffn_gate", "w_ffn_up")
WEIGHTS = ("norm_mix_g", "w_in", "b_in", "attn_sinks", "hgrn_lb_logits", "hgrn_norm_g", "w_branch_attn",
           "w_branch_hgrn", "w_out", "norm_ffn_g", "w_ffn_gate", "w_ffn_up", "w_ffn_down", "norm_final_g")


def kernel(x, norm_mix_g, w_in, b_in, attn_sinks, hgrn_lb_logits, hgrn_norm_g, w_branch_attn, w_branch_hgrn, w_out, norm_ffn_g, w_ffn_gate, w_ffn_up, w_ffn_down, norm_final_g, loss_target, m_norm_mix_g, m_w_in, m_b_in, m_attn_sinks, m_hgrn_lb_logits, m_hgrn_norm_g, m_w_branch_attn, m_w_branch_hgrn, m_w_out, m_norm_ffn_g, m_w_ffn_gate, m_w_ffn_up, m_w_ffn_down, m_norm_final_g, v_norm_mix_g, v_w_in, v_b_in, v_attn_sinks, v_hgrn_lb_logits, v_hgrn_norm_g, v_w_branch_attn, v_w_branch_hgrn, v_w_out, v_norm_ffn_g, v_w_ffn_gate, v_w_ffn_up, v_w_ffn_down, v_norm_final_g):
    given = dict(locals())
    w = {n: given[n] for n in WEIGHTS}
    m = {n: given["m_" + n] for n in WEIGHTS}
    v = {n: given["v_" + n] for n in WEIGHTS}

    block = lambda a, n: jnp.transpose(a[0]) if n in COLUMN_SHARDED else a[0]
    unblock = lambda a, n: (jnp.transpose(a) if n in COLUMN_SHARDED else a)[None]
    net = _Net({n: block(w[n], n).astype(MXU_DTYPE) for n in MATRICES})
    net.gathered(("w_in",), _gather_two_level("gather_w_in", [net.shards["w_in"]]))
    vec = dict(norm_mix_g=norm_mix_g, b_in=b_in, attn_sinks=attn_sinks, hgrn_lb_logits=hgrn_lb_logits,
               hgrn_norm_g=hgrn_norm_g, norm_ffn_g=norm_ffn_g, norm_final_g=norm_final_g.reshape(1, D_MODEL))
    loss_part, dx, d_vecs = _local_step(x[0], loss_target[0], vec, net)

    small_all, = net.received(*net.last, carried=_small_copies(_pack_small(d_vecs, loss_part)))
    grads, deltas, new_m, new_v = {}, {}, {}, {}
    for n in ("w_ffn_down", "w_ffn_gate", "w_ffn_up", "w_out", "w_branch_attn", "w_branch_hgrn", "w_in"):
        res, got = _adamw("adamw_" + n, block(w[n], n), block(m[n], n), block(v[n], n), net.sums[n], net.other[n],
                          comm=net.swap(("w_in",)) if n == "w_ffn_down" else None)
        if n == "w_ffn_down":
            net.other["w_in"], = got
        grads[n], deltas[n], new_m[n], new_v[n] = (unblock(r, n) for r in res)
    shapes = {n: w[n].shape for n in SMALL_LAYOUT}
    res = _adamw_small(_pack_small(w), _pack_small(m), _pack_small(v), small_all)
    for dst, packed in zip((grads, deltas, new_m, new_v), res):
        dst.update(_unpack_small(packed, shapes))

    loss = res[0].reshape(-1)[SMALL_LOSS]
    return (loss, dx[None], *[grads[n] for n in WEIGHTS], *[deltas[n] for n in WEIGHTS],
            *[new_m[n] for n in WEIGHTS], *[new_v[n] for n in WEIGHTS])
```

```python
import functools
import math

import jax
import jax.numpy as jnp
from jax import lax
from jax.experimental import pallas as pl
from jax.experimental.pallas import tpu as pltpu

F32 = jnp.float32
BF16 = jnp.bfloat16
MXU_DTYPE = jnp.bfloat16
MESH_ID = pl.DeviceIdType.MESH

D_MODEL = 1024
HEAD_DIM = 64
Q_HEADS = 16
KV_HEADS = 2
GROUP = Q_HEADS // KV_HEADS
KV_WIDTH = KV_HEADS * HEAD_DIM
ATTN_BLOCK = 128
HGRN_HEADS = 8
HGRN_K = 128
CHUNK = 64
HGRN_TOKENS = 256
FFN = 2816
IN_SPLITS = (1024, 256, 4096, 2048)
EPS = 1e-6
NEG_INF = -1e30
ADAM_LR, ADAM_B1, ADAM_B2, ADAM_EPS, ADAM_WD, ADAM_STEP = 0.001, 0.9, 0.999, 1e-08, 0.01, 10
N_CHIPS = 4
VMEM_LIMIT = 60 * 1024 * 1024


def _params(sem=None):
    return pltpu.CompilerParams(dimension_semantics=sem, vmem_limit_bytes=VMEM_LIMIT)


def _sigmoid(v):
    return 0.5 * jnp.tanh(0.5 * v) + 0.5


def _dot(a, b, dims):
    return lax.dot_general(a.astype(MXU_DTYPE), b.astype(MXU_DTYPE), (dims, ((), ())),
                           preferred_element_type=F32)


def _nn(a, b):
    return _dot(a, b, ((1,), (0,)))


def _nt(a, b):
    return _dot(a, b, ((1,), (1,)))


def _tn(a, b):
    return _dot(a, b, ((0,), (0,)))


HBM_SPEC = pl.BlockSpec(memory_space=pl.ANY)


class _Carried:
    def __init__(self, arrays, out_shapes, n_remote, n_local, build):
        self.parts = [(len(arrays), len(out_shapes), build)]
        self.arrays, self.out_shapes = list(arrays), list(out_shapes)
        self.scratch = [pltpu.SemaphoreType.DMA((n_remote,)), pltpu.SemaphoreType.DMA((n_remote,)),
                        pltpu.SemaphoreType.DMA((max(n_local, 1),))]

    def __add__(self, other):
        both = _Carried([], [], 1, 0, None)
        both.parts = self.parts + other.parts
        both.arrays, both.out_shapes = self.arrays + other.arrays, self.out_shapes + other.out_shapes
        both.scratch = self.scratch + other.scratch
        return both

    def _built(self, ins, outs, sems):
        for p, (ni, no, build) in enumerate(self.parts):
            yield build(ins[:ni], outs[:no], *sems[3 * p:3 * p + 3])
            ins, outs = ins[ni:], outs[no:]

    def start(self, ins, outs, sems):
        for sends, _, local in self._built(ins, outs, sems):
            for cp in local + sends:
                cp.start()

    def wait(self, ins, outs, sems):
        for sends, recvs, local in self._built(ins, outs, sems):
            for cp in recvs:
                cp.wait_recv()
            for cp in sends:
                cp.wait_send()
            for cp in local:
                cp.wait()


def _join(*comms):
    comms = [c for c in comms if c is not None]
    return functools.reduce(lambda a, b: a + b, comms) if comms else None


def _call(name, body, *, grid, in_specs, out_specs, out_shape, args, scratch=(), semantics=None, comm=None):
    n_in, n_out, n_scr = len(in_specs), len(out_specs), len(scratch)
    if comm is None:
        res = pl.pallas_call(body, name=name, grid=grid, in_specs=in_specs, out_specs=out_specs, out_shape=out_shape,
                             scratch_shapes=list(scratch), compiler_params=_params(semantics))(*args)
        return list(res), []
    ci, co = len(comm.arrays), len(comm.out_shapes)

    def carrying(*refs):
        ins, refs = refs[:n_in], refs[n_in:]
        c_ins, refs = refs[:ci], refs[ci:]
        outs, refs = refs[:n_out], refs[n_out:]
        c_outs, refs = refs[:co], refs[co:]
        scr, sems = refs[:n_scr], refs[n_scr:]
        if not grid:
            comm.start(c_ins, c_outs, sems)
            body(*ins, *outs, *scr)
            comm.wait(c_ins, c_outs, sems)
            return
        first = functools.reduce(jnp.logical_and, [pl.program_id(a) == 0 for a in range(len(grid))])
        last = functools.reduce(jnp.logical_and, [pl.program_id(a) == g - 1 for a, g in enumerate(grid)])

        @pl.when(first)
        def _():
            comm.start(c_ins, c_outs, sems)

        body(*ins, *outs, *scr)

        @pl.when(last)
        def _():
            comm.wait(c_ins, c_outs, sems)

    res = pl.pallas_call(
        carrying, name=name, grid=grid, in_specs=list(in_specs) + [HBM_SPEC] * ci,
        out_specs=list(out_specs) + [HBM_SPEC] * co, out_shape=list(out_shape) + comm.out_shapes,
        scratch_shapes=list(scratch) + comm.scratch,
        compiler_params=_params(("arbitrary",) * len(grid) if grid else None),
    )(*args, *comm.arrays)
    return list(res[:n_out]), list(res[n_out:])


_NO_COPIES = object()


def _matmul(name, pairs, mode, *, tm, tn, tk, outs, extras=(), epilogue=None, a_colsum=False, comm=_NO_COPIES):
    prod = dict(nn=_nn, nt=_nt, tn=_tn)[mode]
    a0, b0 = pairs[0]
    M = a0.shape[1] if mode == "tn" else a0.shape[0]
    N = b0.shape[0] if mode == "nt" else b0.shape[1]
    steps, in_specs, offset = [], [], 0
    for a, b in pairs:
        K = a.shape[0] if mode == "tn" else a.shape[1]
        t = min(tk, K)
        assert K % t == 0, (name, K, t)
        kmap = functools.partial(lambda k, off, n: jnp.clip(k - off, 0, n - 1), off=offset, n=K // t)
        if mode == "tn":
            in_specs.append(pl.BlockSpec((t, tm), functools.partial(lambda i, j, k, f: (f(k), i), f=kmap)))
        else:
            in_specs.append(pl.BlockSpec((tm, t), functools.partial(lambda i, j, k, f: (i, f(k)), f=kmap)))
        if mode == "nt":
            in_specs.append(pl.BlockSpec((tn, t), functools.partial(lambda i, j, k, f: (j, f(k)), f=kmap)))
        else:
            in_specs.append(pl.BlockSpec((t, tn), functools.partial(lambda i, j, k, f: (f(k), j), f=kmap)))
        steps.append((offset, offset + K // t))
        offset += K // t
    assert M % tm == 0 and N % tn == 0, (name, M, N, tm, tn)
    ni, nj, nk = M // tm, N // tn, offset
    npair, ne, no = len(pairs), len(extras), len(outs)
    if epilogue is None:
        epilogue = lambda acc: (acc,)

    def finish(acc, extra_refs, out_refs):
        vals = epilogue(acc, *[r[...] for r in extra_refs])
        for (kind, _), o_ref, val in zip(outs, out_refs, vals):
            if kind == "pn":
                val = jnp.broadcast_to(val, o_ref.shape)
            o_ref[...] = val.astype(o_ref.dtype)

    def body(*refs):
        ab, rest = refs[:2 * npair], refs[2 * npair:]
        extra_refs, out_refs = rest[:ne], rest[ne:ne + no]
        if nk == 1:
            finish(prod(ab[0][...], ab[1][...]), extra_refs, out_refs)
            return
        acc_ref = rest[-1]
        k = pl.program_id(2)

        @pl.when(k == 0)
        def _():
            acc_ref[...] = jnp.zeros_like(acc_ref)
            if a_colsum:
                rest[ne + no][...] = jnp.zeros((1, tm), F32)

        if a_colsum:
            rest[ne + no][...] += jnp.sum(ab[0][...].astype(F32), axis=0, keepdims=True)
        for p, (lo, hi) in enumerate(steps):
            @pl.when(jnp.logical_and(k >= lo, k < hi))
            def _():
                acc_ref[...] += prod(ab[2 * p][...], ab[2 * p + 1][...])

        @pl.when(k == nk - 1)
        def _():
            finish(acc_ref[...], extra_refs, out_refs)

    for _, shape, im in extras:
        in_specs.append(pl.BlockSpec(shape, functools.partial(lambda i, j, k, im: im(i, j), im=im)))
    out_shape, out_specs = [], []
    for kind, dt in outs:
        if kind == "mn":
            out_shape.append(jax.ShapeDtypeStruct((M, N), dt))
            out_specs.append(pl.BlockSpec((tm, tn), lambda i, j, k: (i, j)))
        else:
            out_shape.append(jax.ShapeDtypeStruct((8 * ni, N), dt))
            out_specs.append(pl.BlockSpec((8, tn), lambda i, j, k: (i, j)))
    if a_colsum:
        assert mode == "tn" and npair == 1 and nj == 1 and nk > 1, name
        out_shape.append(jax.ShapeDtypeStruct((1, M), F32))
        out_specs.append(pl.BlockSpec((1, tm), lambda i, j, k: (0, i)))
    grid = (ni, nj, nk)
    if nj > 1 and nk == 1:
        turned = lambda spec: pl.BlockSpec(spec.block_shape, functools.partial(
            lambda j, i, k, im: im(i, j, k), im=spec.index_map))
        in_specs, out_specs, grid = [turned(s) for s in in_specs], [turned(s) for s in out_specs], (nj, ni, nk)
    res, got = _call(name, body, grid=grid, in_specs=in_specs, out_specs=out_specs, out_shape=out_shape,
                     args=[t for pair in pairs for t in pair] + [e[0] for e in extras],
                     scratch=[pltpu.VMEM((tm, tn), F32)] if nk > 1 else [],
                     semantics=("parallel", "parallel", "arbitrary"), comm=None if comm is _NO_COPIES else comm)
    return res if comm is _NO_COPIES else (res, got)


def _swiglu_fwd(u, w_gate_t, w_up_t, *, tm, tn, comm=None):
    (T, D), F = u.shape, w_gate_t.shape[0]

    def body(u_ref, wg_ref, wu_ref, g_ref, up_ref, z_ref):
        g, up = _nt(u_ref[...], wg_ref[...]), _nt(u_ref[...], wu_ref[...])
        g_ref[...], up_ref[...] = g, up
        z_ref[...] = (g * _sigmoid(g) * up).astype(z_ref.dtype)

    w_spec = pl.BlockSpec((tn, D), lambda j, i: (j, 0))
    o_spec = pl.BlockSpec((tm, tn), lambda j, i: (i, j))
    return _call("ffn_hidden", body, grid=(F // tn, T // tm),
                 in_specs=[pl.BlockSpec((tm, D), lambda j, i: (i, 0)), w_spec, w_spec], out_specs=[o_spec] * 3,
                 out_shape=[jax.ShapeDtypeStruct((T, F), F32)] * 2 + [jax.ShapeDtypeStruct((T, F), MXU_DTYPE)],
                 args=[u, w_gate_t, w_up_t], semantics=("parallel", "parallel"), comm=comm)


def _colsum_partials(p):
    return jnp.sum(p.reshape(-1, 8, p.shape[-1])[:, 0, :], axis=0, keepdims=True)


def _rmsnorm_fwd(name, x, g, tr=512):
    T, D = x.shape

    def body(x_ref, g_ref, u_ref):
        xv = x_ref[...]
        r = lax.rsqrt(jnp.mean(xv * xv, axis=-1, keepdims=True) + EPS)
        u_ref[...] = (xv * r * g_ref[...]).astype(u_ref.dtype)

    return pl.pallas_call(
        body, name=name, grid=(T // tr,),
        in_specs=[pl.BlockSpec((tr, D), lambda i: (i, 0)), pl.BlockSpec((1, D), lambda i: (0, 0))],
        out_specs=pl.BlockSpec((tr, D), lambda i: (i, 0)),
        out_shape=jax.ShapeDtypeStruct((T, D), MXU_DTYPE),
        compiler_params=_params(("parallel",)),
    )(x, g)


def _rmsnorm_bwd_vals(dy, xin, g):
    rstd = lax.rsqrt(jnp.mean(xin * xin, axis=-1, keepdims=True) + EPS)
    xhat = xin * rstd
    dg = jnp.sum(dy * xhat, axis=0, keepdims=True)
    dxh = dy * g
    dx = rstd * (dxh - xhat * jnp.mean(dxh * xhat, axis=-1, keepdims=True))
    return dx, dg


def _colsum(name, a, tr=512, comm=_NO_COPIES):
    T, N = a.shape

    def body(a_ref, o_ref):
        @pl.when(pl.program_id(0) == 0)
        def _():
            o_ref[...] = jnp.zeros_like(o_ref)

        o_ref[...] += jnp.sum(a_ref[...].astype(F32), axis=0, keepdims=True)

    (res,), got = _call(name, body, grid=(T // tr,), in_specs=[pl.BlockSpec((tr, N), lambda i: (i, 0))],
                        out_specs=[pl.BlockSpec((1, N), lambda i: (0, 0))],
                        out_shape=[jax.ShapeDtypeStruct((1, N), F32)], args=[a], semantics=("arbitrary",),
                        comm=None if comm is _NO_COPIES else comm)
    return res if comm is _NO_COPIES else (res, got)


ATTN_SCALE = 1.0 / math.sqrt(HEAD_DIM)
GROUP_LANES = GROUP * ATTN_BLOCK
PAIR = 2 * HEAD_DIM


def _attn_mask():
    kj = lax.broadcasted_iota(jnp.int32, (ATTN_BLOCK, GROUP_LANES), 0)
    qi = lax.broadcasted_iota(jnp.int32, (ATTN_BLOCK, GROUP_LANES), 1) & (ATTN_BLOCK - 1)
    return kj <= qi


def _heads_transposed(ref, g, scale=None):
    parts = []
    for a in range(GROUP // 2):
        lo = (g * GROUP // 2 + a) * PAIR
        pair = ref[:, lo:lo + PAIR].astype(F32)
        pair = (pair if scale is None else pair * scale).T
        parts += [pair[:HEAD_DIM], pair[HEAD_DIM:]]
    return jnp.concatenate(parts, axis=1).astype(MXU_DTYPE)


def _heads_back(ref, g, vt):
    for a in range(GROUP // 2):
        lo = (g * GROUP // 2 + a) * PAIR
        pair = jnp.concatenate([vt[:, (2 * a) * ATTN_BLOCK:(2 * a + 1) * ATTN_BLOCK],
                                vt[:, (2 * a + 1) * ATTN_BLOCK:(2 * a + 2) * ATTN_BLOCK]], axis=0)
        ref[:, lo:lo + PAIR] = pair.T.astype(ref.dtype)


def _kv_parts(kv_ref, g):
    ks = slice(g * HEAD_DIM, (g + 1) * HEAD_DIM)
    vs = slice(KV_WIDTH + g * HEAD_DIM, KV_WIDTH + (g + 1) * HEAD_DIM)
    return kv_ref[:, ks].astype(MXU_DTYPE), kv_ref[:, vs].astype(MXU_DTYPE)


def _sink_rows(sinks):
    return jnp.repeat(sinks.reshape(KV_HEADS, GROUP), ATTN_BLOCK, axis=1)


def _attn_fwd(pq, pkv, sinks, comm=None):
    T = pq.shape[0]
    nb = T // ATTN_BLOCK

    def body(q_ref, kvc_ref, kvp_ref, s_ref, y_ref, lse_ref):
        mask_c = _attn_mask()
        has_prev = pl.program_id(0) > 0
        for g in range(KV_HEADS):
            (kc, vc), (kp, vp) = _kv_parts(kvc_ref, g), _kv_parts(kvp_ref, g)
            qt = _heads_transposed(q_ref, g, ATTN_SCALE)
            s = jnp.where(mask_c, _nn(kc, qt), jnp.where(has_prev, _nn(kp, qt), NEG_INF))
            sink = s_ref[g:g + 1, :]
            m = jnp.maximum(jnp.max(s, axis=0, keepdims=True), sink)
            p = jnp.exp(s - m)
            den = jnp.sum(p, axis=0, keepdims=True) + jnp.exp(sink - m)
            pc = jnp.where(mask_c, p, 0.0)
            _heads_back(y_ref, g, (_tn(vc, pc) + _tn(vp, p - pc)) / den)
            lse = m + jnp.log(den)
            for i in range(GROUP):
                lse_ref[g * GROUP + i:g * GROUP + i + 1, :] = lse[:, i * ATTN_BLOCK:(i + 1) * ATTN_BLOCK]

    return _call(
        "attn_fwd", body, grid=(nb,),
        in_specs=[pl.BlockSpec((ATTN_BLOCK, D_MODEL), lambda n: (n, 0)),
                  pl.BlockSpec((ATTN_BLOCK, 2 * KV_WIDTH), lambda n: (n, 0)),
                  pl.BlockSpec((ATTN_BLOCK, 2 * KV_WIDTH), lambda n: (jnp.maximum(n - 1, 0), 0)),
                  pl.BlockSpec((KV_HEADS, GROUP_LANES), lambda n: (0, 0))],
        out_specs=[pl.BlockSpec((ATTN_BLOCK, D_MODEL), lambda n: (n, 0)),
                   pl.BlockSpec((Q_HEADS, ATTN_BLOCK), lambda n: (0, n))],
        out_shape=[jax.ShapeDtypeStruct((T, D_MODEL), MXU_DTYPE), jax.ShapeDtypeStruct((Q_HEADS, T), F32)],
        args=[pq, pkv, pkv, _sink_rows(sinks)], semantics=("parallel",), comm=comm)


def _attn_bwd(pq, pkv, sinks, lse, dy, comm=None):
    T = pq.shape[0]
    nb = T // ATTN_BLOCK
    cur = lambda n: (jnp.minimum(n, nb - 1), 0)

    def body(q_ref, kvc_ref, kvp_ref, s_ref, lse_ref, dy_ref, dq_ref, dkv_ref, ds_ref, carry, top, bot):
        n = pl.program_id(0)

        @pl.when(n == 0)
        def _():
            carry[...] = jnp.zeros_like(carry)
            ds_ref[...] = jnp.zeros_like(ds_ref)

        @pl.when(n < nb)
        def _():
            mask_c = _attn_mask()
            valid = jnp.logical_or(mask_c, n > 0)
            for g in range(KV_HEADS):
                ks = slice(g * HEAD_DIM, (g + 1) * HEAD_DIM)
                vs = slice(KV_WIDTH + g * HEAD_DIM, KV_WIDTH + (g + 1) * HEAD_DIM)
                (kc, vc), (kp, vp) = _kv_parts(kvc_ref, g), _kv_parts(kvp_ref, g)
                qt = _heads_transposed(q_ref, g, ATTN_SCALE)
                dot = _heads_transposed(dy_ref, g)
                lse = jnp.concatenate([lse_ref[g * GROUP + i:g * GROUP + i + 1, :] for i in range(GROUP)], axis=1)
                p = jnp.where(valid, jnp.exp(jnp.where(mask_c, _nn(kc, qt), _nn(kp, qt)) - lse), 0.0)
                dp = jnp.where(mask_c, _nn(vc, dot), _nn(vp, dot))
                delta = jnp.sum(p * dp, axis=0, keepdims=True)
                ds = p * (dp - delta)
                ds_c, p_c = jnp.where(mask_c, ds, 0.0), jnp.where(mask_c, p, 0.0)
                ds_p, p_p = ds - ds_c, p - p_c
                _heads_back(dq_ref, g, (_tn(kc, ds_c) + _tn(kp, ds_p)) * ATTN_SCALE)
                bot[:, ks], bot[:, vs] = _nt(ds_c, qt), _nt(p_c, dot)
                top[:, ks], top[:, vs] = _nt(ds_p, qt), _nt(p_p, dot)
                ds_ref[g:g + 1, :] -= jnp.exp(s_ref[g:g + 1, :] - lse) * delta
            dkv_ref[...] = (carry[...] + top[...]).astype(dkv_ref.dtype)
            carry[...] = bot[...]

        @pl.when(n == nb)
        def _():
            dkv_ref[...] = carry[...].astype(dkv_ref.dtype)

    return _call(
        "attn_bwd", body, grid=(nb + 1,),
        in_specs=[pl.BlockSpec((ATTN_BLOCK, D_MODEL), cur),
                  pl.BlockSpec((ATTN_BLOCK, 2 * KV_WIDTH), cur),
                  pl.BlockSpec((ATTN_BLOCK, 2 * KV_WIDTH), lambda n: (jnp.maximum(jnp.minimum(n, nb - 1) - 1, 0), 0)),
                  pl.BlockSpec((KV_HEADS, GROUP_LANES), lambda n: (0, 0)),
                  pl.BlockSpec((Q_HEADS, ATTN_BLOCK), lambda n: (0, jnp.minimum(n, nb - 1))),
                  pl.BlockSpec((ATTN_BLOCK, D_MODEL), cur)],
        out_specs=[pl.BlockSpec((ATTN_BLOCK, D_MODEL), cur),
                   pl.BlockSpec((ATTN_BLOCK, 2 * KV_WIDTH), lambda n: (jnp.maximum(n - 1, 0), 0)),
                   pl.BlockSpec((KV_HEADS, GROUP_LANES), lambda n: (0, 0))],
        out_shape=[jax.ShapeDtypeStruct((T, D_MODEL), MXU_DTYPE),
                   jax.ShapeDtypeStruct((T, 2 * KV_WIDTH), MXU_DTYPE),
                   jax.ShapeDtypeStruct((KV_HEADS, GROUP_LANES), F32)],
        scratch=[pltpu.VMEM((ATTN_BLOCK, 2 * KV_WIDTH), F32)] * 3,
        args=[pq, pkv, pkv, _sink_rows(sinks), lse, dy], semantics=("arbitrary",), comm=comm)


def _lower_bound(l):
    m = jnp.maximum(l[0:1], l[1:2])
    e0, e1 = jnp.exp(l[0:1] - m), jnp.exp(l[1:2] - m)
    return e0 / (e0 + e1)


def _tri(lower):
    r = lax.broadcasted_iota(jnp.int32, (CHUNK, CHUNK), 0)
    c = lax.broadcasted_iota(jnp.int32, (CHUNK, CHUNK), 1)
    return (r >= c) if lower else (c >= r)


def _chunk_sum(mask, v):
    ones = mask.astype(BF16)
    hi = v.astype(BF16)
    rest = v - hi.astype(F32)
    mid = rest.astype(BF16)
    lo = (rest - mid.astype(F32)).astype(BF16)
    part = lambda t: lax.dot_general(ones, t, (((1,), (0,)), ((), ())), preferred_element_type=F32)
    return part(hi) + part(mid) + part(lo)


def _hgrn_chunk_inputs(hq, hf, lb, causal):
    sg, sgn = _sigmoid(hf), _sigmoid(-hf)
    f = lb + (1.0 - lb) * sg
    kk = (1.0 - lb) * sgn
    sq = _sigmoid(hq)
    q = hq * sq
    b = _chunk_sum(causal, jnp.log(f))
    bm, bl = b[CHUNK // 2 - 1:CHUNK // 2, :], b[CHUNK - 1:CHUNK, :]
    e_qm, e_km, e_qs, e_kl = jnp.exp(b - bm), jnp.exp(bm - b), jnp.exp(b), jnp.exp(bl - b)
    return dict(sg=sg, sgn=sgn, f=f, kk=kk, sq=sq, q=q, e_qm=e_qm, e_km=e_km, e_qs=e_qs, e_kl=e_kl,
                qm=q * e_qm, km=kk * e_km, qs=q * e_qs, kl=kk * e_kl, el=jnp.exp(bl))


def _hgrn_fwd(ph, lb_logits, norm_g, comm=None):
    T = ph.shape[0]
    nblk, cpb = T // HGRN_TOKENS, HGRN_TOKENS // CHUNK
    col = lambda c: pl.BlockSpec((HGRN_TOKENS, D_MODEL), functools.partial(lambda i, c: (i, c), c=c))

    def body(hq_ref, hf_ref, hi_ref, hg_ref, l_ref, ng_ref, y_ref, o_ref, st_ref, s_ref):
        @pl.when(pl.program_id(0) == 0)
        def _():
            s_ref[...] = jnp.zeros_like(s_ref)

        lb = _lower_bound(l_ref[...])
        causal = _tri(True)
        for c in range(cpb):
            rows = slice(c * CHUNK, (c + 1) * CHUNK)
            t = _hgrn_chunk_inputs(hq_ref[rows, :], hf_ref[rows, :], lb, causal)
            qm, km, qs, kl = (t[n].astype(MXU_DTYPE) for n in ("qm", "km", "qs", "kl"))
            v = hi_ref[rows, :].astype(MXU_DTYPE)
            for h in range(HGRN_HEADS):
                ls = slice(h * HGRN_K, (h + 1) * HGRN_K)
                st = s_ref[h]
                st_ref[c, ls, :] = st
                a = jnp.where(causal, _nt(qm[:, ls], km[:, ls]), 0.0)
                o_ref[rows, ls] = _nn(a, v[:, ls]) + _nt(qs[:, ls], st)
                s_ref[h] = t["el"][:, ls] * st + _tn(v[:, ls], kl[:, ls])
        for h in range(HGRN_HEADS):
            ls = slice(h * HGRN_K, (h + 1) * HGRN_K)
            o = o_ref[:, ls]
            r = lax.rsqrt(jnp.mean(o * o, axis=-1, keepdims=True) + EPS)
            y_ref[:, ls] = (o * r * ng_ref[:, ls] * _sigmoid(hg_ref[:, ls])).astype(y_ref.dtype)

    return _call(
        "hgrn_fwd", body, grid=(nblk,),
        in_specs=[col(0), col(1), col(2), col(3),
                  pl.BlockSpec((2, D_MODEL), lambda i: (0, 0)), pl.BlockSpec((1, D_MODEL), lambda i: (0, 0))],
        out_specs=[pl.BlockSpec((HGRN_TOKENS, D_MODEL), lambda i: (i, 0)),
                   pl.BlockSpec((HGRN_TOKENS, D_MODEL), lambda i: (i, 0)),
                   pl.BlockSpec((cpb, D_MODEL, HGRN_K), lambda i: (i, 0, 0))],
        out_shape=[jax.ShapeDtypeStruct((T, D_MODEL), MXU_DTYPE), jax.ShapeDtypeStruct((T, D_MODEL), F32),
                   jax.ShapeDtypeStruct((T // CHUNK, D_MODEL, HGRN_K), F32)],
        scratch=[pltpu.VMEM((HGRN_HEADS, HGRN_K, HGRN_K), F32)],
        args=[ph, ph, ph, ph, lb_logits, norm_g], semantics=("arbitrary",), comm=comm)


def _hgrn_bwd(ph, o_raw, states, dy, lb_logits, norm_g, comm=None):
    T = ph.shape[0]
    nblk, cpb = T // HGRN_TOKENS, HGRN_TOKENS // CHUNK
    rev = lambda i: nblk - 1 - i
    col = lambda c: pl.BlockSpec((HGRN_TOKENS, D_MODEL), functools.partial(lambda i, c: (rev(i), c), c=c))
    tok = pl.BlockSpec((HGRN_TOKENS, D_MODEL), lambda i: (rev(i), 0))

    def body(hq_ref, hf_ref, hi_ref, hg_ref, o_ref, st_ref, dy_ref, l_ref, ng_ref,
             dph_ref, dng_ref, dl_ref, dst_ref, dlb_ref, do_s, dqm_s, dkm_s, dqs_s, dkl_s, dv_s, del_s):
        i = pl.program_id(0)

        @pl.when(i == 0)
        def _():
            dst_ref[...] = jnp.zeros_like(dst_ref)
            dlb_ref[...] = jnp.zeros_like(dlb_ref)
            dng_ref[...] = jnp.zeros_like(dng_ref)

        lb = _lower_bound(l_ref[...])
        causal, anti = _tri(True), _tri(False)
        row = lax.broadcasted_iota(jnp.int32, (CHUNK, D_MODEL), 0)
        for c in reversed(range(cpb)):
            rows = slice(c * CHUNK, (c + 1) * CHUNK)
            hq = hq_ref[rows, :]
            t = _hgrn_chunk_inputs(hq, hf_ref[rows, :], lb, causal)
            sgg = _sigmoid(hg_ref[rows, :])
            dyv = dy_ref[rows, :]
            for h in range(HGRN_HEADS):
                ls = slice(h * HGRN_K, (h + 1) * HGRN_K)
                o = o_ref[rows, ls]
                r = lax.rsqrt(jnp.mean(o * o, axis=-1, keepdims=True) + EPS)
                nrm = o * r
                g_h = sgg[:, ls]
                dph_ref[rows, 3 * D_MODEL + h * HGRN_K:3 * D_MODEL + (h + 1) * HGRN_K] = (
                    dyv[:, ls] * nrm * ng_ref[:, ls] * g_h * (1.0 - g_h)).astype(dph_ref.dtype)
                dyg = dyv[:, ls] * g_h
                dng_ref[:, ls] += jnp.sum(dyg * nrm, axis=0, keepdims=True)
                dn = dyg * ng_ref[:, ls]
                do_s[:, ls] = r * (dn - nrm * jnp.mean(dn * nrm, axis=-1, keepdims=True))
            qm, km, qs, kl = (t[n].astype(MXU_DTYPE) for n in ("qm", "km", "qs", "kl"))
            v = hi_ref[rows, :].astype(MXU_DTYPE)
            do = do_s[...].astype(MXU_DTYPE)
            for h in range(HGRN_HEADS):
                ls = slice(h * HGRN_K, (h + 1) * HGRN_K)
                st = st_ref[c, ls, :]
                dst = dst_ref[h]
                a = jnp.where(causal, _nt(qm[:, ls], km[:, ls]), 0.0)
                da = jnp.where(causal, _nt(do[:, ls], v[:, ls]), 0.0)
                dv_s[:, ls] = _tn(a, do[:, ls]) + _nt(kl[:, ls], dst)
                dkl_s[:, ls] = _nn(v[:, ls], dst)
                dqs_s[:, ls] = _nn(do[:, ls], st)
                del_s[:, ls] = jnp.sum(dst * st, axis=0, keepdims=True)
                dst_ref[h] = _tn(do[:, ls], qs[:, ls]) + t["el"][:, ls] * dst
                dqm_s[:, ls] = _nn(da, km[:, ls])
                dkm_s[:, ls] = _tn(da, qm[:, ls])
            dqm, dkm, dqs, dkl = dqm_s[...], dkm_s[...], dqs_s[...], dkl_s[...]
            dq = dqm * t["e_qm"] + dqs * t["e_qs"]
            dk = dkm * t["e_km"] + dkl * t["e_kl"]
            t_qm, t_km, t_kl = dqm * t["qm"], dkm * t["km"], dkl * t["kl"]
            db = t_qm - t_km + dqs * t["qs"] - t_kl
            db_mid = jnp.sum(t_km - t_qm, axis=0, keepdims=True)
            db_last = jnp.sum(t_kl, axis=0, keepdims=True) + del_s[...] * t["el"]
            db = db + jnp.where(row == CHUNK // 2 - 1, db_mid, 0.0) + jnp.where(row == CHUNK - 1, db_last, 0.0)
            dlogf = _chunk_sum(anti, db)
            sq, sg, sgn, f = t["sq"], t["sg"], t["sgn"], t["f"]
            dph_ref[rows, 0:D_MODEL] = (dq * (sq * (1.0 + hq * (1.0 - sq)))).astype(dph_ref.dtype)
            dph_ref[rows, D_MODEL:2 * D_MODEL] = (
                dlogf * (1.0 - lb) * sg * (1.0 - sg) / f - dk * (1.0 - lb) * sgn * (1.0 - sgn)).astype(dph_ref.dtype)
            dph_ref[rows, 2 * D_MODEL:3 * D_MODEL] = dv_s[...].astype(dph_ref.dtype)
            dlb_ref[...] += jnp.sum(dlogf * (1.0 - sg) / f - dk * sgn, axis=0, keepdims=True)

        @pl.when(i == nblk - 1)
        def _():
            dl0 = dlb_ref[...] * lb * (1.0 - lb)
            dl_ref[0:1, :] = dl0
            dl_ref[1:2, :] = -dl0

    wide = pltpu.VMEM((CHUNK, D_MODEL), F32)
    return _call(
        "hgrn_bwd", body, grid=(nblk,),
        in_specs=[col(0), col(1), col(2), col(3), tok,
                  pl.BlockSpec((cpb, D_MODEL, HGRN_K), lambda i: (rev(i), 0, 0)), tok,
                  pl.BlockSpec((2, D_MODEL), lambda i: (0, 0)), pl.BlockSpec((1, D_MODEL), lambda i: (0, 0))],
        out_specs=[pl.BlockSpec((HGRN_TOKENS, 4 * D_MODEL), lambda i: (rev(i), 0)),
                   pl.BlockSpec((1, D_MODEL), lambda i: (0, 0)), pl.BlockSpec((2, D_MODEL), lambda i: (0, 0))],
        out_shape=[jax.ShapeDtypeStruct((T, 4 * D_MODEL), MXU_DTYPE), jax.ShapeDtypeStruct((1, D_MODEL), F32),
                   jax.ShapeDtypeStruct((2, D_MODEL), F32)],
        scratch=[pltpu.VMEM((HGRN_HEADS, HGRN_K, HGRN_K), F32), pltpu.VMEM((1, D_MODEL), F32),
                 wide, wide, wide, wide, wide, wide, pltpu.VMEM((1, D_MODEL), F32)],
        args=[ph, ph, ph, ph, o_raw, states, dy, lb_logits, norm_g], semantics=("arbitrary",), comm=comm)


def _local_step(x, target, vec, net):
    T, D = x.shape
    norm_mix_g, b_in, sinks, lb_logits = vec["norm_mix_g"], vec["b_in"], vec["attn_sinks"], vec["hgrn_lb_logits"]
    hgrn_norm_g, norm_ffn_g, norm_final_g = vec["hgrn_norm_g"], vec["norm_ffn_g"], vec["norm_final_g"]
    w_in = net.full("w_in")
    o_q, o_kv, o_h, o_g = (sum(IN_SPLITS[:i]) for i in range(4))
    w_q, w_kv, w_h, w_g = (w_in[o:o + n] for o, n in zip((o_q, o_kv, o_h, o_g), IN_SPLITS))
    b_q, b_kv, b_h, b_g = (b_in[:, o:o + n] for o, n in zip((o_q, o_kv, o_h, o_g), IN_SPLITS))
    bias = lambda acc, b: (acc + b,)
    both = lambda acc: (acc, acc)
    grad_outs = [("mn", F32), ("mn", MXU_DTYPE)]
    row_vec = lambda n: ((1, n), lambda i, j: (0, j))
    tile = lambda tm, tn: ((tm, tn), lambda i, j: (i, j))
    TM = 512
    BIG = min(T, 1024)

    u = _rmsnorm_fwd("norm_mix", x, norm_mix_g)
    pq, = _matmul("in_q", [(u, w_q)], "nt", tm=TM, tn=1024, tk=1024, outs=[("mn", MXU_DTYPE)],
                  extras=[(b_q, *row_vec(1024))], epilogue=bias)
    pkv, = _matmul("in_kv", [(u, w_kv)], "nt", tm=TM, tn=256, tk=1024, outs=[("mn", MXU_DTYPE)],
                   extras=[(b_kv, *row_vec(256))], epilogue=bias)
    names = ("w_branch_attn", "w_branch_hgrn", "w_out")
    (ph,), got = _matmul("in_h", [(u, w_h)], "nt", tm=TM, tn=1024, tk=1024, outs=[("mn", F32)],
                         extras=[(b_h, *row_vec(1024))], epilogue=bias, comm=net.gather(names))
    net.gathered(names, got)
    pg, = _matmul("in_g", [(u, w_g)], "nt", tm=TM, tn=1024, tk=1024, outs=[("mn", F32)],
                  extras=[(b_g, *row_vec(1024))], epilogue=bias)
    names = ("w_ffn_gate",)
    (y_attn, lse), got = _attn_fwd(pq, pkv, sinks, comm=net.gather(names))
    net.gathered(names, got)
    names = ("w_ffn_up",)
    (y_hgrn, o_raw, states), got = _hgrn_fwd(ph, lb_logits, hgrn_norm_g, comm=net.gather(names))
    net.gathered(names, got)
    w_ba, w_bh, w_out = net.full("w_branch_attn"), net.full("w_branch_hgrn"), net.full("w_out")
    w_gate, w_up = net.full("w_ffn_gate"), net.full("w_ffn_up")
    ya, = _matmul("branch_a", [(y_attn, w_ba)], "nn", tm=TM, tn=1024, tk=1024, outs=[("mn", F32)])
    gate_a = (pg, (TM, 1024), lambda i, j: (i, 0))
    gate_b = (pg, (TM, 1024), lambda i, j: (i, 1))

    def merge(acc, ya_v, ga, gb):
        return acc, _sigmoid(ga) * ya_v + _sigmoid(gb) * acc

    yb, merged = _matmul("branch_b", [(y_hgrn, w_bh)], "nn", tm=TM, tn=1024, tk=1024,
                         outs=[("mn", F32), ("mn", MXU_DTYPE)],
                         extras=[(ya, *tile(TM, 1024)), gate_a, gate_b], epilogue=merge)

    def resid_norm(acc, xin, g):
        h = xin + acc
        r = lax.rsqrt(jnp.mean(h * h, axis=-1, keepdims=True) + EPS)
        return h, h * r * g

    h1, u2 = _matmul("out_proj", [(merged, w_out)], "nn", tm=TM, tn=1024, tk=1024,
                     outs=[("mn", F32), ("mn", MXU_DTYPE)],
                     extras=[(x, *tile(TM, 1024)), (norm_ffn_g, *row_vec(1024))], epilogue=resid_norm)
    FT = FFN // 2
    names = ("w_ffn_down",)
    (gpre, up, z), got = _swiglu_fwd(u2, w_gate, w_up, tm=TM, tn=FT, comm=net.gather(names))
    net.gathered(names, got)
    w_down = net.full("w_ffn_down")

    def loss_head(acc, h1_v, tgt, g):
        h2 = h1_v + acc
        r = lax.rsqrt(jnp.mean(h2 * h2, axis=-1, keepdims=True) + EPS)
        xhat = h2 * r
        err = xhat * g - tgt
        part = 0.5 * jnp.sum(jnp.sum(err * err, axis=-1, keepdims=True), axis=0, keepdims=True) / D
        dyv = err / D
        dxh = dyv * g
        dh2 = r * (dxh - xhat * jnp.mean(dxh * xhat, axis=-1, keepdims=True))
        return dh2, dh2, jnp.sum(dyv * xhat, axis=0, keepdims=True), jnp.broadcast_to(part, (1, D))

    dh2, dh2b, dgf_p, loss_p = _matmul(
        "ffn_down", [(z, w_down)], "nn", tm=TM, tn=1024, tk=FFN,
        outs=[("mn", F32), ("mn", MXU_DTYPE), ("pn", F32), ("pn", F32)],
        extras=[(h1, *tile(TM, 1024)), (target, *tile(TM, 1024)), (norm_final_g, *row_vec(1024))],
        epilogue=loss_head)
    loss = jnp.sum(loss_p.reshape(-1, 8, D)[:, 0, 0])
    d_norm_final = _colsum_partials(dgf_p)

    def swiglu_bwd(acc, gv, upv):
        s = _sigmoid(gv)
        return acc * upv * (s * (1.0 + gv * (1.0 - s))), acc * (gv * s)

    dgp, dup = _matmul("d_ffn_hidden", [(dh2b, w_down)], "nt", tm=TM, tn=FT, tk=1024,
                       outs=[("mn", MXU_DTYPE), ("mn", MXU_DTYPE)],
                       extras=[(gpre, *tile(TM, FT)), (up, *tile(TM, FT))], epilogue=swiglu_bwd)
    d_w_down = _matmul("dw_down", [(z, dh2b)], "tn", tm=FT, tn=1024, tk=1024, outs=grad_outs, epilogue=both)

    def norm_ffn_bwd(acc, h1_v, g, dres):
        dx, dg = _rmsnorm_bwd_vals(acc, h1_v, g)
        dh = dres + dx
        return dh, dh, dg

    names = ("w_ffn_down",)
    (dh1, dh1b, dg2_p), got = _matmul(
        "d_ffn_in", [(dgp, w_gate), (dup, w_up)], "nn", tm=BIG, tn=1024, tk=FT,
        outs=[("mn", F32), ("mn", MXU_DTYPE), ("pn", F32)],
        extras=[(h1, *tile(BIG, 1024)), (norm_ffn_g, *row_vec(1024)), (dh2, *tile(BIG, 1024))],
        epilogue=norm_ffn_bwd,
        comm=net.exchange(dict(w_ffn_down=[d_w_down])))
    net.received(names, (), got)
    d_norm_ffn = _colsum_partials(dg2_p)
    d_w_gate = _matmul("dw_gate", [(dgp, u2)], "tn", tm=FT, tn=1024, tk=1024, outs=grad_outs, epilogue=both)
    d_w_up = _matmul("dw_up", [(dup, u2)], "tn", tm=FT, tn=1024, tk=1024, outs=grad_outs, epilogue=both)

    def merge_bwd(acc, ya_v, yb_v, ga, gb):
        sa, sb = _sigmoid(ga), _sigmoid(gb)
        return acc * sa, acc * sb, acc * ya_v * sa * (1.0 - sa), acc * yb_v * sb * (1.0 - sb)

    dya, dyb, dga, dgb = _matmul(
        "d_merged", [(dh1b, w_out)], "nt", tm=TM, tn=1024, tk=1024, outs=[("mn", MXU_DTYPE)] * 4,
        extras=[(ya, *tile(TM, 1024)), (yb, *tile(TM, 1024)), gate_a, gate_b], epilogue=merge_bwd)
    tn_grad = functools.partial(_matmul, mode="tn", tn=1024, tk=1024, outs=grad_outs, epilogue=both)
    d_w_out = tn_grad("dw_out", [(merged, dh1b)], tm=1024)
    dy_attn, = _matmul("d_y_attn", [(dya, w_ba)], "nt", tm=TM, tn=1024, tk=1024, outs=[("mn", MXU_DTYPE)])
    dy_hgrn, = _matmul("d_y_hgrn", [(dyb, w_bh)], "nt", tm=TM, tn=1024, tk=1024, outs=[("mn", F32)])
    d_w_ba = tn_grad("dw_branch_a", [(y_attn, dya)], tm=1024)
    d_w_bh = tn_grad("dw_branch_b", [(y_hgrn, dyb)], tm=1024)

    names, swap = ("w_ffn_gate",), ("w_ffn_down",)
    (dq, dkv, dsink), got = _attn_bwd(pq, pkv, sinks, lse, dy_attn,
                                      comm=net.exchange(dict(w_ffn_gate=[d_w_gate]), swap))
    net.received(names, swap, got)
    names, swap = ("w_ffn_up", "w_out", "w_branch_attn", "w_branch_hgrn"), ("w_ffn_gate",)
    (dph, d_hgrn_norm, d_lb_logits), got = _hgrn_bwd(
        ph, o_raw, states, dy_hgrn, lb_logits, hgrn_norm_g,
        comm=net.exchange(dict(w_ffn_up=[d_w_up], w_out=[d_w_out], w_branch_attn=[d_w_ba],
                               w_branch_hgrn=[d_w_bh]), swap))
    net.received(names, swap, got)

    d_w_in = [tn_grad("dw_in_q", [(dq, u)], tm=1024, a_colsum=True),
              tn_grad("dw_in_kv", [(dkv, u)], tm=256, a_colsum=True),
              tn_grad("dw_in_h", [(dph, u)], tm=1024),
              tn_grad("dw_in_ga", [(dga, u)], tm=1024, a_colsum=True),
              tn_grad("dw_in_gb", [(dgb, u)], tm=1024, a_colsum=True)]

    def norm_mix_bwd(acc, xin, g, dres):
        dx, dg = _rmsnorm_bwd_vals(acc, xin, g)
        return dres + dx, dg

    db_h, got = _colsum("db_h", dph, comm=net.presum_begin("w_in", d_w_in))
    halves = net.presum_end("w_in", got)
    names, swap = ("w_in",), ("w_ffn_up", "w_out", "w_branch_attn", "w_branch_hgrn")
    (dx, dg1_p), got = _matmul(
        "d_u", [(dq, w_q), (dkv, w_kv), (dph, w_h), (dga, w_g[:D]), (dgb, w_g[D:])], "nn",
        tm=BIG, tn=1024, tk=512, outs=[("mn", F32), ("pn", F32)],
        extras=[(x, *tile(BIG, 1024)), (norm_mix_g, *row_vec(1024)), (dh1, *tile(BIG, 1024))],
        epilogue=norm_mix_bwd,
        comm=_join(halves, net.swap(swap)))
    net.last = (names, swap, got)
    d_norm_mix = _colsum_partials(dg1_p)
    d_b_in = jnp.concatenate([d_w_in[0][2], d_w_in[1][2], db_h, d_w_in[3][2], d_w_in[4][2]], axis=1)
    vecs = dict(norm_mix_g=d_norm_mix, b_in=d_b_in, attn_sinks=jnp.sum(dsink.reshape(Q_HEADS, ATTN_BLOCK), axis=1).reshape(1, Q_HEADS),
                hgrn_lb_logits=d_lb_logits,
                hgrn_norm_g=d_hgrn_norm, norm_ffn_g=d_norm_ffn, norm_final_g=d_norm_final)
    return loss, dx, vecs


def _place():
    return lax.axis_index("x"), lax.axis_index("y"), lax.axis_index("c")


def _other_chips(x, y):
    return [(1 - x, y), (x, 1 - y), (1 - x, 1 - y)]


def _gather_copies(shards):
    n = len(shards)

    def build(ins, outs, send_sems, recv_sems, local_sems):
        x, y, c = _place()
        mine = 2 * x + y
        local = [pltpu.make_async_copy(ins[w], outs[w].at[mine], local_sems.at[w]) for w in range(n)]
        sends, recvs = [], []
        for w in range(n):
            for k, (px, py) in enumerate(_other_chips(x, y)):
                sem = 3 * w + k
                sends.append(pltpu.make_async_remote_copy(
                    src_ref=ins[w], dst_ref=outs[w].at[mine], send_sem=send_sems.at[sem], recv_sem=recv_sems.at[sem],
                    device_id=(px, py, c), device_id_type=MESH_ID))
                recvs.append(pltpu.make_async_remote_copy(
                    src_ref=ins[w], dst_ref=outs[w].at[2 * px + py], send_sem=send_sems.at[sem],
                    recv_sem=recv_sems.at[sem], device_id=(px, py, c), device_id_type=MESH_ID))
        return sends, recvs, local

    return _Carried(shards, [jax.ShapeDtypeStruct((N_CHIPS,) + s.shape, s.dtype) for s in shards], 3 * n, n, build)


def _grad_copies(stacked):
    n = len(stacked)

    def build(ins, outs, send_sems, recv_sems, local_sems):
        x, y, c = _place()
        sends = []
        for w in range(n):
            for k, (px, py) in enumerate(_other_chips(x, y)):
                sem = 3 * w + k
                sends.append(pltpu.make_async_remote_copy(
                    src_ref=ins[w].at[2 * px + py], dst_ref=outs[w].at[k], send_sem=send_sems.at[sem],
                    recv_sem=recv_sems.at[sem], device_id=(px, py, c), device_id_type=MESH_ID))
        return sends, sends, []

    return _Carried(stacked, [jax.ShapeDtypeStruct((3,) + s.shape[1:], s.dtype) for s in stacked], 3 * n, 0, build)


def _small_copies(small):
    def build(ins, outs, send_sems, recv_sems, local_sems):
        small_ref, all_ref = ins[0], outs[0]
        x, y, c = _place()
        me = 4 * x + 2 * y + c
        sends, recvs = [], []
        for r in range(1, 8):
            px = 1 - x if r & 4 else x
            py = 1 - y if r & 2 else y
            pc = 1 - c if r & 1 else c
            sends.append(pltpu.make_async_remote_copy(
                src_ref=small_ref, dst_ref=all_ref.at[me], send_sem=send_sems.at[r - 1], recv_sem=recv_sems.at[r - 1],
                device_id=(px, py, pc), device_id_type=MESH_ID))
            recvs.append(pltpu.make_async_remote_copy(
                src_ref=small_ref, dst_ref=all_ref.at[4 * px + 2 * py + pc], send_sem=send_sems.at[r - 1],
                recv_sem=recv_sems.at[r - 1], device_id=(px, py, pc), device_id_type=MESH_ID))
        return sends, recvs, [pltpu.make_async_copy(small_ref, all_ref.at[me], local_sems.at[0])]

    return _Carried([small], [jax.ShapeDtypeStruct((8,) + small.shape, small.dtype)], 7, 1, build)


def _gather_by_neighbours(name, shard):
    half = shard.shape[0] // 2
    quarter = half // 2

    def body(in_ref, out_ref, send_sems, recv_sems, local_sem):
        x, y, c = _place()
        chip = lambda px, py: 2 * px + py
        to_x, to_y, sibling = (1 - x, y, c), (x, 1 - y, c), (x, y, 1 - c)
        x_blk, y_blk, d_blk = chip(1 - x, y), chip(x, 1 - y), chip(1 - x, 1 - y)
        mine, theirs = c * half, (1 - c) * half

        def copy(sem, rows, block, to, src=None):
            place = out_ref.at[block, pl.ds(rows[0], rows[1])]
            return pltpu.make_async_remote_copy(
                src_ref=place if src is None else src, dst_ref=place, send_sem=send_sems.at[sem],
                recv_sem=recv_sems.at[sem], device_id=to, device_id_type=MESH_ID)

        own = pltpu.make_async_copy(in_ref, out_ref.at[chip(x, y)], local_sem)
        own.start()
        my_rows = in_ref.at[pl.ds(mine, half)]
        first = [copy(0, (mine, half), chip(x, y), to_x, src=my_rows), copy(1, (mine, half), chip(x, y), to_y, src=my_rows)]
        for cp in first:
            cp.start()
        copy(0, (mine, half), x_blk, to_x).wait_recv()
        onward = [copy(3, (mine + quarter, quarter), x_blk, to_y), copy(4, (mine, half), x_blk, sibling)]
        for cp in onward:
            cp.start()
        copy(1, (mine, half), y_blk, to_y).wait_recv()
        onward += [copy(2, (mine, quarter), y_blk, to_x), copy(5, (mine, half), y_blk, sibling)]
        for cp in onward[2:]:
            cp.start()
        copy(2, (mine, quarter), d_blk, to_x).wait_recv()
        copy(3, (mine + quarter, quarter), d_blk, to_y).wait_recv()
        onward.append(copy(6, (mine, half), d_blk, sibling))
        onward[-1].start()
        for sem, block in ((4, x_blk), (5, y_blk), (6, d_blk)):
            copy(sem, (theirs, half), block, sibling).wait_recv()
        for cp in first + onward:
            cp.wait_send()
        own.wait()

    return pl.pallas_call(
        body, name=name, in_specs=[HBM_SPEC], out_specs=HBM_SPEC,
        out_shape=jax.ShapeDtypeStruct((N_CHIPS,) + shard.shape, shard.dtype),
        scratch_shapes=[pltpu.SemaphoreType.DMA((7,)), pltpu.SemaphoreType.DMA((7,)), pltpu.SemaphoreType.DMA(())],
    )(shard)


def _copies_alone(name, comm):
    return _call(name, lambda: None, grid=(), in_specs=[], out_specs=[], out_shape=[], args=[], comm=comm)[1]


class _Net:
    def __init__(self, shards):
        self.shards = shards
        self.whole, self.own, self.theirs, self.sums, self.other = {}, {}, {}, {}, {}
        x, y, _ = _place()
        self.chip = 2 * x + y

    def gather(self, names):
        return _gather_copies([self.shards[n] for n in names])

    def gathered(self, names, got):
        for n, g in zip(names, got):
            self.whole[n] = g.reshape(-1, g.shape[-1])

    def full(self, name):
        return self.whole[name]

    def exchange(self, grads, swap=()):
        stacked = []
        for n, pieces in grads.items():
            keep = jnp.concatenate([p[0] for p in pieces], axis=0) if len(pieces) > 1 else pieces[0][0]
            send = jnp.concatenate([p[1] for p in pieces], axis=0) if len(pieces) > 1 else pieces[0][1]
            rows = keep.shape[0] // N_CHIPS
            self.own[n] = lax.dynamic_slice_in_dim(keep, self.chip * rows, rows, axis=0)
            stacked.append(send.reshape(N_CHIPS, rows, send.shape[-1]))
        return _join(_grad_copies(stacked), self.swap(swap))

    def swap(self, names):
        return _sibling_copies([self.sums[n] for n in names]) if names else None

    def presum_begin(self, name, pieces):
        keep = jnp.concatenate([p[0] for p in pieces], axis=0)
        send = jnp.concatenate([p[1] for p in pieces], axis=0)
        rows = keep.shape[0] // N_CHIPS
        self.held = keep.reshape(N_CHIPS, rows, keep.shape[-1])
        return _half_rows_copies(send.reshape(N_CHIPS, rows, send.shape[-1]))

    def presum_end(self, name, got):
        x, y, c = _place()
        to_send, self.own[name] = _pre_sum("presum_" + name, self.held, got[0], jnp.stack([c, self.chip]))
        return _grad_copies([to_send])

    def received(self, names, swap, got, carried=None):
        self.theirs.update(zip(names, got[:len(names)]))
        self.other.update(zip(swap, got[len(names):]))
        for n in names:
            (self.sums[n],), more = _partial_sum("sum_" + n, self.own[n], self.theirs[n], comm=carried)
        return more


def _half_rows_copies(stacked):
    n, rows = stacked.shape[0], stacked.shape[1] // 2

    def build(ins, outs, send_sems, recv_sems, local_sems):
        x, y, c = _place()
        copies = [pltpu.make_async_remote_copy(
            src_ref=ins[0].at[s, pl.ds((1 - c) * rows, rows)], dst_ref=outs[0].at[s], send_sem=send_sems.at[s],
            recv_sem=recv_sems.at[s], device_id=(x, y, 1 - c), device_id_type=MESH_ID) for s in range(n)]
        return copies, copies, []

    return _Carried([stacked], [jax.ShapeDtypeStruct((n, rows, stacked.shape[2]), stacked.dtype)], n, 0, build)


def _pre_sum(name, held, theirs, core_and_chip):
    n, R, C = held.shape
    half = R // 2
    tr = _row_tile(half)
    per_half = half // tr

    def body(place_ref, h_ref, t_ref, send_ref, own_ref):
        total = h_ref[0] + t_ref[0].astype(F32)
        send_ref[0] = total.astype(send_ref.dtype)

        @pl.when(pl.program_id(1) == place_ref[1])
        def _():
            own_ref[...] = total

    return pl.pallas_call(
        body, name=name,
        grid_spec=pltpu.PrefetchScalarGridSpec(
            num_scalar_prefetch=1, grid=(per_half, n),
            in_specs=[pl.BlockSpec((1, tr, C), lambda i, s, place: (s, place[0] * per_half + i, 0)),
                      pl.BlockSpec((1, tr, C), lambda i, s, place: (s, i, 0))],
            out_specs=[pl.BlockSpec((1, tr, C), lambda i, s, place: (s, i, 0)),
                       pl.BlockSpec((tr, C), lambda i, s, place: (i, 0))]),
        out_shape=[jax.ShapeDtypeStruct((n, half, C), MXU_DTYPE), jax.ShapeDtypeStruct((half, C), F32)],
        compiler_params=_params(("arbitrary", "arbitrary")),
    )(core_and_chip, held, theirs)


def _sibling_copies(parts):
    n = len(parts)

    def build(ins, outs, send_sems, recv_sems, local_sems):
        x, y, c = _place()
        copies = [pltpu.make_async_remote_copy(
            src_ref=ins[w], dst_ref=outs[w], send_sem=send_sems.at[w], recv_sem=recv_sems.at[w],
            device_id=(x, y, 1 - c), device_id_type=MESH_ID) for w in range(n)]
        return copies, copies, []

    return _Carried(parts, [jax.ShapeDtypeStruct(p.shape, p.dtype) for p in parts], n, 0, build)


def _row_tile(rows, most=512, sublanes=16):
    return max(t for t in range(sublanes, most + 1, sublanes) if rows % t == 0)


def _partial_sum(name, own, recv, comm=None):
    R, C = own.shape
    tr = _row_tile(R, most=128)

    def body(o_ref, r_ref, p_ref):
        p_ref[...] = ((o_ref[...] + r_ref[0].astype(F32)) + r_ref[1].astype(F32)) + r_ref[2].astype(F32)

    return _call(name, body, grid=(R // tr,),
                 in_specs=[pl.BlockSpec((tr, C), lambda i: (i, 0)), pl.BlockSpec((3, tr, C), lambda i: (0, i, 0))],
                 out_specs=[pl.BlockSpec((tr, C), lambda i: (i, 0))], out_shape=[jax.ShapeDtypeStruct((R, C), F32)],
                 args=[own, recv], semantics=("parallel",), comm=comm)


def _adam_vals(w, g, m, v):
    m = ADAM_B1 * m + (1.0 - ADAM_B1) * g
    v = ADAM_B2 * v + (1.0 - ADAM_B2) * (g * g)
    m_hat = m / (1.0 - ADAM_B1 ** ADAM_STEP)
    v_hat = v / (1.0 - ADAM_B2 ** ADAM_STEP)
    delta = -ADAM_LR * (m_hat / (jnp.sqrt(v_hat) + ADAM_EPS) + ADAM_WD * w)
    return delta, m, v


def _adamw(name, w, m, v, mine, other, comm=None):
    R, C = w.shape
    tr = _row_tile(R)

    def body(w_ref, m_ref, v_ref, s_ref, n_ref, g_ref, d_ref, nm_ref, nv_ref):
        g = s_ref[...] + n_ref[...]
        d, nm, nv = _adam_vals(w_ref[...], g, m_ref[...], v_ref[...])
        g_ref[...], d_ref[...], nm_ref[...], nv_ref[...] = g, d, nm, nv

    spec = pl.BlockSpec((tr, C), lambda i: (i, 0))
    return _call(name, body, grid=(R // tr,), in_specs=[spec] * 5, out_specs=[spec] * 4,
                 out_shape=[jax.ShapeDtypeStruct((R, C), F32)] * 4, args=[w, m, v, mine, other],
                 semantics=("parallel",), comm=comm)


def _adamw_by_halves(name, w, m, v, mine, other, core):
    R, C = w.shape
    tr = _row_tile(R // 2)
    per_half = R // 2 // tr

    def body(c_ref, w_ref, m_ref, v_ref, s_ref, n_ref, g_ref, d_ref, nm_ref, nv_ref):
        g = jnp.where(pl.program_id(0) // per_half == c_ref[0, 0], s_ref[...], n_ref[...])
        d, nm, nv = _adam_vals(w_ref[...], g, m_ref[...], v_ref[...])
        g_ref[...], d_ref[...], nm_ref[...], nv_ref[...] = g, d, nm, nv

    spec = pl.BlockSpec((tr, C), lambda i: (i, 0))
    part = pl.BlockSpec((tr, C), lambda i: (i % per_half, 0))
    return pl.pallas_call(
        body, name=name, grid=(R // tr,),
        in_specs=[pl.BlockSpec(memory_space=pltpu.SMEM), spec, spec, spec, part, part], out_specs=[spec] * 4,
        out_shape=[jax.ShapeDtypeStruct((R, C), F32)] * 4, compiler_params=_params(("parallel",)),
    )(core, w, m, v, mine, other)


def _adamw_small(w, m, v, g_all):
    def body(w_ref, m_ref, v_ref, a_ref, g_ref, d_ref, nm_ref, nv_ref):
        g = a_ref[0]
        for dev in range(1, 8):
            g = g + a_ref[dev]
        d, nm, nv = _adam_vals(w_ref[...], g, m_ref[...], v_ref[...])
        g_ref[...], d_ref[...], nm_ref[...], nv_ref[...] = g, d, nm, nv

    return pl.pallas_call(
        body, name="adamw_small", out_shape=[jax.ShapeDtypeStruct(w.shape, F32)] * 4,
    )(w, m, v, g_all)


SMALL_LAYOUT = dict(norm_mix_g=(0, 1024), b_in=(1024, 7424), hgrn_norm_g=(8448, 1024), norm_ffn_g=(9472, 1024),
                    norm_final_g=(10496, 1024), hgrn_lb_logits=(11520, 2048), attn_sinks=(13568, 16))
SMALL_LOSS, SMALL_SIZE = 13584, 16 * D_MODEL


def _pack_small(vals, loss=None):
    names = list(SMALL_LAYOUT)
    tail = jnp.zeros((SMALL_SIZE - SMALL_LAYOUT[names[-1]][0],), F32)
    tail = lax.dynamic_update_slice(tail, vals[names[-1]].astype(F32).reshape(-1), (0,))
    if loss is not None:
        tail = lax.dynamic_update_slice(tail, loss.astype(F32).reshape(1), (SMALL_LOSS - SMALL_LAYOUT[names[-1]][0],))
    flat = jnp.concatenate([vals[n].astype(F32).reshape(-1) for n in names[:-1]] + [tail])
    return flat.reshape(-1, D_MODEL)


def _unpack_small(packed, shapes):
    flat = packed.reshape(-1)
    return {name: flat[lo:lo + n].reshape(shapes[name]) for name, (lo, n) in SMALL_LAYOUT.items()}


MATRICES = ("w_in", "w_branch_attn", "w_branch_hgrn", "w_out", "w_ffn_gate", "w_ffn_up", "w_ffn_down")
COLUMN_SHARDED = ("w_in", "w_ffn_gate", "w_ffn_up")
WEIGHTS = ("norm_mix_g", "w_in", "b_in", "attn_sinks", "hgrn_lb_logits", "hgrn_norm_g", "w_branch_attn",
           "w_branch_hgrn", "w_out", "norm_ffn_g", "w_ffn_gate", "w_ffn_up", "w_ffn_down", "norm_final_g")


def kernel(x, norm_mix_g, w_in, b_in, attn_sinks, hgrn_lb_logits, hgrn_norm_g, w_branch_attn, w_branch_hgrn, w_out, norm_ffn_g, w_ffn_gate, w_ffn_up, w_ffn_down, norm_final_g, loss_target, m_norm_mix_g, m_w_in, m_b_in, m_attn_sinks, m_hgrn_lb_logits, m_hgrn_norm_g, m_w_branch_attn, m_w_branch_hgrn, m_w_out, m_norm_ffn_g, m_w_ffn_gate, m_w_ffn_up, m_w_ffn_down, m_norm_final_g, v_norm_mix_g, v_w_in, v_b_in, v_attn_sinks, v_hgrn_lb_logits, v_hgrn_norm_g, v_w_branch_attn, v_w_branch_hgrn, v_w_out, v_norm_ffn_g, v_w_ffn_gate, v_w_ffn_up, v_w_ffn_down, v_norm_final_g):
    given = dict(locals())
    w = {n: given[n] for n in WEIGHTS}
    m = {n: given["m_" + n] for n in WEIGHTS}
    v = {n: given["v_" + n] for n in WEIGHTS}

    block = lambda a, n: jnp.transpose(a[0]) if n in COLUMN_SHARDED else a[0]
    unblock = lambda a, n: (jnp.transpose(a) if n in COLUMN_SHARDED else a)[None]
    net = _Net({n: block(w[n], n).astype(MXU_DTYPE) for n in MATRICES})
    net.gathered(("w_in",), [_gather_by_neighbours("gather_w_in", net.shards["w_in"])])
    vec = dict(norm_mix_g=norm_mix_g, b_in=b_in, attn_sinks=attn_sinks, hgrn_lb_logits=hgrn_lb_logits,
               hgrn_norm_g=hgrn_norm_g, norm_ffn_g=norm_ffn_g, norm_final_g=norm_final_g.reshape(1, D_MODEL))
    loss_part, dx, d_vecs = _local_step(x[0], loss_target[0], vec, net)

    small_all, = net.received(*net.last, carried=_small_copies(_pack_small(d_vecs, loss_part)))
    grads, deltas, new_m, new_v = {}, {}, {}, {}
    for n in ("w_ffn_down", "w_ffn_gate", "w_ffn_up", "w_out", "w_branch_attn", "w_branch_hgrn"):
        res, got = _adamw("adamw_" + n, block(w[n], n), block(m[n], n), block(v[n], n), net.sums[n], net.other[n],
                          comm=net.swap(("w_in",)) if n == "w_ffn_down" else None)
        if n == "w_ffn_down":
            net.other["w_in"], = got
        grads[n], deltas[n], new_m[n], new_v[n] = (unblock(r, n) for r in res)
    n = "w_in"
    res = _adamw_by_halves("adamw_" + n, block(w[n], n), block(m[n], n), block(v[n], n), net.sums[n], net.other[n],
                           _place()[2].reshape(1, 1))
    grads[n], deltas[n], new_m[n], new_v[n] = (unblock(r, n) for r in res)
    shapes = {n: w[n].shape for n in SMALL_LAYOUT}
    res = _adamw_small(_pack_small(w), _pack_small(m), _pack_small(v), small_all)
    for dst, packed in zip((grads, deltas, new_m, new_v), res):
        dst.update(_unpack_small(packed, shapes))

    loss = res[0].reshape(-1)[SMALL_LOSS]
    return (loss, dx[None], *[grads[n] for n in WEIGHTS], *[deltas[n] for n in WEIGHTS],
            *[new_m[n] for n in WEIGHTS], *[new_v[n] for n in WEIGHTS])
```

```python
import functools
import math

import jax
import jax.numpy as jnp
from jax import lax
from jax.experimental import pallas as pl
from jax.experimental.pallas import tpu as pltpu

F32 = jnp.float32
BF16 = jnp.bfloat16
MXU_DTYPE = jnp.bfloat16
MESH_ID = pl.DeviceIdType.MESH

D_MODEL = 1024
HEAD_DIM = 64
Q_HEADS = 16
KV_HEADS = 2
GROUP = Q_HEADS // KV_HEADS
KV_WIDTH = KV_HEADS * HEAD_DIM
ATTN_BLOCK = 128
HGRN_HEADS = 8
HGRN_K = 128
CHUNK = 64
HGRN_TOKENS = 256
FFN = 2816
IN_SPLITS = (1024, 256, 4096, 2048)
EPS = 1e-6
NEG_INF = -1e30
ADAM_LR, ADAM_B1, ADAM_B2, ADAM_EPS, ADAM_WD, ADAM_STEP = 0.001, 0.9, 0.999, 1e-08, 0.01, 10
N_CHIPS = 4
VMEM_LIMIT = 60 * 1024 * 1024


def _params(sem=None):
    return pltpu.CompilerParams(dimension_semantics=sem, vmem_limit_bytes=VMEM_LIMIT)


def _sigmoid(v):
    return 0.5 * jnp.tanh(0.5 * v) + 0.5


def _dot(a, b, dims):
    return lax.dot_general(a.astype(MXU_DTYPE), b.astype(MXU_DTYPE), (dims, ((), ())),
                           preferred_element_type=F32)


def _nn(a, b):
    return _dot(a, b, ((1,), (0,)))


def _nt(a, b):
    return _dot(a, b, ((1,), (1,)))


def _tn(a, b):
    return _dot(a, b, ((0,), (0,)))


HBM_SPEC = pl.BlockSpec(memory_space=pl.ANY)


class _Carried:
    def __init__(self, arrays, out_shapes, n_remote, n_local, build):
        self.parts = [(len(arrays), len(out_shapes), build)]
        self.arrays, self.out_shapes = list(arrays), list(out_shapes)
        self.scratch = [pltpu.SemaphoreType.DMA((n_remote,)), pltpu.SemaphoreType.DMA((n_remote,)),
                        pltpu.SemaphoreType.DMA((max(n_local, 1),))]

    def __add__(self, other):
        both = _Carried([], [], 1, 0, None)
        both.parts = self.parts + other.parts
        both.arrays, both.out_shapes = self.arrays + other.arrays, self.out_shapes + other.out_shapes
        both.scratch = self.scratch + other.scratch
        return both

    def _built(self, ins, outs, sems):
        for p, (ni, no, build) in enumerate(self.parts):
            yield build(ins[:ni], outs[:no], *sems[3 * p:3 * p + 3])
            ins, outs = ins[ni:], outs[no:]

    def start(self, ins, outs, sems):
        core = lax.axis_index("c")
        for sends, _, local, *other_order in self._built(ins, outs, sems):
            for cp in local:
                cp.start()
            if not other_order:
                for cp in sends:
                    cp.start()
                continue

            @pl.when(core == 0)
            def _():
                for cp in sends:
                    cp.start()

            @pl.when(core == 1)
            def _():
                for cp in other_order[0]:
                    cp.start()

    def wait(self, ins, outs, sems):
        for sends, recvs, local, *_ in self._built(ins, outs, sems):
            for cp in recvs:
                cp.wait_recv()
            for cp in sends:
                cp.wait_send()
            for cp in local:
                cp.wait()


def _join(*comms):
    comms = [c for c in comms if c is not None]
    return functools.reduce(lambda a, b: a + b, comms) if comms else None


def _call(name, body, *, grid, in_specs, out_specs, out_shape, args, scratch=(), semantics=None, comm=None):
    n_in, n_out, n_scr = len(in_specs), len(out_specs), len(scratch)
    if comm is None:
        res = pl.pallas_call(body, name=name, grid=grid, in_specs=in_specs, out_specs=out_specs, out_shape=out_shape,
                             scratch_shapes=list(scratch), compiler_params=_params(semantics))(*args)
        return list(res), []
    ci, co = len(comm.arrays), len(comm.out_shapes)

    def carrying(*refs):
        ins, refs = refs[:n_in], refs[n_in:]
        c_ins, refs = refs[:ci], refs[ci:]
        outs, refs = refs[:n_out], refs[n_out:]
        c_outs, refs = refs[:co], refs[co:]
        scr, sems = refs[:n_scr], refs[n_scr:]
        if not grid:
            comm.start(c_ins, c_outs, sems)
            body(*ins, *outs, *scr)
            comm.wait(c_ins, c_outs, sems)
            return
        first = functools.reduce(jnp.logical_and, [pl.program_id(a) == 0 for a in range(len(grid))])
        last = functools.reduce(jnp.logical_and, [pl.program_id(a) == g - 1 for a, g in enumerate(grid)])

        @pl.when(first)
        def _():
            comm.start(c_ins, c_outs, sems)

        body(*ins, *outs, *scr)

        @pl.when(last)
        def _():
            comm.wait(c_ins, c_outs, sems)

    res = pl.pallas_call(
        carrying, name=name, grid=grid, in_specs=list(in_specs) + [HBM_SPEC] * ci,
        out_specs=list(out_specs) + [HBM_SPEC] * co, out_shape=list(out_shape) + comm.out_shapes,
        scratch_shapes=list(scratch) + comm.scratch,
        compiler_params=_params(("arbitrary",) * len(grid) if grid else None),
    )(*args, *comm.arrays)
    return list(res[:n_out]), list(res[n_out:])


_NO_COPIES = object()


def _matmul(name, pairs, mode, *, tm, tn, tk, outs, extras=(), epilogue=None, a_colsum=False, comm=_NO_COPIES):
    prod = dict(nn=_nn, nt=_nt, tn=_tn)[mode]
    a0, b0 = pairs[0]
    M = a0.shape[1] if mode == "tn" else a0.shape[0]
    N = b0.shape[0] if mode == "nt" else b0.shape[1]
    steps, in_specs, offset = [], [], 0
    for a, b in pairs:
        K = a.shape[0] if mode == "tn" else a.shape[1]
        t = min(tk, K)
        assert K % t == 0, (name, K, t)
        kmap = functools.partial(lambda k, off, n: jnp.clip(k - off, 0, n - 1), off=offset, n=K // t)
        if mode == "tn":
            in_specs.append(pl.BlockSpec((t, tm), functools.partial(lambda i, j, k, f: (f(k), i), f=kmap)))
        else:
            in_specs.append(pl.BlockSpec((tm, t), functools.partial(lambda i, j, k, f: (i, f(k)), f=kmap)))
        if mode == "nt":
            in_specs.append(pl.BlockSpec((tn, t), functools.partial(lambda i, j, k, f: (j, f(k)), f=kmap)))
        else:
            in_specs.append(pl.BlockSpec((t, tn), functools.partial(lambda i, j, k, f: (f(k), j), f=kmap)))
        steps.append((offset, offset + K // t))
        offset += K // t
    assert M % tm == 0 and N % tn == 0, (name, M, N, tm, tn)
    ni, nj, nk = M // tm, N // tn, offset
    npair, ne, no = len(pairs), len(extras), len(outs)
    if epilogue is None:
        epilogue = lambda acc: (acc,)

    def finish(acc, extra_refs, out_refs):
        vals = epilogue(acc, *[r[...] for r in extra_refs])
        for (kind, _), o_ref, val in zip(outs, out_refs, vals):
            if kind == "pn":
                val = jnp.broadcast_to(val, o_ref.shape)
            o_ref[...] = val.astype(o_ref.dtype)

    def body(*refs):
        ab, rest = refs[:2 * npair], refs[2 * npair:]
        extra_refs, out_refs = rest[:ne], rest[ne:ne + no]
        if nk == 1:
            finish(prod(ab[0][...], ab[1][...]), extra_refs, out_refs)
            return
        acc_ref = rest[-1]
        k = pl.program_id(2)

        @pl.when(k == 0)
        def _():
            acc_ref[...] = jnp.zeros_like(acc_ref)
            if a_colsum:
                rest[ne + no][...] = jnp.zeros((1, tm), F32)

        if a_colsum:
            rest[ne + no][...] += jnp.sum(ab[0][...].astype(F32), axis=0, keepdims=True)
        for p, (lo, hi) in enumerate(steps):
            @pl.when(jnp.logical_and(k >= lo, k < hi))
            def _():
                acc_ref[...] += prod(ab[2 * p][...], ab[2 * p + 1][...])

        @pl.when(k == nk - 1)
        def _():
            finish(acc_ref[...], extra_refs, out_refs)

    for _, shape, im in extras:
        in_specs.append(pl.BlockSpec(shape, functools.partial(lambda i, j, k, im: im(i, j), im=im)))
    out_shape, out_specs = [], []
    for kind, dt in outs:
        if kind == "mn":
            out_shape.append(jax.ShapeDtypeStruct((M, N), dt))
            out_specs.append(pl.BlockSpec((tm, tn), lambda i, j, k: (i, j)))
        else:
            out_shape.append(jax.ShapeDtypeStruct((8 * ni, N), dt))
            out_specs.append(pl.BlockSpec((8, tn), lambda i, j, k: (i, j)))
    if a_colsum:
        assert mode == "tn" and npair == 1 and nj == 1 and nk > 1, name
        out_shape.append(jax.ShapeDtypeStruct((1, M), F32))
        out_specs.append(pl.BlockSpec((1, tm), lambda i, j, k: (0, i)))
    grid = (ni, nj, nk)
    if nj > 1 and nk == 1:
        turned = lambda spec: pl.BlockSpec(spec.block_shape, functools.partial(
            lambda j, i, k, im: im(i, j, k), im=spec.index_map))
        in_specs, out_specs, grid = [turned(s) for s in in_specs], [turned(s) for s in out_specs], (nj, ni, nk)
    res, got = _call(name, body, grid=grid, in_specs=in_specs, out_specs=out_specs, out_shape=out_shape,
                     args=[t for pair in pairs for t in pair] + [e[0] for e in extras],
                     scratch=[pltpu.VMEM((tm, tn), F32)] if nk > 1 else [],
                     semantics=("parallel", "parallel", "arbitrary"), comm=None if comm is _NO_COPIES else comm)
    return res if comm is _NO_COPIES else (res, got)


def _swiglu_fwd(u, w_gate_t, w_up_t, *, tm, tn, comm=None):
    (T, D), F = u.shape, w_gate_t.shape[0]

    def body(u_ref, wg_ref, wu_ref, g_ref, up_ref, z_ref):
        g, up = _nt(u_ref[...], wg_ref[...]), _nt(u_ref[...], wu_ref[...])
        g_ref[...], up_ref[...] = g, up
        z_ref[...] = (g * _sigmoid(g) * up).astype(z_ref.dtype)

    w_spec = pl.BlockSpec((tn, D), lambda j, i: (j, 0))
    o_spec = pl.BlockSpec((tm, tn), lambda j, i: (i, j))
    return _call("ffn_hidden", body, grid=(F // tn, T // tm),
                 in_specs=[pl.BlockSpec((tm, D), lambda j, i: (i, 0)), w_spec, w_spec], out_specs=[o_spec] * 3,
                 out_shape=[jax.ShapeDtypeStruct((T, F), F32)] * 2 + [jax.ShapeDtypeStruct((T, F), MXU_DTYPE)],
                 args=[u, w_gate_t, w_up_t], semantics=("parallel", "parallel"), comm=comm)


def _colsum_partials(p):
    return jnp.sum(p.reshape(-1, 8, p.shape[-1])[:, 0, :], axis=0, keepdims=True)


def _rmsnorm_fwd(name, x, g, tr=512):
    T, D = x.shape

    def body(x_ref, g_ref, u_ref):
        xv = x_ref[...]
        r = lax.rsqrt(jnp.mean(xv * xv, axis=-1, keepdims=True) + EPS)
        u_ref[...] = (xv * r * g_ref[...]).astype(u_ref.dtype)

    return pl.pallas_call(
        body, name=name, grid=(T // tr,),
        in_specs=[pl.BlockSpec((tr, D), lambda i: (i, 0)), pl.BlockSpec((1, D), lambda i: (0, 0))],
        out_specs=pl.BlockSpec((tr, D), lambda i: (i, 0)),
        out_shape=jax.ShapeDtypeStruct((T, D), MXU_DTYPE),
        compiler_params=_params(("parallel",)),
    )(x, g)


def _rmsnorm_bwd_vals(dy, xin, g):
    rstd = lax.rsqrt(jnp.mean(xin * xin, axis=-1, keepdims=True) + EPS)
    xhat = xin * rstd
    dg = jnp.sum(dy * xhat, axis=0, keepdims=True)
    dxh = dy * g
    dx = rstd * (dxh - xhat * jnp.mean(dxh * xhat, axis=-1, keepdims=True))
    return dx, dg


def _colsum(name, a, tr=512, comm=_NO_COPIES):
    T, N = a.shape

    def body(a_ref, o_ref):
        @pl.when(pl.program_id(0) == 0)
        def _():
            o_ref[...] = jnp.zeros_like(o_ref)

        o_ref[...] += jnp.sum(a_ref[...].astype(F32), axis=0, keepdims=True)

    (res,), got = _call(name, body, grid=(T // tr,), in_specs=[pl.BlockSpec((tr, N), lambda i: (i, 0))],
                        out_specs=[pl.BlockSpec((1, N), lambda i: (0, 0))],
                        out_shape=[jax.ShapeDtypeStruct((1, N), F32)], args=[a], semantics=("arbitrary",),
                        comm=None if comm is _NO_COPIES else comm)
    return res if comm is _NO_COPIES else (res, got)


ATTN_SCALE = 1.0 / math.sqrt(HEAD_DIM)
GROUP_LANES = GROUP * ATTN_BLOCK
PAIR = 2 * HEAD_DIM


def _attn_mask():
    kj = lax.broadcasted_iota(jnp.int32, (ATTN_BLOCK, GROUP_LANES), 0)
    qi = lax.broadcasted_iota(jnp.int32, (ATTN_BLOCK, GROUP_LANES), 1) & (ATTN_BLOCK - 1)
    return kj <= qi


def _heads_transposed(ref, g, scale=None):
    parts = []
    for a in range(GROUP // 2):
        lo = (g * GROUP // 2 + a) * PAIR
        pair = ref[:, lo:lo + PAIR].astype(F32)
        pair = (pair if scale is None else pair * scale).T
        parts += [pair[:HEAD_DIM], pair[HEAD_DIM:]]
    return jnp.concatenate(parts, axis=1).astype(MXU_DTYPE)


def _heads_back(ref, g, vt):
    for a in range(GROUP // 2):
        lo = (g * GROUP // 2 + a) * PAIR
        pair = jnp.concatenate([vt[:, (2 * a) * ATTN_BLOCK:(2 * a + 1) * ATTN_BLOCK],
                                vt[:, (2 * a + 1) * ATTN_BLOCK:(2 * a + 2) * ATTN_BLOCK]], axis=0)
        ref[:, lo:lo + PAIR] = pair.T.astype(ref.dtype)


def _kv_parts(kv_ref, g):
    ks = slice(g * HEAD_DIM, (g + 1) * HEAD_DIM)
    vs = slice(KV_WIDTH + g * HEAD_DIM, KV_WIDTH + (g + 1) * HEAD_DIM)
    return kv_ref[:, ks].astype(MXU_DTYPE), kv_ref[:, vs].astype(MXU_DTYPE)


def _sink_rows(sinks):
    return jnp.repeat(sinks.reshape(KV_HEADS, GROUP), ATTN_BLOCK, axis=1)


def _attn_fwd(pq, pkv, sinks, comm=None):
    T = pq.shape[0]
    nb = T // ATTN_BLOCK

    def body(q_ref, kvc_ref, kvp_ref, s_ref, y_ref, lse_ref):
        mask_c = _attn_mask()
        has_prev = pl.program_id(0) > 0
        for g in range(KV_HEADS):
            (kc, vc), (kp, vp) = _kv_parts(kvc_ref, g), _kv_parts(kvp_ref, g)
            qt = _heads_transposed(q_ref, g, ATTN_SCALE)
            s = jnp.where(mask_c, _nn(kc, qt), jnp.where(has_prev, _nn(kp, qt), NEG_INF))
            sink = s_ref[g:g + 1, :]
            m = jnp.maximum(jnp.max(s, axis=0, keepdims=True), sink)
            p = jnp.exp(s - m)
            den = jnp.sum(p, axis=0, keepdims=True) + jnp.exp(sink - m)
            pc = jnp.where(mask_c, p, 0.0)
            _heads_back(y_ref, g, (_tn(vc, pc) + _tn(vp, p - pc)) / den)
            lse = m + jnp.log(den)
            for i in range(GROUP):
                lse_ref[g * GROUP + i:g * GROUP + i + 1, :] = lse[:, i * ATTN_BLOCK:(i + 1) * ATTN_BLOCK]

    return _call(
        "attn_fwd", body, grid=(nb,),
        in_specs=[pl.BlockSpec((ATTN_BLOCK, D_MODEL), lambda n: (n, 0)),
                  pl.BlockSpec((ATTN_BLOCK, 2 * KV_WIDTH), lambda n: (n, 0)),
                  pl.BlockSpec((ATTN_BLOCK, 2 * KV_WIDTH), lambda n: (jnp.maximum(n - 1, 0), 0)),
                  pl.BlockSpec((KV_HEADS, GROUP_LANES), lambda n: (0, 0))],
        out_specs=[pl.BlockSpec((ATTN_BLOCK, D_MODEL), lambda n: (n, 0)),
                   pl.BlockSpec((Q_HEADS, ATTN_BLOCK), lambda n: (0, n))],
        out_shape=[jax.ShapeDtypeStruct((T, D_MODEL), MXU_DTYPE), jax.ShapeDtypeStruct((Q_HEADS, T), F32)],
        args=[pq, pkv, pkv, _sink_rows(sinks)], semantics=("parallel",), comm=comm)


def _attn_bwd(pq, pkv, sinks, lse, dy, comm=None):
    T = pq.shape[0]
    nb = T // ATTN_BLOCK
    cur = lambda n: (jnp.minimum(n, nb - 1), 0)

    def body(q_ref, kvc_ref, kvp_ref, s_ref, lse_ref, dy_ref, dq_ref, dkv_ref, ds_ref, carry, top, bot):
        n = pl.program_id(0)

        @pl.when(n == 0)
        def _():
            carry[...] = jnp.zeros_like(carry)
            ds_ref[...] = jnp.zeros_like(ds_ref)

        @pl.when(n < nb)
        def _():
            mask_c = _attn_mask()
            valid = jnp.logical_or(mask_c, n > 0)
            for g in range(KV_HEADS):
                ks = slice(g * HEAD_DIM, (g + 1) * HEAD_DIM)
                vs = slice(KV_WIDTH + g * HEAD_DIM, KV_WIDTH + (g + 1) * HEAD_DIM)
                (kc, vc), (kp, vp) = _kv_parts(kvc_ref, g), _kv_parts(kvp_ref, g)
                qt = _heads_transposed(q_ref, g, ATTN_SCALE)
                dot = _heads_transposed(dy_ref, g)
                lse = jnp.concatenate([lse_ref[g * GROUP + i:g * GROUP + i + 1, :] for i in range(GROUP)], axis=1)
                p = jnp.where(valid, jnp.exp(jnp.where(mask_c, _nn(kc, qt), _nn(kp, qt)) - lse), 0.0)
                dp = jnp.where(mask_c, _nn(vc, dot), _nn(vp, dot))
                delta = jnp.sum(p * dp, axis=0, keepdims=True)
                ds = p * (dp - delta)
                ds_c, p_c = jnp.where(mask_c, ds, 0.0), jnp.where(mask_c, p, 0.0)
                ds_p, p_p = ds - ds_c, p - p_c
                _heads_back(dq_ref, g, (_tn(kc, ds_c) + _tn(kp, ds_p)) * ATTN_SCALE)
                bot[:, ks], bot[:, vs] = _nt(ds_c, qt), _nt(p_c, dot)
                top[:, ks], top[:, vs] = _nt(ds_p, qt), _nt(p_p, dot)
                ds_ref[g:g + 1, :] -= jnp.exp(s_ref[g:g + 1, :] - lse) * delta
            dkv_ref[...] = (carry[...] + top[...]).astype(dkv_ref.dtype)
            carry[...] = bot[...]

        @pl.when(n == nb)
        def _():
            dkv_ref[...] = carry[...].astype(dkv_ref.dtype)

    return _call(
        "attn_bwd", body, grid=(nb + 1,),
        in_specs=[pl.BlockSpec((ATTN_BLOCK, D_MODEL), cur),
                  pl.BlockSpec((ATTN_BLOCK, 2 * KV_WIDTH), cur),
                  pl.BlockSpec((ATTN_BLOCK, 2 * KV_WIDTH), lambda n: (jnp.maximum(jnp.minimum(n, nb - 1) - 1, 0), 0)),
                  pl.BlockSpec((KV_HEADS, GROUP_LANES), lambda n: (0, 0)),
                  pl.BlockSpec((Q_HEADS, ATTN_BLOCK), lambda n: (0, jnp.minimum(n, nb - 1))),
                  pl.BlockSpec((ATTN_BLOCK, D_MODEL), cur)],
        out_specs=[pl.BlockSpec((ATTN_BLOCK, D_MODEL), cur),
                   pl.BlockSpec((ATTN_BLOCK, 2 * KV_WIDTH), lambda n: (jnp.maximum(n - 1, 0), 0)),
                   pl.BlockSpec((KV_HEADS, GROUP_LANES), lambda n: (0, 0))],
        out_shape=[jax.ShapeDtypeStruct((T, D_MODEL), MXU_DTYPE),
                   jax.ShapeDtypeStruct((T, 2 * KV_WIDTH), MXU_DTYPE),
                   jax.ShapeDtypeStruct((KV_HEADS, GROUP_LANES), F32)],
        scratch=[pltpu.VMEM((ATTN_BLOCK, 2 * KV_WIDTH), F32)] * 3,
        args=[pq, pkv, pkv, _sink_rows(sinks), lse, dy], semantics=("arbitrary",), comm=comm)


def _lower_bound(l):
    m = jnp.maximum(l[0:1], l[1:2])
    e0, e1 = jnp.exp(l[0:1] - m), jnp.exp(l[1:2] - m)
    return e0 / (e0 + e1)


def _tri(lower):
    r = lax.broadcasted_iota(jnp.int32, (CHUNK, CHUNK), 0)
    c = lax.broadcasted_iota(jnp.int32, (CHUNK, CHUNK), 1)
    return (r >= c) if lower else (c >= r)


def _chunk_sum(mask, v):
    ones = mask.astype(BF16)
    hi = v.astype(BF16)
    rest = v - hi.astype(F32)
    mid = rest.astype(BF16)
    lo = (rest - mid.astype(F32)).astype(BF16)
    part = lambda t: lax.dot_general(ones, t, (((1,), (0,)), ((), ())), preferred_element_type=F32)
    return part(hi) + part(mid) + part(lo)


def _hgrn_chunk_inputs(hq, hf, lb, causal):
    sg, sgn = _sigmoid(hf), _sigmoid(-hf)
    f = lb + (1.0 - lb) * sg
    kk = (1.0 - lb) * sgn
    sq = _sigmoid(hq)
    q = hq * sq
    b = _chunk_sum(causal, jnp.log(f))
    bm, bl = b[CHUNK // 2 - 1:CHUNK // 2, :], b[CHUNK - 1:CHUNK, :]
    e_qm, e_km, e_qs, e_kl = jnp.exp(b - bm), jnp.exp(bm - b), jnp.exp(b), jnp.exp(bl - b)
    return dict(sg=sg, sgn=sgn, f=f, kk=kk, sq=sq, q=q, e_qm=e_qm, e_km=e_km, e_qs=e_qs, e_kl=e_kl,
                qm=q * e_qm, km=kk * e_km, qs=q * e_qs, kl=kk * e_kl, el=jnp.exp(bl))


def _hgrn_fwd(ph, lb_logits, norm_g, comm=None):
    T = ph.shape[0]
    nblk, cpb = T // HGRN_TOKENS, HGRN_TOKENS // CHUNK
    col = lambda c: pl.BlockSpec((HGRN_TOKENS, D_MODEL), functools.partial(lambda i, c: (i, c), c=c))

    def body(hq_ref, hf_ref, hi_ref, hg_ref, l_ref, ng_ref, y_ref, o_ref, st_ref, s_ref):
        @pl.when(pl.program_id(0) == 0)
        def _():
            s_ref[...] = jnp.zeros_like(s_ref)

        lb = _lower_bound(l_ref[...])
        causal = _tri(True)
        for c in range(cpb):
            rows = slice(c * CHUNK, (c + 1) * CHUNK)
            t = _hgrn_chunk_inputs(hq_ref[rows, :], hf_ref[rows, :], lb, causal)
            qm, km, qs, kl = (t[n].astype(MXU_DTYPE) for n in ("qm", "km", "qs", "kl"))
            v = hi_ref[rows, :].astype(MXU_DTYPE)
            for h in range(HGRN_HEADS):
                ls = slice(h * HGRN_K, (h + 1) * HGRN_K)
                st = s_ref[h]
                st_ref[c, ls, :] = st
                a = jnp.where(causal, _nt(qm[:, ls], km[:, ls]), 0.0)
                o_ref[rows, ls] = _nn(a, v[:, ls]) + _nt(qs[:, ls], st)
                s_ref[h] = t["el"][:, ls] * st + _tn(v[:, ls], kl[:, ls])
        for h in range(HGRN_HEADS):
            ls = slice(h * HGRN_K, (h + 1) * HGRN_K)
            o = o_ref[:, ls]
            r = lax.rsqrt(jnp.mean(o * o, axis=-1, keepdims=True) + EPS)
            y_ref[:, ls] = (o * r * ng_ref[:, ls] * _sigmoid(hg_ref[:, ls])).astype(y_ref.dtype)

    return _call(
        "hgrn_fwd", body, grid=(nblk,),
        in_specs=[col(0), col(1), col(2), col(3),
                  pl.BlockSpec((2, D_MODEL), lambda i: (0, 0)), pl.BlockSpec((1, D_MODEL), lambda i: (0, 0))],
        out_specs=[pl.BlockSpec((HGRN_TOKENS, D_MODEL), lambda i: (i, 0)),
                   pl.BlockSpec((HGRN_TOKENS, D_MODEL), lambda i: (i, 0)),
                   pl.BlockSpec((cpb, D_MODEL, HGRN_K), lambda i: (i, 0, 0))],
        out_shape=[jax.ShapeDtypeStruct((T, D_MODEL), MXU_DTYPE), jax.ShapeDtypeStruct((T, D_MODEL), F32),
                   jax.ShapeDtypeStruct((T // CHUNK, D_MODEL, HGRN_K), F32)],
        scratch=[pltpu.VMEM((HGRN_HEADS, HGRN_K, HGRN_K), F32)],
        args=[ph, ph, ph, ph, lb_logits, norm_g], semantics=("arbitrary",), comm=comm)


def _hgrn_bwd(ph, o_raw, states, dy, lb_logits, norm_g, comm=None):
    T = ph.shape[0]
    nblk, cpb = T // HGRN_TOKENS, HGRN_TOKENS // CHUNK
    rev = lambda i: nblk - 1 - i
    col = lambda c: pl.BlockSpec((HGRN_TOKENS, D_MODEL), functools.partial(lambda i, c: (rev(i), c), c=c))
    tok = pl.BlockSpec((HGRN_TOKENS, D_MODEL), lambda i: (rev(i), 0))

    def body(hq_ref, hf_ref, hi_ref, hg_ref, o_ref, st_ref, dy_ref, l_ref, ng_ref,
             dph_ref, dng_ref, dl_ref, dst_ref, dlb_ref, do_s, dqm_s, dkm_s, dqs_s, dkl_s, dv_s, del_s):
        i = pl.program_id(0)

        @pl.when(i == 0)
        def _():
            dst_ref[...] = jnp.zeros_like(dst_ref)
            dlb_ref[...] = jnp.zeros_like(dlb_ref)
            dng_ref[...] = jnp.zeros_like(dng_ref)

        lb = _lower_bound(l_ref[...])
        causal, anti = _tri(True), _tri(False)
        row = lax.broadcasted_iota(jnp.int32, (CHUNK, D_MODEL), 0)
        for c in reversed(range(cpb)):
            rows = slice(c * CHUNK, (c + 1) * CHUNK)
            hq = hq_ref[rows, :]
            t = _hgrn_chunk_inputs(hq, hf_ref[rows, :], lb, causal)
            sgg = _sigmoid(hg_ref[rows, :])
            dyv = dy_ref[rows, :]
            for h in range(HGRN_HEADS):
                ls = slice(h * HGRN_K, (h + 1) * HGRN_K)
                o = o_ref[rows, ls]
                r = lax.rsqrt(jnp.mean(o * o, axis=-1, keepdims=True) + EPS)
                nrm = o * r
                g_h = sgg[:, ls]
                dph_ref[rows, 3 * D_MODEL + h * HGRN_K:3 * D_MODEL + (h + 1) * HGRN_K] = (
                    dyv[:, ls] * nrm * ng_ref[:, ls] * g_h * (1.0 - g_h)).astype(dph_ref.dtype)
                dyg = dyv[:, ls] * g_h
                dng_ref[:, ls] += jnp.sum(dyg * nrm, axis=0, keepdims=True)
                dn = dyg * ng_ref[:, ls]
                do_s[:, ls] = r * (dn - nrm * jnp.mean(dn * nrm, axis=-1, keepdims=True))
            qm, km, qs, kl = (t[n].astype(MXU_DTYPE) for n in ("qm", "km", "qs", "kl"))
            v = hi_ref[rows, :].astype(MXU_DTYPE)
            do = do_s[...].astype(MXU_DTYPE)
            for h in range(HGRN_HEADS):
                ls = slice(h * HGRN_K, (h + 1) * HGRN_K)
                st = st_ref[c, ls, :]
                dst = dst_ref[h]
                a = jnp.where(causal, _nt(qm[:, ls], km[:, ls]), 0.0)
                da = jnp.where(causal, _nt(do[:, ls], v[:, ls]), 0.0)
                dv_s[:, ls] = _tn(a, do[:, ls]) + _nt(kl[:, ls], dst)
                dkl_s[:, ls] = _nn(v[:, ls], dst)
                dqs_s[:, ls] = _nn(do[:, ls], st)
                del_s[:, ls] = jnp.sum(dst * st, axis=0, keepdims=True)
                dst_ref[h] = _tn(do[:, ls], qs[:, ls]) + t["el"][:, ls] * dst
                dqm_s[:, ls] = _nn(da, km[:, ls])
                dkm_s[:, ls] = _tn(da, qm[:, ls])
            dqm, dkm, dqs, dkl = dqm_s[...], dkm_s[...], dqs_s[...], dkl_s[...]
            dq = dqm * t["e_qm"] + dqs * t["e_qs"]
            dk = dkm * t["e_km"] + dkl * t["e_kl"]
            t_qm, t_km, t_kl = dqm * t["qm"], dkm * t["km"], dkl * t["kl"]
            db = t_qm - t_km + dqs * t["qs"] - t_kl
            db_mid = jnp.sum(t_km - t_qm, axis=0, keepdims=True)
            db_last = jnp.sum(t_kl, axis=0, keepdims=True) + del_s[...] * t["el"]
            db = db + jnp.where(row == CHUNK // 2 - 1, db_mid, 0.0) + jnp.where(row == CHUNK - 1, db_last, 0.0)
            dlogf = _chunk_sum(anti, db)
            sq, sg, sgn, f = t["sq"], t["sg"], t["sgn"], t["f"]
            dph_ref[rows, 0:D_MODEL] = (dq * (sq * (1.0 + hq * (1.0 - sq)))).astype(dph_ref.dtype)
            dph_ref[rows, D_MODEL:2 * D_MODEL] = (
                dlogf * (1.0 - lb) * sg * (1.0 - sg) / f - dk * (1.0 - lb) * sgn * (1.0 - sgn)).astype(dph_ref.dtype)
            dph_ref[rows, 2 * D_MODEL:3 * D_MODEL] = dv_s[...].astype(dph_ref.dtype)
            dlb_ref[...] += jnp.sum(dlogf * (1.0 - sg) / f - dk * sgn, axis=0, keepdims=True)

        @pl.when(i == nblk - 1)
        def _():
            dl0 = dlb_ref[...] * lb * (1.0 - lb)
            dl_ref[0:1, :] = dl0
            dl_ref[1:2, :] = -dl0

    wide = pltpu.VMEM((CHUNK, D_MODEL), F32)
    return _call(
        "hgrn_bwd", body, grid=(nblk,),
        in_specs=[col(0), col(1), col(2), col(3), tok,
                  pl.BlockSpec((cpb, D_MODEL, HGRN_K), lambda i: (rev(i), 0, 0)), tok,
                  pl.BlockSpec((2, D_MODEL), lambda i: (0, 0)), pl.BlockSpec((1, D_MODEL), lambda i: (0, 0))],
        out_specs=[pl.BlockSpec((HGRN_TOKENS, 4 * D_MODEL), lambda i: (rev(i), 0)),
                   pl.BlockSpec((1, D_MODEL), lambda i: (0, 0)), pl.BlockSpec((2, D_MODEL), lambda i: (0, 0))],
        out_shape=[jax.ShapeDtypeStruct((T, 4 * D_MODEL), MXU_DTYPE), jax.ShapeDtypeStruct((1, D_MODEL), F32),
                   jax.ShapeDtypeStruct((2, D_MODEL), F32)],
        scratch=[pltpu.VMEM((HGRN_HEADS, HGRN_K, HGRN_K), F32), pltpu.VMEM((1, D_MODEL), F32),
                 wide, wide, wide, wide, wide, wide, pltpu.VMEM((1, D_MODEL), F32)],
        args=[ph, ph, ph, ph, o_raw, states, dy, lb_logits, norm_g], semantics=("arbitrary",), comm=comm)


def _local_step(x, target, vec, net):
    T, D = x.shape
    norm_mix_g, b_in, sinks, lb_logits = vec["norm_mix_g"], vec["b_in"], vec["attn_sinks"], vec["hgrn_lb_logits"]
    hgrn_norm_g, norm_ffn_g, norm_final_g = vec["hgrn_norm_g"], vec["norm_ffn_g"], vec["norm_final_g"]
    w_in = net.full("w_in")
    o_q, o_kv, o_h, o_g = (sum(IN_SPLITS[:i]) for i in range(4))
    w_q, w_kv, w_h, w_g = (w_in[o:o + n] for o, n in zip((o_q, o_kv, o_h, o_g), IN_SPLITS))
    b_q, b_kv, b_h, b_g = (b_in[:, o:o + n] for o, n in zip((o_q, o_kv, o_h, o_g), IN_SPLITS))
    bias = lambda acc, b: (acc + b,)
    both = lambda acc: (acc, acc)
    grad_outs = [("mn", F32), ("mn", MXU_DTYPE)]
    row_vec = lambda n: ((1, n), lambda i, j: (0, j))
    tile = lambda tm, tn: ((tm, tn), lambda i, j: (i, j))
    TM = 512
    BIG = min(T, 1024)

    u = _rmsnorm_fwd("norm_mix", x, norm_mix_g)
    pq, = _matmul("in_q", [(u, w_q)], "nt", tm=TM, tn=1024, tk=1024, outs=[("mn", MXU_DTYPE)],
                  extras=[(b_q, *row_vec(1024))], epilogue=bias)
    pkv, = _matmul("in_kv", [(u, w_kv)], "nt", tm=TM, tn=256, tk=1024, outs=[("mn", MXU_DTYPE)],
                   extras=[(b_kv, *row_vec(256))], epilogue=bias)
    names = ("w_branch_attn", "w_branch_hgrn", "w_out")
    (ph,), got = _matmul("in_h", [(u, w_h)], "nt", tm=TM, tn=1024, tk=1024, outs=[("mn", F32)],
                         extras=[(b_h, *row_vec(1024))], epilogue=bias, comm=net.gather(names))
    net.gathered(names, got)
    pg, = _matmul("in_g", [(u, w_g)], "nt", tm=TM, tn=1024, tk=1024, outs=[("mn", F32)],
                  extras=[(b_g, *row_vec(1024))], epilogue=bias)
    names = ("w_ffn_gate",)
    (y_attn, lse), got = _attn_fwd(pq, pkv, sinks, comm=net.gather(names))
    net.gathered(names, got)
    names = ("w_ffn_up",)
    (y_hgrn, o_raw, states), got = _hgrn_fwd(ph, lb_logits, hgrn_norm_g, comm=net.gather(names))
    net.gathered(names, got)
    w_ba, w_bh, w_out = net.full("w_branch_attn"), net.full("w_branch_hgrn"), net.full("w_out")
    w_gate, w_up = net.full("w_ffn_gate"), net.full("w_ffn_up")
    ya, = _matmul("branch_a", [(y_attn, w_ba)], "nn", tm=TM, tn=1024, tk=1024, outs=[("mn", F32)])
    gate_a = (pg, (TM, 1024), lambda i, j: (i, 0))
    gate_b = (pg, (TM, 1024), lambda i, j: (i, 1))

    def merge(acc, ya_v, ga, gb):
        return acc, _sigmoid(ga) * ya_v + _sigmoid(gb) * acc

    yb, merged = _matmul("branch_b", [(y_hgrn, w_bh)], "nn", tm=TM, tn=1024, tk=1024,
                         outs=[("mn", F32), ("mn", MXU_DTYPE)],
                         extras=[(ya, *tile(TM, 1024)), gate_a, gate_b], epilogue=merge)

    def resid_norm(acc, xin, g):
        h = xin + acc
        r = lax.rsqrt(jnp.mean(h * h, axis=-1, keepdims=True) + EPS)
        return h, h * r * g

    h1, u2 = _matmul("out_proj", [(merged, w_out)], "nn", tm=TM, tn=1024, tk=1024,
                     outs=[("mn", F32), ("mn", MXU_DTYPE)],
                     extras=[(x, *tile(TM, 1024)), (norm_ffn_g, *row_vec(1024))], epilogue=resid_norm)
    FT = FFN // 2
    names = ("w_ffn_down",)
    (gpre, up, z), got = _swiglu_fwd(u2, w_gate, w_up, tm=TM, tn=FT, comm=net.gather(names))
    net.gathered(names, got)
    w_down = net.full("w_ffn_down")

    def loss_head(acc, h1_v, tgt, g):
        h2 = h1_v + acc
        r = lax.rsqrt(jnp.mean(h2 * h2, axis=-1, keepdims=True) + EPS)
        xhat = h2 * r
        err = xhat * g - tgt
        part = 0.5 * jnp.sum(jnp.sum(err * err, axis=-1, keepdims=True), axis=0, keepdims=True) / D
        dyv = err / D
        dxh = dyv * g
        dh2 = r * (dxh - xhat * jnp.mean(dxh * xhat, axis=-1, keepdims=True))
        return dh2, dh2, jnp.sum(dyv * xhat, axis=0, keepdims=True), jnp.broadcast_to(part, (1, D))

    dh2, dh2b, dgf_p, loss_p = _matmul(
        "ffn_down", [(z, w_down)], "nn", tm=TM, tn=1024, tk=FFN,
        outs=[("mn", F32), ("mn", MXU_DTYPE), ("pn", F32), ("pn", F32)],
        extras=[(h1, *tile(TM, 1024)), (target, *tile(TM, 1024)), (norm_final_g, *row_vec(1024))],
        epilogue=loss_head)
    loss = jnp.sum(loss_p.reshape(-1, 8, D)[:, 0, 0])
    d_norm_final = _colsum_partials(dgf_p)

    def swiglu_bwd(acc, gv, upv):
        s = _sigmoid(gv)
        return acc * upv * (s * (1.0 + gv * (1.0 - s))), acc * (gv * s)

    dgp, dup = _matmul("d_ffn_hidden", [(dh2b, w_down)], "nt", tm=TM, tn=FT, tk=1024,
                       outs=[("mn", MXU_DTYPE), ("mn", MXU_DTYPE)],
                       extras=[(gpre, *tile(TM, FT)), (up, *tile(TM, FT))], epilogue=swiglu_bwd)
    d_w_down = _matmul("dw_down", [(z, dh2b)], "tn", tm=FT, tn=1024, tk=1024, outs=grad_outs, epilogue=both)

    def norm_ffn_bwd(acc, h1_v, g, dres):
        dx, dg = _rmsnorm_bwd_vals(acc, h1_v, g)
        dh = dres + dx
        return dh, dh, dg

    names = ("w_ffn_down",)
    (dh1, dh1b, dg2_p), got = _matmul(
        "d_ffn_in", [(dgp, w_gate), (dup, w_up)], "nn", tm=BIG, tn=1024, tk=FT,
        outs=[("mn", F32), ("mn", MXU_DTYPE), ("pn", F32)],
        extras=[(h1, *tile(BIG, 1024)), (norm_ffn_g, *row_vec(1024)), (dh2, *tile(BIG, 1024))],
        epilogue=norm_ffn_bwd,
        comm=net.exchange(dict(w_ffn_down=[d_w_down])))
    net.received(names, (), got)
    d_norm_ffn = _colsum_partials(dg2_p)
    d_w_gate = _matmul("dw_gate", [(dgp, u2)], "tn", tm=FT, tn=1024, tk=1024, outs=grad_outs, epilogue=both)
    d_w_up = _matmul("dw_up", [(dup, u2)], "tn", tm=FT, tn=1024, tk=1024, outs=grad_outs, epilogue=both)

    def merge_bwd(acc, ya_v, yb_v, ga, gb):
        sa, sb = _sigmoid(ga), _sigmoid(gb)
        return acc * sa, acc * sb, acc * ya_v * sa * (1.0 - sa), acc * yb_v * sb * (1.0 - sb)

    dya, dyb, dga, dgb = _matmul(
        "d_merged", [(dh1b, w_out)], "nt", tm=TM, tn=1024, tk=1024, outs=[("mn", MXU_DTYPE)] * 4,
        extras=[(ya, *tile(TM, 1024)), (yb, *tile(TM, 1024)), gate_a, gate_b], epilogue=merge_bwd)
    tn_grad = functools.partial(_matmul, mode="tn", tn=1024, tk=1024, outs=grad_outs, epilogue=both)
    d_w_out = tn_grad("dw_out", [(merged, dh1b)], tm=1024)
    dy_attn, = _matmul("d_y_attn", [(dya, w_ba)], "nt", tm=TM, tn=1024, tk=1024, outs=[("mn", MXU_DTYPE)])
    dy_hgrn, = _matmul("d_y_hgrn", [(dyb, w_bh)], "nt", tm=TM, tn=1024, tk=1024, outs=[("mn", F32)])
    d_w_ba = tn_grad("dw_branch_a", [(y_attn, dya)], tm=1024)
    d_w_bh = tn_grad("dw_branch_b", [(y_hgrn, dyb)], tm=1024)

    names, swap = ("w_ffn_gate",), ("w_ffn_down",)
    (dq, dkv, dsink), got = _attn_bwd(pq, pkv, sinks, lse, dy_attn,
                                      comm=net.exchange(dict(w_ffn_gate=[d_w_gate]), swap))
    net.received(names, swap, got)
    names, swap = ("w_ffn_up", "w_out", "w_branch_attn", "w_branch_hgrn"), ("w_ffn_gate",)
    (dph, d_hgrn_norm, d_lb_logits), got = _hgrn_bwd(
        ph, o_raw, states, dy_hgrn, lb_logits, hgrn_norm_g,
        comm=net.exchange(dict(w_ffn_up=[d_w_up], w_out=[d_w_out], w_branch_attn=[d_w_ba],
                               w_branch_hgrn=[d_w_bh]), swap))
    net.received(names, swap, got)

    d_w_in = [tn_grad("dw_in_q", [(dq, u)], tm=1024, a_colsum=True),
              tn_grad("dw_in_kv", [(dkv, u)], tm=256, a_colsum=True),
              tn_grad("dw_in_h", [(dph, u)], tm=1024),
              tn_grad("dw_in_ga", [(dga, u)], tm=1024, a_colsum=True),
              tn_grad("dw_in_gb", [(dgb, u)], tm=1024, a_colsum=True)]

    def norm_mix_bwd(acc, xin, g, dres):
        dx, dg = _rmsnorm_bwd_vals(acc, xin, g)
        return dres + dx, dg

    db_h, got = _colsum("db_h", dph, comm=net.presum_begin("w_in", d_w_in))
    halves = net.presum_end("w_in", got)
    names, swap = ("w_in",), ("w_ffn_up", "w_out", "w_branch_attn", "w_branch_hgrn")
    (dx, dg1_p), got = _matmul(
        "d_u", [(dq, w_q), (dkv, w_kv), (dph, w_h), (dga, w_g[:D]), (dgb, w_g[D:])], "nn",
        tm=BIG, tn=1024, tk=512, outs=[("mn", F32), ("pn", F32)],
        extras=[(x, *tile(BIG, 1024)), (norm_mix_g, *row_vec(1024)), (dh1, *tile(BIG, 1024))],
        epilogue=norm_mix_bwd,
        comm=_join(halves, net.swap(swap)))
    net.last = (names, swap, got)
    d_norm_mix = _colsum_partials(dg1_p)
    d_b_in = jnp.concatenate([d_w_in[0][2], d_w_in[1][2], db_h, d_w_in[3][2], d_w_in[4][2]], axis=1)
    vecs = dict(norm_mix_g=d_norm_mix, b_in=d_b_in, attn_sinks=jnp.sum(dsink.reshape(Q_HEADS, ATTN_BLOCK), axis=1).reshape(1, Q_HEADS),
                hgrn_lb_logits=d_lb_logits,
                hgrn_norm_g=d_hgrn_norm, norm_ffn_g=d_norm_ffn, norm_final_g=d_norm_final)
    return loss, dx, vecs


def _place():
    return lax.axis_index("x"), lax.axis_index("y"), lax.axis_index("c")


def _other_chips(x, y):
    return [(1 - x, y), (x, 1 - y), (1 - x, 1 - y)]


def _y_first(copies):
    return [copies[3 * (i // 3) + (1, 0, 2)[i % 3]] for i in range(len(copies))]


def _gather_copies(shards):
    n = len(shards)

    def build(ins, outs, send_sems, recv_sems, local_sems):
        x, y, c = _place()
        mine = 2 * x + y
        local = [pltpu.make_async_copy(ins[w], outs[w].at[mine], local_sems.at[w]) for w in range(n)]
        sends, recvs = [], []
        for w in range(n):
            for k, (px, py) in enumerate(_other_chips(x, y)):
                sem = 3 * w + k
                sends.append(pltpu.make_async_remote_copy(
                    src_ref=ins[w], dst_ref=outs[w].at[mine], send_sem=send_sems.at[sem], recv_sem=recv_sems.at[sem],
                    device_id=(px, py, c), device_id_type=MESH_ID))
                recvs.append(pltpu.make_async_remote_copy(
                    src_ref=ins[w], dst_ref=outs[w].at[2 * px + py], send_sem=send_sems.at[sem],
                    recv_sem=recv_sems.at[sem], device_id=(px, py, c), device_id_type=MESH_ID))
        return sends, recvs, local, _y_first(sends)

    return _Carried(shards, [jax.ShapeDtypeStruct((N_CHIPS,) + s.shape, s.dtype) for s in shards], 3 * n, n, build)


def _grad_copies(stacked):
    n = len(stacked)

    def build(ins, outs, send_sems, recv_sems, local_sems):
        x, y, c = _place()
        sends = []
        for w in range(n):
            for k, (px, py) in enumerate(_other_chips(x, y)):
                sem = 3 * w + k
                sends.append(pltpu.make_async_remote_copy(
                    src_ref=ins[w].at[2 * px + py], dst_ref=outs[w].at[k], send_sem=send_sems.at[sem],
                    recv_sem=recv_sems.at[sem], device_id=(px, py, c), device_id_type=MESH_ID))
        return sends, sends, [], _y_first(sends)

    return _Carried(stacked, [jax.ShapeDtypeStruct((3,) + s.shape[1:], s.dtype) for s in stacked], 3 * n, 0, build)


def _small_copies(small):
    def build(ins, outs, send_sems, recv_sems, local_sems):
        small_ref, all_ref = ins[0], outs[0]
        x, y, c = _place()
        me = 4 * x + 2 * y + c
        sends, recvs = [], []
        for r in range(1, 8):
            px = 1 - x if r & 4 else x
            py = 1 - y if r & 2 else y
            pc = 1 - c if r & 1 else c
            sends.append(pltpu.make_async_remote_copy(
                src_ref=small_ref, dst_ref=all_ref.at[me], send_sem=send_sems.at[r - 1], recv_sem=recv_sems.at[r - 1],
                device_id=(px, py, pc), device_id_type=MESH_ID))
            recvs.append(pltpu.make_async_remote_copy(
                src_ref=small_ref, dst_ref=all_ref.at[4 * px + 2 * py + pc], send_sem=send_sems.at[r - 1],
                recv_sem=recv_sems.at[r - 1], device_id=(px, py, pc), device_id_type=MESH_ID))
        return sends, recvs, [pltpu.make_async_copy(small_ref, all_ref.at[me], local_sems.at[0])]

    return _Carried([small], [jax.ShapeDtypeStruct((8,) + small.shape, small.dtype)], 7, 1, build)


def _gather_by_neighbours(name, shard):
    half = shard.shape[0] // 2
    quarter = half // 2

    def body(in_ref, out_ref, send_sems, recv_sems, local_sem):
        x, y, c = _place()
        chip = lambda px, py: 2 * px + py
        to_x, to_y, sibling = (1 - x, y, c), (x, 1 - y, c), (x, y, 1 - c)
        x_blk, y_blk, d_blk = chip(1 - x, y), chip(x, 1 - y), chip(1 - x, 1 - y)
        mine, theirs = c * half, (1 - c) * half

        def copy(sem, rows, block, to, src=None):
            place = out_ref.at[block, pl.ds(rows[0], rows[1])]
            return pltpu.make_async_remote_copy(
                src_ref=place if src is None else src, dst_ref=place, send_sem=send_sems.at[sem],
                recv_sem=recv_sems.at[sem], device_id=to, device_id_type=MESH_ID)

        own = pltpu.make_async_copy(in_ref, out_ref.at[chip(x, y)], local_sem)
        own.start()
        my_rows = in_ref.at[pl.ds(mine, half)]
        along_x = dict(send=copy(0, (mine, half), chip(x, y), to_x, src=my_rows),
                       landed=copy(0, (mine, half), x_blk, to_x),
                       onward=[copy(3, (mine + quarter, quarter), x_blk, to_y), copy(4, (mine, half), x_blk, sibling)],
                       diagonal=copy(2, (mine, quarter), d_blk, to_x))
        along_y = dict(send=copy(1, (mine, half), chip(x, y), to_y, src=my_rows),
                       landed=copy(1, (mine, half), y_blk, to_y),
                       onward=[copy(2, (mine, quarter), y_blk, to_x), copy(5, (mine, half), y_blk, sibling)],
                       diagonal=copy(3, (mine + quarter, quarter), d_blk, to_y))
        last = copy(6, (mine, half), d_blk, sibling)

        def run(order):
            for axis in order:
                axis["send"].start()
            for axis in order:
                axis["landed"].wait_recv()
                for cp in axis["onward"]:
                    cp.start()
            for axis in order:
                axis["diagonal"].wait_recv()
            last.start()

        @pl.when(c == 0)
        def _():
            run((along_x, along_y))

        @pl.when(c == 1)
        def _():
            run((along_y, along_x))

        for sem, block in ((4, x_blk), (5, y_blk), (6, d_blk)):
            copy(sem, (theirs, half), block, sibling).wait_recv()
        for cp in [along_x["send"], along_y["send"]] + along_x["onward"] + along_y["onward"] + [last]:
            cp.wait_send()
        own.wait()

    return pl.pallas_call(
        body, name=name, in_specs=[HBM_SPEC], out_specs=HBM_SPEC,
        out_shape=jax.ShapeDtypeStruct((N_CHIPS,) + shard.shape, shard.dtype),
        scratch_shapes=[pltpu.SemaphoreType.DMA((7,)), pltpu.SemaphoreType.DMA((7,)), pltpu.SemaphoreType.DMA(())],
    )(shard)


def _copies_alone(name, comm):
    return _call(name, lambda: None, grid=(), in_specs=[], out_specs=[], out_shape=[], args=[], comm=comm)[1]


class _Net:
    def __init__(self, shards):
        self.shards = shards
        self.whole, self.own, self.theirs, self.sums, self.other = {}, {}, {}, {}, {}
        x, y, _ = _place()
        self.chip = 2 * x + y

    def gather(self, names):
        return _gather_copies([self.shards[n] for n in names])

    def gathered(self, names, got):
        for n, g in zip(names, got):
            self.whole[n] = g.reshape(-1, g.shape[-1])

    def full(self, name):
        return self.whole[name]

    def exchange(self, grads, swap=()):
        stacked = []
        for n, pieces in grads.items():
            keep = jnp.concatenate([p[0] for p in pieces], axis=0) if len(pieces) > 1 else pieces[0][0]
            send = jnp.concatenate([p[1] for p in pieces], axis=0) if len(pieces) > 1 else pieces[0][1]
            rows = keep.shape[0] // N_CHIPS
            self.own[n] = lax.dynamic_slice_in_dim(keep, self.chip * rows, rows, axis=0)
            stacked.append(send.reshape(N_CHIPS, rows, send.shape[-1]))
        return _join(_grad_copies(stacked), self.swap(swap))

    def swap(self, names):
        return _sibling_copies([self.sums[n] for n in names]) if names else None

    def presum_begin(self, name, pieces):
        keep = jnp.concatenate([p[0] for p in pieces], axis=0)
        send = jnp.concatenate([p[1] for p in pieces], axis=0)
        rows = keep.shape[0] // N_CHIPS
        self.held = keep.reshape(N_CHIPS, rows, keep.shape[-1])
        return _half_rows_copies(send.reshape(N_CHIPS, rows, send.shape[-1]))

    def presum_end(self, name, got):
        x, y, c = _place()
        to_send, self.own[name] = _pre_sum("presum_" + name, self.held, got[0], jnp.stack([c, self.chip]))
        return _grad_copies([to_send])

    def received(self, names, swap, got, carried=None):
        self.theirs.update(zip(names, got[:len(names)]))
        self.other.update(zip(swap, got[len(names):]))
        for n in names:
            (self.sums[n],), more = _partial_sum("sum_" + n, self.own[n], self.theirs[n], comm=carried)
        return more


def _half_rows_copies(stacked):
    n, rows = stacked.shape[0], stacked.shape[1] // 2

    def build(ins, outs, send_sems, recv_sems, local_sems):
        x, y, c = _place()
        copies = [pltpu.make_async_remote_copy(
            src_ref=ins[0].at[s, pl.ds((1 - c) * rows, rows)], dst_ref=outs[0].at[s], send_sem=send_sems.at[s],
            recv_sem=recv_sems.at[s], device_id=(x, y, 1 - c), device_id_type=MESH_ID) for s in range(n)]
        return copies, copies, []

    return _Carried([stacked], [jax.ShapeDtypeStruct((n, rows, stacked.shape[2]), stacked.dtype)], n, 0, build)


def _pre_sum(name, held, theirs, core_and_chip):
    n, R, C = held.shape
    half = R // 2
    tr = _row_tile(half)
    per_half = half // tr

    def body(place_ref, h_ref, t_ref, send_ref, own_ref):
        total = h_ref[0] + t_ref[0].astype(F32)
        send_ref[0] = total.astype(send_ref.dtype)

        @pl.when(pl.program_id(1) == place_ref[1])
        def _():
            own_ref[...] = total

    return pl.pallas_call(
        body, name=name,
        grid_spec=pltpu.PrefetchScalarGridSpec(
            num_scalar_prefetch=1, grid=(per_half, n),
            in_specs=[pl.BlockSpec((1, tr, C), lambda i, s, place: (s, place[0] * per_half + i, 0)),
                      pl.BlockSpec((1, tr, C), lambda i, s, place: (s, i, 0))],
            out_specs=[pl.BlockSpec((1, tr, C), lambda i, s, place: (s, i, 0)),
                       pl.BlockSpec((tr, C), lambda i, s, place: (i, 0))]),
        out_shape=[jax.ShapeDtypeStruct((n, half, C), MXU_DTYPE), jax.ShapeDtypeStruct((half, C), F32)],
        compiler_params=_params(("arbitrary", "arbitrary")),
    )(core_and_chip, held, theirs)


def _sibling_copies(parts):
    n = len(parts)

    def build(ins, outs, send_sems, recv_sems, local_sems):
        x, y, c = _place()
        copies = [pltpu.make_async_remote_copy(
            src_ref=ins[w], dst_ref=outs[w], send_sem=send_sems.at[w], recv_sem=recv_sems.at[w],
            device_id=(x, y, 1 - c), device_id_type=MESH_ID) for w in range(n)]
        return copies, copies, []

    return _Carried(parts, [jax.ShapeDtypeStruct(p.shape, p.dtype) for p in parts], n, 0, build)


def _row_tile(rows, most=512, sublanes=16):
    return max(t for t in range(sublanes, most + 1, sublanes) if rows % t == 0)


def _partial_sum(name, own, recv, comm=None):
    R, C = own.shape
    tr = _row_tile(R, most=128)

    def body(o_ref, r_ref, p_ref):
        p_ref[...] = ((o_ref[...] + r_ref[0].astype(F32)) + r_ref[1].astype(F32)) + r_ref[2].astype(F32)

    return _call(name, body, grid=(R // tr,),
                 in_specs=[pl.BlockSpec((tr, C), lambda i: (i, 0)), pl.BlockSpec((3, tr, C), lambda i: (0, i, 0))],
                 out_specs=[pl.BlockSpec((tr, C), lambda i: (i, 0))], out_shape=[jax.ShapeDtypeStruct((R, C), F32)],
                 args=[own, recv], semantics=("parallel",), comm=comm)


def _adam_vals(w, g, m, v):
    m = ADAM_B1 * m + (1.0 - ADAM_B1) * g
    v = ADAM_B2 * v + (1.0 - ADAM_B2) * (g * g)
    m_hat = m / (1.0 - ADAM_B1 ** ADAM_STEP)
    v_hat = v / (1.0 - ADAM_B2 ** ADAM_STEP)
    delta = -ADAM_LR * (m_hat / (jnp.sqrt(v_hat) + ADAM_EPS) + ADAM_WD * w)
    return delta, m, v


def _adamw(name, w, m, v, mine, other, comm=None):
    R, C = w.shape
    tr = _row_tile(R)

    def body(w_ref, m_ref, v_ref, s_ref, n_ref, g_ref, d_ref, nm_ref, nv_ref):
        g = s_ref[...] + n_ref[...]
        d, nm, nv = _adam_vals(w_ref[...], g, m_ref[...], v_ref[...])
        g_ref[...], d_ref[...], nm_ref[...], nv_ref[...] = g, d, nm, nv

    spec = pl.BlockSpec((tr, C), lambda i: (i, 0))
    return _call(name, body, grid=(R // tr,), in_specs=[spec] * 5, out_specs=[spec] * 4,
                 out_shape=[jax.ShapeDtypeStruct((R, C), F32)] * 4, args=[w, m, v, mine, other],
                 semantics=("parallel",), comm=comm)


def _adamw_by_halves(name, w, m, v, mine, other, core):
    R, C = w.shape
    tr = _row_tile(R // 2)
    per_half = R // 2 // tr

    def body(c_ref, w_ref, m_ref, v_ref, s_ref, n_ref, g_ref, d_ref, nm_ref, nv_ref):
        g = jnp.where(pl.program_id(0) // per_half == c_ref[0, 0], s_ref[...], n_ref[...])
        d, nm, nv = _adam_vals(w_ref[...], g, m_ref[...], v_ref[...])
        g_ref[...], d_ref[...], nm_ref[...], nv_ref[...] = g, d, nm, nv

    spec = pl.BlockSpec((tr, C), lambda i: (i, 0))
    part = pl.BlockSpec((tr, C), lambda i: (i % per_half, 0))
    return pl.pallas_call(
        body, name=name, grid=(R // tr,),
        in_specs=[pl.BlockSpec(memory_space=pltpu.SMEM), spec, spec, spec, part, part], out_specs=[spec] * 4,
        out_shape=[jax.ShapeDtypeStruct((R, C), F32)] * 4, compiler_params=_params(("parallel",)),
    )(core, w, m, v, mine, other)


def _adamw_small(w, m, v, g_all):
    def body(w_ref, m_ref, v_ref, a_ref, g_ref, d_ref, nm_ref, nv_ref):
        g = a_ref[0]
        for dev in range(1, 8):
            g = g + a_ref[dev]
        d, nm, nv = _adam_vals(w_ref[...], g, m_ref[...], v_ref[...])
        g_ref[...], d_ref[...], nm_ref[...], nv_ref[...] = g, d, nm, nv

    return pl.pallas_call(
        body, name="adamw_small", out_shape=[jax.ShapeDtypeStruct(w.shape, F32)] * 4,
    )(w, m, v, g_all)


SMALL_LAYOUT = dict(norm_mix_g=(0, 1024), b_in=(1024, 7424), hgrn_norm_g=(8448, 1024), norm_ffn_g=(9472, 1024),
                    norm_final_g=(10496, 1024), hgrn_lb_logits=(11520, 2048), attn_sinks=(13568, 16))
SMALL_LOSS, SMALL_SIZE = 13584, 16 * D_MODEL


def _pack_small(vals, loss=None):
    names = list(SMALL_LAYOUT)
    tail = jnp.zeros((SMALL_SIZE - SMALL_LAYOUT[names[-1]][0],), F32)
    tail = lax.dynamic_update_slice(tail, vals[names[-1]].astype(F32).reshape(-1), (0,))
    if loss is not None:
        tail = lax.dynamic_update_slice(tail, loss.astype(F32).reshape(1), (SMALL_LOSS - SMALL_LAYOUT[names[-1]][0],))
    flat = jnp.concatenate([vals[n].astype(F32).reshape(-1) for n in names[:-1]] + [tail])
    return flat.reshape(-1, D_MODEL)


def _unpack_small(packed, shapes):
    flat = packed.reshape(-1)
    return {name: flat[lo:lo + n].reshape(shapes[name]) for name, (lo, n) in SMALL_LAYOUT.items()}


MATRICES = ("w_in", "w_branch_attn", "w_branch_hgrn", "w_out", "w_ffn_gate", "w_ffn_up", "w_ffn_down")
COLUMN_SHARDED = ("w_in", "w_ffn_gate", "w_ffn_up")
WEIGHTS = ("norm_mix_g", "w_in", "b_in", "attn_sinks", "hgrn_lb_logits", "hgrn_norm_g", "w_branch_attn",
           "w_branch_hgrn", "w_out", "norm_ffn_g", "w_ffn_gate", "w_ffn_up", "w_ffn_down", "norm_final_g")


def kernel(x, norm_mix_g, w_in, b_in, attn_sinks, hgrn_lb_logits, hgrn_norm_g, w_branch_attn, w_branch_hgrn, w_out, norm_ffn_g, w_ffn_gate, w_ffn_up, w_ffn_down, norm_final_g, loss_target, m_norm_mix_g, m_w_in, m_b_in, m_attn_sinks, m_hgrn_lb_logits, m_hgrn_norm_g, m_w_branch_attn, m_w_branch_hgrn, m_w_out, m_norm_ffn_g, m_w_ffn_gate, m_w_ffn_up, m_w_ffn_down, m_norm_final_g, v_norm_mix_g, v_w_in, v_b_in, v_attn_sinks, v_hgrn_lb_logits, v_hgrn_norm_g, v_w_branch_attn, v_w_branch_hgrn, v_w_out, v_norm_ffn_g, v_w_ffn_gate, v_w_ffn_up, v_w_ffn_down, v_norm_final_g):
    given = dict(locals())
    w = {n: given[n] for n in WEIGHTS}
    m = {n: given["m_" + n] for n in WEIGHTS}
    v = {n: given["v_" + n] for n in WEIGHTS}

    block = lambda a, n: jnp.transpose(a[0]) if n in COLUMN_SHARDED else a[0]
    unblock = lambda a, n: (jnp.transpose(a) if n in COLUMN_SHARDED else a)[None]
    net = _Net({n: block(w[n], n).astype(MXU_DTYPE) for n in MATRICES})
    net.gathered(("w_in",), [_gather_by_neighbours("gather_w_in", net.shards["w_in"])])
    vec = dict(norm_mix_g=norm_mix_g, b_in=b_in, attn_sinks=attn_sinks, hgrn_lb_logits=hgrn_lb_logits,
               hgrn_norm_g=hgrn_norm_g, norm_ffn_g=norm_ffn_g, norm_final_g=norm_final_g.reshape(1, D_MODEL))
    loss_part, dx, d_vecs = _local_step(x[0], loss_target[0], vec, net)

    small_all, = net.received(*net.last, carried=_small_copies(_pack_small(d_vecs, loss_part)))
    grads, deltas, new_m, new_v = {}, {}, {}, {}
    for n in ("w_ffn_down", "w_ffn_gate", "w_ffn_up", "w_out", "w_branch_attn", "w_branch_hgrn"):
        res, got = _adamw("adamw_" + n, block(w[n], n), block(m[n], n), block(v[n], n), net.sums[n], net.other[n],
                          comm=net.swap(("w_in",)) if n == "w_ffn_down" else None)
        if n == "w_ffn_down":
            net.other["w_in"], = got
        grads[n], deltas[n], new_m[n], new_v[n] = (unblock(r, n) for r in res)
    n = "w_in"
    res = _adamw_by_halves("adamw_" + n, block(w[n], n), block(m[n], n), block(v[n], n), net.sums[n], net.other[n],
                           _place()[2].reshape(1, 1))
    grads[n], deltas[n], new_m[n], new_v[n] = (unblock(r, n) for r in res)
    shapes = {n: w[n].shape for n in SMALL_LAYOUT}
    res = _adamw_small(_pack_small(w), _pack_small(m), _pack_small(v), small_all)
    for dst, packed in zip((grads, deltas, new_m, new_v), res):
        dst.update(_unpack_small(packed, shapes))

    loss = res[0].reshape(-1)[SMALL_LOSS]
    return (loss, dx[None], *[grads[n] for n in WEIGHTS], *[deltas[n] for n in WEIGHTS],
            *[new_m[n] for n in WEIGHTS], *[new_v[n] for n in WEIGHTS])
```

```python
import functools
import math

import jax
import jax.numpy as jnp
from jax import lax
from jax.experimental import pallas as pl
from jax.experimental.pallas import tpu as pltpu

F32 = jnp.float32
BF16 = jnp.bfloat16
MXU_DTYPE = jnp.bfloat16
MESH_ID = pl.DeviceIdType.MESH

D_MODEL = 1024
HEAD_DIM = 64
Q_HEADS = 16
KV_HEADS = 2
GROUP = Q_HEADS // KV_HEADS
KV_WIDTH = KV_HEADS * HEAD_DIM
ATTN_BLOCK = 128
HGRN_HEADS = 8
HGRN_K = 128
CHUNK = 64
HGRN_TOKENS = 256
FFN = 2816
IN_SPLITS = (1024, 256, 4096, 2048)
EPS = 1e-6
NEG_INF = -1e30
ADAM_LR, ADAM_B1, ADAM_B2, ADAM_EPS, ADAM_WD, ADAM_STEP = 0.001, 0.9, 0.999, 1e-08, 0.01, 10
N_CHIPS = 4
VMEM_LIMIT = 60 * 1024 * 1024


def _params(sem=None):
    return pltpu.CompilerParams(dimension_semantics=sem, vmem_limit_bytes=VMEM_LIMIT)


def _sigmoid(v):
    return 0.5 * jnp.tanh(0.5 * v) + 0.5


def _dot(a, b, dims):
    return lax.dot_general(a.astype(MXU_DTYPE), b.astype(MXU_DTYPE), (dims, ((), ())),
                           preferred_element_type=F32)


def _nn(a, b):
    return _dot(a, b, ((1,), (0,)))


def _nt(a, b):
    return _dot(a, b, ((1,), (1,)))


def _tn(a, b):
    return _dot(a, b, ((0,), (0,)))


HBM_SPEC = pl.BlockSpec(memory_space=pl.ANY)


class _Carried:
    def __init__(self, arrays, out_shapes, n_remote, n_local, build):
        self.parts = [(len(arrays), len(out_shapes), build)]
        self.arrays, self.out_shapes = list(arrays), list(out_shapes)
        self.scratch = [pltpu.SemaphoreType.DMA((n_remote,)), pltpu.SemaphoreType.DMA((n_remote,)),
                        pltpu.SemaphoreType.DMA((max(n_local, 1),))]

    def __add__(self, other):
        both = _Carried([], [], 1, 0, None)
        both.parts = self.parts + other.parts
        both.arrays, both.out_shapes = self.arrays + other.arrays, self.out_shapes + other.out_shapes
        both.scratch = self.scratch + other.scratch
        return both

    def _built(self, ins, outs, sems):
        for p, (ni, no, build) in enumerate(self.parts):
            yield build(ins[:ni], outs[:no], *sems[3 * p:3 * p + 3])
            ins, outs = ins[ni:], outs[no:]

    def start(self, ins, outs, sems):
        core = lax.axis_index("c")
        for sends, _, local, *other_order in self._built(ins, outs, sems):
            for cp in local:
                cp.start()
            if not other_order:
                for cp in sends:
                    cp.start()
                continue

            @pl.when(core == 0)
            def _():
                for cp in sends:
                    cp.start()

            @pl.when(core == 1)
            def _():
                for cp in other_order[0]:
                    cp.start()

    def wait(self, ins, outs, sems):
        for sends, recvs, local, *_ in self._built(ins, outs, sems):
            for cp in recvs:
                cp.wait_recv()
            for cp in sends:
                cp.wait_send()
            for cp in local:
                cp.wait()


def _join(*comms):
    comms = [c for c in comms if c is not None]
    return functools.reduce(lambda a, b: a + b, comms) if comms else None


def _call(name, body, *, grid, in_specs, out_specs, out_shape, args, scratch=(), semantics=None, comm=None):
    n_in, n_out, n_scr = len(in_specs), len(out_specs), len(scratch)
    if comm is None:
        res = pl.pallas_call(body, name=name, grid=grid, in_specs=in_specs, out_specs=out_specs, out_shape=out_shape,
                             scratch_shapes=list(scratch), compiler_params=_params(semantics))(*args)
        return list(res), []
    ci, co = len(comm.arrays), len(comm.out_shapes)

    def carrying(*refs):
        ins, refs = refs[:n_in], refs[n_in:]
        c_ins, refs = refs[:ci], refs[ci:]
        outs, refs = refs[:n_out], refs[n_out:]
        c_outs, refs = refs[:co], refs[co:]
        scr, sems = refs[:n_scr], refs[n_scr:]
        if not grid:
            comm.start(c_ins, c_outs, sems)
            body(*ins, *outs, *scr)
            comm.wait(c_ins, c_outs, sems)
            return
        first = functools.reduce(jnp.logical_and, [pl.program_id(a) == 0 for a in range(len(grid))])
        last = functools.reduce(jnp.logical_and, [pl.program_id(a) == g - 1 for a, g in enumerate(grid)])

        @pl.when(first)
        def _():
            comm.start(c_ins, c_outs, sems)

        body(*ins, *outs, *scr)

        @pl.when(last)
        def _():
            comm.wait(c_ins, c_outs, sems)

    res = pl.pallas_call(
        carrying, name=name, grid=grid, in_specs=list(in_specs) + [HBM_SPEC] * ci,
        out_specs=list(out_specs) + [HBM_SPEC] * co, out_shape=list(out_shape) + comm.out_shapes,
        scratch_shapes=list(scratch) + comm.scratch,
        compiler_params=_params(("arbitrary",) * len(grid) if grid else None),
    )(*args, *comm.arrays)
    return list(res[:n_out]), list(res[n_out:])


_NO_COPIES = object()


def _matmul(name, pairs, mode, *, tm, tn, tk, outs, extras=(), epilogue=None, a_colsum=False, comm=_NO_COPIES):
    prod = dict(nn=_nn, nt=_nt, tn=_tn)[mode]
    a0, b0 = pairs[0]
    M = a0.shape[1] if mode == "tn" else a0.shape[0]
    N = b0.shape[0] if mode == "nt" else b0.shape[1]
    steps, in_specs, offset = [], [], 0
    for a, b in pairs:
        K = a.shape[0] if mode == "tn" else a.shape[1]
        t = min(tk, K)
        assert K % t == 0, (name, K, t)
        kmap = functools.partial(lambda k, off, n: jnp.clip(k - off, 0, n - 1), off=offset, n=K // t)
        if mode == "tn":
            in_specs.append(pl.BlockSpec((t, tm), functools.partial(lambda i, j, k, f: (f(k), i), f=kmap)))
        else:
            in_specs.append(pl.BlockSpec((tm, t), functools.partial(lambda i, j, k, f: (i, f(k)), f=kmap)))
        if mode == "nt":
            in_specs.append(pl.BlockSpec((tn, t), functools.partial(lambda i, j, k, f: (j, f(k)), f=kmap)))
        else:
            in_specs.append(pl.BlockSpec((t, tn), functools.partial(lambda i, j, k, f: (f(k), j), f=kmap)))
        steps.append((offset, offset + K // t))
        offset += K // t
    assert M % tm == 0 and N % tn == 0, (name, M, N, tm, tn)
    ni, nj, nk = M // tm, N // tn, offset
    npair, ne, no = len(pairs), len(extras), len(outs)
    if epilogue is None:
        epilogue = lambda acc: (acc,)

    def finish(acc, extra_refs, out_refs):
        vals = epilogue(acc, *[r[...] for r in extra_refs])
        for (kind, _), o_ref, val in zip(outs, out_refs, vals):
            if kind == "pn":
                val = jnp.broadcast_to(val, o_ref.shape)
            o_ref[...] = val.astype(o_ref.dtype)

    def body(*refs):
        ab, rest = refs[:2 * npair], refs[2 * npair:]
        extra_refs, out_refs = rest[:ne], rest[ne:ne + no]
        if nk == 1:
            finish(prod(ab[0][...], ab[1][...]), extra_refs, out_refs)
            return
        acc_ref = rest[-1]
        k = pl.program_id(2)

        @pl.when(k == 0)
        def _():
            acc_ref[...] = jnp.zeros_like(acc_ref)
            if a_colsum:
                rest[ne + no][...] = jnp.zeros((1, tm), F32)

        if a_colsum:
            rest[ne + no][...] += jnp.sum(ab[0][...].astype(F32), axis=0, keepdims=True)
        for p, (lo, hi) in enumerate(steps):
            @pl.when(jnp.logical_and(k >= lo, k < hi))
            def _():
                acc_ref[...] += prod(ab[2 * p][...], ab[2 * p + 1][...])

        @pl.when(k == nk - 1)
        def _():
            finish(acc_ref[...], extra_refs, out_refs)

    for _, shape, im in extras:
        in_specs.append(pl.BlockSpec(shape, functools.partial(lambda i, j, k, im: im(i, j), im=im)))
    out_shape, out_specs = [], []
    for kind, dt in outs:
        if kind == "mn":
            out_shape.append(jax.ShapeDtypeStruct((M, N), dt))
            out_specs.append(pl.BlockSpec((tm, tn), lambda i, j, k: (i, j)))
        else:
            out_shape.append(jax.ShapeDtypeStruct((8 * ni, N), dt))
            out_specs.append(pl.BlockSpec((8, tn), lambda i, j, k: (i, j)))
    if a_colsum:
        assert mode == "tn" and npair == 1 and nj == 1 and nk > 1, name
        out_shape.append(jax.ShapeDtypeStruct((1, M), F32))
        out_specs.append(pl.BlockSpec((1, tm), lambda i, j, k: (0, i)))
    grid = (ni, nj, nk)
    if nj > 1 and nk == 1:
        turned = lambda spec: pl.BlockSpec(spec.block_shape, functools.partial(
            lambda j, i, k, im: im(i, j, k), im=spec.index_map))
        in_specs, out_specs, grid = [turned(s) for s in in_specs], [turned(s) for s in out_specs], (nj, ni, nk)
    res, got = _call(name, body, grid=grid, in_specs=in_specs, out_specs=out_specs, out_shape=out_shape,
                     args=[t for pair in pairs for t in pair] + [e[0] for e in extras],
                     scratch=[pltpu.VMEM((tm, tn), F32)] if nk > 1 else [],
                     semantics=("parallel", "parallel", "arbitrary"), comm=None if comm is _NO_COPIES else comm)
    return res if comm is _NO_COPIES else (res, got)


def _swiglu_fwd(u, w_gate_t, w_up_t, *, tm, tn, comm=None):
    (T, D), F = u.shape, w_gate_t.shape[0]

    def body(u_ref, wg_ref, wu_ref, g_ref, up_ref, z_ref):
        g, up = _nt(u_ref[...], wg_ref[...]), _nt(u_ref[...], wu_ref[...])
        g_ref[...], up_ref[...] = g, up
        z_ref[...] = (g * _sigmoid(g) * up).astype(z_ref.dtype)

    w_spec = pl.BlockSpec((tn, D), lambda j, i: (j, 0))
    o_spec = pl.BlockSpec((tm, tn), lambda j, i: (i, j))
    return _call("ffn_hidden", body, grid=(F // tn, T // tm),
                 in_specs=[pl.BlockSpec((tm, D), lambda j, i: (i, 0)), w_spec, w_spec], out_specs=[o_spec] * 3,
                 out_shape=[jax.ShapeDtypeStruct((T, F), F32)] * 2 + [jax.ShapeDtypeStruct((T, F), MXU_DTYPE)],
                 args=[u, w_gate_t, w_up_t], semantics=("parallel", "parallel"), comm=comm)


def _colsum_partials(p):
    return jnp.sum(p.reshape(-1, 8, p.shape[-1])[:, 0, :], axis=0, keepdims=True)


def _rmsnorm_fwd(name, x, g, tr=512):
    T, D = x.shape

    def body(x_ref, g_ref, u_ref):
        xv = x_ref[...]
        r = lax.rsqrt(jnp.mean(xv * xv, axis=-1, keepdims=True) + EPS)
        u_ref[...] = (xv * r * g_ref[...]).astype(u_ref.dtype)

    return pl.pallas_call(
        body, name=name, grid=(T // tr,),
        in_specs=[pl.BlockSpec((tr, D), lambda i: (i, 0)), pl.BlockSpec((1, D), lambda i: (0, 0))],
        out_specs=pl.BlockSpec((tr, D), lambda i: (i, 0)),
        out_shape=jax.ShapeDtypeStruct((T, D), MXU_DTYPE),
        compiler_params=_params(("parallel",)),
    )(x, g)


def _rmsnorm_bwd_vals(dy, xin, g):
    rstd = lax.rsqrt(jnp.mean(xin * xin, axis=-1, keepdims=True) + EPS)
    xhat = xin * rstd
    dg = jnp.sum(dy * xhat, axis=0, keepdims=True)
    dxh = dy * g
    dx = rstd * (dxh - xhat * jnp.mean(dxh * xhat, axis=-1, keepdims=True))
    return dx, dg


def _colsum(name, a, tr=512, comm=_NO_COPIES):
    T, N = a.shape

    def body(a_ref, o_ref):
        @pl.when(pl.program_id(0) == 0)
        def _():
            o_ref[...] = jnp.zeros_like(o_ref)

        o_ref[...] += jnp.sum(a_ref[...].astype(F32), axis=0, keepdims=True)

    (res,), got = _call(name, body, grid=(T // tr,), in_specs=[pl.BlockSpec((tr, N), lambda i: (i, 0))],
                        out_specs=[pl.BlockSpec((1, N), lambda i: (0, 0))],
                        out_shape=[jax.ShapeDtypeStruct((1, N), F32)], args=[a], semantics=("arbitrary",),
                        comm=None if comm is _NO_COPIES else comm)
    return res if comm is _NO_COPIES else (res, got)


ATTN_SCALE = 1.0 / math.sqrt(HEAD_DIM)
GROUP_LANES = GROUP * ATTN_BLOCK
PAIR = 2 * HEAD_DIM


def _attn_mask():
    kj = lax.broadcasted_iota(jnp.int32, (ATTN_BLOCK, GROUP_LANES), 0)
    qi = lax.broadcasted_iota(jnp.int32, (ATTN_BLOCK, GROUP_LANES), 1) & (ATTN_BLOCK - 1)
    return kj <= qi


def _heads_transposed(ref, g, scale=None):
    parts = []
    for a in range(GROUP // 2):
        lo = (g * GROUP // 2 + a) * PAIR
        pair = ref[:, lo:lo + PAIR].astype(F32)
        pair = (pair if scale is None else pair * scale).T
        parts += [pair[:HEAD_DIM], pair[HEAD_DIM:]]
    return jnp.concatenate(parts, axis=1).astype(MXU_DTYPE)


def _heads_back(ref, g, vt):
    for a in range(GROUP // 2):
        lo = (g * GROUP // 2 + a) * PAIR
        pair = jnp.concatenate([vt[:, (2 * a) * ATTN_BLOCK:(2 * a + 1) * ATTN_BLOCK],
                                vt[:, (2 * a + 1) * ATTN_BLOCK:(2 * a + 2) * ATTN_BLOCK]], axis=0)
        ref[:, lo:lo + PAIR] = pair.T.astype(ref.dtype)


def _kv_parts(kv_ref, g):
    ks = slice(g * HEAD_DIM, (g + 1) * HEAD_DIM)
    vs = slice(KV_WIDTH + g * HEAD_DIM, KV_WIDTH + (g + 1) * HEAD_DIM)
    return kv_ref[:, ks].astype(MXU_DTYPE), kv_ref[:, vs].astype(MXU_DTYPE)


def _sink_rows(sinks):
    return jnp.repeat(sinks.reshape(KV_HEADS, GROUP), ATTN_BLOCK, axis=1)


def _attn_fwd(pq, pkv, sinks, comm=None):
    T = pq.shape[0]
    nb = T // ATTN_BLOCK

    def body(q_ref, kvc_ref, kvp_ref, s_ref, y_ref, lse_ref):
        mask_c = _attn_mask()
        has_prev = pl.program_id(0) > 0
        for g in range(KV_HEADS):
            (kc, vc), (kp, vp) = _kv_parts(kvc_ref, g), _kv_parts(kvp_ref, g)
            qt = _heads_transposed(q_ref, g, ATTN_SCALE)
            s = jnp.where(mask_c, _nn(kc, qt), jnp.where(has_prev, _nn(kp, qt), NEG_INF))
            sink = s_ref[g:g + 1, :]
            m = jnp.maximum(jnp.max(s, axis=0, keepdims=True), sink)
            p = jnp.exp(s - m)
            den = jnp.sum(p, axis=0, keepdims=True) + jnp.exp(sink - m)
            pc = jnp.where(mask_c, p, 0.0)
            _heads_back(y_ref, g, (_tn(vc, pc) + _tn(vp, p - pc)) / den)
            lse = m + jnp.log(den)
            for i in range(GROUP):
                lse_ref[g * GROUP + i:g * GROUP + i + 1, :] = lse[:, i * ATTN_BLOCK:(i + 1) * ATTN_BLOCK]

    return _call(
        "attn_fwd", body, grid=(nb,),
        in_specs=[pl.BlockSpec((ATTN_BLOCK, D_MODEL), lambda n: (n, 0)),
                  pl.BlockSpec((ATTN_BLOCK, 2 * KV_WIDTH), lambda n: (n, 0)),
                  pl.BlockSpec((ATTN_BLOCK, 2 * KV_WIDTH), lambda n: (jnp.maximum(n - 1, 0), 0)),
                  pl.BlockSpec((KV_HEADS, GROUP_LANES), lambda n: (0, 0))],
        out_specs=[pl.BlockSpec((ATTN_BLOCK, D_MODEL), lambda n: (n, 0)),
                   pl.BlockSpec((Q_HEADS, ATTN_BLOCK), lambda n: (0, n))],
        out_shape=[jax.ShapeDtypeStruct((T, D_MODEL), MXU_DTYPE), jax.ShapeDtypeStruct((Q_HEADS, T), F32)],
        args=[pq, pkv, pkv, _sink_rows(sinks)], semantics=("parallel",), comm=comm)


def _attn_bwd(pq, pkv, sinks, lse, dy, comm=None):
    T = pq.shape[0]
    nb = T // ATTN_BLOCK
    cur = lambda n: (jnp.minimum(n, nb - 1), 0)

    def body(q_ref, kvc_ref, kvp_ref, s_ref, lse_ref, dy_ref, dq_ref, dkv_ref, ds_ref, carry, top, bot):
        n = pl.program_id(0)

        @pl.when(n == 0)
        def _():
            carry[...] = jnp.zeros_like(carry)
            ds_ref[...] = jnp.zeros_like(ds_ref)

        @pl.when(n < nb)
        def _():
            mask_c = _attn_mask()
            valid = jnp.logical_or(mask_c, n > 0)
            for g in range(KV_HEADS):
                ks = slice(g * HEAD_DIM, (g + 1) * HEAD_DIM)
                vs = slice(KV_WIDTH + g * HEAD_DIM, KV_WIDTH + (g + 1) * HEAD_DIM)
                (kc, vc), (kp, vp) = _kv_parts(kvc_ref, g), _kv_parts(kvp_ref, g)
                qt = _heads_transposed(q_ref, g, ATTN_SCALE)
                dot = _heads_transposed(dy_ref, g)
                lse = jnp.concatenate([lse_ref[g * GROUP + i:g * GROUP + i + 1, :] for i in range(GROUP)], axis=1)
                p = jnp.where(valid, jnp.exp(jnp.where(mask_c, _nn(kc, qt), _nn(kp, qt)) - lse), 0.0)
                dp = jnp.where(mask_c, _nn(vc, dot), _nn(vp, dot))
                delta = jnp.sum(p * dp, axis=0, keepdims=True)
                ds = p * (dp - delta)
                ds_c, p_c = jnp.where(mask_c, ds, 0.0), jnp.where(mask_c, p, 0.0)
                ds_p, p_p = ds - ds_c, p - p_c
                _heads_back(dq_ref, g, (_tn(kc, ds_c) + _tn(kp, ds_p)) * ATTN_SCALE)
                bot[:, ks], bot[:, vs] = _nt(ds_c, qt), _nt(p_c, dot)
                top[:, ks], top[:, vs] = _nt(ds_p, qt), _nt(p_p, dot)
                ds_ref[g:g + 1, :] -= jnp.exp(s_ref[g:g + 1, :] - lse) * delta
            dkv_ref[...] = (carry[...] + top[...]).astype(dkv_ref.dtype)
            carry[...] = bot[...]

        @pl.when(n == nb)
        def _():
            dkv_ref[...] = carry[...].astype(dkv_ref.dtype)

    return _call(
        "attn_bwd", body, grid=(nb + 1,),
        in_specs=[pl.BlockSpec((ATTN_BLOCK, D_MODEL), cur),
                  pl.BlockSpec((ATTN_BLOCK, 2 * KV_WIDTH), cur),
                  pl.BlockSpec((ATTN_BLOCK, 2 * KV_WIDTH), lambda n: (jnp.maximum(jnp.minimum(n, nb - 1) - 1, 0), 0)),
                  pl.BlockSpec((KV_HEADS, GROUP_LANES), lambda n: (0, 0)),
                  pl.BlockSpec((Q_HEADS, ATTN_BLOCK), lambda n: (0, jnp.minimum(n, nb - 1))),
                  pl.BlockSpec((ATTN_BLOCK, D_MODEL), cur)],
        out_specs=[pl.BlockSpec((ATTN_BLOCK, D_MODEL), cur),
                   pl.BlockSpec((ATTN_BLOCK, 2 * KV_WIDTH), lambda n: (jnp.maximum(n - 1, 0), 0)),
                   pl.BlockSpec((KV_HEADS, GROUP_LANES), lambda n: (0, 0))],
        out_shape=[jax.ShapeDtypeStruct((T, D_MODEL), MXU_DTYPE),
                   jax.ShapeDtypeStruct((T, 2 * KV_WIDTH), MXU_DTYPE),
                   jax.ShapeDtypeStruct((KV_HEADS, GROUP_LANES), F32)],
        scratch=[pltpu.VMEM((ATTN_BLOCK, 2 * KV_WIDTH), F32)] * 3,
        args=[pq, pkv, pkv, _sink_rows(sinks), lse, dy], semantics=("arbitrary",), comm=comm)


def _lower_bound(l):
    m = jnp.maximum(l[0:1], l[1:2])
    e0, e1 = jnp.exp(l[0:1] - m), jnp.exp(l[1:2] - m)
    return e0 / (e0 + e1)


def _tri(lower):
    r = lax.broadcasted_iota(jnp.int32, (CHUNK, CHUNK), 0)
    c = lax.broadcasted_iota(jnp.int32, (CHUNK, CHUNK), 1)
    return (r >= c) if lower else (c >= r)


def _chunk_sum(mask, v):
    ones = mask.astype(BF16)
    hi = v.astype(BF16)
    rest = v - hi.astype(F32)
    mid = rest.astype(BF16)
    lo = (rest - mid.astype(F32)).astype(BF16)
    part = lambda t: lax.dot_general(ones, t, (((1,), (0,)), ((), ())), preferred_element_type=F32)
    return part(hi) + part(mid) + part(lo)


def _hgrn_chunk_inputs(hq, hf, lb, causal):
    sg, sgn = _sigmoid(hf), _sigmoid(-hf)
    f = lb + (1.0 - lb) * sg
    kk = (1.0 - lb) * sgn
    sq = _sigmoid(hq)
    q = hq * sq
    b = _chunk_sum(causal, jnp.log(f))
    bm, bl = b[CHUNK // 2 - 1:CHUNK // 2, :], b[CHUNK - 1:CHUNK, :]
    e_qm, e_km, e_qs, e_kl = jnp.exp(b - bm), jnp.exp(bm - b), jnp.exp(b), jnp.exp(bl - b)
    return dict(sg=sg, sgn=sgn, f=f, kk=kk, sq=sq, q=q, e_qm=e_qm, e_km=e_km, e_qs=e_qs, e_kl=e_kl,
                qm=q * e_qm, km=kk * e_km, qs=q * e_qs, kl=kk * e_kl, el=jnp.exp(bl))


def _hgrn_fwd(ph, lb_logits, norm_g, comm=None):
    T = ph.shape[0]
    nblk, cpb = T // HGRN_TOKENS, HGRN_TOKENS // CHUNK
    col = lambda c: pl.BlockSpec((HGRN_TOKENS, D_MODEL), functools.partial(lambda i, c: (i, c), c=c))

    def body(hq_ref, hf_ref, hi_ref, hg_ref, l_ref, ng_ref, y_ref, o_ref, st_ref, s_ref):
        @pl.when(pl.program_id(0) == 0)
        def _():
            s_ref[...] = jnp.zeros_like(s_ref)

        lb = _lower_bound(l_ref[...])
        causal = _tri(True)
        for c in range(cpb):
            rows = slice(c * CHUNK, (c + 1) * CHUNK)
            t = _hgrn_chunk_inputs(hq_ref[rows, :], hf_ref[rows, :], lb, causal)
            qm, km, qs, kl = (t[n].astype(MXU_DTYPE) for n in ("qm", "km", "qs", "kl"))
            v = hi_ref[rows, :].astype(MXU_DTYPE)
            for h in range(HGRN_HEADS):
                ls = slice(h * HGRN_K, (h + 1) * HGRN_K)
                st = s_ref[h]
                st_ref[c, ls, :] = st
                a = jnp.where(causal, _nt(qm[:, ls], km[:, ls]), 0.0)
                o_ref[rows, ls] = _nn(a, v[:, ls]) + _nt(qs[:, ls], st)
                s_ref[h] = t["el"][:, ls] * st + _tn(v[:, ls], kl[:, ls])
        for h in range(HGRN_HEADS):
            ls = slice(h * HGRN_K, (h + 1) * HGRN_K)
            o = o_ref[:, ls]
            r = lax.rsqrt(jnp.mean(o * o, axis=-1, keepdims=True) + EPS)
            y_ref[:, ls] = (o * r * ng_ref[:, ls] * _sigmoid(hg_ref[:, ls])).astype(y_ref.dtype)

    return _call(
        "hgrn_fwd", body, grid=(nblk,),
        in_specs=[col(0), col(1), col(2), col(3),
                  pl.BlockSpec((2, D_MODEL), lambda i: (0, 0)), pl.BlockSpec((1, D_MODEL), lambda i: (0, 0))],
        out_specs=[pl.BlockSpec((HGRN_TOKENS, D_MODEL), lambda i: (i, 0)),
                   pl.BlockSpec((HGRN_TOKENS, D_MODEL), lambda i: (i, 0)),
                   pl.BlockSpec((cpb, D_MODEL, HGRN_K), lambda i: (i, 0, 0))],
        out_shape=[jax.ShapeDtypeStruct((T, D_MODEL), MXU_DTYPE), jax.ShapeDtypeStruct((T, D_MODEL), F32),
                   jax.ShapeDtypeStruct((T // CHUNK, D_MODEL, HGRN_K), F32)],
        scratch=[pltpu.VMEM((HGRN_HEADS, HGRN_K, HGRN_K), F32)],
        args=[ph, ph, ph, ph, lb_logits, norm_g], semantics=("arbitrary",), comm=comm)


def _hgrn_bwd(ph, o_raw, states, dy, lb_logits, norm_g, comm=None):
    T = ph.shape[0]
    nblk, cpb = T // HGRN_TOKENS, HGRN_TOKENS // CHUNK
    rev = lambda i: nblk - 1 - i
    col = lambda c: pl.BlockSpec((HGRN_TOKENS, D_MODEL), functools.partial(lambda i, c: (rev(i), c), c=c))
    tok = pl.BlockSpec((HGRN_TOKENS, D_MODEL), lambda i: (rev(i), 0))

    def body(hq_ref, hf_ref, hi_ref, hg_ref, o_ref, st_ref, dy_ref, l_ref, ng_ref,
             dph_ref, dng_ref, dl_ref, dst_ref, dlb_ref, do_s, dqm_s, dkm_s, dqs_s, dkl_s, dv_s, del_s):
        i = pl.program_id(0)

        @pl.when(i == 0)
        def _():
            dst_ref[...] = jnp.zeros_like(dst_ref)
            dlb_ref[...] = jnp.zeros_like(dlb_ref)
            dng_ref[...] = jnp.zeros_like(dng_ref)

        lb = _lower_bound(l_ref[...])
        causal, anti = _tri(True), _tri(False)
        row = lax.broadcasted_iota(jnp.int32, (CHUNK, D_MODEL), 0)
        for c in reversed(range(cpb)):
            rows = slice(c * CHUNK, (c + 1) * CHUNK)
            hq = hq_ref[rows, :]
            t = _hgrn_chunk_inputs(hq, hf_ref[rows, :], lb, causal)
            sgg = _sigmoid(hg_ref[rows, :])
            dyv = dy_ref[rows, :]
            for h in range(HGRN_HEADS):
                ls = slice(h * HGRN_K, (h + 1) * HGRN_K)
                o = o_ref[rows, ls]
                r = lax.rsqrt(jnp.mean(o * o, axis=-1, keepdims=True) + EPS)
                nrm = o * r
                g_h = sgg[:, ls]
                dph_ref[rows, 3 * D_MODEL + h * HGRN_K:3 * D_MODEL + (h + 1) * HGRN_K] = (
                    dyv[:, ls] * nrm * ng_ref[:, ls] * g_h * (1.0 - g_h)).astype(dph_ref.dtype)
                dyg = dyv[:, ls] * g_h
                dng_ref[:, ls] += jnp.sum(dyg * nrm, axis=0, keepdims=True)
                dn = dyg * ng_ref[:, ls]
                do_s[:, ls] = r * (dn - nrm * jnp.mean(dn * nrm, axis=-1, keepdims=True))
            qm, km, qs, kl = (t[n].astype(MXU_DTYPE) for n in ("qm", "km", "qs", "kl"))
            v = hi_ref[rows, :].astype(MXU_DTYPE)
            do = do_s[...].astype(MXU_DTYPE)
            for h in range(HGRN_HEADS):
                ls = slice(h * HGRN_K, (h + 1) * HGRN_K)
                st = st_ref[c, ls, :]
                dst = dst_ref[h]
                a = jnp.where(causal, _nt(qm[:, ls], km[:, ls]), 0.0)
                da = jnp.where(causal, _nt(do[:, ls], v[:, ls]), 0.0)
                dv_s[:, ls] = _tn(a, do[:, ls]) + _nt(kl[:, ls], dst)
                dkl_s[:, ls] = _nn(v[:, ls], dst)
                dqs_s[:, ls] = _nn(do[:, ls], st)
                del_s[:, ls] = jnp.sum(dst * st, axis=0, keepdims=True)
                dst_ref[h] = _tn(do[:, ls], qs[:, ls]) + t["el"][:, ls] * dst
                dqm_s[:, ls] = _nn(da, km[:, ls])
                dkm_s[:, ls] = _tn(da, qm[:, ls])
            dqm, dkm, dqs, dkl = dqm_s[...], dkm_s[...], dqs_s[...], dkl_s[...]
            dq = dqm * t["e_qm"] + dqs * t["e_qs"]
            dk = dkm * t["e_km"] + dkl * t["e_kl"]
            t_qm, t_km, t_kl = dqm * t["qm"], dkm * t["km"], dkl * t["kl"]
            db = t_qm - t_km + dqs * t["qs"] - t_kl
            db_mid = jnp.sum(t_km - t_qm, axis=0, keepdims=True)
            db_last = jnp.sum(t_kl, axis=0, keepdims=True) + del_s[...] * t["el"]
            db = db + jnp.where(row == CHUNK // 2 - 1, db_mid, 0.0) + jnp.where(row == CHUNK - 1, db_last, 0.0)
            dlogf = _chunk_sum(anti, db)
            sq, sg, sgn, f = t["sq"], t["sg"], t["sgn"], t["f"]
            dph_ref[rows, 0:D_MODEL] = (dq * (sq * (1.0 + hq * (1.0 - sq)))).astype(dph_ref.dtype)
            dph_ref[rows, D_MODEL:2 * D_MODEL] = (
                dlogf * (1.0 - lb) * sg * (1.0 - sg) / f - dk * (1.0 - lb) * sgn * (1.0 - sgn)).astype(dph_ref.dtype)
            dph_ref[rows, 2 * D_MODEL:3 * D_MODEL] = dv_s[...].astype(dph_ref.dtype)
            dlb_ref[...] += jnp.sum(dlogf * (1.0 - sg) / f - dk * sgn, axis=0, keepdims=True)

        @pl.when(i == nblk - 1)
        def _():
            dl0 = dlb_ref[...] * lb * (1.0 - lb)
            dl_ref[0:1, :] = dl0
            dl_ref[1:2, :] = -dl0

    wide = pltpu.VMEM((CHUNK, D_MODEL), F32)
    return _call(
        "hgrn_bwd", body, grid=(nblk,),
        in_specs=[col(0), col(1), col(2), col(3), tok,
                  pl.BlockSpec((cpb, D_MODEL, HGRN_K), lambda i: (rev(i), 0, 0)), tok,
                  pl.BlockSpec((2, D_MODEL), lambda i: (0, 0)), pl.BlockSpec((1, D_MODEL), lambda i: (0, 0))],
        out_specs=[pl.BlockSpec((HGRN_TOKENS, 4 * D_MODEL), lambda i: (rev(i), 0)),
                   pl.BlockSpec((1, D_MODEL), lambda i: (0, 0)), pl.BlockSpec((2, D_MODEL), lambda i: (0, 0))],
        out_shape=[jax.ShapeDtypeStruct((T, 4 * D_MODEL), MXU_DTYPE), jax.ShapeDtypeStruct((1, D_MODEL), F32),
                   jax.ShapeDtypeStruct((2, D_MODEL), F32)],
        scratch=[pltpu.VMEM((HGRN_HEADS, HGRN_K, HGRN_K), F32), pltpu.VMEM((1, D_MODEL), F32),
                 wide, wide, wide, wide, wide, wide, pltpu.VMEM((1, D_MODEL), F32)],
        args=[ph, ph, ph, ph, o_raw, states, dy, lb_logits, norm_g], semantics=("arbitrary",), comm=comm)


def _local_step(x, target, vec, net):
    T, D = x.shape
    norm_mix_g, b_in, sinks, lb_logits = vec["norm_mix_g"], vec["b_in"], vec["attn_sinks"], vec["hgrn_lb_logits"]
    hgrn_norm_g, norm_ffn_g, norm_final_g = vec["hgrn_norm_g"], vec["norm_ffn_g"], vec["norm_final_g"]
    w_in = net.full("w_in")
    o_q, o_kv, o_h, o_g = (sum(IN_SPLITS[:i]) for i in range(4))
    w_q, w_kv, w_h, w_g = (w_in[o:o + n] for o, n in zip((o_q, o_kv, o_h, o_g), IN_SPLITS))
    b_q, b_kv, b_h, b_g = (b_in[:, o:o + n] for o, n in zip((o_q, o_kv, o_h, o_g), IN_SPLITS))
    bias = lambda acc, b: (acc + b,)
    both = lambda acc: (acc, acc)
    grad_outs = [("mn", F32), ("mn", MXU_DTYPE)]
    row_vec = lambda n: ((1, n), lambda i, j: (0, j))
    tile = lambda tm, tn: ((tm, tn), lambda i, j: (i, j))
    TM = 512
    BIG = min(T, 1024)

    u = _rmsnorm_fwd("norm_mix", x, norm_mix_g)
    pq, = _matmul("in_q", [(u, w_q)], "nt", tm=TM, tn=1024, tk=1024, outs=[("mn", MXU_DTYPE)],
                  extras=[(b_q, *row_vec(1024))], epilogue=bias)
    pkv, = _matmul("in_kv", [(u, w_kv)], "nt", tm=TM, tn=256, tk=1024, outs=[("mn", MXU_DTYPE)],
                   extras=[(b_kv, *row_vec(256))], epilogue=bias)
    names = ("w_branch_attn", "w_branch_hgrn", "w_out")
    (ph,), got = _matmul("in_h", [(u, w_h)], "nt", tm=TM, tn=1024, tk=1024, outs=[("mn", F32)],
                         extras=[(b_h, *row_vec(1024))], epilogue=bias, comm=net.gather(names))
    net.gathered(names, got)
    pg, = _matmul("in_g", [(u, w_g)], "nt", tm=TM, tn=1024, tk=1024, outs=[("mn", F32)],
                  extras=[(b_g, *row_vec(1024))], epilogue=bias)
    names = ("w_ffn_gate",)
    (y_attn, lse), got = _attn_fwd(pq, pkv, sinks, comm=net.gather(names))
    net.gathered(names, got)
    names = ("w_ffn_up",)
    (y_hgrn, o_raw, states), got = _hgrn_fwd(ph, lb_logits, hgrn_norm_g, comm=net.gather(names))
    net.gathered(names, got)
    w_ba, w_bh, w_out = net.full("w_branch_attn"), net.full("w_branch_hgrn"), net.full("w_out")
    w_gate, w_up = net.full("w_ffn_gate"), net.full("w_ffn_up")
    ya, = _matmul("branch_a", [(y_attn, w_ba)], "nn", tm=TM, tn=1024, tk=1024, outs=[("mn", F32)])
    gate_a = (pg, (TM, 1024), lambda i, j: (i, 0))
    gate_b = (pg, (TM, 1024), lambda i, j: (i, 1))

    def merge(acc, ya_v, ga, gb):
        return acc, _sigmoid(ga) * ya_v + _sigmoid(gb) * acc

    yb, merged = _matmul("branch_b", [(y_hgrn, w_bh)], "nn", tm=TM, tn=1024, tk=1024,
                         outs=[("mn", F32), ("mn", MXU_DTYPE)],
                         extras=[(ya, *tile(TM, 1024)), gate_a, gate_b], epilogue=merge)

    def resid_norm(acc, xin, g):
        h = xin + acc
        r = lax.rsqrt(jnp.mean(h * h, axis=-1, keepdims=True) + EPS)
        return h, h * r * g

    h1, u2 = _matmul("out_proj", [(merged, w_out)], "nn", tm=TM, tn=1024, tk=1024,
                     outs=[("mn", F32), ("mn", MXU_DTYPE)],
                     extras=[(x, *tile(TM, 1024)), (norm_ffn_g, *row_vec(1024))], epilogue=resid_norm)
    FT = FFN // 2
    names = ("w_ffn_down",)
    (gpre, up, z), got = _swiglu_fwd(u2, w_gate, w_up, tm=TM, tn=FT, comm=net.gather(names))
    net.gathered(names, got)
    w_down = net.full("w_ffn_down")

    def loss_head(acc, h1_v, tgt, g):
        h2 = h1_v + acc
        r = lax.rsqrt(jnp.mean(h2 * h2, axis=-1, keepdims=True) + EPS)
        xhat = h2 * r
        err = xhat * g - tgt
        part = 0.5 * jnp.sum(jnp.sum(err * err, axis=-1, keepdims=True), axis=0, keepdims=True) / D
        dyv = err / D
        dxh = dyv * g
        dh2 = r * (dxh - xhat * jnp.mean(dxh * xhat, axis=-1, keepdims=True))
        return dh2, dh2, jnp.sum(dyv * xhat, axis=0, keepdims=True), jnp.broadcast_to(part, (1, D))

    dh2, dh2b, dgf_p, loss_p = _matmul(
        "ffn_down", [(z, w_down)], "nn", tm=TM, tn=1024, tk=FFN,
        outs=[("mn", F32), ("mn", MXU_DTYPE), ("pn", F32), ("pn", F32)],
        extras=[(h1, *tile(TM, 1024)), (target, *tile(TM, 1024)), (norm_final_g, *row_vec(1024))],
        epilogue=loss_head)
    loss = jnp.sum(loss_p.reshape(-1, 8, D)[:, 0, 0])
    d_norm_final = _colsum_partials(dgf_p)

    def swiglu_bwd(acc, gv, upv):
        s = _sigmoid(gv)
        return acc * upv * (s * (1.0 + gv * (1.0 - s))), acc * (gv * s)

    dgp, dup = _matmul("d_ffn_hidden", [(dh2b, w_down)], "nt", tm=TM, tn=FT, tk=1024,
                       outs=[("mn", MXU_DTYPE), ("mn", MXU_DTYPE)],
                       extras=[(gpre, *tile(TM, FT)), (up, *tile(TM, FT))], epilogue=swiglu_bwd)
    d_w_down = _matmul("dw_down", [(z, dh2b)], "tn", tm=FT, tn=1024, tk=1024, outs=grad_outs, epilogue=both)

    def norm_ffn_bwd(acc, h1_v, g, dres):
        dx, dg = _rmsnorm_bwd_vals(acc, h1_v, g)
        dh = dres + dx
        return dh, dh, dg

    names = ("w_ffn_down",)
    (dh1, dh1b, dg2_p), got = _matmul(
        "d_ffn_in", [(dgp, w_gate), (dup, w_up)], "nn", tm=BIG, tn=1024, tk=FT,
        outs=[("mn", F32), ("mn", MXU_DTYPE), ("pn", F32)],
        extras=[(h1, *tile(BIG, 1024)), (norm_ffn_g, *row_vec(1024)), (dh2, *tile(BIG, 1024))],
        epilogue=norm_ffn_bwd,
        comm=net.exchange(dict(w_ffn_down=[d_w_down])))
    net.received(names, (), got)
    d_norm_ffn = _colsum_partials(dg2_p)
    d_w_gate = _matmul("dw_gate", [(dgp, u2)], "tn", tm=FT, tn=1024, tk=1024, outs=grad_outs, epilogue=both)
    d_w_up = _matmul("dw_up", [(dup, u2)], "tn", tm=FT, tn=1024, tk=1024, outs=grad_outs, epilogue=both)

    def merge_bwd(acc, ya_v, yb_v, ga, gb):
        sa, sb = _sigmoid(ga), _sigmoid(gb)
        return acc * sa, acc * sb, acc * ya_v * sa * (1.0 - sa), acc * yb_v * sb * (1.0 - sb)

    dya, dyb, dga, dgb = _matmul(
        "d_merged", [(dh1b, w_out)], "nt", tm=TM, tn=1024, tk=1024, outs=[("mn", MXU_DTYPE)] * 4,
        extras=[(ya, *tile(TM, 1024)), (yb, *tile(TM, 1024)), gate_a, gate_b], epilogue=merge_bwd)
    tn_grad = functools.partial(_matmul, mode="tn", tn=1024, tk=1024, outs=grad_outs, epilogue=both)
    d_w_out = tn_grad("dw_out", [(merged, dh1b)], tm=1024)
    dy_attn, = _matmul("d_y_attn", [(dya, w_ba)], "nt", tm=TM, tn=1024, tk=1024, outs=[("mn", MXU_DTYPE)])
    dy_hgrn, = _matmul("d_y_hgrn", [(dyb, w_bh)], "nt", tm=TM, tn=1024, tk=1024, outs=[("mn", F32)])
    d_w_ba = tn_grad("dw_branch_a", [(y_attn, dya)], tm=1024)
    d_w_bh = tn_grad("dw_branch_b", [(y_hgrn, dyb)], tm=1024)

    names, swap = ("w_ffn_gate",), ("w_ffn_down",)
    (dq, dkv, dsink), got = _attn_bwd(pq, pkv, sinks, lse, dy_attn,
                                      comm=net.exchange(dict(w_ffn_gate=[d_w_gate]), swap))
    net.received(names, swap, got)
    names, swap = ("w_ffn_up", "w_out", "w_branch_attn", "w_branch_hgrn"), ("w_ffn_gate",)
    (dph, d_hgrn_norm, d_lb_logits), got = _hgrn_bwd(
        ph, o_raw, states, dy_hgrn, lb_logits, hgrn_norm_g,
        comm=net.exchange(dict(w_ffn_up=[d_w_up], w_out=[d_w_out], w_branch_attn=[d_w_ba],
                               w_branch_hgrn=[d_w_bh]), swap))
    net.received(names, swap, got)

    d_w_in = [tn_grad("dw_in_q", [(dq, u)], tm=1024, a_colsum=True),
              tn_grad("dw_in_kv", [(dkv, u)], tm=256, a_colsum=True),
              tn_grad("dw_in_h", [(dph, u)], tm=1024),
              tn_grad("dw_in_ga", [(dga, u)], tm=1024, a_colsum=True),
              tn_grad("dw_in_gb", [(dgb, u)], tm=1024, a_colsum=True)]

    def norm_mix_bwd(acc, xin, g, dres):
        dx, dg = _rmsnorm_bwd_vals(acc, xin, g)
        return dres + dx, dg

    db_h, got = _colsum("db_h", dph, comm=net.presum_begin("w_in", d_w_in))
    halves = net.presum_end("w_in", got)
    names, swap = ("w_in",), ("w_ffn_up", "w_out", "w_branch_attn", "w_branch_hgrn")
    (dx, dg1_p), got = _matmul(
        "d_u", [(dq, w_q), (dkv, w_kv), (dph, w_h), (dga, w_g[:D]), (dgb, w_g[D:])], "nn",
        tm=BIG, tn=1024, tk=512, outs=[("mn", F32), ("pn", F32)],
        extras=[(x, *tile(BIG, 1024)), (norm_mix_g, *row_vec(1024)), (dh1, *tile(BIG, 1024))],
        epilogue=norm_mix_bwd,
        comm=_join(halves, net.swap(swap)))
    net.last = (names, swap, got)
    d_norm_mix = _colsum_partials(dg1_p)
    d_b_in = jnp.concatenate([d_w_in[0][2], d_w_in[1][2], db_h, d_w_in[3][2], d_w_in[4][2]], axis=1)
    vecs = dict(norm_mix_g=d_norm_mix, b_in=d_b_in, attn_sinks=jnp.sum(dsink.reshape(Q_HEADS, ATTN_BLOCK), axis=1).reshape(1, Q_HEADS),
                hgrn_lb_logits=d_lb_logits,
                hgrn_norm_g=d_hgrn_norm, norm_ffn_g=d_norm_ffn, norm_final_g=d_norm_final)
    return loss, dx, vecs


def _place():
    return lax.axis_index("x"), lax.axis_index("y"), lax.axis_index("c")


def _other_chips(x, y):
    return [(1 - x, y), (x, 1 - y), (1 - x, 1 - y)]


def _y_first(copies):
    return [copies[3 * (i // 3) + (1, 0, 2)[i % 3]] for i in range(len(copies))]


def _gather_copies(shards):
    n = len(shards)

    def build(ins, outs, send_sems, recv_sems, local_sems):
        x, y, c = _place()
        mine = 2 * x + y
        local = [pltpu.make_async_copy(ins[w], outs[w].at[mine], local_sems.at[w]) for w in range(n)]
        sends, recvs = [], []
        for w in range(n):
            for k, (px, py) in enumerate(_other_chips(x, y)):
                sem = 3 * w + k
                sends.append(pltpu.make_async_remote_copy(
                    src_ref=ins[w], dst_ref=outs[w].at[mine], send_sem=send_sems.at[sem], recv_sem=recv_sems.at[sem],
                    device_id=(px, py, c), device_id_type=MESH_ID))
                recvs.append(pltpu.make_async_remote_copy(
                    src_ref=ins[w], dst_ref=outs[w].at[2 * px + py], send_sem=send_sems.at[sem],
                    recv_sem=recv_sems.at[sem], device_id=(px, py, c), device_id_type=MESH_ID))
        return sends, recvs, local, _y_first(sends)

    return _Carried(shards, [jax.ShapeDtypeStruct((N_CHIPS,) + s.shape, s.dtype) for s in shards], 3 * n, n, build)


def _grad_copies(stacked):
    n = len(stacked)

    def build(ins, outs, send_sems, recv_sems, local_sems):
        x, y, c = _place()
        sends = []
        for w in range(n):
            for k, (px, py) in enumerate(_other_chips(x, y)):
                sem = 3 * w + k
                sends.append(pltpu.make_async_remote_copy(
                    src_ref=ins[w].at[2 * px + py], dst_ref=outs[w].at[k], send_sem=send_sems.at[sem],
                    recv_sem=recv_sems.at[sem], device_id=(px, py, c), device_id_type=MESH_ID))
        return sends, sends, [], _y_first(sends)

    return _Carried(stacked, [jax.ShapeDtypeStruct((3,) + s.shape[1:], s.dtype) for s in stacked], 3 * n, 0, build)


def _small_copies(small):
    def build(ins, outs, send_sems, recv_sems, local_sems):
        small_ref, all_ref = ins[0], outs[0]
        x, y, c = _place()
        me = 4 * x + 2 * y + c
        sends, recvs = [], []
        for r in range(1, 8):
            px = 1 - x if r & 4 else x
            py = 1 - y if r & 2 else y
            pc = 1 - c if r & 1 else c
            sends.append(pltpu.make_async_remote_copy(
                src_ref=small_ref, dst_ref=all_ref.at[me], send_sem=send_sems.at[r - 1], recv_sem=recv_sems.at[r - 1],
                device_id=(px, py, pc), device_id_type=MESH_ID))
            recvs.append(pltpu.make_async_remote_copy(
                src_ref=small_ref, dst_ref=all_ref.at[4 * px + 2 * py + pc], send_sem=send_sems.at[r - 1],
                recv_sem=recv_sems.at[r - 1], device_id=(px, py, pc), device_id_type=MESH_ID))
        return sends, recvs, [pltpu.make_async_copy(small_ref, all_ref.at[me], local_sems.at[0])]

    return _Carried([small], [jax.ShapeDtypeStruct((8,) + small.shape, small.dtype)], 7, 1, build)


def _gather_by_neighbours(name, shard):
    half = shard.shape[0] // 2
    quarter = half // 2

    def body(in_ref, out_ref, send_sems, recv_sems, local_sem):
        for core in (0, 1):
            @pl.when(lax.axis_index("c") == core)
            def _():
                program(core, in_ref, out_ref, send_sems, recv_sems, local_sem)

    def program(c, in_ref, out_ref, send_sems, recv_sems, local_sem):
        x, y, _ = _place()
        chip = lambda px, py: 2 * px + py
        to_x, to_y, sibling = (1 - x, y, c), (x, 1 - y, c), (x, y, 1 - c)
        x_blk, y_blk, d_blk = chip(1 - x, y), chip(x, 1 - y), chip(1 - x, 1 - y)
        mine, theirs = c * half, (1 - c) * half

        def copy(sem, rows, block, to, src=None):
            place = out_ref.at[block, pl.ds(rows[0], rows[1])]
            return pltpu.make_async_remote_copy(
                src_ref=place if src is None else src, dst_ref=place, send_sem=send_sems.at[sem],
                recv_sem=recv_sems.at[sem], device_id=to, device_id_type=MESH_ID)

        own = pltpu.make_async_copy(in_ref, out_ref.at[chip(x, y)], local_sem)
        own.start()
        my_rows = in_ref.at[pl.ds(mine, half)]
        along_x = dict(send=copy(0, (mine, half), chip(x, y), to_x, src=my_rows),
                       landed=copy(0, (mine, half), x_blk, to_x),
                       onward=[copy(3, (mine + quarter, quarter), x_blk, to_y), copy(4, (mine, half), x_blk, sibling)],
                       diagonal=copy(2, (mine, quarter), d_blk, to_x))
        along_y = dict(send=copy(1, (mine, half), chip(x, y), to_y, src=my_rows),
                       landed=copy(1, (mine, half), y_blk, to_y),
                       onward=[copy(2, (mine, quarter), y_blk, to_x), copy(5, (mine, half), y_blk, sibling)],
                       diagonal=copy(3, (mine + quarter, quarter), d_blk, to_y))
        last = copy(6, (mine, half), d_blk, sibling)

        order = (along_x, along_y) if c == 0 else (along_y, along_x)
        for axis in order:
            axis["send"].start()
        for axis in order:
            axis["landed"].wait_recv()
            for cp in axis["onward"]:
                cp.start()
        for axis in order:
            axis["diagonal"].wait_recv()
        last.start()
        for sem, block in ((4, x_blk), (5, y_blk), (6, d_blk)):
            copy(sem, (theirs, half), block, sibling).wait_recv()
        for cp in [along_x["send"], along_y["send"]] + along_x["onward"] + along_y["onward"] + [last]:
            cp.wait_send()
        own.wait()

    return pl.pallas_call(
        body, name=name, in_specs=[HBM_SPEC], out_specs=HBM_SPEC,
        out_shape=jax.ShapeDtypeStruct((N_CHIPS,) + shard.shape, shard.dtype),
        scratch_shapes=[pltpu.SemaphoreType.DMA((7,)), pltpu.SemaphoreType.DMA((7,)), pltpu.SemaphoreType.DMA(())],
    )(shard)


def _copies_alone(name, comm):
    return _call(name, lambda: None, grid=(), in_specs=[], out_specs=[], out_shape=[], args=[], comm=comm)[1]


class _Net:
    def __init__(self, shards):
        self.shards = shards
        self.whole, self.own, self.theirs, self.sums, self.other = {}, {}, {}, {}, {}
        x, y, _ = _place()
        self.chip = 2 * x + y

    def gather(self, names):
        return _gather_copies([self.shards[n] for n in names])

    def gathered(self, names, got):
        for n, g in zip(names, got):
            self.whole[n] = g.reshape(-1, g.shape[-1])

    def full(self, name):
        return self.whole[name]

    def exchange(self, grads, swap=()):
        stacked = []
        for n, pieces in grads.items():
            keep = jnp.concatenate([p[0] for p in pieces], axis=0) if len(pieces) > 1 else pieces[0][0]
            send = jnp.concatenate([p[1] for p in pieces], axis=0) if len(pieces) > 1 else pieces[0][1]
            rows = keep.shape[0] // N_CHIPS
            self.own[n] = lax.dynamic_slice_in_dim(keep, self.chip * rows, rows, axis=0)
            stacked.append(send.reshape(N_CHIPS, rows, send.shape[-1]))
        return _join(_grad_copies(stacked), self.swap(swap))

    def swap(self, names):
        return _sibling_copies([self.sums[n] for n in names]) if names else None

    def presum_begin(self, name, pieces):
        keep = jnp.concatenate([p[0] for p in pieces], axis=0)
        send = jnp.concatenate([p[1] for p in pieces], axis=0)
        rows = keep.shape[0] // N_CHIPS
        self.held = keep.reshape(N_CHIPS, rows, keep.shape[-1])
        return _half_rows_copies(send.reshape(N_CHIPS, rows, send.shape[-1]))

    def presum_end(self, name, got):
        x, y, c = _place()
        to_send, self.own[name] = _pre_sum("presum_" + name, self.held, got[0], jnp.stack([c, self.chip]))
        return _grad_copies([to_send])

    def received(self, names, swap, got, carried=None):
        self.theirs.update(zip(names, got[:len(names)]))
        self.other.update(zip(swap, got[len(names):]))
        for n in names:
            (self.sums[n],), more = _partial_sum("sum_" + n, self.own[n], self.theirs[n], comm=carried)
        return more


def _half_rows_copies(stacked):
    n, rows = stacked.shape[0], stacked.shape[1] // 2

    def build(ins, outs, send_sems, recv_sems, local_sems):
        x, y, c = _place()
        copies = [pltpu.make_async_remote_copy(
            src_ref=ins[0].at[s, pl.ds((1 - c) * rows, rows)], dst_ref=outs[0].at[s], send_sem=send_sems.at[s],
            recv_sem=recv_sems.at[s], device_id=(x, y, 1 - c), device_id_type=MESH_ID) for s in range(n)]
        return copies, copies, []

    return _Carried([stacked], [jax.ShapeDtypeStruct((n, rows, stacked.shape[2]), stacked.dtype)], n, 0, build)


def _pre_sum(name, held, theirs, core_and_chip):
    n, R, C = held.shape
    half = R // 2
    tr = _row_tile(half)
    per_half = half // tr

    def body(place_ref, h_ref, t_ref, send_ref, own_ref):
        total = h_ref[0] + t_ref[0].astype(F32)
        send_ref[0] = total.astype(send_ref.dtype)

        @pl.when(pl.program_id(1) == place_ref[1])
        def _():
            own_ref[...] = total

    return pl.pallas_call(
        body, name=name,
        grid_spec=pltpu.PrefetchScalarGridSpec(
            num_scalar_prefetch=1, grid=(per_half, n),
            in_specs=[pl.BlockSpec((1, tr, C), lambda i, s, place: (s, place[0] * per_half + i, 0)),
                      pl.BlockSpec((1, tr, C), lambda i, s, place: (s, i, 0))],
            out_specs=[pl.BlockSpec((1, tr, C), lambda i, s, place: (s, i, 0)),
                       pl.BlockSpec((tr, C), lambda i, s, place: (i, 0))]),
        out_shape=[jax.ShapeDtypeStruct((n, half, C), MXU_DTYPE), jax.ShapeDtypeStruct((half, C), F32)],
        compiler_params=_params(("arbitrary", "arbitrary")),
    )(core_and_chip, held, theirs)


def _sibling_copies(parts):
    n = len(parts)

    def build(ins, outs, send_sems, recv_sems, local_sems):
        x, y, c = _place()
        copies = [pltpu.make_async_remote_copy(
            src_ref=ins[w], dst_ref=outs[w], send_sem=send_sems.at[w], recv_sem=recv_sems.at[w],
            device_id=(x, y, 1 - c), device_id_type=MESH_ID) for w in range(n)]
        return copies, copies, []

    return _Carried(parts, [jax.ShapeDtypeStruct(p.shape, p.dtype) for p in parts], n, 0, build)


def _row_tile(rows, most=512, sublanes=16):
    return max(t for t in range(sublanes, most + 1, sublanes) if rows % t == 0)


def _partial_sum(name, own, recv, comm=None):
    R, C = own.shape
    tr = _row_tile(R, most=128)

    def body(o_ref, r_ref, p_ref):
        p_ref[...] = ((o_ref[...] + r_ref[0].astype(F32)) + r_ref[1].astype(F32)) + r_ref[2].astype(F32)

    return _call(name, body, grid=(R // tr,),
                 in_specs=[pl.BlockSpec((tr, C), lambda i: (i, 0)), pl.BlockSpec((3, tr, C), lambda i: (0, i, 0))],
                 out_specs=[pl.BlockSpec((tr, C), lambda i: (i, 0))], out_shape=[jax.ShapeDtypeStruct((R, C), F32)],
                 args=[own, recv], semantics=("parallel",), comm=comm)


def _adam_vals(w, g, m, v):
    m = ADAM_B1 * m + (1.0 - ADAM_B1) * g
    v = ADAM_B2 * v + (1.0 - ADAM_B2) * (g * g)
    m_hat = m / (1.0 - ADAM_B1 ** ADAM_STEP)
    v_hat = v / (1.0 - ADAM_B2 ** ADAM_STEP)
    delta = -ADAM_LR * (m_hat / (jnp.sqrt(v_hat) + ADAM_EPS) + ADAM_WD * w)
    return delta, m, v


def _adamw(name, w, m, v, mine, other, comm=None):
    R, C = w.shape
    tr = _row_tile(R)

    def body(w_ref, m_ref, v_ref, s_ref, n_ref, g_ref, d_ref, nm_ref, nv_ref):
        g = s_ref[...] + n_ref[...]
        d, nm, nv = _adam_vals(w_ref[...], g, m_ref[...], v_ref[...])
        g_ref[...], d_ref[...], nm_ref[...], nv_ref[...] = g, d, nm, nv

    spec = pl.BlockSpec((tr, C), lambda i: (i, 0))
    return _call(name, body, grid=(R // tr,), in_specs=[spec] * 5, out_specs=[spec] * 4,
                 out_shape=[jax.ShapeDtypeStruct((R, C), F32)] * 4, args=[w, m, v, mine, other],
                 semantics=("parallel",), comm=comm)


def _adamw_by_halves(name, w, m, v, mine, other, core):
    R, C = w.shape
    tr = _row_tile(R // 2)
    per_half = R // 2 // tr

    def body(c_ref, w_ref, m_ref, v_ref, s_ref, n_ref, g_ref, d_ref, nm_ref, nv_ref):
        g = jnp.where(pl.program_id(0) // per_half == c_ref[0, 0], s_ref[...], n_ref[...])
        d, nm, nv = _adam_vals(w_ref[...], g, m_ref[...], v_ref[...])
        g_ref[...], d_ref[...], nm_ref[...], nv_ref[...] = g, d, nm, nv

    spec = pl.BlockSpec((tr, C), lambda i: (i, 0))
    part = pl.BlockSpec((tr, C), lambda i: (i % per_half, 0))
    return pl.pallas_call(
        body, name=name, grid=(R // tr,),
        in_specs=[pl.BlockSpec(memory_space=pltpu.SMEM), spec, spec, spec, part, part], out_specs=[spec] * 4,
        out_shape=[jax.ShapeDtypeStruct((R, C), F32)] * 4, compiler_params=_params(("parallel",)),
    )(core, w, m, v, mine, other)


def _adamw_small(w, m, v, g_all):
    def body(w_ref, m_ref, v_ref, a_ref, g_ref, d_ref, nm_ref, nv_ref):
        g = a_ref[0]
        for dev in range(1, 8):
            g = g + a_ref[dev]
        d, nm, nv = _adam_vals(w_ref[...], g, m_ref[...], v_ref[...])
        g_ref[...], d_ref[...], nm_ref[...], nv_ref[...] = g, d, nm, nv

    return pl.pallas_call(
        body, name="adamw_small", out_shape=[jax.ShapeDtypeStruct(w.shape, F32)] * 4,
    )(w, m, v, g_all)


SMALL_LAYOUT = dict(norm_mix_g=(0, 1024), b_in=(1024, 7424), hgrn_norm_g=(8448, 1024), norm_ffn_g=(9472, 1024),
                    norm_final_g=(10496, 1024), hgrn_lb_logits=(11520, 2048), attn_sinks=(13568, 16))
SMALL_LOSS, SMALL_SIZE = 13584, 16 * D_MODEL


def _pack_small(vals, loss=None):
    names = list(SMALL_LAYOUT)
    tail = jnp.zeros((SMALL_SIZE - SMALL_LAYOUT[names[-1]][0],), F32)
    tail = lax.dynamic_update_slice(tail, vals[names[-1]].astype(F32).reshape(-1), (0,))
    if loss is not None:
        tail = lax.dynamic_update_slice(tail, loss.astype(F32).reshape(1), (SMALL_LOSS - SMALL_LAYOUT[names[-1]][0],))
    flat = jnp.concatenate([vals[n].astype(F32).reshape(-1) for n in names[:-1]] + [tail])
    return flat.reshape(-1, D_MODEL)


def _unpack_small(packed, shapes):
    flat = packed.reshape(-1)
    return {name: flat[lo:lo + n].reshape(shapes[name]) for name, (lo, n) in SMALL_LAYOUT.items()}


MATRICES = ("w_in", "w_branch_attn", "w_branch_hgrn", "w_out", "w_ffn_gate", "w_ffn_up", "w_ffn_down")
COLUMN_SHARDED = ("w_in", "w_ffn_gate", "w_ffn_up")
WEIGHTS = ("norm_mix_g", "w_in", "b_in", "attn_sinks", "hgrn_lb_logits", "hgrn_norm_g", "w_branch_attn",
           "w_branch_hgrn", "w_out", "norm_ffn_g", "w_ffn_gate", "w_ffn_up", "w_ffn_down", "norm_final_g")


def kernel(x, norm_mix_g, w_in, b_in, attn_sinks, hgrn_lb_logits, hgrn_norm_g, w_branch_attn, w_branch_hgrn, w_out, norm_ffn_g, w_ffn_gate, w_ffn_up, w_ffn_down, norm_final_g, loss_target, m_norm_mix_g, m_w_in, m_b_in, m_attn_sinks, m_hgrn_lb_logits, m_hgrn_norm_g, m_w_branch_attn, m_w_branch_hgrn, m_w_out, m_norm_ffn_g, m_w_ffn_gate, m_w_ffn_up, m_w_ffn_down, m_norm_final_g, v_norm_mix_g, v_w_in, v_b_in, v_attn_sinks, v_hgrn_lb_logits, v_hgrn_norm_g, v_w_branch_attn, v_w_branch_hgrn, v_w_out, v_norm_ffn_g, v_w_ffn_gate, v_w_ffn_up, v_w_ffn_down, v_norm_final_g):
    given = dict(locals())
    w = {n: given[n] for n in WEIGHTS}
    m = {n: given["m_" + n] for n in WEIGHTS}
    v = {n: given["v_" + n] for n in WEIGHTS}

    block = lambda a, n: jnp.transpose(a[0]) if n in COLUMN_SHARDED else a[0]
    unblock = lambda a, n: (jnp.transpose(a) if n in COLUMN_SHARDED else a)[None]
    net = _Net({n: block(w[n], n).astype(MXU_DTYPE) for n in MATRICES})
    net.gathered(("w_in",), [_gather_by_neighbours("gather_w_in", net.shards["w_in"])])
    vec = dict(norm_mix_g=norm_mix_g, b_in=b_in, attn_sinks=attn_sinks, hgrn_lb_logits=hgrn_lb_logits,
               hgrn_norm_g=hgrn_norm_g, norm_ffn_g=norm_ffn_g, norm_final_g=norm_final_g.reshape(1, D_MODEL))
    loss_part, dx, d_vecs = _local_step(x[0], loss_target[0], vec, net)

    small_all, = net.received(*net.last, carried=_small_copies(_pack_small(d_vecs, loss_part)))
    grads, deltas, new_m, new_v = {}, {}, {}, {}
    for n in ("w_ffn_down", "w_ffn_gate", "w_ffn_up", "w_out", "w_branch_attn", "w_branch_hgrn"):
        res, got = _adamw("adamw_" + n, block(w[n], n), block(m[n], n), block(v[n], n), net.sums[n], net.other[n],
                          comm=net.swap(("w_in",)) if n == "w_ffn_down" else None)
        if n == "w_ffn_down":
            net.other["w_in"], = got
        grads[n], deltas[n], new_m[n], new_v[n] = (unblock(r, n) for r in res)
    n = "w_in"
    res = _adamw_by_halves("adamw_" + n, block(w[n], n), block(m[n], n), block(v[n], n), net.sums[n], net.other[n],
                           _place()[2].reshape(1, 1))
    grads[n], deltas[n], new_m[n], new_v[n] = (unblock(r, n) for r in res)
    shapes = {n: w[n].shape for n in SMALL_LAYOUT}
    res = _adamw_small(_pack_small(w), _pack_small(m), _pack_small(v), small_all)
    for dst, packed in zip((grads, deltas, new_m, new_v), res):
        dst.update(_unpack_small(packed, shapes))

    loss = res[0].reshape(-1)[SMALL_LOSS]
    return (loss, dx[None], *[grads[n] for n in WEIGHTS], *[deltas[n] for n in WEIGHTS],
            *[new_m[n] for n in WEIGHTS], *[new_v[n] for n in WEIGHTS])
```

```python
import collections
import functools
import math

import jax
import jax.numpy as jnp
from jax import lax
from jax.experimental import pallas as pl
from jax.experimental.pallas import tpu as pltpu

F32 = jnp.float32
BF16 = jnp.bfloat16
MXU_DTYPE = jnp.bfloat16
MESH_ID = pl.DeviceIdType.MESH

D_MODEL = 1024
HEAD_DIM = 64
Q_HEADS = 16
KV_HEADS = 2
GROUP = Q_HEADS // KV_HEADS
KV_WIDTH = KV_HEADS * HEAD_DIM
ATTN_BLOCK = 128
HGRN_HEADS = 8
HGRN_K = 128
CHUNK = 64
HGRN_TOKENS = 256
FFN = 2816
IN_SPLITS = (1024, 256, 4096, 2048)
EPS = 1e-6
NEG_INF = -1e30
ADAM_LR, ADAM_B1, ADAM_B2, ADAM_EPS, ADAM_WD, ADAM_STEP = 0.001, 0.9, 0.999, 1e-08, 0.01, 10
N_CHIPS = 4
VMEM_LIMIT = 60 * 1024 * 1024
ROW_ALIGN = 16


def _params(sem=None):
    return pltpu.CompilerParams(dimension_semantics=sem, vmem_limit_bytes=VMEM_LIMIT)


def _sigmoid(v):
    return 0.5 * jnp.tanh(0.5 * v) + 0.5


def _dot(a, b, dims):
    return lax.dot_general(a.astype(MXU_DTYPE), b.astype(MXU_DTYPE), (dims, ((), ())),
                           preferred_element_type=F32)


def _nn(a, b):
    return _dot(a, b, ((1,), (0,)))


def _nt(a, b):
    return _dot(a, b, ((1,), (1,)))


def _tn(a, b):
    return _dot(a, b, ((0,), (0,)))


HBM_SPEC = pl.BlockSpec(memory_space=pl.ANY)


class _Carried:
    def __init__(self, arrays, out_shapes, n_remote, n_local, build):
        self.parts = [(len(arrays), len(out_shapes), build)]
        self.arrays, self.out_shapes = list(arrays), list(out_shapes)
        self.scratch = [pltpu.SemaphoreType.DMA((n_remote,)), pltpu.SemaphoreType.DMA((n_remote,)),
                        pltpu.SemaphoreType.DMA((max(n_local, 1),))]

    def __add__(self, other):
        both = _Carried([], [], 1, 0, None)
        both.parts = self.parts + other.parts
        both.arrays, both.out_shapes = self.arrays + other.arrays, self.out_shapes + other.out_shapes
        both.scratch = self.scratch + other.scratch
        return both

    def _built(self, ins, outs, sems):
        for p, (ni, no, build) in enumerate(self.parts):
            yield build(ins[:ni], outs[:no], *sems[3 * p:3 * p + 3])
            ins, outs = ins[ni:], outs[no:]

    def start(self, ins, outs, sems):
        core = lax.axis_index("c")
        for sends, _, local, *other_order in self._built(ins, outs, sems):
            for cp in local:
                cp.start()
            if not other_order:
                for cp in sends:
                    cp.start()
                continue

            @pl.when(core == 0)
            def _():
                for cp in sends:
                    cp.start()

            @pl.when(core == 1)
            def _():
                for cp in other_order[0]:
                    cp.start()

    def wait(self, ins, outs, sems):
        for sends, recvs, local, *_ in self._built(ins, outs, sems):
            for cp in recvs:
                cp.wait_recv()
            for cp in sends:
                cp.wait_send()
            for cp in local:
                cp.wait()


def _join(*comms):
    comms = [c for c in comms if c is not None]
    return functools.reduce(lambda a, b: a + b, comms) if comms else None


def _call(name, body, *, grid, in_specs, out_specs, out_shape, args, scratch=(), semantics=None, comm=None):
    n_in, n_out, n_scr = len(in_specs), len(out_specs), len(scratch)
    if comm is None:
        res = pl.pallas_call(body, name=name, grid=grid, in_specs=in_specs, out_specs=out_specs, out_shape=out_shape,
                             scratch_shapes=list(scratch), compiler_params=_params(semantics))(*args)
        return list(res), []
    ci, co = len(comm.arrays), len(comm.out_shapes)

    def carrying(*refs):
        ins, refs = refs[:n_in], refs[n_in:]
        c_ins, refs = refs[:ci], refs[ci:]
        outs, refs = refs[:n_out], refs[n_out:]
        c_outs, refs = refs[:co], refs[co:]
        scr, sems = refs[:n_scr], refs[n_scr:]
        if not grid:
            comm.start(c_ins, c_outs, sems)
            body(*ins, *outs, *scr)
            comm.wait(c_ins, c_outs, sems)
            return
        first = functools.reduce(jnp.logical_and, [pl.program_id(a) == 0 for a in range(len(grid))])
        last = functools.reduce(jnp.logical_and, [pl.program_id(a) == g - 1 for a, g in enumerate(grid)])

        @pl.when(first)
        def _():
            comm.start(c_ins, c_outs, sems)

        body(*ins, *outs, *scr)

        @pl.when(last)
        def _():
            comm.wait(c_ins, c_outs, sems)

    res = pl.pallas_call(
        carrying, name=name, grid=grid, in_specs=list(in_specs) + [HBM_SPEC] * ci,
        out_specs=list(out_specs) + [HBM_SPEC] * co, out_shape=list(out_shape) + comm.out_shapes,
        scratch_shapes=list(scratch) + comm.scratch,
        compiler_params=_params(("arbitrary",) * len(grid) if grid else None),
    )(*args, *comm.arrays)
    return list(res[:n_out]), list(res[n_out:])


_NO_COPIES = object()
_Rows = collections.namedtuple("_Rows", "array first rows")


def _matmul(name, pairs, mode, *, tm, tn, tk, outs, extras=(), epilogue=None, a_colsum=False, comm=_NO_COPIES):
    prod = dict(nn=_nn, nt=_nt, tn=_tn)[mode]
    pairs = [(a, b if isinstance(b, _Rows) else _Rows(b, 0, b.shape[0])) for a, b in pairs]
    a0, b0 = pairs[0]
    M = a0.shape[1] if mode == "tn" else a0.shape[0]
    N = b0.rows if mode == "nt" else b0.array.shape[1]
    steps, in_specs, offset = [], [], 0
    for a, b in pairs:
        K = a.shape[0] if mode == "tn" else a.shape[1]
        t = min(tk, K)
        assert K % t == 0, (name, K, t)
        kmap = functools.partial(lambda k, off, n: jnp.clip(k - off, 0, n - 1), off=offset, n=K // t)
        if mode == "tn":
            in_specs.append(pl.BlockSpec((t, tm), functools.partial(lambda i, j, k, f: (f(k), i), f=kmap)))
        else:
            in_specs.append(pl.BlockSpec((tm, t), functools.partial(lambda i, j, k, f: (i, f(k)), f=kmap)))
        whole = b.first == 0 and b.rows == b.array.shape[0]
        if mode == "nt":
            shape = (tn, t) if whole else (pl.Element(tn), pl.Element(t))
            in_specs.append(pl.BlockSpec(shape, functools.partial(
                lambda i, j, k, f, b, t, whole: (j, f(k)) if whole else (
                    pl.multiple_of(b.first + j * tn, ROW_ALIGN), pl.multiple_of(f(k) * t, 128)),
                f=kmap, b=b, t=t, whole=whole)))
        else:
            assert b.rows == K, (name, b.rows, K)
            shape = (t, tn) if whole else (pl.Element(t), pl.Element(tn))
            in_specs.append(pl.BlockSpec(shape, functools.partial(
                lambda i, j, k, f, b, t, whole: (f(k), j) if whole else (
                    pl.multiple_of(b.first + f(k) * t, ROW_ALIGN), pl.multiple_of(j * tn, 128)),
                f=kmap, b=b, t=t, whole=whole)))
        steps.append((offset, offset + K // t))
        offset += K // t
    assert M % tm == 0 and N % tn == 0, (name, M, N, tm, tn)
    ni, nj, nk = M // tm, N // tn, offset
    npair, ne, no = len(pairs), len(extras), len(outs)
    if epilogue is None:
        epilogue = lambda acc: (acc,)

    def finish(acc, extra_refs, out_refs):
        vals = epilogue(acc, *[r[...] for r in extra_refs])
        for (kind, _), o_ref, val in zip(outs, out_refs, vals):
            if kind == "pn":
                val = jnp.broadcast_to(val, o_ref.shape)
            o_ref[...] = val.astype(o_ref.dtype)

    def body(*refs):
        ab, rest = refs[:2 * npair], refs[2 * npair:]
        extra_refs, out_refs = rest[:ne], rest[ne:ne + no]
        if nk == 1:
            finish(prod(ab[0][...], ab[1][...]), extra_refs, out_refs)
            return
        acc_ref = rest[-1]
        k = pl.program_id(2)

        @pl.when(k == 0)
        def _():
            acc_ref[...] = jnp.zeros_like(acc_ref)
            if a_colsum:
                rest[ne + no][...] = jnp.zeros((1, tm), F32)

        if a_colsum:
            rest[ne + no][...] += jnp.sum(ab[0][...].astype(F32), axis=0, keepdims=True)
        for p, (lo, hi) in enumerate(steps):
            @pl.when(jnp.logical_and(k >= lo, k < hi))
            def _():
                acc_ref[...] += prod(ab[2 * p][...], ab[2 * p + 1][...])

        @pl.when(k == nk - 1)
        def _():
            finish(acc_ref[...], extra_refs, out_refs)

    for _, shape, im in extras:
        in_specs.append(pl.BlockSpec(shape, functools.partial(lambda i, j, k, im: im(i, j), im=im)))
    out_shape, out_specs = [], []
    for kind, dt in outs:
        if kind == "mn":
            out_shape.append(jax.ShapeDtypeStruct((M, N), dt))
            out_specs.append(pl.BlockSpec((tm, tn), lambda i, j, k: (i, j)))
        else:
            out_shape.append(jax.ShapeDtypeStruct((8 * ni, N), dt))
            out_specs.append(pl.BlockSpec((8, tn), lambda i, j, k: (i, j)))
    if a_colsum:
        assert mode == "tn" and npair == 1 and nj == 1 and nk > 1, name
        out_shape.append(jax.ShapeDtypeStruct((1, M), F32))
        out_specs.append(pl.BlockSpec((1, tm), lambda i, j, k: (0, i)))
    grid = (ni, nj, nk)
    if nj > 1 and nk == 1:
        turned = lambda spec: pl.BlockSpec(spec.block_shape, functools.partial(
            lambda j, i, k, im: im(i, j, k), im=spec.index_map))
        in_specs, out_specs, grid = [turned(s) for s in in_specs], [turned(s) for s in out_specs], (nj, ni, nk)
    res, got = _call(name, body, grid=grid, in_specs=in_specs, out_specs=out_specs, out_shape=out_shape,
                     args=[t for a, b in pairs for t in (a, b.array)] + [e[0] for e in extras],
                     scratch=[pltpu.VMEM((tm, tn), F32)] if nk > 1 else [],
                     semantics=("parallel", "parallel", "arbitrary"), comm=None if comm is _NO_COPIES else comm)
    return res if comm is _NO_COPIES else (res, got)


def _swiglu_fwd(u, w_gate_t, w_up_t, *, tm, tn, comm=None):
    (T, D), F = u.shape, w_gate_t.shape[0]

    def body(u_ref, wg_ref, wu_ref, g_ref, up_ref, z_ref):
        g, up = _nt(u_ref[...], wg_ref[...]), _nt(u_ref[...], wu_ref[...])
        g_ref[...], up_ref[...] = g, up
        z_ref[...] = (g * _sigmoid(g) * up).astype(z_ref.dtype)

    w_spec = pl.BlockSpec((tn, D), lambda j, i: (j, 0))
    o_spec = pl.BlockSpec((tm, tn), lambda j, i: (i, j))
    return _call("ffn_hidden", body, grid=(F // tn, T // tm),
                 in_specs=[pl.BlockSpec((tm, D), lambda j, i: (i, 0)), w_spec, w_spec], out_specs=[o_spec] * 3,
                 out_shape=[jax.ShapeDtypeStruct((T, F), F32)] * 2 + [jax.ShapeDtypeStruct((T, F), MXU_DTYPE)],
                 args=[u, w_gate_t, w_up_t], semantics=("parallel", "parallel"), comm=comm)


def _colsum_partials(p):
    return jnp.sum(p.reshape(-1, 8, p.shape[-1])[:, 0, :], axis=0, keepdims=True)


def _rmsnorm_fwd(name, x, g, tr=512):
    T, D = x.shape

    def body(x_ref, g_ref, u_ref):
        xv = x_ref[...]
        r = lax.rsqrt(jnp.mean(xv * xv, axis=-1, keepdims=True) + EPS)
        u_ref[...] = (xv * r * g_ref[...]).astype(u_ref.dtype)

    return pl.pallas_call(
        body, name=name, grid=(T // tr,),
        in_specs=[pl.BlockSpec((tr, D), lambda i: (i, 0)), pl.BlockSpec((1, D), lambda i: (0, 0))],
        out_specs=pl.BlockSpec((tr, D), lambda i: (i, 0)),
        out_shape=jax.ShapeDtypeStruct((T, D), MXU_DTYPE),
        compiler_params=_params(("parallel",)),
    )(x, g)


def _rmsnorm_bwd_vals(dy, xin, g):
    rstd = lax.rsqrt(jnp.mean(xin * xin, axis=-1, keepdims=True) + EPS)
    xhat = xin * rstd
    dg = jnp.sum(dy * xhat, axis=0, keepdims=True)
    dxh = dy * g
    dx = rstd * (dxh - xhat * jnp.mean(dxh * xhat, axis=-1, keepdims=True))
    return dx, dg


def _colsum(name, a, tr=512, comm=_NO_COPIES):
    T, N = a.shape

    def body(a_ref, o_ref):
        @pl.when(pl.program_id(0) == 0)
        def _():
            o_ref[...] = jnp.zeros_like(o_ref)

        o_ref[...] += jnp.sum(a_ref[...].astype(F32), axis=0, keepdims=True)

    (res,), got = _call(name, body, grid=(T // tr,), in_specs=[pl.BlockSpec((tr, N), lambda i: (i, 0))],
                        out_specs=[pl.BlockSpec((1, N), lambda i: (0, 0))],
                        out_shape=[jax.ShapeDtypeStruct((1, N), F32)], args=[a], semantics=("arbitrary",),
                        comm=None if comm is _NO_COPIES else comm)
    return res if comm is _NO_COPIES else (res, got)


ATTN_SCALE = 1.0 / math.sqrt(HEAD_DIM)
GROUP_LANES = GROUP * ATTN_BLOCK
PAIR = 2 * HEAD_DIM


def _attn_mask():
    kj = lax.broadcasted_iota(jnp.int32, (ATTN_BLOCK, GROUP_LANES), 0)
    qi = lax.broadcasted_iota(jnp.int32, (ATTN_BLOCK, GROUP_LANES), 1) & (ATTN_BLOCK - 1)
    return kj <= qi


def _heads_transposed(ref, g, scale=None):
    parts = []
    for a in range(GROUP // 2):
        lo = (g * GROUP // 2 + a) * PAIR
        pair = ref[:, lo:lo + PAIR].astype(F32)
        pair = (pair if scale is None else pair * scale).T
        parts += [pair[:HEAD_DIM], pair[HEAD_DIM:]]
    return jnp.concatenate(parts, axis=1).astype(MXU_DTYPE)


def _heads_back(ref, g, vt):
    for a in range(GROUP // 2):
        lo = (g * GROUP // 2 + a) * PAIR
        pair = jnp.concatenate([vt[:, (2 * a) * ATTN_BLOCK:(2 * a + 1) * ATTN_BLOCK],
                                vt[:, (2 * a + 1) * ATTN_BLOCK:(2 * a + 2) * ATTN_BLOCK]], axis=0)
        ref[:, lo:lo + PAIR] = pair.T.astype(ref.dtype)


def _kv_parts(kv_ref, g):
    ks = slice(g * HEAD_DIM, (g + 1) * HEAD_DIM)
    vs = slice(KV_WIDTH + g * HEAD_DIM, KV_WIDTH + (g + 1) * HEAD_DIM)
    return kv_ref[:, ks].astype(MXU_DTYPE), kv_ref[:, vs].astype(MXU_DTYPE)


def _sink_rows(sinks):
    return jnp.repeat(sinks.reshape(KV_HEADS, GROUP), ATTN_BLOCK, axis=1)


def _attn_fwd(pq, pkv, sinks, comm=None):
    T = pq.shape[0]
    nb = T // ATTN_BLOCK

    def body(q_ref, kvc_ref, kvp_ref, s_ref, y_ref, lse_ref):
        mask_c = _attn_mask()
        has_prev = pl.program_id(0) > 0
        for g in range(KV_HEADS):
            (kc, vc), (kp, vp) = _kv_parts(kvc_ref, g), _kv_parts(kvp_ref, g)
            qt = _heads_transposed(q_ref, g, ATTN_SCALE)
            s = jnp.where(mask_c, _nn(kc, qt), jnp.where(has_prev, _nn(kp, qt), NEG_INF))
            sink = s_ref[g:g + 1, :]
            m = jnp.maximum(jnp.max(s, axis=0, keepdims=True), sink)
            p = jnp.exp(s - m)
            den = jnp.sum(p, axis=0, keepdims=True) + jnp.exp(sink - m)
            pc = jnp.where(mask_c, p, 0.0)
            _heads_back(y_ref, g, (_tn(vc, pc) + _tn(vp, p - pc)) / den)
            lse = m + jnp.log(den)
            for i in range(GROUP):
                lse_ref[g * GROUP + i:g * GROUP + i + 1, :] = lse[:, i * ATTN_BLOCK:(i + 1) * ATTN_BLOCK]

    return _call(
        "attn_fwd", body, grid=(nb,),
        in_specs=[pl.BlockSpec((ATTN_BLOCK, D_MODEL), lambda n: (n, 0)),
                  pl.BlockSpec((ATTN_BLOCK, 2 * KV_WIDTH), lambda n: (n, 0)),
                  pl.BlockSpec((ATTN_BLOCK, 2 * KV_WIDTH), lambda n: (jnp.maximum(n - 1, 0), 0)),
                  pl.BlockSpec((KV_HEADS, GROUP_LANES), lambda n: (0, 0))],
        out_specs=[pl.BlockSpec((ATTN_BLOCK, D_MODEL), lambda n: (n, 0)),
                   pl.BlockSpec((Q_HEADS, ATTN_BLOCK), lambda n: (0, n))],
        out_shape=[jax.ShapeDtypeStruct((T, D_MODEL), MXU_DTYPE), jax.ShapeDtypeStruct((Q_HEADS, T), F32)],
        args=[pq, pkv, pkv, _sink_rows(sinks)], semantics=("parallel",), comm=comm)


def _attn_bwd(pq, pkv, sinks, lse, dy, comm=None):
    T = pq.shape[0]
    nb = T // ATTN_BLOCK
    cur = lambda n: (jnp.minimum(n, nb - 1), 0)

    def body(q_ref, kvc_ref, kvp_ref, s_ref, lse_ref, dy_ref, dq_ref, dkv_ref, ds_ref, carry, top, bot):
        n = pl.program_id(0)

        @pl.when(n == 0)
        def _():
            carry[...] = jnp.zeros_like(carry)
            ds_ref[...] = jnp.zeros_like(ds_ref)

        @pl.when(n < nb)
        def _():
            mask_c = _attn_mask()
            valid = jnp.logical_or(mask_c, n > 0)
            for g in range(KV_HEADS):
                ks = slice(g * HEAD_DIM, (g + 1) * HEAD_DIM)
                vs = slice(KV_WIDTH + g * HEAD_DIM, KV_WIDTH + (g + 1) * HEAD_DIM)
                (kc, vc), (kp, vp) = _kv_parts(kvc_ref, g), _kv_parts(kvp_ref, g)
                qt = _heads_transposed(q_ref, g, ATTN_SCALE)
                dot = _heads_transposed(dy_ref, g)
                lse = jnp.concatenate([lse_ref[g * GROUP + i:g * GROUP + i + 1, :] for i in range(GROUP)], axis=1)
                p = jnp.where(valid, jnp.exp(jnp.where(mask_c, _nn(kc, qt), _nn(kp, qt)) - lse), 0.0)
                dp = jnp.where(mask_c, _nn(vc, dot), _nn(vp, dot))
                delta = jnp.sum(p * dp, axis=0, keepdims=True)
                ds = p * (dp - delta)
                ds_c, p_c = jnp.where(mask_c, ds, 0.0), jnp.where(mask_c, p, 0.0)
                ds_p, p_p = ds - ds_c, p - p_c
                _heads_back(dq_ref, g, (_tn(kc, ds_c) + _tn(kp, ds_p)) * ATTN_SCALE)
                bot[:, ks], bot[:, vs] = _nt(ds_c, qt), _nt(p_c, dot)
                top[:, ks], top[:, vs] = _nt(ds_p, qt), _nt(p_p, dot)
                ds_ref[g:g + 1, :] -= jnp.exp(s_ref[g:g + 1, :] - lse) * delta
            dkv_ref[...] = (carry[...] + top[...]).astype(dkv_ref.dtype)
            carry[...] = bot[...]

        @pl.when(n == nb)
        def _():
            dkv_ref[...] = carry[...].astype(dkv_ref.dtype)

    return _call(
        "attn_bwd", body, grid=(nb + 1,),
        in_specs=[pl.BlockSpec((ATTN_BLOCK, D_MODEL), cur),
                  pl.BlockSpec((ATTN_BLOCK, 2 * KV_WIDTH), cur),
                  pl.BlockSpec((ATTN_BLOCK, 2 * KV_WIDTH), lambda n: (jnp.maximum(jnp.minimum(n, nb - 1) - 1, 0), 0)),
                  pl.BlockSpec((KV_HEADS, GROUP_LANES), lambda n: (0, 0)),
                  pl.BlockSpec((Q_HEADS, ATTN_BLOCK), lambda n: (0, jnp.minimum(n, nb - 1))),
                  pl.BlockSpec((ATTN_BLOCK, D_MODEL), cur)],
        out_specs=[pl.BlockSpec((ATTN_BLOCK, D_MODEL), cur),
                   pl.BlockSpec((ATTN_BLOCK, 2 * KV_WIDTH), lambda n: (jnp.maximum(n - 1, 0), 0)),
                   pl.BlockSpec((KV_HEADS, GROUP_LANES), lambda n: (0, 0))],
        out_shape=[jax.ShapeDtypeStruct((T, D_MODEL), MXU_DTYPE),
                   jax.ShapeDtypeStruct((T, 2 * KV_WIDTH), MXU_DTYPE),
                   jax.ShapeDtypeStruct((KV_HEADS, GROUP_LANES), F32)],
        scratch=[pltpu.VMEM((ATTN_BLOCK, 2 * KV_WIDTH), F32)] * 3,
        args=[pq, pkv, pkv, _sink_rows(sinks), lse, dy], semantics=("arbitrary",), comm=comm)


def _lower_bound(l):
    m = jnp.maximum(l[0:1], l[1:2])
    e0, e1 = jnp.exp(l[0:1] - m), jnp.exp(l[1:2] - m)
    return e0 / (e0 + e1)


def _tri(lower):
    r = lax.broadcasted_iota(jnp.int32, (CHUNK, CHUNK), 0)
    c = lax.broadcasted_iota(jnp.int32, (CHUNK, CHUNK), 1)
    return (r >= c) if lower else (c >= r)


def _chunk_sum(mask, v):
    ones = mask.astype(BF16)
    hi = v.astype(BF16)
    rest = v - hi.astype(F32)
    mid = rest.astype(BF16)
    lo = (rest - mid.astype(F32)).astype(BF16)
    part = lambda t: lax.dot_general(ones, t, (((1,), (0,)), ((), ())), preferred_element_type=F32)
    return part(hi) + part(mid) + part(lo)


def _hgrn_chunk_inputs(hq, hf, lb, causal):
    sg, sgn = _sigmoid(hf), _sigmoid(-hf)
    f = lb + (1.0 - lb) * sg
    kk = (1.0 - lb) * sgn
    sq = _sigmoid(hq)
    q = hq * sq
    b = _chunk_sum(causal, jnp.log(f))
    bm, bl = b[CHUNK // 2 - 1:CHUNK // 2, :], b[CHUNK - 1:CHUNK, :]
    e_qm, e_km, e_qs, e_kl = jnp.exp(b - bm), jnp.exp(bm - b), jnp.exp(b), jnp.exp(bl - b)
    return dict(sg=sg, sgn=sgn, f=f, kk=kk, sq=sq, q=q, e_qm=e_qm, e_km=e_km, e_qs=e_qs, e_kl=e_kl,
                qm=q * e_qm, km=kk * e_km, qs=q * e_qs, kl=kk * e_kl, el=jnp.exp(bl))


def _hgrn_fwd(ph, lb_logits, norm_g, comm=None):
    T = ph.shape[0]
    nblk, cpb = T // HGRN_TOKENS, HGRN_TOKENS // CHUNK
    col = lambda c: pl.BlockSpec((HGRN_TOKENS, D_MODEL), functools.partial(lambda i, c: (i, c), c=c))

    def body(hq_ref, hf_ref, hi_ref, hg_ref, l_ref, ng_ref, y_ref, o_ref, st_ref, s_ref):
        @pl.when(pl.program_id(0) == 0)
        def _():
            s_ref[...] = jnp.zeros_like(s_ref)

        lb = _lower_bound(l_ref[...])
        causal = _tri(True)
        for c in range(cpb):
            rows = slice(c * CHUNK, (c + 1) * CHUNK)
            t = _hgrn_chunk_inputs(hq_ref[rows, :], hf_ref[rows, :], lb, causal)
            qm, km, qs, kl = (t[n].astype(MXU_DTYPE) for n in ("qm", "km", "qs", "kl"))
            v = hi_ref[rows, :].astype(MXU_DTYPE)
            for h in range(HGRN_HEADS):
                ls = slice(h * HGRN_K, (h + 1) * HGRN_K)
                st = s_ref[h]
                st_ref[c, ls, :] = st
                a = jnp.where(causal, _nt(qm[:, ls], km[:, ls]), 0.0)
                o_ref[rows, ls] = _nn(a, v[:, ls]) + _nt(qs[:, ls], st)
                s_ref[h] = t["el"][:, ls] * st + _tn(v[:, ls], kl[:, ls])
        for h in range(HGRN_HEADS):
            ls = slice(h * HGRN_K, (h + 1) * HGRN_K)
            o = o_ref[:, ls]
            r = lax.rsqrt(jnp.mean(o * o, axis=-1, keepdims=True) + EPS)
            y_ref[:, ls] = (o * r * ng_ref[:, ls] * _sigmoid(hg_ref[:, ls])).astype(y_ref.dtype)

    return _call(
        "hgrn_fwd", body, grid=(nblk,),
        in_specs=[col(0), col(1), col(2), col(3),
                  pl.BlockSpec((2, D_MODEL), lambda i: (0, 0)), pl.BlockSpec((1, D_MODEL), lambda i: (0, 0))],
        out_specs=[pl.BlockSpec((HGRN_TOKENS, D_MODEL), lambda i: (i, 0)),
                   pl.BlockSpec((HGRN_TOKENS, D_MODEL), lambda i: (i, 0)),
                   pl.BlockSpec((cpb, D_MODEL, HGRN_K), lambda i: (i, 0, 0))],
        out_shape=[jax.ShapeDtypeStruct((T, D_MODEL), MXU_DTYPE), jax.ShapeDtypeStruct((T, D_MODEL), F32),
                   jax.ShapeDtypeStruct((T // CHUNK, D_MODEL, HGRN_K), F32)],
        scratch=[pltpu.VMEM((HGRN_HEADS, HGRN_K, HGRN_K), F32)],
        args=[ph, ph, ph, ph, lb_logits, norm_g], semantics=("arbitrary",), comm=comm)


def _hgrn_bwd(ph, o_raw, states, dy, lb_logits, norm_g, comm=None):
    T = ph.shape[0]
    nblk, cpb = T // HGRN_TOKENS, HGRN_TOKENS // CHUNK
    rev = lambda i: nblk - 1 - i
    col = lambda c: pl.BlockSpec((HGRN_TOKENS, D_MODEL), functools.partial(lambda i, c: (rev(i), c), c=c))
    tok = pl.BlockSpec((HGRN_TOKENS, D_MODEL), lambda i: (rev(i), 0))

    def body(hq_ref, hf_ref, hi_ref, hg_ref, o_ref, st_ref, dy_ref, l_ref, ng_ref,
             dph_ref, dng_ref, dl_ref, dst_ref, dlb_ref, do_s, dqm_s, dkm_s, dqs_s, dkl_s, dv_s, del_s):
        i = pl.program_id(0)

        @pl.when(i == 0)
        def _():
            dst_ref[...] = jnp.zeros_like(dst_ref)
            dlb_ref[...] = jnp.zeros_like(dlb_ref)
            dng_ref[...] = jnp.zeros_like(dng_ref)

        lb = _lower_bound(l_ref[...])
        causal, anti = _tri(True), _tri(False)
        row = lax.broadcasted_iota(jnp.int32, (CHUNK, D_MODEL), 0)
        for c in reversed(range(cpb)):
            rows = slice(c * CHUNK, (c + 1) * CHUNK)
            hq = hq_ref[rows, :]
            t = _hgrn_chunk_inputs(hq, hf_ref[rows, :], lb, causal)
            sgg = _sigmoid(hg_ref[rows, :])
            dyv = dy_ref[rows, :]
            for h in range(HGRN_HEADS):
                ls = slice(h * HGRN_K, (h + 1) * HGRN_K)
                o = o_ref[rows, ls]
                r = lax.rsqrt(jnp.mean(o * o, axis=-1, keepdims=True) + EPS)
                nrm = o * r
                g_h = sgg[:, ls]
                dph_ref[rows, 3 * D_MODEL + h * HGRN_K:3 * D_MODEL + (h + 1) * HGRN_K] = (
                    dyv[:, ls] * nrm * ng_ref[:, ls] * g_h * (1.0 - g_h)).astype(dph_ref.dtype)
                dyg = dyv[:, ls] * g_h
                dng_ref[:, ls] += jnp.sum(dyg * nrm, axis=0, keepdims=True)
                dn = dyg * ng_ref[:, ls]
                do_s[:, ls] = r * (dn - nrm * jnp.mean(dn * nrm, axis=-1, keepdims=True))
            qm, km, qs, kl = (t[n].astype(MXU_DTYPE) for n in ("qm", "km", "qs", "kl"))
            v = hi_ref[rows, :].astype(MXU_DTYPE)
            do = do_s[...].astype(MXU_DTYPE)
            for h in range(HGRN_HEADS):
                ls = slice(h * HGRN_K, (h + 1) * HGRN_K)
                st = st_ref[c, ls, :]
                dst = dst_ref[h]
                a = jnp.where(causal, _nt(qm[:, ls], km[:, ls]), 0.0)
                da = jnp.where(causal, _nt(do[:, ls], v[:, ls]), 0.0)
                dv_s[:, ls] = _tn(a, do[:, ls]) + _nt(kl[:, ls], dst)
                dkl_s[:, ls] = _nn(v[:, ls], dst)
                dqs_s[:, ls] = _nn(do[:, ls], st)
                del_s[:, ls] = jnp.sum(dst * st, axis=0, keepdims=True)
                dst_ref[h] = _tn(do[:, ls], qs[:, ls]) + t["el"][:, ls] * dst
                dqm_s[:, ls] = _nn(da, km[:, ls])
                dkm_s[:, ls] = _tn(da, qm[:, ls])
            dqm, dkm, dqs, dkl = dqm_s[...], dkm_s[...], dqs_s[...], dkl_s[...]
            dq = dqm * t["e_qm"] + dqs * t["e_qs"]
            dk = dkm * t["e_km"] + dkl * t["e_kl"]
            t_qm, t_km, t_kl = dqm * t["qm"], dkm * t["km"], dkl * t["kl"]
            db = t_qm - t_km + dqs * t["qs"] - t_kl
            db_mid = jnp.sum(t_km - t_qm, axis=0, keepdims=True)
            db_last = jnp.sum(t_kl, axis=0, keepdims=True) + del_s[...] * t["el"]
            db = db + jnp.where(row == CHUNK // 2 - 1, db_mid, 0.0) + jnp.where(row == CHUNK - 1, db_last, 0.0)
            dlogf = _chunk_sum(anti, db)
            sq, sg, sgn, f = t["sq"], t["sg"], t["sgn"], t["f"]
            dph_ref[rows, 0:D_MODEL] = (dq * (sq * (1.0 + hq * (1.0 - sq)))).astype(dph_ref.dtype)
            dph_ref[rows, D_MODEL:2 * D_MODEL] = (
                dlogf * (1.0 - lb) * sg * (1.0 - sg) / f - dk * (1.0 - lb) * sgn * (1.0 - sgn)).astype(dph_ref.dtype)
            dph_ref[rows, 2 * D_MODEL:3 * D_MODEL] = dv_s[...].astype(dph_ref.dtype)
            dlb_ref[...] += jnp.sum(dlogf * (1.0 - sg) / f - dk * sgn, axis=0, keepdims=True)

        @pl.when(i == nblk - 1)
        def _():
            dl0 = dlb_ref[...] * lb * (1.0 - lb)
            dl_ref[0:1, :] = dl0
            dl_ref[1:2, :] = -dl0

    wide = pltpu.VMEM((CHUNK, D_MODEL), F32)
    return _call(
        "hgrn_bwd", body, grid=(nblk,),
        in_specs=[col(0), col(1), col(2), col(3), tok,
                  pl.BlockSpec((cpb, D_MODEL, HGRN_K), lambda i: (rev(i), 0, 0)), tok,
                  pl.BlockSpec((2, D_MODEL), lambda i: (0, 0)), pl.BlockSpec((1, D_MODEL), lambda i: (0, 0))],
        out_specs=[pl.BlockSpec((HGRN_TOKENS, 4 * D_MODEL), lambda i: (rev(i), 0)),
                   pl.BlockSpec((1, D_MODEL), lambda i: (0, 0)), pl.BlockSpec((2, D_MODEL), lambda i: (0, 0))],
        out_shape=[jax.ShapeDtypeStruct((T, 4 * D_MODEL), MXU_DTYPE), jax.ShapeDtypeStruct((1, D_MODEL), F32),
                   jax.ShapeDtypeStruct((2, D_MODEL), F32)],
        scratch=[pltpu.VMEM((HGRN_HEADS, HGRN_K, HGRN_K), F32), pltpu.VMEM((1, D_MODEL), F32),
                 wide, wide, wide, wide, wide, wide, pltpu.VMEM((1, D_MODEL), F32)],
        args=[ph, ph, ph, ph, o_raw, states, dy, lb_logits, norm_g], semantics=("arbitrary",), comm=comm)


def _local_step(x, target, vec, net):
    T, D = x.shape
    norm_mix_g, b_in, sinks, lb_logits = vec["norm_mix_g"], vec["b_in"], vec["attn_sinks"], vec["hgrn_lb_logits"]
    hgrn_norm_g, norm_ffn_g, norm_final_g = vec["hgrn_norm_g"], vec["norm_ffn_g"], vec["norm_final_g"]
    w_in = net.full("w_in")
    o_q, o_kv, o_h, o_g = (sum(IN_SPLITS[:i]) for i in range(4))
    w_q, w_kv, w_h, w_g = (_Rows(w_in, o, n) for o, n in zip((o_q, o_kv, o_h, o_g), IN_SPLITS))
    b_q, b_kv, b_h, b_g = (b_in[:, o:o + n] for o, n in zip((o_q, o_kv, o_h, o_g), IN_SPLITS))
    bias = lambda acc, b: (acc + b,)
    both = lambda acc: (acc, acc)
    grad_outs = [("mn", F32), ("mn", MXU_DTYPE)]
    row_vec = lambda n: ((1, n), lambda i, j: (0, j))
    tile = lambda tm, tn: ((tm, tn), lambda i, j: (i, j))
    TM = 512
    BIG = min(T, 1024)

    u = _rmsnorm_fwd("norm_mix", x, norm_mix_g)
    pq, = _matmul("in_q", [(u, w_q)], "nt", tm=TM, tn=1024, tk=1024, outs=[("mn", MXU_DTYPE)],
                  extras=[(b_q, *row_vec(1024))], epilogue=bias)
    pkv, = _matmul("in_kv", [(u, w_kv)], "nt", tm=TM, tn=256, tk=1024, outs=[("mn", MXU_DTYPE)],
                   extras=[(b_kv, *row_vec(256))], epilogue=bias)
    names = ("w_branch_attn", "w_branch_hgrn")
    (ph,), got = _matmul("in_h", [(u, w_h)], "nt", tm=TM, tn=1024, tk=1024, outs=[("mn", F32)],
                         extras=[(b_h, *row_vec(1024))], epilogue=bias, comm=net.gather(names))
    net.gathered(names, got)
    names = ("w_out",)
    (pg,), got = _matmul("in_g", [(u, w_g)], "nt", tm=TM, tn=1024, tk=1024, outs=[("mn", F32)],
                         extras=[(b_g, *row_vec(1024))], epilogue=bias, comm=net.gather(names))
    net.gathered(names, got)
    names = ("w_ffn_gate",)
    (y_attn, lse), got = _attn_fwd(pq, pkv, sinks, comm=net.gather(names))
    net.gathered(names, got)
    names = ("w_ffn_up",)
    (y_hgrn, o_raw, states), got = _hgrn_fwd(ph, lb_logits, hgrn_norm_g, comm=net.gather(names))
    net.gathered(names, got)
    w_ba, w_bh, w_out = net.full("w_branch_attn"), net.full("w_branch_hgrn"), net.full("w_out")
    w_gate, w_up = net.full("w_ffn_gate"), net.full("w_ffn_up")
    ya, = _matmul("branch_a", [(y_attn, w_ba)], "nn", tm=TM, tn=1024, tk=1024, outs=[("mn", F32)])
    gate_a = (pg, (TM, 1024), lambda i, j: (i, 0))
    gate_b = (pg, (TM, 1024), lambda i, j: (i, 1))

    def merge(acc, ya_v, ga, gb):
        return acc, _sigmoid(ga) * ya_v + _sigmoid(gb) * acc

    yb, merged = _matmul("branch_b", [(y_hgrn, w_bh)], "nn", tm=TM, tn=1024, tk=1024,
                         outs=[("mn", F32), ("mn", MXU_DTYPE)],
                         extras=[(ya, *tile(TM, 1024)), gate_a, gate_b], epilogue=merge)

    def resid_norm(acc, xin, g):
        h = xin + acc
        r = lax.rsqrt(jnp.mean(h * h, axis=-1, keepdims=True) + EPS)
        return h, h * r * g

    h1, u2 = _matmul("out_proj", [(merged, w_out)], "nn", tm=TM, tn=1024, tk=1024,
                     outs=[("mn", F32), ("mn", MXU_DTYPE)],
                     extras=[(x, *tile(TM, 1024)), (norm_ffn_g, *row_vec(1024))], epilogue=resid_norm)
    FT = FFN // 2
    names = ("w_ffn_down",)
    (gpre, up, z), got = _swiglu_fwd(u2, w_gate, w_up, tm=TM, tn=FT, comm=net.gather(names))
    net.gathered(names, got)
    w_down = net.full("w_ffn_down")

    def loss_head(acc, h1_v, tgt, g):
        h2 = h1_v + acc
        r = lax.rsqrt(jnp.mean(h2 * h2, axis=-1, keepdims=True) + EPS)
        xhat = h2 * r
        err = xhat * g - tgt
        part = 0.5 * jnp.sum(jnp.sum(err * err, axis=-1, keepdims=True), axis=0, keepdims=True) / D
        dyv = err / D
        dxh = dyv * g
        dh2 = r * (dxh - xhat * jnp.mean(dxh * xhat, axis=-1, keepdims=True))
        return dh2, dh2, jnp.sum(dyv * xhat, axis=0, keepdims=True), jnp.broadcast_to(part, (1, D))

    dh2, dh2b, dgf_p, loss_p = _matmul(
        "ffn_down", [(z, w_down)], "nn", tm=TM, tn=1024, tk=FFN,
        outs=[("mn", F32), ("mn", MXU_DTYPE), ("pn", F32), ("pn", F32)],
        extras=[(h1, *tile(TM, 1024)), (target, *tile(TM, 1024)), (norm_final_g, *row_vec(1024))],
        epilogue=loss_head)
    loss = jnp.sum(loss_p.reshape(-1, 8, D)[:, 0, 0])
    d_norm_final = _colsum_partials(dgf_p)

    def swiglu_bwd(acc, gv, upv):
        s = _sigmoid(gv)
        return acc * upv * (s * (1.0 + gv * (1.0 - s))), acc * (gv * s)

    dgp, dup = _matmul("d_ffn_hidden", [(dh2b, w_down)], "nt", tm=TM, tn=FT, tk=1024,
                       outs=[("mn", MXU_DTYPE), ("mn", MXU_DTYPE)],
                       extras=[(gpre, *tile(TM, FT)), (up, *tile(TM, FT))], epilogue=swiglu_bwd)
    d_w_down = _matmul("dw_down", [(z, dh2b)], "tn", tm=FT, tn=1024, tk=1024, outs=grad_outs, epilogue=both)

    def norm_ffn_bwd(acc, h1_v, g, dres):
        dx, dg = _rmsnorm_bwd_vals(acc, h1_v, g)
        dh = dres + dx
        return dh, dh, dg

    names = ("w_ffn_down",)
    (dh1, dh1b, dg2_p), got = _matmul(
        "d_ffn_in", [(dgp, w_gate), (dup, w_up)], "nn", tm=BIG, tn=1024, tk=FT,
        outs=[("mn", F32), ("mn", MXU_DTYPE), ("pn", F32)],
        extras=[(h1, *tile(BIG, 1024)), (norm_ffn_g, *row_vec(1024)), (dh2, *tile(BIG, 1024))],
        epilogue=norm_ffn_bwd,
        comm=net.exchange(dict(w_ffn_down=[d_w_down])))
    net.received(names, (), got)
    d_norm_ffn = _colsum_partials(dg2_p)
    d_w_gate = _matmul("dw_gate", [(dgp, u2)], "tn", tm=FT, tn=1024, tk=1024, outs=grad_outs, epilogue=both)
    d_w_up = _matmul("dw_up", [(dup, u2)], "tn", tm=FT, tn=1024, tk=1024, outs=grad_outs, epilogue=both)

    def merge_bwd(acc, ya_v, yb_v, ga, gb):
        sa, sb = _sigmoid(ga), _sigmoid(gb)
        return acc * sa, acc * sb, acc * ya_v * sa * (1.0 - sa), acc * yb_v * sb * (1.0 - sb)

    dya, dyb, dga, dgb = _matmul(
        "d_merged", [(dh1b, w_out)], "nt", tm=TM, tn=1024, tk=1024, outs=[("mn", MXU_DTYPE)] * 4,
        extras=[(ya, *tile(TM, 1024)), (yb, *tile(TM, 1024)), gate_a, gate_b], epilogue=merge_bwd)
    tn_grad = functools.partial(_matmul, mode="tn", tn=1024, tk=1024, outs=grad_outs, epilogue=both)
    d_w_out = tn_grad("dw_out", [(merged, dh1b)], tm=1024)
    dy_attn, = _matmul("d_y_attn", [(dya, w_ba)], "nt", tm=TM, tn=1024, tk=1024, outs=[("mn", MXU_DTYPE)])
    dy_hgrn, = _matmul("d_y_hgrn", [(dyb, w_bh)], "nt", tm=TM, tn=1024, tk=1024, outs=[("mn", F32)])
    d_w_ba = tn_grad("dw_branch_a", [(y_attn, dya)], tm=1024)
    d_w_bh = tn_grad("dw_branch_b", [(y_hgrn, dyb)], tm=1024)

    names, swap = ("w_ffn_gate",), ("w_ffn_down",)
    (dq, dkv, dsink), got = _attn_bwd(pq, pkv, sinks, lse, dy_attn,
                                      comm=net.exchange(dict(w_ffn_gate=[d_w_gate]), swap))
    net.received(names, swap, got)
    names, swap = ("w_ffn_up", "w_out", "w_branch_attn", "w_branch_hgrn"), ("w_ffn_gate",)
    (dph, d_hgrn_norm, d_lb_logits), got = _hgrn_bwd(
        ph, o_raw, states, dy_hgrn, lb_logits, hgrn_norm_g,
        comm=net.exchange(dict(w_ffn_up=[d_w_up], w_out=[d_w_out], w_branch_attn=[d_w_ba],
                               w_branch_hgrn=[d_w_bh]), swap))
    net.received(names, swap, got)

    d_w_in = [tn_grad("dw_in_q", [(dq, u)], tm=1024, a_colsum=True),
              tn_grad("dw_in_kv", [(dkv, u)], tm=256, a_colsum=True),
              tn_grad("dw_in_h", [(dph, u)], tm=1024),
              tn_grad("dw_in_ga", [(dga, u)], tm=1024, a_colsum=True),
              tn_grad("dw_in_gb", [(dgb, u)], tm=1024, a_colsum=True)]

    def norm_mix_bwd(acc, xin, g, dres):
        dx, dg = _rmsnorm_bwd_vals(acc, xin, g)
        return dres + dx, dg

    db_h, got = _colsum("db_h", dph, comm=net.presum_begin("w_in", d_w_in))
    halves = net.presum_end("w_in", got)
    names, swap = ("w_in",), ("w_ffn_up", "w_out", "w_branch_attn", "w_branch_hgrn")
    (dx, dg1_p), got = _matmul(
        "d_u", [(dq, w_q), (dkv, w_kv), (dph, w_h), (dga, _Rows(w_in, o_g, D)), (dgb, _Rows(w_in, o_g + D, D))], "nn",
        tm=BIG, tn=1024, tk=512, outs=[("mn", F32), ("pn", F32)],
        extras=[(x, *tile(BIG, 1024)), (norm_mix_g, *row_vec(1024)), (dh1, *tile(BIG, 1024))],
        epilogue=norm_mix_bwd,
        comm=_join(halves, net.swap(swap)))
    net.last = (names, swap, got)
    d_norm_mix = _colsum_partials(dg1_p)
    d_b_in = jnp.concatenate([d_w_in[0][2], d_w_in[1][2], db_h, d_w_in[3][2], d_w_in[4][2]], axis=1)
    vecs = dict(norm_mix_g=d_norm_mix, b_in=d_b_in, attn_sinks=jnp.sum(dsink.reshape(Q_HEADS, ATTN_BLOCK), axis=1).reshape(1, Q_HEADS),
                hgrn_lb_logits=d_lb_logits,
                hgrn_norm_g=d_hgrn_norm, norm_ffn_g=d_norm_ffn, norm_final_g=d_norm_final)
    return loss, dx, vecs


def _place():
    return lax.axis_index("x"), lax.axis_index("y"), lax.axis_index("c")


def _other_chips(x, y):
    return [(1 - x, y), (x, 1 - y), (1 - x, 1 - y)]


def _y_first(copies):
    return [copies[3 * (i // 3) + (1, 0, 2)[i % 3]] for i in range(len(copies))]


def _gather_copies(shards):
    n = len(shards)

    def build(ins, outs, send_sems, recv_sems, local_sems):
        x, y, c = _place()
        mine = 2 * x + y
        local = [pltpu.make_async_copy(ins[w], outs[w].at[mine], local_sems.at[w]) for w in range(n)]
        sends, recvs = [], []
        for w in range(n):
            for k, (px, py) in enumerate(_other_chips(x, y)):
                sem = 3 * w + k
                sends.append(pltpu.make_async_remote_copy(
                    src_ref=ins[w], dst_ref=outs[w].at[mine], send_sem=send_sems.at[sem], recv_sem=recv_sems.at[sem],
                    device_id=(px, py, c), device_id_type=MESH_ID))
                recvs.append(pltpu.make_async_remote_copy(
                    src_ref=ins[w], dst_ref=outs[w].at[2 * px + py], send_sem=send_sems.at[sem],
                    recv_sem=recv_sems.at[sem], device_id=(px, py, c), device_id_type=MESH_ID))
        return sends, recvs, local, _y_first(sends)

    return _Carried(shards, [jax.ShapeDtypeStruct((N_CHIPS,) + s.shape, s.dtype) for s in shards], 3 * n, n, build)


def _grad_copies(stacked):
    n = len(stacked)

    def build(ins, outs, send_sems, recv_sems, local_sems):
        x, y, c = _place()
        sends = []
        for w in range(n):
            for k, (px, py) in enumerate(_other_chips(x, y)):
                sem = 3 * w + k
                sends.append(pltpu.make_async_remote_copy(
                    src_ref=ins[w].at[2 * px + py], dst_ref=outs[w].at[k], send_sem=send_sems.at[sem],
                    recv_sem=recv_sems.at[sem], device_id=(px, py, c), device_id_type=MESH_ID))
        return sends, sends, [], _y_first(sends)

    return _Carried(stacked, [jax.ShapeDtypeStruct((3,) + s.shape[1:], s.dtype) for s in stacked], 3 * n, 0, build)


def _small_copies(small):
    def build(ins, outs, send_sems, recv_sems, local_sems):
        small_ref, all_ref = ins[0], outs[0]
        x, y, c = _place()
        me = 4 * x + 2 * y + c
        sends, recvs = [], []
        for r in range(1, 8):
            px = 1 - x if r & 4 else x
            py = 1 - y if r & 2 else y
            pc = 1 - c if r & 1 else c
            sends.append(pltpu.make_async_remote_copy(
                src_ref=small_ref, dst_ref=all_ref.at[me], send_sem=send_sems.at[r - 1], recv_sem=recv_sems.at[r - 1],
                device_id=(px, py, pc), device_id_type=MESH_ID))
            recvs.append(pltpu.make_async_remote_copy(
                src_ref=small_ref, dst_ref=all_ref.at[4 * px + 2 * py + pc], send_sem=send_sems.at[r - 1],
                recv_sem=recv_sems.at[r - 1], device_id=(px, py, pc), device_id_type=MESH_ID))
        return sends, recvs, [pltpu.make_async_copy(small_ref, all_ref.at[me], local_sems.at[0])]

    return _Carried([small], [jax.ShapeDtypeStruct((8,) + small.shape, small.dtype)], 7, 1, build)


def _gather_by_neighbours(name, shard):
    half = shard.shape[0] // 2
    quarter = half // 2

    def body(in_ref, out_ref, send_sems, recv_sems, local_sem):
        for core in (0, 1):
            @pl.when(lax.axis_index("c") == core)
            def _():
                program(core, in_ref, out_ref, send_sems, recv_sems, local_sem)

    def program(c, in_ref, out_ref, send_sems, recv_sems, local_sem):
        x, y, _ = _place()
        chip = lambda px, py: 2 * px + py
        to_x, to_y, sibling = (1 - x, y, c), (x, 1 - y, c), (x, y, 1 - c)
        x_blk, y_blk, d_blk = chip(1 - x, y), chip(x, 1 - y), chip(1 - x, 1 - y)
        mine, theirs = c * half, (1 - c) * half

        def copy(sem, rows, block, to, src=None):
            place = out_ref.at[block, pl.ds(rows[0], rows[1])]
            return pltpu.make_async_remote_copy(
                src_ref=place if src is None else src, dst_ref=place, send_sem=send_sems.at[sem],
                recv_sem=recv_sems.at[sem], device_id=to, device_id_type=MESH_ID)

        own = pltpu.make_async_copy(in_ref, out_ref.at[chip(x, y)], local_sem)
        own.start()
        my_rows = in_ref.at[pl.ds(mine, half)]
        along_x = dict(send=copy(0, (mine, half), chip(x, y), to_x, src=my_rows),
                       landed=copy(0, (mine, half), x_blk, to_x),
                       onward=[copy(3, (mine + quarter, quarter), x_blk, to_y), copy(4, (mine, half), x_blk, sibling)],
                       diagonal=copy(2, (mine, quarter), d_blk, to_x))
        along_y = dict(send=copy(1, (mine, half), chip(x, y), to_y, src=my_rows),
                       landed=copy(1, (mine, half), y_blk, to_y),
                       onward=[copy(2, (mine, quarter), y_blk, to_x), copy(5, (mine, half), y_blk, sibling)],
                       diagonal=copy(3, (mine + quarter, quarter), d_blk, to_y))
        last = copy(6, (mine, half), d_blk, sibling)

        order = (along_x, along_y) if c == 0 else (along_y, along_x)
        for axis in order:
            axis["send"].start()
        for axis in order:
            axis["landed"].wait_recv()
            for cp in axis["onward"]:
                cp.start()
        for axis in order:
            axis["diagonal"].wait_recv()
        last.start()
        for sem, block in ((4, x_blk), (5, y_blk), (6, d_blk)):
            copy(sem, (theirs, half), block, sibling).wait_recv()
        for cp in [along_x["send"], along_y["send"]] + along_x["onward"] + along_y["onward"] + [last]:
            cp.wait_send()
        own.wait()

    return pl.pallas_call(
        body, name=name, in_specs=[HBM_SPEC], out_specs=HBM_SPEC,
        out_shape=jax.ShapeDtypeStruct((N_CHIPS,) + shard.shape, shard.dtype),
        scratch_shapes=[pltpu.SemaphoreType.DMA((7,)), pltpu.SemaphoreType.DMA((7,)), pltpu.SemaphoreType.DMA(())],
    )(shard)


def _copies_alone(name, comm):
    return _call(name, lambda: None, grid=(), in_specs=[], out_specs=[], out_shape=[], args=[], comm=comm)[1]


class _Net:
    def __init__(self, shards):
        self.shards = shards
        self.whole, self.own, self.theirs, self.sums, self.other = {}, {}, {}, {}, {}
        x, y, _ = _place()
        self.chip = 2 * x + y

    def gather(self, names):
        return _gather_copies([self.shards[n] for n in names])

    def gathered(self, names, got):
        for n, g in zip(names, got):
            self.whole[n] = g.reshape(-1, g.shape[-1])

    def full(self, name):
        return self.whole[name]

    def exchange(self, grads, swap=()):
        stacked = []
        for n, pieces in grads.items():
            keep = jnp.concatenate([p[0] for p in pieces], axis=0) if len(pieces) > 1 else pieces[0][0]
            send = jnp.concatenate([p[1] for p in pieces], axis=0) if len(pieces) > 1 else pieces[0][1]
            rows = keep.shape[0] // N_CHIPS
            self.own[n] = lax.dynamic_slice_in_dim(keep, self.chip * rows, rows, axis=0)
            stacked.append(send.reshape(N_CHIPS, rows, send.shape[-1]))
        return _join(_grad_copies(stacked), self.swap(swap))

    def swap(self, names):
        return _sibling_copies([self.sums[n] for n in names]) if names else None

    def presum_begin(self, name, pieces):
        keep = jnp.concatenate([p[0] for p in pieces], axis=0)
        send = jnp.concatenate([p[1] for p in pieces], axis=0)
        rows = keep.shape[0] // N_CHIPS
        self.held = keep.reshape(N_CHIPS, rows, keep.shape[-1])
        return _half_rows_copies(send.reshape(N_CHIPS, rows, send.shape[-1]))

    def presum_end(self, name, got):
        x, y, c = _place()
        to_send, self.own[name] = _pre_sum("presum_" + name, self.held, got[0], jnp.stack([c, self.chip]))
        return _grad_copies([to_send])

    def received(self, names, swap, got, carried=None):
        self.theirs.update(zip(names, got[:len(names)]))
        self.other.update(zip(swap, got[len(names):]))
        for n in names:
            (self.sums[n],), more = _partial_sum("sum_" + n, self.own[n], self.theirs[n], comm=carried)
        return more


def _half_rows_copies(stacked):
    n, rows = stacked.shape[0], stacked.shape[1] // 2

    def build(ins, outs, send_sems, recv_sems, local_sems):
        x, y, c = _place()
        copies = [pltpu.make_async_remote_copy(
            src_ref=ins[0].at[s, pl.ds((1 - c) * rows, rows)], dst_ref=outs[0].at[s], send_sem=send_sems.at[s],
            recv_sem=recv_sems.at[s], device_id=(x, y, 1 - c), device_id_type=MESH_ID) for s in range(n)]
        return copies, copies, []

    return _Carried([stacked], [jax.ShapeDtypeStruct((n, rows, stacked.shape[2]), stacked.dtype)], n, 0, build)


def _pre_sum(name, held, theirs, core_and_chip):
    n, R, C = held.shape
    half = R // 2
    tr = _row_tile(half)
    per_half = half // tr

    def body(place_ref, h_ref, t_ref, send_ref, own_ref):
        total = h_ref[0] + t_ref[0].astype(F32)
        send_ref[0] = total.astype(send_ref.dtype)

        @pl.when(pl.program_id(1) == place_ref[1])
        def _():
            own_ref[...] = total

    return pl.pallas_call(
        body, name=name,
        grid_spec=pltpu.PrefetchScalarGridSpec(
            num_scalar_prefetch=1, grid=(per_half, n),
            in_specs=[pl.BlockSpec((1, tr, C), lambda i, s, place: (s, place[0] * per_half + i, 0)),
                      pl.BlockSpec((1, tr, C), lambda i, s, place: (s, i, 0))],
            out_specs=[pl.BlockSpec((1, tr, C), lambda i, s, place: (s, i, 0)),
                       pl.BlockSpec((tr, C), lambda i, s, place: (i, 0))]),
        out_shape=[jax.ShapeDtypeStruct((n, half, C), MXU_DTYPE), jax.ShapeDtypeStruct((half, C), F32)],
        compiler_params=_params(("arbitrary", "arbitrary")),
    )(core_and_chip, held, theirs)


def _sibling_copies(parts):
    n = len(parts)

    def build(ins, outs, send_sems, recv_sems, local_sems):
        x, y, c = _place()
        copies = [pltpu.make_async_remote_copy(
            src_ref=ins[w], dst_ref=outs[w], send_sem=send_sems.at[w], recv_sem=recv_sems.at[w],
            device_id=(x, y, 1 - c), device_id_type=MESH_ID) for w in range(n)]
        return copies, copies, []

    return _Carried(parts, [jax.ShapeDtypeStruct(p.shape, p.dtype) for p in parts], n, 0, build)


def _row_tile(rows, most=512, sublanes=16):
    return max(t for t in range(sublanes, most + 1, sublanes) if rows % t == 0)


def _partial_sum(name, own, recv, comm=None):
    R, C = own.shape
    tr = _row_tile(R, most=128)

    def body(o_ref, r_ref, p_ref):
        p_ref[...] = ((o_ref[...] + r_ref[0].astype(F32)) + r_ref[1].astype(F32)) + r_ref[2].astype(F32)

    return _call(name, body, grid=(R // tr,),
                 in_specs=[pl.BlockSpec((tr, C), lambda i: (i, 0)), pl.BlockSpec((3, tr, C), lambda i: (0, i, 0))],
                 out_specs=[pl.BlockSpec((tr, C), lambda i: (i, 0))], out_shape=[jax.ShapeDtypeStruct((R, C), F32)],
                 args=[own, recv], semantics=("parallel",), comm=comm)


def _adam_vals(w, g, m, v):
    m = ADAM_B1 * m + (1.0 - ADAM_B1) * g
    v = ADAM_B2 * v + (1.0 - ADAM_B2) * (g * g)
    m_hat = m / (1.0 - ADAM_B1 ** ADAM_STEP)
    v_hat = v / (1.0 - ADAM_B2 ** ADAM_STEP)
    delta = -ADAM_LR * (m_hat / (jnp.sqrt(v_hat) + ADAM_EPS) + ADAM_WD * w)
    return delta, m, v


def _adamw(name, w, m, v, mine, other, comm=None):
    R, C = w.shape
    tr = _row_tile(R)

    def body(w_ref, m_ref, v_ref, s_ref, n_ref, g_ref, d_ref, nm_ref, nv_ref):
        g = s_ref[...] + n_ref[...]
        d, nm, nv = _adam_vals(w_ref[...], g, m_ref[...], v_ref[...])
        g_ref[...], d_ref[...], nm_ref[...], nv_ref[...] = g, d, nm, nv

    spec = pl.BlockSpec((tr, C), lambda i: (i, 0))
    return _call(name, body, grid=(R // tr,), in_specs=[spec] * 5, out_specs=[spec] * 4,
                 out_shape=[jax.ShapeDtypeStruct((R, C), F32)] * 4, args=[w, m, v, mine, other],
                 semantics=("parallel",), comm=comm)


def _adamw_by_halves(name, w, m, v, mine, other, core):
    R, C = w.shape
    tr = _row_tile(R // 2)
    per_half = R // 2 // tr

    def body(c_ref, w_ref, m_ref, v_ref, s_ref, n_ref, g_ref, d_ref, nm_ref, nv_ref):
        g = jnp.where(pl.program_id(0) // per_half == c_ref[0, 0], s_ref[...], n_ref[...])
        d, nm, nv = _adam_vals(w_ref[...], g, m_ref[...], v_ref[...])
        g_ref[...], d_ref[...], nm_ref[...], nv_ref[...] = g, d, nm, nv

    spec = pl.BlockSpec((tr, C), lambda i: (i, 0))
    part = pl.BlockSpec((tr, C), lambda i: (i % per_half, 0))
    return pl.pallas_call(
        body, name=name, grid=(R // tr,),
        in_specs=[pl.BlockSpec(memory_space=pltpu.SMEM), spec, spec, spec, part, part], out_specs=[spec] * 4,
        out_shape=[jax.ShapeDtypeStruct((R, C), F32)] * 4, compiler_params=_params(("parallel",)),
    )(core, w, m, v, mine, other)


def _adamw_small(w, m, v, g_all):
    def body(w_ref, m_ref, v_ref, a_ref, g_ref, d_ref, nm_ref, nv_ref):
        g = a_ref[0]
        for dev in range(1, 8):
            g = g + a_ref[dev]
        d, nm, nv = _adam_vals(w_ref[...], g, m_ref[...], v_ref[...])
        g_ref[...], d_ref[...], nm_ref[...], nv_ref[...] = g, d, nm, nv

    return pl.pallas_call(
        body, name="adamw_small", out_shape=[jax.ShapeDtypeStruct(w.shape, F32)] * 4,
    )(w, m, v, g_all)


SMALL_LAYOUT = dict(norm_mix_g=(0, 1024), b_in=(1024, 7424), hgrn_norm_g=(8448, 1024), norm_ffn_g=(9472, 1024),
                    norm_final_g=(10496, 1024), hgrn_lb_logits=(11520, 2048), attn_sinks=(13568, 16))
SMALL_LOSS, SMALL_SIZE = 13584, 16 * D_MODEL


def _pack_small(vals, loss=None):
    names = list(SMALL_LAYOUT)
    tail = jnp.zeros((SMALL_SIZE - SMALL_LAYOUT[names[-1]][0],), F32)
    tail = lax.dynamic_update_slice(tail, vals[names[-1]].astype(F32).reshape(-1), (0,))
    if loss is not None:
        tail = lax.dynamic_update_slice(tail, loss.astype(F32).reshape(1), (SMALL_LOSS - SMALL_LAYOUT[names[-1]][0],))
    flat = jnp.concatenate([vals[n].astype(F32).reshape(-1) for n in names[:-1]] + [tail])
    return flat.reshape(-1, D_MODEL)


def _unpack_small(packed, shapes):
    flat = packed.reshape(-1)
    return {name: flat[lo:lo + n].reshape(shapes[name]) for name, (lo, n) in SMALL_LAYOUT.items()}


MATRICES = ("w_in", "w_branch_attn", "w_branch_hgrn", "w_out", "w_ffn_gate", "w_ffn_up", "w_ffn_down")
COLUMN_SHARDED = ("w_in", "w_ffn_gate", "w_ffn_up")
WEIGHTS = ("norm_mix_g", "w_in", "b_in", "attn_sinks", "hgrn_lb_logits", "hgrn_norm_g", "w_branch_attn",
           "w_branch_hgrn", "w_out", "norm_ffn_g", "w_ffn_gate", "w_ffn_up", "w_ffn_down", "norm_final_g")


def kernel(x, norm_mix_g, w_in, b_in, attn_sinks, hgrn_lb_logits, hgrn_norm_g, w_branch_attn, w_branch_hgrn, w_out, norm_ffn_g, w_ffn_gate, w_ffn_up, w_ffn_down, norm_final_g, loss_target, m_norm_mix_g, m_w_in, m_b_in, m_attn_sinks, m_hgrn_lb_logits, m_hgrn_norm_g, m_w_branch_attn, m_w_branch_hgrn, m_w_out, m_norm_ffn_g, m_w_ffn_gate, m_w_ffn_up, m_w_ffn_down, m_norm_final_g, v_norm_mix_g, v_w_in, v_b_in, v_attn_sinks, v_hgrn_lb_logits, v_hgrn_norm_g, v_w_branch_attn, v_w_branch_hgrn, v_w_out, v_norm_ffn_g, v_w_ffn_gate, v_w_ffn_up, v_w_ffn_down, v_norm_final_g):
    given = dict(locals())
    w = {n: given[n] for n in WEIGHTS}
    m = {n: given["m_" + n] for n in WEIGHTS}
    v = {n: given["v_" + n] for n in WEIGHTS}

    block = lambda a, n: jnp.transpose(a[0]) if n in COLUMN_SHARDED else a[0]
    unblock = lambda a, n: (jnp.transpose(a) if n in COLUMN_SHARDED else a)[None]
    net = _Net({n: block(w[n], n).astype(MXU_DTYPE) for n in MATRICES})
    net.gathered(("w_in",), [_gather_by_neighbours("gather_w_in", net.shards["w_in"])])
    vec = dict(norm_mix_g=norm_mix_g, b_in=b_in, attn_sinks=attn_sinks, hgrn_lb_logits=hgrn_lb_logits,
               hgrn_norm_g=hgrn_norm_g, norm_ffn_g=norm_ffn_g, norm_final_g=norm_final_g.reshape(1, D_MODEL))
    loss_part, dx, d_vecs = _local_step(x[0], loss_target[0], vec, net)

    small_all, = net.received(*net.last, carried=_small_copies(_pack_small(d_vecs, loss_part)))
    grads, deltas, new_m, new_v = {}, {}, {}, {}
    for n in ("w_ffn_down", "w_ffn_gate", "w_ffn_up", "w_out", "w_branch_attn", "w_branch_hgrn"):
        res, got = _adamw("adamw_" + n, block(w[n], n), block(m[n], n), block(v[n], n), net.sums[n], net.other[n],
                          comm=net.swap(("w_in",)) if n == "w_ffn_down" else None)
        if n == "w_ffn_down":
            net.other["w_in"], = got
        grads[n], deltas[n], new_m[n], new_v[n] = (unblock(r, n) for r in res)
    n = "w_in"
    res = _adamw_by_halves("adamw_" + n, block(w[n], n), block(m[n], n), block(v[n], n), net.sums[n], net.other[n],
                           _place()[2].reshape(1, 1))
    grads[n], deltas[n], new_m[n], new_v[n] = (unblock(r, n) for r in res)
    shapes = {n: w[n].shape for n in SMALL_LAYOUT}
    res = _adamw_small(_pack_small(w), _pack_small(m), _pack_small(v), small_all)
    for dst, packed in zip((grads, deltas, new_m, new_v), res):
        dst.update(_unpack_small(packed, shapes))

    loss = res[0].reshape(-1)[SMALL_LOSS]
    return (loss, dx[None], *[grads[n] for n in WEIGHTS], *[deltas[n] for n in WEIGHTS],
            *[new_m[n] for n in WEIGHTS], *[new_v[n] for n in WEIGHTS])
```

```python
import collections
import functools
import math

import jax
import jax.numpy as jnp
from jax import lax
from jax.experimental import pallas as pl
from jax.experimental.pallas import tpu as pltpu

F32 = jnp.float32
BF16 = jnp.bfloat16
MXU_DTYPE = jnp.bfloat16
SAVED_DTYPE = jnp.bfloat16
MESH_ID = pl.DeviceIdType.MESH

D_MODEL = 1024
HEAD_DIM = 64
Q_HEADS = 16
KV_HEADS = 2
GROUP = Q_HEADS // KV_HEADS
KV_WIDTH = KV_HEADS * HEAD_DIM
ATTN_BLOCK = 128
HGRN_HEADS = 8
HGRN_K = 128
CHUNK = 64
HGRN_TOKENS = 256
FFN = 2816
IN_SPLITS = (1024, 256, 4096, 2048)
EPS = 1e-6
NEG_INF = -1e30
ADAM_LR, ADAM_B1, ADAM_B2, ADAM_EPS, ADAM_WD, ADAM_STEP = 0.001, 0.9, 0.999, 1e-08, 0.01, 10
N_CHIPS = 4
VMEM_LIMIT = 60 * 1024 * 1024
ROW_ALIGN = 16


def _params(sem=None):
    return pltpu.CompilerParams(dimension_semantics=sem, vmem_limit_bytes=VMEM_LIMIT)


def _sigmoid(v):
    return 0.5 * jnp.tanh(0.5 * v) + 0.5


def _dot(a, b, dims):
    return lax.dot_general(a.astype(MXU_DTYPE), b.astype(MXU_DTYPE), (dims, ((), ())),
                           preferred_element_type=F32)


def _nn(a, b):
    return _dot(a, b, ((1,), (0,)))


def _nt(a, b):
    return _dot(a, b, ((1,), (1,)))


def _tn(a, b):
    return _dot(a, b, ((0,), (0,)))


HBM_SPEC = pl.BlockSpec(memory_space=pl.ANY)


class _Carried:
    def __init__(self, arrays, out_shapes, n_remote, n_local, build):
        self.parts = [(len(arrays), len(out_shapes), build)]
        self.arrays, self.out_shapes = list(arrays), list(out_shapes)
        self.scratch = [pltpu.SemaphoreType.DMA((n_remote,)), pltpu.SemaphoreType.DMA((n_remote,)),
                        pltpu.SemaphoreType.DMA((max(n_local, 1),))]

    def __add__(self, other):
        both = _Carried([], [], 1, 0, None)
        both.parts = self.parts + other.parts
        both.arrays, both.out_shapes = self.arrays + other.arrays, self.out_shapes + other.out_shapes
        both.scratch = self.scratch + other.scratch
        return both

    def _built(self, ins, outs, sems):
        for p, (ni, no, build) in enumerate(self.parts):
            yield build(ins[:ni], outs[:no], *sems[3 * p:3 * p + 3])
            ins, outs = ins[ni:], outs[no:]

    def start(self, ins, outs, sems):
        core = lax.axis_index("c")
        for sends, _, local, *other_order in self._built(ins, outs, sems):
            for cp in local:
                cp.start()
            if not other_order:
                for cp in sends:
                    cp.start()
                continue

            @pl.when(core == 0)
            def _():
                for cp in sends:
                    cp.start()

            @pl.when(core == 1)
            def _():
                for cp in other_order[0]:
                    cp.start()

    def wait(self, ins, outs, sems):
        for sends, recvs, local, *_ in self._built(ins, outs, sems):
            for cp in recvs:
                cp.wait_recv()
            for cp in sends:
                cp.wait_send()
            for cp in local:
                cp.wait()


def _join(*comms):
    comms = [c for c in comms if c is not None]
    return functools.reduce(lambda a, b: a + b, comms) if comms else None


def _call(name, body, *, grid, in_specs, out_specs, out_shape, args, scratch=(), semantics=None, comm=None):
    n_in, n_out, n_scr = len(in_specs), len(out_specs), len(scratch)
    if comm is None:
        res = pl.pallas_call(body, name=name, grid=grid, in_specs=in_specs, out_specs=out_specs, out_shape=out_shape,
                             scratch_shapes=list(scratch), compiler_params=_params(semantics))(*args)
        return list(res), []
    ci, co = len(comm.arrays), len(comm.out_shapes)

    def carrying(*refs):
        ins, refs = refs[:n_in], refs[n_in:]
        c_ins, refs = refs[:ci], refs[ci:]
        outs, refs = refs[:n_out], refs[n_out:]
        c_outs, refs = refs[:co], refs[co:]
        scr, sems = refs[:n_scr], refs[n_scr:]
        if not grid:
            comm.start(c_ins, c_outs, sems)
            body(*ins, *outs, *scr)
            comm.wait(c_ins, c_outs, sems)
            return
        first = functools.reduce(jnp.logical_and, [pl.program_id(a) == 0 for a in range(len(grid))])
        last = functools.reduce(jnp.logical_and, [pl.program_id(a) == g - 1 for a, g in enumerate(grid)])

        @pl.when(first)
        def _():
            comm.start(c_ins, c_outs, sems)

        body(*ins, *outs, *scr)

        @pl.when(last)
        def _():
            comm.wait(c_ins, c_outs, sems)

    res = pl.pallas_call(
        carrying, name=name, grid=grid, in_specs=list(in_specs) + [HBM_SPEC] * ci,
        out_specs=list(out_specs) + [HBM_SPEC] * co, out_shape=list(out_shape) + comm.out_shapes,
        scratch_shapes=list(scratch) + comm.scratch,
        compiler_params=_params(("arbitrary",) * len(grid) if grid else None),
    )(*args, *comm.arrays)
    return list(res[:n_out]), list(res[n_out:])


_NO_COPIES = object()
_Rows = collections.namedtuple("_Rows", "array first rows")


def _matmul(name, pairs, mode, *, tm, tn, tk, outs, extras=(), epilogue=None, a_colsum=False, comm=_NO_COPIES):
    prod = dict(nn=_nn, nt=_nt, tn=_tn)[mode]
    pairs = [(a, b if isinstance(b, _Rows) else _Rows(b, 0, b.shape[0])) for a, b in pairs]
    a0, b0 = pairs[0]
    M = a0.shape[1] if mode == "tn" else a0.shape[0]
    N = b0.rows if mode == "nt" else b0.array.shape[1]
    steps, in_specs, offset = [], [], 0
    for a, b in pairs:
        K = a.shape[0] if mode == "tn" else a.shape[1]
        t = min(tk, K)
        assert K % t == 0, (name, K, t)
        kmap = functools.partial(lambda k, off, n: jnp.clip(k - off, 0, n - 1), off=offset, n=K // t)
        if mode == "tn":
            in_specs.append(pl.BlockSpec((t, tm), functools.partial(lambda i, j, k, f: (f(k), i), f=kmap)))
        else:
            in_specs.append(pl.BlockSpec((tm, t), functools.partial(lambda i, j, k, f: (i, f(k)), f=kmap)))
        whole = b.first == 0 and b.rows == b.array.shape[0]
        if mode == "nt":
            shape = (tn, t) if whole else (pl.Element(tn), pl.Element(t))
            in_specs.append(pl.BlockSpec(shape, functools.partial(
                lambda i, j, k, f, b, t, whole: (j, f(k)) if whole else (
                    pl.multiple_of(b.first + j * tn, ROW_ALIGN), pl.multiple_of(f(k) * t, 128)),
                f=kmap, b=b, t=t, whole=whole)))
        else:
            assert b.rows == K, (name, b.rows, K)
            shape = (t, tn) if whole else (pl.Element(t), pl.Element(tn))
            in_specs.append(pl.BlockSpec(shape, functools.partial(
                lambda i, j, k, f, b, t, whole: (f(k), j) if whole else (
                    pl.multiple_of(b.first + f(k) * t, ROW_ALIGN), pl.multiple_of(j * tn, 128)),
                f=kmap, b=b, t=t, whole=whole)))
        steps.append((offset, offset + K // t))
        offset += K // t
    assert M % tm == 0 and N % tn == 0, (name, M, N, tm, tn)
    ni, nj, nk = M // tm, N // tn, offset
    npair, ne, no = len(pairs), len(extras), len(outs)
    if epilogue is None:
        epilogue = lambda acc: (acc,)

    def finish(acc, extra_refs, out_refs):
        vals = epilogue(acc, *[r[...] for r in extra_refs])
        for (kind, _), o_ref, val in zip(outs, out_refs, vals):
            if kind == "pn":
                val = jnp.broadcast_to(val, o_ref.shape)
            o_ref[...] = val.astype(o_ref.dtype)

    def body(*refs):
        ab, rest = refs[:2 * npair], refs[2 * npair:]
        extra_refs, out_refs = rest[:ne], rest[ne:ne + no]
        if nk == 1:
            finish(prod(ab[0][...], ab[1][...]), extra_refs, out_refs)
            return
        acc_ref = rest[-1]
        k = pl.program_id(2)

        @pl.when(k == 0)
        def _():
            acc_ref[...] = jnp.zeros_like(acc_ref)
            if a_colsum:
                rest[ne + no][...] = jnp.zeros((1, tm), F32)

        if a_colsum:
            rest[ne + no][...] += jnp.sum(ab[0][...].astype(F32), axis=0, keepdims=True)
        for p, (lo, hi) in enumerate(steps):
            @pl.when(jnp.logical_and(k >= lo, k < hi))
            def _():
                acc_ref[...] += prod(ab[2 * p][...], ab[2 * p + 1][...])

        @pl.when(k == nk - 1)
        def _():
            finish(acc_ref[...], extra_refs, out_refs)

    for _, shape, im in extras:
        in_specs.append(pl.BlockSpec(shape, functools.partial(lambda i, j, k, im: im(i, j), im=im)))
    out_shape, out_specs = [], []
    for kind, dt in outs:
        if kind == "mn":
            out_shape.append(jax.ShapeDtypeStruct((M, N), dt))
            out_specs.append(pl.BlockSpec((tm, tn), lambda i, j, k: (i, j)))
        else:
            out_shape.append(jax.ShapeDtypeStruct((8 * ni, N), dt))
            out_specs.append(pl.BlockSpec((8, tn), lambda i, j, k: (i, j)))
    if a_colsum:
        assert mode == "tn" and npair == 1 and nj == 1 and nk > 1, name
        out_shape.append(jax.ShapeDtypeStruct((1, M), F32))
        out_specs.append(pl.BlockSpec((1, tm), lambda i, j, k: (0, i)))
    grid = (ni, nj, nk)
    if nj > 1 and nk == 1:
        turned = lambda spec: pl.BlockSpec(spec.block_shape, functools.partial(
            lambda j, i, k, im: im(i, j, k), im=spec.index_map))
        in_specs, out_specs, grid = [turned(s) for s in in_specs], [turned(s) for s in out_specs], (nj, ni, nk)
    res, got = _call(name, body, grid=grid, in_specs=in_specs, out_specs=out_specs, out_shape=out_shape,
                     args=[t for a, b in pairs for t in (a, b.array)] + [e[0] for e in extras],
                     scratch=[pltpu.VMEM((tm, tn), F32)] if nk > 1 else [],
                     semantics=("parallel", "parallel", "arbitrary"), comm=None if comm is _NO_COPIES else comm)
    return res if comm is _NO_COPIES else (res, got)


def _swiglu_fwd(u, w_gate_t, w_up_t, *, tm, tn, comm=None):
    (T, D), F = u.shape, w_gate_t.shape[0]

    def body(u_ref, wg_ref, wu_ref, g_ref, up_ref, z_ref):
        g, up = _nt(u_ref[...], wg_ref[...]), _nt(u_ref[...], wu_ref[...])
        g_ref[...], up_ref[...] = g.astype(g_ref.dtype), up.astype(up_ref.dtype)
        z_ref[...] = (g * _sigmoid(g) * up).astype(z_ref.dtype)

    w_spec = pl.BlockSpec((tn, D), lambda j, i: (j, 0))
    o_spec = pl.BlockSpec((tm, tn), lambda j, i: (i, j))
    return _call("ffn_hidden", body, grid=(F // tn, T // tm),
                 in_specs=[pl.BlockSpec((tm, D), lambda j, i: (i, 0)), w_spec, w_spec], out_specs=[o_spec] * 3,
                 out_shape=[jax.ShapeDtypeStruct((T, F), SAVED_DTYPE)] * 2 + [jax.ShapeDtypeStruct((T, F), MXU_DTYPE)],
                 args=[u, w_gate_t, w_up_t], semantics=("parallel", "parallel"), comm=comm)


def _colsum_partials(p):
    return jnp.sum(p.reshape(-1, 8, p.shape[-1])[:, 0, :], axis=0, keepdims=True)


def _rmsnorm_fwd(name, x, g, tr=512):
    T, D = x.shape

    def body(x_ref, g_ref, u_ref):
        xv = x_ref[...]
        r = lax.rsqrt(jnp.mean(xv * xv, axis=-1, keepdims=True) + EPS)
        u_ref[...] = (xv * r * g_ref[...]).astype(u_ref.dtype)

    return pl.pallas_call(
        body, name=name, grid=(T // tr,),
        in_specs=[pl.BlockSpec((tr, D), lambda i: (i, 0)), pl.BlockSpec((1, D), lambda i: (0, 0))],
        out_specs=pl.BlockSpec((tr, D), lambda i: (i, 0)),
        out_shape=jax.ShapeDtypeStruct((T, D), MXU_DTYPE),
        compiler_params=_params(("parallel",)),
    )(x, g)


def _rmsnorm_bwd_vals(dy, xin, g):
    rstd = lax.rsqrt(jnp.mean(xin * xin, axis=-1, keepdims=True) + EPS)
    xhat = xin * rstd
    dg = jnp.sum(dy * xhat, axis=0, keepdims=True)
    dxh = dy * g
    dx = rstd * (dxh - xhat * jnp.mean(dxh * xhat, axis=-1, keepdims=True))
    return dx, dg


def _colsum(name, a, tr=512, comm=_NO_COPIES):
    T, N = a.shape

    def body(a_ref, o_ref):
        @pl.when(pl.program_id(0) == 0)
        def _():
            o_ref[...] = jnp.zeros_like(o_ref)

        o_ref[...] += jnp.sum(a_ref[...].astype(F32), axis=0, keepdims=True)

    (res,), got = _call(name, body, grid=(T // tr,), in_specs=[pl.BlockSpec((tr, N), lambda i: (i, 0))],
                        out_specs=[pl.BlockSpec((1, N), lambda i: (0, 0))],
                        out_shape=[jax.ShapeDtypeStruct((1, N), F32)], args=[a], semantics=("arbitrary",),
                        comm=None if comm is _NO_COPIES else comm)
    return res if comm is _NO_COPIES else (res, got)


ATTN_SCALE = 1.0 / math.sqrt(HEAD_DIM)
GROUP_LANES = GROUP * ATTN_BLOCK
PAIR = 2 * HEAD_DIM


def _attn_mask():
    kj = lax.broadcasted_iota(jnp.int32, (ATTN_BLOCK, GROUP_LANES), 0)
    qi = lax.broadcasted_iota(jnp.int32, (ATTN_BLOCK, GROUP_LANES), 1) & (ATTN_BLOCK - 1)
    return kj <= qi


def _heads_transposed(ref, g, scale=None):
    parts = []
    for a in range(GROUP // 2):
        lo = (g * GROUP // 2 + a) * PAIR
        pair = ref[:, lo:lo + PAIR].astype(F32)
        pair = (pair if scale is None else pair * scale).T
        parts += [pair[:HEAD_DIM], pair[HEAD_DIM:]]
    return jnp.concatenate(parts, axis=1).astype(MXU_DTYPE)


def _heads_back(ref, g, vt):
    for a in range(GROUP // 2):
        lo = (g * GROUP // 2 + a) * PAIR
        pair = jnp.concatenate([vt[:, (2 * a) * ATTN_BLOCK:(2 * a + 1) * ATTN_BLOCK],
                                vt[:, (2 * a + 1) * ATTN_BLOCK:(2 * a + 2) * ATTN_BLOCK]], axis=0)
        ref[:, lo:lo + PAIR] = pair.T.astype(ref.dtype)


def _kv_parts(kv_ref, g):
    ks = slice(g * HEAD_DIM, (g + 1) * HEAD_DIM)
    vs = slice(KV_WIDTH + g * HEAD_DIM, KV_WIDTH + (g + 1) * HEAD_DIM)
    return kv_ref[:, ks].astype(MXU_DTYPE), kv_ref[:, vs].astype(MXU_DTYPE)


def _sink_rows(sinks):
    return jnp.repeat(sinks.reshape(KV_HEADS, GROUP), ATTN_BLOCK, axis=1)


def _attn_fwd(pq, pkv, sinks, comm=None):
    T = pq.shape[0]
    nb = T // ATTN_BLOCK

    def body(q_ref, kvc_ref, kvp_ref, s_ref, y_ref, lse_ref):
        mask_c = _attn_mask()
        has_prev = pl.program_id(0) > 0
        for g in range(KV_HEADS):
            (kc, vc), (kp, vp) = _kv_parts(kvc_ref, g), _kv_parts(kvp_ref, g)
            qt = _heads_transposed(q_ref, g, ATTN_SCALE)
            s = jnp.where(mask_c, _nn(kc, qt), jnp.where(has_prev, _nn(kp, qt), NEG_INF))
            sink = s_ref[g:g + 1, :]
            m = jnp.maximum(jnp.max(s, axis=0, keepdims=True), sink)
            p = jnp.exp(s - m)
            den = jnp.sum(p, axis=0, keepdims=True) + jnp.exp(sink - m)
            pc = jnp.where(mask_c, p, 0.0)
            _heads_back(y_ref, g, (_tn(vc, pc) + _tn(vp, p - pc)) / den)
            lse = m + jnp.log(den)
            for i in range(GROUP):
                lse_ref[g * GROUP + i:g * GROUP + i + 1, :] = lse[:, i * ATTN_BLOCK:(i + 1) * ATTN_BLOCK]

    return _call(
        "attn_fwd", body, grid=(nb,),
        in_specs=[pl.BlockSpec((ATTN_BLOCK, D_MODEL), lambda n: (n, 0)),
                  pl.BlockSpec((ATTN_BLOCK, 2 * KV_WIDTH), lambda n: (n, 0)),
                  pl.BlockSpec((ATTN_BLOCK, 2 * KV_WIDTH), lambda n: (jnp.maximum(n - 1, 0), 0)),
                  pl.BlockSpec((KV_HEADS, GROUP_LANES), lambda n: (0, 0))],
        out_specs=[pl.BlockSpec((ATTN_BLOCK, D_MODEL), lambda n: (n, 0)),
                   pl.BlockSpec((Q_HEADS, ATTN_BLOCK), lambda n: (0, n))],
        out_shape=[jax.ShapeDtypeStruct((T, D_MODEL), MXU_DTYPE), jax.ShapeDtypeStruct((Q_HEADS, T), F32)],
        args=[pq, pkv, pkv, _sink_rows(sinks)], semantics=("parallel",), comm=comm)


def _attn_bwd(pq, pkv, sinks, lse, dy, comm=None):
    T = pq.shape[0]
    nb = T // ATTN_BLOCK
    cur = lambda n: (jnp.minimum(n, nb - 1), 0)

    def body(q_ref, kvc_ref, kvp_ref, s_ref, lse_ref, dy_ref, dq_ref, dkv_ref, ds_ref, carry, top, bot):
        n = pl.program_id(0)

        @pl.when(n == 0)
        def _():
            carry[...] = jnp.zeros_like(carry)
            ds_ref[...] = jnp.zeros_like(ds_ref)

        @pl.when(n < nb)
        def _():
            mask_c = _attn_mask()
            valid = jnp.logical_or(mask_c, n > 0)
            for g in range(KV_HEADS):
                ks = slice(g * HEAD_DIM, (g + 1) * HEAD_DIM)
                vs = slice(KV_WIDTH + g * HEAD_DIM, KV_WIDTH + (g + 1) * HEAD_DIM)
                (kc, vc), (kp, vp) = _kv_parts(kvc_ref, g), _kv_parts(kvp_ref, g)
                qt = _heads_transposed(q_ref, g, ATTN_SCALE)
                dot = _heads_transposed(dy_ref, g)
                lse = jnp.concatenate([lse_ref[g * GROUP + i:g * GROUP + i + 1, :] for i in range(GROUP)], axis=1)
                p = jnp.where(valid, jnp.exp(jnp.where(mask_c, _nn(kc, qt), _nn(kp, qt)) - lse), 0.0)
                dp = jnp.where(mask_c, _nn(vc, dot), _nn(vp, dot))
                delta = jnp.sum(p * dp, axis=0, keepdims=True)
                ds = p * (dp - delta)
                ds_c, p_c = jnp.where(mask_c, ds, 0.0), jnp.where(mask_c, p, 0.0)
                ds_p, p_p = ds - ds_c, p - p_c
                _heads_back(dq_ref, g, (_tn(kc, ds_c) + _tn(kp, ds_p)) * ATTN_SCALE)
                bot[:, ks], bot[:, vs] = _nt(ds_c, qt), _nt(p_c, dot)
                top[:, ks], top[:, vs] = _nt(ds_p, qt), _nt(p_p, dot)
                ds_ref[g:g + 1, :] -= jnp.exp(s_ref[g:g + 1, :] - lse) * delta
            dkv_ref[...] = (carry[...] + top[...]).astype(dkv_ref.dtype)
            carry[...] = bot[...]

        @pl.when(n == nb)
        def _():
            dkv_ref[...] = carry[...].astype(dkv_ref.dtype)

    return _call(
        "attn_bwd", body, grid=(nb + 1,),
        in_specs=[pl.BlockSpec((ATTN_BLOCK, D_MODEL), cur),
                  pl.BlockSpec((ATTN_BLOCK, 2 * KV_WIDTH), cur),
                  pl.BlockSpec((ATTN_BLOCK, 2 * KV_WIDTH), lambda n: (jnp.maximum(jnp.minimum(n, nb - 1) - 1, 0), 0)),
                  pl.BlockSpec((KV_HEADS, GROUP_LANES), lambda n: (0, 0)),
                  pl.BlockSpec((Q_HEADS, ATTN_BLOCK), lambda n: (0, jnp.minimum(n, nb - 1))),
                  pl.BlockSpec((ATTN_BLOCK, D_MODEL), cur)],
        out_specs=[pl.BlockSpec((ATTN_BLOCK, D_MODEL), cur),
                   pl.BlockSpec((ATTN_BLOCK, 2 * KV_WIDTH), lambda n: (jnp.maximum(n - 1, 0), 0)),
                   pl.BlockSpec((KV_HEADS, GROUP_LANES), lambda n: (0, 0))],
        out_shape=[jax.ShapeDtypeStruct((T, D_MODEL), MXU_DTYPE),
                   jax.ShapeDtypeStruct((T, 2 * KV_WIDTH), MXU_DTYPE),
                   jax.ShapeDtypeStruct((KV_HEADS, GROUP_LANES), F32)],
        scratch=[pltpu.VMEM((ATTN_BLOCK, 2 * KV_WIDTH), F32)] * 3,
        args=[pq, pkv, pkv, _sink_rows(sinks), lse, dy], semantics=("arbitrary",), comm=comm)


def _lower_bound(l):
    m = jnp.maximum(l[0:1], l[1:2])
    e0, e1 = jnp.exp(l[0:1] - m), jnp.exp(l[1:2] - m)
    return e0 / (e0 + e1)


def _tri(lower):
    r = lax.broadcasted_iota(jnp.int32, (CHUNK, CHUNK), 0)
    c = lax.broadcasted_iota(jnp.int32, (CHUNK, CHUNK), 1)
    return (r >= c) if lower else (c >= r)


def _chunk_sum(mask, v):
    ones = mask.astype(BF16)
    hi = v.astype(BF16)
    rest = v - hi.astype(F32)
    mid = rest.astype(BF16)
    lo = (rest - mid.astype(F32)).astype(BF16)
    part = lambda t: lax.dot_general(ones, t, (((1,), (0,)), ((), ())), preferred_element_type=F32)
    return part(hi) + part(mid) + part(lo)


def _hgrn_chunk_inputs(hq, hf, lb, causal):
    half_t = 0.5 * jnp.tanh(0.5 * hf)
    sg, sgn = 0.5 + half_t, 0.5 - half_t
    f = lb + (1.0 - lb) * sg
    kk = (1.0 - lb) * sgn
    sq = _sigmoid(hq)
    q = hq * sq
    b = _chunk_sum(causal, jnp.log(f))
    bm, bl = b[CHUNK // 2 - 1:CHUNK // 2, :], b[CHUNK - 1:CHUNK, :]
    e_qm, e_km = jnp.exp(b - bm), jnp.exp(bm - b)
    e_qs, e_kl = e_qm * jnp.exp(bm), e_km * jnp.exp(bl - bm)
    return dict(sg=sg, sgn=sgn, f=f, kk=kk, sq=sq, q=q, e_qm=e_qm, e_km=e_km, e_qs=e_qs, e_kl=e_kl,
                qm=q * e_qm, km=kk * e_km, qs=q * e_qs, kl=kk * e_kl, el=jnp.exp(bl))


def _hgrn_fwd(ph, lb_logits, norm_g, comm=None):
    T = ph.shape[0]
    nblk, cpb = T // HGRN_TOKENS, HGRN_TOKENS // CHUNK
    col = lambda c: pl.BlockSpec((HGRN_TOKENS, D_MODEL), functools.partial(lambda i, c: (i, c), c=c))

    def body(hq_ref, hf_ref, hi_ref, hg_ref, l_ref, ng_ref, y_ref, o_ref, st_ref, s_ref):
        @pl.when(pl.program_id(0) == 0)
        def _():
            s_ref[...] = jnp.zeros_like(s_ref)

        lb = _lower_bound(l_ref[...])
        causal = _tri(True)
        for c in range(cpb):
            rows = slice(c * CHUNK, (c + 1) * CHUNK)
            t = _hgrn_chunk_inputs(hq_ref[rows, :], hf_ref[rows, :], lb, causal)
            qm, km, qs, kl = (t[n].astype(MXU_DTYPE) for n in ("qm", "km", "qs", "kl"))
            v = hi_ref[rows, :].astype(MXU_DTYPE)
            for h in range(HGRN_HEADS):
                ls = slice(h * HGRN_K, (h + 1) * HGRN_K)
                st = s_ref[h]
                st_ref[c, ls, :] = st
                a = jnp.where(causal, _nt(qm[:, ls], km[:, ls]), 0.0)
                o_ref[rows, ls] = _nn(a, v[:, ls]) + _nt(qs[:, ls], st)
                s_ref[h] = t["el"][:, ls] * st + _tn(v[:, ls], kl[:, ls])
        for h in range(HGRN_HEADS):
            ls = slice(h * HGRN_K, (h + 1) * HGRN_K)
            o = o_ref[:, ls]
            r = lax.rsqrt(jnp.mean(o * o, axis=-1, keepdims=True) + EPS)
            y_ref[:, ls] = (o * r * ng_ref[:, ls] * _sigmoid(hg_ref[:, ls])).astype(y_ref.dtype)

    return _call(
        "hgrn_fwd", body, grid=(nblk,),
        in_specs=[col(0), col(1), col(2), col(3),
                  pl.BlockSpec((2, D_MODEL), lambda i: (0, 0)), pl.BlockSpec((1, D_MODEL), lambda i: (0, 0))],
        out_specs=[pl.BlockSpec((HGRN_TOKENS, D_MODEL), lambda i: (i, 0)),
                   pl.BlockSpec((HGRN_TOKENS, D_MODEL), lambda i: (i, 0)),
                   pl.BlockSpec((cpb, D_MODEL, HGRN_K), lambda i: (i, 0, 0))],
        out_shape=[jax.ShapeDtypeStruct((T, D_MODEL), MXU_DTYPE), jax.ShapeDtypeStruct((T, D_MODEL), F32),
                   jax.ShapeDtypeStruct((T // CHUNK, D_MODEL, HGRN_K), F32)],
        scratch=[pltpu.VMEM((HGRN_HEADS, HGRN_K, HGRN_K), F32)],
        args=[ph, ph, ph, ph, lb_logits, norm_g], semantics=("arbitrary",), comm=comm)


def _hgrn_bwd(ph, o_raw, states, dy, lb_logits, norm_g, comm=None):
    T = ph.shape[0]
    nblk, cpb = T // HGRN_TOKENS, HGRN_TOKENS // CHUNK
    rev = lambda i: nblk - 1 - i
    col = lambda c: pl.BlockSpec((HGRN_TOKENS, D_MODEL), functools.partial(lambda i, c: (rev(i), c), c=c))
    tok = pl.BlockSpec((HGRN_TOKENS, D_MODEL), lambda i: (rev(i), 0))

    def body(hq_ref, hf_ref, hi_ref, hg_ref, o_ref, st_ref, dy_ref, l_ref, ng_ref,
             dph_ref, dng_ref, dl_ref, dst_ref, dlb_ref, do_s, dqm_s, dkm_s, dqs_s, dkl_s, dv_s, del_s):
        i = pl.program_id(0)

        @pl.when(i == 0)
        def _():
            dst_ref[...] = jnp.zeros_like(dst_ref)
            dlb_ref[...] = jnp.zeros_like(dlb_ref)
            dng_ref[...] = jnp.zeros_like(dng_ref)

        lb = _lower_bound(l_ref[...])
        causal, anti = _tri(True), _tri(False)
        row = lax.broadcasted_iota(jnp.int32, (CHUNK, D_MODEL), 0)
        for c in reversed(range(cpb)):
            rows = slice(c * CHUNK, (c + 1) * CHUNK)
            hq = hq_ref[rows, :]
            t = _hgrn_chunk_inputs(hq, hf_ref[rows, :], lb, causal)
            sgg = _sigmoid(hg_ref[rows, :])
            dyv = dy_ref[rows, :]
            for h in range(HGRN_HEADS):
                ls = slice(h * HGRN_K, (h + 1) * HGRN_K)
                o = o_ref[rows, ls]
                r = lax.rsqrt(jnp.mean(o * o, axis=-1, keepdims=True) + EPS)
                nrm = o * r
                g_h = sgg[:, ls]
                dph_ref[rows, 3 * D_MODEL + h * HGRN_K:3 * D_MODEL + (h + 1) * HGRN_K] = (
                    dyv[:, ls] * nrm * ng_ref[:, ls] * g_h * (1.0 - g_h)).astype(dph_ref.dtype)
                dyg = dyv[:, ls] * g_h
                dng_ref[:, ls] += jnp.sum(dyg * nrm, axis=0, keepdims=True)
                dn = dyg * ng_ref[:, ls]
                do_s[:, ls] = r * (dn - nrm * jnp.mean(dn * nrm, axis=-1, keepdims=True))
            qm, km, qs, kl = (t[n].astype(MXU_DTYPE) for n in ("qm", "km", "qs", "kl"))
            v = hi_ref[rows, :].astype(MXU_DTYPE)
            do = do_s[...].astype(MXU_DTYPE)
            for h in range(HGRN_HEADS):
                ls = slice(h * HGRN_K, (h + 1) * HGRN_K)
                st = st_ref[c, ls, :]
                dst = dst_ref[h]
                a = jnp.where(causal, _nt(qm[:, ls], km[:, ls]), 0.0)
                da = jnp.where(causal, _nt(do[:, ls], v[:, ls]), 0.0)
                dv_s[:, ls] = _tn(a, do[:, ls]) + _nt(kl[:, ls], dst)
                dkl_s[:, ls] = _nn(v[:, ls], dst)
                dqs_s[:, ls] = _nn(do[:, ls], st)
                del_s[:, ls] = jnp.sum(dst * st, axis=0, keepdims=True)
                dst_ref[h] = _tn(do[:, ls], qs[:, ls]) + t["el"][:, ls] * dst
                dqm_s[:, ls] = _nn(da, km[:, ls])
                dkm_s[:, ls] = _tn(da, qm[:, ls])
            dqm, dkm, dqs, dkl = dqm_s[...], dkm_s[...], dqs_s[...], dkl_s[...]
            dq = dqm * t["e_qm"] + dqs * t["e_qs"]
            dk = dkm * t["e_km"] + dkl * t["e_kl"]
            t_qm, t_km, t_kl = dqm * t["qm"], dkm * t["km"], dkl * t["kl"]
            db = t_qm - t_km + dqs * t["qs"] - t_kl
            db_mid = jnp.sum(t_km - t_qm, axis=0, keepdims=True)
            db_last = jnp.sum(t_kl, axis=0, keepdims=True) + del_s[...] * t["el"]
            db = db + jnp.where(row == CHUNK // 2 - 1, db_mid, 0.0) + jnp.where(row == CHUNK - 1, db_last, 0.0)
            dlogf = _chunk_sum(anti, db)
            sq, sg, sgn, f = t["sq"], t["sg"], t["sgn"], t["f"]
            dph_ref[rows, 0:D_MODEL] = (dq * (sq * (1.0 + hq * (1.0 - sq)))).astype(dph_ref.dtype)
            dph_ref[rows, D_MODEL:2 * D_MODEL] = (
                dlogf * (1.0 - lb) * sg * (1.0 - sg) / f - dk * (1.0 - lb) * sgn * (1.0 - sgn)).astype(dph_ref.dtype)
            dph_ref[rows, 2 * D_MODEL:3 * D_MODEL] = dv_s[...].astype(dph_ref.dtype)
            dlb_ref[...] += jnp.sum(dlogf * (1.0 - sg) / f - dk * sgn, axis=0, keepdims=True)

        @pl.when(i == nblk - 1)
        def _():
            dl0 = dlb_ref[...] * lb * (1.0 - lb)
            dl_ref[0:1, :] = dl0
            dl_ref[1:2, :] = -dl0

    wide = pltpu.VMEM((CHUNK, D_MODEL), F32)
    return _call(
        "hgrn_bwd", body, grid=(nblk,),
        in_specs=[col(0), col(1), col(2), col(3), tok,
                  pl.BlockSpec((cpb, D_MODEL, HGRN_K), lambda i: (rev(i), 0, 0)), tok,
                  pl.BlockSpec((2, D_MODEL), lambda i: (0, 0)), pl.BlockSpec((1, D_MODEL), lambda i: (0, 0))],
        out_specs=[pl.BlockSpec((HGRN_TOKENS, 4 * D_MODEL), lambda i: (rev(i), 0)),
                   pl.BlockSpec((1, D_MODEL), lambda i: (0, 0)), pl.BlockSpec((2, D_MODEL), lambda i: (0, 0))],
        out_shape=[jax.ShapeDtypeStruct((T, 4 * D_MODEL), MXU_DTYPE), jax.ShapeDtypeStruct((1, D_MODEL), F32),
                   jax.ShapeDtypeStruct((2, D_MODEL), F32)],
        scratch=[pltpu.VMEM((HGRN_HEADS, HGRN_K, HGRN_K), F32), pltpu.VMEM((1, D_MODEL), F32),
                 wide, wide, wide, wide, wide, wide, pltpu.VMEM((1, D_MODEL), F32)],
        args=[ph, ph, ph, ph, o_raw, states, dy, lb_logits, norm_g], semantics=("arbitrary",), comm=comm)


def _local_step(x, target, vec, net):
    T, D = x.shape
    norm_mix_g, b_in, sinks, lb_logits = vec["norm_mix_g"], vec["b_in"], vec["attn_sinks"], vec["hgrn_lb_logits"]
    hgrn_norm_g, norm_ffn_g, norm_final_g = vec["hgrn_norm_g"], vec["norm_ffn_g"], vec["norm_final_g"]
    w_in = net.full("w_in")
    o_q, o_kv, o_h, o_g = (sum(IN_SPLITS[:i]) for i in range(4))
    w_q, w_kv, w_h, w_g = (_Rows(w_in, o, n) for o, n in zip((o_q, o_kv, o_h, o_g), IN_SPLITS))
    b_q, b_kv, b_h, b_g = (b_in[:, o:o + n] for o, n in zip((o_q, o_kv, o_h, o_g), IN_SPLITS))
    bias = lambda acc, b: (acc + b,)
    both = lambda acc: (acc, acc)
    grad_outs = [("mn", F32), ("mn", MXU_DTYPE)]
    row_vec = lambda n: ((1, n), lambda i, j: (0, j))
    tile = lambda tm, tn: ((tm, tn), lambda i, j: (i, j))
    TM = 512
    BIG = min(T, 1024)

    u = _rmsnorm_fwd("norm_mix", x, norm_mix_g)
    pq, = _matmul("in_q", [(u, w_q)], "nt", tm=TM, tn=1024, tk=1024, outs=[("mn", MXU_DTYPE)],
                  extras=[(b_q, *row_vec(1024))], epilogue=bias)
    pkv, = _matmul("in_kv", [(u, w_kv)], "nt", tm=TM, tn=256, tk=1024, outs=[("mn", MXU_DTYPE)],
                   extras=[(b_kv, *row_vec(256))], epilogue=bias)
    names = ("w_branch_attn", "w_branch_hgrn")
    (ph,), got = _matmul("in_h", [(u, w_h)], "nt", tm=TM, tn=1024, tk=1024, outs=[("mn", F32)],
                         extras=[(b_h, *row_vec(1024))], epilogue=bias, comm=net.gather(names))
    net.gathered(names, got)
    names = ("w_out",)
    (pg,), got = _matmul("in_g", [(u, w_g)], "nt", tm=TM, tn=1024, tk=1024, outs=[("mn", F32)],
                         extras=[(b_g, *row_vec(1024))], epilogue=bias, comm=net.gather(names))
    net.gathered(names, got)
    names = ("w_ffn_gate",)
    (y_attn, lse), got = _attn_fwd(pq, pkv, sinks, comm=net.gather(names))
    net.gathered(names, got)
    names = ("w_ffn_up",)
    (y_hgrn, o_raw, states), got = _hgrn_fwd(ph, lb_logits, hgrn_norm_g, comm=net.gather(names))
    net.gathered(names, got)
    w_ba, w_bh, w_out = net.full("w_branch_attn"), net.full("w_branch_hgrn"), net.full("w_out")
    w_gate, w_up = net.full("w_ffn_gate"), net.full("w_ffn_up")
    ya, = _matmul("branch_a", [(y_attn, w_ba)], "nn", tm=TM, tn=1024, tk=1024, outs=[("mn", F32)])
    gate_a = (pg, (TM, 1024), lambda i, j: (i, 0))
    gate_b = (pg, (TM, 1024), lambda i, j: (i, 1))

    def merge(acc, ya_v, ga, gb):
        return acc, _sigmoid(ga) * ya_v + _sigmoid(gb) * acc

    yb, merged = _matmul("branch_b", [(y_hgrn, w_bh)], "nn", tm=TM, tn=1024, tk=1024,
                         outs=[("mn", F32), ("mn", MXU_DTYPE)],
                         extras=[(ya, *tile(TM, 1024)), gate_a, gate_b], epilogue=merge)

    def resid_norm(acc, xin, g):
        h = xin + acc
        r = lax.rsqrt(jnp.mean(h * h, axis=-1, keepdims=True) + EPS)
        return h, h * r * g

    h1, u2 = _matmul("out_proj", [(merged, w_out)], "nn", tm=TM, tn=1024, tk=1024,
                     outs=[("mn", F32), ("mn", MXU_DTYPE)],
                     extras=[(x, *tile(TM, 1024)), (norm_ffn_g, *row_vec(1024))], epilogue=resid_norm)
    FT = FFN // 2
    names = ("w_ffn_down",)
    (gpre, up, z), got = _swiglu_fwd(u2, w_gate, w_up, tm=TM, tn=FT, comm=net.gather(names))
    net.gathered(names, got)
    w_down = net.full("w_ffn_down")

    def loss_head(acc, h1_v, tgt, g):
        h2 = h1_v + acc
        r = lax.rsqrt(jnp.mean(h2 * h2, axis=-1, keepdims=True) + EPS)
        xhat = h2 * r
        err = xhat * g - tgt
        part = 0.5 * jnp.sum(jnp.sum(err * err, axis=-1, keepdims=True), axis=0, keepdims=True) / D
        dyv = err / D
        dxh = dyv * g
        dh2 = r * (dxh - xhat * jnp.mean(dxh * xhat, axis=-1, keepdims=True))
        return dh2, dh2, jnp.sum(dyv * xhat, axis=0, keepdims=True), jnp.broadcast_to(part, (1, D))

    dh2, dh2b, dgf_p, loss_p = _matmul(
        "ffn_down", [(z, w_down)], "nn", tm=TM, tn=1024, tk=FFN,
        outs=[("mn", F32), ("mn", MXU_DTYPE), ("pn", F32), ("pn", F32)],
        extras=[(h1, *tile(TM, 1024)), (target, *tile(TM, 1024)), (norm_final_g, *row_vec(1024))],
        epilogue=loss_head)
    loss = jnp.sum(loss_p.reshape(-1, 8, D)[:, 0, 0])
    d_norm_final = _colsum_partials(dgf_p)

    def swiglu_bwd(acc, gv, upv):
        gv, upv = gv.astype(F32), upv.astype(F32)
        s = _sigmoid(gv)
        return acc * upv * (s * (1.0 + gv * (1.0 - s))), acc * (gv * s)

    dgp, dup = _matmul("d_ffn_hidden", [(dh2b, w_down)], "nt", tm=TM, tn=FT, tk=1024,
                       outs=[("mn", MXU_DTYPE), ("mn", MXU_DTYPE)],
                       extras=[(gpre, *tile(TM, FT)), (up, *tile(TM, FT))], epilogue=swiglu_bwd)
    d_w_down = _matmul("dw_down", [(z, dh2b)], "tn", tm=FT, tn=1024, tk=1024, outs=grad_outs, epilogue=both)

    def norm_ffn_bwd(acc, h1_v, g, dres):
        dx, dg = _rmsnorm_bwd_vals(acc, h1_v, g)
        dh = dres + dx
        return dh, dh, dg

    names = ("w_ffn_down",)
    (dh1, dh1b, dg2_p), got = _matmul(
        "d_ffn_in", [(dgp, w_gate), (dup, w_up)], "nn", tm=BIG, tn=1024, tk=FT,
        outs=[("mn", F32), ("mn", MXU_DTYPE), ("pn", F32)],
        extras=[(h1, *tile(BIG, 1024)), (norm_ffn_g, *row_vec(1024)), (dh2, *tile(BIG, 1024))],
        epilogue=norm_ffn_bwd,
        comm=net.exchange(dict(w_ffn_down=[d_w_down])))
    net.received(names, (), got)
    d_norm_ffn = _colsum_partials(dg2_p)
    d_w_gate = _matmul("dw_gate", [(dgp, u2)], "tn", tm=FT, tn=1024, tk=1024, outs=grad_outs, epilogue=both)
    d_w_up = _matmul("dw_up", [(dup, u2)], "tn", tm=FT, tn=1024, tk=1024, outs=grad_outs, epilogue=both)

    def merge_bwd(acc, ya_v, yb_v, ga, gb):
        sa, sb = _sigmoid(ga), _sigmoid(gb)
        return acc * sa, acc * sb, acc * ya_v * sa * (1.0 - sa), acc * yb_v * sb * (1.0 - sb)

    dya, dyb, dga, dgb = _matmul(
        "d_merged", [(dh1b, w_out)], "nt", tm=TM, tn=1024, tk=1024, outs=[("mn", MXU_DTYPE)] * 4,
        extras=[(ya, *tile(TM, 1024)), (yb, *tile(TM, 1024)), gate_a, gate_b], epilogue=merge_bwd)
    tn_grad = functools.partial(_matmul, mode="tn", tn=1024, tk=1024, outs=grad_outs, epilogue=both)
    d_w_out = tn_grad("dw_out", [(merged, dh1b)], tm=1024)
    dy_attn, = _matmul("d_y_attn", [(dya, w_ba)], "nt", tm=TM, tn=1024, tk=1024, outs=[("mn", MXU_DTYPE)])
    dy_hgrn, = _matmul("d_y_hgrn", [(dyb, w_bh)], "nt", tm=TM, tn=1024, tk=1024, outs=[("mn", F32)])
    d_w_ba = tn_grad("dw_branch_a", [(y_attn, dya)], tm=1024)
    d_w_bh = tn_grad("dw_branch_b", [(y_hgrn, dyb)], tm=1024)

    names, swap = ("w_ffn_gate",), ("w_ffn_down",)
    (dq, dkv, dsink), got = _attn_bwd(pq, pkv, sinks, lse, dy_attn,
                                      comm=net.exchange(dict(w_ffn_gate=[d_w_gate]), swap))
    net.received(names, swap, got)
    names, swap = ("w_ffn_up", "w_out", "w_branch_attn", "w_branch_hgrn"), ("w_ffn_gate",)
    (dph, d_hgrn_norm, d_lb_logits), got = _hgrn_bwd(
        ph, o_raw, states, dy_hgrn, lb_logits, hgrn_norm_g,
        comm=net.exchange(dict(w_ffn_up=[d_w_up], w_out=[d_w_out], w_branch_attn=[d_w_ba],
                               w_branch_hgrn=[d_w_bh]), swap))
    net.received(names, swap, got)

    d_w_in = [tn_grad("dw_in_q", [(dq, u)], tm=1024, a_colsum=True),
              tn_grad("dw_in_kv", [(dkv, u)], tm=256, a_colsum=True),
              tn_grad("dw_in_h", [(dph, u)], tm=1024),
              tn_grad("dw_in_ga", [(dga, u)], tm=1024, a_colsum=True),
              tn_grad("dw_in_gb", [(dgb, u)], tm=1024, a_colsum=True)]

    def norm_mix_bwd(acc, xin, g, dres):
        dx, dg = _rmsnorm_bwd_vals(acc, xin, g)
        return dres + dx, dg

    db_h, got = _colsum("db_h", dph, comm=net.presum_begin("w_in", d_w_in))
    halves = net.presum_end("w_in", got)
    names, swap = ("w_in",), ("w_ffn_up", "w_out", "w_branch_attn", "w_branch_hgrn")
    (dx, dg1_p), got = _matmul(
        "d_u", [(dq, w_q), (dkv, w_kv), (dph, w_h), (dga, _Rows(w_in, o_g, D)), (dgb, _Rows(w_in, o_g + D, D))], "nn",
        tm=BIG, tn=1024, tk=512, outs=[("mn", F32), ("pn", F32)],
        extras=[(x, *tile(BIG, 1024)), (norm_mix_g, *row_vec(1024)), (dh1, *tile(BIG, 1024))],
        epilogue=norm_mix_bwd,
        comm=_join(halves, net.swap(swap)))
    net.last = (names, swap, got)
    d_norm_mix = _colsum_partials(dg1_p)
    d_b_in = jnp.concatenate([d_w_in[0][2], d_w_in[1][2], db_h, d_w_in[3][2], d_w_in[4][2]], axis=1)
    vecs = dict(norm_mix_g=d_norm_mix, b_in=d_b_in, attn_sinks=jnp.sum(dsink.reshape(Q_HEADS, ATTN_BLOCK), axis=1).reshape(1, Q_HEADS),
                hgrn_lb_logits=d_lb_logits,
                hgrn_norm_g=d_hgrn_norm, norm_ffn_g=d_norm_ffn, norm_final_g=d_norm_final)
    return loss, dx, vecs


def _place():
    return lax.axis_index("x"), lax.axis_index("y"), lax.axis_index("c")


def _other_chips(x, y):
    return [(1 - x, y), (x, 1 - y), (1 - x, 1 - y)]


def _y_first(copies):
    return [copies[3 * (i // 3) + (1, 0, 2)[i % 3]] for i in range(len(copies))]


def _gather_copies(shards):
    n = len(shards)

    def build(ins, outs, send_sems, recv_sems, local_sems):
        x, y, c = _place()
        mine = 2 * x + y
        local = [pltpu.make_async_copy(ins[w], outs[w].at[mine], local_sems.at[w]) for w in range(n)]
        sends, recvs = [], []
        for w in range(n):
            for k, (px, py) in enumerate(_other_chips(x, y)):
                sem = 3 * w + k
                sends.append(pltpu.make_async_remote_copy(
                    src_ref=ins[w], dst_ref=outs[w].at[mine], send_sem=send_sems.at[sem], recv_sem=recv_sems.at[sem],
                    device_id=(px, py, c), device_id_type=MESH_ID))
                recvs.append(pltpu.make_async_remote_copy(
                    src_ref=ins[w], dst_ref=outs[w].at[2 * px + py], send_sem=send_sems.at[sem],
                    recv_sem=recv_sems.at[sem], device_id=(px, py, c), device_id_type=MESH_ID))
        return sends, recvs, local, _y_first(sends)

    return _Carried(shards, [jax.ShapeDtypeStruct((N_CHIPS,) + s.shape, s.dtype) for s in shards], 3 * n, n, build)


def _grad_copies(stacked):
    n = len(stacked)

    def build(ins, outs, send_sems, recv_sems, local_sems):
        x, y, c = _place()
        sends = []
        for w in range(n):
            for k, (px, py) in enumerate(_other_chips(x, y)):
                sem = 3 * w + k
                sends.append(pltpu.make_async_remote_copy(
                    src_ref=ins[w].at[2 * px + py], dst_ref=outs[w].at[k], send_sem=send_sems.at[sem],
                    recv_sem=recv_sems.at[sem], device_id=(px, py, c), device_id_type=MESH_ID))
        return sends, sends, [], _y_first(sends)

    return _Carried(stacked, [jax.ShapeDtypeStruct((3,) + s.shape[1:], s.dtype) for s in stacked], 3 * n, 0, build)


def _small_copies(small):
    def build(ins, outs, send_sems, recv_sems, local_sems):
        small_ref, all_ref = ins[0], outs[0]
        x, y, c = _place()
        me = 4 * x + 2 * y + c
        sends, recvs = [], []
        for r in range(1, 8):
            px = 1 - x if r & 4 else x
            py = 1 - y if r & 2 else y
            pc = 1 - c if r & 1 else c
            sends.append(pltpu.make_async_remote_copy(
                src_ref=small_ref, dst_ref=all_ref.at[me], send_sem=send_sems.at[r - 1], recv_sem=recv_sems.at[r - 1],
                device_id=(px, py, pc), device_id_type=MESH_ID))
            recvs.append(pltpu.make_async_remote_copy(
                src_ref=small_ref, dst_ref=all_ref.at[4 * px + 2 * py + pc], send_sem=send_sems.at[r - 1],
                recv_sem=recv_sems.at[r - 1], device_id=(px, py, pc), device_id_type=MESH_ID))
        return sends, recvs, [pltpu.make_async_copy(small_ref, all_ref.at[me], local_sems.at[0])]

    return _Carried([small], [jax.ShapeDtypeStruct((8,) + small.shape, small.dtype)], 7, 1, build)


def _gather_by_neighbours(name, shard):
    half = shard.shape[0] // 2
    quarter = half // 2

    def body(in_ref, out_ref, send_sems, recv_sems, local_sem):
        for core in (0, 1):
            @pl.when(lax.axis_index("c") == core)
            def _():
                program(core, in_ref, out_ref, send_sems, recv_sems, local_sem)

    def program(c, in_ref, out_ref, send_sems, recv_sems, local_sem):
        x, y, _ = _place()
        chip = lambda px, py: 2 * px + py
        to_x, to_y, sibling = (1 - x, y, c), (x, 1 - y, c), (x, y, 1 - c)
        x_blk, y_blk, d_blk = chip(1 - x, y), chip(x, 1 - y), chip(1 - x, 1 - y)
        mine, theirs = c * half, (1 - c) * half

        def copy(sem, rows, block, to, src=None):
            place = out_ref.at[block, pl.ds(rows[0], rows[1])]
            return pltpu.make_async_remote_copy(
                src_ref=place if src is None else src, dst_ref=place, send_sem=send_sems.at[sem],
                recv_sem=recv_sems.at[sem], device_id=to, device_id_type=MESH_ID)

        own = pltpu.make_async_copy(in_ref, out_ref.at[chip(x, y)], local_sem)
        own.start()
        my_rows = in_ref.at[pl.ds(mine, half)]
        along_x = dict(send=copy(0, (mine, half), chip(x, y), to_x, src=my_rows),
                       landed=copy(0, (mine, half), x_blk, to_x),
                       onward=[copy(3, (mine + quarter, quarter), x_blk, to_y), copy(4, (mine, half), x_blk, sibling)],
                       diagonal=copy(2, (mine, quarter), d_blk, to_x))
        along_y = dict(send=copy(1, (mine, half), chip(x, y), to_y, src=my_rows),
                       landed=copy(1, (mine, half), y_blk, to_y),
                       onward=[copy(2, (mine, quarter), y_blk, to_x), copy(5, (mine, half), y_blk, sibling)],
                       diagonal=copy(3, (mine + quarter, quarter), d_blk, to_y))
        last = copy(6, (mine, half), d_blk, sibling)

        order = (along_x, along_y) if c == 0 else (along_y, along_x)
        for axis in order:
            axis["send"].start()
        for axis in order:
            axis["landed"].wait_recv()
            for cp in axis["onward"]:
                cp.start()
        for axis in order:
            axis["diagonal"].wait_recv()
        last.start()
        for sem, block in ((4, x_blk), (5, y_blk), (6, d_blk)):
            copy(sem, (theirs, half), block, sibling).wait_recv()
        for cp in [along_x["send"], along_y["send"]] + along_x["onward"] + along_y["onward"] + [last]:
            cp.wait_send()
        own.wait()

    return pl.pallas_call(
        body, name=name, in_specs=[HBM_SPEC], out_specs=HBM_SPEC,
        out_shape=jax.ShapeDtypeStruct((N_CHIPS,) + shard.shape, shard.dtype),
        scratch_shapes=[pltpu.SemaphoreType.DMA((7,)), pltpu.SemaphoreType.DMA((7,)), pltpu.SemaphoreType.DMA(())],
    )(shard)


def _copies_alone(name, comm):
    return _call(name, lambda: None, grid=(), in_specs=[], out_specs=[], out_shape=[], args=[], comm=comm)[1]


class _Net:
    def __init__(self, shards):
        self.shards = shards
        self.whole, self.own, self.theirs, self.sums, self.other = {}, {}, {}, {}, {}
        x, y, _ = _place()
        self.chip = 2 * x + y

    def gather(self, names):
        return _gather_copies([self.shards[n] for n in names])

    def gathered(self, names, got):
        for n, g in zip(names, got):
            self.whole[n] = g.reshape(-1, g.shape[-1])

    def full(self, name):
        return self.whole[name]

    def exchange(self, grads, swap=()):
        stacked = []
        for n, pieces in grads.items():
            keep = jnp.concatenate([p[0] for p in pieces], axis=0) if len(pieces) > 1 else pieces[0][0]
            send = jnp.concatenate([p[1] for p in pieces], axis=0) if len(pieces) > 1 else pieces[0][1]
            rows = keep.shape[0] // N_CHIPS
            self.own[n] = lax.dynamic_slice_in_dim(keep, self.chip * rows, rows, axis=0)
            stacked.append(send.reshape(N_CHIPS, rows, send.shape[-1]))
        return _join(_grad_copies(stacked), self.swap(swap))

    def swap(self, names):
        return _sibling_copies([self.sums[n] for n in names]) if names else None

    def presum_begin(self, name, pieces):
        keep = jnp.concatenate([p[0] for p in pieces], axis=0)
        send = jnp.concatenate([p[1] for p in pieces], axis=0)
        rows = keep.shape[0] // N_CHIPS
        self.held = keep.reshape(N_CHIPS, rows, keep.shape[-1])
        return _half_rows_copies(send.reshape(N_CHIPS, rows, send.shape[-1]))

    def presum_end(self, name, got):
        x, y, c = _place()
        to_send, self.own[name] = _pre_sum("presum_" + name, self.held, got[0], jnp.stack([c, self.chip]))
        return _grad_copies([to_send])

    def received(self, names, swap, got, carried=None):
        self.theirs.update(zip(names, got[:len(names)]))
        self.other.update(zip(swap, got[len(names):]))
        for n in names:
            (self.sums[n],), more = _partial_sum("sum_" + n, self.own[n], self.theirs[n], comm=carried)
        return more


def _half_rows_copies(stacked):
    n, rows = stacked.shape[0], stacked.shape[1] // 2

    def build(ins, outs, send_sems, recv_sems, local_sems):
        x, y, c = _place()
        copies = [pltpu.make_async_remote_copy(
            src_ref=ins[0].at[s, pl.ds((1 - c) * rows, rows)], dst_ref=outs[0].at[s], send_sem=send_sems.at[s],
            recv_sem=recv_sems.at[s], device_id=(x, y, 1 - c), device_id_type=MESH_ID) for s in range(n)]
        return copies, copies, []

    return _Carried([stacked], [jax.ShapeDtypeStruct((n, rows, stacked.shape[2]), stacked.dtype)], n, 0, build)


def _pre_sum(name, held, theirs, core_and_chip):
    n, R, C = held.shape
    half = R // 2
    tr = _row_tile(half)
    per_half = half // tr

    def body(place_ref, h_ref, t_ref, send_ref, own_ref):
        total = h_ref[0] + t_ref[0].astype(F32)
        send_ref[0] = total.astype(send_ref.dtype)

        @pl.when(pl.program_id(1) == place_ref[1])
        def _():
            own_ref[...] = total

    return pl.pallas_call(
        body, name=name,
        grid_spec=pltpu.PrefetchScalarGridSpec(
            num_scalar_prefetch=1, grid=(per_half, n),
            in_specs=[pl.BlockSpec((1, tr, C), lambda i, s, place: (s, place[0] * per_half + i, 0)),
                      pl.BlockSpec((1, tr, C), lambda i, s, place: (s, i, 0))],
            out_specs=[pl.BlockSpec((1, tr, C), lambda i, s, place: (s, i, 0)),
                       pl.BlockSpec((tr, C), lambda i, s, place: (i, 0))]),
        out_shape=[jax.ShapeDtypeStruct((n, half, C), MXU_DTYPE), jax.ShapeDtypeStruct((half, C), F32)],
        compiler_params=_params(("arbitrary", "arbitrary")),
    )(core_and_chip, held, theirs)


def _sibling_copies(parts):
    n = len(parts)

    def build(ins, outs, send_sems, recv_sems, local_sems):
        x, y, c = _place()
        copies = [pltpu.make_async_remote_copy(
            src_ref=ins[w], dst_ref=outs[w], send_sem=send_sems.at[w], recv_sem=recv_sems.at[w],
            device_id=(x, y, 1 - c), device_id_type=MESH_ID) for w in range(n)]
        return copies, copies, []

    return _Carried(parts, [jax.ShapeDtypeStruct(p.shape, p.dtype) for p in parts], n, 0, build)


def _row_tile(rows, most=512, sublanes=16):
    return max(t for t in range(sublanes, most + 1, sublanes) if rows % t == 0)


def _partial_sum(name, own, recv, comm=None):
    R, C = own.shape
    tr = _row_tile(R, most=128)

    def body(o_ref, r_ref, p_ref):
        p_ref[...] = ((o_ref[...] + r_ref[0].astype(F32)) + r_ref[1].astype(F32)) + r_ref[2].astype(F32)

    return _call(name, body, grid=(R // tr,),
                 in_specs=[pl.BlockSpec((tr, C), lambda i: (i, 0)), pl.BlockSpec((3, tr, C), lambda i: (0, i, 0))],
                 out_specs=[pl.BlockSpec((tr, C), lambda i: (i, 0))], out_shape=[jax.ShapeDtypeStruct((R, C), F32)],
                 args=[own, recv], semantics=("parallel",), comm=comm)


def _adam_vals(w, g, m, v):
    m = ADAM_B1 * m + (1.0 - ADAM_B1) * g
    v = ADAM_B2 * v + (1.0 - ADAM_B2) * (g * g)
    m_hat = m / (1.0 - ADAM_B1 ** ADAM_STEP)
    v_hat = v / (1.0 - ADAM_B2 ** ADAM_STEP)
    delta = -ADAM_LR * (m_hat / (jnp.sqrt(v_hat) + ADAM_EPS) + ADAM_WD * w)
    return delta, m, v


def _adamw(name, w, m, v, mine, other, comm=None):
    R, C = w.shape
    tr = _row_tile(R)

    def body(w_ref, m_ref, v_ref, s_ref, n_ref, g_ref, d_ref, nm_ref, nv_ref):
        g = s_ref[...] + n_ref[...]
        d, nm, nv = _adam_vals(w_ref[...], g, m_ref[...], v_ref[...])
        g_ref[...], d_ref[...], nm_ref[...], nv_ref[...] = g, d, nm, nv

    spec = pl.BlockSpec((tr, C), lambda i: (i, 0))
    return _call(name, body, grid=(R // tr,), in_specs=[spec] * 5, out_specs=[spec] * 4,
                 out_shape=[jax.ShapeDtypeStruct((R, C), F32)] * 4, args=[w, m, v, mine, other],
                 semantics=("parallel",), comm=comm)


def _adamw_by_halves(name, w, m, v, mine, other, core):
    R, C = w.shape
    tr = _row_tile(R // 2)
    per_half = R // 2 // tr

    def body(c_ref, w_ref, m_ref, v_ref, s_ref, n_ref, g_ref, d_ref, nm_ref, nv_ref):
        g = jnp.where(pl.program_id(0) // per_half == c_ref[0, 0], s_ref[...], n_ref[...])
        d, nm, nv = _adam_vals(w_ref[...], g, m_ref[...], v_ref[...])
        g_ref[...], d_ref[...], nm_ref[...], nv_ref[...] = g, d, nm, nv

    spec = pl.BlockSpec((tr, C), lambda i: (i, 0))
    part = pl.BlockSpec((tr, C), lambda i: (i % per_half, 0))
    return pl.pallas_call(
        body, name=name, grid=(R // tr,),
        in_specs=[pl.BlockSpec(memory_space=pltpu.SMEM), spec, spec, spec, part, part], out_specs=[spec] * 4,
        out_shape=[jax.ShapeDtypeStruct((R, C), F32)] * 4, compiler_params=_params(("parallel",)),
    )(core, w, m, v, mine, other)


def _adamw_small(w, m, v, g_all):
    def body(w_ref, m_ref, v_ref, a_ref, g_ref, d_ref, nm_ref, nv_ref):
        g = a_ref[0]
        for dev in range(1, 8):
            g = g + a_ref[dev]
        d, nm, nv = _adam_vals(w_ref[...], g, m_ref[...], v_ref[...])
        g_ref[...], d_ref[...], nm_ref[...], nv_ref[...] = g, d, nm, nv

    return pl.pallas_call(
        body, name="adamw_small", out_shape=[jax.ShapeDtypeStruct(w.shape, F32)] * 4,
    )(w, m, v, g_all)


SMALL_LAYOUT = dict(norm_mix_g=(0, 1024), b_in=(1024, 7424), hgrn_norm_g=(8448, 1024), norm_ffn_g=(9472, 1024),
                    norm_final_g=(10496, 1024), hgrn_lb_logits=(11520, 2048), attn_sinks=(13568, 16))
SMALL_LOSS, SMALL_SIZE = 13584, 16 * D_MODEL


def _pack_small(vals, loss=None):
    names = list(SMALL_LAYOUT)
    tail = jnp.zeros((SMALL_SIZE - SMALL_LAYOUT[names[-1]][0],), F32)
    tail = lax.dynamic_update_slice(tail, vals[names[-1]].astype(F32).reshape(-1), (0,))
    if loss is not None:
        tail = lax.dynamic_update_slice(tail, loss.astype(F32).reshape(1), (SMALL_LOSS - SMALL_LAYOUT[names[-1]][0],))
    flat = jnp.concatenate([vals[n].astype(F32).reshape(-1) for n in names[:-1]] + [tail])
    return flat.reshape(-1, D_MODEL)


def _unpack_small(packed, shapes):
    flat = packed.reshape(-1)
    return {name: flat[lo:lo + n].reshape(shapes[name]) for name, (lo, n) in SMALL_LAYOUT.items()}


MATRICES = ("w_in", "w_branch_attn", "w_branch_hgrn", "w_out", "w_ffn_gate", "w_ffn_up", "w_ffn_down")
COLUMN_SHARDED = ("w_in", "w_ffn_gate", "w_ffn_up")
WEIGHTS = ("norm_mix_g", "w_in", "b_in", "attn_sinks", "hgrn_lb_logits", "hgrn_norm_g", "w_branch_attn",
           "w_branch_hgrn", "w_out", "norm_ffn_g", "w_ffn_gate", "w_ffn_up", "w_ffn_down", "norm_final_g")


def kernel(x, norm_mix_g, w_in, b_in, attn_sinks, hgrn_lb_logits, hgrn_norm_g, w_branch_attn, w_branch_hgrn, w_out, norm_ffn_g, w_ffn_gate, w_ffn_up, w_ffn_down, norm_final_g, loss_target, m_norm_mix_g, m_w_in, m_b_in, m_attn_sinks, m_hgrn_lb_logits, m_hgrn_norm_g, m_w_branch_attn, m_w_branch_hgrn, m_w_out, m_norm_ffn_g, m_w_ffn_gate, m_w_ffn_up, m_w_ffn_down, m_norm_final_g, v_norm_mix_g, v_w_in, v_b_in, v_attn_sinks, v_hgrn_lb_logits, v_hgrn_norm_g, v_w_branch_attn, v_w_branch_hgrn, v_w_out, v_norm_ffn_g, v_w_ffn_gate, v_w_ffn_up, v_w_ffn_down, v_norm_final_g):
    given = dict(locals())
    w = {n: given[n] for n in WEIGHTS}
    m = {n: given["m_" + n] for n in WEIGHTS}
    v = {n: given["v_" + n] for n in WEIGHTS}

    block = lambda a, n: jnp.transpose(a[0]) if n in COLUMN_SHARDED else a[0]
    unblock = lambda a, n: (jnp.transpose(a) if n in COLUMN_SHARDED else a)[None]
    net = _Net({n: block(w[n], n).astype(MXU_DTYPE) for n in MATRICES})
    net.gathered(("w_in",), [_gather_by_neighbours("gather_w_in", net.shards["w_in"])])
    vec = dict(norm_mix_g=norm_mix_g, b_in=b_in, attn_sinks=attn_sinks, hgrn_lb_logits=hgrn_lb_logits,
               hgrn_norm_g=hgrn_norm_g, norm_ffn_g=norm_ffn_g, norm_final_g=norm_final_g.reshape(1, D_MODEL))
    loss_part, dx, d_vecs = _local_step(x[0], loss_target[0], vec, net)

    small_all, = net.received(*net.last, carried=_small_copies(_pack_small(d_vecs, loss_part)))
    grads, deltas, new_m, new_v = {}, {}, {}, {}
    for n in ("w_ffn_down", "w_ffn_gate", "w_ffn_up", "w_out", "w_branch_attn", "w_branch_hgrn"):
        res, got = _adamw("adamw_" + n, block(w[n], n), block(m[n], n), block(v[n], n), net.sums[n], net.other[n],
                          comm=net.swap(("w_in",)) if n == "w_ffn_down" else None)
        if n == "w_ffn_down":
            net.other["w_in"], = got
        grads[n], deltas[n], new_m[n], new_v[n] = (unblock(r, n) for r in res)
    n = "w_in"
    res = _adamw_by_halves("adamw_" + n, block(w[n], n), block(m[n], n), block(v[n], n), net.sums[n], net.other[n],
                           _place()[2].reshape(1, 1))
    grads[n], deltas[n], new_m[n], new_v[n] = (unblock(r, n) for r in res)
    shapes = {n: w[n].shape for n in SMALL_LAYOUT}
    res = _adamw_small(_pack_small(w), _pack_small(m), _pack_small(v), small_all)
    for dst, packed in zip((grads, deltas, new_m, new_v), res):
        dst.update(_unpack_small(packed, shapes))

    loss = res[0].reshape(-1)[SMALL_LOSS]
    return (loss, dx[None], *[grads[n] for n in WEIGHTS], *[deltas[n] for n in WEIGHTS],
            *[new_m[n] for n in WEIGHTS], *[new_v[n] for n in WEIGHTS])
```

```python
import collections
import functools
import math

import jax
import jax.numpy as jnp
from jax import lax
from jax.experimental import pallas as pl
from jax.experimental.pallas import tpu as pltpu

F32 = jnp.float32
BF16 = jnp.bfloat16
MXU_DTYPE = jnp.bfloat16
SAVED_DTYPE = jnp.bfloat16
MESH_ID = pl.DeviceIdType.MESH

D_MODEL = 1024
HEAD_DIM = 64
Q_HEADS = 16
KV_HEADS = 2
GROUP = Q_HEADS // KV_HEADS
KV_WIDTH = KV_HEADS * HEAD_DIM
ATTN_BLOCK = 128
HGRN_HEADS = 8
HGRN_K = 128
CHUNK = 64
HGRN_TOKENS = 256
FFN = 2816
IN_SPLITS = (1024, 256, 4096, 2048)
EPS = 1e-6
NEG_INF = -1e30
ADAM_LR, ADAM_B1, ADAM_B2, ADAM_EPS, ADAM_WD, ADAM_STEP = 0.001, 0.9, 0.999, 1e-08, 0.01, 10
N_CHIPS = 4
VMEM_LIMIT = 60 * 1024 * 1024
ROW_ALIGN = 16


def _params(sem=None):
    return pltpu.CompilerParams(dimension_semantics=sem, vmem_limit_bytes=VMEM_LIMIT)


def _sigmoid(v):
    return 0.5 * jnp.tanh(0.5 * v) + 0.5


def _dot(a, b, dims):
    return lax.dot_general(a.astype(MXU_DTYPE), b.astype(MXU_DTYPE), (dims, ((), ())),
                           preferred_element_type=F32)


def _nn(a, b):
    return _dot(a, b, ((1,), (0,)))


def _nt(a, b):
    return _dot(a, b, ((1,), (1,)))


def _tn(a, b):
    return _dot(a, b, ((0,), (0,)))


HBM_SPEC = pl.BlockSpec(memory_space=pl.ANY)


class _Carried:
    def __init__(self, arrays, out_shapes, n_remote, n_local, build):
        self.parts = [(len(arrays), len(out_shapes), build)]
        self.arrays, self.out_shapes = list(arrays), list(out_shapes)
        self.scratch = [pltpu.SemaphoreType.DMA((n_remote,)), pltpu.SemaphoreType.DMA((n_remote,)),
                        pltpu.SemaphoreType.DMA((max(n_local, 1),))]

    def __add__(self, other):
        both = _Carried([], [], 1, 0, None)
        both.parts = self.parts + other.parts
        both.arrays, both.out_shapes = self.arrays + other.arrays, self.out_shapes + other.out_shapes
        both.scratch = self.scratch + other.scratch
        return both

    def _built(self, ins, outs, sems):
        for p, (ni, no, build) in enumerate(self.parts):
            yield build(ins[:ni], outs[:no], *sems[3 * p:3 * p + 3])
            ins, outs = ins[ni:], outs[no:]

    def start(self, ins, outs, sems):
        core = lax.axis_index("c")
        for sends, _, local, *other_order in self._built(ins, outs, sems):
            for cp in local:
                cp.start()
            if not other_order:
                for cp in sends:
                    cp.start()
                continue

            @pl.when(core == 0)
            def _():
                for cp in sends:
                    cp.start()

            @pl.when(core == 1)
            def _():
                for cp in other_order[0]:
                    cp.start()

    def wait(self, ins, outs, sems):
        for sends, recvs, local, *_ in self._built(ins, outs, sems):
            for cp in recvs:
                cp.wait_recv()
            for cp in sends:
                cp.wait_send()
            for cp in local:
                cp.wait()


def _join(*comms):
    comms = [c for c in comms if c is not None]
    return functools.reduce(lambda a, b: a + b, comms) if comms else None


def _call(name, body, *, grid, in_specs, out_specs, out_shape, args, scratch=(), semantics=None, comm=None):
    n_in, n_out, n_scr = len(in_specs), len(out_specs), len(scratch)
    if comm is None:
        res = pl.pallas_call(body, name=name, grid=grid, in_specs=in_specs, out_specs=out_specs, out_shape=out_shape,
                             scratch_shapes=list(scratch), compiler_params=_params(semantics))(*args)
        return list(res), []
    ci, co = len(comm.arrays), len(comm.out_shapes)

    def carrying(*refs):
        ins, refs = refs[:n_in], refs[n_in:]
        c_ins, refs = refs[:ci], refs[ci:]
        outs, refs = refs[:n_out], refs[n_out:]
        c_outs, refs = refs[:co], refs[co:]
        scr, sems = refs[:n_scr], refs[n_scr:]
        if not grid:
            comm.start(c_ins, c_outs, sems)
            body(*ins, *outs, *scr)
            comm.wait(c_ins, c_outs, sems)
            return
        first = functools.reduce(jnp.logical_and, [pl.program_id(a) == 0 for a in range(len(grid))])
        last = functools.reduce(jnp.logical_and, [pl.program_id(a) == g - 1 for a, g in enumerate(grid)])

        @pl.when(first)
        def _():
            comm.start(c_ins, c_outs, sems)

        body(*ins, *outs, *scr)

        @pl.when(last)
        def _():
            comm.wait(c_ins, c_outs, sems)

    res = pl.pallas_call(
        carrying, name=name, grid=grid, in_specs=list(in_specs) + [HBM_SPEC] * ci,
        out_specs=list(out_specs) + [HBM_SPEC] * co, out_shape=list(out_shape) + comm.out_shapes,
        scratch_shapes=list(scratch) + comm.scratch,
        compiler_params=_params(("arbitrary",) * len(grid) if grid else None),
    )(*args, *comm.arrays)
    return list(res[:n_out]), list(res[n_out:])


_NO_COPIES = object()
_Rows = collections.namedtuple("_Rows", "array first rows")


def _matmul(name, pairs, mode, *, tm, tn, tk, outs, extras=(), epilogue=None, a_colsum=False, comm=_NO_COPIES):
    prod = dict(nn=_nn, nt=_nt, tn=_tn)[mode]
    pairs = [(a, b if isinstance(b, _Rows) else _Rows(b, 0, b.shape[0])) for a, b in pairs]
    a0, b0 = pairs[0]
    M = a0.shape[1] if mode == "tn" else a0.shape[0]
    N = b0.rows if mode == "nt" else b0.array.shape[1]
    steps, in_specs, offset = [], [], 0
    for a, b in pairs:
        K = a.shape[0] if mode == "tn" else a.shape[1]
        t = min(tk, K)
        assert K % t == 0, (name, K, t)
        kmap = functools.partial(lambda k, off, n: jnp.clip(k - off, 0, n - 1), off=offset, n=K // t)
        if mode == "tn":
            in_specs.append(pl.BlockSpec((t, tm), functools.partial(lambda i, j, k, f: (f(k), i), f=kmap)))
        else:
            in_specs.append(pl.BlockSpec((tm, t), functools.partial(lambda i, j, k, f: (i, f(k)), f=kmap)))
        whole = b.first == 0 and b.rows == b.array.shape[0]
        if mode == "nt":
            shape = (tn, t) if whole else (pl.Element(tn), pl.Element(t))
            in_specs.append(pl.BlockSpec(shape, functools.partial(
                lambda i, j, k, f, b, t, whole: (j, f(k)) if whole else (
                    pl.multiple_of(b.first + j * tn, ROW_ALIGN), pl.multiple_of(f(k) * t, 128)),
                f=kmap, b=b, t=t, whole=whole)))
        else:
            assert b.rows == K, (name, b.rows, K)
            shape = (t, tn) if whole else (pl.Element(t), pl.Element(tn))
            in_specs.append(pl.BlockSpec(shape, functools.partial(
                lambda i, j, k, f, b, t, whole: (f(k), j) if whole else (
                    pl.multiple_of(b.first + f(k) * t, ROW_ALIGN), pl.multiple_of(j * tn, 128)),
                f=kmap, b=b, t=t, whole=whole)))
        steps.append((offset, offset + K // t))
        offset += K // t
    assert M % tm == 0 and N % tn == 0, (name, M, N, tm, tn)
    ni, nj, nk = M // tm, N // tn, offset
    npair, ne, no = len(pairs), len(extras), len(outs)
    if epilogue is None:
        epilogue = lambda acc: (acc,)

    def finish(acc, extra_refs, out_refs):
        vals = epilogue(acc, *[r[...] for r in extra_refs])
        for (kind, _), o_ref, val in zip(outs, out_refs, vals):
            if kind == "pn":
                val = jnp.broadcast_to(val, o_ref.shape)
            o_ref[...] = val.astype(o_ref.dtype)

    def body(*refs):
        ab, rest = refs[:2 * npair], refs[2 * npair:]
        extra_refs, out_refs = rest[:ne], rest[ne:ne + no]
        if nk == 1:
            finish(prod(ab[0][...], ab[1][...]), extra_refs, out_refs)
            return
        acc_ref = rest[-1]
        k = pl.program_id(2)

        @pl.when(k == 0)
        def _():
            acc_ref[...] = jnp.zeros_like(acc_ref)
            if a_colsum:
                rest[ne + no][...] = jnp.zeros((1, tm), F32)

        if a_colsum:
            rest[ne + no][...] += jnp.sum(ab[0][...].astype(F32), axis=0, keepdims=True)
        for p, (lo, hi) in enumerate(steps):
            @pl.when(jnp.logical_and(k >= lo, k < hi))
            def _():
                acc_ref[...] += prod(ab[2 * p][...], ab[2 * p + 1][...])

        @pl.when(k == nk - 1)
        def _():
            finish(acc_ref[...], extra_refs, out_refs)

    for _, shape, im in extras:
        in_specs.append(pl.BlockSpec(shape, functools.partial(lambda i, j, k, im: im(i, j), im=im)))
    out_shape, out_specs = [], []
    for kind, dt in outs:
        if kind == "mn":
            out_shape.append(jax.ShapeDtypeStruct((M, N), dt))
            out_specs.append(pl.BlockSpec((tm, tn), lambda i, j, k: (i, j)))
        else:
            out_shape.append(jax.ShapeDtypeStruct((8 * ni, N), dt))
            out_specs.append(pl.BlockSpec((8, tn), lambda i, j, k: (i, j)))
    if a_colsum:
        assert mode == "tn" and npair == 1 and nj == 1 and nk > 1, name
        out_shape.append(jax.ShapeDtypeStruct((1, M), F32))
        out_specs.append(pl.BlockSpec((1, tm), lambda i, j, k: (0, i)))
    grid = (ni, nj, nk)
    if nj > 1 and nk == 1:
        turned = lambda spec: pl.BlockSpec(spec.block_shape, functools.partial(
            lambda j, i, k, im: im(i, j, k), im=spec.index_map))
        in_specs, out_specs, grid = [turned(s) for s in in_specs], [turned(s) for s in out_specs], (nj, ni, nk)
    res, got = _call(name, body, grid=grid, in_specs=in_specs, out_specs=out_specs, out_shape=out_shape,
                     args=[t for a, b in pairs for t in (a, b.array)] + [e[0] for e in extras],
                     scratch=[pltpu.VMEM((tm, tn), F32)] if nk > 1 else [],
                     semantics=("parallel", "parallel", "arbitrary"), comm=None if comm is _NO_COPIES else comm)
    return res if comm is _NO_COPIES else (res, got)


def _swiglu_fwd(u, w_gate_t, w_up_t, *, tm, tn, comm=None):
    (T, D), F = u.shape, w_gate_t.shape[0]

    def body(u_ref, wg_ref, wu_ref, g_ref, up_ref, z_ref):
        g, up = _nt(u_ref[...], wg_ref[...]), _nt(u_ref[...], wu_ref[...])
        g_ref[...], up_ref[...] = g.astype(g_ref.dtype), up.astype(up_ref.dtype)
        z_ref[...] = (g * _sigmoid(g) * up).astype(z_ref.dtype)

    w_spec = pl.BlockSpec((tn, D), lambda j, i: (j, 0))
    o_spec = pl.BlockSpec((tm, tn), lambda j, i: (i, j))
    return _call("ffn_hidden", body, grid=(F // tn, T // tm),
                 in_specs=[pl.BlockSpec((tm, D), lambda j, i: (i, 0)), w_spec, w_spec], out_specs=[o_spec] * 3,
                 out_shape=[jax.ShapeDtypeStruct((T, F), SAVED_DTYPE)] * 2 + [jax.ShapeDtypeStruct((T, F), MXU_DTYPE)],
                 args=[u, w_gate_t, w_up_t], semantics=("parallel", "parallel"), comm=comm)


def _colsum_partials(p):
    return jnp.sum(p.reshape(-1, 8, p.shape[-1])[:, 0, :], axis=0, keepdims=True)


def _rmsnorm_fwd(name, x, g, tr=512):
    T, D = x.shape

    def body(x_ref, g_ref, u_ref):
        xv = x_ref[...]
        r = lax.rsqrt(jnp.mean(xv * xv, axis=-1, keepdims=True) + EPS)
        u_ref[...] = (xv * r * g_ref[...]).astype(u_ref.dtype)

    return pl.pallas_call(
        body, name=name, grid=(T // tr,),
        in_specs=[pl.BlockSpec((tr, D), lambda i: (i, 0)), pl.BlockSpec((1, D), lambda i: (0, 0))],
        out_specs=pl.BlockSpec((tr, D), lambda i: (i, 0)),
        out_shape=jax.ShapeDtypeStruct((T, D), MXU_DTYPE),
        compiler_params=_params(("parallel",)),
    )(x, g)


def _rmsnorm_bwd_vals(dy, xin, g):
    rstd = lax.rsqrt(jnp.mean(xin * xin, axis=-1, keepdims=True) + EPS)
    xhat = xin * rstd
    dg = jnp.sum(dy * xhat, axis=0, keepdims=True)
    dxh = dy * g
    dx = rstd * (dxh - xhat * jnp.mean(dxh * xhat, axis=-1, keepdims=True))
    return dx, dg


def _colsum(name, a, tr=512, comm=_NO_COPIES):
    T, N = a.shape

    def body(a_ref, o_ref):
        @pl.when(pl.program_id(0) == 0)
        def _():
            o_ref[...] = jnp.zeros_like(o_ref)

        o_ref[...] += jnp.sum(a_ref[...].astype(F32), axis=0, keepdims=True)

    (res,), got = _call(name, body, grid=(T // tr,), in_specs=[pl.BlockSpec((tr, N), lambda i: (i, 0))],
                        out_specs=[pl.BlockSpec((1, N), lambda i: (0, 0))],
                        out_shape=[jax.ShapeDtypeStruct((1, N), F32)], args=[a], semantics=("arbitrary",),
                        comm=None if comm is _NO_COPIES else comm)
    return res if comm is _NO_COPIES else (res, got)


ATTN_SCALE = 1.0 / math.sqrt(HEAD_DIM)
GROUP_LANES = GROUP * ATTN_BLOCK
PAIR = 2 * HEAD_DIM


def _attn_mask():
    kj = lax.broadcasted_iota(jnp.int32, (ATTN_BLOCK, GROUP_LANES), 0)
    qi = lax.broadcasted_iota(jnp.int32, (ATTN_BLOCK, GROUP_LANES), 1) & (ATTN_BLOCK - 1)
    return kj <= qi


def _heads_transposed(ref, g, scale=None):
    parts = []
    for a in range(GROUP // 2):
        lo = (g * GROUP // 2 + a) * PAIR
        pair = ref[:, lo:lo + PAIR].astype(F32)
        pair = (pair if scale is None else pair * scale).T
        parts += [pair[:HEAD_DIM], pair[HEAD_DIM:]]
    return jnp.concatenate(parts, axis=1).astype(MXU_DTYPE)


def _heads_back(ref, g, vt):
    for a in range(GROUP // 2):
        lo = (g * GROUP // 2 + a) * PAIR
        pair = jnp.concatenate([vt[:, (2 * a) * ATTN_BLOCK:(2 * a + 1) * ATTN_BLOCK],
                                vt[:, (2 * a + 1) * ATTN_BLOCK:(2 * a + 2) * ATTN_BLOCK]], axis=0)
        ref[:, lo:lo + PAIR] = pair.T.astype(ref.dtype)


def _kv_parts(kv_ref, g):
    ks = slice(g * HEAD_DIM, (g + 1) * HEAD_DIM)
    vs = slice(KV_WIDTH + g * HEAD_DIM, KV_WIDTH + (g + 1) * HEAD_DIM)
    return kv_ref[:, ks].astype(MXU_DTYPE), kv_ref[:, vs].astype(MXU_DTYPE)


def _sink_rows(sinks):
    return jnp.repeat(sinks.reshape(KV_HEADS, GROUP), ATTN_BLOCK, axis=1)


def _attn_fwd(pq, pkv, sinks, comm=None):
    T = pq.shape[0]
    nb = T // ATTN_BLOCK

    def body(q_ref, kvc_ref, kvp_ref, s_ref, y_ref, lse_ref):
        mask_c = _attn_mask()
        has_prev = pl.program_id(0) > 0
        for g in range(KV_HEADS):
            (kc, vc), (kp, vp) = _kv_parts(kvc_ref, g), _kv_parts(kvp_ref, g)
            qt = _heads_transposed(q_ref, g, ATTN_SCALE)
            s = jnp.where(mask_c, _nn(kc, qt), jnp.where(has_prev, _nn(kp, qt), NEG_INF))
            sink = s_ref[g:g + 1, :]
            m = jnp.maximum(jnp.max(s, axis=0, keepdims=True), sink)
            p = jnp.exp(s - m)
            den = jnp.sum(p, axis=0, keepdims=True) + jnp.exp(sink - m)
            pc = jnp.where(mask_c, p, 0.0)
            _heads_back(y_ref, g, (_tn(vc, pc) + _tn(vp, p - pc)) / den)
            lse = m + jnp.log(den)
            for i in range(GROUP):
                lse_ref[g * GROUP + i:g * GROUP + i + 1, :] = lse[:, i * ATTN_BLOCK:(i + 1) * ATTN_BLOCK]

    return _call(
        "attn_fwd", body, grid=(nb,),
        in_specs=[pl.BlockSpec((ATTN_BLOCK, D_MODEL), lambda n: (n, 0)),
                  pl.BlockSpec((ATTN_BLOCK, 2 * KV_WIDTH), lambda n: (n, 0)),
                  pl.BlockSpec((ATTN_BLOCK, 2 * KV_WIDTH), lambda n: (jnp.maximum(n - 1, 0), 0)),
                  pl.BlockSpec((KV_HEADS, GROUP_LANES), lambda n: (0, 0))],
        out_specs=[pl.BlockSpec((ATTN_BLOCK, D_MODEL), lambda n: (n, 0)),
                   pl.BlockSpec((Q_HEADS, ATTN_BLOCK), lambda n: (0, n))],
        out_shape=[jax.ShapeDtypeStruct((T, D_MODEL), MXU_DTYPE), jax.ShapeDtypeStruct((Q_HEADS, T), F32)],
        args=[pq, pkv, pkv, _sink_rows(sinks)], semantics=("parallel",), comm=comm)


def _attn_bwd(pq, pkv, sinks, lse, dy, comm=None):
    T = pq.shape[0]
    nb = T // ATTN_BLOCK
    cur = lambda n: (jnp.minimum(n, nb - 1), 0)

    def body(q_ref, kvc_ref, kvp_ref, s_ref, lse_ref, dy_ref, dq_ref, dkv_ref, ds_ref, carry, top, bot):
        n = pl.program_id(0)

        @pl.when(n == 0)
        def _():
            carry[...] = jnp.zeros_like(carry)
            ds_ref[...] = jnp.zeros_like(ds_ref)

        @pl.when(n < nb)
        def _():
            mask_c = _attn_mask()
            valid = jnp.logical_or(mask_c, n > 0)
            for g in range(KV_HEADS):
                ks = slice(g * HEAD_DIM, (g + 1) * HEAD_DIM)
                vs = slice(KV_WIDTH + g * HEAD_DIM, KV_WIDTH + (g + 1) * HEAD_DIM)
                (kc, vc), (kp, vp) = _kv_parts(kvc_ref, g), _kv_parts(kvp_ref, g)
                qt = _heads_transposed(q_ref, g, ATTN_SCALE)
                dot = _heads_transposed(dy_ref, g)
                lse = jnp.concatenate([lse_ref[g * GROUP + i:g * GROUP + i + 1, :] for i in range(GROUP)], axis=1)
                p = jnp.where(valid, jnp.exp(jnp.where(mask_c, _nn(kc, qt), _nn(kp, qt)) - lse), 0.0)
                dp = jnp.where(mask_c, _nn(vc, dot), _nn(vp, dot))
                delta = jnp.sum(p * dp, axis=0, keepdims=True)
                ds = p * (dp - delta)
                ds_c, p_c = jnp.where(mask_c, ds, 0.0), jnp.where(mask_c, p, 0.0)
                ds_p, p_p = ds - ds_c, p - p_c
                _heads_back(dq_ref, g, (_tn(kc, ds_c) + _tn(kp, ds_p)) * ATTN_SCALE)
                bot[:, ks], bot[:, vs] = _nt(ds_c, qt), _nt(p_c, dot)
                top[:, ks], top[:, vs] = _nt(ds_p, qt), _nt(p_p, dot)
                ds_ref[g:g + 1, :] -= jnp.exp(s_ref[g:g + 1, :] - lse) * delta
            dkv_ref[...] = (carry[...] + top[...]).astype(dkv_ref.dtype)
            carry[...] = bot[...]

        @pl.when(n == nb)
        def _():
            dkv_ref[...] = carry[...].astype(dkv_ref.dtype)

    return _call(
        "attn_bwd", body, grid=(nb + 1,),
        in_specs=[pl.BlockSpec((ATTN_BLOCK, D_MODEL), cur),
                  pl.BlockSpec((ATTN_BLOCK, 2 * KV_WIDTH), cur),
                  pl.BlockSpec((ATTN_BLOCK, 2 * KV_WIDTH), lambda n: (jnp.maximum(jnp.minimum(n, nb - 1) - 1, 0), 0)),
                  pl.BlockSpec((KV_HEADS, GROUP_LANES), lambda n: (0, 0)),
                  pl.BlockSpec((Q_HEADS, ATTN_BLOCK), lambda n: (0, jnp.minimum(n, nb - 1))),
                  pl.BlockSpec((ATTN_BLOCK, D_MODEL), cur)],
        out_specs=[pl.BlockSpec((ATTN_BLOCK, D_MODEL), cur),
                   pl.BlockSpec((ATTN_BLOCK, 2 * KV_WIDTH), lambda n: (jnp.maximum(n - 1, 0), 0)),
                   pl.BlockSpec((KV_HEADS, GROUP_LANES), lambda n: (0, 0))],
        out_shape=[jax.ShapeDtypeStruct((T, D_MODEL), MXU_DTYPE),
                   jax.ShapeDtypeStruct((T, 2 * KV_WIDTH), MXU_DTYPE),
                   jax.ShapeDtypeStruct((KV_HEADS, GROUP_LANES), F32)],
        scratch=[pltpu.VMEM((ATTN_BLOCK, 2 * KV_WIDTH), F32)] * 3,
        args=[pq, pkv, pkv, _sink_rows(sinks), lse, dy], semantics=("arbitrary",), comm=comm)


def _lower_bound(l):
    m = jnp.maximum(l[0:1], l[1:2])
    e0, e1 = jnp.exp(l[0:1] - m), jnp.exp(l[1:2] - m)
    return e0 / (e0 + e1)


def _tri(lower):
    r = lax.broadcasted_iota(jnp.int32, (CHUNK, CHUNK), 0)
    c = lax.broadcasted_iota(jnp.int32, (CHUNK, CHUNK), 1)
    return (r >= c) if lower else (c >= r)


def _chunk_sum(mask, v):
    ones = mask.astype(BF16)
    hi = v.astype(BF16)
    rest = v - hi.astype(F32)
    mid = rest.astype(BF16)
    lo = (rest - mid.astype(F32)).astype(BF16)
    part = lambda t: lax.dot_general(ones, t, (((1,), (0,)), ((), ())), preferred_element_type=F32)
    return part(hi) + part(mid) + part(lo)


def _hgrn_chunk_inputs(hq, hf, lb, causal):
    half_t = 0.5 * jnp.tanh(0.5 * hf)
    sg, sgn = 0.5 + half_t, 0.5 - half_t
    f = lb + (1.0 - lb) * sg
    kk = (1.0 - lb) * sgn
    sq = _sigmoid(hq)
    q = hq * sq
    b = _chunk_sum(causal, jnp.log(f))
    bm, bl = b[CHUNK // 2 - 1:CHUNK // 2, :], b[CHUNK - 1:CHUNK, :]
    e_qm, e_km = jnp.exp(b - bm), jnp.exp(bm - b)
    e_qs, e_kl = e_qm * jnp.exp(bm), e_km * jnp.exp(bl - bm)
    return dict(sg=sg, sgn=sgn, f=f, kk=kk, sq=sq, q=q, e_qm=e_qm, e_km=e_km, e_qs=e_qs, e_kl=e_kl,
                qm=q * e_qm, km=kk * e_km, qs=q * e_qs, kl=kk * e_kl, el=jnp.exp(bl))


def _hgrn_fwd(ph, lb_logits, norm_g, comm=None):
    T = ph.shape[0]
    nblk, cpb = T // HGRN_TOKENS, HGRN_TOKENS // CHUNK
    col = lambda c: pl.BlockSpec((HGRN_TOKENS, D_MODEL), functools.partial(lambda i, c: (i, c), c=c))

    def body(hq_ref, hf_ref, hi_ref, hg_ref, l_ref, ng_ref, y_ref, o_ref, st_ref, s_ref):
        @pl.when(pl.program_id(0) == 0)
        def _():
            s_ref[...] = jnp.zeros_like(s_ref)

        lb = _lower_bound(l_ref[...])
        causal = _tri(True)
        for c in range(cpb):
            rows = slice(c * CHUNK, (c + 1) * CHUNK)
            t = _hgrn_chunk_inputs(hq_ref[rows, :], hf_ref[rows, :], lb, causal)
            qm, km, qs, kl = (t[n].astype(MXU_DTYPE) for n in ("qm", "km", "qs", "kl"))
            v = hi_ref[rows, :].astype(MXU_DTYPE)
            for h in range(HGRN_HEADS):
                ls = slice(h * HGRN_K, (h + 1) * HGRN_K)
                st = s_ref[h]
                st_ref[c, ls, :] = st
                a = jnp.where(causal, _nt(qm[:, ls], km[:, ls]), 0.0)
                o_ref[rows, ls] = _nn(a, v[:, ls]) + _nt(qs[:, ls], st)
                s_ref[h] = t["el"][:, ls] * st + _tn(v[:, ls], kl[:, ls])
        for h in range(HGRN_HEADS):
            ls = slice(h * HGRN_K, (h + 1) * HGRN_K)
            o = o_ref[:, ls]
            r = lax.rsqrt(jnp.mean(o * o, axis=-1, keepdims=True) + EPS)
            y_ref[:, ls] = (o * r * ng_ref[:, ls] * _sigmoid(hg_ref[:, ls])).astype(y_ref.dtype)

    return _call(
        "hgrn_fwd", body, grid=(nblk,),
        in_specs=[col(0), col(1), col(2), col(3),
                  pl.BlockSpec((2, D_MODEL), lambda i: (0, 0)), pl.BlockSpec((1, D_MODEL), lambda i: (0, 0))],
        out_specs=[pl.BlockSpec((HGRN_TOKENS, D_MODEL), lambda i: (i, 0)),
                   pl.BlockSpec((HGRN_TOKENS, D_MODEL), lambda i: (i, 0)),
                   pl.BlockSpec((cpb, D_MODEL, HGRN_K), lambda i: (i, 0, 0))],
        out_shape=[jax.ShapeDtypeStruct((T, D_MODEL), MXU_DTYPE), jax.ShapeDtypeStruct((T, D_MODEL), F32),
                   jax.ShapeDtypeStruct((T // CHUNK, D_MODEL, HGRN_K), F32)],
        scratch=[pltpu.VMEM((HGRN_HEADS, HGRN_K, HGRN_K), F32)],
        args=[ph, ph, ph, ph, lb_logits, norm_g], semantics=("arbitrary",), comm=comm)


def _hgrn_bwd(ph, o_raw, states, dy, lb_logits, norm_g, comm=None):
    T = ph.shape[0]
    nblk, cpb = T // HGRN_TOKENS, HGRN_TOKENS // CHUNK
    rev = lambda i: nblk - 1 - i
    col = lambda c: pl.BlockSpec((HGRN_TOKENS, D_MODEL), functools.partial(lambda i, c: (rev(i), c), c=c))
    tok = pl.BlockSpec((HGRN_TOKENS, D_MODEL), lambda i: (rev(i), 0))

    def body(hq_ref, hf_ref, hi_ref, hg_ref, o_ref, st_ref, dy_ref, l_ref, ng_ref,
             dph_ref, dng_ref, dl_ref, dst_ref, dlb_ref, do_s, dqm_s, dkm_s, dqs_s, dkl_s, dv_s, del_s):
        i = pl.program_id(0)

        @pl.when(i == 0)
        def _():
            dst_ref[...] = jnp.zeros_like(dst_ref)
            dlb_ref[...] = jnp.zeros_like(dlb_ref)
            dng_ref[...] = jnp.zeros_like(dng_ref)

        lb = _lower_bound(l_ref[...])
        causal, anti = _tri(True), _tri(False)
        row = lax.broadcasted_iota(jnp.int32, (CHUNK, D_MODEL), 0)
        for c in reversed(range(cpb)):
            rows = slice(c * CHUNK, (c + 1) * CHUNK)
            hq = hq_ref[rows, :]
            t = _hgrn_chunk_inputs(hq, hf_ref[rows, :], lb, causal)
            sgg = _sigmoid(hg_ref[rows, :])
            dyv = dy_ref[rows, :]
            for h in range(HGRN_HEADS):
                ls = slice(h * HGRN_K, (h + 1) * HGRN_K)
                o = o_ref[rows, ls]
                r = lax.rsqrt(jnp.mean(o * o, axis=-1, keepdims=True) + EPS)
                nrm = o * r
                g_h = sgg[:, ls]
                dph_ref[rows, 3 * D_MODEL + h * HGRN_K:3 * D_MODEL + (h + 1) * HGRN_K] = (
                    dyv[:, ls] * nrm * ng_ref[:, ls] * g_h * (1.0 - g_h)).astype(dph_ref.dtype)
                dyg = dyv[:, ls] * g_h
                dng_ref[:, ls] += jnp.sum(dyg * nrm, axis=0, keepdims=True)
                dn = dyg * ng_ref[:, ls]
                do_s[:, ls] = r * (dn - nrm * jnp.mean(dn * nrm, axis=-1, keepdims=True))
            qm, km, qs, kl = (t[n].astype(MXU_DTYPE) for n in ("qm", "km", "qs", "kl"))
            v = hi_ref[rows, :].astype(MXU_DTYPE)
            do = do_s[...].astype(MXU_DTYPE)
            for h in range(HGRN_HEADS):
                ls = slice(h * HGRN_K, (h + 1) * HGRN_K)
                st = st_ref[c, ls, :]
                dst = dst_ref[h]
                a = jnp.where(causal, _nt(qm[:, ls], km[:, ls]), 0.0)
                da = jnp.where(causal, _nt(do[:, ls], v[:, ls]), 0.0)
                dv_s[:, ls] = _tn(a, do[:, ls]) + _nt(kl[:, ls], dst)
                dkl_s[:, ls] = _nn(v[:, ls], dst)
                dqs_s[:, ls] = _nn(do[:, ls], st)
                del_s[:, ls] = jnp.sum(dst * st, axis=0, keepdims=True)
                dst_ref[h] = _tn(do[:, ls], qs[:, ls]) + t["el"][:, ls] * dst
                dqm_s[:, ls] = _nn(da, km[:, ls])
                dkm_s[:, ls] = _tn(da, qm[:, ls])
            dqm, dkm, dqs, dkl = dqm_s[...], dkm_s[...], dqs_s[...], dkl_s[...]
            dq = dqm * t["e_qm"] + dqs * t["e_qs"]
            dk = dkm * t["e_km"] + dkl * t["e_kl"]
            t_qm, t_km, t_kl = dqm * t["qm"], dkm * t["km"], dkl * t["kl"]
            db = t_qm - t_km + dqs * t["qs"] - t_kl
            db_mid = jnp.sum(t_km - t_qm, axis=0, keepdims=True)
            db_last = jnp.sum(t_kl, axis=0, keepdims=True) + del_s[...] * t["el"]
            db = db + jnp.where(row == CHUNK // 2 - 1, db_mid, 0.0) + jnp.where(row == CHUNK - 1, db_last, 0.0)
            dlogf = _chunk_sum(anti, db)
            sq, sg, sgn, f = t["sq"], t["sg"], t["sgn"], t["f"]
            dph_ref[rows, 0:D_MODEL] = (dq * (sq * (1.0 + hq * (1.0 - sq)))).astype(dph_ref.dtype)
            dph_ref[rows, D_MODEL:2 * D_MODEL] = (
                dlogf * (1.0 - lb) * sg * (1.0 - sg) / f - dk * (1.0 - lb) * sgn * (1.0 - sgn)).astype(dph_ref.dtype)
            dph_ref[rows, 2 * D_MODEL:3 * D_MODEL] = dv_s[...].astype(dph_ref.dtype)
            dlb_ref[...] += jnp.sum(dlogf * (1.0 - sg) / f - dk * sgn, axis=0, keepdims=True)

        @pl.when(i == nblk - 1)
        def _():
            dl0 = dlb_ref[...] * lb * (1.0 - lb)
            dl_ref[0:1, :] = dl0
            dl_ref[1:2, :] = -dl0

    wide = pltpu.VMEM((CHUNK, D_MODEL), F32)
    return _call(
        "hgrn_bwd", body, grid=(nblk,),
        in_specs=[col(0), col(1), col(2), col(3), tok,
                  pl.BlockSpec((cpb, D_MODEL, HGRN_K), lambda i: (rev(i), 0, 0)), tok,
                  pl.BlockSpec((2, D_MODEL), lambda i: (0, 0)), pl.BlockSpec((1, D_MODEL), lambda i: (0, 0))],
        out_specs=[pl.BlockSpec((HGRN_TOKENS, 4 * D_MODEL), lambda i: (rev(i), 0)),
                   pl.BlockSpec((1, D_MODEL), lambda i: (0, 0)), pl.BlockSpec((2, D_MODEL), lambda i: (0, 0))],
        out_shape=[jax.ShapeDtypeStruct((T, 4 * D_MODEL), MXU_DTYPE), jax.ShapeDtypeStruct((1, D_MODEL), F32),
                   jax.ShapeDtypeStruct((2, D_MODEL), F32)],
        scratch=[pltpu.VMEM((HGRN_HEADS, HGRN_K, HGRN_K), F32), pltpu.VMEM((1, D_MODEL), F32),
                 wide, wide, wide, wide, wide, wide, pltpu.VMEM((1, D_MODEL), F32)],
        args=[ph, ph, ph, ph, o_raw, states, dy, lb_logits, norm_g], semantics=("arbitrary",), comm=comm)


def _local_step(x, target, vec, net):
    T, D = x.shape
    norm_mix_g, b_in, sinks, lb_logits = vec["norm_mix_g"], vec["b_in"], vec["attn_sinks"], vec["hgrn_lb_logits"]
    hgrn_norm_g, norm_ffn_g, norm_final_g = vec["hgrn_norm_g"], vec["norm_ffn_g"], vec["norm_final_g"]
    w_in = net.full("w_in")
    o_q, o_kv, o_h, o_g = (sum(IN_SPLITS[:i]) for i in range(4))
    w_q, w_kv, w_h, w_g = (_Rows(w_in, o, n) for o, n in zip((o_q, o_kv, o_h, o_g), IN_SPLITS))
    b_q, b_kv, b_h, b_g = (b_in[:, o:o + n] for o, n in zip((o_q, o_kv, o_h, o_g), IN_SPLITS))
    bias = lambda acc, b: (acc + b,)
    both = lambda acc: (acc, acc)
    grad_outs = [("mn", F32), ("mn", MXU_DTYPE)]
    row_vec = lambda n: ((1, n), lambda i, j: (0, j))
    tile = lambda tm, tn: ((tm, tn), lambda i, j: (i, j))
    TM = 512
    BIG = min(T, 1024)

    u = _rmsnorm_fwd("norm_mix", x, norm_mix_g)
    pq, = _matmul("in_q", [(u, w_q)], "nt", tm=TM, tn=1024, tk=1024, outs=[("mn", MXU_DTYPE)],
                  extras=[(b_q, *row_vec(1024))], epilogue=bias)
    pkv, = _matmul("in_kv", [(u, w_kv)], "nt", tm=TM, tn=256, tk=1024, outs=[("mn", MXU_DTYPE)],
                   extras=[(b_kv, *row_vec(256))], epilogue=bias)
    names = ("w_branch_attn", "w_branch_hgrn")
    (ph,), got = _matmul("in_h", [(u, w_h)], "nt", tm=TM, tn=1024, tk=1024, outs=[("mn", F32)],
                         extras=[(b_h, *row_vec(1024))], epilogue=bias, comm=net.gather(names))
    net.gathered(names, got)
    names = ("w_out",)
    (pg,), got = _matmul("in_g", [(u, w_g)], "nt", tm=TM, tn=1024, tk=1024, outs=[("mn", F32)],
                         extras=[(b_g, *row_vec(1024))], epilogue=bias, comm=net.gather(names))
    net.gathered(names, got)
    names = ("w_ffn_gate",)
    (y_attn, lse), got = _attn_fwd(pq, pkv, sinks, comm=net.gather(names))
    net.gathered(names, got)
    names = ("w_ffn_up",)
    (y_hgrn, o_raw, states), got = _hgrn_fwd(ph, lb_logits, hgrn_norm_g, comm=net.gather(names))
    net.gathered(names, got)
    w_ba, w_bh, w_out = net.full("w_branch_attn"), net.full("w_branch_hgrn"), net.full("w_out")
    w_gate, w_up = net.full("w_ffn_gate"), net.full("w_ffn_up")
    ya, = _matmul("branch_a", [(y_attn, w_ba)], "nn", tm=TM, tn=1024, tk=1024, outs=[("mn", F32)])
    gate_a = (pg, (TM, 1024), lambda i, j: (i, 0))
    gate_b = (pg, (TM, 1024), lambda i, j: (i, 1))

    def merge(acc, ya_v, ga, gb):
        return acc, _sigmoid(ga) * ya_v + _sigmoid(gb) * acc

    yb, merged = _matmul("branch_b", [(y_hgrn, w_bh)], "nn", tm=TM, tn=1024, tk=1024,
                         outs=[("mn", F32), ("mn", MXU_DTYPE)],
                         extras=[(ya, *tile(TM, 1024)), gate_a, gate_b], epilogue=merge)

    def resid_norm(acc, xin, g):
        h = xin + acc
        r = lax.rsqrt(jnp.mean(h * h, axis=-1, keepdims=True) + EPS)
        return h, h * r * g

    h1, u2 = _matmul("out_proj", [(merged, w_out)], "nn", tm=TM, tn=1024, tk=1024,
                     outs=[("mn", F32), ("mn", MXU_DTYPE)],
                     extras=[(x, *tile(TM, 1024)), (norm_ffn_g, *row_vec(1024))], epilogue=resid_norm)
    FT = FFN // 2
    names = ("w_ffn_down",)
    (gpre, up, z), got = _swiglu_fwd(u2, w_gate, w_up, tm=TM, tn=FT, comm=net.gather(names))
    net.gathered(names, got)
    w_down = net.full("w_ffn_down")

    def loss_head(acc, h1_v, tgt, g):
        h2 = h1_v + acc
        r = lax.rsqrt(jnp.mean(h2 * h2, axis=-1, keepdims=True) + EPS)
        xhat = h2 * r
        err = xhat * g - tgt
        part = 0.5 * jnp.sum(jnp.sum(err * err, axis=-1, keepdims=True), axis=0, keepdims=True) / D
        dyv = err / D
        dxh = dyv * g
        dh2 = r * (dxh - xhat * jnp.mean(dxh * xhat, axis=-1, keepdims=True))
        return dh2, dh2, jnp.sum(dyv * xhat, axis=0, keepdims=True), jnp.broadcast_to(part, (1, D))

    dh2, dh2b, dgf_p, loss_p = _matmul(
        "ffn_down", [(z, w_down)], "nn", tm=TM, tn=1024, tk=FFN,
        outs=[("mn", F32), ("mn", MXU_DTYPE), ("pn", F32), ("pn", F32)],
        extras=[(h1, *tile(TM, 1024)), (target, *tile(TM, 1024)), (norm_final_g, *row_vec(1024))],
        epilogue=loss_head)
    loss = jnp.sum(loss_p.reshape(-1, 8, D)[:, 0, 0])
    d_norm_final = _colsum_partials(dgf_p)

    def swiglu_bwd(acc, gv, upv):
        gv, upv = gv.astype(F32), upv.astype(F32)
        s = _sigmoid(gv)
        return acc * upv * (s * (1.0 + gv * (1.0 - s))), acc * (gv * s)

    dgp, dup = _matmul("d_ffn_hidden", [(dh2b, w_down)], "nt", tm=TM, tn=FT, tk=1024,
                       outs=[("mn", MXU_DTYPE), ("mn", MXU_DTYPE)],
                       extras=[(gpre, *tile(TM, FT)), (up, *tile(TM, FT))], epilogue=swiglu_bwd)
    d_w_down = _matmul("dw_down", [(z, dh2b)], "tn", tm=FT, tn=1024, tk=1024, outs=grad_outs, epilogue=both)

    def norm_ffn_bwd(acc, h1_v, g, dres):
        dx, dg = _rmsnorm_bwd_vals(acc, h1_v, g)
        dh = dres + dx
        return dh, dh, dg

    names = ("w_ffn_down",)
    (dh1, dh1b, dg2_p), got = _matmul(
        "d_ffn_in", [(dgp, w_gate), (dup, w_up)], "nn", tm=BIG, tn=1024, tk=FT,
        outs=[("mn", F32), ("mn", MXU_DTYPE), ("pn", F32)],
        extras=[(h1, *tile(BIG, 1024)), (norm_ffn_g, *row_vec(1024)), (dh2, *tile(BIG, 1024))],
        epilogue=norm_ffn_bwd,
        comm=net.exchange(dict(w_ffn_down=[d_w_down])))
    net.received(names, (), got)
    d_norm_ffn = _colsum_partials(dg2_p)
    d_w_gate = _matmul("dw_gate", [(dgp, u2)], "tn", tm=FT, tn=1024, tk=1024, outs=grad_outs, epilogue=both)
    d_w_up = _matmul("dw_up", [(dup, u2)], "tn", tm=FT, tn=1024, tk=1024, outs=grad_outs, epilogue=both)

    def merge_bwd(acc, ya_v, yb_v, ga, gb):
        sa, sb = _sigmoid(ga), _sigmoid(gb)
        return acc * sa, acc * sb, acc * ya_v * sa * (1.0 - sa), acc * yb_v * sb * (1.0 - sb)

    dya, dyb, dga, dgb = _matmul(
        "d_merged", [(dh1b, w_out)], "nt", tm=TM, tn=1024, tk=1024, outs=[("mn", MXU_DTYPE)] * 4,
        extras=[(ya, *tile(TM, 1024)), (yb, *tile(TM, 1024)), gate_a, gate_b], epilogue=merge_bwd)
    tn_grad = functools.partial(_matmul, mode="tn", tn=1024, tk=1024, outs=grad_outs, epilogue=both)
    d_w_out = tn_grad("dw_out", [(merged, dh1b)], tm=1024)
    dy_attn, = _matmul("d_y_attn", [(dya, w_ba)], "nt", tm=TM, tn=1024, tk=1024, outs=[("mn", MXU_DTYPE)])
    dy_hgrn, = _matmul("d_y_hgrn", [(dyb, w_bh)], "nt", tm=TM, tn=1024, tk=1024, outs=[("mn", F32)])
    d_w_ba = tn_grad("dw_branch_a", [(y_attn, dya)], tm=1024)
    d_w_bh = tn_grad("dw_branch_b", [(y_hgrn, dyb)], tm=1024)

    names, swap = ("w_ffn_gate",), ("w_ffn_down",)
    (dq, dkv, dsink), got = _attn_bwd(pq, pkv, sinks, lse, dy_attn,
                                      comm=net.exchange(dict(w_ffn_gate=[d_w_gate]), swap))
    net.received(names, swap, got)
    names, swap = ("w_ffn_up", "w_out", "w_branch_attn", "w_branch_hgrn"), ("w_ffn_gate",)
    (dph, d_hgrn_norm, d_lb_logits), got = _hgrn_bwd(
        ph, o_raw, states, dy_hgrn, lb_logits, hgrn_norm_g,
        comm=net.exchange(dict(w_ffn_up=[d_w_up], w_out=[d_w_out], w_branch_attn=[d_w_ba],
                               w_branch_hgrn=[d_w_bh]), swap))
    net.received(names, swap, got)

    d_w_in = [tn_grad("dw_in_q", [(dq, u)], tm=1024, a_colsum=True),
              tn_grad("dw_in_kv", [(dkv, u)], tm=256, a_colsum=True),
              tn_grad("dw_in_h", [(dph, u)], tm=1024),
              tn_grad("dw_in_ga", [(dga, u)], tm=1024, a_colsum=True),
              tn_grad("dw_in_gb", [(dgb, u)], tm=1024, a_colsum=True)]

    def norm_mix_bwd(acc, xin, g, dres):
        dx, dg = _rmsnorm_bwd_vals(acc, xin, g)
        return dres + dx, dg

    db_h, got = _colsum("db_h", dph, comm=net.presum_begin("w_in", d_w_in))
    halves = net.presum_end("w_in", got)
    names, swap = ("w_in",), ("w_ffn_up", "w_out", "w_branch_attn", "w_branch_hgrn")
    (dx, dg1_p), got = _matmul(
        "d_u", [(dq, w_q), (dkv, w_kv), (dph, w_h), (dga, _Rows(w_in, o_g, D)), (dgb, _Rows(w_in, o_g + D, D))], "nn",
        tm=BIG, tn=1024, tk=512, outs=[("mn", F32), ("pn", F32)],
        extras=[(x, *tile(BIG, 1024)), (norm_mix_g, *row_vec(1024)), (dh1, *tile(BIG, 1024))],
        epilogue=norm_mix_bwd,
        comm=_join(halves, net.swap(swap)))
    net.last = (names, swap, got)
    d_norm_mix = _colsum_partials(dg1_p)
    d_b_in = jnp.concatenate([d_w_in[0][2], d_w_in[1][2], db_h, d_w_in[3][2], d_w_in[4][2]], axis=1)
    vecs = dict(norm_mix_g=d_norm_mix, b_in=d_b_in, attn_sinks=jnp.sum(dsink.reshape(Q_HEADS, ATTN_BLOCK), axis=1).reshape(1, Q_HEADS),
                hgrn_lb_logits=d_lb_logits,
                hgrn_norm_g=d_hgrn_norm, norm_ffn_g=d_norm_ffn, norm_final_g=d_norm_final)
    return loss, dx, vecs


def _place():
    return lax.axis_index("x"), lax.axis_index("y"), lax.axis_index("c")


def _other_chips(x, y):
    return [(1 - x, y), (x, 1 - y), (1 - x, 1 - y)]


def _y_first(copies):
    return [copies[3 * (i // 3) + (1, 0, 2)[i % 3]] for i in range(len(copies))]


def _gather_copies(shards):
    n = len(shards)

    def build(ins, outs, send_sems, recv_sems, local_sems):
        x, y, c = _place()
        mine = 2 * x + y
        local = [pltpu.make_async_copy(ins[w], outs[w].at[mine], local_sems.at[w]) for w in range(n)]
        sends, recvs = [], []
        for w in range(n):
            for k, (px, py) in enumerate(_other_chips(x, y)):
                sem = 3 * w + k
                sends.append(pltpu.make_async_remote_copy(
                    src_ref=ins[w], dst_ref=outs[w].at[mine], send_sem=send_sems.at[sem], recv_sem=recv_sems.at[sem],
                    device_id=(px, py, c), device_id_type=MESH_ID))
                recvs.append(pltpu.make_async_remote_copy(
                    src_ref=ins[w], dst_ref=outs[w].at[2 * px + py], send_sem=send_sems.at[sem],
                    recv_sem=recv_sems.at[sem], device_id=(px, py, c), device_id_type=MESH_ID))
        return sends, recvs, local, _y_first(sends)

    return _Carried(shards, [jax.ShapeDtypeStruct((N_CHIPS,) + s.shape, s.dtype) for s in shards], 3 * n, n, build)


def _grad_copies(stacked):
    n = len(stacked)

    def build(ins, outs, send_sems, recv_sems, local_sems):
        x, y, c = _place()
        sends = []
        for w in range(n):
            for k, (px, py) in enumerate(_other_chips(x, y)):
                sem = 3 * w + k
                sends.append(pltpu.make_async_remote_copy(
                    src_ref=ins[w].at[2 * px + py], dst_ref=outs[w].at[k], send_sem=send_sems.at[sem],
                    recv_sem=recv_sems.at[sem], device_id=(px, py, c), device_id_type=MESH_ID))
        return sends, sends, [], _y_first(sends)

    return _Carried(stacked, [jax.ShapeDtypeStruct((3,) + s.shape[1:], s.dtype) for s in stacked], 3 * n, 0, build)


def _small_copies(small):
    def build(ins, outs, send_sems, recv_sems, local_sems):
        small_ref, all_ref = ins[0], outs[0]
        x, y, c = _place()
        me = 4 * x + 2 * y + c
        sends, recvs = [], []
        for r in range(1, 8):
            px = 1 - x if r & 4 else x
            py = 1 - y if r & 2 else y
            pc = 1 - c if r & 1 else c
            sends.append(pltpu.make_async_remote_copy(
                src_ref=small_ref, dst_ref=all_ref.at[me], send_sem=send_sems.at[r - 1], recv_sem=recv_sems.at[r - 1],
                device_id=(px, py, pc), device_id_type=MESH_ID))
            recvs.append(pltpu.make_async_remote_copy(
                src_ref=small_ref, dst_ref=all_ref.at[4 * px + 2 * py + pc], send_sem=send_sems.at[r - 1],
                recv_sem=recv_sems.at[r - 1], device_id=(px, py, pc), device_id_type=MESH_ID))
        return sends, recvs, [pltpu.make_async_copy(small_ref, all_ref.at[me], local_sems.at[0])]

    return _Carried([small], [jax.ShapeDtypeStruct((8,) + small.shape, small.dtype)], 7, 1, build)


def _gather_by_neighbours(name, shard):
    half = shard.shape[0] // 2
    quarter = half // 2

    def body(in_ref, out_ref, send_sems, recv_sems, local_sem):
        for core in (0, 1):
            @pl.when(lax.axis_index("c") == core)
            def _():
                program(core, in_ref, out_ref, send_sems, recv_sems, local_sem)

    def program(c, in_ref, out_ref, send_sems, recv_sems, local_sem):
        x, y, _ = _place()
        chip = lambda px, py: 2 * px + py
        to_x, to_y, sibling = (1 - x, y, c), (x, 1 - y, c), (x, y, 1 - c)
        x_blk, y_blk, d_blk = chip(1 - x, y), chip(x, 1 - y), chip(1 - x, 1 - y)
        mine, theirs = c * half, (1 - c) * half

        def copy(sem, rows, block, to, src=None):
            place = out_ref.at[block, pl.ds(rows[0], rows[1])]
            return pltpu.make_async_remote_copy(
                src_ref=place if src is None else src, dst_ref=place, send_sem=send_sems.at[sem],
                recv_sem=recv_sems.at[sem], device_id=to, device_id_type=MESH_ID)

        own = pltpu.make_async_copy(in_ref, out_ref.at[chip(x, y)], local_sem)
        own.start()
        my_rows = in_ref.at[pl.ds(mine, half)]
        along_x = dict(send=copy(0, (mine, half), chip(x, y), to_x, src=my_rows),
                       landed=copy(0, (mine, half), x_blk, to_x),
                       onward=[copy(3, (mine + quarter, quarter), x_blk, to_y), copy(4, (mine, half), x_blk, sibling)],
                       diagonal=copy(2, (mine, quarter), d_blk, to_x))
        along_y = dict(send=copy(1, (mine, half), chip(x, y), to_y, src=my_rows),
                       landed=copy(1, (mine, half), y_blk, to_y),
                       onward=[copy(2, (mine, quarter), y_blk, to_x), copy(5, (mine, half), y_blk, sibling)],
                       diagonal=copy(3, (mine + quarter, quarter), d_blk, to_y))
        last = copy(6, (mine, half), d_blk, sibling)

        order = (along_x, along_y) if c == 0 else (along_y, along_x)
        for axis in order:
            axis["send"].start()
        for axis in order:
            axis["landed"].wait_recv()
            for cp in axis["onward"]:
                cp.start()
        for axis in order:
            axis["diagonal"].wait_recv()
        last.start()
        for sem, block in ((4, x_blk), (5, y_blk), (6, d_blk)):
            copy(sem, (theirs, half), block, sibling).wait_recv()
        for cp in [along_x["send"], along_y["send"]] + along_x["onward"] + along_y["onward"] + [last]:
            cp.wait_send()
        own.wait()

    return pl.pallas_call(
        body, name=name, in_specs=[HBM_SPEC], out_specs=HBM_SPEC,
        out_shape=jax.ShapeDtypeStruct((N_CHIPS,) + shard.shape, shard.dtype),
        scratch_shapes=[pltpu.SemaphoreType.DMA((7,)), pltpu.SemaphoreType.DMA((7,)), pltpu.SemaphoreType.DMA(())],
    )(shard)


def _copies_alone(name, comm):
    return _call(name, lambda: None, grid=(), in_specs=[], out_specs=[], out_shape=[], args=[], comm=comm)[1]


class _Net:
    def __init__(self, shards):
        self.shards = shards
        self.whole, self.own, self.theirs, self.sums, self.other = {}, {}, {}, {}, {}
        x, y, _ = _place()
        self.chip = 2 * x + y

    def gather(self, names):
        return _gather_copies([self.shards[n] for n in names])

    def gathered(self, names, got):
        for n, g in zip(names, got):
            self.whole[n] = g.reshape(-1, g.shape[-1])

    def full(self, name):
        return self.whole[name]

    def exchange(self, grads, swap=()):
        stacked = []
        for n, pieces in grads.items():
            keep = jnp.concatenate([p[0] for p in pieces], axis=0) if len(pieces) > 1 else pieces[0][0]
            send = jnp.concatenate([p[1] for p in pieces], axis=0) if len(pieces) > 1 else pieces[0][1]
            rows = keep.shape[0] // N_CHIPS
            self.own[n] = lax.dynamic_slice_in_dim(keep, self.chip * rows, rows, axis=0)
            stacked.append(send.reshape(N_CHIPS, rows, send.shape[-1]))
        return _join(_grad_copies(stacked), self.swap(swap))

    def swap(self, names):
        return _sibling_copies([self.sums[n] for n in names]) if names else None

    def presum_begin(self, name, pieces):
        keep = jnp.concatenate([p[0] for p in pieces], axis=0)
        send = jnp.concatenate([p[1] for p in pieces], axis=0)
        rows = keep.shape[0] // N_CHIPS
        self.held = keep.reshape(N_CHIPS, rows, keep.shape[-1])
        return _half_rows_copies(send.reshape(N_CHIPS, rows, send.shape[-1]))

    def presum_end(self, name, got):
        x, y, c = _place()
        to_send, self.own[name] = _pre_sum("presum_" + name, self.held, got[0], jnp.stack([c, self.chip]))
        return _grad_copies([to_send])

    def received(self, names, swap, got, carried=None):
        self.theirs.update(zip(names, got[:len(names)]))
        self.other.update(zip(swap, got[len(names):]))
        for n in names:
            (self.sums[n],), more = _partial_sum("sum_" + n, self.own[n], self.theirs[n], comm=carried)
        return more


def _half_rows_copies(stacked):
    n, rows = stacked.shape[0], stacked.shape[1] // 2

    def build(ins, outs, send_sems, recv_sems, local_sems):
        x, y, c = _place()
        copies = [pltpu.make_async_remote_copy(
            src_ref=ins[0].at[s, pl.ds((1 - c) * rows, rows)], dst_ref=outs[0].at[s], send_sem=send_sems.at[s],
            recv_sem=recv_sems.at[s], device_id=(x, y, 1 - c), device_id_type=MESH_ID) for s in range(n)]
        return copies, copies, []

    return _Carried([stacked], [jax.ShapeDtypeStruct((n, rows, stacked.shape[2]), stacked.dtype)], n, 0, build)


def _pre_sum(name, held, theirs, core_and_chip):
    n, R, C = held.shape
    half = R // 2
    tr = _row_tile(half)
    per_half = half // tr

    def body(place_ref, h_ref, t_ref, send_ref, own_ref):
        total = h_ref[0] + t_ref[0].astype(F32)
        send_ref[0] = total.astype(send_ref.dtype)

        @pl.when(pl.program_id(1) == place_ref[1])
        def _():
            own_ref[...] = total

    return pl.pallas_call(
        body, name=name,
        grid_spec=pltpu.PrefetchScalarGridSpec(
            num_scalar_prefetch=1, grid=(per_half, n),
            in_specs=[pl.BlockSpec((1, tr, C), lambda i, s, place: (s, place[0] * per_half + i, 0)),
                      pl.BlockSpec((1, tr, C), lambda i, s, place: (s, i, 0))],
            out_specs=[pl.BlockSpec((1, tr, C), lambda i, s, place: (s, i, 0)),
                       pl.BlockSpec((tr, C), lambda i, s, place: (i, 0))]),
        out_shape=[jax.ShapeDtypeStruct((n, half, C), MXU_DTYPE), jax.ShapeDtypeStruct((half, C), F32)],
        compiler_params=_params(("arbitrary", "arbitrary")),
    )(core_and_chip, held, theirs)


def _sibling_copies(parts):
    n = len(parts)

    def build(ins, outs, send_sems, recv_sems, local_sems):
        x, y, c = _place()
        copies = [pltpu.make_async_remote_copy(
            src_ref=ins[w], dst_ref=outs[w], send_sem=send_sems.at[w], recv_sem=recv_sems.at[w],
            device_id=(x, y, 1 - c), device_id_type=MESH_ID) for w in range(n)]
        return copies, copies, []

    return _Carried(parts, [jax.ShapeDtypeStruct(p.shape, p.dtype) for p in parts], n, 0, build)


def _row_tile(rows, most=512, sublanes=16):
    return max(t for t in range(sublanes, most + 1, sublanes) if rows % t == 0)


def _partial_sum(name, own, recv, comm=None):
    R, C = own.shape
    tr = _row_tile(R, most=128)

    def body(o_ref, r_ref, p_ref):
        p_ref[...] = ((o_ref[...] + r_ref[0].astype(F32)) + r_ref[1].astype(F32)) + r_ref[2].astype(F32)

    return _call(name, body, grid=(R // tr,),
                 in_specs=[pl.BlockSpec((tr, C), lambda i: (i, 0)), pl.BlockSpec((3, tr, C), lambda i: (0, i, 0))],
                 out_specs=[pl.BlockSpec((tr, C), lambda i: (i, 0))], out_shape=[jax.ShapeDtypeStruct((R, C), F32)],
                 args=[own, recv], semantics=("parallel",), comm=comm)


def _adam_vals(w, g, m, v):
    m = ADAM_B1 * m + (1.0 - ADAM_B1) * g
    v = ADAM_B2 * v + (1.0 - ADAM_B2) * (g * g)
    m_hat = m / (1.0 - ADAM_B1 ** ADAM_STEP)
    v_hat = v / (1.0 - ADAM_B2 ** ADAM_STEP)
    delta = -ADAM_LR * (m_hat / (jnp.sqrt(v_hat) + ADAM_EPS) + ADAM_WD * w)
    return delta, m, v


def _adamw(name, w, m, v, mine, other, comm=None):
    R, C = w.shape
    tr = _row_tile(R)

    def body(w_ref, m_ref, v_ref, s_ref, n_ref, g_ref, d_ref, nm_ref, nv_ref):
        g = s_ref[...] + n_ref[...]
        d, nm, nv = _adam_vals(w_ref[...], g, m_ref[...], v_ref[...])
        g_ref[...], d_ref[...], nm_ref[...], nv_ref[...] = g, d, nm, nv

    spec = pl.BlockSpec((tr, C), lambda i: (i, 0))
    return _call(name, body, grid=(R // tr,), in_specs=[spec] * 5, out_specs=[spec] * 4,
                 out_shape=[jax.ShapeDtypeStruct((R, C), F32)] * 4, args=[w, m, v, mine, other],
                 semantics=("parallel",), comm=comm)


def _adamw_by_halves(name, w, m, v, mine, other, core):
    R, C = w.shape
    tr = _row_tile(R // 2)
    per_half = R // 2 // tr

    def body(c_ref, w_ref, m_ref, v_ref, s_ref, n_ref, g_ref, d_ref, nm_ref, nv_ref):
        g = jnp.where(pl.program_id(0) // per_half == c_ref[0, 0], s_ref[...], n_ref[...])
        d, nm, nv = _adam_vals(w_ref[...], g, m_ref[...], v_ref[...])
        g_ref[...], d_ref[...], nm_ref[...], nv_ref[...] = g, d, nm, nv

    spec = pl.BlockSpec((tr, C), lambda i: (i, 0))
    part = pl.BlockSpec((tr, C), lambda i: (i % per_half, 0))
    return pl.pallas_call(
        body, name=name, grid=(R // tr,),
        in_specs=[pl.BlockSpec(memory_space=pltpu.SMEM), spec, spec, spec, part, part], out_specs=[spec] * 4,
        out_shape=[jax.ShapeDtypeStruct((R, C), F32)] * 4, compiler_params=_params(("parallel",)),
    )(core, w, m, v, mine, other)


SMALL_LAYOUT = dict(norm_mix_g=(0, 1, 1024), b_in=(1, 8, 7424), hgrn_norm_g=(9, 1, 1024), norm_ffn_g=(10, 1, 1024),
                    norm_final_g=(11, 1, 1024), hgrn_lb_logits=(12, 2, 2048), attn_sinks=(14, 1, 16))
SMALL_LOSS_ROW, SMALL_ROWS = 15, 16


def _pack_small(grads, loss):
    rows = [jnp.pad(grads[name].astype(F32).reshape(-1), (0, nrows * D_MODEL - n))
            for name, (_, nrows, n) in SMALL_LAYOUT.items()]
    rows.append(jnp.pad(loss.astype(F32).reshape(1), (0, D_MODEL - 1)))
    return jnp.concatenate(rows).reshape(SMALL_ROWS, D_MODEL)


def _adamw_small(w, m, v, g_all):
    names = list(SMALL_LAYOUT)
    n = len(names)

    def body(a_ref, *refs):
        ins, outs = refs[:3 * n], refs[3 * n:]
        g_all_rows = a_ref[0]
        for dev in range(1, 8):
            g_all_rows = g_all_rows + a_ref[dev]
        for i, name in enumerate(names):
            first, nrows, count = SMALL_LAYOUT[name]
            w_ref, m_ref, v_ref = ins[3 * i:3 * i + 3]
            if w_ref.shape[0] == nrows:
                g = g_all_rows[first:first + nrows, :w_ref.shape[1]]
            else:
                last = count - (nrows - 1) * D_MODEL
                g = jnp.concatenate([g_all_rows[r:r + 1, :] for r in range(first, first + nrows - 1)]
                                    + [g_all_rows[first + nrows - 1:first + nrows, :last]], axis=1)
            d, nm, nv = _adam_vals(w_ref[...], g, m_ref[...], v_ref[...])
            for o_ref, val in zip(outs[4 * i:4 * i + 4], (g, d, nm, nv)):
                o_ref[...] = val
        outs[4 * n][...] = g_all_rows[SMALL_LOSS_ROW:SMALL_LOSS_ROW + 1, 0:1]

    res = pl.pallas_call(
        body, name="adamw_small",
        out_shape=[jax.ShapeDtypeStruct(w[name].shape, F32) for name in names for _ in range(4)]
        + [jax.ShapeDtypeStruct((1, 1), F32)],
    )(g_all, *[t[name] for name in names for t in (w, m, v)])
    return {name: res[4 * i:4 * i + 4] for i, name in enumerate(names)}, res[4 * n]


MATRICES = ("w_in", "w_branch_attn", "w_branch_hgrn", "w_out", "w_ffn_gate", "w_ffn_up", "w_ffn_down")
COLUMN_SHARDED = ("w_in", "w_ffn_gate", "w_ffn_up")
WEIGHTS = ("norm_mix_g", "w_in", "b_in", "attn_sinks", "hgrn_lb_logits", "hgrn_norm_g", "w_branch_attn",
           "w_branch_hgrn", "w_out", "norm_ffn_g", "w_ffn_gate", "w_ffn_up", "w_ffn_down", "norm_final_g")


def kernel(x, norm_mix_g, w_in, b_in, attn_sinks, hgrn_lb_logits, hgrn_norm_g, w_branch_attn, w_branch_hgrn, w_out, norm_ffn_g, w_ffn_gate, w_ffn_up, w_ffn_down, norm_final_g, loss_target, m_norm_mix_g, m_w_in, m_b_in, m_attn_sinks, m_hgrn_lb_logits, m_hgrn_norm_g, m_w_branch_attn, m_w_branch_hgrn, m_w_out, m_norm_ffn_g, m_w_ffn_gate, m_w_ffn_up, m_w_ffn_down, m_norm_final_g, v_norm_mix_g, v_w_in, v_b_in, v_attn_sinks, v_hgrn_lb_logits, v_hgrn_norm_g, v_w_branch_attn, v_w_branch_hgrn, v_w_out, v_norm_ffn_g, v_w_ffn_gate, v_w_ffn_up, v_w_ffn_down, v_norm_final_g):
    given = dict(locals())
    w = {n: given[n] for n in WEIGHTS}
    m = {n: given["m_" + n] for n in WEIGHTS}
    v = {n: given["v_" + n] for n in WEIGHTS}

    block = lambda a, n: jnp.transpose(a[0]) if n in COLUMN_SHARDED else a[0]
    unblock = lambda a, n: (jnp.transpose(a) if n in COLUMN_SHARDED else a)[None]
    net = _Net({n: block(w[n], n).astype(MXU_DTYPE) for n in MATRICES})
    net.gathered(("w_in",), [_gather_by_neighbours("gather_w_in", net.shards["w_in"])])
    vec = dict(norm_mix_g=norm_mix_g, b_in=b_in, attn_sinks=attn_sinks, hgrn_lb_logits=hgrn_lb_logits,
               hgrn_norm_g=hgrn_norm_g, norm_ffn_g=norm_ffn_g, norm_final_g=norm_final_g.reshape(1, D_MODEL))
    loss_part, dx, d_vecs = _local_step(x[0], loss_target[0], vec, net)

    small_all, = net.received(*net.last, carried=_small_copies(_pack_small(d_vecs, loss_part)))
    grads, deltas, new_m, new_v = {}, {}, {}, {}
    for n in ("w_ffn_down", "w_ffn_gate", "w_ffn_up", "w_out", "w_branch_attn", "w_branch_hgrn"):
        res, got = _adamw("adamw_" + n, block(w[n], n), block(m[n], n), block(v[n], n), net.sums[n], net.other[n],
                          comm=net.swap(("w_in",)) if n == "w_ffn_down" else None)
        if n == "w_ffn_down":
            net.other["w_in"], = got
        grads[n], deltas[n], new_m[n], new_v[n] = (unblock(r, n) for r in res)
    n = "w_in"
    res = _adamw_by_halves("adamw_" + n, block(w[n], n), block(m[n], n), block(v[n], n), net.sums[n], net.other[n],
                           _place()[2].reshape(1, 1))
    grads[n], deltas[n], new_m[n], new_v[n] = (unblock(r, n) for r in res)
    rows = lambda t: {n: t[n].reshape(-1, t[n].shape[-1]) for n in SMALL_LAYOUT}
    res, loss = _adamw_small(rows(w), rows(m), rows(v), small_all)
    for n, four in res.items():
        grads[n], deltas[n], new_m[n], new_v[n] = (r.reshape(w[n].shape) for r in four)
    loss = loss.reshape(())
    return (loss, dx[None], *[grads[n] for n in WEIGHTS], *[deltas[n] for n in WEIGHTS],
            *[new_m[n] for n in WEIGHTS], *[new_v[n] for n in WEIGHTS])
```

```python
import collections
import functools
import math

import jax
import jax.numpy as jnp
from jax import lax
from jax.experimental import pallas as pl
from jax.experimental.pallas import tpu as pltpu

F32 = jnp.float32
BF16 = jnp.bfloat16
MXU_DTYPE = jnp.bfloat16
SAVED_DTYPE = jnp.bfloat16
MESH_ID = pl.DeviceIdType.MESH

D_MODEL = 1024
HEAD_DIM = 64
Q_HEADS = 16
KV_HEADS = 2
GROUP = Q_HEADS // KV_HEADS
KV_WIDTH = KV_HEADS * HEAD_DIM
ATTN_BLOCK = 128
HGRN_HEADS = 8
HGRN_K = 128
CHUNK = 64
HGRN_TOKENS = 256
FFN = 2816
IN_SPLITS = (1024, 256, 4096, 2048)
EPS = 1e-6
NEG_INF = -1e30
ADAM_LR, ADAM_B1, ADAM_B2, ADAM_EPS, ADAM_WD, ADAM_STEP = 0.001, 0.9, 0.999, 1e-08, 0.01, 10
N_CHIPS = 4
VMEM_LIMIT = 60 * 1024 * 1024
ROW_ALIGN = 16


def _params(sem=None):
    return pltpu.CompilerParams(dimension_semantics=sem, vmem_limit_bytes=VMEM_LIMIT)


def _sigmoid(v):
    return 0.5 * jnp.tanh(0.5 * v) + 0.5


def _dot(a, b, dims):
    return lax.dot_general(a.astype(MXU_DTYPE), b.astype(MXU_DTYPE), (dims, ((), ())),
                           preferred_element_type=F32)


def _nn(a, b):
    return _dot(a, b, ((1,), (0,)))


def _nt(a, b):
    return _dot(a, b, ((1,), (1,)))


def _tn(a, b):
    return _dot(a, b, ((0,), (0,)))


HBM_SPEC = pl.BlockSpec(memory_space=pl.ANY)


class _Carried:
    def __init__(self, arrays, out_shapes, n_remote, n_local, build):
        self.parts = [(len(arrays), len(out_shapes), build)]
        self.arrays, self.out_shapes = list(arrays), list(out_shapes)
        self.scratch = [pltpu.SemaphoreType.DMA((n_remote,)), pltpu.SemaphoreType.DMA((n_remote,)),
                        pltpu.SemaphoreType.DMA((max(n_local, 1),))]

    def __add__(self, other):
        both = _Carried([], [], 1, 0, None)
        both.parts = self.parts + other.parts
        both.arrays, both.out_shapes = self.arrays + other.arrays, self.out_shapes + other.out_shapes
        both.scratch = self.scratch + other.scratch
        return both

    def _built(self, ins, outs, sems):
        for p, (ni, no, build) in enumerate(self.parts):
            yield build(ins[:ni], outs[:no], *sems[3 * p:3 * p + 3])
            ins, outs = ins[ni:], outs[no:]

    def start(self, ins, outs, sems):
        core = lax.axis_index("c")
        for sends, _, local, *other_order in self._built(ins, outs, sems):
            for cp in local:
                cp.start()
            if not other_order:
                for cp in sends:
                    cp.start()
                continue

            @pl.when(core == 0)
            def _():
                for cp in sends:
                    cp.start()

            @pl.when(core == 1)
            def _():
                for cp in other_order[0]:
                    cp.start()

    def wait(self, ins, outs, sems):
        for sends, recvs, local, *_ in self._built(ins, outs, sems):
            for cp in recvs:
                cp.wait_recv()
            for cp in sends:
                cp.wait_send()
            for cp in local:
                cp.wait()


def _join(*comms):
    comms = [c for c in comms if c is not None]
    return functools.reduce(lambda a, b: a + b, comms) if comms else None


def _call(name, body, *, grid, in_specs, out_specs, out_shape, args, scratch=(), semantics=None, comm=None,
          aliases=None):
    n_in, n_out, n_scr = len(in_specs), len(out_specs), len(scratch)
    aliases = aliases or {}
    if comm is None:
        res = pl.pallas_call(body, name=name, grid=grid, in_specs=in_specs, out_specs=out_specs, out_shape=out_shape,
                             scratch_shapes=list(scratch), input_output_aliases=aliases,
                             compiler_params=_params(semantics))(*args)
        return list(res), []
    ci, co = len(comm.arrays), len(comm.out_shapes)

    def carrying(*refs):
        ins, refs = refs[:n_in], refs[n_in:]
        c_ins, refs = refs[:ci], refs[ci:]
        outs, refs = refs[:n_out], refs[n_out:]
        c_outs, refs = refs[:co], refs[co:]
        scr, sems = refs[:n_scr], refs[n_scr:]
        if not grid:
            comm.start(c_ins, c_outs, sems)
            body(*ins, *outs, *scr)
            comm.wait(c_ins, c_outs, sems)
            return
        first = functools.reduce(jnp.logical_and, [pl.program_id(a) == 0 for a in range(len(grid))])
        last = functools.reduce(jnp.logical_and, [pl.program_id(a) == g - 1 for a, g in enumerate(grid)])

        @pl.when(first)
        def _():
            comm.start(c_ins, c_outs, sems)

        body(*ins, *outs, *scr)

        @pl.when(last)
        def _():
            comm.wait(c_ins, c_outs, sems)

    res = pl.pallas_call(
        carrying, name=name, grid=grid, in_specs=list(in_specs) + [HBM_SPEC] * ci,
        out_specs=list(out_specs) + [HBM_SPEC] * co, out_shape=list(out_shape) + comm.out_shapes,
        scratch_shapes=list(scratch) + comm.scratch, input_output_aliases=aliases,
        compiler_params=_params(("arbitrary",) * len(grid) if grid else None),
    )(*args, *comm.arrays)
    return list(res[:n_out]), list(res[n_out:])


_NO_COPIES = object()
_Rows = collections.namedtuple("_Rows", "array first rows")


_Into = collections.namedtuple("_Into", "rows first held")


def _matmul(name, pairs, mode, *, tm, tn, tk, outs, extras=(), epilogue=None, a_colsum=False, into=None,
            comm=_NO_COPIES):
    prod = dict(nn=_nn, nt=_nt, tn=_tn)[mode]
    pairs = [(a, b if isinstance(b, _Rows) else _Rows(b, 0, b.shape[0])) for a, b in pairs]
    a0, b0 = pairs[0]
    M = a0.shape[1] if mode == "tn" else a0.shape[0]
    N = b0.rows if mode == "nt" else b0.array.shape[1]
    steps, in_specs, offset = [], [], 0
    for a, b in pairs:
        K = a.shape[0] if mode == "tn" else a.shape[1]
        t = min(tk, K)
        assert K % t == 0, (name, K, t)
        kmap = functools.partial(lambda k, off, n: jnp.clip(k - off, 0, n - 1), off=offset, n=K // t)
        if mode == "tn":
            in_specs.append(pl.BlockSpec((t, tm), functools.partial(lambda i, j, k, f: (f(k), i), f=kmap)))
        else:
            in_specs.append(pl.BlockSpec((tm, t), functools.partial(lambda i, j, k, f: (i, f(k)), f=kmap)))
        whole = b.first == 0 and b.rows == b.array.shape[0]
        if mode == "nt":
            shape = (tn, t) if whole else (pl.Element(tn), pl.Element(t))
            in_specs.append(pl.BlockSpec(shape, functools.partial(
                lambda i, j, k, f, b, t, whole: (j, f(k)) if whole else (
                    pl.multiple_of(b.first + j * tn, ROW_ALIGN), pl.multiple_of(f(k) * t, 128)),
                f=kmap, b=b, t=t, whole=whole)))
        else:
            assert b.rows == K, (name, b.rows, K)
            shape = (t, tn) if whole else (pl.Element(t), pl.Element(tn))
            in_specs.append(pl.BlockSpec(shape, functools.partial(
                lambda i, j, k, f, b, t, whole: (f(k), j) if whole else (
                    pl.multiple_of(b.first + f(k) * t, ROW_ALIGN), pl.multiple_of(j * tn, 128)),
                f=kmap, b=b, t=t, whole=whole)))
        steps.append((offset, offset + K // t))
        offset += K // t
    assert M % tm == 0 and N % tn == 0, (name, M, N, tm, tn)
    ni, nj, nk = M // tm, N // tn, offset
    npair, ne, no = len(pairs), len(extras), len(outs)
    if epilogue is None:
        epilogue = lambda acc: (acc,)

    def finish(acc, extra_refs, out_refs):
        vals = epilogue(acc, *[r[...] for r in extra_refs])
        for (kind, _), o_ref, val in zip(outs, out_refs, vals):
            if kind == "pn":
                val = jnp.broadcast_to(val, o_ref.shape)
            o_ref[...] = val.astype(o_ref.dtype)

    held = list(into.held) if into is not None and into.held is not None else []

    def body(*refs):
        ab, rest = refs[:2 * npair], refs[2 * npair:]
        extra_refs, rest = rest[:ne], rest[ne + len(held):]
        out_refs = rest[:no]
        if nk == 1:
            finish(prod(ab[0][...], ab[1][...]), extra_refs, out_refs)
            return
        sums_ref, acc_ref = rest[no], rest[-1]
        k = pl.program_id(2)

        @pl.when(k == 0)
        def _():
            acc_ref[...] = jnp.zeros_like(acc_ref)
            if a_colsum:
                sums_ref[...] = jnp.zeros((1, tm), F32)

        if a_colsum:
            sums_ref[...] += jnp.sum(ab[0][...].astype(F32), axis=0, keepdims=True)
        for p, (lo, hi) in enumerate(steps):
            @pl.when(jnp.logical_and(k >= lo, k < hi))
            def _():
                acc_ref[...] += prod(ab[2 * p][...], ab[2 * p + 1][...])

        @pl.when(k == nk - 1)
        def _():
            finish(acc_ref[...], extra_refs, out_refs)

    for _, shape, im in extras:
        in_specs.append(pl.BlockSpec(shape, functools.partial(lambda i, j, k, im: im(i, j), im=im)))
    in_specs += [HBM_SPEC] * len(held)
    aliases = {2 * npair + ne + p: p for p in range(len(held))}
    out_shape, out_specs = [], []
    for kind, dt in outs:
        if kind == "mn" and into is not None:
            out_shape.append(jax.ShapeDtypeStruct((into.rows, N), dt))
            out_specs.append(pl.BlockSpec((pl.Element(tm), pl.Element(tn)), lambda i, j, k: (
                pl.multiple_of(into.first + i * tm, ROW_ALIGN), pl.multiple_of(j * tn, 128))))
        elif kind == "mn":
            out_shape.append(jax.ShapeDtypeStruct((M, N), dt))
            out_specs.append(pl.BlockSpec((tm, tn), lambda i, j, k: (i, j)))
        else:
            out_shape.append(jax.ShapeDtypeStruct((8 * ni, N), dt))
            out_specs.append(pl.BlockSpec((8, tn), lambda i, j, k: (i, j)))
    if a_colsum:
        assert mode == "tn" and npair == 1 and nj == 1 and nk > 1, name
        out_shape.append(jax.ShapeDtypeStruct((1, M), F32))
        out_specs.append(pl.BlockSpec((1, tm), lambda i, j, k: (0, i)))
    grid = (ni, nj, nk)
    if nj > 1 and nk == 1:
        turned = lambda spec: pl.BlockSpec(spec.block_shape, functools.partial(
            lambda j, i, k, im: im(i, j, k), im=spec.index_map))
        in_specs, out_specs, grid = [turned(s) for s in in_specs], [turned(s) for s in out_specs], (nj, ni, nk)
    res, got = _call(name, body, grid=grid, in_specs=in_specs, out_specs=out_specs, out_shape=out_shape,
                     args=[t for a, b in pairs for t in (a, b.array)] + [e[0] for e in extras] + held,
                     scratch=[pltpu.VMEM((tm, tn), F32)] if nk > 1 else [], aliases=aliases,
                     semantics=("parallel", "parallel", "arbitrary"), comm=None if comm is _NO_COPIES else comm)
    return res if comm is _NO_COPIES else (res, got)


def _swiglu_fwd(u, w_gate_t, w_up_t, *, tm, tn, comm=None):
    (T, D), F = u.shape, w_gate_t.shape[0]

    def body(u_ref, wg_ref, wu_ref, g_ref, up_ref, z_ref):
        g, up = _nt(u_ref[...], wg_ref[...]), _nt(u_ref[...], wu_ref[...])
        g_ref[...], up_ref[...] = g.astype(g_ref.dtype), up.astype(up_ref.dtype)
        z_ref[...] = (g * _sigmoid(g) * up).astype(z_ref.dtype)

    w_spec = pl.BlockSpec((tn, D), lambda j, i: (j, 0))
    o_spec = pl.BlockSpec((tm, tn), lambda j, i: (i, j))
    return _call("ffn_hidden", body, grid=(F // tn, T // tm),
                 in_specs=[pl.BlockSpec((tm, D), lambda j, i: (i, 0)), w_spec, w_spec], out_specs=[o_spec] * 3,
                 out_shape=[jax.ShapeDtypeStruct((T, F), SAVED_DTYPE)] * 2 + [jax.ShapeDtypeStruct((T, F), MXU_DTYPE)],
                 args=[u, w_gate_t, w_up_t], semantics=("parallel", "parallel"), comm=comm)


def _colsum_partials(p):
    return jnp.sum(p.reshape(-1, 8, p.shape[-1])[:, 0, :], axis=0, keepdims=True)


def _rmsnorm_fwd(name, x, g, tr=512):
    T, D = x.shape

    def body(x_ref, g_ref, u_ref):
        xv = x_ref[...]
        r = lax.rsqrt(jnp.mean(xv * xv, axis=-1, keepdims=True) + EPS)
        u_ref[...] = (xv * r * g_ref[...]).astype(u_ref.dtype)

    return pl.pallas_call(
        body, name=name, grid=(T // tr,),
        in_specs=[pl.BlockSpec((tr, D), lambda i: (i, 0)), pl.BlockSpec((1, D), lambda i: (0, 0))],
        out_specs=pl.BlockSpec((tr, D), lambda i: (i, 0)),
        out_shape=jax.ShapeDtypeStruct((T, D), MXU_DTYPE),
        compiler_params=_params(("parallel",)),
    )(x, g)


def _rmsnorm_bwd_vals(dy, xin, g):
    rstd = lax.rsqrt(jnp.mean(xin * xin, axis=-1, keepdims=True) + EPS)
    xhat = xin * rstd
    dg = jnp.sum(dy * xhat, axis=0, keepdims=True)
    dxh = dy * g
    dx = rstd * (dxh - xhat * jnp.mean(dxh * xhat, axis=-1, keepdims=True))
    return dx, dg


def _colsum(name, a, tr=512, comm=_NO_COPIES):
    T, N = a.shape

    def body(a_ref, o_ref):
        @pl.when(pl.program_id(0) == 0)
        def _():
            o_ref[...] = jnp.zeros_like(o_ref)

        o_ref[...] += jnp.sum(a_ref[...].astype(F32), axis=0, keepdims=True)

    (res,), got = _call(name, body, grid=(T // tr,), in_specs=[pl.BlockSpec((tr, N), lambda i: (i, 0))],
                        out_specs=[pl.BlockSpec((1, N), lambda i: (0, 0))],
                        out_shape=[jax.ShapeDtypeStruct((1, N), F32)], args=[a], semantics=("arbitrary",),
                        comm=None if comm is _NO_COPIES else comm)
    return res if comm is _NO_COPIES else (res, got)


ATTN_SCALE = 1.0 / math.sqrt(HEAD_DIM)
GROUP_LANES = GROUP * ATTN_BLOCK
PAIR = 2 * HEAD_DIM


def _attn_mask():
    kj = lax.broadcasted_iota(jnp.int32, (ATTN_BLOCK, GROUP_LANES), 0)
    qi = lax.broadcasted_iota(jnp.int32, (ATTN_BLOCK, GROUP_LANES), 1) & (ATTN_BLOCK - 1)
    return kj <= qi


def _heads_transposed(ref, g, scale=None):
    parts = []
    for a in range(GROUP // 2):
        lo = (g * GROUP // 2 + a) * PAIR
        pair = ref[:, lo:lo + PAIR].astype(F32)
        pair = (pair if scale is None else pair * scale).T
        parts += [pair[:HEAD_DIM], pair[HEAD_DIM:]]
    return jnp.concatenate(parts, axis=1).astype(MXU_DTYPE)


def _heads_back(ref, g, vt):
    for a in range(GROUP // 2):
        lo = (g * GROUP // 2 + a) * PAIR
        pair = jnp.concatenate([vt[:, (2 * a) * ATTN_BLOCK:(2 * a + 1) * ATTN_BLOCK],
                                vt[:, (2 * a + 1) * ATTN_BLOCK:(2 * a + 2) * ATTN_BLOCK]], axis=0)
        ref[:, lo:lo + PAIR] = pair.T.astype(ref.dtype)


def _kv_parts(kv_ref, g):
    ks = slice(g * HEAD_DIM, (g + 1) * HEAD_DIM)
    vs = slice(KV_WIDTH + g * HEAD_DIM, KV_WIDTH + (g + 1) * HEAD_DIM)
    return kv_ref[:, ks].astype(MXU_DTYPE), kv_ref[:, vs].astype(MXU_DTYPE)


def _sink_rows(sinks):
    return jnp.repeat(sinks.reshape(KV_HEADS, GROUP), ATTN_BLOCK, axis=1)


def _attn_fwd(pq, pkv, sinks, comm=None):
    T = pq.shape[0]
    nb = T // ATTN_BLOCK

    def body(q_ref, kvc_ref, kvp_ref, s_ref, y_ref, lse_ref):
        mask_c = _attn_mask()
        has_prev = pl.program_id(0) > 0
        for g in range(KV_HEADS):
            (kc, vc), (kp, vp) = _kv_parts(kvc_ref, g), _kv_parts(kvp_ref, g)
            qt = _heads_transposed(q_ref, g, ATTN_SCALE)
            s = jnp.where(mask_c, _nn(kc, qt), jnp.where(has_prev, _nn(kp, qt), NEG_INF))
            sink = s_ref[g:g + 1, :]
            m = jnp.maximum(jnp.max(s, axis=0, keepdims=True), sink)
            p = jnp.exp(s - m)
            den = jnp.sum(p, axis=0, keepdims=True) + jnp.exp(sink - m)
            pc = jnp.where(mask_c, p, 0.0)
            _heads_back(y_ref, g, (_tn(vc, pc) + _tn(vp, p - pc)) / den)
            lse = m + jnp.log(den)
            for i in range(GROUP):
                lse_ref[g * GROUP + i:g * GROUP + i + 1, :] = lse[:, i * ATTN_BLOCK:(i + 1) * ATTN_BLOCK]

    return _call(
        "attn_fwd", body, grid=(nb,),
        in_specs=[pl.BlockSpec((ATTN_BLOCK, D_MODEL), lambda n: (n, 0)),
                  pl.BlockSpec((ATTN_BLOCK, 2 * KV_WIDTH), lambda n: (n, 0)),
                  pl.BlockSpec((ATTN_BLOCK, 2 * KV_WIDTH), lambda n: (jnp.maximum(n - 1, 0), 0)),
                  pl.BlockSpec((KV_HEADS, GROUP_LANES), lambda n: (0, 0))],
        out_specs=[pl.BlockSpec((ATTN_BLOCK, D_MODEL), lambda n: (n, 0)),
                   pl.BlockSpec((Q_HEADS, ATTN_BLOCK), lambda n: (0, n))],
        out_shape=[jax.ShapeDtypeStruct((T, D_MODEL), MXU_DTYPE), jax.ShapeDtypeStruct((Q_HEADS, T), F32)],
        args=[pq, pkv, pkv, _sink_rows(sinks)], semantics=("parallel",), comm=comm)


def _attn_bwd(pq, pkv, sinks, lse, dy, comm=None):
    T = pq.shape[0]
    nb = T // ATTN_BLOCK
    cur = lambda n: (jnp.minimum(n, nb - 1), 0)

    def body(q_ref, kvc_ref, kvp_ref, s_ref, lse_ref, dy_ref, dq_ref, dkv_ref, ds_ref, carry, top, bot):
        n = pl.program_id(0)

        @pl.when(n == 0)
        def _():
            carry[...] = jnp.zeros_like(carry)
            ds_ref[...] = jnp.zeros_like(ds_ref)

        @pl.when(n < nb)
        def _():
            mask_c = _attn_mask()
            valid = jnp.logical_or(mask_c, n > 0)
            for g in range(KV_HEADS):
                ks = slice(g * HEAD_DIM, (g + 1) * HEAD_DIM)
                vs = slice(KV_WIDTH + g * HEAD_DIM, KV_WIDTH + (g + 1) * HEAD_DIM)
                (kc, vc), (kp, vp) = _kv_parts(kvc_ref, g), _kv_parts(kvp_ref, g)
                qt = _heads_transposed(q_ref, g, ATTN_SCALE)
                dot = _heads_transposed(dy_ref, g)
                lse = jnp.concatenate([lse_ref[g * GROUP + i:g * GROUP + i + 1, :] for i in range(GROUP)], axis=1)
                p = jnp.where(valid, jnp.exp(jnp.where(mask_c, _nn(kc, qt), _nn(kp, qt)) - lse), 0.0)
                dp = jnp.where(mask_c, _nn(vc, dot), _nn(vp, dot))
                delta = jnp.sum(p * dp, axis=0, keepdims=True)
                ds = p * (dp - delta)
                ds_c, p_c = jnp.where(mask_c, ds, 0.0), jnp.where(mask_c, p, 0.0)
                ds_p, p_p = ds - ds_c, p - p_c
                _heads_back(dq_ref, g, (_tn(kc, ds_c) + _tn(kp, ds_p)) * ATTN_SCALE)
                bot[:, ks], bot[:, vs] = _nt(ds_c, qt), _nt(p_c, dot)
                top[:, ks], top[:, vs] = _nt(ds_p, qt), _nt(p_p, dot)
                ds_ref[g:g + 1, :] -= jnp.exp(s_ref[g:g + 1, :] - lse) * delta
            dkv_ref[...] = (carry[...] + top[...]).astype(dkv_ref.dtype)
            carry[...] = bot[...]

        @pl.when(n == nb)
        def _():
            dkv_ref[...] = carry[...].astype(dkv_ref.dtype)

    return _call(
        "attn_bwd", body, grid=(nb + 1,),
        in_specs=[pl.BlockSpec((ATTN_BLOCK, D_MODEL), cur),
                  pl.BlockSpec((ATTN_BLOCK, 2 * KV_WIDTH), cur),
                  pl.BlockSpec((ATTN_BLOCK, 2 * KV_WIDTH), lambda n: (jnp.maximum(jnp.minimum(n, nb - 1) - 1, 0), 0)),
                  pl.BlockSpec((KV_HEADS, GROUP_LANES), lambda n: (0, 0)),
                  pl.BlockSpec((Q_HEADS, ATTN_BLOCK), lambda n: (0, jnp.minimum(n, nb - 1))),
                  pl.BlockSpec((ATTN_BLOCK, D_MODEL), cur)],
        out_specs=[pl.BlockSpec((ATTN_BLOCK, D_MODEL), cur),
                   pl.BlockSpec((ATTN_BLOCK, 2 * KV_WIDTH), lambda n: (jnp.maximum(n - 1, 0), 0)),
                   pl.BlockSpec((KV_HEADS, GROUP_LANES), lambda n: (0, 0))],
        out_shape=[jax.ShapeDtypeStruct((T, D_MODEL), MXU_DTYPE),
                   jax.ShapeDtypeStruct((T, 2 * KV_WIDTH), MXU_DTYPE),
                   jax.ShapeDtypeStruct((KV_HEADS, GROUP_LANES), F32)],
        scratch=[pltpu.VMEM((ATTN_BLOCK, 2 * KV_WIDTH), F32)] * 3,
        args=[pq, pkv, pkv, _sink_rows(sinks), lse, dy], semantics=("arbitrary",), comm=comm)


def _lower_bound(l):
    m = jnp.maximum(l[0:1], l[1:2])
    e0, e1 = jnp.exp(l[0:1] - m), jnp.exp(l[1:2] - m)
    return e0 / (e0 + e1)


def _tri(lower):
    r = lax.broadcasted_iota(jnp.int32, (CHUNK, CHUNK), 0)
    c = lax.broadcasted_iota(jnp.int32, (CHUNK, CHUNK), 1)
    return (r >= c) if lower else (c >= r)


def _chunk_sum(mask, v):
    ones = mask.astype(BF16)
    hi = v.astype(BF16)
    rest = v - hi.astype(F32)
    mid = rest.astype(BF16)
    lo = (rest - mid.astype(F32)).astype(BF16)
    part = lambda t: lax.dot_general(ones, t, (((1,), (0,)), ((), ())), preferred_element_type=F32)
    return part(hi) + part(mid) + part(lo)


def _hgrn_chunk_inputs(hq, hf, lb, causal):
    half_t = 0.5 * jnp.tanh(0.5 * hf)
    sg, sgn = 0.5 + half_t, 0.5 - half_t
    f = lb + (1.0 - lb) * sg
    kk = (1.0 - lb) * sgn
    sq = _sigmoid(hq)
    q = hq * sq
    b = _chunk_sum(causal, jnp.log(f))
    bm, bl = b[CHUNK // 2 - 1:CHUNK // 2, :], b[CHUNK - 1:CHUNK, :]
    e_qm, e_km = jnp.exp(b - bm), jnp.exp(bm - b)
    e_qs, e_kl = e_qm * jnp.exp(bm), e_km * jnp.exp(bl - bm)
    return dict(sg=sg, sgn=sgn, f=f, kk=kk, sq=sq, q=q, e_qm=e_qm, e_km=e_km, e_qs=e_qs, e_kl=e_kl,
                qm=q * e_qm, km=kk * e_km, qs=q * e_qs, kl=kk * e_kl, el=jnp.exp(bl))


def _hgrn_fwd(ph, lb_logits, norm_g, comm=None):
    T = ph.shape[0]
    nblk, cpb = T // HGRN_TOKENS, HGRN_TOKENS // CHUNK
    col = lambda c: pl.BlockSpec((HGRN_TOKENS, D_MODEL), functools.partial(lambda i, c: (i, c), c=c))

    def body(hq_ref, hf_ref, hi_ref, hg_ref, l_ref, ng_ref, y_ref, o_ref, st_ref, s_ref):
        @pl.when(pl.program_id(0) == 0)
        def _():
            s_ref[...] = jnp.zeros_like(s_ref)

        lb = _lower_bound(l_ref[...])
        causal = _tri(True)
        for c in range(cpb):
            rows = slice(c * CHUNK, (c + 1) * CHUNK)
            t = _hgrn_chunk_inputs(hq_ref[rows, :], hf_ref[rows, :], lb, causal)
            qm, km, qs, kl = (t[n].astype(MXU_DTYPE) for n in ("qm", "km", "qs", "kl"))
            v = hi_ref[rows, :].astype(MXU_DTYPE)
            for h in range(HGRN_HEADS):
                ls = slice(h * HGRN_K, (h + 1) * HGRN_K)
                st = s_ref[h]
                st_ref[c, ls, :] = st
                a = jnp.where(causal, _nt(qm[:, ls], km[:, ls]), 0.0)
                o_ref[rows, ls] = _nn(a, v[:, ls]) + _nt(qs[:, ls], st)
                s_ref[h] = t["el"][:, ls] * st + _tn(v[:, ls], kl[:, ls])
        for h in range(HGRN_HEADS):
            ls = slice(h * HGRN_K, (h + 1) * HGRN_K)
            o = o_ref[:, ls]
            r = lax.rsqrt(jnp.mean(o * o, axis=-1, keepdims=True) + EPS)
            y_ref[:, ls] = (o * r * ng_ref[:, ls] * _sigmoid(hg_ref[:, ls])).astype(y_ref.dtype)

    return _call(
        "hgrn_fwd", body, grid=(nblk,),
        in_specs=[col(0), col(1), col(2), col(3),
                  pl.BlockSpec((2, D_MODEL), lambda i: (0, 0)), pl.BlockSpec((1, D_MODEL), lambda i: (0, 0))],
        out_specs=[pl.BlockSpec((HGRN_TOKENS, D_MODEL), lambda i: (i, 0)),
                   pl.BlockSpec((HGRN_TOKENS, D_MODEL), lambda i: (i, 0)),
                   pl.BlockSpec((cpb, D_MODEL, HGRN_K), lambda i: (i, 0, 0))],
        out_shape=[jax.ShapeDtypeStruct((T, D_MODEL), MXU_DTYPE), jax.ShapeDtypeStruct((T, D_MODEL), F32),
                   jax.ShapeDtypeStruct((T // CHUNK, D_MODEL, HGRN_K), F32)],
        scratch=[pltpu.VMEM((HGRN_HEADS, HGRN_K, HGRN_K), F32)],
        args=[ph, ph, ph, ph, lb_logits, norm_g], semantics=("arbitrary",), comm=comm)


def _hgrn_bwd(ph, o_raw, states, dy, lb_logits, norm_g, comm=None):
    T = ph.shape[0]
    nblk, cpb = T // HGRN_TOKENS, HGRN_TOKENS // CHUNK
    rev = lambda i: nblk - 1 - i
    col = lambda c: pl.BlockSpec((HGRN_TOKENS, D_MODEL), functools.partial(lambda i, c: (rev(i), c), c=c))
    tok = pl.BlockSpec((HGRN_TOKENS, D_MODEL), lambda i: (rev(i), 0))

    def body(hq_ref, hf_ref, hi_ref, hg_ref, o_ref, st_ref, dy_ref, l_ref, ng_ref,
             dph_ref, dng_ref, dl_ref, dst_ref, dlb_ref, do_s, dqm_s, dkm_s, dqs_s, dkl_s, dv_s, del_s):
        i = pl.program_id(0)

        @pl.when(i == 0)
        def _():
            dst_ref[...] = jnp.zeros_like(dst_ref)
            dlb_ref[...] = jnp.zeros_like(dlb_ref)
            dng_ref[...] = jnp.zeros_like(dng_ref)

        lb = _lower_bound(l_ref[...])
        causal, anti = _tri(True), _tri(False)
        row = lax.broadcasted_iota(jnp.int32, (CHUNK, D_MODEL), 0)
        for c in reversed(range(cpb)):
            rows = slice(c * CHUNK, (c + 1) * CHUNK)
            hq = hq_ref[rows, :]
            t = _hgrn_chunk_inputs(hq, hf_ref[rows, :], lb, causal)
            sgg = _sigmoid(hg_ref[rows, :])
            dyv = dy_ref[rows, :]
            for h in range(HGRN_HEADS):
                ls = slice(h * HGRN_K, (h + 1) * HGRN_K)
                o = o_ref[rows, ls]
                r = lax.rsqrt(jnp.mean(o * o, axis=-1, keepdims=True) + EPS)
                nrm = o * r
                g_h = sgg[:, ls]
                dph_ref[rows, 3 * D_MODEL + h * HGRN_K:3 * D_MODEL + (h + 1) * HGRN_K] = (
                    dyv[:, ls] * nrm * ng_ref[:, ls] * g_h * (1.0 - g_h)).astype(dph_ref.dtype)
                dyg = dyv[:, ls] * g_h
                dng_ref[:, ls] += jnp.sum(dyg * nrm, axis=0, keepdims=True)
                dn = dyg * ng_ref[:, ls]
                do_s[:, ls] = r * (dn - nrm * jnp.mean(dn * nrm, axis=-1, keepdims=True))
            qm, km, qs, kl = (t[n].astype(MXU_DTYPE) for n in ("qm", "km", "qs", "kl"))
            v = hi_ref[rows, :].astype(MXU_DTYPE)
            do = do_s[...].astype(MXU_DTYPE)
            for h in range(HGRN_HEADS):
                ls = slice(h * HGRN_K, (h + 1) * HGRN_K)
                st = st_ref[c, ls, :]
                dst = dst_ref[h]
                a = jnp.where(causal, _nt(qm[:, ls], km[:, ls]), 0.0)
                da = jnp.where(causal, _nt(do[:, ls], v[:, ls]), 0.0)
                dv_s[:, ls] = _tn(a, do[:, ls]) + _nt(kl[:, ls], dst)
                dkl_s[:, ls] = _nn(v[:, ls], dst)
                dqs_s[:, ls] = _nn(do[:, ls], st)
                del_s[:, ls] = jnp.sum(dst * st, axis=0, keepdims=True)
                dst_ref[h] = _tn(do[:, ls], qs[:, ls]) + t["el"][:, ls] * dst
                dqm_s[:, ls] = _nn(da, km[:, ls])
                dkm_s[:, ls] = _tn(da, qm[:, ls])
            dqm, dkm, dqs, dkl = dqm_s[...], dkm_s[...], dqs_s[...], dkl_s[...]
            dq = dqm * t["e_qm"] + dqs * t["e_qs"]
            dk = dkm * t["e_km"] + dkl * t["e_kl"]
            t_qm, t_km, t_kl = dqm * t["qm"], dkm * t["km"], dkl * t["kl"]
            db = t_qm - t_km + dqs * t["qs"] - t_kl
            db_mid = jnp.sum(t_km - t_qm, axis=0, keepdims=True)
            db_last = jnp.sum(t_kl, axis=0, keepdims=True) + del_s[...] * t["el"]
            db = db + jnp.where(row == CHUNK // 2 - 1, db_mid, 0.0) + jnp.where(row == CHUNK - 1, db_last, 0.0)
            dlogf = _chunk_sum(anti, db)
            sq, sg, sgn, f = t["sq"], t["sg"], t["sgn"], t["f"]
            dph_ref[rows, 0:D_MODEL] = (dq * (sq * (1.0 + hq * (1.0 - sq)))).astype(dph_ref.dtype)
            dph_ref[rows, D_MODEL:2 * D_MODEL] = (
                dlogf * (1.0 - lb) * sg * (1.0 - sg) / f - dk * (1.0 - lb) * sgn * (1.0 - sgn)).astype(dph_ref.dtype)
            dph_ref[rows, 2 * D_MODEL:3 * D_MODEL] = dv_s[...].astype(dph_ref.dtype)
            dlb_ref[...] += jnp.sum(dlogf * (1.0 - sg) / f - dk * sgn, axis=0, keepdims=True)

        @pl.when(i == nblk - 1)
        def _():
            dl0 = dlb_ref[...] * lb * (1.0 - lb)
            dl_ref[0:1, :] = dl0
            dl_ref[1:2, :] = -dl0

    wide = pltpu.VMEM((CHUNK, D_MODEL), F32)
    return _call(
        "hgrn_bwd", body, grid=(nblk,),
        in_specs=[col(0), col(1), col(2), col(3), tok,
                  pl.BlockSpec((cpb, D_MODEL, HGRN_K), lambda i: (rev(i), 0, 0)), tok,
                  pl.BlockSpec((2, D_MODEL), lambda i: (0, 0)), pl.BlockSpec((1, D_MODEL), lambda i: (0, 0))],
        out_specs=[pl.BlockSpec((HGRN_TOKENS, 4 * D_MODEL), lambda i: (rev(i), 0)),
                   pl.BlockSpec((1, D_MODEL), lambda i: (0, 0)), pl.BlockSpec((2, D_MODEL), lambda i: (0, 0))],
        out_shape=[jax.ShapeDtypeStruct((T, 4 * D_MODEL), MXU_DTYPE), jax.ShapeDtypeStruct((1, D_MODEL), F32),
                   jax.ShapeDtypeStruct((2, D_MODEL), F32)],
        scratch=[pltpu.VMEM((HGRN_HEADS, HGRN_K, HGRN_K), F32), pltpu.VMEM((1, D_MODEL), F32),
                 wide, wide, wide, wide, wide, wide, pltpu.VMEM((1, D_MODEL), F32)],
        args=[ph, ph, ph, ph, o_raw, states, dy, lb_logits, norm_g], semantics=("arbitrary",), comm=comm)


def _local_step(x, target, vec, net):
    T, D = x.shape
    norm_mix_g, b_in, sinks, lb_logits = vec["norm_mix_g"], vec["b_in"], vec["attn_sinks"], vec["hgrn_lb_logits"]
    hgrn_norm_g, norm_ffn_g, norm_final_g = vec["hgrn_norm_g"], vec["norm_ffn_g"], vec["norm_final_g"]
    w_in = net.full("w_in")
    o_q, o_kv, o_h, o_g = (sum(IN_SPLITS[:i]) for i in range(4))
    w_q, w_kv, w_h, w_g = (_Rows(w_in, o, n) for o, n in zip((o_q, o_kv, o_h, o_g), IN_SPLITS))
    b_q, b_kv, b_h, b_g = (b_in[:, o:o + n] for o, n in zip((o_q, o_kv, o_h, o_g), IN_SPLITS))
    bias = lambda acc, b: (acc + b,)
    both = lambda acc: (acc, acc)
    grad_outs = [("mn", F32), ("mn", MXU_DTYPE)]
    row_vec = lambda n: ((1, n), lambda i, j: (0, j))
    tile = lambda tm, tn: ((tm, tn), lambda i, j: (i, j))
    TM = 512
    BIG = min(T, 1024)

    u = _rmsnorm_fwd("norm_mix", x, norm_mix_g)
    pq, = _matmul("in_q", [(u, w_q)], "nt", tm=TM, tn=1024, tk=1024, outs=[("mn", MXU_DTYPE)],
                  extras=[(b_q, *row_vec(1024))], epilogue=bias)
    pkv, = _matmul("in_kv", [(u, w_kv)], "nt", tm=TM, tn=256, tk=1024, outs=[("mn", MXU_DTYPE)],
                   extras=[(b_kv, *row_vec(256))], epilogue=bias)
    names = ("w_branch_attn", "w_branch_hgrn")
    (ph,), got = _matmul("in_h", [(u, w_h)], "nt", tm=TM, tn=1024, tk=1024, outs=[("mn", F32)],
                         extras=[(b_h, *row_vec(1024))], epilogue=bias, comm=net.gather(names))
    net.gathered(names, got)
    names = ("w_out",)
    (pg,), got = _matmul("in_g", [(u, w_g)], "nt", tm=TM, tn=1024, tk=1024, outs=[("mn", F32)],
                         extras=[(b_g, *row_vec(1024))], epilogue=bias, comm=net.gather(names))
    net.gathered(names, got)
    names = ("w_ffn_gate",)
    (y_attn, lse), got = _attn_fwd(pq, pkv, sinks, comm=net.gather(names))
    net.gathered(names, got)
    names = ("w_ffn_up",)
    (y_hgrn, o_raw, states), got = _hgrn_fwd(ph, lb_logits, hgrn_norm_g, comm=net.gather(names))
    net.gathered(names, got)
    w_ba, w_bh, w_out = net.full("w_branch_attn"), net.full("w_branch_hgrn"), net.full("w_out")
    w_gate, w_up = net.full("w_ffn_gate"), net.full("w_ffn_up")
    ya, = _matmul("branch_a", [(y_attn, w_ba)], "nn", tm=TM, tn=1024, tk=1024, outs=[("mn", F32)])
    gate_a = (pg, (TM, 1024), lambda i, j: (i, 0))
    gate_b = (pg, (TM, 1024), lambda i, j: (i, 1))

    def merge(acc, ya_v, ga, gb):
        return acc, _sigmoid(ga) * ya_v + _sigmoid(gb) * acc

    yb, merged = _matmul("branch_b", [(y_hgrn, w_bh)], "nn", tm=TM, tn=1024, tk=1024,
                         outs=[("mn", F32), ("mn", MXU_DTYPE)],
                         extras=[(ya, *tile(TM, 1024)), gate_a, gate_b], epilogue=merge)

    def resid_norm(acc, xin, g):
        h = xin + acc
        r = lax.rsqrt(jnp.mean(h * h, axis=-1, keepdims=True) + EPS)
        return h, h * r * g

    h1, u2 = _matmul("out_proj", [(merged, w_out)], "nn", tm=TM, tn=1024, tk=1024,
                     outs=[("mn", F32), ("mn", MXU_DTYPE)],
                     extras=[(x, *tile(TM, 1024)), (norm_ffn_g, *row_vec(1024))], epilogue=resid_norm)
    FT = FFN // 2
    names = ("w_ffn_down",)
    (gpre, up, z), got = _swiglu_fwd(u2, w_gate, w_up, tm=TM, tn=FT, comm=net.gather(names))
    net.gathered(names, got)
    w_down = net.full("w_ffn_down")

    def loss_head(acc, h1_v, tgt, g):
        h2 = h1_v + acc
        r = lax.rsqrt(jnp.mean(h2 * h2, axis=-1, keepdims=True) + EPS)
        xhat = h2 * r
        err = xhat * g - tgt
        part = 0.5 * jnp.sum(jnp.sum(err * err, axis=-1, keepdims=True), axis=0, keepdims=True) / D
        dyv = err / D
        dxh = dyv * g
        dh2 = r * (dxh - xhat * jnp.mean(dxh * xhat, axis=-1, keepdims=True))
        return dh2, dh2, jnp.sum(dyv * xhat, axis=0, keepdims=True), jnp.broadcast_to(part, (1, D))

    dh2, dh2b, dgf_p, loss_p = _matmul(
        "ffn_down", [(z, w_down)], "nn", tm=TM, tn=1024, tk=FFN,
        outs=[("mn", F32), ("mn", MXU_DTYPE), ("pn", F32), ("pn", F32)],
        extras=[(h1, *tile(TM, 1024)), (target, *tile(TM, 1024)), (norm_final_g, *row_vec(1024))],
        epilogue=loss_head)
    loss = jnp.sum(loss_p.reshape(-1, 8, D)[:, 0, 0])
    d_norm_final = _colsum_partials(dgf_p)

    def swiglu_bwd(acc, gv, upv):
        gv, upv = gv.astype(F32), upv.astype(F32)
        s = _sigmoid(gv)
        return acc * upv * (s * (1.0 + gv * (1.0 - s))), acc * (gv * s)

    dgp, dup = _matmul("d_ffn_hidden", [(dh2b, w_down)], "nt", tm=TM, tn=FT, tk=1024,
                       outs=[("mn", MXU_DTYPE), ("mn", MXU_DTYPE)],
                       extras=[(gpre, *tile(TM, FT)), (up, *tile(TM, FT))], epilogue=swiglu_bwd)
    d_w_down = _matmul("dw_down", [(z, dh2b)], "tn", tm=FT, tn=1024, tk=1024, outs=grad_outs, epilogue=both)

    def norm_ffn_bwd(acc, h1_v, g, dres):
        dx, dg = _rmsnorm_bwd_vals(acc, h1_v, g)
        dh = dres + dx
        return dh, dh, dg

    names = ("w_ffn_down",)
    (dh1, dh1b, dg2_p), got = _matmul(
        "d_ffn_in", [(dgp, w_gate), (dup, w_up)], "nn", tm=BIG, tn=1024, tk=FT,
        outs=[("mn", F32), ("mn", MXU_DTYPE), ("pn", F32)],
        extras=[(h1, *tile(BIG, 1024)), (norm_ffn_g, *row_vec(1024)), (dh2, *tile(BIG, 1024))],
        epilogue=norm_ffn_bwd,
        comm=net.exchange(dict(w_ffn_down=[d_w_down])))
    net.received(names, (), got)
    d_norm_ffn = _colsum_partials(dg2_p)
    d_w_gate = _matmul("dw_gate", [(dgp, u2)], "tn", tm=FT, tn=1024, tk=1024, outs=grad_outs, epilogue=both)
    d_w_up = _matmul("dw_up", [(dup, u2)], "tn", tm=FT, tn=1024, tk=1024, outs=grad_outs, epilogue=both)

    def merge_bwd(acc, ya_v, yb_v, ga, gb):
        sa, sb = _sigmoid(ga), _sigmoid(gb)
        return acc * sa, acc * sb, acc * ya_v * sa * (1.0 - sa), acc * yb_v * sb * (1.0 - sb)

    dya, dyb, dga, dgb = _matmul(
        "d_merged", [(dh1b, w_out)], "nt", tm=TM, tn=1024, tk=1024, outs=[("mn", MXU_DTYPE)] * 4,
        extras=[(ya, *tile(TM, 1024)), (yb, *tile(TM, 1024)), gate_a, gate_b], epilogue=merge_bwd)
    tn_grad = functools.partial(_matmul, mode="tn", tn=1024, tk=1024, outs=grad_outs, epilogue=both)
    d_w_out = tn_grad("dw_out", [(merged, dh1b)], tm=1024)
    dy_attn, = _matmul("d_y_attn", [(dya, w_ba)], "nt", tm=TM, tn=1024, tk=1024, outs=[("mn", MXU_DTYPE)])
    dy_hgrn, = _matmul("d_y_hgrn", [(dyb, w_bh)], "nt", tm=TM, tn=1024, tk=1024, outs=[("mn", F32)])
    d_w_ba = tn_grad("dw_branch_a", [(y_attn, dya)], tm=1024)
    d_w_bh = tn_grad("dw_branch_b", [(y_hgrn, dyb)], tm=1024)

    names, swap = ("w_ffn_gate",), ("w_ffn_down",)
    (dq, dkv, dsink), got = _attn_bwd(pq, pkv, sinks, lse, dy_attn,
                                      comm=net.exchange(dict(w_ffn_gate=[d_w_gate]), swap))
    net.received(names, swap, got)
    names, swap = ("w_ffn_up", "w_out", "w_branch_attn", "w_branch_hgrn"), ("w_ffn_gate",)
    (dph, d_hgrn_norm, d_lb_logits), got = _hgrn_bwd(
        ph, o_raw, states, dy_hgrn, lb_logits, hgrn_norm_g,
        comm=net.exchange(dict(w_ffn_up=[d_w_up], w_out=[d_w_out], w_branch_attn=[d_w_ba],
                               w_branch_hgrn=[d_w_bh]), swap))
    net.received(names, swap, got)

    d_w_in, db, rows_in = None, {}, sum(IN_SPLITS)
    for piece, dp, first, tm_p in (("q", dq, o_q, 1024), ("kv", dkv, o_kv, 256), ("h", dph, o_h, 1024),
                                   ("ga", dga, o_g, 1024), ("gb", dgb, o_g + D, 1024)):
        *d_w_in, db[piece] = tn_grad("dw_in_" + piece, [(dp, u)], tm=tm_p, a_colsum=True,
                                     into=_Into(rows_in, first, d_w_in))
    d_w_in = [tuple(d_w_in)]

    def norm_mix_bwd(acc, xin, g, dres):
        dx, dg = _rmsnorm_bwd_vals(acc, xin, g)
        return dres + dx, dg

    first_level = net.presum_begin("w_in", d_w_in)
    halves = net.presum_end("w_in", [] if first_level is None else _copies_alone("presum_swap_w_in", first_level))
    names, swap = ("w_in",), ("w_ffn_up", "w_out", "w_branch_attn", "w_branch_hgrn")
    (dx, dg1_p), got = _matmul(
        "d_u", [(dq, w_q), (dkv, w_kv), (dph, w_h), (dga, _Rows(w_in, o_g, D)), (dgb, _Rows(w_in, o_g + D, D))], "nn",
        tm=BIG, tn=1024, tk=512, outs=[("mn", F32), ("pn", F32)],
        extras=[(x, *tile(BIG, 1024)), (norm_mix_g, *row_vec(1024)), (dh1, *tile(BIG, 1024))],
        epilogue=norm_mix_bwd,
        comm=_join(halves, net.swap(swap)))
    net.last = (names, swap, got)
    d_norm_mix = _colsum_partials(dg1_p)
    d_b_in = jnp.concatenate([db[piece] for piece in ("q", "kv", "h", "ga", "gb")], axis=1)
    vecs = dict(norm_mix_g=d_norm_mix, b_in=d_b_in, attn_sinks=jnp.sum(dsink.reshape(Q_HEADS, ATTN_BLOCK), axis=1).reshape(1, Q_HEADS),
                hgrn_lb_logits=d_lb_logits,
                hgrn_norm_g=d_hgrn_norm, norm_ffn_g=d_norm_ffn, norm_final_g=d_norm_final)
    return loss, dx, vecs


def _place():
    return lax.axis_index("x"), lax.axis_index("y"), lax.axis_index("c")


def _other_chips(x, y):
    return [(1 - x, y), (x, 1 - y), (1 - x, 1 - y)]


def _y_first(copies):
    return [copies[3 * (i // 3) + (1, 0, 2)[i % 3]] for i in range(len(copies))]


def _gather_copies(shards):
    n = len(shards)

    def build(ins, outs, send_sems, recv_sems, local_sems):
        x, y, c = _place()
        mine = 2 * x + y
        local = [pltpu.make_async_copy(ins[w], outs[w].at[mine], local_sems.at[w]) for w in range(n)]
        sends, recvs = [], []
        for w in range(n):
            for k, (px, py) in enumerate(_other_chips(x, y)):
                sem = 3 * w + k
                sends.append(pltpu.make_async_remote_copy(
                    src_ref=ins[w], dst_ref=outs[w].at[mine], send_sem=send_sems.at[sem], recv_sem=recv_sems.at[sem],
                    device_id=(px, py, c), device_id_type=MESH_ID))
                recvs.append(pltpu.make_async_remote_copy(
                    src_ref=ins[w], dst_ref=outs[w].at[2 * px + py], send_sem=send_sems.at[sem],
                    recv_sem=recv_sems.at[sem], device_id=(px, py, c), device_id_type=MESH_ID))
        return sends, recvs, local, _y_first(sends)

    return _Carried(shards, [jax.ShapeDtypeStruct((N_CHIPS,) + s.shape, s.dtype) for s in shards], 3 * n, n, build)


def _grad_copies(stacked):
    n = len(stacked)

    def build(ins, outs, send_sems, recv_sems, local_sems):
        x, y, c = _place()
        sends = []
        for w in range(n):
            for k, (px, py) in enumerate(_other_chips(x, y)):
                sem = 3 * w + k
                sends.append(pltpu.make_async_remote_copy(
                    src_ref=ins[w].at[2 * px + py], dst_ref=outs[w].at[k], send_sem=send_sems.at[sem],
                    recv_sem=recv_sems.at[sem], device_id=(px, py, c), device_id_type=MESH_ID))
        return sends, sends, [], _y_first(sends)

    return _Carried(stacked, [jax.ShapeDtypeStruct((3,) + s.shape[1:], s.dtype) for s in stacked], 3 * n, 0, build)


def _small_copies(small):
    def build(ins, outs, send_sems, recv_sems, local_sems):
        small_ref, all_ref = ins[0], outs[0]
        x, y, c = _place()
        me = 4 * x + 2 * y + c
        sends, recvs = [], []
        for r in range(1, 8):
            px = 1 - x if r & 4 else x
            py = 1 - y if r & 2 else y
            pc = 1 - c if r & 1 else c
            sends.append(pltpu.make_async_remote_copy(
                src_ref=small_ref, dst_ref=all_ref.at[me], send_sem=send_sems.at[r - 1], recv_sem=recv_sems.at[r - 1],
                device_id=(px, py, pc), device_id_type=MESH_ID))
            recvs.append(pltpu.make_async_remote_copy(
                src_ref=small_ref, dst_ref=all_ref.at[4 * px + 2 * py + pc], send_sem=send_sems.at[r - 1],
                recv_sem=recv_sems.at[r - 1], device_id=(px, py, pc), device_id_type=MESH_ID))
        return sends, recvs, [pltpu.make_async_copy(small_ref, all_ref.at[me], local_sems.at[0])]

    return _Carried([small], [jax.ShapeDtypeStruct((8,) + small.shape, small.dtype)], 7, 1, build)


def _gather_by_neighbours(name, shard):
    half = shard.shape[0] // 2
    quarter = half // 2

    def body(in_ref, out_ref, send_sems, recv_sems, local_sem):
        for core in (0, 1):
            @pl.when(lax.axis_index("c") == core)
            def _():
                program(core, in_ref, out_ref, send_sems, recv_sems, local_sem)

    def program(c, in_ref, out_ref, send_sems, recv_sems, local_sem):
        x, y, _ = _place()
        chip = lambda px, py: 2 * px + py
        to_x, to_y, sibling = (1 - x, y, c), (x, 1 - y, c), (x, y, 1 - c)
        x_blk, y_blk, d_blk = chip(1 - x, y), chip(x, 1 - y), chip(1 - x, 1 - y)
        mine, theirs = c * half, (1 - c) * half

        def copy(sem, rows, block, to, src=None):
            place = out_ref.at[block, pl.ds(rows[0], rows[1])]
            return pltpu.make_async_remote_copy(
                src_ref=place if src is None else src, dst_ref=place, send_sem=send_sems.at[sem],
                recv_sem=recv_sems.at[sem], device_id=to, device_id_type=MESH_ID)

        own = pltpu.make_async_copy(in_ref, out_ref.at[chip(x, y)], local_sem)
        own.start()
        my_rows = in_ref.at[pl.ds(mine, half)]
        along_x = dict(send=copy(0, (mine, half), chip(x, y), to_x, src=my_rows),
                       landed=copy(0, (mine, half), x_blk, to_x),
                       onward=[copy(3, (mine + quarter, quarter), x_blk, to_y), copy(4, (mine, half), x_blk, sibling)],
                       diagonal=copy(2, (mine, quarter), d_blk, to_x))
        along_y = dict(send=copy(1, (mine, half), chip(x, y), to_y, src=my_rows),
                       landed=copy(1, (mine, half), y_blk, to_y),
                       onward=[copy(2, (mine, quarter), y_blk, to_x), copy(5, (mine, half), y_blk, sibling)],
                       diagonal=copy(3, (mine + quarter, quarter), d_blk, to_y))
        last = copy(6, (mine, half), d_blk, sibling)

        order = (along_x, along_y) if c == 0 else (along_y, along_x)
        for axis in order:
            axis["send"].start()
        for axis in order:
            axis["landed"].wait_recv()
            for cp in axis["onward"]:
                cp.start()
        for axis in order:
            axis["diagonal"].wait_recv()
        last.start()
        for sem, block in ((4, x_blk), (5, y_blk), (6, d_blk)):
            copy(sem, (theirs, half), block, sibling).wait_recv()
        for cp in [along_x["send"], along_y["send"]] + along_x["onward"] + along_y["onward"] + [last]:
            cp.wait_send()
        own.wait()

    return pl.pallas_call(
        body, name=name, in_specs=[HBM_SPEC], out_specs=HBM_SPEC,
        out_shape=jax.ShapeDtypeStruct((N_CHIPS,) + shard.shape, shard.dtype),
        scratch_shapes=[pltpu.SemaphoreType.DMA((7,)), pltpu.SemaphoreType.DMA((7,)), pltpu.SemaphoreType.DMA(())],
    )(shard)


def _copies_alone(name, comm):
    return _call(name, lambda: None, grid=(), in_specs=[], out_specs=[], out_shape=[], args=[], comm=comm)[1]


class _Net:
    def __init__(self, shards):
        self.shards = shards
        self.whole, self.own, self.theirs, self.sums, self.other = {}, {}, {}, {}, {}
        x, y, _ = _place()
        self.chip = 2 * x + y

    def gather(self, names):
        return _gather_copies([self.shards[n] for n in names])

    def gathered(self, names, got):
        for n, g in zip(names, got):
            self.whole[n] = g.reshape(-1, g.shape[-1])

    def full(self, name):
        return self.whole[name]

    def exchange(self, grads, swap=()):
        stacked = []
        for n, pieces in grads.items():
            keep = jnp.concatenate([p[0] for p in pieces], axis=0) if len(pieces) > 1 else pieces[0][0]
            send = jnp.concatenate([p[1] for p in pieces], axis=0) if len(pieces) > 1 else pieces[0][1]
            rows = keep.shape[0] // N_CHIPS
            self.own[n] = lax.dynamic_slice_in_dim(keep, self.chip * rows, rows, axis=0)
            stacked.append(send.reshape(N_CHIPS, rows, send.shape[-1]))
        return _join(_grad_copies(stacked), self.swap(swap))

    def swap(self, names):
        return _sibling_copies([self.sums[n] for n in names]) if names else None

    def presum_begin(self, name, pieces):
        keep = jnp.concatenate([p[0] for p in pieces], axis=0) if len(pieces) > 1 else pieces[0][0]
        send = jnp.concatenate([p[1] for p in pieces], axis=0) if len(pieces) > 1 else pieces[0][1]
        rows = keep.shape[0] // N_CHIPS
        self.held = keep.reshape(N_CHIPS, rows, keep.shape[-1])
        return _half_rows_copies(send.reshape(N_CHIPS, rows, send.shape[-1]))

    def presum_end(self, name, got):
        x, y, c = _place()
        to_send, self.own[name] = _pre_sum("presum_" + name, self.held, got[0], jnp.stack([c, self.chip]))
        return _grad_copies([to_send])

    def received(self, names, swap, got, carried=None):
        self.theirs.update(zip(names, got[:len(names)]))
        self.other.update(zip(swap, got[len(names):]))
        for n in names:
            (self.sums[n],), more = _partial_sum("sum_" + n, self.own[n], self.theirs[n], comm=carried)
        return more


def _half_rows_copies(stacked):
    n, rows = stacked.shape[0], stacked.shape[1] // 2

    def build(ins, outs, send_sems, recv_sems, local_sems):
        x, y, c = _place()
        copies = [pltpu.make_async_remote_copy(
            src_ref=ins[0].at[s, pl.ds((1 - c) * rows, rows)], dst_ref=outs[0].at[s], send_sem=send_sems.at[s],
            recv_sem=recv_sems.at[s], device_id=(x, y, 1 - c), device_id_type=MESH_ID) for s in range(n)]
        return copies, copies, []

    return _Carried([stacked], [jax.ShapeDtypeStruct((n, rows, stacked.shape[2]), stacked.dtype)], n, 0, build)


def _pre_sum(name, held, theirs, core_and_chip):
    n, R, C = held.shape
    half = R // 2
    tr = _row_tile(half)
    per_half = half // tr

    def body(place_ref, h_ref, t_ref, send_ref, own_ref):
        total = h_ref[0] + t_ref[0].astype(F32)
        send_ref[0] = total.astype(send_ref.dtype)

        @pl.when(pl.program_id(1) == place_ref[1])
        def _():
            own_ref[...] = total

    return pl.pallas_call(
        body, name=name,
        grid_spec=pltpu.PrefetchScalarGridSpec(
            num_scalar_prefetch=1, grid=(per_half, n),
            in_specs=[pl.BlockSpec((1, tr, C), lambda i, s, place: (s, place[0] * per_half + i, 0)),
                      pl.BlockSpec((1, tr, C), lambda i, s, place: (s, i, 0))],
            out_specs=[pl.BlockSpec((1, tr, C), lambda i, s, place: (s, i, 0)),
                       pl.BlockSpec((tr, C), lambda i, s, place: (i, 0))]),
        out_shape=[jax.ShapeDtypeStruct((n, half, C), MXU_DTYPE), jax.ShapeDtypeStruct((half, C), F32)],
        compiler_params=_params(("arbitrary", "arbitrary")),
    )(core_and_chip, held, theirs)


def _sibling_copies(parts):
    n = len(parts)

    def build(ins, outs, send_sems, recv_sems, local_sems):
        x, y, c = _place()
        copies = [pltpu.make_async_remote_copy(
            src_ref=ins[w], dst_ref=outs[w], send_sem=send_sems.at[w], recv_sem=recv_sems.at[w],
            device_id=(x, y, 1 - c), device_id_type=MESH_ID) for w in range(n)]
        return copies, copies, []

    return _Carried(parts, [jax.ShapeDtypeStruct(p.shape, p.dtype) for p in parts], n, 0, build)


def _row_tile(rows, most=512, sublanes=16):
    return max(t for t in range(sublanes, most + 1, sublanes) if rows % t == 0)


def _partial_sum(name, own, recv, comm=None):
    R, C = own.shape
    tr = _row_tile(R, most=128)

    def body(o_ref, r_ref, p_ref):
        p_ref[...] = ((o_ref[...] + r_ref[0].astype(F32)) + r_ref[1].astype(F32)) + r_ref[2].astype(F32)

    return _call(name, body, grid=(R // tr,),
                 in_specs=[pl.BlockSpec((tr, C), lambda i: (i, 0)), pl.BlockSpec((3, tr, C), lambda i: (0, i, 0))],
                 out_specs=[pl.BlockSpec((tr, C), lambda i: (i, 0))], out_shape=[jax.ShapeDtypeStruct((R, C), F32)],
                 args=[own, recv], semantics=("parallel",), comm=comm)


def _adam_vals(w, g, m, v):
    m = ADAM_B1 * m + (1.0 - ADAM_B1) * g
    v = ADAM_B2 * v + (1.0 - ADAM_B2) * (g * g)
    m_hat = m / (1.0 - ADAM_B1 ** ADAM_STEP)
    v_hat = v / (1.0 - ADAM_B2 ** ADAM_STEP)
    delta = -ADAM_LR * (m_hat / (jnp.sqrt(v_hat) + ADAM_EPS) + ADAM_WD * w)
    return delta, m, v


def _adamw(name, w, m, v, mine, other, comm=None):
    R, C = w.shape
    tr = _row_tile(R)

    def body(w_ref, m_ref, v_ref, s_ref, n_ref, g_ref, d_ref, nm_ref, nv_ref):
        g = s_ref[...] + n_ref[...]
        d, nm, nv = _adam_vals(w_ref[...], g, m_ref[...], v_ref[...])
        g_ref[...], d_ref[...], nm_ref[...], nv_ref[...] = g, d, nm, nv

    spec = pl.BlockSpec((tr, C), lambda i: (i, 0))
    return _call(name, body, grid=(R // tr,), in_specs=[spec] * 5, out_specs=[spec] * 4,
                 out_shape=[jax.ShapeDtypeStruct((R, C), F32)] * 4, args=[w, m, v, mine, other],
                 semantics=("parallel",), comm=comm)


def _adamw_by_halves(name, w, m, v, mine, other, core):
    R, C = w.shape
    tr = _row_tile(R // 2)
    per_half = R // 2 // tr

    def body(c_ref, w_ref, m_ref, v_ref, s_ref, n_ref, g_ref, d_ref, nm_ref, nv_ref):
        g = jnp.where(pl.program_id(0) // per_half == c_ref[0, 0], s_ref[...], n_ref[...])
        d, nm, nv = _adam_vals(w_ref[...], g, m_ref[...], v_ref[...])
        g_ref[...], d_ref[...], nm_ref[...], nv_ref[...] = g, d, nm, nv

    spec = pl.BlockSpec((tr, C), lambda i: (i, 0))
    part = pl.BlockSpec((tr, C), lambda i: (i % per_half, 0))
    return pl.pallas_call(
        body, name=name, grid=(R // tr,),
        in_specs=[pl.BlockSpec(memory_space=pltpu.SMEM), spec, spec, spec, part, part], out_specs=[spec] * 4,
        out_shape=[jax.ShapeDtypeStruct((R, C), F32)] * 4, compiler_params=_params(("parallel",)),
    )(core, w, m, v, mine, other)


SMALL_LAYOUT = dict(norm_mix_g=(0, 1, 1024), b_in=(1, 8, 7424), hgrn_norm_g=(9, 1, 1024), norm_ffn_g=(10, 1, 1024),
                    norm_final_g=(11, 1, 1024), hgrn_lb_logits=(12, 2, 2048), attn_sinks=(14, 1, 16))
SMALL_LOSS_ROW, SMALL_ROWS = 15, 16


def _pack_small(grads, loss):
    rows = [jnp.pad(grads[name].astype(F32).reshape(-1), (0, nrows * D_MODEL - n))
            for name, (_, nrows, n) in SMALL_LAYOUT.items()]
    rows.append(jnp.pad(loss.astype(F32).reshape(1), (0, D_MODEL - 1)))
    return jnp.concatenate(rows).reshape(SMALL_ROWS, D_MODEL)


def _adamw_small(w, m, v, g_all):
    names = list(SMALL_LAYOUT)
    n = len(names)

    def body(a_ref, *refs):
        ins, outs = refs[:3 * n], refs[3 * n:]
        g_all_rows = a_ref[0]
        for dev in range(1, 8):
            g_all_rows = g_all_rows + a_ref[dev]
        for i, name in enumerate(names):
            first, nrows, count = SMALL_LAYOUT[name]
            w_ref, m_ref, v_ref = ins[3 * i:3 * i + 3]
            if w_ref.shape[0] == nrows:
                g = g_all_rows[first:first + nrows, :w_ref.shape[1]]
            else:
                last = count - (nrows - 1) * D_MODEL
                g = jnp.concatenate([g_all_rows[r:r + 1, :] for r in range(first, first + nrows - 1)]
                                    + [g_all_rows[first + nrows - 1:first + nrows, :last]], axis=1)
            d, nm, nv = _adam_vals(w_ref[...], g, m_ref[...], v_ref[...])
            for o_ref, val in zip(outs[4 * i:4 * i + 4], (g, d, nm, nv)):
                o_ref[...] = val
        outs[4 * n][...] = g_all_rows[SMALL_LOSS_ROW:SMALL_LOSS_ROW + 1, 0:1]

    res = pl.pallas_call(
        body, name="adamw_small",
        out_shape=[jax.ShapeDtypeStruct(w[name].shape, F32) for name in names for _ in range(4)]
        + [jax.ShapeDtypeStruct((1, 1), F32)],
    )(g_all, *[t[name] for name in names for t in (w, m, v)])
    return {name: res[4 * i:4 * i + 4] for i, name in enumerate(names)}, res[4 * n]


MATRICES = ("w_in", "w_branch_attn", "w_branch_hgrn", "w_out", "w_ffn_gate", "w_ffn_up", "w_ffn_down")
COLUMN_SHARDED = ("w_in", "w_ffn_gate", "w_ffn_up")
WEIGHTS = ("norm_mix_g", "w_in", "b_in", "attn_sinks", "hgrn_lb_logits", "hgrn_norm_g", "w_branch_attn",
           "w_branch_hgrn", "w_out", "norm_ffn_g", "w_ffn_gate", "w_ffn_up", "w_ffn_down", "norm_final_g")


def kernel(x, norm_mix_g, w_in, b_in, attn_sinks, hgrn_lb_logits, hgrn_norm_g, w_branch_attn, w_branch_hgrn, w_out, norm_ffn_g, w_ffn_gate, w_ffn_up, w_ffn_down, norm_final_g, loss_target, m_norm_mix_g, m_w_in, m_b_in, m_attn_sinks, m_hgrn_lb_logits, m_hgrn_norm_g, m_w_branch_attn, m_w_branch_hgrn, m_w_out, m_norm_ffn_g, m_w_ffn_gate, m_w_ffn_up, m_w_ffn_down, m_norm_final_g, v_norm_mix_g, v_w_in, v_b_in, v_attn_sinks, v_hgrn_lb_logits, v_hgrn_norm_g, v_w_branch_attn, v_w_branch_hgrn, v_w_out, v_norm_ffn_g, v_w_ffn_gate, v_w_ffn_up, v_w_ffn_down, v_norm_final_g):
    given = dict(locals())
    w = {n: given[n] for n in WEIGHTS}
    m = {n: given["m_" + n] for n in WEIGHTS}
    v = {n: given["v_" + n] for n in WEIGHTS}

    block = lambda a, n: jnp.transpose(a[0]) if n in COLUMN_SHARDED else a[0]
    unblock = lambda a, n: (jnp.transpose(a) if n in COLUMN_SHARDED else a)[None]
    net = _Net({n: block(w[n], n).astype(MXU_DTYPE) for n in MATRICES})
    net.gathered(("w_in",), [_gather_by_neighbours("gather_w_in", net.shards["w_in"])])
    vec = dict(norm_mix_g=norm_mix_g, b_in=b_in, attn_sinks=attn_sinks, hgrn_lb_logits=hgrn_lb_logits,
               hgrn_norm_g=hgrn_norm_g, norm_ffn_g=norm_ffn_g, norm_final_g=norm_final_g.reshape(1, D_MODEL))
    loss_part, dx, d_vecs = _local_step(x[0], loss_target[0], vec, net)

    small_all, = net.received(*net.last, carried=_small_copies(_pack_small(d_vecs, loss_part)))
    grads, deltas, new_m, new_v = {}, {}, {}, {}
    for n in ("w_ffn_down", "w_ffn_gate", "w_ffn_up", "w_out", "w_branch_attn", "w_branch_hgrn"):
        res, got = _adamw("adamw_" + n, block(w[n], n), block(m[n], n), block(v[n], n), net.sums[n], net.other[n],
                          comm=net.swap(("w_in",)) if n == "w_ffn_down" else None)
        if n == "w_ffn_down":
            net.other["w_in"], = got
        grads[n], deltas[n], new_m[n], new_v[n] = (unblock(r, n) for r in res)
    n = "w_in"
    res = _adamw_by_halves("adamw_" + n, block(w[n], n), block(m[n], n), block(v[n], n), net.sums[n], net.other[n],
                           _place()[2].reshape(1, 1))
    grads[n], deltas[n], new_m[n], new_v[n] = (unblock(r, n) for r in res)
    rows = lambda t: {n: t[n].reshape(-1, t[n].shape[-1]) for n in SMALL_LAYOUT}
    res, loss = _adamw_small(rows(w), rows(m), rows(v), small_all)
    for n, four in res.items():
        grads[n], deltas[n], new_m[n], new_v[n] = (r.reshape(w[n].shape) for r in four)
    loss = loss.reshape(())
    return (loss, dx[None], *[grads[n] for n in WEIGHTS], *[deltas[n] for n in WEIGHTS],
            *[new_m[n] for n in WEIGHTS], *[new_v[n] for n in WEIGHTS])
```

```python
import collections
import functools
import math

import jax
import jax.numpy as jnp
from jax import lax
from jax.experimental import pallas as pl
from jax.experimental.pallas import tpu as pltpu

F32 = jnp.float32
BF16 = jnp.bfloat16
MXU_DTYPE = jnp.bfloat16
SAVED_DTYPE = jnp.bfloat16
MESH_ID = pl.DeviceIdType.MESH

D_MODEL = 1024
HEAD_DIM = 64
Q_HEADS = 16
KV_HEADS = 2
GROUP = Q_HEADS // KV_HEADS
KV_WIDTH = KV_HEADS * HEAD_DIM
ATTN_BLOCK = 128
HGRN_HEADS = 8
HGRN_K = 128
CHUNK = 64
HGRN_TOKENS = 256
FFN = 2816
IN_SPLITS = (1024, 256, 4096, 2048)
EPS = 1e-6
NEG_INF = -1e30
ADAM_LR, ADAM_B1, ADAM_B2, ADAM_EPS, ADAM_WD, ADAM_STEP = 0.001, 0.9, 0.999, 1e-08, 0.01, 10
N_CHIPS = 4
VMEM_LIMIT = 60 * 1024 * 1024
ROW_ALIGN = 16


def _params(sem=None):
    return pltpu.CompilerParams(dimension_semantics=sem, vmem_limit_bytes=VMEM_LIMIT)


def _sigmoid(v):
    return 0.5 * jnp.tanh(0.5 * v) + 0.5


def _dot(a, b, dims):
    return lax.dot_general(a.astype(MXU_DTYPE), b.astype(MXU_DTYPE), (dims, ((), ())),
                           preferred_element_type=F32)


def _nn(a, b):
    return _dot(a, b, ((1,), (0,)))


def _nt(a, b):
    return _dot(a, b, ((1,), (1,)))


def _tn(a, b):
    return _dot(a, b, ((0,), (0,)))


HBM_SPEC = pl.BlockSpec(memory_space=pl.ANY)


class _Carried:
    def __init__(self, arrays, out_shapes, n_remote, n_local, build):
        self.parts = [(len(arrays), len(out_shapes), build)]
        self.arrays, self.out_shapes = list(arrays), list(out_shapes)
        self.scratch = [pltpu.SemaphoreType.DMA((n_remote,)), pltpu.SemaphoreType.DMA((n_remote,)),
                        pltpu.SemaphoreType.DMA((max(n_local, 1),))]

    def __add__(self, other):
        both = _Carried([], [], 1, 0, None)
        both.parts = self.parts + other.parts
        both.arrays, both.out_shapes = self.arrays + other.arrays, self.out_shapes + other.out_shapes
        both.scratch = self.scratch + other.scratch
        return both

    def _built(self, ins, outs, sems):
        for p, (ni, no, build) in enumerate(self.parts):
            yield build(ins[:ni], outs[:no], *sems[3 * p:3 * p + 3])
            ins, outs = ins[ni:], outs[no:]

    def start(self, ins, outs, sems):
        core = lax.axis_index("c")
        for sends, _, local, *other_order in self._built(ins, outs, sems):
            for cp in local:
                cp.start()
            if not other_order:
                for cp in sends:
                    cp.start()
                continue

            @pl.when(core == 0)
            def _():
                for cp in sends:
                    cp.start()

            @pl.when(core == 1)
            def _():
                for cp in other_order[0]:
                    cp.start()

    def wait(self, ins, outs, sems):
        for sends, recvs, local, *_ in self._built(ins, outs, sems):
            for cp in recvs:
                cp.wait_recv()
            for cp in sends:
                cp.wait_send()
            for cp in local:
                cp.wait()


def _join(*comms):
    comms = [c for c in comms if c is not None]
    return functools.reduce(lambda a, b: a + b, comms) if comms else None


def _call(name, body, *, grid, in_specs, out_specs, out_shape, args, scratch=(), semantics=None, comm=None,
          aliases=None):
    n_in, n_out, n_scr = len(in_specs), len(out_specs), len(scratch)
    aliases = aliases or {}
    if comm is None:
        res = pl.pallas_call(body, name=name, grid=grid, in_specs=in_specs, out_specs=out_specs, out_shape=out_shape,
                             scratch_shapes=list(scratch), input_output_aliases=aliases,
                             compiler_params=_params(semantics))(*args)
        return list(res), []
    ci, co = len(comm.arrays), len(comm.out_shapes)

    def carrying(*refs):
        ins, refs = refs[:n_in], refs[n_in:]
        c_ins, refs = refs[:ci], refs[ci:]
        outs, refs = refs[:n_out], refs[n_out:]
        c_outs, refs = refs[:co], refs[co:]
        scr, sems = refs[:n_scr], refs[n_scr:]
        if not grid:
            comm.start(c_ins, c_outs, sems)
            body(*ins, *outs, *scr)
            comm.wait(c_ins, c_outs, sems)
            return
        first = functools.reduce(jnp.logical_and, [pl.program_id(a) == 0 for a in range(len(grid))])
        last = functools.reduce(jnp.logical_and, [pl.program_id(a) == g - 1 for a, g in enumerate(grid)])

        @pl.when(first)
        def _():
            comm.start(c_ins, c_outs, sems)

        body(*ins, *outs, *scr)

        @pl.when(last)
        def _():
            comm.wait(c_ins, c_outs, sems)

    res = pl.pallas_call(
        carrying, name=name, grid=grid, in_specs=list(in_specs) + [HBM_SPEC] * ci,
        out_specs=list(out_specs) + [HBM_SPEC] * co, out_shape=list(out_shape) + comm.out_shapes,
        scratch_shapes=list(scratch) + comm.scratch, input_output_aliases=aliases,
        compiler_params=_params(("arbitrary",) * len(grid) if grid else None),
    )(*args, *comm.arrays)
    return list(res[:n_out]), list(res[n_out:])


_NO_COPIES = object()
_Rows = collections.namedtuple("_Rows", "array first rows")


_Into = collections.namedtuple("_Into", "rows first held")


def _matmul(name, pairs, mode, *, tm, tn, tk, outs, extras=(), epilogue=None, a_colsum=False, into=None,
            comm=_NO_COPIES):
    prod = dict(nn=_nn, nt=_nt, tn=_tn)[mode]
    pairs = [(a, b if isinstance(b, _Rows) else _Rows(b, 0, b.shape[0])) for a, b in pairs]
    a0, b0 = pairs[0]
    M = a0.shape[1] if mode == "tn" else a0.shape[0]
    N = b0.rows if mode == "nt" else b0.array.shape[1]
    steps, in_specs, offset = [], [], 0
    for a, b in pairs:
        K = a.shape[0] if mode == "tn" else a.shape[1]
        t = min(tk, K)
        assert K % t == 0, (name, K, t)
        kmap = functools.partial(lambda k, off, n: jnp.clip(k - off, 0, n - 1), off=offset, n=K // t)
        if mode == "tn":
            in_specs.append(pl.BlockSpec((t, tm), functools.partial(lambda i, j, k, f: (f(k), i), f=kmap)))
        else:
            in_specs.append(pl.BlockSpec((tm, t), functools.partial(lambda i, j, k, f: (i, f(k)), f=kmap)))
        whole = b.first == 0 and b.rows == b.array.shape[0]
        if mode == "nt":
            shape = (tn, t) if whole else (pl.Element(tn), pl.Element(t))
            in_specs.append(pl.BlockSpec(shape, functools.partial(
                lambda i, j, k, f, b, t, whole: (j, f(k)) if whole else (
                    pl.multiple_of(b.first + j * tn, ROW_ALIGN), pl.multiple_of(f(k) * t, 128)),
                f=kmap, b=b, t=t, whole=whole)))
        else:
            assert b.rows == K, (name, b.rows, K)
            shape = (t, tn) if whole else (pl.Element(t), pl.Element(tn))
            in_specs.append(pl.BlockSpec(shape, functools.partial(
                lambda i, j, k, f, b, t, whole: (f(k), j) if whole else (
                    pl.multiple_of(b.first + f(k) * t, ROW_ALIGN), pl.multiple_of(j * tn, 128)),
                f=kmap, b=b, t=t, whole=whole)))
        steps.append((offset, offset + K // t))
        offset += K // t
    assert M % tm == 0 and N % tn == 0, (name, M, N, tm, tn)
    ni, nj, nk = M // tm, N // tn, offset
    npair, ne, no = len(pairs), len(extras), len(outs)
    if epilogue is None:
        epilogue = lambda acc: (acc,)

    def finish(acc, extra_refs, out_refs):
        vals = epilogue(acc, *[r[...] for r in extra_refs])
        for (kind, _), o_ref, val in zip(outs, out_refs, vals):
            if kind == "pn":
                val = jnp.broadcast_to(val, o_ref.shape)
            o_ref[...] = val.astype(o_ref.dtype)

    held = list(into.held) if into is not None and into.held is not None else []

    def body(*refs):
        ab, rest = refs[:2 * npair], refs[2 * npair:]
        extra_refs, rest = rest[:ne], rest[ne + len(held):]
        out_refs = rest[:no]
        if nk == 1:
            finish(prod(ab[0][...], ab[1][...]), extra_refs, out_refs)
            return
        sums_ref, acc_ref = rest[no], rest[-1]
        k = pl.program_id(2)

        @pl.when(k == 0)
        def _():
            acc_ref[...] = jnp.zeros_like(acc_ref)
            if a_colsum:
                sums_ref[...] = jnp.zeros((1, tm), F32)

        if a_colsum:
            sums_ref[...] += jnp.sum(ab[0][...].astype(F32), axis=0, keepdims=True)
        for p, (lo, hi) in enumerate(steps):
            @pl.when(jnp.logical_and(k >= lo, k < hi))
            def _():
                acc_ref[...] += prod(ab[2 * p][...], ab[2 * p + 1][...])

        @pl.when(k == nk - 1)
        def _():
            finish(acc_ref[...], extra_refs, out_refs)

    for _, shape, im in extras:
        in_specs.append(pl.BlockSpec(shape, functools.partial(lambda i, j, k, im: im(i, j), im=im)))
    in_specs += [HBM_SPEC] * len(held)
    aliases = {2 * npair + ne + p: p for p in range(len(held))}
    out_shape, out_specs = [], []
    for kind, dt in outs:
        if kind == "mn" and into is not None:
            out_shape.append(jax.ShapeDtypeStruct((into.rows, N), dt))
            out_specs.append(pl.BlockSpec((pl.Element(tm), pl.Element(tn)), lambda i, j, k: (
                pl.multiple_of(into.first + i * tm, ROW_ALIGN), pl.multiple_of(j * tn, 128))))
        elif kind == "mn":
            out_shape.append(jax.ShapeDtypeStruct((M, N), dt))
            out_specs.append(pl.BlockSpec((tm, tn), lambda i, j, k: (i, j)))
        else:
            out_shape.append(jax.ShapeDtypeStruct((8 * ni, N), dt))
            out_specs.append(pl.BlockSpec((8, tn), lambda i, j, k: (i, j)))
    if a_colsum:
        assert mode == "tn" and npair == 1 and nj == 1 and nk > 1, name
        out_shape.append(jax.ShapeDtypeStruct((1, M), F32))
        out_specs.append(pl.BlockSpec((1, tm), lambda i, j, k: (0, i)))
    grid = (ni, nj, nk)
    if nj > 1 and nk == 1:
        turned = lambda spec: pl.BlockSpec(spec.block_shape, functools.partial(
            lambda j, i, k, im: im(i, j, k), im=spec.index_map))
        in_specs, out_specs, grid = [turned(s) for s in in_specs], [turned(s) for s in out_specs], (nj, ni, nk)
    res, got = _call(name, body, grid=grid, in_specs=in_specs, out_specs=out_specs, out_shape=out_shape,
                     args=[t for a, b in pairs for t in (a, b.array)] + [e[0] for e in extras] + held,
                     scratch=[pltpu.VMEM((tm, tn), F32)] if nk > 1 else [], aliases=aliases,
                     semantics=("parallel", "parallel", "arbitrary"), comm=None if comm is _NO_COPIES else comm)
    return res if comm is _NO_COPIES else (res, got)


def _swiglu_fwd(u, w_gate_t, w_up_t, *, tm, tn, comm=None):
    (T, D), F = u.shape, w_gate_t.shape[0]

    def body(u_ref, wg_ref, wu_ref, g_ref, up_ref, z_ref):
        g, up = _nt(u_ref[...], wg_ref[...]), _nt(u_ref[...], wu_ref[...])
        g_ref[...], up_ref[...] = g.astype(g_ref.dtype), up.astype(up_ref.dtype)
        z_ref[...] = (g * _sigmoid(g) * up).astype(z_ref.dtype)

    w_spec = pl.BlockSpec((tn, D), lambda j, i: (j, 0))
    o_spec = pl.BlockSpec((tm, tn), lambda j, i: (i, j))
    return _call("ffn_hidden", body, grid=(F // tn, T // tm),
                 in_specs=[pl.BlockSpec((tm, D), lambda j, i: (i, 0)), w_spec, w_spec], out_specs=[o_spec] * 3,
                 out_shape=[jax.ShapeDtypeStruct((T, F), SAVED_DTYPE)] * 2 + [jax.ShapeDtypeStruct((T, F), MXU_DTYPE)],
                 args=[u, w_gate_t, w_up_t], semantics=("parallel", "parallel"), comm=comm)


def _merge_fwd(y_a, y_b, w_a, w_b, gates, *, tm):
    T, D = y_a.shape

    def body(ya_ref, yb_ref, wa_ref, wb_ref, ga_ref, gb_ref, pa_ref, pb_ref, m_ref):
        pa, pb = _nn(ya_ref[...], wa_ref[...]), _nn(yb_ref[...], wb_ref[...])
        pa_ref[...], pb_ref[...] = pa, pb
        m_ref[...] = (_sigmoid(ga_ref[...]) * pa + _sigmoid(gb_ref[...]) * pb).astype(m_ref.dtype)

    rows = pl.BlockSpec((tm, D), lambda i: (i, 0))
    whole = pl.BlockSpec((D, D), lambda i: (0, 0))
    return pl.pallas_call(
        body, name="branch_merge", grid=(T // tm,),
        in_specs=[rows, rows, whole, whole, rows, pl.BlockSpec((tm, D), lambda i: (i, 1))], out_specs=[rows] * 3,
        out_shape=[jax.ShapeDtypeStruct((T, D), F32)] * 2 + [jax.ShapeDtypeStruct((T, D), MXU_DTYPE)],
        compiler_params=_params(("parallel",)),
    )(y_a, y_b, w_a, w_b, gates, gates)


def _colsum_partials(p):
    return jnp.sum(p.reshape(-1, 8, p.shape[-1])[:, 0, :], axis=0, keepdims=True)


def _rmsnorm_fwd(name, x, g, tr=512):
    T, D = x.shape

    def body(x_ref, g_ref, u_ref):
        xv = x_ref[...]
        r = lax.rsqrt(jnp.mean(xv * xv, axis=-1, keepdims=True) + EPS)
        u_ref[...] = (xv * r * g_ref[...]).astype(u_ref.dtype)

    return pl.pallas_call(
        body, name=name, grid=(T // tr,),
        in_specs=[pl.BlockSpec((tr, D), lambda i: (i, 0)), pl.BlockSpec((1, D), lambda i: (0, 0))],
        out_specs=pl.BlockSpec((tr, D), lambda i: (i, 0)),
        out_shape=jax.ShapeDtypeStruct((T, D), MXU_DTYPE),
        compiler_params=_params(("parallel",)),
    )(x, g)


def _rmsnorm_bwd_vals(dy, xin, g):
    rstd = lax.rsqrt(jnp.mean(xin * xin, axis=-1, keepdims=True) + EPS)
    xhat = xin * rstd
    dg = jnp.sum(dy * xhat, axis=0, keepdims=True)
    dxh = dy * g
    dx = rstd * (dxh - xhat * jnp.mean(dxh * xhat, axis=-1, keepdims=True))
    return dx, dg


def _colsum(name, a, tr=512, comm=_NO_COPIES):
    T, N = a.shape

    def body(a_ref, o_ref):
        @pl.when(pl.program_id(0) == 0)
        def _():
            o_ref[...] = jnp.zeros_like(o_ref)

        o_ref[...] += jnp.sum(a_ref[...].astype(F32), axis=0, keepdims=True)

    (res,), got = _call(name, body, grid=(T // tr,), in_specs=[pl.BlockSpec((tr, N), lambda i: (i, 0))],
                        out_specs=[pl.BlockSpec((1, N), lambda i: (0, 0))],
                        out_shape=[jax.ShapeDtypeStruct((1, N), F32)], args=[a], semantics=("arbitrary",),
                        comm=None if comm is _NO_COPIES else comm)
    return res if comm is _NO_COPIES else (res, got)


ATTN_SCALE = 1.0 / math.sqrt(HEAD_DIM)
GROUP_LANES = GROUP * ATTN_BLOCK
PAIR = 2 * HEAD_DIM


def _attn_mask():
    kj = lax.broadcasted_iota(jnp.int32, (ATTN_BLOCK, GROUP_LANES), 0)
    qi = lax.broadcasted_iota(jnp.int32, (ATTN_BLOCK, GROUP_LANES), 1) & (ATTN_BLOCK - 1)
    return kj <= qi


def _heads_transposed(ref, g, scale=None):
    parts = []
    for a in range(GROUP // 2):
        lo = (g * GROUP // 2 + a) * PAIR
        pair = ref[:, lo:lo + PAIR].astype(F32)
        pair = (pair if scale is None else pair * scale).T
        parts += [pair[:HEAD_DIM], pair[HEAD_DIM:]]
    return jnp.concatenate(parts, axis=1).astype(MXU_DTYPE)


def _heads_back(ref, g, vt):
    for a in range(GROUP // 2):
        lo = (g * GROUP // 2 + a) * PAIR
        pair = jnp.concatenate([vt[:, (2 * a) * ATTN_BLOCK:(2 * a + 1) * ATTN_BLOCK],
                                vt[:, (2 * a + 1) * ATTN_BLOCK:(2 * a + 2) * ATTN_BLOCK]], axis=0)
        ref[:, lo:lo + PAIR] = pair.T.astype(ref.dtype)


def _kv_parts(kv_ref, g):
    ks = slice(g * HEAD_DIM, (g + 1) * HEAD_DIM)
    vs = slice(KV_WIDTH + g * HEAD_DIM, KV_WIDTH + (g + 1) * HEAD_DIM)
    return kv_ref[:, ks].astype(MXU_DTYPE), kv_ref[:, vs].astype(MXU_DTYPE)


def _sink_rows(sinks):
    return jnp.repeat(sinks.reshape(KV_HEADS, GROUP), ATTN_BLOCK, axis=1)


def _attn_fwd(pq, pkv, sinks, comm=None):
    T = pq.shape[0]
    nb = T // ATTN_BLOCK

    def body(q_ref, kvc_ref, kvp_ref, s_ref, y_ref, lse_ref):
        mask_c = _attn_mask()
        has_prev = pl.program_id(0) > 0
        for g in range(KV_HEADS):
            (kc, vc), (kp, vp) = _kv_parts(kvc_ref, g), _kv_parts(kvp_ref, g)
            qt = _heads_transposed(q_ref, g, ATTN_SCALE)
            s = jnp.where(mask_c, _nn(kc, qt), jnp.where(has_prev, _nn(kp, qt), NEG_INF))
            sink = s_ref[g:g + 1, :]
            m = jnp.maximum(jnp.max(s, axis=0, keepdims=True), sink)
            p = jnp.exp(s - m)
            den = jnp.sum(p, axis=0, keepdims=True) + jnp.exp(sink - m)
            pc = jnp.where(mask_c, p, 0.0)
            _heads_back(y_ref, g, (_tn(vc, pc) + _tn(vp, p - pc)) / den)
            lse = m + jnp.log(den)
            for i in range(GROUP):
                lse_ref[g * GROUP + i:g * GROUP + i + 1, :] = lse[:, i * ATTN_BLOCK:(i + 1) * ATTN_BLOCK]

    return _call(
        "attn_fwd", body, grid=(nb,),
        in_specs=[pl.BlockSpec((ATTN_BLOCK, D_MODEL), lambda n: (n, 0)),
                  pl.BlockSpec((ATTN_BLOCK, 2 * KV_WIDTH), lambda n: (n, 0)),
                  pl.BlockSpec((ATTN_BLOCK, 2 * KV_WIDTH), lambda n: (jnp.maximum(n - 1, 0), 0)),
                  pl.BlockSpec((KV_HEADS, GROUP_LANES), lambda n: (0, 0))],
        out_specs=[pl.BlockSpec((ATTN_BLOCK, D_MODEL), lambda n: (n, 0)),
                   pl.BlockSpec((Q_HEADS, ATTN_BLOCK), lambda n: (0, n))],
        out_shape=[jax.ShapeDtypeStruct((T, D_MODEL), MXU_DTYPE), jax.ShapeDtypeStruct((Q_HEADS, T), F32)],
        args=[pq, pkv, pkv, _sink_rows(sinks)], semantics=("parallel",), comm=comm)


def _attn_bwd(pq, pkv, sinks, lse, dy, comm=None):
    T = pq.shape[0]
    nb = T // ATTN_BLOCK
    cur = lambda n: (jnp.minimum(n, nb - 1), 0)

    def body(q_ref, kvc_ref, kvp_ref, s_ref, lse_ref, dy_ref, dq_ref, dkv_ref, ds_ref, carry, top, bot):
        n = pl.program_id(0)

        @pl.when(n == 0)
        def _():
            carry[...] = jnp.zeros_like(carry)
            ds_ref[...] = jnp.zeros_like(ds_ref)

        @pl.when(n < nb)
        def _():
            mask_c = _attn_mask()
            valid = jnp.logical_or(mask_c, n > 0)
            for g in range(KV_HEADS):
                ks = slice(g * HEAD_DIM, (g + 1) * HEAD_DIM)
                vs = slice(KV_WIDTH + g * HEAD_DIM, KV_WIDTH + (g + 1) * HEAD_DIM)
                (kc, vc), (kp, vp) = _kv_parts(kvc_ref, g), _kv_parts(kvp_ref, g)
                qt = _heads_transposed(q_ref, g, ATTN_SCALE)
                dot = _heads_transposed(dy_ref, g)
                lse = jnp.concatenate([lse_ref[g * GROUP + i:g * GROUP + i + 1, :] for i in range(GROUP)], axis=1)
                p = jnp.where(valid, jnp.exp(jnp.where(mask_c, _nn(kc, qt), _nn(kp, qt)) - lse), 0.0)
                dp = jnp.where(mask_c, _nn(vc, dot), _nn(vp, dot))
                delta = jnp.sum(p * dp, axis=0, keepdims=True)
                ds = p * (dp - delta)
                ds_c, p_c = jnp.where(mask_c, ds, 0.0), jnp.where(mask_c, p, 0.0)
                ds_p, p_p = ds - ds_c, p - p_c
                _heads_back(dq_ref, g, (_tn(kc, ds_c) + _tn(kp, ds_p)) * ATTN_SCALE)
                bot[:, ks], bot[:, vs] = _nt(ds_c, qt), _nt(p_c, dot)
                top[:, ks], top[:, vs] = _nt(ds_p, qt), _nt(p_p, dot)
                ds_ref[g:g + 1, :] -= jnp.exp(s_ref[g:g + 1, :] - lse) * delta
            dkv_ref[...] = (carry[...] + top[...]).astype(dkv_ref.dtype)
            carry[...] = bot[...]

        @pl.when(n == nb)
        def _():
            dkv_ref[...] = carry[...].astype(dkv_ref.dtype)

    return _call(
        "attn_bwd", body, grid=(nb + 1,),
        in_specs=[pl.BlockSpec((ATTN_BLOCK, D_MODEL), cur),
                  pl.BlockSpec((ATTN_BLOCK, 2 * KV_WIDTH), cur),
                  pl.BlockSpec((ATTN_BLOCK, 2 * KV_WIDTH), lambda n: (jnp.maximum(jnp.minimum(n, nb - 1) - 1, 0), 0)),
                  pl.BlockSpec((KV_HEADS, GROUP_LANES), lambda n: (0, 0)),
                  pl.BlockSpec((Q_HEADS, ATTN_BLOCK), lambda n: (0, jnp.minimum(n, nb - 1))),
                  pl.BlockSpec((ATTN_BLOCK, D_MODEL), cur)],
        out_specs=[pl.BlockSpec((ATTN_BLOCK, D_MODEL), cur),
                   pl.BlockSpec((ATTN_BLOCK, 2 * KV_WIDTH), lambda n: (jnp.maximum(n - 1, 0), 0)),
                   pl.BlockSpec((KV_HEADS, GROUP_LANES), lambda n: (0, 0))],
        out_shape=[jax.ShapeDtypeStruct((T, D_MODEL), MXU_DTYPE),
                   jax.ShapeDtypeStruct((T, 2 * KV_WIDTH), MXU_DTYPE),
                   jax.ShapeDtypeStruct((KV_HEADS, GROUP_LANES), F32)],
        scratch=[pltpu.VMEM((ATTN_BLOCK, 2 * KV_WIDTH), F32)] * 3,
        args=[pq, pkv, pkv, _sink_rows(sinks), lse, dy], semantics=("arbitrary",), comm=comm)


def _lower_bound(l):
    m = jnp.maximum(l[0:1], l[1:2])
    e0, e1 = jnp.exp(l[0:1] - m), jnp.exp(l[1:2] - m)
    return e0 / (e0 + e1)


def _tri(lower):
    r = lax.broadcasted_iota(jnp.int32, (CHUNK, CHUNK), 0)
    c = lax.broadcasted_iota(jnp.int32, (CHUNK, CHUNK), 1)
    return (r >= c) if lower else (c >= r)


def _chunk_sum(mask, v):
    ones = mask.astype(BF16)
    hi = v.astype(BF16)
    rest = v - hi.astype(F32)
    mid = rest.astype(BF16)
    lo = (rest - mid.astype(F32)).astype(BF16)
    part = lambda t: lax.dot_general(ones, t, (((1,), (0,)), ((), ())), preferred_element_type=F32)
    return part(hi) + part(mid) + part(lo)


def _hgrn_chunk_inputs(hq, hf, lb, causal):
    half_t = 0.5 * jnp.tanh(0.5 * hf)
    sg, sgn = 0.5 + half_t, 0.5 - half_t
    f = lb + (1.0 - lb) * sg
    kk = (1.0 - lb) * sgn
    sq = _sigmoid(hq)
    q = hq * sq
    b = _chunk_sum(causal, jnp.log(f))
    bm, bl = b[CHUNK // 2 - 1:CHUNK // 2, :], b[CHUNK - 1:CHUNK, :]
    e_qm, e_km = jnp.exp(b - bm), jnp.exp(bm - b)
    e_qs, e_kl = e_qm * jnp.exp(bm), e_km * jnp.exp(bl - bm)
    return dict(sg=sg, sgn=sgn, f=f, kk=kk, sq=sq, q=q, e_qm=e_qm, e_km=e_km, e_qs=e_qs, e_kl=e_kl,
                qm=q * e_qm, km=kk * e_km, qs=q * e_qs, kl=kk * e_kl, el=jnp.exp(bl))


def _hgrn_fwd(ph, lb_logits, norm_g, comm=None):
    T = ph.shape[0]
    nblk, cpb = T // HGRN_TOKENS, HGRN_TOKENS // CHUNK
    col = lambda c: pl.BlockSpec((HGRN_TOKENS, D_MODEL), functools.partial(lambda i, c: (i, c), c=c))

    def body(hq_ref, hf_ref, hi_ref, hg_ref, l_ref, ng_ref, y_ref, o_ref, st_ref, s_ref):
        @pl.when(pl.program_id(0) == 0)
        def _():
            s_ref[...] = jnp.zeros_like(s_ref)

        lb = _lower_bound(l_ref[...])
        causal = _tri(True)
        for c in range(cpb):
            rows = slice(c * CHUNK, (c + 1) * CHUNK)
            t = _hgrn_chunk_inputs(hq_ref[rows, :], hf_ref[rows, :], lb, causal)
            qm, km, qs, kl = (t[n].astype(MXU_DTYPE) for n in ("qm", "km", "qs", "kl"))
            v = hi_ref[rows, :].astype(MXU_DTYPE)
            for h in range(HGRN_HEADS):
                ls = slice(h * HGRN_K, (h + 1) * HGRN_K)
                st = s_ref[h]
                st_ref[c, ls, :] = st
                a = jnp.where(causal, _nt(qm[:, ls], km[:, ls]), 0.0)
                o_ref[rows, ls] = _nn(a, v[:, ls]) + _nt(qs[:, ls], st)
                s_ref[h] = t["el"][:, ls] * st + _tn(v[:, ls], kl[:, ls])
        for h in range(HGRN_HEADS):
            ls = slice(h * HGRN_K, (h + 1) * HGRN_K)
            o = o_ref[:, ls]
            r = lax.rsqrt(jnp.mean(o * o, axis=-1, keepdims=True) + EPS)
            y_ref[:, ls] = (o * r * ng_ref[:, ls] * _sigmoid(hg_ref[:, ls])).astype(y_ref.dtype)

    return _call(
        "hgrn_fwd", body, grid=(nblk,),
        in_specs=[col(0), col(1), col(2), col(3),
                  pl.BlockSpec((2, D_MODEL), lambda i: (0, 0)), pl.BlockSpec((1, D_MODEL), lambda i: (0, 0))],
        out_specs=[pl.BlockSpec((HGRN_TOKENS, D_MODEL), lambda i: (i, 0)),
                   pl.BlockSpec((HGRN_TOKENS, D_MODEL), lambda i: (i, 0)),
                   pl.BlockSpec((cpb, D_MODEL, HGRN_K), lambda i: (i, 0, 0))],
        out_shape=[jax.ShapeDtypeStruct((T, D_MODEL), MXU_DTYPE), jax.ShapeDtypeStruct((T, D_MODEL), F32),
                   jax.ShapeDtypeStruct((T // CHUNK, D_MODEL, HGRN_K), F32)],
        scratch=[pltpu.VMEM((HGRN_HEADS, HGRN_K, HGRN_K), F32)],
        args=[ph, ph, ph, ph, lb_logits, norm_g], semantics=("arbitrary",), comm=comm)


def _hgrn_bwd(ph, o_raw, states, dy, lb_logits, norm_g, comm=None):
    T = ph.shape[0]
    nblk, cpb = T // HGRN_TOKENS, HGRN_TOKENS // CHUNK
    rev = lambda i: nblk - 1 - i
    col = lambda c: pl.BlockSpec((HGRN_TOKENS, D_MODEL), functools.partial(lambda i, c: (rev(i), c), c=c))
    tok = pl.BlockSpec((HGRN_TOKENS, D_MODEL), lambda i: (rev(i), 0))

    def body(hq_ref, hf_ref, hi_ref, hg_ref, o_ref, st_ref, dy_ref, l_ref, ng_ref,
             dph_ref, dng_ref, dl_ref, dst_ref, dlb_ref, do_s, dqm_s, dkm_s, dqs_s, dkl_s, dv_s, del_s):
        i = pl.program_id(0)

        @pl.when(i == 0)
        def _():
            dst_ref[...] = jnp.zeros_like(dst_ref)
            dlb_ref[...] = jnp.zeros_like(dlb_ref)
            dng_ref[...] = jnp.zeros_like(dng_ref)

        lb = _lower_bound(l_ref[...])
        causal, anti = _tri(True), _tri(False)
        row = lax.broadcasted_iota(jnp.int32, (CHUNK, D_MODEL), 0)
        for c in reversed(range(cpb)):
            rows = slice(c * CHUNK, (c + 1) * CHUNK)
            hq = hq_ref[rows, :]
            t = _hgrn_chunk_inputs(hq, hf_ref[rows, :], lb, causal)
            sgg = _sigmoid(hg_ref[rows, :])
            dyv = dy_ref[rows, :]
            for h in range(HGRN_HEADS):
                ls = slice(h * HGRN_K, (h + 1) * HGRN_K)
                o = o_ref[rows, ls]
                r = lax.rsqrt(jnp.mean(o * o, axis=-1, keepdims=True) + EPS)
                nrm = o * r
                g_h = sgg[:, ls]
                dph_ref[rows, 3 * D_MODEL + h * HGRN_K:3 * D_MODEL + (h + 1) * HGRN_K] = (
                    dyv[:, ls] * nrm * ng_ref[:, ls] * g_h * (1.0 - g_h)).astype(dph_ref.dtype)
                dyg = dyv[:, ls] * g_h
                dng_ref[:, ls] += jnp.sum(dyg * nrm, axis=0, keepdims=True)
                dn = dyg * ng_ref[:, ls]
                do_s[:, ls] = r * (dn - nrm * jnp.mean(dn * nrm, axis=-1, keepdims=True))
            qm, km, qs, kl = (t[n].astype(MXU_DTYPE) for n in ("qm", "km", "qs", "kl"))
            v = hi_ref[rows, :].astype(MXU_DTYPE)
            do = do_s[...].astype(MXU_DTYPE)
            for h in range(HGRN_HEADS):
                ls = slice(h * HGRN_K, (h + 1) * HGRN_K)
                st = st_ref[c, ls, :]
                dst = dst_ref[h]
                a = jnp.where(causal, _nt(qm[:, ls], km[:, ls]), 0.0)
                da = jnp.where(causal, _nt(do[:, ls], v[:, ls]), 0.0)
                dv_s[:, ls] = _tn(a, do[:, ls]) + _nt(kl[:, ls], dst)
                dkl_s[:, ls] = _nn(v[:, ls], dst)
                dqs_s[:, ls] = _nn(do[:, ls], st)
                del_s[:, ls] = jnp.sum(dst * st, axis=0, keepdims=True)
                dst_ref[h] = _tn(do[:, ls], qs[:, ls]) + t["el"][:, ls] * dst
                dqm_s[:, ls] = _nn(da, km[:, ls])
                dkm_s[:, ls] = _tn(da, qm[:, ls])
            dqm, dkm, dqs, dkl = dqm_s[...], dkm_s[...], dqs_s[...], dkl_s[...]
            dq = dqm * t["e_qm"] + dqs * t["e_qs"]
            dk = dkm * t["e_km"] + dkl * t["e_kl"]
            t_qm, t_km, t_kl = dqm * t["qm"], dkm * t["km"], dkl * t["kl"]
            db = t_qm - t_km + dqs * t["qs"] - t_kl
            db_mid = jnp.sum(t_km - t_qm, axis=0, keepdims=True)
            db_last = jnp.sum(t_kl, axis=0, keepdims=True) + del_s[...] * t["el"]
            db = db + jnp.where(row == CHUNK // 2 - 1, db_mid, 0.0) + jnp.where(row == CHUNK - 1, db_last, 0.0)
            dlogf = _chunk_sum(anti, db)
            sq, sg, sgn, f = t["sq"], t["sg"], t["sgn"], t["f"]
            dph_ref[rows, 0:D_MODEL] = (dq * (sq * (1.0 + hq * (1.0 - sq)))).astype(dph_ref.dtype)
            dph_ref[rows, D_MODEL:2 * D_MODEL] = (
                dlogf * (1.0 - lb) * sg * (1.0 - sg) / f - dk * (1.0 - lb) * sgn * (1.0 - sgn)).astype(dph_ref.dtype)
            dph_ref[rows, 2 * D_MODEL:3 * D_MODEL] = dv_s[...].astype(dph_ref.dtype)
            dlb_ref[...] += jnp.sum(dlogf * (1.0 - sg) / f - dk * sgn, axis=0, keepdims=True)

        @pl.when(i == nblk - 1)
        def _():
            dl0 = dlb_ref[...] * lb * (1.0 - lb)
            dl_ref[0:1, :] = dl0
            dl_ref[1:2, :] = -dl0

    wide = pltpu.VMEM((CHUNK, D_MODEL), F32)
    return _call(
        "hgrn_bwd", body, grid=(nblk,),
        in_specs=[col(0), col(1), col(2), col(3), tok,
                  pl.BlockSpec((cpb, D_MODEL, HGRN_K), lambda i: (rev(i), 0, 0)), tok,
                  pl.BlockSpec((2, D_MODEL), lambda i: (0, 0)), pl.BlockSpec((1, D_MODEL), lambda i: (0, 0))],
        out_specs=[pl.BlockSpec((HGRN_TOKENS, 4 * D_MODEL), lambda i: (rev(i), 0)),
                   pl.BlockSpec((1, D_MODEL), lambda i: (0, 0)), pl.BlockSpec((2, D_MODEL), lambda i: (0, 0))],
        out_shape=[jax.ShapeDtypeStruct((T, 4 * D_MODEL), MXU_DTYPE), jax.ShapeDtypeStruct((1, D_MODEL), F32),
                   jax.ShapeDtypeStruct((2, D_MODEL), F32)],
        scratch=[pltpu.VMEM((HGRN_HEADS, HGRN_K, HGRN_K), F32), pltpu.VMEM((1, D_MODEL), F32),
                 wide, wide, wide, wide, wide, wide, pltpu.VMEM((1, D_MODEL), F32)],
        args=[ph, ph, ph, ph, o_raw, states, dy, lb_logits, norm_g], semantics=("arbitrary",), comm=comm)


def _local_step(x, target, vec, net):
    T, D = x.shape
    norm_mix_g, b_in, sinks, lb_logits = vec["norm_mix_g"], vec["b_in"], vec["attn_sinks"], vec["hgrn_lb_logits"]
    hgrn_norm_g, norm_ffn_g, norm_final_g = vec["hgrn_norm_g"], vec["norm_ffn_g"], vec["norm_final_g"]
    w_in = net.full("w_in")
    o_q, o_kv, o_h, o_g = (sum(IN_SPLITS[:i]) for i in range(4))
    w_q, w_kv, w_h, w_g = (_Rows(w_in, o, n) for o, n in zip((o_q, o_kv, o_h, o_g), IN_SPLITS))
    b_q, b_kv, b_h, b_g = (b_in[:, o:o + n] for o, n in zip((o_q, o_kv, o_h, o_g), IN_SPLITS))
    bias = lambda acc, b: (acc + b,)
    both = lambda acc: (acc, acc)
    grad_outs = [("mn", F32), ("mn", MXU_DTYPE)]
    row_vec = lambda n: ((1, n), lambda i, j: (0, j))
    tile = lambda tm, tn: ((tm, tn), lambda i, j: (i, j))
    TM = 512
    BIG = min(T, 1024)

    u = _rmsnorm_fwd("norm_mix", x, norm_mix_g)
    pq, = _matmul("in_q", [(u, w_q)], "nt", tm=TM, tn=1024, tk=1024, outs=[("mn", MXU_DTYPE)],
                  extras=[(b_q, *row_vec(1024))], epilogue=bias)
    pkv, = _matmul("in_kv", [(u, w_kv)], "nt", tm=TM, tn=256, tk=1024, outs=[("mn", MXU_DTYPE)],
                   extras=[(b_kv, *row_vec(256))], epilogue=bias)
    names = ("w_branch_attn", "w_branch_hgrn")
    (ph,), got = _matmul("in_h", [(u, w_h)], "nt", tm=TM, tn=1024, tk=1024, outs=[("mn", F32)],
                         extras=[(b_h, *row_vec(1024))], epilogue=bias, comm=net.gather(names))
    net.gathered(names, got)
    names = ("w_out",)
    (pg,), got = _matmul("in_g", [(u, w_g)], "nt", tm=TM, tn=1024, tk=1024, outs=[("mn", F32)],
                         extras=[(b_g, *row_vec(1024))], epilogue=bias, comm=net.gather(names))
    net.gathered(names, got)
    names = ("w_ffn_gate",)
    (y_attn, lse), got = _attn_fwd(pq, pkv, sinks, comm=net.gather(names))
    net.gathered(names, got)
    names = ("w_ffn_up",)
    (y_hgrn, o_raw, states), got = _hgrn_fwd(ph, lb_logits, hgrn_norm_g, comm=net.gather(names))
    net.gathered(names, got)
    w_ba, w_bh, w_out = net.full("w_branch_attn"), net.full("w_branch_hgrn"), net.full("w_out")
    w_gate, w_up = net.full("w_ffn_gate"), net.full("w_ffn_up")
    ya, yb, merged = _merge_fwd(y_attn, y_hgrn, w_ba, w_bh, pg, tm=TM)
    gate_a = (pg, (TM, 1024), lambda i, j: (i, 0))
    gate_b = (pg, (TM, 1024), lambda i, j: (i, 1))

    def resid_norm(acc, xin, g):
        h = xin + acc
        r = lax.rsqrt(jnp.mean(h * h, axis=-1, keepdims=True) + EPS)
        return h, h * r * g

    h1, u2 = _matmul("out_proj", [(merged, w_out)], "nn", tm=TM, tn=1024, tk=1024,
                     outs=[("mn", F32), ("mn", MXU_DTYPE)],
                     extras=[(x, *tile(TM, 1024)), (norm_ffn_g, *row_vec(1024))], epilogue=resid_norm)
    FT = FFN // 2
    names = ("w_ffn_down",)
    (gpre, up, z), got = _swiglu_fwd(u2, w_gate, w_up, tm=TM, tn=FT, comm=net.gather(names))
    net.gathered(names, got)
    w_down = net.full("w_ffn_down")

    def loss_head(acc, h1_v, tgt, g):
        h2 = h1_v + acc
        r = lax.rsqrt(jnp.mean(h2 * h2, axis=-1, keepdims=True) + EPS)
        xhat = h2 * r
        err = xhat * g - tgt
        part = 0.5 * jnp.sum(jnp.sum(err * err, axis=-1, keepdims=True), axis=0, keepdims=True) / D
        dyv = err / D
        dxh = dyv * g
        dh2 = r * (dxh - xhat * jnp.mean(dxh * xhat, axis=-1, keepdims=True))
        return dh2, dh2, jnp.sum(dyv * xhat, axis=0, keepdims=True), jnp.broadcast_to(part, (1, D))

    dh2, dh2b, dgf_p, loss_p = _matmul(
        "ffn_down", [(z, w_down)], "nn", tm=TM, tn=1024, tk=FFN,
        outs=[("mn", F32), ("mn", MXU_DTYPE), ("pn", F32), ("pn", F32)],
        extras=[(h1, *tile(TM, 1024)), (target, *tile(TM, 1024)), (norm_final_g, *row_vec(1024))],
        epilogue=loss_head)
    loss = jnp.sum(loss_p.reshape(-1, 8, D)[:, 0, 0])
    d_norm_final = _colsum_partials(dgf_p)

    def swiglu_bwd(acc, gv, upv):
        gv, upv = gv.astype(F32), upv.astype(F32)
        s = _sigmoid(gv)
        return acc * upv * (s * (1.0 + gv * (1.0 - s))), acc * (gv * s)

    dgp, dup = _matmul("d_ffn_hidden", [(dh2b, w_down)], "nt", tm=TM, tn=FT, tk=1024,
                       outs=[("mn", MXU_DTYPE), ("mn", MXU_DTYPE)],
                       extras=[(gpre, *tile(TM, FT)), (up, *tile(TM, FT))], epilogue=swiglu_bwd)
    d_w_down = _matmul("dw_down", [(z, dh2b)], "tn", tm=FT, tn=1024, tk=1024, outs=grad_outs, epilogue=both)

    def norm_ffn_bwd(acc, h1_v, g, dres):
        dx, dg = _rmsnorm_bwd_vals(acc, h1_v, g)
        dh = dres + dx
        return dh, dh, dg

    names = ("w_ffn_down",)
    (dh1, dh1b, dg2_p), got = _matmul(
        "d_ffn_in", [(dgp, w_gate), (dup, w_up)], "nn", tm=BIG, tn=1024, tk=FT,
        outs=[("mn", F32), ("mn", MXU_DTYPE), ("pn", F32)],
        extras=[(h1, *tile(BIG, 1024)), (norm_ffn_g, *row_vec(1024)), (dh2, *tile(BIG, 1024))],
        epilogue=norm_ffn_bwd,
        comm=net.exchange(dict(w_ffn_down=[d_w_down])))
    net.received(names, (), got)
    d_norm_ffn = _colsum_partials(dg2_p)
    d_w_gate = _matmul("dw_gate", [(dgp, u2)], "tn", tm=FT, tn=1024, tk=1024, outs=grad_outs, epilogue=both)
    d_w_up = _matmul("dw_up", [(dup, u2)], "tn", tm=FT, tn=1024, tk=1024, outs=grad_outs, epilogue=both)

    def merge_bwd(acc, ya_v, yb_v, ga, gb):
        sa, sb = _sigmoid(ga), _sigmoid(gb)
        return acc * sa, acc * sb, acc * ya_v * sa * (1.0 - sa), acc * yb_v * sb * (1.0 - sb)

    dya, dyb, dga, dgb = _matmul(
        "d_merged", [(dh1b, w_out)], "nt", tm=TM, tn=1024, tk=1024, outs=[("mn", MXU_DTYPE)] * 4,
        extras=[(ya, *tile(TM, 1024)), (yb, *tile(TM, 1024)), gate_a, gate_b], epilogue=merge_bwd)
    tn_grad = functools.partial(_matmul, mode="tn", tn=1024, tk=1024, outs=grad_outs, epilogue=both)
    d_w_out = tn_grad("dw_out", [(merged, dh1b)], tm=1024)
    dy_attn, = _matmul("d_y_attn", [(dya, w_ba)], "nt", tm=TM, tn=1024, tk=1024, outs=[("mn", MXU_DTYPE)])
    dy_hgrn, = _matmul("d_y_hgrn", [(dyb, w_bh)], "nt", tm=TM, tn=1024, tk=1024, outs=[("mn", F32)])
    d_w_ba = tn_grad("dw_branch_a", [(y_attn, dya)], tm=1024)
    d_w_bh = tn_grad("dw_branch_b", [(y_hgrn, dyb)], tm=1024)

    names, swap = ("w_ffn_gate",), ("w_ffn_down",)
    (dq, dkv, dsink), got = _attn_bwd(pq, pkv, sinks, lse, dy_attn,
                                      comm=net.exchange(dict(w_ffn_gate=[d_w_gate]), swap))
    net.received(names, swap, got)
    names, swap = ("w_ffn_up", "w_out", "w_branch_attn", "w_branch_hgrn"), ("w_ffn_gate",)
    (dph, d_hgrn_norm, d_lb_logits), got = _hgrn_bwd(
        ph, o_raw, states, dy_hgrn, lb_logits, hgrn_norm_g,
        comm=net.exchange(dict(w_ffn_up=[d_w_up], w_out=[d_w_out], w_branch_attn=[d_w_ba],
                               w_branch_hgrn=[d_w_bh]), swap))
    net.received(names, swap, got)

    d_w_in, db, rows_in = None, {}, sum(IN_SPLITS)
    for piece, dp, first, tm_p in (("q", dq, o_q, 1024), ("kv", dkv, o_kv, 256), ("h", dph, o_h, 1024),
                                   ("ga", dga, o_g, 1024), ("gb", dgb, o_g + D, 1024)):
        *d_w_in, db[piece] = tn_grad("dw_in_" + piece, [(dp, u)], tm=tm_p, a_colsum=True,
                                     into=_Into(rows_in, first, d_w_in))
    d_w_in = [tuple(d_w_in)]

    def norm_mix_bwd(acc, xin, g, dres):
        dx, dg = _rmsnorm_bwd_vals(acc, xin, g)
        return dres + dx, dg

    first_level = net.presum_begin("w_in", d_w_in)
    halves = net.presum_end("w_in", [] if first_level is None else _copies_alone("presum_swap_w_in", first_level))
    names, swap = ("w_in",), ("w_ffn_up", "w_out", "w_branch_attn", "w_branch_hgrn")
    (dx, dg1_p), got = _matmul(
        "d_u", [(dq, w_q), (dkv, w_kv), (dph, w_h), (dga, _Rows(w_in, o_g, D)), (dgb, _Rows(w_in, o_g + D, D))], "nn",
        tm=BIG, tn=1024, tk=512, outs=[("mn", F32), ("pn", F32)],
        extras=[(x, *tile(BIG, 1024)), (norm_mix_g, *row_vec(1024)), (dh1, *tile(BIG, 1024))],
        epilogue=norm_mix_bwd,
        comm=_join(halves, net.swap(swap)))
    net.last = (names, swap, got)
    d_norm_mix = _colsum_partials(dg1_p)
    d_b_in = jnp.concatenate([db[piece] for piece in ("q", "kv", "h", "ga", "gb")], axis=1)
    vecs = dict(norm_mix_g=d_norm_mix, b_in=d_b_in, attn_sinks=jnp.sum(dsink.reshape(Q_HEADS, ATTN_BLOCK), axis=1).reshape(1, Q_HEADS),
                hgrn_lb_logits=d_lb_logits,
                hgrn_norm_g=d_hgrn_norm, norm_ffn_g=d_norm_ffn, norm_final_g=d_norm_final)
    return loss, dx, vecs


def _place():
    return lax.axis_index("x"), lax.axis_index("y"), lax.axis_index("c")


def _other_chips(x, y):
    return [(1 - x, y), (x, 1 - y), (1 - x, 1 - y)]


def _y_first(copies):
    return [copies[3 * (i // 3) + (1, 0, 2)[i % 3]] for i in range(len(copies))]


def _gather_copies(shards):
    n = len(shards)

    def build(ins, outs, send_sems, recv_sems, local_sems):
        x, y, c = _place()
        mine = 2 * x + y
        local = [pltpu.make_async_copy(ins[w], outs[w].at[mine], local_sems.at[w]) for w in range(n)]
        sends, recvs = [], []
        for w in range(n):
            for k, (px, py) in enumerate(_other_chips(x, y)):
                sem = 3 * w + k
                sends.append(pltpu.make_async_remote_copy(
                    src_ref=ins[w], dst_ref=outs[w].at[mine], send_sem=send_sems.at[sem], recv_sem=recv_sems.at[sem],
                    device_id=(px, py, c), device_id_type=MESH_ID))
                recvs.append(pltpu.make_async_remote_copy(
                    src_ref=ins[w], dst_ref=outs[w].at[2 * px + py], send_sem=send_sems.at[sem],
                    recv_sem=recv_sems.at[sem], device_id=(px, py, c), device_id_type=MESH_ID))
        return sends, recvs, local, _y_first(sends)

    return _Carried(shards, [jax.ShapeDtypeStruct((N_CHIPS,) + s.shape, s.dtype) for s in shards], 3 * n, n, build)


def _grad_copies(stacked):
    n = len(stacked)

    def build(ins, outs, send_sems, recv_sems, local_sems):
        x, y, c = _place()
        sends = []
        for w in range(n):
            for k, (px, py) in enumerate(_other_chips(x, y)):
                sem = 3 * w + k
                sends.append(pltpu.make_async_remote_copy(
                    src_ref=ins[w].at[2 * px + py], dst_ref=outs[w].at[k], send_sem=send_sems.at[sem],
                    recv_sem=recv_sems.at[sem], device_id=(px, py, c), device_id_type=MESH_ID))
        return sends, sends, [], _y_first(sends)

    return _Carried(stacked, [jax.ShapeDtypeStruct((3,) + s.shape[1:], s.dtype) for s in stacked], 3 * n, 0, build)


def _small_copies(small):
    def build(ins, outs, send_sems, recv_sems, local_sems):
        small_ref, all_ref = ins[0], outs[0]
        x, y, c = _place()
        me = 4 * x + 2 * y + c
        sends, recvs = [], []
        for r in range(1, 8):
            px = 1 - x if r & 4 else x
            py = 1 - y if r & 2 else y
            pc = 1 - c if r & 1 else c
            sends.append(pltpu.make_async_remote_copy(
                src_ref=small_ref, dst_ref=all_ref.at[me], send_sem=send_sems.at[r - 1], recv_sem=recv_sems.at[r - 1],
                device_id=(px, py, pc), device_id_type=MESH_ID))
            recvs.append(pltpu.make_async_remote_copy(
                src_ref=small_ref, dst_ref=all_ref.at[4 * px + 2 * py + pc], send_sem=send_sems.at[r - 1],
                recv_sem=recv_sems.at[r - 1], device_id=(px, py, pc), device_id_type=MESH_ID))
        return sends, recvs, [pltpu.make_async_copy(small_ref, all_ref.at[me], local_sems.at[0])]

    return _Carried([small], [jax.ShapeDtypeStruct((8,) + small.shape, small.dtype)], 7, 1, build)


def _gather_by_neighbours(name, shard):
    half = shard.shape[0] // 2
    quarter = half // 2

    def body(in_ref, out_ref, send_sems, recv_sems, local_sem):
        for core in (0, 1):
            @pl.when(lax.axis_index("c") == core)
            def _():
                program(core, in_ref, out_ref, send_sems, recv_sems, local_sem)

    def program(c, in_ref, out_ref, send_sems, recv_sems, local_sem):
        x, y, _ = _place()
        chip = lambda px, py: 2 * px + py
        to_x, to_y, sibling = (1 - x, y, c), (x, 1 - y, c), (x, y, 1 - c)
        x_blk, y_blk, d_blk = chip(1 - x, y), chip(x, 1 - y), chip(1 - x, 1 - y)
        mine, theirs = c * half, (1 - c) * half

        def copy(sem, rows, block, to, src=None):
            place = out_ref.at[block, pl.ds(rows[0], rows[1])]
            return pltpu.make_async_remote_copy(
                src_ref=place if src is None else src, dst_ref=place, send_sem=send_sems.at[sem],
                recv_sem=recv_sems.at[sem], device_id=to, device_id_type=MESH_ID)

        own = pltpu.make_async_copy(in_ref, out_ref.at[chip(x, y)], local_sem)
        own.start()
        my_rows = in_ref.at[pl.ds(mine, half)]
        along_x = dict(send=copy(0, (mine, half), chip(x, y), to_x, src=my_rows),
                       landed=copy(0, (mine, half), x_blk, to_x),
                       onward=[copy(3, (mine + quarter, quarter), x_blk, to_y), copy(4, (mine, half), x_blk, sibling)],
                       diagonal=copy(2, (mine, quarter), d_blk, to_x))
        along_y = dict(send=copy(1, (mine, half), chip(x, y), to_y, src=my_rows),
                       landed=copy(1, (mine, half), y_blk, to_y),
                       onward=[copy(2, (mine, quarter), y_blk, to_x), copy(5, (mine, half), y_blk, sibling)],
                       diagonal=copy(3, (mine + quarter, quarter), d_blk, to_y))
        last = copy(6, (mine, half), d_blk, sibling)

        order = (along_x, along_y) if c == 0 else (along_y, along_x)
        for axis in order:
            axis["send"].start()
        for axis in order:
            axis["landed"].wait_recv()
            for cp in axis["onward"]:
                cp.start()
        for axis in order:
            axis["diagonal"].wait_recv()
        last.start()
        for sem, block in ((4, x_blk), (5, y_blk), (6, d_blk)):
            copy(sem, (theirs, half), block, sibling).wait_recv()
        for cp in [along_x["send"], along_y["send"]] + along_x["onward"] + along_y["onward"] + [last]:
            cp.wait_send()
        own.wait()

    return pl.pallas_call(
        body, name=name, in_specs=[HBM_SPEC], out_specs=HBM_SPEC,
        out_shape=jax.ShapeDtypeStruct((N_CHIPS,) + shard.shape, shard.dtype),
        scratch_shapes=[pltpu.SemaphoreType.DMA((7,)), pltpu.SemaphoreType.DMA((7,)), pltpu.SemaphoreType.DMA(())],
    )(shard)


def _copies_alone(name, comm):
    return _call(name, lambda: None, grid=(), in_specs=[], out_specs=[], out_shape=[], args=[], comm=comm)[1]


class _Net:
    def __init__(self, shards):
        self.shards = shards
        self.whole, self.own, self.theirs, self.sums, self.other = {}, {}, {}, {}, {}
        x, y, _ = _place()
        self.chip = 2 * x + y

    def gather(self, names):
        return _gather_copies([self.shards[n] for n in names])

    def gathered(self, names, got):
        for n, g in zip(names, got):
            self.whole[n] = g.reshape(-1, g.shape[-1])

    def full(self, name):
        return self.whole[name]

    def exchange(self, grads, swap=()):
        stacked = []
        for n, pieces in grads.items():
            (keep, send), = pieces
            self.own[n] = keep
            stacked.append(send.reshape(N_CHIPS, keep.shape[0] // N_CHIPS, send.shape[-1]))
        return _join(_grad_copies(stacked), self.swap(swap))

    def swap(self, names):
        return _sibling_copies([self.sums[n] for n in names]) if names else None

    def presum_begin(self, name, pieces):
        keep = jnp.concatenate([p[0] for p in pieces], axis=0) if len(pieces) > 1 else pieces[0][0]
        send = jnp.concatenate([p[1] for p in pieces], axis=0) if len(pieces) > 1 else pieces[0][1]
        rows = keep.shape[0] // N_CHIPS
        self.held = keep.reshape(N_CHIPS, rows, keep.shape[-1])
        return _half_rows_copies(send.reshape(N_CHIPS, rows, send.shape[-1]))

    def presum_end(self, name, got):
        x, y, c = _place()
        to_send, self.own[name] = _pre_sum("presum_" + name, self.held, got[0], jnp.stack([c, self.chip]))
        return _grad_copies([to_send])

    def received(self, names, swap, got, carried=None):
        self.theirs.update(zip(names, got[:len(names)]))
        self.other.update(zip(swap, got[len(names):]))
        for n in names:
            (self.sums[n],), more = _partial_sum("sum_" + n, self.own[n], self.theirs[n], self.chip, comm=carried)
        return more


def _half_rows_copies(stacked):
    n, rows = stacked.shape[0], stacked.shape[1] // 2

    def build(ins, outs, send_sems, recv_sems, local_sems):
        x, y, c = _place()
        copies = [pltpu.make_async_remote_copy(
            src_ref=ins[0].at[s, pl.ds((1 - c) * rows, rows)], dst_ref=outs[0].at[s], send_sem=send_sems.at[s],
            recv_sem=recv_sems.at[s], device_id=(x, y, 1 - c), device_id_type=MESH_ID) for s in range(n)]
        return copies, copies, []

    return _Carried([stacked], [jax.ShapeDtypeStruct((n, rows, stacked.shape[2]), stacked.dtype)], n, 0, build)


def _pre_sum(name, held, theirs, core_and_chip):
    n, R, C = held.shape
    half = R // 2
    tr = _row_tile(half)
    per_half = half // tr

    def body(place_ref, h_ref, t_ref, send_ref, own_ref):
        total = h_ref[0] + t_ref[0].astype(F32)
        send_ref[0] = total.astype(send_ref.dtype)

        @pl.when(pl.program_id(1) == place_ref[1])
        def _():
            own_ref[...] = total

    return pl.pallas_call(
        body, name=name,
        grid_spec=pltpu.PrefetchScalarGridSpec(
            num_scalar_prefetch=1, grid=(per_half, n),
            in_specs=[pl.BlockSpec((1, tr, C), lambda i, s, place: (s, place[0] * per_half + i, 0)),
                      pl.BlockSpec((1, tr, C), lambda i, s, place: (s, i, 0))],
            out_specs=[pl.BlockSpec((1, tr, C), lambda i, s, place: (s, i, 0)),
                       pl.BlockSpec((tr, C), lambda i, s, place: (i, 0))]),
        out_shape=[jax.ShapeDtypeStruct((n, half, C), MXU_DTYPE), jax.ShapeDtypeStruct((half, C), F32)],
        compiler_params=_params(("arbitrary", "arbitrary")),
    )(core_and_chip, held, theirs)


def _sibling_copies(parts):
    n = len(parts)

    def build(ins, outs, send_sems, recv_sems, local_sems):
        x, y, c = _place()
        copies = [pltpu.make_async_remote_copy(
            src_ref=ins[w], dst_ref=outs[w], send_sem=send_sems.at[w], recv_sem=recv_sems.at[w],
            device_id=(x, y, 1 - c), device_id_type=MESH_ID) for w in range(n)]
        return copies, copies, []

    return _Carried(parts, [jax.ShapeDtypeStruct(p.shape, p.dtype) for p in parts], n, 0, build)


def _row_tile(rows, most=512, sublanes=16):
    return max(t for t in range(sublanes, most + 1, sublanes) if rows % t == 0)


def _partial_sum(name, own, recv, chip, comm=None):
    _, R, C = recv.shape
    tr = _row_tile(R, most=128)

    def body(o_ref, r_ref, p_ref):
        p_ref[...] = ((o_ref[...] + r_ref[0].astype(F32)) + r_ref[1].astype(F32)) + r_ref[2].astype(F32)

    if own.shape[0] != R:
        assert comm is None and own.shape[0] == N_CHIPS * R
        total = pl.pallas_call(
            lambda chip_ref, *refs: body(*refs), name=name,
            grid_spec=pltpu.PrefetchScalarGridSpec(
                num_scalar_prefetch=1, grid=(R // tr,),
                in_specs=[pl.BlockSpec((tr, C), lambda i, chip_ref: (chip_ref[0] * (R // tr) + i, 0)),
                          pl.BlockSpec((3, tr, C), lambda i, chip_ref: (0, i, 0))],
                out_specs=pl.BlockSpec((tr, C), lambda i, chip_ref: (i, 0))),
            out_shape=jax.ShapeDtypeStruct((R, C), F32), compiler_params=_params(("parallel",)),
        )(chip.reshape(1), own, recv)
        return [total], []
    return _call(name, body, grid=(R // tr,),
                 in_specs=[pl.BlockSpec((tr, C), lambda i: (i, 0)), pl.BlockSpec((3, tr, C), lambda i: (0, i, 0))],
                 out_specs=[pl.BlockSpec((tr, C), lambda i: (i, 0))], out_shape=[jax.ShapeDtypeStruct((R, C), F32)],
                 args=[own, recv], semantics=("parallel",), comm=comm)


def _adam_vals(w, g, m, v):
    m = ADAM_B1 * m + (1.0 - ADAM_B1) * g
    v = ADAM_B2 * v + (1.0 - ADAM_B2) * (g * g)
    m_hat = m / (1.0 - ADAM_B1 ** ADAM_STEP)
    v_hat = v / (1.0 - ADAM_B2 ** ADAM_STEP)
    delta = -ADAM_LR * (m_hat / (jnp.sqrt(v_hat) + ADAM_EPS) + ADAM_WD * w)
    return delta, m, v


def _adamw(name, w, m, v, mine, other, comm=None):
    R, C = w.shape
    tr = _row_tile(R)

    def body(w_ref, m_ref, v_ref, s_ref, n_ref, g_ref, d_ref, nm_ref, nv_ref):
        g = s_ref[...] + n_ref[...]
        d, nm, nv = _adam_vals(w_ref[...], g, m_ref[...], v_ref[...])
        g_ref[...], d_ref[...], nm_ref[...], nv_ref[...] = g, d, nm, nv

    spec = pl.BlockSpec((tr, C), lambda i: (i, 0))
    return _call(name, body, grid=(R // tr,), in_specs=[spec] * 5, out_specs=[spec] * 4,
                 out_shape=[jax.ShapeDtypeStruct((R, C), F32)] * 4, args=[w, m, v, mine, other],
                 semantics=("parallel",), comm=comm)


def _adamw_by_halves(name, w, m, v, mine, other, core):
    R, C = w.shape
    tr = _row_tile(R // 2)
    per_half = R // 2 // tr

    def body(c_ref, w_ref, m_ref, v_ref, s_ref, n_ref, g_ref, d_ref, nm_ref, nv_ref):
        g = jnp.where(pl.program_id(0) // per_half == c_ref[0, 0], s_ref[...], n_ref[...])
        d, nm, nv = _adam_vals(w_ref[...], g, m_ref[...], v_ref[...])
        g_ref[...], d_ref[...], nm_ref[...], nv_ref[...] = g, d, nm, nv

    spec = pl.BlockSpec((tr, C), lambda i: (i, 0))
    part = pl.BlockSpec((tr, C), lambda i: (i % per_half, 0))
    return pl.pallas_call(
        body, name=name, grid=(R // tr,),
        in_specs=[pl.BlockSpec(memory_space=pltpu.SMEM), spec, spec, spec, part, part], out_specs=[spec] * 4,
        out_shape=[jax.ShapeDtypeStruct((R, C), F32)] * 4, compiler_params=_params(("parallel",)),
    )(core, w, m, v, mine, other)


SMALL_LAYOUT = dict(norm_mix_g=(0, 1, 1024), b_in=(1, 8, 7424), hgrn_norm_g=(9, 1, 1024), norm_ffn_g=(10, 1, 1024),
                    norm_final_g=(11, 1, 1024), hgrn_lb_logits=(12, 2, 2048), attn_sinks=(14, 1, 16))
SMALL_LOSS_ROW, SMALL_ROWS = 15, 16


def _pack_small(grads, loss):
    rows = [jnp.pad(grads[name].astype(F32).reshape(-1), (0, nrows * D_MODEL - n))
            for name, (_, nrows, n) in SMALL_LAYOUT.items()]
    rows.append(jnp.pad(loss.astype(F32).reshape(1), (0, D_MODEL - 1)))
    return jnp.concatenate(rows).reshape(SMALL_ROWS, D_MODEL)


def _adamw_small(w, m, v, g_all):
    names = list(SMALL_LAYOUT)
    n = len(names)

    def body(a_ref, *refs):
        ins, outs = refs[:3 * n], refs[3 * n:]
        g_all_rows = a_ref[0]
        for dev in range(1, 8):
            g_all_rows = g_all_rows + a_ref[dev]
        for i, name in enumerate(names):
            first, nrows, count = SMALL_LAYOUT[name]
            w_ref, m_ref, v_ref = ins[3 * i:3 * i + 3]
            if w_ref.shape[0] == nrows:
                g = g_all_rows[first:first + nrows, :w_ref.shape[1]]
            else:
                last = count - (nrows - 1) * D_MODEL
                g = jnp.concatenate([g_all_rows[r:r + 1, :] for r in range(first, first + nrows - 1)]
                                    + [g_all_rows[first + nrows - 1:first + nrows, :last]], axis=1)
            d, nm, nv = _adam_vals(w_ref[...], g, m_ref[...], v_ref[...])
            for o_ref, val in zip(outs[4 * i:4 * i + 4], (g, d, nm, nv)):
                o_ref[...] = val
        outs[4 * n][...] = g_all_rows[SMALL_LOSS_ROW:SMALL_LOSS_ROW + 1, 0:1]

    res = pl.pallas_call(
        body, name="adamw_small",
        out_shape=[jax.ShapeDtypeStruct(w[name].shape, F32) for name in names for _ in range(4)]
        + [jax.ShapeDtypeStruct((1, 1), F32)],
    )(g_all, *[t[name] for name in names for t in (w, m, v)])
    return {name: res[4 * i:4 * i + 4] for i, name in enumerate(names)}, res[4 * n]


MATRICES = ("w_in", "w_branch_attn", "w_branch_hgrn", "w_out", "w_ffn_gate", "w_ffn_up", "w_ffn_down")
COLUMN_SHARDED = ("w_in", "w_ffn_gate", "w_ffn_up")
WEIGHTS = ("norm_mix_g", "w_in", "b_in", "attn_sinks", "hgrn_lb_logits", "hgrn_norm_g", "w_branch_attn",
           "w_branch_hgrn", "w_out", "norm_ffn_g", "w_ffn_gate", "w_ffn_up", "w_ffn_down", "norm_final_g")


def kernel(x, norm_mix_g, w_in, b_in, attn_sinks, hgrn_lb_logits, hgrn_norm_g, w_branch_attn, w_branch_hgrn, w_out, norm_ffn_g, w_ffn_gate, w_ffn_up, w_ffn_down, norm_final_g, loss_target, m_norm_mix_g, m_w_in, m_b_in, m_attn_sinks, m_hgrn_lb_logits, m_hgrn_norm_g, m_w_branch_attn, m_w_branch_hgrn, m_w_out, m_norm_ffn_g, m_w_ffn_gate, m_w_ffn_up, m_w_ffn_down, m_norm_final_g, v_norm_mix_g, v_w_in, v_b_in, v_attn_sinks, v_hgrn_lb_logits, v_hgrn_norm_g, v_w_branch_attn, v_w_branch_hgrn, v_w_out, v_norm_ffn_g, v_w_ffn_gate, v_w_ffn_up, v_w_ffn_down, v_norm_final_g):
    given = dict(locals())
    w = {n: given[n] for n in WEIGHTS}
    m = {n: given["m_" + n] for n in WEIGHTS}
    v = {n: given["v_" + n] for n in WEIGHTS}

    block = lambda a, n: jnp.transpose(a[0]) if n in COLUMN_SHARDED else a[0]
    unblock = lambda a, n: (jnp.transpose(a) if n in COLUMN_SHARDED else a)[None]
    net = _Net({n: block(w[n], n).astype(MXU_DTYPE) for n in MATRICES})
    net.gathered(("w_in",), [_gather_by_neighbours("gather_w_in", net.shards["w_in"])])
    vec = dict(norm_mix_g=norm_mix_g, b_in=b_in, attn_sinks=attn_sinks, hgrn_lb_logits=hgrn_lb_logits,
               hgrn_norm_g=hgrn_norm_g, norm_ffn_g=norm_ffn_g, norm_final_g=norm_final_g.reshape(1, D_MODEL))
    loss_part, dx, d_vecs = _local_step(x[0], loss_target[0], vec, net)

    small_all, = net.received(*net.last, carried=_small_copies(_pack_small(d_vecs, loss_part)))
    grads, deltas, new_m, new_v = {}, {}, {}, {}
    for n in ("w_ffn_down", "w_ffn_gate", "w_ffn_up", "w_out", "w_branch_attn", "w_branch_hgrn"):
        res, got = _adamw("adamw_" + n, block(w[n], n), block(m[n], n), block(v[n], n), net.sums[n], net.other[n],
                          comm=net.swap(("w_in",)) if n == "w_ffn_down" else None)
        if n == "w_ffn_down":
            net.other["w_in"], = got
        grads[n], deltas[n], new_m[n], new_v[n] = (unblock(r, n) for r in res)
    n = "w_in"
    res = _adamw_by_halves("adamw_" + n, block(w[n], n), block(m[n], n), block(v[n], n), net.sums[n], net.other[n],
                           _place()[2].reshape(1, 1))
    grads[n], deltas[n], new_m[n], new_v[n] = (unblock(r, n) for r in res)
    rows = lambda t: {n: t[n].reshape(-1, t[n].shape[-1]) for n in SMALL_LAYOUT}
    res, loss = _adamw_small(rows(w), rows(m), rows(v), small_all)
    for n, four in res.items():
        grads[n], deltas[n], new_m[n], new_v[n] = (r.reshape(w[n].shape) for r in four)
    loss = loss.reshape(())
    return (loss, dx[None], *[grads[n] for n in WEIGHTS], *[deltas[n] for n in WEIGHTS],
            *[new_m[n] for n in WEIGHTS], *[new_v[n] for n in WEIGHTS])
```

```python
import collections
import functools
import math

import jax
import jax.numpy as jnp
from jax import lax
from jax.experimental import pallas as pl
from jax.experimental.pallas import tpu as pltpu

F32 = jnp.float32
BF16 = jnp.bfloat16
MXU_DTYPE = jnp.bfloat16
SAVED_DTYPE = jnp.bfloat16
MESH_ID = pl.DeviceIdType.MESH

D_MODEL = 1024
HEAD_DIM = 64
Q_HEADS = 16
KV_HEADS = 2
GROUP = Q_HEADS // KV_HEADS
KV_WIDTH = KV_HEADS * HEAD_DIM
ATTN_BLOCK = 128
HGRN_HEADS = 8
HGRN_K = 128
CHUNK = 64
HGRN_TOKENS = 256
FFN = 2816
IN_SPLITS = (1024, 256, 4096, 2048)
EPS = 1e-6
NEG_INF = -1e30
ADAM_LR, ADAM_B1, ADAM_B2, ADAM_EPS, ADAM_WD, ADAM_STEP = 0.001, 0.9, 0.999, 1e-08, 0.01, 10
N_CHIPS = 4
VMEM_LIMIT = 60 * 1024 * 1024
ROW_ALIGN = 16


def _params(sem=None):
    return pltpu.CompilerParams(dimension_semantics=sem, vmem_limit_bytes=VMEM_LIMIT)


def _sigmoid(v):
    return 0.5 * jnp.tanh(0.5 * v) + 0.5


def _dot(a, b, dims):
    return lax.dot_general(a.astype(MXU_DTYPE), b.astype(MXU_DTYPE), (dims, ((), ())),
                           preferred_element_type=F32)


def _nn(a, b):
    return _dot(a, b, ((1,), (0,)))


def _nt(a, b):
    return _dot(a, b, ((1,), (1,)))


def _tn(a, b):
    return _dot(a, b, ((0,), (0,)))


HBM_SPEC = pl.BlockSpec(memory_space=pl.ANY)


class _Carried:
    def __init__(self, arrays, out_shapes, n_remote, n_local, build):
        self.parts = [(len(arrays), len(out_shapes), build)]
        self.arrays, self.out_shapes = list(arrays), list(out_shapes)
        self.scratch = [pltpu.SemaphoreType.DMA((n_remote,)), pltpu.SemaphoreType.DMA((n_remote,)),
                        pltpu.SemaphoreType.DMA((max(n_local, 1),))]

    def __add__(self, other):
        both = _Carried([], [], 1, 0, None)
        both.parts = self.parts + other.parts
        both.arrays, both.out_shapes = self.arrays + other.arrays, self.out_shapes + other.out_shapes
        both.scratch = self.scratch + other.scratch
        return both

    def _built(self, ins, outs, sems):
        for p, (ni, no, build) in enumerate(self.parts):
            yield build(ins[:ni], outs[:no], *sems[3 * p:3 * p + 3])
            ins, outs = ins[ni:], outs[no:]

    def start(self, ins, outs, sems):
        core = lax.axis_index("c")
        for sends, _, local, *other_order in self._built(ins, outs, sems):
            for cp in local:
                cp.start()
            if not other_order:
                for cp in sends:
                    cp.start()
                continue

            @pl.when(core == 0)
            def _():
                for cp in sends:
                    cp.start()

            @pl.when(core == 1)
            def _():
                for cp in other_order[0]:
                    cp.start()

    def wait(self, ins, outs, sems):
        for sends, recvs, local, *_ in self._built(ins, outs, sems):
            for cp in recvs:
                cp.wait_recv()
            for cp in sends:
                cp.wait_send()
            for cp in local:
                cp.wait()


def _join(*comms):
    comms = [c for c in comms if c is not None]
    return functools.reduce(lambda a, b: a + b, comms) if comms else None


def _call(name, body, *, grid, in_specs, out_specs, out_shape, args, scratch=(), semantics=None, comm=None,
          aliases=None):
    n_in, n_out, n_scr = len(in_specs), len(out_specs), len(scratch)
    aliases = aliases or {}
    if comm is None:
        res = pl.pallas_call(body, name=name, grid=grid, in_specs=in_specs, out_specs=out_specs, out_shape=out_shape,
                             scratch_shapes=list(scratch), input_output_aliases=aliases,
                             compiler_params=_params(semantics))(*args)
        return list(res), []
    ci, co = len(comm.arrays), len(comm.out_shapes)

    def carrying(*refs):
        ins, refs = refs[:n_in], refs[n_in:]
        c_ins, refs = refs[:ci], refs[ci:]
        outs, refs = refs[:n_out], refs[n_out:]
        c_outs, refs = refs[:co], refs[co:]
        scr, sems = refs[:n_scr], refs[n_scr:]
        if not grid:
            comm.start(c_ins, c_outs, sems)
            body(*ins, *outs, *scr)
            comm.wait(c_ins, c_outs, sems)
            return
        first = functools.reduce(jnp.logical_and, [pl.program_id(a) == 0 for a in range(len(grid))])
        last = functools.reduce(jnp.logical_and, [pl.program_id(a) == g - 1 for a, g in enumerate(grid)])

        @pl.when(first)
        def _():
            comm.start(c_ins, c_outs, sems)

        body(*ins, *outs, *scr)

        @pl.when(last)
        def _():
            comm.wait(c_ins, c_outs, sems)

    res = pl.pallas_call(
        carrying, name=name, grid=grid, in_specs=list(in_specs) + [HBM_SPEC] * ci,
        out_specs=list(out_specs) + [HBM_SPEC] * co, out_shape=list(out_shape) + comm.out_shapes,
        scratch_shapes=list(scratch) + comm.scratch, input_output_aliases=aliases,
        compiler_params=_params(("arbitrary",) * len(grid) if grid else None),
    )(*args, *comm.arrays)
    return list(res[:n_out]), list(res[n_out:])


_NO_COPIES = object()
_Rows = collections.namedtuple("_Rows", "array first rows")


_Into = collections.namedtuple("_Into", "rows first held")


def _matmul(name, pairs, mode, *, tm, tn, tk, outs, extras=(), epilogue=None, a_colsum=False, into=None,
            comm=_NO_COPIES):
    prod = dict(nn=_nn, nt=_nt, tn=_tn)[mode]
    pairs = [(a, b if isinstance(b, _Rows) else _Rows(b, 0, b.shape[0])) for a, b in pairs]
    a0, b0 = pairs[0]
    M = a0.shape[1] if mode == "tn" else a0.shape[0]
    N = b0.rows if mode == "nt" else b0.array.shape[1]
    steps, in_specs, offset = [], [], 0
    for a, b in pairs:
        K = a.shape[0] if mode == "tn" else a.shape[1]
        t = min(tk, K)
        assert K % t == 0, (name, K, t)
        kmap = functools.partial(lambda k, off, n: jnp.clip(k - off, 0, n - 1), off=offset, n=K // t)
        if mode == "tn":
            in_specs.append(pl.BlockSpec((t, tm), functools.partial(lambda i, j, k, f: (f(k), i), f=kmap)))
        else:
            in_specs.append(pl.BlockSpec((tm, t), functools.partial(lambda i, j, k, f: (i, f(k)), f=kmap)))
        whole = b.first == 0 and b.rows == b.array.shape[0]
        if mode == "nt":
            shape = (tn, t) if whole else (pl.Element(tn), pl.Element(t))
            in_specs.append(pl.BlockSpec(shape, functools.partial(
                lambda i, j, k, f, b, t, whole: (j, f(k)) if whole else (
                    pl.multiple_of(b.first + j * tn, ROW_ALIGN), pl.multiple_of(f(k) * t, 128)),
                f=kmap, b=b, t=t, whole=whole)))
        else:
            assert b.rows == K, (name, b.rows, K)
            shape = (t, tn) if whole else (pl.Element(t), pl.Element(tn))
            in_specs.append(pl.BlockSpec(shape, functools.partial(
                lambda i, j, k, f, b, t, whole: (f(k), j) if whole else (
                    pl.multiple_of(b.first + f(k) * t, ROW_ALIGN), pl.multiple_of(j * tn, 128)),
                f=kmap, b=b, t=t, whole=whole)))
        steps.append((offset, offset + K // t))
        offset += K // t
    assert M % tm == 0 and N % tn == 0, (name, M, N, tm, tn)
    ni, nj, nk = M // tm, N // tn, offset
    npair, ne, no = len(pairs), len(extras), len(outs)
    if epilogue is None:
        epilogue = lambda acc: (acc,)

    def finish(acc, extra_refs, out_refs):
        vals = epilogue(acc, *[r[...] for r in extra_refs])
        for (kind, _), o_ref, val in zip(outs, out_refs, vals):
            if kind == "pn":
                val = jnp.broadcast_to(val, o_ref.shape)
            o_ref[...] = val.astype(o_ref.dtype)

    held = list(into.held) if into is not None and into.held is not None else []

    def body(*refs):
        ab, rest = refs[:2 * npair], refs[2 * npair:]
        extra_refs, rest = rest[:ne], rest[ne + len(held):]
        out_refs = rest[:no]
        if nk == 1:
            finish(prod(ab[0][...], ab[1][...]), extra_refs, out_refs)
            return
        sums_ref, acc_ref = rest[no], rest[-1]
        k = pl.program_id(2)

        @pl.when(k == 0)
        def _():
            acc_ref[...] = jnp.zeros_like(acc_ref)
            if a_colsum:
                sums_ref[...] = jnp.zeros((1, tm), F32)

        if a_colsum:
            sums_ref[...] += jnp.sum(ab[0][...].astype(F32), axis=0, keepdims=True)
        for p, (lo, hi) in enumerate(steps):
            @pl.when(jnp.logical_and(k >= lo, k < hi))
            def _():
                acc_ref[...] += prod(ab[2 * p][...], ab[2 * p + 1][...])

        @pl.when(k == nk - 1)
        def _():
            finish(acc_ref[...], extra_refs, out_refs)

    for _, shape, im in extras:
        in_specs.append(pl.BlockSpec(shape, functools.partial(lambda i, j, k, im: im(i, j), im=im)))
    in_specs += [HBM_SPEC] * len(held)
    aliases = {2 * npair + ne + p: p for p in range(len(held))}
    out_shape, out_specs = [], []
    for kind, dt in outs:
        if kind == "mn" and into is not None:
            out_shape.append(jax.ShapeDtypeStruct((into.rows, N), dt))
            out_specs.append(pl.BlockSpec((pl.Element(tm), pl.Element(tn)), lambda i, j, k: (
                pl.multiple_of(into.first + i * tm, ROW_ALIGN), pl.multiple_of(j * tn, 128))))
        elif kind == "mn":
            out_shape.append(jax.ShapeDtypeStruct((M, N), dt))
            out_specs.append(pl.BlockSpec((tm, tn), lambda i, j, k: (i, j)))
        else:
            out_shape.append(jax.ShapeDtypeStruct((8 * ni, N), dt))
            out_specs.append(pl.BlockSpec((8, tn), lambda i, j, k: (i, j)))
    if a_colsum:
        assert mode == "tn" and npair == 1 and nj == 1 and nk > 1, name
        out_shape.append(jax.ShapeDtypeStruct((1, M), F32))
        out_specs.append(pl.BlockSpec((1, tm), lambda i, j, k: (0, i)))
    grid = (ni, nj, nk)
    if nj > 1 and nk == 1:
        turned = lambda spec: pl.BlockSpec(spec.block_shape, functools.partial(
            lambda j, i, k, im: im(i, j, k), im=spec.index_map))
        in_specs, out_specs, grid = [turned(s) for s in in_specs], [turned(s) for s in out_specs], (nj, ni, nk)
    res, got = _call(name, body, grid=grid, in_specs=in_specs, out_specs=out_specs, out_shape=out_shape,
                     args=[t for a, b in pairs for t in (a, b.array)] + [e[0] for e in extras] + held,
                     scratch=[pltpu.VMEM((tm, tn), F32)] if nk > 1 else [], aliases=aliases,
                     semantics=("parallel", "parallel", "arbitrary"), comm=None if comm is _NO_COPIES else comm)
    return res if comm is _NO_COPIES else (res, got)


def _swiglu_fwd(u, w_gate_t, w_up_t, *, tm, tn, comm=None):
    (T, D), F = u.shape, w_gate_t.shape[0]

    def body(u_ref, wg_ref, wu_ref, g_ref, up_ref, z_ref):
        g, up = _nt(u_ref[...], wg_ref[...]), _nt(u_ref[...], wu_ref[...])
        g_ref[...], up_ref[...] = g.astype(g_ref.dtype), up.astype(up_ref.dtype)
        z_ref[...] = (g * _sigmoid(g) * up).astype(z_ref.dtype)

    w_spec = pl.BlockSpec((tn, D), lambda j, i: (j, 0))
    o_spec = pl.BlockSpec((tm, tn), lambda j, i: (i, j))
    return _call("ffn_hidden", body, grid=(F // tn, T // tm),
                 in_specs=[pl.BlockSpec((tm, D), lambda j, i: (i, 0)), w_spec, w_spec], out_specs=[o_spec] * 3,
                 out_shape=[jax.ShapeDtypeStruct((T, F), SAVED_DTYPE)] * 2 + [jax.ShapeDtypeStruct((T, F), MXU_DTYPE)],
                 args=[u, w_gate_t, w_up_t], semantics=("parallel", "parallel"), comm=comm)


def _merge_fwd(y_a, y_b, w_a, w_b, gates, *, tm):
    T, D = y_a.shape

    def body(ya_ref, yb_ref, wa_ref, wb_ref, ga_ref, gb_ref, pa_ref, pb_ref, m_ref):
        pa, pb = _nn(ya_ref[...], wa_ref[...]), _nn(yb_ref[...], wb_ref[...])
        pa_ref[...], pb_ref[...] = pa, pb
        m_ref[...] = (_sigmoid(ga_ref[...]) * pa + _sigmoid(gb_ref[...]) * pb).astype(m_ref.dtype)

    rows = pl.BlockSpec((tm, D), lambda i: (i, 0))
    whole = pl.BlockSpec((D, D), lambda i: (0, 0))
    return pl.pallas_call(
        body, name="branch_merge", grid=(T // tm,),
        in_specs=[rows, rows, whole, whole, rows, pl.BlockSpec((tm, D), lambda i: (i, 1))], out_specs=[rows] * 3,
        out_shape=[jax.ShapeDtypeStruct((T, D), F32)] * 2 + [jax.ShapeDtypeStruct((T, D), MXU_DTYPE)],
        compiler_params=_params(("parallel",)),
    )(y_a, y_b, w_a, w_b, gates, gates)


def _merge_bwd(dh, w_out, w_a, w_b, p_a, p_b, gates, *, tm):
    T, D = dh.shape

    def body(dh_ref, wo_ref, wa_ref, wb_ref, pa_ref, pb_ref, ga_ref, gb_ref, dpa_ref, dpb_ref, dga_ref, dgb_ref,
             dya_ref, dyb_ref):
        dm = _nt(dh_ref[...], wo_ref[...])
        sa, sb = _sigmoid(ga_ref[...]), _sigmoid(gb_ref[...])
        dpa, dpb = (dm * sa).astype(dpa_ref.dtype), (dm * sb).astype(dpb_ref.dtype)
        dpa_ref[...], dpb_ref[...] = dpa, dpb
        dga_ref[...] = (dm * pa_ref[...] * sa * (1.0 - sa)).astype(dga_ref.dtype)
        dgb_ref[...] = (dm * pb_ref[...] * sb * (1.0 - sb)).astype(dgb_ref.dtype)
        dya_ref[...] = _nt(dpa, wa_ref[...]).astype(dya_ref.dtype)
        dyb_ref[...] = _nt(dpb, wb_ref[...]).astype(dyb_ref.dtype)

    rows = pl.BlockSpec((tm, D), lambda i: (i, 0))
    whole = pl.BlockSpec((D, D), lambda i: (0, 0))
    low = jax.ShapeDtypeStruct((T, D), MXU_DTYPE)
    return pl.pallas_call(
        body, name="d_branch_merge", grid=(T // tm,),
        in_specs=[rows, whole, whole, whole, rows, rows, rows, pl.BlockSpec((tm, D), lambda i: (i, 1))],
        out_specs=[rows] * 6, out_shape=[low] * 5 + [jax.ShapeDtypeStruct((T, D), F32)],
        compiler_params=_params(("parallel",)),
    )(dh, w_out, w_a, w_b, p_a, p_b, gates, gates)


def _colsum_partials(p):
    return jnp.sum(p.reshape(-1, 8, p.shape[-1])[:, 0, :], axis=0, keepdims=True)


def _rmsnorm_fwd(name, x, g, tr=512):
    T, D = x.shape

    def body(x_ref, g_ref, u_ref):
        xv = x_ref[...]
        r = lax.rsqrt(jnp.mean(xv * xv, axis=-1, keepdims=True) + EPS)
        u_ref[...] = (xv * r * g_ref[...]).astype(u_ref.dtype)

    return pl.pallas_call(
        body, name=name, grid=(T // tr,),
        in_specs=[pl.BlockSpec((tr, D), lambda i: (i, 0)), pl.BlockSpec((1, D), lambda i: (0, 0))],
        out_specs=pl.BlockSpec((tr, D), lambda i: (i, 0)),
        out_shape=jax.ShapeDtypeStruct((T, D), MXU_DTYPE),
        compiler_params=_params(("parallel",)),
    )(x, g)


def _rmsnorm_bwd_vals(dy, xin, g):
    rstd = lax.rsqrt(jnp.mean(xin * xin, axis=-1, keepdims=True) + EPS)
    xhat = xin * rstd
    dg = jnp.sum(dy * xhat, axis=0, keepdims=True)
    dxh = dy * g
    dx = rstd * (dxh - xhat * jnp.mean(dxh * xhat, axis=-1, keepdims=True))
    return dx, dg


def _colsum(name, a, tr=512, comm=_NO_COPIES):
    T, N = a.shape

    def body(a_ref, o_ref):
        @pl.when(pl.program_id(0) == 0)
        def _():
            o_ref[...] = jnp.zeros_like(o_ref)

        o_ref[...] += jnp.sum(a_ref[...].astype(F32), axis=0, keepdims=True)

    (res,), got = _call(name, body, grid=(T // tr,), in_specs=[pl.BlockSpec((tr, N), lambda i: (i, 0))],
                        out_specs=[pl.BlockSpec((1, N), lambda i: (0, 0))],
                        out_shape=[jax.ShapeDtypeStruct((1, N), F32)], args=[a], semantics=("arbitrary",),
                        comm=None if comm is _NO_COPIES else comm)
    return res if comm is _NO_COPIES else (res, got)


ATTN_SCALE = 1.0 / math.sqrt(HEAD_DIM)
GROUP_LANES = GROUP * ATTN_BLOCK
PAIR = 2 * HEAD_DIM


def _attn_mask():
    kj = lax.broadcasted_iota(jnp.int32, (ATTN_BLOCK, GROUP_LANES), 0)
    qi = lax.broadcasted_iota(jnp.int32, (ATTN_BLOCK, GROUP_LANES), 1) & (ATTN_BLOCK - 1)
    return kj <= qi


def _heads_transposed(ref, g, scale=None):
    parts = []
    for a in range(GROUP // 2):
        lo = (g * GROUP // 2 + a) * PAIR
        pair = ref[:, lo:lo + PAIR].astype(F32)
        pair = (pair if scale is None else pair * scale).T
        parts += [pair[:HEAD_DIM], pair[HEAD_DIM:]]
    return jnp.concatenate(parts, axis=1).astype(MXU_DTYPE)


def _heads_back(ref, g, vt):
    for a in range(GROUP // 2):
        lo = (g * GROUP // 2 + a) * PAIR
        pair = jnp.concatenate([vt[:, (2 * a) * ATTN_BLOCK:(2 * a + 1) * ATTN_BLOCK],
                                vt[:, (2 * a + 1) * ATTN_BLOCK:(2 * a + 2) * ATTN_BLOCK]], axis=0)
        ref[:, lo:lo + PAIR] = pair.T.astype(ref.dtype)


def _kv_parts(kv_ref, g):
    ks = slice(g * HEAD_DIM, (g + 1) * HEAD_DIM)
    vs = slice(KV_WIDTH + g * HEAD_DIM, KV_WIDTH + (g + 1) * HEAD_DIM)
    return kv_ref[:, ks].astype(MXU_DTYPE), kv_ref[:, vs].astype(MXU_DTYPE)


def _sink_rows(sinks):
    return jnp.repeat(sinks.reshape(KV_HEADS, GROUP), ATTN_BLOCK, axis=1)


def _attn_fwd(pq, pkv, sinks, comm=None):
    T = pq.shape[0]
    nb = T // ATTN_BLOCK

    def body(q_ref, kvc_ref, kvp_ref, s_ref, y_ref, lse_ref):
        mask_c = _attn_mask()
        has_prev = pl.program_id(0) > 0
        for g in range(KV_HEADS):
            (kc, vc), (kp, vp) = _kv_parts(kvc_ref, g), _kv_parts(kvp_ref, g)
            qt = _heads_transposed(q_ref, g, ATTN_SCALE)
            s = jnp.where(mask_c, _nn(kc, qt), jnp.where(has_prev, _nn(kp, qt), NEG_INF))
            sink = s_ref[g:g + 1, :]
            m = jnp.maximum(jnp.max(s, axis=0, keepdims=True), sink)
            p = jnp.exp(s - m)
            den = jnp.sum(p, axis=0, keepdims=True) + jnp.exp(sink - m)
            pc = jnp.where(mask_c, p, 0.0)
            _heads_back(y_ref, g, (_tn(vc, pc) + _tn(vp, p - pc)) / den)
            lse = m + jnp.log(den)
            for i in range(GROUP):
                lse_ref[g * GROUP + i:g * GROUP + i + 1, :] = lse[:, i * ATTN_BLOCK:(i + 1) * ATTN_BLOCK]

    return _call(
        "attn_fwd", body, grid=(nb,),
        in_specs=[pl.BlockSpec((ATTN_BLOCK, D_MODEL), lambda n: (n, 0)),
                  pl.BlockSpec((ATTN_BLOCK, 2 * KV_WIDTH), lambda n: (n, 0)),
                  pl.BlockSpec((ATTN_BLOCK, 2 * KV_WIDTH), lambda n: (jnp.maximum(n - 1, 0), 0)),
                  pl.BlockSpec((KV_HEADS, GROUP_LANES), lambda n: (0, 0))],
        out_specs=[pl.BlockSpec((ATTN_BLOCK, D_MODEL), lambda n: (n, 0)),
                   pl.BlockSpec((Q_HEADS, ATTN_BLOCK), lambda n: (0, n))],
        out_shape=[jax.ShapeDtypeStruct((T, D_MODEL), MXU_DTYPE), jax.ShapeDtypeStruct((Q_HEADS, T), F32)],
        args=[pq, pkv, pkv, _sink_rows(sinks)], semantics=("parallel",), comm=comm)


def _attn_bwd(pq, pkv, sinks, lse, dy, comm=None):
    T = pq.shape[0]
    nb = T // ATTN_BLOCK
    cur = lambda n: (jnp.minimum(n, nb - 1), 0)

    def body(q_ref, kvc_ref, kvp_ref, s_ref, lse_ref, dy_ref, dq_ref, dkv_ref, ds_ref, carry, top, bot):
        n = pl.program_id(0)

        @pl.when(n == 0)
        def _():
            carry[...] = jnp.zeros_like(carry)
            ds_ref[...] = jnp.zeros_like(ds_ref)

        @pl.when(n < nb)
        def _():
            mask_c = _attn_mask()
            valid = jnp.logical_or(mask_c, n > 0)
            for g in range(KV_HEADS):
                ks = slice(g * HEAD_DIM, (g + 1) * HEAD_DIM)
                vs = slice(KV_WIDTH + g * HEAD_DIM, KV_WIDTH + (g + 1) * HEAD_DIM)
                (kc, vc), (kp, vp) = _kv_parts(kvc_ref, g), _kv_parts(kvp_ref, g)
                qt = _heads_transposed(q_ref, g, ATTN_SCALE)
                dot = _heads_transposed(dy_ref, g)
                lse = jnp.concatenate([lse_ref[g * GROUP + i:g * GROUP + i + 1, :] for i in range(GROUP)], axis=1)
                p = jnp.where(valid, jnp.exp(jnp.where(mask_c, _nn(kc, qt), _nn(kp, qt)) - lse), 0.0)
                dp = jnp.where(mask_c, _nn(vc, dot), _nn(vp, dot))
                delta = jnp.sum(p * dp, axis=0, keepdims=True)
                ds = p * (dp - delta)
                ds_c, p_c = jnp.where(mask_c, ds, 0.0), jnp.where(mask_c, p, 0.0)
                ds_p, p_p = ds - ds_c, p - p_c
                _heads_back(dq_ref, g, (_tn(kc, ds_c) + _tn(kp, ds_p)) * ATTN_SCALE)
                bot[:, ks], bot[:, vs] = _nt(ds_c, qt), _nt(p_c, dot)
                top[:, ks], top[:, vs] = _nt(ds_p, qt), _nt(p_p, dot)
                ds_ref[g:g + 1, :] -= jnp.exp(s_ref[g:g + 1, :] - lse) * delta
            dkv_ref[...] = (carry[...] + top[...]).astype(dkv_ref.dtype)
            carry[...] = bot[...]

        @pl.when(n == nb)
        def _():
            dkv_ref[...] = carry[...].astype(dkv_ref.dtype)

    return _call(
        "attn_bwd", body, grid=(nb + 1,),
        in_specs=[pl.BlockSpec((ATTN_BLOCK, D_MODEL), cur),
                  pl.BlockSpec((ATTN_BLOCK, 2 * KV_WIDTH), cur),
                  pl.BlockSpec((ATTN_BLOCK, 2 * KV_WIDTH), lambda n: (jnp.maximum(jnp.minimum(n, nb - 1) - 1, 0), 0)),
                  pl.BlockSpec((KV_HEADS, GROUP_LANES), lambda n: (0, 0)),
                  pl.BlockSpec((Q_HEADS, ATTN_BLOCK), lambda n: (0, jnp.minimum(n, nb - 1))),
                  pl.BlockSpec((ATTN_BLOCK, D_MODEL), cur)],
        out_specs=[pl.BlockSpec((ATTN_BLOCK, D_MODEL), cur),
                   pl.BlockSpec((ATTN_BLOCK, 2 * KV_WIDTH), lambda n: (jnp.maximum(n - 1, 0), 0)),
                   pl.BlockSpec((KV_HEADS, GROUP_LANES), lambda n: (0, 0))],
        out_shape=[jax.ShapeDtypeStruct((T, D_MODEL), MXU_DTYPE),
                   jax.ShapeDtypeStruct((T, 2 * KV_WIDTH), MXU_DTYPE),
                   jax.ShapeDtypeStruct((KV_HEADS, GROUP_LANES), F32)],
        scratch=[pltpu.VMEM((ATTN_BLOCK, 2 * KV_WIDTH), F32)] * 3,
        args=[pq, pkv, pkv, _sink_rows(sinks), lse, dy], semantics=("arbitrary",), comm=comm)


def _lower_bound(l):
    m = jnp.maximum(l[0:1], l[1:2])
    e0, e1 = jnp.exp(l[0:1] - m), jnp.exp(l[1:2] - m)
    return e0 / (e0 + e1)


def _tri(lower):
    r = lax.broadcasted_iota(jnp.int32, (CHUNK, CHUNK), 0)
    c = lax.broadcasted_iota(jnp.int32, (CHUNK, CHUNK), 1)
    return (r >= c) if lower else (c >= r)


def _chunk_sum(mask, v):
    ones = mask.astype(BF16)
    hi = v.astype(BF16)
    rest = v - hi.astype(F32)
    mid = rest.astype(BF16)
    lo = (rest - mid.astype(F32)).astype(BF16)
    part = lambda t: lax.dot_general(ones, t, (((1,), (0,)), ((), ())), preferred_element_type=F32)
    return part(hi) + part(mid) + part(lo)


def _hgrn_chunk_inputs(hq, hf, lb, causal):
    half_t = 0.5 * jnp.tanh(0.5 * hf)
    sg, sgn = 0.5 + half_t, 0.5 - half_t
    f = lb + (1.0 - lb) * sg
    kk = (1.0 - lb) * sgn
    sq = _sigmoid(hq)
    q = hq * sq
    b = _chunk_sum(causal, jnp.log(f))
    bm, bl = b[CHUNK // 2 - 1:CHUNK // 2, :], b[CHUNK - 1:CHUNK, :]
    e_qm, e_km = jnp.exp(b - bm), jnp.exp(bm - b)
    e_qs, e_kl = e_qm * jnp.exp(bm), e_km * jnp.exp(bl - bm)
    return dict(sg=sg, sgn=sgn, f=f, kk=kk, sq=sq, q=q, e_qm=e_qm, e_km=e_km, e_qs=e_qs, e_kl=e_kl,
                qm=q * e_qm, km=kk * e_km, qs=q * e_qs, kl=kk * e_kl, el=jnp.exp(bl))


def _hgrn_fwd(ph, lb_logits, norm_g, comm=None):
    T = ph.shape[0]
    nblk, cpb = T // HGRN_TOKENS, HGRN_TOKENS // CHUNK
    col = lambda c: pl.BlockSpec((HGRN_TOKENS, D_MODEL), functools.partial(lambda i, c: (i, c), c=c))

    def body(hq_ref, hf_ref, hi_ref, hg_ref, l_ref, ng_ref, y_ref, o_ref, st_ref, s_ref):
        @pl.when(pl.program_id(0) == 0)
        def _():
            s_ref[...] = jnp.zeros_like(s_ref)

        lb = _lower_bound(l_ref[...])
        causal = _tri(True)
        for c in range(cpb):
            rows = slice(c * CHUNK, (c + 1) * CHUNK)
            t = _hgrn_chunk_inputs(hq_ref[rows, :], hf_ref[rows, :], lb, causal)
            qm, km, qs, kl = (t[n].astype(MXU_DTYPE) for n in ("qm", "km", "qs", "kl"))
            v = hi_ref[rows, :].astype(MXU_DTYPE)
            for h in range(HGRN_HEADS):
                ls = slice(h * HGRN_K, (h + 1) * HGRN_K)
                st = s_ref[h]
                st_ref[c, ls, :] = st
                a = jnp.where(causal, _nt(qm[:, ls], km[:, ls]), 0.0)
                o_ref[rows, ls] = _nn(a, v[:, ls]) + _nt(qs[:, ls], st)
                s_ref[h] = t["el"][:, ls] * st + _tn(v[:, ls], kl[:, ls])
        for h in range(HGRN_HEADS):
            ls = slice(h * HGRN_K, (h + 1) * HGRN_K)
            o = o_ref[:, ls]
            r = lax.rsqrt(jnp.mean(o * o, axis=-1, keepdims=True) + EPS)
            y_ref[:, ls] = (o * r * ng_ref[:, ls] * _sigmoid(hg_ref[:, ls])).astype(y_ref.dtype)

    return _call(
        "hgrn_fwd", body, grid=(nblk,),
        in_specs=[col(0), col(1), col(2), col(3),
                  pl.BlockSpec((2, D_MODEL), lambda i: (0, 0)), pl.BlockSpec((1, D_MODEL), lambda i: (0, 0))],
        out_specs=[pl.BlockSpec((HGRN_TOKENS, D_MODEL), lambda i: (i, 0)),
                   pl.BlockSpec((HGRN_TOKENS, D_MODEL), lambda i: (i, 0)),
                   pl.BlockSpec((cpb, D_MODEL, HGRN_K), lambda i: (i, 0, 0))],
        out_shape=[jax.ShapeDtypeStruct((T, D_MODEL), MXU_DTYPE), jax.ShapeDtypeStruct((T, D_MODEL), F32),
                   jax.ShapeDtypeStruct((T // CHUNK, D_MODEL, HGRN_K), F32)],
        scratch=[pltpu.VMEM((HGRN_HEADS, HGRN_K, HGRN_K), F32)],
        args=[ph, ph, ph, ph, lb_logits, norm_g], semantics=("arbitrary",), comm=comm)


def _hgrn_bwd(ph, o_raw, states, dy, lb_logits, norm_g, comm=None):
    T = ph.shape[0]
    nblk, cpb = T // HGRN_TOKENS, HGRN_TOKENS // CHUNK
    rev = lambda i: nblk - 1 - i
    col = lambda c: pl.BlockSpec((HGRN_TOKENS, D_MODEL), functools.partial(lambda i, c: (rev(i), c), c=c))
    tok = pl.BlockSpec((HGRN_TOKENS, D_MODEL), lambda i: (rev(i), 0))

    def body(hq_ref, hf_ref, hi_ref, hg_ref, o_ref, st_ref, dy_ref, l_ref, ng_ref,
             dph_ref, dng_ref, dl_ref, dst_ref, dlb_ref, do_s, dqm_s, dkm_s, dqs_s, dkl_s, dv_s, del_s):
        i = pl.program_id(0)

        @pl.when(i == 0)
        def _():
            dst_ref[...] = jnp.zeros_like(dst_ref)
            dlb_ref[...] = jnp.zeros_like(dlb_ref)
            dng_ref[...] = jnp.zeros_like(dng_ref)

        lb = _lower_bound(l_ref[...])
        causal, anti = _tri(True), _tri(False)
        row = lax.broadcasted_iota(jnp.int32, (CHUNK, D_MODEL), 0)
        for c in reversed(range(cpb)):
            rows = slice(c * CHUNK, (c + 1) * CHUNK)
            hq = hq_ref[rows, :]
            t = _hgrn_chunk_inputs(hq, hf_ref[rows, :], lb, causal)
            sgg = _sigmoid(hg_ref[rows, :])
            dyv = dy_ref[rows, :]
            for h in range(HGRN_HEADS):
                ls = slice(h * HGRN_K, (h + 1) * HGRN_K)
                o = o_ref[rows, ls]
                r = lax.rsqrt(jnp.mean(o * o, axis=-1, keepdims=True) + EPS)
                nrm = o * r
                g_h = sgg[:, ls]
                dph_ref[rows, 3 * D_MODEL + h * HGRN_K:3 * D_MODEL + (h + 1) * HGRN_K] = (
                    dyv[:, ls] * nrm * ng_ref[:, ls] * g_h * (1.0 - g_h)).astype(dph_ref.dtype)
                dyg = dyv[:, ls] * g_h
                dng_ref[:, ls] += jnp.sum(dyg * nrm, axis=0, keepdims=True)
                dn = dyg * ng_ref[:, ls]
                do_s[:, ls] = r * (dn - nrm * jnp.mean(dn * nrm, axis=-1, keepdims=True))
            qm, km, qs, kl = (t[n].astype(MXU_DTYPE) for n in ("qm", "km", "qs", "kl"))
            v = hi_ref[rows, :].astype(MXU_DTYPE)
            do = do_s[...].astype(MXU_DTYPE)
            for h in range(HGRN_HEADS):
                ls = slice(h * HGRN_K, (h + 1) * HGRN_K)
                st = st_ref[c, ls, :]
                dst = dst_ref[h]
                a = jnp.where(causal, _nt(qm[:, ls], km[:, ls]), 0.0)
                da = jnp.where(causal, _nt(do[:, ls], v[:, ls]), 0.0)
                dv_s[:, ls] = _tn(a, do[:, ls]) + _nt(kl[:, ls], dst)
                dkl_s[:, ls] = _nn(v[:, ls], dst)
                dqs_s[:, ls] = _nn(do[:, ls], st)
                del_s[:, ls] = jnp.sum(dst * st, axis=0, keepdims=True)
                dst_ref[h] = _tn(do[:, ls], qs[:, ls]) + t["el"][:, ls] * dst
                dqm_s[:, ls] = _nn(da, km[:, ls])
                dkm_s[:, ls] = _tn(da, qm[:, ls])
            dqm, dkm, dqs, dkl = dqm_s[...], dkm_s[...], dqs_s[...], dkl_s[...]
            dq = dqm * t["e_qm"] + dqs * t["e_qs"]
            dk = dkm * t["e_km"] + dkl * t["e_kl"]
            t_qm, t_km, t_kl = dqm * t["qm"], dkm * t["km"], dkl * t["kl"]
            db = t_qm - t_km + dqs * t["qs"] - t_kl
            db_mid = jnp.sum(t_km - t_qm, axis=0, keepdims=True)
            db_last = jnp.sum(t_kl, axis=0, keepdims=True) + del_s[...] * t["el"]
            db = db + jnp.where(row == CHUNK // 2 - 1, db_mid, 0.0) + jnp.where(row == CHUNK - 1, db_last, 0.0)
            dlogf = _chunk_sum(anti, db)
            sq, sg, sgn, f = t["sq"], t["sg"], t["sgn"], t["f"]
            dph_ref[rows, 0:D_MODEL] = (dq * (sq * (1.0 + hq * (1.0 - sq)))).astype(dph_ref.dtype)
            dph_ref[rows, D_MODEL:2 * D_MODEL] = (
                dlogf * (1.0 - lb) * sg * (1.0 - sg) / f - dk * (1.0 - lb) * sgn * (1.0 - sgn)).astype(dph_ref.dtype)
            dph_ref[rows, 2 * D_MODEL:3 * D_MODEL] = dv_s[...].astype(dph_ref.dtype)
            dlb_ref[...] += jnp.sum(dlogf * (1.0 - sg) / f - dk * sgn, axis=0, keepdims=True)

        @pl.when(i == nblk - 1)
        def _():
            dl0 = dlb_ref[...] * lb * (1.0 - lb)
            dl_ref[0:1, :] = dl0
            dl_ref[1:2, :] = -dl0

    wide = pltpu.VMEM((CHUNK, D_MODEL), F32)
    return _call(
        "hgrn_bwd", body, grid=(nblk,),
        in_specs=[col(0), col(1), col(2), col(3), tok,
                  pl.BlockSpec((cpb, D_MODEL, HGRN_K), lambda i: (rev(i), 0, 0)), tok,
                  pl.BlockSpec((2, D_MODEL), lambda i: (0, 0)), pl.BlockSpec((1, D_MODEL), lambda i: (0, 0))],
        out_specs=[pl.BlockSpec((HGRN_TOKENS, 4 * D_MODEL), lambda i: (rev(i), 0)),
                   pl.BlockSpec((1, D_MODEL), lambda i: (0, 0)), pl.BlockSpec((2, D_MODEL), lambda i: (0, 0))],
        out_shape=[jax.ShapeDtypeStruct((T, 4 * D_MODEL), MXU_DTYPE), jax.ShapeDtypeStruct((1, D_MODEL), F32),
                   jax.ShapeDtypeStruct((2, D_MODEL), F32)],
        scratch=[pltpu.VMEM((HGRN_HEADS, HGRN_K, HGRN_K), F32), pltpu.VMEM((1, D_MODEL), F32),
                 wide, wide, wide, wide, wide, wide, pltpu.VMEM((1, D_MODEL), F32)],
        args=[ph, ph, ph, ph, o_raw, states, dy, lb_logits, norm_g], semantics=("arbitrary",), comm=comm)


def _local_step(x, target, vec, net):
    T, D = x.shape
    norm_mix_g, b_in, sinks, lb_logits = vec["norm_mix_g"], vec["b_in"], vec["attn_sinks"], vec["hgrn_lb_logits"]
    hgrn_norm_g, norm_ffn_g, norm_final_g = vec["hgrn_norm_g"], vec["norm_ffn_g"], vec["norm_final_g"]
    w_in = net.full("w_in")
    o_q, o_kv, o_h, o_g = (sum(IN_SPLITS[:i]) for i in range(4))
    w_q, w_kv, w_h, w_g = (_Rows(w_in, o, n) for o, n in zip((o_q, o_kv, o_h, o_g), IN_SPLITS))
    b_q, b_kv, b_h, b_g = (b_in[:, o:o + n] for o, n in zip((o_q, o_kv, o_h, o_g), IN_SPLITS))
    bias = lambda acc, b: (acc + b,)
    both = lambda acc: (acc, acc)
    grad_outs = [("mn", F32), ("mn", MXU_DTYPE)]
    row_vec = lambda n: ((1, n), lambda i, j: (0, j))
    tile = lambda tm, tn: ((tm, tn), lambda i, j: (i, j))
    TM = 512
    BIG = min(T, 1024)

    u = _rmsnorm_fwd("norm_mix", x, norm_mix_g)
    pq, = _matmul("in_q", [(u, w_q)], "nt", tm=TM, tn=1024, tk=1024, outs=[("mn", MXU_DTYPE)],
                  extras=[(b_q, *row_vec(1024))], epilogue=bias)
    pkv, = _matmul("in_kv", [(u, w_kv)], "nt", tm=TM, tn=256, tk=1024, outs=[("mn", MXU_DTYPE)],
                   extras=[(b_kv, *row_vec(256))], epilogue=bias)
    names = ("w_branch_attn", "w_branch_hgrn")
    (ph,), got = _matmul("in_h", [(u, w_h)], "nt", tm=TM, tn=1024, tk=1024, outs=[("mn", F32)],
                         extras=[(b_h, *row_vec(1024))], epilogue=bias, comm=net.gather(names))
    net.gathered(names, got)
    names = ("w_out",)
    (pg,), got = _matmul("in_g", [(u, w_g)], "nt", tm=TM, tn=1024, tk=1024, outs=[("mn", F32)],
                         extras=[(b_g, *row_vec(1024))], epilogue=bias, comm=net.gather(names))
    net.gathered(names, got)
    names = ("w_ffn_gate",)
    (y_attn, lse), got = _attn_fwd(pq, pkv, sinks, comm=net.gather(names))
    net.gathered(names, got)
    names = ("w_ffn_up",)
    (y_hgrn, o_raw, states), got = _hgrn_fwd(ph, lb_logits, hgrn_norm_g, comm=net.gather(names))
    net.gathered(names, got)
    w_ba, w_bh, w_out = net.full("w_branch_attn"), net.full("w_branch_hgrn"), net.full("w_out")
    w_gate, w_up = net.full("w_ffn_gate"), net.full("w_ffn_up")
    ya, yb, merged = _merge_fwd(y_attn, y_hgrn, w_ba, w_bh, pg, tm=TM)

    def resid_norm(acc, xin, g):
        h = xin + acc
        r = lax.rsqrt(jnp.mean(h * h, axis=-1, keepdims=True) + EPS)
        return h, h * r * g

    h1, u2 = _matmul("out_proj", [(merged, w_out)], "nn", tm=TM, tn=1024, tk=1024,
                     outs=[("mn", F32), ("mn", MXU_DTYPE)],
                     extras=[(x, *tile(TM, 1024)), (norm_ffn_g, *row_vec(1024))], epilogue=resid_norm)
    FT = FFN // 2
    names = ("w_ffn_down",)
    (gpre, up, z), got = _swiglu_fwd(u2, w_gate, w_up, tm=TM, tn=FT, comm=net.gather(names))
    net.gathered(names, got)
    w_down = net.full("w_ffn_down")

    def loss_head(acc, h1_v, tgt, g):
        h2 = h1_v + acc
        r = lax.rsqrt(jnp.mean(h2 * h2, axis=-1, keepdims=True) + EPS)
        xhat = h2 * r
        err = xhat * g - tgt
        part = 0.5 * jnp.sum(jnp.sum(err * err, axis=-1, keepdims=True), axis=0, keepdims=True) / D
        dyv = err / D
        dxh = dyv * g
        dh2 = r * (dxh - xhat * jnp.mean(dxh * xhat, axis=-1, keepdims=True))
        return dh2, dh2, jnp.sum(dyv * xhat, axis=0, keepdims=True), jnp.broadcast_to(part, (1, D))

    dh2, dh2b, dgf_p, loss_p = _matmul(
        "ffn_down", [(z, w_down)], "nn", tm=TM, tn=1024, tk=FFN,
        outs=[("mn", F32), ("mn", MXU_DTYPE), ("pn", F32), ("pn", F32)],
        extras=[(h1, *tile(TM, 1024)), (target, *tile(TM, 1024)), (norm_final_g, *row_vec(1024))],
        epilogue=loss_head)
    loss = jnp.sum(loss_p.reshape(-1, 8, D)[:, 0, 0])
    d_norm_final = _colsum_partials(dgf_p)

    def swiglu_bwd(acc, gv, upv):
        gv, upv = gv.astype(F32), upv.astype(F32)
        s = _sigmoid(gv)
        return acc * upv * (s * (1.0 + gv * (1.0 - s))), acc * (gv * s)

    dgp, dup = _matmul("d_ffn_hidden", [(dh2b, w_down)], "nt", tm=TM, tn=FT, tk=1024,
                       outs=[("mn", MXU_DTYPE), ("mn", MXU_DTYPE)],
                       extras=[(gpre, *tile(TM, FT)), (up, *tile(TM, FT))], epilogue=swiglu_bwd)
    d_w_down = _matmul("dw_down", [(z, dh2b)], "tn", tm=FT, tn=1024, tk=1024, outs=grad_outs, epilogue=both)

    def norm_ffn_bwd(acc, h1_v, g, dres):
        dx, dg = _rmsnorm_bwd_vals(acc, h1_v, g)
        dh = dres + dx
        return dh, dh, dg

    names = ("w_ffn_down",)
    (dh1, dh1b, dg2_p), got = _matmul(
        "d_ffn_in", [(dgp, w_gate), (dup, w_up)], "nn", tm=BIG, tn=1024, tk=FT,
        outs=[("mn", F32), ("mn", MXU_DTYPE), ("pn", F32)],
        extras=[(h1, *tile(BIG, 1024)), (norm_ffn_g, *row_vec(1024)), (dh2, *tile(BIG, 1024))],
        epilogue=norm_ffn_bwd,
        comm=net.exchange(dict(w_ffn_down=[d_w_down])))
    net.received(names, (), got)
    d_norm_ffn = _colsum_partials(dg2_p)
    d_w_gate = _matmul("dw_gate", [(dgp, u2)], "tn", tm=FT, tn=1024, tk=1024, outs=grad_outs, epilogue=both)
    d_w_up = _matmul("dw_up", [(dup, u2)], "tn", tm=FT, tn=1024, tk=1024, outs=grad_outs, epilogue=both)

    dya, dyb, dga, dgb, dy_attn, dy_hgrn = _merge_bwd(dh1b, w_out, w_ba, w_bh, ya, yb, pg, tm=TM)
    tn_grad = functools.partial(_matmul, mode="tn", tn=1024, tk=1024, outs=grad_outs, epilogue=both)
    d_w_out = tn_grad("dw_out", [(merged, dh1b)], tm=1024)
    d_w_ba = tn_grad("dw_branch_a", [(y_attn, dya)], tm=1024)
    d_w_bh = tn_grad("dw_branch_b", [(y_hgrn, dyb)], tm=1024)

    names, swap = ("w_ffn_gate",), ("w_ffn_down",)
    (dq, dkv, dsink), got = _attn_bwd(pq, pkv, sinks, lse, dy_attn,
                                      comm=net.exchange(dict(w_ffn_gate=[d_w_gate]), swap))
    net.received(names, swap, got)
    names, swap = ("w_ffn_up", "w_out", "w_branch_attn", "w_branch_hgrn"), ("w_ffn_gate",)
    (dph, d_hgrn_norm, d_lb_logits), got = _hgrn_bwd(
        ph, o_raw, states, dy_hgrn, lb_logits, hgrn_norm_g,
        comm=net.exchange(dict(w_ffn_up=[d_w_up], w_out=[d_w_out], w_branch_attn=[d_w_ba],
                               w_branch_hgrn=[d_w_bh]), swap))
    net.received(names, swap, got)

    d_w_in, db, rows_in = None, {}, sum(IN_SPLITS)
    for piece, dp, first, tm_p in (("q", dq, o_q, 1024), ("kv", dkv, o_kv, 256), ("h", dph, o_h, 1024),
                                   ("ga", dga, o_g, 1024), ("gb", dgb, o_g + D, 1024)):
        *d_w_in, db[piece] = tn_grad("dw_in_" + piece, [(dp, u)], tm=tm_p, a_colsum=True,
                                     into=_Into(rows_in, first, d_w_in))
    d_w_in = [tuple(d_w_in)]

    def norm_mix_bwd(acc, xin, g, dres):
        dx, dg = _rmsnorm_bwd_vals(acc, xin, g)
        return dres + dx, dg

    first_level = net.presum_begin("w_in", d_w_in)
    halves = net.presum_end("w_in", [] if first_level is None else _copies_alone("presum_swap_w_in", first_level))
    names, swap = ("w_in",), ("w_ffn_up", "w_out", "w_branch_attn", "w_branch_hgrn")
    (dx, dg1_p), got = _matmul(
        "d_u", [(dq, w_q), (dkv, w_kv), (dph, w_h), (dga, _Rows(w_in, o_g, D)), (dgb, _Rows(w_in, o_g + D, D))], "nn",
        tm=BIG, tn=1024, tk=512, outs=[("mn", F32), ("pn", F32)],
        extras=[(x, *tile(BIG, 1024)), (norm_mix_g, *row_vec(1024)), (dh1, *tile(BIG, 1024))],
        epilogue=norm_mix_bwd,
        comm=_join(halves, net.swap(swap)))
    net.last = (names, swap, got)
    d_norm_mix = _colsum_partials(dg1_p)
    d_b_in = jnp.concatenate([db[piece] for piece in ("q", "kv", "h", "ga", "gb")], axis=1)
    vecs = dict(norm_mix_g=d_norm_mix, b_in=d_b_in, attn_sinks=jnp.sum(dsink.reshape(Q_HEADS, ATTN_BLOCK), axis=1).reshape(1, Q_HEADS),
                hgrn_lb_logits=d_lb_logits,
                hgrn_norm_g=d_hgrn_norm, norm_ffn_g=d_norm_ffn, norm_final_g=d_norm_final)
    return loss, dx, vecs


def _place():
    return lax.axis_index("x"), lax.axis_index("y"), lax.axis_index("c")


def _other_chips(x, y):
    return [(1 - x, y), (x, 1 - y), (1 - x, 1 - y)]


def _y_first(copies):
    return [copies[3 * (i // 3) + (1, 0, 2)[i % 3]] for i in range(len(copies))]


def _gather_copies(shards):
    n = len(shards)

    def build(ins, outs, send_sems, recv_sems, local_sems):
        x, y, c = _place()
        mine = 2 * x + y
        local = [pltpu.make_async_copy(ins[w], outs[w].at[mine], local_sems.at[w]) for w in range(n)]
        sends, recvs = [], []
        for w in range(n):
            for k, (px, py) in enumerate(_other_chips(x, y)):
                sem = 3 * w + k
                sends.append(pltpu.make_async_remote_copy(
                    src_ref=ins[w], dst_ref=outs[w].at[mine], send_sem=send_sems.at[sem], recv_sem=recv_sems.at[sem],
                    device_id=(px, py, c), device_id_type=MESH_ID))
                recvs.append(pltpu.make_async_remote_copy(
                    src_ref=ins[w], dst_ref=outs[w].at[2 * px + py], send_sem=send_sems.at[sem],
                    recv_sem=recv_sems.at[sem], device_id=(px, py, c), device_id_type=MESH_ID))
        return sends, recvs, local, _y_first(sends)

    return _Carried(shards, [jax.ShapeDtypeStruct((N_CHIPS,) + s.shape, s.dtype) for s in shards], 3 * n, n, build)


def _grad_copies(stacked):
    n = len(stacked)

    def build(ins, outs, send_sems, recv_sems, local_sems):
        x, y, c = _place()
        sends = []
        for w in range(n):
            for k, (px, py) in enumerate(_other_chips(x, y)):
                sem = 3 * w + k
                sends.append(pltpu.make_async_remote_copy(
                    src_ref=ins[w].at[2 * px + py], dst_ref=outs[w].at[k], send_sem=send_sems.at[sem],
                    recv_sem=recv_sems.at[sem], device_id=(px, py, c), device_id_type=MESH_ID))
        return sends, sends, [], _y_first(sends)

    return _Carried(stacked, [jax.ShapeDtypeStruct((3,) + s.shape[1:], s.dtype) for s in stacked], 3 * n, 0, build)


def _small_copies(small):
    def build(ins, outs, send_sems, recv_sems, local_sems):
        small_ref, all_ref = ins[0], outs[0]
        x, y, c = _place()
        me = 4 * x + 2 * y + c
        sends, recvs = [], []
        for r in range(1, 8):
            px = 1 - x if r & 4 else x
            py = 1 - y if r & 2 else y
            pc = 1 - c if r & 1 else c
            sends.append(pltpu.make_async_remote_copy(
                src_ref=small_ref, dst_ref=all_ref.at[me], send_sem=send_sems.at[r - 1], recv_sem=recv_sems.at[r - 1],
                device_id=(px, py, pc), device_id_type=MESH_ID))
            recvs.append(pltpu.make_async_remote_copy(
                src_ref=small_ref, dst_ref=all_ref.at[4 * px + 2 * py + pc], send_sem=send_sems.at[r - 1],
                recv_sem=recv_sems.at[r - 1], device_id=(px, py, pc), device_id_type=MESH_ID))
        return sends, recvs, [pltpu.make_async_copy(small_ref, all_ref.at[me], local_sems.at[0])]

    return _Carried([small], [jax.ShapeDtypeStruct((8,) + small.shape, small.dtype)], 7, 1, build)


def _gather_by_neighbours(name, shard):
    half = shard.shape[0] // 2
    quarter = half // 2

    def body(in_ref, out_ref, send_sems, recv_sems, local_sem):
        for core in (0, 1):
            @pl.when(lax.axis_index("c") == core)
            def _():
                program(core, in_ref, out_ref, send_sems, recv_sems, local_sem)

    def program(c, in_ref, out_ref, send_sems, recv_sems, local_sem):
        x, y, _ = _place()
        chip = lambda px, py: 2 * px + py
        to_x, to_y, sibling = (1 - x, y, c), (x, 1 - y, c), (x, y, 1 - c)
        x_blk, y_blk, d_blk = chip(1 - x, y), chip(x, 1 - y), chip(1 - x, 1 - y)
        mine, theirs = c * half, (1 - c) * half

        def copy(sem, rows, block, to, src=None):
            place = out_ref.at[block, pl.ds(rows[0], rows[1])]
            return pltpu.make_async_remote_copy(
                src_ref=place if src is None else src, dst_ref=place, send_sem=send_sems.at[sem],
                recv_sem=recv_sems.at[sem], device_id=to, device_id_type=MESH_ID)

        own = pltpu.make_async_copy(in_ref, out_ref.at[chip(x, y)], local_sem)
        own.start()
        my_rows = in_ref.at[pl.ds(mine, half)]
        along_x = dict(send=copy(0, (mine, half), chip(x, y), to_x, src=my_rows),
                       landed=copy(0, (mine, half), x_blk, to_x),
                       onward=[copy(3, (mine + quarter, quarter), x_blk, to_y), copy(4, (mine, half), x_blk, sibling)],
                       diagonal=copy(2, (mine, quarter), d_blk, to_x))
        along_y = dict(send=copy(1, (mine, half), chip(x, y), to_y, src=my_rows),
                       landed=copy(1, (mine, half), y_blk, to_y),
                       onward=[copy(2, (mine, quarter), y_blk, to_x), copy(5, (mine, half), y_blk, sibling)],
                       diagonal=copy(3, (mine + quarter, quarter), d_blk, to_y))
        last = copy(6, (mine, half), d_blk, sibling)

        order = (along_x, along_y) if c == 0 else (along_y, along_x)
        for axis in order:
            axis["send"].start()
        for axis in order:
            axis["landed"].wait_recv()
            for cp in axis["onward"]:
                cp.start()
        for axis in order:
            axis["diagonal"].wait_recv()
        last.start()
        for sem, block in ((4, x_blk), (5, y_blk), (6, d_blk)):
            copy(sem, (theirs, half), block, sibling).wait_recv()
        for cp in [along_x["send"], along_y["send"]] + along_x["onward"] + along_y["onward"] + [last]:
            cp.wait_send()
        own.wait()

    return pl.pallas_call(
        body, name=name, in_specs=[HBM_SPEC], out_specs=HBM_SPEC,
        out_shape=jax.ShapeDtypeStruct((N_CHIPS,) + shard.shape, shard.dtype),
        scratch_shapes=[pltpu.SemaphoreType.DMA((7,)), pltpu.SemaphoreType.DMA((7,)), pltpu.SemaphoreType.DMA(())],
    )(shard)


def _copies_alone(name, comm):
    return _call(name, lambda: None, grid=(), in_specs=[], out_specs=[], out_shape=[], args=[], comm=comm)[1]


class _Net:
    def __init__(self, shards):
        self.shards = shards
        self.whole, self.own, self.theirs, self.sums, self.other = {}, {}, {}, {}, {}
        x, y, _ = _place()
        self.chip = 2 * x + y

    def gather(self, names):
        return _gather_copies([self.shards[n] for n in names])

    def gathered(self, names, got):
        for n, g in zip(names, got):
            self.whole[n] = g.reshape(-1, g.shape[-1])

    def full(self, name):
        return self.whole[name]

    def exchange(self, grads, swap=()):
        stacked = []
        for n, pieces in grads.items():
            (keep, send), = pieces
            self.own[n] = keep
            stacked.append(send.reshape(N_CHIPS, keep.shape[0] // N_CHIPS, send.shape[-1]))
        return _join(_grad_copies(stacked), self.swap(swap))

    def swap(self, names):
        return _sibling_copies([self.sums[n] for n in names]) if names else None

    def presum_begin(self, name, pieces):
        keep = jnp.concatenate([p[0] for p in pieces], axis=0) if len(pieces) > 1 else pieces[0][0]
        send = jnp.concatenate([p[1] for p in pieces], axis=0) if len(pieces) > 1 else pieces[0][1]
        rows = keep.shape[0] // N_CHIPS
        self.held = keep.reshape(N_CHIPS, rows, keep.shape[-1])
        return _half_rows_copies(send.reshape(N_CHIPS, rows, send.shape[-1]))

    def presum_end(self, name, got):
        x, y, c = _place()
        to_send, self.own[name] = _pre_sum("presum_" + name, self.held, got[0], jnp.stack([c, self.chip]))
        return _grad_copies([to_send])

    def received(self, names, swap, got, carried=None):
        self.theirs.update(zip(names, got[:len(names)]))
        self.other.update(zip(swap, got[len(names):]))
        for n in names:
            (self.sums[n],), more = _partial_sum("sum_" + n, self.own[n], self.theirs[n], self.chip, comm=carried)
        return more


def _half_rows_copies(stacked):
    n, rows = stacked.shape[0], stacked.shape[1] // 2

    def build(ins, outs, send_sems, recv_sems, local_sems):
        x, y, c = _place()
        copies = [pltpu.make_async_remote_copy(
            src_ref=ins[0].at[s, pl.ds((1 - c) * rows, rows)], dst_ref=outs[0].at[s], send_sem=send_sems.at[s],
            recv_sem=recv_sems.at[s], device_id=(x, y, 1 - c), device_id_type=MESH_ID) for s in range(n)]
        return copies, copies, []

    return _Carried([stacked], [jax.ShapeDtypeStruct((n, rows, stacked.shape[2]), stacked.dtype)], n, 0, build)


def _pre_sum(name, held, theirs, core_and_chip):
    n, R, C = held.shape
    half = R // 2
    tr = _row_tile(half)
    per_half = half // tr

    def body(place_ref, h_ref, t_ref, send_ref, own_ref):
        total = h_ref[0] + t_ref[0].astype(F32)
        send_ref[0] = total.astype(send_ref.dtype)

        @pl.when(pl.program_id(1) == place_ref[1])
        def _():
            own_ref[...] = total

    return pl.pallas_call(
        body, name=name,
        grid_spec=pltpu.PrefetchScalarGridSpec(
            num_scalar_prefetch=1, grid=(per_half, n),
            in_specs=[pl.BlockSpec((1, tr, C), lambda i, s, place: (s, place[0] * per_half + i, 0)),
                      pl.BlockSpec((1, tr, C), lambda i, s, place: (s, i, 0))],
            out_specs=[pl.BlockSpec((1, tr, C), lambda i, s, place: (s, i, 0)),
                       pl.BlockSpec((tr, C), lambda i, s, place: (i, 0))]),
        out_shape=[jax.ShapeDtypeStruct((n, half, C), MXU_DTYPE), jax.ShapeDtypeStruct((half, C), F32)],
        compiler_params=_params(("arbitrary", "arbitrary")),
    )(core_and_chip, held, theirs)


def _sibling_copies(parts):
    n = len(parts)

    def build(ins, outs, send_sems, recv_sems, local_sems):
        x, y, c = _place()
        copies = [pltpu.make_async_remote_copy(
            src_ref=ins[w], dst_ref=outs[w], send_sem=send_sems.at[w], recv_sem=recv_sems.at[w],
            device_id=(x, y, 1 - c), device_id_type=MESH_ID) for w in range(n)]
        return copies, copies, []

    return _Carried(parts, [jax.ShapeDtypeStruct(p.shape, p.dtype) for p in parts], n, 0, build)


def _row_tile(rows, most=512, sublanes=16):
    return max(t for t in range(sublanes, most + 1, sublanes) if rows % t == 0)


def _partial_sum(name, own, recv, chip, comm=None):
    _, R, C = recv.shape
    tr = _row_tile(R, most=128)

    def body(o_ref, r_ref, p_ref):
        p_ref[...] = ((o_ref[...] + r_ref[0].astype(F32)) + r_ref[1].astype(F32)) + r_ref[2].astype(F32)

    if own.shape[0] != R:
        assert comm is None and own.shape[0] == N_CHIPS * R
        total = pl.pallas_call(
            lambda chip_ref, *refs: body(*refs), name=name,
            grid_spec=pltpu.PrefetchScalarGridSpec(
                num_scalar_prefetch=1, grid=(R // tr,),
                in_specs=[pl.BlockSpec((tr, C), lambda i, chip_ref: (chip_ref[0] * (R // tr) + i, 0)),
                          pl.BlockSpec((3, tr, C), lambda i, chip_ref: (0, i, 0))],
                out_specs=pl.BlockSpec((tr, C), lambda i, chip_ref: (i, 0))),
            out_shape=jax.ShapeDtypeStruct((R, C), F32), compiler_params=_params(("parallel",)),
        )(chip.reshape(1), own, recv)
        return [total], []
    return _call(name, body, grid=(R // tr,),
                 in_specs=[pl.BlockSpec((tr, C), lambda i: (i, 0)), pl.BlockSpec((3, tr, C), lambda i: (0, i, 0))],
                 out_specs=[pl.BlockSpec((tr, C), lambda i: (i, 0))], out_shape=[jax.ShapeDtypeStruct((R, C), F32)],
                 args=[own, recv], semantics=("parallel",), comm=comm)


def _adam_vals(w, g, m, v):
    m = ADAM_B1 * m + (1.0 - ADAM_B1) * g
    v = ADAM_B2 * v + (1.0 - ADAM_B2) * (g * g)
    m_hat = m / (1.0 - ADAM_B1 ** ADAM_STEP)
    v_hat = v / (1.0 - ADAM_B2 ** ADAM_STEP)
    delta = -ADAM_LR * (m_hat / (jnp.sqrt(v_hat) + ADAM_EPS) + ADAM_WD * w)
    return delta, m, v


def _adamw(name, w, m, v, mine, other, comm=None):
    R, C = w.shape
    tr = _row_tile(R)

    def body(w_ref, m_ref, v_ref, s_ref, n_ref, g_ref, d_ref, nm_ref, nv_ref):
        g = s_ref[...] + n_ref[...]
        d, nm, nv = _adam_vals(w_ref[...], g, m_ref[...], v_ref[...])
        g_ref[...], d_ref[...], nm_ref[...], nv_ref[...] = g, d, nm, nv

    spec = pl.BlockSpec((tr, C), lambda i: (i, 0))
    return _call(name, body, grid=(R // tr,), in_specs=[spec] * 5, out_specs=[spec] * 4,
                 out_shape=[jax.ShapeDtypeStruct((R, C), F32)] * 4, args=[w, m, v, mine, other],
                 semantics=("parallel",), comm=comm)


def _adamw_by_halves(name, w, m, v, mine, other, core):
    R, C = w.shape
    tr = _row_tile(R // 2)
    per_half = R // 2 // tr

    def body(c_ref, w_ref, m_ref, v_ref, s_ref, n_ref, g_ref, d_ref, nm_ref, nv_ref):
        g = jnp.where(pl.program_id(0) // per_half == c_ref[0, 0], s_ref[...], n_ref[...])
        d, nm, nv = _adam_vals(w_ref[...], g, m_ref[...], v_ref[...])
        g_ref[...], d_ref[...], nm_ref[...], nv_ref[...] = g, d, nm, nv

    spec = pl.BlockSpec((tr, C), lambda i: (i, 0))
    part = pl.BlockSpec((tr, C), lambda i: (i % per_half, 0))
    return pl.pallas_call(
        body, name=name, grid=(R // tr,),
        in_specs=[pl.BlockSpec(memory_space=pltpu.SMEM), spec, spec, spec, part, part], out_specs=[spec] * 4,
        out_shape=[jax.ShapeDtypeStruct((R, C), F32)] * 4, compiler_params=_params(("parallel",)),
    )(core, w, m, v, mine, other)


SMALL_LAYOUT = dict(norm_mix_g=(0, 1, 1024), b_in=(1, 8, 7424), hgrn_norm_g=(9, 1, 1024), norm_ffn_g=(10, 1, 1024),
                    norm_final_g=(11, 1, 1024), hgrn_lb_logits=(12, 2, 2048), attn_sinks=(14, 1, 16))
SMALL_LOSS_ROW, SMALL_ROWS = 15, 16


def _pack_small(grads, loss):
    rows = [jnp.pad(grads[name].astype(F32).reshape(-1), (0, nrows * D_MODEL - n))
            for name, (_, nrows, n) in SMALL_LAYOUT.items()]
    rows.append(jnp.pad(loss.astype(F32).reshape(1), (0, D_MODEL - 1)))
    return jnp.concatenate(rows).reshape(SMALL_ROWS, D_MODEL)


def _adamw_small(w, m, v, g_all):
    names = list(SMALL_LAYOUT)
    n = len(names)

    def body(a_ref, *refs):
        ins, outs = refs[:3 * n], refs[3 * n:]
        g_all_rows = a_ref[0]
        for dev in range(1, 8):
            g_all_rows = g_all_rows + a_ref[dev]
        for i, name in enumerate(names):
            first, nrows, count = SMALL_LAYOUT[name]
            w_ref, m_ref, v_ref = ins[3 * i:3 * i + 3]
            if w_ref.shape[0] == nrows:
                g = g_all_rows[first:first + nrows, :w_ref.shape[1]]
            else:
                last = count - (nrows - 1) * D_MODEL
                g = jnp.concatenate([g_all_rows[r:r + 1, :] for r in range(first, first + nrows - 1)]
                                    + [g_all_rows[first + nrows - 1:first + nrows, :last]], axis=1)
            d, nm, nv = _adam_vals(w_ref[...], g, m_ref[...], v_ref[...])
            for o_ref, val in zip(outs[4 * i:4 * i + 4], (g, d, nm, nv)):
                o_ref[...] = val
        outs[4 * n][...] = g_all_rows[SMALL_LOSS_ROW:SMALL_LOSS_ROW + 1, 0:1]

    res = pl.pallas_call(
        body, name="adamw_small",
        out_shape=[jax.ShapeDtypeStruct(w[name].shape, F32) for name in names for _ in range(4)]
        + [jax.ShapeDtypeStruct((1, 1), F32)],
    )(g_all, *[t[name] for name in names for t in (w, m, v)])
    return {name: res[4 * i:4 * i + 4] for i, name in enumerate(names)}, res[4 * n]


MATRICES = ("w_in", "w_branch_attn", "w_branch_hgrn", "w_out", "w_ffn_gate", "w_ffn_up", "w_ffn_down")
COLUMN_SHARDED = ("w_in", "w_ffn_gate", "w_ffn_up")
WEIGHTS = ("norm_mix_g", "w_in", "b_in", "attn_sinks", "hgrn_lb_logits", "hgrn_norm_g", "w_branch_attn",
           "w_branch_hgrn", "w_out", "norm_ffn_g", "w_ffn_gate", "w_ffn_up", "w_ffn_down", "norm_final_g")


def kernel(x, norm_mix_g, w_in, b_in, attn_sinks, hgrn_lb_logits, hgrn_norm_g, w_branch_attn, w_branch_hgrn, w_out, norm_ffn_g, w_ffn_gate, w_ffn_up, w_ffn_down, norm_final_g, loss_target, m_norm_mix_g, m_w_in, m_b_in, m_attn_sinks, m_hgrn_lb_logits, m_hgrn_norm_g, m_w_branch_attn, m_w_branch_hgrn, m_w_out, m_norm_ffn_g, m_w_ffn_gate, m_w_ffn_up, m_w_ffn_down, m_norm_final_g, v_norm_mix_g, v_w_in, v_b_in, v_attn_sinks, v_hgrn_lb_logits, v_hgrn_norm_g, v_w_branch_attn, v_w_branch_hgrn, v_w_out, v_norm_ffn_g, v_w_ffn_gate, v_w_ffn_up, v_w_ffn_down, v_norm_final_g):
    given = dict(locals())
    w = {n: given[n] for n in WEIGHTS}
    m = {n: given["m_" + n] for n in WEIGHTS}
    v = {n: given["v_" + n] for n in WEIGHTS}

    block = lambda a, n: jnp.transpose(a[0]) if n in COLUMN_SHARDED else a[0]
    unblock = lambda a, n: (jnp.transpose(a) if n in COLUMN_SHARDED else a)[None]
    net = _Net({n: block(w[n], n).astype(MXU_DTYPE) for n in MATRICES})
    net.gathered(("w_in",), [_gather_by_neighbours("gather_w_in", net.shards["w_in"])])
    vec = dict(norm_mix_g=norm_mix_g, b_in=b_in, attn_sinks=attn_sinks, hgrn_lb_logits=hgrn_lb_logits,
               hgrn_norm_g=hgrn_norm_g, norm_ffn_g=norm_ffn_g, norm_final_g=norm_final_g.reshape(1, D_MODEL))
    loss_part, dx, d_vecs = _local_step(x[0], loss_target[0], vec, net)

    small_all, = net.received(*net.last, carried=_small_copies(_pack_small(d_vecs, loss_part)))
    grads, deltas, new_m, new_v = {}, {}, {}, {}
    for n in ("w_ffn_down", "w_ffn_gate", "w_ffn_up", "w_out", "w_branch_attn", "w_branch_hgrn"):
        res, got = _adamw("adamw_" + n, block(w[n], n), block(m[n], n), block(v[n], n), net.sums[n], net.other[n],
                          comm=net.swap(("w_in",)) if n == "w_ffn_down" else None)
        if n == "w_ffn_down":
            net.other["w_in"], = got
        grads[n], deltas[n], new_m[n], new_v[n] = (unblock(r, n) for r in res)
    n = "w_in"
    res = _adamw_by_halves("adamw_" + n, block(w[n], n), block(m[n], n), block(v[n], n), net.sums[n], net.other[n],
                           _place()[2].reshape(1, 1))
    grads[n], deltas[n], new_m[n], new_v[n] = (unblock(r, n) for r in res)
    rows = lambda t: {n: t[n].reshape(-1, t[n].shape[-1]) for n in SMALL_LAYOUT}
    res, loss = _adamw_small(rows(w), rows(m), rows(v), small_all)
    for n, four in res.items():
        grads[n], deltas[n], new_m[n], new_v[n] = (r.reshape(w[n].shape) for r in four)
    loss = loss.reshape(())
    return (loss, dx[None], *[grads[n] for n in WEIGHTS], *[deltas[n] for n in WEIGHTS],
            *[new_m[n] for n in WEIGHTS], *[new_v[n] for n in WEIGHTS])
```

```python
import collections
import functools
import math

import jax
import jax.numpy as jnp
from jax import lax
from jax.experimental import pallas as pl
from jax.experimental.pallas import tpu as pltpu

F32 = jnp.float32
BF16 = jnp.bfloat16
MXU_DTYPE = jnp.bfloat16
SAVED_DTYPE = jnp.bfloat16
MESH_ID = pl.DeviceIdType.MESH

D_MODEL = 1024
HEAD_DIM = 64
Q_HEADS = 16
KV_HEADS = 2
GROUP = Q_HEADS // KV_HEADS
KV_WIDTH = KV_HEADS * HEAD_DIM
ATTN_BLOCK = 128
HGRN_HEADS = 8
HGRN_K = 128
CHUNK = 64
HGRN_TOKENS = 256
FFN = 2816
IN_SPLITS = (1024, 256, 4096, 2048)
EPS = 1e-6
NEG_INF = -1e30
ADAM_LR, ADAM_B1, ADAM_B2, ADAM_EPS, ADAM_WD, ADAM_STEP = 0.001, 0.9, 0.999, 1e-08, 0.01, 10
N_CHIPS = 4
VMEM_LIMIT = 60 * 1024 * 1024
ROW_ALIGN = 16


def _params(sem=None):
    return pltpu.CompilerParams(dimension_semantics=sem, vmem_limit_bytes=VMEM_LIMIT)


def _sigmoid(v):
    return 0.5 * jnp.tanh(0.5 * v) + 0.5


def _dot(a, b, dims):
    return lax.dot_general(a.astype(MXU_DTYPE), b.astype(MXU_DTYPE), (dims, ((), ())),
                           preferred_element_type=F32)


def _nn(a, b):
    return _dot(a, b, ((1,), (0,)))


def _nt(a, b):
    return _dot(a, b, ((1,), (1,)))


def _tn(a, b):
    return _dot(a, b, ((0,), (0,)))


HBM_SPEC = pl.BlockSpec(memory_space=pl.ANY)


class _Carried:
    def __init__(self, arrays, out_shapes, n_remote, n_local, build):
        self.parts = [(len(arrays), len(out_shapes), build)]
        self.arrays, self.out_shapes = list(arrays), list(out_shapes)
        self.scratch = [pltpu.SemaphoreType.DMA((n_remote,)), pltpu.SemaphoreType.DMA((n_remote,)),
                        pltpu.SemaphoreType.DMA((max(n_local, 1),))]

    def __add__(self, other):
        both = _Carried([], [], 1, 0, None)
        both.parts = self.parts + other.parts
        both.arrays, both.out_shapes = self.arrays + other.arrays, self.out_shapes + other.out_shapes
        both.scratch = self.scratch + other.scratch
        return both

    def _built(self, ins, outs, sems):
        for p, (ni, no, build) in enumerate(self.parts):
            yield build(ins[:ni], outs[:no], *sems[3 * p:3 * p + 3])
            ins, outs = ins[ni:], outs[no:]

    def start(self, ins, outs, sems):
        core = lax.axis_index("c")
        for sends, _, local, *other_order in self._built(ins, outs, sems):
            for cp in local:
                cp.start()
            if not other_order:
                for cp in sends:
                    cp.start()
                continue

            @pl.when(core == 0)
            def _():
                for cp in sends:
                    cp.start()

            @pl.when(core == 1)
            def _():
                for cp in other_order[0]:
                    cp.start()

    def wait(self, ins, outs, sems):
        for sends, recvs, local, *_ in self._built(ins, outs, sems):
            for cp in recvs:
                cp.wait_recv()
            for cp in sends:
                cp.wait_send()
            for cp in local:
                cp.wait()


def _join(*comms):
    comms = [c for c in comms if c is not None]
    return functools.reduce(lambda a, b: a + b, comms) if comms else None


def _call(name, body, *, grid, in_specs, out_specs, out_shape, args, scratch=(), semantics=None, comm=None,
          aliases=None):
    n_in, n_out, n_scr = len(in_specs), len(out_specs), len(scratch)
    aliases = aliases or {}
    if comm is None:
        res = pl.pallas_call(body, name=name, grid=grid, in_specs=in_specs, out_specs=out_specs, out_shape=out_shape,
                             scratch_shapes=list(scratch), input_output_aliases=aliases,
                             compiler_params=_params(semantics))(*args)
        return list(res), []
    ci, co = len(comm.arrays), len(comm.out_shapes)

    def carrying(*refs):
        ins, refs = refs[:n_in], refs[n_in:]
        c_ins, refs = refs[:ci], refs[ci:]
        outs, refs = refs[:n_out], refs[n_out:]
        c_outs, refs = refs[:co], refs[co:]
        scr, sems = refs[:n_scr], refs[n_scr:]
        if not grid:
            comm.start(c_ins, c_outs, sems)
            body(*ins, *outs, *scr)
            comm.wait(c_ins, c_outs, sems)
            return
        first = functools.reduce(jnp.logical_and, [pl.program_id(a) == 0 for a in range(len(grid))])
        last = functools.reduce(jnp.logical_and, [pl.program_id(a) == g - 1 for a, g in enumerate(grid)])

        @pl.when(first)
        def _():
            comm.start(c_ins, c_outs, sems)

        body(*ins, *outs, *scr)

        @pl.when(last)
        def _():
            comm.wait(c_ins, c_outs, sems)

    res = pl.pallas_call(
        carrying, name=name, grid=grid, in_specs=list(in_specs) + [HBM_SPEC] * ci,
        out_specs=list(out_specs) + [HBM_SPEC] * co, out_shape=list(out_shape) + comm.out_shapes,
        scratch_shapes=list(scratch) + comm.scratch, input_output_aliases=aliases,
        compiler_params=_params(("arbitrary",) * len(grid) if grid else None),
    )(*args, *comm.arrays)
    return list(res[:n_out]), list(res[n_out:])


_NO_COPIES = object()
_Rows = collections.namedtuple("_Rows", "array first rows")


_Into = collections.namedtuple("_Into", "rows first held")


def _matmul(name, pairs, mode, *, tm, tn, tk, outs, extras=(), epilogue=None, a_colsum=False, into=None,
            comm=_NO_COPIES):
    prod = dict(nn=_nn, nt=_nt, tn=_tn)[mode]
    pairs = [(a, b if isinstance(b, _Rows) else _Rows(b, 0, b.shape[0])) for a, b in pairs]
    a0, b0 = pairs[0]
    M = a0.shape[1] if mode == "tn" else a0.shape[0]
    N = b0.rows if mode == "nt" else b0.array.shape[1]
    steps, in_specs, offset = [], [], 0
    for a, b in pairs:
        K = a.shape[0] if mode == "tn" else a.shape[1]
        t = min(tk, K)
        assert K % t == 0, (name, K, t)
        kmap = functools.partial(lambda k, off, n: jnp.clip(k - off, 0, n - 1), off=offset, n=K // t)
        if mode == "tn":
            in_specs.append(pl.BlockSpec((t, tm), functools.partial(lambda i, j, k, f: (f(k), i), f=kmap)))
        else:
            in_specs.append(pl.BlockSpec((tm, t), functools.partial(lambda i, j, k, f: (i, f(k)), f=kmap)))
        whole = b.first == 0 and b.rows == b.array.shape[0]
        if mode == "nt":
            shape = (tn, t) if whole else (pl.Element(tn), pl.Element(t))
            in_specs.append(pl.BlockSpec(shape, functools.partial(
                lambda i, j, k, f, b, t, whole: (j, f(k)) if whole else (
                    pl.multiple_of(b.first + j * tn, ROW_ALIGN), pl.multiple_of(f(k) * t, 128)),
                f=kmap, b=b, t=t, whole=whole)))
        else:
            assert b.rows == K, (name, b.rows, K)
            shape = (t, tn) if whole else (pl.Element(t), pl.Element(tn))
            in_specs.append(pl.BlockSpec(shape, functools.partial(
                lambda i, j, k, f, b, t, whole: (f(k), j) if whole else (
                    pl.multiple_of(b.first + f(k) * t, ROW_ALIGN), pl.multiple_of(j * tn, 128)),
                f=kmap, b=b, t=t, whole=whole)))
        steps.append((offset, offset + K // t))
        offset += K // t
    assert M % tm == 0 and N % tn == 0, (name, M, N, tm, tn)
    ni, nj, nk = M // tm, N // tn, offset
    npair, ne, no = len(pairs), len(extras), len(outs)
    if epilogue is None:
        epilogue = lambda acc: (acc,)

    def finish(acc, extra_refs, out_refs):
        vals = epilogue(acc, *[r[...] for r in extra_refs])
        for (kind, _), o_ref, val in zip(outs, out_refs, vals):
            if kind == "pn":
                val = jnp.broadcast_to(val, o_ref.shape)
            o_ref[...] = val.astype(o_ref.dtype)

    held = list(into.held) if into is not None and into.held is not None else []

    def body(*refs):
        ab, rest = refs[:2 * npair], refs[2 * npair:]
        extra_refs, rest = rest[:ne], rest[ne + len(held):]
        out_refs = rest[:no]
        if nk == 1:
            finish(prod(ab[0][...], ab[1][...]), extra_refs, out_refs)
            return
        sums_ref, acc_ref = rest[no], rest[-1]
        k = pl.program_id(2)

        @pl.when(k == 0)
        def _():
            acc_ref[...] = jnp.zeros_like(acc_ref)
            if a_colsum:
                sums_ref[...] = jnp.zeros((1, tm), F32)

        if a_colsum:
            sums_ref[...] += jnp.sum(ab[0][...].astype(F32), axis=0, keepdims=True)
        for p, (lo, hi) in enumerate(steps):
            @pl.when(jnp.logical_and(k >= lo, k < hi))
            def _():
                acc_ref[...] += prod(ab[2 * p][...], ab[2 * p + 1][...])

        @pl.when(k == nk - 1)
        def _():
            finish(acc_ref[...], extra_refs, out_refs)

    for _, shape, im in extras:
        in_specs.append(pl.BlockSpec(shape, functools.partial(lambda i, j, k, im: im(i, j), im=im)))
    in_specs += [HBM_SPEC] * len(held)
    aliases = {2 * npair + ne + p: p for p in range(len(held))}
    out_shape, out_specs = [], []
    for kind, dt in outs:
        if kind == "mn" and into is not None:
            out_shape.append(jax.ShapeDtypeStruct((into.rows, N), dt))
            out_specs.append(pl.BlockSpec((pl.Element(tm), pl.Element(tn)), lambda i, j, k: (
                pl.multiple_of(into.first + i * tm, ROW_ALIGN), pl.multiple_of(j * tn, 128))))
        elif kind == "mn":
            out_shape.append(jax.ShapeDtypeStruct((M, N), dt))
            out_specs.append(pl.BlockSpec((tm, tn), lambda i, j, k: (i, j)))
        else:
            out_shape.append(jax.ShapeDtypeStruct((8 * ni, N), dt))
            out_specs.append(pl.BlockSpec((8, tn), lambda i, j, k: (i, j)))
    if a_colsum:
        assert mode == "tn" and npair == 1 and nj == 1 and nk > 1, name
        out_shape.append(jax.ShapeDtypeStruct((1, M), F32))
        out_specs.append(pl.BlockSpec((1, tm), lambda i, j, k: (0, i)))
    grid = (ni, nj, nk)
    if nj > 1 and nk == 1:
        turned = lambda spec: pl.BlockSpec(spec.block_shape, functools.partial(
            lambda j, i, k, im: im(i, j, k), im=spec.index_map))
        in_specs, out_specs, grid = [turned(s) for s in in_specs], [turned(s) for s in out_specs], (nj, ni, nk)
    res, got = _call(name, body, grid=grid, in_specs=in_specs, out_specs=out_specs, out_shape=out_shape,
                     args=[t for a, b in pairs for t in (a, b.array)] + [e[0] for e in extras] + held,
                     scratch=[pltpu.VMEM((tm, tn), F32)] if nk > 1 else [], aliases=aliases,
                     semantics=("parallel", "parallel", "arbitrary"), comm=None if comm is _NO_COPIES else comm)
    return res if comm is _NO_COPIES else (res, got)


def _ffn_fwd(merged, w_out, x, gain, w_gate_t, w_up_t, *, tm, comm=None):
    (T, D), F = x.shape, w_gate_t.shape[0]

    def body(m_ref, wo_ref, x_ref, g_ref, wg_ref, wu_ref, h_ref, u_ref, gate_ref, up_ref, z_ref):
        h = x_ref[...] + _nn(m_ref[...], wo_ref[...])
        h_ref[...] = h
        u = (h * lax.rsqrt(jnp.mean(h * h, axis=-1, keepdims=True) + EPS) * g_ref[...]).astype(u_ref.dtype)
        u_ref[...] = u
        gate, up = _nt(u, wg_ref[...]), _nt(u, wu_ref[...])
        gate_ref[...], up_ref[...] = gate.astype(gate_ref.dtype), up.astype(up_ref.dtype)
        z_ref[...] = (gate * _sigmoid(gate) * up).astype(z_ref.dtype)

    rows = lambda n: pl.BlockSpec((tm, n), lambda i: (i, 0))
    fixed = lambda a: pl.BlockSpec(a.shape, lambda i: (0, 0))
    return _call("ffn_hidden", body, grid=(T // tm,),
                 in_specs=[rows(D), fixed(w_out), rows(D), fixed(gain), fixed(w_gate_t), fixed(w_up_t)],
                 out_specs=[rows(D), rows(D), rows(F), rows(F), rows(F)],
                 out_shape=[jax.ShapeDtypeStruct((T, D), F32), jax.ShapeDtypeStruct((T, D), MXU_DTYPE)]
                 + [jax.ShapeDtypeStruct((T, F), SAVED_DTYPE)] * 2 + [jax.ShapeDtypeStruct((T, F), MXU_DTYPE)],
                 args=[merged, w_out, x, gain, w_gate_t, w_up_t], semantics=("parallel",), comm=comm)


def _in_proj(x, gain, w_in_t, b_in, *, tm, comm=None):
    T, D = x.shape
    bounds = [sum(IN_SPLITS[:i]) for i in range(len(IN_SPLITS) + 1)]

    def body(x_ref, g_ref, w_ref, b_ref, u_ref, *piece_refs):
        xv = x_ref[...]
        r = lax.rsqrt(jnp.mean(xv * xv, axis=-1, keepdims=True) + EPS)
        u = (xv * r * g_ref[...]).astype(u_ref.dtype)
        u_ref[...] = u
        for o_ref, lo, hi in zip(piece_refs, bounds[:-1], bounds[1:]):
            o_ref[...] = (_nt(u, w_ref[lo:hi, :]) + b_ref[:, lo:hi]).astype(o_ref.dtype)

    rows = lambda n: pl.BlockSpec((tm, n), lambda i: (i, 0))
    fixed = lambda a: pl.BlockSpec(a.shape, lambda i: (0, 0))
    dtypes = (MXU_DTYPE, MXU_DTYPE, F32, F32)
    return _call("in_proj", body, grid=(T // tm,),
                 in_specs=[rows(D), fixed(gain), fixed(w_in_t), fixed(b_in)],
                 out_specs=[rows(D)] + [rows(n) for n in IN_SPLITS],
                 out_shape=[jax.ShapeDtypeStruct((T, D), MXU_DTYPE)]
                 + [jax.ShapeDtypeStruct((T, n), dt) for n, dt in zip(IN_SPLITS, dtypes)],
                 args=[x, gain, w_in_t, b_in], semantics=("parallel",), comm=comm)


def _merge_fwd(y_a, y_b, w_a, w_b, gates, *, tm):
    T, D = y_a.shape

    def body(ya_ref, yb_ref, wa_ref, wb_ref, ga_ref, gb_ref, pa_ref, pb_ref, m_ref):
        pa, pb = _nn(ya_ref[...], wa_ref[...]), _nn(yb_ref[...], wb_ref[...])
        pa_ref[...], pb_ref[...] = pa, pb
        m_ref[...] = (_sigmoid(ga_ref[...]) * pa + _sigmoid(gb_ref[...]) * pb).astype(m_ref.dtype)

    rows = pl.BlockSpec((tm, D), lambda i: (i, 0))
    whole = pl.BlockSpec((D, D), lambda i: (0, 0))
    return pl.pallas_call(
        body, name="branch_merge", grid=(T // tm,),
        in_specs=[rows, rows, whole, whole, rows, pl.BlockSpec((tm, D), lambda i: (i, 1))], out_specs=[rows] * 3,
        out_shape=[jax.ShapeDtypeStruct((T, D), F32)] * 2 + [jax.ShapeDtypeStruct((T, D), MXU_DTYPE)],
        compiler_params=_params(("parallel",)),
    )(y_a, y_b, w_a, w_b, gates, gates)


def _merge_bwd(dh, w_out, w_a, w_b, p_a, p_b, gates, *, tm):
    T, D = dh.shape

    def body(dh_ref, wo_ref, wa_ref, wb_ref, pa_ref, pb_ref, ga_ref, gb_ref, dpa_ref, dpb_ref, dga_ref, dgb_ref,
             dya_ref, dyb_ref):
        dm = _nt(dh_ref[...], wo_ref[...])
        sa, sb = _sigmoid(ga_ref[...]), _sigmoid(gb_ref[...])
        dpa, dpb = (dm * sa).astype(dpa_ref.dtype), (dm * sb).astype(dpb_ref.dtype)
        dpa_ref[...], dpb_ref[...] = dpa, dpb
        dga_ref[...] = (dm * pa_ref[...] * sa * (1.0 - sa)).astype(dga_ref.dtype)
        dgb_ref[...] = (dm * pb_ref[...] * sb * (1.0 - sb)).astype(dgb_ref.dtype)
        dya_ref[...] = _nt(dpa, wa_ref[...]).astype(dya_ref.dtype)
        dyb_ref[...] = _nt(dpb, wb_ref[...]).astype(dyb_ref.dtype)

    rows = pl.BlockSpec((tm, D), lambda i: (i, 0))
    whole = pl.BlockSpec((D, D), lambda i: (0, 0))
    low = jax.ShapeDtypeStruct((T, D), MXU_DTYPE)
    return pl.pallas_call(
        body, name="d_branch_merge", grid=(T // tm,),
        in_specs=[rows, whole, whole, whole, rows, rows, rows, pl.BlockSpec((tm, D), lambda i: (i, 1))],
        out_specs=[rows] * 6, out_shape=[low] * 5 + [jax.ShapeDtypeStruct((T, D), F32)],
        compiler_params=_params(("parallel",)),
    )(dh, w_out, w_a, w_b, p_a, p_b, gates, gates)


def _colsum_partials(p):
    return jnp.sum(p.reshape(-1, 8, p.shape[-1])[:, 0, :], axis=0, keepdims=True)


def _rmsnorm_bwd_vals(dy, xin, g):
    rstd = lax.rsqrt(jnp.mean(xin * xin, axis=-1, keepdims=True) + EPS)
    xhat = xin * rstd
    dg = jnp.sum(dy * xhat, axis=0, keepdims=True)
    dxh = dy * g
    dx = rstd * (dxh - xhat * jnp.mean(dxh * xhat, axis=-1, keepdims=True))
    return dx, dg


def _colsum(name, a, tr=512, comm=_NO_COPIES):
    T, N = a.shape

    def body(a_ref, o_ref):
        @pl.when(pl.program_id(0) == 0)
        def _():
            o_ref[...] = jnp.zeros_like(o_ref)

        o_ref[...] += jnp.sum(a_ref[...].astype(F32), axis=0, keepdims=True)

    (res,), got = _call(name, body, grid=(T // tr,), in_specs=[pl.BlockSpec((tr, N), lambda i: (i, 0))],
                        out_specs=[pl.BlockSpec((1, N), lambda i: (0, 0))],
                        out_shape=[jax.ShapeDtypeStruct((1, N), F32)], args=[a], semantics=("arbitrary",),
                        comm=None if comm is _NO_COPIES else comm)
    return res if comm is _NO_COPIES else (res, got)


ATTN_SCALE = 1.0 / math.sqrt(HEAD_DIM)
GROUP_LANES = GROUP * ATTN_BLOCK
PAIR = 2 * HEAD_DIM


def _attn_mask():
    kj = lax.broadcasted_iota(jnp.int32, (ATTN_BLOCK, GROUP_LANES), 0)
    qi = lax.broadcasted_iota(jnp.int32, (ATTN_BLOCK, GROUP_LANES), 1) & (ATTN_BLOCK - 1)
    return kj <= qi


def _heads_transposed(ref, g, scale=None):
    parts = []
    for a in range(GROUP // 2):
        lo = (g * GROUP // 2 + a) * PAIR
        pair = ref[:, lo:lo + PAIR].astype(F32)
        pair = (pair if scale is None else pair * scale).T
        parts += [pair[:HEAD_DIM], pair[HEAD_DIM:]]
    return jnp.concatenate(parts, axis=1).astype(MXU_DTYPE)


def _heads_back(ref, g, vt):
    for a in range(GROUP // 2):
        lo = (g * GROUP // 2 + a) * PAIR
        pair = jnp.concatenate([vt[:, (2 * a) * ATTN_BLOCK:(2 * a + 1) * ATTN_BLOCK],
                                vt[:, (2 * a + 1) * ATTN_BLOCK:(2 * a + 2) * ATTN_BLOCK]], axis=0)
        ref[:, lo:lo + PAIR] = pair.T.astype(ref.dtype)


def _kv_parts(kv_ref, g):
    ks = slice(g * HEAD_DIM, (g + 1) * HEAD_DIM)
    vs = slice(KV_WIDTH + g * HEAD_DIM, KV_WIDTH + (g + 1) * HEAD_DIM)
    return kv_ref[:, ks].astype(MXU_DTYPE), kv_ref[:, vs].astype(MXU_DTYPE)


def _sink_rows(sinks):
    return jnp.repeat(sinks.reshape(KV_HEADS, GROUP), ATTN_BLOCK, axis=1)


def _attn_fwd(pq, pkv, sinks, comm=None):
    T = pq.shape[0]
    nb = T // ATTN_BLOCK

    def body(q_ref, kvc_ref, kvp_ref, s_ref, y_ref, lse_ref):
        mask_c = _attn_mask()
        has_prev = pl.program_id(0) > 0
        for g in range(KV_HEADS):
            (kc, vc), (kp, vp) = _kv_parts(kvc_ref, g), _kv_parts(kvp_ref, g)
            qt = _heads_transposed(q_ref, g, ATTN_SCALE)
            s = jnp.where(mask_c, _nn(kc, qt), jnp.where(has_prev, _nn(kp, qt), NEG_INF))
            sink = s_ref[g:g + 1, :]
            m = jnp.maximum(jnp.max(s, axis=0, keepdims=True), sink)
            p = jnp.exp(s - m)
            den = jnp.sum(p, axis=0, keepdims=True) + jnp.exp(sink - m)
            pc = jnp.where(mask_c, p, 0.0)
            _heads_back(y_ref, g, (_tn(vc, pc) + _tn(vp, p - pc)) / den)
            lse = m + jnp.log(den)
            for i in range(GROUP):
                lse_ref[g * GROUP + i:g * GROUP + i + 1, :] = lse[:, i * ATTN_BLOCK:(i + 1) * ATTN_BLOCK]

    return _call(
        "attn_fwd", body, grid=(nb,),
        in_specs=[pl.BlockSpec((ATTN_BLOCK, D_MODEL), lambda n: (n, 0)),
                  pl.BlockSpec((ATTN_BLOCK, 2 * KV_WIDTH), lambda n: (n, 0)),
                  pl.BlockSpec((ATTN_BLOCK, 2 * KV_WIDTH), lambda n: (jnp.maximum(n - 1, 0), 0)),
                  pl.BlockSpec((KV_HEADS, GROUP_LANES), lambda n: (0, 0))],
        out_specs=[pl.BlockSpec((ATTN_BLOCK, D_MODEL), lambda n: (n, 0)),
                   pl.BlockSpec((Q_HEADS, ATTN_BLOCK), lambda n: (0, n))],
        out_shape=[jax.ShapeDtypeStruct((T, D_MODEL), MXU_DTYPE), jax.ShapeDtypeStruct((Q_HEADS, T), F32)],
        args=[pq, pkv, pkv, _sink_rows(sinks)], semantics=("parallel",), comm=comm)


def _attn_bwd(pq, pkv, sinks, lse, dy, comm=None):
    T = pq.shape[0]
    nb = T // ATTN_BLOCK
    cur = lambda n: (jnp.minimum(n, nb - 1), 0)

    def body(q_ref, kvc_ref, kvp_ref, s_ref, lse_ref, dy_ref, dq_ref, dkv_ref, ds_ref, carry, top, bot):
        n = pl.program_id(0)

        @pl.when(n == 0)
        def _():
            carry[...] = jnp.zeros_like(carry)
            ds_ref[...] = jnp.zeros_like(ds_ref)

        @pl.when(n < nb)
        def _():
            mask_c = _attn_mask()
            valid = jnp.logical_or(mask_c, n > 0)
            for g in range(KV_HEADS):
                ks = slice(g * HEAD_DIM, (g + 1) * HEAD_DIM)
                vs = slice(KV_WIDTH + g * HEAD_DIM, KV_WIDTH + (g + 1) * HEAD_DIM)
                (kc, vc), (kp, vp) = _kv_parts(kvc_ref, g), _kv_parts(kvp_ref, g)
                qt = _heads_transposed(q_ref, g, ATTN_SCALE)
                dot = _heads_transposed(dy_ref, g)
                lse = jnp.concatenate([lse_ref[g * GROUP + i:g * GROUP + i + 1, :] for i in range(GROUP)], axis=1)
                p = jnp.where(valid, jnp.exp(jnp.where(mask_c, _nn(kc, qt), _nn(kp, qt)) - lse), 0.0)
                dp = jnp.where(mask_c, _nn(vc, dot), _nn(vp, dot))
                delta = jnp.sum(p * dp, axis=0, keepdims=True)
                ds = p * (dp - delta)
                ds_c, p_c = jnp.where(mask_c, ds, 0.0), jnp.where(mask_c, p, 0.0)
                ds_p, p_p = ds - ds_c, p - p_c
                _heads_back(dq_ref, g, (_tn(kc, ds_c) + _tn(kp, ds_p)) * ATTN_SCALE)
                bot[:, ks], bot[:, vs] = _nt(ds_c, qt), _nt(p_c, dot)
                top[:, ks], top[:, vs] = _nt(ds_p, qt), _nt(p_p, dot)
                ds_ref[g:g + 1, :] -= jnp.exp(s_ref[g:g + 1, :] - lse) * delta
            dkv_ref[...] = (carry[...] + top[...]).astype(dkv_ref.dtype)
            carry[...] = bot[...]

        @pl.when(n == nb)
        def _():
            dkv_ref[...] = carry[...].astype(dkv_ref.dtype)

    return _call(
        "attn_bwd", body, grid=(nb + 1,),
        in_specs=[pl.BlockSpec((ATTN_BLOCK, D_MODEL), cur),
                  pl.BlockSpec((ATTN_BLOCK, 2 * KV_WIDTH), cur),
                  pl.BlockSpec((ATTN_BLOCK, 2 * KV_WIDTH), lambda n: (jnp.maximum(jnp.minimum(n, nb - 1) - 1, 0), 0)),
                  pl.BlockSpec((KV_HEADS, GROUP_LANES), lambda n: (0, 0)),
                  pl.BlockSpec((Q_HEADS, ATTN_BLOCK), lambda n: (0, jnp.minimum(n, nb - 1))),
                  pl.BlockSpec((ATTN_BLOCK, D_MODEL), cur)],
        out_specs=[pl.BlockSpec((ATTN_BLOCK, D_MODEL), cur),
                   pl.BlockSpec((ATTN_BLOCK, 2 * KV_WIDTH), lambda n: (jnp.maximum(n - 1, 0), 0)),
                   pl.BlockSpec((KV_HEADS, GROUP_LANES), lambda n: (0, 0))],
        out_shape=[jax.ShapeDtypeStruct((T, D_MODEL), MXU_DTYPE),
                   jax.ShapeDtypeStruct((T, 2 * KV_WIDTH), MXU_DTYPE),
                   jax.ShapeDtypeStruct((KV_HEADS, GROUP_LANES), F32)],
        scratch=[pltpu.VMEM((ATTN_BLOCK, 2 * KV_WIDTH), F32)] * 3,
        args=[pq, pkv, pkv, _sink_rows(sinks), lse, dy], semantics=("arbitrary",), comm=comm)


def _lower_bound(l):
    m = jnp.maximum(l[0:1], l[1:2])
    e0, e1 = jnp.exp(l[0:1] - m), jnp.exp(l[1:2] - m)
    return e0 / (e0 + e1)


def _tri(lower):
    r = lax.broadcasted_iota(jnp.int32, (CHUNK, CHUNK), 0)
    c = lax.broadcasted_iota(jnp.int32, (CHUNK, CHUNK), 1)
    return (r >= c) if lower else (c >= r)


def _chunk_sum(mask, v):
    ones = mask.astype(BF16)
    hi = v.astype(BF16)
    rest = v - hi.astype(F32)
    mid = rest.astype(BF16)
    lo = (rest - mid.astype(F32)).astype(BF16)
    part = lambda t: lax.dot_general(ones, t, (((1,), (0,)), ((), ())), preferred_element_type=F32)
    return part(hi) + part(mid) + part(lo)


def _hgrn_chunk_inputs(hq, hf, lb, causal):
    half_t = 0.5 * jnp.tanh(0.5 * hf)
    sg, sgn = 0.5 + half_t, 0.5 - half_t
    f = lb + (1.0 - lb) * sg
    kk = (1.0 - lb) * sgn
    sq = _sigmoid(hq)
    q = hq * sq
    b = _chunk_sum(causal, jnp.log(f))
    bm, bl = b[CHUNK // 2 - 1:CHUNK // 2, :], b[CHUNK - 1:CHUNK, :]
    e_qm, e_km = jnp.exp(b - bm), jnp.exp(bm - b)
    e_qs, e_kl = e_qm * jnp.exp(bm), e_km * jnp.exp(bl - bm)
    return dict(sg=sg, sgn=sgn, f=f, kk=kk, sq=sq, q=q, e_qm=e_qm, e_km=e_km, e_qs=e_qs, e_kl=e_kl,
                qm=q * e_qm, km=kk * e_km, qs=q * e_qs, kl=kk * e_kl, el=jnp.exp(bl))


def _hgrn_fwd(ph, lb_logits, norm_g, comm=None):
    T = ph.shape[0]
    nblk, cpb = T // HGRN_TOKENS, HGRN_TOKENS // CHUNK
    col = lambda c: pl.BlockSpec((HGRN_TOKENS, D_MODEL), functools.partial(lambda i, c: (i, c), c=c))

    def body(hq_ref, hf_ref, hi_ref, hg_ref, l_ref, ng_ref, y_ref, o_ref, st_ref, s_ref):
        @pl.when(pl.program_id(0) == 0)
        def _():
            s_ref[...] = jnp.zeros_like(s_ref)

        lb = _lower_bound(l_ref[...])
        causal = _tri(True)
        for c in range(cpb):
            rows = slice(c * CHUNK, (c + 1) * CHUNK)
            t = _hgrn_chunk_inputs(hq_ref[rows, :], hf_ref[rows, :], lb, causal)
            qm, km, qs, kl = (t[n].astype(MXU_DTYPE) for n in ("qm", "km", "qs", "kl"))
            v = hi_ref[rows, :].astype(MXU_DTYPE)
            for h in range(HGRN_HEADS):
                ls = slice(h * HGRN_K, (h + 1) * HGRN_K)
                st = s_ref[h]
                st_ref[c, ls, :] = st
                a = jnp.where(causal, _nt(qm[:, ls], km[:, ls]), 0.0)
                o_ref[rows, ls] = _nn(a, v[:, ls]) + _nt(qs[:, ls], st)
                s_ref[h] = t["el"][:, ls] * st + _tn(v[:, ls], kl[:, ls])
        for h in range(HGRN_HEADS):
            ls = slice(h * HGRN_K, (h + 1) * HGRN_K)
            o = o_ref[:, ls]
            r = lax.rsqrt(jnp.mean(o * o, axis=-1, keepdims=True) + EPS)
            y_ref[:, ls] = (o * r * ng_ref[:, ls] * _sigmoid(hg_ref[:, ls])).astype(y_ref.dtype)

    return _call(
        "hgrn_fwd", body, grid=(nblk,),
        in_specs=[col(0), col(1), col(2), col(3),
                  pl.BlockSpec((2, D_MODEL), lambda i: (0, 0)), pl.BlockSpec((1, D_MODEL), lambda i: (0, 0))],
        out_specs=[pl.BlockSpec((HGRN_TOKENS, D_MODEL), lambda i: (i, 0)),
                   pl.BlockSpec((HGRN_TOKENS, D_MODEL), lambda i: (i, 0)),
                   pl.BlockSpec((cpb, D_MODEL, HGRN_K), lambda i: (i, 0, 0))],
        out_shape=[jax.ShapeDtypeStruct((T, D_MODEL), MXU_DTYPE), jax.ShapeDtypeStruct((T, D_MODEL), F32),
                   jax.ShapeDtypeStruct((T // CHUNK, D_MODEL, HGRN_K), F32)],
        scratch=[pltpu.VMEM((HGRN_HEADS, HGRN_K, HGRN_K), F32)],
        args=[ph, ph, ph, ph, lb_logits, norm_g], semantics=("arbitrary",), comm=comm)


def _hgrn_bwd(ph, o_raw, states, dy, lb_logits, norm_g, comm=None):
    T = ph.shape[0]
    nblk, cpb = T // HGRN_TOKENS, HGRN_TOKENS // CHUNK
    rev = lambda i: nblk - 1 - i
    col = lambda c: pl.BlockSpec((HGRN_TOKENS, D_MODEL), functools.partial(lambda i, c: (rev(i), c), c=c))
    tok = pl.BlockSpec((HGRN_TOKENS, D_MODEL), lambda i: (rev(i), 0))

    def body(hq_ref, hf_ref, hi_ref, hg_ref, o_ref, st_ref, dy_ref, l_ref, ng_ref,
             dph_ref, dng_ref, dl_ref, dst_ref, dlb_ref, do_s, dqm_s, dkm_s, dqs_s, dkl_s, dv_s, del_s):
        i = pl.program_id(0)

        @pl.when(i == 0)
        def _():
            dst_ref[...] = jnp.zeros_like(dst_ref)
            dlb_ref[...] = jnp.zeros_like(dlb_ref)
            dng_ref[...] = jnp.zeros_like(dng_ref)

        lb = _lower_bound(l_ref[...])
        causal, anti = _tri(True), _tri(False)
        row = lax.broadcasted_iota(jnp.int32, (CHUNK, D_MODEL), 0)
        for c in reversed(range(cpb)):
            rows = slice(c * CHUNK, (c + 1) * CHUNK)
            hq = hq_ref[rows, :]
            t = _hgrn_chunk_inputs(hq, hf_ref[rows, :], lb, causal)
            sgg = _sigmoid(hg_ref[rows, :])
            dyv = dy_ref[rows, :]
            for h in range(HGRN_HEADS):
                ls = slice(h * HGRN_K, (h + 1) * HGRN_K)
                o = o_ref[rows, ls]
                r = lax.rsqrt(jnp.mean(o * o, axis=-1, keepdims=True) + EPS)
                nrm = o * r
                g_h = sgg[:, ls]
                dph_ref[rows, 3 * D_MODEL + h * HGRN_K:3 * D_MODEL + (h + 1) * HGRN_K] = (
                    dyv[:, ls] * nrm * ng_ref[:, ls] * g_h * (1.0 - g_h)).astype(dph_ref.dtype)
                dyg = dyv[:, ls] * g_h
                dng_ref[:, ls] += jnp.sum(dyg * nrm, axis=0, keepdims=True)
                dn = dyg * ng_ref[:, ls]
                do_s[:, ls] = r * (dn - nrm * jnp.mean(dn * nrm, axis=-1, keepdims=True))
            qm, km, qs, kl = (t[n].astype(MXU_DTYPE) for n in ("qm", "km", "qs", "kl"))
            v = hi_ref[rows, :].astype(MXU_DTYPE)
            do = do_s[...].astype(MXU_DTYPE)
            for h in range(HGRN_HEADS):
                ls = slice(h * HGRN_K, (h + 1) * HGRN_K)
                st = st_ref[c, ls, :]
                dst = dst_ref[h]
                a = jnp.where(causal, _nt(qm[:, ls], km[:, ls]), 0.0)
                da = jnp.where(causal, _nt(do[:, ls], v[:, ls]), 0.0)
                dv_s[:, ls] = _tn(a, do[:, ls]) + _nt(kl[:, ls], dst)
                dkl_s[:, ls] = _nn(v[:, ls], dst)
                dqs_s[:, ls] = _nn(do[:, ls], st)
                del_s[:, ls] = jnp.sum(dst * st, axis=0, keepdims=True)
                dst_ref[h] = _tn(do[:, ls], qs[:, ls]) + t["el"][:, ls] * dst
                dqm_s[:, ls] = _nn(da, km[:, ls])
                dkm_s[:, ls] = _tn(da, qm[:, ls])
            dqm, dkm, dqs, dkl = dqm_s[...], dkm_s[...], dqs_s[...], dkl_s[...]
            dq = dqm * t["e_qm"] + dqs * t["e_qs"]
            dk = dkm * t["e_km"] + dkl * t["e_kl"]
            t_qm, t_km, t_kl = dqm * t["qm"], dkm * t["km"], dkl * t["kl"]
            db = t_qm - t_km + dqs * t["qs"] - t_kl
            db_mid = jnp.sum(t_km - t_qm, axis=0, keepdims=True)
            db_last = jnp.sum(t_kl, axis=0, keepdims=True) + del_s[...] * t["el"]
            db = db + jnp.where(row == CHUNK // 2 - 1, db_mid, 0.0) + jnp.where(row == CHUNK - 1, db_last, 0.0)
            dlogf = _chunk_sum(anti, db)
            sq, sg, sgn, f = t["sq"], t["sg"], t["sgn"], t["f"]
            dph_ref[rows, 0:D_MODEL] = (dq * (sq * (1.0 + hq * (1.0 - sq)))).astype(dph_ref.dtype)
            dph_ref[rows, D_MODEL:2 * D_MODEL] = (
                dlogf * (1.0 - lb) * sg * (1.0 - sg) / f - dk * (1.0 - lb) * sgn * (1.0 - sgn)).astype(dph_ref.dtype)
            dph_ref[rows, 2 * D_MODEL:3 * D_MODEL] = dv_s[...].astype(dph_ref.dtype)
            dlb_ref[...] += jnp.sum(dlogf * (1.0 - sg) / f - dk * sgn, axis=0, keepdims=True)

        @pl.when(i == nblk - 1)
        def _():
            dl0 = dlb_ref[...] * lb * (1.0 - lb)
            dl_ref[0:1, :] = dl0
            dl_ref[1:2, :] = -dl0

    wide = pltpu.VMEM((CHUNK, D_MODEL), F32)
    return _call(
        "hgrn_bwd", body, grid=(nblk,),
        in_specs=[col(0), col(1), col(2), col(3), tok,
                  pl.BlockSpec((cpb, D_MODEL, HGRN_K), lambda i: (rev(i), 0, 0)), tok,
                  pl.BlockSpec((2, D_MODEL), lambda i: (0, 0)), pl.BlockSpec((1, D_MODEL), lambda i: (0, 0))],
        out_specs=[pl.BlockSpec((HGRN_TOKENS, 4 * D_MODEL), lambda i: (rev(i), 0)),
                   pl.BlockSpec((1, D_MODEL), lambda i: (0, 0)), pl.BlockSpec((2, D_MODEL), lambda i: (0, 0))],
        out_shape=[jax.ShapeDtypeStruct((T, 4 * D_MODEL), MXU_DTYPE), jax.ShapeDtypeStruct((1, D_MODEL), F32),
                   jax.ShapeDtypeStruct((2, D_MODEL), F32)],
        scratch=[pltpu.VMEM((HGRN_HEADS, HGRN_K, HGRN_K), F32), pltpu.VMEM((1, D_MODEL), F32),
                 wide, wide, wide, wide, wide, wide, pltpu.VMEM((1, D_MODEL), F32)],
        args=[ph, ph, ph, ph, o_raw, states, dy, lb_logits, norm_g], semantics=("arbitrary",), comm=comm)


def _local_step(x, target, vec, net):
    T, D = x.shape
    norm_mix_g, b_in, sinks, lb_logits = vec["norm_mix_g"], vec["b_in"], vec["attn_sinks"], vec["hgrn_lb_logits"]
    hgrn_norm_g, norm_ffn_g, norm_final_g = vec["hgrn_norm_g"], vec["norm_ffn_g"], vec["norm_final_g"]
    w_in = net.full("w_in")
    o_q, o_kv, o_h, o_g = (sum(IN_SPLITS[:i]) for i in range(4))
    w_q, w_kv, w_h, w_g = (_Rows(w_in, o, n) for o, n in zip((o_q, o_kv, o_h, o_g), IN_SPLITS))
    both = lambda acc: (acc, acc)
    grad_outs = [("mn", F32), ("mn", MXU_DTYPE)]
    row_vec = lambda n: ((1, n), lambda i, j: (0, j))
    tile = lambda tm, tn: ((tm, tn), lambda i, j: (i, j))
    TM = 512
    BIG = min(T, 1024)

    names = ("w_branch_attn", "w_branch_hgrn", "w_out")
    (u, pq, pkv, ph, pg), got = _in_proj(x, norm_mix_g, w_in, b_in, tm=256, comm=net.gather(names))
    net.gathered(names, got)
    names = ("w_ffn_gate",)
    (y_attn, lse), got = _attn_fwd(pq, pkv, sinks, comm=net.gather(names))
    net.gathered(names, got)
    names = ("w_ffn_up",)
    (y_hgrn, o_raw, states), got = _hgrn_fwd(ph, lb_logits, hgrn_norm_g, comm=net.gather(names))
    net.gathered(names, got)
    w_ba, w_bh, w_out = net.full("w_branch_attn"), net.full("w_branch_hgrn"), net.full("w_out")
    w_gate, w_up = net.full("w_ffn_gate"), net.full("w_ffn_up")
    ya, yb, merged = _merge_fwd(y_attn, y_hgrn, w_ba, w_bh, pg, tm=TM)

    FT = FFN // 2
    names = ("w_ffn_down",)
    (h1, u2, gpre, up, z), got = _ffn_fwd(merged, w_out, x, norm_ffn_g, w_gate, w_up, tm=256,
                                          comm=net.gather(names))
    net.gathered(names, got)
    w_down = net.full("w_ffn_down")

    def loss_head(acc, h1_v, tgt, g):
        h2 = h1_v + acc
        r = lax.rsqrt(jnp.mean(h2 * h2, axis=-1, keepdims=True) + EPS)
        xhat = h2 * r
        err = xhat * g - tgt
        part = 0.5 * jnp.sum(jnp.sum(err * err, axis=-1, keepdims=True), axis=0, keepdims=True) / D
        dyv = err / D
        dxh = dyv * g
        dh2 = r * (dxh - xhat * jnp.mean(dxh * xhat, axis=-1, keepdims=True))
        return dh2, dh2, jnp.sum(dyv * xhat, axis=0, keepdims=True), jnp.broadcast_to(part, (1, D))

    dh2, dh2b, dgf_p, loss_p = _matmul(
        "ffn_down", [(z, w_down)], "nn", tm=TM, tn=1024, tk=FFN,
        outs=[("mn", F32), ("mn", MXU_DTYPE), ("pn", F32), ("pn", F32)],
        extras=[(h1, *tile(TM, 1024)), (target, *tile(TM, 1024)), (norm_final_g, *row_vec(1024))],
        epilogue=loss_head)
    loss = jnp.sum(loss_p.reshape(-1, 8, D)[:, 0, 0])
    d_norm_final = _colsum_partials(dgf_p)

    def swiglu_bwd(acc, gv, upv):
        gv, upv = gv.astype(F32), upv.astype(F32)
        s = _sigmoid(gv)
        return acc * upv * (s * (1.0 + gv * (1.0 - s))), acc * (gv * s)

    dgp, dup = _matmul("d_ffn_hidden", [(dh2b, w_down)], "nt", tm=TM, tn=FT, tk=1024,
                       outs=[("mn", MXU_DTYPE), ("mn", MXU_DTYPE)],
                       extras=[(gpre, *tile(TM, FT)), (up, *tile(TM, FT))], epilogue=swiglu_bwd)
    d_w_down = _matmul("dw_down", [(z, dh2b)], "tn", tm=FT, tn=1024, tk=1024, outs=grad_outs, epilogue=both)

    def norm_ffn_bwd(acc, h1_v, g, dres):
        dx, dg = _rmsnorm_bwd_vals(acc, h1_v, g)
        dh = dres + dx
        return dh, dh, dg

    names = ("w_ffn_down",)
    (dh1, dh1b, dg2_p), got = _matmul(
        "d_ffn_in", [(dgp, w_gate), (dup, w_up)], "nn", tm=BIG, tn=1024, tk=FT,
        outs=[("mn", F32), ("mn", MXU_DTYPE), ("pn", F32)],
        extras=[(h1, *tile(BIG, 1024)), (norm_ffn_g, *row_vec(1024)), (dh2, *tile(BIG, 1024))],
        epilogue=norm_ffn_bwd,
        comm=net.exchange(dict(w_ffn_down=[d_w_down])))
    net.received(names, (), got)
    d_norm_ffn = _colsum_partials(dg2_p)
    d_w_gate = _matmul("dw_gate", [(dgp, u2)], "tn", tm=FT, tn=1024, tk=1024, outs=grad_outs, epilogue=both)
    d_w_up = _matmul("dw_up", [(dup, u2)], "tn", tm=FT, tn=1024, tk=1024, outs=grad_outs, epilogue=both)

    dya, dyb, dga, dgb, dy_attn, dy_hgrn = _merge_bwd(dh1b, w_out, w_ba, w_bh, ya, yb, pg, tm=TM)
    tn_grad = functools.partial(_matmul, mode="tn", tn=1024, tk=1024, outs=grad_outs, epilogue=both)
    d_w_out = tn_grad("dw_out", [(merged, dh1b)], tm=1024)
    d_w_ba = tn_grad("dw_branch_a", [(y_attn, dya)], tm=1024)
    d_w_bh = tn_grad("dw_branch_b", [(y_hgrn, dyb)], tm=1024)

    names, swap = ("w_ffn_gate",), ("w_ffn_down",)
    (dq, dkv, dsink), got = _attn_bwd(pq, pkv, sinks, lse, dy_attn,
                                      comm=net.exchange(dict(w_ffn_gate=[d_w_gate]), swap))
    net.received(names, swap, got)
    names, swap = ("w_ffn_up", "w_out", "w_branch_attn", "w_branch_hgrn"), ("w_ffn_gate",)
    (dph, d_hgrn_norm, d_lb_logits), got = _hgrn_bwd(
        ph, o_raw, states, dy_hgrn, lb_logits, hgrn_norm_g,
        comm=net.exchange(dict(w_ffn_up=[d_w_up], w_out=[d_w_out], w_branch_attn=[d_w_ba],
                               w_branch_hgrn=[d_w_bh]), swap))
    net.received(names, swap, got)

    d_w_in, db, rows_in = None, {}, sum(IN_SPLITS)
    for piece, dp, first, tm_p in (("q", dq, o_q, 1024), ("kv", dkv, o_kv, 256), ("h", dph, o_h, 1024),
                                   ("ga", dga, o_g, 1024), ("gb", dgb, o_g + D, 1024)):
        *d_w_in, db[piece] = tn_grad("dw_in_" + piece, [(dp, u)], tm=tm_p, a_colsum=True,
                                     into=_Into(rows_in, first, d_w_in))
    d_w_in = [tuple(d_w_in)]

    def norm_mix_bwd(acc, xin, g, dres):
        dx, dg = _rmsnorm_bwd_vals(acc, xin, g)
        return dres + dx, dg

    first_level = net.presum_begin("w_in", d_w_in)
    halves = net.presum_end("w_in", [] if first_level is None else _copies_alone("presum_swap_w_in", first_level))
    names, swap = ("w_in",), ("w_ffn_up", "w_out", "w_branch_attn", "w_branch_hgrn")
    (dx, dg1_p), got = _matmul(
        "d_u", [(dq, w_q), (dkv, w_kv), (dph, w_h), (dga, _Rows(w_in, o_g, D)), (dgb, _Rows(w_in, o_g + D, D))], "nn",
        tm=BIG, tn=1024, tk=512, outs=[("mn", F32), ("pn", F32)],
        extras=[(x, *tile(BIG, 1024)), (norm_mix_g, *row_vec(1024)), (dh1, *tile(BIG, 1024))],
        epilogue=norm_mix_bwd,
        comm=_join(halves, net.swap(swap)))
    net.last = (names, swap, got)
    d_norm_mix = _colsum_partials(dg1_p)
    d_b_in = jnp.concatenate([db[piece] for piece in ("q", "kv", "h", "ga", "gb")], axis=1)
    vecs = dict(norm_mix_g=d_norm_mix, b_in=d_b_in, attn_sinks=jnp.sum(dsink.reshape(Q_HEADS, ATTN_BLOCK), axis=1).reshape(1, Q_HEADS),
                hgrn_lb_logits=d_lb_logits,
                hgrn_norm_g=d_hgrn_norm, norm_ffn_g=d_norm_ffn, norm_final_g=d_norm_final)
    return loss, dx, vecs


def _place():
    return lax.axis_index("x"), lax.axis_index("y"), lax.axis_index("c")


def _other_chips(x, y):
    return [(1 - x, y), (x, 1 - y), (1 - x, 1 - y)]


def _y_first(copies):
    return [copies[3 * (i // 3) + (1, 0, 2)[i % 3]] for i in range(len(copies))]


def _gather_copies(shards):
    n = len(shards)

    def build(ins, outs, send_sems, recv_sems, local_sems):
        x, y, c = _place()
        mine = 2 * x + y
        local = [pltpu.make_async_copy(ins[w], outs[w].at[mine], local_sems.at[w]) for w in range(n)]
        sends, recvs = [], []
        for w in range(n):
            for k, (px, py) in enumerate(_other_chips(x, y)):
                sem = 3 * w + k
                sends.append(pltpu.make_async_remote_copy(
                    src_ref=ins[w], dst_ref=outs[w].at[mine], send_sem=send_sems.at[sem], recv_sem=recv_sems.at[sem],
                    device_id=(px, py, c), device_id_type=MESH_ID))
                recvs.append(pltpu.make_async_remote_copy(
                    src_ref=ins[w], dst_ref=outs[w].at[2 * px + py], send_sem=send_sems.at[sem],
                    recv_sem=recv_sems.at[sem], device_id=(px, py, c), device_id_type=MESH_ID))
        return sends, recvs, local, _y_first(sends)

    return _Carried(shards, [jax.ShapeDtypeStruct((N_CHIPS,) + s.shape, s.dtype) for s in shards], 3 * n, n, build)


def _grad_copies(stacked):
    n = len(stacked)

    def build(ins, outs, send_sems, recv_sems, local_sems):
        x, y, c = _place()
        sends = []
        for w in range(n):
            for k, (px, py) in enumerate(_other_chips(x, y)):
                sem = 3 * w + k
                sends.append(pltpu.make_async_remote_copy(
                    src_ref=ins[w].at[2 * px + py], dst_ref=outs[w].at[k], send_sem=send_sems.at[sem],
                    recv_sem=recv_sems.at[sem], device_id=(px, py, c), device_id_type=MESH_ID))
        return sends, sends, [], _y_first(sends)

    return _Carried(stacked, [jax.ShapeDtypeStruct((3,) + s.shape[1:], s.dtype) for s in stacked], 3 * n, 0, build)


def _small_copies(small):
    def build(ins, outs, send_sems, recv_sems, local_sems):
        small_ref, all_ref = ins[0], outs[0]
        x, y, c = _place()
        me = 4 * x + 2 * y + c
        sends, recvs = [], []
        for r in range(1, 8):
            px = 1 - x if r & 4 else x
            py = 1 - y if r & 2 else y
            pc = 1 - c if r & 1 else c
            sends.append(pltpu.make_async_remote_copy(
                src_ref=small_ref, dst_ref=all_ref.at[me], send_sem=send_sems.at[r - 1], recv_sem=recv_sems.at[r - 1],
                device_id=(px, py, pc), device_id_type=MESH_ID))
            recvs.append(pltpu.make_async_remote_copy(
                src_ref=small_ref, dst_ref=all_ref.at[4 * px + 2 * py + pc], send_sem=send_sems.at[r - 1],
                recv_sem=recv_sems.at[r - 1], device_id=(px, py, pc), device_id_type=MESH_ID))
        return sends, recvs, [pltpu.make_async_copy(small_ref, all_ref.at[me], local_sems.at[0])]

    return _Carried([small], [jax.ShapeDtypeStruct((8,) + small.shape, small.dtype)], 7, 1, build)


def _gather_by_neighbours(name, shard):
    half = shard.shape[0] // 2
    quarter = half // 2

    def body(in_ref, out_ref, send_sems, recv_sems, local_sem):
        for core in (0, 1):
            @pl.when(lax.axis_index("c") == core)
            def _():
                program(core, in_ref, out_ref, send_sems, recv_sems, local_sem)

    def program(c, in_ref, out_ref, send_sems, recv_sems, local_sem):
        x, y, _ = _place()
        chip = lambda px, py: 2 * px + py
        to_x, to_y, sibling = (1 - x, y, c), (x, 1 - y, c), (x, y, 1 - c)
        x_blk, y_blk, d_blk = chip(1 - x, y), chip(x, 1 - y), chip(1 - x, 1 - y)
        mine, theirs = c * half, (1 - c) * half

        def copy(sem, rows, block, to, src=None):
            place = out_ref.at[block, pl.ds(rows[0], rows[1])]
            return pltpu.make_async_remote_copy(
                src_ref=place if src is None else src, dst_ref=place, send_sem=send_sems.at[sem],
                recv_sem=recv_sems.at[sem], device_id=to, device_id_type=MESH_ID)

        own = pltpu.make_async_copy(in_ref, out_ref.at[chip(x, y)], local_sem)
        own.start()
        my_rows = in_ref.at[pl.ds(mine, half)]
        along_x = dict(send=copy(0, (mine, half), chip(x, y), to_x, src=my_rows),
                       landed=copy(0, (mine, half), x_blk, to_x),
                       onward=[copy(3, (mine + quarter, quarter), x_blk, to_y), copy(4, (mine, half), x_blk, sibling)],
                       diagonal=copy(2, (mine, quarter), d_blk, to_x))
        along_y = dict(send=copy(1, (mine, half), chip(x, y), to_y, src=my_rows),
                       landed=copy(1, (mine, half), y_blk, to_y),
                       onward=[copy(2, (mine, quarter), y_blk, to_x), copy(5, (mine, half), y_blk, sibling)],
                       diagonal=copy(3, (mine + quarter, quarter), d_blk, to_y))
        last = copy(6, (mine, half), d_blk, sibling)

        order = (along_x, along_y) if c == 0 else (along_y, along_x)
        for axis in order:
            axis["send"].start()
        for axis in order:
            axis["landed"].wait_recv()
            for cp in axis["onward"]:
                cp.start()
        for axis in order:
            axis["diagonal"].wait_recv()
        last.start()
        for sem, block in ((4, x_blk), (5, y_blk), (6, d_blk)):
            copy(sem, (theirs, half), block, sibling).wait_recv()
        for cp in [along_x["send"], along_y["send"]] + along_x["onward"] + along_y["onward"] + [last]:
            cp.wait_send()
        own.wait()

    return pl.pallas_call(
        body, name=name, in_specs=[HBM_SPEC], out_specs=HBM_SPEC,
        out_shape=jax.ShapeDtypeStruct((N_CHIPS,) + shard.shape, shard.dtype),
        scratch_shapes=[pltpu.SemaphoreType.DMA((7,)), pltpu.SemaphoreType.DMA((7,)), pltpu.SemaphoreType.DMA(())],
    )(shard)


def _copies_alone(name, comm):
    return _call(name, lambda: None, grid=(), in_specs=[], out_specs=[], out_shape=[], args=[], comm=comm)[1]


class _Net:
    def __init__(self, shards):
        self.shards = shards
        self.whole, self.own, self.theirs, self.sums, self.other = {}, {}, {}, {}, {}
        x, y, _ = _place()
        self.chip = 2 * x + y

    def gather(self, names):
        return _gather_copies([self.shards[n] for n in names])

    def gathered(self, names, got):
        for n, g in zip(names, got):
            self.whole[n] = g.reshape(-1, g.shape[-1])

    def full(self, name):
        return self.whole[name]

    def exchange(self, grads, swap=()):
        stacked = []
        for n, pieces in grads.items():
            (keep, send), = pieces
            self.own[n] = keep
            stacked.append(send.reshape(N_CHIPS, keep.shape[0] // N_CHIPS, send.shape[-1]))
        return _join(_grad_copies(stacked), self.swap(swap))

    def swap(self, names):
        return _sibling_copies([self.sums[n] for n in names]) if names else None

    def presum_begin(self, name, pieces):
        keep = jnp.concatenate([p[0] for p in pieces], axis=0) if len(pieces) > 1 else pieces[0][0]
        send = jnp.concatenate([p[1] for p in pieces], axis=0) if len(pieces) > 1 else pieces[0][1]
        rows = keep.shape[0] // N_CHIPS
        self.held = keep.reshape(N_CHIPS, rows, keep.shape[-1])
        return _half_rows_copies(send.reshape(N_CHIPS, rows, send.shape[-1]))

    def presum_end(self, name, got):
        x, y, c = _place()
        to_send, self.own[name] = _pre_sum("presum_" + name, self.held, got[0], jnp.stack([c, self.chip]))
        return _grad_copies([to_send])

    def received(self, names, swap, got, carried=None):
        self.theirs.update(zip(names, got[:len(names)]))
        self.other.update(zip(swap, got[len(names):]))
        for n in names:
            (self.sums[n],), more = _partial_sum("sum_" + n, self.own[n], self.theirs[n], self.chip, comm=carried)
        return more


def _half_rows_copies(stacked):
    n, rows = stacked.shape[0], stacked.shape[1] // 2

    def build(ins, outs, send_sems, recv_sems, local_sems):
        x, y, c = _place()
        copies = [pltpu.make_async_remote_copy(
            src_ref=ins[0].at[s, pl.ds((1 - c) * rows, rows)], dst_ref=outs[0].at[s], send_sem=send_sems.at[s],
            recv_sem=recv_sems.at[s], device_id=(x, y, 1 - c), device_id_type=MESH_ID) for s in range(n)]
        return copies, copies, []

    return _Carried([stacked], [jax.ShapeDtypeStruct((n, rows, stacked.shape[2]), stacked.dtype)], n, 0, build)


def _pre_sum(name, held, theirs, core_and_chip):
    n, R, C = held.shape
    half = R // 2
    tr = _row_tile(half)
    per_half = half // tr

    def body(place_ref, h_ref, t_ref, send_ref, own_ref):
        total = h_ref[0] + t_ref[0].astype(F32)
        send_ref[0] = total.astype(send_ref.dtype)

        @pl.when(pl.program_id(1) == place_ref[1])
        def _():
            own_ref[...] = total

    return pl.pallas_call(
        body, name=name,
        grid_spec=pltpu.PrefetchScalarGridSpec(
            num_scalar_prefetch=1, grid=(per_half, n),
            in_specs=[pl.BlockSpec((1, tr, C), lambda i, s, place: (s, place[0] * per_half + i, 0)),
                      pl.BlockSpec((1, tr, C), lambda i, s, place: (s, i, 0))],
            out_specs=[pl.BlockSpec((1, tr, C), lambda i, s, place: (s, i, 0)),
                       pl.BlockSpec((tr, C), lambda i, s, place: (i, 0))]),
        out_shape=[jax.ShapeDtypeStruct((n, half, C), MXU_DTYPE), jax.ShapeDtypeStruct((half, C), F32)],
        compiler_params=_params(("arbitrary", "arbitrary")),
    )(core_and_chip, held, theirs)


def _sibling_copies(parts):
    n = len(parts)

    def build(ins, outs, send_sems, recv_sems, local_sems):
        x, y, c = _place()
        copies = [pltpu.make_async_remote_copy(
            src_ref=ins[w], dst_ref=outs[w], send_sem=send_sems.at[w], recv_sem=recv_sems.at[w],
            device_id=(x, y, 1 - c), device_id_type=MESH_ID) for w in range(n)]
        return copies, copies, []

    return _Carried(parts, [jax.ShapeDtypeStruct(p.shape, p.dtype) for p in parts], n, 0, build)


def _row_tile(rows, most=512, sublanes=16):
    return max(t for t in range(sublanes, most + 1, sublanes) if rows % t == 0)


def _partial_sum(name, own, recv, chip, comm=None):
    _, R, C = recv.shape
    tr = _row_tile(R, most=128)

    def body(o_ref, r_ref, p_ref):
        p_ref[...] = ((o_ref[...] + r_ref[0].astype(F32)) + r_ref[1].astype(F32)) + r_ref[2].astype(F32)

    if own.shape[0] != R:
        assert comm is None and own.shape[0] == N_CHIPS * R
        total = pl.pallas_call(
            lambda chip_ref, *refs: body(*refs), name=name,
            grid_spec=pltpu.PrefetchScalarGridSpec(
                num_scalar_prefetch=1, grid=(R // tr,),
                in_specs=[pl.BlockSpec((tr, C), lambda i, chip_ref: (chip_ref[0] * (R // tr) + i, 0)),
                          pl.BlockSpec((3, tr, C), lambda i, chip_ref: (0, i, 0))],
                out_specs=pl.BlockSpec((tr, C), lambda i, chip_ref: (i, 0))),
            out_shape=jax.ShapeDtypeStruct((R, C), F32), compiler_params=_params(("parallel",)),
        )(chip.reshape(1), own, recv)
        return [total], []
    return _call(name, body, grid=(R // tr,),
                 in_specs=[pl.BlockSpec((tr, C), lambda i: (i, 0)), pl.BlockSpec((3, tr, C), lambda i: (0, i, 0))],
                 out_specs=[pl.BlockSpec((tr, C), lambda i: (i, 0))], out_shape=[jax.ShapeDtypeStruct((R, C), F32)],
                 args=[own, recv], semantics=("parallel",), comm=comm)


def _adam_vals(w, g, m, v):
    m = ADAM_B1 * m + (1.0 - ADAM_B1) * g
    v = ADAM_B2 * v + (1.0 - ADAM_B2) * (g * g)
    m_hat = m / (1.0 - ADAM_B1 ** ADAM_STEP)
    v_hat = v / (1.0 - ADAM_B2 ** ADAM_STEP)
    delta = -ADAM_LR * (m_hat / (jnp.sqrt(v_hat) + ADAM_EPS) + ADAM_WD * w)
    return delta, m, v


def _adamw(name, w, m, v, mine, other, comm=None):
    R, C = w.shape
    tr = _row_tile(R)

    def body(w_ref, m_ref, v_ref, s_ref, n_ref, g_ref, d_ref, nm_ref, nv_ref):
        g = s_ref[...] + n_ref[...]
        d, nm, nv = _adam_vals(w_ref[...], g, m_ref[...], v_ref[...])
        g_ref[...], d_ref[...], nm_ref[...], nv_ref[...] = g, d, nm, nv

    spec = pl.BlockSpec((tr, C), lambda i: (i, 0))
    return _call(name, body, grid=(R // tr,), in_specs=[spec] * 5, out_specs=[spec] * 4,
                 out_shape=[jax.ShapeDtypeStruct((R, C), F32)] * 4, args=[w, m, v, mine, other],
                 semantics=("parallel",), comm=comm)


def _adamw_by_halves(name, w, m, v, mine, other, core):
    R, C = w.shape
    tr = _row_tile(R // 2)
    per_half = R // 2 // tr

    def body(c_ref, w_ref, m_ref, v_ref, s_ref, n_ref, g_ref, d_ref, nm_ref, nv_ref):
        g = jnp.where(pl.program_id(0) // per_half == c_ref[0, 0], s_ref[...], n_ref[...])
        d, nm, nv = _adam_vals(w_ref[...], g, m_ref[...], v_ref[...])
        g_ref[...], d_ref[...], nm_ref[...], nv_ref[...] = g, d, nm, nv

    spec = pl.BlockSpec((tr, C), lambda i: (i, 0))
    part = pl.BlockSpec((tr, C), lambda i: (i % per_half, 0))
    return pl.pallas_call(
        body, name=name, grid=(R // tr,),
        in_specs=[pl.BlockSpec(memory_space=pltpu.SMEM), spec, spec, spec, part, part], out_specs=[spec] * 4,
        out_shape=[jax.ShapeDtypeStruct((R, C), F32)] * 4, compiler_params=_params(("parallel",)),
    )(core, w, m, v, mine, other)


SMALL_LAYOUT = dict(norm_mix_g=(0, 1, 1024), b_in=(1, 8, 7424), hgrn_norm_g=(9, 1, 1024), norm_ffn_g=(10, 1, 1024),
                    norm_final_g=(11, 1, 1024), hgrn_lb_logits=(12, 2, 2048), attn_sinks=(14, 1, 16))
SMALL_LOSS_ROW, SMALL_ROWS = 15, 16


def _pack_small(grads, loss):
    rows = [jnp.pad(grads[name].astype(F32).reshape(-1), (0, nrows * D_MODEL - n))
            for name, (_, nrows, n) in SMALL_LAYOUT.items()]
    rows.append(jnp.pad(loss.astype(F32).reshape(1), (0, D_MODEL - 1)))
    return jnp.concatenate(rows).reshape(SMALL_ROWS, D_MODEL)


def _adamw_small(w, m, v, g_all):
    names = list(SMALL_LAYOUT)
    n = len(names)

    def body(a_ref, *refs):
        ins, outs = refs[:3 * n], refs[3 * n:]
        g_all_rows = a_ref[0]
        for dev in range(1, 8):
            g_all_rows = g_all_rows + a_ref[dev]
        for i, name in enumerate(names):
            first, nrows, count = SMALL_LAYOUT[name]
            w_ref, m_ref, v_ref = ins[3 * i:3 * i + 3]
            if w_ref.shape[0] == nrows:
                g = g_all_rows[first:first + nrows, :w_ref.shape[1]]
            else:
                last = count - (nrows - 1) * D_MODEL
                g = jnp.concatenate([g_all_rows[r:r + 1, :] for r in range(first, first + nrows - 1)]
                                    + [g_all_rows[first + nrows - 1:first + nrows, :last]], axis=1)
            d, nm, nv = _adam_vals(w_ref[...], g, m_ref[...], v_ref[...])
            for o_ref, val in zip(outs[4 * i:4 * i + 4], (g, d, nm, nv)):
                o_ref[...] = val
        outs[4 * n][...] = g_all_rows[SMALL_LOSS_ROW:SMALL_LOSS_ROW + 1, 0:1]

    res = pl.pallas_call(
        body, name="adamw_small",
        out_shape=[jax.ShapeDtypeStruct(w[name].shape, F32) for name in names for _ in range(4)]
        + [jax.ShapeDtypeStruct((1, 1), F32)],
    )(g_all, *[t[name] for name in names for t in (w, m, v)])
    return {name: res[4 * i:4 * i + 4] for i, name in enumerate(names)}, res[4 * n]


MATRICES = ("w_in", "w_branch_attn", "w_branch_hgrn", "w_out", "w_ffn_gate", "w_ffn_up", "w_ffn_down")
COLUMN_SHARDED = ("w_in", "w_ffn_gate", "w_ffn_up")
WEIGHTS = ("norm_mix_g", "w_in", "b_in", "attn_sinks", "hgrn_lb_logits", "hgrn_norm_g", "w_branch_attn",
           "w_branch_hgrn", "w_out", "norm_ffn_g", "w_ffn_gate", "w_ffn_up", "w_ffn_down", "norm_final_g")


def kernel(x, norm_mix_g, w_in, b_in, attn_sinks, hgrn_lb_logits, hgrn_norm_g, w_branch_attn, w_branch_hgrn, w_out, norm_ffn_g, w_ffn_gate, w_ffn_up, w_ffn_down, norm_final_g, loss_target, m_norm_mix_g, m_w_in, m_b_in, m_attn_sinks, m_hgrn_lb_logits, m_hgrn_norm_g, m_w_branch_attn, m_w_branch_hgrn, m_w_out, m_norm_ffn_g, m_w_ffn_gate, m_w_ffn_up, m_w_ffn_down, m_norm_final_g, v_norm_mix_g, v_w_in, v_b_in, v_attn_sinks, v_hgrn_lb_logits, v_hgrn_norm_g, v_w_branch_attn, v_w_branch_hgrn, v_w_out, v_norm_ffn_g, v_w_ffn_gate, v_w_ffn_up, v_w_ffn_down, v_norm_final_g):
    given = dict(locals())
    w = {n: given[n] for n in WEIGHTS}
    m = {n: given["m_" + n] for n in WEIGHTS}
    v = {n: given["v_" + n] for n in WEIGHTS}

    block = lambda a, n: jnp.transpose(a[0]) if n in COLUMN_SHARDED else a[0]
    unblock = lambda a, n: (jnp.transpose(a) if n in COLUMN_SHARDED else a)[None]
    net = _Net({n: block(w[n], n).astype(MXU_DTYPE) for n in MATRICES})
    net.gathered(("w_in",), [_gather_by_neighbours("gather_w_in", net.shards["w_in"])])
    vec = dict(norm_mix_g=norm_mix_g, b_in=b_in, attn_sinks=attn_sinks, hgrn_lb_logits=hgrn_lb_logits,
               hgrn_norm_g=hgrn_norm_g, norm_ffn_g=norm_ffn_g, norm_final_g=norm_final_g.reshape(1, D_MODEL))
    loss_part, dx, d_vecs = _local_step(x[0], loss_target[0], vec, net)

    small_all, = net.received(*net.last, carried=_small_copies(_pack_small(d_vecs, loss_part)))
    grads, deltas, new_m, new_v = {}, {}, {}, {}
    for n in ("w_ffn_down", "w_ffn_gate", "w_ffn_up", "w_out", "w_branch_attn", "w_branch_hgrn"):
        res, got = _adamw("adamw_" + n, block(w[n], n), block(m[n], n), block(v[n], n), net.sums[n], net.other[n],
                          comm=net.swap(("w_in",)) if n == "w_ffn_down" else None)
        if n == "w_ffn_down":
            net.other["w_in"], = got
        grads[n], deltas[n], new_m[n], new_v[n] = (unblock(r, n) for r in res)
    n = "w_in"
    res = _adamw_by_halves("adamw_" + n, block(w[n], n), block(m[n], n), block(v[n], n), net.sums[n], net.other[n],
                           _place()[2].reshape(1, 1))
    grads[n], deltas[n], new_m[n], new_v[n] = (unblock(r, n) for r in res)
    rows = lambda t: {n: t[n].reshape(-1, t[n].shape[-1]) for n in SMALL_LAYOUT}
    res, loss = _adamw_small(rows(w), rows(m), rows(v), small_all)
    for n, four in res.items():
        grads[n], deltas[n], new_m[n], new_v[n] = (r.reshape(w[n].shape) for r in four)
    loss = loss.reshape(())
    return (loss, dx[None], *[grads[n] for n in WEIGHTS], *[deltas[n] for n in WEIGHTS],
            *[new_m[n] for n in WEIGHTS], *[new_v[n] for n in WEIGHTS])
```

```python
import collections
import functools
import math

import jax
import jax.numpy as jnp
from jax import lax
from jax.experimental import pallas as pl
from jax.experimental.pallas import tpu as pltpu

F32 = jnp.float32
BF16 = jnp.bfloat16
MXU_DTYPE = jnp.bfloat16
SAVED_DTYPE = jnp.bfloat16
MESH_ID = pl.DeviceIdType.MESH

D_MODEL = 1024
HEAD_DIM = 64
Q_HEADS = 16
KV_HEADS = 2
GROUP = Q_HEADS // KV_HEADS
KV_WIDTH = KV_HEADS * HEAD_DIM
ATTN_BLOCK = 128
HGRN_HEADS = 8
HGRN_K = 128
CHUNK = 64
HGRN_TOKENS = 256
FFN = 2816
IN_SPLITS = (1024, 256, 4096, 2048)
EPS = 1e-6
NEG_INF = -1e30
ADAM_LR, ADAM_B1, ADAM_B2, ADAM_EPS, ADAM_WD, ADAM_STEP = 0.001, 0.9, 0.999, 1e-08, 0.01, 10
N_CHIPS = 4
VMEM_LIMIT = 60 * 1024 * 1024
ROW_ALIGN = 16


def _params(sem=None):
    return pltpu.CompilerParams(dimension_semantics=sem, vmem_limit_bytes=VMEM_LIMIT)


def _sigmoid(v):
    return 0.5 * jnp.tanh(0.5 * v) + 0.5


def _dot(a, b, dims):
    return lax.dot_general(a.astype(MXU_DTYPE), b.astype(MXU_DTYPE), (dims, ((), ())),
                           preferred_element_type=F32)


def _nn(a, b):
    return _dot(a, b, ((1,), (0,)))


def _nt(a, b):
    return _dot(a, b, ((1,), (1,)))


def _tn(a, b):
    return _dot(a, b, ((0,), (0,)))


HBM_SPEC = pl.BlockSpec(memory_space=pl.ANY)


class _Carried:
    def __init__(self, arrays, out_shapes, n_remote, n_local, build):
        self.parts = [(len(arrays), len(out_shapes), build)]
        self.arrays, self.out_shapes = list(arrays), list(out_shapes)
        self.scratch = [pltpu.SemaphoreType.DMA((n_remote,)), pltpu.SemaphoreType.DMA((n_remote,)),
                        pltpu.SemaphoreType.DMA((max(n_local, 1),))]

    def __add__(self, other):
        both = _Carried([], [], 1, 0, None)
        both.parts = self.parts + other.parts
        both.arrays, both.out_shapes = self.arrays + other.arrays, self.out_shapes + other.out_shapes
        both.scratch = self.scratch + other.scratch
        return both

    def _built(self, ins, outs, sems):
        for p, (ni, no, build) in enumerate(self.parts):
            yield build(ins[:ni], outs[:no], *sems[3 * p:3 * p + 3])
            ins, outs = ins[ni:], outs[no:]

    def start(self, ins, outs, sems):
        core = lax.axis_index("c")
        for sends, _, local, *other_order in self._built(ins, outs, sems):
            for cp in local:
                cp.start()
            if not other_order:
                for cp in sends:
                    cp.start()
                continue

            @pl.when(core == 0)
            def _():
                for cp in sends:
                    cp.start()

            @pl.when(core == 1)
            def _():
                for cp in other_order[0]:
                    cp.start()

    def wait(self, ins, outs, sems):
        for sends, recvs, local, *_ in self._built(ins, outs, sems):
            for cp in recvs:
                cp.wait_recv()
            for cp in sends:
                cp.wait_send()
            for cp in local:
                cp.wait()


def _join(*comms):
    comms = [c for c in comms if c is not None]
    return functools.reduce(lambda a, b: a + b, comms) if comms else None


def _call(name, body, *, grid, in_specs, out_specs, out_shape, args, scratch=(), semantics=None, comm=None,
          aliases=None):
    n_in, n_out, n_scr = len(in_specs), len(out_specs), len(scratch)
    aliases = aliases or {}
    if comm is None:
        res = pl.pallas_call(body, name=name, grid=grid, in_specs=in_specs, out_specs=out_specs, out_shape=out_shape,
                             scratch_shapes=list(scratch), input_output_aliases=aliases,
                             compiler_params=_params(semantics))(*args)
        return list(res), []
    ci, co = len(comm.arrays), len(comm.out_shapes)

    def carrying(*refs):
        ins, refs = refs[:n_in], refs[n_in:]
        c_ins, refs = refs[:ci], refs[ci:]
        outs, refs = refs[:n_out], refs[n_out:]
        c_outs, refs = refs[:co], refs[co:]
        scr, sems = refs[:n_scr], refs[n_scr:]
        if not grid:
            comm.start(c_ins, c_outs, sems)
            body(*ins, *outs, *scr)
            comm.wait(c_ins, c_outs, sems)
            return
        first = functools.reduce(jnp.logical_and, [pl.program_id(a) == 0 for a in range(len(grid))])
        last = functools.reduce(jnp.logical_and, [pl.program_id(a) == g - 1 for a, g in enumerate(grid)])

        @pl.when(first)
        def _():
            comm.start(c_ins, c_outs, sems)

        body(*ins, *outs, *scr)

        @pl.when(last)
        def _():
            comm.wait(c_ins, c_outs, sems)

    res = pl.pallas_call(
        carrying, name=name, grid=grid, in_specs=list(in_specs) + [HBM_SPEC] * ci,
        out_specs=list(out_specs) + [HBM_SPEC] * co, out_shape=list(out_shape) + comm.out_shapes,
        scratch_shapes=list(scratch) + comm.scratch, input_output_aliases=aliases,
        compiler_params=_params(("arbitrary",) * len(grid) if grid else None),
    )(*args, *comm.arrays)
    return list(res[:n_out]), list(res[n_out:])


_NO_COPIES = object()
_Rows = collections.namedtuple("_Rows", "array first rows")


_Into = collections.namedtuple("_Into", "rows first held")


def _matmul(name, pairs, mode, *, tm, tn, tk, outs, extras=(), epilogue=None, a_colsum=False, into=None,
            comm=_NO_COPIES):
    prod = dict(nn=_nn, nt=_nt, tn=_tn)[mode]
    pairs = [(a, b if isinstance(b, _Rows) else _Rows(b, 0, b.shape[0])) for a, b in pairs]
    a0, b0 = pairs[0]
    M = a0.shape[1] if mode == "tn" else a0.shape[0]
    N = b0.rows if mode == "nt" else b0.array.shape[1]
    steps, in_specs, offset = [], [], 0
    for a, b in pairs:
        K = a.shape[0] if mode == "tn" else a.shape[1]
        t = min(tk, K)
        assert K % t == 0, (name, K, t)
        kmap = functools.partial(lambda k, off, n: jnp.clip(k - off, 0, n - 1), off=offset, n=K // t)
        if mode == "tn":
            in_specs.append(pl.BlockSpec((t, tm), functools.partial(lambda i, j, k, f: (f(k), i), f=kmap)))
        else:
            in_specs.append(pl.BlockSpec((tm, t), functools.partial(lambda i, j, k, f: (i, f(k)), f=kmap)))
        whole = b.first == 0 and b.rows == b.array.shape[0]
        if mode == "nt":
            shape = (tn, t) if whole else (pl.Element(tn), pl.Element(t))
            in_specs.append(pl.BlockSpec(shape, functools.partial(
                lambda i, j, k, f, b, t, whole: (j, f(k)) if whole else (
                    pl.multiple_of(b.first + j * tn, ROW_ALIGN), pl.multiple_of(f(k) * t, 128)),
                f=kmap, b=b, t=t, whole=whole)))
        else:
            assert b.rows == K, (name, b.rows, K)
            shape = (t, tn) if whole else (pl.Element(t), pl.Element(tn))
            in_specs.append(pl.BlockSpec(shape, functools.partial(
                lambda i, j, k, f, b, t, whole: (f(k), j) if whole else (
                    pl.multiple_of(b.first + f(k) * t, ROW_ALIGN), pl.multiple_of(j * tn, 128)),
                f=kmap, b=b, t=t, whole=whole)))
        steps.append((offset, offset + K // t))
        offset += K // t
    assert M % tm == 0 and N % tn == 0, (name, M, N, tm, tn)
    ni, nj, nk = M // tm, N // tn, offset
    npair, ne, no = len(pairs), len(extras), len(outs)
    if epilogue is None:
        epilogue = lambda acc: (acc,)

    def finish(acc, extra_refs, out_refs):
        vals = epilogue(acc, *[r[...] for r in extra_refs])
        for (kind, _), o_ref, val in zip(outs, out_refs, vals):
            if kind == "pn":
                val = jnp.broadcast_to(val, o_ref.shape)
            o_ref[...] = val.astype(o_ref.dtype)

    held = list(into.held) if into is not None and into.held is not None else []

    def body(*refs):
        ab, rest = refs[:2 * npair], refs[2 * npair:]
        extra_refs, rest = rest[:ne], rest[ne + len(held):]
        out_refs = rest[:no]
        if nk == 1:
            finish(prod(ab[0][...], ab[1][...]), extra_refs, out_refs)
            return
        sums_ref, acc_ref = rest[no], rest[-1]
        k = pl.program_id(2)

        @pl.when(k == 0)
        def _():
            acc_ref[...] = jnp.zeros_like(acc_ref)
            if a_colsum:
                sums_ref[...] = jnp.zeros((1, tm), F32)

        if a_colsum:
            sums_ref[...] += jnp.sum(ab[0][...].astype(F32), axis=0, keepdims=True)
        for p, (lo, hi) in enumerate(steps):
            @pl.when(jnp.logical_and(k >= lo, k < hi))
            def _():
                acc_ref[...] += prod(ab[2 * p][...], ab[2 * p + 1][...])

        @pl.when(k == nk - 1)
        def _():
            finish(acc_ref[...], extra_refs, out_refs)

    for _, shape, im in extras:
        in_specs.append(pl.BlockSpec(shape, functools.partial(lambda i, j, k, im: im(i, j), im=im)))
    in_specs += [HBM_SPEC] * len(held)
    aliases = {2 * npair + ne + p: p for p in range(len(held))}
    out_shape, out_specs = [], []
    for kind, dt in outs:
        if kind == "mn" and into is not None:
            out_shape.append(jax.ShapeDtypeStruct((into.rows, N), dt))
            out_specs.append(pl.BlockSpec((pl.Element(tm), pl.Element(tn)), lambda i, j, k: (
                pl.multiple_of(into.first + i * tm, ROW_ALIGN), pl.multiple_of(j * tn, 128))))
        elif kind == "mn":
            out_shape.append(jax.ShapeDtypeStruct((M, N), dt))
            out_specs.append(pl.BlockSpec((tm, tn), lambda i, j, k: (i, j)))
        else:
            out_shape.append(jax.ShapeDtypeStruct((8 * ni, N), dt))
            out_specs.append(pl.BlockSpec((8, tn), lambda i, j, k: (i, j)))
    if a_colsum:
        assert mode == "tn" and npair == 1 and nj == 1 and nk > 1, name
        out_shape.append(jax.ShapeDtypeStruct((1, M), F32))
        out_specs.append(pl.BlockSpec((1, tm), lambda i, j, k: (0, i)))
    grid = (ni, nj, nk)
    if nj > 1 and nk == 1:
        turned = lambda spec: pl.BlockSpec(spec.block_shape, functools.partial(
            lambda j, i, k, im: im(i, j, k), im=spec.index_map))
        in_specs, out_specs, grid = [turned(s) for s in in_specs], [turned(s) for s in out_specs], (nj, ni, nk)
    res, got = _call(name, body, grid=grid, in_specs=in_specs, out_specs=out_specs, out_shape=out_shape,
                     args=[t for a, b in pairs for t in (a, b.array)] + [e[0] for e in extras] + held,
                     scratch=[pltpu.VMEM((tm, tn), F32)] if nk > 1 else [], aliases=aliases,
                     semantics=("parallel", "parallel", "arbitrary"), comm=None if comm is _NO_COPIES else comm)
    return res if comm is _NO_COPIES else (res, got)


def _ffn_fwd(merged, w_out, x, gain, w_gate_t, w_up_t, *, tm, comm=None):
    (T, D), F = x.shape, w_gate_t.shape[0]

    def body(m_ref, wo_ref, x_ref, g_ref, wg_ref, wu_ref, h_ref, u_ref, gate_ref, up_ref, z_ref):
        h = x_ref[...] + _nn(m_ref[...], wo_ref[...])
        h_ref[...] = h
        u = (h * lax.rsqrt(jnp.mean(h * h, axis=-1, keepdims=True) + EPS) * g_ref[...]).astype(u_ref.dtype)
        u_ref[...] = u
        gate, up = _nt(u, wg_ref[...]), _nt(u, wu_ref[...])
        gate_ref[...], up_ref[...] = gate.astype(gate_ref.dtype), up.astype(up_ref.dtype)
        z_ref[...] = (gate * _sigmoid(gate) * up).astype(z_ref.dtype)

    rows = lambda n: pl.BlockSpec((tm, n), lambda i: (i, 0))
    fixed = lambda a: pl.BlockSpec(a.shape, lambda i: (0, 0))
    return _call("ffn_hidden", body, grid=(T // tm,),
                 in_specs=[rows(D), fixed(w_out), rows(D), fixed(gain), fixed(w_gate_t), fixed(w_up_t)],
                 out_specs=[rows(D), rows(D), rows(F), rows(F), rows(F)],
                 out_shape=[jax.ShapeDtypeStruct((T, D), F32), jax.ShapeDtypeStruct((T, D), MXU_DTYPE)]
                 + [jax.ShapeDtypeStruct((T, F), SAVED_DTYPE)] * 2 + [jax.ShapeDtypeStruct((T, F), MXU_DTYPE)],
                 args=[merged, w_out, x, gain, w_gate_t, w_up_t], semantics=("parallel",), comm=comm)


def _in_proj(x, gain, w_in_t, b_in, *, tm, comm=None):
    T, D = x.shape
    bounds = [sum(IN_SPLITS[:i]) for i in range(len(IN_SPLITS) + 1)]

    def body(x_ref, g_ref, w_ref, b_ref, u_ref, *piece_refs):
        xv = x_ref[...]
        r = lax.rsqrt(jnp.mean(xv * xv, axis=-1, keepdims=True) + EPS)
        u = (xv * r * g_ref[...]).astype(u_ref.dtype)
        u_ref[...] = u
        for o_ref, lo, hi in zip(piece_refs, bounds[:-1], bounds[1:]):
            o_ref[...] = (_nt(u, w_ref[lo:hi, :]) + b_ref[:, lo:hi]).astype(o_ref.dtype)

    rows = lambda n: pl.BlockSpec((tm, n), lambda i: (i, 0))
    fixed = lambda a: pl.BlockSpec(a.shape, lambda i: (0, 0))
    dtypes = (MXU_DTYPE, MXU_DTYPE, F32, F32)
    return _call("in_proj", body, grid=(T // tm,),
                 in_specs=[rows(D), fixed(gain), fixed(w_in_t), fixed(b_in)],
                 out_specs=[rows(D)] + [rows(n) for n in IN_SPLITS],
                 out_shape=[jax.ShapeDtypeStruct((T, D), MXU_DTYPE)]
                 + [jax.ShapeDtypeStruct((T, n), dt) for n, dt in zip(IN_SPLITS, dtypes)],
                 args=[x, gain, w_in_t, b_in], semantics=("parallel",), comm=comm)


def _row_spec(tm, n):
    return pl.BlockSpec((tm, n), lambda i: (i, 0))


def _fixed_spec(a):
    return pl.BlockSpec(a.shape, lambda i: (0,) * a.ndim)


def _partials_spec(n):
    return pl.BlockSpec((8, n), lambda i: (i, 0))


def _ffn_tail(z, w_down, h1, target, gain, gate, up, *, tm):
    (T, F), D = z.shape, h1.shape[1]

    def body(z_ref, w_ref, h_ref, t_ref, g_ref, gate_ref, up_ref, dh_ref, dhb_ref, dgate_ref, dup_ref, dg_ref, l_ref):
        h2 = h_ref[...] + _nn(z_ref[...], w_ref[...])
        r = lax.rsqrt(jnp.mean(h2 * h2, axis=-1, keepdims=True) + EPS)
        xhat = h2 * r
        err = xhat * g_ref[...] - t_ref[...]
        part = 0.5 * jnp.sum(jnp.sum(err * err, axis=-1, keepdims=True), axis=0, keepdims=True) / D
        dy = err / D
        dxh = dy * g_ref[...]
        dh2 = r * (dxh - xhat * jnp.mean(dxh * xhat, axis=-1, keepdims=True))
        dh_ref[...] = dh2
        dhb = dh2.astype(dhb_ref.dtype)
        dhb_ref[...] = dhb
        dg_ref[...] = jnp.broadcast_to(jnp.sum(dy * xhat, axis=0, keepdims=True), dg_ref.shape)
        l_ref[...] = jnp.broadcast_to(part, l_ref.shape)
        dz = _nt(dhb, w_ref[...])
        gv, upv = gate_ref[...].astype(F32), up_ref[...].astype(F32)
        s = _sigmoid(gv)
        dgate_ref[...] = (dz * upv * (s * (1.0 + gv * (1.0 - s)))).astype(dgate_ref.dtype)
        dup_ref[...] = (dz * (gv * s)).astype(dup_ref.dtype)

    low = lambda n: jax.ShapeDtypeStruct((T, n), MXU_DTYPE)
    part = jax.ShapeDtypeStruct((8 * (T // tm), D), F32)
    return pl.pallas_call(
        body, name="ffn_tail", grid=(T // tm,),
        in_specs=[_row_spec(tm, F), _fixed_spec(w_down), _row_spec(tm, D), _row_spec(tm, D), _fixed_spec(gain),
                  _row_spec(tm, F), _row_spec(tm, F)],
        out_specs=[_row_spec(tm, D), _row_spec(tm, D), _row_spec(tm, F), _row_spec(tm, F), _partials_spec(D),
                   _partials_spec(D)],
        out_shape=[jax.ShapeDtypeStruct((T, D), F32), low(D), low(F), low(F), part, part],
        compiler_params=_params(("parallel",)),
    )(z, w_down, h1, target, gain, gate, up)


def _ffn_in_bwd(dgate, dup, w_gate_t, w_up_t, h1, gain, dres, *, tm, comm=None):
    (T, F), D = dgate.shape, h1.shape[1]

    def body(dg_ref, du_ref, wg_ref, wu_ref, h_ref, g_ref, r_ref, dh_ref, dhb_ref, dgain_ref):
        d_u2 = _nn(dg_ref[...], wg_ref[...]) + _nn(du_ref[...], wu_ref[...])
        dx, dgain = _rmsnorm_bwd_vals(d_u2, h_ref[...], g_ref[...])
        dh = r_ref[...] + dx
        dh_ref[...] = dh
        dhb_ref[...] = dh.astype(dhb_ref.dtype)
        dgain_ref[...] = jnp.broadcast_to(dgain, dgain_ref.shape)

    return _call("d_ffn_in", body, grid=(T // tm,),
                 in_specs=[_row_spec(tm, F), _row_spec(tm, F), _fixed_spec(w_gate_t), _fixed_spec(w_up_t),
                           _row_spec(tm, D), _fixed_spec(gain), _row_spec(tm, D)],
                 out_specs=[_row_spec(tm, D), _row_spec(tm, D), _partials_spec(D)],
                 out_shape=[jax.ShapeDtypeStruct((T, D), F32), jax.ShapeDtypeStruct((T, D), MXU_DTYPE),
                            jax.ShapeDtypeStruct((8 * (T // tm), D), F32)],
                 args=[dgate, dup, w_gate_t, w_up_t, h1, gain, dres], semantics=("parallel",), comm=comm)


def _in_proj_bwd(pieces, w_in_t, x, gain, dres, *, tm, comm=None):
    T, D = x.shape
    n = len(pieces)

    def body(*refs):
        dps, (w_ref, x_ref, g_ref, r_ref, dx_ref, dgain_ref) = refs[:n], refs[n:]
        d_u = None
        for dp_ref, (dp, first) in zip(dps, pieces):
            term = _nn(dp_ref[...], w_ref[first:first + dp.shape[1], :])
            d_u = term if d_u is None else d_u + term
        dx, dgain = _rmsnorm_bwd_vals(d_u, x_ref[...], g_ref[...])
        dx_ref[...] = r_ref[...] + dx
        dgain_ref[...] = jnp.broadcast_to(dgain, dgain_ref.shape)

    return _call("d_u", body, grid=(T // tm,),
                 in_specs=[_row_spec(tm, dp.shape[1]) for dp, _ in pieces]
                 + [_fixed_spec(w_in_t), _row_spec(tm, D), _fixed_spec(gain), _row_spec(tm, D)],
                 out_specs=[_row_spec(tm, D), _partials_spec(D)],
                 out_shape=[jax.ShapeDtypeStruct((T, D), F32), jax.ShapeDtypeStruct((8 * (T // tm), D), F32)],
                 args=[dp for dp, _ in pieces] + [w_in_t, x, gain, dres], semantics=("parallel",), comm=comm)


def _merge_fwd(y_a, y_b, w_a, w_b, gates, *, tm):
    T, D = y_a.shape

    def body(ya_ref, yb_ref, wa_ref, wb_ref, ga_ref, gb_ref, pa_ref, pb_ref, m_ref):
        pa, pb = _nn(ya_ref[...], wa_ref[...]), _nn(yb_ref[...], wb_ref[...])
        pa_ref[...], pb_ref[...] = pa, pb
        m_ref[...] = (_sigmoid(ga_ref[...]) * pa + _sigmoid(gb_ref[...]) * pb).astype(m_ref.dtype)

    rows = pl.BlockSpec((tm, D), lambda i: (i, 0))
    whole = pl.BlockSpec((D, D), lambda i: (0, 0))
    return pl.pallas_call(
        body, name="branch_merge", grid=(T // tm,),
        in_specs=[rows, rows, whole, whole, rows, pl.BlockSpec((tm, D), lambda i: (i, 1))], out_specs=[rows] * 3,
        out_shape=[jax.ShapeDtypeStruct((T, D), F32)] * 2 + [jax.ShapeDtypeStruct((T, D), MXU_DTYPE)],
        compiler_params=_params(("parallel",)),
    )(y_a, y_b, w_a, w_b, gates, gates)


def _merge_bwd(dh, w_out, w_a, w_b, p_a, p_b, gates, *, tm):
    T, D = dh.shape

    def body(dh_ref, wo_ref, wa_ref, wb_ref, pa_ref, pb_ref, ga_ref, gb_ref, dpa_ref, dpb_ref, dga_ref, dgb_ref,
             dya_ref, dyb_ref):
        dm = _nt(dh_ref[...], wo_ref[...])
        sa, sb = _sigmoid(ga_ref[...]), _sigmoid(gb_ref[...])
        dpa, dpb = (dm * sa).astype(dpa_ref.dtype), (dm * sb).astype(dpb_ref.dtype)
        dpa_ref[...], dpb_ref[...] = dpa, dpb
        dga_ref[...] = (dm * pa_ref[...] * sa * (1.0 - sa)).astype(dga_ref.dtype)
        dgb_ref[...] = (dm * pb_ref[...] * sb * (1.0 - sb)).astype(dgb_ref.dtype)
        dya_ref[...] = _nt(dpa, wa_ref[...]).astype(dya_ref.dtype)
        dyb_ref[...] = _nt(dpb, wb_ref[...]).astype(dyb_ref.dtype)

    rows = pl.BlockSpec((tm, D), lambda i: (i, 0))
    whole = pl.BlockSpec((D, D), lambda i: (0, 0))
    low = jax.ShapeDtypeStruct((T, D), MXU_DTYPE)
    return pl.pallas_call(
        body, name="d_branch_merge", grid=(T // tm,),
        in_specs=[rows, whole, whole, whole, rows, rows, rows, pl.BlockSpec((tm, D), lambda i: (i, 1))],
        out_specs=[rows] * 6, out_shape=[low] * 5 + [jax.ShapeDtypeStruct((T, D), F32)],
        compiler_params=_params(("parallel",)),
    )(dh, w_out, w_a, w_b, p_a, p_b, gates, gates)


def _colsum_partials(p):
    return jnp.sum(p.reshape(-1, 8, p.shape[-1])[:, 0, :], axis=0, keepdims=True)


def _rmsnorm_bwd_vals(dy, xin, g):
    rstd = lax.rsqrt(jnp.mean(xin * xin, axis=-1, keepdims=True) + EPS)
    xhat = xin * rstd
    dg = jnp.sum(dy * xhat, axis=0, keepdims=True)
    dxh = dy * g
    dx = rstd * (dxh - xhat * jnp.mean(dxh * xhat, axis=-1, keepdims=True))
    return dx, dg


def _colsum(name, a, tr=512, comm=_NO_COPIES):
    T, N = a.shape

    def body(a_ref, o_ref):
        @pl.when(pl.program_id(0) == 0)
        def _():
            o_ref[...] = jnp.zeros_like(o_ref)

        o_ref[...] += jnp.sum(a_ref[...].astype(F32), axis=0, keepdims=True)

    (res,), got = _call(name, body, grid=(T // tr,), in_specs=[pl.BlockSpec((tr, N), lambda i: (i, 0))],
                        out_specs=[pl.BlockSpec((1, N), lambda i: (0, 0))],
                        out_shape=[jax.ShapeDtypeStruct((1, N), F32)], args=[a], semantics=("arbitrary",),
                        comm=None if comm is _NO_COPIES else comm)
    return res if comm is _NO_COPIES else (res, got)


ATTN_SCALE = 1.0 / math.sqrt(HEAD_DIM)
GROUP_LANES = GROUP * ATTN_BLOCK
PAIR = 2 * HEAD_DIM


def _attn_mask():
    kj = lax.broadcasted_iota(jnp.int32, (ATTN_BLOCK, GROUP_LANES), 0)
    qi = lax.broadcasted_iota(jnp.int32, (ATTN_BLOCK, GROUP_LANES), 1) & (ATTN_BLOCK - 1)
    return kj <= qi


def _heads_transposed(ref, g, scale=None):
    parts = []
    for a in range(GROUP // 2):
        lo = (g * GROUP // 2 + a) * PAIR
        pair = ref[:, lo:lo + PAIR].astype(F32)
        pair = (pair if scale is None else pair * scale).T
        parts += [pair[:HEAD_DIM], pair[HEAD_DIM:]]
    return jnp.concatenate(parts, axis=1).astype(MXU_DTYPE)


def _heads_back(ref, g, vt):
    for a in range(GROUP // 2):
        lo = (g * GROUP // 2 + a) * PAIR
        pair = jnp.concatenate([vt[:, (2 * a) * ATTN_BLOCK:(2 * a + 1) * ATTN_BLOCK],
                                vt[:, (2 * a + 1) * ATTN_BLOCK:(2 * a + 2) * ATTN_BLOCK]], axis=0)
        ref[:, lo:lo + PAIR] = pair.T.astype(ref.dtype)


def _kv_parts(kv_ref, g):
    ks = slice(g * HEAD_DIM, (g + 1) * HEAD_DIM)
    vs = slice(KV_WIDTH + g * HEAD_DIM, KV_WIDTH + (g + 1) * HEAD_DIM)
    return kv_ref[:, ks].astype(MXU_DTYPE), kv_ref[:, vs].astype(MXU_DTYPE)


def _sink_rows(sinks):
    return jnp.repeat(sinks.reshape(KV_HEADS, GROUP), ATTN_BLOCK, axis=1)


def _attn_fwd(pq, pkv, sinks, comm=None):
    T = pq.shape[0]
    nb = T // ATTN_BLOCK

    def body(q_ref, kvc_ref, kvp_ref, s_ref, y_ref, lse_ref):
        mask_c = _attn_mask()
        has_prev = pl.program_id(0) > 0
        for g in range(KV_HEADS):
            (kc, vc), (kp, vp) = _kv_parts(kvc_ref, g), _kv_parts(kvp_ref, g)
            qt = _heads_transposed(q_ref, g, ATTN_SCALE)
            s = jnp.where(mask_c, _nn(kc, qt), jnp.where(has_prev, _nn(kp, qt), NEG_INF))
            sink = s_ref[g:g + 1, :]
            m = jnp.maximum(jnp.max(s, axis=0, keepdims=True), sink)
            p = jnp.exp(s - m)
            den = jnp.sum(p, axis=0, keepdims=True) + jnp.exp(sink - m)
            pc = jnp.where(mask_c, p, 0.0)
            _heads_back(y_ref, g, (_tn(vc, pc) + _tn(vp, p - pc)) / den)
            lse = m + jnp.log(den)
            for i in range(GROUP):
                lse_ref[g * GROUP + i:g * GROUP + i + 1, :] = lse[:, i * ATTN_BLOCK:(i + 1) * ATTN_BLOCK]

    return _call(
        "attn_fwd", body, grid=(nb,),
        in_specs=[pl.BlockSpec((ATTN_BLOCK, D_MODEL), lambda n: (n, 0)),
                  pl.BlockSpec((ATTN_BLOCK, 2 * KV_WIDTH), lambda n: (n, 0)),
                  pl.BlockSpec((ATTN_BLOCK, 2 * KV_WIDTH), lambda n: (jnp.maximum(n - 1, 0), 0)),
                  pl.BlockSpec((KV_HEADS, GROUP_LANES), lambda n: (0, 0))],
        out_specs=[pl.BlockSpec((ATTN_BLOCK, D_MODEL), lambda n: (n, 0)),
                   pl.BlockSpec((Q_HEADS, ATTN_BLOCK), lambda n: (0, n))],
        out_shape=[jax.ShapeDtypeStruct((T, D_MODEL), MXU_DTYPE), jax.ShapeDtypeStruct((Q_HEADS, T), F32)],
        args=[pq, pkv, pkv, _sink_rows(sinks)], semantics=("parallel",), comm=comm)


def _attn_bwd(pq, pkv, sinks, lse, dy, comm=None):
    T = pq.shape[0]
    nb = T // ATTN_BLOCK
    cur = lambda n: (jnp.minimum(n, nb - 1), 0)

    def body(q_ref, kvc_ref, kvp_ref, s_ref, lse_ref, dy_ref, dq_ref, dkv_ref, ds_ref, carry, top, bot):
        n = pl.program_id(0)

        @pl.when(n == 0)
        def _():
            carry[...] = jnp.zeros_like(carry)
            ds_ref[...] = jnp.zeros_like(ds_ref)

        @pl.when(n < nb)
        def _():
            mask_c = _attn_mask()
            valid = jnp.logical_or(mask_c, n > 0)
            for g in range(KV_HEADS):
                ks = slice(g * HEAD_DIM, (g + 1) * HEAD_DIM)
                vs = slice(KV_WIDTH + g * HEAD_DIM, KV_WIDTH + (g + 1) * HEAD_DIM)
                (kc, vc), (kp, vp) = _kv_parts(kvc_ref, g), _kv_parts(kvp_ref, g)
                qt = _heads_transposed(q_ref, g, ATTN_SCALE)
                dot = _heads_transposed(dy_ref, g)
                lse = jnp.concatenate([lse_ref[g * GROUP + i:g * GROUP + i + 1, :] for i in range(GROUP)], axis=1)
                p = jnp.where(valid, jnp.exp(jnp.where(mask_c, _nn(kc, qt), _nn(kp, qt)) - lse), 0.0)
                dp = jnp.where(mask_c, _nn(vc, dot), _nn(vp, dot))
                delta = jnp.sum(p * dp, axis=0, keepdims=True)
                ds = p * (dp - delta)
                ds_c, p_c = jnp.where(mask_c, ds, 0.0), jnp.where(mask_c, p, 0.0)
                ds_p, p_p = ds - ds_c, p - p_c
                _heads_back(dq_ref, g, (_tn(kc, ds_c) + _tn(kp, ds_p)) * ATTN_SCALE)
                bot[:, ks], bot[:, vs] = _nt(ds_c, qt), _nt(p_c, dot)
                top[:, ks], top[:, vs] = _nt(ds_p, qt), _nt(p_p, dot)
                ds_ref[g:g + 1, :] -= jnp.exp(s_ref[g:g + 1, :] - lse) * delta
            dkv_ref[...] = (carry[...] + top[...]).astype(dkv_ref.dtype)
            carry[...] = bot[...]

        @pl.when(n == nb)
        def _():
            dkv_ref[...] = carry[...].astype(dkv_ref.dtype)

    return _call(
        "attn_bwd", body, grid=(nb + 1,),
        in_specs=[pl.BlockSpec((ATTN_BLOCK, D_MODEL), cur),
                  pl.BlockSpec((ATTN_BLOCK, 2 * KV_WIDTH), cur),
                  pl.BlockSpec((ATTN_BLOCK, 2 * KV_WIDTH), lambda n: (jnp.maximum(jnp.minimum(n, nb - 1) - 1, 0), 0)),
                  pl.BlockSpec((KV_HEADS, GROUP_LANES), lambda n: (0, 0)),
                  pl.BlockSpec((Q_HEADS, ATTN_BLOCK), lambda n: (0, jnp.minimum(n, nb - 1))),
                  pl.BlockSpec((ATTN_BLOCK, D_MODEL), cur)],
        out_specs=[pl.BlockSpec((ATTN_BLOCK, D_MODEL), cur),
                   pl.BlockSpec((ATTN_BLOCK, 2 * KV_WIDTH), lambda n: (jnp.maximum(n - 1, 0), 0)),
                   pl.BlockSpec((KV_HEADS, GROUP_LANES), lambda n: (0, 0))],
        out_shape=[jax.ShapeDtypeStruct((T, D_MODEL), MXU_DTYPE),
                   jax.ShapeDtypeStruct((T, 2 * KV_WIDTH), MXU_DTYPE),
                   jax.ShapeDtypeStruct((KV_HEADS, GROUP_LANES), F32)],
        scratch=[pltpu.VMEM((ATTN_BLOCK, 2 * KV_WIDTH), F32)] * 3,
        args=[pq, pkv, pkv, _sink_rows(sinks), lse, dy], semantics=("arbitrary",), comm=comm)


def _lower_bound(l):
    m = jnp.maximum(l[0:1], l[1:2])
    e0, e1 = jnp.exp(l[0:1] - m), jnp.exp(l[1:2] - m)
    return e0 / (e0 + e1)


def _tri(lower):
    r = lax.broadcasted_iota(jnp.int32, (CHUNK, CHUNK), 0)
    c = lax.broadcasted_iota(jnp.int32, (CHUNK, CHUNK), 1)
    return (r >= c) if lower else (c >= r)


def _chunk_sum(mask, v):
    ones = mask.astype(BF16)
    hi = v.astype(BF16)
    rest = v - hi.astype(F32)
    mid = rest.astype(BF16)
    lo = (rest - mid.astype(F32)).astype(BF16)
    part = lambda t: lax.dot_general(ones, t, (((1,), (0,)), ((), ())), preferred_element_type=F32)
    return part(hi) + part(mid) + part(lo)


def _hgrn_chunk_inputs(hq, hf, lb, causal):
    half_t = 0.5 * jnp.tanh(0.5 * hf)
    sg, sgn = 0.5 + half_t, 0.5 - half_t
    f = lb + (1.0 - lb) * sg
    kk = (1.0 - lb) * sgn
    sq = _sigmoid(hq)
    q = hq * sq
    b = _chunk_sum(causal, jnp.log(f))
    bm, bl = b[CHUNK // 2 - 1:CHUNK // 2, :], b[CHUNK - 1:CHUNK, :]
    e_qm, e_km = jnp.exp(b - bm), jnp.exp(bm - b)
    e_qs, e_kl = e_qm * jnp.exp(bm), e_km * jnp.exp(bl - bm)
    return dict(sg=sg, sgn=sgn, f=f, kk=kk, sq=sq, q=q, e_qm=e_qm, e_km=e_km, e_qs=e_qs, e_kl=e_kl,
                qm=q * e_qm, km=kk * e_km, qs=q * e_qs, kl=kk * e_kl, el=jnp.exp(bl))


def _hgrn_fwd(ph, lb_logits, norm_g, comm=None):
    T = ph.shape[0]
    nblk, cpb = T // HGRN_TOKENS, HGRN_TOKENS // CHUNK
    col = lambda c: pl.BlockSpec((HGRN_TOKENS, D_MODEL), functools.partial(lambda i, c: (i, c), c=c))

    def body(hq_ref, hf_ref, hi_ref, hg_ref, l_ref, ng_ref, y_ref, o_ref, st_ref, s_ref):
        @pl.when(pl.program_id(0) == 0)
        def _():
            s_ref[...] = jnp.zeros_like(s_ref)

        lb = _lower_bound(l_ref[...])
        causal = _tri(True)
        for c in range(cpb):
            rows = slice(c * CHUNK, (c + 1) * CHUNK)
            t = _hgrn_chunk_inputs(hq_ref[rows, :], hf_ref[rows, :], lb, causal)
            qm, km, qs, kl = (t[n].astype(MXU_DTYPE) for n in ("qm", "km", "qs", "kl"))
            v = hi_ref[rows, :].astype(MXU_DTYPE)
            for h in range(HGRN_HEADS):
                ls = slice(h * HGRN_K, (h + 1) * HGRN_K)
                st = s_ref[h]
                st_ref[c, ls, :] = st
                a = jnp.where(causal, _nt(qm[:, ls], km[:, ls]), 0.0)
                o_ref[rows, ls] = _nn(a, v[:, ls]) + _nt(qs[:, ls], st)
                s_ref[h] = t["el"][:, ls] * st + _tn(v[:, ls], kl[:, ls])
        for h in range(HGRN_HEADS):
            ls = slice(h * HGRN_K, (h + 1) * HGRN_K)
            o = o_ref[:, ls]
            r = lax.rsqrt(jnp.mean(o * o, axis=-1, keepdims=True) + EPS)
            y_ref[:, ls] = (o * r * ng_ref[:, ls] * _sigmoid(hg_ref[:, ls])).astype(y_ref.dtype)

    return _call(
        "hgrn_fwd", body, grid=(nblk,),
        in_specs=[col(0), col(1), col(2), col(3),
                  pl.BlockSpec((2, D_MODEL), lambda i: (0, 0)), pl.BlockSpec((1, D_MODEL), lambda i: (0, 0))],
        out_specs=[pl.BlockSpec((HGRN_TOKENS, D_MODEL), lambda i: (i, 0)),
                   pl.BlockSpec((HGRN_TOKENS, D_MODEL), lambda i: (i, 0)),
                   pl.BlockSpec((cpb, D_MODEL, HGRN_K), lambda i: (i, 0, 0))],
        out_shape=[jax.ShapeDtypeStruct((T, D_MODEL), MXU_DTYPE), jax.ShapeDtypeStruct((T, D_MODEL), F32),
                   jax.ShapeDtypeStruct((T // CHUNK, D_MODEL, HGRN_K), F32)],
        scratch=[pltpu.VMEM((HGRN_HEADS, HGRN_K, HGRN_K), F32)],
        args=[ph, ph, ph, ph, lb_logits, norm_g], semantics=("arbitrary",), comm=comm)


def _hgrn_bwd(ph, o_raw, states, dy, lb_logits, norm_g, comm=None):
    T = ph.shape[0]
    nblk, cpb = T // HGRN_TOKENS, HGRN_TOKENS // CHUNK
    rev = lambda i: nblk - 1 - i
    col = lambda c: pl.BlockSpec((HGRN_TOKENS, D_MODEL), functools.partial(lambda i, c: (rev(i), c), c=c))
    tok = pl.BlockSpec((HGRN_TOKENS, D_MODEL), lambda i: (rev(i), 0))

    def body(hq_ref, hf_ref, hi_ref, hg_ref, o_ref, st_ref, dy_ref, l_ref, ng_ref,
             dph_ref, dng_ref, dl_ref, dst_ref, dlb_ref, do_s, dqm_s, dkm_s, dqs_s, dkl_s, dv_s, del_s):
        i = pl.program_id(0)

        @pl.when(i == 0)
        def _():
            dst_ref[...] = jnp.zeros_like(dst_ref)
            dlb_ref[...] = jnp.zeros_like(dlb_ref)
            dng_ref[...] = jnp.zeros_like(dng_ref)

        lb = _lower_bound(l_ref[...])
        causal, anti = _tri(True), _tri(False)
        row = lax.broadcasted_iota(jnp.int32, (CHUNK, D_MODEL), 0)
        for c in reversed(range(cpb)):
            rows = slice(c * CHUNK, (c + 1) * CHUNK)
            hq = hq_ref[rows, :]
            t = _hgrn_chunk_inputs(hq, hf_ref[rows, :], lb, causal)
            sgg = _sigmoid(hg_ref[rows, :])
            dyv = dy_ref[rows, :]
            for h in range(HGRN_HEADS):
                ls = slice(h * HGRN_K, (h + 1) * HGRN_K)
                o = o_ref[rows, ls]
                r = lax.rsqrt(jnp.mean(o * o, axis=-1, keepdims=True) + EPS)
                nrm = o * r
                g_h = sgg[:, ls]
                dph_ref[rows, 3 * D_MODEL + h * HGRN_K:3 * D_MODEL + (h + 1) * HGRN_K] = (
                    dyv[:, ls] * nrm * ng_ref[:, ls] * g_h * (1.0 - g_h)).astype(dph_ref.dtype)
                dyg = dyv[:, ls] * g_h
                dng_ref[:, ls] += jnp.sum(dyg * nrm, axis=0, keepdims=True)
                dn = dyg * ng_ref[:, ls]
                do_s[:, ls] = r * (dn - nrm * jnp.mean(dn * nrm, axis=-1, keepdims=True))
            qm, km, qs, kl = (t[n].astype(MXU_DTYPE) for n in ("qm", "km", "qs", "kl"))
            v = hi_ref[rows, :].astype(MXU_DTYPE)
            do = do_s[...].astype(MXU_DTYPE)
            for h in range(HGRN_HEADS):
                ls = slice(h * HGRN_K, (h + 1) * HGRN_K)
                st = st_ref[c, ls, :]
                dst = dst_ref[h]
                a = jnp.where(causal, _nt(qm[:, ls], km[:, ls]), 0.0)
                da = jnp.where(causal, _nt(do[:, ls], v[:, ls]), 0.0)
                dv_s[:, ls] = _tn(a, do[:, ls]) + _nt(kl[:, ls], dst)
                dkl_s[:, ls] = _nn(v[:, ls], dst)
                dqs_s[:, ls] = _nn(do[:, ls], st)
                del_s[:, ls] = jnp.sum(dst * st, axis=0, keepdims=True)
                dst_ref[h] = _tn(do[:, ls], qs[:, ls]) + t["el"][:, ls] * dst
                dqm_s[:, ls] = _nn(da, km[:, ls])
                dkm_s[:, ls] = _tn(da, qm[:, ls])
            dqm, dkm, dqs, dkl = dqm_s[...], dkm_s[...], dqs_s[...], dkl_s[...]
            dq = dqm * t["e_qm"] + dqs * t["e_qs"]
            dk = dkm * t["e_km"] + dkl * t["e_kl"]
            t_qm, t_km, t_kl = dqm * t["qm"], dkm * t["km"], dkl * t["kl"]
            db = t_qm - t_km + dqs * t["qs"] - t_kl
            db_mid = jnp.sum(t_km - t_qm, axis=0, keepdims=True)
            db_last = jnp.sum(t_kl, axis=0, keepdims=True) + del_s[...] * t["el"]
            db = db + jnp.where(row == CHUNK // 2 - 1, db_mid, 0.0) + jnp.where(row == CHUNK - 1, db_last, 0.0)
            dlogf = _chunk_sum(anti, db)
            sq, sg, sgn, f = t["sq"], t["sg"], t["sgn"], t["f"]
            dph_ref[rows, 0:D_MODEL] = (dq * (sq * (1.0 + hq * (1.0 - sq)))).astype(dph_ref.dtype)
            dph_ref[rows, D_MODEL:2 * D_MODEL] = (
                dlogf * (1.0 - lb) * sg * (1.0 - sg) / f - dk * (1.0 - lb) * sgn * (1.0 - sgn)).astype(dph_ref.dtype)
            dph_ref[rows, 2 * D_MODEL:3 * D_MODEL] = dv_s[...].astype(dph_ref.dtype)
            dlb_ref[...] += jnp.sum(dlogf * (1.0 - sg) / f - dk * sgn, axis=0, keepdims=True)

        @pl.when(i == nblk - 1)
        def _():
            dl0 = dlb_ref[...] * lb * (1.0 - lb)
            dl_ref[0:1, :] = dl0
            dl_ref[1:2, :] = -dl0

    wide = pltpu.VMEM((CHUNK, D_MODEL), F32)
    return _call(
        "hgrn_bwd", body, grid=(nblk,),
        in_specs=[col(0), col(1), col(2), col(3), tok,
                  pl.BlockSpec((cpb, D_MODEL, HGRN_K), lambda i: (rev(i), 0, 0)), tok,
                  pl.BlockSpec((2, D_MODEL), lambda i: (0, 0)), pl.BlockSpec((1, D_MODEL), lambda i: (0, 0))],
        out_specs=[pl.BlockSpec((HGRN_TOKENS, 4 * D_MODEL), lambda i: (rev(i), 0)),
                   pl.BlockSpec((1, D_MODEL), lambda i: (0, 0)), pl.BlockSpec((2, D_MODEL), lambda i: (0, 0))],
        out_shape=[jax.ShapeDtypeStruct((T, 4 * D_MODEL), MXU_DTYPE), jax.ShapeDtypeStruct((1, D_MODEL), F32),
                   jax.ShapeDtypeStruct((2, D_MODEL), F32)],
        scratch=[pltpu.VMEM((HGRN_HEADS, HGRN_K, HGRN_K), F32), pltpu.VMEM((1, D_MODEL), F32),
                 wide, wide, wide, wide, wide, wide, pltpu.VMEM((1, D_MODEL), F32)],
        args=[ph, ph, ph, ph, o_raw, states, dy, lb_logits, norm_g], semantics=("arbitrary",), comm=comm)


def _local_step(x, target, vec, net):
    T, D = x.shape
    norm_mix_g, b_in, sinks, lb_logits = vec["norm_mix_g"], vec["b_in"], vec["attn_sinks"], vec["hgrn_lb_logits"]
    hgrn_norm_g, norm_ffn_g, norm_final_g = vec["hgrn_norm_g"], vec["norm_ffn_g"], vec["norm_final_g"]
    w_in = net.full("w_in")
    o_q, o_kv, o_h, o_g = (sum(IN_SPLITS[:i]) for i in range(4))
    w_q, w_kv, w_h, w_g = (_Rows(w_in, o, n) for o, n in zip((o_q, o_kv, o_h, o_g), IN_SPLITS))
    both = lambda acc: (acc, acc)
    grad_outs = [("mn", F32), ("mn", MXU_DTYPE)]
    row_vec = lambda n: ((1, n), lambda i, j: (0, j))
    tile = lambda tm, tn: ((tm, tn), lambda i, j: (i, j))
    TM = 512
    BIG = min(T, 1024)

    names = ("w_branch_attn", "w_branch_hgrn", "w_out")
    (u, pq, pkv, ph, pg), got = _in_proj(x, norm_mix_g, w_in, b_in, tm=256, comm=net.gather(names))
    net.gathered(names, got)
    names = ("w_ffn_gate",)
    (y_attn, lse), got = _attn_fwd(pq, pkv, sinks, comm=net.gather(names))
    net.gathered(names, got)
    names = ("w_ffn_up",)
    (y_hgrn, o_raw, states), got = _hgrn_fwd(ph, lb_logits, hgrn_norm_g, comm=net.gather(names))
    net.gathered(names, got)
    w_ba, w_bh, w_out = net.full("w_branch_attn"), net.full("w_branch_hgrn"), net.full("w_out")
    w_gate, w_up = net.full("w_ffn_gate"), net.full("w_ffn_up")
    ya, yb, merged = _merge_fwd(y_attn, y_hgrn, w_ba, w_bh, pg, tm=TM)

    FT = FFN // 2
    names = ("w_ffn_down",)
    (h1, u2, gpre, up, z), got = _ffn_fwd(merged, w_out, x, norm_ffn_g, w_gate, w_up, tm=256,
                                          comm=net.gather(names))
    net.gathered(names, got)
    w_down = net.full("w_ffn_down")

    dh2, dh2b, dgp, dup, dgf_p, loss_p = _ffn_tail(z, w_down, h1, target, norm_final_g, gpre, up, tm=256)
    loss = jnp.sum(loss_p.reshape(-1, 8, D)[:, 0, 0])
    d_norm_final = _colsum_partials(dgf_p)
    d_w_down = _matmul("dw_down", [(z, dh2b)], "tn", tm=FT, tn=1024, tk=1024, outs=grad_outs, epilogue=both)

    names = ("w_ffn_down",)
    (dh1, dh1b, dg2_p), got = _ffn_in_bwd(dgp, dup, w_gate, w_up, h1, norm_ffn_g, dh2, tm=256,
        comm=net.exchange(dict(w_ffn_down=[d_w_down])))
    net.received(names, (), got)
    d_norm_ffn = _colsum_partials(dg2_p)
    d_w_gate = _matmul("dw_gate", [(dgp, u2)], "tn", tm=FT, tn=1024, tk=1024, outs=grad_outs, epilogue=both)
    d_w_up = _matmul("dw_up", [(dup, u2)], "tn", tm=FT, tn=1024, tk=1024, outs=grad_outs, epilogue=both)

    dya, dyb, dga, dgb, dy_attn, dy_hgrn = _merge_bwd(dh1b, w_out, w_ba, w_bh, ya, yb, pg, tm=TM)
    tn_grad = functools.partial(_matmul, mode="tn", tn=1024, tk=1024, outs=grad_outs, epilogue=both)
    d_w_out = tn_grad("dw_out", [(merged, dh1b)], tm=1024)
    d_w_ba = tn_grad("dw_branch_a", [(y_attn, dya)], tm=1024)
    d_w_bh = tn_grad("dw_branch_b", [(y_hgrn, dyb)], tm=1024)

    names, swap = ("w_ffn_gate",), ("w_ffn_down",)
    (dq, dkv, dsink), got = _attn_bwd(pq, pkv, sinks, lse, dy_attn,
                                      comm=net.exchange(dict(w_ffn_gate=[d_w_gate]), swap))
    net.received(names, swap, got)
    names, swap = ("w_ffn_up", "w_out", "w_branch_attn", "w_branch_hgrn"), ("w_ffn_gate",)
    (dph, d_hgrn_norm, d_lb_logits), got = _hgrn_bwd(
        ph, o_raw, states, dy_hgrn, lb_logits, hgrn_norm_g,
        comm=net.exchange(dict(w_ffn_up=[d_w_up], w_out=[d_w_out], w_branch_attn=[d_w_ba],
                               w_branch_hgrn=[d_w_bh]), swap))
    net.received(names, swap, got)

    d_w_in, db, rows_in = None, {}, sum(IN_SPLITS)
    for piece, dp, first, tm_p in (("q", dq, o_q, 1024), ("kv", dkv, o_kv, 256), ("h", dph, o_h, 1024),
                                   ("ga", dga, o_g, 1024), ("gb", dgb, o_g + D, 1024)):
        *d_w_in, db[piece] = tn_grad("dw_in_" + piece, [(dp, u)], tm=tm_p, a_colsum=True,
                                     into=_Into(rows_in, first, d_w_in))
    d_w_in = [tuple(d_w_in)]

    first_level = net.presum_begin("w_in", d_w_in)
    halves = net.presum_end("w_in", [] if first_level is None else _copies_alone("presum_swap_w_in", first_level))
    names, swap = ("w_in",), ("w_ffn_up", "w_out", "w_branch_attn", "w_branch_hgrn")
    pieces = [(dq, o_q), (dkv, o_kv), (dph, o_h), (dga, o_g), (dgb, o_g + D)]
    (dx, dg1_p), got = _in_proj_bwd(pieces, w_in, x, norm_mix_g, dh1, tm=256, comm=_join(halves, net.swap(swap)))
    net.last = (names, swap, got)
    d_norm_mix = _colsum_partials(dg1_p)
    d_b_in = jnp.concatenate([db[piece] for piece in ("q", "kv", "h", "ga", "gb")], axis=1)
    vecs = dict(norm_mix_g=d_norm_mix, b_in=d_b_in, attn_sinks=jnp.sum(dsink.reshape(Q_HEADS, ATTN_BLOCK), axis=1).reshape(1, Q_HEADS),
                hgrn_lb_logits=d_lb_logits,
                hgrn_norm_g=d_hgrn_norm, norm_ffn_g=d_norm_ffn, norm_final_g=d_norm_final)
    return loss, dx, vecs


def _place():
    return lax.axis_index("x"), lax.axis_index("y"), lax.axis_index("c")


def _other_chips(x, y):
    return [(1 - x, y), (x, 1 - y), (1 - x, 1 - y)]


def _y_first(copies):
    return [copies[3 * (i // 3) + (1, 0, 2)[i % 3]] for i in range(len(copies))]


def _gather_copies(shards):
    n = len(shards)

    def build(ins, outs, send_sems, recv_sems, local_sems):
        x, y, c = _place()
        mine = 2 * x + y
        local = [pltpu.make_async_copy(ins[w], outs[w].at[mine], local_sems.at[w]) for w in range(n)]
        sends, recvs = [], []
        for w in range(n):
            for k, (px, py) in enumerate(_other_chips(x, y)):
                sem = 3 * w + k
                sends.append(pltpu.make_async_remote_copy(
                    src_ref=ins[w], dst_ref=outs[w].at[mine], send_sem=send_sems.at[sem], recv_sem=recv_sems.at[sem],
                    device_id=(px, py, c), device_id_type=MESH_ID))
                recvs.append(pltpu.make_async_remote_copy(
                    src_ref=ins[w], dst_ref=outs[w].at[2 * px + py], send_sem=send_sems.at[sem],
                    recv_sem=recv_sems.at[sem], device_id=(px, py, c), device_id_type=MESH_ID))
        return sends, recvs, local, _y_first(sends)

    return _Carried(shards, [jax.ShapeDtypeStruct((N_CHIPS,) + s.shape, s.dtype) for s in shards], 3 * n, n, build)


def _grad_copies(stacked):
    n = len(stacked)

    def build(ins, outs, send_sems, recv_sems, local_sems):
        x, y, c = _place()
        sends = []
        for w in range(n):
            for k, (px, py) in enumerate(_other_chips(x, y)):
                sem = 3 * w + k
                sends.append(pltpu.make_async_remote_copy(
                    src_ref=ins[w].at[2 * px + py], dst_ref=outs[w].at[k], send_sem=send_sems.at[sem],
                    recv_sem=recv_sems.at[sem], device_id=(px, py, c), device_id_type=MESH_ID))
        return sends, sends, [], _y_first(sends)

    return _Carried(stacked, [jax.ShapeDtypeStruct((3,) + s.shape[1:], s.dtype) for s in stacked], 3 * n, 0, build)


def _small_copies(small):
    def build(ins, outs, send_sems, recv_sems, local_sems):
        small_ref, all_ref = ins[0], outs[0]
        x, y, c = _place()
        me = 4 * x + 2 * y + c
        sends, recvs = [], []
        for r in range(1, 8):
            px = 1 - x if r & 4 else x
            py = 1 - y if r & 2 else y
            pc = 1 - c if r & 1 else c
            sends.append(pltpu.make_async_remote_copy(
                src_ref=small_ref, dst_ref=all_ref.at[me], send_sem=send_sems.at[r - 1], recv_sem=recv_sems.at[r - 1],
                device_id=(px, py, pc), device_id_type=MESH_ID))
            recvs.append(pltpu.make_async_remote_copy(
                src_ref=small_ref, dst_ref=all_ref.at[4 * px + 2 * py + pc], send_sem=send_sems.at[r - 1],
                recv_sem=recv_sems.at[r - 1], device_id=(px, py, pc), device_id_type=MESH_ID))
        return sends, recvs, [pltpu.make_async_copy(small_ref, all_ref.at[me], local_sems.at[0])]

    return _Carried([small], [jax.ShapeDtypeStruct((8,) + small.shape, small.dtype)], 7, 1, build)


def _gather_by_neighbours(name, shard):
    half = shard.shape[0] // 2
    quarter = half // 2

    def body(in_ref, out_ref, send_sems, recv_sems, local_sem):
        for core in (0, 1):
            @pl.when(lax.axis_index("c") == core)
            def _():
                program(core, in_ref, out_ref, send_sems, recv_sems, local_sem)

    def program(c, in_ref, out_ref, send_sems, recv_sems, local_sem):
        x, y, _ = _place()
        chip = lambda px, py: 2 * px + py
        to_x, to_y, sibling = (1 - x, y, c), (x, 1 - y, c), (x, y, 1 - c)
        x_blk, y_blk, d_blk = chip(1 - x, y), chip(x, 1 - y), chip(1 - x, 1 - y)
        mine, theirs = c * half, (1 - c) * half

        def copy(sem, rows, block, to, src=None):
            place = out_ref.at[block, pl.ds(rows[0], rows[1])]
            return pltpu.make_async_remote_copy(
                src_ref=place if src is None else src, dst_ref=place, send_sem=send_sems.at[sem],
                recv_sem=recv_sems.at[sem], device_id=to, device_id_type=MESH_ID)

        own = pltpu.make_async_copy(in_ref, out_ref.at[chip(x, y)], local_sem)
        own.start()
        my_rows = in_ref.at[pl.ds(mine, half)]
        along_x = dict(send=copy(0, (mine, half), chip(x, y), to_x, src=my_rows),
                       landed=copy(0, (mine, half), x_blk, to_x),
                       onward=[copy(3, (mine + quarter, quarter), x_blk, to_y), copy(4, (mine, half), x_blk, sibling)],
                       diagonal=copy(2, (mine, quarter), d_blk, to_x))
        along_y = dict(send=copy(1, (mine, half), chip(x, y), to_y, src=my_rows),
                       landed=copy(1, (mine, half), y_blk, to_y),
                       onward=[copy(2, (mine, quarter), y_blk, to_x), copy(5, (mine, half), y_blk, sibling)],
                       diagonal=copy(3, (mine + quarter, quarter), d_blk, to_y))
        last = copy(6, (mine, half), d_blk, sibling)

        order = (along_x, along_y) if c == 0 else (along_y, along_x)
        for axis in order:
            axis["send"].start()
        for axis in order:
            axis["landed"].wait_recv()
            for cp in axis["onward"]:
                cp.start()
        for axis in order:
            axis["diagonal"].wait_recv()
        last.start()
        for sem, block in ((4, x_blk), (5, y_blk), (6, d_blk)):
            copy(sem, (theirs, half), block, sibling).wait_recv()
        for cp in [along_x["send"], along_y["send"]] + along_x["onward"] + along_y["onward"] + [last]:
            cp.wait_send()
        own.wait()

    return pl.pallas_call(
        body, name=name, in_specs=[HBM_SPEC], out_specs=HBM_SPEC,
        out_shape=jax.ShapeDtypeStruct((N_CHIPS,) + shard.shape, shard.dtype),
        scratch_shapes=[pltpu.SemaphoreType.DMA((7,)), pltpu.SemaphoreType.DMA((7,)), pltpu.SemaphoreType.DMA(())],
    )(shard)


def _copies_alone(name, comm):
    return _call(name, lambda: None, grid=(), in_specs=[], out_specs=[], out_shape=[], args=[], comm=comm)[1]


class _Net:
    def __init__(self, shards):
        self.shards = shards
        self.whole, self.own, self.theirs, self.sums, self.other = {}, {}, {}, {}, {}
        x, y, _ = _place()
        self.chip = 2 * x + y

    def gather(self, names):
        return _gather_copies([self.shards[n] for n in names])

    def gathered(self, names, got):
        for n, g in zip(names, got):
            self.whole[n] = g.reshape(-1, g.shape[-1])

    def full(self, name):
        return self.whole[name]

    def exchange(self, grads, swap=()):
        stacked = []
        for n, pieces in grads.items():
            (keep, send), = pieces
            self.own[n] = keep
            stacked.append(send.reshape(N_CHIPS, keep.shape[0] // N_CHIPS, send.shape[-1]))
        return _join(_grad_copies(stacked), self.swap(swap))

    def swap(self, names):
        return _sibling_copies([self.sums[n] for n in names]) if names else None

    def presum_begin(self, name, pieces):
        keep = jnp.concatenate([p[0] for p in pieces], axis=0) if len(pieces) > 1 else pieces[0][0]
        send = jnp.concatenate([p[1] for p in pieces], axis=0) if len(pieces) > 1 else pieces[0][1]
        rows = keep.shape[0] // N_CHIPS
        self.held = keep.reshape(N_CHIPS, rows, keep.shape[-1])
        return _half_rows_copies(send.reshape(N_CHIPS, rows, send.shape[-1]))

    def presum_end(self, name, got):
        x, y, c = _place()
        to_send, self.own[name] = _pre_sum("presum_" + name, self.held, got[0], jnp.stack([c, self.chip]))
        return _grad_copies([to_send])

    def received(self, names, swap, got, carried=None):
        self.theirs.update(zip(names, got[:len(names)]))
        self.other.update(zip(swap, got[len(names):]))
        for n in names:
            (self.sums[n],), more = _partial_sum("sum_" + n, self.own[n], self.theirs[n], self.chip, comm=carried)
        return more


def _half_rows_copies(stacked):
    n, rows = stacked.shape[0], stacked.shape[1] // 2

    def build(ins, outs, send_sems, recv_sems, local_sems):
        x, y, c = _place()
        copies = [pltpu.make_async_remote_copy(
            src_ref=ins[0].at[s, pl.ds((1 - c) * rows, rows)], dst_ref=outs[0].at[s], send_sem=send_sems.at[s],
            recv_sem=recv_sems.at[s], device_id=(x, y, 1 - c), device_id_type=MESH_ID) for s in range(n)]
        return copies, copies, []

    return _Carried([stacked], [jax.ShapeDtypeStruct((n, rows, stacked.shape[2]), stacked.dtype)], n, 0, build)


def _pre_sum(name, held, theirs, core_and_chip):
    n, R, C = held.shape
    half = R // 2
    tr = _row_tile(half)
    per_half = half // tr

    def body(place_ref, h_ref, t_ref, send_ref, own_ref):
        total = h_ref[0] + t_ref[0].astype(F32)
        send_ref[0] = total.astype(send_ref.dtype)

        @pl.when(pl.program_id(1) == place_ref[1])
        def _():
            own_ref[...] = total

    return pl.pallas_call(
        body, name=name,
        grid_spec=pltpu.PrefetchScalarGridSpec(
            num_scalar_prefetch=1, grid=(per_half, n),
            in_specs=[pl.BlockSpec((1, tr, C), lambda i, s, place: (s, place[0] * per_half + i, 0)),
                      pl.BlockSpec((1, tr, C), lambda i, s, place: (s, i, 0))],
            out_specs=[pl.BlockSpec((1, tr, C), lambda i, s, place: (s, i, 0)),
                       pl.BlockSpec((tr, C), lambda i, s, place: (i, 0))]),
        out_shape=[jax.ShapeDtypeStruct((n, half, C), MXU_DTYPE), jax.ShapeDtypeStruct((half, C), F32)],
        compiler_params=_params(("arbitrary", "arbitrary")),
    )(core_and_chip, held, theirs)


def _sibling_copies(parts):
    n = len(parts)

    def build(ins, outs, send_sems, recv_sems, local_sems):
        x, y, c = _place()
        copies = [pltpu.make_async_remote_copy(
            src_ref=ins[w], dst_ref=outs[w], send_sem=send_sems.at[w], recv_sem=recv_sems.at[w],
            device_id=(x, y, 1 - c), device_id_type=MESH_ID) for w in range(n)]
        return copies, copies, []

    return _Carried(parts, [jax.ShapeDtypeStruct(p.shape, p.dtype) for p in parts], n, 0, build)


def _row_tile(rows, most=512, sublanes=16):
    return max(t for t in range(sublanes, most + 1, sublanes) if rows % t == 0)


def _partial_sum(name, own, recv, chip, comm=None):
    _, R, C = recv.shape
    tr = _row_tile(R, most=128)

    def body(o_ref, r_ref, p_ref):
        p_ref[...] = ((o_ref[...] + r_ref[0].astype(F32)) + r_ref[1].astype(F32)) + r_ref[2].astype(F32)

    if own.shape[0] != R:
        assert comm is None and own.shape[0] == N_CHIPS * R
        total = pl.pallas_call(
            lambda chip_ref, *refs: body(*refs), name=name,
            grid_spec=pltpu.PrefetchScalarGridSpec(
                num_scalar_prefetch=1, grid=(R // tr,),
                in_specs=[pl.BlockSpec((tr, C), lambda i, chip_ref: (chip_ref[0] * (R // tr) + i, 0)),
                          pl.BlockSpec((3, tr, C), lambda i, chip_ref: (0, i, 0))],
                out_specs=pl.BlockSpec((tr, C), lambda i, chip_ref: (i, 0))),
            out_shape=jax.ShapeDtypeStruct((R, C), F32), compiler_params=_params(("parallel",)),
        )(chip.reshape(1), own, recv)
        return [total], []
    return _call(name, body, grid=(R // tr,),
                 in_specs=[pl.BlockSpec((tr, C), lambda i: (i, 0)), pl.BlockSpec((3, tr, C), lambda i: (0, i, 0))],
                 out_specs=[pl.BlockSpec((tr, C), lambda i: (i, 0))], out_shape=[jax.ShapeDtypeStruct((R, C), F32)],
                 args=[own, recv], semantics=("parallel",), comm=comm)


def _adam_vals(w, g, m, v):
    m = ADAM_B1 * m + (1.0 - ADAM_B1) * g
    v = ADAM_B2 * v + (1.0 - ADAM_B2) * (g * g)
    m_hat = m / (1.0 - ADAM_B1 ** ADAM_STEP)
    v_hat = v / (1.0 - ADAM_B2 ** ADAM_STEP)
    delta = -ADAM_LR * (m_hat / (jnp.sqrt(v_hat) + ADAM_EPS) + ADAM_WD * w)
    return delta, m, v


def _adamw(name, w, m, v, mine, other, comm=None):
    R, C = w.shape
    tr = _row_tile(R)

    def body(w_ref, m_ref, v_ref, s_ref, n_ref, g_ref, d_ref, nm_ref, nv_ref):
        g = s_ref[...] + n_ref[...]
        d, nm, nv = _adam_vals(w_ref[...], g, m_ref[...], v_ref[...])
        g_ref[...], d_ref[...], nm_ref[...], nv_ref[...] = g, d, nm, nv

    spec = pl.BlockSpec((tr, C), lambda i: (i, 0))
    return _call(name, body, grid=(R // tr,), in_specs=[spec] * 5, out_specs=[spec] * 4,
                 out_shape=[jax.ShapeDtypeStruct((R, C), F32)] * 4, args=[w, m, v, mine, other],
                 semantics=("parallel",), comm=comm)


def _adamw_by_halves(name, w, m, v, mine, other, core):
    R, C = w.shape
    tr = _row_tile(R // 2)
    per_half = R // 2 // tr

    def body(c_ref, w_ref, m_ref, v_ref, s_ref, n_ref, g_ref, d_ref, nm_ref, nv_ref):
        g = jnp.where(pl.program_id(0) // per_half == c_ref[0, 0], s_ref[...], n_ref[...])
        d, nm, nv = _adam_vals(w_ref[...], g, m_ref[...], v_ref[...])
        g_ref[...], d_ref[...], nm_ref[...], nv_ref[...] = g, d, nm, nv

    spec = pl.BlockSpec((tr, C), lambda i: (i, 0))
    part = pl.BlockSpec((tr, C), lambda i: (i % per_half, 0))
    return pl.pallas_call(
        body, name=name, grid=(R // tr,),
        in_specs=[pl.BlockSpec(memory_space=pltpu.SMEM), spec, spec, spec, part, part], out_specs=[spec] * 4,
        out_shape=[jax.ShapeDtypeStruct((R, C), F32)] * 4, compiler_params=_params(("parallel",)),
    )(core, w, m, v, mine, other)


SMALL_LAYOUT = dict(norm_mix_g=(0, 1, 1024), b_in=(1, 8, 7424), hgrn_norm_g=(9, 1, 1024), norm_ffn_g=(10, 1, 1024),
                    norm_final_g=(11, 1, 1024), hgrn_lb_logits=(12, 2, 2048), attn_sinks=(14, 1, 16))
SMALL_LOSS_ROW, SMALL_ROWS = 15, 16


def _pack_small(grads, loss):
    rows = [jnp.pad(grads[name].astype(F32).reshape(-1), (0, nrows * D_MODEL - n))
            for name, (_, nrows, n) in SMALL_LAYOUT.items()]
    rows.append(jnp.pad(loss.astype(F32).reshape(1), (0, D_MODEL - 1)))
    return jnp.concatenate(rows).reshape(SMALL_ROWS, D_MODEL)


def _adamw_small(w, m, v, g_all):
    names = list(SMALL_LAYOUT)
    n = len(names)

    def body(a_ref, *refs):
        ins, outs = refs[:3 * n], refs[3 * n:]
        g_all_rows = a_ref[0]
        for dev in range(1, 8):
            g_all_rows = g_all_rows + a_ref[dev]
        for i, name in enumerate(names):
            first, nrows, count = SMALL_LAYOUT[name]
            w_ref, m_ref, v_ref = ins[3 * i:3 * i + 3]
            if w_ref.shape[0] == nrows:
                g = g_all_rows[first:first + nrows, :w_ref.shape[1]]
            else:
                last = count - (nrows - 1) * D_MODEL
                g = jnp.concatenate([g_all_rows[r:r + 1, :] for r in range(first, first + nrows - 1)]
                                    + [g_all_rows[first + nrows - 1:first + nrows, :last]], axis=1)
            d, nm, nv = _adam_vals(w_ref[...], g, m_ref[...], v_ref[...])
            for o_ref, val in zip(outs[4 * i:4 * i + 4], (g, d, nm, nv)):
                o_ref[...] = val
        outs[4 * n][...] = g_all_rows[SMALL_LOSS_ROW:SMALL_LOSS_ROW + 1, 0:1]

    res = pl.pallas_call(
        body, name="adamw_small",
        out_shape=[jax.ShapeDtypeStruct(w[name].shape, F32) for name in names for _ in range(4)]
        + [jax.ShapeDtypeStruct((1, 1), F32)],
    )(g_all, *[t[name] for name in names for t in (w, m, v)])
    return {name: res[4 * i:4 * i + 4] for i, name in enumerate(names)}, res[4 * n]


MATRICES = ("w_in", "w_branch_attn", "w_branch_hgrn", "w_out", "w_ffn_gate", "w_ffn_up", "w_ffn_down")
COLUMN_SHARDED = ("w_in", "w_ffn_gate", "w_ffn_up")
WEIGHTS = ("norm_mix_g", "w_in", "b_in", "attn_sinks", "hgrn_lb_logits", "hgrn_norm_g", "w_branch_attn",
           "w_branch_hgrn", "w_out", "norm_ffn_g", "w_ffn_gate", "w_ffn_up", "w_ffn_down", "norm_final_g")


def kernel(x, norm_mix_g, w_in, b_in, attn_sinks, hgrn_lb_logits, hgrn_norm_g, w_branch_attn, w_branch_hgrn, w_out, norm_ffn_g, w_ffn_gate, w_ffn_up, w_ffn_down, norm_final_g, loss_target, m_norm_mix_g, m_w_in, m_b_in, m_attn_sinks, m_hgrn_lb_logits, m_hgrn_norm_g, m_w_branch_attn, m_w_branch_hgrn, m_w_out, m_norm_ffn_g, m_w_ffn_gate, m_w_ffn_up, m_w_ffn_down, m_norm_final_g, v_norm_mix_g, v_w_in, v_b_in, v_attn_sinks, v_hgrn_lb_logits, v_hgrn_norm_g, v_w_branch_attn, v_w_branch_hgrn, v_w_out, v_norm_ffn_g, v_w_ffn_gate, v_w_ffn_up, v_w_ffn_down, v_norm_final_g):
    given = dict(locals())
    w = {n: given[n] for n in WEIGHTS}
    m = {n: given["m_" + n] for n in WEIGHTS}
    v = {n: given["v_" + n] for n in WEIGHTS}

    block = lambda a, n: jnp.transpose(a[0]) if n in COLUMN_SHARDED else a[0]
    unblock = lambda a, n: (jnp.transpose(a) if n in COLUMN_SHARDED else a)[None]
    net = _Net({n: block(w[n], n).astype(MXU_DTYPE) for n in MATRICES})
    net.gathered(("w_in",), [_gather_by_neighbours("gather_w_in", net.shards["w_in"])])
    vec = dict(norm_mix_g=norm_mix_g, b_in=b_in, attn_sinks=attn_sinks, hgrn_lb_logits=hgrn_lb_logits,
               hgrn_norm_g=hgrn_norm_g, norm_ffn_g=norm_ffn_g, norm_final_g=norm_final_g.reshape(1, D_MODEL))
    loss_part, dx, d_vecs = _local_step(x[0], loss_target[0], vec, net)

    small_all, = net.received(*net.last, carried=_small_copies(_pack_small(d_vecs, loss_part)))
    grads, deltas, new_m, new_v = {}, {}, {}, {}
    for n in ("w_ffn_down", "w_ffn_gate", "w_ffn_up", "w_out", "w_branch_attn", "w_branch_hgrn"):
        res, got = _adamw("adamw_" + n, block(w[n], n), block(m[n], n), block(v[n], n), net.sums[n], net.other[n],
                          comm=net.swap(("w_in",)) if n == "w_ffn_down" else None)
        if n == "w_ffn_down":
            net.other["w_in"], = got
        grads[n], deltas[n], new_m[n], new_v[n] = (unblock(r, n) for r in res)
    n = "w_in"
    res = _adamw_by_halves("adamw_" + n, block(w[n], n), block(m[n], n), block(v[n], n), net.sums[n], net.other[n],
                           _place()[2].reshape(1, 1))
    grads[n], deltas[n], new_m[n], new_v[n] = (unblock(r, n) for r in res)
    rows = lambda t: {n: t[n].reshape(-1, t[n].shape[-1]) for n in SMALL_LAYOUT}
    res, loss = _adamw_small(rows(w), rows(m), rows(v), small_all)
    for n, four in res.items():
        grads[n], deltas[n], new_m[n], new_v[n] = (r.reshape(w[n].shape) for r in four)
    loss = loss.reshape(())
    return (loss, dx[None], *[grads[n] for n in WEIGHTS], *[deltas[n] for n in WEIGHTS],
            *[new_m[n] for n in WEIGHTS], *[new_v[n] for n in WEIGHTS])
```

```python
import collections
import functools
import math

import jax
import jax.numpy as jnp
from jax import lax
from jax.experimental import pallas as pl
from jax.experimental.pallas import tpu as pltpu

F32 = jnp.float32
BF16 = jnp.bfloat16
MXU_DTYPE = jnp.bfloat16
SAVED_DTYPE = jnp.bfloat16
MESH_ID = pl.DeviceIdType.MESH

D_MODEL = 1024
HEAD_DIM = 64
Q_HEADS = 16
KV_HEADS = 2
GROUP = Q_HEADS // KV_HEADS
KV_WIDTH = KV_HEADS * HEAD_DIM
ATTN_BLOCK = 128
HGRN_HEADS = 8
HGRN_K = 128
CHUNK = 64
HGRN_TOKENS = 256
FFN = 2816
IN_SPLITS = (1024, 256, 4096, 2048)
EPS = 1e-6
NEG_INF = -1e30
ADAM_LR, ADAM_B1, ADAM_B2, ADAM_EPS, ADAM_WD, ADAM_STEP = 0.001, 0.9, 0.999, 1e-08, 0.01, 10
N_CHIPS = 4
VMEM_LIMIT = 60 * 1024 * 1024
ROW_ALIGN = 16


def _params(sem=None):
    return pltpu.CompilerParams(dimension_semantics=sem, vmem_limit_bytes=VMEM_LIMIT)


def _sigmoid(v):
    return 0.5 * jnp.tanh(0.5 * v) + 0.5


def _dot(a, b, dims):
    return lax.dot_general(a.astype(MXU_DTYPE), b.astype(MXU_DTYPE), (dims, ((), ())),
                           preferred_element_type=F32)


def _nn(a, b):
    return _dot(a, b, ((1,), (0,)))


def _nt(a, b):
    return _dot(a, b, ((1,), (1,)))


def _tn(a, b):
    return _dot(a, b, ((0,), (0,)))


HBM_SPEC = pl.BlockSpec(memory_space=pl.ANY)


class _Carried:
    def __init__(self, arrays, out_shapes, n_remote, n_local, build):
        self.parts = [(len(arrays), len(out_shapes), build)]
        self.arrays, self.out_shapes = list(arrays), list(out_shapes)
        self.scratch = [pltpu.SemaphoreType.DMA((n_remote,)), pltpu.SemaphoreType.DMA((n_remote,)),
                        pltpu.SemaphoreType.DMA((max(n_local, 1),))]

    def __add__(self, other):
        both = _Carried([], [], 1, 0, None)
        both.parts = self.parts + other.parts
        both.arrays, both.out_shapes = self.arrays + other.arrays, self.out_shapes + other.out_shapes
        both.scratch = self.scratch + other.scratch
        return both

    def _built(self, ins, outs, sems):
        for p, (ni, no, build) in enumerate(self.parts):
            yield build(ins[:ni], outs[:no], *sems[3 * p:3 * p + 3])
            ins, outs = ins[ni:], outs[no:]

    def start(self, ins, outs, sems):
        core = lax.axis_index("c")
        for sends, _, local, *other_order in self._built(ins, outs, sems):
            for cp in local:
                cp.start()
            if not other_order:
                for cp in sends:
                    cp.start()
                continue

            @pl.when(core == 0)
            def _():
                for cp in sends:
                    cp.start()

            @pl.when(core == 1)
            def _():
                for cp in other_order[0]:
                    cp.start()

    def wait(self, ins, outs, sems):
        for sends, recvs, local, *_ in self._built(ins, outs, sems):
            for cp in recvs:
                cp.wait_recv()
            for cp in sends:
                cp.wait_send()
            for cp in local:
                cp.wait()


def _join(*comms):
    comms = [c for c in comms if c is not None]
    return functools.reduce(lambda a, b: a + b, comms) if comms else None


def _call(name, body, *, grid, in_specs, out_specs, out_shape, args, scratch=(), semantics=None, comm=None,
          aliases=None):
    n_in, n_out, n_scr = len(in_specs), len(out_specs), len(scratch)
    aliases = aliases or {}
    if comm is None:
        res = pl.pallas_call(body, name=name, grid=grid, in_specs=in_specs, out_specs=out_specs, out_shape=out_shape,
                             scratch_shapes=list(scratch), input_output_aliases=aliases,
                             compiler_params=_params(semantics))(*args)
        return list(res), []
    ci, co = len(comm.arrays), len(comm.out_shapes)

    def carrying(*refs):
        ins, refs = refs[:n_in], refs[n_in:]
        c_ins, refs = refs[:ci], refs[ci:]
        outs, refs = refs[:n_out], refs[n_out:]
        c_outs, refs = refs[:co], refs[co:]
        scr, sems = refs[:n_scr], refs[n_scr:]
        if not grid:
            comm.start(c_ins, c_outs, sems)
            body(*ins, *outs, *scr)
            comm.wait(c_ins, c_outs, sems)
            return
        first = functools.reduce(jnp.logical_and, [pl.program_id(a) == 0 for a in range(len(grid))])
        last = functools.reduce(jnp.logical_and, [pl.program_id(a) == g - 1 for a, g in enumerate(grid)])

        @pl.when(first)
        def _():
            comm.start(c_ins, c_outs, sems)

        body(*ins, *outs, *scr)

        @pl.when(last)
        def _():
            comm.wait(c_ins, c_outs, sems)

    res = pl.pallas_call(
        carrying, name=name, grid=grid, in_specs=list(in_specs) + [HBM_SPEC] * ci,
        out_specs=list(out_specs) + [HBM_SPEC] * co, out_shape=list(out_shape) + comm.out_shapes,
        scratch_shapes=list(scratch) + comm.scratch, input_output_aliases=aliases,
        compiler_params=_params(("arbitrary",) * len(grid) if grid else None),
    )(*args, *comm.arrays)
    return list(res[:n_out]), list(res[n_out:])


_NO_COPIES = object()
_Rows = collections.namedtuple("_Rows", "array first rows")


_Into = collections.namedtuple("_Into", "rows first held")


def _matmul(name, pairs, mode, *, tm, tn, tk, outs, extras=(), epilogue=None, a_colsum=False, into=None,
            comm=_NO_COPIES):
    prod = dict(nn=_nn, nt=_nt, tn=_tn)[mode]
    pairs = [(a, b if isinstance(b, _Rows) else _Rows(b, 0, b.shape[0])) for a, b in pairs]
    a0, b0 = pairs[0]
    M = a0.shape[1] if mode == "tn" else a0.shape[0]
    N = b0.rows if mode == "nt" else b0.array.shape[1]
    steps, in_specs, offset = [], [], 0
    for a, b in pairs:
        K = a.shape[0] if mode == "tn" else a.shape[1]
        t = min(tk, K)
        assert K % t == 0, (name, K, t)
        kmap = functools.partial(lambda k, off, n: jnp.clip(k - off, 0, n - 1), off=offset, n=K // t)
        if mode == "tn":
            in_specs.append(pl.BlockSpec((t, tm), functools.partial(lambda i, j, k, f: (f(k), i), f=kmap)))
        else:
            in_specs.append(pl.BlockSpec((tm, t), functools.partial(lambda i, j, k, f: (i, f(k)), f=kmap)))
        whole = b.first == 0 and b.rows == b.array.shape[0]
        if mode == "nt":
            shape = (tn, t) if whole else (pl.Element(tn), pl.Element(t))
            in_specs.append(pl.BlockSpec(shape, functools.partial(
                lambda i, j, k, f, b, t, whole: (j, f(k)) if whole else (
                    pl.multiple_of(b.first + j * tn, ROW_ALIGN), pl.multiple_of(f(k) * t, 128)),
                f=kmap, b=b, t=t, whole=whole)))
        else:
            assert b.rows == K, (name, b.rows, K)
            shape = (t, tn) if whole else (pl.Element(t), pl.Element(tn))
            in_specs.append(pl.BlockSpec(shape, functools.partial(
                lambda i, j, k, f, b, t, whole: (f(k), j) if whole else (
                    pl.multiple_of(b.first + f(k) * t, ROW_ALIGN), pl.multiple_of(j * tn, 128)),
                f=kmap, b=b, t=t, whole=whole)))
        steps.append((offset, offset + K // t))
        offset += K // t
    assert M % tm == 0 and N % tn == 0, (name, M, N, tm, tn)
    ni, nj, nk = M // tm, N // tn, offset
    npair, ne, no = len(pairs), len(extras), len(outs)
    if epilogue is None:
        epilogue = lambda acc: (acc,)

    def finish(acc, extra_refs, out_refs):
        vals = epilogue(acc, *[r[...] for r in extra_refs])
        for (kind, _), o_ref, val in zip(outs, out_refs, vals):
            if kind == "pn":
                val = jnp.broadcast_to(val, o_ref.shape)
            o_ref[...] = val.astype(o_ref.dtype)

    held = list(into.held) if into is not None and into.held is not None else []

    def body(*refs):
        ab, rest = refs[:2 * npair], refs[2 * npair:]
        extra_refs, rest = rest[:ne], rest[ne + len(held):]
        out_refs = rest[:no]
        if nk == 1:
            finish(prod(ab[0][...], ab[1][...]), extra_refs, out_refs)
            return
        sums_ref, acc_ref = rest[no], rest[-1]
        k = pl.program_id(2)

        @pl.when(k == 0)
        def _():
            acc_ref[...] = jnp.zeros_like(acc_ref)
            if a_colsum:
                sums_ref[...] = jnp.zeros((1, tm), F32)

        if a_colsum:
            sums_ref[...] += jnp.sum(ab[0][...].astype(F32), axis=0, keepdims=True)
        for p, (lo, hi) in enumerate(steps):
            @pl.when(jnp.logical_and(k >= lo, k < hi))
            def _():
                acc_ref[...] += prod(ab[2 * p][...], ab[2 * p + 1][...])

        @pl.when(k == nk - 1)
        def _():
            finish(acc_ref[...], extra_refs, out_refs)

    for _, shape, im in extras:
        in_specs.append(pl.BlockSpec(shape, functools.partial(lambda i, j, k, im: im(i, j), im=im)))
    in_specs += [HBM_SPEC] * len(held)
    aliases = {2 * npair + ne + p: p for p in range(len(held))}
    out_shape, out_specs = [], []
    for kind, dt in outs:
        if kind == "mn" and into is not None:
            out_shape.append(jax.ShapeDtypeStruct((into.rows, N), dt))
            out_specs.append(pl.BlockSpec((pl.Element(tm), pl.Element(tn)), lambda i, j, k: (
                pl.multiple_of(into.first + i * tm, ROW_ALIGN), pl.multiple_of(j * tn, 128))))
        elif kind == "mn":
            out_shape.append(jax.ShapeDtypeStruct((M, N), dt))
            out_specs.append(pl.BlockSpec((tm, tn), lambda i, j, k: (i, j)))
        else:
            out_shape.append(jax.ShapeDtypeStruct((8 * ni, N), dt))
            out_specs.append(pl.BlockSpec((8, tn), lambda i, j, k: (i, j)))
    if a_colsum:
        assert mode == "tn" and npair == 1 and nj == 1 and nk > 1, name
        out_shape.append(jax.ShapeDtypeStruct((1, M), F32))
        out_specs.append(pl.BlockSpec((1, tm), lambda i, j, k: (0, i)))
    grid = (ni, nj, nk)
    if nj > 1 and nk == 1:
        turned = lambda spec: pl.BlockSpec(spec.block_shape, functools.partial(
            lambda j, i, k, im: im(i, j, k), im=spec.index_map))
        in_specs, out_specs, grid = [turned(s) for s in in_specs], [turned(s) for s in out_specs], (nj, ni, nk)
    res, got = _call(name, body, grid=grid, in_specs=in_specs, out_specs=out_specs, out_shape=out_shape,
                     args=[t for a, b in pairs for t in (a, b.array)] + [e[0] for e in extras] + held,
                     scratch=[pltpu.VMEM((tm, tn), F32)] if nk > 1 else [], aliases=aliases,
                     semantics=("parallel", "parallel", "arbitrary"), comm=None if comm is _NO_COPIES else comm)
    return res if comm is _NO_COPIES else (res, got)


def _ffn_fwd(merged, w_out, x, gain, w_gate_t, w_up_t, *, tm, comm=None):
    (T, D), F = x.shape, w_gate_t.shape[0]

    def body(m_ref, wo_ref, x_ref, g_ref, wg_ref, wu_ref, h_ref, u_ref, gate_ref, up_ref, z_ref):
        h = x_ref[...] + _nn(m_ref[...], wo_ref[...])
        h_ref[...] = h
        u = (h * lax.rsqrt(jnp.mean(h * h, axis=-1, keepdims=True) + EPS) * g_ref[...]).astype(u_ref.dtype)
        u_ref[...] = u
        gate, up = _nt(u, wg_ref[...]), _nt(u, wu_ref[...])
        gate_ref[...], up_ref[...] = gate.astype(gate_ref.dtype), up.astype(up_ref.dtype)
        z_ref[...] = (gate * _sigmoid(gate) * up).astype(z_ref.dtype)

    rows = lambda n: pl.BlockSpec((tm, n), lambda i: (i, 0))
    fixed = lambda a: pl.BlockSpec(a.shape, lambda i: (0, 0))
    return _call("ffn_hidden", body, grid=(T // tm,),
                 in_specs=[rows(D), fixed(w_out), rows(D), fixed(gain), fixed(w_gate_t), fixed(w_up_t)],
                 out_specs=[rows(D), rows(D), rows(F), rows(F), rows(F)],
                 out_shape=[jax.ShapeDtypeStruct((T, D), F32), jax.ShapeDtypeStruct((T, D), MXU_DTYPE)]
                 + [jax.ShapeDtypeStruct((T, F), SAVED_DTYPE)] * 2 + [jax.ShapeDtypeStruct((T, F), MXU_DTYPE)],
                 args=[merged, w_out, x, gain, w_gate_t, w_up_t], semantics=("parallel",), comm=comm)


def _in_proj(x, gain, w_in_t, b_in, *, tm, comm=None):
    T, D = x.shape
    bounds = [sum(IN_SPLITS[:i]) for i in range(len(IN_SPLITS) + 1)]

    def body(x_ref, g_ref, w_ref, b_ref, u_ref, *piece_refs):
        xv = x_ref[...]
        r = lax.rsqrt(jnp.mean(xv * xv, axis=-1, keepdims=True) + EPS)
        u = (xv * r * g_ref[...]).astype(u_ref.dtype)
        u_ref[...] = u
        for o_ref, lo, hi in zip(piece_refs, bounds[:-1], bounds[1:]):
            o_ref[...] = (_nt(u, w_ref[lo:hi, :]) + b_ref[:, lo:hi]).astype(o_ref.dtype)

    rows = lambda n: pl.BlockSpec((tm, n), lambda i: (i, 0))
    fixed = lambda a: pl.BlockSpec(a.shape, lambda i: (0, 0))
    dtypes = (MXU_DTYPE, MXU_DTYPE, F32, F32)
    return _call("in_proj", body, grid=(T // tm,),
                 in_specs=[rows(D), fixed(gain), fixed(w_in_t), fixed(b_in)],
                 out_specs=[rows(D)] + [rows(n) for n in IN_SPLITS],
                 out_shape=[jax.ShapeDtypeStruct((T, D), MXU_DTYPE)]
                 + [jax.ShapeDtypeStruct((T, n), dt) for n, dt in zip(IN_SPLITS, dtypes)],
                 args=[x, gain, w_in_t, b_in], semantics=("parallel",), comm=comm)


def _row_spec(tm, n):
    return pl.BlockSpec((tm, n), lambda i: (i, 0))


def _fixed_spec(a):
    return pl.BlockSpec(a.shape, lambda i: (0,) * a.ndim)


def _partials_spec(n):
    return pl.BlockSpec((8, n), lambda i: (i, 0))


def _ffn_tail(z, w_down, h1, target, gain, gate, up, *, tm):
    (T, F), D = z.shape, h1.shape[1]

    def body(z_ref, w_ref, h_ref, t_ref, g_ref, gate_ref, up_ref, dh_ref, dhb_ref, dgate_ref, dup_ref, dg_ref, l_ref):
        h2 = h_ref[...] + _nn(z_ref[...], w_ref[...])
        r = lax.rsqrt(jnp.mean(h2 * h2, axis=-1, keepdims=True) + EPS)
        xhat = h2 * r
        err = xhat * g_ref[...] - t_ref[...]
        part = 0.5 * jnp.sum(jnp.sum(err * err, axis=-1, keepdims=True), axis=0, keepdims=True) / D
        dy = err / D
        dxh = dy * g_ref[...]
        dh2 = r * (dxh - xhat * jnp.mean(dxh * xhat, axis=-1, keepdims=True))
        dh_ref[...] = dh2
        dhb = dh2.astype(dhb_ref.dtype)
        dhb_ref[...] = dhb
        dg_ref[...] = jnp.broadcast_to(jnp.sum(dy * xhat, axis=0, keepdims=True), dg_ref.shape)
        l_ref[...] = jnp.broadcast_to(part, l_ref.shape)
        dz = _nt(dhb, w_ref[...])
        gv, upv = gate_ref[...].astype(F32), up_ref[...].astype(F32)
        s = _sigmoid(gv)
        dgate_ref[...] = (dz * upv * (s * (1.0 + gv * (1.0 - s)))).astype(dgate_ref.dtype)
        dup_ref[...] = (dz * (gv * s)).astype(dup_ref.dtype)

    low = lambda n: jax.ShapeDtypeStruct((T, n), MXU_DTYPE)
    part = jax.ShapeDtypeStruct((8 * (T // tm), D), F32)
    return pl.pallas_call(
        body, name="ffn_tail", grid=(T // tm,),
        in_specs=[_row_spec(tm, F), _fixed_spec(w_down), _row_spec(tm, D), _row_spec(tm, D), _fixed_spec(gain),
                  _row_spec(tm, F), _row_spec(tm, F)],
        out_specs=[_row_spec(tm, D), _row_spec(tm, D), _row_spec(tm, F), _row_spec(tm, F), _partials_spec(D),
                   _partials_spec(D)],
        out_shape=[jax.ShapeDtypeStruct((T, D), F32), low(D), low(F), low(F), part, part],
        compiler_params=_params(("parallel",)),
    )(z, w_down, h1, target, gain, gate, up)


def _ffn_in_bwd(dgate, dup, w_gate_t, w_up_t, h1, gain, dres, *, tm, comm=None):
    (T, F), D = dgate.shape, h1.shape[1]

    def body(dg_ref, du_ref, wg_ref, wu_ref, h_ref, g_ref, r_ref, dh_ref, dhb_ref, dgain_ref):
        d_u2 = _nn(dg_ref[...], wg_ref[...]) + _nn(du_ref[...], wu_ref[...])
        dx, dgain = _rmsnorm_bwd_vals(d_u2, h_ref[...], g_ref[...])
        dh = r_ref[...] + dx
        dh_ref[...] = dh
        dhb_ref[...] = dh.astype(dhb_ref.dtype)
        dgain_ref[...] = jnp.broadcast_to(dgain, dgain_ref.shape)

    return _call("d_ffn_in", body, grid=(T // tm,),
                 in_specs=[_row_spec(tm, F), _row_spec(tm, F), _fixed_spec(w_gate_t), _fixed_spec(w_up_t),
                           _row_spec(tm, D), _fixed_spec(gain), _row_spec(tm, D)],
                 out_specs=[_row_spec(tm, D), _row_spec(tm, D), _partials_spec(D)],
                 out_shape=[jax.ShapeDtypeStruct((T, D), F32), jax.ShapeDtypeStruct((T, D), MXU_DTYPE),
                            jax.ShapeDtypeStruct((8 * (T // tm), D), F32)],
                 args=[dgate, dup, w_gate_t, w_up_t, h1, gain, dres], semantics=("parallel",), comm=comm)


def _in_proj_bwd(pieces, w_in_t, x, gain, dres, *, tm, comm=None):
    T, D = x.shape
    n = len(pieces)

    def body(*refs):
        dps, (w_ref, x_ref, g_ref, r_ref, dx_ref, dgain_ref) = refs[:n], refs[n:]
        d_u = None
        for dp_ref, (dp, first) in zip(dps, pieces):
            term = _nn(dp_ref[...], w_ref[first:first + dp.shape[1], :])
            d_u = term if d_u is None else d_u + term
        dx, dgain = _rmsnorm_bwd_vals(d_u, x_ref[...], g_ref[...])
        dx_ref[...] = r_ref[...] + dx
        dgain_ref[...] = jnp.broadcast_to(dgain, dgain_ref.shape)

    return _call("d_u", body, grid=(T // tm,),
                 in_specs=[_row_spec(tm, dp.shape[1]) for dp, _ in pieces]
                 + [_fixed_spec(w_in_t), _row_spec(tm, D), _fixed_spec(gain), _row_spec(tm, D)],
                 out_specs=[_row_spec(tm, D), _partials_spec(D)],
                 out_shape=[jax.ShapeDtypeStruct((T, D), F32), jax.ShapeDtypeStruct((8 * (T // tm), D), F32)],
                 args=[dp for dp, _ in pieces] + [w_in_t, x, gain, dres], semantics=("parallel",), comm=comm)


def _merge_fwd(y_a, y_b, w_a, w_b, gates, *, tm):
    T, D = y_a.shape

    def body(ya_ref, yb_ref, wa_ref, wb_ref, ga_ref, gb_ref, pa_ref, pb_ref, m_ref):
        pa, pb = _nn(ya_ref[...], wa_ref[...]), _nn(yb_ref[...], wb_ref[...])
        pa_ref[...], pb_ref[...] = pa.astype(pa_ref.dtype), pb.astype(pb_ref.dtype)
        m_ref[...] = (_sigmoid(ga_ref[...]) * pa + _sigmoid(gb_ref[...]) * pb).astype(m_ref.dtype)

    rows = pl.BlockSpec((tm, D), lambda i: (i, 0))
    whole = pl.BlockSpec((D, D), lambda i: (0, 0))
    return pl.pallas_call(
        body, name="branch_merge", grid=(T // tm,),
        in_specs=[rows, rows, whole, whole, rows, pl.BlockSpec((tm, D), lambda i: (i, 1))], out_specs=[rows] * 3,
        out_shape=[jax.ShapeDtypeStruct((T, D), SAVED_DTYPE)] * 2 + [jax.ShapeDtypeStruct((T, D), MXU_DTYPE)],
        compiler_params=_params(("parallel",)),
    )(y_a, y_b, w_a, w_b, gates, gates)


def _merge_bwd(dh, w_out, w_a, w_b, p_a, p_b, gates, *, tm):
    T, D = dh.shape

    def body(dh_ref, wo_ref, wa_ref, wb_ref, pa_ref, pb_ref, ga_ref, gb_ref, dpa_ref, dpb_ref, dga_ref, dgb_ref,
             dya_ref, dyb_ref):
        dm = _nt(dh_ref[...], wo_ref[...])
        sa, sb = _sigmoid(ga_ref[...]), _sigmoid(gb_ref[...])
        dpa, dpb = (dm * sa).astype(dpa_ref.dtype), (dm * sb).astype(dpb_ref.dtype)
        dpa_ref[...], dpb_ref[...] = dpa, dpb
        dga_ref[...] = (dm * pa_ref[...].astype(F32) * sa * (1.0 - sa)).astype(dga_ref.dtype)
        dgb_ref[...] = (dm * pb_ref[...].astype(F32) * sb * (1.0 - sb)).astype(dgb_ref.dtype)
        dya_ref[...] = _nt(dpa, wa_ref[...]).astype(dya_ref.dtype)
        dyb_ref[...] = _nt(dpb, wb_ref[...]).astype(dyb_ref.dtype)

    rows = pl.BlockSpec((tm, D), lambda i: (i, 0))
    whole = pl.BlockSpec((D, D), lambda i: (0, 0))
    low = jax.ShapeDtypeStruct((T, D), MXU_DTYPE)
    return pl.pallas_call(
        body, name="d_branch_merge", grid=(T // tm,),
        in_specs=[rows, whole, whole, whole, rows, rows, rows, pl.BlockSpec((tm, D), lambda i: (i, 1))],
        out_specs=[rows] * 6, out_shape=[low] * 5 + [jax.ShapeDtypeStruct((T, D), F32)],
        compiler_params=_params(("parallel",)),
    )(dh, w_out, w_a, w_b, p_a, p_b, gates, gates)


def _colsum_partials(p):
    return jnp.sum(p.reshape(-1, 8, p.shape[-1])[:, 0, :], axis=0, keepdims=True)


def _rmsnorm_bwd_vals(dy, xin, g):
    rstd = lax.rsqrt(jnp.mean(xin * xin, axis=-1, keepdims=True) + EPS)
    xhat = xin * rstd
    dg = jnp.sum(dy * xhat, axis=0, keepdims=True)
    dxh = dy * g
    dx = rstd * (dxh - xhat * jnp.mean(dxh * xhat, axis=-1, keepdims=True))
    return dx, dg


def _colsum(name, a, tr=512, comm=_NO_COPIES):
    T, N = a.shape

    def body(a_ref, o_ref):
        @pl.when(pl.program_id(0) == 0)
        def _():
            o_ref[...] = jnp.zeros_like(o_ref)

        o_ref[...] += jnp.sum(a_ref[...].astype(F32), axis=0, keepdims=True)

    (res,), got = _call(name, body, grid=(T // tr,), in_specs=[pl.BlockSpec((tr, N), lambda i: (i, 0))],
                        out_specs=[pl.BlockSpec((1, N), lambda i: (0, 0))],
                        out_shape=[jax.ShapeDtypeStruct((1, N), F32)], args=[a], semantics=("arbitrary",),
                        comm=None if comm is _NO_COPIES else comm)
    return res if comm is _NO_COPIES else (res, got)


ATTN_SCALE = 1.0 / math.sqrt(HEAD_DIM)
GROUP_LANES = GROUP * ATTN_BLOCK
PAIR = 2 * HEAD_DIM


def _attn_mask():
    kj = lax.broadcasted_iota(jnp.int32, (ATTN_BLOCK, GROUP_LANES), 0)
    qi = lax.broadcasted_iota(jnp.int32, (ATTN_BLOCK, GROUP_LANES), 1) & (ATTN_BLOCK - 1)
    return kj <= qi


def _heads_transposed(ref, g, scale=None):
    parts = []
    for a in range(GROUP // 2):
        lo = (g * GROUP // 2 + a) * PAIR
        pair = ref[:, lo:lo + PAIR].astype(F32)
        pair = (pair if scale is None else pair * scale).T
        parts += [pair[:HEAD_DIM], pair[HEAD_DIM:]]
    return jnp.concatenate(parts, axis=1).astype(MXU_DTYPE)


def _heads_back(ref, g, vt):
    for a in range(GROUP // 2):
        lo = (g * GROUP // 2 + a) * PAIR
        pair = jnp.concatenate([vt[:, (2 * a) * ATTN_BLOCK:(2 * a + 1) * ATTN_BLOCK],
                                vt[:, (2 * a + 1) * ATTN_BLOCK:(2 * a + 2) * ATTN_BLOCK]], axis=0)
        ref[:, lo:lo + PAIR] = pair.T.astype(ref.dtype)


def _kv_parts(kv_ref, g):
    ks = slice(g * HEAD_DIM, (g + 1) * HEAD_DIM)
    vs = slice(KV_WIDTH + g * HEAD_DIM, KV_WIDTH + (g + 1) * HEAD_DIM)
    return kv_ref[:, ks].astype(MXU_DTYPE), kv_ref[:, vs].astype(MXU_DTYPE)


def _sink_rows(sinks):
    return jnp.repeat(sinks.reshape(KV_HEADS, GROUP), ATTN_BLOCK, axis=1)


def _attn_fwd(pq, pkv, sinks, comm=None):
    T = pq.shape[0]
    nb = T // ATTN_BLOCK

    def body(q_ref, kvc_ref, kvp_ref, s_ref, y_ref, lse_ref):
        mask_c = _attn_mask()
        has_prev = pl.program_id(0) > 0
        for g in range(KV_HEADS):
            (kc, vc), (kp, vp) = _kv_parts(kvc_ref, g), _kv_parts(kvp_ref, g)
            qt = _heads_transposed(q_ref, g, ATTN_SCALE)
            s = jnp.where(mask_c, _nn(kc, qt), jnp.where(has_prev, _nn(kp, qt), NEG_INF))
            sink = s_ref[g:g + 1, :]
            m = jnp.maximum(jnp.max(s, axis=0, keepdims=True), sink)
            p = jnp.exp(s - m)
            den = jnp.sum(p, axis=0, keepdims=True) + jnp.exp(sink - m)
            pc = jnp.where(mask_c, p, 0.0)
            _heads_back(y_ref, g, (_tn(vc, pc) + _tn(vp, p - pc)) / den)
            lse = m + jnp.log(den)
            for i in range(GROUP):
                lse_ref[g * GROUP + i:g * GROUP + i + 1, :] = lse[:, i * ATTN_BLOCK:(i + 1) * ATTN_BLOCK]

    return _call(
        "attn_fwd", body, grid=(nb,),
        in_specs=[pl.BlockSpec((ATTN_BLOCK, D_MODEL), lambda n: (n, 0)),
                  pl.BlockSpec((ATTN_BLOCK, 2 * KV_WIDTH), lambda n: (n, 0)),
                  pl.BlockSpec((ATTN_BLOCK, 2 * KV_WIDTH), lambda n: (jnp.maximum(n - 1, 0), 0)),
                  pl.BlockSpec((KV_HEADS, GROUP_LANES), lambda n: (0, 0))],
        out_specs=[pl.BlockSpec((ATTN_BLOCK, D_MODEL), lambda n: (n, 0)),
                   pl.BlockSpec((Q_HEADS, ATTN_BLOCK), lambda n: (0, n))],
        out_shape=[jax.ShapeDtypeStruct((T, D_MODEL), MXU_DTYPE), jax.ShapeDtypeStruct((Q_HEADS, T), F32)],
        args=[pq, pkv, pkv, _sink_rows(sinks)], semantics=("parallel",), comm=comm)


def _attn_bwd(pq, pkv, sinks, lse, dy, comm=None):
    T = pq.shape[0]
    nb = T // ATTN_BLOCK
    cur = lambda n: (jnp.minimum(n, nb - 1), 0)

    def body(q_ref, kvc_ref, kvp_ref, s_ref, lse_ref, dy_ref, dq_ref, dkv_ref, ds_ref, carry, top, bot):
        n = pl.program_id(0)

        @pl.when(n == 0)
        def _():
            carry[...] = jnp.zeros_like(carry)
            ds_ref[...] = jnp.zeros_like(ds_ref)

        @pl.when(n < nb)
        def _():
            mask_c = _attn_mask()
            valid = jnp.logical_or(mask_c, n > 0)
            for g in range(KV_HEADS):
                ks = slice(g * HEAD_DIM, (g + 1) * HEAD_DIM)
                vs = slice(KV_WIDTH + g * HEAD_DIM, KV_WIDTH + (g + 1) * HEAD_DIM)
                (kc, vc), (kp, vp) = _kv_parts(kvc_ref, g), _kv_parts(kvp_ref, g)
                qt = _heads_transposed(q_ref, g, ATTN_SCALE)
                dot = _heads_transposed(dy_ref, g)
                lse = jnp.concatenate([lse_ref[g * GROUP + i:g * GROUP + i + 1, :] for i in range(GROUP)], axis=1)
                p = jnp.where(valid, jnp.exp(jnp.where(mask_c, _nn(kc, qt), _nn(kp, qt)) - lse), 0.0)
                dp = jnp.where(mask_c, _nn(vc, dot), _nn(vp, dot))
                delta = jnp.sum(p * dp, axis=0, keepdims=True)
                ds = p * (dp - delta)
                ds_c, p_c = jnp.where(mask_c, ds, 0.0), jnp.where(mask_c, p, 0.0)
                ds_p, p_p = ds - ds_c, p - p_c
                _heads_back(dq_ref, g, (_tn(kc, ds_c) + _tn(kp, ds_p)) * ATTN_SCALE)
                bot[:, ks], bot[:, vs] = _nt(ds_c, qt), _nt(p_c, dot)
                top[:, ks], top[:, vs] = _nt(ds_p, qt), _nt(p_p, dot)
                ds_ref[g:g + 1, :] -= jnp.exp(s_ref[g:g + 1, :] - lse) * delta
            dkv_ref[...] = (carry[...] + top[...]).astype(dkv_ref.dtype)
            carry[...] = bot[...]

        @pl.when(n == nb)
        def _():
            dkv_ref[...] = carry[...].astype(dkv_ref.dtype)

    return _call(
        "attn_bwd", body, grid=(nb + 1,),
        in_specs=[pl.BlockSpec((ATTN_BLOCK, D_MODEL), cur),
                  pl.BlockSpec((ATTN_BLOCK, 2 * KV_WIDTH), cur),
                  pl.BlockSpec((ATTN_BLOCK, 2 * KV_WIDTH), lambda n: (jnp.maximum(jnp.minimum(n, nb - 1) - 1, 0), 0)),
                  pl.BlockSpec((KV_HEADS, GROUP_LANES), lambda n: (0, 0)),
                  pl.BlockSpec((Q_HEADS, ATTN_BLOCK), lambda n: (0, jnp.minimum(n, nb - 1))),
                  pl.BlockSpec((ATTN_BLOCK, D_MODEL), cur)],
        out_specs=[pl.BlockSpec((ATTN_BLOCK, D_MODEL), cur),
                   pl.BlockSpec((ATTN_BLOCK, 2 * KV_WIDTH), lambda n: (jnp.maximum(n - 1, 0), 0)),
                   pl.BlockSpec((KV_HEADS, GROUP_LANES), lambda n: (0, 0))],
        out_shape=[jax.ShapeDtypeStruct((T, D_MODEL), MXU_DTYPE),
                   jax.ShapeDtypeStruct((T, 2 * KV_WIDTH), MXU_DTYPE),
                   jax.ShapeDtypeStruct((KV_HEADS, GROUP_LANES), F32)],
        scratch=[pltpu.VMEM((ATTN_BLOCK, 2 * KV_WIDTH), F32)] * 3,
        args=[pq, pkv, pkv, _sink_rows(sinks), lse, dy], semantics=("arbitrary",), comm=comm)


def _lower_bound(l):
    m = jnp.maximum(l[0:1], l[1:2])
    e0, e1 = jnp.exp(l[0:1] - m), jnp.exp(l[1:2] - m)
    return e0 / (e0 + e1)


def _tri(lower):
    r = lax.broadcasted_iota(jnp.int32, (CHUNK, CHUNK), 0)
    c = lax.broadcasted_iota(jnp.int32, (CHUNK, CHUNK), 1)
    return (r >= c) if lower else (c >= r)


def _chunk_sum(mask, v):
    ones = mask.astype(BF16)
    hi = v.astype(BF16)
    rest = v - hi.astype(F32)
    mid = rest.astype(BF16)
    lo = (rest - mid.astype(F32)).astype(BF16)
    part = lambda t: lax.dot_general(ones, t, (((1,), (0,)), ((), ())), preferred_element_type=F32)
    return part(hi) + part(mid) + part(lo)


def _hgrn_chunk_inputs(hq, hf, lb, causal):
    half_t = 0.5 * jnp.tanh(0.5 * hf)
    sg, sgn = 0.5 + half_t, 0.5 - half_t
    f = lb + (1.0 - lb) * sg
    kk = (1.0 - lb) * sgn
    sq = _sigmoid(hq)
    q = hq * sq
    b = _chunk_sum(causal, jnp.log(f))
    bm, bl = b[CHUNK // 2 - 1:CHUNK // 2, :], b[CHUNK - 1:CHUNK, :]
    e_qm, e_km = jnp.exp(b - bm), jnp.exp(bm - b)
    e_qs, e_kl = e_qm * jnp.exp(bm), e_km * jnp.exp(bl - bm)
    return dict(sg=sg, sgn=sgn, f=f, kk=kk, sq=sq, q=q, e_qm=e_qm, e_km=e_km, e_qs=e_qs, e_kl=e_kl,
                qm=q * e_qm, km=kk * e_km, qs=q * e_qs, kl=kk * e_kl, el=jnp.exp(bl))


def _hgrn_fwd(ph, lb_logits, norm_g, comm=None):
    T = ph.shape[0]
    nblk, cpb = T // HGRN_TOKENS, HGRN_TOKENS // CHUNK
    col = lambda c: pl.BlockSpec((HGRN_TOKENS, D_MODEL), functools.partial(lambda i, c: (i, c), c=c))

    def body(hq_ref, hf_ref, hi_ref, hg_ref, l_ref, ng_ref, y_ref, o_ref, st_ref, s_ref):
        @pl.when(pl.program_id(0) == 0)
        def _():
            s_ref[...] = jnp.zeros_like(s_ref)

        lb = _lower_bound(l_ref[...])
        causal = _tri(True)
        for c in range(cpb):
            rows = slice(c * CHUNK, (c + 1) * CHUNK)
            t = _hgrn_chunk_inputs(hq_ref[rows, :], hf_ref[rows, :], lb, causal)
            qm, km, qs, kl = (t[n].astype(MXU_DTYPE) for n in ("qm", "km", "qs", "kl"))
            v = hi_ref[rows, :].astype(MXU_DTYPE)
            for h in range(HGRN_HEADS):
                ls = slice(h * HGRN_K, (h + 1) * HGRN_K)
                st = s_ref[h]
                st_ref[c, ls, :] = st
                a = jnp.where(causal, _nt(qm[:, ls], km[:, ls]), 0.0)
                o_ref[rows, ls] = _nn(a, v[:, ls]) + _nt(qs[:, ls], st)
                s_ref[h] = t["el"][:, ls] * st + _tn(v[:, ls], kl[:, ls])
        for h in range(HGRN_HEADS):
            ls = slice(h * HGRN_K, (h + 1) * HGRN_K)
            o = o_ref[:, ls]
            r = lax.rsqrt(jnp.mean(o * o, axis=-1, keepdims=True) + EPS)
            y_ref[:, ls] = (o * r * ng_ref[:, ls] * _sigmoid(hg_ref[:, ls])).astype(y_ref.dtype)

    return _call(
        "hgrn_fwd", body, grid=(nblk,),
        in_specs=[col(0), col(1), col(2), col(3),
                  pl.BlockSpec((2, D_MODEL), lambda i: (0, 0)), pl.BlockSpec((1, D_MODEL), lambda i: (0, 0))],
        out_specs=[pl.BlockSpec((HGRN_TOKENS, D_MODEL), lambda i: (i, 0)),
                   pl.BlockSpec((HGRN_TOKENS, D_MODEL), lambda i: (i, 0)),
                   pl.BlockSpec((cpb, D_MODEL, HGRN_K), lambda i: (i, 0, 0))],
        out_shape=[jax.ShapeDtypeStruct((T, D_MODEL), MXU_DTYPE), jax.ShapeDtypeStruct((T, D_MODEL), F32),
                   jax.ShapeDtypeStruct((T // CHUNK, D_MODEL, HGRN_K), F32)],
        scratch=[pltpu.VMEM((HGRN_HEADS, HGRN_K, HGRN_K), F32)],
        args=[ph, ph, ph, ph, lb_logits, norm_g], semantics=("arbitrary",), comm=comm)


def _hgrn_bwd(ph, o_raw, states, dy, lb_logits, norm_g, comm=None):
    T = ph.shape[0]
    nblk, cpb = T // HGRN_TOKENS, HGRN_TOKENS // CHUNK
    rev = lambda i: nblk - 1 - i
    col = lambda c: pl.BlockSpec((HGRN_TOKENS, D_MODEL), functools.partial(lambda i, c: (rev(i), c), c=c))
    tok = pl.BlockSpec((HGRN_TOKENS, D_MODEL), lambda i: (rev(i), 0))

    def body(hq_ref, hf_ref, hi_ref, hg_ref, o_ref, st_ref, dy_ref, l_ref, ng_ref,
             dph_ref, dng_ref, dl_ref, dst_ref, dlb_ref, do_s, dqm_s, dkm_s, dqs_s, dkl_s, dv_s, del_s):
        i = pl.program_id(0)

        @pl.when(i == 0)
        def _():
            dst_ref[...] = jnp.zeros_like(dst_ref)
            dlb_ref[...] = jnp.zeros_like(dlb_ref)
            dng_ref[...] = jnp.zeros_like(dng_ref)

        lb = _lower_bound(l_ref[...])
        causal, anti = _tri(True), _tri(False)
        row = lax.broadcasted_iota(jnp.int32, (CHUNK, D_MODEL), 0)
        for c in reversed(range(cpb)):
            rows = slice(c * CHUNK, (c + 1) * CHUNK)
            hq = hq_ref[rows, :]
            t = _hgrn_chunk_inputs(hq, hf_ref[rows, :], lb, causal)
            sgg = _sigmoid(hg_ref[rows, :])
            dyv = dy_ref[rows, :]
            for h in range(HGRN_HEADS):
                ls = slice(h * HGRN_K, (h + 1) * HGRN_K)
                o = o_ref[rows, ls]
                r = lax.rsqrt(jnp.mean(o * o, axis=-1, keepdims=True) + EPS)
                nrm = o * r
                g_h = sgg[:, ls]
                dph_ref[rows, 3 * D_MODEL + h * HGRN_K:3 * D_MODEL + (h + 1) * HGRN_K] = (
                    dyv[:, ls] * nrm * ng_ref[:, ls] * g_h * (1.0 - g_h)).astype(dph_ref.dtype)
                dyg = dyv[:, ls] * g_h
                dng_ref[:, ls] += jnp.sum(dyg * nrm, axis=0, keepdims=True)
                dn = dyg * ng_ref[:, ls]
                do_s[:, ls] = r * (dn - nrm * jnp.mean(dn * nrm, axis=-1, keepdims=True))
            qm, km, qs, kl = (t[n].astype(MXU_DTYPE) for n in ("qm", "km", "qs", "kl"))
            v = hi_ref[rows, :].astype(MXU_DTYPE)
            do = do_s[...].astype(MXU_DTYPE)
            for h in range(HGRN_HEADS):
                ls = slice(h * HGRN_K, (h + 1) * HGRN_K)
                st = st_ref[c, ls, :]
                dst = dst_ref[h]
                a = jnp.where(causal, _nt(qm[:, ls], km[:, ls]), 0.0)
                da = jnp.where(causal, _nt(do[:, ls], v[:, ls]), 0.0)
                dv_s[:, ls] = _tn(a, do[:, ls]) + _nt(kl[:, ls], dst)
                dkl_s[:, ls] = _nn(v[:, ls], dst)
                dqs_s[:, ls] = _nn(do[:, ls], st)
                del_s[:, ls] = jnp.sum(dst * st, axis=0, keepdims=True)
                dst_ref[h] = _tn(do[:, ls], qs[:, ls]) + t["el"][:, ls] * dst
                dqm_s[:, ls] = _nn(da, km[:, ls])
                dkm_s[:, ls] = _tn(da, qm[:, ls])
            dqm, dkm, dqs, dkl = dqm_s[...], dkm_s[...], dqs_s[...], dkl_s[...]
            dq = dqm * t["e_qm"] + dqs * t["e_qs"]
            dk = dkm * t["e_km"] + dkl * t["e_kl"]
            t_qm, t_km, t_kl = dqm * t["qm"], dkm * t["km"], dkl * t["kl"]
            db = t_qm - t_km + dqs * t["qs"] - t_kl
            db_mid = jnp.sum(t_km - t_qm, axis=0, keepdims=True)
            db_last = jnp.sum(t_kl, axis=0, keepdims=True) + del_s[...] * t["el"]
            db = db + jnp.where(row == CHUNK // 2 - 1, db_mid, 0.0) + jnp.where(row == CHUNK - 1, db_last, 0.0)
            dlogf = _chunk_sum(anti, db)
            sq, sg, sgn, f = t["sq"], t["sg"], t["sgn"], t["f"]
            dph_ref[rows, 0:D_MODEL] = (dq * (sq * (1.0 + hq * (1.0 - sq)))).astype(dph_ref.dtype)
            dph_ref[rows, D_MODEL:2 * D_MODEL] = (
                dlogf * (1.0 - lb) * sg * (1.0 - sg) / f - dk * (1.0 - lb) * sgn * (1.0 - sgn)).astype(dph_ref.dtype)
            dph_ref[rows, 2 * D_MODEL:3 * D_MODEL] = dv_s[...].astype(dph_ref.dtype)
            dlb_ref[...] += jnp.sum(dlogf * (1.0 - sg) / f - dk * sgn, axis=0, keepdims=True)

        @pl.when(i == nblk - 1)
        def _():
            dl0 = dlb_ref[...] * lb * (1.0 - lb)
            dl_ref[0:1, :] = dl0
            dl_ref[1:2, :] = -dl0

    wide = pltpu.VMEM((CHUNK, D_MODEL), F32)
    return _call(
        "hgrn_bwd", body, grid=(nblk,),
        in_specs=[col(0), col(1), col(2), col(3), tok,
                  pl.BlockSpec((cpb, D_MODEL, HGRN_K), lambda i: (rev(i), 0, 0)), tok,
                  pl.BlockSpec((2, D_MODEL), lambda i: (0, 0)), pl.BlockSpec((1, D_MODEL), lambda i: (0, 0))],
        out_specs=[pl.BlockSpec((HGRN_TOKENS, 4 * D_MODEL), lambda i: (rev(i), 0)),
                   pl.BlockSpec((1, D_MODEL), lambda i: (0, 0)), pl.BlockSpec((2, D_MODEL), lambda i: (0, 0))],
        out_shape=[jax.ShapeDtypeStruct((T, 4 * D_MODEL), MXU_DTYPE), jax.ShapeDtypeStruct((1, D_MODEL), F32),
                   jax.ShapeDtypeStruct((2, D_MODEL), F32)],
        scratch=[pltpu.VMEM((HGRN_HEADS, HGRN_K, HGRN_K), F32), pltpu.VMEM((1, D_MODEL), F32),
                 wide, wide, wide, wide, wide, wide, pltpu.VMEM((1, D_MODEL), F32)],
        args=[ph, ph, ph, ph, o_raw, states, dy, lb_logits, norm_g], semantics=("arbitrary",), comm=comm)


def _local_step(x, target, vec, net):
    T, D = x.shape
    norm_mix_g, b_in, sinks, lb_logits = vec["norm_mix_g"], vec["b_in"], vec["attn_sinks"], vec["hgrn_lb_logits"]
    hgrn_norm_g, norm_ffn_g, norm_final_g = vec["hgrn_norm_g"], vec["norm_ffn_g"], vec["norm_final_g"]
    w_in = net.full("w_in")
    o_q, o_kv, o_h, o_g = (sum(IN_SPLITS[:i]) for i in range(4))
    w_q, w_kv, w_h, w_g = (_Rows(w_in, o, n) for o, n in zip((o_q, o_kv, o_h, o_g), IN_SPLITS))
    both = lambda acc: (acc, acc)
    grad_outs = [("mn", F32), ("mn", MXU_DTYPE)]
    row_vec = lambda n: ((1, n), lambda i, j: (0, j))
    tile = lambda tm, tn: ((tm, tn), lambda i, j: (i, j))
    TM = 512
    BIG = min(T, 1024)

    names = ("w_branch_attn", "w_branch_hgrn", "w_out")
    (u, pq, pkv, ph, pg), got = _in_proj(x, norm_mix_g, w_in, b_in, tm=256, comm=net.gather(names))
    net.gathered(names, got)
    names = ("w_ffn_gate",)
    (y_attn, lse), got = _attn_fwd(pq, pkv, sinks, comm=net.gather(names))
    net.gathered(names, got)
    names = ("w_ffn_up",)
    (y_hgrn, o_raw, states), got = _hgrn_fwd(ph, lb_logits, hgrn_norm_g, comm=net.gather(names))
    net.gathered(names, got)
    w_ba, w_bh, w_out = net.full("w_branch_attn"), net.full("w_branch_hgrn"), net.full("w_out")
    w_gate, w_up = net.full("w_ffn_gate"), net.full("w_ffn_up")
    ya, yb, merged = _merge_fwd(y_attn, y_hgrn, w_ba, w_bh, pg, tm=TM)

    FT = FFN // 2
    names = ("w_ffn_down",)
    (h1, u2, gpre, up, z), got = _ffn_fwd(merged, w_out, x, norm_ffn_g, w_gate, w_up, tm=256,
                                          comm=net.gather(names))
    net.gathered(names, got)
    w_down = net.full("w_ffn_down")

    dh2, dh2b, dgp, dup, dgf_p, loss_p = _ffn_tail(z, w_down, h1, target, norm_final_g, gpre, up, tm=256)
    loss = jnp.sum(loss_p.reshape(-1, 8, D)[:, 0, 0])
    d_norm_final = _colsum_partials(dgf_p)
    d_w_down = _matmul("dw_down", [(z, dh2b)], "tn", tm=FT, tn=1024, tk=1024, outs=grad_outs, epilogue=both)

    names = ("w_ffn_down",)
    (dh1, dh1b, dg2_p), got = _ffn_in_bwd(dgp, dup, w_gate, w_up, h1, norm_ffn_g, dh2, tm=256,
        comm=net.exchange(dict(w_ffn_down=[d_w_down])))
    net.received(names, (), got)
    d_norm_ffn = _colsum_partials(dg2_p)
    d_w_gate = _matmul("dw_gate", [(dgp, u2)], "tn", tm=FT, tn=1024, tk=1024, outs=grad_outs, epilogue=both)
    d_w_up = _matmul("dw_up", [(dup, u2)], "tn", tm=FT, tn=1024, tk=1024, outs=grad_outs, epilogue=both)

    dya, dyb, dga, dgb, dy_attn, dy_hgrn = _merge_bwd(dh1b, w_out, w_ba, w_bh, ya, yb, pg, tm=TM)
    tn_grad = functools.partial(_matmul, mode="tn", tn=1024, tk=1024, outs=grad_outs, epilogue=both)
    d_w_out = tn_grad("dw_out", [(merged, dh1b)], tm=1024)
    d_w_ba = tn_grad("dw_branch_a", [(y_attn, dya)], tm=1024)
    d_w_bh = tn_grad("dw_branch_b", [(y_hgrn, dyb)], tm=1024)

    names, swap = ("w_ffn_gate",), ("w_ffn_down",)
    (dq, dkv, dsink), got = _attn_bwd(pq, pkv, sinks, lse, dy_attn,
                                      comm=net.exchange(dict(w_ffn_gate=[d_w_gate]), swap))
    net.received(names, swap, got)
    names, swap = ("w_ffn_up", "w_out", "w_branch_attn", "w_branch_hgrn"), ("w_ffn_gate",)
    (dph, d_hgrn_norm, d_lb_logits), got = _hgrn_bwd(
        ph, o_raw, states, dy_hgrn, lb_logits, hgrn_norm_g,
        comm=net.exchange(dict(w_ffn_up=[d_w_up], w_out=[d_w_out], w_branch_attn=[d_w_ba],
                               w_branch_hgrn=[d_w_bh]), swap))
    net.received(names, swap, got)

    d_w_in, db, rows_in = None, {}, sum(IN_SPLITS)
    for piece, dp, first, tm_p in (("q", dq, o_q, 1024), ("kv", dkv, o_kv, 256), ("h", dph, o_h, 1024),
                                   ("ga", dga, o_g, 1024), ("gb", dgb, o_g + D, 1024)):
        *d_w_in, db[piece] = tn_grad("dw_in_" + piece, [(dp, u)], tm=tm_p, a_colsum=True,
                                     into=_Into(rows_in, first, d_w_in))
    d_w_in = [tuple(d_w_in)]

    first_level = net.presum_begin("w_in", d_w_in)
    halves = net.presum_end("w_in", [] if first_level is None else _copies_alone("presum_swap_w_in", first_level))
    names, swap = ("w_in",), ("w_ffn_up", "w_out", "w_branch_attn", "w_branch_hgrn")
    pieces = [(dq, o_q), (dkv, o_kv), (dph, o_h), (dga, o_g), (dgb, o_g + D)]
    (dx, dg1_p), got = _in_proj_bwd(pieces, w_in, x, norm_mix_g, dh1, tm=256, comm=_join(halves, net.swap(swap)))
    net.last = (names, swap, got)
    d_norm_mix = _colsum_partials(dg1_p)
    d_b_in = jnp.concatenate([db[piece] for piece in ("q", "kv", "h", "ga", "gb")], axis=1)
    vecs = dict(norm_mix_g=d_norm_mix, b_in=d_b_in, attn_sinks=jnp.sum(dsink.reshape(Q_HEADS, ATTN_BLOCK), axis=1).reshape(1, Q_HEADS),
                hgrn_lb_logits=d_lb_logits,
                hgrn_norm_g=d_hgrn_norm, norm_ffn_g=d_norm_ffn, norm_final_g=d_norm_final)
    return loss, dx, vecs


def _place():
    return lax.axis_index("x"), lax.axis_index("y"), lax.axis_index("c")


def _other_chips(x, y):
    return [(1 - x, y), (x, 1 - y), (1 - x, 1 - y)]


def _y_first(copies):
    return [copies[3 * (i // 3) + (1, 0, 2)[i % 3]] for i in range(len(copies))]


def _gather_copies(shards):
    n = len(shards)

    def build(ins, outs, send_sems, recv_sems, local_sems):
        x, y, c = _place()
        mine = 2 * x + y
        local = [pltpu.make_async_copy(ins[w], outs[w].at[mine], local_sems.at[w]) for w in range(n)]
        sends, recvs = [], []
        for w in range(n):
            for k, (px, py) in enumerate(_other_chips(x, y)):
                sem = 3 * w + k
                sends.append(pltpu.make_async_remote_copy(
                    src_ref=ins[w], dst_ref=outs[w].at[mine], send_sem=send_sems.at[sem], recv_sem=recv_sems.at[sem],
                    device_id=(px, py, c), device_id_type=MESH_ID))
                recvs.append(pltpu.make_async_remote_copy(
                    src_ref=ins[w], dst_ref=outs[w].at[2 * px + py], send_sem=send_sems.at[sem],
                    recv_sem=recv_sems.at[sem], device_id=(px, py, c), device_id_type=MESH_ID))
        return sends, recvs, local, _y_first(sends)

    return _Carried(shards, [jax.ShapeDtypeStruct((N_CHIPS,) + s.shape, s.dtype) for s in shards], 3 * n, n, build)


def _grad_copies(stacked):
    n = len(stacked)

    def build(ins, outs, send_sems, recv_sems, local_sems):
        x, y, c = _place()
        sends = []
        for w in range(n):
            for k, (px, py) in enumerate(_other_chips(x, y)):
                sem = 3 * w + k
                sends.append(pltpu.make_async_remote_copy(
                    src_ref=ins[w].at[2 * px + py], dst_ref=outs[w].at[k], send_sem=send_sems.at[sem],
                    recv_sem=recv_sems.at[sem], device_id=(px, py, c), device_id_type=MESH_ID))
        return sends, sends, [], _y_first(sends)

    return _Carried(stacked, [jax.ShapeDtypeStruct((3,) + s.shape[1:], s.dtype) for s in stacked], 3 * n, 0, build)


def _small_copies(small):
    def build(ins, outs, send_sems, recv_sems, local_sems):
        small_ref, all_ref = ins[0], outs[0]
        x, y, c = _place()
        me = 4 * x + 2 * y + c
        sends, recvs = [], []
        for r in range(1, 8):
            px = 1 - x if r & 4 else x
            py = 1 - y if r & 2 else y
            pc = 1 - c if r & 1 else c
            sends.append(pltpu.make_async_remote_copy(
                src_ref=small_ref, dst_ref=all_ref.at[me], send_sem=send_sems.at[r - 1], recv_sem=recv_sems.at[r - 1],
                device_id=(px, py, pc), device_id_type=MESH_ID))
            recvs.append(pltpu.make_async_remote_copy(
                src_ref=small_ref, dst_ref=all_ref.at[4 * px + 2 * py + pc], send_sem=send_sems.at[r - 1],
                recv_sem=recv_sems.at[r - 1], device_id=(px, py, pc), device_id_type=MESH_ID))
        return sends, recvs, [pltpu.make_async_copy(small_ref, all_ref.at[me], local_sems.at[0])]

    return _Carried([small], [jax.ShapeDtypeStruct((8,) + small.shape, small.dtype)], 7, 1, build)


def _gather_by_neighbours(name, shard):
    half = shard.shape[0] // 2
    quarter = half // 2

    def body(in_ref, out_ref, send_sems, recv_sems, local_sem):
        for core in (0, 1):
            @pl.when(lax.axis_index("c") == core)
            def _():
                program(core, in_ref, out_ref, send_sems, recv_sems, local_sem)

    def program(c, in_ref, out_ref, send_sems, recv_sems, local_sem):
        x, y, _ = _place()
        chip = lambda px, py: 2 * px + py
        to_x, to_y, sibling = (1 - x, y, c), (x, 1 - y, c), (x, y, 1 - c)
        x_blk, y_blk, d_blk = chip(1 - x, y), chip(x, 1 - y), chip(1 - x, 1 - y)
        mine, theirs = c * half, (1 - c) * half

        def copy(sem, rows, block, to, src=None):
            place = out_ref.at[block, pl.ds(rows[0], rows[1])]
            return pltpu.make_async_remote_copy(
                src_ref=place if src is None else src, dst_ref=place, send_sem=send_sems.at[sem],
                recv_sem=recv_sems.at[sem], device_id=to, device_id_type=MESH_ID)

        own = pltpu.make_async_copy(in_ref, out_ref.at[chip(x, y)], local_sem)
        own.start()
        my_rows = in_ref.at[pl.ds(mine, half)]
        along_x = dict(send=copy(0, (mine, half), chip(x, y), to_x, src=my_rows),
                       landed=copy(0, (mine, half), x_blk, to_x),
                       onward=[copy(3, (mine + quarter, quarter), x_blk, to_y), copy(4, (mine, half), x_blk, sibling)],
                       diagonal=copy(2, (mine, quarter), d_blk, to_x))
        along_y = dict(send=copy(1, (mine, half), chip(x, y), to_y, src=my_rows),
                       landed=copy(1, (mine, half), y_blk, to_y),
                       onward=[copy(2, (mine, quarter), y_blk, to_x), copy(5, (mine, half), y_blk, sibling)],
                       diagonal=copy(3, (mine + quarter, quarter), d_blk, to_y))
        last = copy(6, (mine, half), d_blk, sibling)

        order = (along_x, along_y) if c == 0 else (along_y, along_x)
        for axis in order:
            axis["send"].start()
        for axis in order:
            axis["landed"].wait_recv()
            for cp in axis["onward"]:
                cp.start()
        for axis in order:
            axis["diagonal"].wait_recv()
        last.start()
        for sem, block in ((4, x_blk), (5, y_blk), (6, d_blk)):
            copy(sem, (theirs, half), block, sibling).wait_recv()
        for cp in [along_x["send"], along_y["send"]] + along_x["onward"] + along_y["onward"] + [last]:
            cp.wait_send()
        own.wait()

    return pl.pallas_call(
        body, name=name, in_specs=[HBM_SPEC], out_specs=HBM_SPEC,
        out_shape=jax.ShapeDtypeStruct((N_CHIPS,) + shard.shape, shard.dtype),
        scratch_shapes=[pltpu.SemaphoreType.DMA((7,)), pltpu.SemaphoreType.DMA((7,)), pltpu.SemaphoreType.DMA(())],
    )(shard)


def _copies_alone(name, comm):
    return _call(name, lambda: None, grid=(), in_specs=[], out_specs=[], out_shape=[], args=[], comm=comm)[1]


class _Net:
    def __init__(self, shards):
        self.shards = shards
        self.whole, self.own, self.theirs, self.sums, self.other = {}, {}, {}, {}, {}
        x, y, _ = _place()
        self.chip = 2 * x + y

    def gather(self, names):
        return _gather_copies([self.shards[n] for n in names])

    def gathered(self, names, got):
        for n, g in zip(names, got):
            self.whole[n] = g.reshape(-1, g.shape[-1])

    def full(self, name):
        return self.whole[name]

    def exchange(self, grads, swap=()):
        stacked = []
        for n, pieces in grads.items():
            (keep, send), = pieces
            self.own[n] = keep
            stacked.append(send.reshape(N_CHIPS, keep.shape[0] // N_CHIPS, send.shape[-1]))
        return _join(_grad_copies(stacked), self.swap(swap))

    def swap(self, names):
        return _sibling_copies([self.sums[n] for n in names]) if names else None

    def presum_begin(self, name, pieces):
        keep = jnp.concatenate([p[0] for p in pieces], axis=0) if len(pieces) > 1 else pieces[0][0]
        send = jnp.concatenate([p[1] for p in pieces], axis=0) if len(pieces) > 1 else pieces[0][1]
        rows = keep.shape[0] // N_CHIPS
        self.held = keep.reshape(N_CHIPS, rows, keep.shape[-1])
        return _half_rows_copies(send.reshape(N_CHIPS, rows, send.shape[-1]))

    def presum_end(self, name, got):
        x, y, c = _place()
        to_send, self.own[name] = _pre_sum("presum_" + name, self.held, got[0], jnp.stack([c, self.chip]))
        return _grad_copies([to_send])

    def received(self, names, swap, got, carried=None):
        self.theirs.update(zip(names, got[:len(names)]))
        self.other.update(zip(swap, got[len(names):]))
        for n in names:
            (self.sums[n],), more = _partial_sum("sum_" + n, self.own[n], self.theirs[n], self.chip, comm=carried)
        return more


def _half_rows_copies(stacked):
    n, rows = stacked.shape[0], stacked.shape[1] // 2

    def build(ins, outs, send_sems, recv_sems, local_sems):
        x, y, c = _place()
        copies = [pltpu.make_async_remote_copy(
            src_ref=ins[0].at[s, pl.ds((1 - c) * rows, rows)], dst_ref=outs[0].at[s], send_sem=send_sems.at[s],
            recv_sem=recv_sems.at[s], device_id=(x, y, 1 - c), device_id_type=MESH_ID) for s in range(n)]
        return copies, copies, []

    return _Carried([stacked], [jax.ShapeDtypeStruct((n, rows, stacked.shape[2]), stacked.dtype)], n, 0, build)


def _pre_sum(name, held, theirs, core_and_chip):
    n, R, C = held.shape
    half = R // 2
    tr = _row_tile(half)
    per_half = half // tr

    def body(place_ref, h_ref, t_ref, send_ref, own_ref):
        total = h_ref[0] + t_ref[0].astype(F32)
        send_ref[0] = total.astype(send_ref.dtype)

        @pl.when(pl.program_id(1) == place_ref[1])
        def _():
            own_ref[...] = total

    return pl.pallas_call(
        body, name=name,
        grid_spec=pltpu.PrefetchScalarGridSpec(
            num_scalar_prefetch=1, grid=(per_half, n),
            in_specs=[pl.BlockSpec((1, tr, C), lambda i, s, place: (s, place[0] * per_half + i, 0)),
                      pl.BlockSpec((1, tr, C), lambda i, s, place: (s, i, 0))],
            out_specs=[pl.BlockSpec((1, tr, C), lambda i, s, place: (s, i, 0)),
                       pl.BlockSpec((tr, C), lambda i, s, place: (i, 0))]),
        out_shape=[jax.ShapeDtypeStruct((n, half, C), MXU_DTYPE), jax.ShapeDtypeStruct((half, C), F32)],
        compiler_params=_params(("arbitrary", "arbitrary")),
    )(core_and_chip, held, theirs)


def _sibling_copies(parts):
    n = len(parts)

    def build(ins, outs, send_sems, recv_sems, local_sems):
        x, y, c = _place()
        copies = [pltpu.make_async_remote_copy(
            src_ref=ins[w], dst_ref=outs[w], send_sem=send_sems.at[w], recv_sem=recv_sems.at[w],
            device_id=(x, y, 1 - c), device_id_type=MESH_ID) for w in range(n)]
        return copies, copies, []

    return _Carried(parts, [jax.ShapeDtypeStruct(p.shape, p.dtype) for p in parts], n, 0, build)


def _row_tile(rows, most=512, sublanes=16):
    return max(t for t in range(sublanes, most + 1, sublanes) if rows % t == 0)


def _partial_sum(name, own, recv, chip, comm=None):
    _, R, C = recv.shape
    tr = _row_tile(R, most=128)

    def body(o_ref, r_ref, p_ref):
        p_ref[...] = ((o_ref[...] + r_ref[0].astype(F32)) + r_ref[1].astype(F32)) + r_ref[2].astype(F32)

    if own.shape[0] != R:
        assert comm is None and own.shape[0] == N_CHIPS * R
        total = pl.pallas_call(
            lambda chip_ref, *refs: body(*refs), name=name,
            grid_spec=pltpu.PrefetchScalarGridSpec(
                num_scalar_prefetch=1, grid=(R // tr,),
                in_specs=[pl.BlockSpec((tr, C), lambda i, chip_ref: (chip_ref[0] * (R // tr) + i, 0)),
                          pl.BlockSpec((3, tr, C), lambda i, chip_ref: (0, i, 0))],
                out_specs=pl.BlockSpec((tr, C), lambda i, chip_ref: (i, 0))),
            out_shape=jax.ShapeDtypeStruct((R, C), F32), compiler_params=_params(("parallel",)),
        )(chip.reshape(1), own, recv)
        return [total], []
    return _call(name, body, grid=(R // tr,),
                 in_specs=[pl.BlockSpec((tr, C), lambda i: (i, 0)), pl.BlockSpec((3, tr, C), lambda i: (0, i, 0))],
                 out_specs=[pl.BlockSpec((tr, C), lambda i: (i, 0))], out_shape=[jax.ShapeDtypeStruct((R, C), F32)],
                 args=[own, recv], semantics=("parallel",), comm=comm)


def _adam_vals(w, g, m, v):
    m = ADAM_B1 * m + (1.0 - ADAM_B1) * g
    v = ADAM_B2 * v + (1.0 - ADAM_B2) * (g * g)
    m_hat = m / (1.0 - ADAM_B1 ** ADAM_STEP)
    v_hat = v / (1.0 - ADAM_B2 ** ADAM_STEP)
    delta = -ADAM_LR * (m_hat / (jnp.sqrt(v_hat) + ADAM_EPS) + ADAM_WD * w)
    return delta, m, v


def _adamw(name, w, m, v, mine, other, comm=None):
    R, C = w.shape
    tr = _row_tile(R)

    def body(w_ref, m_ref, v_ref, s_ref, n_ref, g_ref, d_ref, nm_ref, nv_ref):
        g = s_ref[...] + n_ref[...]
        d, nm, nv = _adam_vals(w_ref[...], g, m_ref[...], v_ref[...])
        g_ref[...], d_ref[...], nm_ref[...], nv_ref[...] = g, d, nm, nv

    spec = pl.BlockSpec((tr, C), lambda i: (i, 0))
    return _call(name, body, grid=(R // tr,), in_specs=[spec] * 5, out_specs=[spec] * 4,
                 out_shape=[jax.ShapeDtypeStruct((R, C), F32)] * 4, args=[w, m, v, mine, other],
                 semantics=("parallel",), comm=comm)


def _adamw_by_halves(name, w, m, v, mine, other, core):
    R, C = w.shape
    tr = _row_tile(R // 2)
    per_half = R // 2 // tr

    def body(c_ref, w_ref, m_ref, v_ref, s_ref, n_ref, g_ref, d_ref, nm_ref, nv_ref):
        g = jnp.where(pl.program_id(0) // per_half == c_ref[0, 0], s_ref[...], n_ref[...])
        d, nm, nv = _adam_vals(w_ref[...], g, m_ref[...], v_ref[...])
        g_ref[...], d_ref[...], nm_ref[...], nv_ref[...] = g, d, nm, nv

    spec = pl.BlockSpec((tr, C), lambda i: (i, 0))
    part = pl.BlockSpec((tr, C), lambda i: (i % per_half, 0))
    return pl.pallas_call(
        body, name=name, grid=(R // tr,),
        in_specs=[pl.BlockSpec(memory_space=pltpu.SMEM), spec, spec, spec, part, part], out_specs=[spec] * 4,
        out_shape=[jax.ShapeDtypeStruct((R, C), F32)] * 4, compiler_params=_params(("parallel",)),
    )(core, w, m, v, mine, other)


SMALL_LAYOUT = dict(norm_mix_g=(0, 1, 1024), b_in=(1, 8, 7424), hgrn_norm_g=(9, 1, 1024), norm_ffn_g=(10, 1, 1024),
                    norm_final_g=(11, 1, 1024), hgrn_lb_logits=(12, 2, 2048), attn_sinks=(14, 1, 16))
SMALL_LOSS_ROW, SMALL_ROWS = 15, 16


def _pack_small(grads, loss):
    rows = [jnp.pad(grads[name].astype(F32).reshape(-1), (0, nrows * D_MODEL - n))
            for name, (_, nrows, n) in SMALL_LAYOUT.items()]
    rows.append(jnp.pad(loss.astype(F32).reshape(1), (0, D_MODEL - 1)))
    return jnp.concatenate(rows).reshape(SMALL_ROWS, D_MODEL)


def _adamw_small(w, m, v, g_all):
    names = list(SMALL_LAYOUT)
    n = len(names)

    def body(a_ref, *refs):
        ins, outs = refs[:3 * n], refs[3 * n:]
        g_all_rows = a_ref[0]
        for dev in range(1, 8):
            g_all_rows = g_all_rows + a_ref[dev]
        for i, name in enumerate(names):
            first, nrows, count = SMALL_LAYOUT[name]
            w_ref, m_ref, v_ref = ins[3 * i:3 * i + 3]
            if w_ref.shape[0] == nrows:
                g = g_all_rows[first:first + nrows, :w_ref.shape[1]]
            else:
                last = count - (nrows - 1) * D_MODEL
                g = jnp.concatenate([g_all_rows[r:r + 1, :] for r in range(first, first + nrows - 1)]
                                    + [g_all_rows[first + nrows - 1:first + nrows, :last]], axis=1)
            d, nm, nv = _adam_vals(w_ref[...], g, m_ref[...], v_ref[...])
            for o_ref, val in zip(outs[4 * i:4 * i + 4], (g, d, nm, nv)):
                o_ref[...] = val
        outs[4 * n][...] = g_all_rows[SMALL_LOSS_ROW:SMALL_LOSS_ROW + 1, 0:1]

    res = pl.pallas_call(
        body, name="adamw_small",
        out_shape=[jax.ShapeDtypeStruct(w[name].shape, F32) for name in names for _ in range(4)]
        + [jax.ShapeDtypeStruct((1, 1), F32)],
    )(g_all, *[t[name] for name in names for t in (w, m, v)])
    return {name: res[4 * i:4 * i + 4] for i, name in enumerate(names)}, res[4 * n]


MATRICES = ("w_in", "w_branch_attn", "w_branch_hgrn", "w_out", "w_ffn_gate", "w_ffn_up", "w_ffn_down")
COLUMN_SHARDED = ("w_in", "w_ffn_gate", "w_ffn_up")
WEIGHTS = ("norm_mix_g", "w_in", "b_in", "attn_sinks", "hgrn_lb_logits", "hgrn_norm_g", "w_branch_attn",
           "w_branch_hgrn", "w_out", "norm_ffn_g", "w_ffn_gate", "w_ffn_up", "w_ffn_down", "norm_final_g")


def kernel(x, norm_mix_g, w_in, b_in, attn_sinks, hgrn_lb_logits, hgrn_norm_g, w_branch_attn, w_branch_hgrn, w_out, norm_ffn_g, w_ffn_gate, w_ffn_up, w_ffn_down, norm_final_g, loss_target, m_norm_mix_g, m_w_in, m_b_in, m_attn_sinks, m_hgrn_lb_logits, m_hgrn_norm_g, m_w_branch_attn, m_w_branch_hgrn, m_w_out, m_norm_ffn_g, m_w_ffn_gate, m_w_ffn_up, m_w_ffn_down, m_norm_final_g, v_norm_mix_g, v_w_in, v_b_in, v_attn_sinks, v_hgrn_lb_logits, v_hgrn_norm_g, v_w_branch_attn, v_w_branch_hgrn, v_w_out, v_norm_ffn_g, v_w_ffn_gate, v_w_ffn_up, v_w_ffn_down, v_norm_final_g):
    given = dict(locals())
    w = {n: given[n] for n in WEIGHTS}
    m = {n: given["m_" + n] for n in WEIGHTS}
    v = {n: given["v_" + n] for n in WEIGHTS}

    block = lambda a, n: jnp.transpose(a[0]) if n in COLUMN_SHARDED else a[0]
    unblock = lambda a, n: (jnp.transpose(a) if n in COLUMN_SHARDED else a)[None]
    net = _Net({n: block(w[n], n).astype(MXU_DTYPE) for n in MATRICES})
    net.gathered(("w_in",), [_gather_by_neighbours("gather_w_in", net.shards["w_in"])])
    vec = dict(norm_mix_g=norm_mix_g, b_in=b_in, attn_sinks=attn_sinks, hgrn_lb_logits=hgrn_lb_logits,
               hgrn_norm_g=hgrn_norm_g, norm_ffn_g=norm_ffn_g, norm_final_g=norm_final_g.reshape(1, D_MODEL))
    loss_part, dx, d_vecs = _local_step(x[0], loss_target[0], vec, net)

    small_all, = net.received(*net.last, carried=_small_copies(_pack_small(d_vecs, loss_part)))
    grads, deltas, new_m, new_v = {}, {}, {}, {}
    for n in ("w_ffn_down", "w_ffn_gate", "w_ffn_up", "w_out", "w_branch_attn", "w_branch_hgrn"):
        res, got = _adamw("adamw_" + n, block(w[n], n), block(m[n], n), block(v[n], n), net.sums[n], net.other[n],
                          comm=net.swap(("w_in",)) if n == "w_ffn_down" else None)
        if n == "w_ffn_down":
            net.other["w_in"], = got
        grads[n], deltas[n], new_m[n], new_v[n] = (unblock(r, n) for r in res)
    n = "w_in"
    res = _adamw_by_halves("adamw_" + n, block(w[n], n), block(m[n], n), block(v[n], n), net.sums[n], net.other[n],
                           _place()[2].reshape(1, 1))
    grads[n], deltas[n], new_m[n], new_v[n] = (unblock(r, n) for r in res)
    rows = lambda t: {n: t[n].reshape(-1, t[n].shape[-1]) for n in SMALL_LAYOUT}
    res, loss = _adamw_small(rows(w), rows(m), rows(v), small_all)
    for n, four in res.items():
        grads[n], deltas[n], new_m[n], new_v[n] = (r.reshape(w[n].shape) for r in four)
    loss = loss.reshape(())
    return (loss, dx[None], *[grads[n] for n in WEIGHTS], *[deltas[n] for n in WEIGHTS],
            *[new_m[n] for n in WEIGHTS], *[new_v[n] for n in WEIGHTS])
```

```python
import collections
import functools
import math

import jax
import jax.numpy as jnp
from jax import lax
from jax.experimental import pallas as pl
from jax.experimental.pallas import tpu as pltpu

F32 = jnp.float32
BF16 = jnp.bfloat16
MXU_DTYPE = jnp.bfloat16
SAVED_DTYPE = jnp.bfloat16
MESH_ID = pl.DeviceIdType.MESH

D_MODEL = 1024
HEAD_DIM = 64
Q_HEADS = 16
KV_HEADS = 2
GROUP = Q_HEADS // KV_HEADS
KV_WIDTH = KV_HEADS * HEAD_DIM
ATTN_BLOCK = 128
HGRN_HEADS = 8
HGRN_K = 128
CHUNK = 64
HGRN_TOKENS = 256
FFN = 2816
IN_SPLITS = (1024, 256, 4096, 2048)
EPS = 1e-6
NEG_INF = -1e30
ADAM_LR, ADAM_B1, ADAM_B2, ADAM_EPS, ADAM_WD, ADAM_STEP = 0.001, 0.9, 0.999, 1e-08, 0.01, 10
N_CHIPS = 4
VMEM_LIMIT = 60 * 1024 * 1024
ROW_ALIGN = 16


def _params(sem=None):
    return pltpu.CompilerParams(dimension_semantics=sem, vmem_limit_bytes=VMEM_LIMIT)


def _sigmoid(v):
    return 0.5 * jnp.tanh(0.5 * v) + 0.5


def _dot(a, b, dims):
    return lax.dot_general(a.astype(MXU_DTYPE), b.astype(MXU_DTYPE), (dims, ((), ())),
                           preferred_element_type=F32)


def _nn(a, b):
    return _dot(a, b, ((1,), (0,)))


def _nt(a, b):
    return _dot(a, b, ((1,), (1,)))


def _tn(a, b):
    return _dot(a, b, ((0,), (0,)))


HBM_SPEC = pl.BlockSpec(memory_space=pl.ANY)


class _Carried:
    def __init__(self, arrays, out_shapes, n_remote, n_local, build):
        self.parts = [(len(arrays), len(out_shapes), build)]
        self.arrays, self.out_shapes = list(arrays), list(out_shapes)
        self.scratch = [pltpu.SemaphoreType.DMA((n_remote,)), pltpu.SemaphoreType.DMA((n_remote,)),
                        pltpu.SemaphoreType.DMA((max(n_local, 1),))]

    def __add__(self, other):
        both = _Carried([], [], 1, 0, None)
        both.parts = self.parts + other.parts
        both.arrays, both.out_shapes = self.arrays + other.arrays, self.out_shapes + other.out_shapes
        both.scratch = self.scratch + other.scratch
        return both

    def _built(self, ins, outs, sems):
        for p, (ni, no, build) in enumerate(self.parts):
            yield build(ins[:ni], outs[:no], *sems[3 * p:3 * p + 3])
            ins, outs = ins[ni:], outs[no:]

    def start(self, ins, outs, sems):
        core = lax.axis_index("c")
        for sends, _, local, *other_order in self._built(ins, outs, sems):
            for cp in local:
                cp.start()
            if not other_order:
                for cp in sends:
                    cp.start()
                continue

            @pl.when(core == 0)
            def _():
                for cp in sends:
                    cp.start()

            @pl.when(core == 1)
            def _():
                for cp in other_order[0]:
                    cp.start()

    def wait(self, ins, outs, sems):
        for sends, recvs, local, *_ in self._built(ins, outs, sems):
            for cp in recvs:
                cp.wait_recv()
            for cp in sends:
                cp.wait_send()
            for cp in local:
                cp.wait()


def _join(*comms):
    comms = [c for c in comms if c is not None]
    return functools.reduce(lambda a, b: a + b, comms) if comms else None


def _call(name, body, *, grid, in_specs, out_specs, out_shape, args, scratch=(), semantics=None, comm=None,
          aliases=None):
    n_in, n_out, n_scr = len(in_specs), len(out_specs), len(scratch)
    aliases = aliases or {}
    if comm is None:
        res = pl.pallas_call(body, name=name, grid=grid, in_specs=in_specs, out_specs=out_specs, out_shape=out_shape,
                             scratch_shapes=list(scratch), input_output_aliases=aliases,
                             compiler_params=_params(semantics))(*args)
        return list(res), []
    ci, co = len(comm.arrays), len(comm.out_shapes)

    def carrying(*refs):
        ins, refs = refs[:n_in], refs[n_in:]
        c_ins, refs = refs[:ci], refs[ci:]
        outs, refs = refs[:n_out], refs[n_out:]
        c_outs, refs = refs[:co], refs[co:]
        scr, sems = refs[:n_scr], refs[n_scr:]
        if not grid:
            comm.start(c_ins, c_outs, sems)
            body(*ins, *outs, *scr)
            comm.wait(c_ins, c_outs, sems)
            return
        first = functools.reduce(jnp.logical_and, [pl.program_id(a) == 0 for a in range(len(grid))])
        last = functools.reduce(jnp.logical_and, [pl.program_id(a) == g - 1 for a, g in enumerate(grid)])

        @pl.when(first)
        def _():
            comm.start(c_ins, c_outs, sems)

        body(*ins, *outs, *scr)

        @pl.when(last)
        def _():
            comm.wait(c_ins, c_outs, sems)

    res = pl.pallas_call(
        carrying, name=name, grid=grid, in_specs=list(in_specs) + [HBM_SPEC] * ci,
        out_specs=list(out_specs) + [HBM_SPEC] * co, out_shape=list(out_shape) + comm.out_shapes,
        scratch_shapes=list(scratch) + comm.scratch, input_output_aliases=aliases,
        compiler_params=_params(("arbitrary",) * len(grid) if grid else None),
    )(*args, *comm.arrays)
    return list(res[:n_out]), list(res[n_out:])


_NO_COPIES = object()
_Rows = collections.namedtuple("_Rows", "array first rows")


_Into = collections.namedtuple("_Into", "rows first held")


def _matmul(name, pairs, mode, *, tm, tn, tk, outs, extras=(), epilogue=None, a_colsum=False, into=None,
            comm=_NO_COPIES):
    prod = dict(nn=_nn, nt=_nt, tn=_tn)[mode]
    pairs = [(a, b if isinstance(b, _Rows) else _Rows(b, 0, b.shape[0])) for a, b in pairs]
    a0, b0 = pairs[0]
    M = a0.shape[1] if mode == "tn" else a0.shape[0]
    N = b0.rows if mode == "nt" else b0.array.shape[1]
    steps, in_specs, offset = [], [], 0
    for a, b in pairs:
        K = a.shape[0] if mode == "tn" else a.shape[1]
        t = min(tk, K)
        assert K % t == 0, (name, K, t)
        kmap = functools.partial(lambda k, off, n: jnp.clip(k - off, 0, n - 1), off=offset, n=K // t)
        if mode == "tn":
            in_specs.append(pl.BlockSpec((t, tm), functools.partial(lambda i, j, k, f: (f(k), i), f=kmap)))
        else:
            in_specs.append(pl.BlockSpec((tm, t), functools.partial(lambda i, j, k, f: (i, f(k)), f=kmap)))
        whole = b.first == 0 and b.rows == b.array.shape[0]
        if mode == "nt":
            shape = (tn, t) if whole else (pl.Element(tn), pl.Element(t))
            in_specs.append(pl.BlockSpec(shape, functools.partial(
                lambda i, j, k, f, b, t, whole: (j, f(k)) if whole else (
                    pl.multiple_of(b.first + j * tn, ROW_ALIGN), pl.multiple_of(f(k) * t, 128)),
                f=kmap, b=b, t=t, whole=whole)))
        else:
            assert b.rows == K, (name, b.rows, K)
            shape = (t, tn) if whole else (pl.Element(t), pl.Element(tn))
            in_specs.append(pl.BlockSpec(shape, functools.partial(
                lambda i, j, k, f, b, t, whole: (f(k), j) if whole else (
                    pl.multiple_of(b.first + f(k) * t, ROW_ALIGN), pl.multiple_of(j * tn, 128)),
                f=kmap, b=b, t=t, whole=whole)))
        steps.append((offset, offset + K // t))
        offset += K // t
    assert M % tm == 0 and N % tn == 0, (name, M, N, tm, tn)
    ni, nj, nk = M // tm, N // tn, offset
    npair, ne, no = len(pairs), len(extras), len(outs)
    if epilogue is None:
        epilogue = lambda acc: (acc,)

    def finish(acc, extra_refs, out_refs):
        vals = epilogue(acc, *[r[...] for r in extra_refs])
        for (kind, _), o_ref, val in zip(outs, out_refs, vals):
            if kind == "pn":
                val = jnp.broadcast_to(val, o_ref.shape)
            o_ref[...] = val.astype(o_ref.dtype)

    held = list(into.held) if into is not None and into.held is not None else []

    def body(*refs):
        ab, rest = refs[:2 * npair], refs[2 * npair:]
        extra_refs, rest = rest[:ne], rest[ne + len(held):]
        out_refs = rest[:no]
        if nk == 1:
            finish(prod(ab[0][...], ab[1][...]), extra_refs, out_refs)
            if a_colsum:
                rest[no][...] = jnp.sum(ab[0][...].astype(F32), axis=0, keepdims=True)
            return
        sums_ref, acc_ref = rest[no], rest[-1]
        k = pl.program_id(2)

        @pl.when(k == 0)
        def _():
            acc_ref[...] = jnp.zeros_like(acc_ref)
            if a_colsum:
                sums_ref[...] = jnp.zeros((1, tm), F32)

        if a_colsum:
            sums_ref[...] += jnp.sum(ab[0][...].astype(F32), axis=0, keepdims=True)
        for p, (lo, hi) in enumerate(steps):
            @pl.when(jnp.logical_and(k >= lo, k < hi))
            def _():
                acc_ref[...] += prod(ab[2 * p][...], ab[2 * p + 1][...])

        @pl.when(k == nk - 1)
        def _():
            finish(acc_ref[...], extra_refs, out_refs)

    for _, shape, im in extras:
        in_specs.append(pl.BlockSpec(shape, functools.partial(lambda i, j, k, im: im(i, j), im=im)))
    in_specs += [HBM_SPEC] * len(held)
    aliases = {2 * npair + ne + p: p for p in range(len(held))}
    out_shape, out_specs = [], []
    for kind, dt in outs:
        if kind == "mn" and into is not None:
            out_shape.append(jax.ShapeDtypeStruct((into.rows, N), dt))
            out_specs.append(pl.BlockSpec((pl.Element(tm), pl.Element(tn)), lambda i, j, k: (
                pl.multiple_of(into.first + i * tm, ROW_ALIGN), pl.multiple_of(j * tn, 128))))
        elif kind == "mn":
            out_shape.append(jax.ShapeDtypeStruct((M, N), dt))
            out_specs.append(pl.BlockSpec((tm, tn), lambda i, j, k: (i, j)))
        else:
            out_shape.append(jax.ShapeDtypeStruct((8 * ni, N), dt))
            out_specs.append(pl.BlockSpec((8, tn), lambda i, j, k: (i, j)))
    if a_colsum:
        assert mode == "tn" and npair == 1 and nj == 1, name
        out_shape.append(jax.ShapeDtypeStruct((1, M), F32))
        out_specs.append(pl.BlockSpec((1, tm), lambda i, j, k: (0, i)))
    grid = (ni, nj, nk)
    if nj > 1 and nk == 1:
        turned = lambda spec: pl.BlockSpec(spec.block_shape, functools.partial(
            lambda j, i, k, im: im(i, j, k), im=spec.index_map))
        in_specs, out_specs, grid = [turned(s) for s in in_specs], [turned(s) for s in out_specs], (nj, ni, nk)
    res, got = _call(name, body, grid=grid, in_specs=in_specs, out_specs=out_specs, out_shape=out_shape,
                     args=[t for a, b in pairs for t in (a, b.array)] + [e[0] for e in extras] + held,
                     scratch=[pltpu.VMEM((tm, tn), F32)] if nk > 1 else [], aliases=aliases,
                     semantics=("parallel", "parallel", "arbitrary"), comm=None if comm is _NO_COPIES else comm)
    return res if comm is _NO_COPIES else (res, got)


def _ffn_fwd(merged, w_out, x, gain, w_gate_t, w_up_t, *, tm, comm=None):
    (T, D), F = x.shape, w_gate_t.shape[0]

    def body(m_ref, wo_ref, x_ref, g_ref, wg_ref, wu_ref, h_ref, u_ref, gate_ref, up_ref, z_ref):
        h = x_ref[...] + _nn(m_ref[...], wo_ref[...])
        h_ref[...] = h
        u = (h * lax.rsqrt(jnp.mean(h * h, axis=-1, keepdims=True) + EPS) * g_ref[...]).astype(u_ref.dtype)
        u_ref[...] = u
        gate, up = _nt(u, wg_ref[...]), _nt(u, wu_ref[...])
        gate_ref[...], up_ref[...] = gate.astype(gate_ref.dtype), up.astype(up_ref.dtype)
        z_ref[...] = (gate * _sigmoid(gate) * up).astype(z_ref.dtype)

    rows = lambda n: pl.BlockSpec((tm, n), lambda i: (i, 0))
    fixed = lambda a: pl.BlockSpec(a.shape, lambda i: (0, 0))
    return _call("ffn_hidden", body, grid=(T // tm,),
                 in_specs=[rows(D), fixed(w_out), rows(D), fixed(gain), fixed(w_gate_t), fixed(w_up_t)],
                 out_specs=[rows(D), rows(D), rows(F), rows(F), rows(F)],
                 out_shape=[jax.ShapeDtypeStruct((T, D), F32), jax.ShapeDtypeStruct((T, D), MXU_DTYPE)]
                 + [jax.ShapeDtypeStruct((T, F), SAVED_DTYPE)] * 2 + [jax.ShapeDtypeStruct((T, F), MXU_DTYPE)],
                 args=[merged, w_out, x, gain, w_gate_t, w_up_t], semantics=("parallel",), comm=comm)


def _in_proj(x, gain, w_in_t, b_in, *, tm, comm=None):
    T, D = x.shape
    bounds = [sum(IN_SPLITS[:i]) for i in range(len(IN_SPLITS) + 1)]

    def body(x_ref, g_ref, w_ref, b_ref, u_ref, *piece_refs):
        xv = x_ref[...]
        r = lax.rsqrt(jnp.mean(xv * xv, axis=-1, keepdims=True) + EPS)
        u = (xv * r * g_ref[...]).astype(u_ref.dtype)
        u_ref[...] = u
        for o_ref, lo, hi in zip(piece_refs, bounds[:-1], bounds[1:]):
            o_ref[...] = (_nt(u, w_ref[lo:hi, :]) + b_ref[:, lo:hi]).astype(o_ref.dtype)

    rows = lambda n: pl.BlockSpec((tm, n), lambda i: (i, 0))
    fixed = lambda a: pl.BlockSpec(a.shape, lambda i: (0, 0))
    dtypes = (MXU_DTYPE, MXU_DTYPE, F32, F32)
    return _call("in_proj", body, grid=(T // tm,),
                 in_specs=[rows(D), fixed(gain), fixed(w_in_t), fixed(b_in)],
                 out_specs=[rows(D)] + [rows(n) for n in IN_SPLITS],
                 out_shape=[jax.ShapeDtypeStruct((T, D), MXU_DTYPE)]
                 + [jax.ShapeDtypeStruct((T, n), dt) for n, dt in zip(IN_SPLITS, dtypes)],
                 args=[x, gain, w_in_t, b_in], semantics=("parallel",), comm=comm)


def _row_spec(tm, n):
    return pl.BlockSpec((tm, n), lambda i: (i, 0))


def _fixed_spec(a):
    return pl.BlockSpec(a.shape, lambda i: (0,) * a.ndim)


def _partials_spec(n):
    return pl.BlockSpec((8, n), lambda i: (i, 0))


def _ffn_tail(z, w_down, h1, target, gain, gate, up, *, tm):
    (T, F), D = z.shape, h1.shape[1]

    def body(z_ref, w_ref, h_ref, t_ref, g_ref, gate_ref, up_ref, dh_ref, dhb_ref, dgate_ref, dup_ref, dg_ref, l_ref):
        h2 = h_ref[...] + _nn(z_ref[...], w_ref[...])
        r = lax.rsqrt(jnp.mean(h2 * h2, axis=-1, keepdims=True) + EPS)
        xhat = h2 * r
        err = xhat * g_ref[...] - t_ref[...]
        part = 0.5 * jnp.sum(jnp.sum(err * err, axis=-1, keepdims=True), axis=0, keepdims=True) / D
        dy = err / D
        dxh = dy * g_ref[...]
        dh2 = r * (dxh - xhat * jnp.mean(dxh * xhat, axis=-1, keepdims=True))
        dh_ref[...] = dh2
        dhb = dh2.astype(dhb_ref.dtype)
        dhb_ref[...] = dhb
        dg_ref[...] = jnp.broadcast_to(jnp.sum(dy * xhat, axis=0, keepdims=True), dg_ref.shape)
        l_ref[...] = jnp.broadcast_to(part, l_ref.shape)
        dz = _nt(dhb, w_ref[...])
        gv, upv = gate_ref[...].astype(F32), up_ref[...].astype(F32)
        s = _sigmoid(gv)
        dgate_ref[...] = (dz * upv * (s * (1.0 + gv * (1.0 - s)))).astype(dgate_ref.dtype)
        dup_ref[...] = (dz * (gv * s)).astype(dup_ref.dtype)

    low = lambda n: jax.ShapeDtypeStruct((T, n), MXU_DTYPE)
    part = jax.ShapeDtypeStruct((8 * (T // tm), D), F32)
    return pl.pallas_call(
        body, name="ffn_tail", grid=(T // tm,),
        in_specs=[_row_spec(tm, F), _fixed_spec(w_down), _row_spec(tm, D), _row_spec(tm, D), _fixed_spec(gain),
                  _row_spec(tm, F), _row_spec(tm, F)],
        out_specs=[_row_spec(tm, D), _row_spec(tm, D), _row_spec(tm, F), _row_spec(tm, F), _partials_spec(D),
                   _partials_spec(D)],
        out_shape=[jax.ShapeDtypeStruct((T, D), F32), low(D), low(F), low(F), part, part],
        compiler_params=_params(("parallel",)),
    )(z, w_down, h1, target, gain, gate, up)


def _ffn_in_bwd(dgate, dup, w_gate_t, w_up_t, h1, gain, dres, *, tm, comm=None):
    (T, F), D = dgate.shape, h1.shape[1]

    def body(dg_ref, du_ref, wg_ref, wu_ref, h_ref, g_ref, r_ref, dh_ref, dhb_ref, dgain_ref):
        d_u2 = _nn(dg_ref[...], wg_ref[...]) + _nn(du_ref[...], wu_ref[...])
        dx, dgain = _rmsnorm_bwd_vals(d_u2, h_ref[...], g_ref[...])
        dh = r_ref[...] + dx
        dh_ref[...] = dh
        dhb_ref[...] = dh.astype(dhb_ref.dtype)
        dgain_ref[...] = jnp.broadcast_to(dgain, dgain_ref.shape)

    return _call("d_ffn_in", body, grid=(T // tm,),
                 in_specs=[_row_spec(tm, F), _row_spec(tm, F), _fixed_spec(w_gate_t), _fixed_spec(w_up_t),
                           _row_spec(tm, D), _fixed_spec(gain), _row_spec(tm, D)],
                 out_specs=[_row_spec(tm, D), _row_spec(tm, D), _partials_spec(D)],
                 out_shape=[jax.ShapeDtypeStruct((T, D), F32), jax.ShapeDtypeStruct((T, D), MXU_DTYPE),
                            jax.ShapeDtypeStruct((8 * (T // tm), D), F32)],
                 args=[dgate, dup, w_gate_t, w_up_t, h1, gain, dres], semantics=("parallel",), comm=comm)


def _in_proj_bwd(pieces, w_in_t, x, gain, dres, *, tm, comm=None):
    T, D = x.shape
    n = len(pieces)

    def body(*refs):
        dps, (w_ref, x_ref, g_ref, r_ref, dx_ref, dgain_ref) = refs[:n], refs[n:]
        d_u = None
        for dp_ref, (dp, first) in zip(dps, pieces):
            term = _nn(dp_ref[...], w_ref[first:first + dp.shape[1], :])
            d_u = term if d_u is None else d_u + term
        dx, dgain = _rmsnorm_bwd_vals(d_u, x_ref[...], g_ref[...])
        dx_ref[...] = r_ref[...] + dx
        dgain_ref[...] = jnp.broadcast_to(dgain, dgain_ref.shape)

    return _call("d_u", body, grid=(T // tm,),
                 in_specs=[_row_spec(tm, dp.shape[1]) for dp, _ in pieces]
                 + [_fixed_spec(w_in_t), _row_spec(tm, D), _fixed_spec(gain), _row_spec(tm, D)],
                 out_specs=[_row_spec(tm, D), _partials_spec(D)],
                 out_shape=[jax.ShapeDtypeStruct((T, D), F32), jax.ShapeDtypeStruct((8 * (T // tm), D), F32)],
                 args=[dp for dp, _ in pieces] + [w_in_t, x, gain, dres], semantics=("parallel",), comm=comm)


def _merge_fwd(y_a, y_b, w_a, w_b, gates, *, tm):
    T, D = y_a.shape

    def body(ya_ref, yb_ref, wa_ref, wb_ref, ga_ref, gb_ref, pa_ref, pb_ref, m_ref):
        pa, pb = _nn(ya_ref[...], wa_ref[...]), _nn(yb_ref[...], wb_ref[...])
        pa_ref[...], pb_ref[...] = pa.astype(pa_ref.dtype), pb.astype(pb_ref.dtype)
        m_ref[...] = (_sigmoid(ga_ref[...]) * pa + _sigmoid(gb_ref[...]) * pb).astype(m_ref.dtype)

    rows = pl.BlockSpec((tm, D), lambda i: (i, 0))
    whole = pl.BlockSpec((D, D), lambda i: (0, 0))
    return pl.pallas_call(
        body, name="branch_merge", grid=(T // tm,),
        in_specs=[rows, rows, whole, whole, rows, pl.BlockSpec((tm, D), lambda i: (i, 1))], out_specs=[rows] * 3,
        out_shape=[jax.ShapeDtypeStruct((T, D), SAVED_DTYPE)] * 2 + [jax.ShapeDtypeStruct((T, D), MXU_DTYPE)],
        compiler_params=_params(("parallel",)),
    )(y_a, y_b, w_a, w_b, gates, gates)


def _merge_bwd(dh, w_out, w_a, w_b, p_a, p_b, gates, *, tm):
    T, D = dh.shape

    def body(dh_ref, wo_ref, wa_ref, wb_ref, pa_ref, pb_ref, ga_ref, gb_ref, dpa_ref, dpb_ref, dga_ref, dgb_ref,
             dya_ref, dyb_ref):
        dm = _nt(dh_ref[...], wo_ref[...])
        sa, sb = _sigmoid(ga_ref[...]), _sigmoid(gb_ref[...])
        dpa, dpb = (dm * sa).astype(dpa_ref.dtype), (dm * sb).astype(dpb_ref.dtype)
        dpa_ref[...], dpb_ref[...] = dpa, dpb
        dga_ref[...] = (dm * pa_ref[...].astype(F32) * sa * (1.0 - sa)).astype(dga_ref.dtype)
        dgb_ref[...] = (dm * pb_ref[...].astype(F32) * sb * (1.0 - sb)).astype(dgb_ref.dtype)
        dya_ref[...] = _nt(dpa, wa_ref[...]).astype(dya_ref.dtype)
        dyb_ref[...] = _nt(dpb, wb_ref[...]).astype(dyb_ref.dtype)

    rows = pl.BlockSpec((tm, D), lambda i: (i, 0))
    whole = pl.BlockSpec((D, D), lambda i: (0, 0))
    low = jax.ShapeDtypeStruct((T, D), MXU_DTYPE)
    return pl.pallas_call(
        body, name="d_branch_merge", grid=(T // tm,),
        in_specs=[rows, whole, whole, whole, rows, rows, rows, pl.BlockSpec((tm, D), lambda i: (i, 1))],
        out_specs=[rows] * 6, out_shape=[low] * 5 + [jax.ShapeDtypeStruct((T, D), F32)],
        compiler_params=_params(("parallel",)),
    )(dh, w_out, w_a, w_b, p_a, p_b, gates, gates)


def _colsum_partials(p):
    return jnp.sum(p.reshape(-1, 8, p.shape[-1])[:, 0, :], axis=0, keepdims=True)


def _rmsnorm_bwd_vals(dy, xin, g):
    rstd = lax.rsqrt(jnp.mean(xin * xin, axis=-1, keepdims=True) + EPS)
    xhat = xin * rstd
    dg = jnp.sum(dy * xhat, axis=0, keepdims=True)
    dxh = dy * g
    dx = rstd * (dxh - xhat * jnp.mean(dxh * xhat, axis=-1, keepdims=True))
    return dx, dg


def _colsum(name, a, tr=512, comm=_NO_COPIES):
    T, N = a.shape

    def body(a_ref, o_ref):
        @pl.when(pl.program_id(0) == 0)
        def _():
            o_ref[...] = jnp.zeros_like(o_ref)

        o_ref[...] += jnp.sum(a_ref[...].astype(F32), axis=0, keepdims=True)

    (res,), got = _call(name, body, grid=(T // tr,), in_specs=[pl.BlockSpec((tr, N), lambda i: (i, 0))],
                        out_specs=[pl.BlockSpec((1, N), lambda i: (0, 0))],
                        out_shape=[jax.ShapeDtypeStruct((1, N), F32)], args=[a], semantics=("arbitrary",),
                        comm=None if comm is _NO_COPIES else comm)
    return res if comm is _NO_COPIES else (res, got)


ATTN_SCALE = 1.0 / math.sqrt(HEAD_DIM)
GROUP_LANES = GROUP * ATTN_BLOCK
PAIR = 2 * HEAD_DIM


def _attn_mask():
    kj = lax.broadcasted_iota(jnp.int32, (ATTN_BLOCK, GROUP_LANES), 0)
    qi = lax.broadcasted_iota(jnp.int32, (ATTN_BLOCK, GROUP_LANES), 1) & (ATTN_BLOCK - 1)
    return kj <= qi


def _heads_transposed(ref, g, scale=None):
    parts = []
    for a in range(GROUP // 2):
        lo = (g * GROUP // 2 + a) * PAIR
        pair = ref[:, lo:lo + PAIR].astype(F32)
        pair = (pair if scale is None else pair * scale).T
        parts += [pair[:HEAD_DIM], pair[HEAD_DIM:]]
    return jnp.concatenate(parts, axis=1).astype(MXU_DTYPE)


def _heads_back(ref, g, vt):
    for a in range(GROUP // 2):
        lo = (g * GROUP // 2 + a) * PAIR
        pair = jnp.concatenate([vt[:, (2 * a) * ATTN_BLOCK:(2 * a + 1) * ATTN_BLOCK],
                                vt[:, (2 * a + 1) * ATTN_BLOCK:(2 * a + 2) * ATTN_BLOCK]], axis=0)
        ref[:, lo:lo + PAIR] = pair.T.astype(ref.dtype)


def _kv_parts(kv_ref, g):
    ks = slice(g * HEAD_DIM, (g + 1) * HEAD_DIM)
    vs = slice(KV_WIDTH + g * HEAD_DIM, KV_WIDTH + (g + 1) * HEAD_DIM)
    return kv_ref[:, ks].astype(MXU_DTYPE), kv_ref[:, vs].astype(MXU_DTYPE)


def _sink_rows(sinks):
    return jnp.repeat(sinks.reshape(KV_HEADS, GROUP), ATTN_BLOCK, axis=1)


def _attn_fwd(pq, pkv, sinks, comm=None):
    T = pq.shape[0]
    nb = T // ATTN_BLOCK

    def body(q_ref, kvc_ref, kvp_ref, s_ref, y_ref, lse_ref):
        mask_c = _attn_mask()
        has_prev = pl.program_id(0) > 0
        for g in range(KV_HEADS):
            (kc, vc), (kp, vp) = _kv_parts(kvc_ref, g), _kv_parts(kvp_ref, g)
            qt = _heads_transposed(q_ref, g, ATTN_SCALE)
            s = jnp.where(mask_c, _nn(kc, qt), jnp.where(has_prev, _nn(kp, qt), NEG_INF))
            sink = s_ref[g:g + 1, :]
            m = jnp.maximum(jnp.max(s, axis=0, keepdims=True), sink)
            p = jnp.exp(s - m)
            den = jnp.sum(p, axis=0, keepdims=True) + jnp.exp(sink - m)
            pc = jnp.where(mask_c, p, 0.0)
            _heads_back(y_ref, g, (_tn(vc, pc) + _tn(vp, p - pc)) / den)
            lse = m + jnp.log(den)
            for i in range(GROUP):
                lse_ref[g * GROUP + i:g * GROUP + i + 1, :] = lse[:, i * ATTN_BLOCK:(i + 1) * ATTN_BLOCK]

    return _call(
        "attn_fwd", body, grid=(nb,),
        in_specs=[pl.BlockSpec((ATTN_BLOCK, D_MODEL), lambda n: (n, 0)),
                  pl.BlockSpec((ATTN_BLOCK, 2 * KV_WIDTH), lambda n: (n, 0)),
                  pl.BlockSpec((ATTN_BLOCK, 2 * KV_WIDTH), lambda n: (jnp.maximum(n - 1, 0), 0)),
                  pl.BlockSpec((KV_HEADS, GROUP_LANES), lambda n: (0, 0))],
        out_specs=[pl.BlockSpec((ATTN_BLOCK, D_MODEL), lambda n: (n, 0)),
                   pl.BlockSpec((Q_HEADS, ATTN_BLOCK), lambda n: (0, n))],
        out_shape=[jax.ShapeDtypeStruct((T, D_MODEL), MXU_DTYPE), jax.ShapeDtypeStruct((Q_HEADS, T), F32)],
        args=[pq, pkv, pkv, _sink_rows(sinks)], semantics=("parallel",), comm=comm)


def _attn_bwd(pq, pkv, sinks, lse, dy, comm=None):
    T = pq.shape[0]
    nb = T // ATTN_BLOCK
    cur = lambda n: (jnp.minimum(n, nb - 1), 0)

    def body(q_ref, kvc_ref, kvp_ref, s_ref, lse_ref, dy_ref, dq_ref, dkv_ref, ds_ref, carry, top, bot):
        n = pl.program_id(0)

        @pl.when(n == 0)
        def _():
            carry[...] = jnp.zeros_like(carry)
            ds_ref[...] = jnp.zeros_like(ds_ref)

        @pl.when(n < nb)
        def _():
            mask_c = _attn_mask()
            valid = jnp.logical_or(mask_c, n > 0)
            for g in range(KV_HEADS):
                ks = slice(g * HEAD_DIM, (g + 1) * HEAD_DIM)
                vs = slice(KV_WIDTH + g * HEAD_DIM, KV_WIDTH + (g + 1) * HEAD_DIM)
                (kc, vc), (kp, vp) = _kv_parts(kvc_ref, g), _kv_parts(kvp_ref, g)
                qt = _heads_transposed(q_ref, g, ATTN_SCALE)
                dot = _heads_transposed(dy_ref, g)
                lse = jnp.concatenate([lse_ref[g * GROUP + i:g * GROUP + i + 1, :] for i in range(GROUP)], axis=1)
                p = jnp.where(valid, jnp.exp(jnp.where(mask_c, _nn(kc, qt), _nn(kp, qt)) - lse), 0.0)
                dp = jnp.where(mask_c, _nn(vc, dot), _nn(vp, dot))
                delta = jnp.sum(p * dp, axis=0, keepdims=True)
                ds = p * (dp - delta)
                ds_c, p_c = jnp.where(mask_c, ds, 0.0), jnp.where(mask_c, p, 0.0)
                ds_p, p_p = ds - ds_c, p - p_c
                _heads_back(dq_ref, g, (_tn(kc, ds_c) + _tn(kp, ds_p)) * ATTN_SCALE)
                bot[:, ks], bot[:, vs] = _nt(ds_c, qt), _nt(p_c, dot)
                top[:, ks], top[:, vs] = _nt(ds_p, qt), _nt(p_p, dot)
                ds_ref[g:g + 1, :] -= jnp.exp(s_ref[g:g + 1, :] - lse) * delta
            dkv_ref[...] = (carry[...] + top[...]).astype(dkv_ref.dtype)
            carry[...] = bot[...]

        @pl.when(n == nb)
        def _():
            dkv_ref[...] = carry[...].astype(dkv_ref.dtype)

    return _call(
        "attn_bwd", body, grid=(nb + 1,),
        in_specs=[pl.BlockSpec((ATTN_BLOCK, D_MODEL), cur),
                  pl.BlockSpec((ATTN_BLOCK, 2 * KV_WIDTH), cur),
                  pl.BlockSpec((ATTN_BLOCK, 2 * KV_WIDTH), lambda n: (jnp.maximum(jnp.minimum(n, nb - 1) - 1, 0), 0)),
                  pl.BlockSpec((KV_HEADS, GROUP_LANES), lambda n: (0, 0)),
                  pl.BlockSpec((Q_HEADS, ATTN_BLOCK), lambda n: (0, jnp.minimum(n, nb - 1))),
                  pl.BlockSpec((ATTN_BLOCK, D_MODEL), cur)],
        out_specs=[pl.BlockSpec((ATTN_BLOCK, D_MODEL), cur),
                   pl.BlockSpec((ATTN_BLOCK, 2 * KV_WIDTH), lambda n: (jnp.maximum(n - 1, 0), 0)),
                   pl.BlockSpec((KV_HEADS, GROUP_LANES), lambda n: (0, 0))],
        out_shape=[jax.ShapeDtypeStruct((T, D_MODEL), MXU_DTYPE),
                   jax.ShapeDtypeStruct((T, 2 * KV_WIDTH), MXU_DTYPE),
                   jax.ShapeDtypeStruct((KV_HEADS, GROUP_LANES), F32)],
        scratch=[pltpu.VMEM((ATTN_BLOCK, 2 * KV_WIDTH), F32)] * 3,
        args=[pq, pkv, pkv, _sink_rows(sinks), lse, dy], semantics=("arbitrary",), comm=comm)


def _lower_bound(l):
    m = jnp.maximum(l[0:1], l[1:2])
    e0, e1 = jnp.exp(l[0:1] - m), jnp.exp(l[1:2] - m)
    return e0 / (e0 + e1)


def _tri(lower):
    r = lax.broadcasted_iota(jnp.int32, (CHUNK, CHUNK), 0)
    c = lax.broadcasted_iota(jnp.int32, (CHUNK, CHUNK), 1)
    return (r >= c) if lower else (c >= r)


def _chunk_sum(mask, v):
    ones = mask.astype(BF16)
    hi = v.astype(BF16)
    rest = v - hi.astype(F32)
    mid = rest.astype(BF16)
    lo = (rest - mid.astype(F32)).astype(BF16)
    part = lambda t: lax.dot_general(ones, t, (((1,), (0,)), ((), ())), preferred_element_type=F32)
    return part(hi) + part(mid) + part(lo)


def _hgrn_chunk_inputs(hq, hf, lb, causal):
    half_t = 0.5 * jnp.tanh(0.5 * hf)
    sg, sgn = 0.5 + half_t, 0.5 - half_t
    f = lb + (1.0 - lb) * sg
    kk = (1.0 - lb) * sgn
    sq = _sigmoid(hq)
    q = hq * sq
    b = _chunk_sum(causal, jnp.log(f))
    bm, bl = b[CHUNK // 2 - 1:CHUNK // 2, :], b[CHUNK - 1:CHUNK, :]
    e_qm, e_km = jnp.exp(b - bm), jnp.exp(bm - b)
    e_qs, e_kl = e_qm * jnp.exp(bm), e_km * jnp.exp(bl - bm)
    return dict(sg=sg, sgn=sgn, f=f, kk=kk, sq=sq, q=q, e_qm=e_qm, e_km=e_km, e_qs=e_qs, e_kl=e_kl,
                qm=q * e_qm, km=kk * e_km, qs=q * e_qs, kl=kk * e_kl, el=jnp.exp(bl))


def _hgrn_fwd(ph, lb_logits, norm_g, comm=None):
    T = ph.shape[0]
    nblk, cpb = T // HGRN_TOKENS, HGRN_TOKENS // CHUNK
    col = lambda c: pl.BlockSpec((HGRN_TOKENS, D_MODEL), functools.partial(lambda i, c: (i, c), c=c))

    def body(hq_ref, hf_ref, hi_ref, hg_ref, l_ref, ng_ref, y_ref, o_ref, st_ref, s_ref):
        @pl.when(pl.program_id(0) == 0)
        def _():
            s_ref[...] = jnp.zeros_like(s_ref)

        lb = _lower_bound(l_ref[...])
        causal = _tri(True)
        for c in range(cpb):
            rows = slice(c * CHUNK, (c + 1) * CHUNK)
            t = _hgrn_chunk_inputs(hq_ref[rows, :], hf_ref[rows, :], lb, causal)
            qm, km, qs, kl = (t[n].astype(MXU_DTYPE) for n in ("qm", "km", "qs", "kl"))
            v = hi_ref[rows, :].astype(MXU_DTYPE)
            for h in range(HGRN_HEADS):
                ls = slice(h * HGRN_K, (h + 1) * HGRN_K)
                st = s_ref[h]
                st_ref[c, ls, :] = st
                a = jnp.where(causal, _nt(qm[:, ls], km[:, ls]), 0.0)
                o_ref[rows, ls] = _nn(a, v[:, ls]) + _nt(qs[:, ls], st)
                s_ref[h] = t["el"][:, ls] * st + _tn(v[:, ls], kl[:, ls])
        for h in range(HGRN_HEADS):
            ls = slice(h * HGRN_K, (h + 1) * HGRN_K)
            o = o_ref[:, ls]
            r = lax.rsqrt(jnp.mean(o * o, axis=-1, keepdims=True) + EPS)
            y_ref[:, ls] = (o * r * ng_ref[:, ls] * _sigmoid(hg_ref[:, ls])).astype(y_ref.dtype)

    return _call(
        "hgrn_fwd", body, grid=(nblk,),
        in_specs=[col(0), col(1), col(2), col(3),
                  pl.BlockSpec((2, D_MODEL), lambda i: (0, 0)), pl.BlockSpec((1, D_MODEL), lambda i: (0, 0))],
        out_specs=[pl.BlockSpec((HGRN_TOKENS, D_MODEL), lambda i: (i, 0)),
                   pl.BlockSpec((HGRN_TOKENS, D_MODEL), lambda i: (i, 0)),
                   pl.BlockSpec((cpb, D_MODEL, HGRN_K), lambda i: (i, 0, 0))],
        out_shape=[jax.ShapeDtypeStruct((T, D_MODEL), MXU_DTYPE), jax.ShapeDtypeStruct((T, D_MODEL), F32),
                   jax.ShapeDtypeStruct((T // CHUNK, D_MODEL, HGRN_K), F32)],
        scratch=[pltpu.VMEM((HGRN_HEADS, HGRN_K, HGRN_K), F32)],
        args=[ph, ph, ph, ph, lb_logits, norm_g], semantics=("arbitrary",), comm=comm)


def _hgrn_bwd(ph, o_raw, states, dy, lb_logits, norm_g, comm=None):
    T = ph.shape[0]
    nblk, cpb = T // HGRN_TOKENS, HGRN_TOKENS // CHUNK
    rev = lambda i: nblk - 1 - i
    col = lambda c: pl.BlockSpec((HGRN_TOKENS, D_MODEL), functools.partial(lambda i, c: (rev(i), c), c=c))
    tok = pl.BlockSpec((HGRN_TOKENS, D_MODEL), lambda i: (rev(i), 0))

    def body(hq_ref, hf_ref, hi_ref, hg_ref, o_ref, st_ref, dy_ref, l_ref, ng_ref,
             dph_ref, dng_ref, dl_ref, dst_ref, dlb_ref, do_s, dqm_s, dkm_s, dqs_s, dkl_s, dv_s, del_s):
        i = pl.program_id(0)

        @pl.when(i == 0)
        def _():
            dst_ref[...] = jnp.zeros_like(dst_ref)
            dlb_ref[...] = jnp.zeros_like(dlb_ref)
            dng_ref[...] = jnp.zeros_like(dng_ref)

        lb = _lower_bound(l_ref[...])
        causal, anti = _tri(True), _tri(False)
        row = lax.broadcasted_iota(jnp.int32, (CHUNK, D_MODEL), 0)
        for c in reversed(range(cpb)):
            rows = slice(c * CHUNK, (c + 1) * CHUNK)
            hq = hq_ref[rows, :]
            t = _hgrn_chunk_inputs(hq, hf_ref[rows, :], lb, causal)
            sgg = _sigmoid(hg_ref[rows, :])
            dyv = dy_ref[rows, :]
            for h in range(HGRN_HEADS):
                ls = slice(h * HGRN_K, (h + 1) * HGRN_K)
                o = o_ref[rows, ls]
                r = lax.rsqrt(jnp.mean(o * o, axis=-1, keepdims=True) + EPS)
                nrm = o * r
                g_h = sgg[:, ls]
                dph_ref[rows, 3 * D_MODEL + h * HGRN_K:3 * D_MODEL + (h + 1) * HGRN_K] = (
                    dyv[:, ls] * nrm * ng_ref[:, ls] * g_h * (1.0 - g_h)).astype(dph_ref.dtype)
                dyg = dyv[:, ls] * g_h
                dng_ref[:, ls] += jnp.sum(dyg * nrm, axis=0, keepdims=True)
                dn = dyg * ng_ref[:, ls]
                do_s[:, ls] = r * (dn - nrm * jnp.mean(dn * nrm, axis=-1, keepdims=True))
            qm, km, qs, kl = (t[n].astype(MXU_DTYPE) for n in ("qm", "km", "qs", "kl"))
            v = hi_ref[rows, :].astype(MXU_DTYPE)
            do = do_s[...].astype(MXU_DTYPE)
            for h in range(HGRN_HEADS):
                ls = slice(h * HGRN_K, (h + 1) * HGRN_K)
                st = st_ref[c, ls, :]
                dst = dst_ref[h]
                a = jnp.where(causal, _nt(qm[:, ls], km[:, ls]), 0.0)
                da = jnp.where(causal, _nt(do[:, ls], v[:, ls]), 0.0)
                dv_s[:, ls] = _tn(a, do[:, ls]) + _nt(kl[:, ls], dst)
                dkl_s[:, ls] = _nn(v[:, ls], dst)
                dqs_s[:, ls] = _nn(do[:, ls], st)
                del_s[:, ls] = jnp.sum(dst * st, axis=0, keepdims=True)
                dst_ref[h] = _tn(do[:, ls], qs[:, ls]) + t["el"][:, ls] * dst
                dqm_s[:, ls] = _nn(da, km[:, ls])
                dkm_s[:, ls] = _tn(da, qm[:, ls])
            dqm, dkm, dqs, dkl = dqm_s[...], dkm_s[...], dqs_s[...], dkl_s[...]
            dq = dqm * t["e_qm"] + dqs * t["e_qs"]
            dk = dkm * t["e_km"] + dkl * t["e_kl"]
            t_qm, t_km, t_kl = dqm * t["qm"], dkm * t["km"], dkl * t["kl"]
            db = t_qm - t_km + dqs * t["qs"] - t_kl
            db_mid = jnp.sum(t_km - t_qm, axis=0, keepdims=True)
            db_last = jnp.sum(t_kl, axis=0, keepdims=True) + del_s[...] * t["el"]
            db = db + jnp.where(row == CHUNK // 2 - 1, db_mid, 0.0) + jnp.where(row == CHUNK - 1, db_last, 0.0)
            dlogf = _chunk_sum(anti, db)
            sq, sg, sgn, f = t["sq"], t["sg"], t["sgn"], t["f"]
            dph_ref[rows, 0:D_MODEL] = (dq * (sq * (1.0 + hq * (1.0 - sq)))).astype(dph_ref.dtype)
            dph_ref[rows, D_MODEL:2 * D_MODEL] = (
                dlogf * (1.0 - lb) * sg * (1.0 - sg) / f - dk * (1.0 - lb) * sgn * (1.0 - sgn)).astype(dph_ref.dtype)
            dph_ref[rows, 2 * D_MODEL:3 * D_MODEL] = dv_s[...].astype(dph_ref.dtype)
            dlb_ref[...] += jnp.sum(dlogf * (1.0 - sg) / f - dk * sgn, axis=0, keepdims=True)

        @pl.when(i == nblk - 1)
        def _():
            dl0 = dlb_ref[...] * lb * (1.0 - lb)
            dl_ref[0:1, :] = dl0
            dl_ref[1:2, :] = -dl0

    wide = pltpu.VMEM((CHUNK, D_MODEL), F32)
    return _call(
        "hgrn_bwd", body, grid=(nblk,),
        in_specs=[col(0), col(1), col(2), col(3), tok,
                  pl.BlockSpec((cpb, D_MODEL, HGRN_K), lambda i: (rev(i), 0, 0)), tok,
                  pl.BlockSpec((2, D_MODEL), lambda i: (0, 0)), pl.BlockSpec((1, D_MODEL), lambda i: (0, 0))],
        out_specs=[pl.BlockSpec((HGRN_TOKENS, 4 * D_MODEL), lambda i: (rev(i), 0)),
                   pl.BlockSpec((1, D_MODEL), lambda i: (0, 0)), pl.BlockSpec((2, D_MODEL), lambda i: (0, 0))],
        out_shape=[jax.ShapeDtypeStruct((T, 4 * D_MODEL), MXU_DTYPE), jax.ShapeDtypeStruct((1, D_MODEL), F32),
                   jax.ShapeDtypeStruct((2, D_MODEL), F32)],
        scratch=[pltpu.VMEM((HGRN_HEADS, HGRN_K, HGRN_K), F32), pltpu.VMEM((1, D_MODEL), F32),
                 wide, wide, wide, wide, wide, wide, pltpu.VMEM((1, D_MODEL), F32)],
        args=[ph, ph, ph, ph, o_raw, states, dy, lb_logits, norm_g], semantics=("arbitrary",), comm=comm)


def _local_step(x, target, vec, net):
    T, D = x.shape
    norm_mix_g, b_in, sinks, lb_logits = vec["norm_mix_g"], vec["b_in"], vec["attn_sinks"], vec["hgrn_lb_logits"]
    hgrn_norm_g, norm_ffn_g, norm_final_g = vec["hgrn_norm_g"], vec["norm_ffn_g"], vec["norm_final_g"]
    w_in = net.full("w_in")
    o_q, o_kv, o_h, o_g = (sum(IN_SPLITS[:i]) for i in range(4))
    w_q, w_kv, w_h, w_g = (_Rows(w_in, o, n) for o, n in zip((o_q, o_kv, o_h, o_g), IN_SPLITS))
    both = lambda acc: (acc, acc)
    grad_outs = [("mn", F32), ("mn", MXU_DTYPE)]
    row_vec = lambda n: ((1, n), lambda i, j: (0, j))
    tile = lambda tm, tn: ((tm, tn), lambda i, j: (i, j))
    TM = 512
    BIG = min(T, 1024)

    names = ("w_branch_attn", "w_branch_hgrn", "w_out")
    (u, pq, pkv, ph, pg), got = _in_proj(x, norm_mix_g, w_in, b_in, tm=256, comm=net.gather(names))
    net.gathered(names, got)
    names = ("w_ffn_gate",)
    (y_attn, lse), got = _attn_fwd(pq, pkv, sinks, comm=net.gather(names))
    net.gathered(names, got)
    names = ("w_ffn_up",)
    (y_hgrn, o_raw, states), got = _hgrn_fwd(ph, lb_logits, hgrn_norm_g, comm=net.gather(names))
    net.gathered(names, got)
    w_ba, w_bh, w_out = net.full("w_branch_attn"), net.full("w_branch_hgrn"), net.full("w_out")
    w_gate, w_up = net.full("w_ffn_gate"), net.full("w_ffn_up")
    ya, yb, merged = _merge_fwd(y_attn, y_hgrn, w_ba, w_bh, pg, tm=TM)

    FT = FFN // 2
    names = ("w_ffn_down",)
    (h1, u2, gpre, up, z), got = _ffn_fwd(merged, w_out, x, norm_ffn_g, w_gate, w_up, tm=256,
                                          comm=net.gather(names))
    net.gathered(names, got)
    w_down = net.full("w_ffn_down")

    dh2, dh2b, dgp, dup, dgf_p, loss_p = _ffn_tail(z, w_down, h1, target, norm_final_g, gpre, up, tm=256)
    loss = jnp.sum(loss_p.reshape(-1, 8, D)[:, 0, 0])
    d_norm_final = _colsum_partials(dgf_p)
    d_w_down = _matmul("dw_down", [(z, dh2b)], "tn", tm=FT, tn=1024, tk=T, outs=grad_outs, epilogue=both)

    names = ("w_ffn_down",)
    (dh1, dh1b, dg2_p), got = _ffn_in_bwd(dgp, dup, w_gate, w_up, h1, norm_ffn_g, dh2, tm=256,
        comm=net.exchange(dict(w_ffn_down=[d_w_down])))
    net.received(names, (), got)
    d_norm_ffn = _colsum_partials(dg2_p)
    d_w_gate = _matmul("dw_gate", [(dgp, u2)], "tn", tm=FT, tn=1024, tk=T, outs=grad_outs, epilogue=both)
    d_w_up = _matmul("dw_up", [(dup, u2)], "tn", tm=FT, tn=1024, tk=T, outs=grad_outs, epilogue=both)

    dya, dyb, dga, dgb, dy_attn, dy_hgrn = _merge_bwd(dh1b, w_out, w_ba, w_bh, ya, yb, pg, tm=TM)
    tn_grad = functools.partial(_matmul, mode="tn", tn=1024, tk=1024, outs=grad_outs, epilogue=both)
    d_w_out = tn_grad("dw_out", [(merged, dh1b)], tm=1024)
    d_w_ba = tn_grad("dw_branch_a", [(y_attn, dya)], tm=1024)
    d_w_bh = tn_grad("dw_branch_b", [(y_hgrn, dyb)], tm=1024)

    names, swap = ("w_ffn_gate",), ("w_ffn_down",)
    (dq, dkv, dsink), got = _attn_bwd(pq, pkv, sinks, lse, dy_attn,
                                      comm=net.exchange(dict(w_ffn_gate=[d_w_gate]), swap))
    net.received(names, swap, got)
    names, swap = ("w_ffn_up", "w_out", "w_branch_attn", "w_branch_hgrn"), ("w_ffn_gate",)
    (dph, d_hgrn_norm, d_lb_logits), got = _hgrn_bwd(
        ph, o_raw, states, dy_hgrn, lb_logits, hgrn_norm_g,
        comm=net.exchange(dict(w_ffn_up=[d_w_up], w_out=[d_w_out], w_branch_attn=[d_w_ba],
                               w_branch_hgrn=[d_w_bh]), swap))
    net.received(names, swap, got)

    d_w_in, db, rows_in = None, {}, sum(IN_SPLITS)
    for piece, dp, first, tm_p in (("q", dq, o_q, 1024), ("kv", dkv, o_kv, 256), ("h", dph, o_h, 1024),
                                   ("ga", dga, o_g, 1024), ("gb", dgb, o_g + D, 1024)):
        *d_w_in, db[piece] = tn_grad("dw_in_" + piece, [(dp, u)], tm=tm_p, a_colsum=True,
                                     into=_Into(rows_in, first, d_w_in), **(dict(tk=T) if piece == "h" else {}))
    d_w_in = [tuple(d_w_in)]

    first_level = net.presum_begin("w_in", d_w_in)
    halves = net.presum_end("w_in", [] if first_level is None else _copies_alone("presum_swap_w_in", first_level))
    names, swap = ("w_in",), ("w_ffn_up", "w_out", "w_branch_attn", "w_branch_hgrn")
    pieces = [(dq, o_q), (dkv, o_kv), (dph, o_h), (dga, o_g), (dgb, o_g + D)]
    (dx, dg1_p), got = _in_proj_bwd(pieces, w_in, x, norm_mix_g, dh1, tm=256, comm=_join(halves, net.swap(swap)))
    net.last = (names, swap, got)
    d_norm_mix = _colsum_partials(dg1_p)
    d_b_in = jnp.concatenate([db[piece] for piece in ("q", "kv", "h", "ga", "gb")], axis=1)
    vecs = dict(norm_mix_g=d_norm_mix, b_in=d_b_in, attn_sinks=jnp.sum(dsink.reshape(Q_HEADS, ATTN_BLOCK), axis=1).reshape(1, Q_HEADS),
                hgrn_lb_logits=d_lb_logits,
                hgrn_norm_g=d_hgrn_norm, norm_ffn_g=d_norm_ffn, norm_final_g=d_norm_final)
    return loss, dx, vecs


def _place():
    return lax.axis_index("x"), lax.axis_index("y"), lax.axis_index("c")


def _other_chips(x, y):
    return [(1 - x, y), (x, 1 - y), (1 - x, 1 - y)]


def _y_first(copies):
    return [copies[3 * (i // 3) + (1, 0, 2)[i % 3]] for i in range(len(copies))]


def _gather_copies(shards):
    n = len(shards)

    def build(ins, outs, send_sems, recv_sems, local_sems):
        x, y, c = _place()
        mine = 2 * x + y
        local = [pltpu.make_async_copy(ins[w], outs[w].at[mine], local_sems.at[w]) for w in range(n)]
        sends, recvs = [], []
        for w in range(n):
            for k, (px, py) in enumerate(_other_chips(x, y)):
                sem = 3 * w + k
                sends.append(pltpu.make_async_remote_copy(
                    src_ref=ins[w], dst_ref=outs[w].at[mine], send_sem=send_sems.at[sem], recv_sem=recv_sems.at[sem],
                    device_id=(px, py, c), device_id_type=MESH_ID))
                recvs.append(pltpu.make_async_remote_copy(
                    src_ref=ins[w], dst_ref=outs[w].at[2 * px + py], send_sem=send_sems.at[sem],
                    recv_sem=recv_sems.at[sem], device_id=(px, py, c), device_id_type=MESH_ID))
        return sends, recvs, local, _y_first(sends)

    return _Carried(shards, [jax.ShapeDtypeStruct((N_CHIPS,) + s.shape, s.dtype) for s in shards], 3 * n, n, build)


def _grad_copies(stacked):
    n = len(stacked)

    def build(ins, outs, send_sems, recv_sems, local_sems):
        x, y, c = _place()
        sends = []
        for w in range(n):
            for k, (px, py) in enumerate(_other_chips(x, y)):
                sem = 3 * w + k
                sends.append(pltpu.make_async_remote_copy(
                    src_ref=ins[w].at[2 * px + py], dst_ref=outs[w].at[k], send_sem=send_sems.at[sem],
                    recv_sem=recv_sems.at[sem], device_id=(px, py, c), device_id_type=MESH_ID))
        return sends, sends, [], _y_first(sends)

    return _Carried(stacked, [jax.ShapeDtypeStruct((3,) + s.shape[1:], s.dtype) for s in stacked], 3 * n, 0, build)


def _small_copies(small):
    def build(ins, outs, send_sems, recv_sems, local_sems):
        small_ref, all_ref = ins[0], outs[0]
        x, y, c = _place()
        me = 4 * x + 2 * y + c
        sends, recvs = [], []
        for r in range(1, 8):
            px = 1 - x if r & 4 else x
            py = 1 - y if r & 2 else y
            pc = 1 - c if r & 1 else c
            sends.append(pltpu.make_async_remote_copy(
                src_ref=small_ref, dst_ref=all_ref.at[me], send_sem=send_sems.at[r - 1], recv_sem=recv_sems.at[r - 1],
                device_id=(px, py, pc), device_id_type=MESH_ID))
            recvs.append(pltpu.make_async_remote_copy(
                src_ref=small_ref, dst_ref=all_ref.at[4 * px + 2 * py + pc], send_sem=send_sems.at[r - 1],
                recv_sem=recv_sems.at[r - 1], device_id=(px, py, pc), device_id_type=MESH_ID))
        return sends, recvs, [pltpu.make_async_copy(small_ref, all_ref.at[me], local_sems.at[0])]

    return _Carried([small], [jax.ShapeDtypeStruct((8,) + small.shape, small.dtype)], 7, 1, build)


def _gather_by_neighbours(name, shard):
    half = shard.shape[0] // 2
    quarter = half // 2

    def body(in_ref, out_ref, send_sems, recv_sems, local_sem):
        for core in (0, 1):
            @pl.when(lax.axis_index("c") == core)
            def _():
                program(core, in_ref, out_ref, send_sems, recv_sems, local_sem)

    def program(c, in_ref, out_ref, send_sems, recv_sems, local_sem):
        x, y, _ = _place()
        chip = lambda px, py: 2 * px + py
        to_x, to_y, sibling = (1 - x, y, c), (x, 1 - y, c), (x, y, 1 - c)
        x_blk, y_blk, d_blk = chip(1 - x, y), chip(x, 1 - y), chip(1 - x, 1 - y)
        mine, theirs = c * half, (1 - c) * half

        def copy(sem, rows, block, to, src=None):
            place = out_ref.at[block, pl.ds(rows[0], rows[1])]
            return pltpu.make_async_remote_copy(
                src_ref=place if src is None else src, dst_ref=place, send_sem=send_sems.at[sem],
                recv_sem=recv_sems.at[sem], device_id=to, device_id_type=MESH_ID)

        own = pltpu.make_async_copy(in_ref, out_ref.at[chip(x, y)], local_sem)
        own.start()
        my_rows = in_ref.at[pl.ds(mine, half)]
        along_x = dict(send=copy(0, (mine, half), chip(x, y), to_x, src=my_rows),
                       landed=copy(0, (mine, half), x_blk, to_x),
                       onward=[copy(3, (mine + quarter, quarter), x_blk, to_y), copy(4, (mine, half), x_blk, sibling)],
                       diagonal=copy(2, (mine, quarter), d_blk, to_x))
        along_y = dict(send=copy(1, (mine, half), chip(x, y), to_y, src=my_rows),
                       landed=copy(1, (mine, half), y_blk, to_y),
                       onward=[copy(2, (mine, quarter), y_blk, to_x), copy(5, (mine, half), y_blk, sibling)],
                       diagonal=copy(3, (mine + quarter, quarter), d_blk, to_y))
        last = copy(6, (mine, half), d_blk, sibling)

        order = (along_x, along_y) if c == 0 else (along_y, along_x)
        for axis in order:
            axis["send"].start()
        for axis in order:
            axis["landed"].wait_recv()
            for cp in axis["onward"]:
                cp.start()
        for axis in order:
            axis["diagonal"].wait_recv()
        last.start()
        for sem, block in ((4, x_blk), (5, y_blk), (6, d_blk)):
            copy(sem, (theirs, half), block, sibling).wait_recv()
        for cp in [along_x["send"], along_y["send"]] + along_x["onward"] + along_y["onward"] + [last]:
            cp.wait_send()
        own.wait()

    return pl.pallas_call(
        body, name=name, in_specs=[HBM_SPEC], out_specs=HBM_SPEC,
        out_shape=jax.ShapeDtypeStruct((N_CHIPS,) + shard.shape, shard.dtype),
        scratch_shapes=[pltpu.SemaphoreType.DMA((7,)), pltpu.SemaphoreType.DMA((7,)), pltpu.SemaphoreType.DMA(())],
    )(shard)


def _copies_alone(name, comm):
    return _call(name, lambda: None, grid=(), in_specs=[], out_specs=[], out_shape=[], args=[], comm=comm)[1]


class _Net:
    def __init__(self, shards):
        self.shards = shards
        self.whole, self.own, self.theirs, self.sums, self.other = {}, {}, {}, {}, {}
        x, y, _ = _place()
        self.chip = 2 * x + y

    def gather(self, names):
        return _gather_copies([self.shards[n] for n in names])

    def gathered(self, names, got):
        for n, g in zip(names, got):
            self.whole[n] = g.reshape(-1, g.shape[-1])

    def full(self, name):
        return self.whole[name]

    def exchange(self, grads, swap=()):
        stacked = []
        for n, pieces in grads.items():
            (keep, send), = pieces
            self.own[n] = keep
            stacked.append(send.reshape(N_CHIPS, keep.shape[0] // N_CHIPS, send.shape[-1]))
        return _join(_grad_copies(stacked), self.swap(swap))

    def swap(self, names):
        return _sibling_copies([self.sums[n] for n in names]) if names else None

    def presum_begin(self, name, pieces):
        keep = jnp.concatenate([p[0] for p in pieces], axis=0) if len(pieces) > 1 else pieces[0][0]
        send = jnp.concatenate([p[1] for p in pieces], axis=0) if len(pieces) > 1 else pieces[0][1]
        rows = keep.shape[0] // N_CHIPS
        self.held = keep.reshape(N_CHIPS, rows, keep.shape[-1])
        return _half_rows_copies(send.reshape(N_CHIPS, rows, send.shape[-1]))

    def presum_end(self, name, got):
        x, y, c = _place()
        to_send, self.own[name] = _pre_sum("presum_" + name, self.held, got[0], jnp.stack([c, self.chip]))
        return _grad_copies([to_send])

    def received(self, names, swap, got, carried=None):
        self.theirs.update(zip(names, got[:len(names)]))
        self.other.update(zip(swap, got[len(names):]))
        for n in names:
            (self.sums[n],), more = _partial_sum("sum_" + n, self.own[n], self.theirs[n], self.chip, comm=carried)
        return more


def _half_rows_copies(stacked):
    n, rows = stacked.shape[0], stacked.shape[1] // 2

    def build(ins, outs, send_sems, recv_sems, local_sems):
        x, y, c = _place()
        copies = [pltpu.make_async_remote_copy(
            src_ref=ins[0].at[s, pl.ds((1 - c) * rows, rows)], dst_ref=outs[0].at[s], send_sem=send_sems.at[s],
            recv_sem=recv_sems.at[s], device_id=(x, y, 1 - c), device_id_type=MESH_ID) for s in range(n)]
        return copies, copies, []

    return _Carried([stacked], [jax.ShapeDtypeStruct((n, rows, stacked.shape[2]), stacked.dtype)], n, 0, build)


def _pre_sum(name, held, theirs, core_and_chip):
    n, R, C = held.shape
    half = R // 2
    tr = _row_tile(half)
    per_half = half // tr

    def body(place_ref, h_ref, t_ref, send_ref, own_ref):
        total = h_ref[0] + t_ref[0].astype(F32)
        send_ref[0] = total.astype(send_ref.dtype)

        @pl.when(pl.program_id(1) == place_ref[1])
        def _():
            own_ref[...] = total

    return pl.pallas_call(
        body, name=name,
        grid_spec=pltpu.PrefetchScalarGridSpec(
            num_scalar_prefetch=1, grid=(per_half, n),
            in_specs=[pl.BlockSpec((1, tr, C), lambda i, s, place: (s, place[0] * per_half + i, 0)),
                      pl.BlockSpec((1, tr, C), lambda i, s, place: (s, i, 0))],
            out_specs=[pl.BlockSpec((1, tr, C), lambda i, s, place: (s, i, 0)),
                       pl.BlockSpec((tr, C), lambda i, s, place: (i, 0))]),
        out_shape=[jax.ShapeDtypeStruct((n, half, C), MXU_DTYPE), jax.ShapeDtypeStruct((half, C), F32)],
        compiler_params=_params(("arbitrary", "arbitrary")),
    )(core_and_chip, held, theirs)


def _sibling_copies(parts):
    n = len(parts)

    def build(ins, outs, send_sems, recv_sems, local_sems):
        x, y, c = _place()
        copies = [pltpu.make_async_remote_copy(
            src_ref=ins[w], dst_ref=outs[w], send_sem=send_sems.at[w], recv_sem=recv_sems.at[w],
            device_id=(x, y, 1 - c), device_id_type=MESH_ID) for w in range(n)]
        return copies, copies, []

    return _Carried(parts, [jax.ShapeDtypeStruct(p.shape, p.dtype) for p in parts], n, 0, build)


def _row_tile(rows, most=512, sublanes=16):
    return max(t for t in range(sublanes, most + 1, sublanes) if rows % t == 0)


def _partial_sum(name, own, recv, chip, comm=None):
    _, R, C = recv.shape
    tr = _row_tile(R, most=128)

    def body(o_ref, r_ref, p_ref):
        p_ref[...] = ((o_ref[...] + r_ref[0].astype(F32)) + r_ref[1].astype(F32)) + r_ref[2].astype(F32)

    if own.shape[0] != R:
        assert comm is None and own.shape[0] == N_CHIPS * R
        total = pl.pallas_call(
            lambda chip_ref, *refs: body(*refs), name=name,
            grid_spec=pltpu.PrefetchScalarGridSpec(
                num_scalar_prefetch=1, grid=(R // tr,),
                in_specs=[pl.BlockSpec((tr, C), lambda i, chip_ref: (chip_ref[0] * (R // tr) + i, 0)),
                          pl.BlockSpec((3, tr, C), lambda i, chip_ref: (0, i, 0))],
                out_specs=pl.BlockSpec((tr, C), lambda i, chip_ref: (i, 0))),
            out_shape=jax.ShapeDtypeStruct((R, C), F32), compiler_params=_params(("parallel",)),
        )(chip.reshape(1), own, recv)
        return [total], []
    return _call(name, body, grid=(R // tr,),
                 in_specs=[pl.BlockSpec((tr, C), lambda i: (i, 0)), pl.BlockSpec((3, tr, C), lambda i: (0, i, 0))],
                 out_specs=[pl.BlockSpec((tr, C), lambda i: (i, 0))], out_shape=[jax.ShapeDtypeStruct((R, C), F32)],
                 args=[own, recv], semantics=("parallel",), comm=comm)


def _adam_vals(w, g, m, v):
    m = ADAM_B1 * m + (1.0 - ADAM_B1) * g
    v = ADAM_B2 * v + (1.0 - ADAM_B2) * (g * g)
    m_hat = m / (1.0 - ADAM_B1 ** ADAM_STEP)
    v_hat = v / (1.0 - ADAM_B2 ** ADAM_STEP)
    delta = -ADAM_LR * (m_hat / (jnp.sqrt(v_hat) + ADAM_EPS) + ADAM_WD * w)
    return delta, m, v


def _adamw(name, w, m, v, mine, other, comm=None):
    R, C = w.shape
    tr = _row_tile(R)

    def body(w_ref, m_ref, v_ref, s_ref, n_ref, g_ref, d_ref, nm_ref, nv_ref):
        g = s_ref[...] + n_ref[...]
        d, nm, nv = _adam_vals(w_ref[...], g, m_ref[...], v_ref[...])
        g_ref[...], d_ref[...], nm_ref[...], nv_ref[...] = g, d, nm, nv

    spec = pl.BlockSpec((tr, C), lambda i: (i, 0))
    return _call(name, body, grid=(R // tr,), in_specs=[spec] * 5, out_specs=[spec] * 4,
                 out_shape=[jax.ShapeDtypeStruct((R, C), F32)] * 4, args=[w, m, v, mine, other],
                 semantics=("parallel",), comm=comm)


def _adamw_by_halves(name, w, m, v, mine, other, core):
    R, C = w.shape
    tr = _row_tile(R // 2)
    per_half = R // 2 // tr

    def body(c_ref, w_ref, m_ref, v_ref, s_ref, n_ref, g_ref, d_ref, nm_ref, nv_ref):
        g = jnp.where(pl.program_id(0) // per_half == c_ref[0, 0], s_ref[...], n_ref[...])
        d, nm, nv = _adam_vals(w_ref[...], g, m_ref[...], v_ref[...])
        g_ref[...], d_ref[...], nm_ref[...], nv_ref[...] = g, d, nm, nv

    spec = pl.BlockSpec((tr, C), lambda i: (i, 0))
    part = pl.BlockSpec((tr, C), lambda i: (i % per_half, 0))
    return pl.pallas_call(
        body, name=name, grid=(R // tr,),
        in_specs=[pl.BlockSpec(memory_space=pltpu.SMEM), spec, spec, spec, part, part], out_specs=[spec] * 4,
        out_shape=[jax.ShapeDtypeStruct((R, C), F32)] * 4, compiler_params=_params(("parallel",)),
    )(core, w, m, v, mine, other)


SMALL_LAYOUT = dict(norm_mix_g=(0, 1, 1024), b_in=(1, 8, 7424), hgrn_norm_g=(9, 1, 1024), norm_ffn_g=(10, 1, 1024),
                    norm_final_g=(11, 1, 1024), hgrn_lb_logits=(12, 2, 2048), attn_sinks=(14, 1, 16))
SMALL_LOSS_ROW, SMALL_ROWS = 15, 16


def _pack_small(grads, loss):
    rows = [jnp.pad(grads[name].astype(F32).reshape(-1), (0, nrows * D_MODEL - n))
            for name, (_, nrows, n) in SMALL_LAYOUT.items()]
    rows.append(jnp.pad(loss.astype(F32).reshape(1), (0, D_MODEL - 1)))
    return jnp.concatenate(rows).reshape(SMALL_ROWS, D_MODEL)


def _adamw_small(w, m, v, g_all):
    names = list(SMALL_LAYOUT)
    n = len(names)

    def body(a_ref, *refs):
        ins, outs = refs[:3 * n], refs[3 * n:]
        g_all_rows = a_ref[0]
        for dev in range(1, 8):
            g_all_rows = g_all_rows + a_ref[dev]
        for i, name in enumerate(names):
            first, nrows, count = SMALL_LAYOUT[name]
            w_ref, m_ref, v_ref = ins[3 * i:3 * i + 3]
            if w_ref.shape[0] == nrows:
                g = g_all_rows[first:first + nrows, :w_ref.shape[1]]
            else:
                last = count - (nrows - 1) * D_MODEL
                g = jnp.concatenate([g_all_rows[r:r + 1, :] for r in range(first, first + nrows - 1)]
                                    + [g_all_rows[first + nrows - 1:first + nrows, :last]], axis=1)
            d, nm, nv = _adam_vals(w_ref[...], g, m_ref[...], v_ref[...])
            for o_ref, val in zip(outs[4 * i:4 * i + 4], (g, d, nm, nv)):
                o_ref[...] = val
        outs[4 * n][...] = g_all_rows[SMALL_LOSS_ROW:SMALL_LOSS_ROW + 1, 0:1]

    res = pl.pallas_call(
        body, name="adamw_small",
        out_shape=[jax.ShapeDtypeStruct(w[name].shape, F32) for name in names for _ in range(4)]
        + [jax.ShapeDtypeStruct((1, 1), F32)],
    )(g_all, *[t[name] for name in names for t in (w, m, v)])
    return {name: res[4 * i:4 * i + 4] for i, name in enumerate(names)}, res[4 * n]


MATRICES = ("w_in", "w_branch_attn", "w_branch_hgrn", "w_out", "w_ffn_gate", "w_ffn_up", "w_ffn_down")
COLUMN_SHARDED = ("w_in", "w_ffn_gate", "w_ffn_up")
WEIGHTS = ("norm_mix_g", "w_in", "b_in", "attn_sinks", "hgrn_lb_logits", "hgrn_norm_g", "w_branch_attn",
           "w_branch_hgrn", "w_out", "norm_ffn_g", "w_ffn_gate", "w_ffn_up", "w_ffn_down", "norm_final_g")


def kernel(x, norm_mix_g, w_in, b_in, attn_sinks, hgrn_lb_logits, hgrn_norm_g, w_branch_attn, w_branch_hgrn, w_out, norm_ffn_g, w_ffn_gate, w_ffn_up, w_ffn_down, norm_final_g, loss_target, m_norm_mix_g, m_w_in, m_b_in, m_attn_sinks, m_hgrn_lb_logits, m_hgrn_norm_g, m_w_branch_attn, m_w_branch_hgrn, m_w_out, m_norm_ffn_g, m_w_ffn_gate, m_w_ffn_up, m_w_ffn_down, m_norm_final_g, v_norm_mix_g, v_w_in, v_b_in, v_attn_sinks, v_hgrn_lb_logits, v_hgrn_norm_g, v_w_branch_attn, v_w_branch_hgrn, v_w_out, v_norm_ffn_g, v_w_ffn_gate, v_w_ffn_up, v_w_ffn_down, v_norm_final_g):
    given = dict(locals())
    w = {n: given[n] for n in WEIGHTS}
    m = {n: given["m_" + n] for n in WEIGHTS}
    v = {n: given["v_" + n] for n in WEIGHTS}

    block = lambda a, n: jnp.transpose(a[0]) if n in COLUMN_SHARDED else a[0]
    unblock = lambda a, n: (jnp.transpose(a) if n in COLUMN_SHARDED else a)[None]
    net = _Net({n: block(w[n], n).astype(MXU_DTYPE) for n in MATRICES})
    net.gathered(("w_in",), [_gather_by_neighbours("gather_w_in", net.shards["w_in"])])
    vec = dict(norm_mix_g=norm_mix_g, b_in=b_in, attn_sinks=attn_sinks, hgrn_lb_logits=hgrn_lb_logits,
               hgrn_norm_g=hgrn_norm_g, norm_ffn_g=norm_ffn_g, norm_final_g=norm_final_g.reshape(1, D_MODEL))
    loss_part, dx, d_vecs = _local_step(x[0], loss_target[0], vec, net)

    small_all, = net.received(*net.last, carried=_small_copies(_pack_small(d_vecs, loss_part)))
    grads, deltas, new_m, new_v = {}, {}, {}, {}
    for n in ("w_ffn_down", "w_ffn_gate", "w_ffn_up", "w_out", "w_branch_attn", "w_branch_hgrn"):
        res, got = _adamw("adamw_" + n, block(w[n], n), block(m[n], n), block(v[n], n), net.sums[n], net.other[n],
                          comm=net.swap(("w_in",)) if n == "w_ffn_down" else None)
        if n == "w_ffn_down":
            net.other["w_in"], = got
        grads[n], deltas[n], new_m[n], new_v[n] = (unblock(r, n) for r in res)
    n = "w_in"
    res = _adamw_by_halves("adamw_" + n, block(w[n], n), block(m[n], n), block(v[n], n), net.sums[n], net.other[n],
                           _place()[2].reshape(1, 1))
    grads[n], deltas[n], new_m[n], new_v[n] = (unblock(r, n) for r in res)
    rows = lambda t: {n: t[n].reshape(-1, t[n].shape[-1]) for n in SMALL_LAYOUT}
    res, loss = _adamw_small(rows(w), rows(m), rows(v), small_all)
    for n, four in res.items():
        grads[n], deltas[n], new_m[n], new_v[n] = (r.reshape(w[n].shape) for r in four)
    loss = loss.reshape(())
    return (loss, dx[None], *[grads[n] for n in WEIGHTS], *[deltas[n] for n in WEIGHTS],
            *[new_m[n] for n in WEIGHTS], *[new_v[n] for n in WEIGHTS])
```

```python
import collections
import functools
import math

import jax
import jax.numpy as jnp
from jax import lax
from jax.experimental import pallas as pl
from jax.experimental.pallas import tpu as pltpu

F32 = jnp.float32
BF16 = jnp.bfloat16
MXU_DTYPE = jnp.bfloat16
SAVED_DTYPE = jnp.bfloat16
MESH_ID = pl.DeviceIdType.MESH

D_MODEL = 1024
HEAD_DIM = 64
Q_HEADS = 16
KV_HEADS = 2
GROUP = Q_HEADS // KV_HEADS
KV_WIDTH = KV_HEADS * HEAD_DIM
ATTN_BLOCK = 128
HGRN_HEADS = 8
HGRN_K = 128
CHUNK = 64
HGRN_TOKENS = 256
FFN = 2816
IN_SPLITS = (1024, 256, 4096, 2048)
EPS = 1e-6
NEG_INF = -1e30
ADAM_LR, ADAM_B1, ADAM_B2, ADAM_EPS, ADAM_WD, ADAM_STEP = 0.001, 0.9, 0.999, 1e-08, 0.01, 10
N_CHIPS = 4
VMEM_LIMIT = 60 * 1024 * 1024
ROW_ALIGN = 16


def _params(sem=None):
    return pltpu.CompilerParams(dimension_semantics=sem, vmem_limit_bytes=VMEM_LIMIT)


def _sigmoid(v):
    return 0.5 * jnp.tanh(0.5 * v) + 0.5


def _dot(a, b, dims):
    return lax.dot_general(a.astype(MXU_DTYPE), b.astype(MXU_DTYPE), (dims, ((), ())),
                           preferred_element_type=F32)


def _nn(a, b):
    return _dot(a, b, ((1,), (0,)))


def _nt(a, b):
    return _dot(a, b, ((1,), (1,)))


def _tn(a, b):
    return _dot(a, b, ((0,), (0,)))


HBM_SPEC = pl.BlockSpec(memory_space=pl.ANY)


class _Carried:
    def __init__(self, arrays, out_shapes, n_remote, n_local, build):
        self.parts = [(len(arrays), len(out_shapes), build)]
        self.arrays, self.out_shapes = list(arrays), list(out_shapes)
        self.scratch = [pltpu.SemaphoreType.DMA((n_remote,)), pltpu.SemaphoreType.DMA((n_remote,)),
                        pltpu.SemaphoreType.DMA((max(n_local, 1),))]

    def __add__(self, other):
        both = _Carried([], [], 1, 0, None)
        both.parts = self.parts + other.parts
        both.arrays, both.out_shapes = self.arrays + other.arrays, self.out_shapes + other.out_shapes
        both.scratch = self.scratch + other.scratch
        return both

    def _built(self, ins, outs, sems):
        for p, (ni, no, build) in enumerate(self.parts):
            yield build(ins[:ni], outs[:no], *sems[3 * p:3 * p + 3])
            ins, outs = ins[ni:], outs[no:]

    def start(self, ins, outs, sems):
        core = lax.axis_index("c")
        for sends, _, local, *other_order in self._built(ins, outs, sems):
            for cp in local:
                cp.start()
            if not other_order:
                for cp in sends:
                    cp.start()
                continue

            @pl.when(core == 0)
            def _():
                for cp in sends:
                    cp.start()

            @pl.when(core == 1)
            def _():
                for cp in other_order[0]:
                    cp.start()

    def wait(self, ins, outs, sems):
        for sends, recvs, local, *_ in self._built(ins, outs, sems):
            for cp in recvs:
                cp.wait_recv()
            for cp in sends:
                cp.wait_send()
            for cp in local:
                cp.wait()


def _join(*comms):
    comms = [c for c in comms if c is not None]
    return functools.reduce(lambda a, b: a + b, comms) if comms else None


def _call(name, body, *, grid, in_specs, out_specs, out_shape, args, scratch=(), semantics=None, comm=None,
          aliases=None):
    n_in, n_out, n_scr = len(in_specs), len(out_specs), len(scratch)
    aliases = aliases or {}
    if comm is None:
        res = pl.pallas_call(body, name=name, grid=grid, in_specs=in_specs, out_specs=out_specs, out_shape=out_shape,
                             scratch_shapes=list(scratch), input_output_aliases=aliases,
                             compiler_params=_params(semantics))(*args)
        return list(res), []
    ci, co = len(comm.arrays), len(comm.out_shapes)

    def carrying(*refs):
        ins, refs = refs[:n_in], refs[n_in:]
        c_ins, refs = refs[:ci], refs[ci:]
        outs, refs = refs[:n_out], refs[n_out:]
        c_outs, refs = refs[:co], refs[co:]
        scr, sems = refs[:n_scr], refs[n_scr:]
        if not grid:
            comm.start(c_ins, c_outs, sems)
            body(*ins, *outs, *scr)
            comm.wait(c_ins, c_outs, sems)
            return
        first = functools.reduce(jnp.logical_and, [pl.program_id(a) == 0 for a in range(len(grid))])
        last = functools.reduce(jnp.logical_and, [pl.program_id(a) == g - 1 for a, g in enumerate(grid)])

        @pl.when(first)
        def _():
            comm.start(c_ins, c_outs, sems)

        body(*ins, *outs, *scr)

        @pl.when(last)
        def _():
            comm.wait(c_ins, c_outs, sems)

    res = pl.pallas_call(
        carrying, name=name, grid=grid, in_specs=list(in_specs) + [HBM_SPEC] * ci,
        out_specs=list(out_specs) + [HBM_SPEC] * co, out_shape=list(out_shape) + comm.out_shapes,
        scratch_shapes=list(scratch) + comm.scratch, input_output_aliases=aliases,
        compiler_params=_params(("arbitrary",) * len(grid) if grid else None),
    )(*args, *comm.arrays)
    return list(res[:n_out]), list(res[n_out:])


_NO_COPIES = object()
_Rows = collections.namedtuple("_Rows", "array first rows")
_Cols = collections.namedtuple("_Cols", "array first cols")


_Into = collections.namedtuple("_Into", "rows first held")


def _matmul(name, pairs, mode, *, tm, tn, tk, outs, extras=(), epilogue=None, a_colsum=False, into=None,
            comm=_NO_COPIES):
    prod = dict(nn=_nn, nt=_nt, tn=_tn)[mode]
    pairs = [(a, b if isinstance(b, _Rows) else _Rows(b, 0, b.shape[0])) for a, b in pairs]
    a0, b0 = pairs[0]
    a_cols = a0 if isinstance(a0, _Cols) else None
    if a_cols is not None:
        assert mode == "tn" and len(pairs) == 1 and a_cols.first % tm == 0, name
        pairs = [(a_cols.array, b0)]
        a0 = a_cols.array
    M = (a_cols.cols if a_cols is not None else a0.shape[1]) if mode == "tn" else a0.shape[0]
    N = b0.rows if mode == "nt" else b0.array.shape[1]
    steps, in_specs, offset = [], [], 0
    for a, b in pairs:
        K = a.shape[0] if mode == "tn" else a.shape[1]
        t = min(tk, K)
        assert K % t == 0, (name, K, t)
        kmap = functools.partial(lambda k, off, n: jnp.clip(k - off, 0, n - 1), off=offset, n=K // t)
        if mode == "tn":
            tile0 = a_cols.first // tm if a_cols is not None else 0
            in_specs.append(pl.BlockSpec((t, tm), functools.partial(
                lambda i, j, k, f, tile0: (f(k), tile0 + i), f=kmap, tile0=tile0)))
        else:
            in_specs.append(pl.BlockSpec((tm, t), functools.partial(lambda i, j, k, f: (i, f(k)), f=kmap)))
        whole = b.first == 0 and b.rows == b.array.shape[0]
        if mode == "nt":
            shape = (tn, t) if whole else (pl.Element(tn), pl.Element(t))
            in_specs.append(pl.BlockSpec(shape, functools.partial(
                lambda i, j, k, f, b, t, whole: (j, f(k)) if whole else (
                    pl.multiple_of(b.first + j * tn, ROW_ALIGN), pl.multiple_of(f(k) * t, 128)),
                f=kmap, b=b, t=t, whole=whole)))
        else:
            assert b.rows == K, (name, b.rows, K)
            shape = (t, tn) if whole else (pl.Element(t), pl.Element(tn))
            in_specs.append(pl.BlockSpec(shape, functools.partial(
                lambda i, j, k, f, b, t, whole: (f(k), j) if whole else (
                    pl.multiple_of(b.first + f(k) * t, ROW_ALIGN), pl.multiple_of(j * tn, 128)),
                f=kmap, b=b, t=t, whole=whole)))
        steps.append((offset, offset + K // t))
        offset += K // t
    assert M % tm == 0 and N % tn == 0, (name, M, N, tm, tn)
    ni, nj, nk = M // tm, N // tn, offset
    npair, ne, no = len(pairs), len(extras), len(outs)
    if epilogue is None:
        epilogue = lambda acc: (acc,)

    def finish(acc, extra_refs, out_refs):
        vals = epilogue(acc, *[r[...] for r in extra_refs])
        for (kind, _), o_ref, val in zip(outs, out_refs, vals):
            if kind == "pn":
                val = jnp.broadcast_to(val, o_ref.shape)
            o_ref[...] = val.astype(o_ref.dtype)

    held = list(into.held) if into is not None and into.held is not None else []

    def body(*refs):
        ab, rest = refs[:2 * npair], refs[2 * npair:]
        extra_refs, rest = rest[:ne], rest[ne + len(held):]
        out_refs = rest[:no]
        if nk == 1:
            finish(prod(ab[0][...], ab[1][...]), extra_refs, out_refs)
            if a_colsum:
                rest[no][...] = jnp.sum(ab[0][...].astype(F32), axis=0, keepdims=True)
            return
        sums_ref, acc_ref = rest[no], rest[-1]
        k = pl.program_id(2)

        @pl.when(k == 0)
        def _():
            acc_ref[...] = jnp.zeros_like(acc_ref)
            if a_colsum:
                sums_ref[...] = jnp.zeros((1, tm), F32)

        if a_colsum:
            sums_ref[...] += jnp.sum(ab[0][...].astype(F32), axis=0, keepdims=True)
        for p, (lo, hi) in enumerate(steps):
            @pl.when(jnp.logical_and(k >= lo, k < hi))
            def _():
                acc_ref[...] += prod(ab[2 * p][...], ab[2 * p + 1][...])

        @pl.when(k == nk - 1)
        def _():
            finish(acc_ref[...], extra_refs, out_refs)

    for _, shape, im in extras:
        in_specs.append(pl.BlockSpec(shape, functools.partial(lambda i, j, k, im: im(i, j), im=im)))
    in_specs += [HBM_SPEC] * len(held)
    aliases = {2 * npair + ne + p: p for p in range(len(held))}
    out_shape, out_specs = [], []
    for kind, dt in outs:
        if kind == "mn" and into is not None:
            out_shape.append(jax.ShapeDtypeStruct((into.rows, N), dt))
            out_specs.append(pl.BlockSpec((pl.Element(tm), pl.Element(tn)), lambda i, j, k: (
                pl.multiple_of(into.first + i * tm, ROW_ALIGN), pl.multiple_of(j * tn, 128))))
        elif kind == "mn":
            out_shape.append(jax.ShapeDtypeStruct((M, N), dt))
            out_specs.append(pl.BlockSpec((tm, tn), lambda i, j, k: (i, j)))
        else:
            out_shape.append(jax.ShapeDtypeStruct((8 * ni, N), dt))
            out_specs.append(pl.BlockSpec((8, tn), lambda i, j, k: (i, j)))
    if a_colsum:
        assert mode == "tn" and npair == 1 and nj == 1, name
        out_shape.append(jax.ShapeDtypeStruct((1, M), F32))
        out_specs.append(pl.BlockSpec((1, tm), lambda i, j, k: (0, i)))
    grid = (ni, nj, nk)
    if nj > 1 and nk == 1:
        turned = lambda spec: pl.BlockSpec(spec.block_shape, functools.partial(
            lambda j, i, k, im: im(i, j, k), im=spec.index_map))
        in_specs, out_specs, grid = [turned(s) for s in in_specs], [turned(s) for s in out_specs], (nj, ni, nk)
    res, got = _call(name, body, grid=grid, in_specs=in_specs, out_specs=out_specs, out_shape=out_shape,
                     args=[t for a, b in pairs for t in (a, b.array)] + [e[0] for e in extras] + held,
                     scratch=[pltpu.VMEM((tm, tn), F32)] if nk > 1 else [], aliases=aliases,
                     semantics=("parallel", "parallel", "arbitrary"), comm=None if comm is _NO_COPIES else comm)
    return res if comm is _NO_COPIES else (res, got)


def _ffn_fwd(merged, w_out, x, gain, w_gate_t, w_up_t, *, tm, comm=None):
    (T, D), F = x.shape, w_gate_t.shape[0]

    def body(m_ref, wo_ref, x_ref, g_ref, wg_ref, wu_ref, h_ref, u_ref, gate_ref, up_ref, z_ref):
        h = x_ref[...] + _nn(m_ref[...], wo_ref[...])
        h_ref[...] = h
        u = (h * lax.rsqrt(jnp.mean(h * h, axis=-1, keepdims=True) + EPS) * g_ref[...]).astype(u_ref.dtype)
        u_ref[...] = u
        gate, up = _nt(u, wg_ref[...]), _nt(u, wu_ref[...])
        gate_ref[...], up_ref[...] = gate.astype(gate_ref.dtype), up.astype(up_ref.dtype)
        z_ref[...] = (gate * _sigmoid(gate) * up).astype(z_ref.dtype)

    rows = lambda n: pl.BlockSpec((tm, n), lambda i: (i, 0))
    fixed = lambda a: pl.BlockSpec(a.shape, lambda i: (0, 0))
    return _call("ffn_hidden", body, grid=(T // tm,),
                 in_specs=[rows(D), fixed(w_out), rows(D), fixed(gain), fixed(w_gate_t), fixed(w_up_t)],
                 out_specs=[rows(D), rows(D), rows(F), rows(F), rows(F)],
                 out_shape=[jax.ShapeDtypeStruct((T, D), F32), jax.ShapeDtypeStruct((T, D), MXU_DTYPE)]
                 + [jax.ShapeDtypeStruct((T, F), SAVED_DTYPE)] * 2 + [jax.ShapeDtypeStruct((T, F), MXU_DTYPE)],
                 args=[merged, w_out, x, gain, w_gate_t, w_up_t], semantics=("parallel",), comm=comm)


def _in_proj(x, gain, w_in_t, b_in, *, tm, comm=None):
    T, D = x.shape
    bounds = [sum(IN_SPLITS[:i]) for i in range(len(IN_SPLITS) + 1)]

    def body(x_ref, g_ref, w_ref, b_ref, u_ref, *piece_refs):
        xv = x_ref[...]
        r = lax.rsqrt(jnp.mean(xv * xv, axis=-1, keepdims=True) + EPS)
        u = (xv * r * g_ref[...]).astype(u_ref.dtype)
        u_ref[...] = u
        for o_ref, lo, hi in zip(piece_refs, bounds[:-1], bounds[1:]):
            o_ref[...] = (_nt(u, w_ref[lo:hi, :]) + b_ref[:, lo:hi]).astype(o_ref.dtype)

    rows = lambda n: pl.BlockSpec((tm, n), lambda i: (i, 0))
    fixed = lambda a: pl.BlockSpec(a.shape, lambda i: (0, 0))
    dtypes = (MXU_DTYPE, MXU_DTYPE, F32, F32)
    return _call("in_proj", body, grid=(T // tm,),
                 in_specs=[rows(D), fixed(gain), fixed(w_in_t), fixed(b_in)],
                 out_specs=[rows(D)] + [rows(n) for n in IN_SPLITS],
                 out_shape=[jax.ShapeDtypeStruct((T, D), MXU_DTYPE)]
                 + [jax.ShapeDtypeStruct((T, n), dt) for n, dt in zip(IN_SPLITS, dtypes)],
                 args=[x, gain, w_in_t, b_in], semantics=("parallel",), comm=comm)


def _row_spec(tm, n):
    return pl.BlockSpec((tm, n), lambda i: (i, 0))


def _fixed_spec(a):
    return pl.BlockSpec(a.shape, lambda i: (0,) * a.ndim)


def _partials_spec(n):
    return pl.BlockSpec((8, n), lambda i: (i, 0))


def _ffn_tail(z, w_down, h1, target, gain, gate, up, *, tm):
    (T, F), D = z.shape, h1.shape[1]

    def body(z_ref, w_ref, h_ref, t_ref, g_ref, gate_ref, up_ref, dh_ref, dhb_ref, dgate_ref, dup_ref, dg_ref, l_ref):
        h2 = h_ref[...] + _nn(z_ref[...], w_ref[...])
        r = lax.rsqrt(jnp.mean(h2 * h2, axis=-1, keepdims=True) + EPS)
        xhat = h2 * r
        err = xhat * g_ref[...] - t_ref[...]
        part = 0.5 * jnp.sum(jnp.sum(err * err, axis=-1, keepdims=True), axis=0, keepdims=True) / D
        dy = err / D
        dxh = dy * g_ref[...]
        dh2 = r * (dxh - xhat * jnp.mean(dxh * xhat, axis=-1, keepdims=True))
        dh_ref[...] = dh2
        dhb = dh2.astype(dhb_ref.dtype)
        dhb_ref[...] = dhb
        dg_ref[...] = jnp.broadcast_to(jnp.sum(dy * xhat, axis=0, keepdims=True), dg_ref.shape)
        l_ref[...] = jnp.broadcast_to(part, l_ref.shape)
        dz = _nt(dhb, w_ref[...])
        gv, upv = gate_ref[...].astype(F32), up_ref[...].astype(F32)
        s = _sigmoid(gv)
        dgate_ref[...] = (dz * upv * (s * (1.0 + gv * (1.0 - s)))).astype(dgate_ref.dtype)
        dup_ref[...] = (dz * (gv * s)).astype(dup_ref.dtype)

    low = lambda n: jax.ShapeDtypeStruct((T, n), MXU_DTYPE)
    part = jax.ShapeDtypeStruct((8 * (T // tm), D), F32)
    return pl.pallas_call(
        body, name="ffn_tail", grid=(T // tm,),
        in_specs=[_row_spec(tm, F), _fixed_spec(w_down), _row_spec(tm, D), _row_spec(tm, D), _fixed_spec(gain),
                  _row_spec(tm, F), _row_spec(tm, F)],
        out_specs=[_row_spec(tm, D), _row_spec(tm, D), _row_spec(tm, F), _row_spec(tm, F), _partials_spec(D),
                   _partials_spec(D)],
        out_shape=[jax.ShapeDtypeStruct((T, D), F32), low(D), low(F), low(F), part, part],
        compiler_params=_params(("parallel",)),
    )(z, w_down, h1, target, gain, gate, up)


def _ffn_in_bwd(dgate, dup, w_gate_t, w_up_t, h1, gain, dres, *, tm, comm=None):
    (T, F), D = dgate.shape, h1.shape[1]

    def body(dg_ref, du_ref, wg_ref, wu_ref, h_ref, g_ref, r_ref, dh_ref, dhb_ref, dgain_ref):
        d_u2 = _nn(dg_ref[...], wg_ref[...]) + _nn(du_ref[...], wu_ref[...])
        dx, dgain = _rmsnorm_bwd_vals(d_u2, h_ref[...], g_ref[...])
        dh = r_ref[...] + dx
        dh_ref[...] = dh
        dhb_ref[...] = dh.astype(dhb_ref.dtype)
        dgain_ref[...] = jnp.broadcast_to(dgain, dgain_ref.shape)

    return _call("d_ffn_in", body, grid=(T // tm,),
                 in_specs=[_row_spec(tm, F), _row_spec(tm, F), _fixed_spec(w_gate_t), _fixed_spec(w_up_t),
                           _row_spec(tm, D), _fixed_spec(gain), _row_spec(tm, D)],
                 out_specs=[_row_spec(tm, D), _row_spec(tm, D), _partials_spec(D)],
                 out_shape=[jax.ShapeDtypeStruct((T, D), F32), jax.ShapeDtypeStruct((T, D), MXU_DTYPE),
                            jax.ShapeDtypeStruct((8 * (T // tm), D), F32)],
                 args=[dgate, dup, w_gate_t, w_up_t, h1, gain, dres], semantics=("parallel",), comm=comm)


def _in_proj_bwd(pieces, w_in_t, x, gain, dres, *, tm, comm=None):
    T, D = x.shape
    n = len(pieces)

    def body(*refs):
        dps, (w_ref, x_ref, g_ref, r_ref, dx_ref, dgain_ref) = refs[:n], refs[n:]
        d_u = None
        for dp_ref, (dp, first) in zip(dps, pieces):
            term = _nn(dp_ref[...], w_ref[first:first + dp.shape[1], :])
            d_u = term if d_u is None else d_u + term
        dx, dgain = _rmsnorm_bwd_vals(d_u, x_ref[...], g_ref[...])
        dx_ref[...] = r_ref[...] + dx
        dgain_ref[...] = jnp.broadcast_to(dgain, dgain_ref.shape)

    return _call("d_u", body, grid=(T // tm,),
                 in_specs=[_row_spec(tm, dp.shape[1]) for dp, _ in pieces]
                 + [_fixed_spec(w_in_t), _row_spec(tm, D), _fixed_spec(gain), _row_spec(tm, D)],
                 out_specs=[_row_spec(tm, D), _partials_spec(D)],
                 out_shape=[jax.ShapeDtypeStruct((T, D), F32), jax.ShapeDtypeStruct((8 * (T // tm), D), F32)],
                 args=[dp for dp, _ in pieces] + [w_in_t, x, gain, dres], semantics=("parallel",), comm=comm)


def _merge_fwd(y_a, y_b, w_a, w_b, gates, *, tm):
    T, D = y_a.shape

    def body(ya_ref, yb_ref, wa_ref, wb_ref, ga_ref, gb_ref, pa_ref, pb_ref, m_ref):
        pa, pb = _nn(ya_ref[...], wa_ref[...]), _nn(yb_ref[...], wb_ref[...])
        pa_ref[...], pb_ref[...] = pa.astype(pa_ref.dtype), pb.astype(pb_ref.dtype)
        m_ref[...] = (_sigmoid(ga_ref[...]) * pa + _sigmoid(gb_ref[...]) * pb).astype(m_ref.dtype)

    rows = pl.BlockSpec((tm, D), lambda i: (i, 0))
    whole = pl.BlockSpec((D, D), lambda i: (0, 0))
    return pl.pallas_call(
        body, name="branch_merge", grid=(T // tm,),
        in_specs=[rows, rows, whole, whole, rows, pl.BlockSpec((tm, D), lambda i: (i, 1))], out_specs=[rows] * 3,
        out_shape=[jax.ShapeDtypeStruct((T, D), SAVED_DTYPE)] * 2 + [jax.ShapeDtypeStruct((T, D), MXU_DTYPE)],
        compiler_params=_params(("parallel",)),
    )(y_a, y_b, w_a, w_b, gates, gates)


def _merge_bwd(dh, w_out, w_a, w_b, p_a, p_b, gates, *, tm, d_in_width, first):
    T, D = dh.shape

    def body(dh_ref, wo_ref, wa_ref, wb_ref, pa_ref, pb_ref, ga_ref, gb_ref, dpa_ref, dpb_ref, dya_ref, dyb_ref,
             din_ref):
        dm = _nt(dh_ref[...], wo_ref[...])
        sa, sb = _sigmoid(ga_ref[...]), _sigmoid(gb_ref[...])
        dpa, dpb = (dm * sa).astype(dpa_ref.dtype), (dm * sb).astype(dpb_ref.dtype)
        dpa_ref[...], dpb_ref[...] = dpa, dpb
        din_ref[:, :D] = (dm * pa_ref[...].astype(F32) * sa * (1.0 - sa)).astype(din_ref.dtype)
        din_ref[:, D:] = (dm * pb_ref[...].astype(F32) * sb * (1.0 - sb)).astype(din_ref.dtype)
        dya_ref[...] = _nt(dpa, wa_ref[...]).astype(dya_ref.dtype)
        dyb_ref[...] = _nt(dpb, wb_ref[...]).astype(dyb_ref.dtype)

    rows = pl.BlockSpec((tm, D), lambda i: (i, 0))
    whole = pl.BlockSpec((D, D), lambda i: (0, 0))
    low = jax.ShapeDtypeStruct((T, D), MXU_DTYPE)
    return pl.pallas_call(
        body, name="d_branch_merge", grid=(T // tm,),
        in_specs=[rows, whole, whole, whole, rows, rows, rows, pl.BlockSpec((tm, D), lambda i: (i, 1))],
        out_specs=[rows] * 4 + [pl.BlockSpec((pl.Element(tm), pl.Element(2 * D)), lambda i: (
            pl.multiple_of(i * tm, ROW_ALIGN), first))],
        out_shape=[low] * 3 + [jax.ShapeDtypeStruct((T, D), F32), jax.ShapeDtypeStruct((T, d_in_width), MXU_DTYPE)],
        compiler_params=_params(("parallel",)),
    )(dh, w_out, w_a, w_b, p_a, p_b, gates, gates)


def _colsum_partials(p):
    return jnp.sum(p.reshape(-1, 8, p.shape[-1])[:, 0, :], axis=0, keepdims=True)


def _rmsnorm_bwd_vals(dy, xin, g):
    rstd = lax.rsqrt(jnp.mean(xin * xin, axis=-1, keepdims=True) + EPS)
    xhat = xin * rstd
    dg = jnp.sum(dy * xhat, axis=0, keepdims=True)
    dxh = dy * g
    dx = rstd * (dxh - xhat * jnp.mean(dxh * xhat, axis=-1, keepdims=True))
    return dx, dg


def _colsum(name, a, tr=512, comm=_NO_COPIES):
    T, N = a.shape

    def body(a_ref, o_ref):
        @pl.when(pl.program_id(0) == 0)
        def _():
            o_ref[...] = jnp.zeros_like(o_ref)

        o_ref[...] += jnp.sum(a_ref[...].astype(F32), axis=0, keepdims=True)

    (res,), got = _call(name, body, grid=(T // tr,), in_specs=[pl.BlockSpec((tr, N), lambda i: (i, 0))],
                        out_specs=[pl.BlockSpec((1, N), lambda i: (0, 0))],
                        out_shape=[jax.ShapeDtypeStruct((1, N), F32)], args=[a], semantics=("arbitrary",),
                        comm=None if comm is _NO_COPIES else comm)
    return res if comm is _NO_COPIES else (res, got)


ATTN_SCALE = 1.0 / math.sqrt(HEAD_DIM)
GROUP_LANES = GROUP * ATTN_BLOCK
PAIR = 2 * HEAD_DIM


def _attn_mask():
    kj = lax.broadcasted_iota(jnp.int32, (ATTN_BLOCK, GROUP_LANES), 0)
    qi = lax.broadcasted_iota(jnp.int32, (ATTN_BLOCK, GROUP_LANES), 1) & (ATTN_BLOCK - 1)
    return kj <= qi


def _heads_transposed(ref, g, scale=None):
    parts = []
    for a in range(GROUP // 2):
        lo = (g * GROUP // 2 + a) * PAIR
        pair = ref[:, lo:lo + PAIR].astype(F32)
        pair = (pair if scale is None else pair * scale).T
        parts += [pair[:HEAD_DIM], pair[HEAD_DIM:]]
    return jnp.concatenate(parts, axis=1).astype(MXU_DTYPE)


def _heads_back(ref, g, vt):
    for a in range(GROUP // 2):
        lo = (g * GROUP // 2 + a) * PAIR
        pair = jnp.concatenate([vt[:, (2 * a) * ATTN_BLOCK:(2 * a + 1) * ATTN_BLOCK],
                                vt[:, (2 * a + 1) * ATTN_BLOCK:(2 * a + 2) * ATTN_BLOCK]], axis=0)
        ref[:, lo:lo + PAIR] = pair.T.astype(ref.dtype)


def _kv_parts(kv_ref, g):
    ks = slice(g * HEAD_DIM, (g + 1) * HEAD_DIM)
    vs = slice(KV_WIDTH + g * HEAD_DIM, KV_WIDTH + (g + 1) * HEAD_DIM)
    return kv_ref[:, ks].astype(MXU_DTYPE), kv_ref[:, vs].astype(MXU_DTYPE)


def _sink_rows(sinks):
    return jnp.repeat(sinks.reshape(KV_HEADS, GROUP), ATTN_BLOCK, axis=1)


def _attn_fwd(pq, pkv, sinks, comm=None):
    T = pq.shape[0]
    nb = T // ATTN_BLOCK

    def body(q_ref, kvc_ref, kvp_ref, s_ref, y_ref, lse_ref):
        mask_c = _attn_mask()
        has_prev = pl.program_id(0) > 0
        for g in range(KV_HEADS):
            (kc, vc), (kp, vp) = _kv_parts(kvc_ref, g), _kv_parts(kvp_ref, g)
            qt = _heads_transposed(q_ref, g, ATTN_SCALE)
            s = jnp.where(mask_c, _nn(kc, qt), jnp.where(has_prev, _nn(kp, qt), NEG_INF))
            sink = s_ref[g:g + 1, :]
            m = jnp.maximum(jnp.max(s, axis=0, keepdims=True), sink)
            p = jnp.exp(s - m)
            den = jnp.sum(p, axis=0, keepdims=True) + jnp.exp(sink - m)
            pc = jnp.where(mask_c, p, 0.0)
            _heads_back(y_ref, g, (_tn(vc, pc) + _tn(vp, p - pc)) / den)
            lse = m + jnp.log(den)
            for i in range(GROUP):
                lse_ref[g * GROUP + i:g * GROUP + i + 1, :] = lse[:, i * ATTN_BLOCK:(i + 1) * ATTN_BLOCK]

    return _call(
        "attn_fwd", body, grid=(nb,),
        in_specs=[pl.BlockSpec((ATTN_BLOCK, D_MODEL), lambda n: (n, 0)),
                  pl.BlockSpec((ATTN_BLOCK, 2 * KV_WIDTH), lambda n: (n, 0)),
                  pl.BlockSpec((ATTN_BLOCK, 2 * KV_WIDTH), lambda n: (jnp.maximum(n - 1, 0), 0)),
                  pl.BlockSpec((KV_HEADS, GROUP_LANES), lambda n: (0, 0))],
        out_specs=[pl.BlockSpec((ATTN_BLOCK, D_MODEL), lambda n: (n, 0)),
                   pl.BlockSpec((Q_HEADS, ATTN_BLOCK), lambda n: (0, n))],
        out_shape=[jax.ShapeDtypeStruct((T, D_MODEL), MXU_DTYPE), jax.ShapeDtypeStruct((Q_HEADS, T), F32)],
        args=[pq, pkv, pkv, _sink_rows(sinks)], semantics=("parallel",), comm=comm)


def _attn_bwd(pq, pkv, sinks, lse, dy, d_in, comm=None):
    T = pq.shape[0]
    nb = T // ATTN_BLOCK
    cur = lambda n: (jnp.minimum(n, nb - 1), 0)
    done = D_MODEL + 2 * KV_WIDTH

    def body(q_ref, kvc_ref, kvp_ref, s_ref, lse_ref, dy_ref, _, out_ref, ds_ref, carry, top, bot, dq_ref):
        n = pl.program_id(0)

        @pl.when(n == 0)
        def _():
            carry[...] = jnp.zeros_like(carry)
            dq_ref[...] = jnp.zeros_like(dq_ref)
            ds_ref[...] = jnp.zeros_like(ds_ref)

        out_ref[:, :D_MODEL] = dq_ref[...]

        @pl.when(n < nb)
        def _():
            mask_c = _attn_mask()
            valid = jnp.logical_or(mask_c, n > 0)
            for g in range(KV_HEADS):
                ks = slice(g * HEAD_DIM, (g + 1) * HEAD_DIM)
                vs = slice(KV_WIDTH + g * HEAD_DIM, KV_WIDTH + (g + 1) * HEAD_DIM)
                (kc, vc), (kp, vp) = _kv_parts(kvc_ref, g), _kv_parts(kvp_ref, g)
                qt = _heads_transposed(q_ref, g, ATTN_SCALE)
                dot = _heads_transposed(dy_ref, g)
                lse = jnp.concatenate([lse_ref[g * GROUP + i:g * GROUP + i + 1, :] for i in range(GROUP)], axis=1)
                p = jnp.where(valid, jnp.exp(jnp.where(mask_c, _nn(kc, qt), _nn(kp, qt)) - lse), 0.0)
                dp = jnp.where(mask_c, _nn(vc, dot), _nn(vp, dot))
                delta = jnp.sum(p * dp, axis=0, keepdims=True)
                ds = p * (dp - delta)
                ds_c, p_c = jnp.where(mask_c, ds, 0.0), jnp.where(mask_c, p, 0.0)
                ds_p, p_p = ds - ds_c, p - p_c
                _heads_back(dq_ref, g, (_tn(kc, ds_c) + _tn(kp, ds_p)) * ATTN_SCALE)
                bot[:, ks], bot[:, vs] = _nt(ds_c, qt), _nt(p_c, dot)
                top[:, ks], top[:, vs] = _nt(ds_p, qt), _nt(p_p, dot)
                ds_ref[g:g + 1, :] -= jnp.exp(s_ref[g:g + 1, :] - lse) * delta
            out_ref[:, D_MODEL:] = (carry[...] + top[...]).astype(out_ref.dtype)
            carry[...] = bot[...]

        @pl.when(n == nb)
        def _():
            out_ref[:, D_MODEL:] = carry[...].astype(out_ref.dtype)

    return _call(
        "attn_bwd", body, grid=(nb + 1,),
        in_specs=[pl.BlockSpec((ATTN_BLOCK, D_MODEL), cur),
                  pl.BlockSpec((ATTN_BLOCK, 2 * KV_WIDTH), cur),
                  pl.BlockSpec((ATTN_BLOCK, 2 * KV_WIDTH), lambda n: (jnp.maximum(jnp.minimum(n, nb - 1) - 1, 0), 0)),
                  pl.BlockSpec((KV_HEADS, GROUP_LANES), lambda n: (0, 0)),
                  pl.BlockSpec((Q_HEADS, ATTN_BLOCK), lambda n: (0, jnp.minimum(n, nb - 1))),
                  pl.BlockSpec((ATTN_BLOCK, D_MODEL), cur), HBM_SPEC],
        out_specs=[pl.BlockSpec((ATTN_BLOCK, done), lambda n: (jnp.maximum(n - 1, 0), 0)),
                   pl.BlockSpec((KV_HEADS, GROUP_LANES), lambda n: (0, 0))],
        out_shape=[jax.ShapeDtypeStruct(d_in.shape, d_in.dtype), jax.ShapeDtypeStruct((KV_HEADS, GROUP_LANES), F32)],
        scratch=[pltpu.VMEM((ATTN_BLOCK, 2 * KV_WIDTH), F32)] * 3 + [pltpu.VMEM((ATTN_BLOCK, D_MODEL), MXU_DTYPE)],
        args=[pq, pkv, pkv, _sink_rows(sinks), lse, dy, d_in], semantics=("arbitrary",), comm=comm, aliases={6: 0})


def _lower_bound(l):
    m = jnp.maximum(l[0:1], l[1:2])
    e0, e1 = jnp.exp(l[0:1] - m), jnp.exp(l[1:2] - m)
    return e0 / (e0 + e1)


def _tri(lower):
    r = lax.broadcasted_iota(jnp.int32, (CHUNK, CHUNK), 0)
    c = lax.broadcasted_iota(jnp.int32, (CHUNK, CHUNK), 1)
    return (r >= c) if lower else (c >= r)


def _chunk_sum(mask, v):
    ones = mask.astype(BF16)
    hi = v.astype(BF16)
    rest = v - hi.astype(F32)
    mid = rest.astype(BF16)
    lo = (rest - mid.astype(F32)).astype(BF16)
    part = lambda t: lax.dot_general(ones, t, (((1,), (0,)), ((), ())), preferred_element_type=F32)
    return part(hi) + part(mid) + part(lo)


def _hgrn_chunk_inputs(hq, hf, lb, causal):
    half_t = 0.5 * jnp.tanh(0.5 * hf)
    sg, sgn = 0.5 + half_t, 0.5 - half_t
    f = lb + (1.0 - lb) * sg
    kk = (1.0 - lb) * sgn
    sq = _sigmoid(hq)
    q = hq * sq
    b = _chunk_sum(causal, jnp.log(f))
    bm, bl = b[CHUNK // 2 - 1:CHUNK // 2, :], b[CHUNK - 1:CHUNK, :]
    e_qm, e_km = jnp.exp(b - bm), jnp.exp(bm - b)
    e_qs, e_kl = e_qm * jnp.exp(bm), e_km * jnp.exp(bl - bm)
    return dict(sg=sg, sgn=sgn, f=f, kk=kk, sq=sq, q=q, e_qm=e_qm, e_km=e_km, e_qs=e_qs, e_kl=e_kl,
                qm=q * e_qm, km=kk * e_km, qs=q * e_qs, kl=kk * e_kl, el=jnp.exp(bl))


def _hgrn_fwd(ph, lb_logits, norm_g, comm=None):
    T = ph.shape[0]
    nblk, cpb = T // HGRN_TOKENS, HGRN_TOKENS // CHUNK
    col = lambda c: pl.BlockSpec((HGRN_TOKENS, D_MODEL), functools.partial(lambda i, c: (i, c), c=c))

    def body(hq_ref, hf_ref, hi_ref, hg_ref, l_ref, ng_ref, y_ref, o_ref, st_ref, s_ref):
        @pl.when(pl.program_id(0) == 0)
        def _():
            s_ref[...] = jnp.zeros_like(s_ref)

        lb = _lower_bound(l_ref[...])
        causal = _tri(True)
        for c in range(cpb):
            rows = slice(c * CHUNK, (c + 1) * CHUNK)
            t = _hgrn_chunk_inputs(hq_ref[rows, :], hf_ref[rows, :], lb, causal)
            qm, km, qs, kl = (t[n].astype(MXU_DTYPE) for n in ("qm", "km", "qs", "kl"))
            v = hi_ref[rows, :].astype(MXU_DTYPE)
            for h in range(HGRN_HEADS):
                ls = slice(h * HGRN_K, (h + 1) * HGRN_K)
                st = s_ref[h]
                st_ref[c, ls, :] = st
                a = jnp.where(causal, _nt(qm[:, ls], km[:, ls]), 0.0)
                o_ref[rows, ls] = _nn(a, v[:, ls]) + _nt(qs[:, ls], st)
                s_ref[h] = t["el"][:, ls] * st + _tn(v[:, ls], kl[:, ls])
        for h in range(HGRN_HEADS):
            ls = slice(h * HGRN_K, (h + 1) * HGRN_K)
            o = o_ref[:, ls]
            r = lax.rsqrt(jnp.mean(o * o, axis=-1, keepdims=True) + EPS)
            y_ref[:, ls] = (o * r * ng_ref[:, ls] * _sigmoid(hg_ref[:, ls])).astype(y_ref.dtype)

    return _call(
        "hgrn_fwd", body, grid=(nblk,),
        in_specs=[col(0), col(1), col(2), col(3),
                  pl.BlockSpec((2, D_MODEL), lambda i: (0, 0)), pl.BlockSpec((1, D_MODEL), lambda i: (0, 0))],
        out_specs=[pl.BlockSpec((HGRN_TOKENS, D_MODEL), lambda i: (i, 0)),
                   pl.BlockSpec((HGRN_TOKENS, D_MODEL), lambda i: (i, 0)),
                   pl.BlockSpec((cpb, D_MODEL, HGRN_K), lambda i: (i, 0, 0))],
        out_shape=[jax.ShapeDtypeStruct((T, D_MODEL), MXU_DTYPE), jax.ShapeDtypeStruct((T, D_MODEL), F32),
                   jax.ShapeDtypeStruct((T // CHUNK, D_MODEL, HGRN_K), F32)],
        scratch=[pltpu.VMEM((HGRN_HEADS, HGRN_K, HGRN_K), F32)],
        args=[ph, ph, ph, ph, lb_logits, norm_g], semantics=("arbitrary",), comm=comm)


def _hgrn_bwd(ph, o_raw, states, dy, lb_logits, norm_g, d_in, first, comm=None):
    T = ph.shape[0]
    nblk, cpb = T // HGRN_TOKENS, HGRN_TOKENS // CHUNK
    rev = lambda i: nblk - 1 - i
    col = lambda c: pl.BlockSpec((HGRN_TOKENS, D_MODEL), functools.partial(lambda i, c: (rev(i), c), c=c))
    tok = pl.BlockSpec((HGRN_TOKENS, D_MODEL), lambda i: (rev(i), 0))

    def body(hq_ref, hf_ref, hi_ref, hg_ref, o_ref, st_ref, dy_ref, l_ref, ng_ref, _,
             dph_ref, dng_ref, dl_ref, dst_ref, dlb_ref, do_s, dqm_s, dkm_s, dqs_s, dkl_s, dv_s, del_s):
        i = pl.program_id(0)

        @pl.when(i == 0)
        def _():
            dst_ref[...] = jnp.zeros_like(dst_ref)
            dlb_ref[...] = jnp.zeros_like(dlb_ref)
            dng_ref[...] = jnp.zeros_like(dng_ref)

        lb = _lower_bound(l_ref[...])
        causal, anti = _tri(True), _tri(False)
        row = lax.broadcasted_iota(jnp.int32, (CHUNK, D_MODEL), 0)
        for c in reversed(range(cpb)):
            rows = slice(c * CHUNK, (c + 1) * CHUNK)
            hq = hq_ref[rows, :]
            t = _hgrn_chunk_inputs(hq, hf_ref[rows, :], lb, causal)
            sgg = _sigmoid(hg_ref[rows, :])
            dyv = dy_ref[rows, :]
            for h in range(HGRN_HEADS):
                ls = slice(h * HGRN_K, (h + 1) * HGRN_K)
                o = o_ref[rows, ls]
                r = lax.rsqrt(jnp.mean(o * o, axis=-1, keepdims=True) + EPS)
                nrm = o * r
                g_h = sgg[:, ls]
                dph_ref[rows, 3 * D_MODEL + h * HGRN_K:3 * D_MODEL + (h + 1) * HGRN_K] = (
                    dyv[:, ls] * nrm * ng_ref[:, ls] * g_h * (1.0 - g_h)).astype(dph_ref.dtype)
                dyg = dyv[:, ls] * g_h
                dng_ref[:, ls] += jnp.sum(dyg * nrm, axis=0, keepdims=True)
                dn = dyg * ng_ref[:, ls]
                do_s[:, ls] = r * (dn - nrm * jnp.mean(dn * nrm, axis=-1, keepdims=True))
            qm, km, qs, kl = (t[n].astype(MXU_DTYPE) for n in ("qm", "km", "qs", "kl"))
            v = hi_ref[rows, :].astype(MXU_DTYPE)
            do = do_s[...].astype(MXU_DTYPE)
            for h in range(HGRN_HEADS):
                ls = slice(h * HGRN_K, (h + 1) * HGRN_K)
                st = st_ref[c, ls, :]
                dst = dst_ref[h]
                a = jnp.where(causal, _nt(qm[:, ls], km[:, ls]), 0.0)
                da = jnp.where(causal, _nt(do[:, ls], v[:, ls]), 0.0)
                dv_s[:, ls] = _tn(a, do[:, ls]) + _nt(kl[:, ls], dst)
                dkl_s[:, ls] = _nn(v[:, ls], dst)
                dqs_s[:, ls] = _nn(do[:, ls], st)
                del_s[:, ls] = jnp.sum(dst * st, axis=0, keepdims=True)
                dst_ref[h] = _tn(do[:, ls], qs[:, ls]) + t["el"][:, ls] * dst
                dqm_s[:, ls] = _nn(da, km[:, ls])
                dkm_s[:, ls] = _tn(da, qm[:, ls])
            dqm, dkm, dqs, dkl = dqm_s[...], dkm_s[...], dqs_s[...], dkl_s[...]
            dq = dqm * t["e_qm"] + dqs * t["e_qs"]
            dk = dkm * t["e_km"] + dkl * t["e_kl"]
            t_qm, t_km, t_kl = dqm * t["qm"], dkm * t["km"], dkl * t["kl"]
            db = t_qm - t_km + dqs * t["qs"] - t_kl
            db_mid = jnp.sum(t_km - t_qm, axis=0, keepdims=True)
            db_last = jnp.sum(t_kl, axis=0, keepdims=True) + del_s[...] * t["el"]
            db = db + jnp.where(row == CHUNK // 2 - 1, db_mid, 0.0) + jnp.where(row == CHUNK - 1, db_last, 0.0)
            dlogf = _chunk_sum(anti, db)
            sq, sg, sgn, f = t["sq"], t["sg"], t["sgn"], t["f"]
            dph_ref[rows, 0:D_MODEL] = (dq * (sq * (1.0 + hq * (1.0 - sq)))).astype(dph_ref.dtype)
            dph_ref[rows, D_MODEL:2 * D_MODEL] = (
                dlogf * (1.0 - lb) * sg * (1.0 - sg) / f - dk * (1.0 - lb) * sgn * (1.0 - sgn)).astype(dph_ref.dtype)
            dph_ref[rows, 2 * D_MODEL:3 * D_MODEL] = dv_s[...].astype(dph_ref.dtype)
            dlb_ref[...] += jnp.sum(dlogf * (1.0 - sg) / f - dk * sgn, axis=0, keepdims=True)

        @pl.when(i == nblk - 1)
        def _():
            dl0 = dlb_ref[...] * lb * (1.0 - lb)
            dl_ref[0:1, :] = dl0
            dl_ref[1:2, :] = -dl0

    wide = pltpu.VMEM((CHUNK, D_MODEL), F32)
    return _call(
        "hgrn_bwd", body, grid=(nblk,),
        in_specs=[col(0), col(1), col(2), col(3), tok,
                  pl.BlockSpec((cpb, D_MODEL, HGRN_K), lambda i: (rev(i), 0, 0)), tok,
                  pl.BlockSpec((2, D_MODEL), lambda i: (0, 0)), pl.BlockSpec((1, D_MODEL), lambda i: (0, 0)), HBM_SPEC],
        out_specs=[pl.BlockSpec((pl.Element(HGRN_TOKENS), pl.Element(4 * D_MODEL)), lambda i: (
                       pl.multiple_of(rev(i) * HGRN_TOKENS, ROW_ALIGN), first)),
                   pl.BlockSpec((1, D_MODEL), lambda i: (0, 0)), pl.BlockSpec((2, D_MODEL), lambda i: (0, 0))],
        out_shape=[jax.ShapeDtypeStruct(d_in.shape, d_in.dtype), jax.ShapeDtypeStruct((1, D_MODEL), F32),
                   jax.ShapeDtypeStruct((2, D_MODEL), F32)],
        scratch=[pltpu.VMEM((HGRN_HEADS, HGRN_K, HGRN_K), F32), pltpu.VMEM((1, D_MODEL), F32),
                 wide, wide, wide, wide, wide, wide, pltpu.VMEM((1, D_MODEL), F32)],
        args=[ph, ph, ph, ph, o_raw, states, dy, lb_logits, norm_g, d_in], semantics=("arbitrary",), comm=comm,
        aliases={9: 0})


def _local_step(x, target, vec, net):
    T, D = x.shape
    norm_mix_g, b_in, sinks, lb_logits = vec["norm_mix_g"], vec["b_in"], vec["attn_sinks"], vec["hgrn_lb_logits"]
    hgrn_norm_g, norm_ffn_g, norm_final_g = vec["hgrn_norm_g"], vec["norm_ffn_g"], vec["norm_final_g"]
    w_in = net.full("w_in")
    o_q, o_kv, o_h, o_g = (sum(IN_SPLITS[:i]) for i in range(4))
    w_q, w_kv, w_h, w_g = (_Rows(w_in, o, n) for o, n in zip((o_q, o_kv, o_h, o_g), IN_SPLITS))
    both = lambda acc: (acc, acc)
    grad_outs = [("mn", F32), ("mn", MXU_DTYPE)]
    row_vec = lambda n: ((1, n), lambda i, j: (0, j))
    tile = lambda tm, tn: ((tm, tn), lambda i, j: (i, j))
    TM = 512
    BIG = min(T, 1024)

    names = ("w_branch_attn", "w_branch_hgrn", "w_out")
    (u, pq, pkv, ph, pg), got = _in_proj(x, norm_mix_g, w_in, b_in, tm=256, comm=net.gather(names))
    net.gathered(names, got)
    names = ("w_ffn_gate",)
    (y_attn, lse), got = _attn_fwd(pq, pkv, sinks, comm=net.gather(names))
    net.gathered(names, got)
    names = ("w_ffn_up",)
    (y_hgrn, o_raw, states), got = _hgrn_fwd(ph, lb_logits, hgrn_norm_g, comm=net.gather(names))
    net.gathered(names, got)
    w_ba, w_bh, w_out = net.full("w_branch_attn"), net.full("w_branch_hgrn"), net.full("w_out")
    w_gate, w_up = net.full("w_ffn_gate"), net.full("w_ffn_up")
    ya, yb, merged = _merge_fwd(y_attn, y_hgrn, w_ba, w_bh, pg, tm=TM)

    FT = FFN // 2
    names = ("w_ffn_down",)
    (h1, u2, gpre, up, z), got = _ffn_fwd(merged, w_out, x, norm_ffn_g, w_gate, w_up, tm=256,
                                          comm=net.gather(names))
    net.gathered(names, got)
    w_down = net.full("w_ffn_down")

    dh2, dh2b, dgp, dup, dgf_p, loss_p = _ffn_tail(z, w_down, h1, target, norm_final_g, gpre, up, tm=256)
    loss = jnp.sum(loss_p.reshape(-1, 8, D)[:, 0, 0])
    d_norm_final = _colsum_partials(dgf_p)
    d_w_down = _matmul("dw_down", [(z, dh2b)], "tn", tm=FT, tn=1024, tk=T, outs=grad_outs, epilogue=both)

    names = ("w_ffn_down",)
    (dh1, dh1b, dg2_p), got = _ffn_in_bwd(dgp, dup, w_gate, w_up, h1, norm_ffn_g, dh2, tm=256,
        comm=net.exchange(dict(w_ffn_down=[d_w_down])))
    net.received(names, (), got)
    d_norm_ffn = _colsum_partials(dg2_p)
    d_w_gate = _matmul("dw_gate", [(dgp, u2)], "tn", tm=FT, tn=1024, tk=T, outs=grad_outs, epilogue=both)
    d_w_up = _matmul("dw_up", [(dup, u2)], "tn", tm=FT, tn=1024, tk=T, outs=grad_outs, epilogue=both)

    rows_in = sum(IN_SPLITS)
    dya, dyb, dy_attn, dy_hgrn, d_in = _merge_bwd(dh1b, w_out, w_ba, w_bh, ya, yb, pg, tm=TM, d_in_width=rows_in,
                                                  first=o_g)
    tn_grad = functools.partial(_matmul, mode="tn", tn=1024, tk=1024, outs=grad_outs, epilogue=both)
    d_w_out = tn_grad("dw_out", [(merged, dh1b)], tm=1024)
    d_w_ba = tn_grad("dw_branch_a", [(y_attn, dya)], tm=1024)
    d_w_bh = tn_grad("dw_branch_b", [(y_hgrn, dyb)], tm=1024)

    names, swap = ("w_ffn_gate",), ("w_ffn_down",)
    (d_in, dsink), got = _attn_bwd(pq, pkv, sinks, lse, dy_attn, d_in,
                                   comm=net.exchange(dict(w_ffn_gate=[d_w_gate]), swap))
    net.received(names, swap, got)
    names, swap = ("w_ffn_up", "w_out", "w_branch_attn", "w_branch_hgrn"), ("w_ffn_gate",)
    (d_in, d_hgrn_norm, d_lb_logits), got = _hgrn_bwd(
        ph, o_raw, states, dy_hgrn, lb_logits, hgrn_norm_g, d_in, o_h,
        comm=net.exchange(dict(w_ffn_up=[d_w_up], w_out=[d_w_out], w_branch_attn=[d_w_ba],
                               w_branch_hgrn=[d_w_bh]), swap))
    net.received(names, swap, got)

    main = rows_in // 1024 * 1024
    *d_w_in, db_main = tn_grad("dw_in", [(_Cols(d_in, 0, main), u)], tm=1024, tk=T, a_colsum=True,
                               into=_Into(rows_in, 0, None))
    *d_w_in, db_rest = tn_grad("dw_in_rest", [(_Cols(d_in, main, rows_in - main), u)], tm=rows_in - main,
                               a_colsum=True, into=_Into(rows_in, main, d_w_in))
    d_w_in = [tuple(d_w_in)]

    first_level = net.presum_begin("w_in", d_w_in)
    halves = net.presum_end("w_in", [] if first_level is None else _copies_alone("presum_swap_w_in", first_level))
    names, swap = ("w_in",), ("w_ffn_up", "w_out", "w_branch_attn", "w_branch_hgrn")
    (dx, dg1_p), got = _in_proj_bwd([(d_in, 0)], w_in, x, norm_mix_g, dh1, tm=256,
                                    comm=_join(halves, net.swap(swap)))
    net.last = (names, swap, got)
    d_norm_mix = _colsum_partials(dg1_p)
    d_b_in = jnp.concatenate([db_main, db_rest], axis=1)
    vecs = dict(norm_mix_g=d_norm_mix, b_in=d_b_in, attn_sinks=jnp.sum(dsink.reshape(Q_HEADS, ATTN_BLOCK), axis=1).reshape(1, Q_HEADS),
                hgrn_lb_logits=d_lb_logits,
                hgrn_norm_g=d_hgrn_norm, norm_ffn_g=d_norm_ffn, norm_final_g=d_norm_final)
    return loss, dx, vecs


def _place():
    return lax.axis_index("x"), lax.axis_index("y"), lax.axis_index("c")


def _other_chips(x, y):
    return [(1 - x, y), (x, 1 - y), (1 - x, 1 - y)]


def _y_first(copies):
    return [copies[3 * (i // 3) + (1, 0, 2)[i % 3]] for i in range(len(copies))]


def _gather_copies(shards):
    n = len(shards)

    def build(ins, outs, send_sems, recv_sems, local_sems):
        x, y, c = _place()
        mine = 2 * x + y
        local = [pltpu.make_async_copy(ins[w], outs[w].at[mine], local_sems.at[w]) for w in range(n)]
        sends, recvs = [], []
        for w in range(n):
            for k, (px, py) in enumerate(_other_chips(x, y)):
                sem = 3 * w + k
                sends.append(pltpu.make_async_remote_copy(
                    src_ref=ins[w], dst_ref=outs[w].at[mine], send_sem=send_sems.at[sem], recv_sem=recv_sems.at[sem],
                    device_id=(px, py, c), device_id_type=MESH_ID))
                recvs.append(pltpu.make_async_remote_copy(
                    src_ref=ins[w], dst_ref=outs[w].at[2 * px + py], send_sem=send_sems.at[sem],
                    recv_sem=recv_sems.at[sem], device_id=(px, py, c), device_id_type=MESH_ID))
        return sends, recvs, local, _y_first(sends)

    return _Carried(shards, [jax.ShapeDtypeStruct((N_CHIPS,) + s.shape, s.dtype) for s in shards], 3 * n, n, build)


def _grad_copies(stacked):
    n = len(stacked)

    def build(ins, outs, send_sems, recv_sems, local_sems):
        x, y, c = _place()
        sends = []
        for w in range(n):
            for k, (px, py) in enumerate(_other_chips(x, y)):
                sem = 3 * w + k
                sends.append(pltpu.make_async_remote_copy(
                    src_ref=ins[w].at[2 * px + py], dst_ref=outs[w].at[k], send_sem=send_sems.at[sem],
                    recv_sem=recv_sems.at[sem], device_id=(px, py, c), device_id_type=MESH_ID))
        return sends, sends, [], _y_first(sends)

    return _Carried(stacked, [jax.ShapeDtypeStruct((3,) + s.shape[1:], s.dtype) for s in stacked], 3 * n, 0, build)


def _small_copies(small):
    def build(ins, outs, send_sems, recv_sems, local_sems):
        small_ref, all_ref = ins[0], outs[0]
        x, y, c = _place()
        me = 4 * x + 2 * y + c
        sends, recvs = [], []
        for r in range(1, 8):
            px = 1 - x if r & 4 else x
            py = 1 - y if r & 2 else y
            pc = 1 - c if r & 1 else c
            sends.append(pltpu.make_async_remote_copy(
                src_ref=small_ref, dst_ref=all_ref.at[me], send_sem=send_sems.at[r - 1], recv_sem=recv_sems.at[r - 1],
                device_id=(px, py, pc), device_id_type=MESH_ID))
            recvs.append(pltpu.make_async_remote_copy(
                src_ref=small_ref, dst_ref=all_ref.at[4 * px + 2 * py + pc], send_sem=send_sems.at[r - 1],
                recv_sem=recv_sems.at[r - 1], device_id=(px, py, pc), device_id_type=MESH_ID))
        return sends, recvs, [pltpu.make_async_copy(small_ref, all_ref.at[me], local_sems.at[0])]

    return _Carried([small], [jax.ShapeDtypeStruct((8,) + small.shape, small.dtype)], 7, 1, build)


def _gather_by_neighbours(name, shard):
    half = shard.shape[0] // 2
    quarter = half // 2

    def body(in_ref, out_ref, send_sems, recv_sems, local_sem):
        for core in (0, 1):
            @pl.when(lax.axis_index("c") == core)
            def _():
                program(core, in_ref, out_ref, send_sems, recv_sems, local_sem)

    def program(c, in_ref, out_ref, send_sems, recv_sems, local_sem):
        x, y, _ = _place()
        chip = lambda px, py: 2 * px + py
        to_x, to_y, sibling = (1 - x, y, c), (x, 1 - y, c), (x, y, 1 - c)
        x_blk, y_blk, d_blk = chip(1 - x, y), chip(x, 1 - y), chip(1 - x, 1 - y)
        mine, theirs = c * half, (1 - c) * half

        def copy(sem, rows, block, to, src=None):
            place = out_ref.at[block, pl.ds(rows[0], rows[1])]
            return pltpu.make_async_remote_copy(
                src_ref=place if src is None else src, dst_ref=place, send_sem=send_sems.at[sem],
                recv_sem=recv_sems.at[sem], device_id=to, device_id_type=MESH_ID)

        own = pltpu.make_async_copy(in_ref, out_ref.at[chip(x, y)], local_sem)
        own.start()
        my_rows = in_ref.at[pl.ds(mine, half)]
        along_x = dict(send=copy(0, (mine, half), chip(x, y), to_x, src=my_rows),
                       landed=copy(0, (mine, half), x_blk, to_x),
                       onward=[copy(3, (mine + quarter, quarter), x_blk, to_y), copy(4, (mine, half), x_blk, sibling)],
                       diagonal=copy(2, (mine, quarter), d_blk, to_x))
        along_y = dict(send=copy(1, (mine, half), chip(x, y), to_y, src=my_rows),
                       landed=copy(1, (mine, half), y_blk, to_y),
                       onward=[copy(2, (mine, quarter), y_blk, to_x), copy(5, (mine, half), y_blk, sibling)],
                       diagonal=copy(3, (mine + quarter, quarter), d_blk, to_y))
        last = copy(6, (mine, half), d_blk, sibling)

        order = (along_x, along_y) if c == 0 else (along_y, along_x)
        for axis in order:
            axis["send"].start()
        for axis in order:
            axis["landed"].wait_recv()
            for cp in axis["onward"]:
                cp.start()
        for axis in order:
            axis["diagonal"].wait_recv()
        last.start()
        for sem, block in ((4, x_blk), (5, y_blk), (6, d_blk)):
            copy(sem, (theirs, half), block, sibling).wait_recv()
        for cp in [along_x["send"], along_y["send"]] + along_x["onward"] + along_y["onward"] + [last]:
            cp.wait_send()
        own.wait()

    return pl.pallas_call(
        body, name=name, in_specs=[HBM_SPEC], out_specs=HBM_SPEC,
        out_shape=jax.ShapeDtypeStruct((N_CHIPS,) + shard.shape, shard.dtype),
        scratch_shapes=[pltpu.SemaphoreType.DMA((7,)), pltpu.SemaphoreType.DMA((7,)), pltpu.SemaphoreType.DMA(())],
    )(shard)


def _copies_alone(name, comm):
    return _call(name, lambda: None, grid=(), in_specs=[], out_specs=[], out_shape=[], args=[], comm=comm)[1]


class _Net:
    def __init__(self, shards):
        self.shards = shards
        self.whole, self.own, self.theirs, self.sums, self.other = {}, {}, {}, {}, {}
        x, y, _ = _place()
        self.chip = 2 * x + y

    def gather(self, names):
        return _gather_copies([self.shards[n] for n in names])

    def gathered(self, names, got):
        for n, g in zip(names, got):
            self.whole[n] = g.reshape(-1, g.shape[-1])

    def full(self, name):
        return self.whole[name]

    def exchange(self, grads, swap=()):
        stacked = []
        for n, pieces in grads.items():
            (keep, send), = pieces
            self.own[n] = keep
            stacked.append(send.reshape(N_CHIPS, keep.shape[0] // N_CHIPS, send.shape[-1]))
        return _join(_grad_copies(stacked), self.swap(swap))

    def swap(self, names):
        return _sibling_copies([self.sums[n] for n in names]) if names else None

    def presum_begin(self, name, pieces):
        keep = jnp.concatenate([p[0] for p in pieces], axis=0) if len(pieces) > 1 else pieces[0][0]
        send = jnp.concatenate([p[1] for p in pieces], axis=0) if len(pieces) > 1 else pieces[0][1]
        rows = keep.shape[0] // N_CHIPS
        self.held = keep.reshape(N_CHIPS, rows, keep.shape[-1])
        return _half_rows_copies(send.reshape(N_CHIPS, rows, send.shape[-1]))

    def presum_end(self, name, got):
        x, y, c = _place()
        to_send, self.own[name] = _pre_sum("presum_" + name, self.held, got[0], jnp.stack([c, self.chip]))
        return _grad_copies([to_send])

    def received(self, names, swap, got, carried=None):
        self.theirs.update(zip(names, got[:len(names)]))
        self.other.update(zip(swap, got[len(names):]))
        for n in names:
            (self.sums[n],), more = _partial_sum("sum_" + n, self.own[n], self.theirs[n], self.chip, comm=carried)
        return more


def _half_rows_copies(stacked):
    n, rows = stacked.shape[0], stacked.shape[1] // 2

    def build(ins, outs, send_sems, recv_sems, local_sems):
        x, y, c = _place()
        copies = [pltpu.make_async_remote_copy(
            src_ref=ins[0].at[s, pl.ds((1 - c) * rows, rows)], dst_ref=outs[0].at[s], send_sem=send_sems.at[s],
            recv_sem=recv_sems.at[s], device_id=(x, y, 1 - c), device_id_type=MESH_ID) for s in range(n)]
        return copies, copies, []

    return _Carried([stacked], [jax.ShapeDtypeStruct((n, rows, stacked.shape[2]), stacked.dtype)], n, 0, build)


def _pre_sum(name, held, theirs, core_and_chip):
    n, R, C = held.shape
    half = R // 2
    tr = _row_tile(half)
    per_half = half // tr

    def body(place_ref, h_ref, t_ref, send_ref, own_ref):
        total = h_ref[0] + t_ref[0].astype(F32)
        send_ref[0] = total.astype(send_ref.dtype)

        @pl.when(pl.program_id(1) == place_ref[1])
        def _():
            own_ref[...] = total

    return pl.pallas_call(
        body, name=name,
        grid_spec=pltpu.PrefetchScalarGridSpec(
            num_scalar_prefetch=1, grid=(per_half, n),
            in_specs=[pl.BlockSpec((1, tr, C), lambda i, s, place: (s, place[0] * per_half + i, 0)),
                      pl.BlockSpec((1, tr, C), lambda i, s, place: (s, i, 0))],
            out_specs=[pl.BlockSpec((1, tr, C), lambda i, s, place: (s, i, 0)),
                       pl.BlockSpec((tr, C), lambda i, s, place: (i, 0))]),
        out_shape=[jax.ShapeDtypeStruct((n, half, C), MXU_DTYPE), jax.ShapeDtypeStruct((half, C), F32)],
        compiler_params=_params(("arbitrary", "arbitrary")),
    )(core_and_chip, held, theirs)


def _sibling_copies(parts):
    n = len(parts)

    def build(ins, outs, send_sems, recv_sems, local_sems):
        x, y, c = _place()
        copies = [pltpu.make_async_remote_copy(
            src_ref=ins[w], dst_ref=outs[w], send_sem=send_sems.at[w], recv_sem=recv_sems.at[w],
            device_id=(x, y, 1 - c), device_id_type=MESH_ID) for w in range(n)]
        return copies, copies, []

    return _Carried(parts, [jax.ShapeDtypeStruct(p.shape, p.dtype) for p in parts], n, 0, build)


def _row_tile(rows, most=512, sublanes=16):
    return max(t for t in range(sublanes, most + 1, sublanes) if rows % t == 0)


def _partial_sum(name, own, recv, chip, comm=None):
    _, R, C = recv.shape
    tr = _row_tile(R, most=128)

    def body(o_ref, r_ref, p_ref):
        p_ref[...] = ((o_ref[...] + r_ref[0].astype(F32)) + r_ref[1].astype(F32)) + r_ref[2].astype(F32)

    if own.shape[0] != R:
        assert comm is None and own.shape[0] == N_CHIPS * R
        total = pl.pallas_call(
            lambda chip_ref, *refs: body(*refs), name=name,
            grid_spec=pltpu.PrefetchScalarGridSpec(
                num_scalar_prefetch=1, grid=(R // tr,),
                in_specs=[pl.BlockSpec((tr, C), lambda i, chip_ref: (chip_ref[0] * (R // tr) + i, 0)),
                          pl.BlockSpec((3, tr, C), lambda i, chip_ref: (0, i, 0))],
                out_specs=pl.BlockSpec((tr, C), lambda i, chip_ref: (i, 0))),
            out_shape=jax.ShapeDtypeStruct((R, C), F32), compiler_params=_params(("parallel",)),
        )(chip.reshape(1), own, recv)
        return [total], []
    return _call(name, body, grid=(R // tr,),
                 in_specs=[pl.BlockSpec((tr, C), lambda i: (i, 0)), pl.BlockSpec((3, tr, C), lambda i: (0, i, 0))],
                 out_specs=[pl.BlockSpec((tr, C), lambda i: (i, 0))], out_shape=[jax.ShapeDtypeStruct((R, C), F32)],
                 args=[own, recv], semantics=("parallel",), comm=comm)


def _adam_vals(w, g, m, v):
    m = ADAM_B1 * m + (1.0 - ADAM_B1) * g
    v = ADAM_B2 * v + (1.0 - ADAM_B2) * (g * g)
    m_hat = m / (1.0 - ADAM_B1 ** ADAM_STEP)
    v_hat = v / (1.0 - ADAM_B2 ** ADAM_STEP)
    delta = -ADAM_LR * (m_hat / (jnp.sqrt(v_hat) + ADAM_EPS) + ADAM_WD * w)
    return delta, m, v


def _adamw(name, w, m, v, mine, other, comm=None):
    R, C = w.shape
    tr = _row_tile(R)

    def body(w_ref, m_ref, v_ref, s_ref, n_ref, g_ref, d_ref, nm_ref, nv_ref):
        g = s_ref[...] + n_ref[...]
        d, nm, nv = _adam_vals(w_ref[...], g, m_ref[...], v_ref[...])
        g_ref[...], d_ref[...], nm_ref[...], nv_ref[...] = g, d, nm, nv

    spec = pl.BlockSpec((tr, C), lambda i: (i, 0))
    return _call(name, body, grid=(R // tr,), in_specs=[spec] * 5, out_specs=[spec] * 4,
                 out_shape=[jax.ShapeDtypeStruct((R, C), F32)] * 4, args=[w, m, v, mine, other],
                 semantics=("parallel",), comm=comm)


def _adamw_by_halves(name, w, m, v, mine, other, core):
    R, C = w.shape
    tr = _row_tile(R // 2)
    per_half = R // 2 // tr

    def body(c_ref, w_ref, m_ref, v_ref, s_ref, n_ref, g_ref, d_ref, nm_ref, nv_ref):
        g = jnp.where(pl.program_id(0) // per_half == c_ref[0, 0], s_ref[...], n_ref[...])
        d, nm, nv = _adam_vals(w_ref[...], g, m_ref[...], v_ref[...])
        g_ref[...], d_ref[...], nm_ref[...], nv_ref[...] = g, d, nm, nv

    spec = pl.BlockSpec((tr, C), lambda i: (i, 0))
    part = pl.BlockSpec((tr, C), lambda i: (i % per_half, 0))
    return pl.pallas_call(
        body, name=name, grid=(R // tr,),
        in_specs=[pl.BlockSpec(memory_space=pltpu.SMEM), spec, spec, spec, part, part], out_specs=[spec] * 4,
        out_shape=[jax.ShapeDtypeStruct((R, C), F32)] * 4, compiler_params=_params(("parallel",)),
    )(core, w, m, v, mine, other)


SMALL_LAYOUT = dict(norm_mix_g=(0, 1, 1024), b_in=(1, 8, 7424), hgrn_norm_g=(9, 1, 1024), norm_ffn_g=(10, 1, 1024),
                    norm_final_g=(11, 1, 1024), hgrn_lb_logits=(12, 2, 2048), attn_sinks=(14, 1, 16))
SMALL_LOSS_ROW, SMALL_ROWS = 15, 16


def _pack_small(grads, loss):
    rows = [jnp.pad(grads[name].astype(F32).reshape(-1), (0, nrows * D_MODEL - n))
            for name, (_, nrows, n) in SMALL_LAYOUT.items()]
    rows.append(jnp.pad(loss.astype(F32).reshape(1), (0, D_MODEL - 1)))
    return jnp.concatenate(rows).reshape(SMALL_ROWS, D_MODEL)


def _adamw_small(w, m, v, g_all):
    names = list(SMALL_LAYOUT)
    n = len(names)

    def body(a_ref, *refs):
        ins, outs = refs[:3 * n], refs[3 * n:]
        g_all_rows = a_ref[0]
        for dev in range(1, 8):
            g_all_rows = g_all_rows + a_ref[dev]
        for i, name in enumerate(names):
            first, nrows, count = SMALL_LAYOUT[name]
            w_ref, m_ref, v_ref = ins[3 * i:3 * i + 3]
            if w_ref.shape[0] == nrows:
                g = g_all_rows[first:first + nrows, :w_ref.shape[1]]
            else:
                last = count - (nrows - 1) * D_MODEL
                g = jnp.concatenate([g_all_rows[r:r + 1, :] for r in range(first, first + nrows - 1)]
                                    + [g_all_rows[first + nrows - 1:first + nrows, :last]], axis=1)
            d, nm, nv = _adam_vals(w_ref[...], g, m_ref[...], v_ref[...])
            for o_ref, val in zip(outs[4 * i:4 * i + 4], (g, d, nm, nv)):
                o_ref[...] = val
        outs[4 * n][...] = g_all_rows[SMALL_LOSS_ROW:SMALL_LOSS_ROW + 1, 0:1]

    res = pl.pallas_call(
        body, name="adamw_small",
        out_shape=[jax.ShapeDtypeStruct(w[name].shape, F32) for name in names for _ in range(4)]
        + [jax.ShapeDtypeStruct((1, 1), F32)],
    )(g_all, *[t[name] for name in names for t in (w, m, v)])
    return {name: res[4 * i:4 * i + 4] for i, name in enumerate(names)}, res[4 * n]


MATRICES = ("w_in", "w_branch_attn", "w_branch_hgrn", "w_out", "w_ffn_gate", "w_ffn_up", "w_ffn_down")
COLUMN_SHARDED = ("w_in", "w_ffn_gate", "w_ffn_up")
WEIGHTS = ("norm_mix_g", "w_in", "b_in", "attn_sinks", "hgrn_lb_logits", "hgrn_norm_g", "w_branch_attn",
           "w_branch_hgrn", "w_out", "norm_ffn_g", "w_ffn_gate", "w_ffn_up", "w_ffn_down", "norm_final_g")


def kernel(x, norm_mix_g, w_in, b_in, attn_sinks, hgrn_lb_logits, hgrn_norm_g, w_branch_attn, w_branch_hgrn, w_out, norm_ffn_g, w_ffn_gate, w_ffn_up, w_ffn_down, norm_final_g, loss_target, m_norm_mix_g, m_w_in, m_b_in, m_attn_sinks, m_hgrn_lb_logits, m_hgrn_norm_g, m_w_branch_attn, m_w_branch_hgrn, m_w_out, m_norm_ffn_g, m_w_ffn_gate, m_w_ffn_up, m_w_ffn_down, m_norm_final_g, v_norm_mix_g, v_w_in, v_b_in, v_attn_sinks, v_hgrn_lb_logits, v_hgrn_norm_g, v_w_branch_attn, v_w_branch_hgrn, v_w_out, v_norm_ffn_g, v_w_ffn_gate, v_w_ffn_up, v_w_ffn_down, v_norm_final_g):
    given = dict(locals())
    w = {n: given[n] for n in WEIGHTS}
    m = {n: given["m_" + n] for n in WEIGHTS}
    v = {n: given["v_" + n] for n in WEIGHTS}

    block = lambda a, n: jnp.transpose(a[0]) if n in COLUMN_SHARDED else a[0]
    unblock = lambda a, n: (jnp.transpose(a) if n in COLUMN_SHARDED else a)[None]
    net = _Net({n: block(w[n], n).astype(MXU_DTYPE) for n in MATRICES})
    net.gathered(("w_in",), [_gather_by_neighbours("gather_w_in", net.shards["w_in"])])
    vec = dict(norm_mix_g=norm_mix_g, b_in=b_in, attn_sinks=attn_sinks, hgrn_lb_logits=hgrn_lb_logits,
               hgrn_norm_g=hgrn_norm_g, norm_ffn_g=norm_ffn_g, norm_final_g=norm_final_g.reshape(1, D_MODEL))
    loss_part, dx, d_vecs = _local_step(x[0], loss_target[0], vec, net)

    small_all, = net.received(*net.last, carried=_small_copies(_pack_small(d_vecs, loss_part)))
    grads, deltas, new_m, new_v = {}, {}, {}, {}
    for n in ("w_ffn_down", "w_ffn_gate", "w_ffn_up", "w_out", "w_branch_attn", "w_branch_hgrn"):
        res, got = _adamw("adamw_" + n, block(w[n], n), block(m[n], n), block(v[n], n), net.sums[n], net.other[n],
                          comm=net.swap(("w_in",)) if n == "w_ffn_down" else None)
        if n == "w_ffn_down":
            net.other["w_in"], = got
        grads[n], deltas[n], new_m[n], new_v[n] = (unblock(r, n) for r in res)
    n = "w_in"
    res = _adamw_by_halves("adamw_" + n, block(w[n], n), block(m[n], n), block(v[n], n), net.sums[n], net.other[n],
                           _place()[2].reshape(1, 1))
    grads[n], deltas[n], new_m[n], new_v[n] = (unblock(r, n) for r in res)
    rows = lambda t: {n: t[n].reshape(-1, t[n].shape[-1]) for n in SMALL_LAYOUT}
    res, loss = _adamw_small(rows(w), rows(m), rows(v), small_all)
    for n, four in res.items():
        grads[n], deltas[n], new_m[n], new_v[n] = (r.reshape(w[n].shape) for r in four)
    loss = loss.reshape(())
    return (loss, dx[None], *[grads[n] for n in WEIGHTS], *[deltas[n] for n in WEIGHTS],
            *[new_m[n] for n in WEIGHTS], *[new_v[n] for n in WEIGHTS])
```

```python
import collections
import functools
import math

import jax
import jax.numpy as jnp
from jax import lax
from jax.experimental import pallas as pl
from jax.experimental.pallas import tpu as pltpu

F32 = jnp.float32
BF16 = jnp.bfloat16
MXU_DTYPE = jnp.bfloat16
SAVED_DTYPE = jnp.bfloat16
MESH_ID = pl.DeviceIdType.MESH

D_MODEL = 1024
HEAD_DIM = 64
Q_HEADS = 16
KV_HEADS = 2
GROUP = Q_HEADS // KV_HEADS
KV_WIDTH = KV_HEADS * HEAD_DIM
ATTN_BLOCK = 128
HGRN_HEADS = 8
HGRN_K = 128
CHUNK = 64
HGRN_TOKENS = 256
FFN = 2816
IN_SPLITS = (1024, 256, 4096, 2048)
EPS = 1e-6
NEG_INF = -1e30
ADAM_LR, ADAM_B1, ADAM_B2, ADAM_EPS, ADAM_WD, ADAM_STEP = 0.001, 0.9, 0.999, 1e-08, 0.01, 10
N_CHIPS = 4
VMEM_LIMIT = 60 * 1024 * 1024
ROW_ALIGN = 16


def _params(sem=None):
    return pltpu.CompilerParams(dimension_semantics=sem, vmem_limit_bytes=VMEM_LIMIT)


def _sigmoid(v):
    return 0.5 * jnp.tanh(0.5 * v) + 0.5


def _dot(a, b, dims):
    return lax.dot_general(a.astype(MXU_DTYPE), b.astype(MXU_DTYPE), (dims, ((), ())),
                           preferred_element_type=F32)


def _nn(a, b):
    return _dot(a, b, ((1,), (0,)))


def _nt(a, b):
    return _dot(a, b, ((1,), (1,)))


def _tn(a, b):
    return _dot(a, b, ((0,), (0,)))


HBM_SPEC = pl.BlockSpec(memory_space=pl.ANY)


class _Carried:
    def __init__(self, arrays, out_shapes, n_remote, n_local, build):
        self.parts = [(len(arrays), len(out_shapes), build)]
        self.arrays, self.out_shapes = list(arrays), list(out_shapes)
        self.scratch = [pltpu.SemaphoreType.DMA((n_remote,)), pltpu.SemaphoreType.DMA((n_remote,)),
                        pltpu.SemaphoreType.DMA((max(n_local, 1),))]

    def __add__(self, other):
        both = _Carried([], [], 1, 0, None)
        both.parts = self.parts + other.parts
        both.arrays, both.out_shapes = self.arrays + other.arrays, self.out_shapes + other.out_shapes
        both.scratch = self.scratch + other.scratch
        return both

    def _built(self, ins, outs, sems):
        for p, (ni, no, build) in enumerate(self.parts):
            yield build(ins[:ni], outs[:no], *sems[3 * p:3 * p + 3])
            ins, outs = ins[ni:], outs[no:]

    def start(self, ins, outs, sems):
        core = lax.axis_index("c")
        for sends, _, local, *other_order in self._built(ins, outs, sems):
            for cp in local:
                cp.start()
            if not other_order:
                for cp in sends:
                    cp.start()
                continue

            @pl.when(core == 0)
            def _():
                for cp in sends:
                    cp.start()

            @pl.when(core == 1)
            def _():
                for cp in other_order[0]:
                    cp.start()

    def wait(self, ins, outs, sems):
        for sends, recvs, local, *_ in self._built(ins, outs, sems):
            for cp in recvs:
                cp.wait_recv()
            for cp in sends:
                cp.wait_send()
            for cp in local:
                cp.wait()


def _join(*comms):
    comms = [c for c in comms if c is not None]
    return functools.reduce(lambda a, b: a + b, comms) if comms else None


def _call(name, body, *, grid, in_specs, out_specs, out_shape, args, scratch=(), semantics=None, comm=None,
          aliases=None):
    n_in, n_out, n_scr = len(in_specs), len(out_specs), len(scratch)
    aliases = aliases or {}
    if comm is None:
        res = pl.pallas_call(body, name=name, grid=grid, in_specs=in_specs, out_specs=out_specs, out_shape=out_shape,
                             scratch_shapes=list(scratch), input_output_aliases=aliases,
                             compiler_params=_params(semantics))(*args)
        return list(res), []
    ci, co = len(comm.arrays), len(comm.out_shapes)

    def carrying(*refs):
        ins, refs = refs[:n_in], refs[n_in:]
        c_ins, refs = refs[:ci], refs[ci:]
        outs, refs = refs[:n_out], refs[n_out:]
        c_outs, refs = refs[:co], refs[co:]
        scr, sems = refs[:n_scr], refs[n_scr:]
        if not grid:
            comm.start(c_ins, c_outs, sems)
            body(*ins, *outs, *scr)
            comm.wait(c_ins, c_outs, sems)
            return
        first = functools.reduce(jnp.logical_and, [pl.program_id(a) == 0 for a in range(len(grid))])
        last = functools.reduce(jnp.logical_and, [pl.program_id(a) == g - 1 for a, g in enumerate(grid)])

        @pl.when(first)
        def _():
            comm.start(c_ins, c_outs, sems)

        body(*ins, *outs, *scr)

        @pl.when(last)
        def _():
            comm.wait(c_ins, c_outs, sems)

    res = pl.pallas_call(
        carrying, name=name, grid=grid, in_specs=list(in_specs) + [HBM_SPEC] * ci,
        out_specs=list(out_specs) + [HBM_SPEC] * co, out_shape=list(out_shape) + comm.out_shapes,
        scratch_shapes=list(scratch) + comm.scratch, input_output_aliases=aliases,
        compiler_params=_params(("arbitrary",) * len(grid) if grid else None),
    )(*args, *comm.arrays)
    return list(res[:n_out]), list(res[n_out:])


_NO_COPIES = object()
_Rows = collections.namedtuple("_Rows", "array first rows")
_Cols = collections.namedtuple("_Cols", "array first cols")


_Into = collections.namedtuple("_Into", "rows first held")


def _matmul(name, pairs, mode, *, tm, tn, tk, outs, extras=(), epilogue=None, a_colsum=False, into=None,
            comm=_NO_COPIES):
    prod = dict(nn=_nn, nt=_nt, tn=_tn)[mode]
    pairs = [(a, b if isinstance(b, _Rows) else _Rows(b, 0, b.shape[0])) for a, b in pairs]
    a0, b0 = pairs[0]
    a_cols = a0 if isinstance(a0, _Cols) else None
    if a_cols is not None:
        assert mode == "tn" and len(pairs) == 1 and a_cols.first % tm == 0, name
        pairs = [(a_cols.array, b0)]
        a0 = a_cols.array
    M = (a_cols.cols if a_cols is not None else a0.shape[1]) if mode == "tn" else a0.shape[0]
    N = b0.rows if mode == "nt" else b0.array.shape[1]
    steps, in_specs, offset = [], [], 0
    for a, b in pairs:
        K = a.shape[0] if mode == "tn" else a.shape[1]
        t = min(tk, K)
        assert K % t == 0, (name, K, t)
        kmap = functools.partial(lambda k, off, n: jnp.clip(k - off, 0, n - 1), off=offset, n=K // t)
        if mode == "tn":
            tile0 = a_cols.first // tm if a_cols is not None else 0
            in_specs.append(pl.BlockSpec((t, tm), functools.partial(
                lambda i, j, k, f, tile0: (f(k), tile0 + i), f=kmap, tile0=tile0)))
        else:
            in_specs.append(pl.BlockSpec((tm, t), functools.partial(lambda i, j, k, f: (i, f(k)), f=kmap)))
        whole = b.first == 0 and b.rows == b.array.shape[0]
        if mode == "nt":
            shape = (tn, t) if whole else (pl.Element(tn), pl.Element(t))
            in_specs.append(pl.BlockSpec(shape, functools.partial(
                lambda i, j, k, f, b, t, whole: (j, f(k)) if whole else (
                    pl.multiple_of(b.first + j * tn, ROW_ALIGN), pl.multiple_of(f(k) * t, 128)),
                f=kmap, b=b, t=t, whole=whole)))
        else:
            assert b.rows == K, (name, b.rows, K)
            shape = (t, tn) if whole else (pl.Element(t), pl.Element(tn))
            in_specs.append(pl.BlockSpec(shape, functools.partial(
                lambda i, j, k, f, b, t, whole: (f(k), j) if whole else (
                    pl.multiple_of(b.first + f(k) * t, ROW_ALIGN), pl.multiple_of(j * tn, 128)),
                f=kmap, b=b, t=t, whole=whole)))
        steps.append((offset, offset + K // t))
        offset += K // t
    assert M % tm == 0 and N % tn == 0, (name, M, N, tm, tn)
    ni, nj, nk = M // tm, N // tn, offset
    npair, ne, no = len(pairs), len(extras), len(outs)
    if epilogue is None:
        epilogue = lambda acc: (acc,)

    def finish(acc, extra_refs, out_refs):
        vals = epilogue(acc, *[r[...] for r in extra_refs])
        for (kind, _), o_ref, val in zip(outs, out_refs, vals):
            if kind == "pn":
                val = jnp.broadcast_to(val, o_ref.shape)
            o_ref[...] = val.astype(o_ref.dtype)

    held = list(into.held) if into is not None and into.held is not None else []

    def body(*refs):
        ab, rest = refs[:2 * npair], refs[2 * npair:]
        extra_refs, rest = rest[:ne], rest[ne + len(held):]
        out_refs = rest[:no]
        if nk == 1:
            finish(prod(ab[0][...], ab[1][...]), extra_refs, out_refs)
            if a_colsum:
                rest[no][...] = jnp.sum(ab[0][...].astype(F32), axis=0, keepdims=True)
            return
        sums_ref, acc_ref = rest[no], rest[-1]
        k = pl.program_id(2)

        @pl.when(k == 0)
        def _():
            acc_ref[...] = jnp.zeros_like(acc_ref)
            if a_colsum:
                sums_ref[...] = jnp.zeros((1, tm), F32)

        if a_colsum:
            sums_ref[...] += jnp.sum(ab[0][...].astype(F32), axis=0, keepdims=True)
        for p, (lo, hi) in enumerate(steps):
            @pl.when(jnp.logical_and(k >= lo, k < hi))
            def _():
                acc_ref[...] += prod(ab[2 * p][...], ab[2 * p + 1][...])

        @pl.when(k == nk - 1)
        def _():
            finish(acc_ref[...], extra_refs, out_refs)

    for _, shape, im in extras:
        in_specs.append(pl.BlockSpec(shape, functools.partial(lambda i, j, k, im: im(i, j), im=im)))
    in_specs += [HBM_SPEC] * len(held)
    aliases = {2 * npair + ne + p: p for p in range(len(held))}
    out_shape, out_specs = [], []
    for kind, dt in outs:
        if kind == "mn" and into is not None:
            out_shape.append(jax.ShapeDtypeStruct((into.rows, N), dt))
            out_specs.append(pl.BlockSpec((pl.Element(tm), pl.Element(tn)), lambda i, j, k: (
                pl.multiple_of(into.first + i * tm, ROW_ALIGN), pl.multiple_of(j * tn, 128))))
        elif kind == "mn":
            out_shape.append(jax.ShapeDtypeStruct((M, N), dt))
            out_specs.append(pl.BlockSpec((tm, tn), lambda i, j, k: (i, j)))
        else:
            out_shape.append(jax.ShapeDtypeStruct((8 * ni, N), dt))
            out_specs.append(pl.BlockSpec((8, tn), lambda i, j, k: (i, j)))
    if a_colsum:
        assert mode == "tn" and npair == 1 and nj == 1, name
        out_shape.append(jax.ShapeDtypeStruct((1, M), F32))
        out_specs.append(pl.BlockSpec((1, tm), lambda i, j, k: (0, i)))
    grid = (ni, nj, nk)
    if nj > 1 and nk == 1:
        turned = lambda spec: pl.BlockSpec(spec.block_shape, functools.partial(
            lambda j, i, k, im: im(i, j, k), im=spec.index_map))
        in_specs, out_specs, grid = [turned(s) for s in in_specs], [turned(s) for s in out_specs], (nj, ni, nk)
    res, got = _call(name, body, grid=grid, in_specs=in_specs, out_specs=out_specs, out_shape=out_shape,
                     args=[t for a, b in pairs for t in (a, b.array)] + [e[0] for e in extras] + held,
                     scratch=[pltpu.VMEM((tm, tn), F32)] if nk > 1 else [], aliases=aliases,
                     semantics=("parallel", "parallel", "arbitrary"), comm=None if comm is _NO_COPIES else comm)
    return res if comm is _NO_COPIES else (res, got)


def _ffn_fwd(merged, w_out, x, gain, w_gate_t, w_up_t, *, tm, comm=None):
    (T, D), F = x.shape, w_gate_t.shape[0]

    def body(m_ref, wo_ref, x_ref, g_ref, wg_ref, wu_ref, h_ref, u_ref, gate_ref, up_ref, z_ref):
        h = x_ref[...] + _nn(m_ref[...], wo_ref[...])
        h_ref[...] = h
        u = (h * lax.rsqrt(jnp.mean(h * h, axis=-1, keepdims=True) + EPS) * g_ref[...]).astype(u_ref.dtype)
        u_ref[...] = u
        gate, up = _nt(u, wg_ref[...]), _nt(u, wu_ref[...])
        gate_ref[...], up_ref[...] = gate.astype(gate_ref.dtype), up.astype(up_ref.dtype)
        z_ref[...] = (gate * _sigmoid(gate) * up).astype(z_ref.dtype)

    rows = lambda n: pl.BlockSpec((tm, n), lambda i: (i, 0))
    fixed = _fixed_spec
    return _call("ffn_hidden", body, grid=(T // tm,),
                 in_specs=[rows(D), fixed(w_out), rows(D), fixed(gain), fixed(w_gate_t), fixed(w_up_t)],
                 out_specs=[rows(D), rows(D), rows(F), rows(F), rows(F)],
                 out_shape=[jax.ShapeDtypeStruct((T, D), F32), jax.ShapeDtypeStruct((T, D), MXU_DTYPE)]
                 + [jax.ShapeDtypeStruct((T, F), SAVED_DTYPE)] * 2 + [jax.ShapeDtypeStruct((T, F), MXU_DTYPE)],
                 args=[merged, w_out, x, gain, w_gate_t, w_up_t], semantics=("parallel",), comm=comm)


def _in_proj(x, gain, w_in_t, b_in, *, tm, comm=None):
    T, D = x.shape
    bounds = [sum(IN_SPLITS[:i]) for i in range(len(IN_SPLITS) + 1)]

    def body(x_ref, g_ref, w_ref, b_ref, u_ref, *piece_refs):
        xv = x_ref[...]
        r = lax.rsqrt(jnp.mean(xv * xv, axis=-1, keepdims=True) + EPS)
        u = (xv * r * g_ref[...]).astype(u_ref.dtype)
        u_ref[...] = u
        for o_ref, lo, hi in zip(piece_refs, bounds[:-1], bounds[1:]):
            o_ref[...] = (_nt(u, w_ref[lo:hi, :]) + b_ref[:, lo:hi]).astype(o_ref.dtype)

    rows = lambda n: pl.BlockSpec((tm, n), lambda i: (i, 0))
    fixed = _fixed_spec
    dtypes = (MXU_DTYPE, MXU_DTYPE, F32, F32)
    return _call("in_proj", body, grid=(T // tm,),
                 in_specs=[rows(D), fixed(gain), fixed(w_in_t), fixed(b_in)],
                 out_specs=[rows(D)] + [rows(n) for n in IN_SPLITS],
                 out_shape=[jax.ShapeDtypeStruct((T, D), MXU_DTYPE)]
                 + [jax.ShapeDtypeStruct((T, n), dt) for n, dt in zip(IN_SPLITS, dtypes)],
                 args=[x, gain, w_in_t, b_in], semantics=("parallel",), comm=comm)


def _row_spec(tm, n):
    return pl.BlockSpec((tm, n), lambda i: (i, 0))


def _fixed_spec(a):
    return pl.BlockSpec(a.shape, lambda i: (0,) * a.ndim, pipeline_mode=pl.Buffered(1))


def _partials_spec(n):
    return pl.BlockSpec((8, n), lambda i: (i, 0))


def _ffn_tail(z, w_down, h1, target, gain, gate, up, *, tm):
    (T, F), D = z.shape, h1.shape[1]

    def body(z_ref, w_ref, h_ref, t_ref, g_ref, gate_ref, up_ref, dh_ref, dhb_ref, dgate_ref, dup_ref, dg_ref, l_ref):
        h2 = h_ref[...] + _nn(z_ref[...], w_ref[...])
        r = lax.rsqrt(jnp.mean(h2 * h2, axis=-1, keepdims=True) + EPS)
        xhat = h2 * r
        err = xhat * g_ref[...] - t_ref[...]
        part = 0.5 * jnp.sum(jnp.sum(err * err, axis=-1, keepdims=True), axis=0, keepdims=True) / D
        dy = err / D
        dxh = dy * g_ref[...]
        dh2 = r * (dxh - xhat * jnp.mean(dxh * xhat, axis=-1, keepdims=True))
        dh_ref[...] = dh2
        dhb = dh2.astype(dhb_ref.dtype)
        dhb_ref[...] = dhb
        dg_ref[...] = jnp.broadcast_to(jnp.sum(dy * xhat, axis=0, keepdims=True), dg_ref.shape)
        l_ref[...] = jnp.broadcast_to(part, l_ref.shape)
        dz = _nt(dhb, w_ref[...])
        gv, upv = gate_ref[...].astype(F32), up_ref[...].astype(F32)
        s = _sigmoid(gv)
        dgate_ref[...] = (dz * upv * (s * (1.0 + gv * (1.0 - s)))).astype(dgate_ref.dtype)
        dup_ref[...] = (dz * (gv * s)).astype(dup_ref.dtype)

    low = lambda n: jax.ShapeDtypeStruct((T, n), MXU_DTYPE)
    part = jax.ShapeDtypeStruct((8 * (T // tm), D), F32)
    return pl.pallas_call(
        body, name="ffn_tail", grid=(T // tm,),
        in_specs=[_row_spec(tm, F), _fixed_spec(w_down), _row_spec(tm, D), _row_spec(tm, D), _fixed_spec(gain),
                  _row_spec(tm, F), _row_spec(tm, F)],
        out_specs=[_row_spec(tm, D), _row_spec(tm, D), _row_spec(tm, F), _row_spec(tm, F), _partials_spec(D),
                   _partials_spec(D)],
        out_shape=[jax.ShapeDtypeStruct((T, D), F32), low(D), low(F), low(F), part, part],
        compiler_params=_params(("parallel",)),
    )(z, w_down, h1, target, gain, gate, up)


def _ffn_in_bwd(dgate, dup, w_gate_t, w_up_t, h1, gain, dres, *, tm, comm=None):
    (T, F), D = dgate.shape, h1.shape[1]

    def body(dg_ref, du_ref, wg_ref, wu_ref, h_ref, g_ref, r_ref, dh_ref, dhb_ref, dgain_ref):
        d_u2 = _nn(dg_ref[...], wg_ref[...]) + _nn(du_ref[...], wu_ref[...])
        dx, dgain = _rmsnorm_bwd_vals(d_u2, h_ref[...], g_ref[...])
        dh = r_ref[...] + dx
        dh_ref[...] = dh
        dhb_ref[...] = dh.astype(dhb_ref.dtype)
        dgain_ref[...] = jnp.broadcast_to(dgain, dgain_ref.shape)

    return _call("d_ffn_in", body, grid=(T // tm,),
                 in_specs=[_row_spec(tm, F), _row_spec(tm, F), _fixed_spec(w_gate_t), _fixed_spec(w_up_t),
                           _row_spec(tm, D), _fixed_spec(gain), _row_spec(tm, D)],
                 out_specs=[_row_spec(tm, D), _row_spec(tm, D), _partials_spec(D)],
                 out_shape=[jax.ShapeDtypeStruct((T, D), F32), jax.ShapeDtypeStruct((T, D), MXU_DTYPE),
                            jax.ShapeDtypeStruct((8 * (T // tm), D), F32)],
                 args=[dgate, dup, w_gate_t, w_up_t, h1, gain, dres], semantics=("parallel",), comm=comm)


def _in_proj_bwd(pieces, w_in_t, x, gain, dres, *, tm, comm=None):
    T, D = x.shape
    n = len(pieces)

    def body(*refs):
        dps, (w_ref, x_ref, g_ref, r_ref, dx_ref, dgain_ref) = refs[:n], refs[n:]
        d_u = None
        for dp_ref, (dp, first) in zip(dps, pieces):
            term = _nn(dp_ref[...], w_ref[first:first + dp.shape[1], :])
            d_u = term if d_u is None else d_u + term
        dx, dgain = _rmsnorm_bwd_vals(d_u, x_ref[...], g_ref[...])
        dx_ref[...] = r_ref[...] + dx
        dgain_ref[...] = jnp.broadcast_to(dgain, dgain_ref.shape)

    return _call("d_u", body, grid=(T // tm,),
                 in_specs=[_row_spec(tm, dp.shape[1]) for dp, _ in pieces]
                 + [_fixed_spec(w_in_t), _row_spec(tm, D), _fixed_spec(gain), _row_spec(tm, D)],
                 out_specs=[_row_spec(tm, D), _partials_spec(D)],
                 out_shape=[jax.ShapeDtypeStruct((T, D), F32), jax.ShapeDtypeStruct((8 * (T // tm), D), F32)],
                 args=[dp for dp, _ in pieces] + [w_in_t, x, gain, dres], semantics=("parallel",), comm=comm)


def _merge_fwd(y_a, y_b, w_a, w_b, gates, *, tm):
    T, D = y_a.shape

    def body(ya_ref, yb_ref, wa_ref, wb_ref, ga_ref, gb_ref, pa_ref, pb_ref, m_ref):
        pa, pb = _nn(ya_ref[...], wa_ref[...]), _nn(yb_ref[...], wb_ref[...])
        pa_ref[...], pb_ref[...] = pa.astype(pa_ref.dtype), pb.astype(pb_ref.dtype)
        m_ref[...] = (_sigmoid(ga_ref[...]) * pa + _sigmoid(gb_ref[...]) * pb).astype(m_ref.dtype)

    rows = pl.BlockSpec((tm, D), lambda i: (i, 0))
    whole = pl.BlockSpec((D, D), lambda i: (0, 0), pipeline_mode=pl.Buffered(1))
    return pl.pallas_call(
        body, name="branch_merge", grid=(T // tm,),
        in_specs=[rows, rows, whole, whole, rows, pl.BlockSpec((tm, D), lambda i: (i, 1))], out_specs=[rows] * 3,
        out_shape=[jax.ShapeDtypeStruct((T, D), SAVED_DTYPE)] * 2 + [jax.ShapeDtypeStruct((T, D), MXU_DTYPE)],
        compiler_params=_params(("parallel",)),
    )(y_a, y_b, w_a, w_b, gates, gates)


def _merge_bwd(dh, w_out, w_a, w_b, p_a, p_b, gates, *, tm, d_in_width, first):
    T, D = dh.shape

    def body(dh_ref, wo_ref, wa_ref, wb_ref, pa_ref, pb_ref, ga_ref, gb_ref, dpa_ref, dpb_ref, dya_ref, dyb_ref,
             din_ref):
        dm = _nt(dh_ref[...], wo_ref[...])
        sa, sb = _sigmoid(ga_ref[...]), _sigmoid(gb_ref[...])
        dpa, dpb = (dm * sa).astype(dpa_ref.dtype), (dm * sb).astype(dpb_ref.dtype)
        dpa_ref[...], dpb_ref[...] = dpa, dpb
        din_ref[:, :D] = (dm * pa_ref[...].astype(F32) * sa * (1.0 - sa)).astype(din_ref.dtype)
        din_ref[:, D:] = (dm * pb_ref[...].astype(F32) * sb * (1.0 - sb)).astype(din_ref.dtype)
        dya_ref[...] = _nt(dpa, wa_ref[...]).astype(dya_ref.dtype)
        dyb_ref[...] = _nt(dpb, wb_ref[...]).astype(dyb_ref.dtype)

    rows = pl.BlockSpec((tm, D), lambda i: (i, 0))
    whole = pl.BlockSpec((D, D), lambda i: (0, 0), pipeline_mode=pl.Buffered(1))
    low = jax.ShapeDtypeStruct((T, D), MXU_DTYPE)
    return pl.pallas_call(
        body, name="d_branch_merge", grid=(T // tm,),
        in_specs=[rows, whole, whole, whole, rows, rows, rows, pl.BlockSpec((tm, D), lambda i: (i, 1))],
        out_specs=[rows] * 4 + [pl.BlockSpec((pl.Element(tm), pl.Element(2 * D)), lambda i: (
            pl.multiple_of(i * tm, ROW_ALIGN), first))],
        out_shape=[low] * 3 + [jax.ShapeDtypeStruct((T, D), F32), jax.ShapeDtypeStruct((T, d_in_width), MXU_DTYPE)],
        compiler_params=_params(("parallel",)),
    )(dh, w_out, w_a, w_b, p_a, p_b, gates, gates)


def _colsum_partials(p):
    return jnp.sum(p.reshape(-1, 8, p.shape[-1])[:, 0, :], axis=0, keepdims=True)


def _rmsnorm_bwd_vals(dy, xin, g):
    rstd = lax.rsqrt(jnp.mean(xin * xin, axis=-1, keepdims=True) + EPS)
    xhat = xin * rstd
    dg = jnp.sum(dy * xhat, axis=0, keepdims=True)
    dxh = dy * g
    dx = rstd * (dxh - xhat * jnp.mean(dxh * xhat, axis=-1, keepdims=True))
    return dx, dg


def _colsum(name, a, tr=512, comm=_NO_COPIES):
    T, N = a.shape

    def body(a_ref, o_ref):
        @pl.when(pl.program_id(0) == 0)
        def _():
            o_ref[...] = jnp.zeros_like(o_ref)

        o_ref[...] += jnp.sum(a_ref[...].astype(F32), axis=0, keepdims=True)

    (res,), got = _call(name, body, grid=(T // tr,), in_specs=[pl.BlockSpec((tr, N), lambda i: (i, 0))],
                        out_specs=[pl.BlockSpec((1, N), lambda i: (0, 0))],
                        out_shape=[jax.ShapeDtypeStruct((1, N), F32)], args=[a], semantics=("arbitrary",),
                        comm=None if comm is _NO_COPIES else comm)
    return res if comm is _NO_COPIES else (res, got)


ATTN_SCALE = 1.0 / math.sqrt(HEAD_DIM)
GROUP_LANES = GROUP * ATTN_BLOCK
PAIR = 2 * HEAD_DIM


def _attn_mask():
    kj = lax.broadcasted_iota(jnp.int32, (ATTN_BLOCK, GROUP_LANES), 0)
    qi = lax.broadcasted_iota(jnp.int32, (ATTN_BLOCK, GROUP_LANES), 1) & (ATTN_BLOCK - 1)
    return kj <= qi


def _heads_transposed(ref, g, scale=None):
    parts = []
    for a in range(GROUP // 2):
        lo = (g * GROUP // 2 + a) * PAIR
        pair = ref[:, lo:lo + PAIR].astype(F32)
        pair = (pair if scale is None else pair * scale).T
        parts += [pair[:HEAD_DIM], pair[HEAD_DIM:]]
    return jnp.concatenate(parts, axis=1).astype(MXU_DTYPE)


def _heads_back(ref, g, vt):
    for a in range(GROUP // 2):
        lo = (g * GROUP // 2 + a) * PAIR
        pair = jnp.concatenate([vt[:, (2 * a) * ATTN_BLOCK:(2 * a + 1) * ATTN_BLOCK],
                                vt[:, (2 * a + 1) * ATTN_BLOCK:(2 * a + 2) * ATTN_BLOCK]], axis=0)
        ref[:, lo:lo + PAIR] = pair.T.astype(ref.dtype)


def _kv_parts(kv_ref, g):
    ks = slice(g * HEAD_DIM, (g + 1) * HEAD_DIM)
    vs = slice(KV_WIDTH + g * HEAD_DIM, KV_WIDTH + (g + 1) * HEAD_DIM)
    return kv_ref[:, ks].astype(MXU_DTYPE), kv_ref[:, vs].astype(MXU_DTYPE)


def _sink_rows(sinks):
    return jnp.repeat(sinks.reshape(KV_HEADS, GROUP), ATTN_BLOCK, axis=1)


def _attn_fwd(pq, pkv, sinks, comm=None):
    T = pq.shape[0]
    nb = T // ATTN_BLOCK

    def body(q_ref, kvc_ref, kvp_ref, s_ref, y_ref, lse_ref):
        mask_c = _attn_mask()
        has_prev = pl.program_id(0) > 0
        for g in range(KV_HEADS):
            (kc, vc), (kp, vp) = _kv_parts(kvc_ref, g), _kv_parts(kvp_ref, g)
            qt = _heads_transposed(q_ref, g, ATTN_SCALE)
            s = jnp.where(mask_c, _nn(kc, qt), jnp.where(has_prev, _nn(kp, qt), NEG_INF))
            sink = s_ref[g:g + 1, :]
            m = jnp.maximum(jnp.max(s, axis=0, keepdims=True), sink)
            p = jnp.exp(s - m)
            den = jnp.sum(p, axis=0, keepdims=True) + jnp.exp(sink - m)
            pc = jnp.where(mask_c, p, 0.0)
            _heads_back(y_ref, g, (_tn(vc, pc) + _tn(vp, p - pc)) / den)
            lse = m + jnp.log(den)
            for i in range(GROUP):
                lse_ref[g * GROUP + i:g * GROUP + i + 1, :] = lse[:, i * ATTN_BLOCK:(i + 1) * ATTN_BLOCK]

    return _call(
        "attn_fwd", body, grid=(nb,),
        in_specs=[pl.BlockSpec((ATTN_BLOCK, D_MODEL), lambda n: (n, 0)),
                  pl.BlockSpec((ATTN_BLOCK, 2 * KV_WIDTH), lambda n: (n, 0)),
                  pl.BlockSpec((ATTN_BLOCK, 2 * KV_WIDTH), lambda n: (jnp.maximum(n - 1, 0), 0)),
                  pl.BlockSpec((KV_HEADS, GROUP_LANES), lambda n: (0, 0))],
        out_specs=[pl.BlockSpec((ATTN_BLOCK, D_MODEL), lambda n: (n, 0)),
                   pl.BlockSpec((Q_HEADS, ATTN_BLOCK), lambda n: (0, n))],
        out_shape=[jax.ShapeDtypeStruct((T, D_MODEL), MXU_DTYPE), jax.ShapeDtypeStruct((Q_HEADS, T), F32)],
        args=[pq, pkv, pkv, _sink_rows(sinks)], semantics=("parallel",), comm=comm)


def _attn_bwd(pq, pkv, sinks, lse, dy, d_in, comm=None):
    T = pq.shape[0]
    nb = T // ATTN_BLOCK
    cur = lambda n: (jnp.minimum(n, nb - 1), 0)
    done = D_MODEL + 2 * KV_WIDTH

    def body(q_ref, kvc_ref, kvp_ref, s_ref, lse_ref, dy_ref, _, out_ref, ds_ref, carry, top, bot, dq_ref):
        n = pl.program_id(0)

        @pl.when(n == 0)
        def _():
            carry[...] = jnp.zeros_like(carry)
            dq_ref[...] = jnp.zeros_like(dq_ref)
            ds_ref[...] = jnp.zeros_like(ds_ref)

        out_ref[:, :D_MODEL] = dq_ref[...]

        @pl.when(n < nb)
        def _():
            mask_c = _attn_mask()
            valid = jnp.logical_or(mask_c, n > 0)
            for g in range(KV_HEADS):
                ks = slice(g * HEAD_DIM, (g + 1) * HEAD_DIM)
                vs = slice(KV_WIDTH + g * HEAD_DIM, KV_WIDTH + (g + 1) * HEAD_DIM)
                (kc, vc), (kp, vp) = _kv_parts(kvc_ref, g), _kv_parts(kvp_ref, g)
                qt = _heads_transposed(q_ref, g, ATTN_SCALE)
                dot = _heads_transposed(dy_ref, g)
                lse = jnp.concatenate([lse_ref[g * GROUP + i:g * GROUP + i + 1, :] for i in range(GROUP)], axis=1)
                p = jnp.where(valid, jnp.exp(jnp.where(mask_c, _nn(kc, qt), _nn(kp, qt)) - lse), 0.0)
                dp = jnp.where(mask_c, _nn(vc, dot), _nn(vp, dot))
                delta = jnp.sum(p * dp, axis=0, keepdims=True)
                ds = p * (dp - delta)
                ds_c, p_c = jnp.where(mask_c, ds, 0.0), jnp.where(mask_c, p, 0.0)
                ds_p, p_p = ds - ds_c, p - p_c
                _heads_back(dq_ref, g, (_tn(kc, ds_c) + _tn(kp, ds_p)) * ATTN_SCALE)
                bot[:, ks], bot[:, vs] = _nt(ds_c, qt), _nt(p_c, dot)
                top[:, ks], top[:, vs] = _nt(ds_p, qt), _nt(p_p, dot)
                ds_ref[g:g + 1, :] -= jnp.exp(s_ref[g:g + 1, :] - lse) * delta
            out_ref[:, D_MODEL:] = (carry[...] + top[...]).astype(out_ref.dtype)
            carry[...] = bot[...]

        @pl.when(n == nb)
        def _():
            out_ref[:, D_MODEL:] = carry[...].astype(out_ref.dtype)

    return _call(
        "attn_bwd", body, grid=(nb + 1,),
        in_specs=[pl.BlockSpec((ATTN_BLOCK, D_MODEL), cur),
                  pl.BlockSpec((ATTN_BLOCK, 2 * KV_WIDTH), cur),
                  pl.BlockSpec((ATTN_BLOCK, 2 * KV_WIDTH), lambda n: (jnp.maximum(jnp.minimum(n, nb - 1) - 1, 0), 0)),
                  pl.BlockSpec((KV_HEADS, GROUP_LANES), lambda n: (0, 0)),
                  pl.BlockSpec((Q_HEADS, ATTN_BLOCK), lambda n: (0, jnp.minimum(n, nb - 1))),
                  pl.BlockSpec((ATTN_BLOCK, D_MODEL), cur), HBM_SPEC],
        out_specs=[pl.BlockSpec((ATTN_BLOCK, done), lambda n: (jnp.maximum(n - 1, 0), 0)),
                   pl.BlockSpec((KV_HEADS, GROUP_LANES), lambda n: (0, 0))],
        out_shape=[jax.ShapeDtypeStruct(d_in.shape, d_in.dtype), jax.ShapeDtypeStruct((KV_HEADS, GROUP_LANES), F32)],
        scratch=[pltpu.VMEM((ATTN_BLOCK, 2 * KV_WIDTH), F32)] * 3 + [pltpu.VMEM((ATTN_BLOCK, D_MODEL), MXU_DTYPE)],
        args=[pq, pkv, pkv, _sink_rows(sinks), lse, dy, d_in], semantics=("arbitrary",), comm=comm, aliases={6: 0})


def _lower_bound(l):
    m = jnp.maximum(l[0:1], l[1:2])
    e0, e1 = jnp.exp(l[0:1] - m), jnp.exp(l[1:2] - m)
    return e0 / (e0 + e1)


def _tri(lower):
    r = lax.broadcasted_iota(jnp.int32, (CHUNK, CHUNK), 0)
    c = lax.broadcasted_iota(jnp.int32, (CHUNK, CHUNK), 1)
    return (r >= c) if lower else (c >= r)


def _chunk_sum(mask, v):
    ones = mask.astype(BF16)
    hi = v.astype(BF16)
    rest = v - hi.astype(F32)
    mid = rest.astype(BF16)
    lo = (rest - mid.astype(F32)).astype(BF16)
    part = lambda t: lax.dot_general(ones, t, (((1,), (0,)), ((), ())), preferred_element_type=F32)
    return part(hi) + part(mid) + part(lo)


def _hgrn_chunk_inputs(hq, hf, lb, causal):
    half_t = 0.5 * jnp.tanh(0.5 * hf)
    sg, sgn = 0.5 + half_t, 0.5 - half_t
    f = lb + (1.0 - lb) * sg
    kk = (1.0 - lb) * sgn
    sq = _sigmoid(hq)
    q = hq * sq
    b = _chunk_sum(causal, jnp.log(f))
    bm, bl = b[CHUNK // 2 - 1:CHUNK // 2, :], b[CHUNK - 1:CHUNK, :]
    e_qm, e_km = jnp.exp(b - bm), jnp.exp(bm - b)
    e_qs, e_kl = e_qm * jnp.exp(bm), e_km * jnp.exp(bl - bm)
    return dict(sg=sg, sgn=sgn, f=f, kk=kk, sq=sq, q=q, e_qm=e_qm, e_km=e_km, e_qs=e_qs, e_kl=e_kl,
                qm=q * e_qm, km=kk * e_km, qs=q * e_qs, kl=kk * e_kl, el=jnp.exp(bl))


def _hgrn_fwd(ph, lb_logits, norm_g, comm=None):
    T = ph.shape[0]
    nblk, cpb = T // HGRN_TOKENS, HGRN_TOKENS // CHUNK
    col = lambda c: pl.BlockSpec((HGRN_TOKENS, D_MODEL), functools.partial(lambda i, c: (i, c), c=c))

    def body(hq_ref, hf_ref, hi_ref, hg_ref, l_ref, ng_ref, y_ref, o_ref, st_ref, s_ref):
        @pl.when(pl.program_id(0) == 0)
        def _():
            s_ref[...] = jnp.zeros_like(s_ref)

        lb = _lower_bound(l_ref[...])
        causal = _tri(True)
        for c in range(cpb):
            rows = slice(c * CHUNK, (c + 1) * CHUNK)
            t = _hgrn_chunk_inputs(hq_ref[rows, :], hf_ref[rows, :], lb, causal)
            qm, km, qs, kl = (t[n].astype(MXU_DTYPE) for n in ("qm", "km", "qs", "kl"))
            v = hi_ref[rows, :].astype(MXU_DTYPE)
            for h in range(HGRN_HEADS):
                ls = slice(h * HGRN_K, (h + 1) * HGRN_K)
                st = s_ref[h]
                st_ref[c, ls, :] = st
                a = jnp.where(causal, _nt(qm[:, ls], km[:, ls]), 0.0)
                o_ref[rows, ls] = _nn(a, v[:, ls]) + _nt(qs[:, ls], st)
                s_ref[h] = t["el"][:, ls] * st + _tn(v[:, ls], kl[:, ls])
        for h in range(HGRN_HEADS):
            ls = slice(h * HGRN_K, (h + 1) * HGRN_K)
            o = o_ref[:, ls]
            r = lax.rsqrt(jnp.mean(o * o, axis=-1, keepdims=True) + EPS)
            y_ref[:, ls] = (o * r * ng_ref[:, ls] * _sigmoid(hg_ref[:, ls])).astype(y_ref.dtype)

    return _call(
        "hgrn_fwd", body, grid=(nblk,),
        in_specs=[col(0), col(1), col(2), col(3),
                  pl.BlockSpec((2, D_MODEL), lambda i: (0, 0)), pl.BlockSpec((1, D_MODEL), lambda i: (0, 0))],
        out_specs=[pl.BlockSpec((HGRN_TOKENS, D_MODEL), lambda i: (i, 0)),
                   pl.BlockSpec((HGRN_TOKENS, D_MODEL), lambda i: (i, 0)),
                   pl.BlockSpec((cpb, D_MODEL, HGRN_K), lambda i: (i, 0, 0))],
        out_shape=[jax.ShapeDtypeStruct((T, D_MODEL), MXU_DTYPE), jax.ShapeDtypeStruct((T, D_MODEL), F32),
                   jax.ShapeDtypeStruct((T // CHUNK, D_MODEL, HGRN_K), F32)],
        scratch=[pltpu.VMEM((HGRN_HEADS, HGRN_K, HGRN_K), F32)],
        args=[ph, ph, ph, ph, lb_logits, norm_g], semantics=("arbitrary",), comm=comm)


def _hgrn_bwd(ph, o_raw, states, dy, lb_logits, norm_g, d_in, first, comm=None):
    T = ph.shape[0]
    nblk, cpb = T // HGRN_TOKENS, HGRN_TOKENS // CHUNK
    rev = lambda i: nblk - 1 - i
    col = lambda c: pl.BlockSpec((HGRN_TOKENS, D_MODEL), functools.partial(lambda i, c: (rev(i), c), c=c))
    tok = pl.BlockSpec((HGRN_TOKENS, D_MODEL), lambda i: (rev(i), 0))

    def body(hq_ref, hf_ref, hi_ref, hg_ref, o_ref, st_ref, dy_ref, l_ref, ng_ref, _,
             dph_ref, dng_ref, dl_ref, dst_ref, dlb_ref, do_s, dqm_s, dkm_s, dqs_s, dkl_s, dv_s, del_s):
        i = pl.program_id(0)

        @pl.when(i == 0)
        def _():
            dst_ref[...] = jnp.zeros_like(dst_ref)
            dlb_ref[...] = jnp.zeros_like(dlb_ref)
            dng_ref[...] = jnp.zeros_like(dng_ref)

        lb = _lower_bound(l_ref[...])
        causal, anti = _tri(True), _tri(False)
        row = lax.broadcasted_iota(jnp.int32, (CHUNK, D_MODEL), 0)
        for c in reversed(range(cpb)):
            rows = slice(c * CHUNK, (c + 1) * CHUNK)
            hq = hq_ref[rows, :]
            t = _hgrn_chunk_inputs(hq, hf_ref[rows, :], lb, causal)
            sgg = _sigmoid(hg_ref[rows, :])
            dyv = dy_ref[rows, :]
            for h in range(HGRN_HEADS):
                ls = slice(h * HGRN_K, (h + 1) * HGRN_K)
                o = o_ref[rows, ls]
                r = lax.rsqrt(jnp.mean(o * o, axis=-1, keepdims=True) + EPS)
                nrm = o * r
                g_h = sgg[:, ls]
                dph_ref[rows, 3 * D_MODEL + h * HGRN_K:3 * D_MODEL + (h + 1) * HGRN_K] = (
                    dyv[:, ls] * nrm * ng_ref[:, ls] * g_h * (1.0 - g_h)).astype(dph_ref.dtype)
                dyg = dyv[:, ls] * g_h
                dng_ref[:, ls] += jnp.sum(dyg * nrm, axis=0, keepdims=True)
                dn = dyg * ng_ref[:, ls]
                do_s[:, ls] = r * (dn - nrm * jnp.mean(dn * nrm, axis=-1, keepdims=True))
            qm, km, qs, kl = (t[n].astype(MXU_DTYPE) for n in ("qm", "km", "qs", "kl"))
            v = hi_ref[rows, :].astype(MXU_DTYPE)
            do = do_s[...].astype(MXU_DTYPE)
            for h in range(HGRN_HEADS):
                ls = slice(h * HGRN_K, (h + 1) * HGRN_K)
                st = st_ref[c, ls, :]
                dst = dst_ref[h]
                a = jnp.where(causal, _nt(qm[:, ls], km[:, ls]), 0.0)
                da = jnp.where(causal, _nt(do[:, ls], v[:, ls]), 0.0)
                dv_s[:, ls] = _tn(a, do[:, ls]) + _nt(kl[:, ls], dst)
                dkl_s[:, ls] = _nn(v[:, ls], dst)
                dqs_s[:, ls] = _nn(do[:, ls], st)
                del_s[:, ls] = jnp.sum(dst * st, axis=0, keepdims=True)
                dst_ref[h] = _tn(do[:, ls], qs[:, ls]) + t["el"][:, ls] * dst
                dqm_s[:, ls] = _nn(da, km[:, ls])
                dkm_s[:, ls] = _tn(da, qm[:, ls])
            dqm, dkm, dqs, dkl = dqm_s[...], dkm_s[...], dqs_s[...], dkl_s[...]
            dq = dqm * t["e_qm"] + dqs * t["e_qs"]
            dk = dkm * t["e_km"] + dkl * t["e_kl"]
            t_qm, t_km, t_kl = dqm * t["qm"], dkm * t["km"], dkl * t["kl"]
            db = t_qm - t_km + dqs * t["qs"] - t_kl
            db_mid = jnp.sum(t_km - t_qm, axis=0, keepdims=True)
            db_last = jnp.sum(t_kl, axis=0, keepdims=True) + del_s[...] * t["el"]
            db = db + jnp.where(row == CHUNK // 2 - 1, db_mid, 0.0) + jnp.where(row == CHUNK - 1, db_last, 0.0)
            dlogf = _chunk_sum(anti, db)
            sq, sg, sgn, f = t["sq"], t["sg"], t["sgn"], t["f"]
            dph_ref[rows, 0:D_MODEL] = (dq * (sq * (1.0 + hq * (1.0 - sq)))).astype(dph_ref.dtype)
            dph_ref[rows, D_MODEL:2 * D_MODEL] = (
                dlogf * (1.0 - lb) * sg * (1.0 - sg) / f - dk * (1.0 - lb) * sgn * (1.0 - sgn)).astype(dph_ref.dtype)
            dph_ref[rows, 2 * D_MODEL:3 * D_MODEL] = dv_s[...].astype(dph_ref.dtype)
            dlb_ref[...] += jnp.sum(dlogf * (1.0 - sg) / f - dk * sgn, axis=0, keepdims=True)

        @pl.when(i == nblk - 1)
        def _():
            dl0 = dlb_ref[...] * lb * (1.0 - lb)
            dl_ref[0:1, :] = dl0
            dl_ref[1:2, :] = -dl0

    wide = pltpu.VMEM((CHUNK, D_MODEL), F32)
    return _call(
        "hgrn_bwd", body, grid=(nblk,),
        in_specs=[col(0), col(1), col(2), col(3), tok,
                  pl.BlockSpec((cpb, D_MODEL, HGRN_K), lambda i: (rev(i), 0, 0)), tok,
                  pl.BlockSpec((2, D_MODEL), lambda i: (0, 0)), pl.BlockSpec((1, D_MODEL), lambda i: (0, 0)), HBM_SPEC],
        out_specs=[pl.BlockSpec((pl.Element(HGRN_TOKENS), pl.Element(4 * D_MODEL)), lambda i: (
                       pl.multiple_of(rev(i) * HGRN_TOKENS, ROW_ALIGN), first)),
                   pl.BlockSpec((1, D_MODEL), lambda i: (0, 0)), pl.BlockSpec((2, D_MODEL), lambda i: (0, 0))],
        out_shape=[jax.ShapeDtypeStruct(d_in.shape, d_in.dtype), jax.ShapeDtypeStruct((1, D_MODEL), F32),
                   jax.ShapeDtypeStruct((2, D_MODEL), F32)],
        scratch=[pltpu.VMEM((HGRN_HEADS, HGRN_K, HGRN_K), F32), pltpu.VMEM((1, D_MODEL), F32),
                 wide, wide, wide, wide, wide, wide, pltpu.VMEM((1, D_MODEL), F32)],
        args=[ph, ph, ph, ph, o_raw, states, dy, lb_logits, norm_g, d_in], semantics=("arbitrary",), comm=comm,
        aliases={9: 0})


def _local_step(x, target, vec, net):
    T, D = x.shape
    norm_mix_g, b_in, sinks, lb_logits = vec["norm_mix_g"], vec["b_in"], vec["attn_sinks"], vec["hgrn_lb_logits"]
    hgrn_norm_g, norm_ffn_g, norm_final_g = vec["hgrn_norm_g"], vec["norm_ffn_g"], vec["norm_final_g"]
    w_in = net.full("w_in")
    o_q, o_kv, o_h, o_g = (sum(IN_SPLITS[:i]) for i in range(4))
    w_q, w_kv, w_h, w_g = (_Rows(w_in, o, n) for o, n in zip((o_q, o_kv, o_h, o_g), IN_SPLITS))
    both = lambda acc: (acc, acc)
    grad_outs = [("mn", F32), ("mn", MXU_DTYPE)]
    row_vec = lambda n: ((1, n), lambda i, j: (0, j))
    tile = lambda tm, tn: ((tm, tn), lambda i, j: (i, j))
    TM = 512
    BIG = min(T, 1024)

    names = ("w_ffn_gate",)
    (u, pq, pkv, ph, pg), got = _in_proj(x, norm_mix_g, w_in, b_in, tm=TM, comm=net.gather(names))
    net.gathered(names, got)
    names = ("w_branch_attn", "w_branch_hgrn", "w_out")
    (y_attn, lse), got = _attn_fwd(pq, pkv, sinks, comm=net.gather(names))
    net.gathered(names, got)
    names = ("w_ffn_up",)
    (y_hgrn, o_raw, states), got = _hgrn_fwd(ph, lb_logits, hgrn_norm_g, comm=net.gather(names))
    net.gathered(names, got)
    w_ba, w_bh, w_out = net.full("w_branch_attn"), net.full("w_branch_hgrn"), net.full("w_out")
    w_gate, w_up = net.full("w_ffn_gate"), net.full("w_ffn_up")
    ya, yb, merged = _merge_fwd(y_attn, y_hgrn, w_ba, w_bh, pg, tm=TM)

    FT = FFN // 2
    names = ("w_ffn_down",)
    (h1, u2, gpre, up, z), got = _ffn_fwd(merged, w_out, x, norm_ffn_g, w_gate, w_up, tm=TM,
                                          comm=net.gather(names))
    net.gathered(names, got)
    w_down = net.full("w_ffn_down")

    dh2, dh2b, dgp, dup, dgf_p, loss_p = _ffn_tail(z, w_down, h1, target, norm_final_g, gpre, up, tm=256)
    loss = jnp.sum(loss_p.reshape(-1, 8, D)[:, 0, 0])
    d_norm_final = _colsum_partials(dgf_p)
    d_w_down = _matmul("dw_down", [(z, dh2b)], "tn", tm=FT, tn=1024, tk=T, outs=grad_outs, epilogue=both)

    names = ("w_ffn_down",)
    (dh1, dh1b, dg2_p), got = _ffn_in_bwd(dgp, dup, w_gate, w_up, h1, norm_ffn_g, dh2, tm=TM,
        comm=net.exchange(dict(w_ffn_down=[d_w_down])))
    net.received(names, (), got)
    d_norm_ffn = _colsum_partials(dg2_p)
    d_w_gate = _matmul("dw_gate", [(dgp, u2)], "tn", tm=FT, tn=1024, tk=T, outs=grad_outs, epilogue=both)
    d_w_up = _matmul("dw_up", [(dup, u2)], "tn", tm=FT, tn=1024, tk=T, outs=grad_outs, epilogue=both)

    rows_in = sum(IN_SPLITS)
    dya, dyb, dy_attn, dy_hgrn, d_in = _merge_bwd(dh1b, w_out, w_ba, w_bh, ya, yb, pg, tm=TM, d_in_width=rows_in,
                                                  first=o_g)
    tn_grad = functools.partial(_matmul, mode="tn", tn=1024, tk=1024, outs=grad_outs, epilogue=both)
    d_w_out = tn_grad("dw_out", [(merged, dh1b)], tm=1024)
    d_w_ba = tn_grad("dw_branch_a", [(y_attn, dya)], tm=1024)
    d_w_bh = tn_grad("dw_branch_b", [(y_hgrn, dyb)], tm=1024)

    names, swap = ("w_ffn_gate",), ("w_ffn_down",)
    (d_in, dsink), got = _attn_bwd(pq, pkv, sinks, lse, dy_attn, d_in,
                                   comm=net.exchange(dict(w_ffn_gate=[d_w_gate]), swap))
    net.received(names, swap, got)
    names, swap = ("w_ffn_up", "w_out", "w_branch_attn", "w_branch_hgrn"), ("w_ffn_gate",)
    (d_in, d_hgrn_norm, d_lb_logits), got = _hgrn_bwd(
        ph, o_raw, states, dy_hgrn, lb_logits, hgrn_norm_g, d_in, o_h,
        comm=net.exchange(dict(w_ffn_up=[d_w_up], w_out=[d_w_out], w_branch_attn=[d_w_ba],
                               w_branch_hgrn=[d_w_bh]), swap))
    net.received(names, swap, got)

    main = rows_in // 1024 * 1024
    *d_w_in, db_main = tn_grad("dw_in", [(_Cols(d_in, 0, main), u)], tm=1024, tk=T, a_colsum=True,
                               into=_Into(rows_in, 0, None))
    *d_w_in, db_rest = tn_grad("dw_in_rest", [(_Cols(d_in, main, rows_in - main), u)], tm=rows_in - main,
                               a_colsum=True, into=_Into(rows_in, main, d_w_in))
    d_w_in = [tuple(d_w_in)]

    first_level = net.presum_begin("w_in", d_w_in)
    halves = net.presum_end("w_in", [] if first_level is None else _copies_alone("presum_swap_w_in", first_level))
    names, swap = ("w_in",), ("w_ffn_up", "w_out", "w_branch_attn", "w_branch_hgrn")
    (dx, dg1_p), got = _in_proj_bwd([(d_in, 0)], w_in, x, norm_mix_g, dh1, tm=TM,
                                    comm=_join(halves, net.swap(swap)))
    net.last = (names, swap, got)
    d_norm_mix = _colsum_partials(dg1_p)
    d_b_in = jnp.concatenate([db_main, db_rest], axis=1)
    vecs = dict(norm_mix_g=d_norm_mix, b_in=d_b_in, attn_sinks=jnp.sum(dsink.reshape(Q_HEADS, ATTN_BLOCK), axis=1).reshape(1, Q_HEADS),
                hgrn_lb_logits=d_lb_logits,
                hgrn_norm_g=d_hgrn_norm, norm_ffn_g=d_norm_ffn, norm_final_g=d_norm_final)
    return loss, dx, vecs


def _place():
    return lax.axis_index("x"), lax.axis_index("y"), lax.axis_index("c")


def _other_chips(x, y):
    return [(1 - x, y), (x, 1 - y), (1 - x, 1 - y)]


def _y_first(copies):
    return [copies[3 * (i // 3) + (1, 0, 2)[i % 3]] for i in range(len(copies))]


def _gather_copies(shards):
    n = len(shards)

    def build(ins, outs, send_sems, recv_sems, local_sems):
        x, y, c = _place()
        mine = 2 * x + y
        local = [pltpu.make_async_copy(ins[w], outs[w].at[mine], local_sems.at[w]) for w in range(n)]
        sends, recvs = [], []
        for w in range(n):
            for k, (px, py) in enumerate(_other_chips(x, y)):
                sem = 3 * w + k
                sends.append(pltpu.make_async_remote_copy(
                    src_ref=ins[w], dst_ref=outs[w].at[mine], send_sem=send_sems.at[sem], recv_sem=recv_sems.at[sem],
                    device_id=(px, py, c), device_id_type=MESH_ID))
                recvs.append(pltpu.make_async_remote_copy(
                    src_ref=ins[w], dst_ref=outs[w].at[2 * px + py], send_sem=send_sems.at[sem],
                    recv_sem=recv_sems.at[sem], device_id=(px, py, c), device_id_type=MESH_ID))
        return sends, recvs, local, _y_first(sends)

    return _Carried(shards, [jax.ShapeDtypeStruct((N_CHIPS,) + s.shape, s.dtype) for s in shards], 3 * n, n, build)


def _grad_copies(stacked):
    n = len(stacked)

    def build(ins, outs, send_sems, recv_sems, local_sems):
        x, y, c = _place()
        sends = []
        for w in range(n):
            for k, (px, py) in enumerate(_other_chips(x, y)):
                sem = 3 * w + k
                sends.append(pltpu.make_async_remote_copy(
                    src_ref=ins[w].at[2 * px + py], dst_ref=outs[w].at[k], send_sem=send_sems.at[sem],
                    recv_sem=recv_sems.at[sem], device_id=(px, py, c), device_id_type=MESH_ID))
        return sends, sends, [], _y_first(sends)

    return _Carried(stacked, [jax.ShapeDtypeStruct((3,) + s.shape[1:], s.dtype) for s in stacked], 3 * n, 0, build)


def _small_copies(small):
    def build(ins, outs, send_sems, recv_sems, local_sems):
        small_ref, all_ref = ins[0], outs[0]
        x, y, c = _place()
        me = 4 * x + 2 * y + c
        sends, recvs = [], []
        for r in range(1, 8):
            px = 1 - x if r & 4 else x
            py = 1 - y if r & 2 else y
            pc = 1 - c if r & 1 else c
            sends.append(pltpu.make_async_remote_copy(
                src_ref=small_ref, dst_ref=all_ref.at[me], send_sem=send_sems.at[r - 1], recv_sem=recv_sems.at[r - 1],
                device_id=(px, py, pc), device_id_type=MESH_ID))
            recvs.append(pltpu.make_async_remote_copy(
                src_ref=small_ref, dst_ref=all_ref.at[4 * px + 2 * py + pc], send_sem=send_sems.at[r - 1],
                recv_sem=recv_sems.at[r - 1], device_id=(px, py, pc), device_id_type=MESH_ID))
        return sends, recvs, [pltpu.make_async_copy(small_ref, all_ref.at[me], local_sems.at[0])]

    return _Carried([small], [jax.ShapeDtypeStruct((8,) + small.shape, small.dtype)], 7, 1, build)


def _gather_by_neighbours(name, shard):
    half = shard.shape[0] // 2
    quarter = half // 2

    def body(in_ref, out_ref, send_sems, recv_sems, local_sem):
        for core in (0, 1):
            @pl.when(lax.axis_index("c") == core)
            def _():
                program(core, in_ref, out_ref, send_sems, recv_sems, local_sem)

    def program(c, in_ref, out_ref, send_sems, recv_sems, local_sem):
        x, y, _ = _place()
        chip = lambda px, py: 2 * px + py
        to_x, to_y, sibling = (1 - x, y, c), (x, 1 - y, c), (x, y, 1 - c)
        x_blk, y_blk, d_blk = chip(1 - x, y), chip(x, 1 - y), chip(1 - x, 1 - y)
        mine, theirs = c * half, (1 - c) * half

        def copy(sem, rows, block, to, src=None):
            place = out_ref.at[block, pl.ds(rows[0], rows[1])]
            return pltpu.make_async_remote_copy(
                src_ref=place if src is None else src, dst_ref=place, send_sem=send_sems.at[sem],
                recv_sem=recv_sems.at[sem], device_id=to, device_id_type=MESH_ID)

        own = pltpu.make_async_copy(in_ref, out_ref.at[chip(x, y)], local_sem)
        own.start()
        my_rows = in_ref.at[pl.ds(mine, half)]
        along_x = dict(send=copy(0, (mine, half), chip(x, y), to_x, src=my_rows),
                       landed=copy(0, (mine, half), x_blk, to_x),
                       onward=[copy(3, (mine + quarter, quarter), x_blk, to_y), copy(4, (mine, half), x_blk, sibling)],
                       diagonal=copy(2, (mine, quarter), d_blk, to_x))
        along_y = dict(send=copy(1, (mine, half), chip(x, y), to_y, src=my_rows),
                       landed=copy(1, (mine, half), y_blk, to_y),
                       onward=[copy(2, (mine, quarter), y_blk, to_x), copy(5, (mine, half), y_blk, sibling)],
                       diagonal=copy(3, (mine + quarter, quarter), d_blk, to_y))
        last = copy(6, (mine, half), d_blk, sibling)

        order = (along_x, along_y) if c == 0 else (along_y, along_x)
        for axis in order:
            axis["send"].start()
        for axis in order:
            axis["landed"].wait_recv()
            for cp in axis["onward"]:
                cp.start()
        for axis in order:
            axis["diagonal"].wait_recv()
        last.start()
        for sem, block in ((4, x_blk), (5, y_blk), (6, d_blk)):
            copy(sem, (theirs, half), block, sibling).wait_recv()
        for cp in [along_x["send"], along_y["send"]] + along_x["onward"] + along_y["onward"] + [last]:
            cp.wait_send()
        own.wait()

    return pl.pallas_call(
        body, name=name, in_specs=[HBM_SPEC], out_specs=HBM_SPEC,
        out_shape=jax.ShapeDtypeStruct((N_CHIPS,) + shard.shape, shard.dtype),
        scratch_shapes=[pltpu.SemaphoreType.DMA((7,)), pltpu.SemaphoreType.DMA((7,)), pltpu.SemaphoreType.DMA(())],
    )(shard)


def _copies_alone(name, comm):
    return _call(name, lambda: None, grid=(), in_specs=[], out_specs=[], out_shape=[], args=[], comm=comm)[1]


class _Net:
    def __init__(self, shards):
        self.shards = shards
        self.whole, self.own, self.theirs, self.sums, self.other = {}, {}, {}, {}, {}
        x, y, _ = _place()
        self.chip = 2 * x + y

    def gather(self, names):
        return _gather_copies([self.shards[n] for n in names])

    def gathered(self, names, got):
        for n, g in zip(names, got):
            self.whole[n] = g.reshape(-1, g.shape[-1])

    def full(self, name):
        return self.whole[name]

    def exchange(self, grads, swap=()):
        stacked = []
        for n, pieces in grads.items():
            (keep, send), = pieces
            self.own[n] = keep
            stacked.append(send.reshape(N_CHIPS, keep.shape[0] // N_CHIPS, send.shape[-1]))
        return _join(_grad_copies(stacked), self.swap(swap))

    def swap(self, names):
        return _sibling_copies([self.sums[n] for n in names]) if names else None

    def presum_begin(self, name, pieces):
        keep = jnp.concatenate([p[0] for p in pieces], axis=0) if len(pieces) > 1 else pieces[0][0]
        send = jnp.concatenate([p[1] for p in pieces], axis=0) if len(pieces) > 1 else pieces[0][1]
        rows = keep.shape[0] // N_CHIPS
        self.held = keep.reshape(N_CHIPS, rows, keep.shape[-1])
        return _half_rows_copies(send.reshape(N_CHIPS, rows, send.shape[-1]))

    def presum_end(self, name, got):
        x, y, c = _place()
        to_send, self.own[name] = _pre_sum("presum_" + name, self.held, got[0], jnp.stack([c, self.chip]))
        return _grad_copies([to_send])

    def received(self, names, swap, got, carried=None):
        self.theirs.update(zip(names, got[:len(names)]))
        self.other.update(zip(swap, got[len(names):]))
        for n in names:
            (self.sums[n],), more = _partial_sum("sum_" + n, self.own[n], self.theirs[n], self.chip, comm=carried)
        return more


def _half_rows_copies(stacked):
    n, rows = stacked.shape[0], stacked.shape[1] // 2

    def build(ins, outs, send_sems, recv_sems, local_sems):
        x, y, c = _place()
        copies = [pltpu.make_async_remote_copy(
            src_ref=ins[0].at[s, pl.ds((1 - c) * rows, rows)], dst_ref=outs[0].at[s], send_sem=send_sems.at[s],
            recv_sem=recv_sems.at[s], device_id=(x, y, 1 - c), device_id_type=MESH_ID) for s in range(n)]
        return copies, copies, []

    return _Carried([stacked], [jax.ShapeDtypeStruct((n, rows, stacked.shape[2]), stacked.dtype)], n, 0, build)


def _pre_sum(name, held, theirs, core_and_chip):
    n, R, C = held.shape
    half = R // 2
    tr = _row_tile(half)
    per_half = half // tr

    def body(place_ref, h_ref, t_ref, send_ref, own_ref):
        total = h_ref[0] + t_ref[0].astype(F32)
        send_ref[0] = total.astype(send_ref.dtype)

        @pl.when(pl.program_id(1) == place_ref[1])
        def _():
            own_ref[...] = total

    return pl.pallas_call(
        body, name=name,
        grid_spec=pltpu.PrefetchScalarGridSpec(
            num_scalar_prefetch=1, grid=(per_half, n),
            in_specs=[pl.BlockSpec((1, tr, C), lambda i, s, place: (s, place[0] * per_half + i, 0)),
                      pl.BlockSpec((1, tr, C), lambda i, s, place: (s, i, 0))],
            out_specs=[pl.BlockSpec((1, tr, C), lambda i, s, place: (s, i, 0)),
                       pl.BlockSpec((tr, C), lambda i, s, place: (i, 0))]),
        out_shape=[jax.ShapeDtypeStruct((n, half, C), MXU_DTYPE), jax.ShapeDtypeStruct((half, C), F32)],
        compiler_params=_params(("arbitrary", "arbitrary")),
    )(core_and_chip, held, theirs)


def _sibling_copies(parts):
    n = len(parts)

    def build(ins, outs, send_sems, recv_sems, local_sems):
        x, y, c = _place()
        copies = [pltpu.make_async_remote_copy(
            src_ref=ins[w], dst_ref=outs[w], send_sem=send_sems.at[w], recv_sem=recv_sems.at[w],
            device_id=(x, y, 1 - c), device_id_type=MESH_ID) for w in range(n)]
        return copies, copies, []

    return _Carried(parts, [jax.ShapeDtypeStruct(p.shape, p.dtype) for p in parts], n, 0, build)


def _row_tile(rows, most=512, sublanes=16):
    return max(t for t in range(sublanes, most + 1, sublanes) if rows % t == 0)


def _partial_sum(name, own, recv, chip, comm=None):
    _, R, C = recv.shape
    tr = _row_tile(R, most=128)

    def body(o_ref, r_ref, p_ref):
        p_ref[...] = ((o_ref[...] + r_ref[0].astype(F32)) + r_ref[1].astype(F32)) + r_ref[2].astype(F32)

    if own.shape[0] != R:
        assert comm is None and own.shape[0] == N_CHIPS * R
        total = pl.pallas_call(
            lambda chip_ref, *refs: body(*refs), name=name,
            grid_spec=pltpu.PrefetchScalarGridSpec(
                num_scalar_prefetch=1, grid=(R // tr,),
                in_specs=[pl.BlockSpec((tr, C), lambda i, chip_ref: (chip_ref[0] * (R // tr) + i, 0)),
                          pl.BlockSpec((3, tr, C), lambda i, chip_ref: (0, i, 0))],
                out_specs=pl.BlockSpec((tr, C), lambda i, chip_ref: (i, 0))),
            out_shape=jax.ShapeDtypeStruct((R, C), F32), compiler_params=_params(("parallel",)),
        )(chip.reshape(1), own, recv)
        return [total], []
    return _call(name, body, grid=(R // tr,),
                 in_specs=[pl.BlockSpec((tr, C), lambda i: (i, 0)), pl.BlockSpec((3, tr, C), lambda i: (0, i, 0))],
                 out_specs=[pl.BlockSpec((tr, C), lambda i: (i, 0))], out_shape=[jax.ShapeDtypeStruct((R, C), F32)],
                 args=[own, recv], semantics=("parallel",), comm=comm)


def _adam_vals(w, g, m, v):
    m = ADAM_B1 * m + (1.0 - ADAM_B1) * g
    v = ADAM_B2 * v + (1.0 - ADAM_B2) * (g * g)
    m_hat = m / (1.0 - ADAM_B1 ** ADAM_STEP)
    v_hat = v / (1.0 - ADAM_B2 ** ADAM_STEP)
    delta = -ADAM_LR * (m_hat / (jnp.sqrt(v_hat) + ADAM_EPS) + ADAM_WD * w)
    return delta, m, v


def _adamw(name, w, m, v, mine, other, comm=None):
    R, C = w.shape
    tr = _row_tile(R)

    def body(w_ref, m_ref, v_ref, s_ref, n_ref, g_ref, d_ref, nm_ref, nv_ref):
        g = s_ref[...] + n_ref[...]
        d, nm, nv = _adam_vals(w_ref[...], g, m_ref[...], v_ref[...])
        g_ref[...], d_ref[...], nm_ref[...], nv_ref[...] = g, d, nm, nv

    spec = pl.BlockSpec((tr, C), lambda i: (i, 0))
    return _call(name, body, grid=(R // tr,), in_specs=[spec] * 5, out_specs=[spec] * 4,
                 out_shape=[jax.ShapeDtypeStruct((R, C), F32)] * 4, args=[w, m, v, mine, other],
                 semantics=("parallel",), comm=comm)


def _adamw_by_halves(name, w, m, v, mine, other, core):
    R, C = w.shape
    tr = _row_tile(R // 2)
    per_half = R // 2 // tr

    def body(c_ref, w_ref, m_ref, v_ref, s_ref, n_ref, g_ref, d_ref, nm_ref, nv_ref):
        g = jnp.where(pl.program_id(0) // per_half == c_ref[0, 0], s_ref[...], n_ref[...])
        d, nm, nv = _adam_vals(w_ref[...], g, m_ref[...], v_ref[...])
        g_ref[...], d_ref[...], nm_ref[...], nv_ref[...] = g, d, nm, nv

    spec = pl.BlockSpec((tr, C), lambda i: (i, 0))
    part = pl.BlockSpec((tr, C), lambda i: (i % per_half, 0))
    return pl.pallas_call(
        body, name=name, grid=(R // tr,),
        in_specs=[pl.BlockSpec(memory_space=pltpu.SMEM), spec, spec, spec, part, part], out_specs=[spec] * 4,
        out_shape=[jax.ShapeDtypeStruct((R, C), F32)] * 4, compiler_params=_params(("parallel",)),
    )(core, w, m, v, mine, other)


SMALL_LAYOUT = dict(norm_mix_g=(0, 1, 1024), b_in=(1, 8, 7424), hgrn_norm_g=(9, 1, 1024), norm_ffn_g=(10, 1, 1024),
                    norm_final_g=(11, 1, 1024), hgrn_lb_logits=(12, 2, 2048), attn_sinks=(14, 1, 16))
SMALL_LOSS_ROW, SMALL_ROWS = 15, 16


def _pack_small(grads, loss):
    rows = [jnp.pad(grads[name].astype(F32).reshape(-1), (0, nrows * D_MODEL - n))
            for name, (_, nrows, n) in SMALL_LAYOUT.items()]
    rows.append(jnp.pad(loss.astype(F32).reshape(1), (0, D_MODEL - 1)))
    return jnp.concatenate(rows).reshape(SMALL_ROWS, D_MODEL)


def _adamw_small(w, m, v, g_all):
    names = list(SMALL_LAYOUT)
    n = len(names)

    def body(a_ref, *refs):
        ins, outs = refs[:3 * n], refs[3 * n:]
        g_all_rows = a_ref[0]
        for dev in range(1, 8):
            g_all_rows = g_all_rows + a_ref[dev]
        for i, name in enumerate(names):
            first, nrows, count = SMALL_LAYOUT[name]
            w_ref, m_ref, v_ref = ins[3 * i:3 * i + 3]
            if w_ref.shape[0] == nrows:
                g = g_all_rows[first:first + nrows, :w_ref.shape[1]]
            else:
                last = count - (nrows - 1) * D_MODEL
                g = jnp.concatenate([g_all_rows[r:r + 1, :] for r in range(first, first + nrows - 1)]
                                    + [g_all_rows[first + nrows - 1:first + nrows, :last]], axis=1)
            d, nm, nv = _adam_vals(w_ref[...], g, m_ref[...], v_ref[...])
            for o_ref, val in zip(outs[4 * i:4 * i + 4], (g, d, nm, nv)):
                o_ref[...] = val
        outs[4 * n][...] = g_all_rows[SMALL_LOSS_ROW:SMALL_LOSS_ROW + 1, 0:1]

    res = pl.pallas_call(
        body, name="adamw_small",
        out_shape=[jax.ShapeDtypeStruct(w[name].shape, F32) for name in names for _ in range(4)]
        + [jax.ShapeDtypeStruct((1, 1), F32)],
    )(g_all, *[t[name] for name in names for t in (w, m, v)])
    return {name: res[4 * i:4 * i + 4] for i, name in enumerate(names)}, res[4 * n]


MATRICES = ("w_in", "w_branch_attn", "w_branch_hgrn", "w_out", "w_ffn_gate", "w_ffn_up", "w_ffn_down")
COLUMN_SHARDED = ("w_in", "w_ffn_gate", "w_ffn_up")
WEIGHTS = ("norm_mix_g", "w_in", "b_in", "attn_sinks", "hgrn_lb_logits", "hgrn_norm_g", "w_branch_attn",
           "w_branch_hgrn", "w_out", "norm_ffn_g", "w_ffn_gate", "w_ffn_up", "w_ffn_down", "norm_final_g")


def kernel(x, norm_mix_g, w_in, b_in, attn_sinks, hgrn_lb_logits, hgrn_norm_g, w_branch_attn, w_branch_hgrn, w_out, norm_ffn_g, w_ffn_gate, w_ffn_up, w_ffn_down, norm_final_g, loss_target, m_norm_mix_g, m_w_in, m_b_in, m_attn_sinks, m_hgrn_lb_logits, m_hgrn_norm_g, m_w_branch_attn, m_w_branch_hgrn, m_w_out, m_norm_ffn_g, m_w_ffn_gate, m_w_ffn_up, m_w_ffn_down, m_norm_final_g, v_norm_mix_g, v_w_in, v_b_in, v_attn_sinks, v_hgrn_lb_logits, v_hgrn_norm_g, v_w_branch_attn, v_w_branch_hgrn, v_w_out, v_norm_ffn_g, v_w_ffn_gate, v_w_ffn_up, v_w_ffn_down, v_norm_final_g):
    given = dict(locals())
    w = {n: given[n] for n in WEIGHTS}
    m = {n: given["m_" + n] for n in WEIGHTS}
    v = {n: given["v_" + n] for n in WEIGHTS}

    block = lambda a, n: jnp.transpose(a[0]) if n in COLUMN_SHARDED else a[0]
    unblock = lambda a, n: (jnp.transpose(a) if n in COLUMN_SHARDED else a)[None]
    net = _Net({n: block(w[n], n).astype(MXU_DTYPE) for n in MATRICES})
    net.gathered(("w_in",), [_gather_by_neighbours("gather_w_in", net.shards["w_in"])])
    vec = dict(norm_mix_g=norm_mix_g, b_in=b_in, attn_sinks=attn_sinks, hgrn_lb_logits=hgrn_lb_logits,
               hgrn_norm_g=hgrn_norm_g, norm_ffn_g=norm_ffn_g, norm_final_g=norm_final_g.reshape(1, D_MODEL))
    loss_part, dx, d_vecs = _local_step(x[0], loss_target[0], vec, net)

    small_all, = net.received(*net.last, carried=_small_copies(_pack_small(d_vecs, loss_part)))
    grads, deltas, new_m, new_v = {}, {}, {}, {}
    for n in ("w_ffn_down", "w_ffn_gate", "w_ffn_up", "w_out", "w_branch_attn", "w_branch_hgrn"):
        res, got = _adamw("adamw_" + n, block(w[n], n), block(m[n], n), block(v[n], n), net.sums[n], net.other[n],
                          comm=net.swap(("w_in",)) if n == "w_ffn_down" else None)
        if n == "w_ffn_down":
            net.other["w_in"], = got
        grads[n], deltas[n], new_m[n], new_v[n] = (unblock(r, n) for r in res)
    n = "w_in"
    res = _adamw_by_halves("adamw_" + n, block(w[n], n), block(m[n], n), block(v[n], n), net.sums[n], net.other[n],
                           _place()[2].reshape(1, 1))
    grads[n], deltas[n], new_m[n], new_v[n] = (unblock(r, n) for r in res)
    rows = lambda t: {n: t[n].reshape(-1, t[n].shape[-1]) for n in SMALL_LAYOUT}
    res, loss = _adamw_small(rows(w), rows(m), rows(v), small_all)
    for n, four in res.items():
        grads[n], deltas[n], new_m[n], new_v[n] = (r.reshape(w[n].shape) for r in four)
    loss = loss.reshape(())
    return (loss, dx[None], *[grads[n] for n in WEIGHTS], *[deltas[n] for n in WEIGHTS],
            *[new_m[n] for n in WEIGHTS], *[new_v[n] for n in WEIGHTS])
```

```python
import collections
import functools
import math

import jax
import jax.numpy as jnp
from jax import lax
from jax.experimental import pallas as pl
from jax.experimental.pallas import tpu as pltpu

F32 = jnp.float32
BF16 = jnp.bfloat16
MXU_DTYPE = jnp.bfloat16
SAVED_DTYPE = jnp.bfloat16
MESH_ID = pl.DeviceIdType.MESH

D_MODEL = 1024
HEAD_DIM = 64
Q_HEADS = 16
KV_HEADS = 2
GROUP = Q_HEADS // KV_HEADS
KV_WIDTH = KV_HEADS * HEAD_DIM
ATTN_BLOCK = 128
HGRN_HEADS = 8
HGRN_K = 128
CHUNK = 64
HGRN_TOKENS = 256
FFN = 2816
IN_SPLITS = (1024, 256, 4096, 2048)
EPS = 1e-6
NEG_INF = -1e30
ADAM_LR, ADAM_B1, ADAM_B2, ADAM_EPS, ADAM_WD, ADAM_STEP = 0.001, 0.9, 0.999, 1e-08, 0.01, 10
N_CHIPS = 4
VMEM_LIMIT = 60 * 1024 * 1024
ROW_ALIGN = 16


def _params(sem=None):
    return pltpu.CompilerParams(dimension_semantics=sem, vmem_limit_bytes=VMEM_LIMIT)


def _sigmoid(v):
    return 0.5 * jnp.tanh(0.5 * v) + 0.5


def _dot(a, b, dims):
    return lax.dot_general(a.astype(MXU_DTYPE), b.astype(MXU_DTYPE), (dims, ((), ())),
                           preferred_element_type=F32)


def _nn(a, b):
    return _dot(a, b, ((1,), (0,)))


def _nt(a, b):
    return _dot(a, b, ((1,), (1,)))


def _tn(a, b):
    return _dot(a, b, ((0,), (0,)))


HBM_SPEC = pl.BlockSpec(memory_space=pl.ANY)


class _Carried:
    def __init__(self, arrays, out_shapes, n_remote, n_local, build):
        self.parts = [(len(arrays), len(out_shapes), build)]
        self.arrays, self.out_shapes = list(arrays), list(out_shapes)
        self.scratch = [pltpu.SemaphoreType.DMA((n_remote,)), pltpu.SemaphoreType.DMA((n_remote,)),
                        pltpu.SemaphoreType.DMA((max(n_local, 1),))]

    def __add__(self, other):
        both = _Carried([], [], 1, 0, None)
        both.parts = self.parts + other.parts
        both.arrays, both.out_shapes = self.arrays + other.arrays, self.out_shapes + other.out_shapes
        both.scratch = self.scratch + other.scratch
        return both

    def _built(self, ins, outs, sems):
        for p, (ni, no, build) in enumerate(self.parts):
            yield build(ins[:ni], outs[:no], *sems[3 * p:3 * p + 3])
            ins, outs = ins[ni:], outs[no:]

    def start(self, ins, outs, sems):
        core = lax.axis_index("c")
        for sends, _, local, *other_order in self._built(ins, outs, sems):
            for cp in local:
                cp.start()
            if not other_order:
                for cp in sends:
                    cp.start()
                continue

            @pl.when(core == 0)
            def _():
                for cp in sends:
                    cp.start()

            @pl.when(core == 1)
            def _():
                for cp in other_order[0]:
                    cp.start()

    def wait(self, ins, outs, sems):
        for sends, recvs, local, *_ in self._built(ins, outs, sems):
            for cp in recvs:
                cp.wait_recv()
            for cp in sends:
                cp.wait_send()
            for cp in local:
                cp.wait()


def _join(*comms):
    comms = [c for c in comms if c is not None]
    return functools.reduce(lambda a, b: a + b, comms) if comms else None


def _call(name, body, *, grid, in_specs, out_specs, out_shape, args, scratch=(), semantics=None, comm=None,
          aliases=None):
    n_in, n_out, n_scr = len(in_specs), len(out_specs), len(scratch)
    aliases = aliases or {}
    if comm is None:
        res = pl.pallas_call(body, name=name, grid=grid, in_specs=in_specs, out_specs=out_specs, out_shape=out_shape,
                             scratch_shapes=list(scratch), input_output_aliases=aliases,
                             compiler_params=_params(semantics))(*args)
        return list(res), []
    ci, co = len(comm.arrays), len(comm.out_shapes)

    def carrying(*refs):
        ins, refs = refs[:n_in], refs[n_in:]
        c_ins, refs = refs[:ci], refs[ci:]
        outs, refs = refs[:n_out], refs[n_out:]
        c_outs, refs = refs[:co], refs[co:]
        scr, sems = refs[:n_scr], refs[n_scr:]
        if not grid:
            comm.start(c_ins, c_outs, sems)
            body(*ins, *outs, *scr)
            comm.wait(c_ins, c_outs, sems)
            return
        first = functools.reduce(jnp.logical_and, [pl.program_id(a) == 0 for a in range(len(grid))])
        last = functools.reduce(jnp.logical_and, [pl.program_id(a) == g - 1 for a, g in enumerate(grid)])

        @pl.when(first)
        def _():
            comm.start(c_ins, c_outs, sems)

        body(*ins, *outs, *scr)

        @pl.when(last)
        def _():
            comm.wait(c_ins, c_outs, sems)

    res = pl.pallas_call(
        carrying, name=name, grid=grid, in_specs=list(in_specs) + [HBM_SPEC] * ci,
        out_specs=list(out_specs) + [HBM_SPEC] * co, out_shape=list(out_shape) + comm.out_shapes,
        scratch_shapes=list(scratch) + comm.scratch, input_output_aliases=aliases,
        compiler_params=_params(("arbitrary",) * len(grid) if grid else None),
    )(*args, *comm.arrays)
    return list(res[:n_out]), list(res[n_out:])


_Cols = collections.namedtuple("_Cols", "array first cols")
_Into = collections.namedtuple("_Into", "rows first held")


def _weight_grad(name, a, b, *, tm, tk, a_colsum=False, into=None):
    cols = a if isinstance(a, _Cols) else _Cols(a, 0, a.shape[1])
    a = cols.array
    (T, N), M = b.shape, cols.cols
    tk = min(tk, T)
    assert cols.first % tm == 0 and M % tm == 0 and T % tk == 0, (name, cols.first, M, tm, T, tk)
    ni, nk, tile0 = M // tm, T // tk, cols.first // tm
    held = list(into.held) if into is not None and into.held is not None else []

    def body(a_ref, b_ref, *rest):
        keep_ref, send_ref = rest[len(held):len(held) + 2]
        sums_ref = rest[len(held) + 2] if a_colsum else None
        if nk == 1:
            acc = _tn(a_ref[...], b_ref[...])
            keep_ref[...], send_ref[...] = acc, acc.astype(send_ref.dtype)
            if a_colsum:
                sums_ref[...] = jnp.sum(a_ref[...].astype(F32), axis=0, keepdims=True)
            return
        acc_ref = rest[-1]
        k = pl.program_id(1)

        @pl.when(k == 0)
        def _():
            acc_ref[...] = jnp.zeros_like(acc_ref)
            if a_colsum:
                sums_ref[...] = jnp.zeros((1, tm), F32)

        if a_colsum:
            sums_ref[...] += jnp.sum(a_ref[...].astype(F32), axis=0, keepdims=True)
        acc_ref[...] += _tn(a_ref[...], b_ref[...])

        @pl.when(k == nk - 1)
        def _():
            keep_ref[...], send_ref[...] = acc_ref[...], acc_ref[...].astype(send_ref.dtype)

    if into is None:
        rows, out_spec = M, pl.BlockSpec((tm, N), lambda i, k: (i, 0))
    else:
        rows = into.rows
        out_spec = pl.BlockSpec((pl.Element(tm), pl.Element(N)),
                                lambda i, k: (pl.multiple_of(into.first + i * tm, ROW_ALIGN), 0))
    out_shape = [jax.ShapeDtypeStruct((rows, N), F32), jax.ShapeDtypeStruct((rows, N), MXU_DTYPE)]
    out_specs = [out_spec, out_spec]
    if a_colsum:
        out_shape.append(jax.ShapeDtypeStruct((1, M), F32))
        out_specs.append(pl.BlockSpec((1, tm), lambda i, k: (0, i)))
    return pl.pallas_call(
        body, name=name, grid=(ni, nk),
        in_specs=[pl.BlockSpec((tk, tm), lambda i, k: (k, tile0 + i)), pl.BlockSpec((tk, N), lambda i, k: (k, 0))]
        + [HBM_SPEC] * len(held),
        out_specs=out_specs, out_shape=out_shape, scratch_shapes=[pltpu.VMEM((tm, N), F32)] if nk > 1 else [],
        input_output_aliases={2 + p: p for p in range(len(held))},
        compiler_params=_params(("parallel", "arbitrary")),
    )(a, b, *held)


def _ffn_fwd(merged, w_out, x, gain, w_gate_t, w_up_t, *, tm, comm=None):
    (T, D), F = x.shape, w_gate_t.shape[0]

    def body(m_ref, wo_ref, x_ref, g_ref, wg_ref, wu_ref, h_ref, u_ref, gate_ref, up_ref, z_ref):
        h = x_ref[...] + _nn(m_ref[...], wo_ref[...])
        h_ref[...] = h
        u = (h * lax.rsqrt(jnp.mean(h * h, axis=-1, keepdims=True) + EPS) * g_ref[...]).astype(u_ref.dtype)
        u_ref[...] = u
        gate, up = _nt(u, wg_ref[...]), _nt(u, wu_ref[...])
        gate_ref[...], up_ref[...] = gate.astype(gate_ref.dtype), up.astype(up_ref.dtype)
        z_ref[...] = (gate * _sigmoid(gate) * up).astype(z_ref.dtype)

    rows = lambda n: pl.BlockSpec((tm, n), lambda i: (i, 0))
    fixed = _fixed_spec
    return _call("ffn_hidden", body, grid=(T // tm,),
                 in_specs=[rows(D), fixed(w_out), rows(D), fixed(gain), fixed(w_gate_t), fixed(w_up_t)],
                 out_specs=[rows(D), rows(D), rows(F), rows(F), rows(F)],
                 out_shape=[jax.ShapeDtypeStruct((T, D), F32), jax.ShapeDtypeStruct((T, D), MXU_DTYPE)]
                 + [jax.ShapeDtypeStruct((T, F), SAVED_DTYPE)] * 2 + [jax.ShapeDtypeStruct((T, F), MXU_DTYPE)],
                 args=[merged, w_out, x, gain, w_gate_t, w_up_t], semantics=("parallel",), comm=comm)


def _in_proj(x, gain, w_in_t, b_in, *, tm, comm=None):
    T, D = x.shape
    bounds = [sum(IN_SPLITS[:i]) for i in range(len(IN_SPLITS) + 1)]

    def body(x_ref, g_ref, w_ref, b_ref, u_ref, *piece_refs):
        xv = x_ref[...]
        r = lax.rsqrt(jnp.mean(xv * xv, axis=-1, keepdims=True) + EPS)
        u = (xv * r * g_ref[...]).astype(u_ref.dtype)
        u_ref[...] = u
        for o_ref, lo, hi in zip(piece_refs, bounds[:-1], bounds[1:]):
            o_ref[...] = (_nt(u, w_ref[lo:hi, :]) + b_ref[:, lo:hi]).astype(o_ref.dtype)

    rows = lambda n: pl.BlockSpec((tm, n), lambda i: (i, 0))
    fixed = _fixed_spec
    dtypes = (MXU_DTYPE, MXU_DTYPE, F32, F32)
    return _call("in_proj", body, grid=(T // tm,),
                 in_specs=[rows(D), fixed(gain), fixed(w_in_t), fixed(b_in)],
                 out_specs=[rows(D)] + [rows(n) for n in IN_SPLITS],
                 out_shape=[jax.ShapeDtypeStruct((T, D), MXU_DTYPE)]
                 + [jax.ShapeDtypeStruct((T, n), dt) for n, dt in zip(IN_SPLITS, dtypes)],
                 args=[x, gain, w_in_t, b_in], semantics=("parallel",), comm=comm)


def _row_spec(tm, n):
    return pl.BlockSpec((tm, n), lambda i: (i, 0))


def _fixed_spec(a):
    return pl.BlockSpec(a.shape, lambda i: (0,) * a.ndim, pipeline_mode=pl.Buffered(1))


def _partials_spec(n):
    return pl.BlockSpec((8, n), lambda i: (i, 0))


def _ffn_tail(z, w_down, h1, target, gain, gate, up, *, tm):
    (T, F), D = z.shape, h1.shape[1]

    def body(z_ref, w_ref, h_ref, t_ref, g_ref, gate_ref, up_ref, dh_ref, dhb_ref, dgate_ref, dup_ref, dg_ref, l_ref):
        h2 = h_ref[...] + _nn(z_ref[...], w_ref[...])
        r = lax.rsqrt(jnp.mean(h2 * h2, axis=-1, keepdims=True) + EPS)
        xhat = h2 * r
        err = xhat * g_ref[...] - t_ref[...]
        part = 0.5 * jnp.sum(jnp.sum(err * err, axis=-1, keepdims=True), axis=0, keepdims=True) / D
        dy = err / D
        dxh = dy * g_ref[...]
        dh2 = r * (dxh - xhat * jnp.mean(dxh * xhat, axis=-1, keepdims=True))
        dh_ref[...] = dh2
        dhb = dh2.astype(dhb_ref.dtype)
        dhb_ref[...] = dhb
        dg_ref[...] = jnp.broadcast_to(jnp.sum(dy * xhat, axis=0, keepdims=True), dg_ref.shape)
        l_ref[...] = jnp.broadcast_to(part, l_ref.shape)
        dz = _nt(dhb, w_ref[...])
        gv, upv = gate_ref[...].astype(F32), up_ref[...].astype(F32)
        s = _sigmoid(gv)
        dgate_ref[...] = (dz * upv * (s * (1.0 + gv * (1.0 - s)))).astype(dgate_ref.dtype)
        dup_ref[...] = (dz * (gv * s)).astype(dup_ref.dtype)

    low = lambda n: jax.ShapeDtypeStruct((T, n), MXU_DTYPE)
    part = jax.ShapeDtypeStruct((8 * (T // tm), D), F32)
    return pl.pallas_call(
        body, name="ffn_tail", grid=(T // tm,),
        in_specs=[_row_spec(tm, F), _fixed_spec(w_down), _row_spec(tm, D), _row_spec(tm, D), _fixed_spec(gain),
                  _row_spec(tm, F), _row_spec(tm, F)],
        out_specs=[_row_spec(tm, D), _row_spec(tm, D), _row_spec(tm, F), _row_spec(tm, F), _partials_spec(D),
                   _partials_spec(D)],
        out_shape=[jax.ShapeDtypeStruct((T, D), F32), low(D), low(F), low(F), part, part],
        compiler_params=_params(("parallel",)),
    )(z, w_down, h1, target, gain, gate, up)


def _ffn_in_bwd(dgate, dup, w_gate_t, w_up_t, h1, gain, dres, *, tm, comm=None):
    (T, F), D = dgate.shape, h1.shape[1]

    def body(dg_ref, du_ref, wg_ref, wu_ref, h_ref, g_ref, r_ref, dh_ref, dhb_ref, dgain_ref):
        d_u2 = _nn(dg_ref[...], wg_ref[...]) + _nn(du_ref[...], wu_ref[...])
        dx, dgain = _rmsnorm_bwd_vals(d_u2, h_ref[...], g_ref[...])
        dh = r_ref[...] + dx
        dh_ref[...] = dh
        dhb_ref[...] = dh.astype(dhb_ref.dtype)
        dgain_ref[...] = jnp.broadcast_to(dgain, dgain_ref.shape)

    return _call("d_ffn_in", body, grid=(T // tm,),
                 in_specs=[_row_spec(tm, F), _row_spec(tm, F), _fixed_spec(w_gate_t), _fixed_spec(w_up_t),
                           _row_spec(tm, D), _fixed_spec(gain), _row_spec(tm, D)],
                 out_specs=[_row_spec(tm, D), _row_spec(tm, D), _partials_spec(D)],
                 out_shape=[jax.ShapeDtypeStruct((T, D), F32), jax.ShapeDtypeStruct((T, D), MXU_DTYPE),
                            jax.ShapeDtypeStruct((8 * (T // tm), D), F32)],
                 args=[dgate, dup, w_gate_t, w_up_t, h1, gain, dres], semantics=("parallel",), comm=comm)


def _in_proj_bwd(pieces, w_in_t, x, gain, dres, *, tm, comm=None):
    T, D = x.shape
    n = len(pieces)

    def body(*refs):
        dps, (w_ref, x_ref, g_ref, r_ref, dx_ref, dgain_ref) = refs[:n], refs[n:]
        d_u = None
        for dp_ref, (dp, first) in zip(dps, pieces):
            term = _nn(dp_ref[...], w_ref[first:first + dp.shape[1], :])
            d_u = term if d_u is None else d_u + term
        dx, dgain = _rmsnorm_bwd_vals(d_u, x_ref[...], g_ref[...])
        dx_ref[...] = r_ref[...] + dx
        dgain_ref[...] = jnp.broadcast_to(dgain, dgain_ref.shape)

    return _call("d_u", body, grid=(T // tm,),
                 in_specs=[_row_spec(tm, dp.shape[1]) for dp, _ in pieces]
                 + [_fixed_spec(w_in_t), _row_spec(tm, D), _fixed_spec(gain), _row_spec(tm, D)],
                 out_specs=[_row_spec(tm, D), _partials_spec(D)],
                 out_shape=[jax.ShapeDtypeStruct((T, D), F32), jax.ShapeDtypeStruct((8 * (T // tm), D), F32)],
                 args=[dp for dp, _ in pieces] + [w_in_t, x, gain, dres], semantics=("parallel",), comm=comm)


def _merge_fwd(y_a, y_b, w_a, w_b, gates, *, tm):
    T, D = y_a.shape

    def body(ya_ref, yb_ref, wa_ref, wb_ref, ga_ref, gb_ref, pa_ref, pb_ref, m_ref):
        pa, pb = _nn(ya_ref[...], wa_ref[...]), _nn(yb_ref[...], wb_ref[...])
        pa_ref[...], pb_ref[...] = pa.astype(pa_ref.dtype), pb.astype(pb_ref.dtype)
        m_ref[...] = (_sigmoid(ga_ref[...]) * pa + _sigmoid(gb_ref[...]) * pb).astype(m_ref.dtype)

    rows = pl.BlockSpec((tm, D), lambda i: (i, 0))
    whole = pl.BlockSpec((D, D), lambda i: (0, 0), pipeline_mode=pl.Buffered(1))
    return pl.pallas_call(
        body, name="branch_merge", grid=(T // tm,),
        in_specs=[rows, rows, whole, whole, rows, pl.BlockSpec((tm, D), lambda i: (i, 1))], out_specs=[rows] * 3,
        out_shape=[jax.ShapeDtypeStruct((T, D), SAVED_DTYPE)] * 2 + [jax.ShapeDtypeStruct((T, D), MXU_DTYPE)],
        compiler_params=_params(("parallel",)),
    )(y_a, y_b, w_a, w_b, gates, gates)


def _merge_bwd(dh, w_out, w_a, w_b, p_a, p_b, gates, *, tm, d_in_width, first):
    T, D = dh.shape

    def body(dh_ref, wo_ref, wa_ref, wb_ref, pa_ref, pb_ref, ga_ref, gb_ref, dpa_ref, dpb_ref, dya_ref, dyb_ref,
             din_ref):
        dm = _nt(dh_ref[...], wo_ref[...])
        sa, sb = _sigmoid(ga_ref[...]), _sigmoid(gb_ref[...])
        dpa, dpb = (dm * sa).astype(dpa_ref.dtype), (dm * sb).astype(dpb_ref.dtype)
        dpa_ref[...], dpb_ref[...] = dpa, dpb
        din_ref[:, :D] = (dm * pa_ref[...].astype(F32) * sa * (1.0 - sa)).astype(din_ref.dtype)
        din_ref[:, D:] = (dm * pb_ref[...].astype(F32) * sb * (1.0 - sb)).astype(din_ref.dtype)
        dya_ref[...] = _nt(dpa, wa_ref[...]).astype(dya_ref.dtype)
        dyb_ref[...] = _nt(dpb, wb_ref[...]).astype(dyb_ref.dtype)

    rows = pl.BlockSpec((tm, D), lambda i: (i, 0))
    whole = pl.BlockSpec((D, D), lambda i: (0, 0), pipeline_mode=pl.Buffered(1))
    low = jax.ShapeDtypeStruct((T, D), MXU_DTYPE)
    return pl.pallas_call(
        body, name="d_branch_merge", grid=(T // tm,),
        in_specs=[rows, whole, whole, whole, rows, rows, rows, pl.BlockSpec((tm, D), lambda i: (i, 1))],
        out_specs=[rows] * 4 + [pl.BlockSpec((pl.Element(tm), pl.Element(2 * D)), lambda i: (
            pl.multiple_of(i * tm, ROW_ALIGN), first))],
        out_shape=[low] * 3 + [jax.ShapeDtypeStruct((T, D), F32), jax.ShapeDtypeStruct((T, d_in_width), MXU_DTYPE)],
        compiler_params=_params(("parallel",)),
    )(dh, w_out, w_a, w_b, p_a, p_b, gates, gates)


def _colsum_partials(p):
    return jnp.sum(p.reshape(-1, 8, p.shape[-1])[:, 0, :], axis=0, keepdims=True)


def _rmsnorm_bwd_vals(dy, xin, g):
    rstd = lax.rsqrt(jnp.mean(xin * xin, axis=-1, keepdims=True) + EPS)
    xhat = xin * rstd
    dg = jnp.sum(dy * xhat, axis=0, keepdims=True)
    dxh = dy * g
    dx = rstd * (dxh - xhat * jnp.mean(dxh * xhat, axis=-1, keepdims=True))
    return dx, dg


ATTN_SCALE = 1.0 / math.sqrt(HEAD_DIM)
GROUP_LANES = GROUP * ATTN_BLOCK
PAIR = 2 * HEAD_DIM


def _attn_mask():
    kj = lax.broadcasted_iota(jnp.int32, (ATTN_BLOCK, GROUP_LANES), 0)
    qi = lax.broadcasted_iota(jnp.int32, (ATTN_BLOCK, GROUP_LANES), 1) & (ATTN_BLOCK - 1)
    return kj <= qi


def _heads_transposed(ref, g, scale=None):
    parts = []
    for a in range(GROUP // 2):
        lo = (g * GROUP // 2 + a) * PAIR
        pair = ref[:, lo:lo + PAIR].astype(F32)
        pair = (pair if scale is None else pair * scale).T
        parts += [pair[:HEAD_DIM], pair[HEAD_DIM:]]
    return jnp.concatenate(parts, axis=1).astype(MXU_DTYPE)


def _heads_back(ref, g, vt):
    for a in range(GROUP // 2):
        lo = (g * GROUP // 2 + a) * PAIR
        pair = jnp.concatenate([vt[:, (2 * a) * ATTN_BLOCK:(2 * a + 1) * ATTN_BLOCK],
                                vt[:, (2 * a + 1) * ATTN_BLOCK:(2 * a + 2) * ATTN_BLOCK]], axis=0)
        ref[:, lo:lo + PAIR] = pair.T.astype(ref.dtype)


def _kv_parts(kv_ref, g):
    ks = slice(g * HEAD_DIM, (g + 1) * HEAD_DIM)
    vs = slice(KV_WIDTH + g * HEAD_DIM, KV_WIDTH + (g + 1) * HEAD_DIM)
    return kv_ref[:, ks].astype(MXU_DTYPE), kv_ref[:, vs].astype(MXU_DTYPE)


def _sink_rows(sinks):
    return jnp.repeat(sinks.reshape(KV_HEADS, GROUP), ATTN_BLOCK, axis=1)


def _attn_fwd(pq, pkv, sinks, comm=None):
    T = pq.shape[0]
    nb = T // ATTN_BLOCK

    def body(q_ref, kvc_ref, kvp_ref, s_ref, y_ref, lse_ref):
        mask_c = _attn_mask()
        has_prev = pl.program_id(0) > 0
        for g in range(KV_HEADS):
            (kc, vc), (kp, vp) = _kv_parts(kvc_ref, g), _kv_parts(kvp_ref, g)
            qt = _heads_transposed(q_ref, g, ATTN_SCALE)
            s = jnp.where(mask_c, _nn(kc, qt), jnp.where(has_prev, _nn(kp, qt), NEG_INF))
            sink = s_ref[g:g + 1, :]
            m = jnp.maximum(jnp.max(s, axis=0, keepdims=True), sink)
            p = jnp.exp(s - m)
            den = jnp.sum(p, axis=0, keepdims=True) + jnp.exp(sink - m)
            pc = jnp.where(mask_c, p, 0.0)
            _heads_back(y_ref, g, (_tn(vc, pc) + _tn(vp, p - pc)) / den)
            lse = m + jnp.log(den)
            for i in range(GROUP):
                lse_ref[g * GROUP + i:g * GROUP + i + 1, :] = lse[:, i * ATTN_BLOCK:(i + 1) * ATTN_BLOCK]

    return _call(
        "attn_fwd", body, grid=(nb,),
        in_specs=[pl.BlockSpec((ATTN_BLOCK, D_MODEL), lambda n: (n, 0)),
                  pl.BlockSpec((ATTN_BLOCK, 2 * KV_WIDTH), lambda n: (n, 0)),
                  pl.BlockSpec((ATTN_BLOCK, 2 * KV_WIDTH), lambda n: (jnp.maximum(n - 1, 0), 0)),
                  pl.BlockSpec((KV_HEADS, GROUP_LANES), lambda n: (0, 0))],
        out_specs=[pl.BlockSpec((ATTN_BLOCK, D_MODEL), lambda n: (n, 0)),
                   pl.BlockSpec((Q_HEADS, ATTN_BLOCK), lambda n: (0, n))],
        out_shape=[jax.ShapeDtypeStruct((T, D_MODEL), MXU_DTYPE), jax.ShapeDtypeStruct((Q_HEADS, T), F32)],
        args=[pq, pkv, pkv, _sink_rows(sinks)], semantics=("parallel",), comm=comm)


def _attn_bwd(pq, pkv, sinks, lse, dy, d_in, comm=None):
    T = pq.shape[0]
    nb = T // ATTN_BLOCK
    cur = lambda n: (jnp.minimum(n, nb - 1), 0)
    done = D_MODEL + 2 * KV_WIDTH

    def body(q_ref, kvc_ref, kvp_ref, s_ref, lse_ref, dy_ref, _, out_ref, ds_ref, carry, top, bot, dq_ref):
        n = pl.program_id(0)

        @pl.when(n == 0)
        def _():
            carry[...] = jnp.zeros_like(carry)
            dq_ref[...] = jnp.zeros_like(dq_ref)
            ds_ref[...] = jnp.zeros_like(ds_ref)

        out_ref[:, :D_MODEL] = dq_ref[...]

        @pl.when(n < nb)
        def _():
            mask_c = _attn_mask()
            valid = jnp.logical_or(mask_c, n > 0)
            for g in range(KV_HEADS):
                ks = slice(g * HEAD_DIM, (g + 1) * HEAD_DIM)
                vs = slice(KV_WIDTH + g * HEAD_DIM, KV_WIDTH + (g + 1) * HEAD_DIM)
                (kc, vc), (kp, vp) = _kv_parts(kvc_ref, g), _kv_parts(kvp_ref, g)
                qt = _heads_transposed(q_ref, g, ATTN_SCALE)
                dot = _heads_transposed(dy_ref, g)
                lse = jnp.concatenate([lse_ref[g * GROUP + i:g * GROUP + i + 1, :] for i in range(GROUP)], axis=1)
                p = jnp.where(valid, jnp.exp(jnp.where(mask_c, _nn(kc, qt), _nn(kp, qt)) - lse), 0.0)
                dp = jnp.where(mask_c, _nn(vc, dot), _nn(vp, dot))
                delta = jnp.sum(p * dp, axis=0, keepdims=True)
                ds = p * (dp - delta)
                ds_c, p_c = jnp.where(mask_c, ds, 0.0), jnp.where(mask_c, p, 0.0)
                ds_p, p_p = ds - ds_c, p - p_c
                _heads_back(dq_ref, g, (_tn(kc, ds_c) + _tn(kp, ds_p)) * ATTN_SCALE)
                bot[:, ks], bot[:, vs] = _nt(ds_c, qt), _nt(p_c, dot)
                top[:, ks], top[:, vs] = _nt(ds_p, qt), _nt(p_p, dot)
                ds_ref[g:g + 1, :] -= jnp.exp(s_ref[g:g + 1, :] - lse) * delta
            out_ref[:, D_MODEL:] = (carry[...] + top[...]).astype(out_ref.dtype)
            carry[...] = bot[...]

        @pl.when(n == nb)
        def _():
            out_ref[:, D_MODEL:] = carry[...].astype(out_ref.dtype)

    return _call(
        "attn_bwd", body, grid=(nb + 1,),
        in_specs=[pl.BlockSpec((ATTN_BLOCK, D_MODEL), cur),
                  pl.BlockSpec((ATTN_BLOCK, 2 * KV_WIDTH), cur),
                  pl.BlockSpec((ATTN_BLOCK, 2 * KV_WIDTH), lambda n: (jnp.maximum(jnp.minimum(n, nb - 1) - 1, 0), 0)),
                  pl.BlockSpec((KV_HEADS, GROUP_LANES), lambda n: (0, 0)),
                  pl.BlockSpec((Q_HEADS, ATTN_BLOCK), lambda n: (0, jnp.minimum(n, nb - 1))),
                  pl.BlockSpec((ATTN_BLOCK, D_MODEL), cur), HBM_SPEC],
        out_specs=[pl.BlockSpec((ATTN_BLOCK, done), lambda n: (jnp.maximum(n - 1, 0), 0)),
                   pl.BlockSpec((KV_HEADS, GROUP_LANES), lambda n: (0, 0))],
        out_shape=[jax.ShapeDtypeStruct(d_in.shape, d_in.dtype), jax.ShapeDtypeStruct((KV_HEADS, GROUP_LANES), F32)],
        scratch=[pltpu.VMEM((ATTN_BLOCK, 2 * KV_WIDTH), F32)] * 3 + [pltpu.VMEM((ATTN_BLOCK, D_MODEL), MXU_DTYPE)],
        args=[pq, pkv, pkv, _sink_rows(sinks), lse, dy, d_in], semantics=("arbitrary",), comm=comm, aliases={6: 0})


def _lower_bound(l):
    m = jnp.maximum(l[0:1], l[1:2])
    e0, e1 = jnp.exp(l[0:1] - m), jnp.exp(l[1:2] - m)
    return e0 / (e0 + e1)


def _tri(lower):
    r = lax.broadcasted_iota(jnp.int32, (CHUNK, CHUNK), 0)
    c = lax.broadcasted_iota(jnp.int32, (CHUNK, CHUNK), 1)
    return (r >= c) if lower else (c >= r)


def _chunk_sum(mask, v):
    ones = mask.astype(BF16)
    hi = v.astype(BF16)
    rest = v - hi.astype(F32)
    mid = rest.astype(BF16)
    lo = (rest - mid.astype(F32)).astype(BF16)
    part = lambda t: lax.dot_general(ones, t, (((1,), (0,)), ((), ())), preferred_element_type=F32)
    return part(hi) + part(mid) + part(lo)


def _hgrn_chunk_inputs(hq, hf, lb, causal):
    half_t = 0.5 * jnp.tanh(0.5 * hf)
    sg, sgn = 0.5 + half_t, 0.5 - half_t
    f = lb + (1.0 - lb) * sg
    kk = (1.0 - lb) * sgn
    sq = _sigmoid(hq)
    q = hq * sq
    b = _chunk_sum(causal, jnp.log(f))
    bm, bl = b[CHUNK // 2 - 1:CHUNK // 2, :], b[CHUNK - 1:CHUNK, :]
    e_qm, e_km = jnp.exp(b - bm), jnp.exp(bm - b)
    e_qs, e_kl = e_qm * jnp.exp(bm), e_km * jnp.exp(bl - bm)
    return dict(sg=sg, sgn=sgn, f=f, kk=kk, sq=sq, q=q, e_qm=e_qm, e_km=e_km, e_qs=e_qs, e_kl=e_kl,
                qm=q * e_qm, km=kk * e_km, qs=q * e_qs, kl=kk * e_kl, el=jnp.exp(bl))


def _hgrn_fwd(ph, lb_logits, norm_g, comm=None):
    T = ph.shape[0]
    nblk, cpb = T // HGRN_TOKENS, HGRN_TOKENS // CHUNK
    col = lambda c: pl.BlockSpec((HGRN_TOKENS, D_MODEL), functools.partial(lambda i, c: (i, c), c=c))

    def body(hq_ref, hf_ref, hi_ref, hg_ref, l_ref, ng_ref, y_ref, o_ref, st_ref, s_ref):
        @pl.when(pl.program_id(0) == 0)
        def _():
            s_ref[...] = jnp.zeros_like(s_ref)

        lb = _lower_bound(l_ref[...])
        causal = _tri(True)
        for c in range(cpb):
            rows = slice(c * CHUNK, (c + 1) * CHUNK)
            t = _hgrn_chunk_inputs(hq_ref[rows, :], hf_ref[rows, :], lb, causal)
            qm, km, qs, kl = (t[n].astype(MXU_DTYPE) for n in ("qm", "km", "qs", "kl"))
            v = hi_ref[rows, :].astype(MXU_DTYPE)
            for h in range(HGRN_HEADS):
                ls = slice(h * HGRN_K, (h + 1) * HGRN_K)
                st = s_ref[h]
                st_ref[c, ls, :] = st
                a = jnp.where(causal, _nt(qm[:, ls], km[:, ls]), 0.0)
                o_ref[rows, ls] = _nn(a, v[:, ls]) + _nt(qs[:, ls], st)
                s_ref[h] = t["el"][:, ls] * st + _tn(v[:, ls], kl[:, ls])
        for h in range(HGRN_HEADS):
            ls = slice(h * HGRN_K, (h + 1) * HGRN_K)
            o = o_ref[:, ls]
            r = lax.rsqrt(jnp.mean(o * o, axis=-1, keepdims=True) + EPS)
            y_ref[:, ls] = (o * r * ng_ref[:, ls] * _sigmoid(hg_ref[:, ls])).astype(y_ref.dtype)

    return _call(
        "hgrn_fwd", body, grid=(nblk,),
        in_specs=[col(0), col(1), col(2), col(3),
                  pl.BlockSpec((2, D_MODEL), lambda i: (0, 0)), pl.BlockSpec((1, D_MODEL), lambda i: (0, 0))],
        out_specs=[pl.BlockSpec((HGRN_TOKENS, D_MODEL), lambda i: (i, 0)),
                   pl.BlockSpec((HGRN_TOKENS, D_MODEL), lambda i: (i, 0)),
                   pl.BlockSpec((cpb, D_MODEL, HGRN_K), lambda i: (i, 0, 0))],
        out_shape=[jax.ShapeDtypeStruct((T, D_MODEL), MXU_DTYPE), jax.ShapeDtypeStruct((T, D_MODEL), F32),
                   jax.ShapeDtypeStruct((T // CHUNK, D_MODEL, HGRN_K), F32)],
        scratch=[pltpu.VMEM((HGRN_HEADS, HGRN_K, HGRN_K), F32)],
        args=[ph, ph, ph, ph, lb_logits, norm_g], semantics=("arbitrary",), comm=comm)


def _hgrn_bwd(ph, o_raw, states, dy, lb_logits, norm_g, d_in, first, comm=None):
    T = ph.shape[0]
    nblk, cpb = T // HGRN_TOKENS, HGRN_TOKENS // CHUNK
    rev = lambda i: nblk - 1 - i
    col = lambda c: pl.BlockSpec((HGRN_TOKENS, D_MODEL), functools.partial(lambda i, c: (rev(i), c), c=c))
    tok = pl.BlockSpec((HGRN_TOKENS, D_MODEL), lambda i: (rev(i), 0))

    def body(hq_ref, hf_ref, hi_ref, hg_ref, o_ref, st_ref, dy_ref, l_ref, ng_ref, _,
             dph_ref, dng_ref, dl_ref, dst_ref, dlb_ref, do_s, dqm_s, dkm_s, dqs_s, dkl_s, dv_s, del_s):
        i = pl.program_id(0)

        @pl.when(i == 0)
        def _():
            dst_ref[...] = jnp.zeros_like(dst_ref)
            dlb_ref[...] = jnp.zeros_like(dlb_ref)
            dng_ref[...] = jnp.zeros_like(dng_ref)

        lb = _lower_bound(l_ref[...])
        causal, anti = _tri(True), _tri(False)
        row = lax.broadcasted_iota(jnp.int32, (CHUNK, D_MODEL), 0)
        for c in reversed(range(cpb)):
            rows = slice(c * CHUNK, (c + 1) * CHUNK)
            hq = hq_ref[rows, :]
            t = _hgrn_chunk_inputs(hq, hf_ref[rows, :], lb, causal)
            sgg = _sigmoid(hg_ref[rows, :])
            dyv = dy_ref[rows, :]
            for h in range(HGRN_HEADS):
                ls = slice(h * HGRN_K, (h + 1) * HGRN_K)
                o = o_ref[rows, ls]
                r = lax.rsqrt(jnp.mean(o * o, axis=-1, keepdims=True) + EPS)
                nrm = o * r
                g_h = sgg[:, ls]
                dph_ref[rows, 3 * D_MODEL + h * HGRN_K:3 * D_MODEL + (h + 1) * HGRN_K] = (
                    dyv[:, ls] * nrm * ng_ref[:, ls] * g_h * (1.0 - g_h)).astype(dph_ref.dtype)
                dyg = dyv[:, ls] * g_h
                dng_ref[:, ls] += jnp.sum(dyg * nrm, axis=0, keepdims=True)
                dn = dyg * ng_ref[:, ls]
                do_s[:, ls] = r * (dn - nrm * jnp.mean(dn * nrm, axis=-1, keepdims=True))
            qm, km, qs, kl = (t[n].astype(MXU_DTYPE) for n in ("qm", "km", "qs", "kl"))
            v = hi_ref[rows, :].astype(MXU_DTYPE)
            do = do_s[...].astype(MXU_DTYPE)
            for h in range(HGRN_HEADS):
                ls = slice(h * HGRN_K, (h + 1) * HGRN_K)
                st = st_ref[c, ls, :]
                dst = dst_ref[h]
                a = jnp.where(causal, _nt(qm[:, ls], km[:, ls]), 0.0)
                da = jnp.where(causal, _nt(do[:, ls], v[:, ls]), 0.0)
                dv_s[:, ls] = _tn(a, do[:, ls]) + _nt(kl[:, ls], dst)
                dkl_s[:, ls] = _nn(v[:, ls], dst)
                dqs_s[:, ls] = _nn(do[:, ls], st)
                del_s[:, ls] = jnp.sum(dst * st, axis=0, keepdims=True)
                dst_ref[h] = _tn(do[:, ls], qs[:, ls]) + t["el"][:, ls] * dst
                dqm_s[:, ls] = _nn(da, km[:, ls])
                dkm_s[:, ls] = _tn(da, qm[:, ls])
            dqm, dkm, dqs, dkl = dqm_s[...], dkm_s[...], dqs_s[...], dkl_s[...]
            dq = dqm * t["e_qm"] + dqs * t["e_qs"]
            dk = dkm * t["e_km"] + dkl * t["e_kl"]
            t_qm, t_km, t_kl = dqm * t["qm"], dkm * t["km"], dkl * t["kl"]
            db = t_qm - t_km + dqs * t["qs"] - t_kl
            db_mid = jnp.sum(t_km - t_qm, axis=0, keepdims=True)
            db_last = jnp.sum(t_kl, axis=0, keepdims=True) + del_s[...] * t["el"]
            db = db + jnp.where(row == CHUNK // 2 - 1, db_mid, 0.0) + jnp.where(row == CHUNK - 1, db_last, 0.0)
            dlogf = _chunk_sum(anti, db)
            sq, sg, sgn, f = t["sq"], t["sg"], t["sgn"], t["f"]
            dph_ref[rows, 0:D_MODEL] = (dq * (sq * (1.0 + hq * (1.0 - sq)))).astype(dph_ref.dtype)
            dph_ref[rows, D_MODEL:2 * D_MODEL] = (
                dlogf * (1.0 - lb) * sg * (1.0 - sg) / f - dk * (1.0 - lb) * sgn * (1.0 - sgn)).astype(dph_ref.dtype)
            dph_ref[rows, 2 * D_MODEL:3 * D_MODEL] = dv_s[...].astype(dph_ref.dtype)
            dlb_ref[...] += jnp.sum(dlogf * (1.0 - sg) / f - dk * sgn, axis=0, keepdims=True)

        @pl.when(i == nblk - 1)
        def _():
            dl0 = dlb_ref[...] * lb * (1.0 - lb)
            dl_ref[0:1, :] = dl0
            dl_ref[1:2, :] = -dl0

    wide = pltpu.VMEM((CHUNK, D_MODEL), F32)
    return _call(
        "hgrn_bwd", body, grid=(nblk,),
        in_specs=[col(0), col(1), col(2), col(3), tok,
                  pl.BlockSpec((cpb, D_MODEL, HGRN_K), lambda i: (rev(i), 0, 0)), tok,
                  pl.BlockSpec((2, D_MODEL), lambda i: (0, 0)), pl.BlockSpec((1, D_MODEL), lambda i: (0, 0)), HBM_SPEC],
        out_specs=[pl.BlockSpec((pl.Element(HGRN_TOKENS), pl.Element(4 * D_MODEL)), lambda i: (
                       pl.multiple_of(rev(i) * HGRN_TOKENS, ROW_ALIGN), first)),
                   pl.BlockSpec((1, D_MODEL), lambda i: (0, 0)), pl.BlockSpec((2, D_MODEL), lambda i: (0, 0))],
        out_shape=[jax.ShapeDtypeStruct(d_in.shape, d_in.dtype), jax.ShapeDtypeStruct((1, D_MODEL), F32),
                   jax.ShapeDtypeStruct((2, D_MODEL), F32)],
        scratch=[pltpu.VMEM((HGRN_HEADS, HGRN_K, HGRN_K), F32), pltpu.VMEM((1, D_MODEL), F32),
                 wide, wide, wide, wide, wide, wide, pltpu.VMEM((1, D_MODEL), F32)],
        args=[ph, ph, ph, ph, o_raw, states, dy, lb_logits, norm_g, d_in], semantics=("arbitrary",), comm=comm,
        aliases={9: 0})


def _local_step(x, target, vec, net):
    T, D = x.shape
    norm_mix_g, b_in, sinks, lb_logits = vec["norm_mix_g"], vec["b_in"], vec["attn_sinks"], vec["hgrn_lb_logits"]
    hgrn_norm_g, norm_ffn_g, norm_final_g = vec["hgrn_norm_g"], vec["norm_ffn_g"], vec["norm_final_g"]
    w_in = net.full("w_in")
    o_q, o_kv, o_h, o_g = (sum(IN_SPLITS[:i]) for i in range(4))
    TM = 512

    names = ("w_branch_attn", "w_branch_hgrn", "w_out")
    (u, pq, pkv, ph, pg), got = _in_proj(x, norm_mix_g, w_in, b_in, tm=256, comm=net.gather(names))
    net.gathered(names, got)
    names = ("w_ffn_gate",)
    (y_attn, lse), got = _attn_fwd(pq, pkv, sinks, comm=net.gather(names))
    net.gathered(names, got)
    names = ("w_ffn_up",)
    (y_hgrn, o_raw, states), got = _hgrn_fwd(ph, lb_logits, hgrn_norm_g, comm=net.gather(names))
    net.gathered(names, got)
    w_ba, w_bh, w_out = net.full("w_branch_attn"), net.full("w_branch_hgrn"), net.full("w_out")
    w_gate, w_up = net.full("w_ffn_gate"), net.full("w_ffn_up")
    ya, yb, merged = _merge_fwd(y_attn, y_hgrn, w_ba, w_bh, pg, tm=TM)

    FT = FFN // 2
    names = ("w_ffn_down",)
    (h1, u2, gpre, up, z), got = _ffn_fwd(merged, w_out, x, norm_ffn_g, w_gate, w_up, tm=256,
                                          comm=net.gather(names))
    net.gathered(names, got)
    w_down = net.full("w_ffn_down")

    dh2, dh2b, dgp, dup, dgf_p, loss_p = _ffn_tail(z, w_down, h1, target, norm_final_g, gpre, up, tm=256)
    loss = jnp.sum(loss_p.reshape(-1, 8, D)[:, 0, 0])
    d_norm_final = _colsum_partials(dgf_p)
    d_w_down = _weight_grad("dw_down", z, dh2b, tm=FT, tk=T)

    names = ("w_ffn_down",)
    (dh1, dh1b, dg2_p), got = _ffn_in_bwd(dgp, dup, w_gate, w_up, h1, norm_ffn_g, dh2, tm=256,
        comm=net.exchange(dict(w_ffn_down=[d_w_down])))
    net.received(names, (), got)
    d_norm_ffn = _colsum_partials(dg2_p)
    d_w_gate = _weight_grad("dw_gate", dgp, u2, tm=FT, tk=T)
    d_w_up = _weight_grad("dw_up", dup, u2, tm=FT, tk=T)

    rows_in = sum(IN_SPLITS)
    dya, dyb, dy_attn, dy_hgrn, d_in = _merge_bwd(dh1b, w_out, w_ba, w_bh, ya, yb, pg, tm=TM, d_in_width=rows_in,
                                                  first=o_g)
    d_w_out = _weight_grad("dw_out", merged, dh1b, tm=1024, tk=1024)
    d_w_ba = _weight_grad("dw_branch_a", y_attn, dya, tm=1024, tk=1024)
    d_w_bh = _weight_grad("dw_branch_b", y_hgrn, dyb, tm=1024, tk=1024)

    names, swap = ("w_ffn_gate",), ("w_ffn_down",)
    (d_in, dsink), got = _attn_bwd(pq, pkv, sinks, lse, dy_attn, d_in,
                                   comm=net.exchange(dict(w_ffn_gate=[d_w_gate]), swap))
    net.received(names, swap, got)
    names, swap = ("w_ffn_up", "w_out", "w_branch_attn", "w_branch_hgrn"), ("w_ffn_gate",)
    (d_in, d_hgrn_norm, d_lb_logits), got = _hgrn_bwd(
        ph, o_raw, states, dy_hgrn, lb_logits, hgrn_norm_g, d_in, o_h,
        comm=net.exchange(dict(w_ffn_up=[d_w_up], w_out=[d_w_out], w_branch_attn=[d_w_ba],
                               w_branch_hgrn=[d_w_bh]), swap))
    net.received(names, swap, got)

    main = rows_in // 1024 * 1024
    *d_w_in, db_main = _weight_grad("dw_in", _Cols(d_in, 0, main), u, tm=1024, tk=T, a_colsum=True,
                                    into=_Into(rows_in, 0, None))
    *d_w_in, db_rest = _weight_grad("dw_in_rest", _Cols(d_in, main, rows_in - main), u, tm=rows_in - main, tk=1024,
                                    a_colsum=True, into=_Into(rows_in, main, d_w_in))
    d_w_in = [tuple(d_w_in)]

    first_level = net.presum_begin("w_in", d_w_in)
    halves = net.presum_end("w_in", [] if first_level is None else _copies_alone("presum_swap_w_in", first_level))
    names, swap = ("w_in",), ("w_ffn_up", "w_out", "w_branch_attn", "w_branch_hgrn")
    (dx, dg1_p), got = _in_proj_bwd([(d_in, 0)], w_in, x, norm_mix_g, dh1, tm=256,
                                    comm=_join(halves, net.swap(swap)))
    net.last = (names, swap, got)
    d_norm_mix = _colsum_partials(dg1_p)
    d_b_in = jnp.concatenate([db_main, db_rest], axis=1)
    vecs = dict(norm_mix_g=d_norm_mix, b_in=d_b_in, attn_sinks=jnp.sum(dsink.reshape(Q_HEADS, ATTN_BLOCK), axis=1).reshape(1, Q_HEADS),
                hgrn_lb_logits=d_lb_logits,
                hgrn_norm_g=d_hgrn_norm, norm_ffn_g=d_norm_ffn, norm_final_g=d_norm_final)
    return loss, dx, vecs


def _place():
    return lax.axis_index("x"), lax.axis_index("y"), lax.axis_index("c")


def _other_chips(x, y):
    return [(1 - x, y), (x, 1 - y), (1 - x, 1 - y)]


def _y_first(copies):
    return [copies[3 * (i // 3) + (1, 0, 2)[i % 3]] for i in range(len(copies))]


def _gather_copies(shards):
    n = len(shards)

    def build(ins, outs, send_sems, recv_sems, local_sems):
        x, y, c = _place()
        mine = 2 * x + y
        local = [pltpu.make_async_copy(ins[w], outs[w].at[mine], local_sems.at[w]) for w in range(n)]
        sends, recvs = [], []
        for w in range(n):
            for k, (px, py) in enumerate(_other_chips(x, y)):
                sem = 3 * w + k
                sends.append(pltpu.make_async_remote_copy(
                    src_ref=ins[w], dst_ref=outs[w].at[mine], send_sem=send_sems.at[sem], recv_sem=recv_sems.at[sem],
                    device_id=(px, py, c), device_id_type=MESH_ID))
                recvs.append(pltpu.make_async_remote_copy(
                    src_ref=ins[w], dst_ref=outs[w].at[2 * px + py], send_sem=send_sems.at[sem],
                    recv_sem=recv_sems.at[sem], device_id=(px, py, c), device_id_type=MESH_ID))
        return sends, recvs, local, _y_first(sends)

    return _Carried(shards, [jax.ShapeDtypeStruct((N_CHIPS,) + s.shape, s.dtype) for s in shards], 3 * n, n, build)


def _grad_copies(stacked):
    n = len(stacked)

    def build(ins, outs, send_sems, recv_sems, local_sems):
        x, y, c = _place()
        sends = []
        for w in range(n):
            for k, (px, py) in enumerate(_other_chips(x, y)):
                sem = 3 * w + k
                sends.append(pltpu.make_async_remote_copy(
                    src_ref=ins[w].at[2 * px + py], dst_ref=outs[w].at[k], send_sem=send_sems.at[sem],
                    recv_sem=recv_sems.at[sem], device_id=(px, py, c), device_id_type=MESH_ID))
        return sends, sends, [], _y_first(sends)

    return _Carried(stacked, [jax.ShapeDtypeStruct((3,) + s.shape[1:], s.dtype) for s in stacked], 3 * n, 0, build)


def _small_copies(small):
    def build(ins, outs, send_sems, recv_sems, local_sems):
        small_ref, all_ref = ins[0], outs[0]
        x, y, c = _place()
        me = 4 * x + 2 * y + c
        sends, recvs = [], []
        for r in range(1, 8):
            px = 1 - x if r & 4 else x
            py = 1 - y if r & 2 else y
            pc = 1 - c if r & 1 else c
            sends.append(pltpu.make_async_remote_copy(
                src_ref=small_ref, dst_ref=all_ref.at[me], send_sem=send_sems.at[r - 1], recv_sem=recv_sems.at[r - 1],
                device_id=(px, py, pc), device_id_type=MESH_ID))
            recvs.append(pltpu.make_async_remote_copy(
                src_ref=small_ref, dst_ref=all_ref.at[4 * px + 2 * py + pc], send_sem=send_sems.at[r - 1],
                recv_sem=recv_sems.at[r - 1], device_id=(px, py, pc), device_id_type=MESH_ID))
        return sends, recvs, [pltpu.make_async_copy(small_ref, all_ref.at[me], local_sems.at[0])]

    return _Carried([small], [jax.ShapeDtypeStruct((8,) + small.shape, small.dtype)], 7, 1, build)


def _gather_by_neighbours(name, shard):
    half = shard.shape[0] // 2
    quarter = half // 2

    def body(in_ref, out_ref, send_sems, recv_sems, local_sem):
        for core in (0, 1):
            @pl.when(lax.axis_index("c") == core)
            def _():
                program(core, in_ref, out_ref, send_sems, recv_sems, local_sem)

    def program(c, in_ref, out_ref, send_sems, recv_sems, local_sem):
        x, y, _ = _place()
        chip = lambda px, py: 2 * px + py
        to_x, to_y, sibling = (1 - x, y, c), (x, 1 - y, c), (x, y, 1 - c)
        x_blk, y_blk, d_blk = chip(1 - x, y), chip(x, 1 - y), chip(1 - x, 1 - y)
        mine, theirs = c * half, (1 - c) * half

        def copy(sem, rows, block, to, src=None):
            place = out_ref.at[block, pl.ds(rows[0], rows[1])]
            return pltpu.make_async_remote_copy(
                src_ref=place if src is None else src, dst_ref=place, send_sem=send_sems.at[sem],
                recv_sem=recv_sems.at[sem], device_id=to, device_id_type=MESH_ID)

        own = pltpu.make_async_copy(in_ref, out_ref.at[chip(x, y)], local_sem)
        own.start()
        my_rows = in_ref.at[pl.ds(mine, half)]
        along_x = dict(send=copy(0, (mine, half), chip(x, y), to_x, src=my_rows),
                       landed=copy(0, (mine, half), x_blk, to_x),
                       onward=[copy(3, (mine + quarter, quarter), x_blk, to_y), copy(4, (mine, half), x_blk, sibling)],
                       diagonal=copy(2, (mine, quarter), d_blk, to_x))
        along_y = dict(send=copy(1, (mine, half), chip(x, y), to_y, src=my_rows),
                       landed=copy(1, (mine, half), y_blk, to_y),
                       onward=[copy(2, (mine, quarter), y_blk, to_x), copy(5, (mine, half), y_blk, sibling)],
                       diagonal=copy(3, (mine + quarter, quarter), d_blk, to_y))
        last = copy(6, (mine, half), d_blk, sibling)

        order = (along_x, along_y) if c == 0 else (along_y, along_x)
        for axis in order:
            axis["send"].start()
        for axis in order:
            axis["landed"].wait_recv()
            for cp in axis["onward"]:
                cp.start()
        for axis in order:
            axis["diagonal"].wait_recv()
        last.start()
        for sem, block in ((4, x_blk), (5, y_blk), (6, d_blk)):
            copy(sem, (theirs, half), block, sibling).wait_recv()
        for cp in [along_x["send"], along_y["send"]] + along_x["onward"] + along_y["onward"] + [last]:
            cp.wait_send()
        own.wait()

    return pl.pallas_call(
        body, name=name, in_specs=[HBM_SPEC], out_specs=HBM_SPEC,
        out_shape=jax.ShapeDtypeStruct((N_CHIPS,) + shard.shape, shard.dtype),
        scratch_shapes=[pltpu.SemaphoreType.DMA((7,)), pltpu.SemaphoreType.DMA((7,)), pltpu.SemaphoreType.DMA(())],
    )(shard)


def _copies_alone(name, comm):
    return _call(name, lambda: None, grid=(), in_specs=[], out_specs=[], out_shape=[], args=[], comm=comm)[1]


class _Net:
    def __init__(self, shards):
        self.shards = shards
        self.whole, self.own, self.theirs, self.sums, self.other = {}, {}, {}, {}, {}
        x, y, _ = _place()
        self.chip = 2 * x + y

    def gather(self, names):
        return _gather_copies([self.shards[n] for n in names])

    def gathered(self, names, got):
        for n, g in zip(names, got):
            self.whole[n] = g.reshape(-1, g.shape[-1])

    def full(self, name):
        return self.whole[name]

    def exchange(self, grads, swap=()):
        stacked = []
        for n, pieces in grads.items():
            (keep, send), = pieces
            self.own[n] = keep
            stacked.append(send.reshape(N_CHIPS, keep.shape[0] // N_CHIPS, send.shape[-1]))
        return _join(_grad_copies(stacked), self.swap(swap))

    def swap(self, names):
        return _sibling_copies([self.sums[n] for n in names]) if names else None

    def presum_begin(self, name, pieces):
        keep = jnp.concatenate([p[0] for p in pieces], axis=0) if len(pieces) > 1 else pieces[0][0]
        send = jnp.concatenate([p[1] for p in pieces], axis=0) if len(pieces) > 1 else pieces[0][1]
        rows = keep.shape[0] // N_CHIPS
        self.held = keep.reshape(N_CHIPS, rows, keep.shape[-1])
        return _half_rows_copies(send.reshape(N_CHIPS, rows, send.shape[-1]))

    def presum_end(self, name, got):
        x, y, c = _place()
        to_send, self.own[name] = _pre_sum("presum_" + name, self.held, got[0], jnp.stack([c, self.chip]))
        return _grad_copies([to_send])

    def received(self, names, swap, got, carried=None):
        self.theirs.update(zip(names, got[:len(names)]))
        self.other.update(zip(swap, got[len(names):]))
        for n in names:
            (self.sums[n],), more = _partial_sum("sum_" + n, self.own[n], self.theirs[n], self.chip, comm=carried)
        return more


def _half_rows_copies(stacked):
    n, rows = stacked.shape[0], stacked.shape[1] // 2

    def build(ins, outs, send_sems, recv_sems, local_sems):
        x, y, c = _place()
        copies = [pltpu.make_async_remote_copy(
            src_ref=ins[0].at[s, pl.ds((1 - c) * rows, rows)], dst_ref=outs[0].at[s], send_sem=send_sems.at[s],
            recv_sem=recv_sems.at[s], device_id=(x, y, 1 - c), device_id_type=MESH_ID) for s in range(n)]
        return copies, copies, []

    return _Carried([stacked], [jax.ShapeDtypeStruct((n, rows, stacked.shape[2]), stacked.dtype)], n, 0, build)


def _pre_sum(name, held, theirs, core_and_chip):
    n, R, C = held.shape
    half = R // 2
    tr = _row_tile(half)
    per_half = half // tr

    def body(place_ref, h_ref, t_ref, send_ref, own_ref):
        total = h_ref[0] + t_ref[0].astype(F32)
        send_ref[0] = total.astype(send_ref.dtype)

        @pl.when(pl.program_id(1) == place_ref[1])
        def _():
            own_ref[...] = total

    return pl.pallas_call(
        body, name=name,
        grid_spec=pltpu.PrefetchScalarGridSpec(
            num_scalar_prefetch=1, grid=(per_half, n),
            in_specs=[pl.BlockSpec((1, tr, C), lambda i, s, place: (s, place[0] * per_half + i, 0)),
                      pl.BlockSpec((1, tr, C), lambda i, s, place: (s, i, 0))],
            out_specs=[pl.BlockSpec((1, tr, C), lambda i, s, place: (s, i, 0)),
                       pl.BlockSpec((tr, C), lambda i, s, place: (i, 0))]),
        out_shape=[jax.ShapeDtypeStruct((n, half, C), MXU_DTYPE), jax.ShapeDtypeStruct((half, C), F32)],
        compiler_params=_params(("arbitrary", "arbitrary")),
    )(core_and_chip, held, theirs)


def _sibling_copies(parts):
    n = len(parts)

    def build(ins, outs, send_sems, recv_sems, local_sems):
        x, y, c = _place()
        copies = [pltpu.make_async_remote_copy(
            src_ref=ins[w], dst_ref=outs[w], send_sem=send_sems.at[w], recv_sem=recv_sems.at[w],
            device_id=(x, y, 1 - c), device_id_type=MESH_ID) for w in range(n)]
        return copies, copies, []

    return _Carried(parts, [jax.ShapeDtypeStruct(p.shape, p.dtype) for p in parts], n, 0, build)


def _row_tile(rows, most=512, sublanes=16):
    return max(t for t in range(sublanes, most + 1, sublanes) if rows % t == 0)


def _partial_sum(name, own, recv, chip, comm=None):
    _, R, C = recv.shape
    tr = _row_tile(R, most=128)

    def body(o_ref, r_ref, p_ref):
        p_ref[...] = ((o_ref[...] + r_ref[0].astype(F32)) + r_ref[1].astype(F32)) + r_ref[2].astype(F32)

    if own.shape[0] != R:
        assert comm is None and own.shape[0] == N_CHIPS * R
        total = pl.pallas_call(
            lambda chip_ref, *refs: body(*refs), name=name,
            grid_spec=pltpu.PrefetchScalarGridSpec(
                num_scalar_prefetch=1, grid=(R // tr,),
                in_specs=[pl.BlockSpec((tr, C), lambda i, chip_ref: (chip_ref[0] * (R // tr) + i, 0)),
                          pl.BlockSpec((3, tr, C), lambda i, chip_ref: (0, i, 0))],
                out_specs=pl.BlockSpec((tr, C), lambda i, chip_ref: (i, 0))),
            out_shape=jax.ShapeDtypeStruct((R, C), F32), compiler_params=_params(("parallel",)),
        )(chip.reshape(1), own, recv)
        return [total], []
    return _call(name, body, grid=(R // tr,),
                 in_specs=[pl.BlockSpec((tr, C), lambda i: (i, 0)), pl.BlockSpec((3, tr, C), lambda i: (0, i, 0))],
                 out_specs=[pl.BlockSpec((tr, C), lambda i: (i, 0))], out_shape=[jax.ShapeDtypeStruct((R, C), F32)],
                 args=[own, recv], semantics=("parallel",), comm=comm)


def _adam_vals(w, g, m, v):
    m = ADAM_B1 * m + (1.0 - ADAM_B1) * g
    v = ADAM_B2 * v + (1.0 - ADAM_B2) * (g * g)
    m_hat = m / (1.0 - ADAM_B1 ** ADAM_STEP)
    v_hat = v / (1.0 - ADAM_B2 ** ADAM_STEP)
    delta = -ADAM_LR * (m_hat / (jnp.sqrt(v_hat) + ADAM_EPS) + ADAM_WD * w)
    return delta, m, v


def _adamw(name, w, m, v, mine, other, comm=None):
    R, C = w.shape
    tr = _row_tile(R)

    def body(w_ref, m_ref, v_ref, s_ref, n_ref, g_ref, d_ref, nm_ref, nv_ref):
        g = s_ref[...] + n_ref[...]
        d, nm, nv = _adam_vals(w_ref[...], g, m_ref[...], v_ref[...])
        g_ref[...], d_ref[...], nm_ref[...], nv_ref[...] = g, d, nm, nv

    spec = pl.BlockSpec((tr, C), lambda i: (i, 0))
    return _call(name, body, grid=(R // tr,), in_specs=[spec] * 5, out_specs=[spec] * 4,
                 out_shape=[jax.ShapeDtypeStruct((R, C), F32)] * 4, args=[w, m, v, mine, other],
                 semantics=("parallel",), comm=comm)


def _adamw_by_halves(name, w, m, v, mine, other, core):
    R, C = w.shape
    tr = _row_tile(R // 2)
    per_half = R // 2 // tr

    def body(c_ref, w_ref, m_ref, v_ref, s_ref, n_ref, g_ref, d_ref, nm_ref, nv_ref):
        g = jnp.where(pl.program_id(0) // per_half == c_ref[0, 0], s_ref[...], n_ref[...])
        d, nm, nv = _adam_vals(w_ref[...], g, m_ref[...], v_ref[...])
        g_ref[...], d_ref[...], nm_ref[...], nv_ref[...] = g, d, nm, nv

    spec = pl.BlockSpec((tr, C), lambda i: (i, 0))
    part = pl.BlockSpec((tr, C), lambda i: (i % per_half, 0))
    return pl.pallas_call(
        body, name=name, grid=(R // tr,),
        in_specs=[pl.BlockSpec(memory_space=pltpu.SMEM), spec, spec, spec, part, part], out_specs=[spec] * 4,
        out_shape=[jax.ShapeDtypeStruct((R, C), F32)] * 4, compiler_params=_params(("parallel",)),
    )(core, w, m, v, mine, other)


SMALL_LAYOUT = dict(norm_mix_g=(0, 1, 1024), b_in=(1, 8, 7424), hgrn_norm_g=(9, 1, 1024), norm_ffn_g=(10, 1, 1024),
                    norm_final_g=(11, 1, 1024), hgrn_lb_logits=(12, 2, 2048), attn_sinks=(14, 1, 16))
SMALL_LOSS_ROW, SMALL_ROWS = 15, 16


def _pack_small(grads, loss):
    rows = [jnp.pad(grads[name].astype(F32).reshape(-1), (0, nrows * D_MODEL - n))
            for name, (_, nrows, n) in SMALL_LAYOUT.items()]
    rows.append(jnp.pad(loss.astype(F32).reshape(1), (0, D_MODEL - 1)))
    return jnp.concatenate(rows).reshape(SMALL_ROWS, D_MODEL)


def _adamw_small(w, m, v, g_all):
    names = list(SMALL_LAYOUT)
    n = len(names)

    def body(a_ref, *refs):
        ins, outs = refs[:3 * n], refs[3 * n:]
        g_all_rows = a_ref[0]
        for dev in range(1, 8):
            g_all_rows = g_all_rows + a_ref[dev]
        for i, name in enumerate(names):
            first, nrows, count = SMALL_LAYOUT[name]
            w_ref, m_ref, v_ref = ins[3 * i:3 * i + 3]
            if w_ref.shape[0] == nrows:
                g = g_all_rows[first:first + nrows, :w_ref.shape[1]]
            else:
                last = count - (nrows - 1) * D_MODEL
                g = jnp.concatenate([g_all_rows[r:r + 1, :] for r in range(first, first + nrows - 1)]
                                    + [g_all_rows[first + nrows - 1:first + nrows, :last]], axis=1)
            d, nm, nv = _adam_vals(w_ref[...], g, m_ref[...], v_ref[...])
            for o_ref, val in zip(outs[4 * i:4 * i + 4], (g, d, nm, nv)):
                o_ref[...] = val
        outs[4 * n][...] = g_all_rows[SMALL_LOSS_ROW:SMALL_LOSS_ROW + 1, 0:1]

    res = pl.pallas_call(
        body, name="adamw_small",
        out_shape=[jax.ShapeDtypeStruct(w[name].shape, F32) for name in names for _ in range(4)]
        + [jax.ShapeDtypeStruct((1, 1), F32)],
    )(g_all, *[t[name] for name in names for t in (w, m, v)])
    return {name: res[4 * i:4 * i + 4] for i, name in enumerate(names)}, res[4 * n]


MATRICES = ("w_in", "w_branch_attn", "w_branch_hgrn", "w_out", "w_ffn_gate", "w_ffn_up", "w_ffn_down")
COLUMN_SHARDED = ("w_in", "w_ffn_gate", "w_ffn_up")
WEIGHTS = ("norm_mix_g", "w_in", "b_in", "attn_sinks", "hgrn_lb_logits", "hgrn_norm_g", "w_branch_attn",
           "w_branch_hgrn", "w_out", "norm_ffn_g", "w_ffn_gate", "w_ffn_up", "w_ffn_down", "norm_final_g")


def kernel(x, norm_mix_g, w_in, b_in, attn_sinks, hgrn_lb_logits, hgrn_norm_g, w_branch_attn, w_branch_hgrn, w_out, norm_ffn_g, w_ffn_gate, w_ffn_up, w_ffn_down, norm_final_g, loss_target, m_norm_mix_g, m_w_in, m_b_in, m_attn_sinks, m_hgrn_lb_logits, m_hgrn_norm_g, m_w_branch_attn, m_w_branch_hgrn, m_w_out, m_norm_ffn_g, m_w_ffn_gate, m_w_ffn_up, m_w_ffn_down, m_norm_final_g, v_norm_mix_g, v_w_in, v_b_in, v_attn_sinks, v_hgrn_lb_logits, v_hgrn_norm_g, v_w_branch_attn, v_w_branch_hgrn, v_w_out, v_norm_ffn_g, v_w_ffn_gate, v_w_ffn_up, v_w_ffn_down, v_norm_final_g):
    given = dict(locals())
    w = {n: given[n] for n in WEIGHTS}
    m = {n: given["m_" + n] for n in WEIGHTS}
    v = {n: given["v_" + n] for n in WEIGHTS}

    block = lambda a, n: jnp.transpose(a[0]) if n in COLUMN_SHARDED else a[0]
    unblock = lambda a, n: (jnp.transpose(a) if n in COLUMN_SHARDED else a)[None]
    net = _Net({n: block(w[n], n).astype(MXU_DTYPE) for n in MATRICES})
    net.gathered(("w_in",), [_gather_by_neighbours("gather_w_in", net.shards["w_in"])])
    vec = dict(norm_mix_g=norm_mix_g, b_in=b_in, attn_sinks=attn_sinks, hgrn_lb_logits=hgrn_lb_logits,
               hgrn_norm_g=hgrn_norm_g, norm_ffn_g=norm_ffn_g, norm_final_g=norm_final_g.reshape(1, D_MODEL))
    loss_part, dx, d_vecs = _local_step(x[0], loss_target[0], vec, net)

    small_all, = net.received(*net.last, carried=_small_copies(_pack_small(d_vecs, loss_part)))
    grads, deltas, new_m, new_v = {}, {}, {}, {}
    for n in ("w_ffn_down", "w_ffn_gate", "w_ffn_up", "w_out", "w_branch_attn", "w_branch_hgrn"):
        res, got = _adamw("adamw_" + n, block(w[n], n), block(m[n], n), block(v[n], n), net.sums[n], net.other[n],
                          comm=net.swap(("w_in",)) if n == "w_ffn_down" else None)
        if n == "w_ffn_down":
            net.other["w_in"], = got
        grads[n], deltas[n], new_m[n], new_v[n] = (unblock(r, n) for r in res)
    n = "w_in"
    res = _adamw_by_halves("adamw_" + n, block(w[n], n), block(m[n], n), block(v[n], n), net.sums[n], net.other[n],
                           _place()[2].reshape(1, 1))
    grads[n], deltas[n], new_m[n], new_v[n] = (unblock(r, n) for r in res)
    rows = lambda t: {n: t[n].reshape(-1, t[n].shape[-1]) for n in SMALL_LAYOUT}
    res, loss = _adamw_small(rows(w), rows(m), rows(v), small_all)
    for n, four in res.items():
        grads[n], deltas[n], new_m[n], new_v[n] = (r.reshape(w[n].shape) for r in four)
    loss = loss.reshape(())
    return (loss, dx[None], *[grads[n] for n in WEIGHTS], *[deltas[n] for n in WEIGHTS],
            *[new_m[n] for n in WEIGHTS], *[new_v[n] for n in WEIGHTS])
```

```python
import collections
import functools
import math

import jax
import jax.numpy as jnp
from jax import lax
from jax.experimental import pallas as pl
from jax.experimental.pallas import tpu as pltpu

F32 = jnp.float32
BF16 = jnp.bfloat16
MXU_DTYPE = jnp.bfloat16
SAVED_DTYPE = jnp.bfloat16
MESH_ID = pl.DeviceIdType.MESH

D_MODEL = 1024
HEAD_DIM = 64
Q_HEADS = 16
KV_HEADS = 2
GROUP = Q_HEADS // KV_HEADS
KV_WIDTH = KV_HEADS * HEAD_DIM
ATTN_BLOCK = 128
HGRN_HEADS = 8
HGRN_K = 128
CHUNK = 64
HGRN_TOKENS = 256
FFN = 2816
IN_SPLITS = (1024, 256, 4096, 2048)
EPS = 1e-6
NEG_INF = -1e30
ADAM_LR, ADAM_B1, ADAM_B2, ADAM_EPS, ADAM_WD, ADAM_STEP = 0.001, 0.9, 0.999, 1e-08, 0.01, 10
N_CHIPS = 4
VMEM_LIMIT = 60 * 1024 * 1024
ROW_ALIGN = 16


def _params(sem=None):
    return pltpu.CompilerParams(dimension_semantics=sem, vmem_limit_bytes=VMEM_LIMIT)


def _sigmoid(v):
    return 0.5 * jnp.tanh(0.5 * v) + 0.5


def _dot(a, b, dims):
    return lax.dot_general(a.astype(MXU_DTYPE), b.astype(MXU_DTYPE), (dims, ((), ())),
                           preferred_element_type=F32)


def _nn(a, b):
    return _dot(a, b, ((1,), (0,)))


def _nt(a, b):
    return _dot(a, b, ((1,), (1,)))


def _tn(a, b):
    return _dot(a, b, ((0,), (0,)))


HBM_SPEC = pl.BlockSpec(memory_space=pl.ANY)


class _Carried:
    def __init__(self, arrays, out_shapes, n_remote, n_local, build):
        self.parts = [(len(arrays), len(out_shapes), build)]
        self.arrays, self.out_shapes = list(arrays), list(out_shapes)
        self.scratch = [pltpu.SemaphoreType.DMA((n_remote,)), pltpu.SemaphoreType.DMA((n_remote,)),
                        pltpu.SemaphoreType.DMA((max(n_local, 1),))]

    def __add__(self, other):
        both = _Carried([], [], 1, 0, None)
        both.parts = self.parts + other.parts
        both.arrays, both.out_shapes = self.arrays + other.arrays, self.out_shapes + other.out_shapes
        both.scratch = self.scratch + other.scratch
        return both

    def _built(self, ins, outs, sems):
        for p, (ni, no, build) in enumerate(self.parts):
            yield build(ins[:ni], outs[:no], *sems[3 * p:3 * p + 3])
            ins, outs = ins[ni:], outs[no:]

    def start(self, ins, outs, sems):
        core = lax.axis_index("c")
        for sends, _, local, *other_order in self._built(ins, outs, sems):
            for cp in local:
                cp.start()
            if not other_order:
                for cp in sends:
                    cp.start()
                continue

            @pl.when(core == 0)
            def _():
                for cp in sends:
                    cp.start()

            @pl.when(core == 1)
            def _():
                for cp in other_order[0]:
                    cp.start()

    def wait(self, ins, outs, sems):
        for sends, recvs, local, *_ in self._built(ins, outs, sems):
            for cp in recvs:
                cp.wait_recv()
            for cp in sends:
                cp.wait_send()
            for cp in local:
                cp.wait()


def _join(*comms):
    comms = [c for c in comms if c is not None]
    return functools.reduce(lambda a, b: a + b, comms) if comms else None


def _call(name, body, *, grid, in_specs, out_specs, out_shape, args, scratch=(), semantics=None, comm=None,
          aliases=None):
    n_in, n_out, n_scr = len(in_specs), len(out_specs), len(scratch)
    aliases = aliases or {}
    if comm is None:
        res = pl.pallas_call(body, name=name, grid=grid, in_specs=in_specs, out_specs=out_specs, out_shape=out_shape,
                             scratch_shapes=list(scratch), input_output_aliases=aliases,
                             compiler_params=_params(semantics))(*args)
        return list(res), []
    ci, co = len(comm.arrays), len(comm.out_shapes)

    def carrying(*refs):
        ins, refs = refs[:n_in], refs[n_in:]
        c_ins, refs = refs[:ci], refs[ci:]
        outs, refs = refs[:n_out], refs[n_out:]
        c_outs, refs = refs[:co], refs[co:]
        scr, sems = refs[:n_scr], refs[n_scr:]
        if not grid:
            comm.start(c_ins, c_outs, sems)
            body(*ins, *outs, *scr)
            comm.wait(c_ins, c_outs, sems)
            return
        first = functools.reduce(jnp.logical_and, [pl.program_id(a) == 0 for a in range(len(grid))])
        last = functools.reduce(jnp.logical_and, [pl.program_id(a) == g - 1 for a, g in enumerate(grid)])

        @pl.when(first)
        def _():
            comm.start(c_ins, c_outs, sems)

        body(*ins, *outs, *scr)

        @pl.when(last)
        def _():
            comm.wait(c_ins, c_outs, sems)

    res = pl.pallas_call(
        carrying, name=name, grid=grid, in_specs=list(in_specs) + [HBM_SPEC] * ci,
        out_specs=list(out_specs) + [HBM_SPEC] * co, out_shape=list(out_shape) + comm.out_shapes,
        scratch_shapes=list(scratch) + comm.scratch, input_output_aliases=aliases,
        compiler_params=_params(("arbitrary",) * len(grid) if grid else None),
    )(*args, *comm.arrays)
    return list(res[:n_out]), list(res[n_out:])


_Cols = collections.namedtuple("_Cols", "array first cols")
_Into = collections.namedtuple("_Into", "rows first held")


def _weight_grad(name, a, b, *, tm, tk, a_colsum=False, into=None):
    cols = a if isinstance(a, _Cols) else _Cols(a, 0, a.shape[1])
    a = cols.array
    (T, N), M = b.shape, cols.cols
    tk = min(tk, T)
    assert cols.first % tm == 0 and M % tm == 0 and T % tk == 0, (name, cols.first, M, tm, T, tk)
    ni, nk, tile0 = M // tm, T // tk, cols.first // tm
    held = list(into.held) if into is not None and into.held is not None else []

    def body(a_ref, b_ref, *rest):
        keep_ref, send_ref = rest[len(held):len(held) + 2]
        sums_ref = rest[len(held) + 2] if a_colsum else None
        if nk == 1:
            acc = _tn(a_ref[...], b_ref[...])
            keep_ref[...], send_ref[...] = acc, acc.astype(send_ref.dtype)
            if a_colsum:
                sums_ref[...] = jnp.sum(a_ref[...].astype(F32), axis=0, keepdims=True)
            return
        acc_ref = rest[-1]
        k = pl.program_id(1)

        @pl.when(k == 0)
        def _():
            acc_ref[...] = jnp.zeros_like(acc_ref)
            if a_colsum:
                sums_ref[...] = jnp.zeros((1, tm), F32)

        if a_colsum:
            sums_ref[...] += jnp.sum(a_ref[...].astype(F32), axis=0, keepdims=True)
        acc_ref[...] += _tn(a_ref[...], b_ref[...])

        @pl.when(k == nk - 1)
        def _():
            keep_ref[...], send_ref[...] = acc_ref[...], acc_ref[...].astype(send_ref.dtype)

    if into is None:
        rows, out_spec = M, pl.BlockSpec((tm, N), lambda i, k: (i, 0))
    else:
        rows = into.rows
        out_spec = pl.BlockSpec((pl.Element(tm), pl.Element(N)),
                                lambda i, k: (pl.multiple_of(into.first + i * tm, ROW_ALIGN), 0))
    out_shape = [jax.ShapeDtypeStruct((rows, N), F32), jax.ShapeDtypeStruct((rows, N), MXU_DTYPE)]
    out_specs = [out_spec, out_spec]
    if a_colsum:
        out_shape.append(jax.ShapeDtypeStruct((1, M), F32))
        out_specs.append(pl.BlockSpec((1, tm), lambda i, k: (0, i)))
    return pl.pallas_call(
        body, name=name, grid=(ni, nk),
        in_specs=[pl.BlockSpec((tk, tm), lambda i, k: (k, tile0 + i)), pl.BlockSpec((tk, N), lambda i, k: (k, 0))]
        + [HBM_SPEC] * len(held),
        out_specs=out_specs, out_shape=out_shape, scratch_shapes=[pltpu.VMEM((tm, N), F32)] if nk > 1 else [],
        input_output_aliases={2 + p: p for p in range(len(held))},
        compiler_params=_params(("parallel", "arbitrary")),
    )(a, b, *held)


def _ffn_fwd(merged, w_out, x, gain, w_gate_t, w_up_t, *, tm, comm=None):
    (T, D), F = x.shape, w_gate_t.shape[0]

    def body(m_ref, wo_ref, x_ref, g_ref, wg_ref, wu_ref, h_ref, u_ref, gate_ref, up_ref, z_ref):
        h = x_ref[...] + _nn(m_ref[...], wo_ref[...])
        h_ref[...] = h
        u = (h * lax.rsqrt(jnp.mean(h * h, axis=-1, keepdims=True) + EPS) * g_ref[...]).astype(u_ref.dtype)
        u_ref[...] = u
        gate, up = _nt(u, wg_ref[...]), _nt(u, wu_ref[...])
        gate_ref[...], up_ref[...] = gate.astype(gate_ref.dtype), up.astype(up_ref.dtype)
        z_ref[...] = (gate * _sigmoid(gate) * up).astype(z_ref.dtype)

    rows = lambda n: pl.BlockSpec((tm, n), lambda i: (i, 0))
    fixed = _fixed_spec
    return _call("ffn_hidden", body, grid=(T // tm,),
                 in_specs=[rows(D), fixed(w_out), rows(D), fixed(gain), fixed(w_gate_t), fixed(w_up_t)],
                 out_specs=[rows(D), rows(D), rows(F), rows(F), rows(F)],
                 out_shape=[jax.ShapeDtypeStruct((T, D), F32), jax.ShapeDtypeStruct((T, D), MXU_DTYPE)]
                 + [jax.ShapeDtypeStruct((T, F), SAVED_DTYPE)] * 2 + [jax.ShapeDtypeStruct((T, F), MXU_DTYPE)],
                 args=[merged, w_out, x, gain, w_gate_t, w_up_t], semantics=("parallel",), comm=comm)


def _in_proj(x, gain, w_in_t, b_in, *, tm, comm=None):
    T, D = x.shape
    bounds = [sum(IN_SPLITS[:i]) for i in range(len(IN_SPLITS) + 1)]

    def body(x_ref, g_ref, w_ref, b_ref, u_ref, *piece_refs):
        xv = x_ref[...]
        r = lax.rsqrt(jnp.mean(xv * xv, axis=-1, keepdims=True) + EPS)
        u = (xv * r * g_ref[...]).astype(u_ref.dtype)
        u_ref[...] = u
        for o_ref, lo, hi in zip(piece_refs, bounds[:-1], bounds[1:]):
            o_ref[...] = (_nt(u, w_ref[lo:hi, :]) + b_ref[:, lo:hi]).astype(o_ref.dtype)

    rows = lambda n: pl.BlockSpec((tm, n), lambda i: (i, 0))
    fixed = _fixed_spec
    dtypes = (MXU_DTYPE, MXU_DTYPE, F32, F32)
    return _call("in_proj", body, grid=(T // tm,),
                 in_specs=[rows(D), fixed(gain), fixed(w_in_t), fixed(b_in)],
                 out_specs=[rows(D)] + [rows(n) for n in IN_SPLITS],
                 out_shape=[jax.ShapeDtypeStruct((T, D), MXU_DTYPE)]
                 + [jax.ShapeDtypeStruct((T, n), dt) for n, dt in zip(IN_SPLITS, dtypes)],
                 args=[x, gain, w_in_t, b_in], semantics=("parallel",), comm=comm)


def _row_spec(tm, n):
    return pl.BlockSpec((tm, n), lambda i: (i, 0))


def _fixed_spec(a):
    return pl.BlockSpec(a.shape, lambda i: (0,) * a.ndim, pipeline_mode=pl.Buffered(1))


def _partials_spec(n):
    return pl.BlockSpec((8, n), lambda i: (i, 0))


def _ffn_tail(z, w_down, h1, target, gain, gate, up, *, tm):
    (T, F), D = z.shape, h1.shape[1]

    def body(z_ref, w_ref, h_ref, t_ref, g_ref, gate_ref, up_ref, dh_ref, dhb_ref, dgate_ref, dup_ref, dg_ref, l_ref):
        h2 = h_ref[...] + _nn(z_ref[...], w_ref[...])
        r = lax.rsqrt(jnp.mean(h2 * h2, axis=-1, keepdims=True) + EPS)
        xhat = h2 * r
        err = xhat * g_ref[...] - t_ref[...]
        part = 0.5 * jnp.sum(jnp.sum(err * err, axis=-1, keepdims=True), axis=0, keepdims=True) / D
        dy = err / D
        dxh = dy * g_ref[...]
        dh2 = r * (dxh - xhat * jnp.mean(dxh * xhat, axis=-1, keepdims=True))
        dh_ref[...] = dh2
        dhb = dh2.astype(dhb_ref.dtype)
        dhb_ref[...] = dhb
        dg_ref[...] = jnp.broadcast_to(jnp.sum(dy * xhat, axis=0, keepdims=True), dg_ref.shape)
        l_ref[...] = jnp.broadcast_to(part, l_ref.shape)
        dz = _nt(dhb, w_ref[...])
        gv, upv = gate_ref[...].astype(F32), up_ref[...].astype(F32)
        s = _sigmoid(gv)
        dgate_ref[...] = (dz * upv * (s * (1.0 + gv * (1.0 - s)))).astype(dgate_ref.dtype)
        dup_ref[...] = (dz * (gv * s)).astype(dup_ref.dtype)

    low = lambda n: jax.ShapeDtypeStruct((T, n), MXU_DTYPE)
    part = jax.ShapeDtypeStruct((8 * (T // tm), D), F32)
    return pl.pallas_call(
        body, name="ffn_tail", grid=(T // tm,),
        in_specs=[_row_spec(tm, F), _fixed_spec(w_down), _row_spec(tm, D), _row_spec(tm, D), _fixed_spec(gain),
                  _row_spec(tm, F), _row_spec(tm, F)],
        out_specs=[_row_spec(tm, D), _row_spec(tm, D), _row_spec(tm, F), _row_spec(tm, F), _partials_spec(D),
                   _partials_spec(D)],
        out_shape=[jax.ShapeDtypeStruct((T, D), F32), low(D), low(F), low(F), part, part],
        compiler_params=_params(("parallel",)),
    )(z, w_down, h1, target, gain, gate, up)


def _ffn_in_bwd(dgate, dup, w_gate_t, w_up_t, h1, gain, dres, *, tm, comm=None):
    (T, F), D = dgate.shape, h1.shape[1]

    def body(dg_ref, du_ref, wg_ref, wu_ref, h_ref, g_ref, r_ref, dh_ref, dhb_ref, dgain_ref):
        d_u2 = _nn(dg_ref[...], wg_ref[...]) + _nn(du_ref[...], wu_ref[...])
        dx, dgain = _rmsnorm_bwd_vals(d_u2, h_ref[...], g_ref[...])
        dh = r_ref[...] + dx
        dh_ref[...] = dh
        dhb_ref[...] = dh.astype(dhb_ref.dtype)
        dgain_ref[...] = jnp.broadcast_to(dgain, dgain_ref.shape)

    return _call("d_ffn_in", body, grid=(T // tm,),
                 in_specs=[_row_spec(tm, F), _row_spec(tm, F), _fixed_spec(w_gate_t), _fixed_spec(w_up_t),
                           _row_spec(tm, D), _fixed_spec(gain), _row_spec(tm, D)],
                 out_specs=[_row_spec(tm, D), _row_spec(tm, D), _partials_spec(D)],
                 out_shape=[jax.ShapeDtypeStruct((T, D), F32), jax.ShapeDtypeStruct((T, D), MXU_DTYPE),
                            jax.ShapeDtypeStruct((8 * (T // tm), D), F32)],
                 args=[dgate, dup, w_gate_t, w_up_t, h1, gain, dres], semantics=("parallel",), comm=comm)


def _in_proj_bwd(pieces, w_in_t, x, gain, dres, *, tm, comm=None):
    T, D = x.shape
    n = len(pieces)

    def body(*refs):
        dps, (w_ref, x_ref, g_ref, r_ref, dx_ref, dgain_ref) = refs[:n], refs[n:]
        d_u = None
        for dp_ref, (dp, first) in zip(dps, pieces):
            term = _nn(dp_ref[...], w_ref[first:first + dp.shape[1], :])
            d_u = term if d_u is None else d_u + term
        dx, dgain = _rmsnorm_bwd_vals(d_u, x_ref[...], g_ref[...])
        dx_ref[...] = r_ref[...] + dx
        dgain_ref[...] = jnp.broadcast_to(dgain, dgain_ref.shape)

    return _call("d_u", body, grid=(T // tm,),
                 in_specs=[_row_spec(tm, dp.shape[1]) for dp, _ in pieces]
                 + [_fixed_spec(w_in_t), _row_spec(tm, D), _fixed_spec(gain), _row_spec(tm, D)],
                 out_specs=[_row_spec(tm, D), _partials_spec(D)],
                 out_shape=[jax.ShapeDtypeStruct((T, D), F32), jax.ShapeDtypeStruct((8 * (T // tm), D), F32)],
                 args=[dp for dp, _ in pieces] + [w_in_t, x, gain, dres], semantics=("parallel",), comm=comm)


def _merge_fwd(y_a, y_b, w_a, w_b, gates, *, tm):
    T, D = y_a.shape

    def body(ya_ref, yb_ref, wa_ref, wb_ref, ga_ref, gb_ref, pa_ref, pb_ref, m_ref):
        pa, pb = _nn(ya_ref[...], wa_ref[...]), _nn(yb_ref[...], wb_ref[...])
        pa_ref[...], pb_ref[...] = pa.astype(pa_ref.dtype), pb.astype(pb_ref.dtype)
        m_ref[...] = (_sigmoid(ga_ref[...]) * pa + _sigmoid(gb_ref[...]) * pb).astype(m_ref.dtype)

    rows = pl.BlockSpec((tm, D), lambda i: (i, 0))
    whole = pl.BlockSpec((D, D), lambda i: (0, 0), pipeline_mode=pl.Buffered(1))
    return pl.pallas_call(
        body, name="branch_merge", grid=(T // tm,),
        in_specs=[rows, rows, whole, whole, rows, pl.BlockSpec((tm, D), lambda i: (i, 1))], out_specs=[rows] * 3,
        out_shape=[jax.ShapeDtypeStruct((T, D), SAVED_DTYPE)] * 2 + [jax.ShapeDtypeStruct((T, D), MXU_DTYPE)],
        compiler_params=_params(("parallel",)),
    )(y_a, y_b, w_a, w_b, gates, gates)


def _merge_bwd(dh, w_out, w_a, w_b, p_a, p_b, gates, *, tm, d_in_width, first):
    T, D = dh.shape

    def body(dh_ref, wo_ref, wa_ref, wb_ref, pa_ref, pb_ref, ga_ref, gb_ref, dpa_ref, dpb_ref, dya_ref, dyb_ref,
             din_ref):
        dm = _nt(dh_ref[...], wo_ref[...])
        sa, sb = _sigmoid(ga_ref[...]), _sigmoid(gb_ref[...])
        dpa, dpb = (dm * sa).astype(dpa_ref.dtype), (dm * sb).astype(dpb_ref.dtype)
        dpa_ref[...], dpb_ref[...] = dpa, dpb
        din_ref[:, :D] = (dm * pa_ref[...].astype(F32) * sa * (1.0 - sa)).astype(din_ref.dtype)
        din_ref[:, D:] = (dm * pb_ref[...].astype(F32) * sb * (1.0 - sb)).astype(din_ref.dtype)
        dya_ref[...] = _nt(dpa, wa_ref[...]).astype(dya_ref.dtype)
        dyb_ref[...] = _nt(dpb, wb_ref[...]).astype(dyb_ref.dtype)

    rows = pl.BlockSpec((tm, D), lambda i: (i, 0))
    whole = pl.BlockSpec((D, D), lambda i: (0, 0), pipeline_mode=pl.Buffered(1))
    low = jax.ShapeDtypeStruct((T, D), MXU_DTYPE)
    return pl.pallas_call(
        body, name="d_branch_merge", grid=(T // tm,),
        in_specs=[rows, whole, whole, whole, rows, rows, rows, pl.BlockSpec((tm, D), lambda i: (i, 1))],
        out_specs=[rows] * 4 + [pl.BlockSpec((pl.Element(tm), pl.Element(2 * D)), lambda i: (
            pl.multiple_of(i * tm, ROW_ALIGN), first))],
        out_shape=[low] * 3 + [jax.ShapeDtypeStruct((T, D), F32), jax.ShapeDtypeStruct((T, d_in_width), MXU_DTYPE)],
        compiler_params=_params(("parallel",)),
    )(dh, w_out, w_a, w_b, p_a, p_b, gates, gates)


def _colsum_partials(p):
    return jnp.sum(p.reshape(-1, 8, p.shape[-1])[:, 0, :], axis=0, keepdims=True)


def _rmsnorm_bwd_vals(dy, xin, g):
    rstd = lax.rsqrt(jnp.mean(xin * xin, axis=-1, keepdims=True) + EPS)
    xhat = xin * rstd
    dg = jnp.sum(dy * xhat, axis=0, keepdims=True)
    dxh = dy * g
    dx = rstd * (dxh - xhat * jnp.mean(dxh * xhat, axis=-1, keepdims=True))
    return dx, dg


ATTN_SCALE = 1.0 / math.sqrt(HEAD_DIM)
GROUP_LANES = GROUP * ATTN_BLOCK
PAIR = 2 * HEAD_DIM


def _attn_mask():
    kj = lax.broadcasted_iota(jnp.int32, (ATTN_BLOCK, GROUP_LANES), 0)
    qi = lax.broadcasted_iota(jnp.int32, (ATTN_BLOCK, GROUP_LANES), 1) & (ATTN_BLOCK - 1)
    return kj <= qi


def _heads_transposed(ref, g, scale=None):
    parts = []
    for a in range(GROUP // 2):
        lo = (g * GROUP // 2 + a) * PAIR
        pair = ref[:, lo:lo + PAIR].astype(F32)
        pair = (pair if scale is None else pair * scale).T
        parts += [pair[:HEAD_DIM], pair[HEAD_DIM:]]
    return jnp.concatenate(parts, axis=1).astype(MXU_DTYPE)


def _heads_back(ref, g, vt):
    for a in range(GROUP // 2):
        lo = (g * GROUP // 2 + a) * PAIR
        pair = jnp.concatenate([vt[:, (2 * a) * ATTN_BLOCK:(2 * a + 1) * ATTN_BLOCK],
                                vt[:, (2 * a + 1) * ATTN_BLOCK:(2 * a + 2) * ATTN_BLOCK]], axis=0)
        ref[:, lo:lo + PAIR] = pair.T.astype(ref.dtype)


def _kv_parts(kv_ref, g):
    ks = slice(g * HEAD_DIM, (g + 1) * HEAD_DIM)
    vs = slice(KV_WIDTH + g * HEAD_DIM, KV_WIDTH + (g + 1) * HEAD_DIM)
    return kv_ref[:, ks].astype(MXU_DTYPE), kv_ref[:, vs].astype(MXU_DTYPE)


def _sink_rows(sinks):
    return jnp.repeat(sinks.reshape(KV_HEADS, GROUP), ATTN_BLOCK, axis=1)


def _attn_fwd(pq, pkv, sinks, comm=None):
    T = pq.shape[0]
    nb = T // ATTN_BLOCK

    def body(q_ref, kvc_ref, kvp_ref, s_ref, y_ref, lse_ref):
        mask_c = _attn_mask()
        has_prev = pl.program_id(0) > 0
        for g in range(KV_HEADS):
            (kc, vc), (kp, vp) = _kv_parts(kvc_ref, g), _kv_parts(kvp_ref, g)
            qt = _heads_transposed(q_ref, g, ATTN_SCALE)
            s = jnp.where(mask_c, _nn(kc, qt), jnp.where(has_prev, _nn(kp, qt), NEG_INF))
            sink = s_ref[g:g + 1, :]
            m = jnp.maximum(jnp.max(s, axis=0, keepdims=True), sink)
            p = jnp.exp(s - m)
            den = jnp.sum(p, axis=0, keepdims=True) + jnp.exp(sink - m)
            pc = jnp.where(mask_c, p, 0.0)
            _heads_back(y_ref, g, (_tn(vc, pc) + _tn(vp, p - pc)) / den)
            lse = m + jnp.log(den)
            for i in range(GROUP):
                lse_ref[g * GROUP + i:g * GROUP + i + 1, :] = lse[:, i * ATTN_BLOCK:(i + 1) * ATTN_BLOCK]

    return _call(
        "attn_fwd", body, grid=(nb,),
        in_specs=[pl.BlockSpec((ATTN_BLOCK, D_MODEL), lambda n: (n, 0)),
                  pl.BlockSpec((ATTN_BLOCK, 2 * KV_WIDTH), lambda n: (n, 0)),
                  pl.BlockSpec((ATTN_BLOCK, 2 * KV_WIDTH), lambda n: (jnp.maximum(n - 1, 0), 0)),
                  pl.BlockSpec((KV_HEADS, GROUP_LANES), lambda n: (0, 0))],
        out_specs=[pl.BlockSpec((ATTN_BLOCK, D_MODEL), lambda n: (n, 0)),
                   pl.BlockSpec((Q_HEADS, ATTN_BLOCK), lambda n: (0, n))],
        out_shape=[jax.ShapeDtypeStruct((T, D_MODEL), MXU_DTYPE), jax.ShapeDtypeStruct((Q_HEADS, T), F32)],
        args=[pq, pkv, pkv, _sink_rows(sinks)], semantics=("parallel",), comm=comm)


def _attn_bwd(pq, pkv, sinks, lse, dy, d_in, comm=None):
    T = pq.shape[0]
    nb = T // ATTN_BLOCK
    cur = lambda n: (jnp.minimum(n, nb - 1), 0)
    done = D_MODEL + 2 * KV_WIDTH

    def body(q_ref, kvc_ref, kvp_ref, s_ref, lse_ref, dy_ref, _, out_ref, ds_ref, carry, top, bot, dq_ref):
        n = pl.program_id(0)

        @pl.when(n == 0)
        def _():
            carry[...] = jnp.zeros_like(carry)
            dq_ref[...] = jnp.zeros_like(dq_ref)
            ds_ref[...] = jnp.zeros_like(ds_ref)

        out_ref[:, :D_MODEL] = dq_ref[...]

        @pl.when(n < nb)
        def _():
            mask_c = _attn_mask()
            valid = jnp.logical_or(mask_c, n > 0)
            for g in range(KV_HEADS):
                ks = slice(g * HEAD_DIM, (g + 1) * HEAD_DIM)
                vs = slice(KV_WIDTH + g * HEAD_DIM, KV_WIDTH + (g + 1) * HEAD_DIM)
                (kc, vc), (kp, vp) = _kv_parts(kvc_ref, g), _kv_parts(kvp_ref, g)
                qt = _heads_transposed(q_ref, g, ATTN_SCALE)
                dot = _heads_transposed(dy_ref, g)
                lse = jnp.concatenate([lse_ref[g * GROUP + i:g * GROUP + i + 1, :] for i in range(GROUP)], axis=1)
                p = jnp.where(valid, jnp.exp(jnp.where(mask_c, _nn(kc, qt), _nn(kp, qt)) - lse), 0.0)
                dp = jnp.where(mask_c, _nn(vc, dot), _nn(vp, dot))
                delta = jnp.sum(p * dp, axis=0, keepdims=True)
                ds = p * (dp - delta)
                ds_c, p_c = jnp.where(mask_c, ds, 0.0), jnp.where(mask_c, p, 0.0)
                ds_p, p_p = ds - ds_c, p - p_c
                _heads_back(dq_ref, g, (_tn(kc, ds_c) + _tn(kp, ds_p)) * ATTN_SCALE)
                bot[:, ks], bot[:, vs] = _nt(ds_c, qt), _nt(p_c, dot)
                top[:, ks], top[:, vs] = _nt(ds_p, qt), _nt(p_p, dot)
                ds_ref[g:g + 1, :] -= jnp.exp(s_ref[g:g + 1, :] - lse) * delta
            out_ref[:, D_MODEL:] = (carry[...] + top[...]).astype(out_ref.dtype)
            carry[...] = bot[...]

        @pl.when(n == nb)
        def _():
            out_ref[:, D_MODEL:] = carry[...].astype(out_ref.dtype)

    return _call(
        "attn_bwd", body, grid=(nb + 1,),
        in_specs=[pl.BlockSpec((ATTN_BLOCK, D_MODEL), cur),
                  pl.BlockSpec((ATTN_BLOCK, 2 * KV_WIDTH), cur),
                  pl.BlockSpec((ATTN_BLOCK, 2 * KV_WIDTH), lambda n: (jnp.maximum(jnp.minimum(n, nb - 1) - 1, 0), 0)),
                  pl.BlockSpec((KV_HEADS, GROUP_LANES), lambda n: (0, 0)),
                  pl.BlockSpec((Q_HEADS, ATTN_BLOCK), lambda n: (0, jnp.minimum(n, nb - 1))),
                  pl.BlockSpec((ATTN_BLOCK, D_MODEL), cur), HBM_SPEC],
        out_specs=[pl.BlockSpec((ATTN_BLOCK, done), lambda n: (jnp.maximum(n - 1, 0), 0)),
                   pl.BlockSpec((KV_HEADS, GROUP_LANES), lambda n: (0, 0))],
        out_shape=[jax.ShapeDtypeStruct(d_in.shape, d_in.dtype), jax.ShapeDtypeStruct((KV_HEADS, GROUP_LANES), F32)],
        scratch=[pltpu.VMEM((ATTN_BLOCK, 2 * KV_WIDTH), F32)] * 3 + [pltpu.VMEM((ATTN_BLOCK, D_MODEL), MXU_DTYPE)],
        args=[pq, pkv, pkv, _sink_rows(sinks), lse, dy, d_in], semantics=("arbitrary",), comm=comm, aliases={6: 0})


def _lower_bound(l):
    m = jnp.maximum(l[0:1], l[1:2])
    e0, e1 = jnp.exp(l[0:1] - m), jnp.exp(l[1:2] - m)
    return e0 / (e0 + e1)


def _tri(lower):
    r = lax.broadcasted_iota(jnp.int32, (CHUNK, CHUNK), 0)
    c = lax.broadcasted_iota(jnp.int32, (CHUNK, CHUNK), 1)
    return (r >= c) if lower else (c >= r)


def _chunk_sum(mask, v):
    ones = mask.astype(BF16)
    hi = v.astype(BF16)
    rest = v - hi.astype(F32)
    mid = rest.astype(BF16)
    lo = (rest - mid.astype(F32)).astype(BF16)
    part = lambda t: lax.dot_general(ones, t, (((1,), (0,)), ((), ())), preferred_element_type=F32)
    return part(hi) + part(mid) + part(lo)


def _hgrn_chunk_inputs(hq, hf, lb, causal):
    half_t = 0.5 * jnp.tanh(0.5 * hf)
    sg, sgn = 0.5 + half_t, 0.5 - half_t
    f = lb + (1.0 - lb) * sg
    kk = (1.0 - lb) * sgn
    sq = _sigmoid(hq)
    q = hq * sq
    b = _chunk_sum(causal, jnp.log(f))
    bm, bl = b[CHUNK // 2 - 1:CHUNK // 2, :], b[CHUNK - 1:CHUNK, :]
    e_qm, e_km = jnp.exp(b - bm), jnp.exp(bm - b)
    e_qs, e_kl = e_qm * jnp.exp(bm), e_km * jnp.exp(bl - bm)
    return dict(sg=sg, sgn=sgn, f=f, kk=kk, sq=sq, q=q, e_qm=e_qm, e_km=e_km, e_qs=e_qs, e_kl=e_kl,
                qm=q * e_qm, km=kk * e_km, qs=q * e_qs, kl=kk * e_kl, el=jnp.exp(bl))


def _hgrn_fwd(ph, lb_logits, norm_g, comm=None):
    T = ph.shape[0]
    nblk, cpb = T // HGRN_TOKENS, HGRN_TOKENS // CHUNK
    col = lambda c: pl.BlockSpec((HGRN_TOKENS, D_MODEL), functools.partial(lambda i, c: (i, c), c=c))

    def body(hq_ref, hf_ref, hi_ref, hg_ref, l_ref, ng_ref, y_ref, o_ref, st_ref, s_ref):
        @pl.when(pl.program_id(0) == 0)
        def _():
            s_ref[...] = jnp.zeros_like(s_ref)

        lb = _lower_bound(l_ref[...])
        causal = _tri(True)
        for c in range(cpb):
            rows = slice(c * CHUNK, (c + 1) * CHUNK)
            t = _hgrn_chunk_inputs(hq_ref[rows, :], hf_ref[rows, :], lb, causal)
            qm, km, qs, kl = (t[n].astype(MXU_DTYPE) for n in ("qm", "km", "qs", "kl"))
            v = hi_ref[rows, :].astype(MXU_DTYPE)
            for h in range(HGRN_HEADS):
                ls = slice(h * HGRN_K, (h + 1) * HGRN_K)
                st = s_ref[h]
                st_ref[c, ls, :] = st
                a = jnp.where(causal, _nt(qm[:, ls], km[:, ls]), 0.0)
                o_ref[rows, ls] = _nn(a, v[:, ls]) + _nt(qs[:, ls], st)
                s_ref[h] = t["el"][:, ls] * st + _tn(v[:, ls], kl[:, ls])
        for h in range(HGRN_HEADS):
            ls = slice(h * HGRN_K, (h + 1) * HGRN_K)
            o = o_ref[:, ls]
            r = lax.rsqrt(jnp.mean(o * o, axis=-1, keepdims=True) + EPS)
            y_ref[:, ls] = (o * r * ng_ref[:, ls] * _sigmoid(hg_ref[:, ls])).astype(y_ref.dtype)

    return _call(
        "hgrn_fwd", body, grid=(nblk,),
        in_specs=[col(0), col(1), col(2), col(3),
                  pl.BlockSpec((2, D_MODEL), lambda i: (0, 0)), pl.BlockSpec((1, D_MODEL), lambda i: (0, 0))],
        out_specs=[pl.BlockSpec((HGRN_TOKENS, D_MODEL), lambda i: (i, 0)),
                   pl.BlockSpec((HGRN_TOKENS, D_MODEL), lambda i: (i, 0)),
                   pl.BlockSpec((cpb, D_MODEL, HGRN_K), lambda i: (i, 0, 0))],
        out_shape=[jax.ShapeDtypeStruct((T, D_MODEL), MXU_DTYPE), jax.ShapeDtypeStruct((T, D_MODEL), F32),
                   jax.ShapeDtypeStruct((T // CHUNK, D_MODEL, HGRN_K), F32)],
        scratch=[pltpu.VMEM((HGRN_HEADS, HGRN_K, HGRN_K), F32)],
        args=[ph, ph, ph, ph, lb_logits, norm_g], semantics=("arbitrary",), comm=comm)


def _hgrn_bwd(ph, o_raw, states, dy, lb_logits, norm_g, d_in, first, comm=None):
    T = ph.shape[0]
    nblk, cpb = T // HGRN_TOKENS, HGRN_TOKENS // CHUNK
    rev = lambda i: nblk - 1 - i
    col = lambda c: pl.BlockSpec((HGRN_TOKENS, D_MODEL), functools.partial(lambda i, c: (rev(i), c), c=c))
    tok = pl.BlockSpec((HGRN_TOKENS, D_MODEL), lambda i: (rev(i), 0))

    def body(hq_ref, hf_ref, hi_ref, hg_ref, o_ref, st_ref, dy_ref, l_ref, ng_ref, _,
             dph_ref, dng_ref, dl_ref, dst_ref, dlb_ref, do_s, dqm_s, dkm_s, dqs_s, dkl_s, dv_s, del_s):
        i = pl.program_id(0)

        @pl.when(i == 0)
        def _():
            dst_ref[...] = jnp.zeros_like(dst_ref)
            dlb_ref[...] = jnp.zeros_like(dlb_ref)
            dng_ref[...] = jnp.zeros_like(dng_ref)

        lb = _lower_bound(l_ref[...])
        causal, anti = _tri(True), _tri(False)
        row = lax.broadcasted_iota(jnp.int32, (CHUNK, D_MODEL), 0)
        for c in reversed(range(cpb)):
            rows = slice(c * CHUNK, (c + 1) * CHUNK)
            hq = hq_ref[rows, :]
            t = _hgrn_chunk_inputs(hq, hf_ref[rows, :], lb, causal)
            sgg = _sigmoid(hg_ref[rows, :])
            dyv = dy_ref[rows, :]
            for h in range(HGRN_HEADS):
                ls = slice(h * HGRN_K, (h + 1) * HGRN_K)
                o = o_ref[rows, ls]
                r = lax.rsqrt(jnp.mean(o * o, axis=-1, keepdims=True) + EPS)
                nrm = o * r
                g_h = sgg[:, ls]
                dph_ref[rows, 3 * D_MODEL + h * HGRN_K:3 * D_MODEL + (h + 1) * HGRN_K] = (
                    dyv[:, ls] * nrm * ng_ref[:, ls] * g_h * (1.0 - g_h)).astype(dph_ref.dtype)
                dyg = dyv[:, ls] * g_h
                dng_ref[:, ls] += jnp.sum(dyg * nrm, axis=0, keepdims=True)
                dn = dyg * ng_ref[:, ls]
                do_s[:, ls] = r * (dn - nrm * jnp.mean(dn * nrm, axis=-1, keepdims=True))
            qm, km, qs, kl = (t[n].astype(MXU_DTYPE) for n in ("qm", "km", "qs", "kl"))
            v = hi_ref[rows, :].astype(MXU_DTYPE)
            do = do_s[...].astype(MXU_DTYPE)
            for h in range(HGRN_HEADS):
                ls = slice(h * HGRN_K, (h + 1) * HGRN_K)
                st = st_ref[c, ls, :]
                dst = dst_ref[h]
                a = jnp.where(causal, _nt(qm[:, ls], km[:, ls]), 0.0)
                da = jnp.where(causal, _nt(do[:, ls], v[:, ls]), 0.0)
                dv_s[:, ls] = _tn(a, do[:, ls]) + _nt(kl[:, ls], dst)
                dkl_s[:, ls] = _nn(v[:, ls], dst)
                dqs_s[:, ls] = _nn(do[:, ls], st)
                del_s[:, ls] = jnp.sum(dst * st, axis=0, keepdims=True)
                dst_ref[h] = _tn(do[:, ls], qs[:, ls]) + t["el"][:, ls] * dst
                dqm_s[:, ls] = _nn(da, km[:, ls])
                dkm_s[:, ls] = _tn(da, qm[:, ls])
            dqm, dkm, dqs, dkl = dqm_s[...], dkm_s[...], dqs_s[...], dkl_s[...]
            dq = dqm * t["e_qm"] + dqs * t["e_qs"]
            dk = dkm * t["e_km"] + dkl * t["e_kl"]
            t_qm, t_km, t_kl = dqm * t["qm"], dkm * t["km"], dkl * t["kl"]
            db = t_qm - t_km + dqs * t["qs"] - t_kl
            db_mid = jnp.sum(t_km - t_qm, axis=0, keepdims=True)
            db_last = jnp.sum(t_kl, axis=0, keepdims=True) + del_s[...] * t["el"]
            db = db + jnp.where(row == CHUNK // 2 - 1, db_mid, 0.0) + jnp.where(row == CHUNK - 1, db_last, 0.0)
            dlogf = _chunk_sum(anti, db)
            sq, sg, sgn, f = t["sq"], t["sg"], t["sgn"], t["f"]
            dph_ref[rows, 0:D_MODEL] = (dq * (sq * (1.0 + hq * (1.0 - sq)))).astype(dph_ref.dtype)
            dph_ref[rows, D_MODEL:2 * D_MODEL] = (
                dlogf * (1.0 - lb) * sg * (1.0 - sg) / f - dk * (1.0 - lb) * sgn * (1.0 - sgn)).astype(dph_ref.dtype)
            dph_ref[rows, 2 * D_MODEL:3 * D_MODEL] = dv_s[...].astype(dph_ref.dtype)
            dlb_ref[...] += jnp.sum(dlogf * (1.0 - sg) / f - dk * sgn, axis=0, keepdims=True)

        @pl.when(i == nblk - 1)
        def _():
            dl0 = dlb_ref[...] * lb * (1.0 - lb)
            dl_ref[0:1, :] = dl0
            dl_ref[1:2, :] = -dl0

    wide = pltpu.VMEM((CHUNK, D_MODEL), F32)
    return _call(
        "hgrn_bwd", body, grid=(nblk,),
        in_specs=[col(0), col(1), col(2), col(3), tok,
                  pl.BlockSpec((cpb, D_MODEL, HGRN_K), lambda i: (rev(i), 0, 0)), tok,
                  pl.BlockSpec((2, D_MODEL), lambda i: (0, 0)), pl.BlockSpec((1, D_MODEL), lambda i: (0, 0)), HBM_SPEC],
        out_specs=[pl.BlockSpec((pl.Element(HGRN_TOKENS), pl.Element(4 * D_MODEL)), lambda i: (
                       pl.multiple_of(rev(i) * HGRN_TOKENS, ROW_ALIGN), first)),
                   pl.BlockSpec((1, D_MODEL), lambda i: (0, 0)), pl.BlockSpec((2, D_MODEL), lambda i: (0, 0))],
        out_shape=[jax.ShapeDtypeStruct(d_in.shape, d_in.dtype), jax.ShapeDtypeStruct((1, D_MODEL), F32),
                   jax.ShapeDtypeStruct((2, D_MODEL), F32)],
        scratch=[pltpu.VMEM((HGRN_HEADS, HGRN_K, HGRN_K), F32), pltpu.VMEM((1, D_MODEL), F32),
                 wide, wide, wide, wide, wide, wide, pltpu.VMEM((1, D_MODEL), F32)],
        args=[ph, ph, ph, ph, o_raw, states, dy, lb_logits, norm_g, d_in], semantics=("arbitrary",), comm=comm,
        aliases={9: 0})


def _local_step(x, target, vec, net):
    T, D = x.shape
    norm_mix_g, b_in, sinks, lb_logits = vec["norm_mix_g"], vec["b_in"], vec["attn_sinks"], vec["hgrn_lb_logits"]
    hgrn_norm_g, norm_ffn_g, norm_final_g = vec["hgrn_norm_g"], vec["norm_ffn_g"], vec["norm_final_g"]
    w_in = net.full("w_in")
    o_q, o_kv, o_h, o_g = (sum(IN_SPLITS[:i]) for i in range(4))
    TM = 512

    names = ("w_branch_attn", "w_branch_hgrn", "w_out")
    (u, pq, pkv, ph, pg), got = _in_proj(x, norm_mix_g, w_in, b_in, tm=256, comm=net.gather(names))
    net.gathered(names, got)
    names = ("w_ffn_gate",)
    (y_attn, lse), got = _attn_fwd(pq, pkv, sinks, comm=net.gather(names))
    net.gathered(names, got)
    names = ("w_ffn_up",)
    (y_hgrn, o_raw, states), got = _hgrn_fwd(ph, lb_logits, hgrn_norm_g, comm=net.gather(names))
    net.gathered(names, got)
    w_ba, w_bh, w_out = net.full("w_branch_attn"), net.full("w_branch_hgrn"), net.full("w_out")
    w_gate, w_up = net.full("w_ffn_gate"), net.full("w_ffn_up")
    ya, yb, merged = _merge_fwd(y_attn, y_hgrn, w_ba, w_bh, pg, tm=TM)

    FT = FFN // 2
    names = ("w_ffn_down",)
    (h1, u2, gpre, up, z), got = _ffn_fwd(merged, w_out, x, norm_ffn_g, w_gate, w_up, tm=256,
                                          comm=net.gather(names))
    net.gathered(names, got)
    w_down = net.full("w_ffn_down")

    dh2, dh2b, dgp, dup, dgf_p, loss_p = _ffn_tail(z, w_down, h1, target, norm_final_g, gpre, up, tm=256)
    loss = jnp.sum(loss_p.reshape(-1, 8, D)[:, 0, 0])
    d_norm_final = _colsum_partials(dgf_p)
    d_w_down = _weight_grad("dw_down", z, dh2b, tm=FT, tk=T)

    names = ("w_ffn_down",)
    (dh1, dh1b, dg2_p), got = _ffn_in_bwd(dgp, dup, w_gate, w_up, h1, norm_ffn_g, dh2, tm=256,
        comm=net.exchange(dict(w_ffn_down=[d_w_down])))
    net.received(names, (), got)
    d_norm_ffn = _colsum_partials(dg2_p)
    d_w_gate = _weight_grad("dw_gate", dgp, u2, tm=FT, tk=T)
    d_w_up = _weight_grad("dw_up", dup, u2, tm=FT, tk=T)

    rows_in = sum(IN_SPLITS)
    dya, dyb, dy_attn, dy_hgrn, d_in = _merge_bwd(dh1b, w_out, w_ba, w_bh, ya, yb, pg, tm=TM, d_in_width=rows_in,
                                                  first=o_g)
    d_w_out = _weight_grad("dw_out", merged, dh1b, tm=1024, tk=1024)
    d_w_ba = _weight_grad("dw_branch_a", y_attn, dya, tm=1024, tk=1024)
    d_w_bh = _weight_grad("dw_branch_b", y_hgrn, dyb, tm=1024, tk=1024)

    names, swap = ("w_ffn_gate",), ("w_ffn_down",)
    (d_in, dsink), got = _attn_bwd(pq, pkv, sinks, lse, dy_attn, d_in,
                                   comm=net.exchange(dict(w_ffn_gate=[d_w_gate]), swap))
    net.received(names, swap, got)
    names, swap = ("w_ffn_up", "w_out", "w_branch_attn", "w_branch_hgrn"), ("w_ffn_gate",)
    (d_in, d_hgrn_norm, d_lb_logits), got = _hgrn_bwd(
        ph, o_raw, states, dy_hgrn, lb_logits, hgrn_norm_g, d_in, o_h,
        comm=net.exchange(dict(w_ffn_up=[d_w_up], w_out=[d_w_out], w_branch_attn=[d_w_ba],
                               w_branch_hgrn=[d_w_bh]), swap))
    net.received(names, swap, got)

    main = rows_in // 1024 * 1024
    *d_w_in, db_main = _weight_grad("dw_in", _Cols(d_in, 0, main), u, tm=1024, tk=T, a_colsum=True,
                                    into=_Into(rows_in, 0, None))
    *d_w_in, db_rest = _weight_grad("dw_in_rest", _Cols(d_in, main, rows_in - main), u, tm=rows_in - main, tk=1024,
                                    a_colsum=True, into=_Into(rows_in, main, d_w_in))
    d_w_in = [tuple(d_w_in)]

    first_level = net.presum_begin("w_in", d_w_in)
    halves = net.presum_end("w_in", [] if first_level is None else _copies_alone("presum_swap_w_in", first_level))
    names, swap = ("w_in",), ("w_ffn_up", "w_out", "w_branch_attn", "w_branch_hgrn")
    (dx, dg1_p), got = _in_proj_bwd([(d_in, 0)], w_in, x, norm_mix_g, dh1, tm=256,
                                    comm=_join(halves, net.swap(swap)))
    net.last = (names, swap, got)
    d_norm_mix = _colsum_partials(dg1_p)
    d_b_in = jnp.concatenate([db_main, db_rest], axis=1)
    vecs = dict(norm_mix_g=d_norm_mix, b_in=d_b_in, attn_sinks=jnp.sum(dsink.reshape(Q_HEADS, ATTN_BLOCK), axis=1).reshape(1, Q_HEADS),
                hgrn_lb_logits=d_lb_logits,
                hgrn_norm_g=d_hgrn_norm, norm_ffn_g=d_norm_ffn, norm_final_g=d_norm_final)
    return loss, dx, vecs


def _place():
    return lax.axis_index("x"), lax.axis_index("y"), lax.axis_index("c")


def _other_chips(x, y):
    return [(1 - x, y), (x, 1 - y), (1 - x, 1 - y)]


def _y_first(copies):
    return [copies[3 * (i // 3) + (1, 0, 2)[i % 3]] for i in range(len(copies))]


def _gather_copies(shards):
    n = len(shards)

    def build(ins, outs, send_sems, recv_sems, local_sems):
        x, y, c = _place()
        mine = 2 * x + y
        local = [pltpu.make_async_copy(ins[w], outs[w].at[mine], local_sems.at[w]) for w in range(n)]
        sends, recvs = [], []
        for w in range(n):
            for k, (px, py) in enumerate(_other_chips(x, y)):
                sem = 3 * w + k
                sends.append(pltpu.make_async_remote_copy(
                    src_ref=ins[w], dst_ref=outs[w].at[mine], send_sem=send_sems.at[sem], recv_sem=recv_sems.at[sem],
                    device_id=(px, py, c), device_id_type=MESH_ID))
                recvs.append(pltpu.make_async_remote_copy(
                    src_ref=ins[w], dst_ref=outs[w].at[2 * px + py], send_sem=send_sems.at[sem],
                    recv_sem=recv_sems.at[sem], device_id=(px, py, c), device_id_type=MESH_ID))
        return sends, recvs, local, _y_first(sends)

    return _Carried(shards, [jax.ShapeDtypeStruct((N_CHIPS,) + s.shape, s.dtype) for s in shards], 3 * n, n, build)


def _grad_copies(stacked):
    n = len(stacked)

    def build(ins, outs, send_sems, recv_sems, local_sems):
        x, y, c = _place()
        sends = []
        for w in range(n):
            for k, (px, py) in enumerate(_other_chips(x, y)):
                sem = 3 * w + k
                sends.append(pltpu.make_async_remote_copy(
                    src_ref=ins[w].at[2 * px + py], dst_ref=outs[w].at[k], send_sem=send_sems.at[sem],
                    recv_sem=recv_sems.at[sem], device_id=(px, py, c), device_id_type=MESH_ID))
        return sends, sends, [], _y_first(sends)

    return _Carried(stacked, [jax.ShapeDtypeStruct((3,) + s.shape[1:], s.dtype) for s in stacked], 3 * n, 0, build)


def _small_copies(small):
    def build(ins, outs, send_sems, recv_sems, local_sems):
        small_ref, all_ref = ins[0], outs[0]
        x, y, c = _place()
        me = 4 * x + 2 * y + c
        sends, recvs = [], []
        for r in range(1, 8):
            px = 1 - x if r & 4 else x
            py = 1 - y if r & 2 else y
            pc = 1 - c if r & 1 else c
            sends.append(pltpu.make_async_remote_copy(
                src_ref=small_ref, dst_ref=all_ref.at[me], send_sem=send_sems.at[r - 1], recv_sem=recv_sems.at[r - 1],
                device_id=(px, py, pc), device_id_type=MESH_ID))
            recvs.append(pltpu.make_async_remote_copy(
                src_ref=small_ref, dst_ref=all_ref.at[4 * px + 2 * py + pc], send_sem=send_sems.at[r - 1],
                recv_sem=recv_sems.at[r - 1], device_id=(px, py, pc), device_id_type=MESH_ID))
        return sends, recvs, [pltpu.make_async_copy(small_ref, all_ref.at[me], local_sems.at[0])]

    return _Carried([small], [jax.ShapeDtypeStruct((8,) + small.shape, small.dtype)], 7, 1, build)


def _gather_by_neighbours(name, shard):
    half = shard.shape[0] // 2
    quarter = half // 2

    def body(in_ref, out_ref, send_sems, recv_sems, local_sem):
        for core in (0, 1):
            @pl.when(lax.axis_index("c") == core)
            def _():
                program(core, in_ref, out_ref, send_sems, recv_sems, local_sem)

    def program(c, in_ref, out_ref, send_sems, recv_sems, local_sem):
        x, y, _ = _place()
        chip = lambda px, py: 2 * px + py
        to_x, to_y, sibling = (1 - x, y, c), (x, 1 - y, c), (x, y, 1 - c)
        x_blk, y_blk, d_blk = chip(1 - x, y), chip(x, 1 - y), chip(1 - x, 1 - y)
        mine, theirs = c * half, (1 - c) * half

        def copy(sem, rows, block, to, src=None):
            place = out_ref.at[block, pl.ds(rows[0], rows[1])]
            return pltpu.make_async_remote_copy(
                src_ref=place if src is None else src, dst_ref=place, send_sem=send_sems.at[sem],
                recv_sem=recv_sems.at[sem], device_id=to, device_id_type=MESH_ID)

        own = pltpu.make_async_copy(in_ref, out_ref.at[chip(x, y)], local_sem)
        own.start()
        my_rows = in_ref.at[pl.ds(mine, half)]
        along_x = dict(send=copy(0, (mine, half), chip(x, y), to_x, src=my_rows),
                       landed=copy(0, (mine, half), x_blk, to_x),
                       onward=[copy(3, (mine + quarter, quarter), x_blk, to_y), copy(4, (mine, half), x_blk, sibling)],
                       diagonal=copy(2, (mine, quarter), d_blk, to_x))
        along_y = dict(send=copy(1, (mine, half), chip(x, y), to_y, src=my_rows),
                       landed=copy(1, (mine, half), y_blk, to_y),
                       onward=[copy(2, (mine, quarter), y_blk, to_x), copy(5, (mine, half), y_blk, sibling)],
                       diagonal=copy(3, (mine + quarter, quarter), d_blk, to_y))
        last = copy(6, (mine, half), d_blk, sibling)

        order = (along_x, along_y) if c == 0 else (along_y, along_x)
        for axis in order:
            axis["send"].start()
        for axis in order:
            axis["landed"].wait_recv()
            for cp in axis["onward"]:
                cp.start()
        for axis in order:
            axis["diagonal"].wait_recv()
        last.start()
        for sem, block in ((4, x_blk), (5, y_blk), (6, d_blk)):
            copy(sem, (theirs, half), block, sibling).wait_recv()
        for cp in [along_x["send"], along_y["send"]] + along_x["onward"] + along_y["onward"] + [last]:
            cp.wait_send()
        own.wait()

    return pl.pallas_call(
        body, name=name, in_specs=[HBM_SPEC], out_specs=HBM_SPEC,
        out_shape=jax.ShapeDtypeStruct((N_CHIPS,) + shard.shape, shard.dtype),
        scratch_shapes=[pltpu.SemaphoreType.DMA((7,)), pltpu.SemaphoreType.DMA((7,)), pltpu.SemaphoreType.DMA(())],
    )(shard)


def _copies_alone(name, comm):
    return _call(name, lambda: None, grid=(), in_specs=[], out_specs=[], out_shape=[], args=[], comm=comm)[1]


class _Net:
    def __init__(self, shards):
        self.shards = shards
        self.whole, self.own, self.theirs, self.sums, self.other = {}, {}, {}, {}, {}
        x, y, _ = _place()
        self.chip = 2 * x + y

    def gather(self, names):
        return _gather_copies([self.shards[n] for n in names])

    def gathered(self, names, got):
        for n, g in zip(names, got):
            self.whole[n] = g.reshape(-1, g.shape[-1])

    def full(self, name):
        return self.whole[name]

    def exchange(self, grads, swap=()):
        stacked = []
        for n, pieces in grads.items():
            (keep, send), = pieces
            self.own[n] = keep
            stacked.append(send.reshape(N_CHIPS, keep.shape[0] // N_CHIPS, send.shape[-1]))
        return _join(_grad_copies(stacked), self.swap(swap))

    def swap(self, names):
        return _sibling_copies([self.sums[n] for n in names]) if names else None

    def presum_begin(self, name, pieces):
        keep = jnp.concatenate([p[0] for p in pieces], axis=0) if len(pieces) > 1 else pieces[0][0]
        send = jnp.concatenate([p[1] for p in pieces], axis=0) if len(pieces) > 1 else pieces[0][1]
        rows = keep.shape[0] // N_CHIPS
        self.held = keep.reshape(N_CHIPS, rows, keep.shape[-1])
        return _half_rows_copies(send.reshape(N_CHIPS, rows, send.shape[-1]))

    def presum_end(self, name, got):
        x, y, c = _place()
        to_send, self.own[name] = _pre_sum("presum_" + name, self.held, got[0], jnp.stack([c, self.chip]))
        return _grad_copies([to_send])

    def received(self, names, swap, got, carried=None):
        self.theirs.update(zip(names, got[:len(names)]))
        self.other.update(zip(swap, got[len(names):]))
        for n in names:
            (self.sums[n],), more = _partial_sum("sum_" + n, self.own[n], self.theirs[n], self.chip, comm=carried)
        return more


def _half_rows_copies(stacked):
    n, rows = stacked.shape[0], stacked.shape[1] // 2

    def build(ins, outs, send_sems, recv_sems, local_sems):
        x, y, c = _place()
        copies = [pltpu.make_async_remote_copy(
            src_ref=ins[0].at[s, pl.ds((1 - c) * rows, rows)], dst_ref=outs[0].at[s], send_sem=send_sems.at[s],
            recv_sem=recv_sems.at[s], device_id=(x, y, 1 - c), device_id_type=MESH_ID) for s in range(n)]
        return copies, copies, []

    return _Carried([stacked], [jax.ShapeDtypeStruct((n, rows, stacked.shape[2]), stacked.dtype)], n, 0, build)


def _pre_sum(name, held, theirs, core_and_chip):
    n, R, C = held.shape
    half = R // 2
    tr = _row_tile(half)
    per_half = half // tr

    def body(place_ref, h_ref, t_ref, send_ref, own_ref):
        total = h_ref[0] + t_ref[0].astype(F32)
        send_ref[0] = total.astype(send_ref.dtype)

        @pl.when(pl.program_id(1) == place_ref[1])
        def _():
            own_ref[...] = total

    return pl.pallas_call(
        body, name=name,
        grid_spec=pltpu.PrefetchScalarGridSpec(
            num_scalar_prefetch=1, grid=(per_half, n),
            in_specs=[pl.BlockSpec((1, tr, C), lambda i, s, place: (s, place[0] * per_half + i, 0)),
                      pl.BlockSpec((1, tr, C), lambda i, s, place: (s, i, 0))],
            out_specs=[pl.BlockSpec((1, tr, C), lambda i, s, place: (s, i, 0)),
                       pl.BlockSpec((tr, C), lambda i, s, place: (i, 0))]),
        out_shape=[jax.ShapeDtypeStruct((n, half, C), MXU_DTYPE), jax.ShapeDtypeStruct((half, C), F32)],
        compiler_params=_params(("arbitrary", "arbitrary")),
    )(core_and_chip, held, theirs)


def _sibling_copies(parts):
    n = len(parts)

    def build(ins, outs, send_sems, recv_sems, local_sems):
        x, y, c = _place()
        copies = [pltpu.make_async_remote_copy(
            src_ref=ins[w], dst_ref=outs[w], send_sem=send_sems.at[w], recv_sem=recv_sems.at[w],
            device_id=(x, y, 1 - c), device_id_type=MESH_ID) for w in range(n)]
        return copies, copies, []

    return _Carried(parts, [jax.ShapeDtypeStruct(p.shape, p.dtype) for p in parts], n, 0, build)


def _row_tile(rows, most=512, sublanes=16):
    return max(t for t in range(sublanes, most + 1, sublanes) if rows % t == 0)


def _partial_sum(name, own, recv, chip, comm=None):
    _, R, C = recv.shape
    tr = _row_tile(R)

    def body(o_ref, r_ref, p_ref):
        p_ref[...] = ((o_ref[...] + r_ref[0].astype(F32)) + r_ref[1].astype(F32)) + r_ref[2].astype(F32)

    if own.shape[0] != R:
        assert comm is None and own.shape[0] == N_CHIPS * R
        total = pl.pallas_call(
            lambda chip_ref, *refs: body(*refs), name=name,
            grid_spec=pltpu.PrefetchScalarGridSpec(
                num_scalar_prefetch=1, grid=(R // tr,),
                in_specs=[pl.BlockSpec((tr, C), lambda i, chip_ref: (chip_ref[0] * (R // tr) + i, 0)),
                          pl.BlockSpec((3, tr, C), lambda i, chip_ref: (0, i, 0))],
                out_specs=pl.BlockSpec((tr, C), lambda i, chip_ref: (i, 0))),
            out_shape=jax.ShapeDtypeStruct((R, C), F32), compiler_params=_params(("parallel",)),
        )(chip.reshape(1), own, recv)
        return [total], []
    return _call(name, body, grid=(R // tr,),
                 in_specs=[pl.BlockSpec((tr, C), lambda i: (i, 0)), pl.BlockSpec((3, tr, C), lambda i: (0, i, 0))],
                 out_specs=[pl.BlockSpec((tr, C), lambda i: (i, 0))], out_shape=[jax.ShapeDtypeStruct((R, C), F32)],
                 args=[own, recv], semantics=("parallel",), comm=comm)


def _adam_vals(w, g, m, v):
    m = ADAM_B1 * m + (1.0 - ADAM_B1) * g
    v = ADAM_B2 * v + (1.0 - ADAM_B2) * (g * g)
    m_hat = m / (1.0 - ADAM_B1 ** ADAM_STEP)
    v_hat = v / (1.0 - ADAM_B2 ** ADAM_STEP)
    delta = -ADAM_LR * (m_hat / (jnp.sqrt(v_hat) + ADAM_EPS) + ADAM_WD * w)
    return delta, m, v


def _adamw(name, w, m, v, mine, other, comm=None):
    R, C = w.shape
    tr = _row_tile(R)

    def body(w_ref, m_ref, v_ref, s_ref, n_ref, g_ref, d_ref, nm_ref, nv_ref):
        g = s_ref[...] + n_ref[...]
        d, nm, nv = _adam_vals(w_ref[...], g, m_ref[...], v_ref[...])
        g_ref[...], d_ref[...], nm_ref[...], nv_ref[...] = g, d, nm, nv

    spec = pl.BlockSpec((tr, C), lambda i: (i, 0))
    return _call(name, body, grid=(R // tr,), in_specs=[spec] * 5, out_specs=[spec] * 4,
                 out_shape=[jax.ShapeDtypeStruct((R, C), F32)] * 4, args=[w, m, v, mine, other],
                 semantics=("parallel",), comm=comm)


def _adamw_by_halves(name, w, m, v, mine, other, core):
    R, C = w.shape
    tr = _row_tile(R // 2)
    per_half = R // 2 // tr

    def body(c_ref, w_ref, m_ref, v_ref, s_ref, n_ref, g_ref, d_ref, nm_ref, nv_ref):
        g = jnp.where(pl.program_id(0) // per_half == c_ref[0, 0], s_ref[...], n_ref[...])
        d, nm, nv = _adam_vals(w_ref[...], g, m_ref[...], v_ref[...])
        g_ref[...], d_ref[...], nm_ref[...], nv_ref[...] = g, d, nm, nv

    spec = pl.BlockSpec((tr, C), lambda i: (i, 0))
    part = pl.BlockSpec((tr, C), lambda i: (i % per_half, 0))
    return pl.pallas_call(
        body, name=name, grid=(R // tr,),
        in_specs=[pl.BlockSpec(memory_space=pltpu.SMEM), spec, spec, spec, part, part], out_specs=[spec] * 4,
        out_shape=[jax.ShapeDtypeStruct((R, C), F32)] * 4, compiler_params=_params(("parallel",)),
    )(core, w, m, v, mine, other)


SMALL_LAYOUT = dict(norm_mix_g=(0, 1, 1024), b_in=(1, 8, 7424), hgrn_norm_g=(9, 1, 1024), norm_ffn_g=(10, 1, 1024),
                    norm_final_g=(11, 1, 1024), hgrn_lb_logits=(12, 2, 2048), attn_sinks=(14, 1, 16))
SMALL_LOSS_ROW, SMALL_ROWS = 15, 16


def _pack_small(grads, loss):
    rows = [jnp.pad(grads[name].astype(F32).reshape(-1), (0, nrows * D_MODEL - n))
            for name, (_, nrows, n) in SMALL_LAYOUT.items()]
    rows.append(jnp.pad(loss.astype(F32).reshape(1), (0, D_MODEL - 1)))
    return jnp.concatenate(rows).reshape(SMALL_ROWS, D_MODEL)


def _adamw_small(w, m, v, g_all):
    names = list(SMALL_LAYOUT)
    n = len(names)

    def body(a_ref, *refs):
        ins, outs = refs[:3 * n], refs[3 * n:]
        g_all_rows = a_ref[0]
        for dev in range(1, 8):
            g_all_rows = g_all_rows + a_ref[dev]
        for i, name in enumerate(names):
            first, nrows, count = SMALL_LAYOUT[name]
            w_ref, m_ref, v_ref = ins[3 * i:3 * i + 3]
            if w_ref.shape[0] == nrows:
                g = g_all_rows[first:first + nrows, :w_ref.shape[1]]
            else:
                last = count - (nrows - 1) * D_MODEL
                g = jnp.concatenate([g_all_rows[r:r + 1, :] for r in range(first, first + nrows - 1)]
                                    + [g_all_rows[first + nrows - 1:first + nrows, :last]], axis=1)
            d, nm, nv = _adam_vals(w_ref[...], g, m_ref[...], v_ref[...])
            for o_ref, val in zip(outs[4 * i:4 * i + 4], (g, d, nm, nv)):
                o_ref[...] = val
        outs[4 * n][...] = g_all_rows[SMALL_LOSS_ROW:SMALL_LOSS_ROW + 1, 0:1]

    res = pl.pallas_call(
        body, name="adamw_small",
        out_shape=[jax.ShapeDtypeStruct(w[name].shape, F32) for name in names for _ in range(4)]
        + [jax.ShapeDtypeStruct((1, 1), F32)],
    )(g_all, *[t[name] for name in names for t in (w, m, v)])
    return {name: res[4 * i:4 * i + 4] for i, name in enumerate(names)}, res[4 * n]


MATRICES = ("w_in", "w_branch_attn", "w_branch_hgrn", "w_out", "w_ffn_gate", "w_ffn_up", "w_ffn_down")
COLUMN_SHARDED = ("w_in", "w_ffn_gate", "w_ffn_up")
WEIGHTS = ("norm_mix_g", "w_in", "b_in", "attn_sinks", "hgrn_lb_logits", "hgrn_norm_g", "w_branch_attn",
           "w_branch_hgrn", "w_out", "norm_ffn_g", "w_ffn_gate", "w_ffn_up", "w_ffn_down", "norm_final_g")


def kernel(x, norm_mix_g, w_in, b_in, attn_sinks, hgrn_lb_logits, hgrn_norm_g, w_branch_attn, w_branch_hgrn, w_out, norm_ffn_g, w_ffn_gate, w_ffn_up, w_ffn_down, norm_final_g, loss_target, m_norm_mix_g, m_w_in, m_b_in, m_attn_sinks, m_hgrn_lb_logits, m_hgrn_norm_g, m_w_branch_attn, m_w_branch_hgrn, m_w_out, m_norm_ffn_g, m_w_ffn_gate, m_w_ffn_up, m_w_ffn_down, m_norm_final_g, v_norm_mix_g, v_w_in, v_b_in, v_attn_sinks, v_hgrn_lb_logits, v_hgrn_norm_g, v_w_branch_attn, v_w_branch_hgrn, v_w_out, v_norm_ffn_g, v_w_ffn_gate, v_w_ffn_up, v_w_ffn_down, v_norm_final_g):
    given = dict(locals())
    w = {n: given[n] for n in WEIGHTS}
    m = {n: given["m_" + n] for n in WEIGHTS}
    v = {n: given["v_" + n] for n in WEIGHTS}

    block = lambda a, n: jnp.transpose(a[0]) if n in COLUMN_SHARDED else a[0]
    unblock = lambda a, n: (jnp.transpose(a) if n in COLUMN_SHARDED else a)[None]
    net = _Net({n: block(w[n], n).astype(MXU_DTYPE) for n in MATRICES})
    net.gathered(("w_in",), [_gather_by_neighbours("gather_w_in", net.shards["w_in"])])
    vec = dict(norm_mix_g=norm_mix_g, b_in=b_in, attn_sinks=attn_sinks, hgrn_lb_logits=hgrn_lb_logits,
               hgrn_norm_g=hgrn_norm_g, norm_ffn_g=norm_ffn_g, norm_final_g=norm_final_g.reshape(1, D_MODEL))
    loss_part, dx, d_vecs = _local_step(x[0], loss_target[0], vec, net)

    small_all, = net.received(*net.last, carried=_small_copies(_pack_small(d_vecs, loss_part)))
    grads, deltas, new_m, new_v = {}, {}, {}, {}
    for n in ("w_ffn_down", "w_ffn_gate", "w_ffn_up", "w_out", "w_branch_attn", "w_branch_hgrn"):
        res, got = _adamw("adamw_" + n, block(w[n], n), block(m[n], n), block(v[n], n), net.sums[n], net.other[n],
                          comm=net.swap(("w_in",)) if n == "w_ffn_down" else None)
        if n == "w_ffn_down":
            net.other["w_in"], = got
        grads[n], deltas[n], new_m[n], new_v[n] = (unblock(r, n) for r in res)
    n = "w_in"
    res = _adamw_by_halves("adamw_" + n, block(w[n], n), block(m[n], n), block(v[n], n), net.sums[n], net.other[n],
                           _place()[2].reshape(1, 1))
    grads[n], deltas[n], new_m[n], new_v[n] = (unblock(r, n) for r in res)
    rows = lambda t: {n: t[n].reshape(-1, t[n].shape[-1]) for n in SMALL_LAYOUT}
    res, loss = _adamw_small(rows(w), rows(m), rows(v), small_all)
    for n, four in res.items():
        grads[n], deltas[n], new_m[n], new_v[n] = (r.reshape(w[n].shape) for r in four)
    loss = loss.reshape(())
    return (loss, dx[None], *[grads[n] for n in WEIGHTS], *[deltas[n] for n in WEIGHTS],
            *[new_m[n] for n in WEIGHTS], *[new_v[n] for n in WEIGHTS])
```

```python
import collections
import functools
import math

import jax
import jax.numpy as jnp
from jax import lax
from jax.experimental import pallas as pl
from jax.experimental.pallas import tpu as pltpu

F32 = jnp.float32
BF16 = jnp.bfloat16
MXU_DTYPE = jnp.bfloat16
SAVED_DTYPE = jnp.bfloat16
MESH_ID = pl.DeviceIdType.MESH

D_MODEL = 1024
HEAD_DIM = 64
Q_HEADS = 16
KV_HEADS = 2
GROUP = Q_HEADS // KV_HEADS
KV_WIDTH = KV_HEADS * HEAD_DIM
ATTN_BLOCK = 128
HGRN_HEADS = 8
HGRN_K = 128
CHUNK = 64
HGRN_TOKENS = 256
FFN = 2816
IN_SPLITS = (1024, 256, 4096, 2048)
EPS = 1e-6
NEG_INF = -1e30
ADAM_LR, ADAM_B1, ADAM_B2, ADAM_EPS, ADAM_WD, ADAM_STEP = 0.001, 0.9, 0.999, 1e-08, 0.01, 10
N_CHIPS = 4
VMEM_LIMIT = 60 * 1024 * 1024
ROW_ALIGN = 16


def _params(sem=None):
    return pltpu.CompilerParams(dimension_semantics=sem, vmem_limit_bytes=VMEM_LIMIT)


def _sigmoid(v):
    return 0.5 * jnp.tanh(0.5 * v) + 0.5


def _dot(a, b, dims):
    return lax.dot_general(a.astype(MXU_DTYPE), b.astype(MXU_DTYPE), (dims, ((), ())),
                           preferred_element_type=F32)


def _nn(a, b):
    return _dot(a, b, ((1,), (0,)))


def _nt(a, b):
    return _dot(a, b, ((1,), (1,)))


def _tn(a, b):
    return _dot(a, b, ((0,), (0,)))


HBM_SPEC = pl.BlockSpec(memory_space=pl.ANY)


class _Carried:
    def __init__(self, arrays, out_shapes, n_remote, n_local, build):
        self.parts = [(len(arrays), len(out_shapes), build)]
        self.arrays, self.out_shapes = list(arrays), list(out_shapes)
        self.scratch = [pltpu.SemaphoreType.DMA((n_remote,)), pltpu.SemaphoreType.DMA((n_remote,)),
                        pltpu.SemaphoreType.DMA((max(n_local, 1),))]

    def __add__(self, other):
        both = _Carried([], [], 1, 0, None)
        both.parts = self.parts + other.parts
        both.arrays, both.out_shapes = self.arrays + other.arrays, self.out_shapes + other.out_shapes
        both.scratch = self.scratch + other.scratch
        return both

    def _built(self, ins, outs, sems):
        for p, (ni, no, build) in enumerate(self.parts):
            yield build(ins[:ni], outs[:no], *sems[3 * p:3 * p + 3])
            ins, outs = ins[ni:], outs[no:]

    def start(self, ins, outs, sems):
        core = lax.axis_index("c")
        for sends, _, local, *other_order in self._built(ins, outs, sems):
            for cp in local:
                cp.start()
            if not other_order:
                for cp in sends:
                    cp.start()
                continue

            @pl.when(core == 0)
            def _():
                for cp in sends:
                    cp.start()

            @pl.when(core == 1)
            def _():
                for cp in other_order[0]:
                    cp.start()

    def wait(self, ins, outs, sems):
        for sends, recvs, local, *_ in self._built(ins, outs, sems):
            for cp in recvs:
                cp.wait_recv()
            for cp in sends:
                cp.wait_send()
            for cp in local:
                cp.wait()


def _join(*comms):
    comms = [c for c in comms if c is not None]
    return functools.reduce(lambda a, b: a + b, comms) if comms else None


def _call(name, body, *, grid, in_specs, out_specs, out_shape, args, scratch=(), semantics=None, comm=None,
          aliases=None):
    n_in, n_out, n_scr = len(in_specs), len(out_specs), len(scratch)
    aliases = aliases or {}
    if comm is None:
        res = pl.pallas_call(body, name=name, grid=grid, in_specs=in_specs, out_specs=out_specs, out_shape=out_shape,
                             scratch_shapes=list(scratch), input_output_aliases=aliases,
                             compiler_params=_params(semantics))(*args)
        return list(res), []
    ci, co = len(comm.arrays), len(comm.out_shapes)

    def carrying(*refs):
        ins, refs = refs[:n_in], refs[n_in:]
        c_ins, refs = refs[:ci], refs[ci:]
        outs, refs = refs[:n_out], refs[n_out:]
        c_outs, refs = refs[:co], refs[co:]
        scr, sems = refs[:n_scr], refs[n_scr:]
        if not grid:
            comm.start(c_ins, c_outs, sems)
            body(*ins, *outs, *scr)
            comm.wait(c_ins, c_outs, sems)
            return
        first = functools.reduce(jnp.logical_and, [pl.program_id(a) == 0 for a in range(len(grid))])
        last = functools.reduce(jnp.logical_and, [pl.program_id(a) == g - 1 for a, g in enumerate(grid)])

        @pl.when(first)
        def _():
            comm.start(c_ins, c_outs, sems)

        body(*ins, *outs, *scr)

        @pl.when(last)
        def _():
            comm.wait(c_ins, c_outs, sems)

    res = pl.pallas_call(
        carrying, name=name, grid=grid, in_specs=list(in_specs) + [HBM_SPEC] * ci,
        out_specs=list(out_specs) + [HBM_SPEC] * co, out_shape=list(out_shape) + comm.out_shapes,
        scratch_shapes=list(scratch) + comm.scratch, input_output_aliases=aliases,
        compiler_params=_params(("arbitrary",) * len(grid) if grid else None),
    )(*args, *comm.arrays)
    return list(res[:n_out]), list(res[n_out:])


_Cols = collections.namedtuple("_Cols", "array first cols")
_Into = collections.namedtuple("_Into", "rows first held")


def _weight_grad(name, a, b, *, tm, tk, a_colsum=False, into=None):
    cols = a if isinstance(a, _Cols) else _Cols(a, 0, a.shape[1])
    a = cols.array
    (T, N), M = b.shape, cols.cols
    tk = min(tk, T)
    assert cols.first % tm == 0 and M % tm == 0 and T % tk == 0, (name, cols.first, M, tm, T, tk)
    ni, nk, tile0 = M // tm, T // tk, cols.first // tm
    held = list(into.held) if into is not None and into.held is not None else []

    def body(a_ref, b_ref, *rest):
        keep_ref, send_ref = rest[len(held):len(held) + 2]
        sums_ref = rest[len(held) + 2] if a_colsum else None
        if nk == 1:
            acc = _tn(a_ref[...], b_ref[...])
            keep_ref[...], send_ref[...] = acc, acc.astype(send_ref.dtype)
            if a_colsum:
                sums_ref[...] = jnp.sum(a_ref[...].astype(F32), axis=0, keepdims=True)
            return
        acc_ref = rest[-1]
        k = pl.program_id(1)

        @pl.when(k == 0)
        def _():
            acc_ref[...] = jnp.zeros_like(acc_ref)
            if a_colsum:
                sums_ref[...] = jnp.zeros((1, tm), F32)

        if a_colsum:
            sums_ref[...] += jnp.sum(a_ref[...].astype(F32), axis=0, keepdims=True)
        acc_ref[...] += _tn(a_ref[...], b_ref[...])

        @pl.when(k == nk - 1)
        def _():
            keep_ref[...], send_ref[...] = acc_ref[...], acc_ref[...].astype(send_ref.dtype)

    if into is None:
        rows, out_spec = M, pl.BlockSpec((tm, N), lambda i, k: (i, 0))
    else:
        rows = into.rows
        out_spec = pl.BlockSpec((pl.Element(tm), pl.Element(N)),
                                lambda i, k: (pl.multiple_of(into.first + i * tm, ROW_ALIGN), 0))
    out_shape = [jax.ShapeDtypeStruct((rows, N), F32), jax.ShapeDtypeStruct((rows, N), MXU_DTYPE)]
    out_specs = [out_spec, out_spec]
    if a_colsum:
        out_shape.append(jax.ShapeDtypeStruct((1, M), F32))
        out_specs.append(pl.BlockSpec((1, tm), lambda i, k: (0, i)))
    return pl.pallas_call(
        body, name=name, grid=(ni, nk),
        in_specs=[pl.BlockSpec((tk, tm), lambda i, k: (k, tile0 + i)), pl.BlockSpec((tk, N), lambda i, k: (k, 0))]
        + [HBM_SPEC] * len(held),
        out_specs=out_specs, out_shape=out_shape, scratch_shapes=[pltpu.VMEM((tm, N), F32)] if nk > 1 else [],
        input_output_aliases={2 + p: p for p in range(len(held))},
        compiler_params=_params(("parallel", "arbitrary")),
    )(a, b, *held)


def _ffn_fwd(merged, w_out, x, gain, w_gate_t, w_up_t, *, tm, comm=None):
    (T, D), F = x.shape, w_gate_t.shape[0]

    def body(m_ref, wo_ref, x_ref, g_ref, wg_ref, wu_ref, h_ref, u_ref, gate_ref, up_ref, z_ref):
        h = x_ref[...] + _nn(m_ref[...], wo_ref[...])
        h_ref[...] = h
        u = (h * lax.rsqrt(jnp.mean(h * h, axis=-1, keepdims=True) + EPS) * g_ref[...]).astype(u_ref.dtype)
        u_ref[...] = u
        gate, up = _nt(u, wg_ref[...]), _nt(u, wu_ref[...])
        gate_ref[...], up_ref[...] = gate.astype(gate_ref.dtype), up.astype(up_ref.dtype)
        z_ref[...] = (gate * _sigmoid(gate) * up).astype(z_ref.dtype)

    rows = lambda n: pl.BlockSpec((tm, n), lambda i: (i, 0))
    fixed = _fixed_spec
    return _call("ffn_hidden", body, grid=(T // tm,),
                 in_specs=[rows(D), fixed(w_out), rows(D), fixed(gain), fixed(w_gate_t), fixed(w_up_t)],
                 out_specs=[rows(D), rows(D), rows(F), rows(F), rows(F)],
                 out_shape=[jax.ShapeDtypeStruct((T, D), F32), jax.ShapeDtypeStruct((T, D), MXU_DTYPE)]
                 + [jax.ShapeDtypeStruct((T, F), SAVED_DTYPE)] * 2 + [jax.ShapeDtypeStruct((T, F), MXU_DTYPE)],
                 args=[merged, w_out, x, gain, w_gate_t, w_up_t], semantics=("parallel",), comm=comm)


def _in_proj(x, gain, w_in_t, b_in, *, tm, comm=None):
    T, D = x.shape
    bounds = [sum(IN_SPLITS[:i]) for i in range(len(IN_SPLITS) + 1)]

    def body(x_ref, g_ref, w_ref, b_ref, u_ref, *piece_refs):
        xv = x_ref[...]
        r = lax.rsqrt(jnp.mean(xv * xv, axis=-1, keepdims=True) + EPS)
        u = (xv * r * g_ref[...]).astype(u_ref.dtype)
        u_ref[...] = u
        for o_ref, lo, hi in zip(piece_refs, bounds[:-1], bounds[1:]):
            o_ref[...] = (_nt(u, w_ref[lo:hi, :]) + b_ref[:, lo:hi]).astype(o_ref.dtype)

    rows = lambda n: pl.BlockSpec((tm, n), lambda i: (i, 0))
    fixed = _fixed_spec
    dtypes = (MXU_DTYPE, MXU_DTYPE, F32, F32)
    return _call("in_proj", body, grid=(T // tm,),
                 in_specs=[rows(D), fixed(gain), fixed(w_in_t), fixed(b_in)],
                 out_specs=[rows(D)] + [rows(n) for n in IN_SPLITS],
                 out_shape=[jax.ShapeDtypeStruct((T, D), MXU_DTYPE)]
                 + [jax.ShapeDtypeStruct((T, n), dt) for n, dt in zip(IN_SPLITS, dtypes)],
                 args=[x, gain, w_in_t, b_in], semantics=("parallel",), comm=comm)


def _row_spec(tm, n):
    return pl.BlockSpec((tm, n), lambda i: (i, 0))


def _fixed_spec(a):
    return pl.BlockSpec(a.shape, lambda i: (0,) * a.ndim, pipeline_mode=pl.Buffered(1))


def _partials_spec(n):
    return pl.BlockSpec((8, n), lambda i: (i, 0))


def _ffn_tail(z, w_down, h1, target, gain, gate, up, *, tm):
    (T, F), D = z.shape, h1.shape[1]

    def body(z_ref, w_ref, h_ref, t_ref, g_ref, gate_ref, up_ref, dh_ref, dhb_ref, dgate_ref, dup_ref, dg_ref, l_ref):
        h2 = h_ref[...] + _nn(z_ref[...], w_ref[...])
        r = lax.rsqrt(jnp.mean(h2 * h2, axis=-1, keepdims=True) + EPS)
        xhat = h2 * r
        err = xhat * g_ref[...] - t_ref[...]
        part = 0.5 * jnp.sum(jnp.sum(err * err, axis=-1, keepdims=True), axis=0, keepdims=True) / D
        dy = err / D
        dxh = dy * g_ref[...]
        dh2 = r * (dxh - xhat * jnp.mean(dxh * xhat, axis=-1, keepdims=True))
        dh_ref[...] = dh2
        dhb = dh2.astype(dhb_ref.dtype)
        dhb_ref[...] = dhb
        dg_ref[...] = jnp.broadcast_to(jnp.sum(dy * xhat, axis=0, keepdims=True), dg_ref.shape)
        l_ref[...] = jnp.broadcast_to(part, l_ref.shape)
        dz = _nt(dhb, w_ref[...])
        gv, upv = gate_ref[...].astype(F32), up_ref[...].astype(F32)
        s = _sigmoid(gv)
        dgate_ref[...] = (dz * upv * (s * (1.0 + gv * (1.0 - s)))).astype(dgate_ref.dtype)
        dup_ref[...] = (dz * (gv * s)).astype(dup_ref.dtype)

    low = lambda n: jax.ShapeDtypeStruct((T, n), MXU_DTYPE)
    part = jax.ShapeDtypeStruct((8 * (T // tm), D), F32)
    return pl.pallas_call(
        body, name="ffn_tail", grid=(T // tm,),
        in_specs=[_row_spec(tm, F), _fixed_spec(w_down), _row_spec(tm, D), _row_spec(tm, D), _fixed_spec(gain),
                  _row_spec(tm, F), _row_spec(tm, F)],
        out_specs=[_row_spec(tm, D), _row_spec(tm, D), _row_spec(tm, F), _row_spec(tm, F), _partials_spec(D),
                   _partials_spec(D)],
        out_shape=[jax.ShapeDtypeStruct((T, D), F32), low(D), low(F), low(F), part, part],
        compiler_params=_params(("parallel",)),
    )(z, w_down, h1, target, gain, gate, up)


def _ffn_in_bwd(dgate, dup, w_gate_t, w_up_t, h1, gain, dres, *, tm, comm=None):
    (T, F), D = dgate.shape, h1.shape[1]

    def body(dg_ref, du_ref, wg_ref, wu_ref, h_ref, g_ref, r_ref, dh_ref, dhb_ref, dgain_ref):
        d_u2 = _nn(dg_ref[...], wg_ref[...]) + _nn(du_ref[...], wu_ref[...])
        dx, dgain = _rmsnorm_bwd_vals(d_u2, h_ref[...], g_ref[...])
        dh = r_ref[...] + dx
        dh_ref[...] = dh
        dhb_ref[...] = dh.astype(dhb_ref.dtype)
        dgain_ref[...] = jnp.broadcast_to(dgain, dgain_ref.shape)

    return _call("d_ffn_in", body, grid=(T // tm,),
                 in_specs=[_row_spec(tm, F), _row_spec(tm, F), _fixed_spec(w_gate_t), _fixed_spec(w_up_t),
                           _row_spec(tm, D), _fixed_spec(gain), _row_spec(tm, D)],
                 out_specs=[_row_spec(tm, D), _row_spec(tm, D), _partials_spec(D)],
                 out_shape=[jax.ShapeDtypeStruct((T, D), F32), jax.ShapeDtypeStruct((T, D), MXU_DTYPE),
                            jax.ShapeDtypeStruct((8 * (T // tm), D), F32)],
                 args=[dgate, dup, w_gate_t, w_up_t, h1, gain, dres], semantics=("parallel",), comm=comm)


def _in_proj_bwd(pieces, w_in_t, x, gain, dres, *, tm, comm=None):
    T, D = x.shape
    n = len(pieces)

    def body(*refs):
        dps, (w_ref, x_ref, g_ref, r_ref, dx_ref, dgain_ref) = refs[:n], refs[n:]
        d_u = None
        for dp_ref, (dp, first) in zip(dps, pieces):
            term = _nn(dp_ref[...], w_ref[first:first + dp.shape[1], :])
            d_u = term if d_u is None else d_u + term
        dx, dgain = _rmsnorm_bwd_vals(d_u, x_ref[...], g_ref[...])
        dx_ref[...] = r_ref[...] + dx
        dgain_ref[...] = jnp.broadcast_to(dgain, dgain_ref.shape)

    return _call("d_u", body, grid=(T // tm,),
                 in_specs=[_row_spec(tm, dp.shape[1]) for dp, _ in pieces]
                 + [_fixed_spec(w_in_t), _row_spec(tm, D), _fixed_spec(gain), _row_spec(tm, D)],
                 out_specs=[_row_spec(tm, D), _partials_spec(D)],
                 out_shape=[jax.ShapeDtypeStruct((T, D), F32), jax.ShapeDtypeStruct((8 * (T // tm), D), F32)],
                 args=[dp for dp, _ in pieces] + [w_in_t, x, gain, dres], semantics=("parallel",), comm=comm)


def _merge_fwd(y_a, y_b, w_a, w_b, gates, *, tm):
    T, D = y_a.shape

    def body(ya_ref, yb_ref, wa_ref, wb_ref, ga_ref, gb_ref, pa_ref, pb_ref, m_ref):
        pa, pb = _nn(ya_ref[...], wa_ref[...]), _nn(yb_ref[...], wb_ref[...])
        pa_ref[...], pb_ref[...] = pa.astype(pa_ref.dtype), pb.astype(pb_ref.dtype)
        m_ref[...] = (_sigmoid(ga_ref[...]) * pa + _sigmoid(gb_ref[...]) * pb).astype(m_ref.dtype)

    rows = pl.BlockSpec((tm, D), lambda i: (i, 0))
    whole = pl.BlockSpec((D, D), lambda i: (0, 0), pipeline_mode=pl.Buffered(1))
    return pl.pallas_call(
        body, name="branch_merge", grid=(T // tm,),
        in_specs=[rows, rows, whole, whole, rows, pl.BlockSpec((tm, D), lambda i: (i, 1))], out_specs=[rows] * 3,
        out_shape=[jax.ShapeDtypeStruct((T, D), SAVED_DTYPE)] * 2 + [jax.ShapeDtypeStruct((T, D), MXU_DTYPE)],
        compiler_params=_params(("parallel",)),
    )(y_a, y_b, w_a, w_b, gates, gates)


def _merge_bwd(dh, w_out, w_a, w_b, p_a, p_b, gates, *, tm, d_in_width, first):
    T, D = dh.shape

    def body(dh_ref, wo_ref, wa_ref, wb_ref, pa_ref, pb_ref, ga_ref, gb_ref, dpa_ref, dpb_ref, dya_ref, dyb_ref,
             din_ref):
        dm = _nt(dh_ref[...], wo_ref[...])
        sa, sb = _sigmoid(ga_ref[...]), _sigmoid(gb_ref[...])
        dpa, dpb = (dm * sa).astype(dpa_ref.dtype), (dm * sb).astype(dpb_ref.dtype)
        dpa_ref[...], dpb_ref[...] = dpa, dpb
        din_ref[:, :D] = (dm * pa_ref[...].astype(F32) * sa * (1.0 - sa)).astype(din_ref.dtype)
        din_ref[:, D:] = (dm * pb_ref[...].astype(F32) * sb * (1.0 - sb)).astype(din_ref.dtype)
        dya_ref[...] = _nt(dpa, wa_ref[...]).astype(dya_ref.dtype)
        dyb_ref[...] = _nt(dpb, wb_ref[...]).astype(dyb_ref.dtype)

    rows = pl.BlockSpec((tm, D), lambda i: (i, 0))
    whole = pl.BlockSpec((D, D), lambda i: (0, 0), pipeline_mode=pl.Buffered(1))
    low = jax.ShapeDtypeStruct((T, D), MXU_DTYPE)
    return pl.pallas_call(
        body, name="d_branch_merge", grid=(T // tm,),
        in_specs=[rows, whole, whole, whole, rows, rows, rows, pl.BlockSpec((tm, D), lambda i: (i, 1))],
        out_specs=[rows] * 4 + [pl.BlockSpec((pl.Element(tm), pl.Element(2 * D)), lambda i: (
            pl.multiple_of(i * tm, ROW_ALIGN), first))],
        out_shape=[low] * 3 + [jax.ShapeDtypeStruct((T, D), F32), jax.ShapeDtypeStruct((T, d_in_width), MXU_DTYPE)],
        compiler_params=_params(("parallel",)),
    )(dh, w_out, w_a, w_b, p_a, p_b, gates, gates)


def _colsum_partials(p):
    return jnp.sum(p.reshape(-1, 8, p.shape[-1])[:, 0, :], axis=0, keepdims=True)


def _rmsnorm_bwd_vals(dy, xin, g):
    rstd = lax.rsqrt(jnp.mean(xin * xin, axis=-1, keepdims=True) + EPS)
    xhat = xin * rstd
    dg = jnp.sum(dy * xhat, axis=0, keepdims=True)
    dxh = dy * g
    dx = rstd * (dxh - xhat * jnp.mean(dxh * xhat, axis=-1, keepdims=True))
    return dx, dg


ATTN_SCALE = 1.0 / math.sqrt(HEAD_DIM)
GROUP_LANES = GROUP * ATTN_BLOCK
PAIR = 2 * HEAD_DIM


def _attn_mask():
    kj = lax.broadcasted_iota(jnp.int32, (ATTN_BLOCK, GROUP_LANES), 0)
    qi = lax.broadcasted_iota(jnp.int32, (ATTN_BLOCK, GROUP_LANES), 1) & (ATTN_BLOCK - 1)
    return kj <= qi


def _heads_transposed(ref, g, scale=None):
    parts = []
    for a in range(GROUP // 2):
        lo = (g * GROUP // 2 + a) * PAIR
        pair = ref[:, lo:lo + PAIR].astype(F32)
        pair = (pair if scale is None else pair * scale).T
        parts += [pair[:HEAD_DIM], pair[HEAD_DIM:]]
    return jnp.concatenate(parts, axis=1).astype(MXU_DTYPE)


def _heads_back(ref, g, vt):
    for a in range(GROUP // 2):
        lo = (g * GROUP // 2 + a) * PAIR
        pair = jnp.concatenate([vt[:, (2 * a) * ATTN_BLOCK:(2 * a + 1) * ATTN_BLOCK],
                                vt[:, (2 * a + 1) * ATTN_BLOCK:(2 * a + 2) * ATTN_BLOCK]], axis=0)
        ref[:, lo:lo + PAIR] = pair.T.astype(ref.dtype)


def _kv_parts(kv_ref, g):
    ks = slice(g * HEAD_DIM, (g + 1) * HEAD_DIM)
    vs = slice(KV_WIDTH + g * HEAD_DIM, KV_WIDTH + (g + 1) * HEAD_DIM)
    return kv_ref[:, ks].astype(MXU_DTYPE), kv_ref[:, vs].astype(MXU_DTYPE)


def _sink_rows(sinks):
    return jnp.repeat(sinks.reshape(KV_HEADS, GROUP), ATTN_BLOCK, axis=1)


def _attn_fwd(pq, pkv, sinks, comm=None):
    T = pq.shape[0]
    nb = T // ATTN_BLOCK

    def body(q_ref, kvc_ref, kvp_ref, s_ref, y_ref, lse_ref):
        mask_c = _attn_mask()
        has_prev = pl.program_id(0) > 0
        for g in range(KV_HEADS):
            (kc, vc), (kp, vp) = _kv_parts(kvc_ref, g), _kv_parts(kvp_ref, g)
            qt = _heads_transposed(q_ref, g, ATTN_SCALE)
            s = jnp.where(mask_c, _nn(kc, qt), jnp.where(has_prev, _nn(kp, qt), NEG_INF))
            sink = s_ref[g:g + 1, :]
            m = jnp.maximum(jnp.max(s, axis=0, keepdims=True), sink)
            p = jnp.exp(s - m)
            den = jnp.sum(p, axis=0, keepdims=True) + jnp.exp(sink - m)
            pc = jnp.where(mask_c, p, 0.0)
            _heads_back(y_ref, g, (_tn(vc, pc) + _tn(vp, p - pc)) / den)
            lse = m + jnp.log(den)
            for i in range(GROUP):
                lse_ref[g * GROUP + i:g * GROUP + i + 1, :] = lse[:, i * ATTN_BLOCK:(i + 1) * ATTN_BLOCK]

    return _call(
        "attn_fwd", body, grid=(nb,),
        in_specs=[pl.BlockSpec((ATTN_BLOCK, D_MODEL), lambda n: (n, 0)),
                  pl.BlockSpec((ATTN_BLOCK, 2 * KV_WIDTH), lambda n: (n, 0)),
                  pl.BlockSpec((ATTN_BLOCK, 2 * KV_WIDTH), lambda n: (jnp.maximum(n - 1, 0), 0)),
                  pl.BlockSpec((KV_HEADS, GROUP_LANES), lambda n: (0, 0))],
        out_specs=[pl.BlockSpec((ATTN_BLOCK, D_MODEL), lambda n: (n, 0)),
                   pl.BlockSpec((Q_HEADS, ATTN_BLOCK), lambda n: (0, n))],
        out_shape=[jax.ShapeDtypeStruct((T, D_MODEL), MXU_DTYPE), jax.ShapeDtypeStruct((Q_HEADS, T), F32)],
        args=[pq, pkv, pkv, _sink_rows(sinks)], semantics=("parallel",), comm=comm)


def _attn_bwd(pq, pkv, sinks, lse, dy, d_in, comm=None):
    T = pq.shape[0]
    nb = T // ATTN_BLOCK
    cur = lambda n: (jnp.minimum(n, nb - 1), 0)
    done = D_MODEL + 2 * KV_WIDTH

    def body(q_ref, kvc_ref, kvp_ref, s_ref, lse_ref, dy_ref, _, out_ref, ds_ref, carry, top, bot, dq_ref):
        n = pl.program_id(0)

        @pl.when(n == 0)
        def _():
            carry[...] = jnp.zeros_like(carry)
            dq_ref[...] = jnp.zeros_like(dq_ref)
            ds_ref[...] = jnp.zeros_like(ds_ref)

        out_ref[:, :D_MODEL] = dq_ref[...]

        @pl.when(n < nb)
        def _():
            mask_c = _attn_mask()
            valid = jnp.logical_or(mask_c, n > 0)
            for g in range(KV_HEADS):
                ks = slice(g * HEAD_DIM, (g + 1) * HEAD_DIM)
                vs = slice(KV_WIDTH + g * HEAD_DIM, KV_WIDTH + (g + 1) * HEAD_DIM)
                (kc, vc), (kp, vp) = _kv_parts(kvc_ref, g), _kv_parts(kvp_ref, g)
                qt = _heads_transposed(q_ref, g, ATTN_SCALE)
                dot = _heads_transposed(dy_ref, g)
                lse = jnp.concatenate([lse_ref[g * GROUP + i:g * GROUP + i + 1, :] for i in range(GROUP)], axis=1)
                p = jnp.where(valid, jnp.exp(jnp.where(mask_c, _nn(kc, qt), _nn(kp, qt)) - lse), 0.0)
                dp = jnp.where(mask_c, _nn(vc, dot), _nn(vp, dot))
                delta = jnp.sum(p * dp, axis=0, keepdims=True)
                ds = p * (dp - delta)
                ds_c, p_c = jnp.where(mask_c, ds, 0.0), jnp.where(mask_c, p, 0.0)
                ds_p, p_p = ds - ds_c, p - p_c
                _heads_back(dq_ref, g, (_tn(kc, ds_c) + _tn(kp, ds_p)) * ATTN_SCALE)
                bot[:, ks], bot[:, vs] = _nt(ds_c, qt), _nt(p_c, dot)
                top[:, ks], top[:, vs] = _nt(ds_p, qt), _nt(p_p, dot)
                ds_ref[g:g + 1, :] -= jnp.exp(s_ref[g:g + 1, :] - lse) * delta
            out_ref[:, D_MODEL:] = (carry[...] + top[...]).astype(out_ref.dtype)
            carry[...] = bot[...]

        @pl.when(n == nb)
        def _():
            out_ref[:, D_MODEL:] = carry[...].astype(out_ref.dtype)

    return _call(
        "attn_bwd", body, grid=(nb + 1,),
        in_specs=[pl.BlockSpec((ATTN_BLOCK, D_MODEL), cur),
                  pl.BlockSpec((ATTN_BLOCK, 2 * KV_WIDTH), cur),
                  pl.BlockSpec((ATTN_BLOCK, 2 * KV_WIDTH), lambda n: (jnp.maximum(jnp.minimum(n, nb - 1) - 1, 0), 0)),
                  pl.BlockSpec((KV_HEADS, GROUP_LANES), lambda n: (0, 0)),
                  pl.BlockSpec((Q_HEADS, ATTN_BLOCK), lambda n: (0, jnp.minimum(n, nb - 1))),
                  pl.BlockSpec((ATTN_BLOCK, D_MODEL), cur), HBM_SPEC],
        out_specs=[pl.BlockSpec((ATTN_BLOCK, done), lambda n: (jnp.maximum(n - 1, 0), 0)),
                   pl.BlockSpec((KV_HEADS, GROUP_LANES), lambda n: (0, 0))],
        out_shape=[jax.ShapeDtypeStruct(d_in.shape, d_in.dtype), jax.ShapeDtypeStruct((KV_HEADS, GROUP_LANES), F32)],
        scratch=[pltpu.VMEM((ATTN_BLOCK, 2 * KV_WIDTH), F32)] * 3 + [pltpu.VMEM((ATTN_BLOCK, D_MODEL), MXU_DTYPE)],
        args=[pq, pkv, pkv, _sink_rows(sinks), lse, dy, d_in], semantics=("arbitrary",), comm=comm, aliases={6: 0})


def _lower_bound(l):
    m = jnp.maximum(l[0:1], l[1:2])
    e0, e1 = jnp.exp(l[0:1] - m), jnp.exp(l[1:2] - m)
    return e0 / (e0 + e1)


def _tri(lower):
    r = lax.broadcasted_iota(jnp.int32, (CHUNK, CHUNK), 0)
    c = lax.broadcasted_iota(jnp.int32, (CHUNK, CHUNK), 1)
    return (r >= c) if lower else (c >= r)


def _chunk_sum(mask, v):
    ones = mask.astype(BF16)
    hi = v.astype(BF16)
    rest = v - hi.astype(F32)
    mid = rest.astype(BF16)
    lo = (rest - mid.astype(F32)).astype(BF16)
    part = lambda t: lax.dot_general(ones, t, (((1,), (0,)), ((), ())), preferred_element_type=F32)
    return part(hi) + part(mid) + part(lo)


def _hgrn_chunk_inputs(hq, hf, lb, causal):
    half_t = 0.5 * jnp.tanh(0.5 * hf)
    sg, sgn = 0.5 + half_t, 0.5 - half_t
    f = lb + (1.0 - lb) * sg
    kk = (1.0 - lb) * sgn
    sq = _sigmoid(hq)
    q = hq * sq
    b = _chunk_sum(causal, jnp.log(f))
    bm, bl = b[CHUNK // 2 - 1:CHUNK // 2, :], b[CHUNK - 1:CHUNK, :]
    e_qm, e_km = jnp.exp(b - bm), jnp.exp(bm - b)
    e_qs, e_kl = e_qm * jnp.exp(bm), e_km * jnp.exp(bl - bm)
    return dict(sg=sg, sgn=sgn, f=f, kk=kk, sq=sq, q=q, e_qm=e_qm, e_km=e_km, e_qs=e_qs, e_kl=e_kl,
                qm=q * e_qm, km=kk * e_km, qs=q * e_qs, kl=kk * e_kl, el=jnp.exp(bl))


def _hgrn_fwd(ph, lb_logits, norm_g, comm=None):
    T = ph.shape[0]
    nblk, cpb = T // HGRN_TOKENS, HGRN_TOKENS // CHUNK
    col = lambda c: pl.BlockSpec((HGRN_TOKENS, D_MODEL), functools.partial(lambda i, c: (i, c), c=c))

    def body(hq_ref, hf_ref, hi_ref, hg_ref, l_ref, ng_ref, y_ref, o_ref, st_ref, s_ref):
        @pl.when(pl.program_id(0) == 0)
        def _():
            s_ref[...] = jnp.zeros_like(s_ref)

        lb = _lower_bound(l_ref[...])
        causal = _tri(True)
        for c in range(cpb):
            rows = slice(c * CHUNK, (c + 1) * CHUNK)
            t = _hgrn_chunk_inputs(hq_ref[rows, :], hf_ref[rows, :], lb, causal)
            qm, km, qs, kl = (t[n].astype(MXU_DTYPE) for n in ("qm", "km", "qs", "kl"))
            v = hi_ref[rows, :].astype(MXU_DTYPE)
            for h in range(HGRN_HEADS):
                ls = slice(h * HGRN_K, (h + 1) * HGRN_K)
                st = s_ref[h]
                st_ref[c, ls, :] = st
                a = jnp.where(causal, _nt(qm[:, ls], km[:, ls]), 0.0)
                o_ref[rows, ls] = _nn(a, v[:, ls]) + _nt(qs[:, ls], st)
                s_ref[h] = t["el"][:, ls] * st + _tn(v[:, ls], kl[:, ls])
        for h in range(HGRN_HEADS):
            ls = slice(h * HGRN_K, (h + 1) * HGRN_K)
            o = o_ref[:, ls]
            r = lax.rsqrt(jnp.mean(o * o, axis=-1, keepdims=True) + EPS)
            y_ref[:, ls] = (o * r * ng_ref[:, ls] * _sigmoid(hg_ref[:, ls])).astype(y_ref.dtype)

    return _call(
        "hgrn_fwd", body, grid=(nblk,),
        in_specs=[col(0), col(1), col(2), col(3),
                  pl.BlockSpec((2, D_MODEL), lambda i: (0, 0)), pl.BlockSpec((1, D_MODEL), lambda i: (0, 0))],
        out_specs=[pl.BlockSpec((HGRN_TOKENS, D_MODEL), lambda i: (i, 0)),
                   pl.BlockSpec((HGRN_TOKENS, D_MODEL), lambda i: (i, 0)),
                   pl.BlockSpec((cpb, D_MODEL, HGRN_K), lambda i: (i, 0, 0))],
        out_shape=[jax.ShapeDtypeStruct((T, D_MODEL), MXU_DTYPE), jax.ShapeDtypeStruct((T, D_MODEL), F32),
                   jax.ShapeDtypeStruct((T // CHUNK, D_MODEL, HGRN_K), F32)],
        scratch=[pltpu.VMEM((HGRN_HEADS, HGRN_K, HGRN_K), F32)],
        args=[ph, ph, ph, ph, lb_logits, norm_g], semantics=("arbitrary",), comm=comm)


def _hgrn_bwd(ph, o_raw, states, dy, lb_logits, norm_g, d_in, first, comm=None):
    T = ph.shape[0]
    nblk, cpb = T // HGRN_TOKENS, HGRN_TOKENS // CHUNK
    rev = lambda i: nblk - 1 - i
    col = lambda c: pl.BlockSpec((HGRN_TOKENS, D_MODEL), functools.partial(lambda i, c: (rev(i), c), c=c))
    tok = pl.BlockSpec((HGRN_TOKENS, D_MODEL), lambda i: (rev(i), 0))

    def body(hq_ref, hf_ref, hi_ref, hg_ref, o_ref, st_ref, dy_ref, l_ref, ng_ref, _,
             dph_ref, dng_ref, dl_ref, dst_ref, dlb_ref, do_s, dqm_s, dkm_s, dqs_s, dkl_s, dv_s, del_s):
        i = pl.program_id(0)

        @pl.when(i == 0)
        def _():
            dst_ref[...] = jnp.zeros_like(dst_ref)
            dlb_ref[...] = jnp.zeros_like(dlb_ref)
            dng_ref[...] = jnp.zeros_like(dng_ref)

        lb = _lower_bound(l_ref[...])
        causal, anti = _tri(True), _tri(False)
        row = lax.broadcasted_iota(jnp.int32, (CHUNK, D_MODEL), 0)
        for c in reversed(range(cpb)):
            rows = slice(c * CHUNK, (c + 1) * CHUNK)
            hq = hq_ref[rows, :]
            t = _hgrn_chunk_inputs(hq, hf_ref[rows, :], lb, causal)
            sgg = _sigmoid(hg_ref[rows, :])
            dyv = dy_ref[rows, :]
            for h in range(HGRN_HEADS):
                ls = slice(h * HGRN_K, (h + 1) * HGRN_K)
                o = o_ref[rows, ls]
                r = lax.rsqrt(jnp.mean(o * o, axis=-1, keepdims=True) + EPS)
                nrm = o * r
                g_h = sgg[:, ls]
                dph_ref[rows, 3 * D_MODEL + h * HGRN_K:3 * D_MODEL + (h + 1) * HGRN_K] = (
                    dyv[:, ls] * nrm * ng_ref[:, ls] * g_h * (1.0 - g_h)).astype(dph_ref.dtype)
                dyg = dyv[:, ls] * g_h
                dng_ref[:, ls] += jnp.sum(dyg * nrm, axis=0, keepdims=True)
                dn = dyg * ng_ref[:, ls]
                do_s[:, ls] = r * (dn - nrm * jnp.mean(dn * nrm, axis=-1, keepdims=True))
            qm, km, qs, kl = (t[n].astype(MXU_DTYPE) for n in ("qm", "km", "qs", "kl"))
            v = hi_ref[rows, :].astype(MXU_DTYPE)
            do = do_s[...].astype(MXU_DTYPE)
            for h in range(HGRN_HEADS):
                ls = slice(h * HGRN_K, (h + 1) * HGRN_K)
                st = st_ref[c, ls, :]
                dst = dst_ref[h]
                a = jnp.where(causal, _nt(qm[:, ls], km[:, ls]), 0.0)
                da = jnp.where(causal, _nt(do[:, ls], v[:, ls]), 0.0)
                dv_s[:, ls] = _tn(a, do[:, ls]) + _nt(kl[:, ls], dst)
                dkl_s[:, ls] = _nn(v[:, ls], dst)
                dqs_s[:, ls] = _nn(do[:, ls], st)
                del_s[:, ls] = jnp.sum(dst * st, axis=0, keepdims=True)
                dst_ref[h] = _tn(do[:, ls], qs[:, ls]) + t["el"][:, ls] * dst
                dqm_s[:, ls] = _nn(da, km[:, ls])
                dkm_s[:, ls] = _tn(da, qm[:, ls])
            dqm, dkm, dqs, dkl = dqm_s[...], dkm_s[...], dqs_s[...], dkl_s[...]
            dq = dqm * t["e_qm"] + dqs * t["e_qs"]
            dk = dkm * t["e_km"] + dkl * t["e_kl"]
            t_qm, t_km, t_kl = dqm * t["qm"], dkm * t["km"], dkl * t["kl"]
            db = t_qm - t_km + dqs * t["qs"] - t_kl
            db_mid = jnp.sum(t_km - t_qm, axis=0, keepdims=True)
            db_last = jnp.sum(t_kl, axis=0, keepdims=True) + del_s[...] * t["el"]
            db = db + jnp.where(row == CHUNK // 2 - 1, db_mid, 0.0) + jnp.where(row == CHUNK - 1, db_last, 0.0)
            dlogf = _chunk_sum(anti, db)
            sq, sg, sgn, f = t["sq"], t["sg"], t["sgn"], t["f"]
            dph_ref[rows, 0:D_MODEL] = (dq * (sq * (1.0 + hq * (1.0 - sq)))).astype(dph_ref.dtype)
            dph_ref[rows, D_MODEL:2 * D_MODEL] = (
                dlogf * (1.0 - lb) * sg * (1.0 - sg) / f - dk * (1.0 - lb) * sgn * (1.0 - sgn)).astype(dph_ref.dtype)
            dph_ref[rows, 2 * D_MODEL:3 * D_MODEL] = dv_s[...].astype(dph_ref.dtype)
            dlb_ref[...] += jnp.sum(dlogf * (1.0 - sg) / f - dk * sgn, axis=0, keepdims=True)

        @pl.when(i == nblk - 1)
        def _():
            dl0 = dlb_ref[...] * lb * (1.0 - lb)
            dl_ref[0:1, :] = dl0
            dl_ref[1:2, :] = -dl0

    wide = pltpu.VMEM((CHUNK, D_MODEL), F32)
    return _call(
        "hgrn_bwd", body, grid=(nblk,),
        in_specs=[col(0), col(1), col(2), col(3), tok,
                  pl.BlockSpec((cpb, D_MODEL, HGRN_K), lambda i: (rev(i), 0, 0)), tok,
                  pl.BlockSpec((2, D_MODEL), lambda i: (0, 0)), pl.BlockSpec((1, D_MODEL), lambda i: (0, 0)), HBM_SPEC],
        out_specs=[pl.BlockSpec((pl.Element(HGRN_TOKENS), pl.Element(4 * D_MODEL)), lambda i: (
                       pl.multiple_of(rev(i) * HGRN_TOKENS, ROW_ALIGN), first)),
                   pl.BlockSpec((1, D_MODEL), lambda i: (0, 0)), pl.BlockSpec((2, D_MODEL), lambda i: (0, 0))],
        out_shape=[jax.ShapeDtypeStruct(d_in.shape, d_in.dtype), jax.ShapeDtypeStruct((1, D_MODEL), F32),
                   jax.ShapeDtypeStruct((2, D_MODEL), F32)],
        scratch=[pltpu.VMEM((HGRN_HEADS, HGRN_K, HGRN_K), F32), pltpu.VMEM((1, D_MODEL), F32),
                 wide, wide, wide, wide, wide, wide, pltpu.VMEM((1, D_MODEL), F32)],
        args=[ph, ph, ph, ph, o_raw, states, dy, lb_logits, norm_g, d_in], semantics=("arbitrary",), comm=comm,
        aliases={9: 0})


def _local_step(x, target, vec, net):
    T, D = x.shape
    norm_mix_g, b_in, sinks, lb_logits = vec["norm_mix_g"], vec["b_in"], vec["attn_sinks"], vec["hgrn_lb_logits"]
    hgrn_norm_g, norm_ffn_g, norm_final_g = vec["hgrn_norm_g"], vec["norm_ffn_g"], vec["norm_final_g"]
    w_in = net.full("w_in")
    o_q, o_kv, o_h, o_g = (sum(IN_SPLITS[:i]) for i in range(4))
    TM = 512

    names = ("w_branch_attn", "w_branch_hgrn", "w_out")
    (u, pq, pkv, ph, pg), got = _in_proj(x, norm_mix_g, w_in, b_in, tm=256, comm=net.gather(names))
    net.gathered(names, got)
    names = ("w_ffn_gate",)
    (y_attn, lse), got = _attn_fwd(pq, pkv, sinks, comm=net.gather(names))
    net.gathered(names, got)
    names = ("w_ffn_up",)
    (y_hgrn, o_raw, states), got = _hgrn_fwd(ph, lb_logits, hgrn_norm_g, comm=net.gather(names))
    net.gathered(names, got)
    w_ba, w_bh, w_out = net.full("w_branch_attn"), net.full("w_branch_hgrn"), net.full("w_out")
    w_gate, w_up = net.full("w_ffn_gate"), net.full("w_ffn_up")
    ya, yb, merged = _merge_fwd(y_attn, y_hgrn, w_ba, w_bh, pg, tm=TM)

    FT = FFN // 2
    names = ("w_ffn_down",)
    (h1, u2, gpre, up, z), got = _ffn_fwd(merged, w_out, x, norm_ffn_g, w_gate, w_up, tm=256,
                                          comm=net.gather(names))
    net.gathered(names, got)
    w_down = net.full("w_ffn_down")

    dh2, dh2b, dgp, dup, dgf_p, loss_p = _ffn_tail(z, w_down, h1, target, norm_final_g, gpre, up, tm=256)
    loss = jnp.sum(loss_p.reshape(-1, 8, D)[:, 0, 0])
    d_norm_final = _colsum_partials(dgf_p)
    d_w_down = _weight_grad("dw_down", z, dh2b, tm=FT, tk=T)

    names = ("w_ffn_down",)
    (dh1, dh1b, dg2_p), got = _ffn_in_bwd(dgp, dup, w_gate, w_up, h1, norm_ffn_g, dh2, tm=256,
        comm=net.exchange(dict(w_ffn_down=[d_w_down])))
    net.received(names, (), got)
    d_norm_ffn = _colsum_partials(dg2_p)
    d_w_gate = _weight_grad("dw_gate", dgp, u2, tm=FT, tk=T)
    d_w_up = _weight_grad("dw_up", dup, u2, tm=FT, tk=T)

    rows_in = sum(IN_SPLITS)
    dya, dyb, dy_attn, dy_hgrn, d_in = _merge_bwd(dh1b, w_out, w_ba, w_bh, ya, yb, pg, tm=TM, d_in_width=rows_in,
                                                  first=o_g)
    d_w_out = _weight_grad("dw_out", merged, dh1b, tm=1024, tk=1024)
    d_w_ba = _weight_grad("dw_branch_a", y_attn, dya, tm=1024, tk=1024)
    d_w_bh = _weight_grad("dw_branch_b", y_hgrn, dyb, tm=1024, tk=1024)

    names, swap = ("w_ffn_gate",), ("w_ffn_down",)
    (d_in, dsink), got = _attn_bwd(pq, pkv, sinks, lse, dy_attn, d_in,
                                   comm=net.exchange(dict(w_ffn_gate=[d_w_gate]), swap))
    net.received(names, swap, got)
    names, swap = ("w_ffn_up", "w_out", "w_branch_attn", "w_branch_hgrn"), ("w_ffn_gate",)
    (d_in, d_hgrn_norm, d_lb_logits), got = _hgrn_bwd(
        ph, o_raw, states, dy_hgrn, lb_logits, hgrn_norm_g, d_in, o_h,
        comm=net.exchange(dict(w_ffn_up=[d_w_up], w_out=[d_w_out], w_branch_attn=[d_w_ba],
                               w_branch_hgrn=[d_w_bh]), swap))
    net.received(names, swap, got)

    main = rows_in // 1024 * 1024
    *d_w_in, db_main = _weight_grad("dw_in", _Cols(d_in, 0, main), u, tm=1024, tk=T, a_colsum=True,
                                    into=_Into(rows_in, 0, None))
    *d_w_in, db_rest = _weight_grad("dw_in_rest", _Cols(d_in, main, rows_in - main), u, tm=rows_in - main, tk=1024,
                                    a_colsum=True, into=_Into(rows_in, main, d_w_in))
    d_w_in = [tuple(d_w_in)]

    first_level = net.presum_begin("w_in", d_w_in)
    halves = net.presum_end("w_in", [] if first_level is None else _copies_alone("presum_swap_w_in", first_level))
    names, swap = ("w_in",), ("w_ffn_up", "w_out", "w_branch_attn", "w_branch_hgrn")
    (dx, dg1_p), got = _in_proj_bwd([(d_in, 0)], w_in, x, norm_mix_g, dh1, tm=256,
                                    comm=_join(halves, net.swap(swap)))
    net.last = (names, swap, got)
    d_norm_mix = _colsum_partials(dg1_p)
    d_b_in = jnp.concatenate([db_main, db_rest], axis=1)
    vecs = dict(norm_mix_g=d_norm_mix, b_in=d_b_in, attn_sinks=jnp.sum(dsink.reshape(Q_HEADS, ATTN_BLOCK), axis=1).reshape(1, Q_HEADS),
                hgrn_lb_logits=d_lb_logits,
                hgrn_norm_g=d_hgrn_norm, norm_ffn_g=d_norm_ffn, norm_final_g=d_norm_final)
    return loss, dx, vecs


def _place():
    return lax.axis_index("x"), lax.axis_index("y"), lax.axis_index("c")


def _other_chips(x, y):
    return [(1 - x, y), (x, 1 - y), (1 - x, 1 - y)]


def _y_first(copies):
    return [copies[3 * (i // 3) + (1, 0, 2)[i % 3]] for i in range(len(copies))]


def _gather_copies(shards):
    n = len(shards)

    def build(ins, outs, send_sems, recv_sems, local_sems):
        x, y, c = _place()
        mine = 2 * x + y
        local = [pltpu.make_async_copy(ins[w], outs[w].at[mine], local_sems.at[w]) for w in range(n)]
        sends, recvs = [], []
        for w in range(n):
            for k, (px, py) in enumerate(_other_chips(x, y)):
                sem = 3 * w + k
                sends.append(pltpu.make_async_remote_copy(
                    src_ref=ins[w], dst_ref=outs[w].at[mine], send_sem=send_sems.at[sem], recv_sem=recv_sems.at[sem],
                    device_id=(px, py, c), device_id_type=MESH_ID))
                recvs.append(pltpu.make_async_remote_copy(
                    src_ref=ins[w], dst_ref=outs[w].at[2 * px + py], send_sem=send_sems.at[sem],
                    recv_sem=recv_sems.at[sem], device_id=(px, py, c), device_id_type=MESH_ID))
        return sends, recvs, local, _y_first(sends)

    return _Carried(shards, [jax.ShapeDtypeStruct((N_CHIPS,) + s.shape, s.dtype) for s in shards], 3 * n, n, build)


def _grad_copies(stacked):
    n = len(stacked)

    def build(ins, outs, send_sems, recv_sems, local_sems):
        x, y, c = _place()
        sends = []
        for w in range(n):
            for k, (px, py) in enumerate(_other_chips(x, y)):
                sem = 3 * w + k
                sends.append(pltpu.make_async_remote_copy(
                    src_ref=ins[w].at[2 * px + py], dst_ref=outs[w].at[k], send_sem=send_sems.at[sem],
                    recv_sem=recv_sems.at[sem], device_id=(px, py, c), device_id_type=MESH_ID))
        return sends, sends, [], _y_first(sends)

    return _Carried(stacked, [jax.ShapeDtypeStruct((3,) + s.shape[1:], s.dtype) for s in stacked], 3 * n, 0, build)


def _small_copies(small):
    def build(ins, outs, send_sems, recv_sems, local_sems):
        small_ref, all_ref = ins[0], outs[0]
        x, y, c = _place()
        me = 4 * x + 2 * y + c
        sends, recvs = [], []
        for r in range(1, 8):
            px = 1 - x if r & 4 else x
            py = 1 - y if r & 2 else y
            pc = 1 - c if r & 1 else c
            sends.append(pltpu.make_async_remote_copy(
                src_ref=small_ref, dst_ref=all_ref.at[me], send_sem=send_sems.at[r - 1], recv_sem=recv_sems.at[r - 1],
                device_id=(px, py, pc), device_id_type=MESH_ID))
            recvs.append(pltpu.make_async_remote_copy(
                src_ref=small_ref, dst_ref=all_ref.at[4 * px + 2 * py + pc], send_sem=send_sems.at[r - 1],
                recv_sem=recv_sems.at[r - 1], device_id=(px, py, pc), device_id_type=MESH_ID))
        return sends, recvs, [pltpu.make_async_copy(small_ref, all_ref.at[me], local_sems.at[0])]

    return _Carried([small], [jax.ShapeDtypeStruct((8,) + small.shape, small.dtype)], 7, 1, build)


def _gather_by_neighbours(name, shard):
    half = shard.shape[0] // 2
    quarter = half // 2

    def body(in_ref, out_ref, send_sems, recv_sems, local_sem):
        for core in (0, 1):
            @pl.when(lax.axis_index("c") == core)
            def _():
                program(core, in_ref, out_ref, send_sems, recv_sems, local_sem)

    def program(c, in_ref, out_ref, send_sems, recv_sems, local_sem):
        x, y, _ = _place()
        chip = lambda px, py: 2 * px + py
        to_x, to_y, sibling = (1 - x, y, c), (x, 1 - y, c), (x, y, 1 - c)
        x_blk, y_blk, d_blk = chip(1 - x, y), chip(x, 1 - y), chip(1 - x, 1 - y)
        mine, theirs = c * half, (1 - c) * half

        def copy(sem, rows, block, to, src=None):
            place = out_ref.at[block, pl.ds(rows[0], rows[1])]
            return pltpu.make_async_remote_copy(
                src_ref=place if src is None else src, dst_ref=place, send_sem=send_sems.at[sem],
                recv_sem=recv_sems.at[sem], device_id=to, device_id_type=MESH_ID)

        own = pltpu.make_async_copy(in_ref, out_ref.at[chip(x, y)], local_sem)
        own.start()
        my_rows = in_ref.at[pl.ds(mine, half)]
        along_x = dict(send=copy(0, (mine, half), chip(x, y), to_x, src=my_rows),
                       landed=copy(0, (mine, half), x_blk, to_x),
                       onward=[copy(3, (mine + quarter, quarter), x_blk, to_y), copy(4, (mine, half), x_blk, sibling)],
                       diagonal=copy(2, (mine, quarter), d_blk, to_x))
        along_y = dict(send=copy(1, (mine, half), chip(x, y), to_y, src=my_rows),
                       landed=copy(1, (mine, half), y_blk, to_y),
                       onward=[copy(2, (mine, quarter), y_blk, to_x), copy(5, (mine, half), y_blk, sibling)],
                       diagonal=copy(3, (mine + quarter, quarter), d_blk, to_y))
        last = copy(6, (mine, half), d_blk, sibling)

        order = (along_x, along_y) if c == 0 else (along_y, along_x)
        for axis in order:
            axis["send"].start()
        for axis in order:
            axis["landed"].wait_recv()
            for cp in axis["onward"]:
                cp.start()
        for axis in order:
            axis["diagonal"].wait_recv()
        last.start()
        for sem, block in ((4, x_blk), (5, y_blk), (6, d_blk)):
            copy(sem, (theirs, half), block, sibling).wait_recv()
        for cp in [along_x["send"], along_y["send"]] + along_x["onward"] + along_y["onward"] + [last]:
            cp.wait_send()
        own.wait()

    return pl.pallas_call(
        body, name=name, in_specs=[HBM_SPEC], out_specs=HBM_SPEC,
        out_shape=jax.ShapeDtypeStruct((N_CHIPS,) + shard.shape, shard.dtype),
        scratch_shapes=[pltpu.SemaphoreType.DMA((7,)), pltpu.SemaphoreType.DMA((7,)), pltpu.SemaphoreType.DMA(())],
    )(shard)


def _copies_alone(name, comm):
    return _call(name, lambda: None, grid=(), in_specs=[], out_specs=[], out_shape=[], args=[], comm=comm)[1]


class _Net:
    def __init__(self, shards):
        self.shards = shards
        self.whole, self.own, self.theirs, self.sums, self.other = {}, {}, {}, {}, {}
        x, y, _ = _place()
        self.chip = 2 * x + y

    def gather(self, names):
        return _gather_copies([self.shards[n] for n in names])

    def gathered(self, names, got):
        for n, g in zip(names, got):
            self.whole[n] = g.reshape(-1, g.shape[-1])

    def full(self, name):
        return self.whole[name]

    def exchange(self, grads, swap=()):
        stacked = []
        for n, pieces in grads.items():
            (keep, send), = pieces
            self.own[n] = keep
            stacked.append(send.reshape(N_CHIPS, keep.shape[0] // N_CHIPS, send.shape[-1]))
        return _join(_grad_copies(stacked), self.swap(swap))

    def swap(self, names):
        return _sibling_copies([self.sums[n] for n in names]) if names else None

    def presum_begin(self, name, pieces):
        keep = jnp.concatenate([p[0] for p in pieces], axis=0) if len(pieces) > 1 else pieces[0][0]
        send = jnp.concatenate([p[1] for p in pieces], axis=0) if len(pieces) > 1 else pieces[0][1]
        rows = keep.shape[0] // N_CHIPS
        self.held = keep.reshape(N_CHIPS, rows, keep.shape[-1])
        return _half_rows_copies(send.reshape(N_CHIPS, rows, send.shape[-1]))

    def presum_end(self, name, got):
        x, y, c = _place()
        to_send, self.own[name] = _pre_sum("presum_" + name, self.held, got[0], jnp.stack([c, self.chip]))
        return _grad_copies([to_send])

    def received(self, names, swap, got, carried=None):
        self.theirs.update(zip(names, got[:len(names)]))
        self.other.update(zip(swap, got[len(names):]))
        for n in names:
            (self.sums[n],), more = _partial_sum("sum_" + n, self.own[n], self.theirs[n], self.chip, comm=carried)
        return more


def _half_rows_copies(stacked):
    n, rows = stacked.shape[0], stacked.shape[1] // 2

    def build(ins, outs, send_sems, recv_sems, local_sems):
        x, y, c = _place()
        copies = [pltpu.make_async_remote_copy(
            src_ref=ins[0].at[s, pl.ds((1 - c) * rows, rows)], dst_ref=outs[0].at[s], send_sem=send_sems.at[s],
            recv_sem=recv_sems.at[s], device_id=(x, y, 1 - c), device_id_type=MESH_ID) for s in range(n)]
        return copies, copies, []

    return _Carried([stacked], [jax.ShapeDtypeStruct((n, rows, stacked.shape[2]), stacked.dtype)], n, 0, build)


def _pre_sum(name, held, theirs, core_and_chip):
    n, R, C = held.shape
    half = R // 2
    tr = _row_tile(half)
    per_half = half // tr

    def body(place_ref, h_ref, t_ref, send_ref, own_ref):
        total = h_ref[0] + t_ref[0].astype(F32)
        send_ref[0] = total.astype(send_ref.dtype)

        @pl.when(pl.program_id(1) == place_ref[1])
        def _():
            own_ref[...] = total

    return pl.pallas_call(
        body, name=name,
        grid_spec=pltpu.PrefetchScalarGridSpec(
            num_scalar_prefetch=1, grid=(per_half, n),
            in_specs=[pl.BlockSpec((1, tr, C), lambda i, s, place: (s, place[0] * per_half + i, 0)),
                      pl.BlockSpec((1, tr, C), lambda i, s, place: (s, i, 0))],
            out_specs=[pl.BlockSpec((1, tr, C), lambda i, s, place: (s, i, 0)),
                       pl.BlockSpec((tr, C), lambda i, s, place: (i, 0))]),
        out_shape=[jax.ShapeDtypeStruct((n, half, C), MXU_DTYPE), jax.ShapeDtypeStruct((half, C), F32)],
        compiler_params=_params(("arbitrary", "arbitrary")),
    )(core_and_chip, held, theirs)


def _sibling_copies(parts):
    n = len(parts)

    def build(ins, outs, send_sems, recv_sems, local_sems):
        x, y, c = _place()
        copies = [pltpu.make_async_remote_copy(
            src_ref=ins[w], dst_ref=outs[w], send_sem=send_sems.at[w], recv_sem=recv_sems.at[w],
            device_id=(x, y, 1 - c), device_id_type=MESH_ID) for w in range(n)]
        return copies, copies, []

    return _Carried(parts, [jax.ShapeDtypeStruct(p.shape, p.dtype) for p in parts], n, 0, build)


def _row_tile(rows, most=512, sublanes=16):
    return max(t for t in range(sublanes, min(most, rows // 2) + 1, sublanes) if rows % t == 0)


def _partial_sum(name, own, recv, chip, comm=None):
    _, R, C = recv.shape
    tr = _row_tile(R)

    def body(o_ref, r_ref, p_ref):
        p_ref[...] = ((o_ref[...] + r_ref[0].astype(F32)) + r_ref[1].astype(F32)) + r_ref[2].astype(F32)

    if own.shape[0] != R:
        assert comm is None and own.shape[0] == N_CHIPS * R
        total = pl.pallas_call(
            lambda chip_ref, *refs: body(*refs), name=name,
            grid_spec=pltpu.PrefetchScalarGridSpec(
                num_scalar_prefetch=1, grid=(R // tr,),
                in_specs=[pl.BlockSpec((tr, C), lambda i, chip_ref: (chip_ref[0] * (R // tr) + i, 0)),
                          pl.BlockSpec((3, tr, C), lambda i, chip_ref: (0, i, 0))],
                out_specs=pl.BlockSpec((tr, C), lambda i, chip_ref: (i, 0))),
            out_shape=jax.ShapeDtypeStruct((R, C), F32), compiler_params=_params(("parallel",)),
        )(chip.reshape(1), own, recv)
        return [total], []
    return _call(name, body, grid=(R // tr,),
                 in_specs=[pl.BlockSpec((tr, C), lambda i: (i, 0)), pl.BlockSpec((3, tr, C), lambda i: (0, i, 0))],
                 out_specs=[pl.BlockSpec((tr, C), lambda i: (i, 0))], out_shape=[jax.ShapeDtypeStruct((R, C), F32)],
                 args=[own, recv], semantics=("parallel",), comm=comm)


def _adam_vals(w, g, m, v):
    m = ADAM_B1 * m + (1.0 - ADAM_B1) * g
    v = ADAM_B2 * v + (1.0 - ADAM_B2) * (g * g)
    m_hat = m / (1.0 - ADAM_B1 ** ADAM_STEP)
    v_hat = v / (1.0 - ADAM_B2 ** ADAM_STEP)
    delta = -ADAM_LR * (m_hat / (jnp.sqrt(v_hat) + ADAM_EPS) + ADAM_WD * w)
    return delta, m, v


def _adamw(name, w, m, v, mine, other, comm=None):
    R, C = w.shape
    tr = _row_tile(R)

    def body(w_ref, m_ref, v_ref, s_ref, n_ref, g_ref, d_ref, nm_ref, nv_ref):
        g = s_ref[...] + n_ref[...]
        d, nm, nv = _adam_vals(w_ref[...], g, m_ref[...], v_ref[...])
        g_ref[...], d_ref[...], nm_ref[...], nv_ref[...] = g, d, nm, nv

    spec = pl.BlockSpec((tr, C), lambda i: (i, 0))
    return _call(name, body, grid=(R // tr,), in_specs=[spec] * 5, out_specs=[spec] * 4,
                 out_shape=[jax.ShapeDtypeStruct((R, C), F32)] * 4, args=[w, m, v, mine, other],
                 semantics=("parallel",), comm=comm)


def _adamw_by_halves(name, w, m, v, mine, other, core):
    R, C = w.shape
    tr = _row_tile(R // 2)
    per_half = R // 2 // tr

    def body(c_ref, w_ref, m_ref, v_ref, s_ref, n_ref, g_ref, d_ref, nm_ref, nv_ref):
        g = jnp.where(pl.program_id(0) // per_half == c_ref[0, 0], s_ref[...], n_ref[...])
        d, nm, nv = _adam_vals(w_ref[...], g, m_ref[...], v_ref[...])
        g_ref[...], d_ref[...], nm_ref[...], nv_ref[...] = g, d, nm, nv

    spec = pl.BlockSpec((tr, C), lambda i: (i, 0))
    part = pl.BlockSpec((tr, C), lambda i: (i % per_half, 0))
    return pl.pallas_call(
        body, name=name, grid=(R // tr,),
        in_specs=[pl.BlockSpec(memory_space=pltpu.SMEM), spec, spec, spec, part, part], out_specs=[spec] * 4,
        out_shape=[jax.ShapeDtypeStruct((R, C), F32)] * 4, compiler_params=_params(("parallel",)),
    )(core, w, m, v, mine, other)


SMALL_LAYOUT = dict(norm_mix_g=(0, 1, 1024), b_in=(1, 8, 7424), hgrn_norm_g=(9, 1, 1024), norm_ffn_g=(10, 1, 1024),
                    norm_final_g=(11, 1, 1024), hgrn_lb_logits=(12, 2, 2048), attn_sinks=(14, 1, 16))
SMALL_LOSS_ROW, SMALL_ROWS = 15, 16


def _pack_small(grads, loss):
    rows = [jnp.pad(grads[name].astype(F32).reshape(-1), (0, nrows * D_MODEL - n))
            for name, (_, nrows, n) in SMALL_LAYOUT.items()]
    rows.append(jnp.pad(loss.astype(F32).reshape(1), (0, D_MODEL - 1)))
    return jnp.concatenate(rows).reshape(SMALL_ROWS, D_MODEL)


def _adamw_small(w, m, v, g_all):
    names = list(SMALL_LAYOUT)
    n = len(names)

    def body(a_ref, *refs):
        ins, outs = refs[:3 * n], refs[3 * n:]
        g_all_rows = a_ref[0]
        for dev in range(1, 8):
            g_all_rows = g_all_rows + a_ref[dev]
        for i, name in enumerate(names):
            first, nrows, count = SMALL_LAYOUT[name]
            w_ref, m_ref, v_ref = ins[3 * i:3 * i + 3]
            if w_ref.shape[0] == nrows:
                g = g_all_rows[first:first + nrows, :w_ref.shape[1]]
            else:
                last = count - (nrows - 1) * D_MODEL
                g = jnp.concatenate([g_all_rows[r:r + 1, :] for r in range(first, first + nrows - 1)]
                                    + [g_all_rows[first + nrows - 1:first + nrows, :last]], axis=1)
            d, nm, nv = _adam_vals(w_ref[...], g, m_ref[...], v_ref[...])
            for o_ref, val in zip(outs[4 * i:4 * i + 4], (g, d, nm, nv)):
                o_ref[...] = val
        outs[4 * n][...] = g_all_rows[SMALL_LOSS_ROW:SMALL_LOSS_ROW + 1, 0:1]

    res = pl.pallas_call(
        body, name="adamw_small",
        out_shape=[jax.ShapeDtypeStruct(w[name].shape, F32) for name in names for _ in range(4)]
        + [jax.ShapeDtypeStruct((1, 1), F32)],
    )(g_all, *[t[name] for name in names for t in (w, m, v)])
    return {name: res[4 * i:4 * i + 4] for i, name in enumerate(names)}, res[4 * n]


MATRICES = ("w_in", "w_branch_attn", "w_branch_hgrn", "w_out", "w_ffn_gate", "w_ffn_up", "w_ffn_down")
COLUMN_SHARDED = ("w_in", "w_ffn_gate", "w_ffn_up")
WEIGHTS = ("norm_mix_g", "w_in", "b_in", "attn_sinks", "hgrn_lb_logits", "hgrn_norm_g", "w_branch_attn",
           "w_branch_hgrn", "w_out", "norm_ffn_g", "w_ffn_gate", "w_ffn_up", "w_ffn_down", "norm_final_g")


def kernel(x, norm_mix_g, w_in, b_in, attn_sinks, hgrn_lb_logits, hgrn_norm_g, w_branch_attn, w_branch_hgrn, w_out, norm_ffn_g, w_ffn_gate, w_ffn_up, w_ffn_down, norm_final_g, loss_target, m_norm_mix_g, m_w_in, m_b_in, m_attn_sinks, m_hgrn_lb_logits, m_hgrn_norm_g, m_w_branch_attn, m_w_branch_hgrn, m_w_out, m_norm_ffn_g, m_w_ffn_gate, m_w_ffn_up, m_w_ffn_down, m_norm_final_g, v_norm_mix_g, v_w_in, v_b_in, v_attn_sinks, v_hgrn_lb_logits, v_hgrn_norm_g, v_w_branch_attn, v_w_branch_hgrn, v_w_out, v_norm_ffn_g, v_w_ffn_gate, v_w_ffn_up, v_w_ffn_down, v_norm_final_g):
    given = dict(locals())
    w = {n: given[n] for n in WEIGHTS}
    m = {n: given["m_" + n] for n in WEIGHTS}
    v = {n: given["v_" + n] for n in WEIGHTS}

    block = lambda a, n: jnp.transpose(a[0]) if n in COLUMN_SHARDED else a[0]
    unblock = lambda a, n: (jnp.transpose(a) if n in COLUMN_SHARDED else a)[None]
    net = _Net({n: block(w[n], n).astype(MXU_DTYPE) for n in MATRICES})
    net.gathered(("w_in",), [_gather_by_neighbours("gather_w_in", net.shards["w_in"])])
    vec = dict(norm_mix_g=norm_mix_g, b_in=b_in, attn_sinks=attn_sinks, hgrn_lb_logits=hgrn_lb_logits,
               hgrn_norm_g=hgrn_norm_g, norm_ffn_g=norm_ffn_g, norm_final_g=norm_final_g.reshape(1, D_MODEL))
    loss_part, dx, d_vecs = _local_step(x[0], loss_target[0], vec, net)

    small_all, = net.received(*net.last, carried=_small_copies(_pack_small(d_vecs, loss_part)))
    grads, deltas, new_m, new_v = {}, {}, {}, {}
    for n in ("w_ffn_down", "w_ffn_gate", "w_ffn_up", "w_out", "w_branch_attn", "w_branch_hgrn"):
        res, got = _adamw("adamw_" + n, block(w[n], n), block(m[n], n), block(v[n], n), net.sums[n], net.other[n],
                          comm=net.swap(("w_in",)) if n == "w_ffn_down" else None)
        if n == "w_ffn_down":
            net.other["w_in"], = got
        grads[n], deltas[n], new_m[n], new_v[n] = (unblock(r, n) for r in res)
    n = "w_in"
    res = _adamw_by_halves("adamw_" + n, block(w[n], n), block(m[n], n), block(v[n], n), net.sums[n], net.other[n],
                           _place()[2].reshape(1, 1))
    grads[n], deltas[n], new_m[n], new_v[n] = (unblock(r, n) for r in res)
    rows = lambda t: {n: t[n].reshape(-1, t[n].shape[-1]) for n in SMALL_LAYOUT}
    res, loss = _adamw_small(rows(w), rows(m), rows(v), small_all)
    for n, four in res.items():
        grads[n], deltas[n], new_m[n], new_v[n] = (r.reshape(w[n].shape) for r in four)
    loss = loss.reshape(())
    return (loss, dx[None], *[grads[n] for n in WEIGHTS], *[deltas[n] for n in WEIGHTS],
            *[new_m[n] for n in WEIGHTS], *[new_v[n] for n in WEIGHTS])
```

```python
import collections
import functools
import math

import jax
import jax.numpy as jnp
from jax import lax
from jax.experimental import pallas as pl
from jax.experimental.pallas import tpu as pltpu

F32 = jnp.float32
BF16 = jnp.bfloat16
MXU_DTYPE = jnp.bfloat16
SAVED_DTYPE = jnp.bfloat16
MESH_ID = pl.DeviceIdType.MESH

D_MODEL = 1024
HEAD_DIM = 64
Q_HEADS = 16
KV_HEADS = 2
GROUP = Q_HEADS // KV_HEADS
KV_WIDTH = KV_HEADS * HEAD_DIM
ATTN_BLOCK = 128
HGRN_HEADS = 8
HGRN_K = 128
CHUNK = 64
HGRN_TOKENS = 256
FFN = 2816
IN_SPLITS = (1024, 256, 4096, 2048)
EPS = 1e-6
NEG_INF = -1e30
ADAM_LR, ADAM_B1, ADAM_B2, ADAM_EPS, ADAM_WD, ADAM_STEP = 0.001, 0.9, 0.999, 1e-08, 0.01, 10
N_CHIPS = 4
VMEM_LIMIT = 60 * 1024 * 1024
ROW_ALIGN = 16


def _params(sem=None):
    return pltpu.CompilerParams(dimension_semantics=sem, vmem_limit_bytes=VMEM_LIMIT)


def _sigmoid(v):
    return 0.5 * jnp.tanh(0.5 * v) + 0.5


def _dot(a, b, dims):
    return lax.dot_general(a.astype(MXU_DTYPE), b.astype(MXU_DTYPE), (dims, ((), ())),
                           preferred_element_type=F32)


def _nn(a, b):
    return _dot(a, b, ((1,), (0,)))


def _nt(a, b):
    return _dot(a, b, ((1,), (1,)))


def _tn(a, b):
    return _dot(a, b, ((0,), (0,)))


HBM_SPEC = pl.BlockSpec(memory_space=pl.ANY)


class _Carried:
    def __init__(self, arrays, out_shapes, n_remote, n_local, build):
        self.parts = [(len(arrays), len(out_shapes), build)]
        self.arrays, self.out_shapes = list(arrays), list(out_shapes)
        self.scratch = [pltpu.SemaphoreType.DMA((n_remote,)), pltpu.SemaphoreType.DMA((n_remote,)),
                        pltpu.SemaphoreType.DMA((max(n_local, 1),))]

    def __add__(self, other):
        both = _Carried([], [], 1, 0, None)
        both.parts = self.parts + other.parts
        both.arrays, both.out_shapes = self.arrays + other.arrays, self.out_shapes + other.out_shapes
        both.scratch = self.scratch + other.scratch
        return both

    def _built(self, ins, outs, sems):
        for p, (ni, no, build) in enumerate(self.parts):
            yield build(ins[:ni], outs[:no], *sems[3 * p:3 * p + 3])
            ins, outs = ins[ni:], outs[no:]

    def start(self, ins, outs, sems):
        core = lax.axis_index("c")
        for sends, _, local, *other_order in self._built(ins, outs, sems):
            for cp in local:
                cp.start()
            if not other_order:
                for cp in sends:
                    cp.start()
                continue

            @pl.when(core == 0)
            def _():
                for cp in sends:
                    cp.start()

            @pl.when(core == 1)
            def _():
                for cp in other_order[0]:
                    cp.start()

    def wait(self, ins, outs, sems):
        for sends, recvs, local, *_ in self._built(ins, outs, sems):
            for cp in recvs:
                cp.wait_recv()
            for cp in sends:
                cp.wait_send()
            for cp in local:
                cp.wait()


def _join(*comms):
    comms = [c for c in comms if c is not None]
    return functools.reduce(lambda a, b: a + b, comms) if comms else None


def _call(name, body, *, grid, in_specs, out_specs, out_shape, args, scratch=(), semantics=None, comm=None,
          aliases=None):
    n_in, n_out, n_scr = len(in_specs), len(out_specs), len(scratch)
    aliases = aliases or {}
    if comm is None:
        res = pl.pallas_call(body, name=name, grid=grid, in_specs=in_specs, out_specs=out_specs, out_shape=out_shape,
                             scratch_shapes=list(scratch), input_output_aliases=aliases,
                             compiler_params=_params(semantics))(*args)
        return list(res), []
    ci, co = len(comm.arrays), len(comm.out_shapes)

    def carrying(*refs):
        ins, refs = refs[:n_in], refs[n_in:]
        c_ins, refs = refs[:ci], refs[ci:]
        outs, refs = refs[:n_out], refs[n_out:]
        c_outs, refs = refs[:co], refs[co:]
        scr, sems = refs[:n_scr], refs[n_scr:]
        if not grid:
            comm.start(c_ins, c_outs, sems)
            body(*ins, *outs, *scr)
            comm.wait(c_ins, c_outs, sems)
            return
        first = functools.reduce(jnp.logical_and, [pl.program_id(a) == 0 for a in range(len(grid))])
        last = functools.reduce(jnp.logical_and, [pl.program_id(a) == g - 1 for a, g in enumerate(grid)])

        @pl.when(first)
        def _():
            comm.start(c_ins, c_outs, sems)

        body(*ins, *outs, *scr)

        @pl.when(last)
        def _():
            comm.wait(c_ins, c_outs, sems)

    res = pl.pallas_call(
        carrying, name=name, grid=grid, in_specs=list(in_specs) + [HBM_SPEC] * ci,
        out_specs=list(out_specs) + [HBM_SPEC] * co, out_shape=list(out_shape) + comm.out_shapes,
        scratch_shapes=list(scratch) + comm.scratch, input_output_aliases=aliases,
        compiler_params=_params(("arbitrary",) * len(grid) if grid else None),
    )(*args, *comm.arrays)
    return list(res[:n_out]), list(res[n_out:])


_Cols = collections.namedtuple("_Cols", "array first cols")
_Into = collections.namedtuple("_Into", "rows first held")


def _weight_grad(name, a, b, *, tm, tk, a_colsum=False, into=None):
    cols = a if isinstance(a, _Cols) else _Cols(a, 0, a.shape[1])
    a = cols.array
    (T, N), M = b.shape, cols.cols
    tk = min(tk, T)
    assert cols.first % tm == 0 and M % tm == 0 and T % tk == 0, (name, cols.first, M, tm, T, tk)
    ni, nk, tile0 = M // tm, T // tk, cols.first // tm
    held = list(into.held) if into is not None and into.held is not None else []

    def body(a_ref, b_ref, *rest):
        keep_ref, send_ref = rest[len(held):len(held) + 2]
        sums_ref = rest[len(held) + 2] if a_colsum else None
        if nk == 1:
            acc = _tn(a_ref[...], b_ref[...])
            keep_ref[...], send_ref[...] = acc, acc.astype(send_ref.dtype)
            if a_colsum:
                sums_ref[...] = jnp.sum(a_ref[...].astype(F32), axis=0, keepdims=True)
            return
        acc_ref = rest[-1]
        k = pl.program_id(1)

        @pl.when(k == 0)
        def _():
            acc_ref[...] = jnp.zeros_like(acc_ref)
            if a_colsum:
                sums_ref[...] = jnp.zeros((1, tm), F32)

        if a_colsum:
            sums_ref[...] += jnp.sum(a_ref[...].astype(F32), axis=0, keepdims=True)
        acc_ref[...] += _tn(a_ref[...], b_ref[...])

        @pl.when(k == nk - 1)
        def _():
            keep_ref[...], send_ref[...] = acc_ref[...], acc_ref[...].astype(send_ref.dtype)

    if into is None:
        rows, out_spec = M, pl.BlockSpec((tm, N), lambda i, k: (i, 0))
    else:
        rows = into.rows
        out_spec = pl.BlockSpec((pl.Element(tm), pl.Element(N)),
                                lambda i, k: (pl.multiple_of(into.first + i * tm, ROW_ALIGN), 0))
    out_shape = [jax.ShapeDtypeStruct((rows, N), F32), jax.ShapeDtypeStruct((rows, N), MXU_DTYPE)]
    out_specs = [out_spec, out_spec]
    if a_colsum:
        out_shape.append(jax.ShapeDtypeStruct((1, M), F32))
        out_specs.append(pl.BlockSpec((1, tm), lambda i, k: (0, i)))
    return pl.pallas_call(
        body, name=name, grid=(ni, nk),
        in_specs=[pl.BlockSpec((tk, tm), lambda i, k: (k, tile0 + i)), pl.BlockSpec((tk, N), lambda i, k: (k, 0))]
        + [HBM_SPEC] * len(held),
        out_specs=out_specs, out_shape=out_shape, scratch_shapes=[pltpu.VMEM((tm, N), F32)] if nk > 1 else [],
        input_output_aliases={2 + p: p for p in range(len(held))},
        compiler_params=_params(("parallel", "arbitrary")),
    )(a, b, *held)


def _ffn_fwd(merged, w_out, x, gain, w_gate_t, w_up_t, *, tm, comm=None):
    (T, D), F = x.shape, w_gate_t.shape[0]

    def body(m_ref, wo_ref, x_ref, g_ref, wg_ref, wu_ref, h_ref, u_ref, gate_ref, up_ref, z_ref):
        h = x_ref[...] + _nn(m_ref[...], wo_ref[...])
        h_ref[...] = h
        u = (h * lax.rsqrt(jnp.mean(h * h, axis=-1, keepdims=True) + EPS) * g_ref[...]).astype(u_ref.dtype)
        u_ref[...] = u
        gate, up = _nt(u, wg_ref[...]), _nt(u, wu_ref[...])
        gate_ref[...], up_ref[...] = gate.astype(gate_ref.dtype), up.astype(up_ref.dtype)
        z_ref[...] = (gate * _sigmoid(gate) * up).astype(z_ref.dtype)

    rows = lambda n: pl.BlockSpec((tm, n), lambda i: (i, 0))
    fixed = _fixed_spec
    return _call("ffn_hidden", body, grid=(T // tm,),
                 in_specs=[rows(D), fixed(w_out), rows(D), fixed(gain), fixed(w_gate_t), fixed(w_up_t)],
                 out_specs=[rows(D), rows(D), rows(F), rows(F), rows(F)],
                 out_shape=[jax.ShapeDtypeStruct((T, D), F32), jax.ShapeDtypeStruct((T, D), MXU_DTYPE)]
                 + [jax.ShapeDtypeStruct((T, F), SAVED_DTYPE)] * 2 + [jax.ShapeDtypeStruct((T, F), MXU_DTYPE)],
                 args=[merged, w_out, x, gain, w_gate_t, w_up_t], semantics=("parallel",), comm=comm)


def _in_proj(x, gain, w_in_t, b_in, *, tm, comm=None):
    T, D = x.shape
    bounds = [sum(IN_SPLITS[:i]) for i in range(len(IN_SPLITS) + 1)]

    def body(x_ref, g_ref, w_ref, b_ref, u_ref, *piece_refs):
        xv = x_ref[...]
        r = lax.rsqrt(jnp.mean(xv * xv, axis=-1, keepdims=True) + EPS)
        u = (xv * r * g_ref[...]).astype(u_ref.dtype)
        u_ref[...] = u
        for o_ref, lo, hi in zip(piece_refs, bounds[:-1], bounds[1:]):
            o_ref[...] = (_nt(u, w_ref[lo:hi, :]) + b_ref[:, lo:hi]).astype(o_ref.dtype)

    rows = lambda n: pl.BlockSpec((tm, n), lambda i: (i, 0))
    fixed = _fixed_spec
    dtypes = (MXU_DTYPE, MXU_DTYPE, F32, F32)
    return _call("in_proj", body, grid=(T // tm,),
                 in_specs=[rows(D), fixed(gain), fixed(w_in_t), fixed(b_in)],
                 out_specs=[rows(D)] + [rows(n) for n in IN_SPLITS],
                 out_shape=[jax.ShapeDtypeStruct((T, D), MXU_DTYPE)]
                 + [jax.ShapeDtypeStruct((T, n), dt) for n, dt in zip(IN_SPLITS, dtypes)],
                 args=[x, gain, w_in_t, b_in], semantics=("parallel",), comm=comm)


def _row_spec(tm, n):
    return pl.BlockSpec((tm, n), lambda i: (i, 0))


def _fixed_spec(a):
    return pl.BlockSpec(a.shape, lambda i: (0,) * a.ndim, pipeline_mode=pl.Buffered(1))


def _partials_spec(n):
    return pl.BlockSpec((8, n), lambda i: (i, 0))


def _ffn_tail(z, w_down, h1, target, gain, gate, up, *, tm):
    (T, F), D = z.shape, h1.shape[1]

    def body(z_ref, w_ref, h_ref, t_ref, g_ref, gate_ref, up_ref, dh_ref, dhb_ref, dgate_ref, dup_ref, dg_ref, l_ref):
        h2 = h_ref[...] + _nn(z_ref[...], w_ref[...])
        r = lax.rsqrt(jnp.mean(h2 * h2, axis=-1, keepdims=True) + EPS)
        xhat = h2 * r
        err = xhat * g_ref[...] - t_ref[...]
        part = 0.5 * jnp.sum(jnp.sum(err * err, axis=-1, keepdims=True), axis=0, keepdims=True) / D
        dy = err / D
        dxh = dy * g_ref[...]
        dh2 = r * (dxh - xhat * jnp.mean(dxh * xhat, axis=-1, keepdims=True))
        dh_ref[...] = dh2
        dhb = dh2.astype(dhb_ref.dtype)
        dhb_ref[...] = dhb
        dg_ref[...] = jnp.broadcast_to(jnp.sum(dy * xhat, axis=0, keepdims=True), dg_ref.shape)
        l_ref[...] = jnp.broadcast_to(part, l_ref.shape)
        dz = _nt(dhb, w_ref[...])
        gv, upv = gate_ref[...].astype(F32), up_ref[...].astype(F32)
        s = _sigmoid(gv)
        dgate_ref[...] = (dz * upv * (s * (1.0 + gv * (1.0 - s)))).astype(dgate_ref.dtype)
        dup_ref[...] = (dz * (gv * s)).astype(dup_ref.dtype)

    low = lambda n: jax.ShapeDtypeStruct((T, n), MXU_DTYPE)
    part = jax.ShapeDtypeStruct((8 * (T // tm), D), F32)
    return pl.pallas_call(
        body, name="ffn_tail", grid=(T // tm,),
        in_specs=[_row_spec(tm, F), _fixed_spec(w_down), _row_spec(tm, D), _row_spec(tm, D), _fixed_spec(gain),
                  _row_spec(tm, F), _row_spec(tm, F)],
        out_specs=[_row_spec(tm, D), _row_spec(tm, D), _row_spec(tm, F), _row_spec(tm, F), _partials_spec(D),
                   _partials_spec(D)],
        out_shape=[jax.ShapeDtypeStruct((T, D), F32), low(D), low(F), low(F), part, part],
        compiler_params=_params(("parallel",)),
    )(z, w_down, h1, target, gain, gate, up)


def _ffn_in_bwd(dgate, dup, w_gate_t, w_up_t, h1, gain, dres, *, tm, comm=None):
    (T, F), D = dgate.shape, h1.shape[1]

    def body(dg_ref, du_ref, wg_ref, wu_ref, h_ref, g_ref, r_ref, dh_ref, dhb_ref, dgain_ref):
        d_u2 = _nn(dg_ref[...], wg_ref[...]) + _nn(du_ref[...], wu_ref[...])
        dx, dgain = _rmsnorm_bwd_vals(d_u2, h_ref[...], g_ref[...])
        dh = r_ref[...] + dx
        dh_ref[...] = dh
        dhb_ref[...] = dh.astype(dhb_ref.dtype)
        dgain_ref[...] = jnp.broadcast_to(dgain, dgain_ref.shape)

    return _call("d_ffn_in", body, grid=(T // tm,),
                 in_specs=[_row_spec(tm, F), _row_spec(tm, F), _fixed_spec(w_gate_t), _fixed_spec(w_up_t),
                           _row_spec(tm, D), _fixed_spec(gain), _row_spec(tm, D)],
                 out_specs=[_row_spec(tm, D), _row_spec(tm, D), _partials_spec(D)],
                 out_shape=[jax.ShapeDtypeStruct((T, D), F32), jax.ShapeDtypeStruct((T, D), MXU_DTYPE),
                            jax.ShapeDtypeStruct((8 * (T // tm), D), F32)],
                 args=[dgate, dup, w_gate_t, w_up_t, h1, gain, dres], semantics=("parallel",), comm=comm)


def _in_proj_bwd(pieces, w_in_t, x, gain, dres, *, tm, comm=None):
    T, D = x.shape
    n = len(pieces)

    def body(*refs):
        dps, (w_ref, x_ref, g_ref, r_ref, dx_ref, dgain_ref) = refs[:n], refs[n:]
        d_u = None
        for dp_ref, (dp, first) in zip(dps, pieces):
            term = _nn(dp_ref[...], w_ref[first:first + dp.shape[1], :])
            d_u = term if d_u is None else d_u + term
        dx, dgain = _rmsnorm_bwd_vals(d_u, x_ref[...], g_ref[...])
        dx_ref[...] = r_ref[...] + dx
        dgain_ref[...] = jnp.broadcast_to(dgain, dgain_ref.shape)

    return _call("d_u", body, grid=(T // tm,),
                 in_specs=[_row_spec(tm, dp.shape[1]) for dp, _ in pieces]
                 + [_fixed_spec(w_in_t), _row_spec(tm, D), _fixed_spec(gain), _row_spec(tm, D)],
                 out_specs=[_row_spec(tm, D), _partials_spec(D)],
                 out_shape=[jax.ShapeDtypeStruct((T, D), F32), jax.ShapeDtypeStruct((8 * (T // tm), D), F32)],
                 args=[dp for dp, _ in pieces] + [w_in_t, x, gain, dres], semantics=("parallel",), comm=comm)


def _merge_fwd(y_a, y_b, w_a, w_b, gates, *, tm, comm=None):
    T, D = y_a.shape

    def body(ya_ref, yb_ref, wa_ref, wb_ref, ga_ref, gb_ref, pa_ref, pb_ref, m_ref):
        pa, pb = _nn(ya_ref[...], wa_ref[...]), _nn(yb_ref[...], wb_ref[...])
        pa_ref[...], pb_ref[...] = pa.astype(pa_ref.dtype), pb.astype(pb_ref.dtype)
        m_ref[...] = (_sigmoid(ga_ref[...]) * pa + _sigmoid(gb_ref[...]) * pb).astype(m_ref.dtype)

    rows = pl.BlockSpec((tm, D), lambda i: (i, 0))
    whole = pl.BlockSpec((D, D), lambda i: (0, 0), pipeline_mode=pl.Buffered(1))
    return _call("branch_merge", body, grid=(T // tm,),
                 in_specs=[rows, rows, whole, whole, rows, pl.BlockSpec((tm, D), lambda i: (i, 1))],
                 out_specs=[rows] * 3,
                 out_shape=[jax.ShapeDtypeStruct((T, D), SAVED_DTYPE)] * 2 + [jax.ShapeDtypeStruct((T, D), MXU_DTYPE)],
                 args=[y_a, y_b, w_a, w_b, gates, gates], semantics=("parallel",), comm=comm)


def _merge_bwd(dh, w_out, w_a, w_b, p_a, p_b, gates, *, tm, d_in_width, first):
    T, D = dh.shape

    def body(dh_ref, wo_ref, wa_ref, wb_ref, pa_ref, pb_ref, ga_ref, gb_ref, dpa_ref, dpb_ref, dya_ref, dyb_ref,
             din_ref):
        dm = _nt(dh_ref[...], wo_ref[...])
        sa, sb = _sigmoid(ga_ref[...]), _sigmoid(gb_ref[...])
        dpa, dpb = (dm * sa).astype(dpa_ref.dtype), (dm * sb).astype(dpb_ref.dtype)
        dpa_ref[...], dpb_ref[...] = dpa, dpb
        din_ref[:, :D] = (dm * pa_ref[...].astype(F32) * sa * (1.0 - sa)).astype(din_ref.dtype)
        din_ref[:, D:] = (dm * pb_ref[...].astype(F32) * sb * (1.0 - sb)).astype(din_ref.dtype)
        dya_ref[...] = _nt(dpa, wa_ref[...]).astype(dya_ref.dtype)
        dyb_ref[...] = _nt(dpb, wb_ref[...]).astype(dyb_ref.dtype)

    rows = pl.BlockSpec((tm, D), lambda i: (i, 0))
    whole = pl.BlockSpec((D, D), lambda i: (0, 0), pipeline_mode=pl.Buffered(1))
    low = jax.ShapeDtypeStruct((T, D), MXU_DTYPE)
    return pl.pallas_call(
        body, name="d_branch_merge", grid=(T // tm,),
        in_specs=[rows, whole, whole, whole, rows, rows, rows, pl.BlockSpec((tm, D), lambda i: (i, 1))],
        out_specs=[rows] * 4 + [pl.BlockSpec((pl.Element(tm), pl.Element(2 * D)), lambda i: (
            pl.multiple_of(i * tm, ROW_ALIGN), first))],
        out_shape=[low] * 3 + [jax.ShapeDtypeStruct((T, D), F32), jax.ShapeDtypeStruct((T, d_in_width), MXU_DTYPE)],
        compiler_params=_params(("parallel",)),
    )(dh, w_out, w_a, w_b, p_a, p_b, gates, gates)


def _colsum_partials(p):
    return jnp.sum(p.reshape(-1, 8, p.shape[-1])[:, 0, :], axis=0, keepdims=True)


def _rmsnorm_bwd_vals(dy, xin, g):
    rstd = lax.rsqrt(jnp.mean(xin * xin, axis=-1, keepdims=True) + EPS)
    xhat = xin * rstd
    dg = jnp.sum(dy * xhat, axis=0, keepdims=True)
    dxh = dy * g
    dx = rstd * (dxh - xhat * jnp.mean(dxh * xhat, axis=-1, keepdims=True))
    return dx, dg


ATTN_SCALE = 1.0 / math.sqrt(HEAD_DIM)
GROUP_LANES = GROUP * ATTN_BLOCK
PAIR = 2 * HEAD_DIM


def _attn_mask():
    kj = lax.broadcasted_iota(jnp.int32, (ATTN_BLOCK, GROUP_LANES), 0)
    qi = lax.broadcasted_iota(jnp.int32, (ATTN_BLOCK, GROUP_LANES), 1) & (ATTN_BLOCK - 1)
    return kj <= qi


def _heads_transposed(ref, g, scale=None):
    parts = []
    for a in range(GROUP // 2):
        lo = (g * GROUP // 2 + a) * PAIR
        pair = ref[:, lo:lo + PAIR].astype(F32)
        pair = (pair if scale is None else pair * scale).T
        parts += [pair[:HEAD_DIM], pair[HEAD_DIM:]]
    return jnp.concatenate(parts, axis=1).astype(MXU_DTYPE)


def _heads_back(ref, g, vt):
    for a in range(GROUP // 2):
        lo = (g * GROUP // 2 + a) * PAIR
        pair = jnp.concatenate([vt[:, (2 * a) * ATTN_BLOCK:(2 * a + 1) * ATTN_BLOCK],
                                vt[:, (2 * a + 1) * ATTN_BLOCK:(2 * a + 2) * ATTN_BLOCK]], axis=0)
        ref[:, lo:lo + PAIR] = pair.T.astype(ref.dtype)


def _kv_parts(kv_ref, g):
    ks = slice(g * HEAD_DIM, (g + 1) * HEAD_DIM)
    vs = slice(KV_WIDTH + g * HEAD_DIM, KV_WIDTH + (g + 1) * HEAD_DIM)
    return kv_ref[:, ks].astype(MXU_DTYPE), kv_ref[:, vs].astype(MXU_DTYPE)


def _sink_rows(sinks):
    return jnp.repeat(sinks.reshape(KV_HEADS, GROUP), ATTN_BLOCK, axis=1)


def _attn_fwd(pq, pkv, sinks, comm=None):
    T = pq.shape[0]
    nb = T // ATTN_BLOCK

    def body(q_ref, kvc_ref, kvp_ref, s_ref, y_ref, lse_ref):
        mask_c = _attn_mask()
        has_prev = pl.program_id(0) > 0
        for g in range(KV_HEADS):
            (kc, vc), (kp, vp) = _kv_parts(kvc_ref, g), _kv_parts(kvp_ref, g)
            qt = _heads_transposed(q_ref, g, ATTN_SCALE)
            s = jnp.where(mask_c, _nn(kc, qt), jnp.where(has_prev, _nn(kp, qt), NEG_INF))
            sink = s_ref[g:g + 1, :]
            m = jnp.maximum(jnp.max(s, axis=0, keepdims=True), sink)
            p = jnp.exp(s - m)
            den = jnp.sum(p, axis=0, keepdims=True) + jnp.exp(sink - m)
            pc = jnp.where(mask_c, p, 0.0)
            _heads_back(y_ref, g, (_tn(vc, pc) + _tn(vp, p - pc)) / den)
            lse = m + jnp.log(den)
            for i in range(GROUP):
                lse_ref[g * GROUP + i:g * GROUP + i + 1, :] = lse[:, i * ATTN_BLOCK:(i + 1) * ATTN_BLOCK]

    return _call(
        "attn_fwd", body, grid=(nb,),
        in_specs=[pl.BlockSpec((ATTN_BLOCK, D_MODEL), lambda n: (n, 0)),
                  pl.BlockSpec((ATTN_BLOCK, 2 * KV_WIDTH), lambda n: (n, 0)),
                  pl.BlockSpec((ATTN_BLOCK, 2 * KV_WIDTH), lambda n: (jnp.maximum(n - 1, 0), 0)),
                  pl.BlockSpec((KV_HEADS, GROUP_LANES), lambda n: (0, 0))],
        out_specs=[pl.BlockSpec((ATTN_BLOCK, D_MODEL), lambda n: (n, 0)),
                   pl.BlockSpec((Q_HEADS, ATTN_BLOCK), lambda n: (0, n))],
        out_shape=[jax.ShapeDtypeStruct((T, D_MODEL), MXU_DTYPE), jax.ShapeDtypeStruct((Q_HEADS, T), F32)],
        args=[pq, pkv, pkv, _sink_rows(sinks)], semantics=("parallel",), comm=comm)


def _attn_bwd(pq, pkv, sinks, lse, dy, d_in, comm=None):
    T = pq.shape[0]
    nb = T // ATTN_BLOCK
    cur = lambda n: (jnp.minimum(n, nb - 1), 0)
    done = D_MODEL + 2 * KV_WIDTH

    def body(q_ref, kvc_ref, kvp_ref, s_ref, lse_ref, dy_ref, _, out_ref, ds_ref, carry, top, bot, dq_ref):
        n = pl.program_id(0)

        @pl.when(n == 0)
        def _():
            carry[...] = jnp.zeros_like(carry)
            dq_ref[...] = jnp.zeros_like(dq_ref)
            ds_ref[...] = jnp.zeros_like(ds_ref)

        out_ref[:, :D_MODEL] = dq_ref[...]

        @pl.when(n < nb)
        def _():
            mask_c = _attn_mask()
            valid = jnp.logical_or(mask_c, n > 0)
            for g in range(KV_HEADS):
                ks = slice(g * HEAD_DIM, (g + 1) * HEAD_DIM)
                vs = slice(KV_WIDTH + g * HEAD_DIM, KV_WIDTH + (g + 1) * HEAD_DIM)
                (kc, vc), (kp, vp) = _kv_parts(kvc_ref, g), _kv_parts(kvp_ref, g)
                qt = _heads_transposed(q_ref, g, ATTN_SCALE)
                dot = _heads_transposed(dy_ref, g)
                lse = jnp.concatenate([lse_ref[g * GROUP + i:g * GROUP + i + 1, :] for i in range(GROUP)], axis=1)
                p = jnp.where(valid, jnp.exp(jnp.where(mask_c, _nn(kc, qt), _nn(kp, qt)) - lse), 0.0)
                dp = jnp.where(mask_c, _nn(vc, dot), _nn(vp, dot))
                delta = jnp.sum(p * dp, axis=0, keepdims=True)
                ds = p * (dp - delta)
                ds_c, p_c = jnp.where(mask_c, ds, 0.0), jnp.where(mask_c, p, 0.0)
                ds_p, p_p = ds - ds_c, p - p_c
                _heads_back(dq_ref, g, (_tn(kc, ds_c) + _tn(kp, ds_p)) * ATTN_SCALE)
                bot[:, ks], bot[:, vs] = _nt(ds_c, qt), _nt(p_c, dot)
                top[:, ks], top[:, vs] = _nt(ds_p, qt), _nt(p_p, dot)
                ds_ref[g:g + 1, :] -= jnp.exp(s_ref[g:g + 1, :] - lse) * delta
            out_ref[:, D_MODEL:] = (carry[...] + top[...]).astype(out_ref.dtype)
            carry[...] = bot[...]

        @pl.when(n == nb)
        def _():
            out_ref[:, D_MODEL:] = carry[...].astype(out_ref.dtype)

    return _call(
        "attn_bwd", body, grid=(nb + 1,),
        in_specs=[pl.BlockSpec((ATTN_BLOCK, D_MODEL), cur),
                  pl.BlockSpec((ATTN_BLOCK, 2 * KV_WIDTH), cur),
                  pl.BlockSpec((ATTN_BLOCK, 2 * KV_WIDTH), lambda n: (jnp.maximum(jnp.minimum(n, nb - 1) - 1, 0), 0)),
                  pl.BlockSpec((KV_HEADS, GROUP_LANES), lambda n: (0, 0)),
                  pl.BlockSpec((Q_HEADS, ATTN_BLOCK), lambda n: (0, jnp.minimum(n, nb - 1))),
                  pl.BlockSpec((ATTN_BLOCK, D_MODEL), cur), HBM_SPEC],
        out_specs=[pl.BlockSpec((ATTN_BLOCK, done), lambda n: (jnp.maximum(n - 1, 0), 0)),
                   pl.BlockSpec((KV_HEADS, GROUP_LANES), lambda n: (0, 0))],
        out_shape=[jax.ShapeDtypeStruct(d_in.shape, d_in.dtype), jax.ShapeDtypeStruct((KV_HEADS, GROUP_LANES), F32)],
        scratch=[pltpu.VMEM((ATTN_BLOCK, 2 * KV_WIDTH), F32)] * 3 + [pltpu.VMEM((ATTN_BLOCK, D_MODEL), MXU_DTYPE)],
        args=[pq, pkv, pkv, _sink_rows(sinks), lse, dy, d_in], semantics=("arbitrary",), comm=comm, aliases={6: 0})


def _lower_bound(l):
    m = jnp.maximum(l[0:1], l[1:2])
    e0, e1 = jnp.exp(l[0:1] - m), jnp.exp(l[1:2] - m)
    return e0 / (e0 + e1)


def _tri(lower):
    r = lax.broadcasted_iota(jnp.int32, (CHUNK, CHUNK), 0)
    c = lax.broadcasted_iota(jnp.int32, (CHUNK, CHUNK), 1)
    return (r >= c) if lower else (c >= r)


def _chunk_sum(mask, v):
    ones = mask.astype(BF16)
    hi = v.astype(BF16)
    rest = v - hi.astype(F32)
    mid = rest.astype(BF16)
    lo = (rest - mid.astype(F32)).astype(BF16)
    part = lambda t: lax.dot_general(ones, t, (((1,), (0,)), ((), ())), preferred_element_type=F32)
    return part(hi) + part(mid) + part(lo)


def _hgrn_chunk_inputs(hq, hf, lb, causal):
    half_t = 0.5 * jnp.tanh(0.5 * hf)
    sg, sgn = 0.5 + half_t, 0.5 - half_t
    f = lb + (1.0 - lb) * sg
    kk = (1.0 - lb) * sgn
    sq = _sigmoid(hq)
    q = hq * sq
    b = _chunk_sum(causal, jnp.log(f))
    bm, bl = b[CHUNK // 2 - 1:CHUNK // 2, :], b[CHUNK - 1:CHUNK, :]
    e_qm, e_km = jnp.exp(b - bm), jnp.exp(bm - b)
    e_qs, e_kl = e_qm * jnp.exp(bm), e_km * jnp.exp(bl - bm)
    return dict(sg=sg, sgn=sgn, f=f, kk=kk, sq=sq, q=q, e_qm=e_qm, e_km=e_km, e_qs=e_qs, e_kl=e_kl,
                qm=q * e_qm, km=kk * e_km, qs=q * e_qs, kl=kk * e_kl, el=jnp.exp(bl))


def _hgrn_fwd(ph, lb_logits, norm_g, comm=None):
    T = ph.shape[0]
    nblk, cpb = T // HGRN_TOKENS, HGRN_TOKENS // CHUNK
    col = lambda c: pl.BlockSpec((HGRN_TOKENS, D_MODEL), functools.partial(lambda i, c: (i, c), c=c))

    def body(hq_ref, hf_ref, hi_ref, hg_ref, l_ref, ng_ref, y_ref, o_ref, st_ref, s_ref):
        @pl.when(pl.program_id(0) == 0)
        def _():
            s_ref[...] = jnp.zeros_like(s_ref)

        lb = _lower_bound(l_ref[...])
        causal = _tri(True)
        for c in range(cpb):
            rows = slice(c * CHUNK, (c + 1) * CHUNK)
            t = _hgrn_chunk_inputs(hq_ref[rows, :], hf_ref[rows, :], lb, causal)
            qm, km, qs, kl = (t[n].astype(MXU_DTYPE) for n in ("qm", "km", "qs", "kl"))
            v = hi_ref[rows, :].astype(MXU_DTYPE)
            for h in range(HGRN_HEADS):
                ls = slice(h * HGRN_K, (h + 1) * HGRN_K)
                st = s_ref[h]
                st_ref[c, ls, :] = st
                a = jnp.where(causal, _nt(qm[:, ls], km[:, ls]), 0.0)
                o_ref[rows, ls] = _nn(a, v[:, ls]) + _nt(qs[:, ls], st)
                s_ref[h] = t["el"][:, ls] * st + _tn(v[:, ls], kl[:, ls])
        for h in range(HGRN_HEADS):
            ls = slice(h * HGRN_K, (h + 1) * HGRN_K)
            o = o_ref[:, ls]
            r = lax.rsqrt(jnp.mean(o * o, axis=-1, keepdims=True) + EPS)
            y_ref[:, ls] = (o * r * ng_ref[:, ls] * _sigmoid(hg_ref[:, ls])).astype(y_ref.dtype)

    return _call(
        "hgrn_fwd", body, grid=(nblk,),
        in_specs=[col(0), col(1), col(2), col(3),
                  pl.BlockSpec((2, D_MODEL), lambda i: (0, 0)), pl.BlockSpec((1, D_MODEL), lambda i: (0, 0))],
        out_specs=[pl.BlockSpec((HGRN_TOKENS, D_MODEL), lambda i: (i, 0)),
                   pl.BlockSpec((HGRN_TOKENS, D_MODEL), lambda i: (i, 0)),
                   pl.BlockSpec((cpb, D_MODEL, HGRN_K), lambda i: (i, 0, 0))],
        out_shape=[jax.ShapeDtypeStruct((T, D_MODEL), MXU_DTYPE), jax.ShapeDtypeStruct((T, D_MODEL), F32),
                   jax.ShapeDtypeStruct((T // CHUNK, D_MODEL, HGRN_K), F32)],
        scratch=[pltpu.VMEM((HGRN_HEADS, HGRN_K, HGRN_K), F32)],
        args=[ph, ph, ph, ph, lb_logits, norm_g], semantics=("arbitrary",), comm=comm)


def _hgrn_bwd(ph, o_raw, states, dy, lb_logits, norm_g, d_in, first, comm=None):
    T = ph.shape[0]
    nblk, cpb = T // HGRN_TOKENS, HGRN_TOKENS // CHUNK
    rev = lambda i: nblk - 1 - i
    col = lambda c: pl.BlockSpec((HGRN_TOKENS, D_MODEL), functools.partial(lambda i, c: (rev(i), c), c=c))
    tok = pl.BlockSpec((HGRN_TOKENS, D_MODEL), lambda i: (rev(i), 0))

    def body(hq_ref, hf_ref, hi_ref, hg_ref, o_ref, st_ref, dy_ref, l_ref, ng_ref, _,
             dph_ref, dng_ref, dl_ref, dst_ref, dlb_ref, do_s, dqm_s, dkm_s, dqs_s, dkl_s, dv_s, del_s):
        i = pl.program_id(0)

        @pl.when(i == 0)
        def _():
            dst_ref[...] = jnp.zeros_like(dst_ref)
            dlb_ref[...] = jnp.zeros_like(dlb_ref)
            dng_ref[...] = jnp.zeros_like(dng_ref)

        lb = _lower_bound(l_ref[...])
        causal, anti = _tri(True), _tri(False)
        row = lax.broadcasted_iota(jnp.int32, (CHUNK, D_MODEL), 0)
        for c in reversed(range(cpb)):
            rows = slice(c * CHUNK, (c + 1) * CHUNK)
            hq = hq_ref[rows, :]
            t = _hgrn_chunk_inputs(hq, hf_ref[rows, :], lb, causal)
            sgg = _sigmoid(hg_ref[rows, :])
            dyv = dy_ref[rows, :]
            for h in range(HGRN_HEADS):
                ls = slice(h * HGRN_K, (h + 1) * HGRN_K)
                o = o_ref[rows, ls]
                r = lax.rsqrt(jnp.mean(o * o, axis=-1, keepdims=True) + EPS)
                nrm = o * r
                g_h = sgg[:, ls]
                dph_ref[rows, 3 * D_MODEL + h * HGRN_K:3 * D_MODEL + (h + 1) * HGRN_K] = (
                    dyv[:, ls] * nrm * ng_ref[:, ls] * g_h * (1.0 - g_h)).astype(dph_ref.dtype)
                dyg = dyv[:, ls] * g_h
                dng_ref[:, ls] += jnp.sum(dyg * nrm, axis=0, keepdims=True)
                dn = dyg * ng_ref[:, ls]
                do_s[:, ls] = r * (dn - nrm * jnp.mean(dn * nrm, axis=-1, keepdims=True))
            qm, km, qs, kl = (t[n].astype(MXU_DTYPE) for n in ("qm", "km", "qs", "kl"))
            v = hi_ref[rows, :].astype(MXU_DTYPE)
            do = do_s[...].astype(MXU_DTYPE)
            for h in range(HGRN_HEADS):
                ls = slice(h * HGRN_K, (h + 1) * HGRN_K)
                st = st_ref[c, ls, :]
                dst = dst_ref[h]
                a = jnp.where(causal, _nt(qm[:, ls], km[:, ls]), 0.0)
                da = jnp.where(causal, _nt(do[:, ls], v[:, ls]), 0.0)
                dv_s[:, ls] = _tn(a, do[:, ls]) + _nt(kl[:, ls], dst)
                dkl_s[:, ls] = _nn(v[:, ls], dst)
                dqs_s[:, ls] = _nn(do[:, ls], st)
                del_s[:, ls] = jnp.sum(dst * st, axis=0, keepdims=True)
                dst_ref[h] = _tn(do[:, ls], qs[:, ls]) + t["el"][:, ls] * dst
                dqm_s[:, ls] = _nn(da, km[:, ls])
                dkm_s[:, ls] = _tn(da, qm[:, ls])
            dqm, dkm, dqs, dkl = dqm_s[...], dkm_s[...], dqs_s[...], dkl_s[...]
            dq = dqm * t["e_qm"] + dqs * t["e_qs"]
            dk = dkm * t["e_km"] + dkl * t["e_kl"]
            t_qm, t_km, t_kl = dqm * t["qm"], dkm * t["km"], dkl * t["kl"]
            db = t_qm - t_km + dqs * t["qs"] - t_kl
            db_mid = jnp.sum(t_km - t_qm, axis=0, keepdims=True)
            db_last = jnp.sum(t_kl, axis=0, keepdims=True) + del_s[...] * t["el"]
            db = db + jnp.where(row == CHUNK // 2 - 1, db_mid, 0.0) + jnp.where(row == CHUNK - 1, db_last, 0.0)
            dlogf = _chunk_sum(anti, db)
            sq, sg, sgn, f = t["sq"], t["sg"], t["sgn"], t["f"]
            dph_ref[rows, 0:D_MODEL] = (dq * (sq * (1.0 + hq * (1.0 - sq)))).astype(dph_ref.dtype)
            dph_ref[rows, D_MODEL:2 * D_MODEL] = (
                dlogf * (1.0 - lb) * sg * (1.0 - sg) / f - dk * (1.0 - lb) * sgn * (1.0 - sgn)).astype(dph_ref.dtype)
            dph_ref[rows, 2 * D_MODEL:3 * D_MODEL] = dv_s[...].astype(dph_ref.dtype)
            dlb_ref[...] += jnp.sum(dlogf * (1.0 - sg) / f - dk * sgn, axis=0, keepdims=True)

        @pl.when(i == nblk - 1)
        def _():
            dl0 = dlb_ref[...] * lb * (1.0 - lb)
            dl_ref[0:1, :] = dl0
            dl_ref[1:2, :] = -dl0

    wide = pltpu.VMEM((CHUNK, D_MODEL), F32)
    return _call(
        "hgrn_bwd", body, grid=(nblk,),
        in_specs=[col(0), col(1), col(2), col(3), tok,
                  pl.BlockSpec((cpb, D_MODEL, HGRN_K), lambda i: (rev(i), 0, 0)), tok,
                  pl.BlockSpec((2, D_MODEL), lambda i: (0, 0)), pl.BlockSpec((1, D_MODEL), lambda i: (0, 0)), HBM_SPEC],
        out_specs=[pl.BlockSpec((pl.Element(HGRN_TOKENS), pl.Element(4 * D_MODEL)), lambda i: (
                       pl.multiple_of(rev(i) * HGRN_TOKENS, ROW_ALIGN), first)),
                   pl.BlockSpec((1, D_MODEL), lambda i: (0, 0)), pl.BlockSpec((2, D_MODEL), lambda i: (0, 0))],
        out_shape=[jax.ShapeDtypeStruct(d_in.shape, d_in.dtype), jax.ShapeDtypeStruct((1, D_MODEL), F32),
                   jax.ShapeDtypeStruct((2, D_MODEL), F32)],
        scratch=[pltpu.VMEM((HGRN_HEADS, HGRN_K, HGRN_K), F32), pltpu.VMEM((1, D_MODEL), F32),
                 wide, wide, wide, wide, wide, wide, pltpu.VMEM((1, D_MODEL), F32)],
        args=[ph, ph, ph, ph, o_raw, states, dy, lb_logits, norm_g, d_in], semantics=("arbitrary",), comm=comm,
        aliases={9: 0})


def _local_step(x, target, vec, net):
    T, D = x.shape
    norm_mix_g, b_in, sinks, lb_logits = vec["norm_mix_g"], vec["b_in"], vec["attn_sinks"], vec["hgrn_lb_logits"]
    hgrn_norm_g, norm_ffn_g, norm_final_g = vec["hgrn_norm_g"], vec["norm_ffn_g"], vec["norm_final_g"]
    w_in = net.full("w_in")
    o_q, o_kv, o_h, o_g = (sum(IN_SPLITS[:i]) for i in range(4))
    TM = 512

    names = ("w_ffn_gate",)
    (u, pq, pkv, ph, pg), got = _in_proj(x, norm_mix_g, w_in, b_in, tm=256, comm=net.gather(names))
    net.gathered(names, got)
    names = ("w_branch_attn", "w_branch_hgrn")
    (y_attn, lse), got = _attn_fwd(pq, pkv, sinks, comm=net.gather(names))
    net.gathered(names, got)
    names = ("w_ffn_up",)
    (y_hgrn, o_raw, states), got = _hgrn_fwd(ph, lb_logits, hgrn_norm_g, comm=net.gather(names))
    net.gathered(names, got)
    w_ba, w_bh = net.full("w_branch_attn"), net.full("w_branch_hgrn")
    w_gate, w_up = net.full("w_ffn_gate"), net.full("w_ffn_up")
    names = ("w_out",)
    (ya, yb, merged), got = _merge_fwd(y_attn, y_hgrn, w_ba, w_bh, pg, tm=TM, comm=net.gather(names))
    net.gathered(names, got)
    w_out = net.full("w_out")

    FT = FFN // 2
    names = ("w_ffn_down",)
    (h1, u2, gpre, up, z), got = _ffn_fwd(merged, w_out, x, norm_ffn_g, w_gate, w_up, tm=256,
                                          comm=net.gather(names))
    net.gathered(names, got)
    w_down = net.full("w_ffn_down")

    dh2, dh2b, dgp, dup, dgf_p, loss_p = _ffn_tail(z, w_down, h1, target, norm_final_g, gpre, up, tm=256)
    loss = jnp.sum(loss_p.reshape(-1, 8, D)[:, 0, 0])
    d_norm_final = _colsum_partials(dgf_p)
    d_w_down = _weight_grad("dw_down", z, dh2b, tm=FT, tk=T)

    names = ("w_ffn_down",)
    (dh1, dh1b, dg2_p), got = _ffn_in_bwd(dgp, dup, w_gate, w_up, h1, norm_ffn_g, dh2, tm=256,
        comm=net.exchange(dict(w_ffn_down=[d_w_down])))
    net.received(names, (), got)
    d_norm_ffn = _colsum_partials(dg2_p)
    d_w_gate = _weight_grad("dw_gate", dgp, u2, tm=FT, tk=T)
    d_w_up = _weight_grad("dw_up", dup, u2, tm=FT, tk=T)

    rows_in = sum(IN_SPLITS)
    dya, dyb, dy_attn, dy_hgrn, d_in = _merge_bwd(dh1b, w_out, w_ba, w_bh, ya, yb, pg, tm=TM, d_in_width=rows_in,
                                                  first=o_g)
    d_w_out = _weight_grad("dw_out", merged, dh1b, tm=1024, tk=1024)
    d_w_ba = _weight_grad("dw_branch_a", y_attn, dya, tm=1024, tk=1024)
    d_w_bh = _weight_grad("dw_branch_b", y_hgrn, dyb, tm=1024, tk=1024)

    names, swap = ("w_ffn_gate",), ("w_ffn_down",)
    (d_in, dsink), got = _attn_bwd(pq, pkv, sinks, lse, dy_attn, d_in,
                                   comm=net.exchange(dict(w_ffn_gate=[d_w_gate]), swap))
    net.received(names, swap, got)
    names, swap = ("w_ffn_up", "w_out", "w_branch_attn", "w_branch_hgrn"), ("w_ffn_gate",)
    (d_in, d_hgrn_norm, d_lb_logits), got = _hgrn_bwd(
        ph, o_raw, states, dy_hgrn, lb_logits, hgrn_norm_g, d_in, o_h,
        comm=net.exchange(dict(w_ffn_up=[d_w_up], w_out=[d_w_out], w_branch_attn=[d_w_ba],
                               w_branch_hgrn=[d_w_bh]), swap))
    net.received(names, swap, got)

    main = rows_in // 1024 * 1024
    *d_w_in, db_main = _weight_grad("dw_in", _Cols(d_in, 0, main), u, tm=1024, tk=T, a_colsum=True,
                                    into=_Into(rows_in, 0, None))
    *d_w_in, db_rest = _weight_grad("dw_in_rest", _Cols(d_in, main, rows_in - main), u, tm=rows_in - main, tk=1024,
                                    a_colsum=True, into=_Into(rows_in, main, d_w_in))
    d_w_in = [tuple(d_w_in)]

    first_level = net.presum_begin("w_in", d_w_in)
    halves = net.presum_end("w_in", [] if first_level is None else _copies_alone("presum_swap_w_in", first_level))
    names, swap = ("w_in",), ("w_ffn_up", "w_out", "w_branch_attn", "w_branch_hgrn")
    (dx, dg1_p), got = _in_proj_bwd([(d_in, 0)], w_in, x, norm_mix_g, dh1, tm=256,
                                    comm=_join(halves, net.swap(swap)))
    net.last = (names, swap, got)
    d_norm_mix = _colsum_partials(dg1_p)
    d_b_in = jnp.concatenate([db_main, db_rest], axis=1)
    vecs = dict(norm_mix_g=d_norm_mix, b_in=d_b_in, attn_sinks=jnp.sum(dsink.reshape(Q_HEADS, ATTN_BLOCK), axis=1).reshape(1, Q_HEADS),
                hgrn_lb_logits=d_lb_logits,
                hgrn_norm_g=d_hgrn_norm, norm_ffn_g=d_norm_ffn, norm_final_g=d_norm_final)
    return loss, dx, vecs


def _place():
    return lax.axis_index("x"), lax.axis_index("y"), lax.axis_index("c")


def _other_chips(x, y):
    return [(1 - x, y), (x, 1 - y), (1 - x, 1 - y)]


def _y_first(copies):
    return [copies[3 * (i // 3) + (1, 0, 2)[i % 3]] for i in range(len(copies))]


def _gather_copies(shards):
    n = len(shards)

    def build(ins, outs, send_sems, recv_sems, local_sems):
        x, y, c = _place()
        mine = 2 * x + y
        local = [pltpu.make_async_copy(ins[w], outs[w].at[mine], local_sems.at[w]) for w in range(n)]
        sends, recvs = [], []
        for w in range(n):
            for k, (px, py) in enumerate(_other_chips(x, y)):
                sem = 3 * w + k
                sends.append(pltpu.make_async_remote_copy(
                    src_ref=ins[w], dst_ref=outs[w].at[mine], send_sem=send_sems.at[sem], recv_sem=recv_sems.at[sem],
                    device_id=(px, py, c), device_id_type=MESH_ID))
                recvs.append(pltpu.make_async_remote_copy(
                    src_ref=ins[w], dst_ref=outs[w].at[2 * px + py], send_sem=send_sems.at[sem],
                    recv_sem=recv_sems.at[sem], device_id=(px, py, c), device_id_type=MESH_ID))
        return sends, recvs, local, _y_first(sends)

    return _Carried(shards, [jax.ShapeDtypeStruct((N_CHIPS,) + s.shape, s.dtype) for s in shards], 3 * n, n, build)


def _grad_copies(stacked):
    n = len(stacked)

    def build(ins, outs, send_sems, recv_sems, local_sems):
        x, y, c = _place()
        sends = []
        for w in range(n):
            for k, (px, py) in enumerate(_other_chips(x, y)):
                sem = 3 * w + k
                sends.append(pltpu.make_async_remote_copy(
                    src_ref=ins[w].at[2 * px + py], dst_ref=outs[w].at[k], send_sem=send_sems.at[sem],
                    recv_sem=recv_sems.at[sem], device_id=(px, py, c), device_id_type=MESH_ID))
        return sends, sends, [], _y_first(sends)

    return _Carried(stacked, [jax.ShapeDtypeStruct((3,) + s.shape[1:], s.dtype) for s in stacked], 3 * n, 0, build)


def _small_copies(small):
    def build(ins, outs, send_sems, recv_sems, local_sems):
        small_ref, all_ref = ins[0], outs[0]
        x, y, c = _place()
        me = 4 * x + 2 * y + c
        sends, recvs = [], []
        for r in range(1, 8):
            px = 1 - x if r & 4 else x
            py = 1 - y if r & 2 else y
            pc = 1 - c if r & 1 else c
            sends.append(pltpu.make_async_remote_copy(
                src_ref=small_ref, dst_ref=all_ref.at[me], send_sem=send_sems.at[r - 1], recv_sem=recv_sems.at[r - 1],
                device_id=(px, py, pc), device_id_type=MESH_ID))
            recvs.append(pltpu.make_async_remote_copy(
                src_ref=small_ref, dst_ref=all_ref.at[4 * px + 2 * py + pc], send_sem=send_sems.at[r - 1],
                recv_sem=recv_sems.at[r - 1], device_id=(px, py, pc), device_id_type=MESH_ID))
        return sends, recvs, [pltpu.make_async_copy(small_ref, all_ref.at[me], local_sems.at[0])]

    return _Carried([small], [jax.ShapeDtypeStruct((8,) + small.shape, small.dtype)], 7, 1, build)


def _gather_by_neighbours(name, shard):
    half = shard.shape[0] // 2
    quarter = half // 2

    def body(in_ref, out_ref, send_sems, recv_sems, local_sem):
        for core in (0, 1):
            @pl.when(lax.axis_index("c") == core)
            def _():
                program(core, in_ref, out_ref, send_sems, recv_sems, local_sem)

    def program(c, in_ref, out_ref, send_sems, recv_sems, local_sem):
        x, y, _ = _place()
        chip = lambda px, py: 2 * px + py
        to_x, to_y, sibling = (1 - x, y, c), (x, 1 - y, c), (x, y, 1 - c)
        x_blk, y_blk, d_blk = chip(1 - x, y), chip(x, 1 - y), chip(1 - x, 1 - y)
        mine, theirs = c * half, (1 - c) * half

        def copy(sem, rows, block, to, src=None):
            place = out_ref.at[block, pl.ds(rows[0], rows[1])]
            return pltpu.make_async_remote_copy(
                src_ref=place if src is None else src, dst_ref=place, send_sem=send_sems.at[sem],
                recv_sem=recv_sems.at[sem], device_id=to, device_id_type=MESH_ID)

        own = pltpu.make_async_copy(in_ref, out_ref.at[chip(x, y)], local_sem)
        own.start()
        my_rows = in_ref.at[pl.ds(mine, half)]
        along_x = dict(send=copy(0, (mine, half), chip(x, y), to_x, src=my_rows),
                       landed=copy(0, (mine, half), x_blk, to_x),
                       onward=[copy(3, (mine + quarter, quarter), x_blk, to_y), copy(4, (mine, half), x_blk, sibling)],
                       diagonal=copy(2, (mine, quarter), d_blk, to_x))
        along_y = dict(send=copy(1, (mine, half), chip(x, y), to_y, src=my_rows),
                       landed=copy(1, (mine, half), y_blk, to_y),
                       onward=[copy(2, (mine, quarter), y_blk, to_x), copy(5, (mine, half), y_blk, sibling)],
                       diagonal=copy(3, (mine + quarter, quarter), d_blk, to_y))
        last = copy(6, (mine, half), d_blk, sibling)

        order = (along_x, along_y) if c == 0 else (along_y, along_x)
        for axis in order:
            axis["send"].start()
        for axis in order:
            axis["landed"].wait_recv()
            for cp in axis["onward"]:
                cp.start()
        for axis in order:
            axis["diagonal"].wait_recv()
        last.start()
        for sem, block in ((4, x_blk), (5, y_blk), (6, d_blk)):
            copy(sem, (theirs, half), block, sibling).wait_recv()
        for cp in [along_x["send"], along_y["send"]] + along_x["onward"] + along_y["onward"] + [last]:
            cp.wait_send()
        own.wait()

    return pl.pallas_call(
        body, name=name, in_specs=[HBM_SPEC], out_specs=HBM_SPEC,
        out_shape=jax.ShapeDtypeStruct((N_CHIPS,) + shard.shape, shard.dtype),
        scratch_shapes=[pltpu.SemaphoreType.DMA((7,)), pltpu.SemaphoreType.DMA((7,)), pltpu.SemaphoreType.DMA(())],
    )(shard)


def _copies_alone(name, comm):
    return _call(name, lambda: None, grid=(), in_specs=[], out_specs=[], out_shape=[], args=[], comm=comm)[1]


class _Net:
    def __init__(self, shards):
        self.shards = shards
        self.whole, self.own, self.theirs, self.sums, self.other = {}, {}, {}, {}, {}
        x, y, _ = _place()
        self.chip = 2 * x + y

    def gather(self, names):
        return _gather_copies([self.shards[n] for n in names])

    def gathered(self, names, got):
        for n, g in zip(names, got):
            self.whole[n] = g.reshape(-1, g.shape[-1])

    def full(self, name):
        return self.whole[name]

    def exchange(self, grads, swap=()):
        stacked = []
        for n, pieces in grads.items():
            (keep, send), = pieces
            self.own[n] = keep
            stacked.append(send.reshape(N_CHIPS, keep.shape[0] // N_CHIPS, send.shape[-1]))
        return _join(_grad_copies(stacked), self.swap(swap))

    def swap(self, names):
        return _sibling_copies([self.sums[n] for n in names]) if names else None

    def presum_begin(self, name, pieces):
        keep = jnp.concatenate([p[0] for p in pieces], axis=0) if len(pieces) > 1 else pieces[0][0]
        send = jnp.concatenate([p[1] for p in pieces], axis=0) if len(pieces) > 1 else pieces[0][1]
        rows = keep.shape[0] // N_CHIPS
        self.held = keep.reshape(N_CHIPS, rows, keep.shape[-1])
        return _half_rows_copies(send.reshape(N_CHIPS, rows, send.shape[-1]))

    def presum_end(self, name, got):
        x, y, c = _place()
        to_send, self.own[name] = _pre_sum("presum_" + name, self.held, got[0], jnp.stack([c, self.chip]))
        return _grad_copies([to_send])

    def received(self, names, swap, got, carried=None):
        self.theirs.update(zip(names, got[:len(names)]))
        self.other.update(zip(swap, got[len(names):]))
        for n in names:
            (self.sums[n],), more = _partial_sum("sum_" + n, self.own[n], self.theirs[n], self.chip, comm=carried)
        return more


def _half_rows_copies(stacked):
    n, rows = stacked.shape[0], stacked.shape[1] // 2

    def build(ins, outs, send_sems, recv_sems, local_sems):
        x, y, c = _place()
        copies = [pltpu.make_async_remote_copy(
            src_ref=ins[0].at[s, pl.ds((1 - c) * rows, rows)], dst_ref=outs[0].at[s], send_sem=send_sems.at[s],
            recv_sem=recv_sems.at[s], device_id=(x, y, 1 - c), device_id_type=MESH_ID) for s in range(n)]
        return copies, copies, []

    return _Carried([stacked], [jax.ShapeDtypeStruct((n, rows, stacked.shape[2]), stacked.dtype)], n, 0, build)


def _pre_sum(name, held, theirs, core_and_chip):
    n, R, C = held.shape
    half = R // 2
    tr = _row_tile(half)
    per_half = half // tr

    def body(place_ref, h_ref, t_ref, send_ref, own_ref):
        total = h_ref[0] + t_ref[0].astype(F32)
        send_ref[0] = total.astype(send_ref.dtype)

        @pl.when(pl.program_id(1) == place_ref[1])
        def _():
            own_ref[...] = total

    return pl.pallas_call(
        body, name=name,
        grid_spec=pltpu.PrefetchScalarGridSpec(
            num_scalar_prefetch=1, grid=(per_half, n),
            in_specs=[pl.BlockSpec((1, tr, C), lambda i, s, place: (s, place[0] * per_half + i, 0)),
                      pl.BlockSpec((1, tr, C), lambda i, s, place: (s, i, 0))],
            out_specs=[pl.BlockSpec((1, tr, C), lambda i, s, place: (s, i, 0)),
                       pl.BlockSpec((tr, C), lambda i, s, place: (i, 0))]),
        out_shape=[jax.ShapeDtypeStruct((n, half, C), MXU_DTYPE), jax.ShapeDtypeStruct((half, C), F32)],
        compiler_params=_params(("arbitrary", "arbitrary")),
    )(core_and_chip, held, theirs)


def _sibling_copies(parts):
    n = len(parts)

    def build(ins, outs, send_sems, recv_sems, local_sems):
        x, y, c = _place()
        copies = [pltpu.make_async_remote_copy(
            src_ref=ins[w], dst_ref=outs[w], send_sem=send_sems.at[w], recv_sem=recv_sems.at[w],
            device_id=(x, y, 1 - c), device_id_type=MESH_ID) for w in range(n)]
        return copies, copies, []

    return _Carried(parts, [jax.ShapeDtypeStruct(p.shape, p.dtype) for p in parts], n, 0, build)


def _row_tile(rows, most=512, sublanes=16):
    return max(t for t in range(sublanes, min(most, rows // 2) + 1, sublanes) if rows % t == 0)


def _partial_sum(name, own, recv, chip, comm=None):
    _, R, C = recv.shape
    tr = _row_tile(R)

    def body(o_ref, r_ref, p_ref):
        p_ref[...] = ((o_ref[...] + r_ref[0].astype(F32)) + r_ref[1].astype(F32)) + r_ref[2].astype(F32)

    if own.shape[0] != R:
        assert comm is None and own.shape[0] == N_CHIPS * R
        total = pl.pallas_call(
            lambda chip_ref, *refs: body(*refs), name=name,
            grid_spec=pltpu.PrefetchScalarGridSpec(
                num_scalar_prefetch=1, grid=(R // tr,),
                in_specs=[pl.BlockSpec((tr, C), lambda i, chip_ref: (chip_ref[0] * (R // tr) + i, 0)),
                          pl.BlockSpec((3, tr, C), lambda i, chip_ref: (0, i, 0))],
                out_specs=pl.BlockSpec((tr, C), lambda i, chip_ref: (i, 0))),
            out_shape=jax.ShapeDtypeStruct((R, C), F32), compiler_params=_params(("parallel",)),
        )(chip.reshape(1), own, recv)
        return [total], []
    return _call(name, body, grid=(R // tr,),
                 in_specs=[pl.BlockSpec((tr, C), lambda i: (i, 0)), pl.BlockSpec((3, tr, C), lambda i: (0, i, 0))],
                 out_specs=[pl.BlockSpec((tr, C), lambda i: (i, 0))], out_shape=[jax.ShapeDtypeStruct((R, C), F32)],
                 args=[own, recv], semantics=("parallel",), comm=comm)


def _adam_vals(w, g, m, v):
    m = ADAM_B1 * m + (1.0 - ADAM_B1) * g
    v = ADAM_B2 * v + (1.0 - ADAM_B2) * (g * g)
    m_hat = m / (1.0 - ADAM_B1 ** ADAM_STEP)
    v_hat = v / (1.0 - ADAM_B2 ** ADAM_STEP)
    delta = -ADAM_LR * (m_hat / (jnp.sqrt(v_hat) + ADAM_EPS) + ADAM_WD * w)
    return delta, m, v


def _adamw(name, w, m, v, mine, other, comm=None):
    R, C = w.shape
    tr = _row_tile(R)

    def body(w_ref, m_ref, v_ref, s_ref, n_ref, g_ref, d_ref, nm_ref, nv_ref):
        g = s_ref[...] + n_ref[...]
        d, nm, nv = _adam_vals(w_ref[...], g, m_ref[...], v_ref[...])
        g_ref[...], d_ref[...], nm_ref[...], nv_ref[...] = g, d, nm, nv

    spec = pl.BlockSpec((tr, C), lambda i: (i, 0))
    return _call(name, body, grid=(R // tr,), in_specs=[spec] * 5, out_specs=[spec] * 4,
                 out_shape=[jax.ShapeDtypeStruct((R, C), F32)] * 4, args=[w, m, v, mine, other],
                 semantics=("parallel",), comm=comm)


def _adamw_by_halves(name, w, m, v, mine, other, core):
    R, C = w.shape
    tr = _row_tile(R // 2)
    per_half = R // 2 // tr

    def body(c_ref, w_ref, m_ref, v_ref, s_ref, n_ref, g_ref, d_ref, nm_ref, nv_ref):
        g = jnp.where(pl.program_id(0) // per_half == c_ref[0, 0], s_ref[...], n_ref[...])
        d, nm, nv = _adam_vals(w_ref[...], g, m_ref[...], v_ref[...])
        g_ref[...], d_ref[...], nm_ref[...], nv_ref[...] = g, d, nm, nv

    spec = pl.BlockSpec((tr, C), lambda i: (i, 0))
    part = pl.BlockSpec((tr, C), lambda i: (i % per_half, 0))
    return pl.pallas_call(
        body, name=name, grid=(R // tr,),
        in_specs=[pl.BlockSpec(memory_space=pltpu.SMEM), spec, spec, spec, part, part], out_specs=[spec] * 4,
        out_shape=[jax.ShapeDtypeStruct((R, C), F32)] * 4, compiler_params=_params(("parallel",)),
    )(core, w, m, v, mine, other)


SMALL_LAYOUT = dict(norm_mix_g=(0, 1, 1024), b_in=(1, 8, 7424), hgrn_norm_g=(9, 1, 1024), norm_ffn_g=(10, 1, 1024),
                    norm_final_g=(11, 1, 1024), hgrn_lb_logits=(12, 2, 2048), attn_sinks=(14, 1, 16))
SMALL_LOSS_ROW, SMALL_ROWS = 15, 16


def _pack_small(grads, loss):
    rows = [jnp.pad(grads[name].astype(F32).reshape(-1), (0, nrows * D_MODEL - n))
            for name, (_, nrows, n) in SMALL_LAYOUT.items()]
    rows.append(jnp.pad(loss.astype(F32).reshape(1), (0, D_MODEL - 1)))
    return jnp.concatenate(rows).reshape(SMALL_ROWS, D_MODEL)


def _adamw_small(w, m, v, g_all):
    names = list(SMALL_LAYOUT)
    n = len(names)

    def body(a_ref, *refs):
        ins, outs = refs[:3 * n], refs[3 * n:]
        g_all_rows = a_ref[0]
        for dev in range(1, 8):
            g_all_rows = g_all_rows + a_ref[dev]
        for i, name in enumerate(names):
            first, nrows, count = SMALL_LAYOUT[name]
            w_ref, m_ref, v_ref = ins[3 * i:3 * i + 3]
            if w_ref.shape[0] == nrows:
                g = g_all_rows[first:first + nrows, :w_ref.shape[1]]
            else:
                last = count - (nrows - 1) * D_MODEL
                g = jnp.concatenate([g_all_rows[r:r + 1, :] for r in range(first, first + nrows - 1)]
                                    + [g_all_rows[first + nrows - 1:first + nrows, :last]], axis=1)
            d, nm, nv = _adam_vals(w_ref[...], g, m_ref[...], v_ref[...])
            for o_ref, val in zip(outs[4 * i:4 * i + 4], (g, d, nm, nv)):
                o_ref[...] = val
        outs[4 * n][...] = g_all_rows[SMALL_LOSS_ROW:SMALL_LOSS_ROW + 1, 0:1]

    res = pl.pallas_call(
        body, name="adamw_small",
        out_shape=[jax.ShapeDtypeStruct(w[name].shape, F32) for name in names for _ in range(4)]
        + [jax.ShapeDtypeStruct((1, 1), F32)],
    )(g_all, *[t[name] for name in names for t in (w, m, v)])
    return {name: res[4 * i:4 * i + 4] for i, name in enumerate(names)}, res[4 * n]


MATRICES = ("w_in", "w_branch_attn", "w_branch_hgrn", "w_out", "w_ffn_gate", "w_ffn_up", "w_ffn_down")
COLUMN_SHARDED = ("w_in", "w_ffn_gate", "w_ffn_up")
WEIGHTS = ("norm_mix_g", "w_in", "b_in", "attn_sinks", "hgrn_lb_logits", "hgrn_norm_g", "w_branch_attn",
           "w_branch_hgrn", "w_out", "norm_ffn_g", "w_ffn_gate", "w_ffn_up", "w_ffn_down", "norm_final_g")


def kernel(x, norm_mix_g, w_in, b_in, attn_sinks, hgrn_lb_logits, hgrn_norm_g, w_branch_attn, w_branch_hgrn, w_out, norm_ffn_g, w_ffn_gate, w_ffn_up, w_ffn_down, norm_final_g, loss_target, m_norm_mix_g, m_w_in, m_b_in, m_attn_sinks, m_hgrn_lb_logits, m_hgrn_norm_g, m_w_branch_attn, m_w_branch_hgrn, m_w_out, m_norm_ffn_g, m_w_ffn_gate, m_w_ffn_up, m_w_ffn_down, m_norm_final_g, v_norm_mix_g, v_w_in, v_b_in, v_attn_sinks, v_hgrn_lb_logits, v_hgrn_norm_g, v_w_branch_attn, v_w_branch_hgrn, v_w_out, v_norm_ffn_g, v_w_ffn_gate, v_w_ffn_up, v_w_ffn_down, v_norm_final_g):
    given = dict(locals())
    w = {n: given[n] for n in WEIGHTS}
    m = {n: given["m_" + n] for n in WEIGHTS}
    v = {n: given["v_" + n] for n in WEIGHTS}

    block = lambda a, n: jnp.transpose(a[0]) if n in COLUMN_SHARDED else a[0]
    unblock = lambda a, n: (jnp.transpose(a) if n in COLUMN_SHARDED else a)[None]
    net = _Net({n: block(w[n], n).astype(MXU_DTYPE) for n in MATRICES})
    net.gathered(("w_in",), [_gather_by_neighbours("gather_w_in", net.shards["w_in"])])
    vec = dict(norm_mix_g=norm_mix_g, b_in=b_in, attn_sinks=attn_sinks, hgrn_lb_logits=hgrn_lb_logits,
               hgrn_norm_g=hgrn_norm_g, norm_ffn_g=norm_ffn_g, norm_final_g=norm_final_g.reshape(1, D_MODEL))
    loss_part, dx, d_vecs = _local_step(x[0], loss_target[0], vec, net)

    small_all, = net.received(*net.last, carried=_small_copies(_pack_small(d_vecs, loss_part)))
    grads, deltas, new_m, new_v = {}, {}, {}, {}
    for n in ("w_ffn_down", "w_ffn_gate", "w_ffn_up", "w_out", "w_branch_attn", "w_branch_hgrn"):
        res, got = _adamw("adamw_" + n, block(w[n], n), block(m[n], n), block(v[n], n), net.sums[n], net.other[n],
                          comm=net.swap(("w_in",)) if n == "w_ffn_down" else None)
        if n == "w_ffn_down":
            net.other["w_in"], = got
        grads[n], deltas[n], new_m[n], new_v[n] = (unblock(r, n) for r in res)
    n = "w_in"
    res = _adamw_by_halves("adamw_" + n, block(w[n], n), block(m[n], n), block(v[n], n), net.sums[n], net.other[n],
                           _place()[2].reshape(1, 1))
    grads[n], deltas[n], new_m[n], new_v[n] = (unblock(r, n) for r in res)
    rows = lambda t: {n: t[n].reshape(-1, t[n].shape[-1]) for n in SMALL_LAYOUT}
    res, loss = _adamw_small(rows(w), rows(m), rows(v), small_all)
    for n, four in res.items():
        grads[n], deltas[n], new_m[n], new_v[n] = (r.reshape(w[n].shape) for r in four)
    loss = loss.reshape(())
    return (loss, dx[None], *[grads[n] for n in WEIGHTS], *[deltas[n] for n in WEIGHTS],
            *[new_m[n] for n in WEIGHTS], *[new_v[n] for n in WEIGHTS])
```

```python
import collections
import functools
import math

import jax
import jax.numpy as jnp
from jax import lax
from jax.experimental import pallas as pl
from jax.experimental.pallas import tpu as pltpu

F32 = jnp.float32
BF16 = jnp.bfloat16
MXU_DTYPE = jnp.bfloat16
SAVED_DTYPE = jnp.bfloat16
MESH_ID = pl.DeviceIdType.MESH

D_MODEL = 1024
HEAD_DIM = 64
Q_HEADS = 16
KV_HEADS = 2
GROUP = Q_HEADS // KV_HEADS
KV_WIDTH = KV_HEADS * HEAD_DIM
ATTN_BLOCK = 128
HGRN_HEADS = 8
HGRN_K = 128
CHUNK = 64
HGRN_TOKENS = 256
FFN = 2816
IN_SPLITS = (1024, 256, 4096, 2048)
EPS = 1e-6
NEG_INF = -1e30
ADAM_LR, ADAM_B1, ADAM_B2, ADAM_EPS, ADAM_WD, ADAM_STEP = 0.001, 0.9, 0.999, 1e-08, 0.01, 10
N_CHIPS = 4
VMEM_LIMIT = 60 * 1024 * 1024
ROW_ALIGN = 16


def _params(sem=None):
    return pltpu.CompilerParams(dimension_semantics=sem, vmem_limit_bytes=VMEM_LIMIT)


def _sigmoid(v):
    return 0.5 * jnp.tanh(0.5 * v) + 0.5


def _dot(a, b, dims):
    return lax.dot_general(a.astype(MXU_DTYPE), b.astype(MXU_DTYPE), (dims, ((), ())),
                           preferred_element_type=F32)


def _nn(a, b):
    return _dot(a, b, ((1,), (0,)))


def _nt(a, b):
    return _dot(a, b, ((1,), (1,)))


def _tn(a, b):
    return _dot(a, b, ((0,), (0,)))


HBM_SPEC = pl.BlockSpec(memory_space=pl.ANY)


class _Carried:
    def __init__(self, arrays, out_shapes, n_remote, n_local, build):
        self.parts = [(len(arrays), len(out_shapes), build)]
        self.arrays, self.out_shapes = list(arrays), list(out_shapes)
        self.scratch = [pltpu.SemaphoreType.DMA((n_remote,)), pltpu.SemaphoreType.DMA((n_remote,)),
                        pltpu.SemaphoreType.DMA((max(n_local, 1),))]

    def __add__(self, other):
        both = _Carried([], [], 1, 0, None)
        both.parts = self.parts + other.parts
        both.arrays, both.out_shapes = self.arrays + other.arrays, self.out_shapes + other.out_shapes
        both.scratch = self.scratch + other.scratch
        return both

    def _built(self, ins, outs, sems):
        for p, (ni, no, build) in enumerate(self.parts):
            yield build(ins[:ni], outs[:no], *sems[3 * p:3 * p + 3])
            ins, outs = ins[ni:], outs[no:]

    def start(self, ins, outs, sems):
        core = lax.axis_index("c")
        for sends, _, local, *other_order in self._built(ins, outs, sems):
            for cp in local:
                cp.start()
            if not other_order:
                for cp in sends:
                    cp.start()
                continue

            @pl.when(core == 0)
            def _():
                for cp in sends:
                    cp.start()

            @pl.when(core == 1)
            def _():
                for cp in other_order[0]:
                    cp.start()

    def wait(self, ins, outs, sems):
        for sends, recvs, local, *_ in self._built(ins, outs, sems):
            for cp in recvs:
                cp.wait_recv()
            for cp in sends:
                cp.wait_send()
            for cp in local:
                cp.wait()


def _join(*comms):
    comms = [c for c in comms if c is not None]
    return functools.reduce(lambda a, b: a + b, comms) if comms else None


def _call(name, body, *, grid, in_specs, out_specs, out_shape, args, scratch=(), semantics=None, comm=None,
          aliases=None):
    n_in, n_out, n_scr = len(in_specs), len(out_specs), len(scratch)
    aliases = aliases or {}
    if comm is None:
        res = pl.pallas_call(body, name=name, grid=grid, in_specs=in_specs, out_specs=out_specs, out_shape=out_shape,
                             scratch_shapes=list(scratch), input_output_aliases=aliases,
                             compiler_params=_params(semantics))(*args)
        return list(res), []
    ci, co = len(comm.arrays), len(comm.out_shapes)

    def carrying(*refs):
        ins, refs = refs[:n_in], refs[n_in:]
        c_ins, refs = refs[:ci], refs[ci:]
        outs, refs = refs[:n_out], refs[n_out:]
        c_outs, refs = refs[:co], refs[co:]
        scr, sems = refs[:n_scr], refs[n_scr:]
        if not grid:
            comm.start(c_ins, c_outs, sems)
            body(*ins, *outs, *scr)
            comm.wait(c_ins, c_outs, sems)
            return
        first = functools.reduce(jnp.logical_and, [pl.program_id(a) == 0 for a in range(len(grid))])
        last = functools.reduce(jnp.logical_and, [pl.program_id(a) == g - 1 for a, g in enumerate(grid)])

        @pl.when(first)
        def _():
            comm.start(c_ins, c_outs, sems)

        body(*ins, *outs, *scr)

        @pl.when(last)
        def _():
            comm.wait(c_ins, c_outs, sems)

    res = pl.pallas_call(
        carrying, name=name, grid=grid, in_specs=list(in_specs) + [HBM_SPEC] * ci,
        out_specs=list(out_specs) + [HBM_SPEC] * co, out_shape=list(out_shape) + comm.out_shapes,
        scratch_shapes=list(scratch) + comm.scratch, input_output_aliases=aliases,
        compiler_params=_params(("arbitrary",) * len(grid) if grid else None),
    )(*args, *comm.arrays)
    return list(res[:n_out]), list(res[n_out:])


_Cols = collections.namedtuple("_Cols", "array first cols")
_Into = collections.namedtuple("_Into", "rows first held")


def _weight_grad(name, a, b, *, tm, tk, a_colsum=False, into=None, carrying=False, comm=None):
    cols = a if isinstance(a, _Cols) else _Cols(a, 0, a.shape[1])
    a = cols.array
    (T, N), M = b.shape, cols.cols
    tk = min(tk, T)
    assert cols.first % tm == 0 and M % tm == 0 and T % tk == 0, (name, cols.first, M, tm, T, tk)
    ni, nk, tile0 = M // tm, T // tk, cols.first // tm
    held = list(into.held) if into is not None and into.held is not None else []

    def body(a_ref, b_ref, *rest):
        keep_ref, send_ref = rest[len(held):len(held) + 2]
        sums_ref = rest[len(held) + 2] if a_colsum else None
        if nk == 1:
            acc = _tn(a_ref[...], b_ref[...])
            keep_ref[...], send_ref[...] = acc, acc.astype(send_ref.dtype)
            if a_colsum:
                sums_ref[...] = jnp.sum(a_ref[...].astype(F32), axis=0, keepdims=True)
            return
        acc_ref = rest[-1]
        k = pl.program_id(1)

        @pl.when(k == 0)
        def _():
            acc_ref[...] = jnp.zeros_like(acc_ref)
            if a_colsum:
                sums_ref[...] = jnp.zeros((1, tm), F32)

        if a_colsum:
            sums_ref[...] += jnp.sum(a_ref[...].astype(F32), axis=0, keepdims=True)
        acc_ref[...] += _tn(a_ref[...], b_ref[...])

        @pl.when(k == nk - 1)
        def _():
            keep_ref[...], send_ref[...] = acc_ref[...], acc_ref[...].astype(send_ref.dtype)

    if into is None:
        rows, out_spec = M, pl.BlockSpec((tm, N), lambda i, k: (i, 0))
    else:
        rows = into.rows
        out_spec = pl.BlockSpec((pl.Element(tm), pl.Element(N)),
                                lambda i, k: (pl.multiple_of(into.first + i * tm, ROW_ALIGN), 0))
    out_shape = [jax.ShapeDtypeStruct((rows, N), F32), jax.ShapeDtypeStruct((rows, N), MXU_DTYPE)]
    out_specs = [out_spec, out_spec]
    if a_colsum:
        out_shape.append(jax.ShapeDtypeStruct((1, M), F32))
        out_specs.append(pl.BlockSpec((1, tm), lambda i, k: (0, i)))
    res, got = _call(
        name, body, grid=(ni, nk),
        in_specs=[pl.BlockSpec((tk, tm), lambda i, k: (k, tile0 + i)), pl.BlockSpec((tk, N), lambda i, k: (k, 0))]
        + [HBM_SPEC] * len(held),
        out_specs=out_specs, out_shape=out_shape, scratch=[pltpu.VMEM((tm, N), F32)] if nk > 1 else [],
        args=[a, b] + held, aliases={2 + p: p for p in range(len(held))}, semantics=("parallel", "arbitrary"),
        comm=comm)
    return (res, got) if carrying else res


def _ffn_fwd(merged, w_out, x, gain, w_gate_t, w_up_t, *, tm, comm=None):
    (T, D), F = x.shape, w_gate_t.shape[0]

    def body(m_ref, wo_ref, x_ref, g_ref, wg_ref, wu_ref, h_ref, u_ref, gate_ref, up_ref, z_ref):
        h = x_ref[...] + _nn(m_ref[...], wo_ref[...])
        h_ref[...] = h
        u = (h * lax.rsqrt(jnp.mean(h * h, axis=-1, keepdims=True) + EPS) * g_ref[...]).astype(u_ref.dtype)
        u_ref[...] = u
        gate, up = _nt(u, wg_ref[...]), _nt(u, wu_ref[...])
        gate_ref[...], up_ref[...] = gate.astype(gate_ref.dtype), up.astype(up_ref.dtype)
        z_ref[...] = (gate * _sigmoid(gate) * up).astype(z_ref.dtype)

    rows = lambda n: pl.BlockSpec((tm, n), lambda i: (i, 0))
    fixed = _fixed_spec
    return _call("ffn_hidden", body, grid=(T // tm,),
                 in_specs=[rows(D), fixed(w_out), rows(D), fixed(gain), fixed(w_gate_t), fixed(w_up_t)],
                 out_specs=[rows(D), rows(D), rows(F), rows(F), rows(F)],
                 out_shape=[jax.ShapeDtypeStruct((T, D), F32), jax.ShapeDtypeStruct((T, D), MXU_DTYPE)]
                 + [jax.ShapeDtypeStruct((T, F), SAVED_DTYPE)] * 2 + [jax.ShapeDtypeStruct((T, F), MXU_DTYPE)],
                 args=[merged, w_out, x, gain, w_gate_t, w_up_t], semantics=("parallel",), comm=comm)


def _in_proj(x, gain, w_in_t, b_in, *, tm, comm=None):
    T, D = x.shape
    bounds = [sum(IN_SPLITS[:i]) for i in range(len(IN_SPLITS) + 1)]

    def body(x_ref, g_ref, w_ref, b_ref, u_ref, *piece_refs):
        xv = x_ref[...]
        r = lax.rsqrt(jnp.mean(xv * xv, axis=-1, keepdims=True) + EPS)
        u = (xv * r * g_ref[...]).astype(u_ref.dtype)
        u_ref[...] = u
        for o_ref, lo, hi in zip(piece_refs, bounds[:-1], bounds[1:]):
            o_ref[...] = (_nt(u, w_ref[lo:hi, :]) + b_ref[:, lo:hi]).astype(o_ref.dtype)

    rows = lambda n: pl.BlockSpec((tm, n), lambda i: (i, 0))
    fixed = _fixed_spec
    dtypes = (MXU_DTYPE, MXU_DTYPE, F32, F32)
    return _call("in_proj", body, grid=(T // tm,),
                 in_specs=[rows(D), fixed(gain), fixed(w_in_t), fixed(b_in)],
                 out_specs=[rows(D)] + [rows(n) for n in IN_SPLITS],
                 out_shape=[jax.ShapeDtypeStruct((T, D), MXU_DTYPE)]
                 + [jax.ShapeDtypeStruct((T, n), dt) for n, dt in zip(IN_SPLITS, dtypes)],
                 args=[x, gain, w_in_t, b_in], semantics=("parallel",), comm=comm)


def _row_spec(tm, n):
    return pl.BlockSpec((tm, n), lambda i: (i, 0))


def _fixed_spec(a):
    return pl.BlockSpec(a.shape, lambda i: (0,) * a.ndim, pipeline_mode=pl.Buffered(1))


def _partials_spec(n):
    return pl.BlockSpec((8, n), lambda i: (i, 0))


def _ffn_tail(z, w_down, h1, target, gain, gate, up, *, tm):
    (T, F), D = z.shape, h1.shape[1]

    def body(z_ref, w_ref, h_ref, t_ref, g_ref, gate_ref, up_ref, dh_ref, dhb_ref, dgate_ref, dup_ref, dg_ref, l_ref):
        h2 = h_ref[...] + _nn(z_ref[...], w_ref[...])
        r = lax.rsqrt(jnp.mean(h2 * h2, axis=-1, keepdims=True) + EPS)
        xhat = h2 * r
        err = xhat * g_ref[...] - t_ref[...]
        part = 0.5 * jnp.sum(jnp.sum(err * err, axis=-1, keepdims=True), axis=0, keepdims=True) / D
        dy = err / D
        dxh = dy * g_ref[...]
        dh2 = r * (dxh - xhat * jnp.mean(dxh * xhat, axis=-1, keepdims=True))
        dh_ref[...] = dh2
        dhb = dh2.astype(dhb_ref.dtype)
        dhb_ref[...] = dhb
        dg_ref[...] = jnp.broadcast_to(jnp.sum(dy * xhat, axis=0, keepdims=True), dg_ref.shape)
        l_ref[...] = jnp.broadcast_to(part, l_ref.shape)
        dz = _nt(dhb, w_ref[...])
        gv, upv = gate_ref[...].astype(F32), up_ref[...].astype(F32)
        s = _sigmoid(gv)
        dgate_ref[...] = (dz * upv * (s * (1.0 + gv * (1.0 - s)))).astype(dgate_ref.dtype)
        dup_ref[...] = (dz * (gv * s)).astype(dup_ref.dtype)

    low = lambda n: jax.ShapeDtypeStruct((T, n), MXU_DTYPE)
    part = jax.ShapeDtypeStruct((8 * (T // tm), D), F32)
    return pl.pallas_call(
        body, name="ffn_tail", grid=(T // tm,),
        in_specs=[_row_spec(tm, F), _fixed_spec(w_down), _row_spec(tm, D), _row_spec(tm, D), _fixed_spec(gain),
                  _row_spec(tm, F), _row_spec(tm, F)],
        out_specs=[_row_spec(tm, D), _row_spec(tm, D), _row_spec(tm, F), _row_spec(tm, F), _partials_spec(D),
                   _partials_spec(D)],
        out_shape=[jax.ShapeDtypeStruct((T, D), F32), low(D), low(F), low(F), part, part],
        compiler_params=_params(("parallel",)),
    )(z, w_down, h1, target, gain, gate, up)


def _ffn_in_bwd(dgate, dup, w_gate_t, w_up_t, h1, gain, dres, *, tm, comm=None):
    (T, F), D = dgate.shape, h1.shape[1]

    def body(dg_ref, du_ref, wg_ref, wu_ref, h_ref, g_ref, r_ref, dh_ref, dhb_ref, dgain_ref):
        d_u2 = _nn(dg_ref[...], wg_ref[...]) + _nn(du_ref[...], wu_ref[...])
        dx, dgain = _rmsnorm_bwd_vals(d_u2, h_ref[...], g_ref[...])
        dh = r_ref[...] + dx
        dh_ref[...] = dh
        dhb_ref[...] = dh.astype(dhb_ref.dtype)
        dgain_ref[...] = jnp.broadcast_to(dgain, dgain_ref.shape)

    return _call("d_ffn_in", body, grid=(T // tm,),
                 in_specs=[_row_spec(tm, F), _row_spec(tm, F), _fixed_spec(w_gate_t), _fixed_spec(w_up_t),
                           _row_spec(tm, D), _fixed_spec(gain), _row_spec(tm, D)],
                 out_specs=[_row_spec(tm, D), _row_spec(tm, D), _partials_spec(D)],
                 out_shape=[jax.ShapeDtypeStruct((T, D), F32), jax.ShapeDtypeStruct((T, D), MXU_DTYPE),
                            jax.ShapeDtypeStruct((8 * (T // tm), D), F32)],
                 args=[dgate, dup, w_gate_t, w_up_t, h1, gain, dres], semantics=("parallel",), comm=comm)


def _in_proj_bwd(pieces, w_in_t, x, gain, dres, *, tm, comm=None):
    T, D = x.shape
    n = len(pieces)

    def body(*refs):
        dps, (w_ref, x_ref, g_ref, r_ref, dx_ref, dgain_ref) = refs[:n], refs[n:]
        d_u = None
        for dp_ref, (dp, first) in zip(dps, pieces):
            term = _nn(dp_ref[...], w_ref[first:first + dp.shape[1], :])
            d_u = term if d_u is None else d_u + term
        dx, dgain = _rmsnorm_bwd_vals(d_u, x_ref[...], g_ref[...])
        dx_ref[...] = r_ref[...] + dx
        dgain_ref[...] = jnp.broadcast_to(dgain, dgain_ref.shape)

    return _call("d_u", body, grid=(T // tm,),
                 in_specs=[_row_spec(tm, dp.shape[1]) for dp, _ in pieces]
                 + [_fixed_spec(w_in_t), _row_spec(tm, D), _fixed_spec(gain), _row_spec(tm, D)],
                 out_specs=[_row_spec(tm, D), _partials_spec(D)],
                 out_shape=[jax.ShapeDtypeStruct((T, D), F32), jax.ShapeDtypeStruct((8 * (T // tm), D), F32)],
                 args=[dp for dp, _ in pieces] + [w_in_t, x, gain, dres], semantics=("parallel",), comm=comm)


def _merge_fwd(y_a, y_b, w_a, w_b, gates, *, tm, comm=None):
    T, D = y_a.shape

    def body(ya_ref, yb_ref, wa_ref, wb_ref, ga_ref, gb_ref, pa_ref, pb_ref, m_ref):
        pa, pb = _nn(ya_ref[...], wa_ref[...]), _nn(yb_ref[...], wb_ref[...])
        pa_ref[...], pb_ref[...] = pa.astype(pa_ref.dtype), pb.astype(pb_ref.dtype)
        m_ref[...] = (_sigmoid(ga_ref[...]) * pa + _sigmoid(gb_ref[...]) * pb).astype(m_ref.dtype)

    rows = pl.BlockSpec((tm, D), lambda i: (i, 0))
    whole = pl.BlockSpec((D, D), lambda i: (0, 0), pipeline_mode=pl.Buffered(1))
    return _call("branch_merge", body, grid=(T // tm,),
                 in_specs=[rows, rows, whole, whole, rows, pl.BlockSpec((tm, D), lambda i: (i, 1))],
                 out_specs=[rows] * 3,
                 out_shape=[jax.ShapeDtypeStruct((T, D), SAVED_DTYPE)] * 2 + [jax.ShapeDtypeStruct((T, D), MXU_DTYPE)],
                 args=[y_a, y_b, w_a, w_b, gates, gates], semantics=("parallel",), comm=comm)


def _merge_bwd(dh, w_out, w_a, w_b, p_a, p_b, gates, *, tm, d_in_width, first):
    T, D = dh.shape

    def body(dh_ref, wo_ref, wa_ref, wb_ref, pa_ref, pb_ref, ga_ref, gb_ref, dpa_ref, dpb_ref, dya_ref, dyb_ref,
             din_ref):
        dm = _nt(dh_ref[...], wo_ref[...])
        sa, sb = _sigmoid(ga_ref[...]), _sigmoid(gb_ref[...])
        dpa, dpb = (dm * sa).astype(dpa_ref.dtype), (dm * sb).astype(dpb_ref.dtype)
        dpa_ref[...], dpb_ref[...] = dpa, dpb
        din_ref[:, :D] = (dm * pa_ref[...].astype(F32) * sa * (1.0 - sa)).astype(din_ref.dtype)
        din_ref[:, D:] = (dm * pb_ref[...].astype(F32) * sb * (1.0 - sb)).astype(din_ref.dtype)
        dya_ref[...] = _nt(dpa, wa_ref[...]).astype(dya_ref.dtype)
        dyb_ref[...] = _nt(dpb, wb_ref[...]).astype(dyb_ref.dtype)

    rows = pl.BlockSpec((tm, D), lambda i: (i, 0))
    whole = pl.BlockSpec((D, D), lambda i: (0, 0), pipeline_mode=pl.Buffered(1))
    low = jax.ShapeDtypeStruct((T, D), MXU_DTYPE)
    return pl.pallas_call(
        body, name="d_branch_merge", grid=(T // tm,),
        in_specs=[rows, whole, whole, whole, rows, rows, rows, pl.BlockSpec((tm, D), lambda i: (i, 1))],
        out_specs=[rows] * 4 + [pl.BlockSpec((pl.Element(tm), pl.Element(2 * D)), lambda i: (
            pl.multiple_of(i * tm, ROW_ALIGN), first))],
        out_shape=[low] * 3 + [jax.ShapeDtypeStruct((T, D), F32), jax.ShapeDtypeStruct((T, d_in_width), MXU_DTYPE)],
        compiler_params=_params(("parallel",)),
    )(dh, w_out, w_a, w_b, p_a, p_b, gates, gates)


def _colsum_partials(p):
    return jnp.sum(p.reshape(-1, 8, p.shape[-1])[:, 0, :], axis=0, keepdims=True)


def _rmsnorm_bwd_vals(dy, xin, g):
    rstd = lax.rsqrt(jnp.mean(xin * xin, axis=-1, keepdims=True) + EPS)
    xhat = xin * rstd
    dg = jnp.sum(dy * xhat, axis=0, keepdims=True)
    dxh = dy * g
    dx = rstd * (dxh - xhat * jnp.mean(dxh * xhat, axis=-1, keepdims=True))
    return dx, dg


ATTN_SCALE = 1.0 / math.sqrt(HEAD_DIM)
GROUP_LANES = GROUP * ATTN_BLOCK
PAIR = 2 * HEAD_DIM


def _attn_mask():
    kj = lax.broadcasted_iota(jnp.int32, (ATTN_BLOCK, GROUP_LANES), 0)
    qi = lax.broadcasted_iota(jnp.int32, (ATTN_BLOCK, GROUP_LANES), 1) & (ATTN_BLOCK - 1)
    return kj <= qi


def _heads_transposed(ref, g, scale=None):
    parts = []
    for a in range(GROUP // 2):
        lo = (g * GROUP // 2 + a) * PAIR
        pair = ref[:, lo:lo + PAIR].astype(F32)
        pair = (pair if scale is None else pair * scale).T
        parts += [pair[:HEAD_DIM], pair[HEAD_DIM:]]
    return jnp.concatenate(parts, axis=1).astype(MXU_DTYPE)


def _heads_back(ref, g, vt):
    for a in range(GROUP // 2):
        lo = (g * GROUP // 2 + a) * PAIR
        pair = jnp.concatenate([vt[:, (2 * a) * ATTN_BLOCK:(2 * a + 1) * ATTN_BLOCK],
                                vt[:, (2 * a + 1) * ATTN_BLOCK:(2 * a + 2) * ATTN_BLOCK]], axis=0)
        ref[:, lo:lo + PAIR] = pair.T.astype(ref.dtype)


def _kv_parts(kv_ref, g):
    ks = slice(g * HEAD_DIM, (g + 1) * HEAD_DIM)
    vs = slice(KV_WIDTH + g * HEAD_DIM, KV_WIDTH + (g + 1) * HEAD_DIM)
    return kv_ref[:, ks].astype(MXU_DTYPE), kv_ref[:, vs].astype(MXU_DTYPE)


def _sink_rows(sinks):
    return jnp.repeat(sinks.reshape(KV_HEADS, GROUP), ATTN_BLOCK, axis=1)


def _attn_fwd(pq, pkv, sinks, comm=None):
    T = pq.shape[0]
    nb = T // ATTN_BLOCK

    def body(q_ref, kvc_ref, kvp_ref, s_ref, y_ref, lse_ref):
        mask_c = _attn_mask()
        has_prev = pl.program_id(0) > 0
        for g in range(KV_HEADS):
            (kc, vc), (kp, vp) = _kv_parts(kvc_ref, g), _kv_parts(kvp_ref, g)
            qt = _heads_transposed(q_ref, g, ATTN_SCALE)
            s = jnp.where(mask_c, _nn(kc, qt), jnp.where(has_prev, _nn(kp, qt), NEG_INF))
            sink = s_ref[g:g + 1, :]
            m = jnp.maximum(jnp.max(s, axis=0, keepdims=True), sink)
            p = jnp.exp(s - m)
            den = jnp.sum(p, axis=0, keepdims=True) + jnp.exp(sink - m)
            pc = jnp.where(mask_c, p, 0.0)
            _heads_back(y_ref, g, (_tn(vc, pc) + _tn(vp, p - pc)) / den)
            lse = m + jnp.log(den)
            for i in range(GROUP):
                lse_ref[g * GROUP + i:g * GROUP + i + 1, :] = lse[:, i * ATTN_BLOCK:(i + 1) * ATTN_BLOCK]

    return _call(
        "attn_fwd", body, grid=(nb,),
        in_specs=[pl.BlockSpec((ATTN_BLOCK, D_MODEL), lambda n: (n, 0)),
                  pl.BlockSpec((ATTN_BLOCK, 2 * KV_WIDTH), lambda n: (n, 0)),
                  pl.BlockSpec((ATTN_BLOCK, 2 * KV_WIDTH), lambda n: (jnp.maximum(n - 1, 0), 0)),
                  pl.BlockSpec((KV_HEADS, GROUP_LANES), lambda n: (0, 0))],
        out_specs=[pl.BlockSpec((ATTN_BLOCK, D_MODEL), lambda n: (n, 0)),
                   pl.BlockSpec((Q_HEADS, ATTN_BLOCK), lambda n: (0, n))],
        out_shape=[jax.ShapeDtypeStruct((T, D_MODEL), MXU_DTYPE), jax.ShapeDtypeStruct((Q_HEADS, T), F32)],
        args=[pq, pkv, pkv, _sink_rows(sinks)], semantics=("parallel",), comm=comm)


def _attn_bwd(pq, pkv, sinks, lse, dy, d_in, comm=None):
    T = pq.shape[0]
    nb = T // ATTN_BLOCK
    cur = lambda n: (jnp.minimum(n, nb - 1), 0)
    done = D_MODEL + 2 * KV_WIDTH

    def body(q_ref, kvc_ref, kvp_ref, s_ref, lse_ref, dy_ref, _, out_ref, ds_ref, carry, top, bot, dq_ref):
        n = pl.program_id(0)

        @pl.when(n == 0)
        def _():
            carry[...] = jnp.zeros_like(carry)
            dq_ref[...] = jnp.zeros_like(dq_ref)
            ds_ref[...] = jnp.zeros_like(ds_ref)

        out_ref[:, :D_MODEL] = dq_ref[...]

        @pl.when(n < nb)
        def _():
            mask_c = _attn_mask()
            valid = jnp.logical_or(mask_c, n > 0)
            for g in range(KV_HEADS):
                ks = slice(g * HEAD_DIM, (g + 1) * HEAD_DIM)
                vs = slice(KV_WIDTH + g * HEAD_DIM, KV_WIDTH + (g + 1) * HEAD_DIM)
                (kc, vc), (kp, vp) = _kv_parts(kvc_ref, g), _kv_parts(kvp_ref, g)
                qt = _heads_transposed(q_ref, g, ATTN_SCALE)
                dot = _heads_transposed(dy_ref, g)
                lse = jnp.concatenate([lse_ref[g * GROUP + i:g * GROUP + i + 1, :] for i in range(GROUP)], axis=1)
                p = jnp.where(valid, jnp.exp(jnp.where(mask_c, _nn(kc, qt), _nn(kp, qt)) - lse), 0.0)
                dp = jnp.where(mask_c, _nn(vc, dot), _nn(vp, dot))
                delta = jnp.sum(p * dp, axis=0, keepdims=True)
                ds = p * (dp - delta)
                ds_c, p_c = jnp.where(mask_c, ds, 0.0), jnp.where(mask_c, p, 0.0)
                ds_p, p_p = ds - ds_c, p - p_c
                _heads_back(dq_ref, g, (_tn(kc, ds_c) + _tn(kp, ds_p)) * ATTN_SCALE)
                bot[:, ks], bot[:, vs] = _nt(ds_c, qt), _nt(p_c, dot)
                top[:, ks], top[:, vs] = _nt(ds_p, qt), _nt(p_p, dot)
                ds_ref[g:g + 1, :] -= jnp.exp(s_ref[g:g + 1, :] - lse) * delta
            out_ref[:, D_MODEL:] = (carry[...] + top[...]).astype(out_ref.dtype)
            carry[...] = bot[...]

        @pl.when(n == nb)
        def _():
            out_ref[:, D_MODEL:] = carry[...].astype(out_ref.dtype)

    return _call(
        "attn_bwd", body, grid=(nb + 1,),
        in_specs=[pl.BlockSpec((ATTN_BLOCK, D_MODEL), cur),
                  pl.BlockSpec((ATTN_BLOCK, 2 * KV_WIDTH), cur),
                  pl.BlockSpec((ATTN_BLOCK, 2 * KV_WIDTH), lambda n: (jnp.maximum(jnp.minimum(n, nb - 1) - 1, 0), 0)),
                  pl.BlockSpec((KV_HEADS, GROUP_LANES), lambda n: (0, 0)),
                  pl.BlockSpec((Q_HEADS, ATTN_BLOCK), lambda n: (0, jnp.minimum(n, nb - 1))),
                  pl.BlockSpec((ATTN_BLOCK, D_MODEL), cur), HBM_SPEC],
        out_specs=[pl.BlockSpec((ATTN_BLOCK, done), lambda n: (jnp.maximum(n - 1, 0), 0)),
                   pl.BlockSpec((KV_HEADS, GROUP_LANES), lambda n: (0, 0))],
        out_shape=[jax.ShapeDtypeStruct(d_in.shape, d_in.dtype), jax.ShapeDtypeStruct((KV_HEADS, GROUP_LANES), F32)],
        scratch=[pltpu.VMEM((ATTN_BLOCK, 2 * KV_WIDTH), F32)] * 3 + [pltpu.VMEM((ATTN_BLOCK, D_MODEL), MXU_DTYPE)],
        args=[pq, pkv, pkv, _sink_rows(sinks), lse, dy, d_in], semantics=("arbitrary",), comm=comm, aliases={6: 0})


def _lower_bound(l):
    m = jnp.maximum(l[0:1], l[1:2])
    e0, e1 = jnp.exp(l[0:1] - m), jnp.exp(l[1:2] - m)
    return e0 / (e0 + e1)


def _tri(lower):
    r = lax.broadcasted_iota(jnp.int32, (CHUNK, CHUNK), 0)
    c = lax.broadcasted_iota(jnp.int32, (CHUNK, CHUNK), 1)
    return (r >= c) if lower else (c >= r)


def _chunk_sum(mask, v):
    ones = mask.astype(BF16)
    hi = v.astype(BF16)
    rest = v - hi.astype(F32)
    mid = rest.astype(BF16)
    lo = (rest - mid.astype(F32)).astype(BF16)
    part = lambda t: lax.dot_general(ones, t, (((1,), (0,)), ((), ())), preferred_element_type=F32)
    return part(hi) + part(mid) + part(lo)


def _hgrn_chunk_inputs(hq, hf, lb, causal):
    half_t = 0.5 * jnp.tanh(0.5 * hf)
    sg, sgn = 0.5 + half_t, 0.5 - half_t
    f = lb + (1.0 - lb) * sg
    kk = (1.0 - lb) * sgn
    sq = _sigmoid(hq)
    q = hq * sq
    b = _chunk_sum(causal, jnp.log(f))
    bm, bl = b[CHUNK // 2 - 1:CHUNK // 2, :], b[CHUNK - 1:CHUNK, :]
    e_qm, e_km = jnp.exp(b - bm), jnp.exp(bm - b)
    e_qs, e_kl = e_qm * jnp.exp(bm), e_km * jnp.exp(bl - bm)
    return dict(sg=sg, sgn=sgn, f=f, kk=kk, sq=sq, q=q, e_qm=e_qm, e_km=e_km, e_qs=e_qs, e_kl=e_kl,
                qm=q * e_qm, km=kk * e_km, qs=q * e_qs, kl=kk * e_kl, el=jnp.exp(bl))


def _hgrn_fwd(ph, lb_logits, norm_g, comm=None):
    T = ph.shape[0]
    nblk, cpb = T // HGRN_TOKENS, HGRN_TOKENS // CHUNK
    col = lambda c: pl.BlockSpec((HGRN_TOKENS, D_MODEL), functools.partial(lambda i, c: (i, c), c=c))

    def body(hq_ref, hf_ref, hi_ref, hg_ref, l_ref, ng_ref, y_ref, o_ref, st_ref, s_ref):
        @pl.when(pl.program_id(0) == 0)
        def _():
            s_ref[...] = jnp.zeros_like(s_ref)

        lb = _lower_bound(l_ref[...])
        causal = _tri(True)
        for c in range(cpb):
            rows = slice(c * CHUNK, (c + 1) * CHUNK)
            t = _hgrn_chunk_inputs(hq_ref[rows, :], hf_ref[rows, :], lb, causal)
            qm, km, qs, kl = (t[n].astype(MXU_DTYPE) for n in ("qm", "km", "qs", "kl"))
            v = hi_ref[rows, :].astype(MXU_DTYPE)
            for h in range(HGRN_HEADS):
                ls = slice(h * HGRN_K, (h + 1) * HGRN_K)
                st = s_ref[h]
                st_ref[c, ls, :] = st
                a = jnp.where(causal, _nt(qm[:, ls], km[:, ls]), 0.0)
                o_ref[rows, ls] = _nn(a, v[:, ls]) + _nt(qs[:, ls], st)
                s_ref[h] = t["el"][:, ls] * st + _tn(v[:, ls], kl[:, ls])
        for h in range(HGRN_HEADS):
            ls = slice(h * HGRN_K, (h + 1) * HGRN_K)
            o = o_ref[:, ls]
            r = lax.rsqrt(jnp.mean(o * o, axis=-1, keepdims=True) + EPS)
            y_ref[:, ls] = (o * r * ng_ref[:, ls] * _sigmoid(hg_ref[:, ls])).astype(y_ref.dtype)

    return _call(
        "hgrn_fwd", body, grid=(nblk,),
        in_specs=[col(0), col(1), col(2), col(3),
                  pl.BlockSpec((2, D_MODEL), lambda i: (0, 0)), pl.BlockSpec((1, D_MODEL), lambda i: (0, 0))],
        out_specs=[pl.BlockSpec((HGRN_TOKENS, D_MODEL), lambda i: (i, 0)),
                   pl.BlockSpec((HGRN_TOKENS, D_MODEL), lambda i: (i, 0)),
                   pl.BlockSpec((cpb, D_MODEL, HGRN_K), lambda i: (i, 0, 0))],
        out_shape=[jax.ShapeDtypeStruct((T, D_MODEL), MXU_DTYPE), jax.ShapeDtypeStruct((T, D_MODEL), F32),
                   jax.ShapeDtypeStruct((T // CHUNK, D_MODEL, HGRN_K), F32)],
        scratch=[pltpu.VMEM((HGRN_HEADS, HGRN_K, HGRN_K), F32)],
        args=[ph, ph, ph, ph, lb_logits, norm_g], semantics=("arbitrary",), comm=comm)


def _hgrn_bwd(ph, o_raw, states, dy, lb_logits, norm_g, d_in, first, comm=None):
    T = ph.shape[0]
    nblk, cpb = T // HGRN_TOKENS, HGRN_TOKENS // CHUNK
    rev = lambda i: nblk - 1 - i
    col = lambda c: pl.BlockSpec((HGRN_TOKENS, D_MODEL), functools.partial(lambda i, c: (rev(i), c), c=c))
    tok = pl.BlockSpec((HGRN_TOKENS, D_MODEL), lambda i: (rev(i), 0))

    def body(hq_ref, hf_ref, hi_ref, hg_ref, o_ref, st_ref, dy_ref, l_ref, ng_ref, _,
             dph_ref, dng_ref, dl_ref, dst_ref, dlb_ref, do_s, dqm_s, dkm_s, dqs_s, dkl_s, dv_s, del_s):
        i = pl.program_id(0)

        @pl.when(i == 0)
        def _():
            dst_ref[...] = jnp.zeros_like(dst_ref)
            dlb_ref[...] = jnp.zeros_like(dlb_ref)
            dng_ref[...] = jnp.zeros_like(dng_ref)

        lb = _lower_bound(l_ref[...])
        causal, anti = _tri(True), _tri(False)
        row = lax.broadcasted_iota(jnp.int32, (CHUNK, D_MODEL), 0)
        for c in reversed(range(cpb)):
            rows = slice(c * CHUNK, (c + 1) * CHUNK)
            hq = hq_ref[rows, :]
            t = _hgrn_chunk_inputs(hq, hf_ref[rows, :], lb, causal)
            sgg = _sigmoid(hg_ref[rows, :])
            dyv = dy_ref[rows, :]
            for h in range(HGRN_HEADS):
                ls = slice(h * HGRN_K, (h + 1) * HGRN_K)
                o = o_ref[rows, ls]
                r = lax.rsqrt(jnp.mean(o * o, axis=-1, keepdims=True) + EPS)
                nrm = o * r
                g_h = sgg[:, ls]
                dph_ref[rows, 3 * D_MODEL + h * HGRN_K:3 * D_MODEL + (h + 1) * HGRN_K] = (
                    dyv[:, ls] * nrm * ng_ref[:, ls] * g_h * (1.0 - g_h)).astype(dph_ref.dtype)
                dyg = dyv[:, ls] * g_h
                dng_ref[:, ls] += jnp.sum(dyg * nrm, axis=0, keepdims=True)
                dn = dyg * ng_ref[:, ls]
                do_s[:, ls] = r * (dn - nrm * jnp.mean(dn * nrm, axis=-1, keepdims=True))
            qm, km, qs, kl = (t[n].astype(MXU_DTYPE) for n in ("qm", "km", "qs", "kl"))
            v = hi_ref[rows, :].astype(MXU_DTYPE)
            do = do_s[...].astype(MXU_DTYPE)
            for h in range(HGRN_HEADS):
                ls = slice(h * HGRN_K, (h + 1) * HGRN_K)
                st = st_ref[c, ls, :]
                dst = dst_ref[h]
                a = jnp.where(causal, _nt(qm[:, ls], km[:, ls]), 0.0)
                da = jnp.where(causal, _nt(do[:, ls], v[:, ls]), 0.0)
                dv_s[:, ls] = _tn(a, do[:, ls]) + _nt(kl[:, ls], dst)
                dkl_s[:, ls] = _nn(v[:, ls], dst)
                dqs_s[:, ls] = _nn(do[:, ls], st)
                del_s[:, ls] = jnp.sum(dst * st, axis=0, keepdims=True)
                dst_ref[h] = _tn(do[:, ls], qs[:, ls]) + t["el"][:, ls] * dst
                dqm_s[:, ls] = _nn(da, km[:, ls])
                dkm_s[:, ls] = _tn(da, qm[:, ls])
            dqm, dkm, dqs, dkl = dqm_s[...], dkm_s[...], dqs_s[...], dkl_s[...]
            dq = dqm * t["e_qm"] + dqs * t["e_qs"]
            dk = dkm * t["e_km"] + dkl * t["e_kl"]
            t_qm, t_km, t_kl = dqm * t["qm"], dkm * t["km"], dkl * t["kl"]
            db = t_qm - t_km + dqs * t["qs"] - t_kl
            db_mid = jnp.sum(t_km - t_qm, axis=0, keepdims=True)
            db_last = jnp.sum(t_kl, axis=0, keepdims=True) + del_s[...] * t["el"]
            db = db + jnp.where(row == CHUNK // 2 - 1, db_mid, 0.0) + jnp.where(row == CHUNK - 1, db_last, 0.0)
            dlogf = _chunk_sum(anti, db)
            sq, sg, sgn, f = t["sq"], t["sg"], t["sgn"], t["f"]
            dph_ref[rows, 0:D_MODEL] = (dq * (sq * (1.0 + hq * (1.0 - sq)))).astype(dph_ref.dtype)
            dph_ref[rows, D_MODEL:2 * D_MODEL] = (
                dlogf * (1.0 - lb) * sg * (1.0 - sg) / f - dk * (1.0 - lb) * sgn * (1.0 - sgn)).astype(dph_ref.dtype)
            dph_ref[rows, 2 * D_MODEL:3 * D_MODEL] = dv_s[...].astype(dph_ref.dtype)
            dlb_ref[...] += jnp.sum(dlogf * (1.0 - sg) / f - dk * sgn, axis=0, keepdims=True)

        @pl.when(i == nblk - 1)
        def _():
            dl0 = dlb_ref[...] * lb * (1.0 - lb)
            dl_ref[0:1, :] = dl0
            dl_ref[1:2, :] = -dl0

    wide = pltpu.VMEM((CHUNK, D_MODEL), F32)
    return _call(
        "hgrn_bwd", body, grid=(nblk,),
        in_specs=[col(0), col(1), col(2), col(3), tok,
                  pl.BlockSpec((cpb, D_MODEL, HGRN_K), lambda i: (rev(i), 0, 0)), tok,
                  pl.BlockSpec((2, D_MODEL), lambda i: (0, 0)), pl.BlockSpec((1, D_MODEL), lambda i: (0, 0)), HBM_SPEC],
        out_specs=[pl.BlockSpec((pl.Element(HGRN_TOKENS), pl.Element(4 * D_MODEL)), lambda i: (
                       pl.multiple_of(rev(i) * HGRN_TOKENS, ROW_ALIGN), first)),
                   pl.BlockSpec((1, D_MODEL), lambda i: (0, 0)), pl.BlockSpec((2, D_MODEL), lambda i: (0, 0))],
        out_shape=[jax.ShapeDtypeStruct(d_in.shape, d_in.dtype), jax.ShapeDtypeStruct((1, D_MODEL), F32),
                   jax.ShapeDtypeStruct((2, D_MODEL), F32)],
        scratch=[pltpu.VMEM((HGRN_HEADS, HGRN_K, HGRN_K), F32), pltpu.VMEM((1, D_MODEL), F32),
                 wide, wide, wide, wide, wide, wide, pltpu.VMEM((1, D_MODEL), F32)],
        args=[ph, ph, ph, ph, o_raw, states, dy, lb_logits, norm_g, d_in], semantics=("arbitrary",), comm=comm,
        aliases={9: 0})


def _local_step(x, target, vec, net):
    T, D = x.shape
    norm_mix_g, b_in, sinks, lb_logits = vec["norm_mix_g"], vec["b_in"], vec["attn_sinks"], vec["hgrn_lb_logits"]
    hgrn_norm_g, norm_ffn_g, norm_final_g = vec["hgrn_norm_g"], vec["norm_ffn_g"], vec["norm_final_g"]
    w_in = net.full("w_in")
    o_q, o_kv, o_h, o_g = (sum(IN_SPLITS[:i]) for i in range(4))
    TM = 512

    names = ("w_ffn_gate",)
    (u, pq, pkv, ph, pg), got = _in_proj(x, norm_mix_g, w_in, b_in, tm=256, comm=net.gather(names))
    net.gathered(names, got)
    names = ("w_branch_attn", "w_branch_hgrn")
    (y_attn, lse), got = _attn_fwd(pq, pkv, sinks, comm=net.gather(names))
    net.gathered(names, got)
    names = ("w_ffn_up",)
    (y_hgrn, o_raw, states), got = _hgrn_fwd(ph, lb_logits, hgrn_norm_g, comm=net.gather(names))
    net.gathered(names, got)
    w_ba, w_bh = net.full("w_branch_attn"), net.full("w_branch_hgrn")
    w_gate, w_up = net.full("w_ffn_gate"), net.full("w_ffn_up")
    names = ("w_out",)
    (ya, yb, merged), got = _merge_fwd(y_attn, y_hgrn, w_ba, w_bh, pg, tm=TM, comm=net.gather(names))
    net.gathered(names, got)
    w_out = net.full("w_out")

    FT = FFN // 2
    names = ("w_ffn_down",)
    (h1, u2, gpre, up, z), got = _ffn_fwd(merged, w_out, x, norm_ffn_g, w_gate, w_up, tm=256,
                                          comm=net.gather(names))
    net.gathered(names, got)
    w_down = net.full("w_ffn_down")

    dh2, dh2b, dgp, dup, dgf_p, loss_p = _ffn_tail(z, w_down, h1, target, norm_final_g, gpre, up, tm=256)
    loss = jnp.sum(loss_p.reshape(-1, 8, D)[:, 0, 0])
    d_norm_final = _colsum_partials(dgf_p)
    d_w_down = _weight_grad("dw_down", z, dh2b, tm=FT, tk=T)

    (dh1, dh1b, dg2_p), _ = _ffn_in_bwd(dgp, dup, w_gate, w_up, h1, norm_ffn_g, dh2, tm=256)
    d_norm_ffn = _colsum_partials(dg2_p)
    d_w_gate = _weight_grad("dw_gate", dgp, u2, tm=FT, tk=T)
    d_w_up = _weight_grad("dw_up", dup, u2, tm=FT, tk=T)

    rows_in = sum(IN_SPLITS)
    dya, dyb, dy_attn, dy_hgrn, d_in = _merge_bwd(dh1b, w_out, w_ba, w_bh, ya, yb, pg, tm=TM, d_in_width=rows_in,
                                                  first=o_g)
    d_w_out = _weight_grad("dw_out", merged, dh1b, tm=1024, tk=1024)
    d_w_ba = _weight_grad("dw_branch_a", y_attn, dya, tm=1024, tk=1024)
    d_w_bh = _weight_grad("dw_branch_b", y_hgrn, dyb, tm=1024, tk=1024)

    names, swap = ("w_ffn_down",), ()
    (d_in, dsink), got = _attn_bwd(pq, pkv, sinks, lse, dy_attn, d_in,
                                   comm=net.exchange(dict(w_ffn_down=[d_w_down]), swap))
    net.received(names, swap, got)
    names, swap = ("w_ffn_gate", "w_ffn_up"), ("w_ffn_down",)
    (d_in, d_hgrn_norm, d_lb_logits), got = _hgrn_bwd(
        ph, o_raw, states, dy_hgrn, lb_logits, hgrn_norm_g, d_in, o_h,
        comm=net.exchange(dict(w_ffn_gate=[d_w_gate], w_ffn_up=[d_w_up]), swap))
    net.received(names, swap, got)

    main = rows_in // 1024 * 1024
    names, swap = ("w_out", "w_branch_attn", "w_branch_hgrn"), ("w_ffn_gate", "w_ffn_up")
    (*d_w_in, db_main), got = _weight_grad(
        "dw_in", _Cols(d_in, 0, main), u, tm=1024, tk=T, a_colsum=True, into=_Into(rows_in, 0, None), carrying=True,
        comm=net.exchange(dict(w_out=[d_w_out], w_branch_attn=[d_w_ba], w_branch_hgrn=[d_w_bh]), swap))
    net.received(names, swap, got)
    *d_w_in, db_rest = _weight_grad("dw_in_rest", _Cols(d_in, main, rows_in - main), u, tm=rows_in - main, tk=1024,
                                    a_colsum=True, into=_Into(rows_in, main, d_w_in))
    d_w_in = [tuple(d_w_in)]

    first_level = net.presum_begin("w_in", d_w_in)
    halves = net.presum_end("w_in", [] if first_level is None else _copies_alone("presum_swap_w_in", first_level))
    names, swap = ("w_in",), ("w_out", "w_branch_attn", "w_branch_hgrn")
    (dx, dg1_p), got = _in_proj_bwd([(d_in, 0)], w_in, x, norm_mix_g, dh1, tm=256,
                                    comm=_join(halves, net.swap(swap)))
    net.last = (names, swap, got)
    d_norm_mix = _colsum_partials(dg1_p)
    d_b_in = jnp.concatenate([db_main, db_rest], axis=1)
    vecs = dict(norm_mix_g=d_norm_mix, b_in=d_b_in, attn_sinks=jnp.sum(dsink.reshape(Q_HEADS, ATTN_BLOCK), axis=1).reshape(1, Q_HEADS),
                hgrn_lb_logits=d_lb_logits,
                hgrn_norm_g=d_hgrn_norm, norm_ffn_g=d_norm_ffn, norm_final_g=d_norm_final)
    return loss, dx, vecs


def _place():
    return lax.axis_index("x"), lax.axis_index("y"), lax.axis_index("c")


def _other_chips(x, y):
    return [(1 - x, y), (x, 1 - y), (1 - x, 1 - y)]


def _y_first(copies):
    return [copies[3 * (i // 3) + (1, 0, 2)[i % 3]] for i in range(len(copies))]


def _gather_copies(shards):
    n = len(shards)

    def build(ins, outs, send_sems, recv_sems, local_sems):
        x, y, c = _place()
        mine = 2 * x + y
        local = [pltpu.make_async_copy(ins[w], outs[w].at[mine], local_sems.at[w]) for w in range(n)]
        sends, recvs = [], []
        for w in range(n):
            for k, (px, py) in enumerate(_other_chips(x, y)):
                sem = 3 * w + k
                sends.append(pltpu.make_async_remote_copy(
                    src_ref=ins[w], dst_ref=outs[w].at[mine], send_sem=send_sems.at[sem], recv_sem=recv_sems.at[sem],
                    device_id=(px, py, c), device_id_type=MESH_ID))
                recvs.append(pltpu.make_async_remote_copy(
                    src_ref=ins[w], dst_ref=outs[w].at[2 * px + py], send_sem=send_sems.at[sem],
                    recv_sem=recv_sems.at[sem], device_id=(px, py, c), device_id_type=MESH_ID))
        return sends, recvs, local, _y_first(sends)

    return _Carried(shards, [jax.ShapeDtypeStruct((N_CHIPS,) + s.shape, s.dtype) for s in shards], 3 * n, n, build)


def _grad_copies(stacked):
    n = len(stacked)

    def build(ins, outs, send_sems, recv_sems, local_sems):
        x, y, c = _place()
        sends = []
        for w in range(n):
            for k, (px, py) in enumerate(_other_chips(x, y)):
                sem = 3 * w + k
                sends.append(pltpu.make_async_remote_copy(
                    src_ref=ins[w].at[2 * px + py], dst_ref=outs[w].at[k], send_sem=send_sems.at[sem],
                    recv_sem=recv_sems.at[sem], device_id=(px, py, c), device_id_type=MESH_ID))
        return sends, sends, [], _y_first(sends)

    return _Carried(stacked, [jax.ShapeDtypeStruct((3,) + s.shape[1:], s.dtype) for s in stacked], 3 * n, 0, build)


def _small_copies(small):
    def build(ins, outs, send_sems, recv_sems, local_sems):
        small_ref, all_ref = ins[0], outs[0]
        x, y, c = _place()
        me = 4 * x + 2 * y + c
        sends, recvs = [], []
        for r in range(1, 8):
            px = 1 - x if r & 4 else x
            py = 1 - y if r & 2 else y
            pc = 1 - c if r & 1 else c
            sends.append(pltpu.make_async_remote_copy(
                src_ref=small_ref, dst_ref=all_ref.at[me], send_sem=send_sems.at[r - 1], recv_sem=recv_sems.at[r - 1],
                device_id=(px, py, pc), device_id_type=MESH_ID))
            recvs.append(pltpu.make_async_remote_copy(
                src_ref=small_ref, dst_ref=all_ref.at[4 * px + 2 * py + pc], send_sem=send_sems.at[r - 1],
                recv_sem=recv_sems.at[r - 1], device_id=(px, py, pc), device_id_type=MESH_ID))
        return sends, recvs, [pltpu.make_async_copy(small_ref, all_ref.at[me], local_sems.at[0])]

    return _Carried([small], [jax.ShapeDtypeStruct((8,) + small.shape, small.dtype)], 7, 1, build)


def _gather_by_neighbours(name, shard):
    half = shard.shape[0] // 2
    quarter = half // 2

    def body(in_ref, out_ref, send_sems, recv_sems, local_sem):
        for core in (0, 1):
            @pl.when(lax.axis_index("c") == core)
            def _():
                program(core, in_ref, out_ref, send_sems, recv_sems, local_sem)

    def program(c, in_ref, out_ref, send_sems, recv_sems, local_sem):
        x, y, _ = _place()
        chip = lambda px, py: 2 * px + py
        to_x, to_y, sibling = (1 - x, y, c), (x, 1 - y, c), (x, y, 1 - c)
        x_blk, y_blk, d_blk = chip(1 - x, y), chip(x, 1 - y), chip(1 - x, 1 - y)
        mine, theirs = c * half, (1 - c) * half

        def copy(sem, rows, block, to, src=None):
            place = out_ref.at[block, pl.ds(rows[0], rows[1])]
            return pltpu.make_async_remote_copy(
                src_ref=place if src is None else src, dst_ref=place, send_sem=send_sems.at[sem],
                recv_sem=recv_sems.at[sem], device_id=to, device_id_type=MESH_ID)

        own = pltpu.make_async_copy(in_ref, out_ref.at[chip(x, y)], local_sem)
        own.start()
        my_rows = in_ref.at[pl.ds(mine, half)]
        along_x = dict(send=copy(0, (mine, half), chip(x, y), to_x, src=my_rows),
                       landed=copy(0, (mine, half), x_blk, to_x),
                       onward=[copy(3, (mine + quarter, quarter), x_blk, to_y), copy(4, (mine, half), x_blk, sibling)],
                       diagonal=copy(2, (mine, quarter), d_blk, to_x))
        along_y = dict(send=copy(1, (mine, half), chip(x, y), to_y, src=my_rows),
                       landed=copy(1, (mine, half), y_blk, to_y),
                       onward=[copy(2, (mine, quarter), y_blk, to_x), copy(5, (mine, half), y_blk, sibling)],
                       diagonal=copy(3, (mine + quarter, quarter), d_blk, to_y))
        last = copy(6, (mine, half), d_blk, sibling)

        order = (along_x, along_y) if c == 0 else (along_y, along_x)
        for axis in order:
            axis["send"].start()
        for axis in order:
            axis["landed"].wait_recv()
            for cp in axis["onward"]:
                cp.start()
        for axis in order:
            axis["diagonal"].wait_recv()
        last.start()
        for sem, block in ((4, x_blk), (5, y_blk), (6, d_blk)):
            copy(sem, (theirs, half), block, sibling).wait_recv()
        for cp in [along_x["send"], along_y["send"]] + along_x["onward"] + along_y["onward"] + [last]:
            cp.wait_send()
        own.wait()

    return pl.pallas_call(
        body, name=name, in_specs=[HBM_SPEC], out_specs=HBM_SPEC,
        out_shape=jax.ShapeDtypeStruct((N_CHIPS,) + shard.shape, shard.dtype),
        scratch_shapes=[pltpu.SemaphoreType.DMA((7,)), pltpu.SemaphoreType.DMA((7,)), pltpu.SemaphoreType.DMA(())],
    )(shard)


def _copies_alone(name, comm):
    return _call(name, lambda: None, grid=(), in_specs=[], out_specs=[], out_shape=[], args=[], comm=comm)[1]


class _Net:
    def __init__(self, shards):
        self.shards = shards
        self.whole, self.own, self.theirs, self.sums, self.other = {}, {}, {}, {}, {}
        x, y, _ = _place()
        self.chip = 2 * x + y

    def gather(self, names):
        return _gather_copies([self.shards[n] for n in names])

    def gathered(self, names, got):
        for n, g in zip(names, got):
            self.whole[n] = g.reshape(-1, g.shape[-1])

    def full(self, name):
        return self.whole[name]

    def exchange(self, grads, swap=()):
        stacked = []
        for n, pieces in grads.items():
            (keep, send), = pieces
            self.own[n] = keep
            stacked.append(send.reshape(N_CHIPS, keep.shape[0] // N_CHIPS, send.shape[-1]))
        return _join(_grad_copies(stacked), self.swap(swap))

    def swap(self, names):
        return _sibling_copies([self.sums[n] for n in names]) if names else None

    def presum_begin(self, name, pieces):
        keep = jnp.concatenate([p[0] for p in pieces], axis=0) if len(pieces) > 1 else pieces[0][0]
        send = jnp.concatenate([p[1] for p in pieces], axis=0) if len(pieces) > 1 else pieces[0][1]
        rows = keep.shape[0] // N_CHIPS
        self.held = keep.reshape(N_CHIPS, rows, keep.shape[-1])
        return _half_rows_copies(send.reshape(N_CHIPS, rows, send.shape[-1]))

    def presum_end(self, name, got):
        x, y, c = _place()
        to_send, self.own[name] = _pre_sum("presum_" + name, self.held, got[0], jnp.stack([c, self.chip]))
        return _grad_copies([to_send])

    def received(self, names, swap, got, carried=None):
        self.theirs.update(zip(names, got[:len(names)]))
        self.other.update(zip(swap, got[len(names):]))
        for n in names:
            (self.sums[n],), more = _partial_sum("sum_" + n, self.own[n], self.theirs[n], self.chip, comm=carried)
        return more


def _half_rows_copies(stacked):
    n, rows = stacked.shape[0], stacked.shape[1] // 2

    def build(ins, outs, send_sems, recv_sems, local_sems):
        x, y, c = _place()
        copies = [pltpu.make_async_remote_copy(
            src_ref=ins[0].at[s, pl.ds((1 - c) * rows, rows)], dst_ref=outs[0].at[s], send_sem=send_sems.at[s],
            recv_sem=recv_sems.at[s], device_id=(x, y, 1 - c), device_id_type=MESH_ID) for s in range(n)]
        return copies, copies, []

    return _Carried([stacked], [jax.ShapeDtypeStruct((n, rows, stacked.shape[2]), stacked.dtype)], n, 0, build)


def _pre_sum(name, held, theirs, core_and_chip):
    n, R, C = held.shape
    half = R // 2
    tr = _row_tile(half)
    per_half = half // tr

    def body(place_ref, h_ref, t_ref, send_ref, own_ref):
        total = h_ref[0] + t_ref[0].astype(F32)
        send_ref[0] = total.astype(send_ref.dtype)

        @pl.when(pl.program_id(1) == place_ref[1])
        def _():
            own_ref[...] = total

    return pl.pallas_call(
        body, name=name,
        grid_spec=pltpu.PrefetchScalarGridSpec(
            num_scalar_prefetch=1, grid=(per_half, n),
            in_specs=[pl.BlockSpec((1, tr, C), lambda i, s, place: (s, place[0] * per_half + i, 0)),
                      pl.BlockSpec((1, tr, C), lambda i, s, place: (s, i, 0))],
            out_specs=[pl.BlockSpec((1, tr, C), lambda i, s, place: (s, i, 0)),
                       pl.BlockSpec((tr, C), lambda i, s, place: (i, 0))]),
        out_shape=[jax.ShapeDtypeStruct((n, half, C), MXU_DTYPE), jax.ShapeDtypeStruct((half, C), F32)],
        compiler_params=_params(("arbitrary", "arbitrary")),
    )(core_and_chip, held, theirs)


def _sibling_copies(parts):
    n = len(parts)

    def build(ins, outs, send_sems, recv_sems, local_sems):
        x, y, c = _place()
        copies = [pltpu.make_async_remote_copy(
            src_ref=ins[w], dst_ref=outs[w], send_sem=send_sems.at[w], recv_sem=recv_sems.at[w],
            device_id=(x, y, 1 - c), device_id_type=MESH_ID) for w in range(n)]
        return copies, copies, []

    return _Carried(parts, [jax.ShapeDtypeStruct(p.shape, p.dtype) for p in parts], n, 0, build)


def _row_tile(rows, most=512, sublanes=16):
    return max(t for t in range(sublanes, min(most, rows // 2) + 1, sublanes) if rows % t == 0)


def _partial_sum(name, own, recv, chip, comm=None):
    _, R, C = recv.shape
    tr = _row_tile(R)

    def body(o_ref, r_ref, p_ref):
        p_ref[...] = ((o_ref[...] + r_ref[0].astype(F32)) + r_ref[1].astype(F32)) + r_ref[2].astype(F32)

    if own.shape[0] != R:
        assert comm is None and own.shape[0] == N_CHIPS * R
        total = pl.pallas_call(
            lambda chip_ref, *refs: body(*refs), name=name,
            grid_spec=pltpu.PrefetchScalarGridSpec(
                num_scalar_prefetch=1, grid=(R // tr,),
                in_specs=[pl.BlockSpec((tr, C), lambda i, chip_ref: (chip_ref[0] * (R // tr) + i, 0)),
                          pl.BlockSpec((3, tr, C), lambda i, chip_ref: (0, i, 0))],
                out_specs=pl.BlockSpec((tr, C), lambda i, chip_ref: (i, 0))),
            out_shape=jax.ShapeDtypeStruct((R, C), F32), compiler_params=_params(("parallel",)),
        )(chip.reshape(1), own, recv)
        return [total], []
    return _call(name, body, grid=(R // tr,),
                 in_specs=[pl.BlockSpec((tr, C), lambda i: (i, 0)), pl.BlockSpec((3, tr, C), lambda i: (0, i, 0))],
                 out_specs=[pl.BlockSpec((tr, C), lambda i: (i, 0))], out_shape=[jax.ShapeDtypeStruct((R, C), F32)],
                 args=[own, recv], semantics=("parallel",), comm=comm)


def _adam_vals(w, g, m, v):
    m = ADAM_B1 * m + (1.0 - ADAM_B1) * g
    v = ADAM_B2 * v + (1.0 - ADAM_B2) * (g * g)
    m_hat = m / (1.0 - ADAM_B1 ** ADAM_STEP)
    v_hat = v / (1.0 - ADAM_B2 ** ADAM_STEP)
    delta = -ADAM_LR * (m_hat / (jnp.sqrt(v_hat) + ADAM_EPS) + ADAM_WD * w)
    return delta, m, v


def _adamw(name, w, m, v, mine, other, comm=None):
    R, C = w.shape
    tr = _row_tile(R)

    def body(w_ref, m_ref, v_ref, s_ref, n_ref, g_ref, d_ref, nm_ref, nv_ref):
        g = s_ref[...] + n_ref[...]
        d, nm, nv = _adam_vals(w_ref[...], g, m_ref[...], v_ref[...])
        g_ref[...], d_ref[...], nm_ref[...], nv_ref[...] = g, d, nm, nv

    spec = pl.BlockSpec((tr, C), lambda i: (i, 0))
    return _call(name, body, grid=(R // tr,), in_specs=[spec] * 5, out_specs=[spec] * 4,
                 out_shape=[jax.ShapeDtypeStruct((R, C), F32)] * 4, args=[w, m, v, mine, other],
                 semantics=("parallel",), comm=comm)


def _adamw_by_halves(name, w, m, v, mine, other, core):
    R, C = w.shape
    tr = _row_tile(R // 2)
    per_half = R // 2 // tr

    def body(c_ref, w_ref, m_ref, v_ref, s_ref, n_ref, g_ref, d_ref, nm_ref, nv_ref):
        g = jnp.where(pl.program_id(0) // per_half == c_ref[0, 0], s_ref[...], n_ref[...])
        d, nm, nv = _adam_vals(w_ref[...], g, m_ref[...], v_ref[...])
        g_ref[...], d_ref[...], nm_ref[...], nv_ref[...] = g, d, nm, nv

    spec = pl.BlockSpec((tr, C), lambda i: (i, 0))
    part = pl.BlockSpec((tr, C), lambda i: (i % per_half, 0))
    return pl.pallas_call(
        body, name=name, grid=(R // tr,),
        in_specs=[pl.BlockSpec(memory_space=pltpu.SMEM), spec, spec, spec, part, part], out_specs=[spec] * 4,
        out_shape=[jax.ShapeDtypeStruct((R, C), F32)] * 4, compiler_params=_params(("parallel",)),
    )(core, w, m, v, mine, other)


SMALL_LAYOUT = dict(norm_mix_g=(0, 1, 1024), b_in=(1, 8, 7424), hgrn_norm_g=(9, 1, 1024), norm_ffn_g=(10, 1, 1024),
                    norm_final_g=(11, 1, 1024), hgrn_lb_logits=(12, 2, 2048), attn_sinks=(14, 1, 16))
SMALL_LOSS_ROW, SMALL_ROWS = 15, 16


def _pack_small(grads, loss):
    rows = [jnp.pad(grads[name].astype(F32).reshape(-1), (0, nrows * D_MODEL - n))
            for name, (_, nrows, n) in SMALL_LAYOUT.items()]
    rows.append(jnp.pad(loss.astype(F32).reshape(1), (0, D_MODEL - 1)))
    return jnp.concatenate(rows).reshape(SMALL_ROWS, D_MODEL)


def _adamw_small(w, m, v, g_all):
    names = list(SMALL_LAYOUT)
    n = len(names)

    def body(a_ref, *refs):
        ins, outs = refs[:3 * n], refs[3 * n:]
        g_all_rows = a_ref[0]
        for dev in range(1, 8):
            g_all_rows = g_all_rows + a_ref[dev]
        for i, name in enumerate(names):
            first, nrows, count = SMALL_LAYOUT[name]
            w_ref, m_ref, v_ref = ins[3 * i:3 * i + 3]
            if w_ref.shape[0] == nrows:
                g = g_all_rows[first:first + nrows, :w_ref.shape[1]]
            else:
                last = count - (nrows - 1) * D_MODEL
                g = jnp.concatenate([g_all_rows[r:r + 1, :] for r in range(first, first + nrows - 1)]
                                    + [g_all_rows[first + nrows - 1:first + nrows, :last]], axis=1)
            d, nm, nv = _adam_vals(w_ref[...], g, m_ref[...], v_ref[...])
            for o_ref, val in zip(outs[4 * i:4 * i + 4], (g, d, nm, nv)):
                o_ref[...] = val
        outs[4 * n][...] = g_all_rows[SMALL_LOSS_ROW:SMALL_LOSS_ROW + 1, 0:1]

    res = pl.pallas_call(
        body, name="adamw_small",
        out_shape=[jax.ShapeDtypeStruct(w[name].shape, F32) for name in names for _ in range(4)]
        + [jax.ShapeDtypeStruct((1, 1), F32)],
    )(g_all, *[t[name] for name in names for t in (w, m, v)])
    return {name: res[4 * i:4 * i + 4] for i, name in enumerate(names)}, res[4 * n]


MATRICES = ("w_in", "w_branch_attn", "w_branch_hgrn", "w_out", "w_ffn_gate", "w_ffn_up", "w_ffn_down")
COLUMN_SHARDED = ("w_in", "w_ffn_gate", "w_ffn_up")
WEIGHTS = ("norm_mix_g", "w_in", "b_in", "attn_sinks", "hgrn_lb_logits", "hgrn_norm_g", "w_branch_attn",
           "w_branch_hgrn", "w_out", "norm_ffn_g", "w_ffn_gate", "w_ffn_up", "w_ffn_down", "norm_final_g")


def kernel(x, norm_mix_g, w_in, b_in, attn_sinks, hgrn_lb_logits, hgrn_norm_g, w_branch_attn, w_branch_hgrn, w_out, norm_ffn_g, w_ffn_gate, w_ffn_up, w_ffn_down, norm_final_g, loss_target, m_norm_mix_g, m_w_in, m_b_in, m_attn_sinks, m_hgrn_lb_logits, m_hgrn_norm_g, m_w_branch_attn, m_w_branch_hgrn, m_w_out, m_norm_ffn_g, m_w_ffn_gate, m_w_ffn_up, m_w_ffn_down, m_norm_final_g, v_norm_mix_g, v_w_in, v_b_in, v_attn_sinks, v_hgrn_lb_logits, v_hgrn_norm_g, v_w_branch_attn, v_w_branch_hgrn, v_w_out, v_norm_ffn_g, v_w_ffn_gate, v_w_ffn_up, v_w_ffn_down, v_norm_final_g):
    given = dict(locals())
    w = {n: given[n] for n in WEIGHTS}
    m = {n: given["m_" + n] for n in WEIGHTS}
    v = {n: given["v_" + n] for n in WEIGHTS}

    block = lambda a, n: jnp.transpose(a[0]) if n in COLUMN_SHARDED else a[0]
    unblock = lambda a, n: (jnp.transpose(a) if n in COLUMN_SHARDED else a)[None]
    net = _Net({n: block(w[n], n).astype(MXU_DTYPE) for n in MATRICES})
    net.gathered(("w_in",), [_gather_by_neighbours("gather_w_in", net.shards["w_in"])])
    vec = dict(norm_mix_g=norm_mix_g, b_in=b_in, attn_sinks=attn_sinks, hgrn_lb_logits=hgrn_lb_logits,
               hgrn_norm_g=hgrn_norm_g, norm_ffn_g=norm_ffn_g, norm_final_g=norm_final_g.reshape(1, D_MODEL))
    loss_part, dx, d_vecs = _local_step(x[0], loss_target[0], vec, net)

    small_all, = net.received(*net.last, carried=_small_copies(_pack_small(d_vecs, loss_part)))
    grads, deltas, new_m, new_v = {}, {}, {}, {}
    for n in ("w_ffn_down", "w_ffn_gate", "w_ffn_up", "w_out", "w_branch_attn", "w_branch_hgrn"):
        res, got = _adamw("adamw_" + n, block(w[n], n), block(m[n], n), block(v[n], n), net.sums[n], net.other[n],
                          comm=net.swap(("w_in",)) if n == "w_ffn_down" else None)
        if n == "w_ffn_down":
            net.other["w_in"], = got
        grads[n], deltas[n], new_m[n], new_v[n] = (unblock(r, n) for r in res)
    n = "w_in"
    res = _adamw_by_halves("adamw_" + n, block(w[n], n), block(m[n], n), block(v[n], n), net.sums[n], net.other[n],
                           _place()[2].reshape(1, 1))
    grads[n], deltas[n], new_m[n], new_v[n] = (unblock(r, n) for r in res)
    rows = lambda t: {n: t[n].reshape(-1, t[n].shape[-1]) for n in SMALL_LAYOUT}
    res, loss = _adamw_small(rows(w), rows(m), rows(v), small_all)
    for n, four in res.items():
        grads[n], deltas[n], new_m[n], new_v[n] = (r.reshape(w[n].shape) for r in four)
    loss = loss.reshape(())
    return (loss, dx[None], *[grads[n] for n in WEIGHTS], *[deltas[n] for n in WEIGHTS],
            *[new_m[n] for n in WEIGHTS], *[new_v[n] for n in WEIGHTS])
```

```python
import collections
import functools
import math

import jax
import jax.numpy as jnp
from jax import lax
from jax.experimental import pallas as pl
from jax.experimental.pallas import tpu as pltpu

F32 = jnp.float32
BF16 = jnp.bfloat16
MXU_DTYPE = jnp.bfloat16
SAVED_DTYPE = jnp.bfloat16
MESH_ID = pl.DeviceIdType.MESH

D_MODEL = 1024
HEAD_DIM = 64
Q_HEADS = 16
KV_HEADS = 2
GROUP = Q_HEADS // KV_HEADS
KV_WIDTH = KV_HEADS * HEAD_DIM
ATTN_BLOCK = 128
HGRN_HEADS = 8
HGRN_K = 128
CHUNK = 64
HGRN_TOKENS = 256
FFN = 2816
IN_SPLITS = (1024, 256, 4096, 2048)
EPS = 1e-6
NEG_INF = -1e30
ADAM_LR, ADAM_B1, ADAM_B2, ADAM_EPS, ADAM_WD, ADAM_STEP = 0.001, 0.9, 0.999, 1e-08, 0.01, 10
N_CHIPS = 4
VMEM_LIMIT = 60 * 1024 * 1024
ROW_ALIGN = 16
DW_ROWS = 256


def _params(sem=None):
    return pltpu.CompilerParams(dimension_semantics=sem, vmem_limit_bytes=VMEM_LIMIT)


def _sigmoid(v):
    return 0.5 * jnp.tanh(0.5 * v) + 0.5


def _dot(a, b, dims):
    return lax.dot_general(a.astype(MXU_DTYPE), b.astype(MXU_DTYPE), (dims, ((), ())),
                           preferred_element_type=F32)


def _nn(a, b):
    return _dot(a, b, ((1,), (0,)))


def _nt(a, b):
    return _dot(a, b, ((1,), (1,)))


def _tn(a, b):
    return _dot(a, b, ((0,), (0,)))


HBM_SPEC = pl.BlockSpec(memory_space=pl.ANY)


class _Carried:
    def __init__(self, arrays, out_shapes, n_remote, n_local, build):
        self.parts = [(len(arrays), len(out_shapes), build)]
        self.arrays, self.out_shapes = list(arrays), list(out_shapes)
        self.scratch = [pltpu.SemaphoreType.DMA((n_remote,)), pltpu.SemaphoreType.DMA((n_remote,)),
                        pltpu.SemaphoreType.DMA((max(n_local, 1),))]

    def __add__(self, other):
        both = _Carried([], [], 1, 0, None)
        both.parts = self.parts + other.parts
        both.arrays, both.out_shapes = self.arrays + other.arrays, self.out_shapes + other.out_shapes
        both.scratch = self.scratch + other.scratch
        return both

    def _built(self, ins, outs, sems):
        for p, (ni, no, build) in enumerate(self.parts):
            yield build(ins[:ni], outs[:no], *sems[3 * p:3 * p + 3])
            ins, outs = ins[ni:], outs[no:]

    def start(self, ins, outs, sems):
        core = lax.axis_index("c")
        for sends, _, local, *other_order in self._built(ins, outs, sems):
            for cp in local:
                cp.start()
            if not other_order:
                for cp in sends:
                    cp.start()
                continue

            @pl.when(core == 0)
            def _():
                for cp in sends:
                    cp.start()

            @pl.when(core == 1)
            def _():
                for cp in other_order[0]:
                    cp.start()

    def wait(self, ins, outs, sems):
        for sends, recvs, local, *_ in self._built(ins, outs, sems):
            for cp in recvs:
                cp.wait_recv()
            for cp in sends:
                cp.wait_send()
            for cp in local:
                cp.wait()


def _join(*comms):
    comms = [c for c in comms if c is not None]
    return functools.reduce(lambda a, b: a + b, comms) if comms else None


def _call(name, body, *, grid, in_specs, out_specs, out_shape, args, scratch=(), semantics=None, comm=None,
          aliases=None):
    n_in, n_out, n_scr = len(in_specs), len(out_specs), len(scratch)
    aliases = aliases or {}
    if comm is None:
        res = pl.pallas_call(body, name=name, grid=grid, in_specs=in_specs, out_specs=out_specs, out_shape=out_shape,
                             scratch_shapes=list(scratch), input_output_aliases=aliases,
                             compiler_params=_params(semantics))(*args)
        return list(res), []
    ci, co = len(comm.arrays), len(comm.out_shapes)

    def carrying(*refs):
        ins, refs = refs[:n_in], refs[n_in:]
        c_ins, refs = refs[:ci], refs[ci:]
        outs, refs = refs[:n_out], refs[n_out:]
        c_outs, refs = refs[:co], refs[co:]
        scr, sems = refs[:n_scr], refs[n_scr:]
        if not grid:
            comm.start(c_ins, c_outs, sems)
            body(*ins, *outs, *scr)
            comm.wait(c_ins, c_outs, sems)
            return
        first = functools.reduce(jnp.logical_and, [pl.program_id(a) == 0 for a in range(len(grid))])
        last = functools.reduce(jnp.logical_and, [pl.program_id(a) == g - 1 for a, g in enumerate(grid)])

        @pl.when(first)
        def _():
            comm.start(c_ins, c_outs, sems)

        body(*ins, *outs, *scr)

        @pl.when(last)
        def _():
            comm.wait(c_ins, c_outs, sems)

    res = pl.pallas_call(
        carrying, name=name, grid=grid, in_specs=list(in_specs) + [HBM_SPEC] * ci,
        out_specs=list(out_specs) + [HBM_SPEC] * co, out_shape=list(out_shape) + comm.out_shapes,
        scratch_shapes=list(scratch) + comm.scratch, input_output_aliases=aliases,
        compiler_params=_params(("arbitrary",) * len(grid) if grid else None),
    )(*args, *comm.arrays)
    return list(res[:n_out]), list(res[n_out:])


_Cols = collections.namedtuple("_Cols", "array first cols")
_Into = collections.namedtuple("_Into", "rows first held")


def _weight_grad(name, a, b, *, tm, tk, a_colsum=False, into=None, carrying=False, comm=None):
    cols = a if isinstance(a, _Cols) else _Cols(a, 0, a.shape[1])
    a = cols.array
    (T, N), M = b.shape, cols.cols
    tk = min(tk, T)
    assert cols.first % tm == 0 and M % tm == 0 and T % tk == 0, (name, cols.first, M, tm, T, tk)
    ni, nk, tile0 = M // tm, T // tk, cols.first // tm
    held = list(into.held) if into is not None and into.held is not None else []

    def body(a_ref, b_ref, *rest):
        keep_ref, send_ref = rest[len(held):len(held) + 2]
        sums_ref = rest[len(held) + 2] if a_colsum else None
        if nk == 1:
            acc = _tn(a_ref[...], b_ref[...])
            keep_ref[...], send_ref[...] = acc, acc.astype(send_ref.dtype)
            if a_colsum:
                sums_ref[...] = jnp.sum(a_ref[...].astype(F32), axis=0, keepdims=True)
            return
        acc_ref = rest[-1]
        k = pl.program_id(1)

        @pl.when(k == 0)
        def _():
            acc_ref[...] = jnp.zeros_like(acc_ref)
            if a_colsum:
                sums_ref[...] = jnp.zeros((1, tm), F32)

        if a_colsum:
            sums_ref[...] += jnp.sum(a_ref[...].astype(F32), axis=0, keepdims=True)
        acc_ref[...] += _tn(a_ref[...], b_ref[...])

        @pl.when(k == nk - 1)
        def _():
            keep_ref[...], send_ref[...] = acc_ref[...], acc_ref[...].astype(send_ref.dtype)

    if into is None:
        rows, out_spec = M, pl.BlockSpec((tm, N), lambda i, k: (i, 0))
    else:
        rows = into.rows
        out_spec = pl.BlockSpec((pl.Element(tm), pl.Element(N)),
                                lambda i, k: (pl.multiple_of(into.first + i * tm, ROW_ALIGN), 0))
    out_shape = [jax.ShapeDtypeStruct((rows, N), F32), jax.ShapeDtypeStruct((rows, N), MXU_DTYPE)]
    out_specs = [out_spec, out_spec]
    if a_colsum:
        out_shape.append(jax.ShapeDtypeStruct((1, M), F32))
        out_specs.append(pl.BlockSpec((1, tm), lambda i, k: (0, i)))
    res, got = _call(
        name, body, grid=(ni, nk),
        in_specs=[pl.BlockSpec((tk, tm), lambda i, k: (k, tile0 + i)), pl.BlockSpec((tk, N), lambda i, k: (k, 0))]
        + [HBM_SPEC] * len(held),
        out_specs=out_specs, out_shape=out_shape, scratch=[pltpu.VMEM((tm, N), F32)] if nk > 1 else [],
        args=[a, b] + held, aliases={2 + p: p for p in range(len(held))}, semantics=("parallel", "arbitrary"),
        comm=comm)
    return (res, got) if carrying else res


def _ffn_fwd(merged, w_out, x, gain, w_gate_t, w_up_t, *, tm, comm=None):
    (T, D), F = x.shape, w_gate_t.shape[0]

    def body(m_ref, wo_ref, x_ref, g_ref, wg_ref, wu_ref, h_ref, u_ref, gate_ref, up_ref, z_ref):
        h = x_ref[...] + _nn(m_ref[...], wo_ref[...])
        h_ref[...] = h
        u = (h * lax.rsqrt(jnp.mean(h * h, axis=-1, keepdims=True) + EPS) * g_ref[...]).astype(u_ref.dtype)
        u_ref[...] = u
        gate, up = _nt(u, wg_ref[...]), _nt(u, wu_ref[...])
        gate_ref[...], up_ref[...] = gate.astype(gate_ref.dtype), up.astype(up_ref.dtype)
        z_ref[...] = (gate * _sigmoid(gate) * up).astype(z_ref.dtype)

    rows = lambda n: pl.BlockSpec((tm, n), lambda i: (i, 0))
    fixed = _fixed_spec
    return _call("ffn_hidden", body, grid=(T // tm,),
                 in_specs=[rows(D), fixed(w_out), rows(D), fixed(gain), fixed(w_gate_t), fixed(w_up_t)],
                 out_specs=[rows(D), rows(D), rows(F), rows(F), rows(F)],
                 out_shape=[jax.ShapeDtypeStruct((T, D), F32), jax.ShapeDtypeStruct((T, D), MXU_DTYPE)]
                 + [jax.ShapeDtypeStruct((T, F), SAVED_DTYPE)] * 2 + [jax.ShapeDtypeStruct((T, F), MXU_DTYPE)],
                 args=[merged, w_out, x, gain, w_gate_t, w_up_t], semantics=("parallel",), comm=comm)


def _in_proj(x, gain, w_in_t, b_in, *, tm, comm=None):
    T, D = x.shape
    bounds = [sum(IN_SPLITS[:i]) for i in range(len(IN_SPLITS) + 1)]

    def body(x_ref, g_ref, w_ref, b_ref, u_ref, *piece_refs):
        xv = x_ref[...]
        r = lax.rsqrt(jnp.mean(xv * xv, axis=-1, keepdims=True) + EPS)
        u = (xv * r * g_ref[...]).astype(u_ref.dtype)
        u_ref[...] = u
        for o_ref, lo, hi in zip(piece_refs, bounds[:-1], bounds[1:]):
            o_ref[...] = (_nt(u, w_ref[lo:hi, :]) + b_ref[:, lo:hi]).astype(o_ref.dtype)

    rows = lambda n: pl.BlockSpec((tm, n), lambda i: (i, 0))
    fixed = _fixed_spec
    dtypes = (MXU_DTYPE, MXU_DTYPE, F32, F32)
    return _call("in_proj", body, grid=(T // tm,),
                 in_specs=[rows(D), fixed(gain), fixed(w_in_t), fixed(b_in)],
                 out_specs=[rows(D)] + [rows(n) for n in IN_SPLITS],
                 out_shape=[jax.ShapeDtypeStruct((T, D), MXU_DTYPE)]
                 + [jax.ShapeDtypeStruct((T, n), dt) for n, dt in zip(IN_SPLITS, dtypes)],
                 args=[x, gain, w_in_t, b_in], semantics=("parallel",), comm=comm)


def _row_spec(tm, n):
    return pl.BlockSpec((tm, n), lambda i: (i, 0))


def _fixed_spec(a):
    return pl.BlockSpec(a.shape, lambda i: (0,) * a.ndim, pipeline_mode=pl.Buffered(1))


def _partials_spec(n):
    return pl.BlockSpec((8, n), lambda i: (i, 0))


def _ffn_tail(z, w_down, h1, target, gain, gate, up, *, tm):
    (T, F), D = z.shape, h1.shape[1]

    def body(z_ref, w_ref, h_ref, t_ref, g_ref, gate_ref, up_ref, dh_ref, dhb_ref, dgate_ref, dup_ref, dg_ref, l_ref):
        h2 = h_ref[...] + _nn(z_ref[...], w_ref[...])
        r = lax.rsqrt(jnp.mean(h2 * h2, axis=-1, keepdims=True) + EPS)
        xhat = h2 * r
        err = xhat * g_ref[...] - t_ref[...]
        part = 0.5 * jnp.sum(jnp.sum(err * err, axis=-1, keepdims=True), axis=0, keepdims=True) / D
        dy = err / D
        dxh = dy * g_ref[...]
        dh2 = r * (dxh - xhat * jnp.mean(dxh * xhat, axis=-1, keepdims=True))
        dh_ref[...] = dh2
        dhb = dh2.astype(dhb_ref.dtype)
        dhb_ref[...] = dhb
        dg_ref[...] = jnp.broadcast_to(jnp.sum(dy * xhat, axis=0, keepdims=True), dg_ref.shape)
        l_ref[...] = jnp.broadcast_to(part, l_ref.shape)
        dz = _nt(dhb, w_ref[...])
        gv, upv = gate_ref[...].astype(F32), up_ref[...].astype(F32)
        s = _sigmoid(gv)
        dgate_ref[...] = (dz * upv * (s * (1.0 + gv * (1.0 - s)))).astype(dgate_ref.dtype)
        dup_ref[...] = (dz * (gv * s)).astype(dup_ref.dtype)

    low = lambda n: jax.ShapeDtypeStruct((T, n), MXU_DTYPE)
    part = jax.ShapeDtypeStruct((8 * (T // tm), D), F32)
    return pl.pallas_call(
        body, name="ffn_tail", grid=(T // tm,),
        in_specs=[_row_spec(tm, F), _fixed_spec(w_down), _row_spec(tm, D), _row_spec(tm, D), _fixed_spec(gain),
                  _row_spec(tm, F), _row_spec(tm, F)],
        out_specs=[_row_spec(tm, D), _row_spec(tm, D), _row_spec(tm, F), _row_spec(tm, F), _partials_spec(D),
                   _partials_spec(D)],
        out_shape=[jax.ShapeDtypeStruct((T, D), F32), low(D), low(F), low(F), part, part],
        compiler_params=_params(("parallel",)),
    )(z, w_down, h1, target, gain, gate, up)


def _ffn_in_bwd(dgate, dup, w_gate_t, w_up_t, h1, gain, dres, *, tm, comm=None):
    (T, F), D = dgate.shape, h1.shape[1]

    def body(dg_ref, du_ref, wg_ref, wu_ref, h_ref, g_ref, r_ref, dh_ref, dhb_ref, dgain_ref):
        d_u2 = _nn(dg_ref[...], wg_ref[...]) + _nn(du_ref[...], wu_ref[...])
        dx, dgain = _rmsnorm_bwd_vals(d_u2, h_ref[...], g_ref[...])
        dh = r_ref[...] + dx
        dh_ref[...] = dh
        dhb_ref[...] = dh.astype(dhb_ref.dtype)
        dgain_ref[...] = jnp.broadcast_to(dgain, dgain_ref.shape)

    return _call("d_ffn_in", body, grid=(T // tm,),
                 in_specs=[_row_spec(tm, F), _row_spec(tm, F), _fixed_spec(w_gate_t), _fixed_spec(w_up_t),
                           _row_spec(tm, D), _fixed_spec(gain), _row_spec(tm, D)],
                 out_specs=[_row_spec(tm, D), _row_spec(tm, D), _partials_spec(D)],
                 out_shape=[jax.ShapeDtypeStruct((T, D), F32), jax.ShapeDtypeStruct((T, D), MXU_DTYPE),
                            jax.ShapeDtypeStruct((8 * (T // tm), D), F32)],
                 args=[dgate, dup, w_gate_t, w_up_t, h1, gain, dres], semantics=("parallel",), comm=comm)


def _in_proj_bwd(pieces, w_in_t, x, gain, dres, *, tm, comm=None):
    T, D = x.shape
    n = len(pieces)

    def body(*refs):
        dps, (w_ref, x_ref, g_ref, r_ref, dx_ref, dgain_ref) = refs[:n], refs[n:]
        d_u = None
        for dp_ref, (dp, first) in zip(dps, pieces):
            term = _nn(dp_ref[...], w_ref[first:first + dp.shape[1], :])
            d_u = term if d_u is None else d_u + term
        dx, dgain = _rmsnorm_bwd_vals(d_u, x_ref[...], g_ref[...])
        dx_ref[...] = r_ref[...] + dx
        dgain_ref[...] = jnp.broadcast_to(dgain, dgain_ref.shape)

    return _call("d_u", body, grid=(T // tm,),
                 in_specs=[_row_spec(tm, dp.shape[1]) for dp, _ in pieces]
                 + [_fixed_spec(w_in_t), _row_spec(tm, D), _fixed_spec(gain), _row_spec(tm, D)],
                 out_specs=[_row_spec(tm, D), _partials_spec(D)],
                 out_shape=[jax.ShapeDtypeStruct((T, D), F32), jax.ShapeDtypeStruct((8 * (T // tm), D), F32)],
                 args=[dp for dp, _ in pieces] + [w_in_t, x, gain, dres], semantics=("parallel",), comm=comm)


def _merge_fwd(y_a, y_b, w_a, w_b, gates, *, tm, comm=None):
    T, D = y_a.shape

    def body(ya_ref, yb_ref, wa_ref, wb_ref, ga_ref, gb_ref, pa_ref, pb_ref, m_ref):
        pa, pb = _nn(ya_ref[...], wa_ref[...]), _nn(yb_ref[...], wb_ref[...])
        pa_ref[...], pb_ref[...] = pa.astype(pa_ref.dtype), pb.astype(pb_ref.dtype)
        m_ref[...] = (_sigmoid(ga_ref[...]) * pa + _sigmoid(gb_ref[...]) * pb).astype(m_ref.dtype)

    rows = pl.BlockSpec((tm, D), lambda i: (i, 0))
    whole = pl.BlockSpec((D, D), lambda i: (0, 0), pipeline_mode=pl.Buffered(1))
    return _call("branch_merge", body, grid=(T // tm,),
                 in_specs=[rows, rows, whole, whole, rows, pl.BlockSpec((tm, D), lambda i: (i, 1))],
                 out_specs=[rows] * 3,
                 out_shape=[jax.ShapeDtypeStruct((T, D), SAVED_DTYPE)] * 2 + [jax.ShapeDtypeStruct((T, D), MXU_DTYPE)],
                 args=[y_a, y_b, w_a, w_b, gates, gates], semantics=("parallel",), comm=comm)


def _merge_bwd(dh, w_out, w_a, w_b, p_a, p_b, gates, *, tm, d_in_width, first):
    T, D = dh.shape

    def body(dh_ref, wo_ref, wa_ref, wb_ref, pa_ref, pb_ref, ga_ref, gb_ref, dpa_ref, dpb_ref, dya_ref, dyb_ref,
             din_ref):
        dm = _nt(dh_ref[...], wo_ref[...])
        sa, sb = _sigmoid(ga_ref[...]), _sigmoid(gb_ref[...])
        dpa, dpb = (dm * sa).astype(dpa_ref.dtype), (dm * sb).astype(dpb_ref.dtype)
        dpa_ref[...], dpb_ref[...] = dpa, dpb
        din_ref[:, :D] = (dm * pa_ref[...].astype(F32) * sa * (1.0 - sa)).astype(din_ref.dtype)
        din_ref[:, D:] = (dm * pb_ref[...].astype(F32) * sb * (1.0 - sb)).astype(din_ref.dtype)
        dya_ref[...] = _nt(dpa, wa_ref[...]).astype(dya_ref.dtype)
        dyb_ref[...] = _nt(dpb, wb_ref[...]).astype(dyb_ref.dtype)

    rows = pl.BlockSpec((tm, D), lambda i: (i, 0))
    whole = pl.BlockSpec((D, D), lambda i: (0, 0), pipeline_mode=pl.Buffered(1))
    low = jax.ShapeDtypeStruct((T, D), MXU_DTYPE)
    return pl.pallas_call(
        body, name="d_branch_merge", grid=(T // tm,),
        in_specs=[rows, whole, whole, whole, rows, rows, rows, pl.BlockSpec((tm, D), lambda i: (i, 1))],
        out_specs=[rows] * 4 + [pl.BlockSpec((pl.Element(tm), pl.Element(2 * D)), lambda i: (
            pl.multiple_of(i * tm, ROW_ALIGN), first))],
        out_shape=[low] * 3 + [jax.ShapeDtypeStruct((T, D), F32), jax.ShapeDtypeStruct((T, d_in_width), MXU_DTYPE)],
        compiler_params=_params(("parallel",)),
    )(dh, w_out, w_a, w_b, p_a, p_b, gates, gates)


def _colsum_partials(p):
    return jnp.sum(p.reshape(-1, 8, p.shape[-1])[:, 0, :], axis=0, keepdims=True)


def _rmsnorm_bwd_vals(dy, xin, g):
    rstd = lax.rsqrt(jnp.mean(xin * xin, axis=-1, keepdims=True) + EPS)
    xhat = xin * rstd
    dg = jnp.sum(dy * xhat, axis=0, keepdims=True)
    dxh = dy * g
    dx = rstd * (dxh - xhat * jnp.mean(dxh * xhat, axis=-1, keepdims=True))
    return dx, dg


ATTN_SCALE = 1.0 / math.sqrt(HEAD_DIM)
GROUP_LANES = GROUP * ATTN_BLOCK
PAIR = 2 * HEAD_DIM


def _attn_mask():
    kj = lax.broadcasted_iota(jnp.int32, (ATTN_BLOCK, GROUP_LANES), 0)
    qi = lax.broadcasted_iota(jnp.int32, (ATTN_BLOCK, GROUP_LANES), 1) & (ATTN_BLOCK - 1)
    return kj <= qi


def _heads_transposed(ref, g, scale=None):
    parts = []
    for a in range(GROUP // 2):
        lo = (g * GROUP // 2 + a) * PAIR
        pair = ref[:, lo:lo + PAIR].astype(F32)
        pair = (pair if scale is None else pair * scale).T
        parts += [pair[:HEAD_DIM], pair[HEAD_DIM:]]
    return jnp.concatenate(parts, axis=1).astype(MXU_DTYPE)


def _heads_back(ref, g, vt):
    for a in range(GROUP // 2):
        lo = (g * GROUP // 2 + a) * PAIR
        pair = jnp.concatenate([vt[:, (2 * a) * ATTN_BLOCK:(2 * a + 1) * ATTN_BLOCK],
                                vt[:, (2 * a + 1) * ATTN_BLOCK:(2 * a + 2) * ATTN_BLOCK]], axis=0)
        ref[:, lo:lo + PAIR] = pair.T.astype(ref.dtype)


def _kv_parts(kv_ref, g):
    ks = slice(g * HEAD_DIM, (g + 1) * HEAD_DIM)
    vs = slice(KV_WIDTH + g * HEAD_DIM, KV_WIDTH + (g + 1) * HEAD_DIM)
    return kv_ref[:, ks].astype(MXU_DTYPE), kv_ref[:, vs].astype(MXU_DTYPE)


def _sink_rows(sinks):
    return jnp.repeat(sinks.reshape(KV_HEADS, GROUP), ATTN_BLOCK, axis=1)


def _attn_fwd(pq, pkv, sinks, comm=None):
    T = pq.shape[0]
    nb = T // ATTN_BLOCK

    def body(q_ref, kvc_ref, kvp_ref, s_ref, y_ref, lse_ref):
        mask_c = _attn_mask()
        has_prev = pl.program_id(0) > 0
        for g in range(KV_HEADS):
            (kc, vc), (kp, vp) = _kv_parts(kvc_ref, g), _kv_parts(kvp_ref, g)
            qt = _heads_transposed(q_ref, g, ATTN_SCALE)
            s = jnp.where(mask_c, _nn(kc, qt), jnp.where(has_prev, _nn(kp, qt), NEG_INF))
            sink = s_ref[g:g + 1, :]
            m = jnp.maximum(jnp.max(s, axis=0, keepdims=True), sink)
            p = jnp.exp(s - m)
            den = jnp.sum(p, axis=0, keepdims=True) + jnp.exp(sink - m)
            pc = jnp.where(mask_c, p, 0.0)
            _heads_back(y_ref, g, (_tn(vc, pc) + _tn(vp, p - pc)) / den)
            lse = m + jnp.log(den)
            for i in range(GROUP):
                lse_ref[g * GROUP + i:g * GROUP + i + 1, :] = lse[:, i * ATTN_BLOCK:(i + 1) * ATTN_BLOCK]

    return _call(
        "attn_fwd", body, grid=(nb,),
        in_specs=[pl.BlockSpec((ATTN_BLOCK, D_MODEL), lambda n: (n, 0)),
                  pl.BlockSpec((ATTN_BLOCK, 2 * KV_WIDTH), lambda n: (n, 0)),
                  pl.BlockSpec((ATTN_BLOCK, 2 * KV_WIDTH), lambda n: (jnp.maximum(n - 1, 0), 0)),
                  pl.BlockSpec((KV_HEADS, GROUP_LANES), lambda n: (0, 0))],
        out_specs=[pl.BlockSpec((ATTN_BLOCK, D_MODEL), lambda n: (n, 0)),
                   pl.BlockSpec((Q_HEADS, ATTN_BLOCK), lambda n: (0, n))],
        out_shape=[jax.ShapeDtypeStruct((T, D_MODEL), MXU_DTYPE), jax.ShapeDtypeStruct((Q_HEADS, T), F32)],
        args=[pq, pkv, pkv, _sink_rows(sinks)], semantics=("parallel",), comm=comm)


def _attn_bwd(pq, pkv, sinks, lse, dy, d_in, comm=None):
    T = pq.shape[0]
    nb = T // ATTN_BLOCK
    cur = lambda n: (jnp.minimum(n, nb - 1), 0)
    done = D_MODEL + 2 * KV_WIDTH

    def body(q_ref, kvc_ref, kvp_ref, s_ref, lse_ref, dy_ref, _, out_ref, ds_ref, carry, top, bot, dq_ref):
        n = pl.program_id(0)

        @pl.when(n == 0)
        def _():
            carry[...] = jnp.zeros_like(carry)
            dq_ref[...] = jnp.zeros_like(dq_ref)
            ds_ref[...] = jnp.zeros_like(ds_ref)

        out_ref[:, :D_MODEL] = dq_ref[...]

        @pl.when(n < nb)
        def _():
            mask_c = _attn_mask()
            valid = jnp.logical_or(mask_c, n > 0)
            for g in range(KV_HEADS):
                ks = slice(g * HEAD_DIM, (g + 1) * HEAD_DIM)
                vs = slice(KV_WIDTH + g * HEAD_DIM, KV_WIDTH + (g + 1) * HEAD_DIM)
                (kc, vc), (kp, vp) = _kv_parts(kvc_ref, g), _kv_parts(kvp_ref, g)
                qt = _heads_transposed(q_ref, g, ATTN_SCALE)
                dot = _heads_transposed(dy_ref, g)
                lse = jnp.concatenate([lse_ref[g * GROUP + i:g * GROUP + i + 1, :] for i in range(GROUP)], axis=1)
                p = jnp.where(valid, jnp.exp(jnp.where(mask_c, _nn(kc, qt), _nn(kp, qt)) - lse), 0.0)
                dp = jnp.where(mask_c, _nn(vc, dot), _nn(vp, dot))
                delta = jnp.sum(p * dp, axis=0, keepdims=True)
                ds = p * (dp - delta)
                ds_c, p_c = jnp.where(mask_c, ds, 0.0), jnp.where(mask_c, p, 0.0)
                ds_p, p_p = ds - ds_c, p - p_c
                _heads_back(dq_ref, g, (_tn(kc, ds_c) + _tn(kp, ds_p)) * ATTN_SCALE)
                bot[:, ks], bot[:, vs] = _nt(ds_c, qt), _nt(p_c, dot)
                top[:, ks], top[:, vs] = _nt(ds_p, qt), _nt(p_p, dot)
                ds_ref[g:g + 1, :] -= jnp.exp(s_ref[g:g + 1, :] - lse) * delta
            out_ref[:, D_MODEL:] = (carry[...] + top[...]).astype(out_ref.dtype)
            carry[...] = bot[...]

        @pl.when(n == nb)
        def _():
            out_ref[:, D_MODEL:] = carry[...].astype(out_ref.dtype)

    return _call(
        "attn_bwd", body, grid=(nb + 1,),
        in_specs=[pl.BlockSpec((ATTN_BLOCK, D_MODEL), cur),
                  pl.BlockSpec((ATTN_BLOCK, 2 * KV_WIDTH), cur),
                  pl.BlockSpec((ATTN_BLOCK, 2 * KV_WIDTH), lambda n: (jnp.maximum(jnp.minimum(n, nb - 1) - 1, 0), 0)),
                  pl.BlockSpec((KV_HEADS, GROUP_LANES), lambda n: (0, 0)),
                  pl.BlockSpec((Q_HEADS, ATTN_BLOCK), lambda n: (0, jnp.minimum(n, nb - 1))),
                  pl.BlockSpec((ATTN_BLOCK, D_MODEL), cur), HBM_SPEC],
        out_specs=[pl.BlockSpec((ATTN_BLOCK, done), lambda n: (jnp.maximum(n - 1, 0), 0)),
                   pl.BlockSpec((KV_HEADS, GROUP_LANES), lambda n: (0, 0))],
        out_shape=[jax.ShapeDtypeStruct(d_in.shape, d_in.dtype), jax.ShapeDtypeStruct((KV_HEADS, GROUP_LANES), F32)],
        scratch=[pltpu.VMEM((ATTN_BLOCK, 2 * KV_WIDTH), F32)] * 3 + [pltpu.VMEM((ATTN_BLOCK, D_MODEL), MXU_DTYPE)],
        args=[pq, pkv, pkv, _sink_rows(sinks), lse, dy, d_in], semantics=("arbitrary",), comm=comm, aliases={6: 0})


def _lower_bound(l):
    m = jnp.maximum(l[0:1], l[1:2])
    e0, e1 = jnp.exp(l[0:1] - m), jnp.exp(l[1:2] - m)
    return e0 / (e0 + e1)


def _tri(lower):
    r = lax.broadcasted_iota(jnp.int32, (CHUNK, CHUNK), 0)
    c = lax.broadcasted_iota(jnp.int32, (CHUNK, CHUNK), 1)
    return (r >= c) if lower else (c >= r)


def _chunk_sum(mask, v):
    ones = mask.astype(BF16)
    hi = v.astype(BF16)
    rest = v - hi.astype(F32)
    mid = rest.astype(BF16)
    lo = (rest - mid.astype(F32)).astype(BF16)
    part = lambda t: lax.dot_general(ones, t, (((1,), (0,)), ((), ())), preferred_element_type=F32)
    return part(hi) + part(mid) + part(lo)


def _hgrn_chunk_inputs(hq, hf, lb, causal):
    half_t = 0.5 * jnp.tanh(0.5 * hf)
    sg, sgn = 0.5 + half_t, 0.5 - half_t
    f = lb + (1.0 - lb) * sg
    kk = (1.0 - lb) * sgn
    sq = _sigmoid(hq)
    q = hq * sq
    b = _chunk_sum(causal, jnp.log(f))
    bm, bl = b[CHUNK // 2 - 1:CHUNK // 2, :], b[CHUNK - 1:CHUNK, :]
    e_qm, e_km = jnp.exp(b - bm), jnp.exp(bm - b)
    e_qs, e_kl = e_qm * jnp.exp(bm), e_km * jnp.exp(bl - bm)
    return dict(sg=sg, sgn=sgn, f=f, kk=kk, sq=sq, q=q, e_qm=e_qm, e_km=e_km, e_qs=e_qs, e_kl=e_kl,
                qm=q * e_qm, km=kk * e_km, qs=q * e_qs, kl=kk * e_kl, el=jnp.exp(bl))


def _hgrn_fwd(ph, lb_logits, norm_g, comm=None):
    T = ph.shape[0]
    nblk, cpb = T // HGRN_TOKENS, HGRN_TOKENS // CHUNK
    col = lambda c: pl.BlockSpec((HGRN_TOKENS, D_MODEL), functools.partial(lambda i, c: (i, c), c=c))

    def body(hq_ref, hf_ref, hi_ref, hg_ref, l_ref, ng_ref, y_ref, o_ref, st_ref, s_ref):
        @pl.when(pl.program_id(0) == 0)
        def _():
            s_ref[...] = jnp.zeros_like(s_ref)

        lb = _lower_bound(l_ref[...])
        causal = _tri(True)
        for c in range(cpb):
            rows = slice(c * CHUNK, (c + 1) * CHUNK)
            t = _hgrn_chunk_inputs(hq_ref[rows, :], hf_ref[rows, :], lb, causal)
            qm, km, qs, kl = (t[n].astype(MXU_DTYPE) for n in ("qm", "km", "qs", "kl"))
            v = hi_ref[rows, :].astype(MXU_DTYPE)
            for h in range(HGRN_HEADS):
                ls = slice(h * HGRN_K, (h + 1) * HGRN_K)
                st = s_ref[h]
                st_ref[c, ls, :] = st
                a = jnp.where(causal, _nt(qm[:, ls], km[:, ls]), 0.0)
                o_ref[rows, ls] = _nn(a, v[:, ls]) + _nt(qs[:, ls], st)
                s_ref[h] = t["el"][:, ls] * st + _tn(v[:, ls], kl[:, ls])
        for h in range(HGRN_HEADS):
            ls = slice(h * HGRN_K, (h + 1) * HGRN_K)
            o = o_ref[:, ls]
            r = lax.rsqrt(jnp.mean(o * o, axis=-1, keepdims=True) + EPS)
            y_ref[:, ls] = (o * r * ng_ref[:, ls] * _sigmoid(hg_ref[:, ls])).astype(y_ref.dtype)

    return _call(
        "hgrn_fwd", body, grid=(nblk,),
        in_specs=[col(0), col(1), col(2), col(3),
                  pl.BlockSpec((2, D_MODEL), lambda i: (0, 0)), pl.BlockSpec((1, D_MODEL), lambda i: (0, 0))],
        out_specs=[pl.BlockSpec((HGRN_TOKENS, D_MODEL), lambda i: (i, 0)),
                   pl.BlockSpec((HGRN_TOKENS, D_MODEL), lambda i: (i, 0)),
                   pl.BlockSpec((cpb, D_MODEL, HGRN_K), lambda i: (i, 0, 0))],
        out_shape=[jax.ShapeDtypeStruct((T, D_MODEL), MXU_DTYPE), jax.ShapeDtypeStruct((T, D_MODEL), F32),
                   jax.ShapeDtypeStruct((T // CHUNK, D_MODEL, HGRN_K), F32)],
        scratch=[pltpu.VMEM((HGRN_HEADS, HGRN_K, HGRN_K), F32)],
        args=[ph, ph, ph, ph, lb_logits, norm_g], semantics=("arbitrary",), comm=comm)


def _hgrn_bwd(ph, o_raw, states, dy, lb_logits, norm_g, d_in, first, comm=None):
    T = ph.shape[0]
    nblk, cpb = T // HGRN_TOKENS, HGRN_TOKENS // CHUNK
    rev = lambda i: nblk - 1 - i
    col = lambda c: pl.BlockSpec((HGRN_TOKENS, D_MODEL), functools.partial(lambda i, c: (rev(i), c), c=c))
    tok = pl.BlockSpec((HGRN_TOKENS, D_MODEL), lambda i: (rev(i), 0))

    def body(hq_ref, hf_ref, hi_ref, hg_ref, o_ref, st_ref, dy_ref, l_ref, ng_ref, _,
             dph_ref, dng_ref, dl_ref, dst_ref, dlb_ref, do_s, dqm_s, dkm_s, dqs_s, dkl_s, dv_s, del_s):
        i = pl.program_id(0)

        @pl.when(i == 0)
        def _():
            dst_ref[...] = jnp.zeros_like(dst_ref)
            dlb_ref[...] = jnp.zeros_like(dlb_ref)
            dng_ref[...] = jnp.zeros_like(dng_ref)

        lb = _lower_bound(l_ref[...])
        causal, anti = _tri(True), _tri(False)
        row = lax.broadcasted_iota(jnp.int32, (CHUNK, D_MODEL), 0)
        for c in reversed(range(cpb)):
            rows = slice(c * CHUNK, (c + 1) * CHUNK)
            hq = hq_ref[rows, :]
            t = _hgrn_chunk_inputs(hq, hf_ref[rows, :], lb, causal)
            sgg = _sigmoid(hg_ref[rows, :])
            dyv = dy_ref[rows, :]
            for h in range(HGRN_HEADS):
                ls = slice(h * HGRN_K, (h + 1) * HGRN_K)
                o = o_ref[rows, ls]
                r = lax.rsqrt(jnp.mean(o * o, axis=-1, keepdims=True) + EPS)
                nrm = o * r
                g_h = sgg[:, ls]
                dph_ref[rows, 3 * D_MODEL + h * HGRN_K:3 * D_MODEL + (h + 1) * HGRN_K] = (
                    dyv[:, ls] * nrm * ng_ref[:, ls] * g_h * (1.0 - g_h)).astype(dph_ref.dtype)
                dyg = dyv[:, ls] * g_h
                dng_ref[:, ls] += jnp.sum(dyg * nrm, axis=0, keepdims=True)
                dn = dyg * ng_ref[:, ls]
                do_s[:, ls] = r * (dn - nrm * jnp.mean(dn * nrm, axis=-1, keepdims=True))
            qm, km, qs, kl = (t[n].astype(MXU_DTYPE) for n in ("qm", "km", "qs", "kl"))
            v = hi_ref[rows, :].astype(MXU_DTYPE)
            do = do_s[...].astype(MXU_DTYPE)
            for h in range(HGRN_HEADS):
                ls = slice(h * HGRN_K, (h + 1) * HGRN_K)
                st = st_ref[c, ls, :]
                dst = dst_ref[h]
                a = jnp.where(causal, _nt(qm[:, ls], km[:, ls]), 0.0)
                da = jnp.where(causal, _nt(do[:, ls], v[:, ls]), 0.0)
                dv_s[:, ls] = _tn(a, do[:, ls]) + _nt(kl[:, ls], dst)
                dkl_s[:, ls] = _nn(v[:, ls], dst)
                dqs_s[:, ls] = _nn(do[:, ls], st)
                del_s[:, ls] = jnp.sum(dst * st, axis=0, keepdims=True)
                dst_ref[h] = _tn(do[:, ls], qs[:, ls]) + t["el"][:, ls] * dst
                dqm_s[:, ls] = _nn(da, km[:, ls])
                dkm_s[:, ls] = _tn(da, qm[:, ls])
            dqm, dkm, dqs, dkl = dqm_s[...], dkm_s[...], dqs_s[...], dkl_s[...]
            dq = dqm * t["e_qm"] + dqs * t["e_qs"]
            dk = dkm * t["e_km"] + dkl * t["e_kl"]
            t_qm, t_km, t_kl = dqm * t["qm"], dkm * t["km"], dkl * t["kl"]
            db = t_qm - t_km + dqs * t["qs"] - t_kl
            db_mid = jnp.sum(t_km - t_qm, axis=0, keepdims=True)
            db_last = jnp.sum(t_kl, axis=0, keepdims=True) + del_s[...] * t["el"]
            db = db + jnp.where(row == CHUNK // 2 - 1, db_mid, 0.0) + jnp.where(row == CHUNK - 1, db_last, 0.0)
            dlogf = _chunk_sum(anti, db)
            sq, sg, sgn, f = t["sq"], t["sg"], t["sgn"], t["f"]
            dph_ref[rows, 0:D_MODEL] = (dq * (sq * (1.0 + hq * (1.0 - sq)))).astype(dph_ref.dtype)
            dph_ref[rows, D_MODEL:2 * D_MODEL] = (
                dlogf * (1.0 - lb) * sg * (1.0 - sg) / f - dk * (1.0 - lb) * sgn * (1.0 - sgn)).astype(dph_ref.dtype)
            dph_ref[rows, 2 * D_MODEL:3 * D_MODEL] = dv_s[...].astype(dph_ref.dtype)
            dlb_ref[...] += jnp.sum(dlogf * (1.0 - sg) / f - dk * sgn, axis=0, keepdims=True)

        @pl.when(i == nblk - 1)
        def _():
            dl0 = dlb_ref[...] * lb * (1.0 - lb)
            dl_ref[0:1, :] = dl0
            dl_ref[1:2, :] = -dl0

    wide = pltpu.VMEM((CHUNK, D_MODEL), F32)
    return _call(
        "hgrn_bwd", body, grid=(nblk,),
        in_specs=[col(0), col(1), col(2), col(3), tok,
                  pl.BlockSpec((cpb, D_MODEL, HGRN_K), lambda i: (rev(i), 0, 0)), tok,
                  pl.BlockSpec((2, D_MODEL), lambda i: (0, 0)), pl.BlockSpec((1, D_MODEL), lambda i: (0, 0)), HBM_SPEC],
        out_specs=[pl.BlockSpec((pl.Element(HGRN_TOKENS), pl.Element(4 * D_MODEL)), lambda i: (
                       pl.multiple_of(rev(i) * HGRN_TOKENS, ROW_ALIGN), first)),
                   pl.BlockSpec((1, D_MODEL), lambda i: (0, 0)), pl.BlockSpec((2, D_MODEL), lambda i: (0, 0))],
        out_shape=[jax.ShapeDtypeStruct(d_in.shape, d_in.dtype), jax.ShapeDtypeStruct((1, D_MODEL), F32),
                   jax.ShapeDtypeStruct((2, D_MODEL), F32)],
        scratch=[pltpu.VMEM((HGRN_HEADS, HGRN_K, HGRN_K), F32), pltpu.VMEM((1, D_MODEL), F32),
                 wide, wide, wide, wide, wide, wide, pltpu.VMEM((1, D_MODEL), F32)],
        args=[ph, ph, ph, ph, o_raw, states, dy, lb_logits, norm_g, d_in], semantics=("arbitrary",), comm=comm,
        aliases={9: 0})


def _local_step(x, target, vec, net):
    T, D = x.shape
    norm_mix_g, b_in, sinks, lb_logits = vec["norm_mix_g"], vec["b_in"], vec["attn_sinks"], vec["hgrn_lb_logits"]
    hgrn_norm_g, norm_ffn_g, norm_final_g = vec["hgrn_norm_g"], vec["norm_ffn_g"], vec["norm_final_g"]
    w_in = net.full("w_in")
    o_q, o_kv, o_h, o_g = (sum(IN_SPLITS[:i]) for i in range(4))
    TM = 512

    names = ("w_ffn_gate",)
    (u, pq, pkv, ph, pg), got = _in_proj(x, norm_mix_g, w_in, b_in, tm=256, comm=net.gather(names))
    net.gathered(names, got)
    names = ("w_branch_attn", "w_branch_hgrn")
    (y_attn, lse), got = _attn_fwd(pq, pkv, sinks, comm=net.gather(names))
    net.gathered(names, got)
    names = ("w_ffn_up",)
    (y_hgrn, o_raw, states), got = _hgrn_fwd(ph, lb_logits, hgrn_norm_g, comm=net.gather(names))
    net.gathered(names, got)
    w_ba, w_bh = net.full("w_branch_attn"), net.full("w_branch_hgrn")
    w_gate, w_up = net.full("w_ffn_gate"), net.full("w_ffn_up")
    names = ("w_out",)
    (ya, yb, merged), got = _merge_fwd(y_attn, y_hgrn, w_ba, w_bh, pg, tm=TM, comm=net.gather(names))
    net.gathered(names, got)
    w_out = net.full("w_out")

    names = ("w_ffn_down",)
    (h1, u2, gpre, up, z), got = _ffn_fwd(merged, w_out, x, norm_ffn_g, w_gate, w_up, tm=256,
                                          comm=net.gather(names))
    net.gathered(names, got)
    w_down = net.full("w_ffn_down")

    dh2, dh2b, dgp, dup, dgf_p, loss_p = _ffn_tail(z, w_down, h1, target, norm_final_g, gpre, up, tm=256)
    loss = jnp.sum(loss_p.reshape(-1, 8, D)[:, 0, 0])
    d_norm_final = _colsum_partials(dgf_p)
    d_w_down = _weight_grad("dw_down", z, dh2b, tm=DW_ROWS, tk=T)

    (dh1, dh1b, dg2_p), _ = _ffn_in_bwd(dgp, dup, w_gate, w_up, h1, norm_ffn_g, dh2, tm=256)
    d_norm_ffn = _colsum_partials(dg2_p)
    d_w_gate = _weight_grad("dw_gate", dgp, u2, tm=DW_ROWS, tk=T)
    d_w_up = _weight_grad("dw_up", dup, u2, tm=DW_ROWS, tk=T)

    rows_in = sum(IN_SPLITS)
    dya, dyb, dy_attn, dy_hgrn, d_in = _merge_bwd(dh1b, w_out, w_ba, w_bh, ya, yb, pg, tm=TM, d_in_width=rows_in,
                                                  first=o_g)
    d_w_out = _weight_grad("dw_out", merged, dh1b, tm=1024, tk=1024)
    d_w_ba = _weight_grad("dw_branch_a", y_attn, dya, tm=1024, tk=1024)
    d_w_bh = _weight_grad("dw_branch_b", y_hgrn, dyb, tm=1024, tk=1024)

    names, swap = ("w_ffn_down",), ()
    (d_in, dsink), got = _attn_bwd(pq, pkv, sinks, lse, dy_attn, d_in,
                                   comm=net.exchange(dict(w_ffn_down=[d_w_down]), swap))
    net.received(names, swap, got)
    names, swap = ("w_ffn_gate", "w_ffn_up"), ("w_ffn_down",)
    (d_in, d_hgrn_norm, d_lb_logits), got = _hgrn_bwd(
        ph, o_raw, states, dy_hgrn, lb_logits, hgrn_norm_g, d_in, o_h,
        comm=net.exchange(dict(w_ffn_gate=[d_w_gate], w_ffn_up=[d_w_up]), swap))
    net.received(names, swap, got)

    names, swap = ("w_out", "w_branch_attn", "w_branch_hgrn"), ("w_ffn_gate", "w_ffn_up")
    (*d_w_in, d_b_in), got = _weight_grad(
        "dw_in", d_in, u, tm=DW_ROWS, tk=T, a_colsum=True, carrying=True,
        comm=net.exchange(dict(w_out=[d_w_out], w_branch_attn=[d_w_ba], w_branch_hgrn=[d_w_bh]), swap))
    net.received(names, swap, got)
    d_w_in = [tuple(d_w_in)]

    first_level = net.presum_begin("w_in", d_w_in)
    halves = net.presum_end("w_in", [] if first_level is None else _copies_alone("presum_swap_w_in", first_level))
    names, swap = ("w_in",), ("w_out", "w_branch_attn", "w_branch_hgrn")
    (dx, dg1_p), got = _in_proj_bwd([(d_in, 0)], w_in, x, norm_mix_g, dh1, tm=256,
                                    comm=_join(halves, net.swap(swap)))
    net.last = (names, swap, got)
    d_norm_mix = _colsum_partials(dg1_p)
    vecs = dict(norm_mix_g=d_norm_mix, b_in=d_b_in, attn_sinks=jnp.sum(dsink.reshape(Q_HEADS, ATTN_BLOCK), axis=1).reshape(1, Q_HEADS),
                hgrn_lb_logits=d_lb_logits,
                hgrn_norm_g=d_hgrn_norm, norm_ffn_g=d_norm_ffn, norm_final_g=d_norm_final)
    return loss, dx, vecs


def _place():
    return lax.axis_index("x"), lax.axis_index("y"), lax.axis_index("c")


def _other_chips(x, y):
    return [(1 - x, y), (x, 1 - y), (1 - x, 1 - y)]


def _y_first(copies):
    return [copies[3 * (i // 3) + (1, 0, 2)[i % 3]] for i in range(len(copies))]


def _gather_copies(shards):
    n = len(shards)

    def build(ins, outs, send_sems, recv_sems, local_sems):
        x, y, c = _place()
        mine = 2 * x + y
        local = [pltpu.make_async_copy(ins[w], outs[w].at[mine], local_sems.at[w]) for w in range(n)]
        sends, recvs = [], []
        for w in range(n):
            for k, (px, py) in enumerate(_other_chips(x, y)):
                sem = 3 * w + k
                sends.append(pltpu.make_async_remote_copy(
                    src_ref=ins[w], dst_ref=outs[w].at[mine], send_sem=send_sems.at[sem], recv_sem=recv_sems.at[sem],
                    device_id=(px, py, c), device_id_type=MESH_ID))
                recvs.append(pltpu.make_async_remote_copy(
                    src_ref=ins[w], dst_ref=outs[w].at[2 * px + py], send_sem=send_sems.at[sem],
                    recv_sem=recv_sems.at[sem], device_id=(px, py, c), device_id_type=MESH_ID))
        return sends, recvs, local, _y_first(sends)

    return _Carried(shards, [jax.ShapeDtypeStruct((N_CHIPS,) + s.shape, s.dtype) for s in shards], 3 * n, n, build)


def _grad_copies(stacked):
    n = len(stacked)

    def build(ins, outs, send_sems, recv_sems, local_sems):
        x, y, c = _place()
        sends = []
        for w in range(n):
            for k, (px, py) in enumerate(_other_chips(x, y)):
                sem = 3 * w + k
                sends.append(pltpu.make_async_remote_copy(
                    src_ref=ins[w].at[2 * px + py], dst_ref=outs[w].at[k], send_sem=send_sems.at[sem],
                    recv_sem=recv_sems.at[sem], device_id=(px, py, c), device_id_type=MESH_ID))
        return sends, sends, [], _y_first(sends)

    return _Carried(stacked, [jax.ShapeDtypeStruct((3,) + s.shape[1:], s.dtype) for s in stacked], 3 * n, 0, build)


def _small_copies(small):
    def build(ins, outs, send_sems, recv_sems, local_sems):
        small_ref, all_ref = ins[0], outs[0]
        x, y, c = _place()
        me = 4 * x + 2 * y + c
        sends, recvs = [], []
        for r in range(1, 8):
            px = 1 - x if r & 4 else x
            py = 1 - y if r & 2 else y
            pc = 1 - c if r & 1 else c
            sends.append(pltpu.make_async_remote_copy(
                src_ref=small_ref, dst_ref=all_ref.at[me], send_sem=send_sems.at[r - 1], recv_sem=recv_sems.at[r - 1],
                device_id=(px, py, pc), device_id_type=MESH_ID))
            recvs.append(pltpu.make_async_remote_copy(
                src_ref=small_ref, dst_ref=all_ref.at[4 * px + 2 * py + pc], send_sem=send_sems.at[r - 1],
                recv_sem=recv_sems.at[r - 1], device_id=(px, py, pc), device_id_type=MESH_ID))
        return sends, recvs, [pltpu.make_async_copy(small_ref, all_ref.at[me], local_sems.at[0])]

    return _Carried([small], [jax.ShapeDtypeStruct((8,) + small.shape, small.dtype)], 7, 1, build)


def _gather_by_neighbours(name, shard):
    half = shard.shape[0] // 2
    quarter = half // 2

    def body(in_ref, out_ref, send_sems, recv_sems, local_sem):
        for core in (0, 1):
            @pl.when(lax.axis_index("c") == core)
            def _():
                program(core, in_ref, out_ref, send_sems, recv_sems, local_sem)

    def program(c, in_ref, out_ref, send_sems, recv_sems, local_sem):
        x, y, _ = _place()
        chip = lambda px, py: 2 * px + py
        to_x, to_y, sibling = (1 - x, y, c), (x, 1 - y, c), (x, y, 1 - c)
        x_blk, y_blk, d_blk = chip(1 - x, y), chip(x, 1 - y), chip(1 - x, 1 - y)
        mine, theirs = c * half, (1 - c) * half

        def copy(sem, rows, block, to, src=None):
            place = out_ref.at[block, pl.ds(rows[0], rows[1])]
            return pltpu.make_async_remote_copy(
                src_ref=place if src is None else src, dst_ref=place, send_sem=send_sems.at[sem],
                recv_sem=recv_sems.at[sem], device_id=to, device_id_type=MESH_ID)

        own = pltpu.make_async_copy(in_ref, out_ref.at[chip(x, y)], local_sem)
        own.start()
        my_rows = in_ref.at[pl.ds(mine, half)]
        along_x = dict(send=copy(0, (mine, half), chip(x, y), to_x, src=my_rows),
                       landed=copy(0, (mine, half), x_blk, to_x),
                       onward=[copy(3, (mine + quarter, quarter), x_blk, to_y), copy(4, (mine, half), x_blk, sibling)],
                       diagonal=copy(2, (mine, quarter), d_blk, to_x))
        along_y = dict(send=copy(1, (mine, half), chip(x, y), to_y, src=my_rows),
                       landed=copy(1, (mine, half), y_blk, to_y),
                       onward=[copy(2, (mine, quarter), y_blk, to_x), copy(5, (mine, half), y_blk, sibling)],
                       diagonal=copy(3, (mine + quarter, quarter), d_blk, to_y))
        last = copy(6, (mine, half), d_blk, sibling)

        order = (along_x, along_y) if c == 0 else (along_y, along_x)
        for axis in order:
            axis["send"].start()
        for axis in order:
            axis["landed"].wait_recv()
            for cp in axis["onward"]:
                cp.start()
        for axis in order:
            axis["diagonal"].wait_recv()
        last.start()
        for sem, block in ((4, x_blk), (5, y_blk), (6, d_blk)):
            copy(sem, (theirs, half), block, sibling).wait_recv()
        for cp in [along_x["send"], along_y["send"]] + along_x["onward"] + along_y["onward"] + [last]:
            cp.wait_send()
        own.wait()

    return pl.pallas_call(
        body, name=name, in_specs=[HBM_SPEC], out_specs=HBM_SPEC,
        out_shape=jax.ShapeDtypeStruct((N_CHIPS,) + shard.shape, shard.dtype),
        scratch_shapes=[pltpu.SemaphoreType.DMA((7,)), pltpu.SemaphoreType.DMA((7,)), pltpu.SemaphoreType.DMA(())],
    )(shard)


def _copies_alone(name, comm):
    return _call(name, lambda: None, grid=(), in_specs=[], out_specs=[], out_shape=[], args=[], comm=comm)[1]


class _Net:
    def __init__(self, shards):
        self.shards = shards
        self.whole, self.own, self.theirs, self.sums, self.other = {}, {}, {}, {}, {}
        x, y, _ = _place()
        self.chip = 2 * x + y

    def gather(self, names):
        return _gather_copies([self.shards[n] for n in names])

    def gathered(self, names, got):
        for n, g in zip(names, got):
            self.whole[n] = g.reshape(-1, g.shape[-1])

    def full(self, name):
        return self.whole[name]

    def exchange(self, grads, swap=()):
        stacked = []
        for n, pieces in grads.items():
            (keep, send), = pieces
            self.own[n] = keep
            stacked.append(send.reshape(N_CHIPS, keep.shape[0] // N_CHIPS, send.shape[-1]))
        return _join(_grad_copies(stacked), self.swap(swap))

    def swap(self, names):
        return _sibling_copies([self.sums[n] for n in names]) if names else None

    def presum_begin(self, name, pieces):
        keep = jnp.concatenate([p[0] for p in pieces], axis=0) if len(pieces) > 1 else pieces[0][0]
        send = jnp.concatenate([p[1] for p in pieces], axis=0) if len(pieces) > 1 else pieces[0][1]
        rows = keep.shape[0] // N_CHIPS
        self.held = keep.reshape(N_CHIPS, rows, keep.shape[-1])
        return _half_rows_copies(send.reshape(N_CHIPS, rows, send.shape[-1]))

    def presum_end(self, name, got):
        x, y, c = _place()
        to_send, self.own[name] = _pre_sum("presum_" + name, self.held, got[0], jnp.stack([c, self.chip]))
        return _grad_copies([to_send])

    def received(self, names, swap, got, carried=None):
        self.theirs.update(zip(names, got[:len(names)]))
        self.other.update(zip(swap, got[len(names):]))
        for n in names:
            (self.sums[n],), more = _partial_sum("sum_" + n, self.own[n], self.theirs[n], self.chip, comm=carried)
        return more


def _half_rows_copies(stacked):
    n, rows = stacked.shape[0], stacked.shape[1] // 2

    def build(ins, outs, send_sems, recv_sems, local_sems):
        x, y, c = _place()
        copies = [pltpu.make_async_remote_copy(
            src_ref=ins[0].at[s, pl.ds((1 - c) * rows, rows)], dst_ref=outs[0].at[s], send_sem=send_sems.at[s],
            recv_sem=recv_sems.at[s], device_id=(x, y, 1 - c), device_id_type=MESH_ID) for s in range(n)]
        return copies, copies, []

    return _Carried([stacked], [jax.ShapeDtypeStruct((n, rows, stacked.shape[2]), stacked.dtype)], n, 0, build)


def _pre_sum(name, held, theirs, core_and_chip):
    n, R, C = held.shape
    half = R // 2
    tr = _row_tile(half)
    per_half = half // tr

    def body(place_ref, h_ref, t_ref, send_ref, own_ref):
        total = h_ref[0] + t_ref[0].astype(F32)
        send_ref[0] = total.astype(send_ref.dtype)

        @pl.when(pl.program_id(1) == place_ref[1])
        def _():
            own_ref[...] = total

    return pl.pallas_call(
        body, name=name,
        grid_spec=pltpu.PrefetchScalarGridSpec(
            num_scalar_prefetch=1, grid=(per_half, n),
            in_specs=[pl.BlockSpec((1, tr, C), lambda i, s, place: (s, place[0] * per_half + i, 0)),
                      pl.BlockSpec((1, tr, C), lambda i, s, place: (s, i, 0))],
            out_specs=[pl.BlockSpec((1, tr, C), lambda i, s, place: (s, i, 0)),
                       pl.BlockSpec((tr, C), lambda i, s, place: (i, 0))]),
        out_shape=[jax.ShapeDtypeStruct((n, half, C), MXU_DTYPE), jax.ShapeDtypeStruct((half, C), F32)],
        compiler_params=_params(("arbitrary", "arbitrary")),
    )(core_and_chip, held, theirs)


def _sibling_copies(parts):
    n = len(parts)

    def build(ins, outs, send_sems, recv_sems, local_sems):
        x, y, c = _place()
        copies = [pltpu.make_async_remote_copy(
            src_ref=ins[w], dst_ref=outs[w], send_sem=send_sems.at[w], recv_sem=recv_sems.at[w],
            device_id=(x, y, 1 - c), device_id_type=MESH_ID) for w in range(n)]
        return copies, copies, []

    return _Carried(parts, [jax.ShapeDtypeStruct(p.shape, p.dtype) for p in parts], n, 0, build)


def _row_tile(rows, most=512, sublanes=16):
    return max(t for t in range(sublanes, min(most, rows // 2) + 1, sublanes) if rows % t == 0)


def _partial_sum(name, own, recv, chip, comm=None):
    _, R, C = recv.shape
    tr = _row_tile(R)

    def body(o_ref, r_ref, p_ref):
        p_ref[...] = ((o_ref[...] + r_ref[0].astype(F32)) + r_ref[1].astype(F32)) + r_ref[2].astype(F32)

    if own.shape[0] != R:
        assert comm is None and own.shape[0] == N_CHIPS * R
        total = pl.pallas_call(
            lambda chip_ref, *refs: body(*refs), name=name,
            grid_spec=pltpu.PrefetchScalarGridSpec(
                num_scalar_prefetch=1, grid=(R // tr,),
                in_specs=[pl.BlockSpec((tr, C), lambda i, chip_ref: (chip_ref[0] * (R // tr) + i, 0)),
                          pl.BlockSpec((3, tr, C), lambda i, chip_ref: (0, i, 0))],
                out_specs=pl.BlockSpec((tr, C), lambda i, chip_ref: (i, 0))),
            out_shape=jax.ShapeDtypeStruct((R, C), F32), compiler_params=_params(("parallel",)),
        )(chip.reshape(1), own, recv)
        return [total], []
    return _call(name, body, grid=(R // tr,),
                 in_specs=[pl.BlockSpec((tr, C), lambda i: (i, 0)), pl.BlockSpec((3, tr, C), lambda i: (0, i, 0))],
                 out_specs=[pl.BlockSpec((tr, C), lambda i: (i, 0))], out_shape=[jax.ShapeDtypeStruct((R, C), F32)],
                 args=[own, recv], semantics=("parallel",), comm=comm)


def _adam_vals(w, g, m, v):
    m = ADAM_B1 * m + (1.0 - ADAM_B1) * g
    v = ADAM_B2 * v + (1.0 - ADAM_B2) * (g * g)
    m_hat = m / (1.0 - ADAM_B1 ** ADAM_STEP)
    v_hat = v / (1.0 - ADAM_B2 ** ADAM_STEP)
    delta = -ADAM_LR * (m_hat / (jnp.sqrt(v_hat) + ADAM_EPS) + ADAM_WD * w)
    return delta, m, v


def _adamw(name, w, m, v, mine, other, comm=None):
    R, C = w.shape
    tr = _row_tile(R)

    def body(w_ref, m_ref, v_ref, s_ref, n_ref, g_ref, d_ref, nm_ref, nv_ref):
        g = s_ref[...] + n_ref[...]
        d, nm, nv = _adam_vals(w_ref[...], g, m_ref[...], v_ref[...])
        g_ref[...], d_ref[...], nm_ref[...], nv_ref[...] = g, d, nm, nv

    spec = pl.BlockSpec((tr, C), lambda i: (i, 0))
    return _call(name, body, grid=(R // tr,), in_specs=[spec] * 5, out_specs=[spec] * 4,
                 out_shape=[jax.ShapeDtypeStruct((R, C), F32)] * 4, args=[w, m, v, mine, other],
                 semantics=("parallel",), comm=comm)


def _adamw_by_halves(name, w, m, v, mine, other, core):
    R, C = w.shape
    tr = _row_tile(R // 2)
    per_half = R // 2 // tr

    def body(c_ref, w_ref, m_ref, v_ref, s_ref, n_ref, g_ref, d_ref, nm_ref, nv_ref):
        g = jnp.where(pl.program_id(0) // per_half == c_ref[0, 0], s_ref[...], n_ref[...])
        d, nm, nv = _adam_vals(w_ref[...], g, m_ref[...], v_ref[...])
        g_ref[...], d_ref[...], nm_ref[...], nv_ref[...] = g, d, nm, nv

    spec = pl.BlockSpec((tr, C), lambda i: (i, 0))
    part = pl.BlockSpec((tr, C), lambda i: (i % per_half, 0))
    return pl.pallas_call(
        body, name=name, grid=(R // tr,),
        in_specs=[pl.BlockSpec(memory_space=pltpu.SMEM), spec, spec, spec, part, part], out_specs=[spec] * 4,
        out_shape=[jax.ShapeDtypeStruct((R, C), F32)] * 4, compiler_params=_params(("parallel",)),
    )(core, w, m, v, mine, other)


SMALL_LAYOUT = dict(norm_mix_g=(0, 1, 1024), b_in=(1, 8, 7424), hgrn_norm_g=(9, 1, 1024), norm_ffn_g=(10, 1, 1024),
                    norm_final_g=(11, 1, 1024), hgrn_lb_logits=(12, 2, 2048), attn_sinks=(14, 1, 16))
SMALL_LOSS_ROW, SMALL_ROWS = 15, 16


def _pack_small(grads, loss):
    rows = [jnp.pad(grads[name].astype(F32).reshape(-1), (0, nrows * D_MODEL - n))
            for name, (_, nrows, n) in SMALL_LAYOUT.items()]
    rows.append(jnp.pad(loss.astype(F32).reshape(1), (0, D_MODEL - 1)))
    return jnp.concatenate(rows).reshape(SMALL_ROWS, D_MODEL)


def _adamw_small(w, m, v, g_all):
    names = list(SMALL_LAYOUT)
    n = len(names)

    def body(a_ref, *refs):
        ins, outs = refs[:3 * n], refs[3 * n:]
        g_all_rows = a_ref[0]
        for dev in range(1, 8):
            g_all_rows = g_all_rows + a_ref[dev]
        for i, name in enumerate(names):
            first, nrows, count = SMALL_LAYOUT[name]
            w_ref, m_ref, v_ref = ins[3 * i:3 * i + 3]
            if w_ref.shape[0] == nrows:
                g = g_all_rows[first:first + nrows, :w_ref.shape[1]]
            else:
                last = count - (nrows - 1) * D_MODEL
                g = jnp.concatenate([g_all_rows[r:r + 1, :] for r in range(first, first + nrows - 1)]
                                    + [g_all_rows[first + nrows - 1:first + nrows, :last]], axis=1)
            d, nm, nv = _adam_vals(w_ref[...], g, m_ref[...], v_ref[...])
            for o_ref, val in zip(outs[4 * i:4 * i + 4], (g, d, nm, nv)):
                o_ref[...] = val
        outs[4 * n][...] = g_all_rows[SMALL_LOSS_ROW:SMALL_LOSS_ROW + 1, 0:1]

    res = pl.pallas_call(
        body, name="adamw_small",
        out_shape=[jax.ShapeDtypeStruct(w[name].shape, F32) for name in names for _ in range(4)]
        + [jax.ShapeDtypeStruct((1, 1), F32)],
    )(g_all, *[t[name] for name in names for t in (w, m, v)])
    return {name: res[4 * i:4 * i + 4] for i, name in enumerate(names)}, res[4 * n]


MATRICES = ("w_in", "w_branch_attn", "w_branch_hgrn", "w_out", "w_ffn_gate", "w_ffn_up", "w_ffn_down")
COLUMN_SHARDED = ("w_in", "w_ffn_gate", "w_ffn_up")
WEIGHTS = ("norm_mix_g", "w_in", "b_in", "attn_sinks", "hgrn_lb_logits", "hgrn_norm_g", "w_branch_attn",
           "w_branch_hgrn", "w_out", "norm_ffn_g", "w_ffn_gate", "w_ffn_up", "w_ffn_down", "norm_final_g")


def kernel(x, norm_mix_g, w_in, b_in, attn_sinks, hgrn_lb_logits, hgrn_norm_g, w_branch_attn, w_branch_hgrn, w_out, norm_ffn_g, w_ffn_gate, w_ffn_up, w_ffn_down, norm_final_g, loss_target, m_norm_mix_g, m_w_in, m_b_in, m_attn_sinks, m_hgrn_lb_logits, m_hgrn_norm_g, m_w_branch_attn, m_w_branch_hgrn, m_w_out, m_norm_ffn_g, m_w_ffn_gate, m_w_ffn_up, m_w_ffn_down, m_norm_final_g, v_norm_mix_g, v_w_in, v_b_in, v_attn_sinks, v_hgrn_lb_logits, v_hgrn_norm_g, v_w_branch_attn, v_w_branch_hgrn, v_w_out, v_norm_ffn_g, v_w_ffn_gate, v_w_ffn_up, v_w_ffn_down, v_norm_final_g):
    given = dict(locals())
    w = {n: given[n] for n in WEIGHTS}
    m = {n: given["m_" + n] for n in WEIGHTS}
    v = {n: given["v_" + n] for n in WEIGHTS}

    block = lambda a, n: jnp.transpose(a[0]) if n in COLUMN_SHARDED else a[0]
    unblock = lambda a, n: (jnp.transpose(a) if n in COLUMN_SHARDED else a)[None]
    net = _Net({n: block(w[n], n).astype(MXU_DTYPE) for n in MATRICES})
    net.gathered(("w_in",), [_gather_by_neighbours("gather_w_in", net.shards["w_in"])])
    vec = dict(norm_mix_g=norm_mix_g, b_in=b_in, attn_sinks=attn_sinks, hgrn_lb_logits=hgrn_lb_logits,
               hgrn_norm_g=hgrn_norm_g, norm_ffn_g=norm_ffn_g, norm_final_g=norm_final_g.reshape(1, D_MODEL))
    loss_part, dx, d_vecs = _local_step(x[0], loss_target[0], vec, net)

    small_all, = net.received(*net.last, carried=_small_copies(_pack_small(d_vecs, loss_part)))
    grads, deltas, new_m, new_v = {}, {}, {}, {}
    for n in ("w_ffn_down", "w_ffn_gate", "w_ffn_up", "w_out", "w_branch_attn", "w_branch_hgrn"):
        res, got = _adamw("adamw_" + n, block(w[n], n), block(m[n], n), block(v[n], n), net.sums[n], net.other[n],
                          comm=net.swap(("w_in",)) if n == "w_ffn_down" else None)
        if n == "w_ffn_down":
            net.other["w_in"], = got
        grads[n], deltas[n], new_m[n], new_v[n] = (unblock(r, n) for r in res)
    n = "w_in"
    res = _adamw_by_halves("adamw_" + n, block(w[n], n), block(m[n], n), block(v[n], n), net.sums[n], net.other[n],
                           _place()[2].reshape(1, 1))
    grads[n], deltas[n], new_m[n], new_v[n] = (unblock(r, n) for r in res)
    rows = lambda t: {n: t[n].reshape(-1, t[n].shape[-1]) for n in SMALL_LAYOUT}
    res, loss = _adamw_small(rows(w), rows(m), rows(v), small_all)
    for n, four in res.items():
        grads[n], deltas[n], new_m[n], new_v[n] = (r.reshape(w[n].shape) for r in four)
    loss = loss.reshape(())
    return (loss, dx[None], *[grads[n] for n in WEIGHTS], *[deltas[n] for n in WEIGHTS],
            *[new_m[n] for n in WEIGHTS], *[new_v[n] for n in WEIGHTS])
```

```python
import collections
import functools
import math

import jax
import jax.numpy as jnp
from jax import lax
from jax.experimental import pallas as pl
from jax.experimental.pallas import tpu as pltpu

F32 = jnp.float32
BF16 = jnp.bfloat16
MXU_DTYPE = jnp.bfloat16
SAVED_DTYPE = jnp.bfloat16
MESH_ID = pl.DeviceIdType.MESH

D_MODEL = 1024
HEAD_DIM = 64
Q_HEADS = 16
KV_HEADS = 2
GROUP = Q_HEADS // KV_HEADS
KV_WIDTH = KV_HEADS * HEAD_DIM
ATTN_BLOCK = 128
HGRN_HEADS = 8
HGRN_K = 128
CHUNK = 64
HGRN_TOKENS = 256
FFN = 2816
IN_SPLITS = (1024, 256, 4096, 2048)
EPS = 1e-6
NEG_INF = -1e30
ADAM_LR, ADAM_B1, ADAM_B2, ADAM_EPS, ADAM_WD, ADAM_STEP = 0.001, 0.9, 0.999, 1e-08, 0.01, 10
N_CHIPS = 4
VMEM_LIMIT = 60 * 1024 * 1024
ROW_ALIGN = 16
DW_ROWS = 256


def _params(sem=None):
    return pltpu.CompilerParams(dimension_semantics=sem, vmem_limit_bytes=VMEM_LIMIT)


def _sigmoid(v):
    return 0.5 * jnp.tanh(0.5 * v) + 0.5


def _dot(a, b, dims):
    return lax.dot_general(a.astype(MXU_DTYPE), b.astype(MXU_DTYPE), (dims, ((), ())),
                           preferred_element_type=F32)


def _nn(a, b):
    return _dot(a, b, ((1,), (0,)))


def _nt(a, b):
    return _dot(a, b, ((1,), (1,)))


def _tn(a, b):
    return _dot(a, b, ((0,), (0,)))


HBM_SPEC = pl.BlockSpec(memory_space=pl.ANY)


class _Carried:
    def __init__(self, arrays, out_shapes, n_remote, n_local, build):
        self.parts = [(len(arrays), len(out_shapes), build)]
        self.arrays, self.out_shapes = list(arrays), list(out_shapes)
        self.scratch = [pltpu.SemaphoreType.DMA((n_remote,)), pltpu.SemaphoreType.DMA((n_remote,)),
                        pltpu.SemaphoreType.DMA((max(n_local, 1),))]

    def __add__(self, other):
        both = _Carried([], [], 1, 0, None)
        both.parts = self.parts + other.parts
        both.arrays, both.out_shapes = self.arrays + other.arrays, self.out_shapes + other.out_shapes
        both.scratch = self.scratch + other.scratch
        return both

    def _built(self, ins, outs, sems):
        for p, (ni, no, build) in enumerate(self.parts):
            yield build(ins[:ni], outs[:no], *sems[3 * p:3 * p + 3])
            ins, outs = ins[ni:], outs[no:]

    def start(self, ins, outs, sems):
        core = lax.axis_index("c")
        for sends, _, local, *other_order in self._built(ins, outs, sems):
            for cp in local:
                cp.start()
            if not other_order:
                for cp in sends:
                    cp.start()
                continue

            @pl.when(core == 0)
            def _():
                for cp in sends:
                    cp.start()

            @pl.when(core == 1)
            def _():
                for cp in other_order[0]:
                    cp.start()

    def wait(self, ins, outs, sems):
        for sends, recvs, local, *_ in self._built(ins, outs, sems):
            for cp in recvs:
                cp.wait_recv()
            for cp in sends:
                cp.wait_send()
            for cp in local:
                cp.wait()


def _join(*comms):
    comms = [c for c in comms if c is not None]
    return functools.reduce(lambda a, b: a + b, comms) if comms else None


def _call(name, body, *, grid, in_specs, out_specs, out_shape, args, scratch=(), semantics=None, comm=None,
          aliases=None):
    n_in, n_out, n_scr = len(in_specs), len(out_specs), len(scratch)
    aliases = aliases or {}
    if comm is None:
        res = pl.pallas_call(body, name=name, grid=grid, in_specs=in_specs, out_specs=out_specs, out_shape=out_shape,
                             scratch_shapes=list(scratch), input_output_aliases=aliases,
                             compiler_params=_params(semantics))(*args)
        return list(res), []
    ci, co = len(comm.arrays), len(comm.out_shapes)

    def carrying(*refs):
        ins, refs = refs[:n_in], refs[n_in:]
        c_ins, refs = refs[:ci], refs[ci:]
        outs, refs = refs[:n_out], refs[n_out:]
        c_outs, refs = refs[:co], refs[co:]
        scr, sems = refs[:n_scr], refs[n_scr:]
        if not grid:
            comm.start(c_ins, c_outs, sems)
            body(*ins, *outs, *scr)
            comm.wait(c_ins, c_outs, sems)
            return
        first = functools.reduce(jnp.logical_and, [pl.program_id(a) == 0 for a in range(len(grid))])
        last = functools.reduce(jnp.logical_and, [pl.program_id(a) == g - 1 for a, g in enumerate(grid)])

        @pl.when(first)
        def _():
            comm.start(c_ins, c_outs, sems)

        body(*ins, *outs, *scr)

        @pl.when(last)
        def _():
            comm.wait(c_ins, c_outs, sems)

    res = pl.pallas_call(
        carrying, name=name, grid=grid, in_specs=list(in_specs) + [HBM_SPEC] * ci,
        out_specs=list(out_specs) + [HBM_SPEC] * co, out_shape=list(out_shape) + comm.out_shapes,
        scratch_shapes=list(scratch) + comm.scratch, input_output_aliases=aliases,
        compiler_params=_params(("arbitrary",) * len(grid) if grid else None),
    )(*args, *comm.arrays)
    return list(res[:n_out]), list(res[n_out:])


_Cols = collections.namedtuple("_Cols", "array first cols")
_Into = collections.namedtuple("_Into", "rows first held")


def _weight_grad(name, a, b, *, tm, tk, a_colsum=False, into=None, carrying=False, comm=None):
    cols = a if isinstance(a, _Cols) else _Cols(a, 0, a.shape[1])
    a = cols.array
    (T, N), M = b.shape, cols.cols
    tk = min(tk, T)
    assert cols.first % tm == 0 and M % tm == 0 and T % tk == 0, (name, cols.first, M, tm, T, tk)
    ni, nk, tile0 = M // tm, T // tk, cols.first // tm
    held = list(into.held) if into is not None and into.held is not None else []

    def body(a_ref, b_ref, *rest):
        keep_ref, send_ref = rest[len(held):len(held) + 2]
        sums_ref = rest[len(held) + 2] if a_colsum else None
        if nk == 1:
            acc = _tn(a_ref[...], b_ref[...])
            keep_ref[...], send_ref[...] = acc, acc.astype(send_ref.dtype)
            if a_colsum:
                sums_ref[...] = jnp.sum(a_ref[...].astype(F32), axis=0, keepdims=True)
            return
        acc_ref = rest[-1]
        k = pl.program_id(1)

        @pl.when(k == 0)
        def _():
            acc_ref[...] = jnp.zeros_like(acc_ref)
            if a_colsum:
                sums_ref[...] = jnp.zeros((1, tm), F32)

        if a_colsum:
            sums_ref[...] += jnp.sum(a_ref[...].astype(F32), axis=0, keepdims=True)
        acc_ref[...] += _tn(a_ref[...], b_ref[...])

        @pl.when(k == nk - 1)
        def _():
            keep_ref[...], send_ref[...] = acc_ref[...], acc_ref[...].astype(send_ref.dtype)

    if into is None:
        rows, out_spec = M, pl.BlockSpec((tm, N), lambda i, k: (i, 0))
    else:
        rows = into.rows
        out_spec = pl.BlockSpec((pl.Element(tm), pl.Element(N)),
                                lambda i, k: (pl.multiple_of(into.first + i * tm, ROW_ALIGN), 0))
    out_shape = [jax.ShapeDtypeStruct((rows, N), F32), jax.ShapeDtypeStruct((rows, N), MXU_DTYPE)]
    out_specs = [out_spec, out_spec]
    if a_colsum:
        out_shape.append(jax.ShapeDtypeStruct((1, M), F32))
        out_specs.append(pl.BlockSpec((1, tm), lambda i, k: (0, i)))
    res, got = _call(
        name, body, grid=(ni, nk),
        in_specs=[pl.BlockSpec((tk, tm), lambda i, k: (k, tile0 + i)), pl.BlockSpec((tk, N), lambda i, k: (k, 0))]
        + [HBM_SPEC] * len(held),
        out_specs=out_specs, out_shape=out_shape, scratch=[pltpu.VMEM((tm, N), F32)] if nk > 1 else [],
        args=[a, b] + held, aliases={2 + p: p for p in range(len(held))}, semantics=("parallel", "arbitrary"),
        comm=comm)
    return (res, got) if carrying else res


def _ffn_fwd(merged, w_out, x, gain, w_gate_t, w_up_t, *, tm, comm=None):
    (T, D), F = x.shape, w_gate_t.shape[0]

    def body(m_ref, wo_ref, x_ref, g_ref, wg_ref, wu_ref, h_ref, u_ref, gate_ref, up_ref, z_ref):
        h = x_ref[...] + _nn(m_ref[...], wo_ref[...])
        h_ref[...] = h
        u = (h * lax.rsqrt(jnp.mean(h * h, axis=-1, keepdims=True) + EPS) * g_ref[...]).astype(u_ref.dtype)
        u_ref[...] = u
        gate, up = _nt(u, wg_ref[...]), _nt(u, wu_ref[...])
        gate_ref[...], up_ref[...] = gate.astype(gate_ref.dtype), up.astype(up_ref.dtype)
        z_ref[...] = (gate * _sigmoid(gate) * up).astype(z_ref.dtype)

    rows = lambda n: pl.BlockSpec((tm, n), lambda i: (i, 0))
    fixed = _fixed_spec
    return _call("ffn_hidden", body, grid=(T // tm,),
                 in_specs=[rows(D), fixed(w_out), rows(D), fixed(gain), fixed(w_gate_t), fixed(w_up_t)],
                 out_specs=[rows(D), rows(D), rows(F), rows(F), rows(F)],
                 out_shape=[jax.ShapeDtypeStruct((T, D), F32), jax.ShapeDtypeStruct((T, D), MXU_DTYPE)]
                 + [jax.ShapeDtypeStruct((T, F), SAVED_DTYPE)] * 2 + [jax.ShapeDtypeStruct((T, F), MXU_DTYPE)],
                 args=[merged, w_out, x, gain, w_gate_t, w_up_t], semantics=("parallel",), comm=comm)


def _in_proj(x, gain, w_in_t, b_in, *, tm, comm=None):
    T, D = x.shape
    bounds = [sum(IN_SPLITS[:i]) for i in range(len(IN_SPLITS) + 1)]

    def body(x_ref, g_ref, w_ref, b_ref, u_ref, *piece_refs):
        xv = x_ref[...]
        r = lax.rsqrt(jnp.mean(xv * xv, axis=-1, keepdims=True) + EPS)
        u = (xv * r * g_ref[...]).astype(u_ref.dtype)
        u_ref[...] = u
        for o_ref, lo, hi in zip(piece_refs, bounds[:-1], bounds[1:]):
            o_ref[...] = (_nt(u, w_ref[lo:hi, :]) + b_ref[:, lo:hi]).astype(o_ref.dtype)

    rows = lambda n: pl.BlockSpec((tm, n), lambda i: (i, 0))
    fixed = _fixed_spec
    dtypes = (MXU_DTYPE, MXU_DTYPE, F32, F32)
    return _call("in_proj", body, grid=(T // tm,),
                 in_specs=[rows(D), fixed(gain), fixed(w_in_t), fixed(b_in)],
                 out_specs=[rows(D)] + [rows(n) for n in IN_SPLITS],
                 out_shape=[jax.ShapeDtypeStruct((T, D), MXU_DTYPE)]
                 + [jax.ShapeDtypeStruct((T, n), dt) for n, dt in zip(IN_SPLITS, dtypes)],
                 args=[x, gain, w_in_t, b_in], semantics=("parallel",), comm=comm)


def _row_spec(tm, n):
    return pl.BlockSpec((tm, n), lambda i: (i, 0))


def _fixed_spec(a):
    return pl.BlockSpec(a.shape, lambda i: (0,) * a.ndim, pipeline_mode=pl.Buffered(1))


def _partials_spec(n):
    return pl.BlockSpec((8, n), lambda i: (i, 0))


def _row_parts(tm, parts):
    assert tm % parts == 0, (tm, parts)
    return [slice(p * (tm // parts), (p + 1) * (tm // parts)) for p in range(parts)]


def _ffn_tail(z, w_down, h1, target, gain, gate, up, *, tm, parts=2):
    (T, F), D = z.shape, h1.shape[1]

    def body(z_ref, w_ref, h_ref, t_ref, g_ref, gate_ref, up_ref, dh_ref, dhb_ref, dgate_ref, dup_ref, dg_ref, l_ref):
        part, dgain = 0.0, 0.0
        pieces = _row_parts(tm, parts)
        h2s = [h_ref[rows, :] + _nn(z_ref[rows, :], w_ref[...]) for rows in pieces]
        for rows, h2 in zip(pieces, h2s):
            r = lax.rsqrt(jnp.mean(h2 * h2, axis=-1, keepdims=True) + EPS)
            xhat = h2 * r
            err = xhat * g_ref[...] - t_ref[rows, :]
            part += 0.5 * jnp.sum(jnp.sum(err * err, axis=-1, keepdims=True), axis=0, keepdims=True) / D
            dy = err / D
            dxh = dy * g_ref[...]
            dh2 = r * (dxh - xhat * jnp.mean(dxh * xhat, axis=-1, keepdims=True))
            dh_ref[rows, :] = dh2
            dhb = dh2.astype(dhb_ref.dtype)
            dhb_ref[rows, :] = dhb
            dgain += jnp.sum(dy * xhat, axis=0, keepdims=True)
            dz = _nt(dhb, w_ref[...])
            gv, upv = gate_ref[rows, :].astype(F32), up_ref[rows, :].astype(F32)
            s = _sigmoid(gv)
            dgate_ref[rows, :] = (dz * upv * (s * (1.0 + gv * (1.0 - s)))).astype(dgate_ref.dtype)
            dup_ref[rows, :] = (dz * (gv * s)).astype(dup_ref.dtype)
        dg_ref[...] = jnp.broadcast_to(dgain, dg_ref.shape)
        l_ref[...] = jnp.broadcast_to(part, l_ref.shape)

    low = lambda n: jax.ShapeDtypeStruct((T, n), MXU_DTYPE)
    part = jax.ShapeDtypeStruct((8 * (T // tm), D), F32)
    return pl.pallas_call(
        body, name="ffn_tail", grid=(T // tm,),
        in_specs=[_row_spec(tm, F), _fixed_spec(w_down), _row_spec(tm, D), _row_spec(tm, D), _fixed_spec(gain),
                  _row_spec(tm, F), _row_spec(tm, F)],
        out_specs=[_row_spec(tm, D), _row_spec(tm, D), _row_spec(tm, F), _row_spec(tm, F), _partials_spec(D),
                   _partials_spec(D)],
        out_shape=[jax.ShapeDtypeStruct((T, D), F32), low(D), low(F), low(F), part, part],
        compiler_params=_params(("parallel",)),
    )(z, w_down, h1, target, gain, gate, up)


def _ffn_in_bwd(dgate, dup, w_gate_t, w_up_t, h1, gain, dres, *, tm, comm=None):
    (T, F), D = dgate.shape, h1.shape[1]

    def body(dg_ref, du_ref, wg_ref, wu_ref, h_ref, g_ref, r_ref, dh_ref, dhb_ref, dgain_ref):
        d_u2 = _nn(dg_ref[...], wg_ref[...]) + _nn(du_ref[...], wu_ref[...])
        dx, dgain = _rmsnorm_bwd_vals(d_u2, h_ref[...], g_ref[...])
        dh = r_ref[...] + dx
        dh_ref[...] = dh
        dhb_ref[...] = dh.astype(dhb_ref.dtype)
        dgain_ref[...] = jnp.broadcast_to(dgain, dgain_ref.shape)

    return _call("d_ffn_in", body, grid=(T // tm,),
                 in_specs=[_row_spec(tm, F), _row_spec(tm, F), _fixed_spec(w_gate_t), _fixed_spec(w_up_t),
                           _row_spec(tm, D), _fixed_spec(gain), _row_spec(tm, D)],
                 out_specs=[_row_spec(tm, D), _row_spec(tm, D), _partials_spec(D)],
                 out_shape=[jax.ShapeDtypeStruct((T, D), F32), jax.ShapeDtypeStruct((T, D), MXU_DTYPE),
                            jax.ShapeDtypeStruct((8 * (T // tm), D), F32)],
                 args=[dgate, dup, w_gate_t, w_up_t, h1, gain, dres], semantics=("parallel",), comm=comm)


def _in_proj_bwd(pieces, w_in_t, x, gain, dres, *, tm, comm=None):
    T, D = x.shape
    n = len(pieces)

    def body(*refs):
        dps, (w_ref, x_ref, g_ref, r_ref, dx_ref, dgain_ref) = refs[:n], refs[n:]
        d_u = None
        for dp_ref, (dp, first) in zip(dps, pieces):
            term = _nn(dp_ref[...], w_ref[first:first + dp.shape[1], :])
            d_u = term if d_u is None else d_u + term
        dx, dgain = _rmsnorm_bwd_vals(d_u, x_ref[...], g_ref[...])
        dx_ref[...] = r_ref[...] + dx
        dgain_ref[...] = jnp.broadcast_to(dgain, dgain_ref.shape)

    return _call("d_u", body, grid=(T // tm,),
                 in_specs=[_row_spec(tm, dp.shape[1]) for dp, _ in pieces]
                 + [_fixed_spec(w_in_t), _row_spec(tm, D), _fixed_spec(gain), _row_spec(tm, D)],
                 out_specs=[_row_spec(tm, D), _partials_spec(D)],
                 out_shape=[jax.ShapeDtypeStruct((T, D), F32), jax.ShapeDtypeStruct((8 * (T // tm), D), F32)],
                 args=[dp for dp, _ in pieces] + [w_in_t, x, gain, dres], semantics=("parallel",), comm=comm)


def _merge_fwd(y_a, y_b, w_a, w_b, gates, *, tm, comm=None):
    T, D = y_a.shape

    def body(ya_ref, yb_ref, wa_ref, wb_ref, ga_ref, gb_ref, pa_ref, pb_ref, m_ref):
        pa, pb = _nn(ya_ref[...], wa_ref[...]), _nn(yb_ref[...], wb_ref[...])
        pa_ref[...], pb_ref[...] = pa.astype(pa_ref.dtype), pb.astype(pb_ref.dtype)
        m_ref[...] = (_sigmoid(ga_ref[...]) * pa + _sigmoid(gb_ref[...]) * pb).astype(m_ref.dtype)

    rows = pl.BlockSpec((tm, D), lambda i: (i, 0))
    whole = pl.BlockSpec((D, D), lambda i: (0, 0), pipeline_mode=pl.Buffered(1))
    return _call("branch_merge", body, grid=(T // tm,),
                 in_specs=[rows, rows, whole, whole, rows, pl.BlockSpec((tm, D), lambda i: (i, 1))],
                 out_specs=[rows] * 3,
                 out_shape=[jax.ShapeDtypeStruct((T, D), SAVED_DTYPE)] * 2 + [jax.ShapeDtypeStruct((T, D), MXU_DTYPE)],
                 args=[y_a, y_b, w_a, w_b, gates, gates], semantics=("parallel",), comm=comm)


def _merge_bwd(dh, w_out, w_a, w_b, p_a, p_b, gates, *, tm, d_in_width, first):
    T, D = dh.shape

    def body(dh_ref, wo_ref, wa_ref, wb_ref, pa_ref, pb_ref, ga_ref, gb_ref, dpa_ref, dpb_ref, dya_ref, dyb_ref,
             din_ref):
        dm = _nt(dh_ref[...], wo_ref[...])
        sa, sb = _sigmoid(ga_ref[...]), _sigmoid(gb_ref[...])
        dpa, dpb = (dm * sa).astype(dpa_ref.dtype), (dm * sb).astype(dpb_ref.dtype)
        dpa_ref[...], dpb_ref[...] = dpa, dpb
        din_ref[:, :D] = (dm * pa_ref[...].astype(F32) * sa * (1.0 - sa)).astype(din_ref.dtype)
        din_ref[:, D:] = (dm * pb_ref[...].astype(F32) * sb * (1.0 - sb)).astype(din_ref.dtype)
        dya_ref[...] = _nt(dpa, wa_ref[...]).astype(dya_ref.dtype)
        dyb_ref[...] = _nt(dpb, wb_ref[...]).astype(dyb_ref.dtype)

    rows = pl.BlockSpec((tm, D), lambda i: (i, 0))
    whole = pl.BlockSpec((D, D), lambda i: (0, 0), pipeline_mode=pl.Buffered(1))
    low = jax.ShapeDtypeStruct((T, D), MXU_DTYPE)
    return pl.pallas_call(
        body, name="d_branch_merge", grid=(T // tm,),
        in_specs=[rows, whole, whole, whole, rows, rows, rows, pl.BlockSpec((tm, D), lambda i: (i, 1))],
        out_specs=[rows] * 4 + [pl.BlockSpec((pl.Element(tm), pl.Element(2 * D)), lambda i: (
            pl.multiple_of(i * tm, ROW_ALIGN), first))],
        out_shape=[low] * 3 + [jax.ShapeDtypeStruct((T, D), F32), jax.ShapeDtypeStruct((T, d_in_width), MXU_DTYPE)],
        compiler_params=_params(("parallel",)),
    )(dh, w_out, w_a, w_b, p_a, p_b, gates, gates)


def _colsum_partials(p):
    return jnp.sum(p.reshape(-1, 8, p.shape[-1])[:, 0, :], axis=0, keepdims=True)


def _rmsnorm_bwd_vals(dy, xin, g):
    rstd = lax.rsqrt(jnp.mean(xin * xin, axis=-1, keepdims=True) + EPS)
    xhat = xin * rstd
    dg = jnp.sum(dy * xhat, axis=0, keepdims=True)
    dxh = dy * g
    dx = rstd * (dxh - xhat * jnp.mean(dxh * xhat, axis=-1, keepdims=True))
    return dx, dg


ATTN_SCALE = 1.0 / math.sqrt(HEAD_DIM)
GROUP_LANES = GROUP * ATTN_BLOCK
PAIR = 2 * HEAD_DIM


def _attn_mask():
    kj = lax.broadcasted_iota(jnp.int32, (ATTN_BLOCK, GROUP_LANES), 0)
    qi = lax.broadcasted_iota(jnp.int32, (ATTN_BLOCK, GROUP_LANES), 1) & (ATTN_BLOCK - 1)
    return kj <= qi


def _heads_transposed(ref, g, scale=None):
    parts = []
    for a in range(GROUP // 2):
        lo = (g * GROUP // 2 + a) * PAIR
        pair = ref[:, lo:lo + PAIR].astype(F32)
        pair = (pair if scale is None else pair * scale).T
        parts += [pair[:HEAD_DIM], pair[HEAD_DIM:]]
    return jnp.concatenate(parts, axis=1).astype(MXU_DTYPE)


def _heads_back(ref, g, vt):
    for a in range(GROUP // 2):
        lo = (g * GROUP // 2 + a) * PAIR
        pair = jnp.concatenate([vt[:, (2 * a) * ATTN_BLOCK:(2 * a + 1) * ATTN_BLOCK],
                                vt[:, (2 * a + 1) * ATTN_BLOCK:(2 * a + 2) * ATTN_BLOCK]], axis=0)
        ref[:, lo:lo + PAIR] = pair.T.astype(ref.dtype)


def _kv_parts(kv_ref, g):
    ks = slice(g * HEAD_DIM, (g + 1) * HEAD_DIM)
    vs = slice(KV_WIDTH + g * HEAD_DIM, KV_WIDTH + (g + 1) * HEAD_DIM)
    return kv_ref[:, ks].astype(MXU_DTYPE), kv_ref[:, vs].astype(MXU_DTYPE)


def _sink_rows(sinks):
    return jnp.repeat(sinks.reshape(KV_HEADS, GROUP), ATTN_BLOCK, axis=1)


def _attn_fwd(pq, pkv, sinks, comm=None):
    T = pq.shape[0]
    nb = T // ATTN_BLOCK

    def body(q_ref, kvc_ref, kvp_ref, s_ref, y_ref, lse_ref):
        mask_c = _attn_mask()
        has_prev = pl.program_id(0) > 0
        for g in range(KV_HEADS):
            (kc, vc), (kp, vp) = _kv_parts(kvc_ref, g), _kv_parts(kvp_ref, g)
            qt = _heads_transposed(q_ref, g, ATTN_SCALE)
            s = jnp.where(mask_c, _nn(kc, qt), jnp.where(has_prev, _nn(kp, qt), NEG_INF))
            sink = s_ref[g:g + 1, :]
            m = jnp.maximum(jnp.max(s, axis=0, keepdims=True), sink)
            p = jnp.exp(s - m)
            den = jnp.sum(p, axis=0, keepdims=True) + jnp.exp(sink - m)
            pc = jnp.where(mask_c, p, 0.0)
            _heads_back(y_ref, g, (_tn(vc, pc) + _tn(vp, p - pc)) / den)
            lse = m + jnp.log(den)
            for i in range(GROUP):
                lse_ref[g * GROUP + i:g * GROUP + i + 1, :] = lse[:, i * ATTN_BLOCK:(i + 1) * ATTN_BLOCK]

    return _call(
        "attn_fwd", body, grid=(nb,),
        in_specs=[pl.BlockSpec((ATTN_BLOCK, D_MODEL), lambda n: (n, 0)),
                  pl.BlockSpec((ATTN_BLOCK, 2 * KV_WIDTH), lambda n: (n, 0)),
                  pl.BlockSpec((ATTN_BLOCK, 2 * KV_WIDTH), lambda n: (jnp.maximum(n - 1, 0), 0)),
                  pl.BlockSpec((KV_HEADS, GROUP_LANES), lambda n: (0, 0))],
        out_specs=[pl.BlockSpec((ATTN_BLOCK, D_MODEL), lambda n: (n, 0)),
                   pl.BlockSpec((Q_HEADS, ATTN_BLOCK), lambda n: (0, n))],
        out_shape=[jax.ShapeDtypeStruct((T, D_MODEL), MXU_DTYPE), jax.ShapeDtypeStruct((Q_HEADS, T), F32)],
        args=[pq, pkv, pkv, _sink_rows(sinks)], semantics=("parallel",), comm=comm)


def _attn_bwd(pq, pkv, sinks, lse, dy, d_in, comm=None):
    T = pq.shape[0]
    nb = T // ATTN_BLOCK
    cur = lambda n: (jnp.minimum(n, nb - 1), 0)
    done = D_MODEL + 2 * KV_WIDTH

    def body(q_ref, kvc_ref, kvp_ref, s_ref, lse_ref, dy_ref, _, out_ref, ds_ref, carry, top, bot, dq_ref):
        n = pl.program_id(0)

        @pl.when(n == 0)
        def _():
            carry[...] = jnp.zeros_like(carry)
            dq_ref[...] = jnp.zeros_like(dq_ref)
            ds_ref[...] = jnp.zeros_like(ds_ref)

        out_ref[:, :D_MODEL] = dq_ref[...]

        @pl.when(n < nb)
        def _():
            mask_c = _attn_mask()
            valid = jnp.logical_or(mask_c, n > 0)
            for g in range(KV_HEADS):
                ks = slice(g * HEAD_DIM, (g + 1) * HEAD_DIM)
                vs = slice(KV_WIDTH + g * HEAD_DIM, KV_WIDTH + (g + 1) * HEAD_DIM)
                (kc, vc), (kp, vp) = _kv_parts(kvc_ref, g), _kv_parts(kvp_ref, g)
                qt = _heads_transposed(q_ref, g, ATTN_SCALE)
                dot = _heads_transposed(dy_ref, g)
                lse = jnp.concatenate([lse_ref[g * GROUP + i:g * GROUP + i + 1, :] for i in range(GROUP)], axis=1)
                p = jnp.where(valid, jnp.exp(jnp.where(mask_c, _nn(kc, qt), _nn(kp, qt)) - lse), 0.0)
                dp = jnp.where(mask_c, _nn(vc, dot), _nn(vp, dot))
                delta = jnp.sum(p * dp, axis=0, keepdims=True)
                ds = p * (dp - delta)
                ds_c, p_c = jnp.where(mask_c, ds, 0.0), jnp.where(mask_c, p, 0.0)
                ds_p, p_p = ds - ds_c, p - p_c
                _heads_back(dq_ref, g, (_tn(kc, ds_c) + _tn(kp, ds_p)) * ATTN_SCALE)
                bot[:, ks], bot[:, vs] = _nt(ds_c, qt), _nt(p_c, dot)
                top[:, ks], top[:, vs] = _nt(ds_p, qt), _nt(p_p, dot)
                ds_ref[g:g + 1, :] -= jnp.exp(s_ref[g:g + 1, :] - lse) * delta
            out_ref[:, D_MODEL:] = (carry[...] + top[...]).astype(out_ref.dtype)
            carry[...] = bot[...]

        @pl.when(n == nb)
        def _():
            out_ref[:, D_MODEL:] = carry[...].astype(out_ref.dtype)

    return _call(
        "attn_bwd", body, grid=(nb + 1,),
        in_specs=[pl.BlockSpec((ATTN_BLOCK, D_MODEL), cur),
                  pl.BlockSpec((ATTN_BLOCK, 2 * KV_WIDTH), cur),
                  pl.BlockSpec((ATTN_BLOCK, 2 * KV_WIDTH), lambda n: (jnp.maximum(jnp.minimum(n, nb - 1) - 1, 0), 0)),
                  pl.BlockSpec((KV_HEADS, GROUP_LANES), lambda n: (0, 0)),
                  pl.BlockSpec((Q_HEADS, ATTN_BLOCK), lambda n: (0, jnp.minimum(n, nb - 1))),
                  pl.BlockSpec((ATTN_BLOCK, D_MODEL), cur), HBM_SPEC],
        out_specs=[pl.BlockSpec((ATTN_BLOCK, done), lambda n: (jnp.maximum(n - 1, 0), 0)),
                   pl.BlockSpec((KV_HEADS, GROUP_LANES), lambda n: (0, 0))],
        out_shape=[jax.ShapeDtypeStruct(d_in.shape, d_in.dtype), jax.ShapeDtypeStruct((KV_HEADS, GROUP_LANES), F32)],
        scratch=[pltpu.VMEM((ATTN_BLOCK, 2 * KV_WIDTH), F32)] * 3 + [pltpu.VMEM((ATTN_BLOCK, D_MODEL), MXU_DTYPE)],
        args=[pq, pkv, pkv, _sink_rows(sinks), lse, dy, d_in], semantics=("arbitrary",), comm=comm, aliases={6: 0})


def _lower_bound(l):
    m = jnp.maximum(l[0:1], l[1:2])
    e0, e1 = jnp.exp(l[0:1] - m), jnp.exp(l[1:2] - m)
    return e0 / (e0 + e1)


def _tri(lower):
    r = lax.broadcasted_iota(jnp.int32, (CHUNK, CHUNK), 0)
    c = lax.broadcasted_iota(jnp.int32, (CHUNK, CHUNK), 1)
    return (r >= c) if lower else (c >= r)


def _chunk_sum(mask, v):
    ones = mask.astype(BF16)
    hi = v.astype(BF16)
    rest = v - hi.astype(F32)
    mid = rest.astype(BF16)
    lo = (rest - mid.astype(F32)).astype(BF16)
    part = lambda t: lax.dot_general(ones, t, (((1,), (0,)), ((), ())), preferred_element_type=F32)
    return part(hi) + part(mid) + part(lo)


def _hgrn_chunk_inputs(hq, hf, lb, causal):
    half_t = 0.5 * jnp.tanh(0.5 * hf)
    sg, sgn = 0.5 + half_t, 0.5 - half_t
    f = lb + (1.0 - lb) * sg
    kk = (1.0 - lb) * sgn
    sq = _sigmoid(hq)
    q = hq * sq
    b = _chunk_sum(causal, jnp.log(f))
    bm, bl = b[CHUNK // 2 - 1:CHUNK // 2, :], b[CHUNK - 1:CHUNK, :]
    e_qm, e_km = jnp.exp(b - bm), jnp.exp(bm - b)
    e_qs, e_kl = e_qm * jnp.exp(bm), e_km * jnp.exp(bl - bm)
    return dict(sg=sg, sgn=sgn, f=f, kk=kk, sq=sq, q=q, e_qm=e_qm, e_km=e_km, e_qs=e_qs, e_kl=e_kl,
                qm=q * e_qm, km=kk * e_km, qs=q * e_qs, kl=kk * e_kl, el=jnp.exp(bl))


def _hgrn_fwd(ph, lb_logits, norm_g, comm=None):
    T = ph.shape[0]
    nblk, cpb = T // HGRN_TOKENS, HGRN_TOKENS // CHUNK
    col = lambda c: pl.BlockSpec((HGRN_TOKENS, D_MODEL), functools.partial(lambda i, c: (i, c), c=c))

    def body(hq_ref, hf_ref, hi_ref, hg_ref, l_ref, ng_ref, y_ref, o_ref, st_ref, s_ref):
        @pl.when(pl.program_id(0) == 0)
        def _():
            s_ref[...] = jnp.zeros_like(s_ref)

        lb = _lower_bound(l_ref[...])
        causal = _tri(True)
        for c in range(cpb):
            rows = slice(c * CHUNK, (c + 1) * CHUNK)
            t = _hgrn_chunk_inputs(hq_ref[rows, :], hf_ref[rows, :], lb, causal)
            qm, km, qs, kl = (t[n].astype(MXU_DTYPE) for n in ("qm", "km", "qs", "kl"))
            v = hi_ref[rows, :].astype(MXU_DTYPE)
            for h in range(HGRN_HEADS):
                ls = slice(h * HGRN_K, (h + 1) * HGRN_K)
                st = s_ref[h]
                st_ref[c, ls, :] = st
                a = jnp.where(causal, _nt(qm[:, ls], km[:, ls]), 0.0)
                o_ref[rows, ls] = _nn(a, v[:, ls]) + _nt(qs[:, ls], st)
                s_ref[h] = t["el"][:, ls] * st + _tn(v[:, ls], kl[:, ls])
        for h in range(HGRN_HEADS):
            ls = slice(h * HGRN_K, (h + 1) * HGRN_K)
            o = o_ref[:, ls]
            r = lax.rsqrt(jnp.mean(o * o, axis=-1, keepdims=True) + EPS)
            y_ref[:, ls] = (o * r * ng_ref[:, ls] * _sigmoid(hg_ref[:, ls])).astype(y_ref.dtype)

    return _call(
        "hgrn_fwd", body, grid=(nblk,),
        in_specs=[col(0), col(1), col(2), col(3),
                  pl.BlockSpec((2, D_MODEL), lambda i: (0, 0)), pl.BlockSpec((1, D_MODEL), lambda i: (0, 0))],
        out_specs=[pl.BlockSpec((HGRN_TOKENS, D_MODEL), lambda i: (i, 0)),
                   pl.BlockSpec((HGRN_TOKENS, D_MODEL), lambda i: (i, 0)),
                   pl.BlockSpec((cpb, D_MODEL, HGRN_K), lambda i: (i, 0, 0))],
        out_shape=[jax.ShapeDtypeStruct((T, D_MODEL), MXU_DTYPE), jax.ShapeDtypeStruct((T, D_MODEL), F32),
                   jax.ShapeDtypeStruct((T // CHUNK, D_MODEL, HGRN_K), F32)],
        scratch=[pltpu.VMEM((HGRN_HEADS, HGRN_K, HGRN_K), F32)],
        args=[ph, ph, ph, ph, lb_logits, norm_g], semantics=("arbitrary",), comm=comm)


def _hgrn_bwd(ph, o_raw, states, dy, lb_logits, norm_g, d_in, first, comm=None):
    T = ph.shape[0]
    nblk, cpb = T // HGRN_TOKENS, HGRN_TOKENS // CHUNK
    rev = lambda i: nblk - 1 - i
    col = lambda c: pl.BlockSpec((HGRN_TOKENS, D_MODEL), functools.partial(lambda i, c: (rev(i), c), c=c))
    tok = pl.BlockSpec((HGRN_TOKENS, D_MODEL), lambda i: (rev(i), 0))

    def body(hq_ref, hf_ref, hi_ref, hg_ref, o_ref, st_ref, dy_ref, l_ref, ng_ref, _,
             dph_ref, dng_ref, dl_ref, dst_ref, dlb_ref, do_s, dqm_s, dkm_s, dqs_s, dkl_s, dv_s, del_s):
        i = pl.program_id(0)

        @pl.when(i == 0)
        def _():
            dst_ref[...] = jnp.zeros_like(dst_ref)
            dlb_ref[...] = jnp.zeros_like(dlb_ref)
            dng_ref[...] = jnp.zeros_like(dng_ref)

        lb = _lower_bound(l_ref[...])
        causal, anti = _tri(True), _tri(False)
        row = lax.broadcasted_iota(jnp.int32, (CHUNK, D_MODEL), 0)
        for c in reversed(range(cpb)):
            rows = slice(c * CHUNK, (c + 1) * CHUNK)
            hq = hq_ref[rows, :]
            t = _hgrn_chunk_inputs(hq, hf_ref[rows, :], lb, causal)
            sgg = _sigmoid(hg_ref[rows, :])
            dyv = dy_ref[rows, :]
            for h in range(HGRN_HEADS):
                ls = slice(h * HGRN_K, (h + 1) * HGRN_K)
                o = o_ref[rows, ls]
                r = lax.rsqrt(jnp.mean(o * o, axis=-1, keepdims=True) + EPS)
                nrm = o * r
                g_h = sgg[:, ls]
                dph_ref[rows, 3 * D_MODEL + h * HGRN_K:3 * D_MODEL + (h + 1) * HGRN_K] = (
                    dyv[:, ls] * nrm * ng_ref[:, ls] * g_h * (1.0 - g_h)).astype(dph_ref.dtype)
                dyg = dyv[:, ls] * g_h
                dng_ref[:, ls] += jnp.sum(dyg * nrm, axis=0, keepdims=True)
                dn = dyg * ng_ref[:, ls]
                do_s[:, ls] = r * (dn - nrm * jnp.mean(dn * nrm, axis=-1, keepdims=True))
            qm, km, qs, kl = (t[n].astype(MXU_DTYPE) for n in ("qm", "km", "qs", "kl"))
            v = hi_ref[rows, :].astype(MXU_DTYPE)
            do = do_s[...].astype(MXU_DTYPE)
            for h in range(HGRN_HEADS):
                ls = slice(h * HGRN_K, (h + 1) * HGRN_K)
                st = st_ref[c, ls, :]
                dst = dst_ref[h]
                a = jnp.where(causal, _nt(qm[:, ls], km[:, ls]), 0.0)
                da = jnp.where(causal, _nt(do[:, ls], v[:, ls]), 0.0)
                dv_s[:, ls] = _tn(a, do[:, ls]) + _nt(kl[:, ls], dst)
                dkl_s[:, ls] = _nn(v[:, ls], dst)
                dqs_s[:, ls] = _nn(do[:, ls], st)
                del_s[:, ls] = jnp.sum(dst * st, axis=0, keepdims=True)
                dst_ref[h] = _tn(do[:, ls], qs[:, ls]) + t["el"][:, ls] * dst
                dqm_s[:, ls] = _nn(da, km[:, ls])
                dkm_s[:, ls] = _tn(da, qm[:, ls])
            dqm, dkm, dqs, dkl = dqm_s[...], dkm_s[...], dqs_s[...], dkl_s[...]
            dq = dqm * t["e_qm"] + dqs * t["e_qs"]
            dk = dkm * t["e_km"] + dkl * t["e_kl"]
            t_qm, t_km, t_kl = dqm * t["qm"], dkm * t["km"], dkl * t["kl"]
            db = t_qm - t_km + dqs * t["qs"] - t_kl
            db_mid = jnp.sum(t_km - t_qm, axis=0, keepdims=True)
            db_last = jnp.sum(t_kl, axis=0, keepdims=True) + del_s[...] * t["el"]
            db = db + jnp.where(row == CHUNK // 2 - 1, db_mid, 0.0) + jnp.where(row == CHUNK - 1, db_last, 0.0)
            dlogf = _chunk_sum(anti, db)
            sq, sg, sgn, f = t["sq"], t["sg"], t["sgn"], t["f"]
            dph_ref[rows, 0:D_MODEL] = (dq * (sq * (1.0 + hq * (1.0 - sq)))).astype(dph_ref.dtype)
            dph_ref[rows, D_MODEL:2 * D_MODEL] = (
                dlogf * (1.0 - lb) * sg * (1.0 - sg) / f - dk * (1.0 - lb) * sgn * (1.0 - sgn)).astype(dph_ref.dtype)
            dph_ref[rows, 2 * D_MODEL:3 * D_MODEL] = dv_s[...].astype(dph_ref.dtype)
            dlb_ref[...] += jnp.sum(dlogf * (1.0 - sg) / f - dk * sgn, axis=0, keepdims=True)

        @pl.when(i == nblk - 1)
        def _():
            dl0 = dlb_ref[...] * lb * (1.0 - lb)
            dl_ref[0:1, :] = dl0
            dl_ref[1:2, :] = -dl0

    wide = pltpu.VMEM((CHUNK, D_MODEL), F32)
    return _call(
        "hgrn_bwd", body, grid=(nblk,),
        in_specs=[col(0), col(1), col(2), col(3), tok,
                  pl.BlockSpec((cpb, D_MODEL, HGRN_K), lambda i: (rev(i), 0, 0)), tok,
                  pl.BlockSpec((2, D_MODEL), lambda i: (0, 0)), pl.BlockSpec((1, D_MODEL), lambda i: (0, 0)), HBM_SPEC],
        out_specs=[pl.BlockSpec((pl.Element(HGRN_TOKENS), pl.Element(4 * D_MODEL)), lambda i: (
                       pl.multiple_of(rev(i) * HGRN_TOKENS, ROW_ALIGN), first)),
                   pl.BlockSpec((1, D_MODEL), lambda i: (0, 0)), pl.BlockSpec((2, D_MODEL), lambda i: (0, 0))],
        out_shape=[jax.ShapeDtypeStruct(d_in.shape, d_in.dtype), jax.ShapeDtypeStruct((1, D_MODEL), F32),
                   jax.ShapeDtypeStruct((2, D_MODEL), F32)],
        scratch=[pltpu.VMEM((HGRN_HEADS, HGRN_K, HGRN_K), F32), pltpu.VMEM((1, D_MODEL), F32),
                 wide, wide, wide, wide, wide, wide, pltpu.VMEM((1, D_MODEL), F32)],
        args=[ph, ph, ph, ph, o_raw, states, dy, lb_logits, norm_g, d_in], semantics=("arbitrary",), comm=comm,
        aliases={9: 0})


def _local_step(x, target, vec, net):
    T, D = x.shape
    norm_mix_g, b_in, sinks, lb_logits = vec["norm_mix_g"], vec["b_in"], vec["attn_sinks"], vec["hgrn_lb_logits"]
    hgrn_norm_g, norm_ffn_g, norm_final_g = vec["hgrn_norm_g"], vec["norm_ffn_g"], vec["norm_final_g"]
    w_in = net.full("w_in")
    o_q, o_kv, o_h, o_g = (sum(IN_SPLITS[:i]) for i in range(4))
    TM = 512

    names = ("w_ffn_gate",)
    (u, pq, pkv, ph, pg), got = _in_proj(x, norm_mix_g, w_in, b_in, tm=256, comm=net.gather(names))
    net.gathered(names, got)
    names = ("w_branch_attn", "w_branch_hgrn")
    (y_attn, lse), got = _attn_fwd(pq, pkv, sinks, comm=net.gather(names))
    net.gathered(names, got)
    names = ("w_ffn_up",)
    (y_hgrn, o_raw, states), got = _hgrn_fwd(ph, lb_logits, hgrn_norm_g, comm=net.gather(names))
    net.gathered(names, got)
    w_ba, w_bh = net.full("w_branch_attn"), net.full("w_branch_hgrn")
    w_gate, w_up = net.full("w_ffn_gate"), net.full("w_ffn_up")
    names = ("w_out",)
    (ya, yb, merged), got = _merge_fwd(y_attn, y_hgrn, w_ba, w_bh, pg, tm=TM, comm=net.gather(names))
    net.gathered(names, got)
    w_out = net.full("w_out")

    names = ("w_ffn_down",)
    (h1, u2, gpre, up, z), got = _ffn_fwd(merged, w_out, x, norm_ffn_g, w_gate, w_up, tm=256,
                                          comm=net.gather(names))
    net.gathered(names, got)
    w_down = net.full("w_ffn_down")

    dh2, dh2b, dgp, dup, dgf_p, loss_p = _ffn_tail(z, w_down, h1, target, norm_final_g, gpre, up, tm=512)
    loss = jnp.sum(loss_p.reshape(-1, 8, D)[:, 0, 0])
    d_norm_final = _colsum_partials(dgf_p)
    d_w_down = _weight_grad("dw_down", z, dh2b, tm=DW_ROWS, tk=T)

    (dh1, dh1b, dg2_p), _ = _ffn_in_bwd(dgp, dup, w_gate, w_up, h1, norm_ffn_g, dh2, tm=256)
    d_norm_ffn = _colsum_partials(dg2_p)
    d_w_gate = _weight_grad("dw_gate", dgp, u2, tm=DW_ROWS, tk=T)
    d_w_up = _weight_grad("dw_up", dup, u2, tm=DW_ROWS, tk=T)

    rows_in = sum(IN_SPLITS)
    dya, dyb, dy_attn, dy_hgrn, d_in = _merge_bwd(dh1b, w_out, w_ba, w_bh, ya, yb, pg, tm=TM, d_in_width=rows_in,
                                                  first=o_g)
    d_w_out = _weight_grad("dw_out", merged, dh1b, tm=1024, tk=1024)
    d_w_ba = _weight_grad("dw_branch_a", y_attn, dya, tm=1024, tk=1024)
    d_w_bh = _weight_grad("dw_branch_b", y_hgrn, dyb, tm=1024, tk=1024)

    names, swap = ("w_ffn_down",), ()
    (d_in, dsink), got = _attn_bwd(pq, pkv, sinks, lse, dy_attn, d_in,
                                   comm=net.exchange(dict(w_ffn_down=[d_w_down]), swap))
    net.received(names, swap, got)
    names, swap = ("w_ffn_gate", "w_ffn_up"), ("w_ffn_down",)
    (d_in, d_hgrn_norm, d_lb_logits), got = _hgrn_bwd(
        ph, o_raw, states, dy_hgrn, lb_logits, hgrn_norm_g, d_in, o_h,
        comm=net.exchange(dict(w_ffn_gate=[d_w_gate], w_ffn_up=[d_w_up]), swap))
    net.received(names, swap, got)

    names, swap = ("w_out", "w_branch_attn", "w_branch_hgrn"), ("w_ffn_gate", "w_ffn_up")
    (*d_w_in, d_b_in), got = _weight_grad(
        "dw_in", d_in, u, tm=DW_ROWS, tk=T, a_colsum=True, carrying=True,
        comm=net.exchange(dict(w_out=[d_w_out], w_branch_attn=[d_w_ba], w_branch_hgrn=[d_w_bh]), swap))
    net.received(names, swap, got)
    d_w_in = [tuple(d_w_in)]

    first_level = net.presum_begin("w_in", d_w_in)
    halves = net.presum_end("w_in", [] if first_level is None else _copies_alone("presum_swap_w_in", first_level))
    names, swap = ("w_in",), ("w_out", "w_branch_attn", "w_branch_hgrn")
    (dx, dg1_p), got = _in_proj_bwd([(d_in, 0)], w_in, x, norm_mix_g, dh1, tm=256,
                                    comm=_join(halves, net.swap(swap)))
    net.last = (names, swap, got)
    d_norm_mix = _colsum_partials(dg1_p)
    vecs = dict(norm_mix_g=d_norm_mix, b_in=d_b_in, attn_sinks=jnp.sum(dsink.reshape(Q_HEADS, ATTN_BLOCK), axis=1).reshape(1, Q_HEADS),
                hgrn_lb_logits=d_lb_logits,
                hgrn_norm_g=d_hgrn_norm, norm_ffn_g=d_norm_ffn, norm_final_g=d_norm_final)
    return loss, dx, vecs


def _place():
    return lax.axis_index("x"), lax.axis_index("y"), lax.axis_index("c")


def _other_chips(x, y):
    return [(1 - x, y), (x, 1 - y), (1 - x, 1 - y)]


def _y_first(copies):
    return [copies[3 * (i // 3) + (1, 0, 2)[i % 3]] for i in range(len(copies))]


def _gather_copies(shards):
    n = len(shards)

    def build(ins, outs, send_sems, recv_sems, local_sems):
        x, y, c = _place()
        mine = 2 * x + y
        local = [pltpu.make_async_copy(ins[w], outs[w].at[mine], local_sems.at[w]) for w in range(n)]
        sends, recvs = [], []
        for w in range(n):
            for k, (px, py) in enumerate(_other_chips(x, y)):
                sem = 3 * w + k
                sends.append(pltpu.make_async_remote_copy(
                    src_ref=ins[w], dst_ref=outs[w].at[mine], send_sem=send_sems.at[sem], recv_sem=recv_sems.at[sem],
                    device_id=(px, py, c), device_id_type=MESH_ID))
                recvs.append(pltpu.make_async_remote_copy(
                    src_ref=ins[w], dst_ref=outs[w].at[2 * px + py], send_sem=send_sems.at[sem],
                    recv_sem=recv_sems.at[sem], device_id=(px, py, c), device_id_type=MESH_ID))
        return sends, recvs, local, _y_first(sends)

    return _Carried(shards, [jax.ShapeDtypeStruct((N_CHIPS,) + s.shape, s.dtype) for s in shards], 3 * n, n, build)


def _grad_copies(stacked):
    n = len(stacked)

    def build(ins, outs, send_sems, recv_sems, local_sems):
        x, y, c = _place()
        sends = []
        for w in range(n):
            for k, (px, py) in enumerate(_other_chips(x, y)):
                sem = 3 * w + k
                sends.append(pltpu.make_async_remote_copy(
                    src_ref=ins[w].at[2 * px + py], dst_ref=outs[w].at[k], send_sem=send_sems.at[sem],
                    recv_sem=recv_sems.at[sem], device_id=(px, py, c), device_id_type=MESH_ID))
        return sends, sends, [], _y_first(sends)

    return _Carried(stacked, [jax.ShapeDtypeStruct((3,) + s.shape[1:], s.dtype) for s in stacked], 3 * n, 0, build)


def _small_copies(small):
    def build(ins, outs, send_sems, recv_sems, local_sems):
        small_ref, all_ref = ins[0], outs[0]
        x, y, c = _place()
        me = 4 * x + 2 * y + c
        sends, recvs = [], []
        for r in range(1, 8):
            px = 1 - x if r & 4 else x
            py = 1 - y if r & 2 else y
            pc = 1 - c if r & 1 else c
            sends.append(pltpu.make_async_remote_copy(
                src_ref=small_ref, dst_ref=all_ref.at[me], send_sem=send_sems.at[r - 1], recv_sem=recv_sems.at[r - 1],
                device_id=(px, py, pc), device_id_type=MESH_ID))
            recvs.append(pltpu.make_async_remote_copy(
                src_ref=small_ref, dst_ref=all_ref.at[4 * px + 2 * py + pc], send_sem=send_sems.at[r - 1],
                recv_sem=recv_sems.at[r - 1], device_id=(px, py, pc), device_id_type=MESH_ID))
        return sends, recvs, [pltpu.make_async_copy(small_ref, all_ref.at[me], local_sems.at[0])]

    return _Carried([small], [jax.ShapeDtypeStruct((8,) + small.shape, small.dtype)], 7, 1, build)


def _gather_by_neighbours(name, shard):
    half = shard.shape[0] // 2
    quarter = half // 2

    def body(in_ref, out_ref, send_sems, recv_sems, local_sem):
        for core in (0, 1):
            @pl.when(lax.axis_index("c") == core)
            def _():
                program(core, in_ref, out_ref, send_sems, recv_sems, local_sem)

    def program(c, in_ref, out_ref, send_sems, recv_sems, local_sem):
        x, y, _ = _place()
        chip = lambda px, py: 2 * px + py
        to_x, to_y, sibling = (1 - x, y, c), (x, 1 - y, c), (x, y, 1 - c)
        x_blk, y_blk, d_blk = chip(1 - x, y), chip(x, 1 - y), chip(1 - x, 1 - y)
        mine, theirs = c * half, (1 - c) * half

        def copy(sem, rows, block, to, src=None):
            place = out_ref.at[block, pl.ds(rows[0], rows[1])]
            return pltpu.make_async_remote_copy(
                src_ref=place if src is None else src, dst_ref=place, send_sem=send_sems.at[sem],
                recv_sem=recv_sems.at[sem], device_id=to, device_id_type=MESH_ID)

        own = pltpu.make_async_copy(in_ref, out_ref.at[chip(x, y)], local_sem)
        own.start()
        my_rows = in_ref.at[pl.ds(mine, half)]
        along_x = dict(send=copy(0, (mine, half), chip(x, y), to_x, src=my_rows),
                       landed=copy(0, (mine, half), x_blk, to_x),
                       onward=[copy(3, (mine + quarter, quarter), x_blk, to_y), copy(4, (mine, half), x_blk, sibling)],
                       diagonal=copy(2, (mine, quarter), d_blk, to_x))
        along_y = dict(send=copy(1, (mine, half), chip(x, y), to_y, src=my_rows),
                       landed=copy(1, (mine, half), y_blk, to_y),
                       onward=[copy(2, (mine, quarter), y_blk, to_x), copy(5, (mine, half), y_blk, sibling)],
                       diagonal=copy(3, (mine + quarter, quarter), d_blk, to_y))
        last = copy(6, (mine, half), d_blk, sibling)

        order = (along_x, along_y) if c == 0 else (along_y, along_x)
        for axis in order:
            axis["send"].start()
        for axis in order:
            axis["landed"].wait_recv()
            for cp in axis["onward"]:
                cp.start()
        for axis in order:
            axis["diagonal"].wait_recv()
        last.start()
        for sem, block in ((4, x_blk), (5, y_blk), (6, d_blk)):
            copy(sem, (theirs, half), block, sibling).wait_recv()
        for cp in [along_x["send"], along_y["send"]] + along_x["onward"] + along_y["onward"] + [last]:
            cp.wait_send()
        own.wait()

    return pl.pallas_call(
        body, name=name, in_specs=[HBM_SPEC], out_specs=HBM_SPEC,
        out_shape=jax.ShapeDtypeStruct((N_CHIPS,) + shard.shape, shard.dtype),
        scratch_shapes=[pltpu.SemaphoreType.DMA((7,)), pltpu.SemaphoreType.DMA((7,)), pltpu.SemaphoreType.DMA(())],
    )(shard)


def _copies_alone(name, comm):
    return _call(name, lambda: None, grid=(), in_specs=[], out_specs=[], out_shape=[], args=[], comm=comm)[1]


class _Net:
    def __init__(self, shards):
        self.shards = shards
        self.whole, self.own, self.theirs, self.sums, self.other = {}, {}, {}, {}, {}
        x, y, _ = _place()
        self.chip = 2 * x + y

    def gather(self, names):
        return _gather_copies([self.shards[n] for n in names])

    def gathered(self, names, got):
        for n, g in zip(names, got):
            self.whole[n] = g.reshape(-1, g.shape[-1])

    def full(self, name):
        return self.whole[name]

    def exchange(self, grads, swap=()):
        stacked = []
        for n, pieces in grads.items():
            (keep, send), = pieces
            self.own[n] = keep
            stacked.append(send.reshape(N_CHIPS, keep.shape[0] // N_CHIPS, send.shape[-1]))
        return _join(_grad_copies(stacked), self.swap(swap))

    def swap(self, names):
        return _sibling_copies([self.sums[n] for n in names]) if names else None

    def presum_begin(self, name, pieces):
        keep = jnp.concatenate([p[0] for p in pieces], axis=0) if len(pieces) > 1 else pieces[0][0]
        send = jnp.concatenate([p[1] for p in pieces], axis=0) if len(pieces) > 1 else pieces[0][1]
        rows = keep.shape[0] // N_CHIPS
        self.held = keep.reshape(N_CHIPS, rows, keep.shape[-1])
        return _half_rows_copies(send.reshape(N_CHIPS, rows, send.shape[-1]))

    def presum_end(self, name, got):
        x, y, c = _place()
        to_send, self.own[name] = _pre_sum("presum_" + name, self.held, got[0], jnp.stack([c, self.chip]))
        return _grad_copies([to_send])

    def received(self, names, swap, got, carried=None):
        self.theirs.update(zip(names, got[:len(names)]))
        self.other.update(zip(swap, got[len(names):]))
        for n in names:
            (self.sums[n],), more = _partial_sum("sum_" + n, self.own[n], self.theirs[n], self.chip, comm=carried)
        return more


def _half_rows_copies(stacked):
    n, rows = stacked.shape[0], stacked.shape[1] // 2

    def build(ins, outs, send_sems, recv_sems, local_sems):
        x, y, c = _place()
        copies = [pltpu.make_async_remote_copy(
            src_ref=ins[0].at[s, pl.ds((1 - c) * rows, rows)], dst_ref=outs[0].at[s], send_sem=send_sems.at[s],
            recv_sem=recv_sems.at[s], device_id=(x, y, 1 - c), device_id_type=MESH_ID) for s in range(n)]
        return copies, copies, []

    return _Carried([stacked], [jax.ShapeDtypeStruct((n, rows, stacked.shape[2]), stacked.dtype)], n, 0, build)


def _pre_sum(name, held, theirs, core_and_chip):
    n, R, C = held.shape
    half = R // 2
    tr = _row_tile(half)
    per_half = half // tr

    def body(place_ref, h_ref, t_ref, send_ref, own_ref):
        total = h_ref[0] + t_ref[0].astype(F32)
        send_ref[0] = total.astype(send_ref.dtype)

        @pl.when(pl.program_id(1) == place_ref[1])
        def _():
            own_ref[...] = total

    return pl.pallas_call(
        body, name=name,
        grid_spec=pltpu.PrefetchScalarGridSpec(
            num_scalar_prefetch=1, grid=(per_half, n),
            in_specs=[pl.BlockSpec((1, tr, C), lambda i, s, place: (s, place[0] * per_half + i, 0)),
                      pl.BlockSpec((1, tr, C), lambda i, s, place: (s, i, 0))],
            out_specs=[pl.BlockSpec((1, tr, C), lambda i, s, place: (s, i, 0)),
                       pl.BlockSpec((tr, C), lambda i, s, place: (i, 0))]),
        out_shape=[jax.ShapeDtypeStruct((n, half, C), MXU_DTYPE), jax.ShapeDtypeStruct((half, C), F32)],
        compiler_params=_params(("arbitrary", "arbitrary")),
    )(core_and_chip, held, theirs)


def _sibling_copies(parts):
    n = len(parts)

    def build(ins, outs, send_sems, recv_sems, local_sems):
        x, y, c = _place()
        copies = [pltpu.make_async_remote_copy(
            src_ref=ins[w], dst_ref=outs[w], send_sem=send_sems.at[w], recv_sem=recv_sems.at[w],
            device_id=(x, y, 1 - c), device_id_type=MESH_ID) for w in range(n)]
        return copies, copies, []

    return _Carried(parts, [jax.ShapeDtypeStruct(p.shape, p.dtype) for p in parts], n, 0, build)


def _row_tile(rows, most=512, sublanes=16):
    return max(t for t in range(sublanes, min(most, rows // 2) + 1, sublanes) if rows % t == 0)


def _partial_sum(name, own, recv, chip, comm=None):
    _, R, C = recv.shape
    tr = _row_tile(R)

    def body(o_ref, r_ref, p_ref):
        p_ref[...] = ((o_ref[...] + r_ref[0].astype(F32)) + r_ref[1].astype(F32)) + r_ref[2].astype(F32)

    if own.shape[0] != R:
        assert comm is None and own.shape[0] == N_CHIPS * R
        total = pl.pallas_call(
            lambda chip_ref, *refs: body(*refs), name=name,
            grid_spec=pltpu.PrefetchScalarGridSpec(
                num_scalar_prefetch=1, grid=(R // tr,),
                in_specs=[pl.BlockSpec((tr, C), lambda i, chip_ref: (chip_ref[0] * (R // tr) + i, 0)),
                          pl.BlockSpec((3, tr, C), lambda i, chip_ref: (0, i, 0))],
                out_specs=pl.BlockSpec((tr, C), lambda i, chip_ref: (i, 0))),
            out_shape=jax.ShapeDtypeStruct((R, C), F32), compiler_params=_params(("parallel",)),
        )(chip.reshape(1), own, recv)
        return [total], []
    return _call(name, body, grid=(R // tr,),
                 in_specs=[pl.BlockSpec((tr, C), lambda i: (i, 0)), pl.BlockSpec((3, tr, C), lambda i: (0, i, 0))],
                 out_specs=[pl.BlockSpec((tr, C), lambda i: (i, 0))], out_shape=[jax.ShapeDtypeStruct((R, C), F32)],
                 args=[own, recv], semantics=("parallel",), comm=comm)


def _adam_vals(w, g, m, v):
    m = ADAM_B1 * m + (1.0 - ADAM_B1) * g
    v = ADAM_B2 * v + (1.0 - ADAM_B2) * (g * g)
    m_hat = m / (1.0 - ADAM_B1 ** ADAM_STEP)
    v_hat = v / (1.0 - ADAM_B2 ** ADAM_STEP)
    delta = -ADAM_LR * (m_hat / (jnp.sqrt(v_hat) + ADAM_EPS) + ADAM_WD * w)
    return delta, m, v


def _adamw(name, w, m, v, mine, other, comm=None):
    R, C = w.shape
    tr = _row_tile(R)

    def body(w_ref, m_ref, v_ref, s_ref, n_ref, g_ref, d_ref, nm_ref, nv_ref):
        g = s_ref[...] + n_ref[...]
        d, nm, nv = _adam_vals(w_ref[...], g, m_ref[...], v_ref[...])
        g_ref[...], d_ref[...], nm_ref[...], nv_ref[...] = g, d, nm, nv

    spec = pl.BlockSpec((tr, C), lambda i: (i, 0))
    return _call(name, body, grid=(R // tr,), in_specs=[spec] * 5, out_specs=[spec] * 4,
                 out_shape=[jax.ShapeDtypeStruct((R, C), F32)] * 4, args=[w, m, v, mine, other],
                 semantics=("parallel",), comm=comm)


def _adamw_by_halves(name, w, m, v, mine, other, core):
    R, C = w.shape
    tr = _row_tile(R // 2)
    per_half = R // 2 // tr

    def body(c_ref, w_ref, m_ref, v_ref, s_ref, n_ref, g_ref, d_ref, nm_ref, nv_ref):
        g = jnp.where(pl.program_id(0) // per_half == c_ref[0, 0], s_ref[...], n_ref[...])
        d, nm, nv = _adam_vals(w_ref[...], g, m_ref[...], v_ref[...])
        g_ref[...], d_ref[...], nm_ref[...], nv_ref[...] = g, d, nm, nv

    spec = pl.BlockSpec((tr, C), lambda i: (i, 0))
    part = pl.BlockSpec((tr, C), lambda i: (i % per_half, 0))
    return pl.pallas_call(
        body, name=name, grid=(R // tr,),
        in_specs=[pl.BlockSpec(memory_space=pltpu.SMEM), spec, spec, spec, part, part], out_specs=[spec] * 4,
        out_shape=[jax.ShapeDtypeStruct((R, C), F32)] * 4, compiler_params=_params(("parallel",)),
    )(core, w, m, v, mine, other)


SMALL_LAYOUT = dict(norm_mix_g=(0, 1, 1024), b_in=(1, 8, 7424), hgrn_norm_g=(9, 1, 1024), norm_ffn_g=(10, 1, 1024),
                    norm_final_g=(11, 1, 1024), hgrn_lb_logits=(12, 2, 2048), attn_sinks=(14, 1, 16))
SMALL_LOSS_ROW, SMALL_ROWS = 15, 16


def _pack_small(grads, loss):
    rows = [jnp.pad(grads[name].astype(F32).reshape(-1), (0, nrows * D_MODEL - n))
            for name, (_, nrows, n) in SMALL_LAYOUT.items()]
    rows.append(jnp.pad(loss.astype(F32).reshape(1), (0, D_MODEL - 1)))
    return jnp.concatenate(rows).reshape(SMALL_ROWS, D_MODEL)


def _adamw_small(w, m, v, g_all):
    names = list(SMALL_LAYOUT)
    n = len(names)

    def body(a_ref, *refs):
        ins, outs = refs[:3 * n], refs[3 * n:]
        g_all_rows = a_ref[0]
        for dev in range(1, 8):
            g_all_rows = g_all_rows + a_ref[dev]
        for i, name in enumerate(names):
            first, nrows, count = SMALL_LAYOUT[name]
            w_ref, m_ref, v_ref = ins[3 * i:3 * i + 3]
            if w_ref.shape[0] == nrows:
                g = g_all_rows[first:first + nrows, :w_ref.shape[1]]
            else:
                last = count - (nrows - 1) * D_MODEL
                g = jnp.concatenate([g_all_rows[r:r + 1, :] for r in range(first, first + nrows - 1)]
                                    + [g_all_rows[first + nrows - 1:first + nrows, :last]], axis=1)
            d, nm, nv = _adam_vals(w_ref[...], g, m_ref[...], v_ref[...])
            for o_ref, val in zip(outs[4 * i:4 * i + 4], (g, d, nm, nv)):
                o_ref[...] = val
        outs[4 * n][...] = g_all_rows[SMALL_LOSS_ROW:SMALL_LOSS_ROW + 1, 0:1]

    res = pl.pallas_call(
        body, name="adamw_small",
        out_shape=[jax.ShapeDtypeStruct(w[name].shape, F32) for name in names for _ in range(4)]
        + [jax.ShapeDtypeStruct((1, 1), F32)],
    )(g_all, *[t[name] for name in names for t in (w, m, v)])
    return {name: res[4 * i:4 * i + 4] for i, name in enumerate(names)}, res[4 * n]


MATRICES = ("w_in", "w_branch_attn", "w_branch_hgrn", "w_out", "w_ffn_gate", "w_ffn_up", "w_ffn_down")
COLUMN_SHARDED = ("w_in", "w_ffn_gate", "w_ffn_up")
WEIGHTS = ("norm_mix_g", "w_in", "b_in", "attn_sinks", "hgrn_lb_logits", "hgrn_norm_g", "w_branch_attn",
           "w_branch_hgrn", "w_out", "norm_ffn_g", "w_ffn_gate", "w_ffn_up", "w_ffn_down", "norm_final_g")


def kernel(x, norm_mix_g, w_in, b_in, attn_sinks, hgrn_lb_logits, hgrn_norm_g, w_branch_attn, w_branch_hgrn, w_out, norm_ffn_g, w_ffn_gate, w_ffn_up, w_ffn_down, norm_final_g, loss_target, m_norm_mix_g, m_w_in, m_b_in, m_attn_sinks, m_hgrn_lb_logits, m_hgrn_norm_g, m_w_branch_attn, m_w_branch_hgrn, m_w_out, m_norm_ffn_g, m_w_ffn_gate, m_w_ffn_up, m_w_ffn_down, m_norm_final_g, v_norm_mix_g, v_w_in, v_b_in, v_attn_sinks, v_hgrn_lb_logits, v_hgrn_norm_g, v_w_branch_attn, v_w_branch_hgrn, v_w_out, v_norm_ffn_g, v_w_ffn_gate, v_w_ffn_up, v_w_ffn_down, v_norm_final_g):
    given = dict(locals())
    w = {n: given[n] for n in WEIGHTS}
    m = {n: given["m_" + n] for n in WEIGHTS}
    v = {n: given["v_" + n] for n in WEIGHTS}

    block = lambda a, n: jnp.transpose(a[0]) if n in COLUMN_SHARDED else a[0]
    unblock = lambda a, n: (jnp.transpose(a) if n in COLUMN_SHARDED else a)[None]
    net = _Net({n: block(w[n], n).astype(MXU_DTYPE) for n in MATRICES})
    net.gathered(("w_in",), [_gather_by_neighbours("gather_w_in", net.shards["w_in"])])
    vec = dict(norm_mix_g=norm_mix_g, b_in=b_in, attn_sinks=attn_sinks, hgrn_lb_logits=hgrn_lb_logits,
               hgrn_norm_g=hgrn_norm_g, norm_ffn_g=norm_ffn_g, norm_final_g=norm_final_g.reshape(1, D_MODEL))
    loss_part, dx, d_vecs = _local_step(x[0], loss_target[0], vec, net)

    small_all, = net.received(*net.last, carried=_small_copies(_pack_small(d_vecs, loss_part)))
    grads, deltas, new_m, new_v = {}, {}, {}, {}
    for n in ("w_ffn_down", "w_ffn_gate", "w_ffn_up", "w_out", "w_branch_attn", "w_branch_hgrn"):
        res, got = _adamw("adamw_" + n, block(w[n], n), block(m[n], n), block(v[n], n), net.sums[n], net.other[n],
                          comm=net.swap(("w_in",)) if n == "w_ffn_down" else None)
        if n == "w_ffn_down":
            net.other["w_in"], = got
        grads[n], deltas[n], new_m[n], new_v[n] = (unblock(r, n) for r in res)
    n = "w_in"
    res = _adamw_by_halves("adamw_" + n, block(w[n], n), block(m[n], n), block(v[n], n), net.sums[n], net.other[n],
                           _place()[2].reshape(1, 1))
    grads[n], deltas[n], new_m[n], new_v[n] = (unblock(r, n) for r in res)
    rows = lambda t: {n: t[n].reshape(-1, t[n].shape[-1]) for n in SMALL_LAYOUT}
    res, loss = _adamw_small(rows(w), rows(m), rows(v), small_all)
    for n, four in res.items():
        grads[n], deltas[n], new_m[n], new_v[n] = (r.reshape(w[n].shape) for r in four)
    loss = loss.reshape(())
    return (loss, dx[None], *[grads[n] for n in WEIGHTS], *[deltas[n] for n in WEIGHTS],
            *[new_m[n] for n in WEIGHTS], *[new_v[n] for n in WEIGHTS])
```

```python
import collections
import functools
import math

import jax
import jax.numpy as jnp
from jax import lax
from jax.experimental import pallas as pl
from jax.experimental.pallas import tpu as pltpu

F32 = jnp.float32
BF16 = jnp.bfloat16
MXU_DTYPE = jnp.bfloat16
SAVED_DTYPE = jnp.bfloat16
MESH_ID = pl.DeviceIdType.MESH

D_MODEL = 1024
HEAD_DIM = 64
Q_HEADS = 16
KV_HEADS = 2
GROUP = Q_HEADS // KV_HEADS
KV_WIDTH = KV_HEADS * HEAD_DIM
ATTN_BLOCK = 128
HGRN_HEADS = 8
HGRN_K = 128
CHUNK = 64
HGRN_TOKENS = 256
FFN = 2816
IN_SPLITS = (1024, 256, 4096, 2048)
EPS = 1e-6
NEG_INF = -1e30
ADAM_LR, ADAM_B1, ADAM_B2, ADAM_EPS, ADAM_WD, ADAM_STEP = 0.001, 0.9, 0.999, 1e-08, 0.01, 10
N_CHIPS = 4
VMEM_LIMIT = 60 * 1024 * 1024
ROW_ALIGN = 16
DW_ROWS = 256


def _params(sem=None):
    return pltpu.CompilerParams(dimension_semantics=sem, vmem_limit_bytes=VMEM_LIMIT)


def _sigmoid(v):
    return 0.5 * jnp.tanh(0.5 * v) + 0.5


def _dot(a, b, dims):
    return lax.dot_general(a.astype(MXU_DTYPE), b.astype(MXU_DTYPE), (dims, ((), ())),
                           preferred_element_type=F32)


def _nn(a, b):
    return _dot(a, b, ((1,), (0,)))


def _nt(a, b):
    return _dot(a, b, ((1,), (1,)))


def _tn(a, b):
    return _dot(a, b, ((0,), (0,)))


HBM_SPEC = pl.BlockSpec(memory_space=pl.ANY)


class _Carried:
    def __init__(self, arrays, out_shapes, n_remote, n_local, build):
        self.parts = [(len(arrays), len(out_shapes), build)]
        self.arrays, self.out_shapes = list(arrays), list(out_shapes)
        self.scratch = [pltpu.SemaphoreType.DMA((n_remote,)), pltpu.SemaphoreType.DMA((n_remote,)),
                        pltpu.SemaphoreType.DMA((max(n_local, 1),))]

    def __add__(self, other):
        both = _Carried([], [], 1, 0, None)
        both.parts = self.parts + other.parts
        both.arrays, both.out_shapes = self.arrays + other.arrays, self.out_shapes + other.out_shapes
        both.scratch = self.scratch + other.scratch
        return both

    def _built(self, ins, outs, sems):
        for p, (ni, no, build) in enumerate(self.parts):
            yield build(ins[:ni], outs[:no], *sems[3 * p:3 * p + 3])
            ins, outs = ins[ni:], outs[no:]

    def start(self, ins, outs, sems):
        core = lax.axis_index("c")
        for sends, _, local, *other_order in self._built(ins, outs, sems):
            for cp in local:
                cp.start()
            if not other_order:
                for cp in sends:
                    cp.start()
                continue

            @pl.when(core == 0)
            def _():
                for cp in sends:
                    cp.start()

            @pl.when(core == 1)
            def _():
                for cp in other_order[0]:
                    cp.start()

    def wait(self, ins, outs, sems):
        for sends, recvs, local, *_ in self._built(ins, outs, sems):
            for cp in recvs:
                cp.wait_recv()
            for cp in sends:
                cp.wait_send()
            for cp in local:
                cp.wait()


def _join(*comms):
    comms = [c for c in comms if c is not None]
    return functools.reduce(lambda a, b: a + b, comms) if comms else None


def _call(name, body, *, grid, in_specs, out_specs, out_shape, args, scratch=(), semantics=None, comm=None,
          aliases=None):
    n_in, n_out, n_scr = len(in_specs), len(out_specs), len(scratch)
    aliases = aliases or {}
    if comm is None:
        res = pl.pallas_call(body, name=name, grid=grid, in_specs=in_specs, out_specs=out_specs, out_shape=out_shape,
                             scratch_shapes=list(scratch), input_output_aliases=aliases,
                             compiler_params=_params(semantics))(*args)
        return list(res), []
    ci, co = len(comm.arrays), len(comm.out_shapes)

    def carrying(*refs):
        ins, refs = refs[:n_in], refs[n_in:]
        c_ins, refs = refs[:ci], refs[ci:]
        outs, refs = refs[:n_out], refs[n_out:]
        c_outs, refs = refs[:co], refs[co:]
        scr, sems = refs[:n_scr], refs[n_scr:]
        if not grid:
            comm.start(c_ins, c_outs, sems)
            body(*ins, *outs, *scr)
            comm.wait(c_ins, c_outs, sems)
            return
        first = functools.reduce(jnp.logical_and, [pl.program_id(a) == 0 for a in range(len(grid))])
        last = functools.reduce(jnp.logical_and, [pl.program_id(a) == g - 1 for a, g in enumerate(grid)])

        @pl.when(first)
        def _():
            comm.start(c_ins, c_outs, sems)

        body(*ins, *outs, *scr)

        @pl.when(last)
        def _():
            comm.wait(c_ins, c_outs, sems)

    res = pl.pallas_call(
        carrying, name=name, grid=grid, in_specs=list(in_specs) + [HBM_SPEC] * ci,
        out_specs=list(out_specs) + [HBM_SPEC] * co, out_shape=list(out_shape) + comm.out_shapes,
        scratch_shapes=list(scratch) + comm.scratch, input_output_aliases=aliases,
        compiler_params=_params(("arbitrary",) * len(grid) if grid else None),
    )(*args, *comm.arrays)
    return list(res[:n_out]), list(res[n_out:])


_Cols = collections.namedtuple("_Cols", "array first cols")
_Into = collections.namedtuple("_Into", "rows first held")


def _weight_grad(name, a, b, *, tm, tk, a_colsum=False, into=None, carrying=False, comm=None):
    cols = a if isinstance(a, _Cols) else _Cols(a, 0, a.shape[1])
    a = cols.array
    (T, N), M = b.shape, cols.cols
    tk = min(tk, T)
    assert cols.first % tm == 0 and M % tm == 0 and T % tk == 0, (name, cols.first, M, tm, T, tk)
    ni, nk, tile0 = M // tm, T // tk, cols.first // tm
    held = list(into.held) if into is not None and into.held is not None else []

    def body(a_ref, b_ref, *rest):
        keep_ref, send_ref = rest[len(held):len(held) + 2]
        sums_ref = rest[len(held) + 2] if a_colsum else None
        if nk == 1:
            acc = _tn(a_ref[...], b_ref[...])
            keep_ref[...], send_ref[...] = acc, acc.astype(send_ref.dtype)
            if a_colsum:
                sums_ref[...] = jnp.sum(a_ref[...].astype(F32), axis=0, keepdims=True)
            return
        acc_ref = rest[-1]
        k = pl.program_id(1)

        @pl.when(k == 0)
        def _():
            acc_ref[...] = jnp.zeros_like(acc_ref)
            if a_colsum:
                sums_ref[...] = jnp.zeros((1, tm), F32)

        if a_colsum:
            sums_ref[...] += jnp.sum(a_ref[...].astype(F32), axis=0, keepdims=True)
        acc_ref[...] += _tn(a_ref[...], b_ref[...])

        @pl.when(k == nk - 1)
        def _():
            keep_ref[...], send_ref[...] = acc_ref[...], acc_ref[...].astype(send_ref.dtype)

    if into is None:
        rows, out_spec = M, pl.BlockSpec((tm, N), lambda i, k: (i, 0))
    else:
        rows = into.rows
        out_spec = pl.BlockSpec((pl.Element(tm), pl.Element(N)),
                                lambda i, k: (pl.multiple_of(into.first + i * tm, ROW_ALIGN), 0))
    out_shape = [jax.ShapeDtypeStruct((rows, N), F32), jax.ShapeDtypeStruct((rows, N), MXU_DTYPE)]
    out_specs = [out_spec, out_spec]
    if a_colsum:
        out_shape.append(jax.ShapeDtypeStruct((1, M), F32))
        out_specs.append(pl.BlockSpec((1, tm), lambda i, k: (0, i)))
    res, got = _call(
        name, body, grid=(ni, nk),
        in_specs=[pl.BlockSpec((tk, tm), lambda i, k: (k, tile0 + i)), pl.BlockSpec((tk, N), lambda i, k: (k, 0))]
        + [HBM_SPEC] * len(held),
        out_specs=out_specs, out_shape=out_shape, scratch=[pltpu.VMEM((tm, N), F32)] if nk > 1 else [],
        args=[a, b] + held, aliases={2 + p: p for p in range(len(held))}, semantics=("parallel", "arbitrary"),
        comm=comm)
    return (res, got) if carrying else res


def _ffn_fwd(merged, w_out, x, gain, w_gate_t, w_up_t, *, tm, comm=None):
    (T, D), F = x.shape, w_gate_t.shape[0]

    def body(m_ref, wo_ref, x_ref, g_ref, wg_ref, wu_ref, h_ref, u_ref, gate_ref, up_ref, z_ref):
        h = x_ref[...] + _nn(m_ref[...], wo_ref[...])
        h_ref[...] = h
        u = (h * lax.rsqrt(jnp.mean(h * h, axis=-1, keepdims=True) + EPS) * g_ref[...]).astype(u_ref.dtype)
        u_ref[...] = u
        gate, up = _nt(u, wg_ref[...]), _nt(u, wu_ref[...])
        gate_ref[...], up_ref[...] = gate.astype(gate_ref.dtype), up.astype(up_ref.dtype)
        z_ref[...] = (gate * _sigmoid(gate) * up).astype(z_ref.dtype)

    rows = lambda n: pl.BlockSpec((tm, n), lambda i: (i, 0))
    fixed = _fixed_spec
    return _call("ffn_hidden", body, grid=(T // tm,),
                 in_specs=[rows(D), fixed(w_out), rows(D), fixed(gain), fixed(w_gate_t), fixed(w_up_t)],
                 out_specs=[rows(D), rows(D), rows(F), rows(F), rows(F)],
                 out_shape=[jax.ShapeDtypeStruct((T, D), F32), jax.ShapeDtypeStruct((T, D), MXU_DTYPE)]
                 + [jax.ShapeDtypeStruct((T, F), SAVED_DTYPE)] * 2 + [jax.ShapeDtypeStruct((T, F), MXU_DTYPE)],
                 args=[merged, w_out, x, gain, w_gate_t, w_up_t], semantics=("parallel",), comm=comm)


def _in_proj(x, gain, w_in_t, b_in, *, tm, comm=None):
    T, D = x.shape
    bounds = [sum(IN_SPLITS[:i]) for i in range(len(IN_SPLITS) + 1)]

    def body(x_ref, g_ref, w_ref, b_ref, u_ref, *piece_refs):
        xv = x_ref[...]
        r = lax.rsqrt(jnp.mean(xv * xv, axis=-1, keepdims=True) + EPS)
        u = (xv * r * g_ref[...]).astype(u_ref.dtype)
        u_ref[...] = u
        for o_ref, lo, hi in zip(piece_refs, bounds[:-1], bounds[1:]):
            o_ref[...] = (_nt(u, w_ref[lo:hi, :]) + b_ref[:, lo:hi]).astype(o_ref.dtype)

    rows = lambda n: pl.BlockSpec((tm, n), lambda i: (i, 0))
    fixed = _fixed_spec
    dtypes = (MXU_DTYPE, MXU_DTYPE, F32, F32)
    return _call("in_proj", body, grid=(T // tm,),
                 in_specs=[rows(D), fixed(gain), fixed(w_in_t), fixed(b_in)],
                 out_specs=[rows(D)] + [rows(n) for n in IN_SPLITS],
                 out_shape=[jax.ShapeDtypeStruct((T, D), MXU_DTYPE)]
                 + [jax.ShapeDtypeStruct((T, n), dt) for n, dt in zip(IN_SPLITS, dtypes)],
                 args=[x, gain, w_in_t, b_in], semantics=("parallel",), comm=comm)


def _row_spec(tm, n):
    return pl.BlockSpec((tm, n), lambda i: (i, 0))


def _fixed_spec(a):
    return pl.BlockSpec(a.shape, lambda i: (0,) * a.ndim, pipeline_mode=pl.Buffered(1))


def _partials_spec(n):
    return pl.BlockSpec((8, n), lambda i: (i, 0))


def _row_parts(tm, parts):
    assert tm % parts == 0, (tm, parts)
    return [slice(p * (tm // parts), (p + 1) * (tm // parts)) for p in range(parts)]


def _ffn_tail(z, w_down, h1, target, gain, gate, up, *, tm, parts=2):
    (T, F), D = z.shape, h1.shape[1]

    def body(z_ref, w_ref, h_ref, t_ref, g_ref, gate_ref, up_ref, dh_ref, dhb_ref, dgate_ref, dup_ref, dg_ref, l_ref):
        part, dgain = 0.0, 0.0
        pieces = _row_parts(tm, parts)
        h2s = [h_ref[rows, :] + _nn(z_ref[rows, :], w_ref[...]) for rows in pieces]
        for rows, h2 in zip(pieces, h2s):
            r = lax.rsqrt(jnp.mean(h2 * h2, axis=-1, keepdims=True) + EPS)
            xhat = h2 * r
            err = xhat * g_ref[...] - t_ref[rows, :]
            part += 0.5 * jnp.sum(jnp.sum(err * err, axis=-1, keepdims=True), axis=0, keepdims=True) / D
            dy = err / D
            dxh = dy * g_ref[...]
            dh2 = r * (dxh - xhat * jnp.mean(dxh * xhat, axis=-1, keepdims=True))
            dh_ref[rows, :] = dh2
            dhb = dh2.astype(dhb_ref.dtype)
            dhb_ref[rows, :] = dhb
            dgain += jnp.sum(dy * xhat, axis=0, keepdims=True)
            dz = _nt(dhb, w_ref[...])
            gv, upv = gate_ref[rows, :].astype(F32), up_ref[rows, :].astype(F32)
            s = _sigmoid(gv)
            dgate_ref[rows, :] = (dz * upv * (s * (1.0 + gv * (1.0 - s)))).astype(dgate_ref.dtype)
            dup_ref[rows, :] = (dz * (gv * s)).astype(dup_ref.dtype)
        dg_ref[...] = jnp.broadcast_to(dgain, dg_ref.shape)
        l_ref[...] = jnp.broadcast_to(part, l_ref.shape)

    low = lambda n: jax.ShapeDtypeStruct((T, n), MXU_DTYPE)
    part = jax.ShapeDtypeStruct((8 * (T // tm), D), F32)
    return pl.pallas_call(
        body, name="ffn_tail", grid=(T // tm,),
        in_specs=[_row_spec(tm, F), _fixed_spec(w_down), _row_spec(tm, D), _row_spec(tm, D), _fixed_spec(gain),
                  _row_spec(tm, F), _row_spec(tm, F)],
        out_specs=[_row_spec(tm, D), _row_spec(tm, D), _row_spec(tm, F), _row_spec(tm, F), _partials_spec(D),
                   _partials_spec(D)],
        out_shape=[jax.ShapeDtypeStruct((T, D), F32), low(D), low(F), low(F), part, part],
        compiler_params=_params(("parallel",)),
    )(z, w_down, h1, target, gain, gate, up)


def _ffn_in_bwd(dgate, dup, w_gate_t, w_up_t, h1, gain, dres, *, tm, comm=None):
    (T, F), D = dgate.shape, h1.shape[1]

    def body(dg_ref, du_ref, wg_ref, wu_ref, h_ref, g_ref, r_ref, dh_ref, dhb_ref, dgain_ref):
        d_u2 = _nn(dg_ref[...], wg_ref[...]) + _nn(du_ref[...], wu_ref[...])
        dx, dgain = _rmsnorm_bwd_vals(d_u2, h_ref[...], g_ref[...])
        dh = r_ref[...] + dx
        dh_ref[...] = dh
        dhb_ref[...] = dh.astype(dhb_ref.dtype)
        dgain_ref[...] = jnp.broadcast_to(dgain, dgain_ref.shape)

    return _call("d_ffn_in", body, grid=(T // tm,),
                 in_specs=[_row_spec(tm, F), _row_spec(tm, F), _fixed_spec(w_gate_t), _fixed_spec(w_up_t),
                           _row_spec(tm, D), _fixed_spec(gain), _row_spec(tm, D)],
                 out_specs=[_row_spec(tm, D), _row_spec(tm, D), _partials_spec(D)],
                 out_shape=[jax.ShapeDtypeStruct((T, D), F32), jax.ShapeDtypeStruct((T, D), MXU_DTYPE),
                            jax.ShapeDtypeStruct((8 * (T // tm), D), F32)],
                 args=[dgate, dup, w_gate_t, w_up_t, h1, gain, dres], semantics=("parallel",), comm=comm)


def _in_proj_bwd(pieces, w_in_t, x, gain, dres, *, tm, comm=None):
    T, D = x.shape
    n = len(pieces)

    def body(*refs):
        dps, (w_ref, x_ref, g_ref, r_ref, dx_ref, dgain_ref) = refs[:n], refs[n:]
        d_u = None
        for dp_ref, (dp, first) in zip(dps, pieces):
            term = _nn(dp_ref[...], w_ref[first:first + dp.shape[1], :])
            d_u = term if d_u is None else d_u + term
        dx, dgain = _rmsnorm_bwd_vals(d_u, x_ref[...], g_ref[...])
        dx_ref[...] = r_ref[...] + dx
        dgain_ref[...] = jnp.broadcast_to(dgain, dgain_ref.shape)

    return _call("d_u", body, grid=(T // tm,),
                 in_specs=[_row_spec(tm, dp.shape[1]) for dp, _ in pieces]
                 + [_fixed_spec(w_in_t), _row_spec(tm, D), _fixed_spec(gain), _row_spec(tm, D)],
                 out_specs=[_row_spec(tm, D), _partials_spec(D)],
                 out_shape=[jax.ShapeDtypeStruct((T, D), F32), jax.ShapeDtypeStruct((8 * (T // tm), D), F32)],
                 args=[dp for dp, _ in pieces] + [w_in_t, x, gain, dres], semantics=("parallel",), comm=comm)


def _merge_fwd(y_a, y_b, w_a, w_b, gates, *, tm, comm=None):
    T, D = y_a.shape

    def body(ya_ref, yb_ref, wa_ref, wb_ref, ga_ref, gb_ref, pa_ref, pb_ref, m_ref):
        pa, pb = _nn(ya_ref[...], wa_ref[...]), _nn(yb_ref[...], wb_ref[...])
        pa_ref[...], pb_ref[...] = pa.astype(pa_ref.dtype), pb.astype(pb_ref.dtype)
        m_ref[...] = (_sigmoid(ga_ref[...]) * pa + _sigmoid(gb_ref[...]) * pb).astype(m_ref.dtype)

    rows = pl.BlockSpec((tm, D), lambda i: (i, 0))
    whole = pl.BlockSpec((D, D), lambda i: (0, 0), pipeline_mode=pl.Buffered(1))
    return _call("branch_merge", body, grid=(T // tm,),
                 in_specs=[rows, rows, whole, whole, rows, pl.BlockSpec((tm, D), lambda i: (i, 1))],
                 out_specs=[rows] * 3,
                 out_shape=[jax.ShapeDtypeStruct((T, D), SAVED_DTYPE)] * 2 + [jax.ShapeDtypeStruct((T, D), MXU_DTYPE)],
                 args=[y_a, y_b, w_a, w_b, gates, gates], semantics=("parallel",), comm=comm)


def _merge_bwd(dh, w_out, w_a, w_b, p_a, p_b, gates, *, tm, d_in_width, first):
    T, D = dh.shape

    def body(dh_ref, wo_ref, wa_ref, wb_ref, pa_ref, pb_ref, ga_ref, gb_ref, dpa_ref, dpb_ref, dya_ref, dyb_ref,
             din_ref):
        dm = _nt(dh_ref[...], wo_ref[...])
        sa, sb = _sigmoid(ga_ref[...]), _sigmoid(gb_ref[...])
        dpa, dpb = (dm * sa).astype(dpa_ref.dtype), (dm * sb).astype(dpb_ref.dtype)
        dpa_ref[...], dpb_ref[...] = dpa, dpb
        din_ref[:, :D] = (dm * pa_ref[...].astype(F32) * sa * (1.0 - sa)).astype(din_ref.dtype)
        din_ref[:, D:] = (dm * pb_ref[...].astype(F32) * sb * (1.0 - sb)).astype(din_ref.dtype)
        dya_ref[...] = _nt(dpa, wa_ref[...]).astype(dya_ref.dtype)
        dyb_ref[...] = _nt(dpb, wb_ref[...]).astype(dyb_ref.dtype)

    rows = pl.BlockSpec((tm, D), lambda i: (i, 0))
    whole = pl.BlockSpec((D, D), lambda i: (0, 0), pipeline_mode=pl.Buffered(1))
    low = jax.ShapeDtypeStruct((T, D), MXU_DTYPE)
    return pl.pallas_call(
        body, name="d_branch_merge", grid=(T // tm,),
        in_specs=[rows, whole, whole, whole, rows, rows, rows, pl.BlockSpec((tm, D), lambda i: (i, 1))],
        out_specs=[rows] * 4 + [pl.BlockSpec((pl.Element(tm), pl.Element(2 * D)), lambda i: (
            pl.multiple_of(i * tm, ROW_ALIGN), first))],
        out_shape=[low] * 3 + [jax.ShapeDtypeStruct((T, D), F32), jax.ShapeDtypeStruct((T, d_in_width), MXU_DTYPE)],
        compiler_params=_params(("parallel",)),
    )(dh, w_out, w_a, w_b, p_a, p_b, gates, gates)


def _colsum_partials(p):
    return jnp.sum(p.reshape(-1, 8, p.shape[-1])[:, 0, :], axis=0, keepdims=True)


def _rmsnorm_bwd_vals(dy, xin, g):
    rstd = lax.rsqrt(jnp.mean(xin * xin, axis=-1, keepdims=True) + EPS)
    xhat = xin * rstd
    dg = jnp.sum(dy * xhat, axis=0, keepdims=True)
    dxh = dy * g
    dx = rstd * (dxh - xhat * jnp.mean(dxh * xhat, axis=-1, keepdims=True))
    return dx, dg


ATTN_SCALE = 1.0 / math.sqrt(HEAD_DIM)
GROUP_LANES = GROUP * ATTN_BLOCK
PAIR = 2 * HEAD_DIM


def _attn_mask():
    kj = lax.broadcasted_iota(jnp.int32, (ATTN_BLOCK, GROUP_LANES), 0)
    qi = lax.broadcasted_iota(jnp.int32, (ATTN_BLOCK, GROUP_LANES), 1) & (ATTN_BLOCK - 1)
    return kj <= qi


def _heads_transposed(ref, g, scale=None):
    parts = []
    for a in range(GROUP // 2):
        lo = (g * GROUP // 2 + a) * PAIR
        pair = ref[:, lo:lo + PAIR].astype(F32)
        pair = (pair if scale is None else pair * scale).T
        parts += [pair[:HEAD_DIM], pair[HEAD_DIM:]]
    return jnp.concatenate(parts, axis=1).astype(MXU_DTYPE)


def _heads_back(ref, g, vt):
    for a in range(GROUP // 2):
        lo = (g * GROUP // 2 + a) * PAIR
        pair = jnp.concatenate([vt[:, (2 * a) * ATTN_BLOCK:(2 * a + 1) * ATTN_BLOCK],
                                vt[:, (2 * a + 1) * ATTN_BLOCK:(2 * a + 2) * ATTN_BLOCK]], axis=0)
        ref[:, lo:lo + PAIR] = pair.T.astype(ref.dtype)


def _kv_parts(kv_ref, g):
    ks = slice(g * HEAD_DIM, (g + 1) * HEAD_DIM)
    vs = slice(KV_WIDTH + g * HEAD_DIM, KV_WIDTH + (g + 1) * HEAD_DIM)
    return kv_ref[:, ks].astype(MXU_DTYPE), kv_ref[:, vs].astype(MXU_DTYPE)


def _sink_rows(sinks):
    return jnp.repeat(sinks.reshape(KV_HEADS, GROUP), ATTN_BLOCK, axis=1)


def _attn_fwd(pq, pkv, sinks, comm=None):
    T = pq.shape[0]
    nb = T // ATTN_BLOCK

    def body(q_ref, kvc_ref, kvp_ref, s_ref, y_ref, lse_ref):
        mask_c = _attn_mask()
        has_prev = pl.program_id(0) > 0
        for g in range(KV_HEADS):
            (kc, vc), (kp, vp) = _kv_parts(kvc_ref, g), _kv_parts(kvp_ref, g)
            qt = _heads_transposed(q_ref, g, ATTN_SCALE)
            s = jnp.where(mask_c, _nn(kc, qt), jnp.where(has_prev, _nn(kp, qt), NEG_INF))
            sink = s_ref[g:g + 1, :]
            m = jnp.maximum(jnp.max(s, axis=0, keepdims=True), sink)
            p = jnp.exp(s - m)
            den = jnp.sum(p, axis=0, keepdims=True) + jnp.exp(sink - m)
            pc = jnp.where(mask_c, p, 0.0)
            _heads_back(y_ref, g, (_tn(vc, pc) + _tn(vp, p - pc)) / den)
            lse = m + jnp.log(den)
            for i in range(GROUP):
                lse_ref[g * GROUP + i:g * GROUP + i + 1, :] = lse[:, i * ATTN_BLOCK:(i + 1) * ATTN_BLOCK]

    return _call(
        "attn_fwd", body, grid=(nb,),
        in_specs=[pl.BlockSpec((ATTN_BLOCK, D_MODEL), lambda n: (n, 0)),
                  pl.BlockSpec((ATTN_BLOCK, 2 * KV_WIDTH), lambda n: (n, 0)),
                  pl.BlockSpec((ATTN_BLOCK, 2 * KV_WIDTH), lambda n: (jnp.maximum(n - 1, 0), 0)),
                  pl.BlockSpec((KV_HEADS, GROUP_LANES), lambda n: (0, 0))],
        out_specs=[pl.BlockSpec((ATTN_BLOCK, D_MODEL), lambda n: (n, 0)),
                   pl.BlockSpec((Q_HEADS, ATTN_BLOCK), lambda n: (0, n))],
        out_shape=[jax.ShapeDtypeStruct((T, D_MODEL), MXU_DTYPE), jax.ShapeDtypeStruct((Q_HEADS, T), F32)],
        args=[pq, pkv, pkv, _sink_rows(sinks)], semantics=("parallel",), comm=comm)


def _attn_bwd(pq, pkv, sinks, lse, dy, d_in, comm=None):
    T = pq.shape[0]
    nb = T // ATTN_BLOCK
    cur = lambda n: (jnp.minimum(n, nb - 1), 0)
    done = D_MODEL + 2 * KV_WIDTH

    def body(q_ref, kvc_ref, kvp_ref, s_ref, lse_ref, dy_ref, _, out_ref, ds_ref, carry, top, bot, dq_ref):
        n = pl.program_id(0)

        @pl.when(n == 0)
        def _():
            carry[...] = jnp.zeros_like(carry)
            dq_ref[...] = jnp.zeros_like(dq_ref)
            ds_ref[...] = jnp.zeros_like(ds_ref)

        out_ref[:, :D_MODEL] = dq_ref[...]

        @pl.when(n < nb)
        def _():
            mask_c = _attn_mask()
            valid = jnp.logical_or(mask_c, n > 0)
            for g in range(KV_HEADS):
                ks = slice(g * HEAD_DIM, (g + 1) * HEAD_DIM)
                vs = slice(KV_WIDTH + g * HEAD_DIM, KV_WIDTH + (g + 1) * HEAD_DIM)
                (kc, vc), (kp, vp) = _kv_parts(kvc_ref, g), _kv_parts(kvp_ref, g)
                qt = _heads_transposed(q_ref, g, ATTN_SCALE)
                dot = _heads_transposed(dy_ref, g)
                lse = jnp.concatenate([lse_ref[g * GROUP + i:g * GROUP + i + 1, :] for i in range(GROUP)], axis=1)
                p = jnp.where(valid, jnp.exp(jnp.where(mask_c, _nn(kc, qt), _nn(kp, qt)) - lse), 0.0)
                dp = jnp.where(mask_c, _nn(vc, dot), _nn(vp, dot))
                delta = jnp.sum(p * dp, axis=0, keepdims=True)
                ds = p * (dp - delta)
                ds_c, p_c = jnp.where(mask_c, ds, 0.0), jnp.where(mask_c, p, 0.0)
                ds_p, p_p = ds - ds_c, p - p_c
                _heads_back(dq_ref, g, (_tn(kc, ds_c) + _tn(kp, ds_p)) * ATTN_SCALE)
                bot[:, ks], bot[:, vs] = _nt(ds_c, qt), _nt(p_c, dot)
                top[:, ks], top[:, vs] = _nt(ds_p, qt), _nt(p_p, dot)
                ds_ref[g:g + 1, :] -= jnp.exp(s_ref[g:g + 1, :] - lse) * delta
            out_ref[:, D_MODEL:] = (carry[...] + top[...]).astype(out_ref.dtype)
            carry[...] = bot[...]

        @pl.when(n == nb)
        def _():
            out_ref[:, D_MODEL:] = carry[...].astype(out_ref.dtype)

    return _call(
        "attn_bwd", body, grid=(nb + 1,),
        in_specs=[pl.BlockSpec((ATTN_BLOCK, D_MODEL), cur),
                  pl.BlockSpec((ATTN_BLOCK, 2 * KV_WIDTH), cur),
                  pl.BlockSpec((ATTN_BLOCK, 2 * KV_WIDTH), lambda n: (jnp.maximum(jnp.minimum(n, nb - 1) - 1, 0), 0)),
                  pl.BlockSpec((KV_HEADS, GROUP_LANES), lambda n: (0, 0)),
                  pl.BlockSpec((Q_HEADS, ATTN_BLOCK), lambda n: (0, jnp.minimum(n, nb - 1))),
                  pl.BlockSpec((ATTN_BLOCK, D_MODEL), cur), HBM_SPEC],
        out_specs=[pl.BlockSpec((ATTN_BLOCK, done), lambda n: (jnp.maximum(n - 1, 0), 0)),
                   pl.BlockSpec((KV_HEADS, GROUP_LANES), lambda n: (0, 0))],
        out_shape=[jax.ShapeDtypeStruct(d_in.shape, d_in.dtype), jax.ShapeDtypeStruct((KV_HEADS, GROUP_LANES), F32)],
        scratch=[pltpu.VMEM((ATTN_BLOCK, 2 * KV_WIDTH), F32)] * 3 + [pltpu.VMEM((ATTN_BLOCK, D_MODEL), MXU_DTYPE)],
        args=[pq, pkv, pkv, _sink_rows(sinks), lse, dy, d_in], semantics=("arbitrary",), comm=comm, aliases={6: 0})


def _lower_bound(l):
    m = jnp.maximum(l[0:1], l[1:2])
    e0, e1 = jnp.exp(l[0:1] - m), jnp.exp(l[1:2] - m)
    return e0 / (e0 + e1)


def _tri(lower):
    r = lax.broadcasted_iota(jnp.int32, (CHUNK, CHUNK), 0)
    c = lax.broadcasted_iota(jnp.int32, (CHUNK, CHUNK), 1)
    return (r >= c) if lower else (c >= r)


def _chunk_sum(mask, v):
    ones = mask.astype(BF16)
    hi = v.astype(BF16)
    rest = v - hi.astype(F32)
    mid = rest.astype(BF16)
    lo = (rest - mid.astype(F32)).astype(BF16)
    part = lambda t: lax.dot_general(ones, t, (((1,), (0,)), ((), ())), preferred_element_type=F32)
    return part(hi) + part(mid) + part(lo)


def _hgrn_chunk_inputs(hq, hf, lb, causal):
    half_t = 0.5 * jnp.tanh(0.5 * hf)
    sg, sgn = 0.5 + half_t, 0.5 - half_t
    f = lb + (1.0 - lb) * sg
    kk = (1.0 - lb) * sgn
    sq = _sigmoid(hq)
    q = hq * sq
    b = _chunk_sum(causal, jnp.log(f))
    bm, bl = b[CHUNK // 2 - 1:CHUNK // 2, :], b[CHUNK - 1:CHUNK, :]
    e_qm, e_km = jnp.exp(b - bm), jnp.exp(bm - b)
    e_qs, e_kl = e_qm * jnp.exp(bm), e_km * jnp.exp(bl - bm)
    return dict(sg=sg, sgn=sgn, f=f, kk=kk, sq=sq, q=q, e_qm=e_qm, e_km=e_km, e_qs=e_qs, e_kl=e_kl,
                qm=q * e_qm, km=kk * e_km, qs=q * e_qs, kl=kk * e_kl, el=jnp.exp(bl))


def _hgrn_fwd(ph, lb_logits, norm_g, comm=None):
    T = ph.shape[0]
    nblk, cpb = T // HGRN_TOKENS, HGRN_TOKENS // CHUNK
    col = lambda c: pl.BlockSpec((HGRN_TOKENS, D_MODEL), functools.partial(lambda i, c: (i, c), c=c))

    def body(hq_ref, hf_ref, hi_ref, hg_ref, l_ref, ng_ref, y_ref, o_ref, st_ref, s_ref):
        @pl.when(pl.program_id(0) == 0)
        def _():
            s_ref[...] = jnp.zeros_like(s_ref)

        lb = _lower_bound(l_ref[...])
        causal = _tri(True)
        for c in range(cpb):
            rows = slice(c * CHUNK, (c + 1) * CHUNK)
            t = _hgrn_chunk_inputs(hq_ref[rows, :], hf_ref[rows, :], lb, causal)
            qm, km, qs, kl = (t[n].astype(MXU_DTYPE) for n in ("qm", "km", "qs", "kl"))
            v = hi_ref[rows, :].astype(MXU_DTYPE)
            heads = [slice(h * HGRN_K, (h + 1) * HGRN_K) for h in range(HGRN_HEADS)]
            a_all = [jnp.where(causal, _nt(qm[:, ls], km[:, ls]), 0.0).astype(MXU_DTYPE) for ls in heads]
            for h, ls in enumerate(heads):
                st = s_ref[h]
                st_ref[c, ls, :] = st
                o_ref[rows, ls] = _nn(a_all[h], v[:, ls]) + _nt(qs[:, ls], st)
                s_ref[h] = t["el"][:, ls] * st + _tn(v[:, ls], kl[:, ls])
        for h in range(HGRN_HEADS):
            ls = slice(h * HGRN_K, (h + 1) * HGRN_K)
            o = o_ref[:, ls]
            r = lax.rsqrt(jnp.mean(o * o, axis=-1, keepdims=True) + EPS)
            y_ref[:, ls] = (o * r * ng_ref[:, ls] * _sigmoid(hg_ref[:, ls])).astype(y_ref.dtype)

    return _call(
        "hgrn_fwd", body, grid=(nblk,),
        in_specs=[col(0), col(1), col(2), col(3),
                  pl.BlockSpec((2, D_MODEL), lambda i: (0, 0)), pl.BlockSpec((1, D_MODEL), lambda i: (0, 0))],
        out_specs=[pl.BlockSpec((HGRN_TOKENS, D_MODEL), lambda i: (i, 0)),
                   pl.BlockSpec((HGRN_TOKENS, D_MODEL), lambda i: (i, 0)),
                   pl.BlockSpec((cpb, D_MODEL, HGRN_K), lambda i: (i, 0, 0))],
        out_shape=[jax.ShapeDtypeStruct((T, D_MODEL), MXU_DTYPE), jax.ShapeDtypeStruct((T, D_MODEL), F32),
                   jax.ShapeDtypeStruct((T // CHUNK, D_MODEL, HGRN_K), F32)],
        scratch=[pltpu.VMEM((HGRN_HEADS, HGRN_K, HGRN_K), F32)],
        args=[ph, ph, ph, ph, lb_logits, norm_g], semantics=("arbitrary",), comm=comm)


def _hgrn_bwd(ph, o_raw, states, dy, lb_logits, norm_g, d_in, first, comm=None):
    T = ph.shape[0]
    nblk, cpb = T // HGRN_TOKENS, HGRN_TOKENS // CHUNK
    rev = lambda i: nblk - 1 - i
    col = lambda c: pl.BlockSpec((HGRN_TOKENS, D_MODEL), functools.partial(lambda i, c: (rev(i), c), c=c))
    tok = pl.BlockSpec((HGRN_TOKENS, D_MODEL), lambda i: (rev(i), 0))

    def body(hq_ref, hf_ref, hi_ref, hg_ref, o_ref, st_ref, dy_ref, l_ref, ng_ref, _,
             dph_ref, dng_ref, dl_ref, dst_ref, dlb_ref, do_s, dqm_s, dkm_s, dqs_s, dkl_s, dv_s, del_s):
        i = pl.program_id(0)

        @pl.when(i == 0)
        def _():
            dst_ref[...] = jnp.zeros_like(dst_ref)
            dlb_ref[...] = jnp.zeros_like(dlb_ref)
            dng_ref[...] = jnp.zeros_like(dng_ref)

        lb = _lower_bound(l_ref[...])
        causal, anti = _tri(True), _tri(False)
        row = lax.broadcasted_iota(jnp.int32, (CHUNK, D_MODEL), 0)
        for c in reversed(range(cpb)):
            rows = slice(c * CHUNK, (c + 1) * CHUNK)
            hq = hq_ref[rows, :]
            t = _hgrn_chunk_inputs(hq, hf_ref[rows, :], lb, causal)
            sgg = _sigmoid(hg_ref[rows, :])
            dyv = dy_ref[rows, :]
            for h in range(HGRN_HEADS):
                ls = slice(h * HGRN_K, (h + 1) * HGRN_K)
                o = o_ref[rows, ls]
                r = lax.rsqrt(jnp.mean(o * o, axis=-1, keepdims=True) + EPS)
                nrm = o * r
                g_h = sgg[:, ls]
                dph_ref[rows, 3 * D_MODEL + h * HGRN_K:3 * D_MODEL + (h + 1) * HGRN_K] = (
                    dyv[:, ls] * nrm * ng_ref[:, ls] * g_h * (1.0 - g_h)).astype(dph_ref.dtype)
                dyg = dyv[:, ls] * g_h
                dng_ref[:, ls] += jnp.sum(dyg * nrm, axis=0, keepdims=True)
                dn = dyg * ng_ref[:, ls]
                do_s[:, ls] = r * (dn - nrm * jnp.mean(dn * nrm, axis=-1, keepdims=True))
            qm, km, qs, kl = (t[n].astype(MXU_DTYPE) for n in ("qm", "km", "qs", "kl"))
            v = hi_ref[rows, :].astype(MXU_DTYPE)
            do = do_s[...].astype(MXU_DTYPE)
            heads = [slice(h * HGRN_K, (h + 1) * HGRN_K) for h in range(HGRN_HEADS)]
            a_all = [jnp.where(causal, _nt(qm[:, ls], km[:, ls]), 0.0).astype(MXU_DTYPE) for ls in heads]
            da_all = [jnp.where(causal, _nt(do[:, ls], v[:, ls]), 0.0).astype(MXU_DTYPE) for ls in heads]
            for h, ls in enumerate(heads):
                st = st_ref[c, ls, :]
                dst = dst_ref[h]
                a, da = a_all[h], da_all[h]
                dv_s[:, ls] = _tn(a, do[:, ls]) + _nt(kl[:, ls], dst)
                dkl_s[:, ls] = _nn(v[:, ls], dst)
                dqs_s[:, ls] = _nn(do[:, ls], st)
                del_s[:, ls] = jnp.sum(dst * st, axis=0, keepdims=True)
                dst_ref[h] = _tn(do[:, ls], qs[:, ls]) + t["el"][:, ls] * dst
                dqm_s[:, ls] = _nn(da, km[:, ls])
                dkm_s[:, ls] = _tn(da, qm[:, ls])
            dqm, dkm, dqs, dkl = dqm_s[...], dkm_s[...], dqs_s[...], dkl_s[...]
            dq = dqm * t["e_qm"] + dqs * t["e_qs"]
            dk = dkm * t["e_km"] + dkl * t["e_kl"]
            t_qm, t_km, t_kl = dqm * t["qm"], dkm * t["km"], dkl * t["kl"]
            db = t_qm - t_km + dqs * t["qs"] - t_kl
            db_mid = jnp.sum(t_km - t_qm, axis=0, keepdims=True)
            db_last = jnp.sum(t_kl, axis=0, keepdims=True) + del_s[...] * t["el"]
            db = db + jnp.where(row == CHUNK // 2 - 1, db_mid, 0.0) + jnp.where(row == CHUNK - 1, db_last, 0.0)
            dlogf = _chunk_sum(anti, db)
            sq, sg, sgn, f = t["sq"], t["sg"], t["sgn"], t["f"]
            dph_ref[rows, 0:D_MODEL] = (dq * (sq * (1.0 + hq * (1.0 - sq)))).astype(dph_ref.dtype)
            dph_ref[rows, D_MODEL:2 * D_MODEL] = (
                dlogf * (1.0 - lb) * sg * (1.0 - sg) / f - dk * (1.0 - lb) * sgn * (1.0 - sgn)).astype(dph_ref.dtype)
            dph_ref[rows, 2 * D_MODEL:3 * D_MODEL] = dv_s[...].astype(dph_ref.dtype)
            dlb_ref[...] += jnp.sum(dlogf * (1.0 - sg) / f - dk * sgn, axis=0, keepdims=True)

        @pl.when(i == nblk - 1)
        def _():
            dl0 = dlb_ref[...] * lb * (1.0 - lb)
            dl_ref[0:1, :] = dl0
            dl_ref[1:2, :] = -dl0

    wide = pltpu.VMEM((CHUNK, D_MODEL), F32)
    return _call(
        "hgrn_bwd", body, grid=(nblk,),
        in_specs=[col(0), col(1), col(2), col(3), tok,
                  pl.BlockSpec((cpb, D_MODEL, HGRN_K), lambda i: (rev(i), 0, 0)), tok,
                  pl.BlockSpec((2, D_MODEL), lambda i: (0, 0)), pl.BlockSpec((1, D_MODEL), lambda i: (0, 0)), HBM_SPEC],
        out_specs=[pl.BlockSpec((pl.Element(HGRN_TOKENS), pl.Element(4 * D_MODEL)), lambda i: (
                       pl.multiple_of(rev(i) * HGRN_TOKENS, ROW_ALIGN), first)),
                   pl.BlockSpec((1, D_MODEL), lambda i: (0, 0)), pl.BlockSpec((2, D_MODEL), lambda i: (0, 0))],
        out_shape=[jax.ShapeDtypeStruct(d_in.shape, d_in.dtype), jax.ShapeDtypeStruct((1, D_MODEL), F32),
                   jax.ShapeDtypeStruct((2, D_MODEL), F32)],
        scratch=[pltpu.VMEM((HGRN_HEADS, HGRN_K, HGRN_K), F32), pltpu.VMEM((1, D_MODEL), F32),
                 wide, wide, wide, wide, wide, wide, pltpu.VMEM((1, D_MODEL), F32)],
        args=[ph, ph, ph, ph, o_raw, states, dy, lb_logits, norm_g, d_in], semantics=("arbitrary",), comm=comm,
        aliases={9: 0})


def _local_step(x, target, vec, net):
    T, D = x.shape
    norm_mix_g, b_in, sinks, lb_logits = vec["norm_mix_g"], vec["b_in"], vec["attn_sinks"], vec["hgrn_lb_logits"]
    hgrn_norm_g, norm_ffn_g, norm_final_g = vec["hgrn_norm_g"], vec["norm_ffn_g"], vec["norm_final_g"]
    w_in = net.full("w_in")
    o_q, o_kv, o_h, o_g = (sum(IN_SPLITS[:i]) for i in range(4))
    TM = 512

    names = ("w_ffn_gate",)
    (u, pq, pkv, ph, pg), got = _in_proj(x, norm_mix_g, w_in, b_in, tm=256, comm=net.gather(names))
    net.gathered(names, got)
    names = ("w_branch_attn", "w_branch_hgrn")
    (y_attn, lse), got = _attn_fwd(pq, pkv, sinks, comm=net.gather(names))
    net.gathered(names, got)
    names = ("w_ffn_up",)
    (y_hgrn, o_raw, states), got = _hgrn_fwd(ph, lb_logits, hgrn_norm_g, comm=net.gather(names))
    net.gathered(names, got)
    w_ba, w_bh = net.full("w_branch_attn"), net.full("w_branch_hgrn")
    w_gate, w_up = net.full("w_ffn_gate"), net.full("w_ffn_up")
    names = ("w_out",)
    (ya, yb, merged), got = _merge_fwd(y_attn, y_hgrn, w_ba, w_bh, pg, tm=TM, comm=net.gather(names))
    net.gathered(names, got)
    w_out = net.full("w_out")

    names = ("w_ffn_down",)
    (h1, u2, gpre, up, z), got = _ffn_fwd(merged, w_out, x, norm_ffn_g, w_gate, w_up, tm=256,
                                          comm=net.gather(names))
    net.gathered(names, got)
    w_down = net.full("w_ffn_down")

    dh2, dh2b, dgp, dup, dgf_p, loss_p = _ffn_tail(z, w_down, h1, target, norm_final_g, gpre, up, tm=512)
    loss = jnp.sum(loss_p.reshape(-1, 8, D)[:, 0, 0])
    d_norm_final = _colsum_partials(dgf_p)
    d_w_down = _weight_grad("dw_down", z, dh2b, tm=DW_ROWS, tk=T)

    (dh1, dh1b, dg2_p), _ = _ffn_in_bwd(dgp, dup, w_gate, w_up, h1, norm_ffn_g, dh2, tm=256)
    d_norm_ffn = _colsum_partials(dg2_p)
    d_w_gate = _weight_grad("dw_gate", dgp, u2, tm=DW_ROWS, tk=T)
    d_w_up = _weight_grad("dw_up", dup, u2, tm=DW_ROWS, tk=T)

    rows_in = sum(IN_SPLITS)
    dya, dyb, dy_attn, dy_hgrn, d_in = _merge_bwd(dh1b, w_out, w_ba, w_bh, ya, yb, pg, tm=TM, d_in_width=rows_in,
                                                  first=o_g)
    d_w_out = _weight_grad("dw_out", merged, dh1b, tm=1024, tk=1024)
    d_w_ba = _weight_grad("dw_branch_a", y_attn, dya, tm=1024, tk=1024)
    d_w_bh = _weight_grad("dw_branch_b", y_hgrn, dyb, tm=1024, tk=1024)

    names, swap = ("w_ffn_down",), ()
    (d_in, dsink), got = _attn_bwd(pq, pkv, sinks, lse, dy_attn, d_in,
                                   comm=net.exchange(dict(w_ffn_down=[d_w_down]), swap))
    net.received(names, swap, got)
    names, swap = ("w_ffn_gate", "w_ffn_up"), ("w_ffn_down",)
    (d_in, d_hgrn_norm, d_lb_logits), got = _hgrn_bwd(
        ph, o_raw, states, dy_hgrn, lb_logits, hgrn_norm_g, d_in, o_h,
        comm=net.exchange(dict(w_ffn_gate=[d_w_gate], w_ffn_up=[d_w_up]), swap))
    net.received(names, swap, got)

    names, swap = ("w_out", "w_branch_attn", "w_branch_hgrn"), ("w_ffn_gate", "w_ffn_up")
    (*d_w_in, d_b_in), got = _weight_grad(
        "dw_in", d_in, u, tm=DW_ROWS, tk=T, a_colsum=True, carrying=True,
        comm=net.exchange(dict(w_out=[d_w_out], w_branch_attn=[d_w_ba], w_branch_hgrn=[d_w_bh]), swap))
    net.received(names, swap, got)
    d_w_in = [tuple(d_w_in)]

    first_level = net.presum_begin("w_in", d_w_in)
    halves = net.presum_end("w_in", [] if first_level is None else _copies_alone("presum_swap_w_in", first_level))
    names, swap = ("w_in",), ("w_out", "w_branch_attn", "w_branch_hgrn")
    (dx, dg1_p), got = _in_proj_bwd([(d_in, 0)], w_in, x, norm_mix_g, dh1, tm=256,
                                    comm=_join(halves, net.swap(swap)))
    net.last = (names, swap, got)
    d_norm_mix = _colsum_partials(dg1_p)
    vecs = dict(norm_mix_g=d_norm_mix, b_in=d_b_in, attn_sinks=jnp.sum(dsink.reshape(Q_HEADS, ATTN_BLOCK), axis=1).reshape(1, Q_HEADS),
                hgrn_lb_logits=d_lb_logits,
                hgrn_norm_g=d_hgrn_norm, norm_ffn_g=d_norm_ffn, norm_final_g=d_norm_final)
    return loss, dx, vecs


def _place():
    return lax.axis_index("x"), lax.axis_index("y"), lax.axis_index("c")


def _other_chips(x, y):
    return [(1 - x, y), (x, 1 - y), (1 - x, 1 - y)]


def _y_first(copies):
    return [copies[3 * (i // 3) + (1, 0, 2)[i % 3]] for i in range(len(copies))]


def _gather_copies(shards):
    n = len(shards)

    def build(ins, outs, send_sems, recv_sems, local_sems):
        x, y, c = _place()
        mine = 2 * x + y
        local = [pltpu.make_async_copy(ins[w], outs[w].at[mine], local_sems.at[w]) for w in range(n)]
        sends, recvs = [], []
        for w in range(n):
            for k, (px, py) in enumerate(_other_chips(x, y)):
                sem = 3 * w + k
                sends.append(pltpu.make_async_remote_copy(
                    src_ref=ins[w], dst_ref=outs[w].at[mine], send_sem=send_sems.at[sem], recv_sem=recv_sems.at[sem],
                    device_id=(px, py, c), device_id_type=MESH_ID))
                recvs.append(pltpu.make_async_remote_copy(
                    src_ref=ins[w], dst_ref=outs[w].at[2 * px + py], send_sem=send_sems.at[sem],
                    recv_sem=recv_sems.at[sem], device_id=(px, py, c), device_id_type=MESH_ID))
        return sends, recvs, local, _y_first(sends)

    return _Carried(shards, [jax.ShapeDtypeStruct((N_CHIPS,) + s.shape, s.dtype) for s in shards], 3 * n, n, build)


def _grad_copies(stacked):
    n = len(stacked)

    def build(ins, outs, send_sems, recv_sems, local_sems):
        x, y, c = _place()
        sends = []
        for w in range(n):
            for k, (px, py) in enumerate(_other_chips(x, y)):
                sem = 3 * w + k
                sends.append(pltpu.make_async_remote_copy(
                    src_ref=ins[w].at[2 * px + py], dst_ref=outs[w].at[k], send_sem=send_sems.at[sem],
                    recv_sem=recv_sems.at[sem], device_id=(px, py, c), device_id_type=MESH_ID))
        return sends, sends, [], _y_first(sends)

    return _Carried(stacked, [jax.ShapeDtypeStruct((3,) + s.shape[1:], s.dtype) for s in stacked], 3 * n, 0, build)


def _small_copies(small):
    def build(ins, outs, send_sems, recv_sems, local_sems):
        small_ref, all_ref = ins[0], outs[0]
        x, y, c = _place()
        me = 4 * x + 2 * y + c
        sends, recvs = [], []
        for r in range(1, 8):
            px = 1 - x if r & 4 else x
            py = 1 - y if r & 2 else y
            pc = 1 - c if r & 1 else c
            sends.append(pltpu.make_async_remote_copy(
                src_ref=small_ref, dst_ref=all_ref.at[me], send_sem=send_sems.at[r - 1], recv_sem=recv_sems.at[r - 1],
                device_id=(px, py, pc), device_id_type=MESH_ID))
            recvs.append(pltpu.make_async_remote_copy(
                src_ref=small_ref, dst_ref=all_ref.at[4 * px + 2 * py + pc], send_sem=send_sems.at[r - 1],
                recv_sem=recv_sems.at[r - 1], device_id=(px, py, pc), device_id_type=MESH_ID))
        return sends, recvs, [pltpu.make_async_copy(small_ref, all_ref.at[me], local_sems.at[0])]

    return _Carried([small], [jax.ShapeDtypeStruct((8,) + small.shape, small.dtype)], 7, 1, build)


def _gather_by_neighbours(name, shard):
    half = shard.shape[0] // 2
    quarter = half // 2

    def body(in_ref, out_ref, send_sems, recv_sems, local_sem):
        for core in (0, 1):
            @pl.when(lax.axis_index("c") == core)
            def _():
                program(core, in_ref, out_ref, send_sems, recv_sems, local_sem)

    def program(c, in_ref, out_ref, send_sems, recv_sems, local_sem):
        x, y, _ = _place()
        chip = lambda px, py: 2 * px + py
        to_x, to_y, sibling = (1 - x, y, c), (x, 1 - y, c), (x, y, 1 - c)
        x_blk, y_blk, d_blk = chip(1 - x, y), chip(x, 1 - y), chip(1 - x, 1 - y)
        mine, theirs = c * half, (1 - c) * half

        def copy(sem, rows, block, to, src=None):
            place = out_ref.at[block, pl.ds(rows[0], rows[1])]
            return pltpu.make_async_remote_copy(
                src_ref=place if src is None else src, dst_ref=place, send_sem=send_sems.at[sem],
                recv_sem=recv_sems.at[sem], device_id=to, device_id_type=MESH_ID)

        own = pltpu.make_async_copy(in_ref, out_ref.at[chip(x, y)], local_sem)
        own.start()
        my_rows = in_ref.at[pl.ds(mine, half)]
        along_x = dict(send=copy(0, (mine, half), chip(x, y), to_x, src=my_rows),
                       landed=copy(0, (mine, half), x_blk, to_x),
                       onward=[copy(3, (mine + quarter, quarter), x_blk, to_y), copy(4, (mine, half), x_blk, sibling)],
                       diagonal=copy(2, (mine, quarter), d_blk, to_x))
        along_y = dict(send=copy(1, (mine, half), chip(x, y), to_y, src=my_rows),
                       landed=copy(1, (mine, half), y_blk, to_y),
                       onward=[copy(2, (mine, quarter), y_blk, to_x), copy(5, (mine, half), y_blk, sibling)],
                       diagonal=copy(3, (mine + quarter, quarter), d_blk, to_y))
        last = copy(6, (mine, half), d_blk, sibling)

        order = (along_x, along_y) if c == 0 else (along_y, along_x)
        for axis in order:
            axis["send"].start()
        for axis in order:
            axis["landed"].wait_recv()
            for cp in axis["onward"]:
                cp.start()
        for axis in order:
            axis["diagonal"].wait_recv()
        last.start()
        for sem, block in ((4, x_blk), (5, y_blk), (6, d_blk)):
            copy(sem, (theirs, half), block, sibling).wait_recv()
        for cp in [along_x["send"], along_y["send"]] + along_x["onward"] + along_y["onward"] + [last]:
            cp.wait_send()
        own.wait()

    return pl.pallas_call(
        body, name=name, in_specs=[HBM_SPEC], out_specs=HBM_SPEC,
        out_shape=jax.ShapeDtypeStruct((N_CHIPS,) + shard.shape, shard.dtype),
        scratch_shapes=[pltpu.SemaphoreType.DMA((7,)), pltpu.SemaphoreType.DMA((7,)), pltpu.SemaphoreType.DMA(())],
    )(shard)


def _copies_alone(name, comm):
    return _call(name, lambda: None, grid=(), in_specs=[], out_specs=[], out_shape=[], args=[], comm=comm)[1]


class _Net:
    def __init__(self, shards):
        self.shards = shards
        self.whole, self.own, self.theirs, self.sums, self.other = {}, {}, {}, {}, {}
        x, y, _ = _place()
        self.chip = 2 * x + y

    def gather(self, names):
        return _gather_copies([self.shards[n] for n in names])

    def gathered(self, names, got):
        for n, g in zip(names, got):
            self.whole[n] = g.reshape(-1, g.shape[-1])

    def full(self, name):
        return self.whole[name]

    def exchange(self, grads, swap=()):
        stacked = []
        for n, pieces in grads.items():
            (keep, send), = pieces
            self.own[n] = keep
            stacked.append(send.reshape(N_CHIPS, keep.shape[0] // N_CHIPS, send.shape[-1]))
        return _join(_grad_copies(stacked), self.swap(swap))

    def swap(self, names):
        return _sibling_copies([self.sums[n] for n in names]) if names else None

    def presum_begin(self, name, pieces):
        keep = jnp.concatenate([p[0] for p in pieces], axis=0) if len(pieces) > 1 else pieces[0][0]
        send = jnp.concatenate([p[1] for p in pieces], axis=0) if len(pieces) > 1 else pieces[0][1]
        rows = keep.shape[0] // N_CHIPS
        self.held = keep.reshape(N_CHIPS, rows, keep.shape[-1])
        return _half_rows_copies(send.reshape(N_CHIPS, rows, send.shape[-1]))

    def presum_end(self, name, got):
        x, y, c = _place()
        to_send, self.own[name] = _pre_sum("presum_" + name, self.held, got[0], jnp.stack([c, self.chip]))
        return _grad_copies([to_send])

    def received(self, names, swap, got, carried=None):
        self.theirs.update(zip(names, got[:len(names)]))
        self.other.update(zip(swap, got[len(names):]))
        for n in names:
            (self.sums[n],), more = _partial_sum("sum_" + n, self.own[n], self.theirs[n], self.chip, comm=carried)
        return more


def _half_rows_copies(stacked):
    n, rows = stacked.shape[0], stacked.shape[1] // 2

    def build(ins, outs, send_sems, recv_sems, local_sems):
        x, y, c = _place()
        copies = [pltpu.make_async_remote_copy(
            src_ref=ins[0].at[s, pl.ds((1 - c) * rows, rows)], dst_ref=outs[0].at[s], send_sem=send_sems.at[s],
            recv_sem=recv_sems.at[s], device_id=(x, y, 1 - c), device_id_type=MESH_ID) for s in range(n)]
        return copies, copies, []

    return _Carried([stacked], [jax.ShapeDtypeStruct((n, rows, stacked.shape[2]), stacked.dtype)], n, 0, build)


def _pre_sum(name, held, theirs, core_and_chip):
    n, R, C = held.shape
    half = R // 2
    tr = _row_tile(half)
    per_half = half // tr

    def body(place_ref, h_ref, t_ref, send_ref, own_ref):
        total = h_ref[0] + t_ref[0].astype(F32)
        send_ref[0] = total.astype(send_ref.dtype)

        @pl.when(pl.program_id(1) == place_ref[1])
        def _():
            own_ref[...] = total

    return pl.pallas_call(
        body, name=name,
        grid_spec=pltpu.PrefetchScalarGridSpec(
            num_scalar_prefetch=1, grid=(per_half, n),
            in_specs=[pl.BlockSpec((1, tr, C), lambda i, s, place: (s, place[0] * per_half + i, 0)),
                      pl.BlockSpec((1, tr, C), lambda i, s, place: (s, i, 0))],
            out_specs=[pl.BlockSpec((1, tr, C), lambda i, s, place: (s, i, 0)),
                       pl.BlockSpec((tr, C), lambda i, s, place: (i, 0))]),
        out_shape=[jax.ShapeDtypeStruct((n, half, C), MXU_DTYPE), jax.ShapeDtypeStruct((half, C), F32)],
        compiler_params=_params(("arbitrary", "arbitrary")),
    )(core_and_chip, held, theirs)


def _sibling_copies(parts):
    n = len(parts)

    def build(ins, outs, send_sems, recv_sems, local_sems):
        x, y, c = _place()
        copies = [pltpu.make_async_remote_copy(
            src_ref=ins[w], dst_ref=outs[w], send_sem=send_sems.at[w], recv_sem=recv_sems.at[w],
            device_id=(x, y, 1 - c), device_id_type=MESH_ID) for w in range(n)]
        return copies, copies, []

    return _Carried(parts, [jax.ShapeDtypeStruct(p.shape, p.dtype) for p in parts], n, 0, build)


def _row_tile(rows, most=512, sublanes=16):
    return max(t for t in range(sublanes, min(most, rows // 2) + 1, sublanes) if rows % t == 0)


def _partial_sum(name, own, recv, chip, comm=None):
    _, R, C = recv.shape
    tr = _row_tile(R)

    def body(o_ref, r_ref, p_ref):
        p_ref[...] = ((o_ref[...] + r_ref[0].astype(F32)) + r_ref[1].astype(F32)) + r_ref[2].astype(F32)

    if own.shape[0] != R:
        assert comm is None and own.shape[0] == N_CHIPS * R
        total = pl.pallas_call(
            lambda chip_ref, *refs: body(*refs), name=name,
            grid_spec=pltpu.PrefetchScalarGridSpec(
                num_scalar_prefetch=1, grid=(R // tr,),
                in_specs=[pl.BlockSpec((tr, C), lambda i, chip_ref: (chip_ref[0] * (R // tr) + i, 0)),
                          pl.BlockSpec((3, tr, C), lambda i, chip_ref: (0, i, 0))],
                out_specs=pl.BlockSpec((tr, C), lambda i, chip_ref: (i, 0))),
            out_shape=jax.ShapeDtypeStruct((R, C), F32), compiler_params=_params(("parallel",)),
        )(chip.reshape(1), own, recv)
        return [total], []
    return _call(name, body, grid=(R // tr,),
                 in_specs=[pl.BlockSpec((tr, C), lambda i: (i, 0)), pl.BlockSpec((3, tr, C), lambda i: (0, i, 0))],
                 out_specs=[pl.BlockSpec((tr, C), lambda i: (i, 0))], out_shape=[jax.ShapeDtypeStruct((R, C), F32)],
                 args=[own, recv], semantics=("parallel",), comm=comm)


def _adam_vals(w, g, m, v):
    m = ADAM_B1 * m + (1.0 - ADAM_B1) * g
    v = ADAM_B2 * v + (1.0 - ADAM_B2) * (g * g)
    m_hat = m / (1.0 - ADAM_B1 ** ADAM_STEP)
    v_hat = v / (1.0 - ADAM_B2 ** ADAM_STEP)
    delta = -ADAM_LR * (m_hat / (jnp.sqrt(v_hat) + ADAM_EPS) + ADAM_WD * w)
    return delta, m, v


def _adamw(name, w, m, v, mine, other, comm=None):
    R, C = w.shape
    tr = _row_tile(R)

    def body(w_ref, m_ref, v_ref, s_ref, n_ref, g_ref, d_ref, nm_ref, nv_ref):
        g = s_ref[...] + n_ref[...]
        d, nm, nv = _adam_vals(w_ref[...], g, m_ref[...], v_ref[...])
        g_ref[...], d_ref[...], nm_ref[...], nv_ref[...] = g, d, nm, nv

    spec = pl.BlockSpec((tr, C), lambda i: (i, 0))
    return _call(name, body, grid=(R // tr,), in_specs=[spec] * 5, out_specs=[spec] * 4,
                 out_shape=[jax.ShapeDtypeStruct((R, C), F32)] * 4, args=[w, m, v, mine, other],
                 semantics=("parallel",), comm=comm)


def _adamw_by_halves(name, w, m, v, mine, other, core):
    R, C = w.shape
    tr = _row_tile(R // 2)
    per_half = R // 2 // tr

    def body(c_ref, w_ref, m_ref, v_ref, s_ref, n_ref, g_ref, d_ref, nm_ref, nv_ref):
        g = jnp.where(pl.program_id(0) // per_half == c_ref[0, 0], s_ref[...], n_ref[...])
        d, nm, nv = _adam_vals(w_ref[...], g, m_ref[...], v_ref[...])
        g_ref[...], d_ref[...], nm_ref[...], nv_ref[...] = g, d, nm, nv

    spec = pl.BlockSpec((tr, C), lambda i: (i, 0))
    part = pl.BlockSpec((tr, C), lambda i: (i % per_half, 0))
    return pl.pallas_call(
        body, name=name, grid=(R // tr,),
        in_specs=[pl.BlockSpec(memory_space=pltpu.SMEM), spec, spec, spec, part, part], out_specs=[spec] * 4,
        out_shape=[jax.ShapeDtypeStruct((R, C), F32)] * 4, compiler_params=_params(("parallel",)),
    )(core, w, m, v, mine, other)


SMALL_LAYOUT = dict(norm_mix_g=(0, 1, 1024), b_in=(1, 8, 7424), hgrn_norm_g=(9, 1, 1024), norm_ffn_g=(10, 1, 1024),
                    norm_final_g=(11, 1, 1024), hgrn_lb_logits=(12, 2, 2048), attn_sinks=(14, 1, 16))
SMALL_LOSS_ROW, SMALL_ROWS = 15, 16


def _pack_small(grads, loss):
    rows = [jnp.pad(grads[name].astype(F32).reshape(-1), (0, nrows * D_MODEL - n))
            for name, (_, nrows, n) in SMALL_LAYOUT.items()]
    rows.append(jnp.pad(loss.astype(F32).reshape(1), (0, D_MODEL - 1)))
    return jnp.concatenate(rows).reshape(SMALL_ROWS, D_MODEL)


def _adamw_small(w, m, v, g_all):
    names = list(SMALL_LAYOUT)
    n = len(names)

    def body(a_ref, *refs):
        ins, outs = refs[:3 * n], refs[3 * n:]
        g_all_rows = a_ref[0]
        for dev in range(1, 8):
            g_all_rows = g_all_rows + a_ref[dev]
        for i, name in enumerate(names):
            first, nrows, count = SMALL_LAYOUT[name]
            w_ref, m_ref, v_ref = ins[3 * i:3 * i + 3]
            if w_ref.shape[0] == nrows:
                g = g_all_rows[first:first + nrows, :w_ref.shape[1]]
            else:
                last = count - (nrows - 1) * D_MODEL
                g = jnp.concatenate([g_all_rows[r:r + 1, :] for r in range(first, first + nrows - 1)]
                                    + [g_all_rows[first + nrows - 1:first + nrows, :last]], axis=1)
            d, nm, nv = _adam_vals(w_ref[...], g, m_ref[...], v_ref[...])
            for o_ref, val in zip(outs[4 * i:4 * i + 4], (g, d, nm, nv)):
                o_ref[...] = val
        outs[4 * n][...] = g_all_rows[SMALL_LOSS_ROW:SMALL_LOSS_ROW + 1, 0:1]

    res = pl.pallas_call(
        body, name="adamw_small",
        out_shape=[jax.ShapeDtypeStruct(w[name].shape, F32) for name in names for _ in range(4)]
        + [jax.ShapeDtypeStruct((1, 1), F32)],
    )(g_all, *[t[name] for name in names for t in (w, m, v)])
    return {name: res[4 * i:4 * i + 4] for i, name in enumerate(names)}, res[4 * n]


MATRICES = ("w_in", "w_branch_attn", "w_branch_hgrn", "w_out", "w_ffn_gate", "w_ffn_up", "w_ffn_down")
COLUMN_SHARDED = ("w_in", "w_ffn_gate", "w_ffn_up")
WEIGHTS = ("norm_mix_g", "w_in", "b_in", "attn_sinks", "hgrn_lb_logits", "hgrn_norm_g", "w_branch_attn",
           "w_branch_hgrn", "w_out", "norm_ffn_g", "w_ffn_gate", "w_ffn_up", "w_ffn_down", "norm_final_g")


def kernel(x, norm_mix_g, w_in, b_in, attn_sinks, hgrn_lb_logits, hgrn_norm_g, w_branch_attn, w_branch_hgrn, w_out, norm_ffn_g, w_ffn_gate, w_ffn_up, w_ffn_down, norm_final_g, loss_target, m_norm_mix_g, m_w_in, m_b_in, m_attn_sinks, m_hgrn_lb_logits, m_hgrn_norm_g, m_w_branch_attn, m_w_branch_hgrn, m_w_out, m_norm_ffn_g, m_w_ffn_gate, m_w_ffn_up, m_w_ffn_down, m_norm_final_g, v_norm_mix_g, v_w_in, v_b_in, v_attn_sinks, v_hgrn_lb_logits, v_hgrn_norm_g, v_w_branch_attn, v_w_branch_hgrn, v_w_out, v_norm_ffn_g, v_w_ffn_gate, v_w_ffn_up, v_w_ffn_down, v_norm_final_g):
    given = dict(locals())
    w = {n: given[n] for n in WEIGHTS}
    m = {n: given["m_" + n] for n in WEIGHTS}
    v = {n: given["v_" + n] for n in WEIGHTS}

    block = lambda a, n: jnp.transpose(a[0]) if n in COLUMN_SHARDED else a[0]
    unblock = lambda a, n: (jnp.transpose(a) if n in COLUMN_SHARDED else a)[None]
    net = _Net({n: block(w[n], n).astype(MXU_DTYPE) for n in MATRICES})
    net.gathered(("w_in",), [_gather_by_neighbours("gather_w_in", net.shards["w_in"])])
    vec = dict(norm_mix_g=norm_mix_g, b_in=b_in, attn_sinks=attn_sinks, hgrn_lb_logits=hgrn_lb_logits,
               hgrn_norm_g=hgrn_norm_g, norm_ffn_g=norm_ffn_g, norm_final_g=norm_final_g.reshape(1, D_MODEL))
    loss_part, dx, d_vecs = _local_step(x[0], loss_target[0], vec, net)

    small_all, = net.received(*net.last, carried=_small_copies(_pack_small(d_vecs, loss_part)))
    grads, deltas, new_m, new_v = {}, {}, {}, {}
    for n in ("w_ffn_down", "w_ffn_gate", "w_ffn_up", "w_out", "w_branch_attn", "w_branch_hgrn"):
        res, got = _adamw("adamw_" + n, block(w[n], n), block(m[n], n), block(v[n], n), net.sums[n], net.other[n],
                          comm=net.swap(("w_in",)) if n == "w_ffn_down" else None)
        if n == "w_ffn_down":
            net.other["w_in"], = got
        grads[n], deltas[n], new_m[n], new_v[n] = (unblock(r, n) for r in res)
    n = "w_in"
    res = _adamw_by_halves("adamw_" + n, block(w[n], n), block(m[n], n), block(v[n], n), net.sums[n], net.other[n],
                           _place()[2].reshape(1, 1))
    grads[n], deltas[n], new_m[n], new_v[n] = (unblock(r, n) for r in res)
    rows = lambda t: {n: t[n].reshape(-1, t[n].shape[-1]) for n in SMALL_LAYOUT}
    res, loss = _adamw_small(rows(w), rows(m), rows(v), small_all)
    for n, four in res.items():
        grads[n], deltas[n], new_m[n], new_v[n] = (r.reshape(w[n].shape) for r in four)
    loss = loss.reshape(())
    return (loss, dx[None], *[grads[n] for n in WEIGHTS], *[deltas[n] for n in WEIGHTS],
            *[new_m[n] for n in WEIGHTS], *[new_v[n] for n in WEIGHTS])
```

```python
import collections
import functools
import math

import jax
import jax.numpy as jnp
from jax import lax
from jax.experimental import pallas as pl
from jax.experimental.pallas import tpu as pltpu

F32 = jnp.float32
BF16 = jnp.bfloat16
MXU_DTYPE = jnp.bfloat16
SAVED_DTYPE = jnp.bfloat16
MESH_ID = pl.DeviceIdType.MESH

D_MODEL = 1024
HEAD_DIM = 64
Q_HEADS = 16
KV_HEADS = 2
GROUP = Q_HEADS // KV_HEADS
KV_WIDTH = KV_HEADS * HEAD_DIM
ATTN_BLOCK = 128
HGRN_HEADS = 8
HGRN_K = 128
CHUNK = 64
HGRN_TOKENS = 256
FFN = 2816
IN_SPLITS = (1024, 256, 4096, 2048)
EPS = 1e-6
NEG_INF = -1e30
ADAM_LR, ADAM_B1, ADAM_B2, ADAM_EPS, ADAM_WD, ADAM_STEP = 0.001, 0.9, 0.999, 1e-08, 0.01, 10
N_CHIPS = 4
VMEM_LIMIT = 60 * 1024 * 1024
ROW_ALIGN = 16
COPY_PARTS = (4, 2, 1)
DW_ROWS = 256


def _params(sem=None):
    return pltpu.CompilerParams(dimension_semantics=sem, vmem_limit_bytes=VMEM_LIMIT)


def _sigmoid(v):
    return 0.5 * jnp.tanh(0.5 * v) + 0.5


def _dot(a, b, dims):
    return lax.dot_general(a.astype(MXU_DTYPE), b.astype(MXU_DTYPE), (dims, ((), ())),
                           preferred_element_type=F32)


def _nn(a, b):
    return _dot(a, b, ((1,), (0,)))


def _nt(a, b):
    return _dot(a, b, ((1,), (1,)))


def _tn(a, b):
    return _dot(a, b, ((0,), (0,)))


HBM_SPEC = pl.BlockSpec(memory_space=pl.ANY)


class _Carried:
    def __init__(self, arrays, out_shapes, n_remote, n_local, build):
        self.parts = [(len(arrays), len(out_shapes), build)]
        self.arrays, self.out_shapes = list(arrays), list(out_shapes)
        self.scratch = [pltpu.SemaphoreType.DMA((n_remote,)), pltpu.SemaphoreType.DMA((n_remote,)),
                        pltpu.SemaphoreType.DMA((max(n_local, 1),))]

    def __add__(self, other):
        both = _Carried([], [], 1, 0, None)
        both.parts = self.parts + other.parts
        both.arrays, both.out_shapes = self.arrays + other.arrays, self.out_shapes + other.out_shapes
        both.scratch = self.scratch + other.scratch
        return both

    def _built(self, ins, outs, sems):
        for p, (ni, no, build) in enumerate(self.parts):
            yield build(ins[:ni], outs[:no], *sems[3 * p:3 * p + 3])
            ins, outs = ins[ni:], outs[no:]

    def start(self, ins, outs, sems):
        core = lax.axis_index("c")
        for sends, _, local, *other_order in self._built(ins, outs, sems):
            for cp in local:
                cp.start()
            if not other_order:
                for cp in sends:
                    cp.start()
                continue

            @pl.when(core == 0)
            def _():
                for cp in sends:
                    cp.start()

            @pl.when(core == 1)
            def _():
                for cp in other_order[0]:
                    cp.start()

    def wait(self, ins, outs, sems):
        for sends, recvs, local, *_ in self._built(ins, outs, sems):
            for cp in recvs:
                cp.wait_recv()
            for cp in sends:
                cp.wait_send()
            for cp in local:
                cp.wait()


def _join(*comms):
    comms = [c for c in comms if c is not None]
    return functools.reduce(lambda a, b: a + b, comms) if comms else None


def _call(name, body, *, grid, in_specs, out_specs, out_shape, args, scratch=(), semantics=None, comm=None,
          aliases=None):
    n_in, n_out, n_scr = len(in_specs), len(out_specs), len(scratch)
    aliases = aliases or {}
    if comm is None:
        res = pl.pallas_call(body, name=name, grid=grid, in_specs=in_specs, out_specs=out_specs, out_shape=out_shape,
                             scratch_shapes=list(scratch), input_output_aliases=aliases,
                             compiler_params=_params(semantics))(*args)
        return list(res), []
    ci, co = len(comm.arrays), len(comm.out_shapes)

    def carrying(*refs):
        ins, refs = refs[:n_in], refs[n_in:]
        c_ins, refs = refs[:ci], refs[ci:]
        outs, refs = refs[:n_out], refs[n_out:]
        c_outs, refs = refs[:co], refs[co:]
        scr, sems = refs[:n_scr], refs[n_scr:]
        if not grid:
            comm.start(c_ins, c_outs, sems)
            body(*ins, *outs, *scr)
            comm.wait(c_ins, c_outs, sems)
            return
        first = functools.reduce(jnp.logical_and, [pl.program_id(a) == 0 for a in range(len(grid))])
        last = functools.reduce(jnp.logical_and, [pl.program_id(a) == g - 1 for a, g in enumerate(grid)])

        @pl.when(first)
        def _():
            comm.start(c_ins, c_outs, sems)

        body(*ins, *outs, *scr)

        @pl.when(last)
        def _():
            comm.wait(c_ins, c_outs, sems)

    res = pl.pallas_call(
        carrying, name=name, grid=grid, in_specs=list(in_specs) + [HBM_SPEC] * ci,
        out_specs=list(out_specs) + [HBM_SPEC] * co, out_shape=list(out_shape) + comm.out_shapes,
        scratch_shapes=list(scratch) + comm.scratch, input_output_aliases=aliases,
        compiler_params=_params(("arbitrary",) * len(grid) if grid else None),
    )(*args, *comm.arrays)
    return list(res[:n_out]), list(res[n_out:])


_Cols = collections.namedtuple("_Cols", "array first cols")
_Into = collections.namedtuple("_Into", "rows first held")


def _weight_grad(name, a, b, *, tm, tk, a_colsum=False, into=None, carrying=False, comm=None):
    cols = a if isinstance(a, _Cols) else _Cols(a, 0, a.shape[1])
    a = cols.array
    (T, N), M = b.shape, cols.cols
    tk = min(tk, T)
    assert cols.first % tm == 0 and M % tm == 0 and T % tk == 0, (name, cols.first, M, tm, T, tk)
    ni, nk, tile0 = M // tm, T // tk, cols.first // tm
    held = list(into.held) if into is not None and into.held is not None else []

    def body(a_ref, b_ref, *rest):
        keep_ref, send_ref = rest[len(held):len(held) + 2]
        sums_ref = rest[len(held) + 2] if a_colsum else None
        if nk == 1:
            acc = _tn(a_ref[...], b_ref[...])
            keep_ref[...], send_ref[...] = acc, acc.astype(send_ref.dtype)
            if a_colsum:
                sums_ref[...] = jnp.sum(a_ref[...].astype(F32), axis=0, keepdims=True)
            return
        acc_ref = rest[-1]
        k = pl.program_id(1)

        @pl.when(k == 0)
        def _():
            acc_ref[...] = jnp.zeros_like(acc_ref)
            if a_colsum:
                sums_ref[...] = jnp.zeros((1, tm), F32)

        if a_colsum:
            sums_ref[...] += jnp.sum(a_ref[...].astype(F32), axis=0, keepdims=True)
        acc_ref[...] += _tn(a_ref[...], b_ref[...])

        @pl.when(k == nk - 1)
        def _():
            keep_ref[...], send_ref[...] = acc_ref[...], acc_ref[...].astype(send_ref.dtype)

    if into is None:
        rows, out_spec = M, pl.BlockSpec((tm, N), lambda i, k: (i, 0))
    else:
        rows = into.rows
        out_spec = pl.BlockSpec((pl.Element(tm), pl.Element(N)),
                                lambda i, k: (pl.multiple_of(into.first + i * tm, ROW_ALIGN), 0))
    out_shape = [jax.ShapeDtypeStruct((rows, N), F32), jax.ShapeDtypeStruct((rows, N), MXU_DTYPE)]
    out_specs = [out_spec, out_spec]
    if a_colsum:
        out_shape.append(jax.ShapeDtypeStruct((1, M), F32))
        out_specs.append(pl.BlockSpec((1, tm), lambda i, k: (0, i)))
    res, got = _call(
        name, body, grid=(ni, nk),
        in_specs=[pl.BlockSpec((tk, tm), lambda i, k: (k, tile0 + i)), pl.BlockSpec((tk, N), lambda i, k: (k, 0))]
        + [HBM_SPEC] * len(held),
        out_specs=out_specs, out_shape=out_shape, scratch=[pltpu.VMEM((tm, N), F32)] if nk > 1 else [],
        args=[a, b] + held, aliases={2 + p: p for p in range(len(held))}, semantics=("parallel", "arbitrary"),
        comm=comm)
    return (res, got) if carrying else res


def _ffn_fwd(merged, w_out, x, gain, w_gate_t, w_up_t, *, tm, comm=None):
    (T, D), F = x.shape, w_gate_t.shape[0]

    def body(m_ref, wo_ref, x_ref, g_ref, wg_ref, wu_ref, h_ref, u_ref, gate_ref, up_ref, z_ref):
        h = x_ref[...] + _nn(m_ref[...], wo_ref[...])
        h_ref[...] = h
        u = (h * lax.rsqrt(jnp.mean(h * h, axis=-1, keepdims=True) + EPS) * g_ref[...]).astype(u_ref.dtype)
        u_ref[...] = u
        gate, up = _nt(u, wg_ref[...]), _nt(u, wu_ref[...])
        gate_ref[...], up_ref[...] = gate.astype(gate_ref.dtype), up.astype(up_ref.dtype)
        z_ref[...] = (gate * _sigmoid(gate) * up).astype(z_ref.dtype)

    rows = lambda n: pl.BlockSpec((tm, n), lambda i: (i, 0))
    fixed = _fixed_spec
    return _call("ffn_hidden", body, grid=(T // tm,),
                 in_specs=[rows(D), fixed(w_out), rows(D), fixed(gain), fixed(w_gate_t), fixed(w_up_t)],
                 out_specs=[rows(D), rows(D), rows(F), rows(F), rows(F)],
                 out_shape=[jax.ShapeDtypeStruct((T, D), F32), jax.ShapeDtypeStruct((T, D), MXU_DTYPE)]
                 + [jax.ShapeDtypeStruct((T, F), SAVED_DTYPE)] * 2 + [jax.ShapeDtypeStruct((T, F), MXU_DTYPE)],
                 args=[merged, w_out, x, gain, w_gate_t, w_up_t], semantics=("parallel",), comm=comm)


def _in_proj(x, gain, w_in_t, b_in, *, tm, comm=None):
    T, D = x.shape
    bounds = [sum(IN_SPLITS[:i]) for i in range(len(IN_SPLITS) + 1)]

    def body(x_ref, g_ref, w_ref, b_ref, u_ref, *piece_refs):
        xv = x_ref[...]
        r = lax.rsqrt(jnp.mean(xv * xv, axis=-1, keepdims=True) + EPS)
        u = (xv * r * g_ref[...]).astype(u_ref.dtype)
        u_ref[...] = u
        for o_ref, lo, hi in zip(piece_refs, bounds[:-1], bounds[1:]):
            o_ref[...] = (_nt(u, w_ref[lo:hi, :]) + b_ref[:, lo:hi]).astype(o_ref.dtype)

    rows = lambda n: pl.BlockSpec((tm, n), lambda i: (i, 0))
    fixed = _fixed_spec
    dtypes = (MXU_DTYPE, MXU_DTYPE, F32, F32)
    return _call("in_proj", body, grid=(T // tm,),
                 in_specs=[rows(D), fixed(gain), fixed(w_in_t), fixed(b_in)],
                 out_specs=[rows(D)] + [rows(n) for n in IN_SPLITS],
                 out_shape=[jax.ShapeDtypeStruct((T, D), MXU_DTYPE)]
                 + [jax.ShapeDtypeStruct((T, n), dt) for n, dt in zip(IN_SPLITS, dtypes)],
                 args=[x, gain, w_in_t, b_in], semantics=("parallel",), comm=comm)


def _row_spec(tm, n):
    return pl.BlockSpec((tm, n), lambda i: (i, 0))


def _fixed_spec(a):
    return pl.BlockSpec(a.shape, lambda i: (0,) * a.ndim, pipeline_mode=pl.Buffered(1))


def _partials_spec(n):
    return pl.BlockSpec((8, n), lambda i: (i, 0))


def _row_parts(tm, parts):
    assert tm % parts == 0, (tm, parts)
    return [slice(p * (tm // parts), (p + 1) * (tm // parts)) for p in range(parts)]


def _ffn_tail(z, w_down, h1, target, gain, gate, up, *, tm, parts=2):
    (T, F), D = z.shape, h1.shape[1]

    def body(z_ref, w_ref, h_ref, t_ref, g_ref, gate_ref, up_ref, dh_ref, dhb_ref, dgate_ref, dup_ref, dg_ref, l_ref):
        part, dgain = 0.0, 0.0
        pieces = _row_parts(tm, parts)
        h2s = [h_ref[rows, :] + _nn(z_ref[rows, :], w_ref[...]) for rows in pieces]
        for rows, h2 in zip(pieces, h2s):
            r = lax.rsqrt(jnp.mean(h2 * h2, axis=-1, keepdims=True) + EPS)
            xhat = h2 * r
            err = xhat * g_ref[...] - t_ref[rows, :]
            part += 0.5 * jnp.sum(jnp.sum(err * err, axis=-1, keepdims=True), axis=0, keepdims=True) / D
            dy = err / D
            dxh = dy * g_ref[...]
            dh2 = r * (dxh - xhat * jnp.mean(dxh * xhat, axis=-1, keepdims=True))
            dh_ref[rows, :] = dh2
            dhb = dh2.astype(dhb_ref.dtype)
            dhb_ref[rows, :] = dhb
            dgain += jnp.sum(dy * xhat, axis=0, keepdims=True)
            dz = _nt(dhb, w_ref[...])
            gv, upv = gate_ref[rows, :].astype(F32), up_ref[rows, :].astype(F32)
            s = _sigmoid(gv)
            dgate_ref[rows, :] = (dz * upv * (s * (1.0 + gv * (1.0 - s)))).astype(dgate_ref.dtype)
            dup_ref[rows, :] = (dz * (gv * s)).astype(dup_ref.dtype)
        dg_ref[...] = jnp.broadcast_to(dgain, dg_ref.shape)
        l_ref[...] = jnp.broadcast_to(part, l_ref.shape)

    low = lambda n: jax.ShapeDtypeStruct((T, n), MXU_DTYPE)
    part = jax.ShapeDtypeStruct((8 * (T // tm), D), F32)
    return pl.pallas_call(
        body, name="ffn_tail", grid=(T // tm,),
        in_specs=[_row_spec(tm, F), _fixed_spec(w_down), _row_spec(tm, D), _row_spec(tm, D), _fixed_spec(gain),
                  _row_spec(tm, F), _row_spec(tm, F)],
        out_specs=[_row_spec(tm, D), _row_spec(tm, D), _row_spec(tm, F), _row_spec(tm, F), _partials_spec(D),
                   _partials_spec(D)],
        out_shape=[jax.ShapeDtypeStruct((T, D), F32), low(D), low(F), low(F), part, part],
        compiler_params=_params(("parallel",)),
    )(z, w_down, h1, target, gain, gate, up)


def _ffn_in_bwd(dgate, dup, w_gate_t, w_up_t, h1, gain, dres, *, tm, comm=None):
    (T, F), D = dgate.shape, h1.shape[1]

    def body(dg_ref, du_ref, wg_ref, wu_ref, h_ref, g_ref, r_ref, dh_ref, dhb_ref, dgain_ref):
        d_u2 = _nn(dg_ref[...], wg_ref[...]) + _nn(du_ref[...], wu_ref[...])
        dx, dgain = _rmsnorm_bwd_vals(d_u2, h_ref[...], g_ref[...])
        dh = r_ref[...] + dx
        dh_ref[...] = dh
        dhb_ref[...] = dh.astype(dhb_ref.dtype)
        dgain_ref[...] = jnp.broadcast_to(dgain, dgain_ref.shape)

    return _call("d_ffn_in", body, grid=(T // tm,),
                 in_specs=[_row_spec(tm, F), _row_spec(tm, F), _fixed_spec(w_gate_t), _fixed_spec(w_up_t),
                           _row_spec(tm, D), _fixed_spec(gain), _row_spec(tm, D)],
                 out_specs=[_row_spec(tm, D), _row_spec(tm, D), _partials_spec(D)],
                 out_shape=[jax.ShapeDtypeStruct((T, D), F32), jax.ShapeDtypeStruct((T, D), MXU_DTYPE),
                            jax.ShapeDtypeStruct((8 * (T // tm), D), F32)],
                 args=[dgate, dup, w_gate_t, w_up_t, h1, gain, dres], semantics=("parallel",), comm=comm)


def _in_proj_bwd(pieces, w_in_t, x, gain, dres, *, tm, comm=None):
    T, D = x.shape
    n = len(pieces)

    def body(*refs):
        dps, (w_ref, x_ref, g_ref, r_ref, dx_ref, dgain_ref) = refs[:n], refs[n:]
        d_u = None
        for dp_ref, (dp, first) in zip(dps, pieces):
            term = _nn(dp_ref[...], w_ref[first:first + dp.shape[1], :])
            d_u = term if d_u is None else d_u + term
        dx, dgain = _rmsnorm_bwd_vals(d_u, x_ref[...], g_ref[...])
        dx_ref[...] = r_ref[...] + dx
        dgain_ref[...] = jnp.broadcast_to(dgain, dgain_ref.shape)

    return _call("d_u", body, grid=(T // tm,),
                 in_specs=[_row_spec(tm, dp.shape[1]) for dp, _ in pieces]
                 + [_fixed_spec(w_in_t), _row_spec(tm, D), _fixed_spec(gain), _row_spec(tm, D)],
                 out_specs=[_row_spec(tm, D), _partials_spec(D)],
                 out_shape=[jax.ShapeDtypeStruct((T, D), F32), jax.ShapeDtypeStruct((8 * (T // tm), D), F32)],
                 args=[dp for dp, _ in pieces] + [w_in_t, x, gain, dres], semantics=("parallel",), comm=comm)


def _merge_fwd(y_a, y_b, w_a, w_b, gates, *, tm, comm=None):
    T, D = y_a.shape

    def body(ya_ref, yb_ref, wa_ref, wb_ref, ga_ref, gb_ref, pa_ref, pb_ref, m_ref):
        pa, pb = _nn(ya_ref[...], wa_ref[...]), _nn(yb_ref[...], wb_ref[...])
        pa_ref[...], pb_ref[...] = pa.astype(pa_ref.dtype), pb.astype(pb_ref.dtype)
        m_ref[...] = (_sigmoid(ga_ref[...]) * pa + _sigmoid(gb_ref[...]) * pb).astype(m_ref.dtype)

    rows = pl.BlockSpec((tm, D), lambda i: (i, 0))
    whole = pl.BlockSpec((D, D), lambda i: (0, 0), pipeline_mode=pl.Buffered(1))
    return _call("branch_merge", body, grid=(T // tm,),
                 in_specs=[rows, rows, whole, whole, rows, pl.BlockSpec((tm, D), lambda i: (i, 1))],
                 out_specs=[rows] * 3,
                 out_shape=[jax.ShapeDtypeStruct((T, D), SAVED_DTYPE)] * 2 + [jax.ShapeDtypeStruct((T, D), MXU_DTYPE)],
                 args=[y_a, y_b, w_a, w_b, gates, gates], semantics=("parallel",), comm=comm)


def _merge_bwd(dh, w_out, w_a, w_b, p_a, p_b, gates, *, tm, d_in_width, first):
    T, D = dh.shape

    def body(dh_ref, wo_ref, wa_ref, wb_ref, pa_ref, pb_ref, ga_ref, gb_ref, dpa_ref, dpb_ref, dya_ref, dyb_ref,
             din_ref):
        dm = _nt(dh_ref[...], wo_ref[...])
        sa, sb = _sigmoid(ga_ref[...]), _sigmoid(gb_ref[...])
        dpa, dpb = (dm * sa).astype(dpa_ref.dtype), (dm * sb).astype(dpb_ref.dtype)
        dpa_ref[...], dpb_ref[...] = dpa, dpb
        din_ref[:, :D] = (dm * pa_ref[...].astype(F32) * sa * (1.0 - sa)).astype(din_ref.dtype)
        din_ref[:, D:] = (dm * pb_ref[...].astype(F32) * sb * (1.0 - sb)).astype(din_ref.dtype)
        dya_ref[...] = _nt(dpa, wa_ref[...]).astype(dya_ref.dtype)
        dyb_ref[...] = _nt(dpb, wb_ref[...]).astype(dyb_ref.dtype)

    rows = pl.BlockSpec((tm, D), lambda i: (i, 0))
    whole = pl.BlockSpec((D, D), lambda i: (0, 0), pipeline_mode=pl.Buffered(1))
    low = jax.ShapeDtypeStruct((T, D), MXU_DTYPE)
    return pl.pallas_call(
        body, name="d_branch_merge", grid=(T // tm,),
        in_specs=[rows, whole, whole, whole, rows, rows, rows, pl.BlockSpec((tm, D), lambda i: (i, 1))],
        out_specs=[rows] * 4 + [pl.BlockSpec((pl.Element(tm), pl.Element(2 * D)), lambda i: (
            pl.multiple_of(i * tm, ROW_ALIGN), first))],
        out_shape=[low] * 3 + [jax.ShapeDtypeStruct((T, D), F32), jax.ShapeDtypeStruct((T, d_in_width), MXU_DTYPE)],
        compiler_params=_params(("parallel",)),
    )(dh, w_out, w_a, w_b, p_a, p_b, gates, gates)


def _colsum_partials(p):
    return jnp.sum(p.reshape(-1, 8, p.shape[-1])[:, 0, :], axis=0, keepdims=True)


def _rmsnorm_bwd_vals(dy, xin, g):
    rstd = lax.rsqrt(jnp.mean(xin * xin, axis=-1, keepdims=True) + EPS)
    xhat = xin * rstd
    dg = jnp.sum(dy * xhat, axis=0, keepdims=True)
    dxh = dy * g
    dx = rstd * (dxh - xhat * jnp.mean(dxh * xhat, axis=-1, keepdims=True))
    return dx, dg


ATTN_SCALE = 1.0 / math.sqrt(HEAD_DIM)
GROUP_LANES = GROUP * ATTN_BLOCK
PAIR = 2 * HEAD_DIM


def _attn_mask():
    kj = lax.broadcasted_iota(jnp.int32, (ATTN_BLOCK, GROUP_LANES), 0)
    qi = lax.broadcasted_iota(jnp.int32, (ATTN_BLOCK, GROUP_LANES), 1) & (ATTN_BLOCK - 1)
    return kj <= qi


def _heads_transposed(ref, g, scale=None):
    parts = []
    for a in range(GROUP // 2):
        lo = (g * GROUP // 2 + a) * PAIR
        pair = ref[:, lo:lo + PAIR].astype(F32)
        pair = (pair if scale is None else pair * scale).T
        parts += [pair[:HEAD_DIM], pair[HEAD_DIM:]]
    return jnp.concatenate(parts, axis=1).astype(MXU_DTYPE)


def _heads_back(ref, g, vt):
    for a in range(GROUP // 2):
        lo = (g * GROUP // 2 + a) * PAIR
        pair = jnp.concatenate([vt[:, (2 * a) * ATTN_BLOCK:(2 * a + 1) * ATTN_BLOCK],
                                vt[:, (2 * a + 1) * ATTN_BLOCK:(2 * a + 2) * ATTN_BLOCK]], axis=0)
        ref[:, lo:lo + PAIR] = pair.T.astype(ref.dtype)


def _kv_parts(kv_ref, g):
    ks = slice(g * HEAD_DIM, (g + 1) * HEAD_DIM)
    vs = slice(KV_WIDTH + g * HEAD_DIM, KV_WIDTH + (g + 1) * HEAD_DIM)
    return kv_ref[:, ks].astype(MXU_DTYPE), kv_ref[:, vs].astype(MXU_DTYPE)


def _sink_rows(sinks):
    return jnp.repeat(sinks.reshape(KV_HEADS, GROUP), ATTN_BLOCK, axis=1)


def _attn_fwd(pq, pkv, sinks, comm=None):
    T = pq.shape[0]
    nb = T // ATTN_BLOCK

    def body(q_ref, kvc_ref, kvp_ref, s_ref, y_ref, lse_ref):
        mask_c = _attn_mask()
        has_prev = pl.program_id(0) > 0
        for g in range(KV_HEADS):
            (kc, vc), (kp, vp) = _kv_parts(kvc_ref, g), _kv_parts(kvp_ref, g)
            qt = _heads_transposed(q_ref, g, ATTN_SCALE)
            s = jnp.where(mask_c, _nn(kc, qt), jnp.where(has_prev, _nn(kp, qt), NEG_INF))
            sink = s_ref[g:g + 1, :]
            m = jnp.maximum(jnp.max(s, axis=0, keepdims=True), sink)
            p = jnp.exp(s - m)
            den = jnp.sum(p, axis=0, keepdims=True) + jnp.exp(sink - m)
            pc = jnp.where(mask_c, p, 0.0)
            _heads_back(y_ref, g, (_tn(vc, pc) + _tn(vp, p - pc)) / den)
            lse = m + jnp.log(den)
            for i in range(GROUP):
                lse_ref[g * GROUP + i:g * GROUP + i + 1, :] = lse[:, i * ATTN_BLOCK:(i + 1) * ATTN_BLOCK]

    return _call(
        "attn_fwd", body, grid=(nb,),
        in_specs=[pl.BlockSpec((ATTN_BLOCK, D_MODEL), lambda n: (n, 0)),
                  pl.BlockSpec((ATTN_BLOCK, 2 * KV_WIDTH), lambda n: (n, 0)),
                  pl.BlockSpec((ATTN_BLOCK, 2 * KV_WIDTH), lambda n: (jnp.maximum(n - 1, 0), 0)),
                  pl.BlockSpec((KV_HEADS, GROUP_LANES), lambda n: (0, 0))],
        out_specs=[pl.BlockSpec((ATTN_BLOCK, D_MODEL), lambda n: (n, 0)),
                   pl.BlockSpec((Q_HEADS, ATTN_BLOCK), lambda n: (0, n))],
        out_shape=[jax.ShapeDtypeStruct((T, D_MODEL), MXU_DTYPE), jax.ShapeDtypeStruct((Q_HEADS, T), F32)],
        args=[pq, pkv, pkv, _sink_rows(sinks)], semantics=("parallel",), comm=comm)


def _attn_bwd(pq, pkv, sinks, lse, dy, d_in, comm=None):
    T = pq.shape[0]
    nb = T // ATTN_BLOCK
    cur = lambda n: (jnp.minimum(n, nb - 1), 0)
    done = D_MODEL + 2 * KV_WIDTH

    def body(q_ref, kvc_ref, kvp_ref, s_ref, lse_ref, dy_ref, _, out_ref, ds_ref, carry, top, bot, dq_ref):
        n = pl.program_id(0)

        @pl.when(n == 0)
        def _():
            carry[...] = jnp.zeros_like(carry)
            dq_ref[...] = jnp.zeros_like(dq_ref)
            ds_ref[...] = jnp.zeros_like(ds_ref)

        out_ref[:, :D_MODEL] = dq_ref[...]

        @pl.when(n < nb)
        def _():
            mask_c = _attn_mask()
            valid = jnp.logical_or(mask_c, n > 0)
            for g in range(KV_HEADS):
                ks = slice(g * HEAD_DIM, (g + 1) * HEAD_DIM)
                vs = slice(KV_WIDTH + g * HEAD_DIM, KV_WIDTH + (g + 1) * HEAD_DIM)
                (kc, vc), (kp, vp) = _kv_parts(kvc_ref, g), _kv_parts(kvp_ref, g)
                qt = _heads_transposed(q_ref, g, ATTN_SCALE)
                dot = _heads_transposed(dy_ref, g)
                lse = jnp.concatenate([lse_ref[g * GROUP + i:g * GROUP + i + 1, :] for i in range(GROUP)], axis=1)
                p = jnp.where(valid, jnp.exp(jnp.where(mask_c, _nn(kc, qt), _nn(kp, qt)) - lse), 0.0)
                dp = jnp.where(mask_c, _nn(vc, dot), _nn(vp, dot))
                delta = jnp.sum(p * dp, axis=0, keepdims=True)
                ds = p * (dp - delta)
                ds_c, p_c = jnp.where(mask_c, ds, 0.0), jnp.where(mask_c, p, 0.0)
                ds_p, p_p = ds - ds_c, p - p_c
                _heads_back(dq_ref, g, (_tn(kc, ds_c) + _tn(kp, ds_p)) * ATTN_SCALE)
                bot[:, ks], bot[:, vs] = _nt(ds_c, qt), _nt(p_c, dot)
                top[:, ks], top[:, vs] = _nt(ds_p, qt), _nt(p_p, dot)
                ds_ref[g:g + 1, :] -= jnp.exp(s_ref[g:g + 1, :] - lse) * delta
            out_ref[:, D_MODEL:] = (carry[...] + top[...]).astype(out_ref.dtype)
            carry[...] = bot[...]

        @pl.when(n == nb)
        def _():
            out_ref[:, D_MODEL:] = carry[...].astype(out_ref.dtype)

    return _call(
        "attn_bwd", body, grid=(nb + 1,),
        in_specs=[pl.BlockSpec((ATTN_BLOCK, D_MODEL), cur),
                  pl.BlockSpec((ATTN_BLOCK, 2 * KV_WIDTH), cur),
                  pl.BlockSpec((ATTN_BLOCK, 2 * KV_WIDTH), lambda n: (jnp.maximum(jnp.minimum(n, nb - 1) - 1, 0), 0)),
                  pl.BlockSpec((KV_HEADS, GROUP_LANES), lambda n: (0, 0)),
                  pl.BlockSpec((Q_HEADS, ATTN_BLOCK), lambda n: (0, jnp.minimum(n, nb - 1))),
                  pl.BlockSpec((ATTN_BLOCK, D_MODEL), cur), HBM_SPEC],
        out_specs=[pl.BlockSpec((ATTN_BLOCK, done), lambda n: (jnp.maximum(n - 1, 0), 0)),
                   pl.BlockSpec((KV_HEADS, GROUP_LANES), lambda n: (0, 0))],
        out_shape=[jax.ShapeDtypeStruct(d_in.shape, d_in.dtype), jax.ShapeDtypeStruct((KV_HEADS, GROUP_LANES), F32)],
        scratch=[pltpu.VMEM((ATTN_BLOCK, 2 * KV_WIDTH), F32)] * 3 + [pltpu.VMEM((ATTN_BLOCK, D_MODEL), MXU_DTYPE)],
        args=[pq, pkv, pkv, _sink_rows(sinks), lse, dy, d_in], semantics=("arbitrary",), comm=comm, aliases={6: 0})


def _lower_bound(l):
    m = jnp.maximum(l[0:1], l[1:2])
    e0, e1 = jnp.exp(l[0:1] - m), jnp.exp(l[1:2] - m)
    return e0 / (e0 + e1)


def _tri(lower):
    r = lax.broadcasted_iota(jnp.int32, (CHUNK, CHUNK), 0)
    c = lax.broadcasted_iota(jnp.int32, (CHUNK, CHUNK), 1)
    return (r >= c) if lower else (c >= r)


def _chunk_sum(mask, v):
    ones = mask.astype(BF16)
    hi = v.astype(BF16)
    rest = v - hi.astype(F32)
    mid = rest.astype(BF16)
    lo = (rest - mid.astype(F32)).astype(BF16)
    part = lambda t: lax.dot_general(ones, t, (((1,), (0,)), ((), ())), preferred_element_type=F32)
    return part(hi) + part(mid) + part(lo)


def _hgrn_chunk_inputs(hq, hf, lb, causal):
    half_t = 0.5 * jnp.tanh(0.5 * hf)
    sg, sgn = 0.5 + half_t, 0.5 - half_t
    f = lb + (1.0 - lb) * sg
    kk = (1.0 - lb) * sgn
    sq = _sigmoid(hq)
    q = hq * sq
    b = _chunk_sum(causal, jnp.log(f))
    bm, bl = b[CHUNK // 2 - 1:CHUNK // 2, :], b[CHUNK - 1:CHUNK, :]
    e_qm, e_km = jnp.exp(b - bm), jnp.exp(bm - b)
    e_qs, e_kl = e_qm * jnp.exp(bm), e_km * jnp.exp(bl - bm)
    return dict(sg=sg, sgn=sgn, f=f, kk=kk, sq=sq, q=q, e_qm=e_qm, e_km=e_km, e_qs=e_qs, e_kl=e_kl,
                qm=q * e_qm, km=kk * e_km, qs=q * e_qs, kl=kk * e_kl, el=jnp.exp(bl))


def _hgrn_fwd(ph, lb_logits, norm_g, comm=None):
    T = ph.shape[0]
    nblk, cpb = T // HGRN_TOKENS, HGRN_TOKENS // CHUNK
    col = lambda c: pl.BlockSpec((HGRN_TOKENS, D_MODEL), functools.partial(lambda i, c: (i, c), c=c))

    def body(hq_ref, hf_ref, hi_ref, hg_ref, l_ref, ng_ref, y_ref, o_ref, st_ref, s_ref):
        @pl.when(pl.program_id(0) == 0)
        def _():
            s_ref[...] = jnp.zeros_like(s_ref)

        lb = _lower_bound(l_ref[...])
        causal = _tri(True)
        for c in range(cpb):
            rows = slice(c * CHUNK, (c + 1) * CHUNK)
            t = _hgrn_chunk_inputs(hq_ref[rows, :], hf_ref[rows, :], lb, causal)
            qm, km, qs, kl = (t[n].astype(MXU_DTYPE) for n in ("qm", "km", "qs", "kl"))
            v = hi_ref[rows, :].astype(MXU_DTYPE)
            heads = [slice(h * HGRN_K, (h + 1) * HGRN_K) for h in range(HGRN_HEADS)]
            a_all = [jnp.where(causal, _nt(qm[:, ls], km[:, ls]), 0.0).astype(MXU_DTYPE) for ls in heads]
            for h, ls in enumerate(heads):
                st = s_ref[h]
                st_ref[c, ls, :] = st
                o_ref[rows, ls] = _nn(a_all[h], v[:, ls]) + _nt(qs[:, ls], st)
                s_ref[h] = t["el"][:, ls] * st + _tn(v[:, ls], kl[:, ls])
        for h in range(HGRN_HEADS):
            ls = slice(h * HGRN_K, (h + 1) * HGRN_K)
            o = o_ref[:, ls]
            r = lax.rsqrt(jnp.mean(o * o, axis=-1, keepdims=True) + EPS)
            y_ref[:, ls] = (o * r * ng_ref[:, ls] * _sigmoid(hg_ref[:, ls])).astype(y_ref.dtype)

    return _call(
        "hgrn_fwd", body, grid=(nblk,),
        in_specs=[col(0), col(1), col(2), col(3),
                  pl.BlockSpec((2, D_MODEL), lambda i: (0, 0)), pl.BlockSpec((1, D_MODEL), lambda i: (0, 0))],
        out_specs=[pl.BlockSpec((HGRN_TOKENS, D_MODEL), lambda i: (i, 0)),
                   pl.BlockSpec((HGRN_TOKENS, D_MODEL), lambda i: (i, 0)),
                   pl.BlockSpec((cpb, D_MODEL, HGRN_K), lambda i: (i, 0, 0))],
        out_shape=[jax.ShapeDtypeStruct((T, D_MODEL), MXU_DTYPE), jax.ShapeDtypeStruct((T, D_MODEL), F32),
                   jax.ShapeDtypeStruct((T // CHUNK, D_MODEL, HGRN_K), F32)],
        scratch=[pltpu.VMEM((HGRN_HEADS, HGRN_K, HGRN_K), F32)],
        args=[ph, ph, ph, ph, lb_logits, norm_g], semantics=("arbitrary",), comm=comm)


def _hgrn_bwd(ph, o_raw, states, dy, lb_logits, norm_g, d_in, first, comm=None):
    T = ph.shape[0]
    nblk, cpb = T // HGRN_TOKENS, HGRN_TOKENS // CHUNK
    rev = lambda i: nblk - 1 - i
    col = lambda c: pl.BlockSpec((HGRN_TOKENS, D_MODEL), functools.partial(lambda i, c: (rev(i), c), c=c))
    tok = pl.BlockSpec((HGRN_TOKENS, D_MODEL), lambda i: (rev(i), 0))

    def body(hq_ref, hf_ref, hi_ref, hg_ref, o_ref, st_ref, dy_ref, l_ref, ng_ref, _,
             dph_ref, dng_ref, dl_ref, dst_ref, dlb_ref, do_s, dqm_s, dkm_s, dqs_s, dkl_s, dv_s, del_s):
        i = pl.program_id(0)

        @pl.when(i == 0)
        def _():
            dst_ref[...] = jnp.zeros_like(dst_ref)
            dlb_ref[...] = jnp.zeros_like(dlb_ref)
            dng_ref[...] = jnp.zeros_like(dng_ref)

        lb = _lower_bound(l_ref[...])
        causal, anti = _tri(True), _tri(False)
        row = lax.broadcasted_iota(jnp.int32, (CHUNK, D_MODEL), 0)
        for c in reversed(range(cpb)):
            rows = slice(c * CHUNK, (c + 1) * CHUNK)
            hq = hq_ref[rows, :]
            t = _hgrn_chunk_inputs(hq, hf_ref[rows, :], lb, causal)
            sgg = _sigmoid(hg_ref[rows, :])
            dyv = dy_ref[rows, :]
            for h in range(HGRN_HEADS):
                ls = slice(h * HGRN_K, (h + 1) * HGRN_K)
                o = o_ref[rows, ls]
                r = lax.rsqrt(jnp.mean(o * o, axis=-1, keepdims=True) + EPS)
                nrm = o * r
                g_h = sgg[:, ls]
                dph_ref[rows, 3 * D_MODEL + h * HGRN_K:3 * D_MODEL + (h + 1) * HGRN_K] = (
                    dyv[:, ls] * nrm * ng_ref[:, ls] * g_h * (1.0 - g_h)).astype(dph_ref.dtype)
                dyg = dyv[:, ls] * g_h
                dng_ref[:, ls] += jnp.sum(dyg * nrm, axis=0, keepdims=True)
                dn = dyg * ng_ref[:, ls]
                do_s[:, ls] = r * (dn - nrm * jnp.mean(dn * nrm, axis=-1, keepdims=True))
            qm, km, qs, kl = (t[n].astype(MXU_DTYPE) for n in ("qm", "km", "qs", "kl"))
            v = hi_ref[rows, :].astype(MXU_DTYPE)
            do = do_s[...].astype(MXU_DTYPE)
            heads = [slice(h * HGRN_K, (h + 1) * HGRN_K) for h in range(HGRN_HEADS)]
            a_all = [jnp.where(causal, _nt(qm[:, ls], km[:, ls]), 0.0).astype(MXU_DTYPE) for ls in heads]
            da_all = [jnp.where(causal, _nt(do[:, ls], v[:, ls]), 0.0).astype(MXU_DTYPE) for ls in heads]
            for h, ls in enumerate(heads):
                st = st_ref[c, ls, :]
                dst = dst_ref[h]
                a, da = a_all[h], da_all[h]
                dv_s[:, ls] = _tn(a, do[:, ls]) + _nt(kl[:, ls], dst)
                dkl_s[:, ls] = _nn(v[:, ls], dst)
                dqs_s[:, ls] = _nn(do[:, ls], st)
                del_s[:, ls] = jnp.sum(dst * st, axis=0, keepdims=True)
                dst_ref[h] = _tn(do[:, ls], qs[:, ls]) + t["el"][:, ls] * dst
                dqm_s[:, ls] = _nn(da, km[:, ls])
                dkm_s[:, ls] = _tn(da, qm[:, ls])
            dqm, dkm, dqs, dkl = dqm_s[...], dkm_s[...], dqs_s[...], dkl_s[...]
            dq = dqm * t["e_qm"] + dqs * t["e_qs"]
            dk = dkm * t["e_km"] + dkl * t["e_kl"]
            t_qm, t_km, t_kl = dqm * t["qm"], dkm * t["km"], dkl * t["kl"]
            db = t_qm - t_km + dqs * t["qs"] - t_kl
            db_mid = jnp.sum(t_km - t_qm, axis=0, keepdims=True)
            db_last = jnp.sum(t_kl, axis=0, keepdims=True) + del_s[...] * t["el"]
            db = db + jnp.where(row == CHUNK // 2 - 1, db_mid, 0.0) + jnp.where(row == CHUNK - 1, db_last, 0.0)
            dlogf = _chunk_sum(anti, db)
            sq, sg, sgn, f = t["sq"], t["sg"], t["sgn"], t["f"]
            dph_ref[rows, 0:D_MODEL] = (dq * (sq * (1.0 + hq * (1.0 - sq)))).astype(dph_ref.dtype)
            dph_ref[rows, D_MODEL:2 * D_MODEL] = (
                dlogf * (1.0 - lb) * sg * (1.0 - sg) / f - dk * (1.0 - lb) * sgn * (1.0 - sgn)).astype(dph_ref.dtype)
            dph_ref[rows, 2 * D_MODEL:3 * D_MODEL] = dv_s[...].astype(dph_ref.dtype)
            dlb_ref[...] += jnp.sum(dlogf * (1.0 - sg) / f - dk * sgn, axis=0, keepdims=True)

        @pl.when(i == nblk - 1)
        def _():
            dl0 = dlb_ref[...] * lb * (1.0 - lb)
            dl_ref[0:1, :] = dl0
            dl_ref[1:2, :] = -dl0

    wide = pltpu.VMEM((CHUNK, D_MODEL), F32)
    return _call(
        "hgrn_bwd", body, grid=(nblk,),
        in_specs=[col(0), col(1), col(2), col(3), tok,
                  pl.BlockSpec((cpb, D_MODEL, HGRN_K), lambda i: (rev(i), 0, 0)), tok,
                  pl.BlockSpec((2, D_MODEL), lambda i: (0, 0)), pl.BlockSpec((1, D_MODEL), lambda i: (0, 0)), HBM_SPEC],
        out_specs=[pl.BlockSpec((pl.Element(HGRN_TOKENS), pl.Element(4 * D_MODEL)), lambda i: (
                       pl.multiple_of(rev(i) * HGRN_TOKENS, ROW_ALIGN), first)),
                   pl.BlockSpec((1, D_MODEL), lambda i: (0, 0)), pl.BlockSpec((2, D_MODEL), lambda i: (0, 0))],
        out_shape=[jax.ShapeDtypeStruct(d_in.shape, d_in.dtype), jax.ShapeDtypeStruct((1, D_MODEL), F32),
                   jax.ShapeDtypeStruct((2, D_MODEL), F32)],
        scratch=[pltpu.VMEM((HGRN_HEADS, HGRN_K, HGRN_K), F32), pltpu.VMEM((1, D_MODEL), F32),
                 wide, wide, wide, wide, wide, wide, pltpu.VMEM((1, D_MODEL), F32)],
        args=[ph, ph, ph, ph, o_raw, states, dy, lb_logits, norm_g, d_in], semantics=("arbitrary",), comm=comm,
        aliases={9: 0})


def _local_step(x, target, vec, net):
    T, D = x.shape
    norm_mix_g, b_in, sinks, lb_logits = vec["norm_mix_g"], vec["b_in"], vec["attn_sinks"], vec["hgrn_lb_logits"]
    hgrn_norm_g, norm_ffn_g, norm_final_g = vec["hgrn_norm_g"], vec["norm_ffn_g"], vec["norm_final_g"]
    w_in = net.full("w_in")
    o_q, o_kv, o_h, o_g = (sum(IN_SPLITS[:i]) for i in range(4))
    TM = 512

    names = ("w_ffn_gate",)
    (u, pq, pkv, ph, pg), got = _in_proj(x, norm_mix_g, w_in, b_in, tm=256, comm=net.gather(names))
    net.gathered(names, got)
    names = ("w_branch_attn", "w_branch_hgrn")
    (y_attn, lse), got = _attn_fwd(pq, pkv, sinks, comm=net.gather(names))
    net.gathered(names, got)
    names = ("w_ffn_up",)
    (y_hgrn, o_raw, states), got = _hgrn_fwd(ph, lb_logits, hgrn_norm_g, comm=net.gather(names))
    net.gathered(names, got)
    w_ba, w_bh = net.full("w_branch_attn"), net.full("w_branch_hgrn")
    w_gate, w_up = net.full("w_ffn_gate"), net.full("w_ffn_up")
    names = ("w_out",)
    (ya, yb, merged), got = _merge_fwd(y_attn, y_hgrn, w_ba, w_bh, pg, tm=TM, comm=net.gather(names))
    net.gathered(names, got)
    w_out = net.full("w_out")

    names = ("w_ffn_down",)
    (h1, u2, gpre, up, z), got = _ffn_fwd(merged, w_out, x, norm_ffn_g, w_gate, w_up, tm=256,
                                          comm=net.gather(names))
    net.gathered(names, got)
    w_down = net.full("w_ffn_down")

    dh2, dh2b, dgp, dup, dgf_p, loss_p = _ffn_tail(z, w_down, h1, target, norm_final_g, gpre, up, tm=512)
    loss = jnp.sum(loss_p.reshape(-1, 8, D)[:, 0, 0])
    d_norm_final = _colsum_partials(dgf_p)
    d_w_down = _weight_grad("dw_down", z, dh2b, tm=DW_ROWS, tk=T)

    (dh1, dh1b, dg2_p), _ = _ffn_in_bwd(dgp, dup, w_gate, w_up, h1, norm_ffn_g, dh2, tm=256)
    d_norm_ffn = _colsum_partials(dg2_p)
    d_w_gate = _weight_grad("dw_gate", dgp, u2, tm=DW_ROWS, tk=T)
    d_w_up = _weight_grad("dw_up", dup, u2, tm=DW_ROWS, tk=T)

    rows_in = sum(IN_SPLITS)
    dya, dyb, dy_attn, dy_hgrn, d_in = _merge_bwd(dh1b, w_out, w_ba, w_bh, ya, yb, pg, tm=TM, d_in_width=rows_in,
                                                  first=o_g)
    d_w_out = _weight_grad("dw_out", merged, dh1b, tm=1024, tk=1024)
    d_w_ba = _weight_grad("dw_branch_a", y_attn, dya, tm=1024, tk=1024)
    d_w_bh = _weight_grad("dw_branch_b", y_hgrn, dyb, tm=1024, tk=1024)

    names, swap = ("w_ffn_down",), ()
    (d_in, dsink), got = _attn_bwd(pq, pkv, sinks, lse, dy_attn, d_in,
                                   comm=net.exchange(dict(w_ffn_down=[d_w_down]), swap))
    net.received(names, swap, got)
    names, swap = ("w_ffn_gate", "w_ffn_up"), ("w_ffn_down",)
    (d_in, d_hgrn_norm, d_lb_logits), got = _hgrn_bwd(
        ph, o_raw, states, dy_hgrn, lb_logits, hgrn_norm_g, d_in, o_h,
        comm=net.exchange(dict(w_ffn_gate=[d_w_gate], w_ffn_up=[d_w_up]), swap))
    net.received(names, swap, got)

    names, swap = ("w_out", "w_branch_attn", "w_branch_hgrn"), ("w_ffn_gate", "w_ffn_up")
    (*d_w_in, d_b_in), got = _weight_grad(
        "dw_in", d_in, u, tm=DW_ROWS, tk=T, a_colsum=True, carrying=True,
        comm=net.exchange(dict(w_out=[d_w_out], w_branch_attn=[d_w_ba], w_branch_hgrn=[d_w_bh]), swap))
    net.received(names, swap, got)
    d_w_in = [tuple(d_w_in)]

    first_level = net.presum_begin("w_in", d_w_in)
    halves = net.presum_end("w_in", [] if first_level is None else _copies_alone("presum_swap_w_in", first_level))
    names, swap = ("w_in",), ("w_out", "w_branch_attn", "w_branch_hgrn")
    (dx, dg1_p), got = _in_proj_bwd([(d_in, 0)], w_in, x, norm_mix_g, dh1, tm=256,
                                    comm=_join(halves, net.swap(swap)))
    net.last = (names, swap, got)
    d_norm_mix = _colsum_partials(dg1_p)
    vecs = dict(norm_mix_g=d_norm_mix, b_in=d_b_in, attn_sinks=jnp.sum(dsink.reshape(Q_HEADS, ATTN_BLOCK), axis=1).reshape(1, Q_HEADS),
                hgrn_lb_logits=d_lb_logits,
                hgrn_norm_g=d_hgrn_norm, norm_ffn_g=d_norm_ffn, norm_final_g=d_norm_final)
    return loss, dx, vecs


def _place():
    return lax.axis_index("x"), lax.axis_index("y"), lax.axis_index("c")


def _other_chips(x, y):
    return [(1 - x, y), (x, 1 - y), (1 - x, 1 - y)]


def _y_first(copies):
    return [copies[3 * (i // 3) + (1, 0, 2)[i % 3]] for i in range(len(copies))]


def _gather_copies(shards):
    n = len(shards)

    def build(ins, outs, send_sems, recv_sems, local_sems):
        x, y, c = _place()
        mine = 2 * x + y
        local = [pltpu.make_async_copy(ins[w], outs[w].at[mine], local_sems.at[w]) for w in range(n)]
        sends, recvs = [], []
        for w in range(n):
            for k, (px, py) in enumerate(_other_chips(x, y)):
                sem = 3 * w + k
                sends.append(pltpu.make_async_remote_copy(
                    src_ref=ins[w], dst_ref=outs[w].at[mine], send_sem=send_sems.at[sem], recv_sem=recv_sems.at[sem],
                    device_id=(px, py, c), device_id_type=MESH_ID))
                recvs.append(pltpu.make_async_remote_copy(
                    src_ref=ins[w], dst_ref=outs[w].at[2 * px + py], send_sem=send_sems.at[sem],
                    recv_sem=recv_sems.at[sem], device_id=(px, py, c), device_id_type=MESH_ID))
        return sends, recvs, local, _y_first(sends)

    return _Carried(shards, [jax.ShapeDtypeStruct((N_CHIPS,) + s.shape, s.dtype) for s in shards], 3 * n, n, build)


def _grad_copies(stacked):
    parts = [max(p for p in COPY_PARTS if s.shape[1] % (p * ROW_ALIGN) == 0) for s in stacked]

    def build(ins, outs, send_sems, recv_sems, local_sems):
        x, y, c = _place()
        sends = []
        for w, s in enumerate(stacked):
            rows = s.shape[1] // parts[w]
            for j in range(parts[w]):
                part = pl.ds(j * rows, rows)
                for k, (px, py) in enumerate(_other_chips(x, y)):
                    sem = len(sends)
                    sends.append(pltpu.make_async_remote_copy(
                        src_ref=ins[w].at[2 * px + py, part], dst_ref=outs[w].at[k, part], send_sem=send_sems.at[sem],
                        recv_sem=recv_sems.at[sem], device_id=(px, py, c), device_id_type=MESH_ID))
        return sends, sends, [], _y_first(sends)

    return _Carried(stacked, [jax.ShapeDtypeStruct((3,) + s.shape[1:], s.dtype) for s in stacked], 3 * sum(parts), 0,
                    build)


def _small_copies(small):
    def build(ins, outs, send_sems, recv_sems, local_sems):
        small_ref, all_ref = ins[0], outs[0]
        x, y, c = _place()
        me = 4 * x + 2 * y + c
        sends, recvs = [], []
        for r in range(1, 8):
            px = 1 - x if r & 4 else x
            py = 1 - y if r & 2 else y
            pc = 1 - c if r & 1 else c
            sends.append(pltpu.make_async_remote_copy(
                src_ref=small_ref, dst_ref=all_ref.at[me], send_sem=send_sems.at[r - 1], recv_sem=recv_sems.at[r - 1],
                device_id=(px, py, pc), device_id_type=MESH_ID))
            recvs.append(pltpu.make_async_remote_copy(
                src_ref=small_ref, dst_ref=all_ref.at[4 * px + 2 * py + pc], send_sem=send_sems.at[r - 1],
                recv_sem=recv_sems.at[r - 1], device_id=(px, py, pc), device_id_type=MESH_ID))
        return sends, recvs, [pltpu.make_async_copy(small_ref, all_ref.at[me], local_sems.at[0])]

    return _Carried([small], [jax.ShapeDtypeStruct((8,) + small.shape, small.dtype)], 7, 1, build)


def _gather_by_neighbours(name, shard):
    half = shard.shape[0] // 2
    quarter = half // 2

    def body(in_ref, out_ref, send_sems, recv_sems, local_sem):
        for core in (0, 1):
            @pl.when(lax.axis_index("c") == core)
            def _():
                program(core, in_ref, out_ref, send_sems, recv_sems, local_sem)

    def program(c, in_ref, out_ref, send_sems, recv_sems, local_sem):
        x, y, _ = _place()
        chip = lambda px, py: 2 * px + py
        to_x, to_y, sibling = (1 - x, y, c), (x, 1 - y, c), (x, y, 1 - c)
        x_blk, y_blk, d_blk = chip(1 - x, y), chip(x, 1 - y), chip(1 - x, 1 - y)
        mine, theirs = c * half, (1 - c) * half

        def copy(sem, rows, block, to, src=None):
            place = out_ref.at[block, pl.ds(rows[0], rows[1])]
            return pltpu.make_async_remote_copy(
                src_ref=place if src is None else src, dst_ref=place, send_sem=send_sems.at[sem],
                recv_sem=recv_sems.at[sem], device_id=to, device_id_type=MESH_ID)

        own = pltpu.make_async_copy(in_ref, out_ref.at[chip(x, y)], local_sem)
        own.start()
        my_rows = in_ref.at[pl.ds(mine, half)]
        along_x = dict(send=copy(0, (mine, half), chip(x, y), to_x, src=my_rows),
                       landed=copy(0, (mine, half), x_blk, to_x),
                       onward=[copy(3, (mine + quarter, quarter), x_blk, to_y), copy(4, (mine, half), x_blk, sibling)],
                       diagonal=copy(2, (mine, quarter), d_blk, to_x))
        along_y = dict(send=copy(1, (mine, half), chip(x, y), to_y, src=my_rows),
                       landed=copy(1, (mine, half), y_blk, to_y),
                       onward=[copy(2, (mine, quarter), y_blk, to_x), copy(5, (mine, half), y_blk, sibling)],
                       diagonal=copy(3, (mine + quarter, quarter), d_blk, to_y))
        last = copy(6, (mine, half), d_blk, sibling)

        order = (along_x, along_y) if c == 0 else (along_y, along_x)
        for axis in order:
            axis["send"].start()
        for axis in order:
            axis["landed"].wait_recv()
            for cp in axis["onward"]:
                cp.start()
        for axis in order:
            axis["diagonal"].wait_recv()
        last.start()
        for sem, block in ((4, x_blk), (5, y_blk), (6, d_blk)):
            copy(sem, (theirs, half), block, sibling).wait_recv()
        for cp in [along_x["send"], along_y["send"]] + along_x["onward"] + along_y["onward"] + [last]:
            cp.wait_send()
        own.wait()

    return pl.pallas_call(
        body, name=name, in_specs=[HBM_SPEC], out_specs=HBM_SPEC,
        out_shape=jax.ShapeDtypeStruct((N_CHIPS,) + shard.shape, shard.dtype),
        scratch_shapes=[pltpu.SemaphoreType.DMA((7,)), pltpu.SemaphoreType.DMA((7,)), pltpu.SemaphoreType.DMA(())],
    )(shard)


def _copies_alone(name, comm):
    return _call(name, lambda: None, grid=(), in_specs=[], out_specs=[], out_shape=[], args=[], comm=comm)[1]


class _Net:
    def __init__(self, shards):
        self.shards = shards
        self.whole, self.own, self.theirs, self.sums, self.other = {}, {}, {}, {}, {}
        x, y, _ = _place()
        self.chip = 2 * x + y

    def gather(self, names):
        return _gather_copies([self.shards[n] for n in names])

    def gathered(self, names, got):
        for n, g in zip(names, got):
            self.whole[n] = g.reshape(-1, g.shape[-1])

    def full(self, name):
        return self.whole[name]

    def exchange(self, grads, swap=()):
        stacked = []
        for n, pieces in grads.items():
            (keep, send), = pieces
            self.own[n] = keep
            stacked.append(send.reshape(N_CHIPS, keep.shape[0] // N_CHIPS, send.shape[-1]))
        return _join(_grad_copies(stacked), self.swap(swap))

    def swap(self, names):
        return _sibling_copies([self.sums[n] for n in names]) if names else None

    def presum_begin(self, name, pieces):
        keep = jnp.concatenate([p[0] for p in pieces], axis=0) if len(pieces) > 1 else pieces[0][0]
        send = jnp.concatenate([p[1] for p in pieces], axis=0) if len(pieces) > 1 else pieces[0][1]
        rows = keep.shape[0] // N_CHIPS
        self.held = keep.reshape(N_CHIPS, rows, keep.shape[-1])
        return _half_rows_copies(send.reshape(N_CHIPS, rows, send.shape[-1]))

    def presum_end(self, name, got):
        x, y, c = _place()
        to_send, self.own[name] = _pre_sum("presum_" + name, self.held, got[0], jnp.stack([c, self.chip]))
        return _grad_copies([to_send])

    def received(self, names, swap, got, carried=None):
        self.theirs.update(zip(names, got[:len(names)]))
        self.other.update(zip(swap, got[len(names):]))
        for n in names:
            (self.sums[n],), more = _partial_sum("sum_" + n, self.own[n], self.theirs[n], self.chip, comm=carried)
        return more


def _half_rows_copies(stacked):
    n, rows = stacked.shape[0], stacked.shape[1] // 2

    def build(ins, outs, send_sems, recv_sems, local_sems):
        x, y, c = _place()
        copies = [pltpu.make_async_remote_copy(
            src_ref=ins[0].at[s, pl.ds((1 - c) * rows, rows)], dst_ref=outs[0].at[s], send_sem=send_sems.at[s],
            recv_sem=recv_sems.at[s], device_id=(x, y, 1 - c), device_id_type=MESH_ID) for s in range(n)]
        return copies, copies, []

    return _Carried([stacked], [jax.ShapeDtypeStruct((n, rows, stacked.shape[2]), stacked.dtype)], n, 0, build)


def _pre_sum(name, held, theirs, core_and_chip):
    n, R, C = held.shape
    half = R // 2
    tr = _row_tile(half)
    per_half = half // tr

    def body(place_ref, h_ref, t_ref, send_ref, own_ref):
        total = h_ref[0] + t_ref[0].astype(F32)
        send_ref[0] = total.astype(send_ref.dtype)

        @pl.when(pl.program_id(1) == place_ref[1])
        def _():
            own_ref[...] = total

    return pl.pallas_call(
        body, name=name,
        grid_spec=pltpu.PrefetchScalarGridSpec(
            num_scalar_prefetch=1, grid=(per_half, n),
            in_specs=[pl.BlockSpec((1, tr, C), lambda i, s, place: (s, place[0] * per_half + i, 0)),
                      pl.BlockSpec((1, tr, C), lambda i, s, place: (s, i, 0))],
            out_specs=[pl.BlockSpec((1, tr, C), lambda i, s, place: (s, i, 0)),
                       pl.BlockSpec((tr, C), lambda i, s, place: (i, 0))]),
        out_shape=[jax.ShapeDtypeStruct((n, half, C), MXU_DTYPE), jax.ShapeDtypeStruct((half, C), F32)],
        compiler_params=_params(("arbitrary", "arbitrary")),
    )(core_and_chip, held, theirs)


def _sibling_copies(parts):
    n = len(parts)

    def build(ins, outs, send_sems, recv_sems, local_sems):
        x, y, c = _place()
        copies = [pltpu.make_async_remote_copy(
            src_ref=ins[w], dst_ref=outs[w], send_sem=send_sems.at[w], recv_sem=recv_sems.at[w],
            device_id=(x, y, 1 - c), device_id_type=MESH_ID) for w in range(n)]
        return copies, copies, []

    return _Carried(parts, [jax.ShapeDtypeStruct(p.shape, p.dtype) for p in parts], n, 0, build)


def _row_tile(rows, most=512, sublanes=16):
    return max(t for t in range(sublanes, min(most, rows // 2) + 1, sublanes) if rows % t == 0)


def _partial_sum(name, own, recv, chip, comm=None):
    _, R, C = recv.shape
    tr = _row_tile(R)

    def body(o_ref, r_ref, p_ref):
        p_ref[...] = ((o_ref[...] + r_ref[0].astype(F32)) + r_ref[1].astype(F32)) + r_ref[2].astype(F32)

    if own.shape[0] != R:
        assert comm is None and own.shape[0] == N_CHIPS * R
        total = pl.pallas_call(
            lambda chip_ref, *refs: body(*refs), name=name,
            grid_spec=pltpu.PrefetchScalarGridSpec(
                num_scalar_prefetch=1, grid=(R // tr,),
                in_specs=[pl.BlockSpec((tr, C), lambda i, chip_ref: (chip_ref[0] * (R // tr) + i, 0)),
                          pl.BlockSpec((3, tr, C), lambda i, chip_ref: (0, i, 0))],
                out_specs=pl.BlockSpec((tr, C), lambda i, chip_ref: (i, 0))),
            out_shape=jax.ShapeDtypeStruct((R, C), F32), compiler_params=_params(("parallel",)),
        )(chip.reshape(1), own, recv)
        return [total], []
    return _call(name, body, grid=(R // tr,),
                 in_specs=[pl.BlockSpec((tr, C), lambda i: (i, 0)), pl.BlockSpec((3, tr, C), lambda i: (0, i, 0))],
                 out_specs=[pl.BlockSpec((tr, C), lambda i: (i, 0))], out_shape=[jax.ShapeDtypeStruct((R, C), F32)],
                 args=[own, recv], semantics=("parallel",), comm=comm)


def _adam_vals(w, g, m, v):
    m = ADAM_B1 * m + (1.0 - ADAM_B1) * g
    v = ADAM_B2 * v + (1.0 - ADAM_B2) * (g * g)
    m_hat = m / (1.0 - ADAM_B1 ** ADAM_STEP)
    v_hat = v / (1.0 - ADAM_B2 ** ADAM_STEP)
    delta = -ADAM_LR * (m_hat / (jnp.sqrt(v_hat) + ADAM_EPS) + ADAM_WD * w)
    return delta, m, v


def _adamw(name, w, m, v, mine, other, comm=None):
    R, C = w.shape
    tr = _row_tile(R)

    def body(w_ref, m_ref, v_ref, s_ref, n_ref, g_ref, d_ref, nm_ref, nv_ref):
        g = s_ref[...] + n_ref[...]
        d, nm, nv = _adam_vals(w_ref[...], g, m_ref[...], v_ref[...])
        g_ref[...], d_ref[...], nm_ref[...], nv_ref[...] = g, d, nm, nv

    spec = pl.BlockSpec((tr, C), lambda i: (i, 0))
    return _call(name, body, grid=(R // tr,), in_specs=[spec] * 5, out_specs=[spec] * 4,
                 out_shape=[jax.ShapeDtypeStruct((R, C), F32)] * 4, args=[w, m, v, mine, other],
                 semantics=("parallel",), comm=comm)


def _adamw_by_halves(name, w, m, v, mine, other, core):
    R, C = w.shape
    tr = _row_tile(R // 2)
    per_half = R // 2 // tr

    def body(c_ref, w_ref, m_ref, v_ref, s_ref, n_ref, g_ref, d_ref, nm_ref, nv_ref):
        g = jnp.where(pl.program_id(0) // per_half == c_ref[0, 0], s_ref[...], n_ref[...])
        d, nm, nv = _adam_vals(w_ref[...], g, m_ref[...], v_ref[...])
        g_ref[...], d_ref[...], nm_ref[...], nv_ref[...] = g, d, nm, nv

    spec = pl.BlockSpec((tr, C), lambda i: (i, 0))
    part = pl.BlockSpec((tr, C), lambda i: (i % per_half, 0))
    return pl.pallas_call(
        body, name=name, grid=(R // tr,),
        in_specs=[pl.BlockSpec(memory_space=pltpu.SMEM), spec, spec, spec, part, part], out_specs=[spec] * 4,
        out_shape=[jax.ShapeDtypeStruct((R, C), F32)] * 4, compiler_params=_params(("parallel",)),
    )(core, w, m, v, mine, other)


SMALL_LAYOUT = dict(norm_mix_g=(0, 1, 1024), b_in=(1, 8, 7424), hgrn_norm_g=(9, 1, 1024), norm_ffn_g=(10, 1, 1024),
                    norm_final_g=(11, 1, 1024), hgrn_lb_logits=(12, 2, 2048), attn_sinks=(14, 1, 16))
SMALL_LOSS_ROW, SMALL_ROWS = 15, 16


def _pack_small(grads, loss):
    rows = [jnp.pad(grads[name].astype(F32).reshape(-1), (0, nrows * D_MODEL - n))
            for name, (_, nrows, n) in SMALL_LAYOUT.items()]
    rows.append(jnp.pad(loss.astype(F32).reshape(1), (0, D_MODEL - 1)))
    return jnp.concatenate(rows).reshape(SMALL_ROWS, D_MODEL)


def _adamw_small(w, m, v, g_all):
    names = list(SMALL_LAYOUT)
    n = len(names)

    def body(a_ref, *refs):
        ins, outs = refs[:3 * n], refs[3 * n:]
        g_all_rows = a_ref[0]
        for dev in range(1, 8):
            g_all_rows = g_all_rows + a_ref[dev]
        for i, name in enumerate(names):
            first, nrows, count = SMALL_LAYOUT[name]
            w_ref, m_ref, v_ref = ins[3 * i:3 * i + 3]
            if w_ref.shape[0] == nrows:
                g = g_all_rows[first:first + nrows, :w_ref.shape[1]]
            else:
                last = count - (nrows - 1) * D_MODEL
                g = jnp.concatenate([g_all_rows[r:r + 1, :] for r in range(first, first + nrows - 1)]
                                    + [g_all_rows[first + nrows - 1:first + nrows, :last]], axis=1)
            d, nm, nv = _adam_vals(w_ref[...], g, m_ref[...], v_ref[...])
            for o_ref, val in zip(outs[4 * i:4 * i + 4], (g, d, nm, nv)):
                o_ref[...] = val
        outs[4 * n][...] = g_all_rows[SMALL_LOSS_ROW:SMALL_LOSS_ROW + 1, 0:1]

    res = pl.pallas_call(
        body, name="adamw_small",
        out_shape=[jax.ShapeDtypeStruct(w[name].shape, F32) for name in names for _ in range(4)]
        + [jax.ShapeDtypeStruct((1, 1), F32)],
    )(g_all, *[t[name] for name in names for t in (w, m, v)])
    return {name: res[4 * i:4 * i + 4] for i, name in enumerate(names)}, res[4 * n]


MATRICES = ("w_in", "w_branch_attn", "w_branch_hgrn", "w_out", "w_ffn_gate", "w_ffn_up", "w_ffn_down")
COLUMN_SHARDED = ("w_in", "w_ffn_gate", "w_ffn_up")
WEIGHTS = ("norm_mix_g", "w_in", "b_in", "attn_sinks", "hgrn_lb_logits", "hgrn_norm_g", "w_branch_attn",
           "w_branch_hgrn", "w_out", "norm_ffn_g", "w_ffn_gate", "w_ffn_up", "w_ffn_down", "norm_final_g")


def kernel(x, norm_mix_g, w_in, b_in, attn_sinks, hgrn_lb_logits, hgrn_norm_g, w_branch_attn, w_branch_hgrn, w_out, norm_ffn_g, w_ffn_gate, w_ffn_up, w_ffn_down, norm_final_g, loss_target, m_norm_mix_g, m_w_in, m_b_in, m_attn_sinks, m_hgrn_lb_logits, m_hgrn_norm_g, m_w_branch_attn, m_w_branch_hgrn, m_w_out, m_norm_ffn_g, m_w_ffn_gate, m_w_ffn_up, m_w_ffn_down, m_norm_final_g, v_norm_mix_g, v_w_in, v_b_in, v_attn_sinks, v_hgrn_lb_logits, v_hgrn_norm_g, v_w_branch_attn, v_w_branch_hgrn, v_w_out, v_norm_ffn_g, v_w_ffn_gate, v_w_ffn_up, v_w_ffn_down, v_norm_final_g):
    given = dict(locals())
    w = {n: given[n] for n in WEIGHTS}
    m = {n: given["m_" + n] for n in WEIGHTS}
    v = {n: given["v_" + n] for n in WEIGHTS}

    block = lambda a, n: jnp.transpose(a[0]) if n in COLUMN_SHARDED else a[0]
    unblock = lambda a, n: (jnp.transpose(a) if n in COLUMN_SHARDED else a)[None]
    net = _Net({n: block(w[n], n).astype(MXU_DTYPE) for n in MATRICES})
    net.gathered(("w_in",), [_gather_by_neighbours("gather_w_in", net.shards["w_in"])])
    vec = dict(norm_mix_g=norm_mix_g, b_in=b_in, attn_sinks=attn_sinks, hgrn_lb_logits=hgrn_lb_logits,
               hgrn_norm_g=hgrn_norm_g, norm_ffn_g=norm_ffn_g, norm_final_g=norm_final_g.reshape(1, D_MODEL))
    loss_part, dx, d_vecs = _local_step(x[0], loss_target[0], vec, net)

    small_all, = net.received(*net.last, carried=_small_copies(_pack_small(d_vecs, loss_part)))
    grads, deltas, new_m, new_v = {}, {}, {}, {}
    for n in ("w_ffn_down", "w_ffn_gate", "w_ffn_up", "w_out", "w_branch_attn", "w_branch_hgrn"):
        res, got = _adamw("adamw_" + n, block(w[n], n), block(m[n], n), block(v[n], n), net.sums[n], net.other[n],
                          comm=net.swap(("w_in",)) if n == "w_ffn_down" else None)
        if n == "w_ffn_down":
            net.other["w_in"], = got
        grads[n], deltas[n], new_m[n], new_v[n] = (unblock(r, n) for r in res)
    n = "w_in"
    res = _adamw_by_halves("adamw_" + n, block(w[n], n), block(m[n], n), block(v[n], n), net.sums[n], net.other[n],
                           _place()[2].reshape(1, 1))
    grads[n], deltas[n], new_m[n], new_v[n] = (unblock(r, n) for r in res)
    rows = lambda t: {n: t[n].reshape(-1, t[n].shape[-1]) for n in SMALL_LAYOUT}
    res, loss = _adamw_small(rows(w), rows(m), rows(v), small_all)
    for n, four in res.items():
        grads[n], deltas[n], new_m[n], new_v[n] = (r.reshape(w[n].shape) for r in four)
    loss = loss.reshape(())
    return (loss, dx[None], *[grads[n] for n in WEIGHTS], *[deltas[n] for n in WEIGHTS],
            *[new_m[n] for n in WEIGHTS], *[new_v[n] for n in WEIGHTS])
```

```python
import collections
import functools
import math

import jax
import jax.numpy as jnp
from jax import lax
from jax.experimental import pallas as pl
from jax.experimental.pallas import tpu as pltpu

F32 = jnp.float32
BF16 = jnp.bfloat16
MXU_DTYPE = jnp.bfloat16
SAVED_DTYPE = jnp.bfloat16
MESH_ID = pl.DeviceIdType.MESH

D_MODEL = 1024
HEAD_DIM = 64
Q_HEADS = 16
KV_HEADS = 2
GROUP = Q_HEADS // KV_HEADS
KV_WIDTH = KV_HEADS * HEAD_DIM
ATTN_BLOCK = 128
HGRN_HEADS = 8
HGRN_K = 128
CHUNK = 64
HGRN_TOKENS = 256
FFN = 2816
IN_SPLITS = (1024, 256, 4096, 2048)
EPS = 1e-6
NEG_INF = -1e30
ADAM_LR, ADAM_B1, ADAM_B2, ADAM_EPS, ADAM_WD, ADAM_STEP = 0.001, 0.9, 0.999, 1e-08, 0.01, 10
N_CHIPS = 4
VMEM_LIMIT = 60 * 1024 * 1024
ROW_ALIGN = 16
DW_ROWS = 256


def _params(sem=None):
    return pltpu.CompilerParams(dimension_semantics=sem, vmem_limit_bytes=VMEM_LIMIT)


def _sigmoid(v):
    return 0.5 * jnp.tanh(0.5 * v) + 0.5


def _dot(a, b, dims):
    return lax.dot_general(a.astype(MXU_DTYPE), b.astype(MXU_DTYPE), (dims, ((), ())),
                           preferred_element_type=F32)


def _nn(a, b):
    return _dot(a, b, ((1,), (0,)))


def _nt(a, b):
    return _dot(a, b, ((1,), (1,)))


def _tn(a, b):
    return _dot(a, b, ((0,), (0,)))


HBM_SPEC = pl.BlockSpec(memory_space=pl.ANY)


class _Carried:
    def __init__(self, arrays, out_shapes, n_remote, n_local, build):
        self.parts = [(len(arrays), len(out_shapes), build)]
        self.arrays, self.out_shapes = list(arrays), list(out_shapes)
        self.scratch = [pltpu.SemaphoreType.DMA((n_remote,)), pltpu.SemaphoreType.DMA((n_remote,)),
                        pltpu.SemaphoreType.DMA((max(n_local, 1),))]

    def __add__(self, other):
        both = _Carried([], [], 1, 0, None)
        both.parts = self.parts + other.parts
        both.arrays, both.out_shapes = self.arrays + other.arrays, self.out_shapes + other.out_shapes
        both.scratch = self.scratch + other.scratch
        return both

    def _built(self, ins, outs, sems):
        for p, (ni, no, build) in enumerate(self.parts):
            yield build(ins[:ni], outs[:no], *sems[3 * p:3 * p + 3])
            ins, outs = ins[ni:], outs[no:]

    def start(self, ins, outs, sems):
        core = lax.axis_index("c")
        for sends, _, local, *other_order in self._built(ins, outs, sems):
            for cp in local:
                cp.start()
            if not other_order:
                for cp in sends:
                    cp.start()
                continue

            @pl.when(core == 0)
            def _():
                for cp in sends:
                    cp.start()

            @pl.when(core == 1)
            def _():
                for cp in other_order[0]:
                    cp.start()

    def wait(self, ins, outs, sems):
        for sends, recvs, local, *_ in self._built(ins, outs, sems):
            for cp in recvs:
                cp.wait_recv()
            for cp in sends:
                cp.wait_send()
            for cp in local:
                cp.wait()


def _join(*comms):
    comms = [c for c in comms if c is not None]
    return functools.reduce(lambda a, b: a + b, comms) if comms else None


def _call(name, body, *, grid, in_specs, out_specs, out_shape, args, scratch=(), semantics=None, comm=None,
          aliases=None):
    n_in, n_out, n_scr = len(in_specs), len(out_specs), len(scratch)
    aliases = aliases or {}
    if comm is None:
        res = pl.pallas_call(body, name=name, grid=grid, in_specs=in_specs, out_specs=out_specs, out_shape=out_shape,
                             scratch_shapes=list(scratch), input_output_aliases=aliases,
                             compiler_params=_params(semantics))(*args)
        return list(res), []
    ci, co = len(comm.arrays), len(comm.out_shapes)

    def carrying(*refs):
        ins, refs = refs[:n_in], refs[n_in:]
        c_ins, refs = refs[:ci], refs[ci:]
        outs, refs = refs[:n_out], refs[n_out:]
        c_outs, refs = refs[:co], refs[co:]
        scr, sems = refs[:n_scr], refs[n_scr:]
        if not grid:
            comm.start(c_ins, c_outs, sems)
            body(*ins, *outs, *scr)
            comm.wait(c_ins, c_outs, sems)
            return
        first = functools.reduce(jnp.logical_and, [pl.program_id(a) == 0 for a in range(len(grid))])
        last = functools.reduce(jnp.logical_and, [pl.program_id(a) == g - 1 for a, g in enumerate(grid)])

        @pl.when(first)
        def _():
            comm.start(c_ins, c_outs, sems)

        body(*ins, *outs, *scr)

        @pl.when(last)
        def _():
            comm.wait(c_ins, c_outs, sems)

    res = pl.pallas_call(
        carrying, name=name, grid=grid, in_specs=list(in_specs) + [HBM_SPEC] * ci,
        out_specs=list(out_specs) + [HBM_SPEC] * co, out_shape=list(out_shape) + comm.out_shapes,
        scratch_shapes=list(scratch) + comm.scratch, input_output_aliases=aliases,
        compiler_params=_params(("arbitrary",) * len(grid) if grid else None),
    )(*args, *comm.arrays)
    return list(res[:n_out]), list(res[n_out:])


_Cols = collections.namedtuple("_Cols", "array first cols")
_Into = collections.namedtuple("_Into", "rows first held")


def _weight_grad(name, a, b, *, tm, tk, a_colsum=False, into=None, carrying=False, comm=None):
    cols = a if isinstance(a, _Cols) else _Cols(a, 0, a.shape[1])
    a = cols.array
    (T, N), M = b.shape, cols.cols
    tk = min(tk, T)
    assert cols.first % tm == 0 and M % tm == 0 and T % tk == 0, (name, cols.first, M, tm, T, tk)
    ni, nk, tile0 = M // tm, T // tk, cols.first // tm
    held = list(into.held) if into is not None and into.held is not None else []

    def body(a_ref, b_ref, *rest):
        keep_ref, send_ref = rest[len(held):len(held) + 2]
        sums_ref = rest[len(held) + 2] if a_colsum else None
        if nk == 1:
            acc = _tn(a_ref[...], b_ref[...])
            keep_ref[...], send_ref[...] = acc, acc.astype(send_ref.dtype)
            if a_colsum:
                sums_ref[...] = jnp.sum(a_ref[...].astype(F32), axis=0, keepdims=True)
            return
        acc_ref = rest[-1]
        k = pl.program_id(1)

        @pl.when(k == 0)
        def _():
            acc_ref[...] = jnp.zeros_like(acc_ref)
            if a_colsum:
                sums_ref[...] = jnp.zeros((1, tm), F32)

        if a_colsum:
            sums_ref[...] += jnp.sum(a_ref[...].astype(F32), axis=0, keepdims=True)
        acc_ref[...] += _tn(a_ref[...], b_ref[...])

        @pl.when(k == nk - 1)
        def _():
            keep_ref[...], send_ref[...] = acc_ref[...], acc_ref[...].astype(send_ref.dtype)

    if into is None:
        rows, out_spec = M, pl.BlockSpec((tm, N), lambda i, k: (i, 0))
    else:
        rows = into.rows
        out_spec = pl.BlockSpec((pl.Element(tm), pl.Element(N)),
                                lambda i, k: (pl.multiple_of(into.first + i * tm, ROW_ALIGN), 0))
    out_shape = [jax.ShapeDtypeStruct((rows, N), F32), jax.ShapeDtypeStruct((rows, N), MXU_DTYPE)]
    out_specs = [out_spec, out_spec]
    if a_colsum:
        out_shape.append(jax.ShapeDtypeStruct((1, M), F32))
        out_specs.append(pl.BlockSpec((1, tm), lambda i, k: (0, i)))
    res, got = _call(
        name, body, grid=(ni, nk),
        in_specs=[pl.BlockSpec((tk, tm), lambda i, k: (k, tile0 + i)), pl.BlockSpec((tk, N), lambda i, k: (k, 0))]
        + [HBM_SPEC] * len(held),
        out_specs=out_specs, out_shape=out_shape, scratch=[pltpu.VMEM((tm, N), F32)] if nk > 1 else [],
        args=[a, b] + held, aliases={2 + p: p for p in range(len(held))}, semantics=("parallel", "arbitrary"),
        comm=comm)
    return (res, got) if carrying else res


def _ffn_fwd(merged, w_out, x, gain, w_gate_t, w_up_t, *, tm, comm=None):
    (T, D), F = x.shape, w_gate_t.shape[0]

    def body(m_ref, wo_ref, x_ref, g_ref, wg_ref, wu_ref, h_ref, u_ref, gate_ref, up_ref, z_ref):
        h = x_ref[...] + _nn(m_ref[...], wo_ref[...])
        h_ref[...] = h
        u = (h * lax.rsqrt(jnp.mean(h * h, axis=-1, keepdims=True) + EPS) * g_ref[...]).astype(u_ref.dtype)
        u_ref[...] = u
        gate, up = _nt(u, wg_ref[...]), _nt(u, wu_ref[...])
        gate_ref[...], up_ref[...] = gate.astype(gate_ref.dtype), up.astype(up_ref.dtype)
        z_ref[...] = (gate * _sigmoid(gate) * up).astype(z_ref.dtype)

    rows = lambda n: pl.BlockSpec((tm, n), lambda i: (i, 0))
    fixed = _fixed_spec
    return _call("ffn_hidden", body, grid=(T // tm,),
                 in_specs=[rows(D), fixed(w_out), rows(D), fixed(gain), fixed(w_gate_t), fixed(w_up_t)],
                 out_specs=[rows(D), rows(D), rows(F), rows(F), rows(F)],
                 out_shape=[jax.ShapeDtypeStruct((T, D), F32), jax.ShapeDtypeStruct((T, D), MXU_DTYPE)]
                 + [jax.ShapeDtypeStruct((T, F), SAVED_DTYPE)] * 2 + [jax.ShapeDtypeStruct((T, F), MXU_DTYPE)],
                 args=[merged, w_out, x, gain, w_gate_t, w_up_t], semantics=("parallel",), comm=comm)


def _in_proj(x, gain, w_in_t, b_in, *, tm, comm=None):
    T, D = x.shape
    bounds = [sum(IN_SPLITS[:i]) for i in range(len(IN_SPLITS) + 1)]

    def body(x_ref, g_ref, w_ref, b_ref, u_ref, *piece_refs):
        xv = x_ref[...]
        r = lax.rsqrt(jnp.mean(xv * xv, axis=-1, keepdims=True) + EPS)
        u = (xv * r * g_ref[...]).astype(u_ref.dtype)
        u_ref[...] = u
        for o_ref, lo, hi in zip(piece_refs, bounds[:-1], bounds[1:]):
            o_ref[...] = (_nt(u, w_ref[lo:hi, :]) + b_ref[:, lo:hi]).astype(o_ref.dtype)

    rows = lambda n: pl.BlockSpec((tm, n), lambda i: (i, 0))
    fixed = _fixed_spec
    dtypes = (MXU_DTYPE, MXU_DTYPE, F32, F32)
    return _call("in_proj", body, grid=(T // tm,),
                 in_specs=[rows(D), fixed(gain), fixed(w_in_t), fixed(b_in)],
                 out_specs=[rows(D)] + [rows(n) for n in IN_SPLITS],
                 out_shape=[jax.ShapeDtypeStruct((T, D), MXU_DTYPE)]
                 + [jax.ShapeDtypeStruct((T, n), dt) for n, dt in zip(IN_SPLITS, dtypes)],
                 args=[x, gain, w_in_t, b_in], semantics=("parallel",), comm=comm)


def _row_spec(tm, n):
    return pl.BlockSpec((tm, n), lambda i: (i, 0))


def _fixed_spec(a):
    return pl.BlockSpec(a.shape, lambda i: (0,) * a.ndim, pipeline_mode=pl.Buffered(1))


def _partials_spec(n):
    return pl.BlockSpec((8, n), lambda i: (i, 0))


def _row_parts(tm, parts):
    assert tm % parts == 0, (tm, parts)
    return [slice(p * (tm // parts), (p + 1) * (tm // parts)) for p in range(parts)]


def _ffn_tail(z, w_down, h1, target, gain, gate, up, *, tm, parts=2):
    (T, F), D = z.shape, h1.shape[1]

    def body(z_ref, w_ref, h_ref, t_ref, g_ref, gate_ref, up_ref, dh_ref, dhb_ref, dgate_ref, dup_ref, dg_ref, l_ref):
        part, dgain = 0.0, 0.0
        pieces = _row_parts(tm, parts)
        h2s = [h_ref[rows, :] + _nn(z_ref[rows, :], w_ref[...]) for rows in pieces]
        for rows, h2 in zip(pieces, h2s):
            r = lax.rsqrt(jnp.mean(h2 * h2, axis=-1, keepdims=True) + EPS)
            xhat = h2 * r
            err = xhat * g_ref[...] - t_ref[rows, :]
            part += 0.5 * jnp.sum(jnp.sum(err * err, axis=-1, keepdims=True), axis=0, keepdims=True) / D
            dy = err / D
            dxh = dy * g_ref[...]
            dh2 = r * (dxh - xhat * jnp.mean(dxh * xhat, axis=-1, keepdims=True))
            dh_ref[rows, :] = dh2
            dhb = dh2.astype(dhb_ref.dtype)
            dhb_ref[rows, :] = dhb
            dgain += jnp.sum(dy * xhat, axis=0, keepdims=True)
            dz = _nt(dhb, w_ref[...])
            gv, upv = gate_ref[rows, :].astype(F32), up_ref[rows, :].astype(F32)
            s = _sigmoid(gv)
            dgate_ref[rows, :] = (dz * upv * (s * (1.0 + gv * (1.0 - s)))).astype(dgate_ref.dtype)
            dup_ref[rows, :] = (dz * (gv * s)).astype(dup_ref.dtype)
        dg_ref[...] = jnp.broadcast_to(dgain, dg_ref.shape)
        l_ref[...] = jnp.broadcast_to(part, l_ref.shape)

    low = lambda n: jax.ShapeDtypeStruct((T, n), MXU_DTYPE)
    part = jax.ShapeDtypeStruct((8 * (T // tm), D), F32)
    return pl.pallas_call(
        body, name="ffn_tail", grid=(T // tm,),
        in_specs=[_row_spec(tm, F), _fixed_spec(w_down), _row_spec(tm, D), _row_spec(tm, D), _fixed_spec(gain),
                  _row_spec(tm, F), _row_spec(tm, F)],
        out_specs=[_row_spec(tm, D), _row_spec(tm, D), _row_spec(tm, F), _row_spec(tm, F), _partials_spec(D),
                   _partials_spec(D)],
        out_shape=[jax.ShapeDtypeStruct((T, D), F32), low(D), low(F), low(F), part, part],
        compiler_params=_params(("parallel",)),
    )(z, w_down, h1, target, gain, gate, up)


def _ffn_in_bwd(dgate, dup, w_gate_t, w_up_t, h1, gain, dres, *, tm, comm=None):
    (T, F), D = dgate.shape, h1.shape[1]

    def body(dg_ref, du_ref, wg_ref, wu_ref, h_ref, g_ref, r_ref, dh_ref, dhb_ref, dgain_ref):
        d_u2 = _nn(dg_ref[...], wg_ref[...]) + _nn(du_ref[...], wu_ref[...])
        dx, dgain = _rmsnorm_bwd_vals(d_u2, h_ref[...], g_ref[...])
        dh = r_ref[...] + dx
        dh_ref[...] = dh
        dhb_ref[...] = dh.astype(dhb_ref.dtype)
        dgain_ref[...] = jnp.broadcast_to(dgain, dgain_ref.shape)

    return _call("d_ffn_in", body, grid=(T // tm,),
                 in_specs=[_row_spec(tm, F), _row_spec(tm, F), _fixed_spec(w_gate_t), _fixed_spec(w_up_t),
                           _row_spec(tm, D), _fixed_spec(gain), _row_spec(tm, D)],
                 out_specs=[_row_spec(tm, D), _row_spec(tm, D), _partials_spec(D)],
                 out_shape=[jax.ShapeDtypeStruct((T, D), F32), jax.ShapeDtypeStruct((T, D), MXU_DTYPE),
                            jax.ShapeDtypeStruct((8 * (T // tm), D), F32)],
                 args=[dgate, dup, w_gate_t, w_up_t, h1, gain, dres], semantics=("parallel",), comm=comm)


def _in_proj_bwd(pieces, w_in_t, x, gain, dres, *, tm, comm=None):
    T, D = x.shape
    n = len(pieces)

    def body(*refs):
        dps, (w_ref, x_ref, g_ref, r_ref, dx_ref, dgain_ref) = refs[:n], refs[n:]
        d_u = None
        for dp_ref, (dp, first) in zip(dps, pieces):
            term = _nn(dp_ref[...], w_ref[first:first + dp.shape[1], :])
            d_u = term if d_u is None else d_u + term
        dx, dgain = _rmsnorm_bwd_vals(d_u, x_ref[...], g_ref[...])
        dx_ref[...] = r_ref[...] + dx
        dgain_ref[...] = jnp.broadcast_to(dgain, dgain_ref.shape)

    return _call("d_u", body, grid=(T // tm,),
                 in_specs=[_row_spec(tm, dp.shape[1]) for dp, _ in pieces]
                 + [_fixed_spec(w_in_t), _row_spec(tm, D), _fixed_spec(gain), _row_spec(tm, D)],
                 out_specs=[_row_spec(tm, D), _partials_spec(D)],
                 out_shape=[jax.ShapeDtypeStruct((T, D), F32), jax.ShapeDtypeStruct((8 * (T // tm), D), F32)],
                 args=[dp for dp, _ in pieces] + [w_in_t, x, gain, dres], semantics=("parallel",), comm=comm)


def _merge_fwd(y_a, y_b, w_a, w_b, gates, *, tm, comm=None):
    T, D = y_a.shape

    def body(ya_ref, yb_ref, wa_ref, wb_ref, ga_ref, gb_ref, pa_ref, pb_ref, m_ref):
        pa, pb = _nn(ya_ref[...], wa_ref[...]), _nn(yb_ref[...], wb_ref[...])
        pa_ref[...], pb_ref[...] = pa.astype(pa_ref.dtype), pb.astype(pb_ref.dtype)
        m_ref[...] = (_sigmoid(ga_ref[...]) * pa + _sigmoid(gb_ref[...]) * pb).astype(m_ref.dtype)

    rows = pl.BlockSpec((tm, D), lambda i: (i, 0))
    whole = pl.BlockSpec((D, D), lambda i: (0, 0), pipeline_mode=pl.Buffered(1))
    return _call("branch_merge", body, grid=(T // tm,),
                 in_specs=[rows, rows, whole, whole, rows, pl.BlockSpec((tm, D), lambda i: (i, 1))],
                 out_specs=[rows] * 3,
                 out_shape=[jax.ShapeDtypeStruct((T, D), SAVED_DTYPE)] * 2 + [jax.ShapeDtypeStruct((T, D), MXU_DTYPE)],
                 args=[y_a, y_b, w_a, w_b, gates, gates], semantics=("parallel",), comm=comm)


def _merge_bwd(dh, w_out, w_a, w_b, p_a, p_b, gates, *, tm, d_in_width, first):
    T, D = dh.shape

    def body(dh_ref, wo_ref, wa_ref, wb_ref, pa_ref, pb_ref, ga_ref, gb_ref, dpa_ref, dpb_ref, dya_ref, dyb_ref,
             din_ref):
        dm = _nt(dh_ref[...], wo_ref[...])
        sa, sb = _sigmoid(ga_ref[...]), _sigmoid(gb_ref[...])
        dpa, dpb = (dm * sa).astype(dpa_ref.dtype), (dm * sb).astype(dpb_ref.dtype)
        dpa_ref[...], dpb_ref[...] = dpa, dpb
        din_ref[:, :D] = (dm * pa_ref[...].astype(F32) * sa * (1.0 - sa)).astype(din_ref.dtype)
        din_ref[:, D:] = (dm * pb_ref[...].astype(F32) * sb * (1.0 - sb)).astype(din_ref.dtype)
        dya_ref[...] = _nt(dpa, wa_ref[...]).astype(dya_ref.dtype)
        dyb_ref[...] = _nt(dpb, wb_ref[...]).astype(dyb_ref.dtype)

    rows = pl.BlockSpec((tm, D), lambda i: (i, 0))
    whole = pl.BlockSpec((D, D), lambda i: (0, 0), pipeline_mode=pl.Buffered(1))
    low = jax.ShapeDtypeStruct((T, D), MXU_DTYPE)
    return pl.pallas_call(
        body, name="d_branch_merge", grid=(T // tm,),
        in_specs=[rows, whole, whole, whole, rows, rows, rows, pl.BlockSpec((tm, D), lambda i: (i, 1))],
        out_specs=[rows] * 4 + [pl.BlockSpec((pl.Element(tm), pl.Element(2 * D)), lambda i: (
            pl.multiple_of(i * tm, ROW_ALIGN), first))],
        out_shape=[low] * 3 + [jax.ShapeDtypeStruct((T, D), F32), jax.ShapeDtypeStruct((T, d_in_width), MXU_DTYPE)],
        compiler_params=_params(("parallel",)),
    )(dh, w_out, w_a, w_b, p_a, p_b, gates, gates)


def _colsum_partials(p):
    return jnp.sum(p.reshape(-1, 8, p.shape[-1])[:, 0, :], axis=0, keepdims=True)


def _rmsnorm_bwd_vals(dy, xin, g):
    rstd = lax.rsqrt(jnp.mean(xin * xin, axis=-1, keepdims=True) + EPS)
    xhat = xin * rstd
    dg = jnp.sum(dy * xhat, axis=0, keepdims=True)
    dxh = dy * g
    dx = rstd * (dxh - xhat * jnp.mean(dxh * xhat, axis=-1, keepdims=True))
    return dx, dg


ATTN_SCALE = 1.0 / math.sqrt(HEAD_DIM)
GROUP_LANES = GROUP * ATTN_BLOCK
PAIR = 2 * HEAD_DIM


def _attn_mask():
    kj = lax.broadcasted_iota(jnp.int32, (ATTN_BLOCK, GROUP_LANES), 0)
    qi = lax.broadcasted_iota(jnp.int32, (ATTN_BLOCK, GROUP_LANES), 1) & (ATTN_BLOCK - 1)
    return kj <= qi


def _heads_transposed(ref, g, scale=None):
    parts = []
    for a in range(GROUP // 2):
        lo = (g * GROUP // 2 + a) * PAIR
        pair = ref[:, lo:lo + PAIR].astype(F32)
        pair = (pair if scale is None else pair * scale).T
        parts += [pair[:HEAD_DIM], pair[HEAD_DIM:]]
    return jnp.concatenate(parts, axis=1).astype(MXU_DTYPE)


def _heads_back(ref, g, vt):
    for a in range(GROUP // 2):
        lo = (g * GROUP // 2 + a) * PAIR
        pair = jnp.concatenate([vt[:, (2 * a) * ATTN_BLOCK:(2 * a + 1) * ATTN_BLOCK],
                                vt[:, (2 * a + 1) * ATTN_BLOCK:(2 * a + 2) * ATTN_BLOCK]], axis=0)
        ref[:, lo:lo + PAIR] = pair.T.astype(ref.dtype)


def _kv_parts(kv_ref, g):
    ks = slice(g * HEAD_DIM, (g + 1) * HEAD_DIM)
    vs = slice(KV_WIDTH + g * HEAD_DIM, KV_WIDTH + (g + 1) * HEAD_DIM)
    return kv_ref[:, ks].astype(MXU_DTYPE), kv_ref[:, vs].astype(MXU_DTYPE)


def _sink_rows(sinks):
    return jnp.repeat(sinks.reshape(KV_HEADS, GROUP), ATTN_BLOCK, axis=1)


def _attn_fwd(pq, pkv, sinks, comm=None):
    T = pq.shape[0]
    nb = T // ATTN_BLOCK

    def body(q_ref, kvc_ref, kvp_ref, s_ref, y_ref, lse_ref):
        mask_c = _attn_mask()
        has_prev = pl.program_id(0) > 0
        for g in range(KV_HEADS):
            (kc, vc), (kp, vp) = _kv_parts(kvc_ref, g), _kv_parts(kvp_ref, g)
            qt = _heads_transposed(q_ref, g, ATTN_SCALE)
            s = jnp.where(mask_c, _nn(kc, qt), jnp.where(has_prev, _nn(kp, qt), NEG_INF))
            sink = s_ref[g:g + 1, :]
            m = jnp.maximum(jnp.max(s, axis=0, keepdims=True), sink)
            p = jnp.exp(s - m)
            den = jnp.sum(p, axis=0, keepdims=True) + jnp.exp(sink - m)
            pc = jnp.where(mask_c, p, 0.0)
            _heads_back(y_ref, g, (_tn(vc, pc) + _tn(vp, p - pc)) / den)
            lse = m + jnp.log(den)
            for i in range(GROUP):
                lse_ref[g * GROUP + i:g * GROUP + i + 1, :] = lse[:, i * ATTN_BLOCK:(i + 1) * ATTN_BLOCK]

    return _call(
        "attn_fwd", body, grid=(nb,),
        in_specs=[pl.BlockSpec((ATTN_BLOCK, D_MODEL), lambda n: (n, 0)),
                  pl.BlockSpec((ATTN_BLOCK, 2 * KV_WIDTH), lambda n: (n, 0)),
                  pl.BlockSpec((ATTN_BLOCK, 2 * KV_WIDTH), lambda n: (jnp.maximum(n - 1, 0), 0)),
                  pl.BlockSpec((KV_HEADS, GROUP_LANES), lambda n: (0, 0))],
        out_specs=[pl.BlockSpec((ATTN_BLOCK, D_MODEL), lambda n: (n, 0)),
                   pl.BlockSpec((Q_HEADS, ATTN_BLOCK), lambda n: (0, n))],
        out_shape=[jax.ShapeDtypeStruct((T, D_MODEL), MXU_DTYPE), jax.ShapeDtypeStruct((Q_HEADS, T), F32)],
        args=[pq, pkv, pkv, _sink_rows(sinks)], semantics=("parallel",), comm=comm)


def _attn_bwd(pq, pkv, sinks, lse, dy, d_in, comm=None):
    T = pq.shape[0]
    nb = T // ATTN_BLOCK
    cur = lambda n: (jnp.minimum(n, nb - 1), 0)
    done = D_MODEL + 2 * KV_WIDTH

    def body(q_ref, kvc_ref, kvp_ref, s_ref, lse_ref, dy_ref, _, out_ref, ds_ref, carry, top, bot, dq_ref):
        n = pl.program_id(0)

        @pl.when(n == 0)
        def _():
            carry[...] = jnp.zeros_like(carry)
            dq_ref[...] = jnp.zeros_like(dq_ref)
            ds_ref[...] = jnp.zeros_like(ds_ref)

        out_ref[:, :D_MODEL] = dq_ref[...]

        @pl.when(n < nb)
        def _():
            mask_c = _attn_mask()
            valid = jnp.logical_or(mask_c, n > 0)
            for g in range(KV_HEADS):
                ks = slice(g * HEAD_DIM, (g + 1) * HEAD_DIM)
                vs = slice(KV_WIDTH + g * HEAD_DIM, KV_WIDTH + (g + 1) * HEAD_DIM)
                (kc, vc), (kp, vp) = _kv_parts(kvc_ref, g), _kv_parts(kvp_ref, g)
                qt = _heads_transposed(q_ref, g, ATTN_SCALE)
                dot = _heads_transposed(dy_ref, g)
                lse = jnp.concatenate([lse_ref[g * GROUP + i:g * GROUP + i + 1, :] for i in range(GROUP)], axis=1)
                p = jnp.where(valid, jnp.exp(jnp.where(mask_c, _nn(kc, qt), _nn(kp, qt)) - lse), 0.0)
                dp = jnp.where(mask_c, _nn(vc, dot), _nn(vp, dot))
                delta = jnp.sum(p * dp, axis=0, keepdims=True)
                ds = p * (dp - delta)
                ds_c, p_c = jnp.where(mask_c, ds, 0.0), jnp.where(mask_c, p, 0.0)
                ds_p, p_p = ds - ds_c, p - p_c
                _heads_back(dq_ref, g, (_tn(kc, ds_c) + _tn(kp, ds_p)) * ATTN_SCALE)
                bot[:, ks], bot[:, vs] = _nt(ds_c, qt), _nt(p_c, dot)
                top[:, ks], top[:, vs] = _nt(ds_p, qt), _nt(p_p, dot)
                ds_ref[g:g + 1, :] -= jnp.exp(s_ref[g:g + 1, :] - lse) * delta
            out_ref[:, D_MODEL:] = (carry[...] + top[...]).astype(out_ref.dtype)
            carry[...] = bot[...]

        @pl.when(n == nb)
        def _():
            out_ref[:, D_MODEL:] = carry[...].astype(out_ref.dtype)

    return _call(
        "attn_bwd", body, grid=(nb + 1,),
        in_specs=[pl.BlockSpec((ATTN_BLOCK, D_MODEL), cur),
                  pl.BlockSpec((ATTN_BLOCK, 2 * KV_WIDTH), cur),
                  pl.BlockSpec((ATTN_BLOCK, 2 * KV_WIDTH), lambda n: (jnp.maximum(jnp.minimum(n, nb - 1) - 1, 0), 0)),
                  pl.BlockSpec((KV_HEADS, GROUP_LANES), lambda n: (0, 0)),
                  pl.BlockSpec((Q_HEADS, ATTN_BLOCK), lambda n: (0, jnp.minimum(n, nb - 1))),
                  pl.BlockSpec((ATTN_BLOCK, D_MODEL), cur), HBM_SPEC],
        out_specs=[pl.BlockSpec((ATTN_BLOCK, done), lambda n: (jnp.maximum(n - 1, 0), 0)),
                   pl.BlockSpec((KV_HEADS, GROUP_LANES), lambda n: (0, 0))],
        out_shape=[jax.ShapeDtypeStruct(d_in.shape, d_in.dtype), jax.ShapeDtypeStruct((KV_HEADS, GROUP_LANES), F32)],
        scratch=[pltpu.VMEM((ATTN_BLOCK, 2 * KV_WIDTH), F32)] * 3 + [pltpu.VMEM((ATTN_BLOCK, D_MODEL), MXU_DTYPE)],
        args=[pq, pkv, pkv, _sink_rows(sinks), lse, dy, d_in], semantics=("arbitrary",), comm=comm, aliases={6: 0})


def _lower_bound(l):
    m = jnp.maximum(l[0:1], l[1:2])
    e0, e1 = jnp.exp(l[0:1] - m), jnp.exp(l[1:2] - m)
    return e0 / (e0 + e1)


def _tri(lower):
    r = lax.broadcasted_iota(jnp.int32, (CHUNK, CHUNK), 0)
    c = lax.broadcasted_iota(jnp.int32, (CHUNK, CHUNK), 1)
    return (r >= c) if lower else (c >= r)


def _chunk_sum(mask, v):
    ones = mask.astype(BF16)
    hi = v.astype(BF16)
    rest = v - hi.astype(F32)
    mid = rest.astype(BF16)
    lo = (rest - mid.astype(F32)).astype(BF16)
    part = lambda t: lax.dot_general(ones, t, (((1,), (0,)), ((), ())), preferred_element_type=F32)
    return part(hi) + part(mid) + part(lo)


def _hgrn_chunk_inputs(hq, hf, lb, causal):
    half_t = 0.5 * jnp.tanh(0.5 * hf)
    sg, sgn = 0.5 + half_t, 0.5 - half_t
    f = lb + (1.0 - lb) * sg
    kk = (1.0 - lb) * sgn
    sq = _sigmoid(hq)
    q = hq * sq
    b = _chunk_sum(causal, jnp.log(f))
    bm, bl = b[CHUNK // 2 - 1:CHUNK // 2, :], b[CHUNK - 1:CHUNK, :]
    e_qm, e_km = jnp.exp(b - bm), jnp.exp(bm - b)
    e_qs, e_kl = e_qm * jnp.exp(bm), e_km * jnp.exp(bl - bm)
    return dict(sg=sg, sgn=sgn, f=f, kk=kk, sq=sq, q=q, e_qm=e_qm, e_km=e_km, e_qs=e_qs, e_kl=e_kl,
                qm=q * e_qm, km=kk * e_km, qs=q * e_qs, kl=kk * e_kl, el=jnp.exp(bl))


def _hgrn_fwd(ph, lb_logits, norm_g, comm=None):
    T = ph.shape[0]
    nblk, cpb = T // HGRN_TOKENS, HGRN_TOKENS // CHUNK
    col = lambda c: pl.BlockSpec((HGRN_TOKENS, D_MODEL), functools.partial(lambda i, c: (i, c), c=c))

    def body(hq_ref, hf_ref, hi_ref, hg_ref, l_ref, ng_ref, y_ref, o_ref, st_ref, s_ref):
        @pl.when(pl.program_id(0) == 0)
        def _():
            s_ref[...] = jnp.zeros_like(s_ref)

        lb = _lower_bound(l_ref[...])
        causal = _tri(True)
        for c in range(cpb):
            rows = slice(c * CHUNK, (c + 1) * CHUNK)
            t = _hgrn_chunk_inputs(hq_ref[rows, :], hf_ref[rows, :], lb, causal)
            qm, km, qs, kl = (t[n].astype(MXU_DTYPE) for n in ("qm", "km", "qs", "kl"))
            v = hi_ref[rows, :].astype(MXU_DTYPE)
            heads = [slice(h * HGRN_K, (h + 1) * HGRN_K) for h in range(HGRN_HEADS)]
            a_all = [jnp.where(causal, _nt(qm[:, ls], km[:, ls]), 0.0).astype(MXU_DTYPE) for ls in heads]
            for h, ls in enumerate(heads):
                st = s_ref[h]
                st_ref[c, ls, :] = st
                o_ref[rows, ls] = _nn(a_all[h], v[:, ls]) + _nt(qs[:, ls], st)
                s_ref[h] = t["el"][:, ls] * st + _tn(v[:, ls], kl[:, ls])
        for h in range(HGRN_HEADS):
            ls = slice(h * HGRN_K, (h + 1) * HGRN_K)
            o = o_ref[:, ls]
            r = lax.rsqrt(jnp.mean(o * o, axis=-1, keepdims=True) + EPS)
            y_ref[:, ls] = (o * r * ng_ref[:, ls] * _sigmoid(hg_ref[:, ls])).astype(y_ref.dtype)

    return _call(
        "hgrn_fwd", body, grid=(nblk,),
        in_specs=[col(0), col(1), col(2), col(3),
                  pl.BlockSpec((2, D_MODEL), lambda i: (0, 0)), pl.BlockSpec((1, D_MODEL), lambda i: (0, 0))],
        out_specs=[pl.BlockSpec((HGRN_TOKENS, D_MODEL), lambda i: (i, 0)),
                   pl.BlockSpec((HGRN_TOKENS, D_MODEL), lambda i: (i, 0)),
                   pl.BlockSpec((cpb, D_MODEL, HGRN_K), lambda i: (i, 0, 0))],
        out_shape=[jax.ShapeDtypeStruct((T, D_MODEL), MXU_DTYPE), jax.ShapeDtypeStruct((T, D_MODEL), F32),
                   jax.ShapeDtypeStruct((T // CHUNK, D_MODEL, HGRN_K), F32)],
        scratch=[pltpu.VMEM((HGRN_HEADS, HGRN_K, HGRN_K), F32)],
        args=[ph, ph, ph, ph, lb_logits, norm_g], semantics=("arbitrary",), comm=comm)


def _hgrn_bwd(ph, o_raw, states, dy, lb_logits, norm_g, d_in, first, comm=None):
    T = ph.shape[0]
    nblk, cpb = T // HGRN_TOKENS, HGRN_TOKENS // CHUNK
    rev = lambda i: nblk - 1 - i
    col = lambda c: pl.BlockSpec((HGRN_TOKENS, D_MODEL), functools.partial(lambda i, c: (rev(i), c), c=c))
    tok = pl.BlockSpec((HGRN_TOKENS, D_MODEL), lambda i: (rev(i), 0))

    def body(hq_ref, hf_ref, hi_ref, hg_ref, o_ref, st_ref, dy_ref, l_ref, ng_ref, _,
             dph_ref, dng_ref, dl_ref, dst_ref, dlb_ref, do_s, dqm_s, dkm_s, dqs_s, dkl_s, dv_s, del_s):
        i = pl.program_id(0)

        @pl.when(i == 0)
        def _():
            dst_ref[...] = jnp.zeros_like(dst_ref)
            dlb_ref[...] = jnp.zeros_like(dlb_ref)
            dng_ref[...] = jnp.zeros_like(dng_ref)

        lb = _lower_bound(l_ref[...])
        causal, anti = _tri(True), _tri(False)
        row = lax.broadcasted_iota(jnp.int32, (CHUNK, D_MODEL), 0)
        for c in reversed(range(cpb)):
            rows = slice(c * CHUNK, (c + 1) * CHUNK)
            hq = hq_ref[rows, :]
            t = _hgrn_chunk_inputs(hq, hf_ref[rows, :], lb, causal)
            sgg = _sigmoid(hg_ref[rows, :])
            dyv = dy_ref[rows, :]
            for h in range(HGRN_HEADS):
                ls = slice(h * HGRN_K, (h + 1) * HGRN_K)
                o = o_ref[rows, ls]
                r = lax.rsqrt(jnp.mean(o * o, axis=-1, keepdims=True) + EPS)
                nrm = o * r
                g_h = sgg[:, ls]
                dph_ref[rows, 3 * D_MODEL + h * HGRN_K:3 * D_MODEL + (h + 1) * HGRN_K] = (
                    dyv[:, ls] * nrm * ng_ref[:, ls] * g_h * (1.0 - g_h)).astype(dph_ref.dtype)
                dyg = dyv[:, ls] * g_h
                dng_ref[:, ls] += jnp.sum(dyg * nrm, axis=0, keepdims=True)
                dn = dyg * ng_ref[:, ls]
                do_s[:, ls] = r * (dn - nrm * jnp.mean(dn * nrm, axis=-1, keepdims=True))
            qm, km, qs, kl = (t[n].astype(MXU_DTYPE) for n in ("qm", "km", "qs", "kl"))
            v = hi_ref[rows, :].astype(MXU_DTYPE)
            do = do_s[...].astype(MXU_DTYPE)
            heads = [slice(h * HGRN_K, (h + 1) * HGRN_K) for h in range(HGRN_HEADS)]
            a_all = [jnp.where(causal, _nt(qm[:, ls], km[:, ls]), 0.0).astype(MXU_DTYPE) for ls in heads]
            da_all = [jnp.where(causal, _nt(do[:, ls], v[:, ls]), 0.0).astype(MXU_DTYPE) for ls in heads]
            for h, ls in enumerate(heads):
                st = st_ref[c, ls, :]
                dst = dst_ref[h]
                a, da = a_all[h], da_all[h]
                dv_s[:, ls] = _tn(a, do[:, ls]) + _nt(kl[:, ls], dst)
                dkl_s[:, ls] = _nn(v[:, ls], dst)
                dqs_s[:, ls] = _nn(do[:, ls], st)
                del_s[:, ls] = jnp.sum(dst * st, axis=0, keepdims=True)
                dst_ref[h] = _tn(do[:, ls], qs[:, ls]) + t["el"][:, ls] * dst
                dqm_s[:, ls] = _nn(da, km[:, ls])
                dkm_s[:, ls] = _tn(da, qm[:, ls])
            dqm, dkm, dqs, dkl = dqm_s[...], dkm_s[...], dqs_s[...], dkl_s[...]
            dq = dqm * t["e_qm"] + dqs * t["e_qs"]
            dk = dkm * t["e_km"] + dkl * t["e_kl"]
            t_qm, t_km, t_kl = dqm * t["qm"], dkm * t["km"], dkl * t["kl"]
            db = t_qm - t_km + dqs * t["qs"] - t_kl
            db_mid = jnp.sum(t_km - t_qm, axis=0, keepdims=True)
            db_last = jnp.sum(t_kl, axis=0, keepdims=True) + del_s[...] * t["el"]
            db = db + jnp.where(row == CHUNK // 2 - 1, db_mid, 0.0) + jnp.where(row == CHUNK - 1, db_last, 0.0)
            dlogf = _chunk_sum(anti, db)
            sq, sg, sgn, f = t["sq"], t["sg"], t["sgn"], t["f"]
            dph_ref[rows, 0:D_MODEL] = (dq * (sq * (1.0 + hq * (1.0 - sq)))).astype(dph_ref.dtype)
            dph_ref[rows, D_MODEL:2 * D_MODEL] = (
                dlogf * (1.0 - lb) * sg * (1.0 - sg) / f - dk * (1.0 - lb) * sgn * (1.0 - sgn)).astype(dph_ref.dtype)
            dph_ref[rows, 2 * D_MODEL:3 * D_MODEL] = dv_s[...].astype(dph_ref.dtype)
            dlb_ref[...] += jnp.sum(dlogf * (1.0 - sg) / f - dk * sgn, axis=0, keepdims=True)

        @pl.when(i == nblk - 1)
        def _():
            dl0 = dlb_ref[...] * lb * (1.0 - lb)
            dl_ref[0:1, :] = dl0
            dl_ref[1:2, :] = -dl0

    wide = pltpu.VMEM((CHUNK, D_MODEL), F32)
    return _call(
        "hgrn_bwd", body, grid=(nblk,),
        in_specs=[col(0), col(1), col(2), col(3), tok,
                  pl.BlockSpec((cpb, D_MODEL, HGRN_K), lambda i: (rev(i), 0, 0)), tok,
                  pl.BlockSpec((2, D_MODEL), lambda i: (0, 0)), pl.BlockSpec((1, D_MODEL), lambda i: (0, 0)), HBM_SPEC],
        out_specs=[pl.BlockSpec((pl.Element(HGRN_TOKENS), pl.Element(4 * D_MODEL)), lambda i: (
                       pl.multiple_of(rev(i) * HGRN_TOKENS, ROW_ALIGN), first)),
                   pl.BlockSpec((1, D_MODEL), lambda i: (0, 0)), pl.BlockSpec((2, D_MODEL), lambda i: (0, 0))],
        out_shape=[jax.ShapeDtypeStruct(d_in.shape, d_in.dtype), jax.ShapeDtypeStruct((1, D_MODEL), F32),
                   jax.ShapeDtypeStruct((2, D_MODEL), F32)],
        scratch=[pltpu.VMEM((HGRN_HEADS, HGRN_K, HGRN_K), F32), pltpu.VMEM((1, D_MODEL), F32),
                 wide, wide, wide, wide, wide, wide, pltpu.VMEM((1, D_MODEL), F32)],
        args=[ph, ph, ph, ph, o_raw, states, dy, lb_logits, norm_g, d_in], semantics=("arbitrary",), comm=comm,
        aliases={9: 0})


def _local_step(x, target, vec, net):
    T, D = x.shape
    norm_mix_g, b_in, sinks, lb_logits = vec["norm_mix_g"], vec["b_in"], vec["attn_sinks"], vec["hgrn_lb_logits"]
    hgrn_norm_g, norm_ffn_g, norm_final_g = vec["hgrn_norm_g"], vec["norm_ffn_g"], vec["norm_final_g"]
    w_in = net.full("w_in")
    o_q, o_kv, o_h, o_g = (sum(IN_SPLITS[:i]) for i in range(4))
    TM = 512

    names = ("w_ffn_gate",)
    (u, pq, pkv, ph, pg), got = _in_proj(x, norm_mix_g, w_in, b_in, tm=256, comm=net.gather(names))
    net.gathered(names, got)
    names = ("w_branch_attn", "w_branch_hgrn")
    (y_attn, lse), got = _attn_fwd(pq, pkv, sinks, comm=net.gather(names))
    net.gathered(names, got)
    names = ("w_ffn_up",)
    (y_hgrn, o_raw, states), got = _hgrn_fwd(ph, lb_logits, hgrn_norm_g, comm=net.gather(names))
    net.gathered(names, got)
    w_ba, w_bh = net.full("w_branch_attn"), net.full("w_branch_hgrn")
    w_gate, w_up = net.full("w_ffn_gate"), net.full("w_ffn_up")
    names = ("w_out",)
    (ya, yb, merged), got = _merge_fwd(y_attn, y_hgrn, w_ba, w_bh, pg, tm=TM, comm=net.gather(names))
    net.gathered(names, got)
    w_out = net.full("w_out")

    names = ("w_ffn_down",)
    (h1, u2, gpre, up, z), got = _ffn_fwd(merged, w_out, x, norm_ffn_g, w_gate, w_up, tm=256,
                                          comm=net.gather(names))
    net.gathered(names, got)
    w_down = net.full("w_ffn_down")

    dh2, dh2b, dgp, dup, dgf_p, loss_p = _ffn_tail(z, w_down, h1, target, norm_final_g, gpre, up, tm=512)
    loss = jnp.sum(loss_p.reshape(-1, 8, D)[:, 0, 0])
    d_norm_final = _colsum_partials(dgf_p)
    d_w_down = _weight_grad("dw_down", z, dh2b, tm=DW_ROWS, tk=T)

    names, swap = ("w_ffn_down",), ()
    (dh1, dh1b, dg2_p), got = _ffn_in_bwd(dgp, dup, w_gate, w_up, h1, norm_ffn_g, dh2, tm=256,
                                          comm=net.exchange(dict(w_ffn_down=[d_w_down]), swap))
    net.received(names, swap, got)
    d_norm_ffn = _colsum_partials(dg2_p)
    d_w_gate = _weight_grad("dw_gate", dgp, u2, tm=DW_ROWS, tk=T)
    d_w_up = _weight_grad("dw_up", dup, u2, tm=DW_ROWS, tk=T)

    rows_in = sum(IN_SPLITS)
    dya, dyb, dy_attn, dy_hgrn, d_in = _merge_bwd(dh1b, w_out, w_ba, w_bh, ya, yb, pg, tm=TM, d_in_width=rows_in,
                                                  first=o_g)
    d_w_out = _weight_grad("dw_out", merged, dh1b, tm=1024, tk=1024)
    d_w_ba = _weight_grad("dw_branch_a", y_attn, dya, tm=1024, tk=1024)
    d_w_bh = _weight_grad("dw_branch_b", y_hgrn, dyb, tm=1024, tk=1024)

    names, swap = ("w_ffn_gate",), ("w_ffn_down",)
    (d_in, dsink), got = _attn_bwd(pq, pkv, sinks, lse, dy_attn, d_in,
                                   comm=net.exchange(dict(w_ffn_gate=[d_w_gate]), swap))
    net.received(names, swap, got)
    names, swap = ("w_ffn_up", "w_out"), ("w_ffn_gate",)
    (d_in, d_hgrn_norm, d_lb_logits), got = _hgrn_bwd(
        ph, o_raw, states, dy_hgrn, lb_logits, hgrn_norm_g, d_in, o_h,
        comm=net.exchange(dict(w_ffn_up=[d_w_up], w_out=[d_w_out]), swap))
    net.received(names, swap, got)

    names, swap = ("w_branch_attn", "w_branch_hgrn"), ("w_ffn_up", "w_out")
    (*d_w_in, d_b_in), got = _weight_grad(
        "dw_in", d_in, u, tm=DW_ROWS, tk=T, a_colsum=True, carrying=True,
        comm=net.exchange(dict(w_branch_attn=[d_w_ba], w_branch_hgrn=[d_w_bh]), swap))
    net.received(names, swap, got)
    d_w_in = [tuple(d_w_in)]

    first_level = net.presum_begin("w_in", d_w_in)
    halves = net.presum_end("w_in", [] if first_level is None else _copies_alone("presum_swap_w_in", first_level))
    names, swap = ("w_in",), ("w_branch_attn", "w_branch_hgrn")
    (dx, dg1_p), got = _in_proj_bwd([(d_in, 0)], w_in, x, norm_mix_g, dh1, tm=256,
                                    comm=_join(halves, net.swap(swap)))
    net.last = (names, swap, got)
    d_norm_mix = _colsum_partials(dg1_p)
    vecs = dict(norm_mix_g=d_norm_mix, b_in=d_b_in, attn_sinks=jnp.sum(dsink.reshape(Q_HEADS, ATTN_BLOCK), axis=1).reshape(1, Q_HEADS),
                hgrn_lb_logits=d_lb_logits,
                hgrn_norm_g=d_hgrn_norm, norm_ffn_g=d_norm_ffn, norm_final_g=d_norm_final)
    return loss, dx, vecs


def _place():
    return lax.axis_index("x"), lax.axis_index("y"), lax.axis_index("c")


def _other_chips(x, y):
    return [(1 - x, y), (x, 1 - y), (1 - x, 1 - y)]


def _y_first(copies):
    return [copies[3 * (i // 3) + (1, 0, 2)[i % 3]] for i in range(len(copies))]


def _gather_copies(shards):
    n = len(shards)

    def build(ins, outs, send_sems, recv_sems, local_sems):
        x, y, c = _place()
        mine = 2 * x + y
        local = [pltpu.make_async_copy(ins[w], outs[w].at[mine], local_sems.at[w]) for w in range(n)]
        sends, recvs = [], []
        for w in range(n):
            for k, (px, py) in enumerate(_other_chips(x, y)):
                sem = 3 * w + k
                sends.append(pltpu.make_async_remote_copy(
                    src_ref=ins[w], dst_ref=outs[w].at[mine], send_sem=send_sems.at[sem], recv_sem=recv_sems.at[sem],
                    device_id=(px, py, c), device_id_type=MESH_ID))
                recvs.append(pltpu.make_async_remote_copy(
                    src_ref=ins[w], dst_ref=outs[w].at[2 * px + py], send_sem=send_sems.at[sem],
                    recv_sem=recv_sems.at[sem], device_id=(px, py, c), device_id_type=MESH_ID))
        return sends, recvs, local, _y_first(sends)

    return _Carried(shards, [jax.ShapeDtypeStruct((N_CHIPS,) + s.shape, s.dtype) for s in shards], 3 * n, n, build)


def _grad_copies(stacked):
    n = len(stacked)

    def build(ins, outs, send_sems, recv_sems, local_sems):
        x, y, c = _place()
        sends = []
        for w in range(n):
            for k, (px, py) in enumerate(_other_chips(x, y)):
                sem = 3 * w + k
                sends.append(pltpu.make_async_remote_copy(
                    src_ref=ins[w].at[2 * px + py], dst_ref=outs[w].at[k], send_sem=send_sems.at[sem],
                    recv_sem=recv_sems.at[sem], device_id=(px, py, c), device_id_type=MESH_ID))
        return sends, sends, [], _y_first(sends)

    return _Carried(stacked, [jax.ShapeDtypeStruct((3,) + s.shape[1:], s.dtype) for s in stacked], 3 * n, 0, build)


def _small_copies(small):
    def build(ins, outs, send_sems, recv_sems, local_sems):
        small_ref, all_ref = ins[0], outs[0]
        x, y, c = _place()
        me = 4 * x + 2 * y + c
        sends, recvs = [], []
        for r in range(1, 8):
            px = 1 - x if r & 4 else x
            py = 1 - y if r & 2 else y
            pc = 1 - c if r & 1 else c
            sends.append(pltpu.make_async_remote_copy(
                src_ref=small_ref, dst_ref=all_ref.at[me], send_sem=send_sems.at[r - 1], recv_sem=recv_sems.at[r - 1],
                device_id=(px, py, pc), device_id_type=MESH_ID))
            recvs.append(pltpu.make_async_remote_copy(
                src_ref=small_ref, dst_ref=all_ref.at[4 * px + 2 * py + pc], send_sem=send_sems.at[r - 1],
                recv_sem=recv_sems.at[r - 1], device_id=(px, py, pc), device_id_type=MESH_ID))
        return sends, recvs, [pltpu.make_async_copy(small_ref, all_ref.at[me], local_sems.at[0])]

    return _Carried([small], [jax.ShapeDtypeStruct((8,) + small.shape, small.dtype)], 7, 1, build)


def _gather_by_neighbours(name, shard):
    half = shard.shape[0] // 2
    quarter = half // 2

    def body(in_ref, out_ref, send_sems, recv_sems, local_sem):
        for core in (0, 1):
            @pl.when(lax.axis_index("c") == core)
            def _():
                program(core, in_ref, out_ref, send_sems, recv_sems, local_sem)

    def program(c, in_ref, out_ref, send_sems, recv_sems, local_sem):
        x, y, _ = _place()
        chip = lambda px, py: 2 * px + py
        to_x, to_y, sibling = (1 - x, y, c), (x, 1 - y, c), (x, y, 1 - c)
        x_blk, y_blk, d_blk = chip(1 - x, y), chip(x, 1 - y), chip(1 - x, 1 - y)
        mine, theirs = c * half, (1 - c) * half

        def copy(sem, rows, block, to, src=None):
            place = out_ref.at[block, pl.ds(rows[0], rows[1])]
            return pltpu.make_async_remote_copy(
                src_ref=place if src is None else src, dst_ref=place, send_sem=send_sems.at[sem],
                recv_sem=recv_sems.at[sem], device_id=to, device_id_type=MESH_ID)

        own = pltpu.make_async_copy(in_ref, out_ref.at[chip(x, y)], local_sem)
        own.start()
        my_rows = in_ref.at[pl.ds(mine, half)]
        along_x = dict(send=copy(0, (mine, half), chip(x, y), to_x, src=my_rows),
                       landed=copy(0, (mine, half), x_blk, to_x),
                       onward=[copy(3, (mine + quarter, quarter), x_blk, to_y), copy(4, (mine, half), x_blk, sibling)],
                       diagonal=copy(2, (mine, quarter), d_blk, to_x))
        along_y = dict(send=copy(1, (mine, half), chip(x, y), to_y, src=my_rows),
                       landed=copy(1, (mine, half), y_blk, to_y),
                       onward=[copy(2, (mine, quarter), y_blk, to_x), copy(5, (mine, half), y_blk, sibling)],
                       diagonal=copy(3, (mine + quarter, quarter), d_blk, to_y))
        last = copy(6, (mine, half), d_blk, sibling)

        order = (along_x, along_y) if c == 0 else (along_y, along_x)
        for axis in order:
            axis["send"].start()
        for axis in order:
            axis["landed"].wait_recv()
            for cp in axis["onward"]:
                cp.start()
        for axis in order:
            axis["diagonal"].wait_recv()
        last.start()
        for sem, block in ((4, x_blk), (5, y_blk), (6, d_blk)):
            copy(sem, (theirs, half), block, sibling).wait_recv()
        for cp in [along_x["send"], along_y["send"]] + along_x["onward"] + along_y["onward"] + [last]:
            cp.wait_send()
        own.wait()

    return pl.pallas_call(
        body, name=name, in_specs=[HBM_SPEC], out_specs=HBM_SPEC,
        out_shape=jax.ShapeDtypeStruct((N_CHIPS,) + shard.shape, shard.dtype),
        scratch_shapes=[pltpu.SemaphoreType.DMA((7,)), pltpu.SemaphoreType.DMA((7,)), pltpu.SemaphoreType.DMA(())],
    )(shard)


def _copies_alone(name, comm):
    return _call(name, lambda: None, grid=(), in_specs=[], out_specs=[], out_shape=[], args=[], comm=comm)[1]


class _Net:
    def __init__(self, shards):
        self.shards = shards
        self.whole, self.own, self.theirs, self.sums, self.other = {}, {}, {}, {}, {}
        x, y, _ = _place()
        self.chip = 2 * x + y

    def gather(self, names):
        return _gather_copies([self.shards[n] for n in names])

    def gathered(self, names, got):
        for n, g in zip(names, got):
            self.whole[n] = g.reshape(-1, g.shape[-1])

    def full(self, name):
        return self.whole[name]

    def exchange(self, grads, swap=()):
        stacked = []
        for n, pieces in grads.items():
            (keep, send), = pieces
            self.own[n] = keep
            stacked.append(send.reshape(N_CHIPS, keep.shape[0] // N_CHIPS, send.shape[-1]))
        return _join(_grad_copies(stacked), self.swap(swap))

    def swap(self, names):
        return _sibling_copies([self.sums[n] for n in names]) if names else None

    def presum_begin(self, name, pieces):
        keep = jnp.concatenate([p[0] for p in pieces], axis=0) if len(pieces) > 1 else pieces[0][0]
        send = jnp.concatenate([p[1] for p in pieces], axis=0) if len(pieces) > 1 else pieces[0][1]
        rows = keep.shape[0] // N_CHIPS
        self.held = keep.reshape(N_CHIPS, rows, keep.shape[-1])
        return _half_rows_copies(send.reshape(N_CHIPS, rows, send.shape[-1]))

    def presum_end(self, name, got):
        x, y, c = _place()
        to_send, self.own[name] = _pre_sum("presum_" + name, self.held, got[0], jnp.stack([c, self.chip]))
        return _grad_copies([to_send])

    def received(self, names, swap, got, carried=None):
        self.theirs.update(zip(names, got[:len(names)]))
        self.other.update(zip(swap, got[len(names):]))
        for n in names:
            (self.sums[n],), more = _partial_sum("sum_" + n, self.own[n], self.theirs[n], self.chip, comm=carried)
        return more


def _half_rows_copies(stacked):
    n, rows = stacked.shape[0], stacked.shape[1] // 2

    def build(ins, outs, send_sems, recv_sems, local_sems):
        x, y, c = _place()
        copies = [pltpu.make_async_remote_copy(
            src_ref=ins[0].at[s, pl.ds((1 - c) * rows, rows)], dst_ref=outs[0].at[s], send_sem=send_sems.at[s],
            recv_sem=recv_sems.at[s], device_id=(x, y, 1 - c), device_id_type=MESH_ID) for s in range(n)]
        return copies, copies, []

    return _Carried([stacked], [jax.ShapeDtypeStruct((n, rows, stacked.shape[2]), stacked.dtype)], n, 0, build)


def _pre_sum(name, held, theirs, core_and_chip):
    n, R, C = held.shape
    half = R // 2
    tr = _row_tile(half)
    per_half = half // tr

    def body(place_ref, h_ref, t_ref, send_ref, own_ref):
        total = h_ref[0] + t_ref[0].astype(F32)
        send_ref[0] = total.astype(send_ref.dtype)

        @pl.when(pl.program_id(1) == place_ref[1])
        def _():
            own_ref[...] = total

    return pl.pallas_call(
        body, name=name,
        grid_spec=pltpu.PrefetchScalarGridSpec(
            num_scalar_prefetch=1, grid=(per_half, n),
            in_specs=[pl.BlockSpec((1, tr, C), lambda i, s, place: (s, place[0] * per_half + i, 0)),
                      pl.BlockSpec((1, tr, C), lambda i, s, place: (s, i, 0))],
            out_specs=[pl.BlockSpec((1, tr, C), lambda i, s, place: (s, i, 0)),
                       pl.BlockSpec((tr, C), lambda i, s, place: (i, 0))]),
        out_shape=[jax.ShapeDtypeStruct((n, half, C), MXU_DTYPE), jax.ShapeDtypeStruct((half, C), F32)],
        compiler_params=_params(("arbitrary", "arbitrary")),
    )(core_and_chip, held, theirs)


def _sibling_copies(parts):
    n = len(parts)

    def build(ins, outs, send_sems, recv_sems, local_sems):
        x, y, c = _place()
        copies = [pltpu.make_async_remote_copy(
            src_ref=ins[w], dst_ref=outs[w], send_sem=send_sems.at[w], recv_sem=recv_sems.at[w],
            device_id=(x, y, 1 - c), device_id_type=MESH_ID) for w in range(n)]
        return copies, copies, []

    return _Carried(parts, [jax.ShapeDtypeStruct(p.shape, p.dtype) for p in parts], n, 0, build)


def _row_tile(rows, most=512, sublanes=16):
    return max(t for t in range(sublanes, min(most, rows // 2) + 1, sublanes) if rows % t == 0)


def _partial_sum(name, own, recv, chip, comm=None):
    _, R, C = recv.shape
    tr = _row_tile(R)

    def body(o_ref, r_ref, p_ref):
        p_ref[...] = ((o_ref[...] + r_ref[0].astype(F32)) + r_ref[1].astype(F32)) + r_ref[2].astype(F32)

    if own.shape[0] != R:
        assert comm is None and own.shape[0] == N_CHIPS * R
        total = pl.pallas_call(
            lambda chip_ref, *refs: body(*refs), name=name,
            grid_spec=pltpu.PrefetchScalarGridSpec(
                num_scalar_prefetch=1, grid=(R // tr,),
                in_specs=[pl.BlockSpec((tr, C), lambda i, chip_ref: (chip_ref[0] * (R // tr) + i, 0)),
                          pl.BlockSpec((3, tr, C), lambda i, chip_ref: (0, i, 0))],
                out_specs=pl.BlockSpec((tr, C), lambda i, chip_ref: (i, 0))),
            out_shape=jax.ShapeDtypeStruct((R, C), F32), compiler_params=_params(("parallel",)),
        )(chip.reshape(1), own, recv)
        return [total], []
    return _call(name, body, grid=(R // tr,),
                 in_specs=[pl.BlockSpec((tr, C), lambda i: (i, 0)), pl.BlockSpec((3, tr, C), lambda i: (0, i, 0))],
                 out_specs=[pl.BlockSpec((tr, C), lambda i: (i, 0))], out_shape=[jax.ShapeDtypeStruct((R, C), F32)],
                 args=[own, recv], semantics=("parallel",), comm=comm)


def _adam_vals(w, g, m, v):
    m = ADAM_B1 * m + (1.0 - ADAM_B1) * g
    v = ADAM_B2 * v + (1.0 - ADAM_B2) * (g * g)
    m_hat = m / (1.0 - ADAM_B1 ** ADAM_STEP)
    v_hat = v / (1.0 - ADAM_B2 ** ADAM_STEP)
    delta = -ADAM_LR * (m_hat / (jnp.sqrt(v_hat) + ADAM_EPS) + ADAM_WD * w)
    return delta, m, v


def _adamw(name, w, m, v, mine, other, comm=None):
    R, C = w.shape
    tr = _row_tile(R)

    def body(w_ref, m_ref, v_ref, s_ref, n_ref, g_ref, d_ref, nm_ref, nv_ref):
        g = s_ref[...] + n_ref[...]
        d, nm, nv = _adam_vals(w_ref[...], g, m_ref[...], v_ref[...])
        g_ref[...], d_ref[...], nm_ref[...], nv_ref[...] = g, d, nm, nv

    spec = pl.BlockSpec((tr, C), lambda i: (i, 0))
    return _call(name, body, grid=(R // tr,), in_specs=[spec] * 5, out_specs=[spec] * 4,
                 out_shape=[jax.ShapeDtypeStruct((R, C), F32)] * 4, args=[w, m, v, mine, other],
                 semantics=("parallel",), comm=comm)


def _adamw_by_halves(name, w, m, v, mine, other, core):
    R, C = w.shape
    tr = _row_tile(R // 2)
    per_half = R // 2 // tr

    def body(c_ref, w_ref, m_ref, v_ref, s_ref, n_ref, g_ref, d_ref, nm_ref, nv_ref):
        g = jnp.where(pl.program_id(0) // per_half == c_ref[0, 0], s_ref[...], n_ref[...])
        d, nm, nv = _adam_vals(w_ref[...], g, m_ref[...], v_ref[...])
        g_ref[...], d_ref[...], nm_ref[...], nv_ref[...] = g, d, nm, nv

    spec = pl.BlockSpec((tr, C), lambda i: (i, 0))
    part = pl.BlockSpec((tr, C), lambda i: (i % per_half, 0))
    return pl.pallas_call(
        body, name=name, grid=(R // tr,),
        in_specs=[pl.BlockSpec(memory_space=pltpu.SMEM), spec, spec, spec, part, part], out_specs=[spec] * 4,
        out_shape=[jax.ShapeDtypeStruct((R, C), F32)] * 4, compiler_params=_params(("parallel",)),
    )(core, w, m, v, mine, other)


SMALL_LAYOUT = dict(norm_mix_g=(0, 1, 1024), b_in=(1, 8, 7424), hgrn_norm_g=(9, 1, 1024), norm_ffn_g=(10, 1, 1024),
                    norm_final_g=(11, 1, 1024), hgrn_lb_logits=(12, 2, 2048), attn_sinks=(14, 1, 16))
SMALL_LOSS_ROW, SMALL_ROWS = 15, 16


def _pack_small(grads, loss):
    rows = [jnp.pad(grads[name].astype(F32).reshape(-1), (0, nrows * D_MODEL - n))
            for name, (_, nrows, n) in SMALL_LAYOUT.items()]
    rows.append(jnp.pad(loss.astype(F32).reshape(1), (0, D_MODEL - 1)))
    return jnp.concatenate(rows).reshape(SMALL_ROWS, D_MODEL)


def _adamw_small(w, m, v, g_all):
    names = list(SMALL_LAYOUT)
    n = len(names)

    def body(a_ref, *refs):
        ins, outs = refs[:3 * n], refs[3 * n:]
        g_all_rows = a_ref[0]
        for dev in range(1, 8):
            g_all_rows = g_all_rows + a_ref[dev]
        for i, name in enumerate(names):
            first, nrows, count = SMALL_LAYOUT[name]
            w_ref, m_ref, v_ref = ins[3 * i:3 * i + 3]
            if w_ref.shape[0] == nrows:
                g = g_all_rows[first:first + nrows, :w_ref.shape[1]]
            else:
                last = count - (nrows - 1) * D_MODEL
                g = jnp.concatenate([g_all_rows[r:r + 1, :] for r in range(first, first + nrows - 1)]
                                    + [g_all_rows[first + nrows - 1:first + nrows, :last]], axis=1)
            d, nm, nv = _adam_vals(w_ref[...], g, m_ref[...], v_ref[...])
            for o_ref, val in zip(outs[4 * i:4 * i + 4], (g, d, nm, nv)):
                o_ref[...] = val
        outs[4 * n][...] = g_all_rows[SMALL_LOSS_ROW:SMALL_LOSS_ROW + 1, 0:1]

    res = pl.pallas_call(
        body, name="adamw_small",
        out_shape=[jax.ShapeDtypeStruct(w[name].shape, F32) for name in names for _ in range(4)]
        + [jax.ShapeDtypeStruct((1, 1), F32)],
    )(g_all, *[t[name] for name in names for t in (w, m, v)])
    return {name: res[4 * i:4 * i + 4] for i, name in enumerate(names)}, res[4 * n]


MATRICES = ("w_in", "w_branch_attn", "w_branch_hgrn", "w_out", "w_ffn_gate", "w_ffn_up", "w_ffn_down")
COLUMN_SHARDED = ("w_in", "w_ffn_gate", "w_ffn_up")
WEIGHTS = ("norm_mix_g", "w_in", "b_in", "attn_sinks", "hgrn_lb_logits", "hgrn_norm_g", "w_branch_attn",
           "w_branch_hgrn", "w_out", "norm_ffn_g", "w_ffn_gate", "w_ffn_up", "w_ffn_down", "norm_final_g")


def kernel(x, norm_mix_g, w_in, b_in, attn_sinks, hgrn_lb_logits, hgrn_norm_g, w_branch_attn, w_branch_hgrn, w_out, norm_ffn_g, w_ffn_gate, w_ffn_up, w_ffn_down, norm_final_g, loss_target, m_norm_mix_g, m_w_in, m_b_in, m_attn_sinks, m_hgrn_lb_logits, m_hgrn_norm_g, m_w_branch_attn, m_w_branch_hgrn, m_w_out, m_norm_ffn_g, m_w_ffn_gate, m_w_ffn_up, m_w_ffn_down, m_norm_final_g, v_norm_mix_g, v_w_in, v_b_in, v_attn_sinks, v_hgrn_lb_logits, v_hgrn_norm_g, v_w_branch_attn, v_w_branch_hgrn, v_w_out, v_norm_ffn_g, v_w_ffn_gate, v_w_ffn_up, v_w_ffn_down, v_norm_final_g):
    given = dict(locals())
    w = {n: given[n] for n in WEIGHTS}
    m = {n: given["m_" + n] for n in WEIGHTS}
    v = {n: given["v_" + n] for n in WEIGHTS}

    block = lambda a, n: jnp.transpose(a[0]) if n in COLUMN_SHARDED else a[0]
    unblock = lambda a, n: (jnp.transpose(a) if n in COLUMN_SHARDED else a)[None]
    net = _Net({n: block(w[n], n).astype(MXU_DTYPE) for n in MATRICES})
    net.gathered(("w_in",), [_gather_by_neighbours("gather_w_in", net.shards["w_in"])])
    vec = dict(norm_mix_g=norm_mix_g, b_in=b_in, attn_sinks=attn_sinks, hgrn_lb_logits=hgrn_lb_logits,
               hgrn_norm_g=hgrn_norm_g, norm_ffn_g=norm_ffn_g, norm_final_g=norm_final_g.reshape(1, D_MODEL))
    loss_part, dx, d_vecs = _local_step(x[0], loss_target[0], vec, net)

    small_all, = net.received(*net.last, carried=_small_copies(_pack_small(d_vecs, loss_part)))
    grads, deltas, new_m, new_v = {}, {}, {}, {}
    for n in ("w_ffn_down", "w_ffn_gate", "w_ffn_up", "w_out", "w_branch_attn", "w_branch_hgrn"):
        res, got = _adamw("adamw_" + n, block(w[n], n), block(m[n], n), block(v[n], n), net.sums[n], net.other[n],
                          comm=net.swap(("w_in",)) if n == "w_ffn_down" else None)
        if n == "w_ffn_down":
            net.other["w_in"], = got
        grads[n], deltas[n], new_m[n], new_v[n] = (unblock(r, n) for r in res)
    n = "w_in"
    res = _adamw_by_halves("adamw_" + n, block(w[n], n), block(m[n], n), block(v[n], n), net.sums[n], net.other[n],
                           _place()[2].reshape(1, 1))
    grads[n], deltas[n], new_m[n], new_v[n] = (unblock(r, n) for r in res)
    rows = lambda t: {n: t[n].reshape(-1, t[n].shape[-1]) for n in SMALL_LAYOUT}
    res, loss = _adamw_small(rows(w), rows(m), rows(v), small_all)
    for n, four in res.items():
        grads[n], deltas[n], new_m[n], new_v[n] = (r.reshape(w[n].shape) for r in four)
    loss = loss.reshape(())
    return (loss, dx[None], *[grads[n] for n in WEIGHTS], *[deltas[n] for n in WEIGHTS],
            *[new_m[n] for n in WEIGHTS], *[new_v[n] for n in WEIGHTS])
```

```python
import collections
import functools
import math

import jax
import jax.numpy as jnp
from jax import lax
from jax.experimental import pallas as pl
from jax.experimental.pallas import tpu as pltpu

F32 = jnp.float32
BF16 = jnp.bfloat16
MXU_DTYPE = jnp.bfloat16
SAVED_DTYPE = jnp.bfloat16
MESH_ID = pl.DeviceIdType.MESH

D_MODEL = 1024
HEAD_DIM = 64
Q_HEADS = 16
KV_HEADS = 2
GROUP = Q_HEADS // KV_HEADS
KV_WIDTH = KV_HEADS * HEAD_DIM
ATTN_BLOCK = 128
HGRN_HEADS = 8
HGRN_K = 128
CHUNK = 64
HGRN_TOKENS = 256
FFN = 2816
IN_SPLITS = (1024, 256, 4096, 2048)
EPS = 1e-6
NEG_INF = -1e30
ADAM_LR, ADAM_B1, ADAM_B2, ADAM_EPS, ADAM_WD, ADAM_STEP = 0.001, 0.9, 0.999, 1e-08, 0.01, 10
N_CHIPS = 4
VMEM_LIMIT = 60 * 1024 * 1024
ROW_ALIGN = 16
DW_ROWS = 256


def _params(sem=None):
    return pltpu.CompilerParams(dimension_semantics=sem, vmem_limit_bytes=VMEM_LIMIT)


def _sigmoid(v):
    return 0.5 * jnp.tanh(0.5 * v) + 0.5


def _dot(a, b, dims):
    return lax.dot_general(a.astype(MXU_DTYPE), b.astype(MXU_DTYPE), (dims, ((), ())),
                           preferred_element_type=F32)


def _nn(a, b):
    return _dot(a, b, ((1,), (0,)))


def _nt(a, b):
    return _dot(a, b, ((1,), (1,)))


def _tn(a, b):
    return _dot(a, b, ((0,), (0,)))


HBM_SPEC = pl.BlockSpec(memory_space=pl.ANY)


class _Carried:
    def __init__(self, arrays, out_shapes, n_remote, n_local, build):
        self.parts = [(len(arrays), len(out_shapes), build)]
        self.arrays, self.out_shapes = list(arrays), list(out_shapes)
        self.scratch = [pltpu.SemaphoreType.DMA((n_remote,)), pltpu.SemaphoreType.DMA((n_remote,)),
                        pltpu.SemaphoreType.DMA((max(n_local, 1),))]

    def __add__(self, other):
        both = _Carried([], [], 1, 0, None)
        both.parts = self.parts + other.parts
        both.arrays, both.out_shapes = self.arrays + other.arrays, self.out_shapes + other.out_shapes
        both.scratch = self.scratch + other.scratch
        return both

    def _built(self, ins, outs, sems):
        for p, (ni, no, build) in enumerate(self.parts):
            yield build(ins[:ni], outs[:no], *sems[3 * p:3 * p + 3])
            ins, outs = ins[ni:], outs[no:]

    def start(self, ins, outs, sems):
        core = lax.axis_index("c")
        for sends, _, local, *other_order in self._built(ins, outs, sems):
            for cp in local:
                cp.start()
            if not other_order:
                for cp in sends:
                    cp.start()
                continue

            @pl.when(core == 0)
            def _():
                for cp in sends:
                    cp.start()

            @pl.when(core == 1)
            def _():
                for cp in other_order[0]:
                    cp.start()

    def wait(self, ins, outs, sems):
        for sends, recvs, local, *_ in self._built(ins, outs, sems):
            for cp in recvs:
                cp.wait_recv()
            for cp in sends:
                cp.wait_send()
            for cp in local:
                cp.wait()


def _join(*comms):
    comms = [c for c in comms if c is not None]
    return functools.reduce(lambda a, b: a + b, comms) if comms else None


def _call(name, body, *, grid, in_specs, out_specs, out_shape, args, scratch=(), semantics=None, comm=None,
          aliases=None):
    n_in, n_out, n_scr = len(in_specs), len(out_specs), len(scratch)
    aliases = aliases or {}
    if comm is None:
        res = pl.pallas_call(body, name=name, grid=grid, in_specs=in_specs, out_specs=out_specs, out_shape=out_shape,
                             scratch_shapes=list(scratch), input_output_aliases=aliases,
                             compiler_params=_params(semantics))(*args)
        return list(res), []
    ci, co = len(comm.arrays), len(comm.out_shapes)

    def carrying(*refs):
        ins, refs = refs[:n_in], refs[n_in:]
        c_ins, refs = refs[:ci], refs[ci:]
        outs, refs = refs[:n_out], refs[n_out:]
        c_outs, refs = refs[:co], refs[co:]
        scr, sems = refs[:n_scr], refs[n_scr:]
        if not grid:
            comm.start(c_ins, c_outs, sems)
            body(*ins, *outs, *scr)
            comm.wait(c_ins, c_outs, sems)
            return
        first = functools.reduce(jnp.logical_and, [pl.program_id(a) == 0 for a in range(len(grid))])
        last = functools.reduce(jnp.logical_and, [pl.program_id(a) == g - 1 for a, g in enumerate(grid)])

        @pl.when(first)
        def _():
            comm.start(c_ins, c_outs, sems)

        body(*ins, *outs, *scr)

        @pl.when(last)
        def _():
            comm.wait(c_ins, c_outs, sems)

    res = pl.pallas_call(
        carrying, name=name, grid=grid, in_specs=list(in_specs) + [HBM_SPEC] * ci,
        out_specs=list(out_specs) + [HBM_SPEC] * co, out_shape=list(out_shape) + comm.out_shapes,
        scratch_shapes=list(scratch) + comm.scratch, input_output_aliases=aliases,
        compiler_params=_params(("arbitrary",) * len(grid) if grid else None),
    )(*args, *comm.arrays)
    return list(res[:n_out]), list(res[n_out:])


_Cols = collections.namedtuple("_Cols", "array first cols")
_Into = collections.namedtuple("_Into", "rows first held")


def _weight_grad(name, a, b, *, tm, tk, a_colsum=False, into=None, carrying=False, comm=None):
    cols = a if isinstance(a, _Cols) else _Cols(a, 0, a.shape[1])
    a = cols.array
    (T, N), M = b.shape, cols.cols
    tk = min(tk, T)
    assert cols.first % tm == 0 and M % tm == 0 and T % tk == 0, (name, cols.first, M, tm, T, tk)
    ni, nk, tile0 = M // tm, T // tk, cols.first // tm
    held = list(into.held) if into is not None and into.held is not None else []

    def body(a_ref, b_ref, *rest):
        keep_ref, send_ref = rest[len(held):len(held) + 2]
        sums_ref = rest[len(held) + 2] if a_colsum else None
        if nk == 1:
            acc = _tn(a_ref[...], b_ref[...])
            keep_ref[...], send_ref[...] = acc, acc.astype(send_ref.dtype)
            if a_colsum:
                sums_ref[...] = jnp.sum(a_ref[...].astype(F32), axis=0, keepdims=True)
            return
        acc_ref = rest[-1]
        k = pl.program_id(1)

        @pl.when(k == 0)
        def _():
            acc_ref[...] = jnp.zeros_like(acc_ref)
            if a_colsum:
                sums_ref[...] = jnp.zeros((1, tm), F32)

        if a_colsum:
            sums_ref[...] += jnp.sum(a_ref[...].astype(F32), axis=0, keepdims=True)
        acc_ref[...] += _tn(a_ref[...], b_ref[...])

        @pl.when(k == nk - 1)
        def _():
            keep_ref[...], send_ref[...] = acc_ref[...], acc_ref[...].astype(send_ref.dtype)

    if into is None:
        rows, out_spec = M, pl.BlockSpec((tm, N), lambda i, k: (i, 0))
    else:
        rows = into.rows
        out_spec = pl.BlockSpec((pl.Element(tm), pl.Element(N)),
                                lambda i, k: (pl.multiple_of(into.first + i * tm, ROW_ALIGN), 0))
    out_shape = [jax.ShapeDtypeStruct((rows, N), F32), jax.ShapeDtypeStruct((rows, N), MXU_DTYPE)]
    out_specs = [out_spec, out_spec]
    if a_colsum:
        out_shape.append(jax.ShapeDtypeStruct((1, M), F32))
        out_specs.append(pl.BlockSpec((1, tm), lambda i, k: (0, i)))
    res, got = _call(
        name, body, grid=(ni, nk),
        in_specs=[pl.BlockSpec((tk, tm), lambda i, k: (k, tile0 + i)), pl.BlockSpec((tk, N), lambda i, k: (k, 0))]
        + [HBM_SPEC] * len(held),
        out_specs=out_specs, out_shape=out_shape, scratch=[pltpu.VMEM((tm, N), F32)] if nk > 1 else [],
        args=[a, b] + held, aliases={2 + p: p for p in range(len(held))}, semantics=("parallel", "arbitrary"),
        comm=comm)
    return (res, got) if carrying else res


def _ffn_fwd(merged, w_out, x, gain, w_gate_t, w_up_t, *, tm, comm=None):
    (T, D), F = x.shape, w_gate_t.shape[0]

    def body(m_ref, wo_ref, x_ref, g_ref, wg_ref, wu_ref, h_ref, u_ref, gate_ref, up_ref, z_ref):
        h = x_ref[...] + _nn(m_ref[...], wo_ref[...])
        h_ref[...] = h
        u = (h * lax.rsqrt(jnp.mean(h * h, axis=-1, keepdims=True) + EPS) * g_ref[...]).astype(u_ref.dtype)
        u_ref[...] = u
        gate, up = _nt(u, wg_ref[...]), _nt(u, wu_ref[...])
        gate_ref[...], up_ref[...] = gate.astype(gate_ref.dtype), up.astype(up_ref.dtype)
        z_ref[...] = (gate * _sigmoid(gate) * up).astype(z_ref.dtype)

    rows = lambda n: pl.BlockSpec((tm, n), lambda i: (i, 0))
    fixed = _fixed_spec
    return _call("ffn_hidden", body, grid=(T // tm,),
                 in_specs=[rows(D), fixed(w_out), rows(D), fixed(gain), fixed(w_gate_t), fixed(w_up_t)],
                 out_specs=[rows(D), rows(D), rows(F), rows(F), rows(F)],
                 out_shape=[jax.ShapeDtypeStruct((T, D), F32), jax.ShapeDtypeStruct((T, D), MXU_DTYPE)]
                 + [jax.ShapeDtypeStruct((T, F), SAVED_DTYPE)] * 2 + [jax.ShapeDtypeStruct((T, F), MXU_DTYPE)],
                 args=[merged, w_out, x, gain, w_gate_t, w_up_t], semantics=("parallel",), comm=comm)


def _in_proj(x, gain, w_in_t, b_in, *, tm, comm=None):
    T, D = x.shape
    bounds = [sum(IN_SPLITS[:i]) for i in range(len(IN_SPLITS) + 1)]

    def body(x_ref, g_ref, w_ref, b_ref, u_ref, *piece_refs):
        xv = x_ref[...]
        r = lax.rsqrt(jnp.mean(xv * xv, axis=-1, keepdims=True) + EPS)
        u = (xv * r * g_ref[...]).astype(u_ref.dtype)
        u_ref[...] = u
        for o_ref, lo, hi in zip(piece_refs, bounds[:-1], bounds[1:]):
            o_ref[...] = (_nt(u, w_ref[lo:hi, :]) + b_ref[:, lo:hi]).astype(o_ref.dtype)

    rows = lambda n: pl.BlockSpec((tm, n), lambda i: (i, 0))
    fixed = _fixed_spec
    dtypes = (MXU_DTYPE, MXU_DTYPE, F32, F32)
    return _call("in_proj", body, grid=(T // tm,),
                 in_specs=[rows(D), fixed(gain), fixed(w_in_t), fixed(b_in)],
                 out_specs=[rows(D)] + [rows(n) for n in IN_SPLITS],
                 out_shape=[jax.ShapeDtypeStruct((T, D), MXU_DTYPE)]
                 + [jax.ShapeDtypeStruct((T, n), dt) for n, dt in zip(IN_SPLITS, dtypes)],
                 args=[x, gain, w_in_t, b_in], semantics=("parallel",), comm=comm)


def _row_spec(tm, n):
    return pl.BlockSpec((tm, n), lambda i: (i, 0))


def _fixed_spec(a):
    return pl.BlockSpec(a.shape, lambda i: (0,) * a.ndim, pipeline_mode=pl.Buffered(1))


def _partials_spec(n):
    return pl.BlockSpec((8, n), lambda i: (i, 0))


def _row_parts(tm, parts):
    assert tm % parts == 0, (tm, parts)
    return [slice(p * (tm // parts), (p + 1) * (tm // parts)) for p in range(parts)]


def _ffn_tail(z, w_down, h1, target, gain, gate, up, *, tm, parts=2):
    (T, F), D = z.shape, h1.shape[1]

    def body(z_ref, w_ref, h_ref, t_ref, g_ref, gate_ref, up_ref, dh_ref, dhb_ref, dgate_ref, dup_ref, dg_ref, l_ref):
        part, dgain = 0.0, 0.0
        pieces = _row_parts(tm, parts)
        h2s = [h_ref[rows, :] + _nn(z_ref[rows, :], w_ref[...]) for rows in pieces]
        for rows, h2 in zip(pieces, h2s):
            r = lax.rsqrt(jnp.mean(h2 * h2, axis=-1, keepdims=True) + EPS)
            xhat = h2 * r
            err = xhat * g_ref[...] - t_ref[rows, :]
            part += 0.5 * jnp.sum(jnp.sum(err * err, axis=-1, keepdims=True), axis=0, keepdims=True) / D
            dy = err / D
            dxh = dy * g_ref[...]
            dh2 = r * (dxh - xhat * jnp.mean(dxh * xhat, axis=-1, keepdims=True))
            dh_ref[rows, :] = dh2
            dhb = dh2.astype(dhb_ref.dtype)
            dhb_ref[rows, :] = dhb
            dgain += jnp.sum(dy * xhat, axis=0, keepdims=True)
            dz = _nt(dhb, w_ref[...])
            gv, upv = gate_ref[rows, :].astype(F32), up_ref[rows, :].astype(F32)
            s = _sigmoid(gv)
            dgate_ref[rows, :] = (dz * upv * (s * (1.0 + gv * (1.0 - s)))).astype(dgate_ref.dtype)
            dup_ref[rows, :] = (dz * (gv * s)).astype(dup_ref.dtype)
        dg_ref[...] = jnp.broadcast_to(dgain, dg_ref.shape)
        l_ref[...] = jnp.broadcast_to(part, l_ref.shape)

    low = lambda n: jax.ShapeDtypeStruct((T, n), MXU_DTYPE)
    part = jax.ShapeDtypeStruct((8 * (T // tm), D), F32)
    return pl.pallas_call(
        body, name="ffn_tail", grid=(T // tm,),
        in_specs=[_row_spec(tm, F), _fixed_spec(w_down), _row_spec(tm, D), _row_spec(tm, D), _fixed_spec(gain),
                  _row_spec(tm, F), _row_spec(tm, F)],
        out_specs=[_row_spec(tm, D), _row_spec(tm, D), _row_spec(tm, F), _row_spec(tm, F), _partials_spec(D),
                   _partials_spec(D)],
        out_shape=[jax.ShapeDtypeStruct((T, D), F32), low(D), low(F), low(F), part, part],
        compiler_params=_params(("parallel",)),
    )(z, w_down, h1, target, gain, gate, up)


def _ffn_in_bwd(dgate, dup, w_gate_t, w_up_t, h1, gain, dres, *, tm, comm=None):
    (T, F), D = dgate.shape, h1.shape[1]

    def body(dg_ref, du_ref, wg_ref, wu_ref, h_ref, g_ref, r_ref, dh_ref, dhb_ref, dgain_ref):
        d_u2 = _nn(dg_ref[...], wg_ref[...]) + _nn(du_ref[...], wu_ref[...])
        dx, dgain = _rmsnorm_bwd_vals(d_u2, h_ref[...], g_ref[...])
        dh = r_ref[...] + dx
        dh_ref[...] = dh
        dhb_ref[...] = dh.astype(dhb_ref.dtype)
        dgain_ref[...] = jnp.broadcast_to(dgain, dgain_ref.shape)

    return _call("d_ffn_in", body, grid=(T // tm,),
                 in_specs=[_row_spec(tm, F), _row_spec(tm, F), _fixed_spec(w_gate_t), _fixed_spec(w_up_t),
                           _row_spec(tm, D), _fixed_spec(gain), _row_spec(tm, D)],
                 out_specs=[_row_spec(tm, D), _row_spec(tm, D), _partials_spec(D)],
                 out_shape=[jax.ShapeDtypeStruct((T, D), F32), jax.ShapeDtypeStruct((T, D), MXU_DTYPE),
                            jax.ShapeDtypeStruct((8 * (T // tm), D), F32)],
                 args=[dgate, dup, w_gate_t, w_up_t, h1, gain, dres], semantics=("parallel",), comm=comm)


def _in_proj_bwd(pieces, w_in_t, x, gain, dres, *, tm, comm=None):
    T, D = x.shape
    n = len(pieces)

    def body(*refs):
        dps, (w_ref, x_ref, g_ref, r_ref, dx_ref, dgain_ref) = refs[:n], refs[n:]
        d_u = None
        for dp_ref, (dp, first) in zip(dps, pieces):
            term = _nn(dp_ref[...], w_ref[first:first + dp.shape[1], :])
            d_u = term if d_u is None else d_u + term
        dx, dgain = _rmsnorm_bwd_vals(d_u, x_ref[...], g_ref[...])
        dx_ref[...] = r_ref[...] + dx
        dgain_ref[...] = jnp.broadcast_to(dgain, dgain_ref.shape)

    return _call("d_u", body, grid=(T // tm,),
                 in_specs=[_row_spec(tm, dp.shape[1]) for dp, _ in pieces]
                 + [_fixed_spec(w_in_t), _row_spec(tm, D), _fixed_spec(gain), _row_spec(tm, D)],
                 out_specs=[_row_spec(tm, D), _partials_spec(D)],
                 out_shape=[jax.ShapeDtypeStruct((T, D), F32), jax.ShapeDtypeStruct((8 * (T // tm), D), F32)],
                 args=[dp for dp, _ in pieces] + [w_in_t, x, gain, dres], semantics=("parallel",), comm=comm)


def _merge_fwd(y_a, y_b, w_a, w_b, gates, *, tm, comm=None):
    T, D = y_a.shape

    def body(ya_ref, yb_ref, wa_ref, wb_ref, ga_ref, gb_ref, pa_ref, pb_ref, m_ref):
        pa, pb = _nn(ya_ref[...], wa_ref[...]), _nn(yb_ref[...], wb_ref[...])
        pa_ref[...], pb_ref[...] = pa.astype(pa_ref.dtype), pb.astype(pb_ref.dtype)
        m_ref[...] = (_sigmoid(ga_ref[...]) * pa + _sigmoid(gb_ref[...]) * pb).astype(m_ref.dtype)

    rows = pl.BlockSpec((tm, D), lambda i: (i, 0))
    whole = pl.BlockSpec((D, D), lambda i: (0, 0), pipeline_mode=pl.Buffered(1))
    return _call("branch_merge", body, grid=(T // tm,),
                 in_specs=[rows, rows, whole, whole, rows, pl.BlockSpec((tm, D), lambda i: (i, 1))],
                 out_specs=[rows] * 3,
                 out_shape=[jax.ShapeDtypeStruct((T, D), SAVED_DTYPE)] * 2 + [jax.ShapeDtypeStruct((T, D), MXU_DTYPE)],
                 args=[y_a, y_b, w_a, w_b, gates, gates], semantics=("parallel",), comm=comm)


def _merge_bwd(dh, w_out, w_a, w_b, p_a, p_b, gates, *, tm, d_in_width, first):
    T, D = dh.shape

    def body(dh_ref, wo_ref, wa_ref, wb_ref, pa_ref, pb_ref, ga_ref, gb_ref, dpa_ref, dpb_ref, dya_ref, dyb_ref,
             din_ref):
        dm = _nt(dh_ref[...], wo_ref[...])
        sa, sb = _sigmoid(ga_ref[...]), _sigmoid(gb_ref[...])
        dpa, dpb = (dm * sa).astype(dpa_ref.dtype), (dm * sb).astype(dpb_ref.dtype)
        dpa_ref[...], dpb_ref[...] = dpa, dpb
        din_ref[:, :D] = (dm * pa_ref[...].astype(F32) * sa * (1.0 - sa)).astype(din_ref.dtype)
        din_ref[:, D:] = (dm * pb_ref[...].astype(F32) * sb * (1.0 - sb)).astype(din_ref.dtype)
        dya_ref[...] = _nt(dpa, wa_ref[...]).astype(dya_ref.dtype)
        dyb_ref[...] = _nt(dpb, wb_ref[...]).astype(dyb_ref.dtype)

    rows = pl.BlockSpec((tm, D), lambda i: (i, 0))
    whole = pl.BlockSpec((D, D), lambda i: (0, 0), pipeline_mode=pl.Buffered(1))
    low = jax.ShapeDtypeStruct((T, D), MXU_DTYPE)
    return pl.pallas_call(
        body, name="d_branch_merge", grid=(T // tm,),
        in_specs=[rows, whole, whole, whole, rows, rows, rows, pl.BlockSpec((tm, D), lambda i: (i, 1))],
        out_specs=[rows] * 4 + [pl.BlockSpec((pl.Element(tm), pl.Element(2 * D)), lambda i: (
            pl.multiple_of(i * tm, ROW_ALIGN), first))],
        out_shape=[low] * 3 + [jax.ShapeDtypeStruct((T, D), F32), jax.ShapeDtypeStruct((T, d_in_width), MXU_DTYPE)],
        compiler_params=_params(("parallel",)),
    )(dh, w_out, w_a, w_b, p_a, p_b, gates, gates)


def _colsum_partials(p):
    return jnp.sum(p.reshape(-1, 8, p.shape[-1])[:, 0, :], axis=0, keepdims=True)


def _rmsnorm_bwd_vals(dy, xin, g):
    rstd = lax.rsqrt(jnp.mean(xin * xin, axis=-1, keepdims=True) + EPS)
    xhat = xin * rstd
    dg = jnp.sum(dy * xhat, axis=0, keepdims=True)
    dxh = dy * g
    dx = rstd * (dxh - xhat * jnp.mean(dxh * xhat, axis=-1, keepdims=True))
    return dx, dg


ATTN_SCALE = 1.0 / math.sqrt(HEAD_DIM)
GROUP_LANES = GROUP * ATTN_BLOCK
PAIR = 2 * HEAD_DIM


def _attn_mask():
    kj = lax.broadcasted_iota(jnp.int32, (ATTN_BLOCK, GROUP_LANES), 0)
    qi = lax.broadcasted_iota(jnp.int32, (ATTN_BLOCK, GROUP_LANES), 1) & (ATTN_BLOCK - 1)
    return kj <= qi


def _heads_transposed(ref, g, scale=None):
    parts = []
    for a in range(GROUP // 2):
        lo = (g * GROUP // 2 + a) * PAIR
        pair = ref[:, lo:lo + PAIR].astype(F32)
        pair = (pair if scale is None else pair * scale).T
        parts += [pair[:HEAD_DIM], pair[HEAD_DIM:]]
    return jnp.concatenate(parts, axis=1).astype(MXU_DTYPE)


def _heads_back(ref, g, vt):
    for a in range(GROUP // 2):
        lo = (g * GROUP // 2 + a) * PAIR
        pair = jnp.concatenate([vt[:, (2 * a) * ATTN_BLOCK:(2 * a + 1) * ATTN_BLOCK],
                                vt[:, (2 * a + 1) * ATTN_BLOCK:(2 * a + 2) * ATTN_BLOCK]], axis=0)
        ref[:, lo:lo + PAIR] = pair.T.astype(ref.dtype)


def _kv_parts(kv_ref, g):
    ks = slice(g * HEAD_DIM, (g + 1) * HEAD_DIM)
    vs = slice(KV_WIDTH + g * HEAD_DIM, KV_WIDTH + (g + 1) * HEAD_DIM)
    return kv_ref[:, ks].astype(MXU_DTYPE), kv_ref[:, vs].astype(MXU_DTYPE)


def _sink_rows(sinks):
    return jnp.repeat(sinks.reshape(KV_HEADS, GROUP), ATTN_BLOCK, axis=1)


def _attn_fwd(pq, pkv, sinks, comm=None):
    T = pq.shape[0]
    nb = T // ATTN_BLOCK

    def body(q_ref, kvc_ref, kvp_ref, s_ref, y_ref, lse_ref):
        mask_c = _attn_mask()
        has_prev = pl.program_id(0) > 0
        for g in range(KV_HEADS):
            (kc, vc), (kp, vp) = _kv_parts(kvc_ref, g), _kv_parts(kvp_ref, g)
            qt = _heads_transposed(q_ref, g, ATTN_SCALE)
            s = jnp.where(mask_c, _nn(kc, qt), jnp.where(has_prev, _nn(kp, qt), NEG_INF))
            sink = s_ref[g:g + 1, :]
            m = jnp.maximum(jnp.max(s, axis=0, keepdims=True), sink)
            p = jnp.exp(s - m)
            den = jnp.sum(p, axis=0, keepdims=True) + jnp.exp(sink - m)
            pc = jnp.where(mask_c, p, 0.0)
            _heads_back(y_ref, g, (_tn(vc, pc) + _tn(vp, p - pc)) / den)
            lse = m + jnp.log(den)
            for i in range(GROUP):
                lse_ref[g * GROUP + i:g * GROUP + i + 1, :] = lse[:, i * ATTN_BLOCK:(i + 1) * ATTN_BLOCK]

    return _call(
        "attn_fwd", body, grid=(nb,),
        in_specs=[pl.BlockSpec((ATTN_BLOCK, D_MODEL), lambda n: (n, 0)),
                  pl.BlockSpec((ATTN_BLOCK, 2 * KV_WIDTH), lambda n: (n, 0)),
                  pl.BlockSpec((ATTN_BLOCK, 2 * KV_WIDTH), lambda n: (jnp.maximum(n - 1, 0), 0)),
                  pl.BlockSpec((KV_HEADS, GROUP_LANES), lambda n: (0, 0))],
        out_specs=[pl.BlockSpec((ATTN_BLOCK, D_MODEL), lambda n: (n, 0)),
                   pl.BlockSpec((Q_HEADS, ATTN_BLOCK), lambda n: (0, n))],
        out_shape=[jax.ShapeDtypeStruct((T, D_MODEL), MXU_DTYPE), jax.ShapeDtypeStruct((Q_HEADS, T), F32)],
        args=[pq, pkv, pkv, _sink_rows(sinks)], semantics=("parallel",), comm=comm)


def _attn_bwd(pq, pkv, sinks, lse, dy, d_in, comm=None):
    T = pq.shape[0]
    nb = T // ATTN_BLOCK
    cur = lambda n: (jnp.minimum(n, nb - 1), 0)
    done = D_MODEL + 2 * KV_WIDTH

    def body(q_ref, kvc_ref, kvp_ref, s_ref, lse_ref, dy_ref, _, out_ref, ds_ref, carry, top, bot, dq_ref):
        n = pl.program_id(0)

        @pl.when(n == 0)
        def _():
            carry[...] = jnp.zeros_like(carry)
            dq_ref[...] = jnp.zeros_like(dq_ref)
            ds_ref[...] = jnp.zeros_like(ds_ref)

        out_ref[:, :D_MODEL] = dq_ref[...]

        @pl.when(n < nb)
        def _():
            mask_c = _attn_mask()
            valid = jnp.logical_or(mask_c, n > 0)
            for g in range(KV_HEADS):
                ks = slice(g * HEAD_DIM, (g + 1) * HEAD_DIM)
                vs = slice(KV_WIDTH + g * HEAD_DIM, KV_WIDTH + (g + 1) * HEAD_DIM)
                (kc, vc), (kp, vp) = _kv_parts(kvc_ref, g), _kv_parts(kvp_ref, g)
                qt = _heads_transposed(q_ref, g, ATTN_SCALE)
                dot = _heads_transposed(dy_ref, g)
                lse = jnp.concatenate([lse_ref[g * GROUP + i:g * GROUP + i + 1, :] for i in range(GROUP)], axis=1)
                p = jnp.where(valid, jnp.exp(jnp.where(mask_c, _nn(kc, qt), _nn(kp, qt)) - lse), 0.0)
                dp = jnp.where(mask_c, _nn(vc, dot), _nn(vp, dot))
                delta = jnp.sum(p * dp, axis=0, keepdims=True)
                ds = p * (dp - delta)
                ds_c, p_c = jnp.where(mask_c, ds, 0.0), jnp.where(mask_c, p, 0.0)
                ds_p, p_p = ds - ds_c, p - p_c
                _heads_back(dq_ref, g, (_tn(kc, ds_c) + _tn(kp, ds_p)) * ATTN_SCALE)
                bot[:, ks], bot[:, vs] = _nt(ds_c, qt), _nt(p_c, dot)
                top[:, ks], top[:, vs] = _nt(ds_p, qt), _nt(p_p, dot)
                ds_ref[g:g + 1, :] -= jnp.exp(s_ref[g:g + 1, :] - lse) * delta
            out_ref[:, D_MODEL:] = (carry[...] + top[...]).astype(out_ref.dtype)
            carry[...] = bot[...]

        @pl.when(n == nb)
        def _():
            out_ref[:, D_MODEL:] = carry[...].astype(out_ref.dtype)

    return _call(
        "attn_bwd", body, grid=(nb + 1,),
        in_specs=[pl.BlockSpec((ATTN_BLOCK, D_MODEL), cur),
                  pl.BlockSpec((ATTN_BLOCK, 2 * KV_WIDTH), cur),
                  pl.BlockSpec((ATTN_BLOCK, 2 * KV_WIDTH), lambda n: (jnp.maximum(jnp.minimum(n, nb - 1) - 1, 0), 0)),
                  pl.BlockSpec((KV_HEADS, GROUP_LANES), lambda n: (0, 0)),
                  pl.BlockSpec((Q_HEADS, ATTN_BLOCK), lambda n: (0, jnp.minimum(n, nb - 1))),
                  pl.BlockSpec((ATTN_BLOCK, D_MODEL), cur), HBM_SPEC],
        out_specs=[pl.BlockSpec((ATTN_BLOCK, done), lambda n: (jnp.maximum(n - 1, 0), 0)),
                   pl.BlockSpec((KV_HEADS, GROUP_LANES), lambda n: (0, 0))],
        out_shape=[jax.ShapeDtypeStruct(d_in.shape, d_in.dtype), jax.ShapeDtypeStruct((KV_HEADS, GROUP_LANES), F32)],
        scratch=[pltpu.VMEM((ATTN_BLOCK, 2 * KV_WIDTH), F32)] * 3 + [pltpu.VMEM((ATTN_BLOCK, D_MODEL), MXU_DTYPE)],
        args=[pq, pkv, pkv, _sink_rows(sinks), lse, dy, d_in], semantics=("arbitrary",), comm=comm, aliases={6: 0})


def _lower_bound(l):
    m = jnp.maximum(l[0:1], l[1:2])
    e0, e1 = jnp.exp(l[0:1] - m), jnp.exp(l[1:2] - m)
    return e0 / (e0 + e1)


def _tri(lower):
    r = lax.broadcasted_iota(jnp.int32, (CHUNK, CHUNK), 0)
    c = lax.broadcasted_iota(jnp.int32, (CHUNK, CHUNK), 1)
    return (r >= c) if lower else (c >= r)


def _chunk_sum(mask, v):
    ones = mask.astype(BF16)
    hi = v.astype(BF16)
    rest = v - hi.astype(F32)
    mid = rest.astype(BF16)
    lo = (rest - mid.astype(F32)).astype(BF16)
    part = lambda t: lax.dot_general(ones, t, (((1,), (0,)), ((), ())), preferred_element_type=F32)
    return part(hi) + part(mid) + part(lo)


def _hgrn_chunk_inputs(hq, hf, lb, causal):
    half_t = 0.5 * jnp.tanh(0.5 * hf)
    sg, sgn = 0.5 + half_t, 0.5 - half_t
    f = lb + (1.0 - lb) * sg
    kk = (1.0 - lb) * sgn
    sq = _sigmoid(hq)
    q = hq * sq
    b = _chunk_sum(causal, jnp.log(f))
    bm, bl = b[CHUNK // 2 - 1:CHUNK // 2, :], b[CHUNK - 1:CHUNK, :]
    e_qm, e_km = jnp.exp(b - bm), jnp.exp(bm - b)
    e_qs, e_kl = e_qm * jnp.exp(bm), e_km * jnp.exp(bl - bm)
    return dict(sg=sg, sgn=sgn, f=f, kk=kk, sq=sq, q=q, e_qm=e_qm, e_km=e_km, e_qs=e_qs, e_kl=e_kl,
                qm=q * e_qm, km=kk * e_km, qs=q * e_qs, kl=kk * e_kl, el=jnp.exp(bl))


def _hgrn_fwd(ph, lb_logits, norm_g, comm=None):
    T = ph.shape[0]
    nblk, cpb = T // HGRN_TOKENS, HGRN_TOKENS // CHUNK
    col = lambda c: pl.BlockSpec((HGRN_TOKENS, D_MODEL), functools.partial(lambda i, c: (i, c), c=c))

    def body(hq_ref, hf_ref, hi_ref, hg_ref, l_ref, ng_ref, y_ref, o_ref, st_ref, s_ref):
        @pl.when(pl.program_id(0) == 0)
        def _():
            s_ref[...] = jnp.zeros_like(s_ref)

        lb = _lower_bound(l_ref[...])
        causal = _tri(True)
        for c in range(cpb):
            rows = slice(c * CHUNK, (c + 1) * CHUNK)
            t = _hgrn_chunk_inputs(hq_ref[rows, :], hf_ref[rows, :], lb, causal)
            qm, km, qs, kl = (t[n].astype(MXU_DTYPE) for n in ("qm", "km", "qs", "kl"))
            v = hi_ref[rows, :].astype(MXU_DTYPE)
            heads = [slice(h * HGRN_K, (h + 1) * HGRN_K) for h in range(HGRN_HEADS)]
            a_all = [jnp.where(causal, _nt(qm[:, ls], km[:, ls]), 0.0).astype(MXU_DTYPE) for ls in heads]
            for h, ls in enumerate(heads):
                st = s_ref[h]
                st_ref[c, ls, :] = st
                o_ref[rows, ls] = _nn(a_all[h], v[:, ls]) + _nt(qs[:, ls], st)
                s_ref[h] = t["el"][:, ls] * st + _tn(v[:, ls], kl[:, ls])
        for h in range(HGRN_HEADS):
            ls = slice(h * HGRN_K, (h + 1) * HGRN_K)
            o = o_ref[:, ls]
            r = lax.rsqrt(jnp.mean(o * o, axis=-1, keepdims=True) + EPS)
            y_ref[:, ls] = (o * r * ng_ref[:, ls] * _sigmoid(hg_ref[:, ls])).astype(y_ref.dtype)

    return _call(
        "hgrn_fwd", body, grid=(nblk,),
        in_specs=[col(0), col(1), col(2), col(3),
                  pl.BlockSpec((2, D_MODEL), lambda i: (0, 0)), pl.BlockSpec((1, D_MODEL), lambda i: (0, 0))],
        out_specs=[pl.BlockSpec((HGRN_TOKENS, D_MODEL), lambda i: (i, 0)),
                   pl.BlockSpec((HGRN_TOKENS, D_MODEL), lambda i: (i, 0)),
                   pl.BlockSpec((cpb, D_MODEL, HGRN_K), lambda i: (i, 0, 0))],
        out_shape=[jax.ShapeDtypeStruct((T, D_MODEL), MXU_DTYPE), jax.ShapeDtypeStruct((T, D_MODEL), F32),
                   jax.ShapeDtypeStruct((T // CHUNK, D_MODEL, HGRN_K), F32)],
        scratch=[pltpu.VMEM((HGRN_HEADS, HGRN_K, HGRN_K), F32)],
        args=[ph, ph, ph, ph, lb_logits, norm_g], semantics=("arbitrary",), comm=comm)


def _hgrn_bwd(ph, o_raw, states, dy, lb_logits, norm_g, d_in, first, comm=None):
    T = ph.shape[0]
    nblk, cpb = T // HGRN_TOKENS, HGRN_TOKENS // CHUNK
    rev = lambda i: nblk - 1 - i
    col = lambda c: pl.BlockSpec((HGRN_TOKENS, D_MODEL), functools.partial(lambda i, c: (rev(i), c), c=c))
    tok = pl.BlockSpec((HGRN_TOKENS, D_MODEL), lambda i: (rev(i), 0))

    def body(hq_ref, hf_ref, hi_ref, hg_ref, o_ref, st_ref, dy_ref, l_ref, ng_ref, _,
             dph_ref, dng_ref, dl_ref, dst_ref, dlb_ref, do_s, dqm_s, dkm_s, dqs_s, dkl_s, dv_s, del_s):
        i = pl.program_id(0)

        @pl.when(i == 0)
        def _():
            dst_ref[...] = jnp.zeros_like(dst_ref)
            dlb_ref[...] = jnp.zeros_like(dlb_ref)
            dng_ref[...] = jnp.zeros_like(dng_ref)

        lb = _lower_bound(l_ref[...])
        causal, anti = _tri(True), _tri(False)
        row = lax.broadcasted_iota(jnp.int32, (CHUNK, D_MODEL), 0)
        for c in reversed(range(cpb)):
            rows = slice(c * CHUNK, (c + 1) * CHUNK)
            hq = hq_ref[rows, :]
            t = _hgrn_chunk_inputs(hq, hf_ref[rows, :], lb, causal)
            sgg = _sigmoid(hg_ref[rows, :])
            dyv = dy_ref[rows, :]
            for h in range(HGRN_HEADS):
                ls = slice(h * HGRN_K, (h + 1) * HGRN_K)
                o = o_ref[rows, ls]
                r = lax.rsqrt(jnp.mean(o * o, axis=-1, keepdims=True) + EPS)
                nrm = o * r
                g_h = sgg[:, ls]
                dph_ref[rows, 3 * D_MODEL + h * HGRN_K:3 * D_MODEL + (h + 1) * HGRN_K] = (
                    dyv[:, ls] * nrm * ng_ref[:, ls] * g_h * (1.0 - g_h)).astype(dph_ref.dtype)
                dyg = dyv[:, ls] * g_h
                dng_ref[:, ls] += jnp.sum(dyg * nrm, axis=0, keepdims=True)
                dn = dyg * ng_ref[:, ls]
                do_s[:, ls] = r * (dn - nrm * jnp.mean(dn * nrm, axis=-1, keepdims=True))
            qm, km, qs, kl = (t[n].astype(MXU_DTYPE) for n in ("qm", "km", "qs", "kl"))
            v = hi_ref[rows, :].astype(MXU_DTYPE)
            do = do_s[...].astype(MXU_DTYPE)
            heads = [slice(h * HGRN_K, (h + 1) * HGRN_K) for h in range(HGRN_HEADS)]
            a_all = [jnp.where(causal, _nt(qm[:, ls], km[:, ls]), 0.0).astype(MXU_DTYPE) for ls in heads]
            da_all = [jnp.where(causal, _nt(do[:, ls], v[:, ls]), 0.0).astype(MXU_DTYPE) for ls in heads]
            for h, ls in enumerate(heads):
                st = st_ref[c, ls, :]
                dst = dst_ref[h]
                a, da = a_all[h], da_all[h]
                dv_s[:, ls] = _tn(a, do[:, ls]) + _nt(kl[:, ls], dst)
                dkl_s[:, ls] = _nn(v[:, ls], dst)
                dqs_s[:, ls] = _nn(do[:, ls], st)
                del_s[:, ls] = jnp.sum(dst * st, axis=0, keepdims=True)
                dst_ref[h] = _tn(do[:, ls], qs[:, ls]) + t["el"][:, ls] * dst
                dqm_s[:, ls] = _nn(da, km[:, ls])
                dkm_s[:, ls] = _tn(da, qm[:, ls])
            dqm, dkm, dqs, dkl = dqm_s[...], dkm_s[...], dqs_s[...], dkl_s[...]
            dq = dqm * t["e_qm"] + dqs * t["e_qs"]
            dk = dkm * t["e_km"] + dkl * t["e_kl"]
            t_qm, t_km, t_kl = dqm * t["qm"], dkm * t["km"], dkl * t["kl"]
            db = t_qm - t_km + dqs * t["qs"] - t_kl
            db_mid = jnp.sum(t_km - t_qm, axis=0, keepdims=True)
            db_last = jnp.sum(t_kl, axis=0, keepdims=True) + del_s[...] * t["el"]
            db = db + jnp.where(row == CHUNK // 2 - 1, db_mid, 0.0) + jnp.where(row == CHUNK - 1, db_last, 0.0)
            dlogf = _chunk_sum(anti, db)
            sq, sg, sgn, f = t["sq"], t["sg"], t["sgn"], t["f"]
            dph_ref[rows, 0:D_MODEL] = (dq * (sq * (1.0 + hq * (1.0 - sq)))).astype(dph_ref.dtype)
            dph_ref[rows, D_MODEL:2 * D_MODEL] = (
                dlogf * (1.0 - lb) * sg * (1.0 - sg) / f - dk * (1.0 - lb) * sgn * (1.0 - sgn)).astype(dph_ref.dtype)
            dph_ref[rows, 2 * D_MODEL:3 * D_MODEL] = dv_s[...].astype(dph_ref.dtype)
            dlb_ref[...] += jnp.sum(dlogf * (1.0 - sg) / f - dk * sgn, axis=0, keepdims=True)

        @pl.when(i == nblk - 1)
        def _():
            dl0 = dlb_ref[...] * lb * (1.0 - lb)
            dl_ref[0:1, :] = dl0
            dl_ref[1:2, :] = -dl0

    wide = pltpu.VMEM((CHUNK, D_MODEL), F32)
    return _call(
        "hgrn_bwd", body, grid=(nblk,),
        in_specs=[col(0), col(1), col(2), col(3), tok,
                  pl.BlockSpec((cpb, D_MODEL, HGRN_K), lambda i: (rev(i), 0, 0)), tok,
                  pl.BlockSpec((2, D_MODEL), lambda i: (0, 0)), pl.BlockSpec((1, D_MODEL), lambda i: (0, 0)), HBM_SPEC],
        out_specs=[pl.BlockSpec((pl.Element(HGRN_TOKENS), pl.Element(4 * D_MODEL)), lambda i: (
                       pl.multiple_of(rev(i) * HGRN_TOKENS, ROW_ALIGN), first)),
                   pl.BlockSpec((1, D_MODEL), lambda i: (0, 0)), pl.BlockSpec((2, D_MODEL), lambda i: (0, 0))],
        out_shape=[jax.ShapeDtypeStruct(d_in.shape, d_in.dtype), jax.ShapeDtypeStruct((1, D_MODEL), F32),
                   jax.ShapeDtypeStruct((2, D_MODEL), F32)],
        scratch=[pltpu.VMEM((HGRN_HEADS, HGRN_K, HGRN_K), F32), pltpu.VMEM((1, D_MODEL), F32),
                 wide, wide, wide, wide, wide, wide, pltpu.VMEM((1, D_MODEL), F32)],
        args=[ph, ph, ph, ph, o_raw, states, dy, lb_logits, norm_g, d_in], semantics=("arbitrary",), comm=comm,
        aliases={9: 0})


def _local_step(x, target, vec, net):
    T, D = x.shape
    norm_mix_g, b_in, sinks, lb_logits = vec["norm_mix_g"], vec["b_in"], vec["attn_sinks"], vec["hgrn_lb_logits"]
    hgrn_norm_g, norm_ffn_g, norm_final_g = vec["hgrn_norm_g"], vec["norm_ffn_g"], vec["norm_final_g"]
    w_in = net.full("w_in")
    o_q, o_kv, o_h, o_g = (sum(IN_SPLITS[:i]) for i in range(4))
    TM = 512

    first = ("w_branch_attn", "w_branch_hgrn", "w_ffn_down")
    (u, pq, pkv, ph, pg), got = _in_proj(x, norm_mix_g, w_in, b_in, tm=256, comm=net.fetch(first))
    net.fetched(first, got)
    second = ("w_ffn_gate",)
    (y_attn, lse), got = _attn_fwd(pq, pkv, sinks, comm=_join(net.relay(first), net.fetch(second)))
    net.gathered(first, got[:len(first)])
    net.fetched(second, got[len(first):])
    third = ("w_ffn_up", "w_out")
    (y_hgrn, o_raw, states), got = _hgrn_fwd(ph, lb_logits, hgrn_norm_g,
                                             comm=_join(net.relay(second), net.fetch(third)))
    net.gathered(second, got[:len(second)])
    net.fetched(third, got[len(second):])
    w_ba, w_bh = net.full("w_branch_attn"), net.full("w_branch_hgrn")
    (ya, yb, merged), got = _merge_fwd(y_attn, y_hgrn, w_ba, w_bh, pg, tm=TM, comm=net.relay(third))
    net.gathered(third, got)
    w_gate, w_up, w_out = net.full("w_ffn_gate"), net.full("w_ffn_up"), net.full("w_out")

    (h1, u2, gpre, up, z), _ = _ffn_fwd(merged, w_out, x, norm_ffn_g, w_gate, w_up, tm=256)
    w_down = net.full("w_ffn_down")

    dh2, dh2b, dgp, dup, dgf_p, loss_p = _ffn_tail(z, w_down, h1, target, norm_final_g, gpre, up, tm=512)
    loss = jnp.sum(loss_p.reshape(-1, 8, D)[:, 0, 0])
    d_norm_final = _colsum_partials(dgf_p)
    d_w_down = _weight_grad("dw_down", z, dh2b, tm=DW_ROWS, tk=T)

    names, swap = ("w_ffn_down",), ()
    (dh1, dh1b, dg2_p), got = _ffn_in_bwd(dgp, dup, w_gate, w_up, h1, norm_ffn_g, dh2, tm=256,
                                          comm=net.exchange(dict(w_ffn_down=[d_w_down]), swap))
    net.received(names, swap, got)
    d_norm_ffn = _colsum_partials(dg2_p)
    d_w_gate = _weight_grad("dw_gate", dgp, u2, tm=DW_ROWS, tk=T)
    d_w_up = _weight_grad("dw_up", dup, u2, tm=DW_ROWS, tk=T)

    rows_in = sum(IN_SPLITS)
    dya, dyb, dy_attn, dy_hgrn, d_in = _merge_bwd(dh1b, w_out, w_ba, w_bh, ya, yb, pg, tm=TM, d_in_width=rows_in,
                                                  first=o_g)
    d_w_out = _weight_grad("dw_out", merged, dh1b, tm=1024, tk=1024)
    d_w_ba = _weight_grad("dw_branch_a", y_attn, dya, tm=1024, tk=1024)
    d_w_bh = _weight_grad("dw_branch_b", y_hgrn, dyb, tm=1024, tk=1024)

    names, swap = ("w_ffn_gate",), ("w_ffn_down",)
    (d_in, dsink), got = _attn_bwd(pq, pkv, sinks, lse, dy_attn, d_in,
                                   comm=net.exchange(dict(w_ffn_gate=[d_w_gate]), swap))
    net.received(names, swap, got)
    names, swap = ("w_ffn_up", "w_out"), ("w_ffn_gate",)
    (d_in, d_hgrn_norm, d_lb_logits), got = _hgrn_bwd(
        ph, o_raw, states, dy_hgrn, lb_logits, hgrn_norm_g, d_in, o_h,
        comm=net.exchange(dict(w_ffn_up=[d_w_up], w_out=[d_w_out]), swap))
    net.received(names, swap, got)

    names, swap = ("w_branch_attn", "w_branch_hgrn"), ("w_ffn_up", "w_out")
    (*d_w_in, d_b_in), got = _weight_grad(
        "dw_in", d_in, u, tm=DW_ROWS, tk=T, a_colsum=True, carrying=True,
        comm=net.exchange(dict(w_branch_attn=[d_w_ba], w_branch_hgrn=[d_w_bh]), swap))
    net.received(names, swap, got)
    d_w_in = [tuple(d_w_in)]

    first_level = net.presum_begin("w_in", d_w_in)
    halves = net.presum_end("w_in", [] if first_level is None else _copies_alone("presum_swap_w_in", first_level))
    names, swap = ("w_in",), ("w_branch_attn", "w_branch_hgrn")
    (dx, dg1_p), got = _in_proj_bwd([(d_in, 0)], w_in, x, norm_mix_g, dh1, tm=256,
                                    comm=_join(halves, net.swap(swap)))
    net.last = (names, swap, got)
    d_norm_mix = _colsum_partials(dg1_p)
    vecs = dict(norm_mix_g=d_norm_mix, b_in=d_b_in, attn_sinks=jnp.sum(dsink.reshape(Q_HEADS, ATTN_BLOCK), axis=1).reshape(1, Q_HEADS),
                hgrn_lb_logits=d_lb_logits,
                hgrn_norm_g=d_hgrn_norm, norm_ffn_g=d_norm_ffn, norm_final_g=d_norm_final)
    return loss, dx, vecs


def _place():
    return lax.axis_index("x"), lax.axis_index("y"), lax.axis_index("c")


def _other_chips(x, y):
    return [(1 - x, y), (x, 1 - y), (1 - x, 1 - y)]


def _y_first(copies):
    return [copies[3 * (i // 3) + (1, 0, 2)[i % 3]] for i in range(len(copies))]


def _gather_copies(shards):
    n = len(shards)

    def build(ins, outs, send_sems, recv_sems, local_sems):
        x, y, c = _place()
        mine = 2 * x + y
        local = [pltpu.make_async_copy(ins[w], outs[w].at[mine], local_sems.at[w]) for w in range(n)]
        sends, recvs = [], []
        for w in range(n):
            for k, (px, py) in enumerate(_other_chips(x, y)):
                sem = 3 * w + k
                sends.append(pltpu.make_async_remote_copy(
                    src_ref=ins[w], dst_ref=outs[w].at[mine], send_sem=send_sems.at[sem], recv_sem=recv_sems.at[sem],
                    device_id=(px, py, c), device_id_type=MESH_ID))
                recvs.append(pltpu.make_async_remote_copy(
                    src_ref=ins[w], dst_ref=outs[w].at[2 * px + py], send_sem=send_sems.at[sem],
                    recv_sem=recv_sems.at[sem], device_id=(px, py, c), device_id_type=MESH_ID))
        return sends, recvs, local, _y_first(sends)

    return _Carried(shards, [jax.ShapeDtypeStruct((N_CHIPS,) + s.shape, s.dtype) for s in shards], 3 * n, n, build)


def _fetch_copies(shards):
    n = len(shards)

    def build(ins, outs, send_sems, recv_sems, local_sems):
        x, y, c = _place()
        sends = []
        for w in range(n):
            half = shards[w].shape[0] // 2
            for k, (px, py) in enumerate(_other_chips(x, y)):
                sem = 3 * w + k
                sends.append(pltpu.make_async_remote_copy(
                    src_ref=ins[w].at[pl.ds(c * half, half)], dst_ref=outs[w].at[k], send_sem=send_sems.at[sem],
                    recv_sem=recv_sems.at[sem], device_id=(px, py, c), device_id_type=MESH_ID))
        return sends, sends, [], _y_first(sends)

    return _Carried(shards, [jax.ShapeDtypeStruct((3, s.shape[0] // 2, s.shape[1]), s.dtype) for s in shards],
                    3 * n, 0, build)


def _relay_copies(shards, halves):
    n = len(shards)

    def build(ins, outs, send_sems, recv_sems, local_sems):
        x, y, c = _place()
        local, sends, recvs = [], [], []
        for w in range(n):
            half = shards[w].shape[0] // 2
            local.append(pltpu.make_async_copy(ins[w], outs[w].at[2 * x + y], local_sems.at[4 * w]))
            for k, (px, py) in enumerate(_other_chips(x, y)):
                block, sem = 2 * px + py, 3 * w + k
                mine, theirs = pl.ds(c * half, half), pl.ds((1 - c) * half, half)
                local.append(pltpu.make_async_copy(ins[n + w].at[k], outs[w].at[block, mine],
                                                   local_sems.at[4 * w + 1 + k]))
                sends.append(pltpu.make_async_remote_copy(
                    src_ref=ins[n + w].at[k], dst_ref=outs[w].at[block, mine], send_sem=send_sems.at[sem],
                    recv_sem=recv_sems.at[sem], device_id=(x, y, 1 - c), device_id_type=MESH_ID))
                recvs.append(pltpu.make_async_remote_copy(
                    src_ref=ins[n + w].at[k], dst_ref=outs[w].at[block, theirs], send_sem=send_sems.at[sem],
                    recv_sem=recv_sems.at[sem], device_id=(x, y, 1 - c), device_id_type=MESH_ID))
        return sends, recvs, local

    return _Carried(list(shards) + list(halves),
                    [jax.ShapeDtypeStruct((N_CHIPS,) + s.shape, s.dtype) for s in shards], 3 * n, 4 * n, build)


def _grad_copies(stacked):
    n = len(stacked)

    def build(ins, outs, send_sems, recv_sems, local_sems):
        x, y, c = _place()
        sends = []
        for w in range(n):
            for k, (px, py) in enumerate(_other_chips(x, y)):
                sem = 3 * w + k
                sends.append(pltpu.make_async_remote_copy(
                    src_ref=ins[w].at[2 * px + py], dst_ref=outs[w].at[k], send_sem=send_sems.at[sem],
                    recv_sem=recv_sems.at[sem], device_id=(px, py, c), device_id_type=MESH_ID))
        return sends, sends, [], _y_first(sends)

    return _Carried(stacked, [jax.ShapeDtypeStruct((3,) + s.shape[1:], s.dtype) for s in stacked], 3 * n, 0, build)


def _small_copies(small):
    def build(ins, outs, send_sems, recv_sems, local_sems):
        small_ref, all_ref = ins[0], outs[0]
        x, y, c = _place()
        me = 4 * x + 2 * y + c
        sends, recvs = [], []
        for r in range(1, 8):
            px = 1 - x if r & 4 else x
            py = 1 - y if r & 2 else y
            pc = 1 - c if r & 1 else c
            sends.append(pltpu.make_async_remote_copy(
                src_ref=small_ref, dst_ref=all_ref.at[me], send_sem=send_sems.at[r - 1], recv_sem=recv_sems.at[r - 1],
                device_id=(px, py, pc), device_id_type=MESH_ID))
            recvs.append(pltpu.make_async_remote_copy(
                src_ref=small_ref, dst_ref=all_ref.at[4 * px + 2 * py + pc], send_sem=send_sems.at[r - 1],
                recv_sem=recv_sems.at[r - 1], device_id=(px, py, pc), device_id_type=MESH_ID))
        return sends, recvs, [pltpu.make_async_copy(small_ref, all_ref.at[me], local_sems.at[0])]

    return _Carried([small], [jax.ShapeDtypeStruct((8,) + small.shape, small.dtype)], 7, 1, build)


def _gather_by_neighbours(name, shard):
    half = shard.shape[0] // 2
    quarter = half // 2

    def body(in_ref, out_ref, send_sems, recv_sems, local_sem):
        for core in (0, 1):
            @pl.when(lax.axis_index("c") == core)
            def _():
                program(core, in_ref, out_ref, send_sems, recv_sems, local_sem)

    def program(c, in_ref, out_ref, send_sems, recv_sems, local_sem):
        x, y, _ = _place()
        chip = lambda px, py: 2 * px + py
        to_x, to_y, sibling = (1 - x, y, c), (x, 1 - y, c), (x, y, 1 - c)
        x_blk, y_blk, d_blk = chip(1 - x, y), chip(x, 1 - y), chip(1 - x, 1 - y)
        mine, theirs = c * half, (1 - c) * half

        def copy(sem, rows, block, to, src=None):
            place = out_ref.at[block, pl.ds(rows[0], rows[1])]
            return pltpu.make_async_remote_copy(
                src_ref=place if src is None else src, dst_ref=place, send_sem=send_sems.at[sem],
                recv_sem=recv_sems.at[sem], device_id=to, device_id_type=MESH_ID)

        own = pltpu.make_async_copy(in_ref, out_ref.at[chip(x, y)], local_sem)
        own.start()
        my_rows = in_ref.at[pl.ds(mine, half)]
        along_x = dict(send=copy(0, (mine, half), chip(x, y), to_x, src=my_rows),
                       landed=copy(0, (mine, half), x_blk, to_x),
                       onward=[copy(3, (mine + quarter, quarter), x_blk, to_y), copy(4, (mine, half), x_blk, sibling)],
                       diagonal=copy(2, (mine, quarter), d_blk, to_x))
        along_y = dict(send=copy(1, (mine, half), chip(x, y), to_y, src=my_rows),
                       landed=copy(1, (mine, half), y_blk, to_y),
                       onward=[copy(2, (mine, quarter), y_blk, to_x), copy(5, (mine, half), y_blk, sibling)],
                       diagonal=copy(3, (mine + quarter, quarter), d_blk, to_y))
        last = copy(6, (mine, half), d_blk, sibling)

        order = (along_x, along_y) if c == 0 else (along_y, along_x)
        for axis in order:
            axis["send"].start()
        for axis in order:
            axis["landed"].wait_recv()
            for cp in axis["onward"]:
                cp.start()
        for axis in order:
            axis["diagonal"].wait_recv()
        last.start()
        for sem, block in ((4, x_blk), (5, y_blk), (6, d_blk)):
            copy(sem, (theirs, half), block, sibling).wait_recv()
        for cp in [along_x["send"], along_y["send"]] + along_x["onward"] + along_y["onward"] + [last]:
            cp.wait_send()
        own.wait()

    return pl.pallas_call(
        body, name=name, in_specs=[HBM_SPEC], out_specs=HBM_SPEC,
        out_shape=jax.ShapeDtypeStruct((N_CHIPS,) + shard.shape, shard.dtype),
        scratch_shapes=[pltpu.SemaphoreType.DMA((7,)), pltpu.SemaphoreType.DMA((7,)), pltpu.SemaphoreType.DMA(())],
    )(shard)


def _copies_alone(name, comm):
    return _call(name, lambda: None, grid=(), in_specs=[], out_specs=[], out_shape=[], args=[], comm=comm)[1]


class _Net:
    def __init__(self, shards):
        self.shards = shards
        self.whole, self.halves, self.own, self.theirs, self.sums, self.other = {}, {}, {}, {}, {}, {}
        x, y, _ = _place()
        self.chip = 2 * x + y

    def fetch(self, names):
        return _fetch_copies([self.shards[n] for n in names])

    def fetched(self, names, got):
        self.halves.update(zip(names, got))

    def relay(self, names):
        return _relay_copies([self.shards[n] for n in names], [self.halves[n] for n in names])

    def gathered(self, names, got):
        for n, g in zip(names, got):
            self.whole[n] = g.reshape(-1, g.shape[-1])

    def full(self, name):
        return self.whole[name]

    def exchange(self, grads, swap=()):
        stacked = []
        for n, pieces in grads.items():
            (keep, send), = pieces
            self.own[n] = keep
            stacked.append(send.reshape(N_CHIPS, keep.shape[0] // N_CHIPS, send.shape[-1]))
        return _join(_grad_copies(stacked), self.swap(swap))

    def swap(self, names):
        return _sibling_copies([self.sums[n] for n in names]) if names else None

    def presum_begin(self, name, pieces):
        keep = jnp.concatenate([p[0] for p in pieces], axis=0) if len(pieces) > 1 else pieces[0][0]
        send = jnp.concatenate([p[1] for p in pieces], axis=0) if len(pieces) > 1 else pieces[0][1]
        rows = keep.shape[0] // N_CHIPS
        self.held = keep.reshape(N_CHIPS, rows, keep.shape[-1])
        return _half_rows_copies(send.reshape(N_CHIPS, rows, send.shape[-1]))

    def presum_end(self, name, got):
        x, y, c = _place()
        to_send, self.own[name] = _pre_sum("presum_" + name, self.held, got[0], jnp.stack([c, self.chip]))
        return _grad_copies([to_send])

    def received(self, names, swap, got, carried=None):
        self.theirs.update(zip(names, got[:len(names)]))
        self.other.update(zip(swap, got[len(names):]))
        for n in names:
            (self.sums[n],), more = _partial_sum("sum_" + n, self.own[n], self.theirs[n], self.chip, comm=carried)
        return more


def _half_rows_copies(stacked):
    n, rows = stacked.shape[0], stacked.shape[1] // 2

    def build(ins, outs, send_sems, recv_sems, local_sems):
        x, y, c = _place()
        copies = [pltpu.make_async_remote_copy(
            src_ref=ins[0].at[s, pl.ds((1 - c) * rows, rows)], dst_ref=outs[0].at[s], send_sem=send_sems.at[s],
            recv_sem=recv_sems.at[s], device_id=(x, y, 1 - c), device_id_type=MESH_ID) for s in range(n)]
        return copies, copies, []

    return _Carried([stacked], [jax.ShapeDtypeStruct((n, rows, stacked.shape[2]), stacked.dtype)], n, 0, build)


def _pre_sum(name, held, theirs, core_and_chip):
    n, R, C = held.shape
    half = R // 2
    tr = _row_tile(half)
    per_half = half // tr

    def body(place_ref, h_ref, t_ref, send_ref, own_ref):
        total = h_ref[0] + t_ref[0].astype(F32)
        send_ref[0] = total.astype(send_ref.dtype)

        @pl.when(pl.program_id(1) == place_ref[1])
        def _():
            own_ref[...] = total

    return pl.pallas_call(
        body, name=name,
        grid_spec=pltpu.PrefetchScalarGridSpec(
            num_scalar_prefetch=1, grid=(per_half, n),
            in_specs=[pl.BlockSpec((1, tr, C), lambda i, s, place: (s, place[0] * per_half + i, 0)),
                      pl.BlockSpec((1, tr, C), lambda i, s, place: (s, i, 0))],
            out_specs=[pl.BlockSpec((1, tr, C), lambda i, s, place: (s, i, 0)),
                       pl.BlockSpec((tr, C), lambda i, s, place: (i, 0))]),
        out_shape=[jax.ShapeDtypeStruct((n, half, C), MXU_DTYPE), jax.ShapeDtypeStruct((half, C), F32)],
        compiler_params=_params(("arbitrary", "arbitrary")),
    )(core_and_chip, held, theirs)


def _sibling_copies(parts):
    n = len(parts)

    def build(ins, outs, send_sems, recv_sems, local_sems):
        x, y, c = _place()
        copies = [pltpu.make_async_remote_copy(
            src_ref=ins[w], dst_ref=outs[w], send_sem=send_sems.at[w], recv_sem=recv_sems.at[w],
            device_id=(x, y, 1 - c), device_id_type=MESH_ID) for w in range(n)]
        return copies, copies, []

    return _Carried(parts, [jax.ShapeDtypeStruct(p.shape, p.dtype) for p in parts], n, 0, build)


def _row_tile(rows, most=512, sublanes=16):
    return max(t for t in range(sublanes, min(most, rows // 2) + 1, sublanes) if rows % t == 0)


def _partial_sum(name, own, recv, chip, comm=None):
    _, R, C = recv.shape
    tr = _row_tile(R)

    def body(o_ref, r_ref, p_ref):
        p_ref[...] = ((o_ref[...] + r_ref[0].astype(F32)) + r_ref[1].astype(F32)) + r_ref[2].astype(F32)

    if own.shape[0] != R:
        assert comm is None and own.shape[0] == N_CHIPS * R
        total = pl.pallas_call(
            lambda chip_ref, *refs: body(*refs), name=name,
            grid_spec=pltpu.PrefetchScalarGridSpec(
                num_scalar_prefetch=1, grid=(R // tr,),
                in_specs=[pl.BlockSpec((tr, C), lambda i, chip_ref: (chip_ref[0] * (R // tr) + i, 0)),
                          pl.BlockSpec((3, tr, C), lambda i, chip_ref: (0, i, 0))],
                out_specs=pl.BlockSpec((tr, C), lambda i, chip_ref: (i, 0))),
            out_shape=jax.ShapeDtypeStruct((R, C), F32), compiler_params=_params(("parallel",)),
        )(chip.reshape(1), own, recv)
        return [total], []
    return _call(name, body, grid=(R // tr,),
                 in_specs=[pl.BlockSpec((tr, C), lambda i: (i, 0)), pl.BlockSpec((3, tr, C), lambda i: (0, i, 0))],
                 out_specs=[pl.BlockSpec((tr, C), lambda i: (i, 0))], out_shape=[jax.ShapeDtypeStruct((R, C), F32)],
                 args=[own, recv], semantics=("parallel",), comm=comm)


def _adam_vals(w, g, m, v):
    m = ADAM_B1 * m + (1.0 - ADAM_B1) * g
    v = ADAM_B2 * v + (1.0 - ADAM_B2) * (g * g)
    m_hat = m / (1.0 - ADAM_B1 ** ADAM_STEP)
    v_hat = v / (1.0 - ADAM_B2 ** ADAM_STEP)
    delta = -ADAM_LR * (m_hat / (jnp.sqrt(v_hat) + ADAM_EPS) + ADAM_WD * w)
    return delta, m, v


def _adamw(name, w, m, v, mine, other, comm=None):
    R, C = w.shape
    tr = _row_tile(R)

    def body(w_ref, m_ref, v_ref, s_ref, n_ref, g_ref, d_ref, nm_ref, nv_ref):
        g = s_ref[...] + n_ref[...]
        d, nm, nv = _adam_vals(w_ref[...], g, m_ref[...], v_ref[...])
        g_ref[...], d_ref[...], nm_ref[...], nv_ref[...] = g, d, nm, nv

    spec = pl.BlockSpec((tr, C), lambda i: (i, 0))
    return _call(name, body, grid=(R // tr,), in_specs=[spec] * 5, out_specs=[spec] * 4,
                 out_shape=[jax.ShapeDtypeStruct((R, C), F32)] * 4, args=[w, m, v, mine, other],
                 semantics=("parallel",), comm=comm)


def _adamw_by_halves(name, w, m, v, mine, other, core):
    R, C = w.shape
    tr = _row_tile(R // 2)
    per_half = R // 2 // tr

    def body(c_ref, w_ref, m_ref, v_ref, s_ref, n_ref, g_ref, d_ref, nm_ref, nv_ref):
        g = jnp.where(pl.program_id(0) // per_half == c_ref[0, 0], s_ref[...], n_ref[...])
        d, nm, nv = _adam_vals(w_ref[...], g, m_ref[...], v_ref[...])
        g_ref[...], d_ref[...], nm_ref[...], nv_ref[...] = g, d, nm, nv

    spec = pl.BlockSpec((tr, C), lambda i: (i, 0))
    part = pl.BlockSpec((tr, C), lambda i: (i % per_half, 0))
    return pl.pallas_call(
        body, name=name, grid=(R // tr,),
        in_specs=[pl.BlockSpec(memory_space=pltpu.SMEM), spec, spec, spec, part, part], out_specs=[spec] * 4,
        out_shape=[jax.ShapeDtypeStruct((R, C), F32)] * 4, compiler_params=_params(("parallel",)),
    )(core, w, m, v, mine, other)


SMALL_LAYOUT = dict(norm_mix_g=(0, 1, 1024), b_in=(1, 8, 7424), hgrn_norm_g=(9, 1, 1024), norm_ffn_g=(10, 1, 1024),
                    norm_final_g=(11, 1, 1024), hgrn_lb_logits=(12, 2, 2048), attn_sinks=(14, 1, 16))
SMALL_LOSS_ROW, SMALL_ROWS = 15, 16


def _pack_small(grads, loss):
    rows = [jnp.pad(grads[name].astype(F32).reshape(-1), (0, nrows * D_MODEL - n))
            for name, (_, nrows, n) in SMALL_LAYOUT.items()]
    rows.append(jnp.pad(loss.astype(F32).reshape(1), (0, D_MODEL - 1)))
    return jnp.concatenate(rows).reshape(SMALL_ROWS, D_MODEL)


def _adamw_small(w, m, v, g_all):
    names = list(SMALL_LAYOUT)
    n = len(names)

    def body(a_ref, *refs):
        ins, outs = refs[:3 * n], refs[3 * n:]
        g_all_rows = a_ref[0]
        for dev in range(1, 8):
            g_all_rows = g_all_rows + a_ref[dev]
        for i, name in enumerate(names):
            first, nrows, count = SMALL_LAYOUT[name]
            w_ref, m_ref, v_ref = ins[3 * i:3 * i + 3]
            if w_ref.shape[0] == nrows:
                g = g_all_rows[first:first + nrows, :w_ref.shape[1]]
            else:
                last = count - (nrows - 1) * D_MODEL
                g = jnp.concatenate([g_all_rows[r:r + 1, :] for r in range(first, first + nrows - 1)]
                                    + [g_all_rows[first + nrows - 1:first + nrows, :last]], axis=1)
            d, nm, nv = _adam_vals(w_ref[...], g, m_ref[...], v_ref[...])
            for o_ref, val in zip(outs[4 * i:4 * i + 4], (g, d, nm, nv)):
                o_ref[...] = val
        outs[4 * n][...] = g_all_rows[SMALL_LOSS_ROW:SMALL_LOSS_ROW + 1, 0:1]

    res = pl.pallas_call(
        body, name="adamw_small",
        out_shape=[jax.ShapeDtypeStruct(w[name].shape, F32) for name in names for _ in range(4)]
        + [jax.ShapeDtypeStruct((1, 1), F32)],
    )(g_all, *[t[name] for name in names for t in (w, m, v)])
    return {name: res[4 * i:4 * i + 4] for i, name in enumerate(names)}, res[4 * n]


MATRICES = ("w_in", "w_branch_attn", "w_branch_hgrn", "w_out", "w_ffn_gate", "w_ffn_up", "w_ffn_down")
COLUMN_SHARDED = ("w_in", "w_ffn_gate", "w_ffn_up")
WEIGHTS = ("norm_mix_g", "w_in", "b_in", "attn_sinks", "hgrn_lb_logits", "hgrn_norm_g", "w_branch_attn",
           "w_branch_hgrn", "w_out", "norm_ffn_g", "w_ffn_gate", "w_ffn_up", "w_ffn_down", "norm_final_g")


def kernel(x, norm_mix_g, w_in, b_in, attn_sinks, hgrn_lb_logits, hgrn_norm_g, w_branch_attn, w_branch_hgrn, w_out, norm_ffn_g, w_ffn_gate, w_ffn_up, w_ffn_down, norm_final_g, loss_target, m_norm_mix_g, m_w_in, m_b_in, m_attn_sinks, m_hgrn_lb_logits, m_hgrn_norm_g, m_w_branch_attn, m_w_branch_hgrn, m_w_out, m_norm_ffn_g, m_w_ffn_gate, m_w_ffn_up, m_w_ffn_down, m_norm_final_g, v_norm_mix_g, v_w_in, v_b_in, v_attn_sinks, v_hgrn_lb_logits, v_hgrn_norm_g, v_w_branch_attn, v_w_branch_hgrn, v_w_out, v_norm_ffn_g, v_w_ffn_gate, v_w_ffn_up, v_w_ffn_down, v_norm_final_g):
    given = dict(locals())
    w = {n: given[n] for n in WEIGHTS}
    m = {n: given["m_" + n] for n in WEIGHTS}
    v = {n: given["v_" + n] for n in WEIGHTS}

    block = lambda a, n: jnp.transpose(a[0]) if n in COLUMN_SHARDED else a[0]
    unblock = lambda a, n: (jnp.transpose(a) if n in COLUMN_SHARDED else a)[None]
    net = _Net({n: block(w[n], n).astype(MXU_DTYPE) for n in MATRICES})
    net.gathered(("w_in",), [_gather_by_neighbours("gather_w_in", net.shards["w_in"])])
    vec = dict(norm_mix_g=norm_mix_g, b_in=b_in, attn_sinks=attn_sinks, hgrn_lb_logits=hgrn_lb_logits,
               hgrn_norm_g=hgrn_norm_g, norm_ffn_g=norm_ffn_g, norm_final_g=norm_final_g.reshape(1, D_MODEL))
    loss_part, dx, d_vecs = _local_step(x[0], loss_target[0], vec, net)

    small_all, = net.received(*net.last, carried=_small_copies(_pack_small(d_vecs, loss_part)))
    grads, deltas, new_m, new_v = {}, {}, {}, {}
    for n in ("w_ffn_down", "w_ffn_gate", "w_ffn_up", "w_out", "w_branch_attn", "w_branch_hgrn"):
        res, got = _adamw("adamw_" + n, block(w[n], n), block(m[n], n), block(v[n], n), net.sums[n], net.other[n],
                          comm=net.swap(("w_in",)) if n == "w_ffn_down" else None)
        if n == "w_ffn_down":
            net.other["w_in"], = got
        grads[n], deltas[n], new_m[n], new_v[n] = (unblock(r, n) for r in res)
    n = "w_in"
    res = _adamw_by_halves("adamw_" + n, block(w[n], n), block(m[n], n), block(v[n], n), net.sums[n], net.other[n],
                           _place()[2].reshape(1, 1))
    grads[n], deltas[n], new_m[n], new_v[n] = (unblock(r, n) for r in res)
    rows = lambda t: {n: t[n].reshape(-1, t[n].shape[-1]) for n in SMALL_LAYOUT}
    res, loss = _adamw_small(rows(w), rows(m), rows(v), small_all)
    for n, four in res.items():
        grads[n], deltas[n], new_m[n], new_v[n] = (r.reshape(w[n].shape) for r in four)
    loss = loss.reshape(())
    return (loss, dx[None], *[grads[n] for n in WEIGHTS], *[deltas[n] for n in WEIGHTS],
            *[new_m[n] for n in WEIGHTS], *[new_v[n] for n in WEIGHTS])
```

```python
import collections
import functools
import math

import jax
import jax.numpy as jnp
from jax import lax
from jax.experimental import pallas as pl
from jax.experimental.pallas import tpu as pltpu

F32 = jnp.float32
BF16 = jnp.bfloat16
MXU_DTYPE = jnp.bfloat16
SAVED_DTYPE = jnp.bfloat16
MESH_ID = pl.DeviceIdType.MESH

D_MODEL = 1024
HEAD_DIM = 64
Q_HEADS = 16
KV_HEADS = 2
GROUP = Q_HEADS // KV_HEADS
KV_WIDTH = KV_HEADS * HEAD_DIM
ATTN_BLOCK = 128
HGRN_HEADS = 8
HGRN_K = 128
CHUNK = 64
HGRN_TOKENS = 256
FFN = 2816
IN_SPLITS = (1024, 256, 4096, 2048)
EPS = 1e-6
NEG_INF = -1e30
ADAM_LR, ADAM_B1, ADAM_B2, ADAM_EPS, ADAM_WD, ADAM_STEP = 0.001, 0.9, 0.999, 1e-08, 0.01, 10
N_CHIPS = 4
VMEM_LIMIT = 60 * 1024 * 1024
ROW_ALIGN = 16
DW_ROWS = 256


def _params(sem=None):
    return pltpu.CompilerParams(dimension_semantics=sem, vmem_limit_bytes=VMEM_LIMIT)


def _sigmoid(v):
    return 0.5 * jnp.tanh(0.5 * v) + 0.5


def _dot(a, b, dims):
    return lax.dot_general(a.astype(MXU_DTYPE), b.astype(MXU_DTYPE), (dims, ((), ())),
                           preferred_element_type=F32)


def _nn(a, b):
    return _dot(a, b, ((1,), (0,)))


def _nt(a, b):
    return _dot(a, b, ((1,), (1,)))


def _tn(a, b):
    return _dot(a, b, ((0,), (0,)))


HBM_SPEC = pl.BlockSpec(memory_space=pl.ANY)


class _Carried:
    def __init__(self, arrays, out_shapes, n_remote, n_local, build):
        self.parts = [(len(arrays), len(out_shapes), build)]
        self.arrays, self.out_shapes = list(arrays), list(out_shapes)
        self.scratch = [pltpu.SemaphoreType.DMA((n_remote,)), pltpu.SemaphoreType.DMA((n_remote,)),
                        pltpu.SemaphoreType.DMA((max(n_local, 1),))]

    def __add__(self, other):
        both = _Carried([], [], 1, 0, None)
        both.parts = self.parts + other.parts
        both.arrays, both.out_shapes = self.arrays + other.arrays, self.out_shapes + other.out_shapes
        both.scratch = self.scratch + other.scratch
        return both

    def _built(self, ins, outs, sems):
        for p, (ni, no, build) in enumerate(self.parts):
            yield build(ins[:ni], outs[:no], *sems[3 * p:3 * p + 3])
            ins, outs = ins[ni:], outs[no:]

    def start(self, ins, outs, sems):
        core = lax.axis_index("c")
        for sends, _, local, *other_order in self._built(ins, outs, sems):
            for cp in local:
                cp.start()
            if not other_order:
                for cp in sends:
                    cp.start()
                continue

            @pl.when(core == 0)
            def _():
                for cp in sends:
                    cp.start()

            @pl.when(core == 1)
            def _():
                for cp in other_order[0]:
                    cp.start()

    def wait(self, ins, outs, sems):
        for sends, recvs, local, *_ in self._built(ins, outs, sems):
            for cp in recvs:
                cp.wait_recv()
            for cp in sends:
                cp.wait_send()
            for cp in local:
                cp.wait()


def _join(*comms):
    comms = [c for c in comms if c is not None]
    return functools.reduce(lambda a, b: a + b, comms) if comms else None


def _call(name, body, *, grid, in_specs, out_specs, out_shape, args, scratch=(), semantics=None, comm=None,
          aliases=None):
    n_in, n_out, n_scr = len(in_specs), len(out_specs), len(scratch)
    aliases = aliases or {}
    if comm is None:
        res = pl.pallas_call(body, name=name, grid=grid, in_specs=in_specs, out_specs=out_specs, out_shape=out_shape,
                             scratch_shapes=list(scratch), input_output_aliases=aliases,
                             compiler_params=_params(semantics))(*args)
        return list(res), []
    ci, co = len(comm.arrays), len(comm.out_shapes)

    def carrying(*refs):
        ins, refs = refs[:n_in], refs[n_in:]
        c_ins, refs = refs[:ci], refs[ci:]
        outs, refs = refs[:n_out], refs[n_out:]
        c_outs, refs = refs[:co], refs[co:]
        scr, sems = refs[:n_scr], refs[n_scr:]
        if not grid:
            comm.start(c_ins, c_outs, sems)
            body(*ins, *outs, *scr)
            comm.wait(c_ins, c_outs, sems)
            return
        first = functools.reduce(jnp.logical_and, [pl.program_id(a) == 0 for a in range(len(grid))])
        last = functools.reduce(jnp.logical_and, [pl.program_id(a) == g - 1 for a, g in enumerate(grid)])

        @pl.when(first)
        def _():
            comm.start(c_ins, c_outs, sems)

        body(*ins, *outs, *scr)

        @pl.when(last)
        def _():
            comm.wait(c_ins, c_outs, sems)

    res = pl.pallas_call(
        carrying, name=name, grid=grid, in_specs=list(in_specs) + [HBM_SPEC] * ci,
        out_specs=list(out_specs) + [HBM_SPEC] * co, out_shape=list(out_shape) + comm.out_shapes,
        scratch_shapes=list(scratch) + comm.scratch, input_output_aliases=aliases,
        compiler_params=_params(("arbitrary",) * len(grid) if grid else None),
    )(*args, *comm.arrays)
    return list(res[:n_out]), list(res[n_out:])


_Cols = collections.namedtuple("_Cols", "array first cols")
_Into = collections.namedtuple("_Into", "rows first held")


def _weight_grad(name, a, b, *, tm, tk, a_colsum=False, into=None, carrying=False, comm=None):
    cols = a if isinstance(a, _Cols) else _Cols(a, 0, a.shape[1])
    a = cols.array
    (T, N), M = b.shape, cols.cols
    tk = min(tk, T)
    assert cols.first % tm == 0 and M % tm == 0 and T % tk == 0, (name, cols.first, M, tm, T, tk)
    ni, nk, tile0 = M // tm, T // tk, cols.first // tm
    held = list(into.held) if into is not None and into.held is not None else []

    def body(a_ref, b_ref, *rest):
        keep_ref, send_ref = rest[len(held):len(held) + 2]
        sums_ref = rest[len(held) + 2] if a_colsum else None
        if nk == 1:
            acc = _tn(a_ref[...], b_ref[...])
            keep_ref[...], send_ref[...] = acc, acc.astype(send_ref.dtype)
            if a_colsum:
                sums_ref[...] = jnp.sum(a_ref[...].astype(F32), axis=0, keepdims=True)
            return
        acc_ref = rest[-1]
        k = pl.program_id(1)

        @pl.when(k == 0)
        def _():
            acc_ref[...] = jnp.zeros_like(acc_ref)
            if a_colsum:
                sums_ref[...] = jnp.zeros((1, tm), F32)

        if a_colsum:
            sums_ref[...] += jnp.sum(a_ref[...].astype(F32), axis=0, keepdims=True)
        acc_ref[...] += _tn(a_ref[...], b_ref[...])

        @pl.when(k == nk - 1)
        def _():
            keep_ref[...], send_ref[...] = acc_ref[...], acc_ref[...].astype(send_ref.dtype)

    if into is None:
        rows, out_spec = M, pl.BlockSpec((tm, N), lambda i, k: (i, 0))
    else:
        rows = into.rows
        out_spec = pl.BlockSpec((pl.Element(tm), pl.Element(N)),
                                lambda i, k: (pl.multiple_of(into.first + i * tm, ROW_ALIGN), 0))
    out_shape = [jax.ShapeDtypeStruct((rows, N), F32), jax.ShapeDtypeStruct((rows, N), MXU_DTYPE)]
    out_specs = [out_spec, out_spec]
    if a_colsum:
        out_shape.append(jax.ShapeDtypeStruct((1, M), F32))
        out_specs.append(pl.BlockSpec((1, tm), lambda i, k: (0, i)))
    res, got = _call(
        name, body, grid=(ni, nk),
        in_specs=[pl.BlockSpec((tk, tm), lambda i, k: (k, tile0 + i)), pl.BlockSpec((tk, N), lambda i, k: (k, 0))]
        + [HBM_SPEC] * len(held),
        out_specs=out_specs, out_shape=out_shape, scratch=[pltpu.VMEM((tm, N), F32)] if nk > 1 else [],
        args=[a, b] + held, aliases={2 + p: p for p in range(len(held))}, semantics=("parallel", "arbitrary"),
        comm=comm)
    return (res, got) if carrying else res


def _ffn_fwd(merged, w_out, x, gain, w_gate_t, w_up_t, *, tm, comm=None):
    (T, D), F = x.shape, w_gate_t.shape[0]

    def body(m_ref, wo_ref, x_ref, g_ref, wg_ref, wu_ref, h_ref, u_ref, gate_ref, up_ref, z_ref):
        h = x_ref[...] + _nn(m_ref[...], wo_ref[...])
        h_ref[...] = h
        u = (h * lax.rsqrt(jnp.mean(h * h, axis=-1, keepdims=True) + EPS) * g_ref[...]).astype(u_ref.dtype)
        u_ref[...] = u
        gate, up = _nt(u, wg_ref[...]), _nt(u, wu_ref[...])
        gate_ref[...], up_ref[...] = gate.astype(gate_ref.dtype), up.astype(up_ref.dtype)
        z_ref[...] = (gate * _sigmoid(gate) * up).astype(z_ref.dtype)

    rows = lambda n: pl.BlockSpec((tm, n), lambda i: (i, 0))
    fixed = _fixed_spec
    return _call("ffn_hidden", body, grid=(T // tm,),
                 in_specs=[rows(D), fixed(w_out), rows(D), fixed(gain), fixed(w_gate_t), fixed(w_up_t)],
                 out_specs=[rows(D), rows(D), rows(F), rows(F), rows(F)],
                 out_shape=[jax.ShapeDtypeStruct((T, D), F32), jax.ShapeDtypeStruct((T, D), MXU_DTYPE)]
                 + [jax.ShapeDtypeStruct((T, F), SAVED_DTYPE)] * 2 + [jax.ShapeDtypeStruct((T, F), MXU_DTYPE)],
                 args=[merged, w_out, x, gain, w_gate_t, w_up_t], semantics=("parallel",), comm=comm)


def _in_proj(x, gain, w_in_t, b_in, *, tm, comm=None):
    T, D = x.shape
    bounds = [sum(IN_SPLITS[:i]) for i in range(len(IN_SPLITS) + 1)]

    def body(x_ref, g_ref, w_ref, b_ref, u_ref, *piece_refs):
        xv = x_ref[...]
        r = lax.rsqrt(jnp.mean(xv * xv, axis=-1, keepdims=True) + EPS)
        u = (xv * r * g_ref[...]).astype(u_ref.dtype)
        u_ref[...] = u
        for o_ref, lo, hi in zip(piece_refs, bounds[:-1], bounds[1:]):
            o_ref[...] = (_nt(u, w_ref[lo:hi, :]) + b_ref[:, lo:hi]).astype(o_ref.dtype)

    rows = lambda n: pl.BlockSpec((tm, n), lambda i: (i, 0))
    fixed = _fixed_spec
    dtypes = (MXU_DTYPE, MXU_DTYPE, F32, F32)
    return _call("in_proj", body, grid=(T // tm,),
                 in_specs=[rows(D), fixed(gain), fixed(w_in_t), fixed(b_in)],
                 out_specs=[rows(D)] + [rows(n) for n in IN_SPLITS],
                 out_shape=[jax.ShapeDtypeStruct((T, D), MXU_DTYPE)]
                 + [jax.ShapeDtypeStruct((T, n), dt) for n, dt in zip(IN_SPLITS, dtypes)],
                 args=[x, gain, w_in_t, b_in], semantics=("parallel",), comm=comm)


def _row_spec(tm, n):
    return pl.BlockSpec((tm, n), lambda i: (i, 0))


def _fixed_spec(a):
    return pl.BlockSpec(a.shape, lambda i: (0,) * a.ndim, pipeline_mode=pl.Buffered(1))


def _partials_spec(n):
    return pl.BlockSpec((8, n), lambda i: (i, 0))


def _row_parts(tm, parts):
    assert tm % parts == 0, (tm, parts)
    return [slice(p * (tm // parts), (p + 1) * (tm // parts)) for p in range(parts)]


def _ffn_tail(z, w_down, h1, target, gain, gate, up, *, tm, parts=2):
    (T, F), D = z.shape, h1.shape[1]

    def body(z_ref, w_ref, h_ref, t_ref, g_ref, gate_ref, up_ref, dh_ref, dhb_ref, dgate_ref, dup_ref, dg_ref, l_ref):
        part, dgain = 0.0, 0.0
        pieces = _row_parts(tm, parts)
        h2s = [h_ref[rows, :] + _nn(z_ref[rows, :], w_ref[...]) for rows in pieces]
        for rows, h2 in zip(pieces, h2s):
            r = lax.rsqrt(jnp.mean(h2 * h2, axis=-1, keepdims=True) + EPS)
            xhat = h2 * r
            err = xhat * g_ref[...] - t_ref[rows, :]
            part += 0.5 * jnp.sum(jnp.sum(err * err, axis=-1, keepdims=True), axis=0, keepdims=True) / D
            dy = err / D
            dxh = dy * g_ref[...]
            dh2 = r * (dxh - xhat * jnp.mean(dxh * xhat, axis=-1, keepdims=True))
            dh_ref[rows, :] = dh2
            dhb = dh2.astype(dhb_ref.dtype)
            dhb_ref[rows, :] = dhb
            dgain += jnp.sum(dy * xhat, axis=0, keepdims=True)
            dz = _nt(dhb, w_ref[...])
            gv, upv = gate_ref[rows, :].astype(F32), up_ref[rows, :].astype(F32)
            s = _sigmoid(gv)
            dgate_ref[rows, :] = (dz * upv * (s * (1.0 + gv * (1.0 - s)))).astype(dgate_ref.dtype)
            dup_ref[rows, :] = (dz * (gv * s)).astype(dup_ref.dtype)
        dg_ref[...] = jnp.broadcast_to(dgain, dg_ref.shape)
        l_ref[...] = jnp.broadcast_to(part, l_ref.shape)

    low = lambda n: jax.ShapeDtypeStruct((T, n), MXU_DTYPE)
    part = jax.ShapeDtypeStruct((8 * (T // tm), D), F32)
    return pl.pallas_call(
        body, name="ffn_tail", grid=(T // tm,),
        in_specs=[_row_spec(tm, F), _fixed_spec(w_down), _row_spec(tm, D), _row_spec(tm, D), _fixed_spec(gain),
                  _row_spec(tm, F), _row_spec(tm, F)],
        out_specs=[_row_spec(tm, D), _row_spec(tm, D), _row_spec(tm, F), _row_spec(tm, F), _partials_spec(D),
                   _partials_spec(D)],
        out_shape=[jax.ShapeDtypeStruct((T, D), F32), low(D), low(F), low(F), part, part],
        compiler_params=_params(("parallel",)),
    )(z, w_down, h1, target, gain, gate, up)


def _ffn_in_bwd(dgate, dup, w_gate_t, w_up_t, h1, gain, dres, *, tm, comm=None):
    (T, F), D = dgate.shape, h1.shape[1]

    def body(dg_ref, du_ref, wg_ref, wu_ref, h_ref, g_ref, r_ref, dh_ref, dhb_ref, dgain_ref):
        d_u2 = _nn(dg_ref[...], wg_ref[...]) + _nn(du_ref[...], wu_ref[...])
        dx, dgain = _rmsnorm_bwd_vals(d_u2, h_ref[...], g_ref[...])
        dh = r_ref[...] + dx
        dh_ref[...] = dh
        dhb_ref[...] = dh.astype(dhb_ref.dtype)
        dgain_ref[...] = jnp.broadcast_to(dgain, dgain_ref.shape)

    return _call("d_ffn_in", body, grid=(T // tm,),
                 in_specs=[_row_spec(tm, F), _row_spec(tm, F), _fixed_spec(w_gate_t), _fixed_spec(w_up_t),
                           _row_spec(tm, D), _fixed_spec(gain), _row_spec(tm, D)],
                 out_specs=[_row_spec(tm, D), _row_spec(tm, D), _partials_spec(D)],
                 out_shape=[jax.ShapeDtypeStruct((T, D), F32), jax.ShapeDtypeStruct((T, D), MXU_DTYPE),
                            jax.ShapeDtypeStruct((8 * (T // tm), D), F32)],
                 args=[dgate, dup, w_gate_t, w_up_t, h1, gain, dres], semantics=("parallel",), comm=comm)


def _in_proj_bwd(pieces, w_in_t, x, gain, dres, *, tm, comm=None):
    T, D = x.shape
    n = len(pieces)

    def body(*refs):
        dps, (w_ref, x_ref, g_ref, r_ref, dx_ref, dgain_ref) = refs[:n], refs[n:]
        d_u = None
        for dp_ref, (dp, first) in zip(dps, pieces):
            term = _nn(dp_ref[...], w_ref[first:first + dp.shape[1], :])
            d_u = term if d_u is None else d_u + term
        dx, dgain = _rmsnorm_bwd_vals(d_u, x_ref[...], g_ref[...])
        dx_ref[...] = r_ref[...] + dx
        dgain_ref[...] = jnp.broadcast_to(dgain, dgain_ref.shape)

    return _call("d_u", body, grid=(T // tm,),
                 in_specs=[_row_spec(tm, dp.shape[1]) for dp, _ in pieces]
                 + [_fixed_spec(w_in_t), _row_spec(tm, D), _fixed_spec(gain), _row_spec(tm, D)],
                 out_specs=[_row_spec(tm, D), _partials_spec(D)],
                 out_shape=[jax.ShapeDtypeStruct((T, D), F32), jax.ShapeDtypeStruct((8 * (T // tm), D), F32)],
                 args=[dp for dp, _ in pieces] + [w_in_t, x, gain, dres], semantics=("parallel",), comm=comm)


def _merge_fwd(y_a, y_b, w_a, w_b, gates, *, tm, comm=None):
    T, D = y_a.shape

    def body(ya_ref, yb_ref, wa_ref, wb_ref, ga_ref, gb_ref, pa_ref, pb_ref, m_ref):
        pa, pb = _nn(ya_ref[...], wa_ref[...]), _nn(yb_ref[...], wb_ref[...])
        pa_ref[...], pb_ref[...] = pa.astype(pa_ref.dtype), pb.astype(pb_ref.dtype)
        m_ref[...] = (_sigmoid(ga_ref[...]) * pa + _sigmoid(gb_ref[...]) * pb).astype(m_ref.dtype)

    rows = pl.BlockSpec((tm, D), lambda i: (i, 0))
    whole = pl.BlockSpec((D, D), lambda i: (0, 0), pipeline_mode=pl.Buffered(1))
    return _call("branch_merge", body, grid=(T // tm,),
                 in_specs=[rows, rows, whole, whole, rows, pl.BlockSpec((tm, D), lambda i: (i, 1))],
                 out_specs=[rows] * 3,
                 out_shape=[jax.ShapeDtypeStruct((T, D), SAVED_DTYPE)] * 2 + [jax.ShapeDtypeStruct((T, D), MXU_DTYPE)],
                 args=[y_a, y_b, w_a, w_b, gates, gates], semantics=("parallel",), comm=comm)


def _merge_bwd(dh, w_out, w_a, w_b, p_a, p_b, gates, *, tm, d_in_width, first):
    T, D = dh.shape

    def body(dh_ref, wo_ref, wa_ref, wb_ref, pa_ref, pb_ref, ga_ref, gb_ref, dpa_ref, dpb_ref, dya_ref, dyb_ref,
             din_ref):
        dm = _nt(dh_ref[...], wo_ref[...])
        sa, sb = _sigmoid(ga_ref[...]), _sigmoid(gb_ref[...])
        dpa, dpb = (dm * sa).astype(dpa_ref.dtype), (dm * sb).astype(dpb_ref.dtype)
        dpa_ref[...], dpb_ref[...] = dpa, dpb
        din_ref[:, :D] = (dm * pa_ref[...].astype(F32) * sa * (1.0 - sa)).astype(din_ref.dtype)
        din_ref[:, D:] = (dm * pb_ref[...].astype(F32) * sb * (1.0 - sb)).astype(din_ref.dtype)
        dya_ref[...] = _nt(dpa, wa_ref[...]).astype(dya_ref.dtype)
        dyb_ref[...] = _nt(dpb, wb_ref[...]).astype(dyb_ref.dtype)

    rows = pl.BlockSpec((tm, D), lambda i: (i, 0))
    whole = pl.BlockSpec((D, D), lambda i: (0, 0), pipeline_mode=pl.Buffered(1))
    low = jax.ShapeDtypeStruct((T, D), MXU_DTYPE)
    return pl.pallas_call(
        body, name="d_branch_merge", grid=(T // tm,),
        in_specs=[rows, whole, whole, whole, rows, rows, rows, pl.BlockSpec((tm, D), lambda i: (i, 1))],
        out_specs=[rows] * 4 + [pl.BlockSpec((pl.Element(tm), pl.Element(2 * D)), lambda i: (
            pl.multiple_of(i * tm, ROW_ALIGN), first))],
        out_shape=[low] * 3 + [jax.ShapeDtypeStruct((T, D), F32), jax.ShapeDtypeStruct((T, d_in_width), MXU_DTYPE)],
        compiler_params=_params(("parallel",)),
    )(dh, w_out, w_a, w_b, p_a, p_b, gates, gates)


def _colsum_partials(p):
    return jnp.sum(p.reshape(-1, 8, p.shape[-1])[:, 0, :], axis=0, keepdims=True)


def _rmsnorm_bwd_vals(dy, xin, g):
    rstd = lax.rsqrt(jnp.mean(xin * xin, axis=-1, keepdims=True) + EPS)
    xhat = xin * rstd
    dg = jnp.sum(dy * xhat, axis=0, keepdims=True)
    dxh = dy * g
    dx = rstd * (dxh - xhat * jnp.mean(dxh * xhat, axis=-1, keepdims=True))
    return dx, dg


ATTN_SCALE = 1.0 / math.sqrt(HEAD_DIM)
GROUP_LANES = GROUP * ATTN_BLOCK
PAIR = 2 * HEAD_DIM


def _attn_mask():
    kj = lax.broadcasted_iota(jnp.int32, (ATTN_BLOCK, GROUP_LANES), 0)
    qi = lax.broadcasted_iota(jnp.int32, (ATTN_BLOCK, GROUP_LANES), 1) & (ATTN_BLOCK - 1)
    return kj <= qi


def _heads_transposed(ref, g, scale=None):
    parts = []
    for a in range(GROUP // 2):
        lo = (g * GROUP // 2 + a) * PAIR
        pair = ref[:, lo:lo + PAIR].astype(F32)
        pair = (pair if scale is None else pair * scale).T
        parts += [pair[:HEAD_DIM], pair[HEAD_DIM:]]
    return jnp.concatenate(parts, axis=1).astype(MXU_DTYPE)


def _heads_back(ref, g, vt):
    for a in range(GROUP // 2):
        lo = (g * GROUP // 2 + a) * PAIR
        pair = jnp.concatenate([vt[:, (2 * a) * ATTN_BLOCK:(2 * a + 1) * ATTN_BLOCK],
                                vt[:, (2 * a + 1) * ATTN_BLOCK:(2 * a + 2) * ATTN_BLOCK]], axis=0)
        ref[:, lo:lo + PAIR] = pair.T.astype(ref.dtype)


def _kv_parts(kv_ref, g):
    ks = slice(g * HEAD_DIM, (g + 1) * HEAD_DIM)
    vs = slice(KV_WIDTH + g * HEAD_DIM, KV_WIDTH + (g + 1) * HEAD_DIM)
    return kv_ref[:, ks].astype(MXU_DTYPE), kv_ref[:, vs].astype(MXU_DTYPE)


def _sink_rows(sinks):
    return jnp.repeat(sinks.reshape(KV_HEADS, GROUP), ATTN_BLOCK, axis=1)


def _attn_fwd(pq, pkv, sinks, comm=None):
    T = pq.shape[0]
    nb = T // ATTN_BLOCK

    def body(q_ref, kvc_ref, kvp_ref, s_ref, y_ref, lse_ref):
        mask_c = _attn_mask()
        has_prev = pl.program_id(0) > 0
        for g in range(KV_HEADS):
            (kc, vc), (kp, vp) = _kv_parts(kvc_ref, g), _kv_parts(kvp_ref, g)
            qt = _heads_transposed(q_ref, g, ATTN_SCALE)
            s = jnp.where(mask_c, _nn(kc, qt), jnp.where(has_prev, _nn(kp, qt), NEG_INF))
            sink = s_ref[g:g + 1, :]
            m = jnp.maximum(jnp.max(s, axis=0, keepdims=True), sink)
            p = jnp.exp(s - m)
            den = jnp.sum(p, axis=0, keepdims=True) + jnp.exp(sink - m)
            pc = jnp.where(mask_c, p, 0.0)
            _heads_back(y_ref, g, (_tn(vc, pc) + _tn(vp, p - pc)) / den)
            lse = m + jnp.log(den)
            for i in range(GROUP):
                lse_ref[g * GROUP + i:g * GROUP + i + 1, :] = lse[:, i * ATTN_BLOCK:(i + 1) * ATTN_BLOCK]

    return _call(
        "attn_fwd", body, grid=(nb,),
        in_specs=[pl.BlockSpec((ATTN_BLOCK, D_MODEL), lambda n: (n, 0)),
                  pl.BlockSpec((ATTN_BLOCK, 2 * KV_WIDTH), lambda n: (n, 0)),
                  pl.BlockSpec((ATTN_BLOCK, 2 * KV_WIDTH), lambda n: (jnp.maximum(n - 1, 0), 0)),
                  pl.BlockSpec((KV_HEADS, GROUP_LANES), lambda n: (0, 0))],
        out_specs=[pl.BlockSpec((ATTN_BLOCK, D_MODEL), lambda n: (n, 0)),
                   pl.BlockSpec((Q_HEADS, ATTN_BLOCK), lambda n: (0, n))],
        out_shape=[jax.ShapeDtypeStruct((T, D_MODEL), MXU_DTYPE), jax.ShapeDtypeStruct((Q_HEADS, T), F32)],
        args=[pq, pkv, pkv, _sink_rows(sinks)], semantics=("parallel",), comm=comm)


def _attn_bwd(pq, pkv, sinks, lse, dy, d_in, comm=None):
    T = pq.shape[0]
    nb = T // ATTN_BLOCK
    cur = lambda n: (jnp.minimum(n, nb - 1), 0)
    done = D_MODEL + 2 * KV_WIDTH

    def body(q_ref, kvc_ref, kvp_ref, s_ref, lse_ref, dy_ref, _, out_ref, ds_ref, carry, top, bot, dq_ref):
        n = pl.program_id(0)

        @pl.when(n == 0)
        def _():
            carry[...] = jnp.zeros_like(carry)
            dq_ref[...] = jnp.zeros_like(dq_ref)
            ds_ref[...] = jnp.zeros_like(ds_ref)

        out_ref[:, :D_MODEL] = dq_ref[...]

        @pl.when(n < nb)
        def _():
            mask_c = _attn_mask()
            valid = jnp.logical_or(mask_c, n > 0)
            for g in range(KV_HEADS):
                ks = slice(g * HEAD_DIM, (g + 1) * HEAD_DIM)
                vs = slice(KV_WIDTH + g * HEAD_DIM, KV_WIDTH + (g + 1) * HEAD_DIM)
                (kc, vc), (kp, vp) = _kv_parts(kvc_ref, g), _kv_parts(kvp_ref, g)
                qt = _heads_transposed(q_ref, g, ATTN_SCALE)
                dot = _heads_transposed(dy_ref, g)
                lse = jnp.concatenate([lse_ref[g * GROUP + i:g * GROUP + i + 1, :] for i in range(GROUP)], axis=1)
                p = jnp.where(valid, jnp.exp(jnp.where(mask_c, _nn(kc, qt), _nn(kp, qt)) - lse), 0.0)
                dp = jnp.where(mask_c, _nn(vc, dot), _nn(vp, dot))
                delta = jnp.sum(p * dp, axis=0, keepdims=True)
                ds = p * (dp - delta)
                ds_c, p_c = jnp.where(mask_c, ds, 0.0), jnp.where(mask_c, p, 0.0)
                ds_p, p_p = ds - ds_c, p - p_c
                _heads_back(dq_ref, g, (_tn(kc, ds_c) + _tn(kp, ds_p)) * ATTN_SCALE)
                bot[:, ks], bot[:, vs] = _nt(ds_c, qt), _nt(p_c, dot)
                top[:, ks], top[:, vs] = _nt(ds_p, qt), _nt(p_p, dot)
                ds_ref[g:g + 1, :] -= jnp.exp(s_ref[g:g + 1, :] - lse) * delta
            out_ref[:, D_MODEL:] = (carry[...] + top[...]).astype(out_ref.dtype)
            carry[...] = bot[...]

        @pl.when(n == nb)
        def _():
            out_ref[:, D_MODEL:] = carry[...].astype(out_ref.dtype)

    return _call(
        "attn_bwd", body, grid=(nb + 1,),
        in_specs=[pl.BlockSpec((ATTN_BLOCK, D_MODEL), cur),
                  pl.BlockSpec((ATTN_BLOCK, 2 * KV_WIDTH), cur),
                  pl.BlockSpec((ATTN_BLOCK, 2 * KV_WIDTH), lambda n: (jnp.maximum(jnp.minimum(n, nb - 1) - 1, 0), 0)),
                  pl.BlockSpec((KV_HEADS, GROUP_LANES), lambda n: (0, 0)),
                  pl.BlockSpec((Q_HEADS, ATTN_BLOCK), lambda n: (0, jnp.minimum(n, nb - 1))),
                  pl.BlockSpec((ATTN_BLOCK, D_MODEL), cur), HBM_SPEC],
        out_specs=[pl.BlockSpec((ATTN_BLOCK, done), lambda n: (jnp.maximum(n - 1, 0), 0)),
                   pl.BlockSpec((KV_HEADS, GROUP_LANES), lambda n: (0, 0))],
        out_shape=[jax.ShapeDtypeStruct(d_in.shape, d_in.dtype), jax.ShapeDtypeStruct((KV_HEADS, GROUP_LANES), F32)],
        scratch=[pltpu.VMEM((ATTN_BLOCK, 2 * KV_WIDTH), F32)] * 3 + [pltpu.VMEM((ATTN_BLOCK, D_MODEL), MXU_DTYPE)],
        args=[pq, pkv, pkv, _sink_rows(sinks), lse, dy, d_in], semantics=("arbitrary",), comm=comm, aliases={6: 0})


def _lower_bound(l):
    m = jnp.maximum(l[0:1], l[1:2])
    e0, e1 = jnp.exp(l[0:1] - m), jnp.exp(l[1:2] - m)
    return e0 / (e0 + e1)


def _tri(lower):
    r = lax.broadcasted_iota(jnp.int32, (CHUNK, CHUNK), 0)
    c = lax.broadcasted_iota(jnp.int32, (CHUNK, CHUNK), 1)
    return (r >= c) if lower else (c >= r)


def _chunk_sum(mask, v):
    ones = mask.astype(BF16)
    hi = v.astype(BF16)
    rest = v - hi.astype(F32)
    mid = rest.astype(BF16)
    lo = (rest - mid.astype(F32)).astype(BF16)
    part = lambda t: lax.dot_general(ones, t, (((1,), (0,)), ((), ())), preferred_element_type=F32)
    return part(hi) + part(mid) + part(lo)


def _hgrn_chunk_inputs(hq, hf, lb, causal):
    half_t = 0.5 * jnp.tanh(0.5 * hf)
    sg, sgn = 0.5 + half_t, 0.5 - half_t
    f = lb + (1.0 - lb) * sg
    kk = (1.0 - lb) * sgn
    sq = _sigmoid(hq)
    q = hq * sq
    b = _chunk_sum(causal, jnp.log(f))
    bm, bl = b[CHUNK // 2 - 1:CHUNK // 2, :], b[CHUNK - 1:CHUNK, :]
    e_qm, e_km = jnp.exp(b - bm), jnp.exp(bm - b)
    e_qs, e_kl = e_qm * jnp.exp(bm), e_km * jnp.exp(bl - bm)
    return dict(sg=sg, sgn=sgn, f=f, kk=kk, sq=sq, q=q, e_qm=e_qm, e_km=e_km, e_qs=e_qs, e_kl=e_kl,
                qm=q * e_qm, km=kk * e_km, qs=q * e_qs, kl=kk * e_kl, el=jnp.exp(bl))


def _hgrn_fwd(ph, lb_logits, norm_g, comm=None):
    T = ph.shape[0]
    nblk, cpb = T // HGRN_TOKENS, HGRN_TOKENS // CHUNK
    col = lambda c: pl.BlockSpec((HGRN_TOKENS, D_MODEL), functools.partial(lambda i, c: (i, c), c=c))

    def body(hq_ref, hf_ref, hi_ref, hg_ref, l_ref, ng_ref, y_ref, o_ref, st_ref, s_ref):
        @pl.when(pl.program_id(0) == 0)
        def _():
            s_ref[...] = jnp.zeros_like(s_ref)

        lb = _lower_bound(l_ref[...])
        causal = _tri(True)
        for c in range(cpb):
            rows = slice(c * CHUNK, (c + 1) * CHUNK)
            t = _hgrn_chunk_inputs(hq_ref[rows, :], hf_ref[rows, :], lb, causal)
            qm, km, qs, kl = (t[n].astype(MXU_DTYPE) for n in ("qm", "km", "qs", "kl"))
            v = hi_ref[rows, :].astype(MXU_DTYPE)
            heads = [slice(h * HGRN_K, (h + 1) * HGRN_K) for h in range(HGRN_HEADS)]
            a_all = [jnp.where(causal, _nt(qm[:, ls], km[:, ls]), 0.0).astype(MXU_DTYPE) for ls in heads]
            for h, ls in enumerate(heads):
                st = s_ref[h]
                st_ref[c, ls, :] = st
                o_ref[rows, ls] = _nn(a_all[h], v[:, ls]) + _nt(qs[:, ls], st)
                s_ref[h] = t["el"][:, ls] * st + _tn(v[:, ls], kl[:, ls])
        for h in range(HGRN_HEADS):
            ls = slice(h * HGRN_K, (h + 1) * HGRN_K)
            o = o_ref[:, ls]
            r = lax.rsqrt(jnp.mean(o * o, axis=-1, keepdims=True) + EPS)
            y_ref[:, ls] = (o * r * ng_ref[:, ls] * _sigmoid(hg_ref[:, ls])).astype(y_ref.dtype)

    return _call(
        "hgrn_fwd", body, grid=(nblk,),
        in_specs=[col(0), col(1), col(2), col(3),
                  pl.BlockSpec((2, D_MODEL), lambda i: (0, 0)), pl.BlockSpec((1, D_MODEL), lambda i: (0, 0))],
        out_specs=[pl.BlockSpec((HGRN_TOKENS, D_MODEL), lambda i: (i, 0)),
                   pl.BlockSpec((HGRN_TOKENS, D_MODEL), lambda i: (i, 0)),
                   pl.BlockSpec((cpb, D_MODEL, HGRN_K), lambda i: (i, 0, 0))],
        out_shape=[jax.ShapeDtypeStruct((T, D_MODEL), MXU_DTYPE), jax.ShapeDtypeStruct((T, D_MODEL), F32),
                   jax.ShapeDtypeStruct((T // CHUNK, D_MODEL, HGRN_K), F32)],
        scratch=[pltpu.VMEM((HGRN_HEADS, HGRN_K, HGRN_K), F32)],
        args=[ph, ph, ph, ph, lb_logits, norm_g], semantics=("arbitrary",), comm=comm)


def _hgrn_bwd(ph, o_raw, states, dy, lb_logits, norm_g, d_in, first, comm=None):
    T = ph.shape[0]
    nblk, cpb = T // HGRN_TOKENS, HGRN_TOKENS // CHUNK
    rev = lambda i: nblk - 1 - i
    col = lambda c: pl.BlockSpec((HGRN_TOKENS, D_MODEL), functools.partial(lambda i, c: (rev(i), c), c=c))
    tok = pl.BlockSpec((HGRN_TOKENS, D_MODEL), lambda i: (rev(i), 0))

    def body(hq_ref, hf_ref, hi_ref, hg_ref, o_ref, st_ref, dy_ref, l_ref, ng_ref, _,
             dph_ref, dng_ref, dl_ref, dst_ref, dlb_ref, do_s, dqm_s, dkm_s, dqs_s, dkl_s, dv_s, del_s):
        i = pl.program_id(0)

        @pl.when(i == 0)
        def _():
            dst_ref[...] = jnp.zeros_like(dst_ref)
            dlb_ref[...] = jnp.zeros_like(dlb_ref)
            dng_ref[...] = jnp.zeros_like(dng_ref)

        lb = _lower_bound(l_ref[...])
        causal, anti = _tri(True), _tri(False)
        row = lax.broadcasted_iota(jnp.int32, (CHUNK, D_MODEL), 0)
        for c in reversed(range(cpb)):
            rows = slice(c * CHUNK, (c + 1) * CHUNK)
            hq = hq_ref[rows, :]
            t = _hgrn_chunk_inputs(hq, hf_ref[rows, :], lb, causal)
            sgg = _sigmoid(hg_ref[rows, :])
            dyv = dy_ref[rows, :]
            for h in range(HGRN_HEADS):
                ls = slice(h * HGRN_K, (h + 1) * HGRN_K)
                o = o_ref[rows, ls]
                r = lax.rsqrt(jnp.mean(o * o, axis=-1, keepdims=True) + EPS)
                nrm = o * r
                g_h = sgg[:, ls]
                dph_ref[rows, 3 * D_MODEL + h * HGRN_K:3 * D_MODEL + (h + 1) * HGRN_K] = (
                    dyv[:, ls] * nrm * ng_ref[:, ls] * g_h * (1.0 - g_h)).astype(dph_ref.dtype)
                dyg = dyv[:, ls] * g_h
                dng_ref[:, ls] += jnp.sum(dyg * nrm, axis=0, keepdims=True)
                dn = dyg * ng_ref[:, ls]
                do_s[:, ls] = r * (dn - nrm * jnp.mean(dn * nrm, axis=-1, keepdims=True))
            qm, km, qs, kl = (t[n].astype(MXU_DTYPE) for n in ("qm", "km", "qs", "kl"))
            v = hi_ref[rows, :].astype(MXU_DTYPE)
            do = do_s[...].astype(MXU_DTYPE)
            heads = [slice(h * HGRN_K, (h + 1) * HGRN_K) for h in range(HGRN_HEADS)]
            a_all = [jnp.where(causal, _nt(qm[:, ls], km[:, ls]), 0.0).astype(MXU_DTYPE) for ls in heads]
            da_all = [jnp.where(causal, _nt(do[:, ls], v[:, ls]), 0.0).astype(MXU_DTYPE) for ls in heads]
            for h, ls in enumerate(heads):
                st = st_ref[c, ls, :]
                dst = dst_ref[h]
                a, da = a_all[h], da_all[h]
                dv_s[:, ls] = _tn(a, do[:, ls]) + _nt(kl[:, ls], dst)
                dkl_s[:, ls] = _nn(v[:, ls], dst)
                dqs_s[:, ls] = _nn(do[:, ls], st)
                del_s[:, ls] = jnp.sum(dst * st, axis=0, keepdims=True)
                dst_ref[h] = _tn(do[:, ls], qs[:, ls]) + t["el"][:, ls] * dst
                dqm_s[:, ls] = _nn(da, km[:, ls])
                dkm_s[:, ls] = _tn(da, qm[:, ls])
            dqm, dkm, dqs, dkl = dqm_s[...], dkm_s[...], dqs_s[...], dkl_s[...]
            dq = dqm * t["e_qm"] + dqs * t["e_qs"]
            dk = dkm * t["e_km"] + dkl * t["e_kl"]
            t_qm, t_km, t_kl = dqm * t["qm"], dkm * t["km"], dkl * t["kl"]
            db = t_qm - t_km + dqs * t["qs"] - t_kl
            db_mid = jnp.sum(t_km - t_qm, axis=0, keepdims=True)
            db_last = jnp.sum(t_kl, axis=0, keepdims=True) + del_s[...] * t["el"]
            db = db + jnp.where(row == CHUNK // 2 - 1, db_mid, 0.0) + jnp.where(row == CHUNK - 1, db_last, 0.0)
            dlogf = _chunk_sum(anti, db)
            sq, sg, sgn, f = t["sq"], t["sg"], t["sgn"], t["f"]
            dph_ref[rows, 0:D_MODEL] = (dq * (sq * (1.0 + hq * (1.0 - sq)))).astype(dph_ref.dtype)
            dph_ref[rows, D_MODEL:2 * D_MODEL] = (
                dlogf * (1.0 - lb) * sg * (1.0 - sg) / f - dk * (1.0 - lb) * sgn * (1.0 - sgn)).astype(dph_ref.dtype)
            dph_ref[rows, 2 * D_MODEL:3 * D_MODEL] = dv_s[...].astype(dph_ref.dtype)
            dlb_ref[...] += jnp.sum(dlogf * (1.0 - sg) / f - dk * sgn, axis=0, keepdims=True)

        @pl.when(i == nblk - 1)
        def _():
            dl0 = dlb_ref[...] * lb * (1.0 - lb)
            dl_ref[0:1, :] = dl0
            dl_ref[1:2, :] = -dl0

    wide = pltpu.VMEM((CHUNK, D_MODEL), F32)
    return _call(
        "hgrn_bwd", body, grid=(nblk,),
        in_specs=[col(0), col(1), col(2), col(3), tok,
                  pl.BlockSpec((cpb, D_MODEL, HGRN_K), lambda i: (rev(i), 0, 0)), tok,
                  pl.BlockSpec((2, D_MODEL), lambda i: (0, 0)), pl.BlockSpec((1, D_MODEL), lambda i: (0, 0)), HBM_SPEC],
        out_specs=[pl.BlockSpec((pl.Element(HGRN_TOKENS), pl.Element(4 * D_MODEL)), lambda i: (
                       pl.multiple_of(rev(i) * HGRN_TOKENS, ROW_ALIGN), first)),
                   pl.BlockSpec((1, D_MODEL), lambda i: (0, 0)), pl.BlockSpec((2, D_MODEL), lambda i: (0, 0))],
        out_shape=[jax.ShapeDtypeStruct(d_in.shape, d_in.dtype), jax.ShapeDtypeStruct((1, D_MODEL), F32),
                   jax.ShapeDtypeStruct((2, D_MODEL), F32)],
        scratch=[pltpu.VMEM((HGRN_HEADS, HGRN_K, HGRN_K), F32), pltpu.VMEM((1, D_MODEL), F32),
                 wide, wide, wide, wide, wide, wide, pltpu.VMEM((1, D_MODEL), F32)],
        args=[ph, ph, ph, ph, o_raw, states, dy, lb_logits, norm_g, d_in], semantics=("arbitrary",), comm=comm,
        aliases={9: 0})


def _local_step(x, target, vec, net):
    T, D = x.shape
    norm_mix_g, b_in, sinks, lb_logits = vec["norm_mix_g"], vec["b_in"], vec["attn_sinks"], vec["hgrn_lb_logits"]
    hgrn_norm_g, norm_ffn_g, norm_final_g = vec["hgrn_norm_g"], vec["norm_ffn_g"], vec["norm_final_g"]
    w_in = net.full("w_in")
    o_q, o_kv, o_h, o_g = (sum(IN_SPLITS[:i]) for i in range(4))
    TM = 512

    names = ("w_ffn_gate",)
    (u, pq, pkv, ph, pg), got = _in_proj(x, norm_mix_g, w_in, b_in, tm=256, comm=net.gather(names))
    net.gathered(names, got)
    names = ("w_branch_attn", "w_branch_hgrn")
    (y_attn, lse), got = _attn_fwd(pq, pkv, sinks, comm=net.gather(names))
    net.gathered(names, got)
    names = ("w_ffn_up",)
    (y_hgrn, o_raw, states), got = _hgrn_fwd(ph, lb_logits, hgrn_norm_g, comm=net.gather(names))
    net.gathered(names, got)
    w_ba, w_bh = net.full("w_branch_attn"), net.full("w_branch_hgrn")
    w_gate, w_up = net.full("w_ffn_gate"), net.full("w_ffn_up")
    names = ("w_out",)
    (ya, yb, merged), got = _merge_fwd(y_attn, y_hgrn, w_ba, w_bh, pg, tm=TM, comm=net.gather(names))
    net.gathered(names, got)
    w_out = net.full("w_out")

    names = ("w_ffn_down",)
    (h1, u2, gpre, up, z), got = _ffn_fwd(merged, w_out, x, norm_ffn_g, w_gate, w_up, tm=256,
                                          comm=net.gather(names))
    net.gathered(names, got)
    w_down = net.full("w_ffn_down")

    dh2, dh2b, dgp, dup, dgf_p, loss_p = _ffn_tail(z, w_down, h1, target, norm_final_g, gpre, up, tm=512)
    loss = jnp.sum(loss_p.reshape(-1, 8, D)[:, 0, 0])
    d_norm_final = _colsum_partials(dgf_p)
    d_w_down = _weight_grad("dw_down", z, dh2b, tm=DW_ROWS, tk=T)

    names, swap = ("w_ffn_down",), ()
    (dh1, dh1b, dg2_p), got = _ffn_in_bwd(dgp, dup, w_gate, w_up, h1, norm_ffn_g, dh2, tm=256,
                                          comm=net.exchange(dict(w_ffn_down=[d_w_down]), swap))
    net.received(names, swap, got)
    d_norm_ffn = _colsum_partials(dg2_p)
    d_w_gate = _weight_grad("dw_gate", dgp, u2, tm=DW_ROWS, tk=T)
    d_w_up = _weight_grad("dw_up", dup, u2, tm=DW_ROWS, tk=T)

    rows_in = sum(IN_SPLITS)
    dya, dyb, dy_attn, dy_hgrn, d_in = _merge_bwd(dh1b, w_out, w_ba, w_bh, ya, yb, pg, tm=TM, d_in_width=rows_in,
                                                  first=o_g)
    d_w_out = _weight_grad("dw_out", merged, dh1b, tm=1024, tk=1024)
    d_w_ba = _weight_grad("dw_branch_a", y_attn, dya, tm=1024, tk=1024)
    d_w_bh = _weight_grad("dw_branch_b", y_hgrn, dyb, tm=1024, tk=1024)

    names, swap = ("w_ffn_gate",), ("w_ffn_down",)
    (d_in, dsink), got = _attn_bwd(pq, pkv, sinks, lse, dy_attn, d_in,
                                   comm=net.exchange(dict(w_ffn_gate=[d_w_gate]), swap))
    net.received(names, swap, got)
    names, swap = ("w_ffn_up", "w_out"), ("w_ffn_gate",)
    (d_in, d_hgrn_norm, d_lb_logits), got = _hgrn_bwd(
        ph, o_raw, states, dy_hgrn, lb_logits, hgrn_norm_g, d_in, o_h,
        comm=net.exchange(dict(w_ffn_up=[d_w_up], w_out=[d_w_out]), swap))
    net.received(names, swap, got)

    names, swap = ("w_branch_attn", "w_branch_hgrn"), ("w_ffn_up", "w_out")
    (*d_w_in, d_b_in), got = _weight_grad(
        "dw_in", d_in, u, tm=DW_ROWS, tk=T, a_colsum=True, carrying=True,
        comm=net.exchange(dict(w_branch_attn=[d_w_ba], w_branch_hgrn=[d_w_bh]), swap))
    net.received(names, swap, got)
    d_w_in = [tuple(d_w_in)]

    first_level = net.presum_begin("w_in", d_w_in)
    halves = net.presum_end("w_in", [] if first_level is None else _copies_alone("presum_swap_w_in", first_level))
    names, swap = ("w_in",), ("w_branch_attn", "w_branch_hgrn")
    (dx, dg1_p), got = _in_proj_bwd([(d_in, 0)], w_in, x, norm_mix_g, dh1, tm=256,
                                    comm=_join(halves, net.swap(swap)))
    net.last = (names, swap, got)
    d_norm_mix = _colsum_partials(dg1_p)
    vecs = dict(norm_mix_g=d_norm_mix, b_in=d_b_in, attn_sinks=jnp.sum(dsink.reshape(Q_HEADS, ATTN_BLOCK), axis=1).reshape(1, Q_HEADS),
                hgrn_lb_logits=d_lb_logits,
                hgrn_norm_g=d_hgrn_norm, norm_ffn_g=d_norm_ffn, norm_final_g=d_norm_final)
    return loss, dx, vecs


def _place():
    return lax.axis_index("x"), lax.axis_index("y"), lax.axis_index("c")


def _other_chips(x, y):
    return [(1 - x, y), (x, 1 - y), (1 - x, 1 - y)]


def _y_first(copies):
    return [copies[3 * (i // 3) + (1, 0, 2)[i % 3]] for i in range(len(copies))]


def _gather_copies(shards):
    n = len(shards)

    def build(ins, outs, send_sems, recv_sems, local_sems):
        x, y, c = _place()
        mine = 2 * x + y
        local = [pltpu.make_async_copy(ins[w], outs[w].at[mine], local_sems.at[w]) for w in range(n)]
        sends, recvs = [], []
        for w in range(n):
            for k, (px, py) in enumerate(_other_chips(x, y)):
                sem = 3 * w + k
                sends.append(pltpu.make_async_remote_copy(
                    src_ref=ins[w], dst_ref=outs[w].at[mine], send_sem=send_sems.at[sem], recv_sem=recv_sems.at[sem],
                    device_id=(px, py, c), device_id_type=MESH_ID))
                recvs.append(pltpu.make_async_remote_copy(
                    src_ref=ins[w], dst_ref=outs[w].at[2 * px + py], send_sem=send_sems.at[sem],
                    recv_sem=recv_sems.at[sem], device_id=(px, py, c), device_id_type=MESH_ID))
        return sends, recvs, local, _y_first(sends)

    return _Carried(shards, [jax.ShapeDtypeStruct((N_CHIPS,) + s.shape, s.dtype) for s in shards], 3 * n, n, build)


def _grad_copies(stacked):
    n = len(stacked)

    def build(ins, outs, send_sems, recv_sems, local_sems):
        x, y, c = _place()
        sends = []
        for w in range(n):
            for k, (px, py) in enumerate(_other_chips(x, y)):
                sem = 3 * w + k
                sends.append(pltpu.make_async_remote_copy(
                    src_ref=ins[w].at[2 * px + py], dst_ref=outs[w].at[k], send_sem=send_sems.at[sem],
                    recv_sem=recv_sems.at[sem], device_id=(px, py, c), device_id_type=MESH_ID))
        return sends, sends, [], _y_first(sends)

    return _Carried(stacked, [jax.ShapeDtypeStruct((3,) + s.shape[1:], s.dtype) for s in stacked], 3 * n, 0, build)


def _small_copies(small):
    def build(ins, outs, send_sems, recv_sems, local_sems):
        small_ref, all_ref = ins[0], outs[0]
        x, y, c = _place()
        me = 4 * x + 2 * y + c
        sends, recvs = [], []
        for r in range(1, 8):
            px = 1 - x if r & 4 else x
            py = 1 - y if r & 2 else y
            pc = 1 - c if r & 1 else c
            sends.append(pltpu.make_async_remote_copy(
                src_ref=small_ref, dst_ref=all_ref.at[me], send_sem=send_sems.at[r - 1], recv_sem=recv_sems.at[r - 1],
                device_id=(px, py, pc), device_id_type=MESH_ID))
            recvs.append(pltpu.make_async_remote_copy(
                src_ref=small_ref, dst_ref=all_ref.at[4 * px + 2 * py + pc], send_sem=send_sems.at[r - 1],
                recv_sem=recv_sems.at[r - 1], device_id=(px, py, pc), device_id_type=MESH_ID))
        return sends, recvs, [pltpu.make_async_copy(small_ref, all_ref.at[me], local_sems.at[0])]

    return _Carried([small], [jax.ShapeDtypeStruct((8,) + small.shape, small.dtype)], 7, 1, build)


def _gather_by_neighbours(name, shard):
    half = shard.shape[0] // 2
    quarter = half // 2

    def body(in_ref, out_ref, send_sems, recv_sems, local_sem, stage_ref):
        for core in (0, 1):
            @pl.when(lax.axis_index("c") == core)
            def _():
                program(core, in_ref, out_ref, send_sems, recv_sems, local_sem, stage_ref)

    def program(c, in_ref, out_ref, send_sems, recv_sems, local_sem, stage_ref):
        x, y, _ = _place()
        chip = lambda px, py: 2 * px + py
        to_x, to_y, sibling = (1 - x, y, c), (x, 1 - y, c), (x, y, 1 - c)
        x_blk, y_blk, d_blk = chip(1 - x, y), chip(x, 1 - y), chip(1 - x, 1 - y)
        mine, theirs = c * half, (1 - c) * half

        def copy(sem, rows, block, to, src=None):
            place = out_ref.at[block, pl.ds(rows[0], rows[1])]
            return pltpu.make_async_remote_copy(
                src_ref=place if src is None else src, dst_ref=place, send_sem=send_sems.at[sem],
                recv_sem=recv_sems.at[sem], device_id=to, device_id_type=MESH_ID)

        stage = pltpu.make_async_copy(in_ref, stage_ref, local_sem.at[0])
        own = pltpu.make_async_copy(stage_ref, out_ref.at[chip(x, y)], local_sem.at[1])
        my_rows = in_ref.at[pl.ds(mine, half)]
        along_x = dict(send=copy(0, (mine, half), chip(x, y), to_x, src=my_rows),
                       landed=copy(0, (mine, half), x_blk, to_x),
                       onward=[copy(3, (mine + quarter, quarter), x_blk, to_y), copy(4, (mine, half), x_blk, sibling)],
                       diagonal=copy(2, (mine, quarter), d_blk, to_x))
        along_y = dict(send=copy(1, (mine, half), chip(x, y), to_y, src=my_rows),
                       landed=copy(1, (mine, half), y_blk, to_y),
                       onward=[copy(2, (mine, quarter), y_blk, to_x), copy(5, (mine, half), y_blk, sibling)],
                       diagonal=copy(3, (mine + quarter, quarter), d_blk, to_y))
        last = copy(6, (mine, half), d_blk, sibling)

        order = (along_x, along_y) if c == 0 else (along_y, along_x)
        for axis in order:
            axis["send"].start()
        stage.start()
        stage.wait()
        own.start()
        for axis in order:
            axis["landed"].wait_recv()
            for cp in axis["onward"]:
                cp.start()
        for axis in order:
            axis["diagonal"].wait_recv()
        last.start()
        for sem, block in ((4, x_blk), (5, y_blk), (6, d_blk)):
            copy(sem, (theirs, half), block, sibling).wait_recv()
        for cp in [along_x["send"], along_y["send"]] + along_x["onward"] + along_y["onward"] + [last]:
            cp.wait_send()
        own.wait()

    return pl.pallas_call(
        body, name=name, in_specs=[HBM_SPEC], out_specs=HBM_SPEC,
        out_shape=jax.ShapeDtypeStruct((N_CHIPS,) + shard.shape, shard.dtype),
        scratch_shapes=[pltpu.SemaphoreType.DMA((7,)), pltpu.SemaphoreType.DMA((7,)), pltpu.SemaphoreType.DMA((2,)),
                        pltpu.VMEM(shard.shape, shard.dtype)],
    )(shard)


def _copies_alone(name, comm):
    return _call(name, lambda: None, grid=(), in_specs=[], out_specs=[], out_shape=[], args=[], comm=comm)[1]


class _Net:
    def __init__(self, shards):
        self.shards = shards
        self.whole, self.own, self.theirs, self.sums, self.other = {}, {}, {}, {}, {}
        x, y, _ = _place()
        self.chip = 2 * x + y

    def gather(self, names):
        return _gather_copies([self.shards[n] for n in names])

    def gathered(self, names, got):
        for n, g in zip(names, got):
            self.whole[n] = g.reshape(-1, g.shape[-1])

    def full(self, name):
        return self.whole[name]

    def exchange(self, grads, swap=()):
        stacked = []
        for n, pieces in grads.items():
            (keep, send), = pieces
            self.own[n] = keep
            stacked.append(send.reshape(N_CHIPS, keep.shape[0] // N_CHIPS, send.shape[-1]))
        return _join(_grad_copies(stacked), self.swap(swap))

    def swap(self, names):
        return _sibling_copies([self.sums[n] for n in names]) if names else None

    def presum_begin(self, name, pieces):
        keep = jnp.concatenate([p[0] for p in pieces], axis=0) if len(pieces) > 1 else pieces[0][0]
        send = jnp.concatenate([p[1] for p in pieces], axis=0) if len(pieces) > 1 else pieces[0][1]
        rows = keep.shape[0] // N_CHIPS
        self.held = keep.reshape(N_CHIPS, rows, keep.shape[-1])
        return _half_rows_copies(send.reshape(N_CHIPS, rows, send.shape[-1]))

    def presum_end(self, name, got):
        x, y, c = _place()
        to_send, self.own[name] = _pre_sum("presum_" + name, self.held, got[0], jnp.stack([c, self.chip]))
        return _grad_copies([to_send])

    def received(self, names, swap, got, carried=None):
        self.theirs.update(zip(names, got[:len(names)]))
        self.other.update(zip(swap, got[len(names):]))
        for n in names:
            (self.sums[n],), more = _partial_sum("sum_" + n, self.own[n], self.theirs[n], self.chip, comm=carried)
        return more


def _half_rows_copies(stacked):
    n, rows = stacked.shape[0], stacked.shape[1] // 2

    def build(ins, outs, send_sems, recv_sems, local_sems):
        x, y, c = _place()
        copies = [pltpu.make_async_remote_copy(
            src_ref=ins[0].at[s, pl.ds((1 - c) * rows, rows)], dst_ref=outs[0].at[s], send_sem=send_sems.at[s],
            recv_sem=recv_sems.at[s], device_id=(x, y, 1 - c), device_id_type=MESH_ID) for s in range(n)]
        return copies, copies, []

    return _Carried([stacked], [jax.ShapeDtypeStruct((n, rows, stacked.shape[2]), stacked.dtype)], n, 0, build)


def _pre_sum(name, held, theirs, core_and_chip):
    n, R, C = held.shape
    half = R // 2
    tr = _row_tile(half)
    per_half = half // tr

    def body(place_ref, h_ref, t_ref, send_ref, own_ref):
        total = h_ref[0] + t_ref[0].astype(F32)
        send_ref[0] = total.astype(send_ref.dtype)

        @pl.when(pl.program_id(1) == place_ref[1])
        def _():
            own_ref[...] = total

    return pl.pallas_call(
        body, name=name,
        grid_spec=pltpu.PrefetchScalarGridSpec(
            num_scalar_prefetch=1, grid=(per_half, n),
            in_specs=[pl.BlockSpec((1, tr, C), lambda i, s, place: (s, place[0] * per_half + i, 0)),
                      pl.BlockSpec((1, tr, C), lambda i, s, place: (s, i, 0))],
            out_specs=[pl.BlockSpec((1, tr, C), lambda i, s, place: (s, i, 0)),
                       pl.BlockSpec((tr, C), lambda i, s, place: (i, 0))]),
        out_shape=[jax.ShapeDtypeStruct((n, half, C), MXU_DTYPE), jax.ShapeDtypeStruct((half, C), F32)],
        compiler_params=_params(("arbitrary", "arbitrary")),
    )(core_and_chip, held, theirs)


def _sibling_copies(parts):
    n = len(parts)

    def build(ins, outs, send_sems, recv_sems, local_sems):
        x, y, c = _place()
        copies = [pltpu.make_async_remote_copy(
            src_ref=ins[w], dst_ref=outs[w], send_sem=send_sems.at[w], recv_sem=recv_sems.at[w],
            device_id=(x, y, 1 - c), device_id_type=MESH_ID) for w in range(n)]
        return copies, copies, []

    return _Carried(parts, [jax.ShapeDtypeStruct(p.shape, p.dtype) for p in parts], n, 0, build)


def _row_tile(rows, most=512, sublanes=16):
    return max(t for t in range(sublanes, min(most, rows // 2) + 1, sublanes) if rows % t == 0)


def _partial_sum(name, own, recv, chip, comm=None):
    _, R, C = recv.shape
    tr = _row_tile(R)

    def body(o_ref, r_ref, p_ref):
        p_ref[...] = ((o_ref[...] + r_ref[0].astype(F32)) + r_ref[1].astype(F32)) + r_ref[2].astype(F32)

    if own.shape[0] != R:
        assert comm is None and own.shape[0] == N_CHIPS * R
        total = pl.pallas_call(
            lambda chip_ref, *refs: body(*refs), name=name,
            grid_spec=pltpu.PrefetchScalarGridSpec(
                num_scalar_prefetch=1, grid=(R // tr,),
                in_specs=[pl.BlockSpec((tr, C), lambda i, chip_ref: (chip_ref[0] * (R // tr) + i, 0)),
                          pl.BlockSpec((3, tr, C), lambda i, chip_ref: (0, i, 0))],
                out_specs=pl.BlockSpec((tr, C), lambda i, chip_ref: (i, 0))),
            out_shape=jax.ShapeDtypeStruct((R, C), F32), compiler_params=_params(("parallel",)),
        )(chip.reshape(1), own, recv)
        return [total], []
    return _call(name, body, grid=(R // tr,),
                 in_specs=[pl.BlockSpec((tr, C), lambda i: (i, 0)), pl.BlockSpec((3, tr, C), lambda i: (0, i, 0))],
                 out_specs=[pl.BlockSpec((tr, C), lambda i: (i, 0))], out_shape=[jax.ShapeDtypeStruct((R, C), F32)],
                 args=[own, recv], semantics=("parallel",), comm=comm)


def _adam_vals(w, g, m, v):
    m = ADAM_B1 * m + (1.0 - ADAM_B1) * g
    v = ADAM_B2 * v + (1.0 - ADAM_B2) * (g * g)
    m_hat = m / (1.0 - ADAM_B1 ** ADAM_STEP)
    v_hat = v / (1.0 - ADAM_B2 ** ADAM_STEP)
    delta = -ADAM_LR * (m_hat / (jnp.sqrt(v_hat) + ADAM_EPS) + ADAM_WD * w)
    return delta, m, v


def _adamw(name, w, m, v, mine, other, comm=None):
    R, C = w.shape
    tr = _row_tile(R)

    def body(w_ref, m_ref, v_ref, s_ref, n_ref, g_ref, d_ref, nm_ref, nv_ref):
        g = s_ref[...] + n_ref[...]
        d, nm, nv = _adam_vals(w_ref[...], g, m_ref[...], v_ref[...])
        g_ref[...], d_ref[...], nm_ref[...], nv_ref[...] = g, d, nm, nv

    spec = pl.BlockSpec((tr, C), lambda i: (i, 0))
    return _call(name, body, grid=(R // tr,), in_specs=[spec] * 5, out_specs=[spec] * 4,
                 out_shape=[jax.ShapeDtypeStruct((R, C), F32)] * 4, args=[w, m, v, mine, other],
                 semantics=("parallel",), comm=comm)


def _adamw_by_halves(name, w, m, v, mine, other, core):
    R, C = w.shape
    tr = _row_tile(R // 2)
    per_half = R // 2 // tr

    def body(c_ref, w_ref, m_ref, v_ref, s_ref, n_ref, g_ref, d_ref, nm_ref, nv_ref):
        g = jnp.where(pl.program_id(0) // per_half == c_ref[0, 0], s_ref[...], n_ref[...])
        d, nm, nv = _adam_vals(w_ref[...], g, m_ref[...], v_ref[...])
        g_ref[...], d_ref[...], nm_ref[...], nv_ref[...] = g, d, nm, nv

    spec = pl.BlockSpec((tr, C), lambda i: (i, 0))
    part = pl.BlockSpec((tr, C), lambda i: (i % per_half, 0))
    return pl.pallas_call(
        body, name=name, grid=(R // tr,),
        in_specs=[pl.BlockSpec(memory_space=pltpu.SMEM), spec, spec, spec, part, part], out_specs=[spec] * 4,
        out_shape=[jax.ShapeDtypeStruct((R, C), F32)] * 4, compiler_params=_params(("parallel",)),
    )(core, w, m, v, mine, other)


SMALL_LAYOUT = dict(norm_mix_g=(0, 1, 1024), b_in=(1, 8, 7424), hgrn_norm_g=(9, 1, 1024), norm_ffn_g=(10, 1, 1024),
                    norm_final_g=(11, 1, 1024), hgrn_lb_logits=(12, 2, 2048), attn_sinks=(14, 1, 16))
SMALL_LOSS_ROW, SMALL_ROWS = 15, 16


def _pack_small(grads, loss):
    rows = [jnp.pad(grads[name].astype(F32).reshape(-1), (0, nrows * D_MODEL - n))
            for name, (_, nrows, n) in SMALL_LAYOUT.items()]
    rows.append(jnp.pad(loss.astype(F32).reshape(1), (0, D_MODEL - 1)))
    return jnp.concatenate(rows).reshape(SMALL_ROWS, D_MODEL)


def _adamw_small(w, m, v, g_all):
    names = list(SMALL_LAYOUT)
    n = len(names)

    def body(a_ref, *refs):
        ins, outs = refs[:3 * n], refs[3 * n:]
        g_all_rows = a_ref[0]
        for dev in range(1, 8):
            g_all_rows = g_all_rows + a_ref[dev]
        for i, name in enumerate(names):
            first, nrows, count = SMALL_LAYOUT[name]
            w_ref, m_ref, v_ref = ins[3 * i:3 * i + 3]
            if w_ref.shape[0] == nrows:
                g = g_all_rows[first:first + nrows, :w_ref.shape[1]]
            else:
                last = count - (nrows - 1) * D_MODEL
                g = jnp.concatenate([g_all_rows[r:r + 1, :] for r in range(first, first + nrows - 1)]
                                    + [g_all_rows[first + nrows - 1:first + nrows, :last]], axis=1)
            d, nm, nv = _adam_vals(w_ref[...], g, m_ref[...], v_ref[...])
            for o_ref, val in zip(outs[4 * i:4 * i + 4], (g, d, nm, nv)):
                o_ref[...] = val
        outs[4 * n][...] = g_all_rows[SMALL_LOSS_ROW:SMALL_LOSS_ROW + 1, 0:1]

    res = pl.pallas_call(
        body, name="adamw_small",
        out_shape=[jax.ShapeDtypeStruct(w[name].shape, F32) for name in names for _ in range(4)]
        + [jax.ShapeDtypeStruct((1, 1), F32)],
    )(g_all, *[t[name] for name in names for t in (w, m, v)])
    return {name: res[4 * i:4 * i + 4] for i, name in enumerate(names)}, res[4 * n]


MATRICES = ("w_in", "w_branch_attn", "w_branch_hgrn", "w_out", "w_ffn_gate", "w_ffn_up", "w_ffn_down")
COLUMN_SHARDED = ("w_in", "w_ffn_gate", "w_ffn_up")
WEIGHTS = ("norm_mix_g", "w_in", "b_in", "attn_sinks", "hgrn_lb_logits", "hgrn_norm_g", "w_branch_attn",
           "w_branch_hgrn", "w_out", "norm_ffn_g", "w_ffn_gate", "w_ffn_up", "w_ffn_down", "norm_final_g")


def kernel(x, norm_mix_g, w_in, b_in, attn_sinks, hgrn_lb_logits, hgrn_norm_g, w_branch_attn, w_branch_hgrn, w_out, norm_ffn_g, w_ffn_gate, w_ffn_up, w_ffn_down, norm_final_g, loss_target, m_norm_mix_g, m_w_in, m_b_in, m_attn_sinks, m_hgrn_lb_logits, m_hgrn_norm_g, m_w_branch_attn, m_w_branch_hgrn, m_w_out, m_norm_ffn_g, m_w_ffn_gate, m_w_ffn_up, m_w_ffn_down, m_norm_final_g, v_norm_mix_g, v_w_in, v_b_in, v_attn_sinks, v_hgrn_lb_logits, v_hgrn_norm_g, v_w_branch_attn, v_w_branch_hgrn, v_w_out, v_norm_ffn_g, v_w_ffn_gate, v_w_ffn_up, v_w_ffn_down, v_norm_final_g):
    given = dict(locals())
    w = {n: given[n] for n in WEIGHTS}
    m = {n: given["m_" + n] for n in WEIGHTS}
    v = {n: given["v_" + n] for n in WEIGHTS}

    block = lambda a, n: jnp.transpose(a[0]) if n in COLUMN_SHARDED else a[0]
    unblock = lambda a, n: (jnp.transpose(a) if n in COLUMN_SHARDED else a)[None]
    net = _Net({n: block(w[n], n).astype(MXU_DTYPE) for n in MATRICES})
    net.gathered(("w_in",), [_gather_by_neighbours("gather_w_in", net.shards["w_in"])])
    vec = dict(norm_mix_g=norm_mix_g, b_in=b_in, attn_sinks=attn_sinks, hgrn_lb_logits=hgrn_lb_logits,
               hgrn_norm_g=hgrn_norm_g, norm_ffn_g=norm_ffn_g, norm_final_g=norm_final_g.reshape(1, D_MODEL))
    loss_part, dx, d_vecs = _local_step(x[0], loss_target[0], vec, net)

    small_all, = net.received(*net.last, carried=_small_copies(_pack_small(d_vecs, loss_part)))
    grads, deltas, new_m, new_v = {}, {}, {}, {}
    for n in ("w_ffn_down", "w_ffn_gate", "w_ffn_up", "w_out", "w_branch_attn", "w_branch_hgrn"):
        res, got = _adamw("adamw_" + n, block(w[n], n), block(m[n], n), block(v[n], n), net.sums[n], net.other[n],
                          comm=net.swap(("w_in",)) if n == "w_ffn_down" else None)
        if n == "w_ffn_down":
            net.other["w_in"], = got
        grads[n], deltas[n], new_m[n], new_v[n] = (unblock(r, n) for r in res)
    n = "w_in"
    res = _adamw_by_halves("adamw_" + n, block(w[n], n), block(m[n], n), block(v[n], n), net.sums[n], net.other[n],
                           _place()[2].reshape(1, 1))
    grads[n], deltas[n], new_m[n], new_v[n] = (unblock(r, n) for r in res)
    rows = lambda t: {n: t[n].reshape(-1, t[n].shape[-1]) for n in SMALL_LAYOUT}
    res, loss = _adamw_small(rows(w), rows(m), rows(v), small_all)
    for n, four in res.items():
        grads[n], deltas[n], new_m[n], new_v[n] = (r.reshape(w[n].shape) for r in four)
    loss = loss.reshape(())
    return (loss, dx[None], *[grads[n] for n in WEIGHTS], *[deltas[n] for n in WEIGHTS],
            *[new_m[n] for n in WEIGHTS], *[new_v[n] for n in WEIGHTS])
```

```python
import collections
import functools
import math

import jax
import jax.numpy as jnp
from jax import lax
from jax.experimental import pallas as pl
from jax.experimental.pallas import tpu as pltpu

F32 = jnp.float32
BF16 = jnp.bfloat16
MXU_DTYPE = jnp.bfloat16
SAVED_DTYPE = jnp.bfloat16
MESH_ID = pl.DeviceIdType.MESH

D_MODEL = 1024
HEAD_DIM = 64
Q_HEADS = 16
KV_HEADS = 2
GROUP = Q_HEADS // KV_HEADS
KV_WIDTH = KV_HEADS * HEAD_DIM
ATTN_BLOCK = 128
HGRN_HEADS = 8
HGRN_K = 128
CHUNK = 64
HGRN_TOKENS = 256
FFN = 2816
IN_SPLITS = (1024, 256, 4096, 2048)
EPS = 1e-6
NEG_INF = -1e30
ADAM_LR, ADAM_B1, ADAM_B2, ADAM_EPS, ADAM_WD, ADAM_STEP = 0.001, 0.9, 0.999, 1e-08, 0.01, 10
N_CHIPS = 4
VMEM_LIMIT = 60 * 1024 * 1024
ROW_ALIGN = 16
DW_ROWS = 256


def _params(sem=None):
    return pltpu.CompilerParams(dimension_semantics=sem, vmem_limit_bytes=VMEM_LIMIT)


def _sigmoid(v):
    return 0.5 * jnp.tanh(0.5 * v) + 0.5


def _dot(a, b, dims):
    return lax.dot_general(a.astype(MXU_DTYPE), b.astype(MXU_DTYPE), (dims, ((), ())),
                           preferred_element_type=F32)


def _nn(a, b):
    return _dot(a, b, ((1,), (0,)))


def _nt(a, b):
    return _dot(a, b, ((1,), (1,)))


def _tn(a, b):
    return _dot(a, b, ((0,), (0,)))


HBM_SPEC = pl.BlockSpec(memory_space=pl.ANY)


class _Carried:
    def __init__(self, arrays, out_shapes, n_remote, n_local, build, stages=()):
        self.parts = [(len(arrays), len(out_shapes), 3 + len(stages), build)]
        self.arrays, self.out_shapes = list(arrays), list(out_shapes)
        self.scratch = [pltpu.SemaphoreType.DMA((n_remote,)), pltpu.SemaphoreType.DMA((n_remote,)),
                        pltpu.SemaphoreType.DMA((max(n_local, 1),))] + list(stages)

    def __add__(self, other):
        both = _Carried([], [], 1, 0, None)
        both.parts = self.parts + other.parts
        both.arrays, both.out_shapes = self.arrays + other.arrays, self.out_shapes + other.out_shapes
        both.scratch = self.scratch + other.scratch
        return both

    def _built(self, ins, outs, sems):
        for ni, no, ns, build in self.parts:
            yield build(ins[:ni], outs[:no], *sems[:ns])
            ins, outs, sems = ins[ni:], outs[no:], sems[ns:]

    def start(self, ins, outs, sems):
        core = lax.axis_index("c")
        for sends, _, local, *other_order in self._built(ins, outs, sems):
            for cp in local:
                (cp[0] if isinstance(cp, tuple) else cp).start()
            if not other_order:
                for cp in sends:
                    cp.start()
                continue

            @pl.when(core == 0)
            def _():
                for cp in sends:
                    cp.start()

            @pl.when(core == 1)
            def _():
                for cp in other_order[0]:
                    cp.start()

    def wait(self, ins, outs, sems):
        for sends, recvs, local, *_ in self._built(ins, outs, sems):
            staged = [cp for cp in local if isinstance(cp, tuple)]
            for into, out_of in staged:
                into.wait()
                out_of.start()
            for cp in recvs:
                cp.wait_recv()
            for cp in sends:
                cp.wait_send()
            for cp in local:
                (cp[1] if isinstance(cp, tuple) else cp).wait()


def _join(*comms):
    comms = [c for c in comms if c is not None]
    return functools.reduce(lambda a, b: a + b, comms) if comms else None


def _call(name, body, *, grid, in_specs, out_specs, out_shape, args, scratch=(), semantics=None, comm=None,
          aliases=None):
    n_in, n_out, n_scr = len(in_specs), len(out_specs), len(scratch)
    aliases = aliases or {}
    if comm is None:
        res = pl.pallas_call(body, name=name, grid=grid, in_specs=in_specs, out_specs=out_specs, out_shape=out_shape,
                             scratch_shapes=list(scratch), input_output_aliases=aliases,
                             compiler_params=_params(semantics))(*args)
        return list(res), []
    ci, co = len(comm.arrays), len(comm.out_shapes)

    def carrying(*refs):
        ins, refs = refs[:n_in], refs[n_in:]
        c_ins, refs = refs[:ci], refs[ci:]
        outs, refs = refs[:n_out], refs[n_out:]
        c_outs, refs = refs[:co], refs[co:]
        scr, sems = refs[:n_scr], refs[n_scr:]
        if not grid:
            comm.start(c_ins, c_outs, sems)
            body(*ins, *outs, *scr)
            comm.wait(c_ins, c_outs, sems)
            return
        first = functools.reduce(jnp.logical_and, [pl.program_id(a) == 0 for a in range(len(grid))])
        last = functools.reduce(jnp.logical_and, [pl.program_id(a) == g - 1 for a, g in enumerate(grid)])

        @pl.when(first)
        def _():
            comm.start(c_ins, c_outs, sems)

        body(*ins, *outs, *scr)

        @pl.when(last)
        def _():
            comm.wait(c_ins, c_outs, sems)

    res = pl.pallas_call(
        carrying, name=name, grid=grid, in_specs=list(in_specs) + [HBM_SPEC] * ci,
        out_specs=list(out_specs) + [HBM_SPEC] * co, out_shape=list(out_shape) + comm.out_shapes,
        scratch_shapes=list(scratch) + comm.scratch, input_output_aliases=aliases,
        compiler_params=_params(("arbitrary",) * len(grid) if grid else None),
    )(*args, *comm.arrays)
    return list(res[:n_out]), list(res[n_out:])


_Cols = collections.namedtuple("_Cols", "array first cols")
_Into = collections.namedtuple("_Into", "rows first held")


def _weight_grad(name, a, b, *, tm, tk, a_colsum=False, into=None, carrying=False, comm=None):
    cols = a if isinstance(a, _Cols) else _Cols(a, 0, a.shape[1])
    a = cols.array
    (T, N), M = b.shape, cols.cols
    tk = min(tk, T)
    assert cols.first % tm == 0 and M % tm == 0 and T % tk == 0, (name, cols.first, M, tm, T, tk)
    ni, nk, tile0 = M // tm, T // tk, cols.first // tm
    held = list(into.held) if into is not None and into.held is not None else []

    def body(a_ref, b_ref, *rest):
        keep_ref, send_ref = rest[len(held):len(held) + 2]
        sums_ref = rest[len(held) + 2] if a_colsum else None
        if nk == 1:
            acc = _tn(a_ref[...], b_ref[...])
            keep_ref[...], send_ref[...] = acc, acc.astype(send_ref.dtype)
            if a_colsum:
                sums_ref[...] = jnp.sum(a_ref[...].astype(F32), axis=0, keepdims=True)
            return
        acc_ref = rest[-1]
        k = pl.program_id(1)

        @pl.when(k == 0)
        def _():
            acc_ref[...] = jnp.zeros_like(acc_ref)
            if a_colsum:
                sums_ref[...] = jnp.zeros((1, tm), F32)

        if a_colsum:
            sums_ref[...] += jnp.sum(a_ref[...].astype(F32), axis=0, keepdims=True)
        acc_ref[...] += _tn(a_ref[...], b_ref[...])

        @pl.when(k == nk - 1)
        def _():
            keep_ref[...], send_ref[...] = acc_ref[...], acc_ref[...].astype(send_ref.dtype)

    if into is None:
        rows, out_spec = M, pl.BlockSpec((tm, N), lambda i, k: (i, 0))
    else:
        rows = into.rows
        out_spec = pl.BlockSpec((pl.Element(tm), pl.Element(N)),
                                lambda i, k: (pl.multiple_of(into.first + i * tm, ROW_ALIGN), 0))
    out_shape = [jax.ShapeDtypeStruct((rows, N), F32), jax.ShapeDtypeStruct((rows, N), MXU_DTYPE)]
    out_specs = [out_spec, out_spec]
    if a_colsum:
        out_shape.append(jax.ShapeDtypeStruct((1, M), F32))
        out_specs.append(pl.BlockSpec((1, tm), lambda i, k: (0, i)))
    res, got = _call(
        name, body, grid=(ni, nk),
        in_specs=[pl.BlockSpec((tk, tm), lambda i, k: (k, tile0 + i)), pl.BlockSpec((tk, N), lambda i, k: (k, 0))]
        + [HBM_SPEC] * len(held),
        out_specs=out_specs, out_shape=out_shape, scratch=[pltpu.VMEM((tm, N), F32)] if nk > 1 else [],
        args=[a, b] + held, aliases={2 + p: p for p in range(len(held))}, semantics=("parallel", "arbitrary"),
        comm=comm)
    return (res, got) if carrying else res


def _ffn_fwd(merged, w_out, x, gain, w_gate_t, w_up_t, *, tm, comm=None):
    (T, D), F = x.shape, w_gate_t.shape[0]

    def body(m_ref, wo_ref, x_ref, g_ref, wg_ref, wu_ref, h_ref, u_ref, gate_ref, up_ref, z_ref):
        h = x_ref[...] + _nn(m_ref[...], wo_ref[...])
        h_ref[...] = h
        u = (h * lax.rsqrt(jnp.mean(h * h, axis=-1, keepdims=True) + EPS) * g_ref[...]).astype(u_ref.dtype)
        u_ref[...] = u
        gate, up = _nt(u, wg_ref[...]), _nt(u, wu_ref[...])
        gate_ref[...], up_ref[...] = gate.astype(gate_ref.dtype), up.astype(up_ref.dtype)
        z_ref[...] = (gate * _sigmoid(gate) * up).astype(z_ref.dtype)

    rows = lambda n: pl.BlockSpec((tm, n), lambda i: (i, 0))
    fixed = _fixed_spec
    return _call("ffn_hidden", body, grid=(T // tm,),
                 in_specs=[rows(D), fixed(w_out), rows(D), fixed(gain), fixed(w_gate_t), fixed(w_up_t)],
                 out_specs=[rows(D), rows(D), rows(F), rows(F), rows(F)],
                 out_shape=[jax.ShapeDtypeStruct((T, D), F32), jax.ShapeDtypeStruct((T, D), MXU_DTYPE)]
                 + [jax.ShapeDtypeStruct((T, F), SAVED_DTYPE)] * 2 + [jax.ShapeDtypeStruct((T, F), MXU_DTYPE)],
                 args=[merged, w_out, x, gain, w_gate_t, w_up_t], semantics=("parallel",), comm=comm)


def _in_proj(x, gain, w_in_t, b_in, *, tm, comm=None):
    T, D = x.shape
    bounds = [sum(IN_SPLITS[:i]) for i in range(len(IN_SPLITS) + 1)]

    def body(x_ref, g_ref, w_ref, b_ref, u_ref, *piece_refs):
        xv = x_ref[...]
        r = lax.rsqrt(jnp.mean(xv * xv, axis=-1, keepdims=True) + EPS)
        u = (xv * r * g_ref[...]).astype(u_ref.dtype)
        u_ref[...] = u
        for o_ref, lo, hi in zip(piece_refs, bounds[:-1], bounds[1:]):
            o_ref[...] = (_nt(u, w_ref[lo:hi, :]) + b_ref[:, lo:hi]).astype(o_ref.dtype)

    rows = lambda n: pl.BlockSpec((tm, n), lambda i: (i, 0))
    fixed = _fixed_spec
    dtypes = (MXU_DTYPE, MXU_DTYPE, F32, F32)
    return _call("in_proj", body, grid=(T // tm,),
                 in_specs=[rows(D), fixed(gain), fixed(w_in_t), fixed(b_in)],
                 out_specs=[rows(D)] + [rows(n) for n in IN_SPLITS],
                 out_shape=[jax.ShapeDtypeStruct((T, D), MXU_DTYPE)]
                 + [jax.ShapeDtypeStruct((T, n), dt) for n, dt in zip(IN_SPLITS, dtypes)],
                 args=[x, gain, w_in_t, b_in], semantics=("parallel",), comm=comm)


def _row_spec(tm, n):
    return pl.BlockSpec((tm, n), lambda i: (i, 0))


def _fixed_spec(a):
    return pl.BlockSpec(a.shape, lambda i: (0,) * a.ndim, pipeline_mode=pl.Buffered(1))


def _partials_spec(n):
    return pl.BlockSpec((8, n), lambda i: (i, 0))


def _row_parts(tm, parts):
    assert tm % parts == 0, (tm, parts)
    return [slice(p * (tm // parts), (p + 1) * (tm // parts)) for p in range(parts)]


def _ffn_tail(z, w_down, h1, target, gain, gate, up, *, tm, parts=2):
    (T, F), D = z.shape, h1.shape[1]

    def body(z_ref, w_ref, h_ref, t_ref, g_ref, gate_ref, up_ref, dh_ref, dhb_ref, dgate_ref, dup_ref, dg_ref, l_ref):
        part, dgain = 0.0, 0.0
        pieces = _row_parts(tm, parts)
        h2s = [h_ref[rows, :] + _nn(z_ref[rows, :], w_ref[...]) for rows in pieces]
        for rows, h2 in zip(pieces, h2s):
            r = lax.rsqrt(jnp.mean(h2 * h2, axis=-1, keepdims=True) + EPS)
            xhat = h2 * r
            err = xhat * g_ref[...] - t_ref[rows, :]
            part += 0.5 * jnp.sum(jnp.sum(err * err, axis=-1, keepdims=True), axis=0, keepdims=True) / D
            dy = err / D
            dxh = dy * g_ref[...]
            dh2 = r * (dxh - xhat * jnp.mean(dxh * xhat, axis=-1, keepdims=True))
            dh_ref[rows, :] = dh2
            dhb = dh2.astype(dhb_ref.dtype)
            dhb_ref[rows, :] = dhb
            dgain += jnp.sum(dy * xhat, axis=0, keepdims=True)
            dz = _nt(dhb, w_ref[...])
            gv, upv = gate_ref[rows, :].astype(F32), up_ref[rows, :].astype(F32)
            s = _sigmoid(gv)
            dgate_ref[rows, :] = (dz * upv * (s * (1.0 + gv * (1.0 - s)))).astype(dgate_ref.dtype)
            dup_ref[rows, :] = (dz * (gv * s)).astype(dup_ref.dtype)
        dg_ref[...] = jnp.broadcast_to(dgain, dg_ref.shape)
        l_ref[...] = jnp.broadcast_to(part, l_ref.shape)

    low = lambda n: jax.ShapeDtypeStruct((T, n), MXU_DTYPE)
    part = jax.ShapeDtypeStruct((8 * (T // tm), D), F32)
    return pl.pallas_call(
        body, name="ffn_tail", grid=(T // tm,),
        in_specs=[_row_spec(tm, F), _fixed_spec(w_down), _row_spec(tm, D), _row_spec(tm, D), _fixed_spec(gain),
                  _row_spec(tm, F), _row_spec(tm, F)],
        out_specs=[_row_spec(tm, D), _row_spec(tm, D), _row_spec(tm, F), _row_spec(tm, F), _partials_spec(D),
                   _partials_spec(D)],
        out_shape=[jax.ShapeDtypeStruct((T, D), F32), low(D), low(F), low(F), part, part],
        compiler_params=_params(("parallel",)),
    )(z, w_down, h1, target, gain, gate, up)


def _ffn_in_bwd(dgate, dup, w_gate_t, w_up_t, h1, gain, dres, *, tm, comm=None):
    (T, F), D = dgate.shape, h1.shape[1]

    def body(dg_ref, du_ref, wg_ref, wu_ref, h_ref, g_ref, r_ref, dh_ref, dhb_ref, dgain_ref):
        d_u2 = _nn(dg_ref[...], wg_ref[...]) + _nn(du_ref[...], wu_ref[...])
        dx, dgain = _rmsnorm_bwd_vals(d_u2, h_ref[...], g_ref[...])
        dh = r_ref[...] + dx
        dh_ref[...] = dh
        dhb_ref[...] = dh.astype(dhb_ref.dtype)
        dgain_ref[...] = jnp.broadcast_to(dgain, dgain_ref.shape)

    return _call("d_ffn_in", body, grid=(T // tm,),
                 in_specs=[_row_spec(tm, F), _row_spec(tm, F), _fixed_spec(w_gate_t), _fixed_spec(w_up_t),
                           _row_spec(tm, D), _fixed_spec(gain), _row_spec(tm, D)],
                 out_specs=[_row_spec(tm, D), _row_spec(tm, D), _partials_spec(D)],
                 out_shape=[jax.ShapeDtypeStruct((T, D), F32), jax.ShapeDtypeStruct((T, D), MXU_DTYPE),
                            jax.ShapeDtypeStruct((8 * (T // tm), D), F32)],
                 args=[dgate, dup, w_gate_t, w_up_t, h1, gain, dres], semantics=("parallel",), comm=comm)


def _in_proj_bwd(pieces, w_in_t, x, gain, dres, *, tm, comm=None):
    T, D = x.shape
    n = len(pieces)

    def body(*refs):
        dps, (w_ref, x_ref, g_ref, r_ref, dx_ref, dgain_ref) = refs[:n], refs[n:]
        d_u = None
        for dp_ref, (dp, first) in zip(dps, pieces):
            term = _nn(dp_ref[...], w_ref[first:first + dp.shape[1], :])
            d_u = term if d_u is None else d_u + term
        dx, dgain = _rmsnorm_bwd_vals(d_u, x_ref[...], g_ref[...])
        dx_ref[...] = r_ref[...] + dx
        dgain_ref[...] = jnp.broadcast_to(dgain, dgain_ref.shape)

    return _call("d_u", body, grid=(T // tm,),
                 in_specs=[_row_spec(tm, dp.shape[1]) for dp, _ in pieces]
                 + [_fixed_spec(w_in_t), _row_spec(tm, D), _fixed_spec(gain), _row_spec(tm, D)],
                 out_specs=[_row_spec(tm, D), _partials_spec(D)],
                 out_shape=[jax.ShapeDtypeStruct((T, D), F32), jax.ShapeDtypeStruct((8 * (T // tm), D), F32)],
                 args=[dp for dp, _ in pieces] + [w_in_t, x, gain, dres], semantics=("parallel",), comm=comm)


def _merge_fwd(y_a, y_b, w_a, w_b, gates, *, tm, comm=None):
    T, D = y_a.shape

    def body(ya_ref, yb_ref, wa_ref, wb_ref, ga_ref, gb_ref, pa_ref, pb_ref, m_ref):
        pa, pb = _nn(ya_ref[...], wa_ref[...]), _nn(yb_ref[...], wb_ref[...])
        pa_ref[...], pb_ref[...] = pa.astype(pa_ref.dtype), pb.astype(pb_ref.dtype)
        m_ref[...] = (_sigmoid(ga_ref[...]) * pa + _sigmoid(gb_ref[...]) * pb).astype(m_ref.dtype)

    rows = pl.BlockSpec((tm, D), lambda i: (i, 0))
    whole = pl.BlockSpec((D, D), lambda i: (0, 0), pipeline_mode=pl.Buffered(1))
    return _call("branch_merge", body, grid=(T // tm,),
                 in_specs=[rows, rows, whole, whole, rows, pl.BlockSpec((tm, D), lambda i: (i, 1))],
                 out_specs=[rows] * 3,
                 out_shape=[jax.ShapeDtypeStruct((T, D), SAVED_DTYPE)] * 2 + [jax.ShapeDtypeStruct((T, D), MXU_DTYPE)],
                 args=[y_a, y_b, w_a, w_b, gates, gates], semantics=("parallel",), comm=comm)


def _merge_bwd(dh, w_out, w_a, w_b, p_a, p_b, gates, *, tm, d_in_width, first):
    T, D = dh.shape

    def body(dh_ref, wo_ref, wa_ref, wb_ref, pa_ref, pb_ref, ga_ref, gb_ref, dpa_ref, dpb_ref, dya_ref, dyb_ref,
             din_ref):
        dm = _nt(dh_ref[...], wo_ref[...])
        sa, sb = _sigmoid(ga_ref[...]), _sigmoid(gb_ref[...])
        dpa, dpb = (dm * sa).astype(dpa_ref.dtype), (dm * sb).astype(dpb_ref.dtype)
        dpa_ref[...], dpb_ref[...] = dpa, dpb
        din_ref[:, :D] = (dm * pa_ref[...].astype(F32) * sa * (1.0 - sa)).astype(din_ref.dtype)
        din_ref[:, D:] = (dm * pb_ref[...].astype(F32) * sb * (1.0 - sb)).astype(din_ref.dtype)
        dya_ref[...] = _nt(dpa, wa_ref[...]).astype(dya_ref.dtype)
        dyb_ref[...] = _nt(dpb, wb_ref[...]).astype(dyb_ref.dtype)

    rows = pl.BlockSpec((tm, D), lambda i: (i, 0))
    whole = pl.BlockSpec((D, D), lambda i: (0, 0), pipeline_mode=pl.Buffered(1))
    low = jax.ShapeDtypeStruct((T, D), MXU_DTYPE)
    return pl.pallas_call(
        body, name="d_branch_merge", grid=(T // tm,),
        in_specs=[rows, whole, whole, whole, rows, rows, rows, pl.BlockSpec((tm, D), lambda i: (i, 1))],
        out_specs=[rows] * 4 + [pl.BlockSpec((pl.Element(tm), pl.Element(2 * D)), lambda i: (
            pl.multiple_of(i * tm, ROW_ALIGN), first))],
        out_shape=[low] * 3 + [jax.ShapeDtypeStruct((T, D), F32), jax.ShapeDtypeStruct((T, d_in_width), MXU_DTYPE)],
        compiler_params=_params(("parallel",)),
    )(dh, w_out, w_a, w_b, p_a, p_b, gates, gates)


def _colsum_partials(p):
    return jnp.sum(p.reshape(-1, 8, p.shape[-1])[:, 0, :], axis=0, keepdims=True)


def _rmsnorm_bwd_vals(dy, xin, g):
    rstd = lax.rsqrt(jnp.mean(xin * xin, axis=-1, keepdims=True) + EPS)
    xhat = xin * rstd
    dg = jnp.sum(dy * xhat, axis=0, keepdims=True)
    dxh = dy * g
    dx = rstd * (dxh - xhat * jnp.mean(dxh * xhat, axis=-1, keepdims=True))
    return dx, dg


ATTN_SCALE = 1.0 / math.sqrt(HEAD_DIM)
GROUP_LANES = GROUP * ATTN_BLOCK
PAIR = 2 * HEAD_DIM


def _attn_mask():
    kj = lax.broadcasted_iota(jnp.int32, (ATTN_BLOCK, GROUP_LANES), 0)
    qi = lax.broadcasted_iota(jnp.int32, (ATTN_BLOCK, GROUP_LANES), 1) & (ATTN_BLOCK - 1)
    return kj <= qi


def _heads_transposed(ref, g, scale=None):
    parts = []
    for a in range(GROUP // 2):
        lo = (g * GROUP // 2 + a) * PAIR
        pair = ref[:, lo:lo + PAIR].astype(F32)
        pair = (pair if scale is None else pair * scale).T
        parts += [pair[:HEAD_DIM], pair[HEAD_DIM:]]
    return jnp.concatenate(parts, axis=1).astype(MXU_DTYPE)


def _heads_back(ref, g, vt):
    for a in range(GROUP // 2):
        lo = (g * GROUP // 2 + a) * PAIR
        pair = jnp.concatenate([vt[:, (2 * a) * ATTN_BLOCK:(2 * a + 1) * ATTN_BLOCK],
                                vt[:, (2 * a + 1) * ATTN_BLOCK:(2 * a + 2) * ATTN_BLOCK]], axis=0)
        ref[:, lo:lo + PAIR] = pair.T.astype(ref.dtype)


def _kv_parts(kv_ref, g):
    ks = slice(g * HEAD_DIM, (g + 1) * HEAD_DIM)
    vs = slice(KV_WIDTH + g * HEAD_DIM, KV_WIDTH + (g + 1) * HEAD_DIM)
    return kv_ref[:, ks].astype(MXU_DTYPE), kv_ref[:, vs].astype(MXU_DTYPE)


def _sink_rows(sinks):
    return jnp.repeat(sinks.reshape(KV_HEADS, GROUP), ATTN_BLOCK, axis=1)


def _attn_fwd(pq, pkv, sinks, comm=None):
    T = pq.shape[0]
    nb = T // ATTN_BLOCK

    def body(q_ref, kvc_ref, kvp_ref, s_ref, y_ref, lse_ref):
        mask_c = _attn_mask()
        has_prev = pl.program_id(0) > 0
        for g in range(KV_HEADS):
            (kc, vc), (kp, vp) = _kv_parts(kvc_ref, g), _kv_parts(kvp_ref, g)
            qt = _heads_transposed(q_ref, g, ATTN_SCALE)
            s = jnp.where(mask_c, _nn(kc, qt), jnp.where(has_prev, _nn(kp, qt), NEG_INF))
            sink = s_ref[g:g + 1, :]
            m = jnp.maximum(jnp.max(s, axis=0, keepdims=True), sink)
            p = jnp.exp(s - m)
            den = jnp.sum(p, axis=0, keepdims=True) + jnp.exp(sink - m)
            pc = jnp.where(mask_c, p, 0.0)
            _heads_back(y_ref, g, (_tn(vc, pc) + _tn(vp, p - pc)) / den)
            lse = m + jnp.log(den)
            for i in range(GROUP):
                lse_ref[g * GROUP + i:g * GROUP + i + 1, :] = lse[:, i * ATTN_BLOCK:(i + 1) * ATTN_BLOCK]

    return _call(
        "attn_fwd", body, grid=(nb,),
        in_specs=[pl.BlockSpec((ATTN_BLOCK, D_MODEL), lambda n: (n, 0)),
                  pl.BlockSpec((ATTN_BLOCK, 2 * KV_WIDTH), lambda n: (n, 0)),
                  pl.BlockSpec((ATTN_BLOCK, 2 * KV_WIDTH), lambda n: (jnp.maximum(n - 1, 0), 0)),
                  pl.BlockSpec((KV_HEADS, GROUP_LANES), lambda n: (0, 0))],
        out_specs=[pl.BlockSpec((ATTN_BLOCK, D_MODEL), lambda n: (n, 0)),
                   pl.BlockSpec((Q_HEADS, ATTN_BLOCK), lambda n: (0, n))],
        out_shape=[jax.ShapeDtypeStruct((T, D_MODEL), MXU_DTYPE), jax.ShapeDtypeStruct((Q_HEADS, T), F32)],
        args=[pq, pkv, pkv, _sink_rows(sinks)], semantics=("parallel",), comm=comm)


def _attn_bwd(pq, pkv, sinks, lse, dy, d_in, comm=None):
    T = pq.shape[0]
    nb = T // ATTN_BLOCK
    cur = lambda n: (jnp.minimum(n, nb - 1), 0)
    done = D_MODEL + 2 * KV_WIDTH

    def body(q_ref, kvc_ref, kvp_ref, s_ref, lse_ref, dy_ref, _, out_ref, ds_ref, carry, top, bot, dq_ref):
        n = pl.program_id(0)

        @pl.when(n == 0)
        def _():
            carry[...] = jnp.zeros_like(carry)
            dq_ref[...] = jnp.zeros_like(dq_ref)
            ds_ref[...] = jnp.zeros_like(ds_ref)

        out_ref[:, :D_MODEL] = dq_ref[...]

        @pl.when(n < nb)
        def _():
            mask_c = _attn_mask()
            valid = jnp.logical_or(mask_c, n > 0)
            for g in range(KV_HEADS):
                ks = slice(g * HEAD_DIM, (g + 1) * HEAD_DIM)
                vs = slice(KV_WIDTH + g * HEAD_DIM, KV_WIDTH + (g + 1) * HEAD_DIM)
                (kc, vc), (kp, vp) = _kv_parts(kvc_ref, g), _kv_parts(kvp_ref, g)
                qt = _heads_transposed(q_ref, g, ATTN_SCALE)
                dot = _heads_transposed(dy_ref, g)
                lse = jnp.concatenate([lse_ref[g * GROUP + i:g * GROUP + i + 1, :] for i in range(GROUP)], axis=1)
                p = jnp.where(valid, jnp.exp(jnp.where(mask_c, _nn(kc, qt), _nn(kp, qt)) - lse), 0.0)
                dp = jnp.where(mask_c, _nn(vc, dot), _nn(vp, dot))
                delta = jnp.sum(p * dp, axis=0, keepdims=True)
                ds = p * (dp - delta)
                ds_c, p_c = jnp.where(mask_c, ds, 0.0), jnp.where(mask_c, p, 0.0)
                ds_p, p_p = ds - ds_c, p - p_c
                _heads_back(dq_ref, g, (_tn(kc, ds_c) + _tn(kp, ds_p)) * ATTN_SCALE)
                bot[:, ks], bot[:, vs] = _nt(ds_c, qt), _nt(p_c, dot)
                top[:, ks], top[:, vs] = _nt(ds_p, qt), _nt(p_p, dot)
                ds_ref[g:g + 1, :] -= jnp.exp(s_ref[g:g + 1, :] - lse) * delta
            out_ref[:, D_MODEL:] = (carry[...] + top[...]).astype(out_ref.dtype)
            carry[...] = bot[...]

        @pl.when(n == nb)
        def _():
            out_ref[:, D_MODEL:] = carry[...].astype(out_ref.dtype)

    return _call(
        "attn_bwd", body, grid=(nb + 1,),
        in_specs=[pl.BlockSpec((ATTN_BLOCK, D_MODEL), cur),
                  pl.BlockSpec((ATTN_BLOCK, 2 * KV_WIDTH), cur),
                  pl.BlockSpec((ATTN_BLOCK, 2 * KV_WIDTH), lambda n: (jnp.maximum(jnp.minimum(n, nb - 1) - 1, 0), 0)),
                  pl.BlockSpec((KV_HEADS, GROUP_LANES), lambda n: (0, 0)),
                  pl.BlockSpec((Q_HEADS, ATTN_BLOCK), lambda n: (0, jnp.minimum(n, nb - 1))),
                  pl.BlockSpec((ATTN_BLOCK, D_MODEL), cur), HBM_SPEC],
        out_specs=[pl.BlockSpec((ATTN_BLOCK, done), lambda n: (jnp.maximum(n - 1, 0), 0)),
                   pl.BlockSpec((KV_HEADS, GROUP_LANES), lambda n: (0, 0))],
        out_shape=[jax.ShapeDtypeStruct(d_in.shape, d_in.dtype), jax.ShapeDtypeStruct((KV_HEADS, GROUP_LANES), F32)],
        scratch=[pltpu.VMEM((ATTN_BLOCK, 2 * KV_WIDTH), F32)] * 3 + [pltpu.VMEM((ATTN_BLOCK, D_MODEL), MXU_DTYPE)],
        args=[pq, pkv, pkv, _sink_rows(sinks), lse, dy, d_in], semantics=("arbitrary",), comm=comm, aliases={6: 0})


def _lower_bound(l):
    m = jnp.maximum(l[0:1], l[1:2])
    e0, e1 = jnp.exp(l[0:1] - m), jnp.exp(l[1:2] - m)
    return e0 / (e0 + e1)


def _tri(lower):
    r = lax.broadcasted_iota(jnp.int32, (CHUNK, CHUNK), 0)
    c = lax.broadcasted_iota(jnp.int32, (CHUNK, CHUNK), 1)
    return (r >= c) if lower else (c >= r)


def _chunk_sum(mask, v):
    ones = mask.astype(BF16)
    hi = v.astype(BF16)
    rest = v - hi.astype(F32)
    mid = rest.astype(BF16)
    lo = (rest - mid.astype(F32)).astype(BF16)
    part = lambda t: lax.dot_general(ones, t, (((1,), (0,)), ((), ())), preferred_element_type=F32)
    return part(hi) + part(mid) + part(lo)


def _hgrn_chunk_inputs(hq, hf, lb, causal):
    half_t = 0.5 * jnp.tanh(0.5 * hf)
    sg, sgn = 0.5 + half_t, 0.5 - half_t
    f = lb + (1.0 - lb) * sg
    kk = (1.0 - lb) * sgn
    sq = _sigmoid(hq)
    q = hq * sq
    b = _chunk_sum(causal, jnp.log(f))
    bm, bl = b[CHUNK // 2 - 1:CHUNK // 2, :], b[CHUNK - 1:CHUNK, :]
    e_qm, e_km = jnp.exp(b - bm), jnp.exp(bm - b)
    e_qs, e_kl = e_qm * jnp.exp(bm), e_km * jnp.exp(bl - bm)
    return dict(sg=sg, sgn=sgn, f=f, kk=kk, sq=sq, q=q, e_qm=e_qm, e_km=e_km, e_qs=e_qs, e_kl=e_kl,
                qm=q * e_qm, km=kk * e_km, qs=q * e_qs, kl=kk * e_kl, el=jnp.exp(bl))


def _hgrn_fwd(ph, lb_logits, norm_g, comm=None):
    T = ph.shape[0]
    nblk, cpb = T // HGRN_TOKENS, HGRN_TOKENS // CHUNK
    col = lambda c: pl.BlockSpec((HGRN_TOKENS, D_MODEL), functools.partial(lambda i, c: (i, c), c=c))

    def body(hq_ref, hf_ref, hi_ref, hg_ref, l_ref, ng_ref, y_ref, o_ref, st_ref, s_ref):
        @pl.when(pl.program_id(0) == 0)
        def _():
            s_ref[...] = jnp.zeros_like(s_ref)

        lb = _lower_bound(l_ref[...])
        causal = _tri(True)
        for c in range(cpb):
            rows = slice(c * CHUNK, (c + 1) * CHUNK)
            t = _hgrn_chunk_inputs(hq_ref[rows, :], hf_ref[rows, :], lb, causal)
            qm, km, qs, kl = (t[n].astype(MXU_DTYPE) for n in ("qm", "km", "qs", "kl"))
            v = hi_ref[rows, :].astype(MXU_DTYPE)
            heads = [slice(h * HGRN_K, (h + 1) * HGRN_K) for h in range(HGRN_HEADS)]
            a_all = [jnp.where(causal, _nt(qm[:, ls], km[:, ls]), 0.0).astype(MXU_DTYPE) for ls in heads]
            for h, ls in enumerate(heads):
                st = s_ref[h]
                st_ref[c, ls, :] = st
                o_ref[rows, ls] = _nn(a_all[h], v[:, ls]) + _nt(qs[:, ls], st)
                s_ref[h] = t["el"][:, ls] * st + _tn(v[:, ls], kl[:, ls])
        for h in range(HGRN_HEADS):
            ls = slice(h * HGRN_K, (h + 1) * HGRN_K)
            o = o_ref[:, ls]
            r = lax.rsqrt(jnp.mean(o * o, axis=-1, keepdims=True) + EPS)
            y_ref[:, ls] = (o * r * ng_ref[:, ls] * _sigmoid(hg_ref[:, ls])).astype(y_ref.dtype)

    return _call(
        "hgrn_fwd", body, grid=(nblk,),
        in_specs=[col(0), col(1), col(2), col(3),
                  pl.BlockSpec((2, D_MODEL), lambda i: (0, 0)), pl.BlockSpec((1, D_MODEL), lambda i: (0, 0))],
        out_specs=[pl.BlockSpec((HGRN_TOKENS, D_MODEL), lambda i: (i, 0)),
                   pl.BlockSpec((HGRN_TOKENS, D_MODEL), lambda i: (i, 0)),
                   pl.BlockSpec((cpb, D_MODEL, HGRN_K), lambda i: (i, 0, 0))],
        out_shape=[jax.ShapeDtypeStruct((T, D_MODEL), MXU_DTYPE), jax.ShapeDtypeStruct((T, D_MODEL), F32),
                   jax.ShapeDtypeStruct((T // CHUNK, D_MODEL, HGRN_K), F32)],
        scratch=[pltpu.VMEM((HGRN_HEADS, HGRN_K, HGRN_K), F32)],
        args=[ph, ph, ph, ph, lb_logits, norm_g], semantics=("arbitrary",), comm=comm)


def _hgrn_bwd(ph, o_raw, states, dy, lb_logits, norm_g, d_in, first, comm=None):
    T = ph.shape[0]
    nblk, cpb = T // HGRN_TOKENS, HGRN_TOKENS // CHUNK
    rev = lambda i: nblk - 1 - i
    col = lambda c: pl.BlockSpec((HGRN_TOKENS, D_MODEL), functools.partial(lambda i, c: (rev(i), c), c=c))
    tok = pl.BlockSpec((HGRN_TOKENS, D_MODEL), lambda i: (rev(i), 0))

    def body(hq_ref, hf_ref, hi_ref, hg_ref, o_ref, st_ref, dy_ref, l_ref, ng_ref, _,
             dph_ref, dng_ref, dl_ref, dst_ref, dlb_ref, do_s, dqm_s, dkm_s, dqs_s, dkl_s, dv_s, del_s):
        i = pl.program_id(0)

        @pl.when(i == 0)
        def _():
            dst_ref[...] = jnp.zeros_like(dst_ref)
            dlb_ref[...] = jnp.zeros_like(dlb_ref)
            dng_ref[...] = jnp.zeros_like(dng_ref)

        lb = _lower_bound(l_ref[...])
        causal, anti = _tri(True), _tri(False)
        row = lax.broadcasted_iota(jnp.int32, (CHUNK, D_MODEL), 0)
        for c in reversed(range(cpb)):
            rows = slice(c * CHUNK, (c + 1) * CHUNK)
            hq = hq_ref[rows, :]
            t = _hgrn_chunk_inputs(hq, hf_ref[rows, :], lb, causal)
            sgg = _sigmoid(hg_ref[rows, :])
            dyv = dy_ref[rows, :]
            for h in range(HGRN_HEADS):
                ls = slice(h * HGRN_K, (h + 1) * HGRN_K)
                o = o_ref[rows, ls]
                r = lax.rsqrt(jnp.mean(o * o, axis=-1, keepdims=True) + EPS)
                nrm = o * r
                g_h = sgg[:, ls]
                dph_ref[rows, 3 * D_MODEL + h * HGRN_K:3 * D_MODEL + (h + 1) * HGRN_K] = (
                    dyv[:, ls] * nrm * ng_ref[:, ls] * g_h * (1.0 - g_h)).astype(dph_ref.dtype)
                dyg = dyv[:, ls] * g_h
                dng_ref[:, ls] += jnp.sum(dyg * nrm, axis=0, keepdims=True)
                dn = dyg * ng_ref[:, ls]
                do_s[:, ls] = r * (dn - nrm * jnp.mean(dn * nrm, axis=-1, keepdims=True))
            qm, km, qs, kl = (t[n].astype(MXU_DTYPE) for n in ("qm", "km", "qs", "kl"))
            v = hi_ref[rows, :].astype(MXU_DTYPE)
            do = do_s[...].astype(MXU_DTYPE)
            heads = [slice(h * HGRN_K, (h + 1) * HGRN_K) for h in range(HGRN_HEADS)]
            a_all = [jnp.where(causal, _nt(qm[:, ls], km[:, ls]), 0.0).astype(MXU_DTYPE) for ls in heads]
            da_all = [jnp.where(causal, _nt(do[:, ls], v[:, ls]), 0.0).astype(MXU_DTYPE) for ls in heads]
            for h, ls in enumerate(heads):
                st = st_ref[c, ls, :]
                dst = dst_ref[h]
                a, da = a_all[h], da_all[h]
                dv_s[:, ls] = _tn(a, do[:, ls]) + _nt(kl[:, ls], dst)
                dkl_s[:, ls] = _nn(v[:, ls], dst)
                dqs_s[:, ls] = _nn(do[:, ls], st)
                del_s[:, ls] = jnp.sum(dst * st, axis=0, keepdims=True)
                dst_ref[h] = _tn(do[:, ls], qs[:, ls]) + t["el"][:, ls] * dst
                dqm_s[:, ls] = _nn(da, km[:, ls])
                dkm_s[:, ls] = _tn(da, qm[:, ls])
            dqm, dkm, dqs, dkl = dqm_s[...], dkm_s[...], dqs_s[...], dkl_s[...]
            dq = dqm * t["e_qm"] + dqs * t["e_qs"]
            dk = dkm * t["e_km"] + dkl * t["e_kl"]
            t_qm, t_km, t_kl = dqm * t["qm"], dkm * t["km"], dkl * t["kl"]
            db = t_qm - t_km + dqs * t["qs"] - t_kl
            db_mid = jnp.sum(t_km - t_qm, axis=0, keepdims=True)
            db_last = jnp.sum(t_kl, axis=0, keepdims=True) + del_s[...] * t["el"]
            db = db + jnp.where(row == CHUNK // 2 - 1, db_mid, 0.0) + jnp.where(row == CHUNK - 1, db_last, 0.0)
            dlogf = _chunk_sum(anti, db)
            sq, sg, sgn, f = t["sq"], t["sg"], t["sgn"], t["f"]
            dph_ref[rows, 0:D_MODEL] = (dq * (sq * (1.0 + hq * (1.0 - sq)))).astype(dph_ref.dtype)
            dph_ref[rows, D_MODEL:2 * D_MODEL] = (
                dlogf * (1.0 - lb) * sg * (1.0 - sg) / f - dk * (1.0 - lb) * sgn * (1.0 - sgn)).astype(dph_ref.dtype)
            dph_ref[rows, 2 * D_MODEL:3 * D_MODEL] = dv_s[...].astype(dph_ref.dtype)
            dlb_ref[...] += jnp.sum(dlogf * (1.0 - sg) / f - dk * sgn, axis=0, keepdims=True)

        @pl.when(i == nblk - 1)
        def _():
            dl0 = dlb_ref[...] * lb * (1.0 - lb)
            dl_ref[0:1, :] = dl0
            dl_ref[1:2, :] = -dl0

    wide = pltpu.VMEM((CHUNK, D_MODEL), F32)
    return _call(
        "hgrn_bwd", body, grid=(nblk,),
        in_specs=[col(0), col(1), col(2), col(3), tok,
                  pl.BlockSpec((cpb, D_MODEL, HGRN_K), lambda i: (rev(i), 0, 0)), tok,
                  pl.BlockSpec((2, D_MODEL), lambda i: (0, 0)), pl.BlockSpec((1, D_MODEL), lambda i: (0, 0)), HBM_SPEC],
        out_specs=[pl.BlockSpec((pl.Element(HGRN_TOKENS), pl.Element(4 * D_MODEL)), lambda i: (
                       pl.multiple_of(rev(i) * HGRN_TOKENS, ROW_ALIGN), first)),
                   pl.BlockSpec((1, D_MODEL), lambda i: (0, 0)), pl.BlockSpec((2, D_MODEL), lambda i: (0, 0))],
        out_shape=[jax.ShapeDtypeStruct(d_in.shape, d_in.dtype), jax.ShapeDtypeStruct((1, D_MODEL), F32),
                   jax.ShapeDtypeStruct((2, D_MODEL), F32)],
        scratch=[pltpu.VMEM((HGRN_HEADS, HGRN_K, HGRN_K), F32), pltpu.VMEM((1, D_MODEL), F32),
                 wide, wide, wide, wide, wide, wide, pltpu.VMEM((1, D_MODEL), F32)],
        args=[ph, ph, ph, ph, o_raw, states, dy, lb_logits, norm_g, d_in], semantics=("arbitrary",), comm=comm,
        aliases={9: 0})


def _local_step(x, target, vec, net):
    T, D = x.shape
    norm_mix_g, b_in, sinks, lb_logits = vec["norm_mix_g"], vec["b_in"], vec["attn_sinks"], vec["hgrn_lb_logits"]
    hgrn_norm_g, norm_ffn_g, norm_final_g = vec["hgrn_norm_g"], vec["norm_ffn_g"], vec["norm_final_g"]
    w_in = net.full("w_in")
    o_q, o_kv, o_h, o_g = (sum(IN_SPLITS[:i]) for i in range(4))
    TM = 512

    names = ("w_ffn_gate",)
    (u, pq, pkv, ph, pg), got = _in_proj(x, norm_mix_g, w_in, b_in, tm=256, comm=net.gather(names))
    net.gathered(names, got)
    names = ("w_branch_attn", "w_branch_hgrn")
    (y_attn, lse), got = _attn_fwd(pq, pkv, sinks, comm=net.gather(names))
    net.gathered(names, got)
    names = ("w_ffn_up",)
    (y_hgrn, o_raw, states), got = _hgrn_fwd(ph, lb_logits, hgrn_norm_g, comm=net.gather(names))
    net.gathered(names, got)
    w_ba, w_bh = net.full("w_branch_attn"), net.full("w_branch_hgrn")
    w_gate, w_up = net.full("w_ffn_gate"), net.full("w_ffn_up")
    names = ("w_out",)
    (ya, yb, merged), got = _merge_fwd(y_attn, y_hgrn, w_ba, w_bh, pg, tm=TM, comm=net.gather(names))
    net.gathered(names, got)
    w_out = net.full("w_out")

    names = ("w_ffn_down",)
    (h1, u2, gpre, up, z), got = _ffn_fwd(merged, w_out, x, norm_ffn_g, w_gate, w_up, tm=256,
                                          comm=net.gather(names))
    net.gathered(names, got)
    w_down = net.full("w_ffn_down")

    dh2, dh2b, dgp, dup, dgf_p, loss_p = _ffn_tail(z, w_down, h1, target, norm_final_g, gpre, up, tm=512)
    loss = jnp.sum(loss_p.reshape(-1, 8, D)[:, 0, 0])
    d_norm_final = _colsum_partials(dgf_p)
    d_w_down = _weight_grad("dw_down", z, dh2b, tm=DW_ROWS, tk=T)

    names, swap = ("w_ffn_down",), ()
    (dh1, dh1b, dg2_p), got = _ffn_in_bwd(dgp, dup, w_gate, w_up, h1, norm_ffn_g, dh2, tm=256,
                                          comm=net.exchange(dict(w_ffn_down=[d_w_down]), swap))
    net.received(names, swap, got)
    d_norm_ffn = _colsum_partials(dg2_p)
    d_w_gate = _weight_grad("dw_gate", dgp, u2, tm=DW_ROWS, tk=T)
    d_w_up = _weight_grad("dw_up", dup, u2, tm=DW_ROWS, tk=T)

    rows_in = sum(IN_SPLITS)
    dya, dyb, dy_attn, dy_hgrn, d_in = _merge_bwd(dh1b, w_out, w_ba, w_bh, ya, yb, pg, tm=TM, d_in_width=rows_in,
                                                  first=o_g)
    d_w_out = _weight_grad("dw_out", merged, dh1b, tm=1024, tk=1024)
    d_w_ba = _weight_grad("dw_branch_a", y_attn, dya, tm=1024, tk=1024)
    d_w_bh = _weight_grad("dw_branch_b", y_hgrn, dyb, tm=1024, tk=1024)

    names, swap = ("w_ffn_gate",), ("w_ffn_down",)
    (d_in, dsink), got = _attn_bwd(pq, pkv, sinks, lse, dy_attn, d_in,
                                   comm=net.exchange(dict(w_ffn_gate=[d_w_gate]), swap))
    net.received(names, swap, got)
    names, swap = ("w_ffn_up", "w_out"), ("w_ffn_gate",)
    (d_in, d_hgrn_norm, d_lb_logits), got = _hgrn_bwd(
        ph, o_raw, states, dy_hgrn, lb_logits, hgrn_norm_g, d_in, o_h,
        comm=net.exchange(dict(w_ffn_up=[d_w_up], w_out=[d_w_out]), swap))
    net.received(names, swap, got)

    names, swap = ("w_branch_attn", "w_branch_hgrn"), ("w_ffn_up", "w_out")
    (*d_w_in, d_b_in), got = _weight_grad(
        "dw_in", d_in, u, tm=DW_ROWS, tk=T, a_colsum=True, carrying=True,
        comm=net.exchange(dict(w_branch_attn=[d_w_ba], w_branch_hgrn=[d_w_bh]), swap))
    net.received(names, swap, got)
    d_w_in = [tuple(d_w_in)]

    first_level = net.presum_begin("w_in", d_w_in)
    halves = net.presum_end("w_in", [] if first_level is None else _copies_alone("presum_swap_w_in", first_level))
    names, swap = ("w_in",), ("w_branch_attn", "w_branch_hgrn")
    (dx, dg1_p), got = _in_proj_bwd([(d_in, 0)], w_in, x, norm_mix_g, dh1, tm=256,
                                    comm=_join(halves, net.swap(swap)))
    net.last = (names, swap, got)
    d_norm_mix = _colsum_partials(dg1_p)
    vecs = dict(norm_mix_g=d_norm_mix, b_in=d_b_in, attn_sinks=jnp.sum(dsink.reshape(Q_HEADS, ATTN_BLOCK), axis=1).reshape(1, Q_HEADS),
                hgrn_lb_logits=d_lb_logits,
                hgrn_norm_g=d_hgrn_norm, norm_ffn_g=d_norm_ffn, norm_final_g=d_norm_final)
    return loss, dx, vecs


def _place():
    return lax.axis_index("x"), lax.axis_index("y"), lax.axis_index("c")


def _other_chips(x, y):
    return [(1 - x, y), (x, 1 - y), (1 - x, 1 - y)]


def _y_first(copies):
    return [copies[3 * (i // 3) + (1, 0, 2)[i % 3]] for i in range(len(copies))]


def _gather_copies(shards):
    n = len(shards)

    def build(ins, outs, send_sems, recv_sems, local_sems, *stages):
        x, y, c = _place()
        mine = 2 * x + y
        local = [(pltpu.make_async_copy(ins[w], stages[w], local_sems.at[2 * w]),
                  pltpu.make_async_copy(stages[w], outs[w].at[mine], local_sems.at[2 * w + 1])) for w in range(n)]
        sends, recvs = [], []
        for w in range(n):
            for k, (px, py) in enumerate(_other_chips(x, y)):
                sem = 3 * w + k
                sends.append(pltpu.make_async_remote_copy(
                    src_ref=ins[w], dst_ref=outs[w].at[mine], send_sem=send_sems.at[sem], recv_sem=recv_sems.at[sem],
                    device_id=(px, py, c), device_id_type=MESH_ID))
                recvs.append(pltpu.make_async_remote_copy(
                    src_ref=ins[w], dst_ref=outs[w].at[2 * px + py], send_sem=send_sems.at[sem],
                    recv_sem=recv_sems.at[sem], device_id=(px, py, c), device_id_type=MESH_ID))
        return sends, recvs, local, _y_first(sends)

    return _Carried(shards, [jax.ShapeDtypeStruct((N_CHIPS,) + s.shape, s.dtype) for s in shards], 3 * n, 2 * n, build,
                    stages=[pltpu.VMEM(s.shape, s.dtype) for s in shards])


def _grad_copies(stacked):
    n = len(stacked)

    def build(ins, outs, send_sems, recv_sems, local_sems):
        x, y, c = _place()
        sends = []
        for w in range(n):
            for k, (px, py) in enumerate(_other_chips(x, y)):
                sem = 3 * w + k
                sends.append(pltpu.make_async_remote_copy(
                    src_ref=ins[w].at[2 * px + py], dst_ref=outs[w].at[k], send_sem=send_sems.at[sem],
                    recv_sem=recv_sems.at[sem], device_id=(px, py, c), device_id_type=MESH_ID))
        return sends, sends, [], _y_first(sends)

    return _Carried(stacked, [jax.ShapeDtypeStruct((3,) + s.shape[1:], s.dtype) for s in stacked], 3 * n, 0, build)


def _small_copies(small):
    def build(ins, outs, send_sems, recv_sems, local_sems):
        small_ref, all_ref = ins[0], outs[0]
        x, y, c = _place()
        me = 4 * x + 2 * y + c
        sends, recvs = [], []
        for r in range(1, 8):
            px = 1 - x if r & 4 else x
            py = 1 - y if r & 2 else y
            pc = 1 - c if r & 1 else c
            sends.append(pltpu.make_async_remote_copy(
                src_ref=small_ref, dst_ref=all_ref.at[me], send_sem=send_sems.at[r - 1], recv_sem=recv_sems.at[r - 1],
                device_id=(px, py, pc), device_id_type=MESH_ID))
            recvs.append(pltpu.make_async_remote_copy(
                src_ref=small_ref, dst_ref=all_ref.at[4 * px + 2 * py + pc], send_sem=send_sems.at[r - 1],
                recv_sem=recv_sems.at[r - 1], device_id=(px, py, pc), device_id_type=MESH_ID))
        return sends, recvs, [pltpu.make_async_copy(small_ref, all_ref.at[me], local_sems.at[0])]

    return _Carried([small], [jax.ShapeDtypeStruct((8,) + small.shape, small.dtype)], 7, 1, build)


def _gather_by_neighbours(name, shard):
    half = shard.shape[0] // 2
    quarter = half // 2

    def body(in_ref, out_ref, send_sems, recv_sems, local_sem, stage_ref):
        for core in (0, 1):
            @pl.when(lax.axis_index("c") == core)
            def _():
                program(core, in_ref, out_ref, send_sems, recv_sems, local_sem, stage_ref)

    def program(c, in_ref, out_ref, send_sems, recv_sems, local_sem, stage_ref):
        x, y, _ = _place()
        chip = lambda px, py: 2 * px + py
        to_x, to_y, sibling = (1 - x, y, c), (x, 1 - y, c), (x, y, 1 - c)
        x_blk, y_blk, d_blk = chip(1 - x, y), chip(x, 1 - y), chip(1 - x, 1 - y)
        mine, theirs = c * half, (1 - c) * half

        def copy(sem, rows, block, to, src=None):
            place = out_ref.at[block, pl.ds(rows[0], rows[1])]
            return pltpu.make_async_remote_copy(
                src_ref=place if src is None else src, dst_ref=place, send_sem=send_sems.at[sem],
                recv_sem=recv_sems.at[sem], device_id=to, device_id_type=MESH_ID)

        stage = pltpu.make_async_copy(in_ref, stage_ref, local_sem.at[0])
        own = pltpu.make_async_copy(stage_ref, out_ref.at[chip(x, y)], local_sem.at[1])
        my_rows = in_ref.at[pl.ds(mine, half)]
        along_x = dict(send=copy(0, (mine, half), chip(x, y), to_x, src=my_rows),
                       landed=copy(0, (mine, half), x_blk, to_x),
                       onward=[copy(3, (mine + quarter, quarter), x_blk, to_y), copy(4, (mine, half), x_blk, sibling)],
                       diagonal=copy(2, (mine, quarter), d_blk, to_x))
        along_y = dict(send=copy(1, (mine, half), chip(x, y), to_y, src=my_rows),
                       landed=copy(1, (mine, half), y_blk, to_y),
                       onward=[copy(2, (mine, quarter), y_blk, to_x), copy(5, (mine, half), y_blk, sibling)],
                       diagonal=copy(3, (mine + quarter, quarter), d_blk, to_y))
        last = copy(6, (mine, half), d_blk, sibling)

        order = (along_x, along_y) if c == 0 else (along_y, along_x)
        for axis in order:
            axis["send"].start()
        stage.start()
        stage.wait()
        own.start()
        for axis in order:
            axis["landed"].wait_recv()
            for cp in axis["onward"]:
                cp.start()
        for axis in order:
            axis["diagonal"].wait_recv()
        last.start()
        for sem, block in ((4, x_blk), (5, y_blk), (6, d_blk)):
            copy(sem, (theirs, half), block, sibling).wait_recv()
        for cp in [along_x["send"], along_y["send"]] + along_x["onward"] + along_y["onward"] + [last]:
            cp.wait_send()
        own.wait()

    return pl.pallas_call(
        body, name=name, in_specs=[HBM_SPEC], out_specs=HBM_SPEC,
        out_shape=jax.ShapeDtypeStruct((N_CHIPS,) + shard.shape, shard.dtype),
        scratch_shapes=[pltpu.SemaphoreType.DMA((7,)), pltpu.SemaphoreType.DMA((7,)), pltpu.SemaphoreType.DMA((2,)),
                        pltpu.VMEM(shard.shape, shard.dtype)],
    )(shard)


def _copies_alone(name, comm):
    return _call(name, lambda: None, grid=(), in_specs=[], out_specs=[], out_shape=[], args=[], comm=comm)[1]


class _Net:
    def __init__(self, shards):
        self.shards = shards
        self.whole, self.own, self.theirs, self.sums, self.other = {}, {}, {}, {}, {}
        x, y, _ = _place()
        self.chip = 2 * x + y

    def gather(self, names):
        return _gather_copies([self.shards[n] for n in names])

    def gathered(self, names, got):
        for n, g in zip(names, got):
            self.whole[n] = g.reshape(-1, g.shape[-1])

    def full(self, name):
        return self.whole[name]

    def exchange(self, grads, swap=()):
        stacked = []
        for n, pieces in grads.items():
            (keep, send), = pieces
            self.own[n] = keep
            stacked.append(send.reshape(N_CHIPS, keep.shape[0] // N_CHIPS, send.shape[-1]))
        return _join(_grad_copies(stacked), self.swap(swap))

    def swap(self, names):
        return _sibling_copies([self.sums[n] for n in names]) if names else None

    def presum_begin(self, name, pieces):
        keep = jnp.concatenate([p[0] for p in pieces], axis=0) if len(pieces) > 1 else pieces[0][0]
        send = jnp.concatenate([p[1] for p in pieces], axis=0) if len(pieces) > 1 else pieces[0][1]
        rows = keep.shape[0] // N_CHIPS
        self.held = keep.reshape(N_CHIPS, rows, keep.shape[-1])
        return _half_rows_copies(send.reshape(N_CHIPS, rows, send.shape[-1]))

    def presum_end(self, name, got):
        x, y, c = _place()
        to_send, self.own[name] = _pre_sum("presum_" + name, self.held, got[0], jnp.stack([c, self.chip]))
        return _grad_copies([to_send])

    def received(self, names, swap, got, carried=None):
        self.theirs.update(zip(names, got[:len(names)]))
        self.other.update(zip(swap, got[len(names):]))
        for n in names:
            (self.sums[n],), more = _partial_sum("sum_" + n, self.own[n], self.theirs[n], self.chip, comm=carried)
        return more


def _half_rows_copies(stacked):
    n, rows = stacked.shape[0], stacked.shape[1] // 2

    def build(ins, outs, send_sems, recv_sems, local_sems):
        x, y, c = _place()
        copies = [pltpu.make_async_remote_copy(
            src_ref=ins[0].at[s, pl.ds((1 - c) * rows, rows)], dst_ref=outs[0].at[s], send_sem=send_sems.at[s],
            recv_sem=recv_sems.at[s], device_id=(x, y, 1 - c), device_id_type=MESH_ID) for s in range(n)]
        return copies, copies, []

    return _Carried([stacked], [jax.ShapeDtypeStruct((n, rows, stacked.shape[2]), stacked.dtype)], n, 0, build)


def _pre_sum(name, held, theirs, core_and_chip):
    n, R, C = held.shape
    half = R // 2
    tr = _row_tile(half)
    per_half = half // tr

    def body(place_ref, h_ref, t_ref, send_ref, own_ref):
        total = h_ref[0] + t_ref[0].astype(F32)
        send_ref[0] = total.astype(send_ref.dtype)

        @pl.when(pl.program_id(1) == place_ref[1])
        def _():
            own_ref[...] = total

    return pl.pallas_call(
        body, name=name,
        grid_spec=pltpu.PrefetchScalarGridSpec(
            num_scalar_prefetch=1, grid=(per_half, n),
            in_specs=[pl.BlockSpec((1, tr, C), lambda i, s, place: (s, place[0] * per_half + i, 0)),
                      pl.BlockSpec((1, tr, C), lambda i, s, place: (s, i, 0))],
            out_specs=[pl.BlockSpec((1, tr, C), lambda i, s, place: (s, i, 0)),
                       pl.BlockSpec((tr, C), lambda i, s, place: (i, 0))]),
        out_shape=[jax.ShapeDtypeStruct((n, half, C), MXU_DTYPE), jax.ShapeDtypeStruct((half, C), F32)],
        compiler_params=_params(("arbitrary", "arbitrary")),
    )(core_and_chip, held, theirs)


def _sibling_copies(parts):
    n = len(parts)

    def build(ins, outs, send_sems, recv_sems, local_sems):
        x, y, c = _place()
        copies = [pltpu.make_async_remote_copy(
            src_ref=ins[w], dst_ref=outs[w], send_sem=send_sems.at[w], recv_sem=recv_sems.at[w],
            device_id=(x, y, 1 - c), device_id_type=MESH_ID) for w in range(n)]
        return copies, copies, []

    return _Carried(parts, [jax.ShapeDtypeStruct(p.shape, p.dtype) for p in parts], n, 0, build)


def _row_tile(rows, most=512, sublanes=16):
    return max(t for t in range(sublanes, min(most, rows // 2) + 1, sublanes) if rows % t == 0)


def _partial_sum(name, own, recv, chip, comm=None):
    _, R, C = recv.shape
    tr = _row_tile(R)

    def body(o_ref, r_ref, p_ref):
        p_ref[...] = ((o_ref[...] + r_ref[0].astype(F32)) + r_ref[1].astype(F32)) + r_ref[2].astype(F32)

    if own.shape[0] != R:
        assert comm is None and own.shape[0] == N_CHIPS * R
        total = pl.pallas_call(
            lambda chip_ref, *refs: body(*refs), name=name,
            grid_spec=pltpu.PrefetchScalarGridSpec(
                num_scalar_prefetch=1, grid=(R // tr,),
                in_specs=[pl.BlockSpec((tr, C), lambda i, chip_ref: (chip_ref[0] * (R // tr) + i, 0)),
                          pl.BlockSpec((3, tr, C), lambda i, chip_ref: (0, i, 0))],
                out_specs=pl.BlockSpec((tr, C), lambda i, chip_ref: (i, 0))),
            out_shape=jax.ShapeDtypeStruct((R, C), F32), compiler_params=_params(("parallel",)),
        )(chip.reshape(1), own, recv)
        return [total], []
    return _call(name, body, grid=(R // tr,),
                 in_specs=[pl.BlockSpec((tr, C), lambda i: (i, 0)), pl.BlockSpec((3, tr, C), lambda i: (0, i, 0))],
                 out_specs=[pl.BlockSpec((tr, C), lambda i: (i, 0))], out_shape=[jax.ShapeDtypeStruct((R, C), F32)],
                 args=[own, recv], semantics=("parallel",), comm=comm)


def _adam_vals(w, g, m, v):
    m = ADAM_B1 * m + (1.0 - ADAM_B1) * g
    v = ADAM_B2 * v + (1.0 - ADAM_B2) * (g * g)
    m_hat = m / (1.0 - ADAM_B1 ** ADAM_STEP)
    v_hat = v / (1.0 - ADAM_B2 ** ADAM_STEP)
    delta = -ADAM_LR * (m_hat / (jnp.sqrt(v_hat) + ADAM_EPS) + ADAM_WD * w)
    return delta, m, v


def _adamw(name, w, m, v, mine, other, comm=None):
    R, C = w.shape
    tr = _row_tile(R)

    def body(w_ref, m_ref, v_ref, s_ref, n_ref, g_ref, d_ref, nm_ref, nv_ref):
        g = s_ref[...] + n_ref[...]
        d, nm, nv = _adam_vals(w_ref[...], g, m_ref[...], v_ref[...])
        g_ref[...], d_ref[...], nm_ref[...], nv_ref[...] = g, d, nm, nv

    spec = pl.BlockSpec((tr, C), lambda i: (i, 0))
    return _call(name, body, grid=(R // tr,), in_specs=[spec] * 5, out_specs=[spec] * 4,
                 out_shape=[jax.ShapeDtypeStruct((R, C), F32)] * 4, args=[w, m, v, mine, other],
                 semantics=("parallel",), comm=comm)


def _adamw_by_halves(name, w, m, v, mine, other, core):
    R, C = w.shape
    tr = _row_tile(R // 2)
    per_half = R // 2 // tr

    def body(c_ref, w_ref, m_ref, v_ref, s_ref, n_ref, g_ref, d_ref, nm_ref, nv_ref):
        g = jnp.where(pl.program_id(0) // per_half == c_ref[0, 0], s_ref[...], n_ref[...])
        d, nm, nv = _adam_vals(w_ref[...], g, m_ref[...], v_ref[...])
        g_ref[...], d_ref[...], nm_ref[...], nv_ref[...] = g, d, nm, nv

    spec = pl.BlockSpec((tr, C), lambda i: (i, 0))
    part = pl.BlockSpec((tr, C), lambda i: (i % per_half, 0))
    return pl.pallas_call(
        body, name=name, grid=(R // tr,),
        in_specs=[pl.BlockSpec(memory_space=pltpu.SMEM), spec, spec, spec, part, part], out_specs=[spec] * 4,
        out_shape=[jax.ShapeDtypeStruct((R, C), F32)] * 4, compiler_params=_params(("parallel",)),
    )(core, w, m, v, mine, other)


SMALL_LAYOUT = dict(norm_mix_g=(0, 1, 1024), b_in=(1, 8, 7424), hgrn_norm_g=(9, 1, 1024), norm_ffn_g=(10, 1, 1024),
                    norm_final_g=(11, 1, 1024), hgrn_lb_logits=(12, 2, 2048), attn_sinks=(14, 1, 16))
SMALL_LOSS_ROW, SMALL_ROWS = 15, 16


def _pack_small(grads, loss):
    rows = [jnp.pad(grads[name].astype(F32).reshape(-1), (0, nrows * D_MODEL - n))
            for name, (_, nrows, n) in SMALL_LAYOUT.items()]
    rows.append(jnp.pad(loss.astype(F32).reshape(1), (0, D_MODEL - 1)))
    return jnp.concatenate(rows).reshape(SMALL_ROWS, D_MODEL)


def _adamw_small(w, m, v, g_all):
    names = list(SMALL_LAYOUT)
    n = len(names)

    def body(a_ref, *refs):
        ins, outs = refs[:3 * n], refs[3 * n:]
        g_all_rows = a_ref[0]
        for dev in range(1, 8):
            g_all_rows = g_all_rows + a_ref[dev]
        for i, name in enumerate(names):
            first, nrows, count = SMALL_LAYOUT[name]
            w_ref, m_ref, v_ref = ins[3 * i:3 * i + 3]
            if w_ref.shape[0] == nrows:
                g = g_all_rows[first:first + nrows, :w_ref.shape[1]]
            else:
                last = count - (nrows - 1) * D_MODEL
                g = jnp.concatenate([g_all_rows[r:r + 1, :] for r in range(first, first + nrows - 1)]
                                    + [g_all_rows[first + nrows - 1:first + nrows, :last]], axis=1)
            d, nm, nv = _adam_vals(w_ref[...], g, m_ref[...], v_ref[...])
            for o_ref, val in zip(outs[4 * i:4 * i + 4], (g, d, nm, nv)):
                o_ref[...] = val
        outs[4 * n][...] = g_all_rows[SMALL_LOSS_ROW:SMALL_LOSS_ROW + 1, 0:1]

    res = pl.pallas_call(
        body, name="adamw_small",
        out_shape=[jax.ShapeDtypeStruct(w[name].shape, F32) for name in names for _ in range(4)]
        + [jax.ShapeDtypeStruct((1, 1), F32)],
    )(g_all, *[t[name] for name in names for t in (w, m, v)])
    return {name: res[4 * i:4 * i + 4] for i, name in enumerate(names)}, res[4 * n]


MATRICES = ("w_in", "w_branch_attn", "w_branch_hgrn", "w_out", "w_ffn_gate", "w_ffn_up", "w_ffn_down")
COLUMN_SHARDED = ("w_in", "w_ffn_gate", "w_ffn_up")
WEIGHTS = ("norm_mix_g", "w_in", "b_in", "attn_sinks", "hgrn_lb_logits", "hgrn_norm_g", "w_branch_attn",
           "w_branch_hgrn", "w_out", "norm_ffn_g", "w_ffn_gate", "w_ffn_up", "w_ffn_down", "norm_final_g")


def kernel(x, norm_mix_g, w_in, b_in, attn_sinks, hgrn_lb_logits, hgrn_norm_g, w_branch_attn, w_branch_hgrn, w_out, norm_ffn_g, w_ffn_gate, w_ffn_up, w_ffn_down, norm_final_g, loss_target, m_norm_mix_g, m_w_in, m_b_in, m_attn_sinks, m_hgrn_lb_logits, m_hgrn_norm_g, m_w_branch_attn, m_w_branch_hgrn, m_w_out, m_norm_ffn_g, m_w_ffn_gate, m_w_ffn_up, m_w_ffn_down, m_norm_final_g, v_norm_mix_g, v_w_in, v_b_in, v_attn_sinks, v_hgrn_lb_logits, v_hgrn_norm_g, v_w_branch_attn, v_w_branch_hgrn, v_w_out, v_norm_ffn_g, v_w_ffn_gate, v_w_ffn_up, v_w_ffn_down, v_norm_final_g):
    given = dict(locals())
    w = {n: given[n] for n in WEIGHTS}
    m = {n: given["m_" + n] for n in WEIGHTS}
    v = {n: given["v_" + n] for n in WEIGHTS}

    block = lambda a, n: jnp.transpose(a[0]) if n in COLUMN_SHARDED else a[0]
    unblock = lambda a, n: (jnp.transpose(a) if n in COLUMN_SHARDED else a)[None]
    net = _Net({n: block(w[n], n).astype(MXU_DTYPE) for n in MATRICES})
    net.gathered(("w_in",), [_gather_by_neighbours("gather_w_in", net.shards["w_in"])])
    vec = dict(norm_mix_g=norm_mix_g, b_in=b_in, attn_sinks=attn_sinks, hgrn_lb_logits=hgrn_lb_logits,
               hgrn_norm_g=hgrn_norm_g, norm_ffn_g=norm_ffn_g, norm_final_g=norm_final_g.reshape(1, D_MODEL))
    loss_part, dx, d_vecs = _local_step(x[0], loss_target[0], vec, net)

    small_all, = net.received(*net.last, carried=_small_copies(_pack_small(d_vecs, loss_part)))
    grads, deltas, new_m, new_v = {}, {}, {}, {}
    for n in ("w_ffn_down", "w_ffn_gate", "w_ffn_up", "w_out", "w_branch_attn", "w_branch_hgrn"):
        res, got = _adamw("adamw_" + n, block(w[n], n), block(m[n], n), block(v[n], n), net.sums[n], net.other[n],
                          comm=net.swap(("w_in",)) if n == "w_ffn_down" else None)
        if n == "w_ffn_down":
            net.other["w_in"], = got
        grads[n], deltas[n], new_m[n], new_v[n] = (unblock(r, n) for r in res)
    n = "w_in"
    res = _adamw_by_halves("adamw_" + n, block(w[n], n), block(m[n], n), block(v[n], n), net.sums[n], net.other[n],
                           _place()[2].reshape(1, 1))
    grads[n], deltas[n], new_m[n], new_v[n] = (unblock(r, n) for r in res)
    rows = lambda t: {n: t[n].reshape(-1, t[n].shape[-1]) for n in SMALL_LAYOUT}
    res, loss = _adamw_small(rows(w), rows(m), rows(v), small_all)
    for n, four in res.items():
        grads[n], deltas[n], new_m[n], new_v[n] = (r.reshape(w[n].shape) for r in four)
    loss = loss.reshape(())
    return (loss, dx[None], *[grads[n] for n in WEIGHTS], *[deltas[n] for n in WEIGHTS],
            *[new_m[n] for n in WEIGHTS], *[new_v[n] for n in WEIGHTS])
```

```python
import collections
import functools
import math

import jax
import jax.numpy as jnp
from jax import lax
from jax.experimental import pallas as pl
from jax.experimental.pallas import tpu as pltpu

F32 = jnp.float32
BF16 = jnp.bfloat16
MXU_DTYPE = jnp.bfloat16
SAVED_DTYPE = jnp.bfloat16
MESH_ID = pl.DeviceIdType.MESH

D_MODEL = 1024
HEAD_DIM = 64
Q_HEADS = 16
KV_HEADS = 2
GROUP = Q_HEADS // KV_HEADS
KV_WIDTH = KV_HEADS * HEAD_DIM
ATTN_BLOCK = 128
HGRN_HEADS = 8
HGRN_K = 128
CHUNK = 64
HGRN_TOKENS = 256
FFN = 2816
IN_SPLITS = (1024, 256, 4096, 2048)
EPS = 1e-6
NEG_INF = -1e30
ADAM_LR, ADAM_B1, ADAM_B2, ADAM_EPS, ADAM_WD, ADAM_STEP = 0.001, 0.9, 0.999, 1e-08, 0.01, 10
N_CHIPS = 4
VMEM_LIMIT = 60 * 1024 * 1024
ROW_ALIGN = 16
DW_ROWS = 256


def _params(sem=None):
    return pltpu.CompilerParams(dimension_semantics=sem, vmem_limit_bytes=VMEM_LIMIT)


def _sigmoid(v):
    return 0.5 * jnp.tanh(0.5 * v) + 0.5


def _dot(a, b, dims):
    return lax.dot_general(a.astype(MXU_DTYPE), b.astype(MXU_DTYPE), (dims, ((), ())),
                           preferred_element_type=F32)


def _nn(a, b):
    return _dot(a, b, ((1,), (0,)))


def _nt(a, b):
    return _dot(a, b, ((1,), (1,)))


def _tn(a, b):
    return _dot(a, b, ((0,), (0,)))


HBM_SPEC = pl.BlockSpec(memory_space=pl.ANY)


class _Carried:
    def __init__(self, arrays, out_shapes, n_remote, n_local, build):
        self.parts = [(len(arrays), len(out_shapes), build)]
        self.arrays, self.out_shapes = list(arrays), list(out_shapes)
        self.scratch = [pltpu.SemaphoreType.DMA((n_remote,)), pltpu.SemaphoreType.DMA((n_remote,)),
                        pltpu.SemaphoreType.DMA((max(n_local, 1),))]

    def __add__(self, other):
        both = _Carried([], [], 1, 0, None)
        both.parts = self.parts + other.parts
        both.arrays, both.out_shapes = self.arrays + other.arrays, self.out_shapes + other.out_shapes
        both.scratch = self.scratch + other.scratch
        return both

    def _built(self, ins, outs, sems):
        for p, (ni, no, build) in enumerate(self.parts):
            yield build(ins[:ni], outs[:no], *sems[3 * p:3 * p + 3])
            ins, outs = ins[ni:], outs[no:]

    def start(self, ins, outs, sems):
        core = lax.axis_index("c")
        for sends, _, local, *other_order in self._built(ins, outs, sems):
            for cp in local:
                cp.start()
            if not other_order:
                for cp in sends:
                    cp.start()
                continue

            @pl.when(core == 0)
            def _():
                for cp in sends:
                    cp.start()

            @pl.when(core == 1)
            def _():
                for cp in other_order[0]:
                    cp.start()

    def wait(self, ins, outs, sems):
        for sends, recvs, local, *_ in self._built(ins, outs, sems):
            for cp in recvs:
                cp.wait_recv()
            for cp in sends:
                cp.wait_send()
            for cp in local:
                cp.wait()


def _join(*comms):
    comms = [c for c in comms if c is not None]
    return functools.reduce(lambda a, b: a + b, comms) if comms else None


def _call(name, body, *, grid, in_specs, out_specs, out_shape, args, scratch=(), semantics=None, comm=None,
          aliases=None):
    n_in, n_out, n_scr = len(in_specs), len(out_specs), len(scratch)
    aliases = aliases or {}
    if comm is None:
        res = pl.pallas_call(body, name=name, grid=grid, in_specs=in_specs, out_specs=out_specs, out_shape=out_shape,
                             scratch_shapes=list(scratch), input_output_aliases=aliases,
                             compiler_params=_params(semantics))(*args)
        return list(res), []
    ci, co = len(comm.arrays), len(comm.out_shapes)

    def carrying(*refs):
        ins, refs = refs[:n_in], refs[n_in:]
        c_ins, refs = refs[:ci], refs[ci:]
        outs, refs = refs[:n_out], refs[n_out:]
        c_outs, refs = refs[:co], refs[co:]
        scr, sems = refs[:n_scr], refs[n_scr:]
        if not grid:
            comm.start(c_ins, c_outs, sems)
            body(*ins, *outs, *scr)
            comm.wait(c_ins, c_outs, sems)
            return
        first = functools.reduce(jnp.logical_and, [pl.program_id(a) == 0 for a in range(len(grid))])
        last = functools.reduce(jnp.logical_and, [pl.program_id(a) == g - 1 for a, g in enumerate(grid)])

        @pl.when(first)
        def _():
            comm.start(c_ins, c_outs, sems)

        body(*ins, *outs, *scr)

        @pl.when(last)
        def _():
            comm.wait(c_ins, c_outs, sems)

    res = pl.pallas_call(
        carrying, name=name, grid=grid, in_specs=list(in_specs) + [HBM_SPEC] * ci,
        out_specs=list(out_specs) + [HBM_SPEC] * co, out_shape=list(out_shape) + comm.out_shapes,
        scratch_shapes=list(scratch) + comm.scratch, input_output_aliases=aliases,
        compiler_params=_params(("arbitrary",) * len(grid) if grid else None),
    )(*args, *comm.arrays)
    return list(res[:n_out]), list(res[n_out:])


_Cols = collections.namedtuple("_Cols", "array first cols")
_Into = collections.namedtuple("_Into", "rows first held")


def _weight_grad(name, a, b, *, tm, tk, a_colsum=False, into=None, carrying=False, comm=None):
    cols = a if isinstance(a, _Cols) else _Cols(a, 0, a.shape[1])
    a = cols.array
    (T, N), M = b.shape, cols.cols
    tk = min(tk, T)
    assert cols.first % tm == 0 and M % tm == 0 and T % tk == 0, (name, cols.first, M, tm, T, tk)
    ni, nk, tile0 = M // tm, T // tk, cols.first // tm
    held = list(into.held) if into is not None and into.held is not None else []

    def body(a_ref, b_ref, *rest):
        keep_ref, send_ref = rest[len(held):len(held) + 2]
        sums_ref = rest[len(held) + 2] if a_colsum else None
        if nk == 1:
            acc = _tn(a_ref[...], b_ref[...])
            keep_ref[...], send_ref[...] = acc, acc.astype(send_ref.dtype)
            if a_colsum:
                sums_ref[...] = jnp.sum(a_ref[...].astype(F32), axis=0, keepdims=True)
            return
        acc_ref = rest[-1]
        k = pl.program_id(1)

        @pl.when(k == 0)
        def _():
            acc_ref[...] = jnp.zeros_like(acc_ref)
            if a_colsum:
                sums_ref[...] = jnp.zeros((1, tm), F32)

        if a_colsum:
            sums_ref[...] += jnp.sum(a_ref[...].astype(F32), axis=0, keepdims=True)
        acc_ref[...] += _tn(a_ref[...], b_ref[...])

        @pl.when(k == nk - 1)
        def _():
            keep_ref[...], send_ref[...] = acc_ref[...], acc_ref[...].astype(send_ref.dtype)

    if into is None:
        rows, out_spec = M, pl.BlockSpec((tm, N), lambda i, k: (i, 0))
    else:
        rows = into.rows
        out_spec = pl.BlockSpec((pl.Element(tm), pl.Element(N)),
                                lambda i, k: (pl.multiple_of(into.first + i * tm, ROW_ALIGN), 0))
    out_shape = [jax.ShapeDtypeStruct((rows, N), F32), jax.ShapeDtypeStruct((rows, N), MXU_DTYPE)]
    out_specs = [out_spec, out_spec]
    if a_colsum:
        out_shape.append(jax.ShapeDtypeStruct((1, M), F32))
        out_specs.append(pl.BlockSpec((1, tm), lambda i, k: (0, i)))
    res, got = _call(
        name, body, grid=(ni, nk),
        in_specs=[pl.BlockSpec((tk, tm), lambda i, k: (k, tile0 + i)), pl.BlockSpec((tk, N), lambda i, k: (k, 0))]
        + [HBM_SPEC] * len(held),
        out_specs=out_specs, out_shape=out_shape, scratch=[pltpu.VMEM((tm, N), F32)] if nk > 1 else [],
        args=[a, b] + held, aliases={2 + p: p for p in range(len(held))}, semantics=("parallel", "arbitrary"),
        comm=comm)
    return (res, got) if carrying else res


def _ffn_fwd(merged, w_out, x, gain, w_gate_t, w_up_t, *, tm, comm=None):
    (T, D), F = x.shape, w_gate_t.shape[0]

    def body(m_ref, wo_ref, x_ref, g_ref, wg_ref, wu_ref, h_ref, u_ref, gate_ref, up_ref, z_ref):
        h = x_ref[...] + _nn(m_ref[...], wo_ref[...])
        h_ref[...] = h
        u = (h * lax.rsqrt(jnp.mean(h * h, axis=-1, keepdims=True) + EPS) * g_ref[...]).astype(u_ref.dtype)
        u_ref[...] = u
        gate, up = _nt(u, wg_ref[...]), _nt(u, wu_ref[...])
        gate_ref[...], up_ref[...] = gate.astype(gate_ref.dtype), up.astype(up_ref.dtype)
        z_ref[...] = (gate * _sigmoid(gate) * up).astype(z_ref.dtype)

    rows = lambda n: pl.BlockSpec((tm, n), lambda i: (i, 0))
    fixed = _fixed_spec
    return _call("ffn_hidden", body, grid=(T // tm,),
                 in_specs=[rows(D), fixed(w_out), rows(D), fixed(gain), fixed(w_gate_t), fixed(w_up_t)],
                 out_specs=[rows(D), rows(D), rows(F), rows(F), rows(F)],
                 out_shape=[jax.ShapeDtypeStruct((T, D), F32), jax.ShapeDtypeStruct((T, D), MXU_DTYPE)]
                 + [jax.ShapeDtypeStruct((T, F), SAVED_DTYPE)] * 2 + [jax.ShapeDtypeStruct((T, F), MXU_DTYPE)],
                 args=[merged, w_out, x, gain, w_gate_t, w_up_t], semantics=("parallel",), comm=comm)


def _in_proj(x, gain, w_in_t, b_in, *, tm, comm=None):
    T, D = x.shape
    bounds = [sum(IN_SPLITS[:i]) for i in range(len(IN_SPLITS) + 1)]

    def body(x_ref, g_ref, w_ref, b_ref, u_ref, *piece_refs):
        xv = x_ref[...]
        r = lax.rsqrt(jnp.mean(xv * xv, axis=-1, keepdims=True) + EPS)
        u = (xv * r * g_ref[...]).astype(u_ref.dtype)
        u_ref[...] = u
        for o_ref, lo, hi in zip(piece_refs, bounds[:-1], bounds[1:]):
            o_ref[...] = (_nt(u, w_ref[lo:hi, :]) + b_ref[:, lo:hi]).astype(o_ref.dtype)

    rows = lambda n: pl.BlockSpec((tm, n), lambda i: (i, 0))
    fixed = _fixed_spec
    dtypes = (MXU_DTYPE, MXU_DTYPE, F32, F32)
    return _call("in_proj", body, grid=(T // tm,),
                 in_specs=[rows(D), fixed(gain), fixed(w_in_t), fixed(b_in)],
                 out_specs=[rows(D)] + [rows(n) for n in IN_SPLITS],
                 out_shape=[jax.ShapeDtypeStruct((T, D), MXU_DTYPE)]
                 + [jax.ShapeDtypeStruct((T, n), dt) for n, dt in zip(IN_SPLITS, dtypes)],
                 args=[x, gain, w_in_t, b_in], semantics=("parallel",), comm=comm)


def _row_spec(tm, n):
    return pl.BlockSpec((tm, n), lambda i: (i, 0))


def _fixed_spec(a):
    return pl.BlockSpec(a.shape, lambda i: (0,) * a.ndim, pipeline_mode=pl.Buffered(1))


def _partials_spec(n):
    return pl.BlockSpec((8, n), lambda i: (i, 0))


def _row_parts(tm, parts):
    assert tm % parts == 0, (tm, parts)
    return [slice(p * (tm // parts), (p + 1) * (tm // parts)) for p in range(parts)]


def _ffn_tail(z, w_down, h1, target, gain, gate, up, *, tm, parts=2):
    (T, F), D = z.shape, h1.shape[1]

    def body(z_ref, w_ref, h_ref, t_ref, g_ref, gate_ref, up_ref, dh_ref, dhb_ref, dgate_ref, dup_ref, dg_ref, l_ref):
        part, dgain = 0.0, 0.0
        pieces = _row_parts(tm, parts)
        h2s = [h_ref[rows, :] + _nn(z_ref[rows, :], w_ref[...]) for rows in pieces]
        for rows, h2 in zip(pieces, h2s):
            r = lax.rsqrt(jnp.mean(h2 * h2, axis=-1, keepdims=True) + EPS)
            xhat = h2 * r
            err = xhat * g_ref[...] - t_ref[rows, :]
            part += 0.5 * jnp.sum(jnp.sum(err * err, axis=-1, keepdims=True), axis=0, keepdims=True) / D
            dy = err / D
            dxh = dy * g_ref[...]
            dh2 = r * (dxh - xhat * jnp.mean(dxh * xhat, axis=-1, keepdims=True))
            dh_ref[rows, :] = dh2
            dhb = dh2.astype(dhb_ref.dtype)
            dhb_ref[rows, :] = dhb
            dgain += jnp.sum(dy * xhat, axis=0, keepdims=True)
            dz = _nt(dhb, w_ref[...])
            gv, upv = gate_ref[rows, :].astype(F32), up_ref[rows, :].astype(F32)
            s = _sigmoid(gv)
            dgate_ref[rows, :] = (dz * upv * (s * (1.0 + gv * (1.0 - s)))).astype(dgate_ref.dtype)
            dup_ref[rows, :] = (dz * (gv * s)).astype(dup_ref.dtype)
        dg_ref[...] = jnp.broadcast_to(dgain, dg_ref.shape)
        l_ref[...] = jnp.broadcast_to(part, l_ref.shape)

    low = lambda n: jax.ShapeDtypeStruct((T, n), MXU_DTYPE)
    part = jax.ShapeDtypeStruct((8 * (T // tm), D), F32)
    return pl.pallas_call(
        body, name="ffn_tail", grid=(T // tm,),
        in_specs=[_row_spec(tm, F), _fixed_spec(w_down), _row_spec(tm, D), _row_spec(tm, D), _fixed_spec(gain),
                  _row_spec(tm, F), _row_spec(tm, F)],
        out_specs=[_row_spec(tm, D), _row_spec(tm, D), _row_spec(tm, F), _row_spec(tm, F), _partials_spec(D),
                   _partials_spec(D)],
        out_shape=[jax.ShapeDtypeStruct((T, D), F32), low(D), low(F), low(F), part, part],
        compiler_params=_params(("parallel",)),
    )(z, w_down, h1, target, gain, gate, up)


def _ffn_in_bwd(dgate, dup, w_gate_t, w_up_t, h1, gain, dres, *, tm, comm=None):
    (T, F), D = dgate.shape, h1.shape[1]

    def body(dg_ref, du_ref, wg_ref, wu_ref, h_ref, g_ref, r_ref, dh_ref, dhb_ref, dgain_ref):
        d_u2 = _nn(dg_ref[...], wg_ref[...]) + _nn(du_ref[...], wu_ref[...])
        dx, dgain = _rmsnorm_bwd_vals(d_u2, h_ref[...], g_ref[...])
        dh = r_ref[...] + dx
        dh_ref[...] = dh
        dhb_ref[...] = dh.astype(dhb_ref.dtype)
        dgain_ref[...] = jnp.broadcast_to(dgain, dgain_ref.shape)

    return _call("d_ffn_in", body, grid=(T // tm,),
                 in_specs=[_row_spec(tm, F), _row_spec(tm, F), _fixed_spec(w_gate_t), _fixed_spec(w_up_t),
                           _row_spec(tm, D), _fixed_spec(gain), _row_spec(tm, D)],
                 out_specs=[_row_spec(tm, D), _row_spec(tm, D), _partials_spec(D)],
                 out_shape=[jax.ShapeDtypeStruct((T, D), F32), jax.ShapeDtypeStruct((T, D), MXU_DTYPE),
                            jax.ShapeDtypeStruct((8 * (T // tm), D), F32)],
                 args=[dgate, dup, w_gate_t, w_up_t, h1, gain, dres], semantics=("parallel",), comm=comm)


def _in_proj_bwd(pieces, w_in_t, x, gain, dres, *, tm, comm=None):
    T, D = x.shape
    n = len(pieces)

    def body(*refs):
        dps, (w_ref, x_ref, g_ref, r_ref, dx_ref, dgain_ref) = refs[:n], refs[n:]
        d_u = None
        for dp_ref, (dp, first) in zip(dps, pieces):
            term = _nn(dp_ref[...], w_ref[first:first + dp.shape[1], :])
            d_u = term if d_u is None else d_u + term
        dx, dgain = _rmsnorm_bwd_vals(d_u, x_ref[...], g_ref[...])
        dx_ref[...] = r_ref[...] + dx
        dgain_ref[...] = jnp.broadcast_to(dgain, dgain_ref.shape)

    return _call("d_u", body, grid=(T // tm,),
                 in_specs=[_row_spec(tm, dp.shape[1]) for dp, _ in pieces]
                 + [_fixed_spec(w_in_t), _row_spec(tm, D), _fixed_spec(gain), _row_spec(tm, D)],
                 out_specs=[_row_spec(tm, D), _partials_spec(D)],
                 out_shape=[jax.ShapeDtypeStruct((T, D), F32), jax.ShapeDtypeStruct((8 * (T // tm), D), F32)],
                 args=[dp for dp, _ in pieces] + [w_in_t, x, gain, dres], semantics=("parallel",), comm=comm)


def _merge_fwd(y_a, y_b, w_a, w_b, gates, *, tm, comm=None):
    T, D = y_a.shape

    def body(ya_ref, yb_ref, wa_ref, wb_ref, ga_ref, gb_ref, pa_ref, pb_ref, m_ref):
        pa, pb = _nn(ya_ref[...], wa_ref[...]), _nn(yb_ref[...], wb_ref[...])
        pa_ref[...], pb_ref[...] = pa.astype(pa_ref.dtype), pb.astype(pb_ref.dtype)
        m_ref[...] = (_sigmoid(ga_ref[...]) * pa + _sigmoid(gb_ref[...]) * pb).astype(m_ref.dtype)

    rows = pl.BlockSpec((tm, D), lambda i: (i, 0))
    whole = pl.BlockSpec((D, D), lambda i: (0, 0), pipeline_mode=pl.Buffered(1))
    return _call("branch_merge", body, grid=(T // tm,),
                 in_specs=[rows, rows, whole, whole, rows, pl.BlockSpec((tm, D), lambda i: (i, 1))],
                 out_specs=[rows] * 3,
                 out_shape=[jax.ShapeDtypeStruct((T, D), SAVED_DTYPE)] * 2 + [jax.ShapeDtypeStruct((T, D), MXU_DTYPE)],
                 args=[y_a, y_b, w_a, w_b, gates, gates], semantics=("parallel",), comm=comm)


def _merge_bwd(dh, w_out, w_a, w_b, p_a, p_b, gates, *, tm, d_in_width, first):
    T, D = dh.shape

    def body(dh_ref, wo_ref, wa_ref, wb_ref, pa_ref, pb_ref, ga_ref, gb_ref, dpa_ref, dpb_ref, dya_ref, dyb_ref,
             din_ref):
        dm = _nt(dh_ref[...], wo_ref[...])
        sa, sb = _sigmoid(ga_ref[...]), _sigmoid(gb_ref[...])
        dpa, dpb = (dm * sa).astype(dpa_ref.dtype), (dm * sb).astype(dpb_ref.dtype)
        dpa_ref[...], dpb_ref[...] = dpa, dpb
        din_ref[:, :D] = (dm * pa_ref[...].astype(F32) * sa * (1.0 - sa)).astype(din_ref.dtype)
        din_ref[:, D:] = (dm * pb_ref[...].astype(F32) * sb * (1.0 - sb)).astype(din_ref.dtype)
        dya_ref[...] = _nt(dpa, wa_ref[...]).astype(dya_ref.dtype)
        dyb_ref[...] = _nt(dpb, wb_ref[...]).astype(dyb_ref.dtype)

    rows = pl.BlockSpec((tm, D), lambda i: (i, 0))
    whole = pl.BlockSpec((D, D), lambda i: (0, 0), pipeline_mode=pl.Buffered(1))
    low = jax.ShapeDtypeStruct((T, D), MXU_DTYPE)
    return pl.pallas_call(
        body, name="d_branch_merge", grid=(T // tm,),
        in_specs=[rows, whole, whole, whole, rows, rows, rows, pl.BlockSpec((tm, D), lambda i: (i, 1))],
        out_specs=[rows] * 4 + [pl.BlockSpec((pl.Element(tm), pl.Element(2 * D)), lambda i: (
            pl.multiple_of(i * tm, ROW_ALIGN), first))],
        out_shape=[low] * 3 + [jax.ShapeDtypeStruct((T, D), F32), jax.ShapeDtypeStruct((T, d_in_width), MXU_DTYPE)],
        compiler_params=_params(("parallel",)),
    )(dh, w_out, w_a, w_b, p_a, p_b, gates, gates)


def _colsum_partials(p):
    return jnp.sum(p.reshape(-1, 8, p.shape[-1])[:, 0, :], axis=0, keepdims=True)


def _rmsnorm_bwd_vals(dy, xin, g):
    rstd = lax.rsqrt(jnp.mean(xin * xin, axis=-1, keepdims=True) + EPS)
    xhat = xin * rstd
    dg = jnp.sum(dy * xhat, axis=0, keepdims=True)
    dxh = dy * g
    dx = rstd * (dxh - xhat * jnp.mean(dxh * xhat, axis=-1, keepdims=True))
    return dx, dg


ATTN_SCALE = 1.0 / math.sqrt(HEAD_DIM)
GROUP_LANES = GROUP * ATTN_BLOCK
PAIR = 2 * HEAD_DIM


def _attn_mask():
    kj = lax.broadcasted_iota(jnp.int32, (ATTN_BLOCK, GROUP_LANES), 0)
    qi = lax.broadcasted_iota(jnp.int32, (ATTN_BLOCK, GROUP_LANES), 1) & (ATTN_BLOCK - 1)
    return kj <= qi


def _heads_transposed(ref, g, scale=None):
    parts = []
    for a in range(GROUP // 2):
        lo = (g * GROUP // 2 + a) * PAIR
        pair = ref[:, lo:lo + PAIR].astype(F32)
        pair = (pair if scale is None else pair * scale).T
        parts += [pair[:HEAD_DIM], pair[HEAD_DIM:]]
    return jnp.concatenate(parts, axis=1).astype(MXU_DTYPE)


def _heads_back(ref, g, vt):
    for a in range(GROUP // 2):
        lo = (g * GROUP // 2 + a) * PAIR
        pair = jnp.concatenate([vt[:, (2 * a) * ATTN_BLOCK:(2 * a + 1) * ATTN_BLOCK],
                                vt[:, (2 * a + 1) * ATTN_BLOCK:(2 * a + 2) * ATTN_BLOCK]], axis=0)
        ref[:, lo:lo + PAIR] = pair.T.astype(ref.dtype)


def _kv_parts(kv_ref, g):
    ks = slice(g * HEAD_DIM, (g + 1) * HEAD_DIM)
    vs = slice(KV_WIDTH + g * HEAD_DIM, KV_WIDTH + (g + 1) * HEAD_DIM)
    return kv_ref[:, ks].astype(MXU_DTYPE), kv_ref[:, vs].astype(MXU_DTYPE)


def _sink_rows(sinks):
    return jnp.repeat(sinks.reshape(KV_HEADS, GROUP), ATTN_BLOCK, axis=1)


def _attn_fwd(pq, pkv, sinks, comm=None):
    T = pq.shape[0]
    nb = T // ATTN_BLOCK

    def body(q_ref, kvc_ref, kvp_ref, s_ref, y_ref, lse_ref):
        mask_c = _attn_mask()
        has_prev = pl.program_id(0) > 0
        early = []
        for g in range(KV_HEADS):
            (kc, vc), (kp, vp) = _kv_parts(kvc_ref, g), _kv_parts(kvp_ref, g)
            qt = _heads_transposed(q_ref, g, ATTN_SCALE)
            early.append((vc, vp, jnp.where(mask_c, _nn(kc, qt), jnp.where(has_prev, _nn(kp, qt), NEG_INF))))
        for g, (vc, vp, s) in enumerate(early):
            sink = s_ref[g:g + 1, :]
            m = jnp.maximum(jnp.max(s, axis=0, keepdims=True), sink)
            p = jnp.exp(s - m)
            den = jnp.sum(p, axis=0, keepdims=True) + jnp.exp(sink - m)
            pc = jnp.where(mask_c, p, 0.0)
            _heads_back(y_ref, g, (_tn(vc, pc) + _tn(vp, p - pc)) / den)
            lse = m + jnp.log(den)
            for i in range(GROUP):
                lse_ref[g * GROUP + i:g * GROUP + i + 1, :] = lse[:, i * ATTN_BLOCK:(i + 1) * ATTN_BLOCK]

    return _call(
        "attn_fwd", body, grid=(nb,),
        in_specs=[pl.BlockSpec((ATTN_BLOCK, D_MODEL), lambda n: (n, 0)),
                  pl.BlockSpec((ATTN_BLOCK, 2 * KV_WIDTH), lambda n: (n, 0)),
                  pl.BlockSpec((ATTN_BLOCK, 2 * KV_WIDTH), lambda n: (jnp.maximum(n - 1, 0), 0)),
                  pl.BlockSpec((KV_HEADS, GROUP_LANES), lambda n: (0, 0))],
        out_specs=[pl.BlockSpec((ATTN_BLOCK, D_MODEL), lambda n: (n, 0)),
                   pl.BlockSpec((Q_HEADS, ATTN_BLOCK), lambda n: (0, n))],
        out_shape=[jax.ShapeDtypeStruct((T, D_MODEL), MXU_DTYPE), jax.ShapeDtypeStruct((Q_HEADS, T), F32)],
        args=[pq, pkv, pkv, _sink_rows(sinks)], semantics=("parallel",), comm=comm)


def _attn_bwd(pq, pkv, sinks, lse, dy, d_in, comm=None):
    T = pq.shape[0]
    nb = T // ATTN_BLOCK
    cur = lambda n: (jnp.minimum(n, nb - 1), 0)
    done = D_MODEL + 2 * KV_WIDTH

    def body(q_ref, kvc_ref, kvp_ref, s_ref, lse_ref, dy_ref, _, out_ref, ds_ref, carry, top, bot, dq_ref):
        n = pl.program_id(0)

        @pl.when(n == 0)
        def _():
            carry[...] = jnp.zeros_like(carry)
            dq_ref[...] = jnp.zeros_like(dq_ref)
            ds_ref[...] = jnp.zeros_like(ds_ref)

        out_ref[:, :D_MODEL] = dq_ref[...]

        @pl.when(n < nb)
        def _():
            mask_c = _attn_mask()
            valid = jnp.logical_or(mask_c, n > 0)
            early = []
            for g in range(KV_HEADS):
                (kc, vc), (kp, vp) = _kv_parts(kvc_ref, g), _kv_parts(kvp_ref, g)
                qt = _heads_transposed(q_ref, g, ATTN_SCALE)
                dot = _heads_transposed(dy_ref, g)
                early.append((kc, kp, qt, dot, jnp.where(mask_c, _nn(kc, qt), _nn(kp, qt)),
                              jnp.where(mask_c, _nn(vc, dot), _nn(vp, dot))))
            for g, (kc, kp, qt, dot, s, dp) in enumerate(early):
                ks = slice(g * HEAD_DIM, (g + 1) * HEAD_DIM)
                vs = slice(KV_WIDTH + g * HEAD_DIM, KV_WIDTH + (g + 1) * HEAD_DIM)
                lse = jnp.concatenate([lse_ref[g * GROUP + i:g * GROUP + i + 1, :] for i in range(GROUP)], axis=1)
                p = jnp.where(valid, jnp.exp(s - lse), 0.0)
                delta = jnp.sum(p * dp, axis=0, keepdims=True)
                ds = p * (dp - delta)
                ds_c, p_c = jnp.where(mask_c, ds, 0.0), jnp.where(mask_c, p, 0.0)
                ds_p, p_p = ds - ds_c, p - p_c
                _heads_back(dq_ref, g, (_tn(kc, ds_c) + _tn(kp, ds_p)) * ATTN_SCALE)
                bot[:, ks], bot[:, vs] = _nt(ds_c, qt), _nt(p_c, dot)
                top[:, ks], top[:, vs] = _nt(ds_p, qt), _nt(p_p, dot)
                ds_ref[g:g + 1, :] -= jnp.exp(s_ref[g:g + 1, :] - lse) * delta
            out_ref[:, D_MODEL:] = (carry[...] + top[...]).astype(out_ref.dtype)
            carry[...] = bot[...]

        @pl.when(n == nb)
        def _():
            out_ref[:, D_MODEL:] = carry[...].astype(out_ref.dtype)

    return _call(
        "attn_bwd", body, grid=(nb + 1,),
        in_specs=[pl.BlockSpec((ATTN_BLOCK, D_MODEL), cur),
                  pl.BlockSpec((ATTN_BLOCK, 2 * KV_WIDTH), cur),
                  pl.BlockSpec((ATTN_BLOCK, 2 * KV_WIDTH), lambda n: (jnp.maximum(jnp.minimum(n, nb - 1) - 1, 0), 0)),
                  pl.BlockSpec((KV_HEADS, GROUP_LANES), lambda n: (0, 0)),
                  pl.BlockSpec((Q_HEADS, ATTN_BLOCK), lambda n: (0, jnp.minimum(n, nb - 1))),
                  pl.BlockSpec((ATTN_BLOCK, D_MODEL), cur), HBM_SPEC],
        out_specs=[pl.BlockSpec((ATTN_BLOCK, done), lambda n: (jnp.maximum(n - 1, 0), 0)),
                   pl.BlockSpec((KV_HEADS, GROUP_LANES), lambda n: (0, 0))],
        out_shape=[jax.ShapeDtypeStruct(d_in.shape, d_in.dtype), jax.ShapeDtypeStruct((KV_HEADS, GROUP_LANES), F32)],
        scratch=[pltpu.VMEM((ATTN_BLOCK, 2 * KV_WIDTH), F32)] * 3 + [pltpu.VMEM((ATTN_BLOCK, D_MODEL), MXU_DTYPE)],
        args=[pq, pkv, pkv, _sink_rows(sinks), lse, dy, d_in], semantics=("arbitrary",), comm=comm, aliases={6: 0})


def _lower_bound(l):
    m = jnp.maximum(l[0:1], l[1:2])
    e0, e1 = jnp.exp(l[0:1] - m), jnp.exp(l[1:2] - m)
    return e0 / (e0 + e1)


def _tri(lower):
    r = lax.broadcasted_iota(jnp.int32, (CHUNK, CHUNK), 0)
    c = lax.broadcasted_iota(jnp.int32, (CHUNK, CHUNK), 1)
    return (r >= c) if lower else (c >= r)


def _chunk_sum(mask, v):
    ones = mask.astype(BF16)
    hi = v.astype(BF16)
    rest = v - hi.astype(F32)
    mid = rest.astype(BF16)
    lo = (rest - mid.astype(F32)).astype(BF16)
    part = lambda t: lax.dot_general(ones, t, (((1,), (0,)), ((), ())), preferred_element_type=F32)
    return part(hi) + part(mid) + part(lo)


def _hgrn_chunk_inputs(hq, hf, lb, causal):
    half_t = 0.5 * jnp.tanh(0.5 * hf)
    sg, sgn = 0.5 + half_t, 0.5 - half_t
    f = lb + (1.0 - lb) * sg
    kk = (1.0 - lb) * sgn
    sq = _sigmoid(hq)
    q = hq * sq
    b = _chunk_sum(causal, jnp.log(f))
    bm, bl = b[CHUNK // 2 - 1:CHUNK // 2, :], b[CHUNK - 1:CHUNK, :]
    e_qm, e_km = jnp.exp(b - bm), jnp.exp(bm - b)
    e_qs, e_kl = e_qm * jnp.exp(bm), e_km * jnp.exp(bl - bm)
    return dict(sg=sg, sgn=sgn, f=f, kk=kk, sq=sq, q=q, e_qm=e_qm, e_km=e_km, e_qs=e_qs, e_kl=e_kl,
                qm=q * e_qm, km=kk * e_km, qs=q * e_qs, kl=kk * e_kl, el=jnp.exp(bl))


def _hgrn_fwd(ph, lb_logits, norm_g, comm=None):
    T = ph.shape[0]
    nblk, cpb = T // HGRN_TOKENS, HGRN_TOKENS // CHUNK
    col = lambda c: pl.BlockSpec((HGRN_TOKENS, D_MODEL), functools.partial(lambda i, c: (i, c), c=c))

    def body(hq_ref, hf_ref, hi_ref, hg_ref, l_ref, ng_ref, y_ref, o_ref, st_ref, s_ref):
        @pl.when(pl.program_id(0) == 0)
        def _():
            s_ref[...] = jnp.zeros_like(s_ref)

        lb = _lower_bound(l_ref[...])
        causal = _tri(True)
        for c in range(cpb):
            rows = slice(c * CHUNK, (c + 1) * CHUNK)
            t = _hgrn_chunk_inputs(hq_ref[rows, :], hf_ref[rows, :], lb, causal)
            qm, km, qs, kl = (t[n].astype(MXU_DTYPE) for n in ("qm", "km", "qs", "kl"))
            v = hi_ref[rows, :].astype(MXU_DTYPE)
            heads = [slice(h * HGRN_K, (h + 1) * HGRN_K) for h in range(HGRN_HEADS)]
            a_all = [jnp.where(causal, _nt(qm[:, ls], km[:, ls]), 0.0).astype(MXU_DTYPE) for ls in heads]
            for h, ls in enumerate(heads):
                st = s_ref[h]
                st_ref[c, ls, :] = st
                o_ref[rows, ls] = _nn(a_all[h], v[:, ls]) + _nt(qs[:, ls], st)
                s_ref[h] = t["el"][:, ls] * st + _tn(v[:, ls], kl[:, ls])
        for h in range(HGRN_HEADS):
            ls = slice(h * HGRN_K, (h + 1) * HGRN_K)
            o = o_ref[:, ls]
            r = lax.rsqrt(jnp.mean(o * o, axis=-1, keepdims=True) + EPS)
            y_ref[:, ls] = (o * r * ng_ref[:, ls] * _sigmoid(hg_ref[:, ls])).astype(y_ref.dtype)

    return _call(
        "hgrn_fwd", body, grid=(nblk,),
        in_specs=[col(0), col(1), col(2), col(3),
                  pl.BlockSpec((2, D_MODEL), lambda i: (0, 0)), pl.BlockSpec((1, D_MODEL), lambda i: (0, 0))],
        out_specs=[pl.BlockSpec((HGRN_TOKENS, D_MODEL), lambda i: (i, 0)),
                   pl.BlockSpec((HGRN_TOKENS, D_MODEL), lambda i: (i, 0)),
                   pl.BlockSpec((cpb, D_MODEL, HGRN_K), lambda i: (i, 0, 0))],
        out_shape=[jax.ShapeDtypeStruct((T, D_MODEL), MXU_DTYPE), jax.ShapeDtypeStruct((T, D_MODEL), F32),
                   jax.ShapeDtypeStruct((T // CHUNK, D_MODEL, HGRN_K), F32)],
        scratch=[pltpu.VMEM((HGRN_HEADS, HGRN_K, HGRN_K), F32)],
        args=[ph, ph, ph, ph, lb_logits, norm_g], semantics=("arbitrary",), comm=comm)


def _hgrn_bwd(ph, o_raw, states, dy, lb_logits, norm_g, d_in, first, comm=None):
    T = ph.shape[0]
    nblk, cpb = T // HGRN_TOKENS, HGRN_TOKENS // CHUNK
    rev = lambda i: nblk - 1 - i
    col = lambda c: pl.BlockSpec((HGRN_TOKENS, D_MODEL), functools.partial(lambda i, c: (rev(i), c), c=c))
    tok = pl.BlockSpec((HGRN_TOKENS, D_MODEL), lambda i: (rev(i), 0))

    def body(hq_ref, hf_ref, hi_ref, hg_ref, o_ref, st_ref, dy_ref, l_ref, ng_ref, _,
             dph_ref, dng_ref, dl_ref, dst_ref, dlb_ref, do_s, dqm_s, dkm_s, dqs_s, dkl_s, dv_s, del_s):
        i = pl.program_id(0)

        @pl.when(i == 0)
        def _():
            dst_ref[...] = jnp.zeros_like(dst_ref)
            dlb_ref[...] = jnp.zeros_like(dlb_ref)
            dng_ref[...] = jnp.zeros_like(dng_ref)

        lb = _lower_bound(l_ref[...])
        causal, anti = _tri(True), _tri(False)
        row = lax.broadcasted_iota(jnp.int32, (CHUNK, D_MODEL), 0)
        for c in reversed(range(cpb)):
            rows = slice(c * CHUNK, (c + 1) * CHUNK)
            hq = hq_ref[rows, :]
            t = _hgrn_chunk_inputs(hq, hf_ref[rows, :], lb, causal)
            sgg = _sigmoid(hg_ref[rows, :])
            dyv = dy_ref[rows, :]
            for h in range(HGRN_HEADS):
                ls = slice(h * HGRN_K, (h + 1) * HGRN_K)
                o = o_ref[rows, ls]
                r = lax.rsqrt(jnp.mean(o * o, axis=-1, keepdims=True) + EPS)
                nrm = o * r
                g_h = sgg[:, ls]
                dph_ref[rows, 3 * D_MODEL + h * HGRN_K:3 * D_MODEL + (h + 1) * HGRN_K] = (
                    dyv[:, ls] * nrm * ng_ref[:, ls] * g_h * (1.0 - g_h)).astype(dph_ref.dtype)
                dyg = dyv[:, ls] * g_h
                dng_ref[:, ls] += jnp.sum(dyg * nrm, axis=0, keepdims=True)
                dn = dyg * ng_ref[:, ls]
                do_s[:, ls] = r * (dn - nrm * jnp.mean(dn * nrm, axis=-1, keepdims=True))
            qm, km, qs, kl = (t[n].astype(MXU_DTYPE) for n in ("qm", "km", "qs", "kl"))
            v = hi_ref[rows, :].astype(MXU_DTYPE)
            do = do_s[...].astype(MXU_DTYPE)
            heads = [slice(h * HGRN_K, (h + 1) * HGRN_K) for h in range(HGRN_HEADS)]
            a_all = [jnp.where(causal, _nt(qm[:, ls], km[:, ls]), 0.0).astype(MXU_DTYPE) for ls in heads]
            da_all = [jnp.where(causal, _nt(do[:, ls], v[:, ls]), 0.0).astype(MXU_DTYPE) for ls in heads]
            for h, ls in enumerate(heads):
                st = st_ref[c, ls, :]
                dst = dst_ref[h]
                a, da = a_all[h], da_all[h]
                dv_s[:, ls] = _tn(a, do[:, ls]) + _nt(kl[:, ls], dst)
                dkl_s[:, ls] = _nn(v[:, ls], dst)
                dqs_s[:, ls] = _nn(do[:, ls], st)
                del_s[:, ls] = jnp.sum(dst * st, axis=0, keepdims=True)
                dst_ref[h] = _tn(do[:, ls], qs[:, ls]) + t["el"][:, ls] * dst
                dqm_s[:, ls] = _nn(da, km[:, ls])
                dkm_s[:, ls] = _tn(da, qm[:, ls])
            dqm, dkm, dqs, dkl = dqm_s[...], dkm_s[...], dqs_s[...], dkl_s[...]
            dq = dqm * t["e_qm"] + dqs * t["e_qs"]
            dk = dkm * t["e_km"] + dkl * t["e_kl"]
            t_qm, t_km, t_kl = dqm * t["qm"], dkm * t["km"], dkl * t["kl"]
            db = t_qm - t_km + dqs * t["qs"] - t_kl
            db_mid = jnp.sum(t_km - t_qm, axis=0, keepdims=True)
            db_last = jnp.sum(t_kl, axis=0, keepdims=True) + del_s[...] * t["el"]
            db = db + jnp.where(row == CHUNK // 2 - 1, db_mid, 0.0) + jnp.where(row == CHUNK - 1, db_last, 0.0)
            dlogf = _chunk_sum(anti, db)
            sq, sg, sgn, f = t["sq"], t["sg"], t["sgn"], t["f"]
            dph_ref[rows, 0:D_MODEL] = (dq * (sq * (1.0 + hq * (1.0 - sq)))).astype(dph_ref.dtype)
            dph_ref[rows, D_MODEL:2 * D_MODEL] = (
                dlogf * (1.0 - lb) * sg * (1.0 - sg) / f - dk * (1.0 - lb) * sgn * (1.0 - sgn)).astype(dph_ref.dtype)
            dph_ref[rows, 2 * D_MODEL:3 * D_MODEL] = dv_s[...].astype(dph_ref.dtype)
            dlb_ref[...] += jnp.sum(dlogf * (1.0 - sg) / f - dk * sgn, axis=0, keepdims=True)

        @pl.when(i == nblk - 1)
        def _():
            dl0 = dlb_ref[...] * lb * (1.0 - lb)
            dl_ref[0:1, :] = dl0
            dl_ref[1:2, :] = -dl0

    wide = pltpu.VMEM((CHUNK, D_MODEL), F32)
    return _call(
        "hgrn_bwd", body, grid=(nblk,),
        in_specs=[col(0), col(1), col(2), col(3), tok,
                  pl.BlockSpec((cpb, D_MODEL, HGRN_K), lambda i: (rev(i), 0, 0)), tok,
                  pl.BlockSpec((2, D_MODEL), lambda i: (0, 0)), pl.BlockSpec((1, D_MODEL), lambda i: (0, 0)), HBM_SPEC],
        out_specs=[pl.BlockSpec((pl.Element(HGRN_TOKENS), pl.Element(4 * D_MODEL)), lambda i: (
                       pl.multiple_of(rev(i) * HGRN_TOKENS, ROW_ALIGN), first)),
                   pl.BlockSpec((1, D_MODEL), lambda i: (0, 0)), pl.BlockSpec((2, D_MODEL), lambda i: (0, 0))],
        out_shape=[jax.ShapeDtypeStruct(d_in.shape, d_in.dtype), jax.ShapeDtypeStruct((1, D_MODEL), F32),
                   jax.ShapeDtypeStruct((2, D_MODEL), F32)],
        scratch=[pltpu.VMEM((HGRN_HEADS, HGRN_K, HGRN_K), F32), pltpu.VMEM((1, D_MODEL), F32),
                 wide, wide, wide, wide, wide, wide, pltpu.VMEM((1, D_MODEL), F32)],
        args=[ph, ph, ph, ph, o_raw, states, dy, lb_logits, norm_g, d_in], semantics=("arbitrary",), comm=comm,
        aliases={9: 0})


def _local_step(x, target, vec, net):
    T, D = x.shape
    norm_mix_g, b_in, sinks, lb_logits = vec["norm_mix_g"], vec["b_in"], vec["attn_sinks"], vec["hgrn_lb_logits"]
    hgrn_norm_g, norm_ffn_g, norm_final_g = vec["hgrn_norm_g"], vec["norm_ffn_g"], vec["norm_final_g"]
    w_in = net.full("w_in")
    o_q, o_kv, o_h, o_g = (sum(IN_SPLITS[:i]) for i in range(4))
    TM = 512

    names = ("w_ffn_gate",)
    (u, pq, pkv, ph, pg), got = _in_proj(x, norm_mix_g, w_in, b_in, tm=256, comm=net.gather(names))
    net.gathered(names, got)
    names = ("w_branch_attn", "w_branch_hgrn")
    (y_attn, lse), got = _attn_fwd(pq, pkv, sinks, comm=net.gather(names))
    net.gathered(names, got)
    names = ("w_ffn_up",)
    (y_hgrn, o_raw, states), got = _hgrn_fwd(ph, lb_logits, hgrn_norm_g, comm=net.gather(names))
    net.gathered(names, got)
    w_ba, w_bh = net.full("w_branch_attn"), net.full("w_branch_hgrn")
    w_gate, w_up = net.full("w_ffn_gate"), net.full("w_ffn_up")
    names = ("w_out",)
    (ya, yb, merged), got = _merge_fwd(y_attn, y_hgrn, w_ba, w_bh, pg, tm=TM, comm=net.gather(names))
    net.gathered(names, got)
    w_out = net.full("w_out")

    names = ("w_ffn_down",)
    (h1, u2, gpre, up, z), got = _ffn_fwd(merged, w_out, x, norm_ffn_g, w_gate, w_up, tm=256,
                                          comm=net.gather(names))
    net.gathered(names, got)
    w_down = net.full("w_ffn_down")

    dh2, dh2b, dgp, dup, dgf_p, loss_p = _ffn_tail(z, w_down, h1, target, norm_final_g, gpre, up, tm=512)
    loss = jnp.sum(loss_p.reshape(-1, 8, D)[:, 0, 0])
    d_norm_final = _colsum_partials(dgf_p)
    d_w_down = _weight_grad("dw_down", z, dh2b, tm=DW_ROWS, tk=T)

    names, swap = ("w_ffn_down",), ()
    (dh1, dh1b, dg2_p), got = _ffn_in_bwd(dgp, dup, w_gate, w_up, h1, norm_ffn_g, dh2, tm=256,
                                          comm=net.exchange(dict(w_ffn_down=[d_w_down]), swap))
    net.received(names, swap, got)
    d_norm_ffn = _colsum_partials(dg2_p)
    d_w_gate = _weight_grad("dw_gate", dgp, u2, tm=DW_ROWS, tk=T)
    d_w_up = _weight_grad("dw_up", dup, u2, tm=DW_ROWS, tk=T)

    rows_in = sum(IN_SPLITS)
    dya, dyb, dy_attn, dy_hgrn, d_in = _merge_bwd(dh1b, w_out, w_ba, w_bh, ya, yb, pg, tm=TM, d_in_width=rows_in,
                                                  first=o_g)
    d_w_out = _weight_grad("dw_out", merged, dh1b, tm=1024, tk=1024)
    d_w_ba = _weight_grad("dw_branch_a", y_attn, dya, tm=1024, tk=1024)
    d_w_bh = _weight_grad("dw_branch_b", y_hgrn, dyb, tm=1024, tk=1024)

    names, swap = ("w_ffn_gate",), ("w_ffn_down",)
    (d_in, dsink), got = _attn_bwd(pq, pkv, sinks, lse, dy_attn, d_in,
                                   comm=net.exchange(dict(w_ffn_gate=[d_w_gate]), swap))
    net.received(names, swap, got)
    names, swap = ("w_ffn_up", "w_out"), ("w_ffn_gate",)
    (d_in, d_hgrn_norm, d_lb_logits), got = _hgrn_bwd(
        ph, o_raw, states, dy_hgrn, lb_logits, hgrn_norm_g, d_in, o_h,
        comm=net.exchange(dict(w_ffn_up=[d_w_up], w_out=[d_w_out]), swap))
    net.received(names, swap, got)

    names, swap = ("w_branch_attn", "w_branch_hgrn"), ("w_ffn_up", "w_out")
    (*d_w_in, d_b_in), got = _weight_grad(
        "dw_in", d_in, u, tm=DW_ROWS, tk=T, a_colsum=True, carrying=True,
        comm=net.exchange(dict(w_branch_attn=[d_w_ba], w_branch_hgrn=[d_w_bh]), swap))
    net.received(names, swap, got)
    d_w_in = [tuple(d_w_in)]

    first_level = net.presum_begin("w_in", d_w_in)
    halves = net.presum_end("w_in", [] if first_level is None else _copies_alone("presum_swap_w_in", first_level))
    names, swap = ("w_in",), ("w_branch_attn", "w_branch_hgrn")
    (dx, dg1_p), got = _in_proj_bwd([(d_in, 0)], w_in, x, norm_mix_g, dh1, tm=256,
                                    comm=_join(halves, net.swap(swap)))
    net.last = (names, swap, got)
    d_norm_mix = _colsum_partials(dg1_p)
    vecs = dict(norm_mix_g=d_norm_mix, b_in=d_b_in, attn_sinks=jnp.sum(dsink.reshape(Q_HEADS, ATTN_BLOCK), axis=1).reshape(1, Q_HEADS),
                hgrn_lb_logits=d_lb_logits,
                hgrn_norm_g=d_hgrn_norm, norm_ffn_g=d_norm_ffn, norm_final_g=d_norm_final)
    return loss, dx, vecs


def _place():
    return lax.axis_index("x"), lax.axis_index("y"), lax.axis_index("c")


def _other_chips(x, y):
    return [(1 - x, y), (x, 1 - y), (1 - x, 1 - y)]


def _y_first(copies):
    return [copies[3 * (i // 3) + (1, 0, 2)[i % 3]] for i in range(len(copies))]


def _gather_copies(shards):
    n = len(shards)

    def build(ins, outs, send_sems, recv_sems, local_sems):
        x, y, c = _place()
        mine = 2 * x + y
        local = [pltpu.make_async_copy(ins[w], outs[w].at[mine], local_sems.at[w]) for w in range(n)]
        sends, recvs = [], []
        for w in range(n):
            for k, (px, py) in enumerate(_other_chips(x, y)):
                sem = 3 * w + k
                sends.append(pltpu.make_async_remote_copy(
                    src_ref=ins[w], dst_ref=outs[w].at[mine], send_sem=send_sems.at[sem], recv_sem=recv_sems.at[sem],
                    device_id=(px, py, c), device_id_type=MESH_ID))
                recvs.append(pltpu.make_async_remote_copy(
                    src_ref=ins[w], dst_ref=outs[w].at[2 * px + py], send_sem=send_sems.at[sem],
                    recv_sem=recv_sems.at[sem], device_id=(px, py, c), device_id_type=MESH_ID))
        return sends, recvs, local, _y_first(sends)

    return _Carried(shards, [jax.ShapeDtypeStruct((N_CHIPS,) + s.shape, s.dtype) for s in shards], 3 * n, n, build)


def _grad_copies(stacked):
    n = len(stacked)

    def build(ins, outs, send_sems, recv_sems, local_sems):
        x, y, c = _place()
        sends = []
        for w in range(n):
            for k, (px, py) in enumerate(_other_chips(x, y)):
                sem = 3 * w + k
                sends.append(pltpu.make_async_remote_copy(
                    src_ref=ins[w].at[2 * px + py], dst_ref=outs[w].at[k], send_sem=send_sems.at[sem],
                    recv_sem=recv_sems.at[sem], device_id=(px, py, c), device_id_type=MESH_ID))
        return sends, sends, [], _y_first(sends)

    return _Carried(stacked, [jax.ShapeDtypeStruct((3,) + s.shape[1:], s.dtype) for s in stacked], 3 * n, 0, build)


def _small_copies(small):
    def build(ins, outs, send_sems, recv_sems, local_sems):
        small_ref, all_ref = ins[0], outs[0]
        x, y, c = _place()
        me = 4 * x + 2 * y + c
        sends, recvs = [], []
        for r in range(1, 8):
            px = 1 - x if r & 4 else x
            py = 1 - y if r & 2 else y
            pc = 1 - c if r & 1 else c
            sends.append(pltpu.make_async_remote_copy(
                src_ref=small_ref, dst_ref=all_ref.at[me], send_sem=send_sems.at[r - 1], recv_sem=recv_sems.at[r - 1],
                device_id=(px, py, pc), device_id_type=MESH_ID))
            recvs.append(pltpu.make_async_remote_copy(
                src_ref=small_ref, dst_ref=all_ref.at[4 * px + 2 * py + pc], send_sem=send_sems.at[r - 1],
                recv_sem=recv_sems.at[r - 1], device_id=(px, py, pc), device_id_type=MESH_ID))
        return sends, recvs, [pltpu.make_async_copy(small_ref, all_ref.at[me], local_sems.at[0])]

    return _Carried([small], [jax.ShapeDtypeStruct((8,) + small.shape, small.dtype)], 7, 1, build)


def _gather_by_neighbours(name, shard):
    half = shard.shape[0] // 2
    quarter = half // 2

    def body(in_ref, out_ref, send_sems, recv_sems, local_sem, stage_ref):
        for core in (0, 1):
            @pl.when(lax.axis_index("c") == core)
            def _():
                program(core, in_ref, out_ref, send_sems, recv_sems, local_sem, stage_ref)

    def program(c, in_ref, out_ref, send_sems, recv_sems, local_sem, stage_ref):
        x, y, _ = _place()
        chip = lambda px, py: 2 * px + py
        to_x, to_y, sibling = (1 - x, y, c), (x, 1 - y, c), (x, y, 1 - c)
        x_blk, y_blk, d_blk = chip(1 - x, y), chip(x, 1 - y), chip(1 - x, 1 - y)
        mine, theirs = c * half, (1 - c) * half

        def copy(sem, rows, block, to, src=None):
            place = out_ref.at[block, pl.ds(rows[0], rows[1])]
            return pltpu.make_async_remote_copy(
                src_ref=place if src is None else src, dst_ref=place, send_sem=send_sems.at[sem],
                recv_sem=recv_sems.at[sem], device_id=to, device_id_type=MESH_ID)

        stage = pltpu.make_async_copy(in_ref, stage_ref, local_sem.at[0])
        own = pltpu.make_async_copy(stage_ref, out_ref.at[chip(x, y)], local_sem.at[1])
        my_rows = in_ref.at[pl.ds(mine, half)]
        along_x = dict(send=copy(0, (mine, half), chip(x, y), to_x, src=my_rows),
                       landed=copy(0, (mine, half), x_blk, to_x),
                       onward=[copy(3, (mine + quarter, quarter), x_blk, to_y), copy(4, (mine, half), x_blk, sibling)],
                       diagonal=copy(2, (mine, quarter), d_blk, to_x))
        along_y = dict(send=copy(1, (mine, half), chip(x, y), to_y, src=my_rows),
                       landed=copy(1, (mine, half), y_blk, to_y),
                       onward=[copy(2, (mine, quarter), y_blk, to_x), copy(5, (mine, half), y_blk, sibling)],
                       diagonal=copy(3, (mine + quarter, quarter), d_blk, to_y))
        last = copy(6, (mine, half), d_blk, sibling)

        order = (along_x, along_y) if c == 0 else (along_y, along_x)
        for axis in order:
            axis["send"].start()
        stage.start()
        stage.wait()
        own.start()
        for axis in order:
            axis["landed"].wait_recv()
            for cp in axis["onward"]:
                cp.start()
        for axis in order:
            axis["diagonal"].wait_recv()
        last.start()
        for sem, block in ((4, x_blk), (5, y_blk), (6, d_blk)):
            copy(sem, (theirs, half), block, sibling).wait_recv()
        for cp in [along_x["send"], along_y["send"]] + along_x["onward"] + along_y["onward"] + [last]:
            cp.wait_send()
        own.wait()

    return pl.pallas_call(
        body, name=name, in_specs=[HBM_SPEC], out_specs=HBM_SPEC,
        out_shape=jax.ShapeDtypeStruct((N_CHIPS,) + shard.shape, shard.dtype),
        scratch_shapes=[pltpu.SemaphoreType.DMA((7,)), pltpu.SemaphoreType.DMA((7,)), pltpu.SemaphoreType.DMA((2,)),
                        pltpu.VMEM(shard.shape, shard.dtype)],
    )(shard)


def _copies_alone(name, comm):
    return _call(name, lambda: None, grid=(), in_specs=[], out_specs=[], out_shape=[], args=[], comm=comm)[1]


class _Net:
    def __init__(self, shards):
        self.shards = shards
        self.whole, self.own, self.theirs, self.sums, self.other = {}, {}, {}, {}, {}
        x, y, _ = _place()
        self.chip = 2 * x + y

    def gather(self, names):
        return _gather_copies([self.shards[n] for n in names])

    def gathered(self, names, got):
        for n, g in zip(names, got):
            self.whole[n] = g.reshape(-1, g.shape[-1])

    def full(self, name):
        return self.whole[name]

    def exchange(self, grads, swap=()):
        stacked = []
        for n, pieces in grads.items():
            (keep, send), = pieces
            self.own[n] = keep
            stacked.append(send.reshape(N_CHIPS, keep.shape[0] // N_CHIPS, send.shape[-1]))
        return _join(_grad_copies(stacked), self.swap(swap))

    def swap(self, names):
        return _sibling_copies([self.sums[n] for n in names]) if names else None

    def presum_begin(self, name, pieces):
        keep = jnp.concatenate([p[0] for p in pieces], axis=0) if len(pieces) > 1 else pieces[0][0]
        send = jnp.concatenate([p[1] for p in pieces], axis=0) if len(pieces) > 1 else pieces[0][1]
        rows = keep.shape[0] // N_CHIPS
        self.held = keep.reshape(N_CHIPS, rows, keep.shape[-1])
        return _half_rows_copies(send.reshape(N_CHIPS, rows, send.shape[-1]))

    def presum_end(self, name, got):
        x, y, c = _place()
        to_send, self.own[name] = _pre_sum("presum_" + name, self.held, got[0], jnp.stack([c, self.chip]))
        return _grad_copies([to_send])

    def received(self, names, swap, got, carried=None):
        self.theirs.update(zip(names, got[:len(names)]))
        self.other.update(zip(swap, got[len(names):]))
        for n in names:
            (self.sums[n],), more = _partial_sum("sum_" + n, self.own[n], self.theirs[n], self.chip, comm=carried)
        return more


def _half_rows_copies(stacked):
    n, rows = stacked.shape[0], stacked.shape[1] // 2

    def build(ins, outs, send_sems, recv_sems, local_sems):
        x, y, c = _place()
        copies = [pltpu.make_async_remote_copy(
            src_ref=ins[0].at[s, pl.ds((1 - c) * rows, rows)], dst_ref=outs[0].at[s], send_sem=send_sems.at[s],
            recv_sem=recv_sems.at[s], device_id=(x, y, 1 - c), device_id_type=MESH_ID) for s in range(n)]
        return copies, copies, []

    return _Carried([stacked], [jax.ShapeDtypeStruct((n, rows, stacked.shape[2]), stacked.dtype)], n, 0, build)


def _pre_sum(name, held, theirs, core_and_chip):
    n, R, C = held.shape
    half = R // 2
    tr = _row_tile(half)
    per_half = half // tr

    def body(place_ref, h_ref, t_ref, send_ref, own_ref):
        total = h_ref[0] + t_ref[0].astype(F32)
        send_ref[0] = total.astype(send_ref.dtype)

        @pl.when(pl.program_id(1) == place_ref[1])
        def _():
            own_ref[...] = total

    return pl.pallas_call(
        body, name=name,
        grid_spec=pltpu.PrefetchScalarGridSpec(
            num_scalar_prefetch=1, grid=(per_half, n),
            in_specs=[pl.BlockSpec((1, tr, C), lambda i, s, place: (s, place[0] * per_half + i, 0)),
                      pl.BlockSpec((1, tr, C), lambda i, s, place: (s, i, 0))],
            out_specs=[pl.BlockSpec((1, tr, C), lambda i, s, place: (s, i, 0)),
                       pl.BlockSpec((tr, C), lambda i, s, place: (i, 0))]),
        out_shape=[jax.ShapeDtypeStruct((n, half, C), MXU_DTYPE), jax.ShapeDtypeStruct((half, C), F32)],
        compiler_params=_params(("arbitrary", "arbitrary")),
    )(core_and_chip, held, theirs)


def _sibling_copies(parts):
    n = len(parts)

    def build(ins, outs, send_sems, recv_sems, local_sems):
        x, y, c = _place()
        copies = [pltpu.make_async_remote_copy(
            src_ref=ins[w], dst_ref=outs[w], send_sem=send_sems.at[w], recv_sem=recv_sems.at[w],
            device_id=(x, y, 1 - c), device_id_type=MESH_ID) for w in range(n)]
        return copies, copies, []

    return _Carried(parts, [jax.ShapeDtypeStruct(p.shape, p.dtype) for p in parts], n, 0, build)


def _row_tile(rows, most=512, sublanes=16):
    return max(t for t in range(sublanes, min(most, rows // 2) + 1, sublanes) if rows % t == 0)


def _partial_sum(name, own, recv, chip, comm=None):
    _, R, C = recv.shape
    tr = _row_tile(R)

    def body(o_ref, r_ref, p_ref):
        p_ref[...] = ((o_ref[...] + r_ref[0].astype(F32)) + r_ref[1].astype(F32)) + r_ref[2].astype(F32)

    if own.shape[0] != R:
        assert comm is None and own.shape[0] == N_CHIPS * R
        total = pl.pallas_call(
            lambda chip_ref, *refs: body(*refs), name=name,
            grid_spec=pltpu.PrefetchScalarGridSpec(
                num_scalar_prefetch=1, grid=(R // tr,),
                in_specs=[pl.BlockSpec((tr, C), lambda i, chip_ref: (chip_ref[0] * (R // tr) + i, 0)),
                          pl.BlockSpec((3, tr, C), lambda i, chip_ref: (0, i, 0))],
                out_specs=pl.BlockSpec((tr, C), lambda i, chip_ref: (i, 0))),
            out_shape=jax.ShapeDtypeStruct((R, C), F32), compiler_params=_params(("parallel",)),
        )(chip.reshape(1), own, recv)
        return [total], []
    return _call(name, body, grid=(R // tr,),
                 in_specs=[pl.BlockSpec((tr, C), lambda i: (i, 0)), pl.BlockSpec((3, tr, C), lambda i: (0, i, 0))],
                 out_specs=[pl.BlockSpec((tr, C), lambda i: (i, 0))], out_shape=[jax.ShapeDtypeStruct((R, C), F32)],
                 args=[own, recv], semantics=("parallel",), comm=comm)


def _adam_vals(w, g, m, v):
    m = ADAM_B1 * m + (1.0 - ADAM_B1) * g
    v = ADAM_B2 * v + (1.0 - ADAM_B2) * (g * g)
    m_hat = m / (1.0 - ADAM_B1 ** ADAM_STEP)
    v_hat = v / (1.0 - ADAM_B2 ** ADAM_STEP)
    delta = -ADAM_LR * (m_hat / (jnp.sqrt(v_hat) + ADAM_EPS) + ADAM_WD * w)
    return delta, m, v


def _adamw(name, w, m, v, mine, other, comm=None):
    R, C = w.shape
    tr = _row_tile(R)

    def body(w_ref, m_ref, v_ref, s_ref, n_ref, g_ref, d_ref, nm_ref, nv_ref):
        g = s_ref[...] + n_ref[...]
        d, nm, nv = _adam_vals(w_ref[...], g, m_ref[...], v_ref[...])
        g_ref[...], d_ref[...], nm_ref[...], nv_ref[...] = g, d, nm, nv

    spec = pl.BlockSpec((tr, C), lambda i: (i, 0))
    return _call(name, body, grid=(R // tr,), in_specs=[spec] * 5, out_specs=[spec] * 4,
                 out_shape=[jax.ShapeDtypeStruct((R, C), F32)] * 4, args=[w, m, v, mine, other],
                 semantics=("parallel",), comm=comm)


def _adamw_by_halves(name, w, m, v, mine, other, core):
    R, C = w.shape
    tr = _row_tile(R // 2)
    per_half = R // 2 // tr

    def body(c_ref, w_ref, m_ref, v_ref, s_ref, n_ref, g_ref, d_ref, nm_ref, nv_ref):
        g = jnp.where(pl.program_id(0) // per_half == c_ref[0, 0], s_ref[...], n_ref[...])
        d, nm, nv = _adam_vals(w_ref[...], g, m_ref[...], v_ref[...])
        g_ref[...], d_ref[...], nm_ref[...], nv_ref[...] = g, d, nm, nv

    spec = pl.BlockSpec((tr, C), lambda i: (i, 0))
    part = pl.BlockSpec((tr, C), lambda i: (i % per_half, 0))
    return pl.pallas_call(
        body, name=name, grid=(R // tr,),
        in_specs=[pl.BlockSpec(memory_space=pltpu.SMEM), spec, spec, spec, part, part], out_specs=[spec] * 4,
        out_shape=[jax.ShapeDtypeStruct((R, C), F32)] * 4, compiler_params=_params(("parallel",)),
    )(core, w, m, v, mine, other)


SMALL_LAYOUT = dict(norm_mix_g=(0, 1, 1024), b_in=(1, 8, 7424), hgrn_norm_g=(9, 1, 1024), norm_ffn_g=(10, 1, 1024),
                    norm_final_g=(11, 1, 1024), hgrn_lb_logits=(12, 2, 2048), attn_sinks=(14, 1, 16))
SMALL_LOSS_ROW, SMALL_ROWS = 15, 16


def _pack_small(grads, loss):
    rows = [jnp.pad(grads[name].astype(F32).reshape(-1), (0, nrows * D_MODEL - n))
            for name, (_, nrows, n) in SMALL_LAYOUT.items()]
    rows.append(jnp.pad(loss.astype(F32).reshape(1), (0, D_MODEL - 1)))
    return jnp.concatenate(rows).reshape(SMALL_ROWS, D_MODEL)


def _adamw_small(w, m, v, g_all):
    names = list(SMALL_LAYOUT)
    n = len(names)

    def body(a_ref, *refs):
        ins, outs = refs[:3 * n], refs[3 * n:]
        g_all_rows = a_ref[0]
        for dev in range(1, 8):
            g_all_rows = g_all_rows + a_ref[dev]
        for i, name in enumerate(names):
            first, nrows, count = SMALL_LAYOUT[name]
            w_ref, m_ref, v_ref = ins[3 * i:3 * i + 3]
            if w_ref.shape[0] == nrows:
                g = g_all_rows[first:first + nrows, :w_ref.shape[1]]
            else:
                last = count - (nrows - 1) * D_MODEL
                g = jnp.concatenate([g_all_rows[r:r + 1, :] for r in range(first, first + nrows - 1)]
                                    + [g_all_rows[first + nrows - 1:first + nrows, :last]], axis=1)
            d, nm, nv = _adam_vals(w_ref[...], g, m_ref[...], v_ref[...])
            for o_ref, val in zip(outs[4 * i:4 * i + 4], (g, d, nm, nv)):
                o_ref[...] = val
        outs[4 * n][...] = g_all_rows[SMALL_LOSS_ROW:SMALL_LOSS_ROW + 1, 0:1]

    res = pl.pallas_call(
        body, name="adamw_small",
        out_shape=[jax.ShapeDtypeStruct(w[name].shape, F32) for name in names for _ in range(4)]
        + [jax.ShapeDtypeStruct((1, 1), F32)],
    )(g_all, *[t[name] for name in names for t in (w, m, v)])
    return {name: res[4 * i:4 * i + 4] for i, name in enumerate(names)}, res[4 * n]


MATRICES = ("w_in", "w_branch_attn", "w_branch_hgrn", "w_out", "w_ffn_gate", "w_ffn_up", "w_ffn_down")
COLUMN_SHARDED = ("w_in", "w_ffn_gate", "w_ffn_up")
WEIGHTS = ("norm_mix_g", "w_in", "b_in", "attn_sinks", "hgrn_lb_logits", "hgrn_norm_g", "w_branch_attn",
           "w_branch_hgrn", "w_out", "norm_ffn_g", "w_ffn_gate", "w_ffn_up", "w_ffn_down", "norm_final_g")


def kernel(x, norm_mix_g, w_in, b_in, attn_sinks, hgrn_lb_logits, hgrn_norm_g, w_branch_attn, w_branch_hgrn, w_out, norm_ffn_g, w_ffn_gate, w_ffn_up, w_ffn_down, norm_final_g, loss_target, m_norm_mix_g, m_w_in, m_b_in, m_attn_sinks, m_hgrn_lb_logits, m_hgrn_norm_g, m_w_branch_attn, m_w_branch_hgrn, m_w_out, m_norm_ffn_g, m_w_ffn_gate, m_w_ffn_up, m_w_ffn_down, m_norm_final_g, v_norm_mix_g, v_w_in, v_b_in, v_attn_sinks, v_hgrn_lb_logits, v_hgrn_norm_g, v_w_branch_attn, v_w_branch_hgrn, v_w_out, v_norm_ffn_g, v_w_ffn_gate, v_w_ffn_up, v_w_ffn_down, v_norm_final_g):
    given = dict(locals())
    w = {n: given[n] for n in WEIGHTS}
    m = {n: given["m_" + n] for n in WEIGHTS}
    v = {n: given["v_" + n] for n in WEIGHTS}

    block = lambda a, n: jnp.transpose(a[0]) if n in COLUMN_SHARDED else a[0]
    unblock = lambda a, n: (jnp.transpose(a) if n in COLUMN_SHARDED else a)[None]
    net = _Net({n: block(w[n], n).astype(MXU_DTYPE) for n in MATRICES})
    net.gathered(("w_in",), [_gather_by_neighbours("gather_w_in", net.shards["w_in"])])
    vec = dict(norm_mix_g=norm_mix_g, b_in=b_in, attn_sinks=attn_sinks, hgrn_lb_logits=hgrn_lb_logits,
               hgrn_norm_g=hgrn_norm_g, norm_ffn_g=norm_ffn_g, norm_final_g=norm_final_g.reshape(1, D_MODEL))
    loss_part, dx, d_vecs = _local_step(x[0], loss_target[0], vec, net)

    small_all, = net.received(*net.last, carried=_small_copies(_pack_small(d_vecs, loss_part)))
    grads, deltas, new_m, new_v = {}, {}, {}, {}
    for n in ("w_ffn_down", "w_ffn_gate", "w_ffn_up", "w_out", "w_branch_attn", "w_branch_hgrn"):
        res, got = _adamw("adamw_" + n, block(w[n], n), block(m[n], n), block(v[n], n), net.sums[n], net.other[n],
                          comm=net.swap(("w_in",)) if n == "w_ffn_down" else None)
        if n == "w_ffn_down":
            net.other["w_in"], = got
        grads[n], deltas[n], new_m[n], new_v[n] = (unblock(r, n) for r in res)
    n = "w_in"
    res = _adamw_by_halves("adamw_" + n, block(w[n], n), block(m[n], n), block(v[n], n), net.sums[n], net.other[n],
                           _place()[2].reshape(1, 1))
    grads[n], deltas[n], new_m[n], new_v[n] = (unblock(r, n) for r in res)
    rows = lambda t: {n: t[n].reshape(-1, t[n].shape[-1]) for n in SMALL_LAYOUT}
    res, loss = _adamw_small(rows(w), rows(m), rows(v), small_all)
    for n, four in res.items():
        grads[n], deltas[n], new_m[n], new_v[n] = (r.reshape(w[n].shape) for r in four)
    loss = loss.reshape(())
    return (loss, dx[None], *[grads[n] for n in WEIGHTS], *[deltas[n] for n in WEIGHTS],
            *[new_m[n] for n in WEIGHTS], *[new_v[n] for n in WEIGHTS])
```

```python
import collections
import functools
import math

import jax
import jax.numpy as jnp
from jax import lax
from jax.experimental import pallas as pl
from jax.experimental.pallas import tpu as pltpu

F32 = jnp.float32
BF16 = jnp.bfloat16
MXU_DTYPE = jnp.bfloat16
SAVED_DTYPE = jnp.bfloat16
MESH_ID = pl.DeviceIdType.MESH

D_MODEL = 1024
HEAD_DIM = 64
Q_HEADS = 16
KV_HEADS = 2
GROUP = Q_HEADS // KV_HEADS
KV_WIDTH = KV_HEADS * HEAD_DIM
ATTN_BLOCK = 128
HGRN_HEADS = 8
HGRN_K = 128
CHUNK = 64
HGRN_TOKENS = 256
FFN = 2816
IN_SPLITS = (1024, 256, 4096, 2048)
EPS = 1e-6
NEG_INF = -1e30
ADAM_LR, ADAM_B1, ADAM_B2, ADAM_EPS, ADAM_WD, ADAM_STEP = 0.001, 0.9, 0.999, 1e-08, 0.01, 10
N_CHIPS = 4
VMEM_LIMIT = 60 * 1024 * 1024
ROW_ALIGN = 16
DW_ROWS = 256


def _params(sem=None):
    return pltpu.CompilerParams(dimension_semantics=sem, vmem_limit_bytes=VMEM_LIMIT)


def _sigmoid(v):
    return 0.5 * jnp.tanh(0.5 * v) + 0.5


def _dot(a, b, dims):
    return lax.dot_general(a.astype(MXU_DTYPE), b.astype(MXU_DTYPE), (dims, ((), ())),
                           preferred_element_type=F32)


def _nn(a, b):
    return _dot(a, b, ((1,), (0,)))


def _nt(a, b):
    return _dot(a, b, ((1,), (1,)))


def _tn(a, b):
    return _dot(a, b, ((0,), (0,)))


HBM_SPEC = pl.BlockSpec(memory_space=pl.ANY)


class _Carried:
    def __init__(self, arrays, out_shapes, n_remote, n_local, build, continued=None):
        self.continued = dict(continued or {})
        self.parts = [(len(arrays), len(out_shapes), build)]
        self.arrays, self.out_shapes = list(arrays), list(out_shapes)
        self.scratch = [pltpu.SemaphoreType.DMA((n_remote,)), pltpu.SemaphoreType.DMA((n_remote,)),
                        pltpu.SemaphoreType.DMA((max(n_local, 1),))]

    def __add__(self, other):
        both = _Carried([], [], 1, 0, None)
        both.parts = self.parts + other.parts
        both.arrays, both.out_shapes = self.arrays + other.arrays, self.out_shapes + other.out_shapes
        both.scratch = self.scratch + other.scratch
        both.continued = dict(self.continued)
        both.continued.update({len(self.arrays) + i: len(self.out_shapes) + o for i, o in other.continued.items()})
        return both

    def _built(self, ins, outs, sems):
        for p, (ni, no, build) in enumerate(self.parts):
            yield build(ins[:ni], outs[:no], *sems[3 * p:3 * p + 3])
            ins, outs = ins[ni:], outs[no:]

    def start(self, ins, outs, sems):
        core = lax.axis_index("c")
        for sends, _, local, *other_order in self._built(ins, outs, sems):
            for cp in local:
                cp.start()
            if not other_order:
                for cp in sends:
                    cp.start()
                continue

            @pl.when(core == 0)
            def _():
                for cp in sends:
                    cp.start()

            @pl.when(core == 1)
            def _():
                for cp in other_order[0]:
                    cp.start()

    def wait(self, ins, outs, sems):
        for sends, recvs, local, *_ in self._built(ins, outs, sems):
            for cp in recvs:
                cp.wait_recv()
            for cp in sends:
                cp.wait_send()
            for cp in local:
                cp.wait()


def _join(*comms):
    comms = [c for c in comms if c is not None]
    return functools.reduce(lambda a, b: a + b, comms) if comms else None


def _call(name, body, *, grid, in_specs, out_specs, out_shape, args, scratch=(), semantics=None, comm=None,
          aliases=None):
    n_in, n_out, n_scr = len(in_specs), len(out_specs), len(scratch)
    aliases = aliases or {}
    if comm is None:
        res = pl.pallas_call(body, name=name, grid=grid, in_specs=in_specs, out_specs=out_specs, out_shape=out_shape,
                             scratch_shapes=list(scratch), input_output_aliases=aliases,
                             compiler_params=_params(semantics))(*args)
        return list(res), []
    ci, co = len(comm.arrays), len(comm.out_shapes)
    aliases = dict(aliases)
    aliases.update({n_in + i: n_out + o for i, o in comm.continued.items()})

    def carrying(*refs):
        ins, refs = refs[:n_in], refs[n_in:]
        c_ins, refs = refs[:ci], refs[ci:]
        outs, refs = refs[:n_out], refs[n_out:]
        c_outs, refs = refs[:co], refs[co:]
        scr, sems = refs[:n_scr], refs[n_scr:]
        if not grid:
            comm.start(c_ins, c_outs, sems)
            body(*ins, *outs, *scr)
            comm.wait(c_ins, c_outs, sems)
            return
        first = functools.reduce(jnp.logical_and, [pl.program_id(a) == 0 for a in range(len(grid))])
        last = functools.reduce(jnp.logical_and, [pl.program_id(a) == g - 1 for a, g in enumerate(grid)])

        @pl.when(first)
        def _():
            comm.start(c_ins, c_outs, sems)

        body(*ins, *outs, *scr)

        @pl.when(last)
        def _():
            comm.wait(c_ins, c_outs, sems)

    res = pl.pallas_call(
        carrying, name=name, grid=grid, in_specs=list(in_specs) + [HBM_SPEC] * ci,
        out_specs=list(out_specs) + [HBM_SPEC] * co, out_shape=list(out_shape) + comm.out_shapes,
        scratch_shapes=list(scratch) + comm.scratch, input_output_aliases=aliases,
        compiler_params=_params(("arbitrary",) * len(grid) if grid else None),
    )(*args, *comm.arrays)
    return list(res[:n_out]), list(res[n_out:])


_Cols = collections.namedtuple("_Cols", "array first cols")
_Into = collections.namedtuple("_Into", "rows first held")


def _weight_grad(name, a, b, *, tm, tk, a_colsum=False, into=None, carrying=False, comm=None):
    cols = a if isinstance(a, _Cols) else _Cols(a, 0, a.shape[1])
    a = cols.array
    (T, N), M = b.shape, cols.cols
    tk = min(tk, T)
    assert cols.first % tm == 0 and M % tm == 0 and T % tk == 0, (name, cols.first, M, tm, T, tk)
    ni, nk, tile0 = M // tm, T // tk, cols.first // tm
    held = list(into.held) if into is not None and into.held is not None else []

    def body(a_ref, b_ref, *rest):
        keep_ref, send_ref = rest[len(held):len(held) + 2]
        sums_ref = rest[len(held) + 2] if a_colsum else None
        if nk == 1:
            acc = _tn(a_ref[...], b_ref[...])
            keep_ref[...], send_ref[...] = acc, acc.astype(send_ref.dtype)
            if a_colsum:
                sums_ref[...] = jnp.sum(a_ref[...].astype(F32), axis=0, keepdims=True)
            return
        acc_ref = rest[-1]
        k = pl.program_id(1)

        @pl.when(k == 0)
        def _():
            acc_ref[...] = jnp.zeros_like(acc_ref)
            if a_colsum:
                sums_ref[...] = jnp.zeros((1, tm), F32)

        if a_colsum:
            sums_ref[...] += jnp.sum(a_ref[...].astype(F32), axis=0, keepdims=True)
        acc_ref[...] += _tn(a_ref[...], b_ref[...])

        @pl.when(k == nk - 1)
        def _():
            keep_ref[...], send_ref[...] = acc_ref[...], acc_ref[...].astype(send_ref.dtype)

    if into is None:
        rows, out_spec = M, pl.BlockSpec((tm, N), lambda i, k: (i, 0))
    else:
        rows = into.rows
        out_spec = pl.BlockSpec((pl.Element(tm), pl.Element(N)),
                                lambda i, k: (pl.multiple_of(into.first + i * tm, ROW_ALIGN), 0))
    out_shape = [jax.ShapeDtypeStruct((rows, N), F32), jax.ShapeDtypeStruct((rows, N), MXU_DTYPE)]
    out_specs = [out_spec, out_spec]
    if a_colsum:
        out_shape.append(jax.ShapeDtypeStruct((1, M), F32))
        out_specs.append(pl.BlockSpec((1, tm), lambda i, k: (0, i)))
    res, got = _call(
        name, body, grid=(ni, nk),
        in_specs=[pl.BlockSpec((tk, tm), lambda i, k: (k, tile0 + i)), pl.BlockSpec((tk, N), lambda i, k: (k, 0))]
        + [HBM_SPEC] * len(held),
        out_specs=out_specs, out_shape=out_shape, scratch=[pltpu.VMEM((tm, N), F32)] if nk > 1 else [],
        args=[a, b] + held, aliases={2 + p: p for p in range(len(held))}, semantics=("parallel", "arbitrary"),
        comm=comm)
    return (res, got) if carrying else res


def _ffn_fwd(merged, w_out, x, gain, w_gate_t, w_up_t, *, tm, comm=None):
    (T, D), F = x.shape, w_gate_t.shape[0]

    def body(m_ref, wo_ref, x_ref, g_ref, wg_ref, wu_ref, h_ref, u_ref, gate_ref, up_ref, z_ref):
        h = x_ref[...] + _nn(m_ref[...], wo_ref[...])
        h_ref[...] = h
        u = (h * lax.rsqrt(jnp.mean(h * h, axis=-1, keepdims=True) + EPS) * g_ref[...]).astype(u_ref.dtype)
        u_ref[...] = u
        gate, up = _nt(u, wg_ref[...]), _nt(u, wu_ref[...])
        gate_ref[...], up_ref[...] = gate.astype(gate_ref.dtype), up.astype(up_ref.dtype)
        z_ref[...] = (gate * _sigmoid(gate) * up).astype(z_ref.dtype)

    rows = lambda n: pl.BlockSpec((tm, n), lambda i: (i, 0))
    fixed = _fixed_spec
    return _call("ffn_hidden", body, grid=(T // tm,),
                 in_specs=[rows(D), fixed(w_out), rows(D), fixed(gain), fixed(w_gate_t), fixed(w_up_t)],
                 out_specs=[rows(D), rows(D), rows(F), rows(F), rows(F)],
                 out_shape=[jax.ShapeDtypeStruct((T, D), F32), jax.ShapeDtypeStruct((T, D), MXU_DTYPE)]
                 + [jax.ShapeDtypeStruct((T, F), SAVED_DTYPE)] * 2 + [jax.ShapeDtypeStruct((T, F), MXU_DTYPE)],
                 args=[merged, w_out, x, gain, w_gate_t, w_up_t], semantics=("parallel",), comm=comm)


def _in_proj(x, gain, w_in_t, b_in, *, tm, comm=None):
    T, D = x.shape
    bounds = [sum(IN_SPLITS[:i]) for i in range(len(IN_SPLITS) + 1)]

    def body(x_ref, g_ref, w_ref, b_ref, u_ref, *piece_refs):
        xv = x_ref[...]
        r = lax.rsqrt(jnp.mean(xv * xv, axis=-1, keepdims=True) + EPS)
        u = (xv * r * g_ref[...]).astype(u_ref.dtype)
        u_ref[...] = u
        for o_ref, lo, hi in zip(piece_refs, bounds[:-1], bounds[1:]):
            o_ref[...] = (_nt(u, w_ref[lo:hi, :]) + b_ref[:, lo:hi]).astype(o_ref.dtype)

    rows = lambda n: pl.BlockSpec((tm, n), lambda i: (i, 0))
    fixed = _fixed_spec
    dtypes = (MXU_DTYPE, MXU_DTYPE, F32, F32)
    return _call("in_proj", body, grid=(T // tm,),
                 in_specs=[rows(D), fixed(gain), fixed(w_in_t), fixed(b_in)],
                 out_specs=[rows(D)] + [rows(n) for n in IN_SPLITS],
                 out_shape=[jax.ShapeDtypeStruct((T, D), MXU_DTYPE)]
                 + [jax.ShapeDtypeStruct((T, n), dt) for n, dt in zip(IN_SPLITS, dtypes)],
                 args=[x, gain, w_in_t, b_in], semantics=("parallel",), comm=comm)


def _row_spec(tm, n):
    return pl.BlockSpec((tm, n), lambda i: (i, 0))


def _fixed_spec(a):
    return pl.BlockSpec(a.shape, lambda i: (0,) * a.ndim, pipeline_mode=pl.Buffered(1))


def _partials_spec(n):
    return pl.BlockSpec((8, n), lambda i: (i, 0))


def _row_parts(tm, parts):
    assert tm % parts == 0, (tm, parts)
    return [slice(p * (tm // parts), (p + 1) * (tm // parts)) for p in range(parts)]


def _ffn_tail(z, w_down, h1, target, gain, gate, up, *, tm, parts=2):
    (T, F), D = z.shape, h1.shape[1]

    def body(z_ref, w_ref, h_ref, t_ref, g_ref, gate_ref, up_ref, dh_ref, dhb_ref, dgate_ref, dup_ref, dg_ref, l_ref):
        part, dgain = 0.0, 0.0
        pieces = _row_parts(tm, parts)
        h2s = [h_ref[rows, :] + _nn(z_ref[rows, :], w_ref[...]) for rows in pieces]
        for rows, h2 in zip(pieces, h2s):
            r = lax.rsqrt(jnp.mean(h2 * h2, axis=-1, keepdims=True) + EPS)
            xhat = h2 * r
            err = xhat * g_ref[...] - t_ref[rows, :]
            part += 0.5 * jnp.sum(jnp.sum(err * err, axis=-1, keepdims=True), axis=0, keepdims=True) / D
            dy = err / D
            dxh = dy * g_ref[...]
            dh2 = r * (dxh - xhat * jnp.mean(dxh * xhat, axis=-1, keepdims=True))
            dh_ref[rows, :] = dh2
            dhb = dh2.astype(dhb_ref.dtype)
            dhb_ref[rows, :] = dhb
            dgain += jnp.sum(dy * xhat, axis=0, keepdims=True)
            dz = _nt(dhb, w_ref[...])
            gv, upv = gate_ref[rows, :].astype(F32), up_ref[rows, :].astype(F32)
            s = _sigmoid(gv)
            dgate_ref[rows, :] = (dz * upv * (s * (1.0 + gv * (1.0 - s)))).astype(dgate_ref.dtype)
            dup_ref[rows, :] = (dz * (gv * s)).astype(dup_ref.dtype)
        dg_ref[...] = jnp.broadcast_to(dgain, dg_ref.shape)
        l_ref[...] = jnp.broadcast_to(part, l_ref.shape)

    low = lambda n: jax.ShapeDtypeStruct((T, n), MXU_DTYPE)
    part = jax.ShapeDtypeStruct((8 * (T // tm), D), F32)
    return pl.pallas_call(
        body, name="ffn_tail", grid=(T // tm,),
        in_specs=[_row_spec(tm, F), _fixed_spec(w_down), _row_spec(tm, D), _row_spec(tm, D), _fixed_spec(gain),
                  _row_spec(tm, F), _row_spec(tm, F)],
        out_specs=[_row_spec(tm, D), _row_spec(tm, D), _row_spec(tm, F), _row_spec(tm, F), _partials_spec(D),
                   _partials_spec(D)],
        out_shape=[jax.ShapeDtypeStruct((T, D), F32), low(D), low(F), low(F), part, part],
        compiler_params=_params(("parallel",)),
    )(z, w_down, h1, target, gain, gate, up)


def _ffn_in_bwd(dgate, dup, w_gate_t, w_up_t, h1, gain, dres, *, tm, comm=None):
    (T, F), D = dgate.shape, h1.shape[1]

    def body(dg_ref, du_ref, wg_ref, wu_ref, h_ref, g_ref, r_ref, dh_ref, dhb_ref, dgain_ref):
        d_u2 = _nn(dg_ref[...], wg_ref[...]) + _nn(du_ref[...], wu_ref[...])
        dx, dgain = _rmsnorm_bwd_vals(d_u2, h_ref[...], g_ref[...])
        dh = r_ref[...] + dx
        dh_ref[...] = dh
        dhb_ref[...] = dh.astype(dhb_ref.dtype)
        dgain_ref[...] = jnp.broadcast_to(dgain, dgain_ref.shape)

    return _call("d_ffn_in", body, grid=(T // tm,),
                 in_specs=[_row_spec(tm, F), _row_spec(tm, F), _fixed_spec(w_gate_t), _fixed_spec(w_up_t),
                           _row_spec(tm, D), _fixed_spec(gain), _row_spec(tm, D)],
                 out_specs=[_row_spec(tm, D), _row_spec(tm, D), _partials_spec(D)],
                 out_shape=[jax.ShapeDtypeStruct((T, D), F32), jax.ShapeDtypeStruct((T, D), MXU_DTYPE),
                            jax.ShapeDtypeStruct((8 * (T // tm), D), F32)],
                 args=[dgate, dup, w_gate_t, w_up_t, h1, gain, dres], semantics=("parallel",), comm=comm)


def _in_proj_bwd(pieces, w_in_t, x, gain, dres, *, tm, comm=None):
    T, D = x.shape
    n = len(pieces)

    def body(*refs):
        dps, (w_ref, x_ref, g_ref, r_ref, dx_ref, dgain_ref) = refs[:n], refs[n:]
        d_u = None
        for dp_ref, (dp, first) in zip(dps, pieces):
            term = _nn(dp_ref[...], w_ref[first:first + dp.shape[1], :])
            d_u = term if d_u is None else d_u + term
        dx, dgain = _rmsnorm_bwd_vals(d_u, x_ref[...], g_ref[...])
        dx_ref[...] = r_ref[...] + dx
        dgain_ref[...] = jnp.broadcast_to(dgain, dgain_ref.shape)

    return _call("d_u", body, grid=(T // tm,),
                 in_specs=[_row_spec(tm, dp.shape[1]) for dp, _ in pieces]
                 + [_fixed_spec(w_in_t), _row_spec(tm, D), _fixed_spec(gain), _row_spec(tm, D)],
                 out_specs=[_row_spec(tm, D), _partials_spec(D)],
                 out_shape=[jax.ShapeDtypeStruct((T, D), F32), jax.ShapeDtypeStruct((8 * (T // tm), D), F32)],
                 args=[dp for dp, _ in pieces] + [w_in_t, x, gain, dres], semantics=("parallel",), comm=comm)


def _merge_fwd(y_a, y_b, w_a, w_b, gates, *, tm, comm=None):
    T, D = y_a.shape

    def body(ya_ref, yb_ref, wa_ref, wb_ref, ga_ref, gb_ref, pa_ref, pb_ref, m_ref):
        pa, pb = _nn(ya_ref[...], wa_ref[...]), _nn(yb_ref[...], wb_ref[...])
        pa_ref[...], pb_ref[...] = pa.astype(pa_ref.dtype), pb.astype(pb_ref.dtype)
        m_ref[...] = (_sigmoid(ga_ref[...]) * pa + _sigmoid(gb_ref[...]) * pb).astype(m_ref.dtype)

    rows = pl.BlockSpec((tm, D), lambda i: (i, 0))
    whole = pl.BlockSpec((D, D), lambda i: (0, 0), pipeline_mode=pl.Buffered(1))
    return _call("branch_merge", body, grid=(T // tm,),
                 in_specs=[rows, rows, whole, whole, rows, pl.BlockSpec((tm, D), lambda i: (i, 1))],
                 out_specs=[rows] * 3,
                 out_shape=[jax.ShapeDtypeStruct((T, D), SAVED_DTYPE)] * 2 + [jax.ShapeDtypeStruct((T, D), MXU_DTYPE)],
                 args=[y_a, y_b, w_a, w_b, gates, gates], semantics=("parallel",), comm=comm)


def _merge_bwd(dh, w_out, w_a, w_b, p_a, p_b, gates, *, tm, d_in_width, first):
    T, D = dh.shape

    def body(dh_ref, wo_ref, wa_ref, wb_ref, pa_ref, pb_ref, ga_ref, gb_ref, dpa_ref, dpb_ref, dya_ref, dyb_ref,
             din_ref):
        dm = _nt(dh_ref[...], wo_ref[...])
        sa, sb = _sigmoid(ga_ref[...]), _sigmoid(gb_ref[...])
        dpa, dpb = (dm * sa).astype(dpa_ref.dtype), (dm * sb).astype(dpb_ref.dtype)
        dpa_ref[...], dpb_ref[...] = dpa, dpb
        din_ref[:, :D] = (dm * pa_ref[...].astype(F32) * sa * (1.0 - sa)).astype(din_ref.dtype)
        din_ref[:, D:] = (dm * pb_ref[...].astype(F32) * sb * (1.0 - sb)).astype(din_ref.dtype)
        dya_ref[...] = _nt(dpa, wa_ref[...]).astype(dya_ref.dtype)
        dyb_ref[...] = _nt(dpb, wb_ref[...]).astype(dyb_ref.dtype)

    rows = pl.BlockSpec((tm, D), lambda i: (i, 0))
    whole = pl.BlockSpec((D, D), lambda i: (0, 0), pipeline_mode=pl.Buffered(1))
    low = jax.ShapeDtypeStruct((T, D), MXU_DTYPE)
    return pl.pallas_call(
        body, name="d_branch_merge", grid=(T // tm,),
        in_specs=[rows, whole, whole, whole, rows, rows, rows, pl.BlockSpec((tm, D), lambda i: (i, 1))],
        out_specs=[rows] * 4 + [pl.BlockSpec((pl.Element(tm), pl.Element(2 * D)), lambda i: (
            pl.multiple_of(i * tm, ROW_ALIGN), first))],
        out_shape=[low] * 3 + [jax.ShapeDtypeStruct((T, D), F32), jax.ShapeDtypeStruct((T, d_in_width), MXU_DTYPE)],
        compiler_params=_params(("parallel",)),
    )(dh, w_out, w_a, w_b, p_a, p_b, gates, gates)


def _colsum_partials(p):
    return jnp.sum(p.reshape(-1, 8, p.shape[-1])[:, 0, :], axis=0, keepdims=True)


def _rmsnorm_bwd_vals(dy, xin, g):
    rstd = lax.rsqrt(jnp.mean(xin * xin, axis=-1, keepdims=True) + EPS)
    xhat = xin * rstd
    dg = jnp.sum(dy * xhat, axis=0, keepdims=True)
    dxh = dy * g
    dx = rstd * (dxh - xhat * jnp.mean(dxh * xhat, axis=-1, keepdims=True))
    return dx, dg


ATTN_SCALE = 1.0 / math.sqrt(HEAD_DIM)
GROUP_LANES = GROUP * ATTN_BLOCK
PAIR = 2 * HEAD_DIM


def _attn_mask():
    kj = lax.broadcasted_iota(jnp.int32, (ATTN_BLOCK, GROUP_LANES), 0)
    qi = lax.broadcasted_iota(jnp.int32, (ATTN_BLOCK, GROUP_LANES), 1) & (ATTN_BLOCK - 1)
    return kj <= qi


def _heads_transposed(ref, g, scale=None):
    parts = []
    for a in range(GROUP // 2):
        lo = (g * GROUP // 2 + a) * PAIR
        pair = ref[:, lo:lo + PAIR].astype(F32)
        pair = (pair if scale is None else pair * scale).T
        parts += [pair[:HEAD_DIM], pair[HEAD_DIM:]]
    return jnp.concatenate(parts, axis=1).astype(MXU_DTYPE)


def _heads_back(ref, g, vt):
    for a in range(GROUP // 2):
        lo = (g * GROUP // 2 + a) * PAIR
        pair = jnp.concatenate([vt[:, (2 * a) * ATTN_BLOCK:(2 * a + 1) * ATTN_BLOCK],
                                vt[:, (2 * a + 1) * ATTN_BLOCK:(2 * a + 2) * ATTN_BLOCK]], axis=0)
        ref[:, lo:lo + PAIR] = pair.T.astype(ref.dtype)


def _kv_parts(kv_ref, g):
    ks = slice(g * HEAD_DIM, (g + 1) * HEAD_DIM)
    vs = slice(KV_WIDTH + g * HEAD_DIM, KV_WIDTH + (g + 1) * HEAD_DIM)
    return kv_ref[:, ks].astype(MXU_DTYPE), kv_ref[:, vs].astype(MXU_DTYPE)


def _sink_rows(sinks):
    return jnp.repeat(sinks.reshape(KV_HEADS, GROUP), ATTN_BLOCK, axis=1)


def _attn_fwd(pq, pkv, sinks, comm=None):
    T = pq.shape[0]
    nb = T // ATTN_BLOCK

    def body(q_ref, kvc_ref, kvp_ref, s_ref, y_ref, lse_ref):
        mask_c = _attn_mask()
        has_prev = pl.program_id(0) > 0
        early = []
        for g in range(KV_HEADS):
            (kc, vc), (kp, vp) = _kv_parts(kvc_ref, g), _kv_parts(kvp_ref, g)
            qt = _heads_transposed(q_ref, g, ATTN_SCALE)
            early.append((vc, vp, jnp.where(mask_c, _nn(kc, qt), jnp.where(has_prev, _nn(kp, qt), NEG_INF))))
        for g, (vc, vp, s) in enumerate(early):
            sink = s_ref[g:g + 1, :]
            m = jnp.maximum(jnp.max(s, axis=0, keepdims=True), sink)
            p = jnp.exp(s - m)
            den = jnp.sum(p, axis=0, keepdims=True) + jnp.exp(sink - m)
            pc = jnp.where(mask_c, p, 0.0)
            _heads_back(y_ref, g, (_tn(vc, pc) + _tn(vp, p - pc)) / den)
            lse = m + jnp.log(den)
            for i in range(GROUP):
                lse_ref[g * GROUP + i:g * GROUP + i + 1, :] = lse[:, i * ATTN_BLOCK:(i + 1) * ATTN_BLOCK]

    return _call(
        "attn_fwd", body, grid=(nb,),
        in_specs=[pl.BlockSpec((ATTN_BLOCK, D_MODEL), lambda n: (n, 0)),
                  pl.BlockSpec((ATTN_BLOCK, 2 * KV_WIDTH), lambda n: (n, 0)),
                  pl.BlockSpec((ATTN_BLOCK, 2 * KV_WIDTH), lambda n: (jnp.maximum(n - 1, 0), 0)),
                  pl.BlockSpec((KV_HEADS, GROUP_LANES), lambda n: (0, 0))],
        out_specs=[pl.BlockSpec((ATTN_BLOCK, D_MODEL), lambda n: (n, 0)),
                   pl.BlockSpec((Q_HEADS, ATTN_BLOCK), lambda n: (0, n))],
        out_shape=[jax.ShapeDtypeStruct((T, D_MODEL), MXU_DTYPE), jax.ShapeDtypeStruct((Q_HEADS, T), F32)],
        args=[pq, pkv, pkv, _sink_rows(sinks)], semantics=("parallel",), comm=comm)


def _attn_bwd(pq, pkv, sinks, lse, dy, d_in, comm=None):
    T = pq.shape[0]
    nb = T // ATTN_BLOCK
    cur = lambda n: (jnp.minimum(n, nb - 1), 0)
    done = D_MODEL + 2 * KV_WIDTH

    def body(q_ref, kvc_ref, kvp_ref, s_ref, lse_ref, dy_ref, _, out_ref, ds_ref, carry, top, bot, dq_ref):
        n = pl.program_id(0)

        @pl.when(n == 0)
        def _():
            carry[...] = jnp.zeros_like(carry)
            dq_ref[...] = jnp.zeros_like(dq_ref)
            ds_ref[...] = jnp.zeros_like(ds_ref)

        out_ref[:, :D_MODEL] = dq_ref[...]

        @pl.when(n < nb)
        def _():
            mask_c = _attn_mask()
            valid = jnp.logical_or(mask_c, n > 0)
            early = []
            for g in range(KV_HEADS):
                (kc, vc), (kp, vp) = _kv_parts(kvc_ref, g), _kv_parts(kvp_ref, g)
                qt = _heads_transposed(q_ref, g, ATTN_SCALE)
                dot = _heads_transposed(dy_ref, g)
                early.append((kc, kp, qt, dot, jnp.where(mask_c, _nn(kc, qt), _nn(kp, qt)),
                              jnp.where(mask_c, _nn(vc, dot), _nn(vp, dot))))
            for g, (kc, kp, qt, dot, s, dp) in enumerate(early):
                ks = slice(g * HEAD_DIM, (g + 1) * HEAD_DIM)
                vs = slice(KV_WIDTH + g * HEAD_DIM, KV_WIDTH + (g + 1) * HEAD_DIM)
                lse = jnp.concatenate([lse_ref[g * GROUP + i:g * GROUP + i + 1, :] for i in range(GROUP)], axis=1)
                p = jnp.where(valid, jnp.exp(s - lse), 0.0)
                delta = jnp.sum(p * dp, axis=0, keepdims=True)
                ds = p * (dp - delta)
                ds_c, p_c = jnp.where(mask_c, ds, 0.0), jnp.where(mask_c, p, 0.0)
                ds_p, p_p = ds - ds_c, p - p_c
                _heads_back(dq_ref, g, (_tn(kc, ds_c) + _tn(kp, ds_p)) * ATTN_SCALE)
                bot[:, ks], bot[:, vs] = _nt(ds_c, qt), _nt(p_c, dot)
                top[:, ks], top[:, vs] = _nt(ds_p, qt), _nt(p_p, dot)
                ds_ref[g:g + 1, :] -= jnp.exp(s_ref[g:g + 1, :] - lse) * delta
            out_ref[:, D_MODEL:] = (carry[...] + top[...]).astype(out_ref.dtype)
            carry[...] = bot[...]

        @pl.when(n == nb)
        def _():
            out_ref[:, D_MODEL:] = carry[...].astype(out_ref.dtype)

    return _call(
        "attn_bwd", body, grid=(nb + 1,),
        in_specs=[pl.BlockSpec((ATTN_BLOCK, D_MODEL), cur),
                  pl.BlockSpec((ATTN_BLOCK, 2 * KV_WIDTH), cur),
                  pl.BlockSpec((ATTN_BLOCK, 2 * KV_WIDTH), lambda n: (jnp.maximum(jnp.minimum(n, nb - 1) - 1, 0), 0)),
                  pl.BlockSpec((KV_HEADS, GROUP_LANES), lambda n: (0, 0)),
                  pl.BlockSpec((Q_HEADS, ATTN_BLOCK), lambda n: (0, jnp.minimum(n, nb - 1))),
                  pl.BlockSpec((ATTN_BLOCK, D_MODEL), cur), HBM_SPEC],
        out_specs=[pl.BlockSpec((ATTN_BLOCK, done), lambda n: (jnp.maximum(n - 1, 0), 0)),
                   pl.BlockSpec((KV_HEADS, GROUP_LANES), lambda n: (0, 0))],
        out_shape=[jax.ShapeDtypeStruct(d_in.shape, d_in.dtype), jax.ShapeDtypeStruct((KV_HEADS, GROUP_LANES), F32)],
        scratch=[pltpu.VMEM((ATTN_BLOCK, 2 * KV_WIDTH), F32)] * 3 + [pltpu.VMEM((ATTN_BLOCK, D_MODEL), MXU_DTYPE)],
        args=[pq, pkv, pkv, _sink_rows(sinks), lse, dy, d_in], semantics=("arbitrary",), comm=comm, aliases={6: 0})


def _lower_bound(l):
    m = jnp.maximum(l[0:1], l[1:2])
    e0, e1 = jnp.exp(l[0:1] - m), jnp.exp(l[1:2] - m)
    return e0 / (e0 + e1)


def _tri(lower):
    r = lax.broadcasted_iota(jnp.int32, (CHUNK, CHUNK), 0)
    c = lax.broadcasted_iota(jnp.int32, (CHUNK, CHUNK), 1)
    return (r >= c) if lower else (c >= r)


def _chunk_sum(mask, v):
    ones = mask.astype(BF16)
    hi = v.astype(BF16)
    rest = v - hi.astype(F32)
    mid = rest.astype(BF16)
    lo = (rest - mid.astype(F32)).astype(BF16)
    part = lambda t: lax.dot_general(ones, t, (((1,), (0,)), ((), ())), preferred_element_type=F32)
    return part(hi) + part(mid) + part(lo)


def _hgrn_chunk_inputs(hq, hf, lb, causal):
    half_t = 0.5 * jnp.tanh(0.5 * hf)
    sg, sgn = 0.5 + half_t, 0.5 - half_t
    f = lb + (1.0 - lb) * sg
    kk = (1.0 - lb) * sgn
    sq = _sigmoid(hq)
    q = hq * sq
    b = _chunk_sum(causal, jnp.log(f))
    bm, bl = b[CHUNK // 2 - 1:CHUNK // 2, :], b[CHUNK - 1:CHUNK, :]
    e_qm, e_km = jnp.exp(b - bm), jnp.exp(bm - b)
    e_qs, e_kl = e_qm * jnp.exp(bm), e_km * jnp.exp(bl - bm)
    return dict(sg=sg, sgn=sgn, f=f, kk=kk, sq=sq, q=q, e_qm=e_qm, e_km=e_km, e_qs=e_qs, e_kl=e_kl,
                qm=q * e_qm, km=kk * e_km, qs=q * e_qs, kl=kk * e_kl, el=jnp.exp(bl))


def _hgrn_fwd(ph, lb_logits, norm_g, comm=None):
    T = ph.shape[0]
    nblk, cpb = T // HGRN_TOKENS, HGRN_TOKENS // CHUNK
    col = lambda c: pl.BlockSpec((HGRN_TOKENS, D_MODEL), functools.partial(lambda i, c: (i, c), c=c))

    def body(hq_ref, hf_ref, hi_ref, hg_ref, l_ref, ng_ref, y_ref, o_ref, st_ref, s_ref):
        @pl.when(pl.program_id(0) == 0)
        def _():
            s_ref[...] = jnp.zeros_like(s_ref)

        lb = _lower_bound(l_ref[...])
        causal = _tri(True)
        for c in range(cpb):
            rows = slice(c * CHUNK, (c + 1) * CHUNK)
            t = _hgrn_chunk_inputs(hq_ref[rows, :], hf_ref[rows, :], lb, causal)
            qm, km, qs, kl = (t[n].astype(MXU_DTYPE) for n in ("qm", "km", "qs", "kl"))
            v = hi_ref[rows, :].astype(MXU_DTYPE)
            heads = [slice(h * HGRN_K, (h + 1) * HGRN_K) for h in range(HGRN_HEADS)]
            a_all = [jnp.where(causal, _nt(qm[:, ls], km[:, ls]), 0.0).astype(MXU_DTYPE) for ls in heads]
            for h, ls in enumerate(heads):
                st = s_ref[h]
                st_ref[c, ls, :] = st
                o_ref[rows, ls] = _nn(a_all[h], v[:, ls]) + _nt(qs[:, ls], st)
                s_ref[h] = t["el"][:, ls] * st + _tn(v[:, ls], kl[:, ls])
        for h in range(HGRN_HEADS):
            ls = slice(h * HGRN_K, (h + 1) * HGRN_K)
            o = o_ref[:, ls]
            r = lax.rsqrt(jnp.mean(o * o, axis=-1, keepdims=True) + EPS)
            y_ref[:, ls] = (o * r * ng_ref[:, ls] * _sigmoid(hg_ref[:, ls])).astype(y_ref.dtype)

    return _call(
        "hgrn_fwd", body, grid=(nblk,),
        in_specs=[col(0), col(1), col(2), col(3),
                  pl.BlockSpec((2, D_MODEL), lambda i: (0, 0)), pl.BlockSpec((1, D_MODEL), lambda i: (0, 0))],
        out_specs=[pl.BlockSpec((HGRN_TOKENS, D_MODEL), lambda i: (i, 0)),
                   pl.BlockSpec((HGRN_TOKENS, D_MODEL), lambda i: (i, 0)),
                   pl.BlockSpec((cpb, D_MODEL, HGRN_K), lambda i: (i, 0, 0))],
        out_shape=[jax.ShapeDtypeStruct((T, D_MODEL), MXU_DTYPE), jax.ShapeDtypeStruct((T, D_MODEL), F32),
                   jax.ShapeDtypeStruct((T // CHUNK, D_MODEL, HGRN_K), F32)],
        scratch=[pltpu.VMEM((HGRN_HEADS, HGRN_K, HGRN_K), F32)],
        args=[ph, ph, ph, ph, lb_logits, norm_g], semantics=("arbitrary",), comm=comm)


def _hgrn_bwd(ph, o_raw, states, dy, lb_logits, norm_g, d_in, first, comm=None):
    T = ph.shape[0]
    nblk, cpb = T // HGRN_TOKENS, HGRN_TOKENS // CHUNK
    rev = lambda i: nblk - 1 - i
    col = lambda c: pl.BlockSpec((HGRN_TOKENS, D_MODEL), functools.partial(lambda i, c: (rev(i), c), c=c))
    tok = pl.BlockSpec((HGRN_TOKENS, D_MODEL), lambda i: (rev(i), 0))

    def body(hq_ref, hf_ref, hi_ref, hg_ref, o_ref, st_ref, dy_ref, l_ref, ng_ref, _,
             dph_ref, dng_ref, dl_ref, dst_ref, dlb_ref, do_s, dqm_s, dkm_s, dqs_s, dkl_s, dv_s, del_s):
        i = pl.program_id(0)

        @pl.when(i == 0)
        def _():
            dst_ref[...] = jnp.zeros_like(dst_ref)
            dlb_ref[...] = jnp.zeros_like(dlb_ref)
            dng_ref[...] = jnp.zeros_like(dng_ref)

        lb = _lower_bound(l_ref[...])
        causal, anti = _tri(True), _tri(False)
        row = lax.broadcasted_iota(jnp.int32, (CHUNK, D_MODEL), 0)
        for c in reversed(range(cpb)):
            rows = slice(c * CHUNK, (c + 1) * CHUNK)
            hq = hq_ref[rows, :]
            t = _hgrn_chunk_inputs(hq, hf_ref[rows, :], lb, causal)
            sgg = _sigmoid(hg_ref[rows, :])
            dyv = dy_ref[rows, :]
            for h in range(HGRN_HEADS):
                ls = slice(h * HGRN_K, (h + 1) * HGRN_K)
                o = o_ref[rows, ls]
                r = lax.rsqrt(jnp.mean(o * o, axis=-1, keepdims=True) + EPS)
                nrm = o * r
                g_h = sgg[:, ls]
                dph_ref[rows, 3 * D_MODEL + h * HGRN_K:3 * D_MODEL + (h + 1) * HGRN_K] = (
                    dyv[:, ls] * nrm * ng_ref[:, ls] * g_h * (1.0 - g_h)).astype(dph_ref.dtype)
                dyg = dyv[:, ls] * g_h
                dng_ref[:, ls] += jnp.sum(dyg * nrm, axis=0, keepdims=True)
                dn = dyg * ng_ref[:, ls]
                do_s[:, ls] = r * (dn - nrm * jnp.mean(dn * nrm, axis=-1, keepdims=True))
            qm, km, qs, kl = (t[n].astype(MXU_DTYPE) for n in ("qm", "km", "qs", "kl"))
            v = hi_ref[rows, :].astype(MXU_DTYPE)
            do = do_s[...].astype(MXU_DTYPE)
            heads = [slice(h * HGRN_K, (h + 1) * HGRN_K) for h in range(HGRN_HEADS)]
            a_all = [jnp.where(causal, _nt(qm[:, ls], km[:, ls]), 0.0).astype(MXU_DTYPE) for ls in heads]
            da_all = [jnp.where(causal, _nt(do[:, ls], v[:, ls]), 0.0).astype(MXU_DTYPE) for ls in heads]
            for h, ls in enumerate(heads):
                st = st_ref[c, ls, :]
                dst = dst_ref[h]
                a, da = a_all[h], da_all[h]
                dv_s[:, ls] = _tn(a, do[:, ls]) + _nt(kl[:, ls], dst)
                dkl_s[:, ls] = _nn(v[:, ls], dst)
                dqs_s[:, ls] = _nn(do[:, ls], st)
                del_s[:, ls] = jnp.sum(dst * st, axis=0, keepdims=True)
                dst_ref[h] = _tn(do[:, ls], qs[:, ls]) + t["el"][:, ls] * dst
                dqm_s[:, ls] = _nn(da, km[:, ls])
                dkm_s[:, ls] = _tn(da, qm[:, ls])
            dqm, dkm, dqs, dkl = dqm_s[...], dkm_s[...], dqs_s[...], dkl_s[...]
            dq = dqm * t["e_qm"] + dqs * t["e_qs"]
            dk = dkm * t["e_km"] + dkl * t["e_kl"]
            t_qm, t_km, t_kl = dqm * t["qm"], dkm * t["km"], dkl * t["kl"]
            db = t_qm - t_km + dqs * t["qs"] - t_kl
            db_mid = jnp.sum(t_km - t_qm, axis=0, keepdims=True)
            db_last = jnp.sum(t_kl, axis=0, keepdims=True) + del_s[...] * t["el"]
            db = db + jnp.where(row == CHUNK // 2 - 1, db_mid, 0.0) + jnp.where(row == CHUNK - 1, db_last, 0.0)
            dlogf = _chunk_sum(anti, db)
            sq, sg, sgn, f = t["sq"], t["sg"], t["sgn"], t["f"]
            dph_ref[rows, 0:D_MODEL] = (dq * (sq * (1.0 + hq * (1.0 - sq)))).astype(dph_ref.dtype)
            dph_ref[rows, D_MODEL:2 * D_MODEL] = (
                dlogf * (1.0 - lb) * sg * (1.0 - sg) / f - dk * (1.0 - lb) * sgn * (1.0 - sgn)).astype(dph_ref.dtype)
            dph_ref[rows, 2 * D_MODEL:3 * D_MODEL] = dv_s[...].astype(dph_ref.dtype)
            dlb_ref[...] += jnp.sum(dlogf * (1.0 - sg) / f - dk * sgn, axis=0, keepdims=True)

        @pl.when(i == nblk - 1)
        def _():
            dl0 = dlb_ref[...] * lb * (1.0 - lb)
            dl_ref[0:1, :] = dl0
            dl_ref[1:2, :] = -dl0

    wide = pltpu.VMEM((CHUNK, D_MODEL), F32)
    return _call(
        "hgrn_bwd", body, grid=(nblk,),
        in_specs=[col(0), col(1), col(2), col(3), tok,
                  pl.BlockSpec((cpb, D_MODEL, HGRN_K), lambda i: (rev(i), 0, 0)), tok,
                  pl.BlockSpec((2, D_MODEL), lambda i: (0, 0)), pl.BlockSpec((1, D_MODEL), lambda i: (0, 0)), HBM_SPEC],
        out_specs=[pl.BlockSpec((pl.Element(HGRN_TOKENS), pl.Element(4 * D_MODEL)), lambda i: (
                       pl.multiple_of(rev(i) * HGRN_TOKENS, ROW_ALIGN), first)),
                   pl.BlockSpec((1, D_MODEL), lambda i: (0, 0)), pl.BlockSpec((2, D_MODEL), lambda i: (0, 0))],
        out_shape=[jax.ShapeDtypeStruct(d_in.shape, d_in.dtype), jax.ShapeDtypeStruct((1, D_MODEL), F32),
                   jax.ShapeDtypeStruct((2, D_MODEL), F32)],
        scratch=[pltpu.VMEM((HGRN_HEADS, HGRN_K, HGRN_K), F32), pltpu.VMEM((1, D_MODEL), F32),
                 wide, wide, wide, wide, wide, wide, pltpu.VMEM((1, D_MODEL), F32)],
        args=[ph, ph, ph, ph, o_raw, states, dy, lb_logits, norm_g, d_in], semantics=("arbitrary",), comm=comm,
        aliases={9: 0})


def _local_step(x, target, vec, net):
    T, D = x.shape
    norm_mix_g, b_in, sinks, lb_logits = vec["norm_mix_g"], vec["b_in"], vec["attn_sinks"], vec["hgrn_lb_logits"]
    hgrn_norm_g, norm_ffn_g, norm_final_g = vec["hgrn_norm_g"], vec["norm_ffn_g"], vec["norm_final_g"]
    w_in = net.full("w_in")
    o_q, o_kv, o_h, o_g = (sum(IN_SPLITS[:i]) for i in range(4))
    TM = 512

    first = ("w_branch_attn", "w_branch_hgrn", "w_ffn_down")
    (u, pq, pkv, ph, pg), got = _in_proj(x, norm_mix_g, w_in, b_in, tm=256, comm=net.fetch(first))
    net.fetched(first, got)
    second = ("w_ffn_gate",)
    (y_attn, lse), got = _attn_fwd(pq, pkv, sinks, comm=_join(net.relay(first), net.fetch(second)))
    net.gathered(first, got[:len(first)])
    net.fetched(second, got[len(first):])
    third = ("w_ffn_up", "w_out")
    (y_hgrn, o_raw, states), got = _hgrn_fwd(ph, lb_logits, hgrn_norm_g,
                                             comm=_join(net.relay(second), net.fetch(third)))
    net.gathered(second, got[:len(second)])
    net.fetched(third, got[len(second):])
    w_ba, w_bh = net.full("w_branch_attn"), net.full("w_branch_hgrn")
    (ya, yb, merged), got = _merge_fwd(y_attn, y_hgrn, w_ba, w_bh, pg, tm=TM, comm=net.relay(third))
    net.gathered(third, got)
    w_gate, w_up, w_out = net.full("w_ffn_gate"), net.full("w_ffn_up"), net.full("w_out")

    (h1, u2, gpre, up, z), _ = _ffn_fwd(merged, w_out, x, norm_ffn_g, w_gate, w_up, tm=256)
    w_down = net.full("w_ffn_down")

    dh2, dh2b, dgp, dup, dgf_p, loss_p = _ffn_tail(z, w_down, h1, target, norm_final_g, gpre, up, tm=512)
    loss = jnp.sum(loss_p.reshape(-1, 8, D)[:, 0, 0])
    d_norm_final = _colsum_partials(dgf_p)
    d_w_down = _weight_grad("dw_down", z, dh2b, tm=DW_ROWS, tk=T)

    names, swap = ("w_ffn_down",), ()
    (dh1, dh1b, dg2_p), got = _ffn_in_bwd(dgp, dup, w_gate, w_up, h1, norm_ffn_g, dh2, tm=256,
                                          comm=net.exchange(dict(w_ffn_down=[d_w_down]), swap))
    net.received(names, swap, got)
    d_norm_ffn = _colsum_partials(dg2_p)
    d_w_gate = _weight_grad("dw_gate", dgp, u2, tm=DW_ROWS, tk=T)
    d_w_up = _weight_grad("dw_up", dup, u2, tm=DW_ROWS, tk=T)

    rows_in = sum(IN_SPLITS)
    dya, dyb, dy_attn, dy_hgrn, d_in = _merge_bwd(dh1b, w_out, w_ba, w_bh, ya, yb, pg, tm=TM, d_in_width=rows_in,
                                                  first=o_g)
    d_w_out = _weight_grad("dw_out", merged, dh1b, tm=1024, tk=1024)
    d_w_ba = _weight_grad("dw_branch_a", y_attn, dya, tm=1024, tk=1024)
    d_w_bh = _weight_grad("dw_branch_b", y_hgrn, dyb, tm=1024, tk=1024)

    names, swap = ("w_ffn_gate",), ("w_ffn_down",)
    (d_in, dsink), got = _attn_bwd(pq, pkv, sinks, lse, dy_attn, d_in,
                                   comm=net.exchange(dict(w_ffn_gate=[d_w_gate]), swap))
    net.received(names, swap, got)
    names, swap = ("w_ffn_up", "w_out"), ("w_ffn_gate",)
    (d_in, d_hgrn_norm, d_lb_logits), got = _hgrn_bwd(
        ph, o_raw, states, dy_hgrn, lb_logits, hgrn_norm_g, d_in, o_h,
        comm=net.exchange(dict(w_ffn_up=[d_w_up], w_out=[d_w_out]), swap))
    net.received(names, swap, got)

    names, swap = ("w_branch_attn", "w_branch_hgrn"), ("w_ffn_up", "w_out")
    (*d_w_in, d_b_in), got = _weight_grad(
        "dw_in", d_in, u, tm=DW_ROWS, tk=T, a_colsum=True, carrying=True,
        comm=net.exchange(dict(w_branch_attn=[d_w_ba], w_branch_hgrn=[d_w_bh]), swap))
    net.received(names, swap, got)
    d_w_in = [tuple(d_w_in)]

    first_level = net.presum_begin("w_in", d_w_in)
    halves = net.presum_end("w_in", [] if first_level is None else _copies_alone("presum_swap_w_in", first_level))
    names, swap = ("w_in",), ("w_branch_attn", "w_branch_hgrn")
    (dx, dg1_p), got = _in_proj_bwd([(d_in, 0)], w_in, x, norm_mix_g, dh1, tm=256,
                                    comm=_join(halves, net.swap(swap)))
    net.last = (names, swap, got)
    d_norm_mix = _colsum_partials(dg1_p)
    vecs = dict(norm_mix_g=d_norm_mix, b_in=d_b_in, attn_sinks=jnp.sum(dsink.reshape(Q_HEADS, ATTN_BLOCK), axis=1).reshape(1, Q_HEADS),
                hgrn_lb_logits=d_lb_logits,
                hgrn_norm_g=d_hgrn_norm, norm_ffn_g=d_norm_ffn, norm_final_g=d_norm_final)
    return loss, dx, vecs


def _place():
    return lax.axis_index("x"), lax.axis_index("y"), lax.axis_index("c")


def _other_chips(x, y):
    return [(1 - x, y), (x, 1 - y), (1 - x, 1 - y)]


def _y_first(copies):
    return [copies[3 * (i // 3) + (1, 0, 2)[i % 3]] for i in range(len(copies))]


def _gather_copies(shards):
    n = len(shards)

    def build(ins, outs, send_sems, recv_sems, local_sems):
        x, y, c = _place()
        mine = 2 * x + y
        local = [pltpu.make_async_copy(ins[w], outs[w].at[mine], local_sems.at[w]) for w in range(n)]
        sends, recvs = [], []
        for w in range(n):
            for k, (px, py) in enumerate(_other_chips(x, y)):
                sem = 3 * w + k
                sends.append(pltpu.make_async_remote_copy(
                    src_ref=ins[w], dst_ref=outs[w].at[mine], send_sem=send_sems.at[sem], recv_sem=recv_sems.at[sem],
                    device_id=(px, py, c), device_id_type=MESH_ID))
                recvs.append(pltpu.make_async_remote_copy(
                    src_ref=ins[w], dst_ref=outs[w].at[2 * px + py], send_sem=send_sems.at[sem],
                    recv_sem=recv_sems.at[sem], device_id=(px, py, c), device_id_type=MESH_ID))
        return sends, recvs, local, _y_first(sends)

    return _Carried(shards, [jax.ShapeDtypeStruct((N_CHIPS,) + s.shape, s.dtype) for s in shards], 3 * n, n, build)


def _fetch_copies(shards):
    n = len(shards)

    def build(ins, outs, send_sems, recv_sems, local_sems):
        x, y, c = _place()
        mine = 2 * x + y
        local = [pltpu.make_async_copy(ins[w], outs[w].at[mine], local_sems.at[w]) for w in range(n)]
        sends, recvs = [], []
        for w in range(n):
            half = shards[w].shape[0] // 2
            rows = pl.ds(c * half, half)
            for k, (px, py) in enumerate(_other_chips(x, y)):
                sem = 3 * w + k
                sends.append(pltpu.make_async_remote_copy(
                    src_ref=ins[w].at[rows], dst_ref=outs[w].at[mine, rows], send_sem=send_sems.at[sem],
                    recv_sem=recv_sems.at[sem], device_id=(px, py, c), device_id_type=MESH_ID))
                recvs.append(pltpu.make_async_remote_copy(
                    src_ref=ins[w].at[rows], dst_ref=outs[w].at[2 * px + py, rows], send_sem=send_sems.at[sem],
                    recv_sem=recv_sems.at[sem], device_id=(px, py, c), device_id_type=MESH_ID))
        return sends, recvs, local, _y_first(sends)

    return _Carried(shards, [jax.ShapeDtypeStruct((N_CHIPS,) + s.shape, s.dtype) for s in shards], 3 * n, n, build)


def _relay_copies(fetched):
    n = len(fetched)

    def build(ins, outs, send_sems, recv_sems, local_sems):
        x, y, c = _place()
        sends, recvs = [], []
        for w in range(n):
            half = fetched[w].shape[1] // 2
            mine, theirs = pl.ds(c * half, half), pl.ds((1 - c) * half, half)
            for k, (px, py) in enumerate(_other_chips(x, y)):
                block, sem = 2 * px + py, 3 * w + k
                sends.append(pltpu.make_async_remote_copy(
                    src_ref=ins[w].at[block, mine], dst_ref=outs[w].at[block, mine], send_sem=send_sems.at[sem],
                    recv_sem=recv_sems.at[sem], device_id=(x, y, 1 - c), device_id_type=MESH_ID))
                recvs.append(pltpu.make_async_remote_copy(
                    src_ref=ins[w].at[block, theirs], dst_ref=outs[w].at[block, theirs], send_sem=send_sems.at[sem],
                    recv_sem=recv_sems.at[sem], device_id=(x, y, 1 - c), device_id_type=MESH_ID))
        return sends, recvs, []

    return _Carried(fetched, [jax.ShapeDtypeStruct(f.shape, f.dtype) for f in fetched], 3 * n, 0, build,
                    continued={w: w for w in range(n)})


def _grad_copies(stacked):
    n = len(stacked)

    def build(ins, outs, send_sems, recv_sems, local_sems):
        x, y, c = _place()
        sends = []
        for w in range(n):
            for k, (px, py) in enumerate(_other_chips(x, y)):
                sem = 3 * w + k
                sends.append(pltpu.make_async_remote_copy(
                    src_ref=ins[w].at[2 * px + py], dst_ref=outs[w].at[k], send_sem=send_sems.at[sem],
                    recv_sem=recv_sems.at[sem], device_id=(px, py, c), device_id_type=MESH_ID))
        return sends, sends, [], _y_first(sends)

    return _Carried(stacked, [jax.ShapeDtypeStruct((3,) + s.shape[1:], s.dtype) for s in stacked], 3 * n, 0, build)


def _small_copies(small):
    def build(ins, outs, send_sems, recv_sems, local_sems):
        small_ref, all_ref = ins[0], outs[0]
        x, y, c = _place()
        me = 4 * x + 2 * y + c
        sends, recvs = [], []
        for r in range(1, 8):
            px = 1 - x if r & 4 else x
            py = 1 - y if r & 2 else y
            pc = 1 - c if r & 1 else c
            sends.append(pltpu.make_async_remote_copy(
                src_ref=small_ref, dst_ref=all_ref.at[me], send_sem=send_sems.at[r - 1], recv_sem=recv_sems.at[r - 1],
                device_id=(px, py, pc), device_id_type=MESH_ID))
            recvs.append(pltpu.make_async_remote_copy(
                src_ref=small_ref, dst_ref=all_ref.at[4 * px + 2 * py + pc], send_sem=send_sems.at[r - 1],
                recv_sem=recv_sems.at[r - 1], device_id=(px, py, pc), device_id_type=MESH_ID))
        return sends, recvs, [pltpu.make_async_copy(small_ref, all_ref.at[me], local_sems.at[0])]

    return _Carried([small], [jax.ShapeDtypeStruct((8,) + small.shape, small.dtype)], 7, 1, build)


def _gather_by_neighbours(name, shard):
    half = shard.shape[0] // 2
    quarter = half // 2

    def body(in_ref, out_ref, send_sems, recv_sems, local_sem, stage_ref):
        for core in (0, 1):
            @pl.when(lax.axis_index("c") == core)
            def _():
                program(core, in_ref, out_ref, send_sems, recv_sems, local_sem, stage_ref)

    def program(c, in_ref, out_ref, send_sems, recv_sems, local_sem, stage_ref):
        x, y, _ = _place()
        chip = lambda px, py: 2 * px + py
        to_x, to_y, sibling = (1 - x, y, c), (x, 1 - y, c), (x, y, 1 - c)
        x_blk, y_blk, d_blk = chip(1 - x, y), chip(x, 1 - y), chip(1 - x, 1 - y)
        mine, theirs = c * half, (1 - c) * half

        def copy(sem, rows, block, to, src=None):
            place = out_ref.at[block, pl.ds(rows[0], rows[1])]
            return pltpu.make_async_remote_copy(
                src_ref=place if src is None else src, dst_ref=place, send_sem=send_sems.at[sem],
                recv_sem=recv_sems.at[sem], device_id=to, device_id_type=MESH_ID)

        stage = pltpu.make_async_copy(in_ref, stage_ref, local_sem.at[0])
        own = pltpu.make_async_copy(stage_ref, out_ref.at[chip(x, y)], local_sem.at[1])
        my_rows = in_ref.at[pl.ds(mine, half)]
        along_x = dict(send=copy(0, (mine, half), chip(x, y), to_x, src=my_rows),
                       landed=copy(0, (mine, half), x_blk, to_x),
                       onward=[copy(3, (mine + quarter, quarter), x_blk, to_y), copy(4, (mine, half), x_blk, sibling)],
                       diagonal=copy(2, (mine, quarter), d_blk, to_x))
        along_y = dict(send=copy(1, (mine, half), chip(x, y), to_y, src=my_rows),
                       landed=copy(1, (mine, half), y_blk, to_y),
                       onward=[copy(2, (mine, quarter), y_blk, to_x), copy(5, (mine, half), y_blk, sibling)],
                       diagonal=copy(3, (mine + quarter, quarter), d_blk, to_y))
        last = copy(6, (mine, half), d_blk, sibling)

        order = (along_x, along_y) if c == 0 else (along_y, along_x)
        for axis in order:
            axis["send"].start()
        stage.start()
        stage.wait()
        own.start()
        for axis in order:
            axis["landed"].wait_recv()
            for cp in axis["onward"]:
                cp.start()
        for axis in order:
            axis["diagonal"].wait_recv()
        last.start()
        for sem, block in ((4, x_blk), (5, y_blk), (6, d_blk)):
            copy(sem, (theirs, half), block, sibling).wait_recv()
        for cp in [along_x["send"], along_y["send"]] + along_x["onward"] + along_y["onward"] + [last]:
            cp.wait_send()
        own.wait()

    return pl.pallas_call(
        body, name=name, in_specs=[HBM_SPEC], out_specs=HBM_SPEC,
        out_shape=jax.ShapeDtypeStruct((N_CHIPS,) + shard.shape, shard.dtype),
        scratch_shapes=[pltpu.SemaphoreType.DMA((7,)), pltpu.SemaphoreType.DMA((7,)), pltpu.SemaphoreType.DMA((2,)),
                        pltpu.VMEM(shard.shape, shard.dtype)],
    )(shard)


def _copies_alone(name, comm):
    return _call(name, lambda: None, grid=(), in_specs=[], out_specs=[], out_shape=[], args=[], comm=comm)[1]


class _Net:
    def __init__(self, shards):
        self.shards = shards
        self.whole, self.partly, self.own, self.theirs, self.sums, self.other = {}, {}, {}, {}, {}, {}
        x, y, _ = _place()
        self.chip = 2 * x + y

    def fetch(self, names):
        return _fetch_copies([self.shards[n] for n in names])

    def fetched(self, names, got):
        self.partly.update(zip(names, got))

    def relay(self, names):
        return _relay_copies([self.partly[n] for n in names])

    def gathered(self, names, got):
        for n, g in zip(names, got):
            self.whole[n] = g.reshape(-1, g.shape[-1])

    def full(self, name):
        return self.whole[name]

    def exchange(self, grads, swap=()):
        stacked = []
        for n, pieces in grads.items():
            (keep, send), = pieces
            self.own[n] = keep
            stacked.append(send.reshape(N_CHIPS, keep.shape[0] // N_CHIPS, send.shape[-1]))
        return _join(_grad_copies(stacked), self.swap(swap))

    def swap(self, names):
        return _sibling_copies([self.sums[n] for n in names]) if names else None

    def presum_begin(self, name, pieces):
        keep = jnp.concatenate([p[0] for p in pieces], axis=0) if len(pieces) > 1 else pieces[0][0]
        send = jnp.concatenate([p[1] for p in pieces], axis=0) if len(pieces) > 1 else pieces[0][1]
        rows = keep.shape[0] // N_CHIPS
        self.held = keep.reshape(N_CHIPS, rows, keep.shape[-1])
        return _half_rows_copies(send.reshape(N_CHIPS, rows, send.shape[-1]))

    def presum_end(self, name, got):
        x, y, c = _place()
        to_send, self.own[name] = _pre_sum("presum_" + name, self.held, got[0], jnp.stack([c, self.chip]))
        return _grad_copies([to_send])

    def received(self, names, swap, got, carried=None):
        self.theirs.update(zip(names, got[:len(names)]))
        self.other.update(zip(swap, got[len(names):]))
        for n in names:
            (self.sums[n],), more = _partial_sum("sum_" + n, self.own[n], self.theirs[n], self.chip, comm=carried)
        return more


def _half_rows_copies(stacked):
    n, rows = stacked.shape[0], stacked.shape[1] // 2

    def build(ins, outs, send_sems, recv_sems, local_sems):
        x, y, c = _place()
        copies = [pltpu.make_async_remote_copy(
            src_ref=ins[0].at[s, pl.ds((1 - c) * rows, rows)], dst_ref=outs[0].at[s], send_sem=send_sems.at[s],
            recv_sem=recv_sems.at[s], device_id=(x, y, 1 - c), device_id_type=MESH_ID) for s in range(n)]
        return copies, copies, []

    return _Carried([stacked], [jax.ShapeDtypeStruct((n, rows, stacked.shape[2]), stacked.dtype)], n, 0, build)


def _pre_sum(name, held, theirs, core_and_chip):
    n, R, C = held.shape
    half = R // 2
    tr = _row_tile(half)
    per_half = half // tr

    def body(place_ref, h_ref, t_ref, send_ref, own_ref):
        total = h_ref[0] + t_ref[0].astype(F32)
        send_ref[0] = total.astype(send_ref.dtype)

        @pl.when(pl.program_id(1) == place_ref[1])
        def _():
            own_ref[...] = total

    return pl.pallas_call(
        body, name=name,
        grid_spec=pltpu.PrefetchScalarGridSpec(
            num_scalar_prefetch=1, grid=(per_half, n),
            in_specs=[pl.BlockSpec((1, tr, C), lambda i, s, place: (s, place[0] * per_half + i, 0)),
                      pl.BlockSpec((1, tr, C), lambda i, s, place: (s, i, 0))],
            out_specs=[pl.BlockSpec((1, tr, C), lambda i, s, place: (s, i, 0)),
                       pl.BlockSpec((tr, C), lambda i, s, place: (i, 0))]),
        out_shape=[jax.ShapeDtypeStruct((n, half, C), MXU_DTYPE), jax.ShapeDtypeStruct((half, C), F32)],
        compiler_params=_params(("arbitrary", "arbitrary")),
    )(core_and_chip, held, theirs)


def _sibling_copies(parts):
    n = len(parts)

    def build(ins, outs, send_sems, recv_sems, local_sems):
        x, y, c = _place()
        copies = [pltpu.make_async_remote_copy(
            src_ref=ins[w], dst_ref=outs[w], send_sem=send_sems.at[w], recv_sem=recv_sems.at[w],
            device_id=(x, y, 1 - c), device_id_type=MESH_ID) for w in range(n)]
        return copies, copies, []

    return _Carried(parts, [jax.ShapeDtypeStruct(p.shape, p.dtype) for p in parts], n, 0, build)


def _row_tile(rows, most=512, sublanes=16):
    return max(t for t in range(sublanes, min(most, rows // 2) + 1, sublanes) if rows % t == 0)


def _partial_sum(name, own, recv, chip, comm=None):
    _, R, C = recv.shape
    tr = _row_tile(R)

    def body(o_ref, r_ref, p_ref):
        p_ref[...] = ((o_ref[...] + r_ref[0].astype(F32)) + r_ref[1].astype(F32)) + r_ref[2].astype(F32)

    if own.shape[0] != R:
        assert comm is None and own.shape[0] == N_CHIPS * R
        total = pl.pallas_call(
            lambda chip_ref, *refs: body(*refs), name=name,
            grid_spec=pltpu.PrefetchScalarGridSpec(
                num_scalar_prefetch=1, grid=(R // tr,),
                in_specs=[pl.BlockSpec((tr, C), lambda i, chip_ref: (chip_ref[0] * (R // tr) + i, 0)),
                          pl.BlockSpec((3, tr, C), lambda i, chip_ref: (0, i, 0))],
                out_specs=pl.BlockSpec((tr, C), lambda i, chip_ref: (i, 0))),
            out_shape=jax.ShapeDtypeStruct((R, C), F32), compiler_params=_params(("parallel",)),
        )(chip.reshape(1), own, recv)
        return [total], []
    return _call(name, body, grid=(R // tr,),
                 in_specs=[pl.BlockSpec((tr, C), lambda i: (i, 0)), pl.BlockSpec((3, tr, C), lambda i: (0, i, 0))],
                 out_specs=[pl.BlockSpec((tr, C), lambda i: (i, 0))], out_shape=[jax.ShapeDtypeStruct((R, C), F32)],
                 args=[own, recv], semantics=("parallel",), comm=comm)


def _adam_vals(w, g, m, v):
    m = ADAM_B1 * m + (1.0 - ADAM_B1) * g
    v = ADAM_B2 * v + (1.0 - ADAM_B2) * (g * g)
    m_hat = m / (1.0 - ADAM_B1 ** ADAM_STEP)
    v_hat = v / (1.0 - ADAM_B2 ** ADAM_STEP)
    delta = -ADAM_LR * (m_hat / (jnp.sqrt(v_hat) + ADAM_EPS) + ADAM_WD * w)
    return delta, m, v


def _adamw(name, w, m, v, mine, other, comm=None):
    R, C = w.shape
    tr = _row_tile(R)

    def body(w_ref, m_ref, v_ref, s_ref, n_ref, g_ref, d_ref, nm_ref, nv_ref):
        g = s_ref[...] + n_ref[...]
        d, nm, nv = _adam_vals(w_ref[...], g, m_ref[...], v_ref[...])
        g_ref[...], d_ref[...], nm_ref[...], nv_ref[...] = g, d, nm, nv

    spec = pl.BlockSpec((tr, C), lambda i: (i, 0))
    return _call(name, body, grid=(R // tr,), in_specs=[spec] * 5, out_specs=[spec] * 4,
                 out_shape=[jax.ShapeDtypeStruct((R, C), F32)] * 4, args=[w, m, v, mine, other],
                 semantics=("parallel",), comm=comm)


def _adamw_by_halves(name, w, m, v, mine, other, core):
    R, C = w.shape
    tr = _row_tile(R // 2)
    per_half = R // 2 // tr

    def body(c_ref, w_ref, m_ref, v_ref, s_ref, n_ref, g_ref, d_ref, nm_ref, nv_ref):
        g = jnp.where(pl.program_id(0) // per_half == c_ref[0, 0], s_ref[...], n_ref[...])
        d, nm, nv = _adam_vals(w_ref[...], g, m_ref[...], v_ref[...])
        g_ref[...], d_ref[...], nm_ref[...], nv_ref[...] = g, d, nm, nv

    spec = pl.BlockSpec((tr, C), lambda i: (i, 0))
    part = pl.BlockSpec((tr, C), lambda i: (i % per_half, 0))
    return pl.pallas_call(
        body, name=name, grid=(R // tr,),
        in_specs=[pl.BlockSpec(memory_space=pltpu.SMEM), spec, spec, spec, part, part], out_specs=[spec] * 4,
        out_shape=[jax.ShapeDtypeStruct((R, C), F32)] * 4, compiler_params=_params(("parallel",)),
    )(core, w, m, v, mine, other)


SMALL_LAYOUT = dict(norm_mix_g=(0, 1, 1024), b_in=(1, 8, 7424), hgrn_norm_g=(9, 1, 1024), norm_ffn_g=(10, 1, 1024),
                    norm_final_g=(11, 1, 1024), hgrn_lb_logits=(12, 2, 2048), attn_sinks=(14, 1, 16))
SMALL_LOSS_ROW, SMALL_ROWS = 15, 16


def _pack_small(grads, loss):
    rows = [jnp.pad(grads[name].astype(F32).reshape(-1), (0, nrows * D_MODEL - n))
            for name, (_, nrows, n) in SMALL_LAYOUT.items()]
    rows.append(jnp.pad(loss.astype(F32).reshape(1), (0, D_MODEL - 1)))
    return jnp.concatenate(rows).reshape(SMALL_ROWS, D_MODEL)


def _adamw_small(w, m, v, g_all):
    names = list(SMALL_LAYOUT)
    n = len(names)

    def body(a_ref, *refs):
        ins, outs = refs[:3 * n], refs[3 * n:]
        g_all_rows = a_ref[0]
        for dev in range(1, 8):
            g_all_rows = g_all_rows + a_ref[dev]
        for i, name in enumerate(names):
            first, nrows, count = SMALL_LAYOUT[name]
            w_ref, m_ref, v_ref = ins[3 * i:3 * i + 3]
            if w_ref.shape[0] == nrows:
                g = g_all_rows[first:first + nrows, :w_ref.shape[1]]
            else:
                last = count - (nrows - 1) * D_MODEL
                g = jnp.concatenate([g_all_rows[r:r + 1, :] for r in range(first, first + nrows - 1)]
                                    + [g_all_rows[first + nrows - 1:first + nrows, :last]], axis=1)
            d, nm, nv = _adam_vals(w_ref[...], g, m_ref[...], v_ref[...])
            for o_ref, val in zip(outs[4 * i:4 * i + 4], (g, d, nm, nv)):
                o_ref[...] = val
        outs[4 * n][...] = g_all_rows[SMALL_LOSS_ROW:SMALL_LOSS_ROW + 1, 0:1]

    res = pl.pallas_call(
        body, name="adamw_small",
        out_shape=[jax.ShapeDtypeStruct(w[name].shape, F32) for name in names for _ in range(4)]
        + [jax.ShapeDtypeStruct((1, 1), F32)],
    )(g_all, *[t[name] for name in names for t in (w, m, v)])
    return {name: res[4 * i:4 * i + 4] for i, name in enumerate(names)}, res[4 * n]


MATRICES = ("w_in", "w_branch_attn", "w_branch_hgrn", "w_out", "w_ffn_gate", "w_ffn_up", "w_ffn_down")
COLUMN_SHARDED = ("w_in", "w_ffn_gate", "w_ffn_up")
WEIGHTS = ("norm_mix_g", "w_in", "b_in", "attn_sinks", "hgrn_lb_logits", "hgrn_norm_g", "w_branch_attn",
           "w_branch_hgrn", "w_out", "norm_ffn_g", "w_ffn_gate", "w_ffn_up", "w_ffn_down", "norm_final_g")


def kernel(x, norm_mix_g, w_in, b_in, attn_sinks, hgrn_lb_logits, hgrn_norm_g, w_branch_attn, w_branch_hgrn, w_out, norm_ffn_g, w_ffn_gate, w_ffn_up, w_ffn_down, norm_final_g, loss_target, m_norm_mix_g, m_w_in, m_b_in, m_attn_sinks, m_hgrn_lb_logits, m_hgrn_norm_g, m_w_branch_attn, m_w_branch_hgrn, m_w_out, m_norm_ffn_g, m_w_ffn_gate, m_w_ffn_up, m_w_ffn_down, m_norm_final_g, v_norm_mix_g, v_w_in, v_b_in, v_attn_sinks, v_hgrn_lb_logits, v_hgrn_norm_g, v_w_branch_attn, v_w_branch_hgrn, v_w_out, v_norm_ffn_g, v_w_ffn_gate, v_w_ffn_up, v_w_ffn_down, v_norm_final_g):
    given = dict(locals())
    w = {n: given[n] for n in WEIGHTS}
    m = {n: given["m_" + n] for n in WEIGHTS}
    v = {n: given["v_" + n] for n in WEIGHTS}

    block = lambda a, n: jnp.transpose(a[0]) if n in COLUMN_SHARDED else a[0]
    unblock = lambda a, n: (jnp.transpose(a) if n in COLUMN_SHARDED else a)[None]
    net = _Net({n: block(w[n], n).astype(MXU_DTYPE) for n in MATRICES})
    net.gathered(("w_in",), [_gather_by_neighbours("gather_w_in", net.shards["w_in"])])
    vec = dict(norm_mix_g=norm_mix_g, b_in=b_in, attn_sinks=attn_sinks, hgrn_lb_logits=hgrn_lb_logits,
               hgrn_norm_g=hgrn_norm_g, norm_ffn_g=norm_ffn_g, norm_final_g=norm_final_g.reshape(1, D_MODEL))
    loss_part, dx, d_vecs = _local_step(x[0], loss_target[0], vec, net)

    small_all, = net.received(*net.last, carried=_small_copies(_pack_small(d_vecs, loss_part)))
    grads, deltas, new_m, new_v = {}, {}, {}, {}
    for n in ("w_ffn_down", "w_ffn_gate", "w_ffn_up", "w_out", "w_branch_attn", "w_branch_hgrn"):
        res, got = _adamw("adamw_" + n, block(w[n], n), block(m[n], n), block(v[n], n), net.sums[n], net.other[n],
                          comm=net.swap(("w_in",)) if n == "w_ffn_down" else None)
        if n == "w_ffn_down":
            net.other["w_in"], = got
        grads[n], deltas[n], new_m[n], new_v[n] = (unblock(r, n) for r in res)
    n = "w_in"
    res = _adamw_by_halves("adamw_" + n, block(w[n], n), block(m[n], n), block(v[n], n), net.sums[n], net.other[n],
                           _place()[2].reshape(1, 1))
    grads[n], deltas[n], new_m[n], new_v[n] = (unblock(r, n) for r in res)
    rows = lambda t: {n: t[n].reshape(-1, t[n].shape[-1]) for n in SMALL_LAYOUT}
    res, loss = _adamw_small(rows(w), rows(m), rows(v), small_all)
    for n, four in res.items():
        grads[n], deltas[n], new_m[n], new_v[n] = (r.reshape(w[n].shape) for r in four)
    loss = loss.reshape(())
    return (loss, dx[None], *[grads[n] for n in WEIGHTS], *[deltas[n] for n in WEIGHTS],
            *[new_m[n] for n in WEIGHTS], *[new_v[n] for n in WEIGHTS])
```

```python
import collections
import functools
import math

import jax
import jax.numpy as jnp
from jax import lax
from jax.experimental import pallas as pl
from jax.experimental.pallas import tpu as pltpu

F32 = jnp.float32
BF16 = jnp.bfloat16
MXU_DTYPE = jnp.bfloat16
SAVED_DTYPE = jnp.bfloat16
MESH_ID = pl.DeviceIdType.MESH

D_MODEL = 1024
HEAD_DIM = 64
Q_HEADS = 16
KV_HEADS = 2
GROUP = Q_HEADS // KV_HEADS
KV_WIDTH = KV_HEADS * HEAD_DIM
ATTN_BLOCK = 128
HGRN_HEADS = 8
HGRN_K = 128
CHUNK = 64
HGRN_TOKENS = 256
FFN = 2816
IN_SPLITS = (1024, 256, 4096, 2048)
EPS = 1e-6
NEG_INF = -1e30
ADAM_LR, ADAM_B1, ADAM_B2, ADAM_EPS, ADAM_WD, ADAM_STEP = 0.001, 0.9, 0.999, 1e-08, 0.01, 10
N_CHIPS = 4
VMEM_LIMIT = 60 * 1024 * 1024
ROW_ALIGN = 16
DW_ROWS = 256


def _params(sem=None):
    return pltpu.CompilerParams(dimension_semantics=sem, vmem_limit_bytes=VMEM_LIMIT)


def _sigmoid(v):
    return 0.5 * jnp.tanh(0.5 * v) + 0.5


def _dot(a, b, dims):
    return lax.dot_general(a.astype(MXU_DTYPE), b.astype(MXU_DTYPE), (dims, ((), ())),
                           preferred_element_type=F32)


def _nn(a, b):
    return _dot(a, b, ((1,), (0,)))


def _nt(a, b):
    return _dot(a, b, ((1,), (1,)))


def _tn(a, b):
    return _dot(a, b, ((0,), (0,)))


HBM_SPEC = pl.BlockSpec(memory_space=pl.ANY)


class _Carried:
    def __init__(self, arrays, out_shapes, n_remote, n_local, build, continued=None, stages=()):
        self.continued = dict(continued or {})
        self.parts = [(len(arrays), len(out_shapes), 3 + len(stages), build)]
        self.arrays, self.out_shapes = list(arrays), list(out_shapes)
        self.scratch = [pltpu.SemaphoreType.DMA((n_remote,)), pltpu.SemaphoreType.DMA((n_remote,)),
                        pltpu.SemaphoreType.DMA((max(n_local, 1),))] + list(stages)

    def __add__(self, other):
        both = _Carried([], [], 1, 0, None)
        both.parts = self.parts + other.parts
        both.arrays, both.out_shapes = self.arrays + other.arrays, self.out_shapes + other.out_shapes
        both.scratch = self.scratch + other.scratch
        both.continued = dict(self.continued)
        both.continued.update({len(self.arrays) + i: len(self.out_shapes) + o for i, o in other.continued.items()})
        return both

    def _built(self, ins, outs, sems):
        for ni, no, ns, build in self.parts:
            yield build(ins[:ni], outs[:no], *sems[:ns])
            ins, outs, sems = ins[ni:], outs[no:], sems[ns:]

    def start(self, ins, outs, sems):
        core = lax.axis_index("c")
        for sends, _, local, *other_order in self._built(ins, outs, sems):
            for cp in local:
                if not isinstance(cp, tuple):
                    cp.start()
            if not other_order:
                for cp in sends:
                    cp.start()
            else:
                @pl.when(core == 0)
                def _():
                    for cp in sends:
                        cp.start()

                @pl.when(core == 1)
                def _():
                    for cp in other_order[0]:
                        cp.start()
            staged = [cp for cp in local if isinstance(cp, tuple)]
            for into, _ in staged:
                into.start()
            for into, out_of in staged:
                into.wait()
                out_of.start()

    def wait(self, ins, outs, sems):
        for sends, recvs, local, *_ in self._built(ins, outs, sems):
            for cp in recvs:
                cp.wait_recv()
            for cp in sends:
                cp.wait_send()
            for cp in local:
                (cp[1] if isinstance(cp, tuple) else cp).wait()


def _join(*comms):
    comms = [c for c in comms if c is not None]
    return functools.reduce(lambda a, b: a + b, comms) if comms else None


def _call(name, body, *, grid, in_specs, out_specs, out_shape, args, scratch=(), semantics=None, comm=None,
          aliases=None):
    n_in, n_out, n_scr = len(in_specs), len(out_specs), len(scratch)
    aliases = aliases or {}
    if comm is None:
        res = pl.pallas_call(body, name=name, grid=grid, in_specs=in_specs, out_specs=out_specs, out_shape=out_shape,
                             scratch_shapes=list(scratch), input_output_aliases=aliases,
                             compiler_params=_params(semantics))(*args)
        return list(res), []
    ci, co = len(comm.arrays), len(comm.out_shapes)
    aliases = dict(aliases)
    aliases.update({n_in + i: n_out + o for i, o in comm.continued.items()})

    def carrying(*refs):
        ins, refs = refs[:n_in], refs[n_in:]
        c_ins, refs = refs[:ci], refs[ci:]
        outs, refs = refs[:n_out], refs[n_out:]
        c_outs, refs = refs[:co], refs[co:]
        scr, sems = refs[:n_scr], refs[n_scr:]
        if not grid:
            comm.start(c_ins, c_outs, sems)
            body(*ins, *outs, *scr)
            comm.wait(c_ins, c_outs, sems)
            return
        first = functools.reduce(jnp.logical_and, [pl.program_id(a) == 0 for a in range(len(grid))])
        last = functools.reduce(jnp.logical_and, [pl.program_id(a) == g - 1 for a, g in enumerate(grid)])

        @pl.when(first)
        def _():
            comm.start(c_ins, c_outs, sems)

        body(*ins, *outs, *scr)

        @pl.when(last)
        def _():
            comm.wait(c_ins, c_outs, sems)

    res = pl.pallas_call(
        carrying, name=name, grid=grid, in_specs=list(in_specs) + [HBM_SPEC] * ci,
        out_specs=list(out_specs) + [HBM_SPEC] * co, out_shape=list(out_shape) + comm.out_shapes,
        scratch_shapes=list(scratch) + comm.scratch, input_output_aliases=aliases,
        compiler_params=_params(("arbitrary",) * len(grid) if grid else None),
    )(*args, *comm.arrays)
    return list(res[:n_out]), list(res[n_out:])


_Cols = collections.namedtuple("_Cols", "array first cols")
_Into = collections.namedtuple("_Into", "rows first held")


def _weight_grad(name, a, b, *, tm, tk, a_colsum=False, into=None, carrying=False, comm=None):
    cols = a if isinstance(a, _Cols) else _Cols(a, 0, a.shape[1])
    a = cols.array
    (T, N), M = b.shape, cols.cols
    tk = min(tk, T)
    assert cols.first % tm == 0 and M % tm == 0 and T % tk == 0, (name, cols.first, M, tm, T, tk)
    ni, nk, tile0 = M // tm, T // tk, cols.first // tm
    held = list(into.held) if into is not None and into.held is not None else []

    def body(a_ref, b_ref, *rest):
        keep_ref, send_ref = rest[len(held):len(held) + 2]
        sums_ref = rest[len(held) + 2] if a_colsum else None
        if nk == 1:
            acc = _tn(a_ref[...], b_ref[...])
            keep_ref[...], send_ref[...] = acc, acc.astype(send_ref.dtype)
            if a_colsum:
                sums_ref[...] = jnp.sum(a_ref[...].astype(F32), axis=0, keepdims=True)
            return
        acc_ref = rest[-1]
        k = pl.program_id(1)

        @pl.when(k == 0)
        def _():
            acc_ref[...] = jnp.zeros_like(acc_ref)
            if a_colsum:
                sums_ref[...] = jnp.zeros((1, tm), F32)

        if a_colsum:
            sums_ref[...] += jnp.sum(a_ref[...].astype(F32), axis=0, keepdims=True)
        acc_ref[...] += _tn(a_ref[...], b_ref[...])

        @pl.when(k == nk - 1)
        def _():
            keep_ref[...], send_ref[...] = acc_ref[...], acc_ref[...].astype(send_ref.dtype)

    if into is None:
        rows, out_spec = M, pl.BlockSpec((tm, N), lambda i, k: (i, 0))
    else:
        rows = into.rows
        out_spec = pl.BlockSpec((pl.Element(tm), pl.Element(N)),
                                lambda i, k: (pl.multiple_of(into.first + i * tm, ROW_ALIGN), 0))
    out_shape = [jax.ShapeDtypeStruct((rows, N), F32), jax.ShapeDtypeStruct((rows, N), MXU_DTYPE)]
    out_specs = [out_spec, out_spec]
    if a_colsum:
        out_shape.append(jax.ShapeDtypeStruct((1, M), F32))
        out_specs.append(pl.BlockSpec((1, tm), lambda i, k: (0, i)))
    res, got = _call(
        name, body, grid=(ni, nk),
        in_specs=[pl.BlockSpec((tk, tm), lambda i, k: (k, tile0 + i)), pl.BlockSpec((tk, N), lambda i, k: (k, 0))]
        + [HBM_SPEC] * len(held),
        out_specs=out_specs, out_shape=out_shape, scratch=[pltpu.VMEM((tm, N), F32)] if nk > 1 else [],
        args=[a, b] + held, aliases={2 + p: p for p in range(len(held))}, semantics=("parallel", "arbitrary"),
        comm=comm)
    return (res, got) if carrying else res


def _ffn_fwd(merged, w_out, x, gain, w_gate_t, w_up_t, *, tm, comm=None):
    (T, D), F = x.shape, w_gate_t.shape[0]

    def body(m_ref, wo_ref, x_ref, g_ref, wg_ref, wu_ref, h_ref, u_ref, gate_ref, up_ref, z_ref):
        h = x_ref[...] + _nn(m_ref[...], wo_ref[...])
        h_ref[...] = h
        u = (h * lax.rsqrt(jnp.mean(h * h, axis=-1, keepdims=True) + EPS) * g_ref[...]).astype(u_ref.dtype)
        u_ref[...] = u
        gate, up = _nt(u, wg_ref[...]), _nt(u, wu_ref[...])
        gate_ref[...], up_ref[...] = gate.astype(gate_ref.dtype), up.astype(up_ref.dtype)
        z_ref[...] = (gate * _sigmoid(gate) * up).astype(z_ref.dtype)

    rows = lambda n: pl.BlockSpec((tm, n), lambda i: (i, 0))
    fixed = _fixed_spec
    return _call("ffn_hidden", body, grid=(T // tm,),
                 in_specs=[rows(D), fixed(w_out), rows(D), fixed(gain), fixed(w_gate_t), fixed(w_up_t)],
                 out_specs=[rows(D), rows(D), rows(F), rows(F), rows(F)],
                 out_shape=[jax.ShapeDtypeStruct((T, D), F32), jax.ShapeDtypeStruct((T, D), MXU_DTYPE)]
                 + [jax.ShapeDtypeStruct((T, F), SAVED_DTYPE)] * 2 + [jax.ShapeDtypeStruct((T, F), MXU_DTYPE)],
                 args=[merged, w_out, x, gain, w_gate_t, w_up_t], semantics=("parallel",), comm=comm)


def _in_proj(x, gain, w_in_t, b_in, *, tm, comm=None):
    T, D = x.shape
    bounds = [sum(IN_SPLITS[:i]) for i in range(len(IN_SPLITS) + 1)]

    def body(x_ref, g_ref, w_ref, b_ref, u_ref, *piece_refs):
        xv = x_ref[...]
        r = lax.rsqrt(jnp.mean(xv * xv, axis=-1, keepdims=True) + EPS)
        u = (xv * r * g_ref[...]).astype(u_ref.dtype)
        u_ref[...] = u
        for o_ref, lo, hi in zip(piece_refs, bounds[:-1], bounds[1:]):
            o_ref[...] = (_nt(u, w_ref[lo:hi, :]) + b_ref[:, lo:hi]).astype(o_ref.dtype)

    rows = lambda n: pl.BlockSpec((tm, n), lambda i: (i, 0))
    fixed = _fixed_spec
    dtypes = (MXU_DTYPE, MXU_DTYPE, F32, F32)
    return _call("in_proj", body, grid=(T // tm,),
                 in_specs=[rows(D), fixed(gain), fixed(w_in_t), fixed(b_in)],
                 out_specs=[rows(D)] + [rows(n) for n in IN_SPLITS],
                 out_shape=[jax.ShapeDtypeStruct((T, D), MXU_DTYPE)]
                 + [jax.ShapeDtypeStruct((T, n), dt) for n, dt in zip(IN_SPLITS, dtypes)],
                 args=[x, gain, w_in_t, b_in], semantics=("parallel",), comm=comm)


def _row_spec(tm, n):
    return pl.BlockSpec((tm, n), lambda i: (i, 0))


def _fixed_spec(a):
    return pl.BlockSpec(a.shape, lambda i: (0,) * a.ndim, pipeline_mode=pl.Buffered(1))


def _partials_spec(n):
    return pl.BlockSpec((8, n), lambda i: (i, 0))


def _row_parts(tm, parts):
    assert tm % parts == 0, (tm, parts)
    return [slice(p * (tm // parts), (p + 1) * (tm // parts)) for p in range(parts)]


def _ffn_tail(z, w_down, h1, target, gain, gate, up, *, tm, parts=2):
    (T, F), D = z.shape, h1.shape[1]

    def body(z_ref, w_ref, h_ref, t_ref, g_ref, gate_ref, up_ref, dh_ref, dhb_ref, dgate_ref, dup_ref, dg_ref, l_ref):
        part, dgain = 0.0, 0.0
        pieces = _row_parts(tm, parts)
        h2s = [h_ref[rows, :] + _nn(z_ref[rows, :], w_ref[...]) for rows in pieces]
        for rows, h2 in zip(pieces, h2s):
            r = lax.rsqrt(jnp.mean(h2 * h2, axis=-1, keepdims=True) + EPS)
            xhat = h2 * r
            err = xhat * g_ref[...] - t_ref[rows, :]
            part += 0.5 * jnp.sum(jnp.sum(err * err, axis=-1, keepdims=True), axis=0, keepdims=True) / D
            dy = err / D
            dxh = dy * g_ref[...]
            dh2 = r * (dxh - xhat * jnp.mean(dxh * xhat, axis=-1, keepdims=True))
            dh_ref[rows, :] = dh2
            dhb = dh2.astype(dhb_ref.dtype)
            dhb_ref[rows, :] = dhb
            dgain += jnp.sum(dy * xhat, axis=0, keepdims=True)
            dz = _nt(dhb, w_ref[...])
            gv, upv = gate_ref[rows, :].astype(F32), up_ref[rows, :].astype(F32)
            s = _sigmoid(gv)
            dgate_ref[rows, :] = (dz * upv * (s * (1.0 + gv * (1.0 - s)))).astype(dgate_ref.dtype)
            dup_ref[rows, :] = (dz * (gv * s)).astype(dup_ref.dtype)
        dg_ref[...] = jnp.broadcast_to(dgain, dg_ref.shape)
        l_ref[...] = jnp.broadcast_to(part, l_ref.shape)

    low = lambda n: jax.ShapeDtypeStruct((T, n), MXU_DTYPE)
    part = jax.ShapeDtypeStruct((8 * (T // tm), D), F32)
    return pl.pallas_call(
        body, name="ffn_tail", grid=(T // tm,),
        in_specs=[_row_spec(tm, F), _fixed_spec(w_down), _row_spec(tm, D), _row_spec(tm, D), _fixed_spec(gain),
                  _row_spec(tm, F), _row_spec(tm, F)],
        out_specs=[_row_spec(tm, D), _row_spec(tm, D), _row_spec(tm, F), _row_spec(tm, F), _partials_spec(D),
                   _partials_spec(D)],
        out_shape=[jax.ShapeDtypeStruct((T, D), F32), low(D), low(F), low(F), part, part],
        compiler_params=_params(("parallel",)),
    )(z, w_down, h1, target, gain, gate, up)


def _ffn_in_bwd(dgate, dup, w_gate_t, w_up_t, h1, gain, dres, *, tm, comm=None):
    (T, F), D = dgate.shape, h1.shape[1]

    def body(dg_ref, du_ref, wg_ref, wu_ref, h_ref, g_ref, r_ref, dh_ref, dhb_ref, dgain_ref):
        d_u2 = _nn(dg_ref[...], wg_ref[...]) + _nn(du_ref[...], wu_ref[...])
        dx, dgain = _rmsnorm_bwd_vals(d_u2, h_ref[...], g_ref[...])
        dh = r_ref[...] + dx
        dh_ref[...] = dh
        dhb_ref[...] = dh.astype(dhb_ref.dtype)
        dgain_ref[...] = jnp.broadcast_to(dgain, dgain_ref.shape)

    return _call("d_ffn_in", body, grid=(T // tm,),
                 in_specs=[_row_spec(tm, F), _row_spec(tm, F), _fixed_spec(w_gate_t), _fixed_spec(w_up_t),
                           _row_spec(tm, D), _fixed_spec(gain), _row_spec(tm, D)],
                 out_specs=[_row_spec(tm, D), _row_spec(tm, D), _partials_spec(D)],
                 out_shape=[jax.ShapeDtypeStruct((T, D), F32), jax.ShapeDtypeStruct((T, D), MXU_DTYPE),
                            jax.ShapeDtypeStruct((8 * (T // tm), D), F32)],
                 args=[dgate, dup, w_gate_t, w_up_t, h1, gain, dres], semantics=("parallel",), comm=comm)


def _in_proj_bwd(pieces, w_in_t, x, gain, dres, *, tm, comm=None):
    T, D = x.shape
    n = len(pieces)

    def body(*refs):
        dps, (w_ref, x_ref, g_ref, r_ref, dx_ref, dgain_ref) = refs[:n], refs[n:]
        d_u = None
        for dp_ref, (dp, first) in zip(dps, pieces):
            term = _nn(dp_ref[...], w_ref[first:first + dp.shape[1], :])
            d_u = term if d_u is None else d_u + term
        dx, dgain = _rmsnorm_bwd_vals(d_u, x_ref[...], g_ref[...])
        dx_ref[...] = r_ref[...] + dx
        dgain_ref[...] = jnp.broadcast_to(dgain, dgain_ref.shape)

    return _call("d_u", body, grid=(T // tm,),
                 in_specs=[_row_spec(tm, dp.shape[1]) for dp, _ in pieces]
                 + [_fixed_spec(w_in_t), _row_spec(tm, D), _fixed_spec(gain), _row_spec(tm, D)],
                 out_specs=[_row_spec(tm, D), _partials_spec(D)],
                 out_shape=[jax.ShapeDtypeStruct((T, D), F32), jax.ShapeDtypeStruct((8 * (T // tm), D), F32)],
                 args=[dp for dp, _ in pieces] + [w_in_t, x, gain, dres], semantics=("parallel",), comm=comm)


def _merge_fwd(y_a, y_b, w_a, w_b, gates, *, tm, comm=None):
    T, D = y_a.shape

    def body(ya_ref, yb_ref, wa_ref, wb_ref, ga_ref, gb_ref, pa_ref, pb_ref, m_ref):
        pa, pb = _nn(ya_ref[...], wa_ref[...]), _nn(yb_ref[...], wb_ref[...])
        pa_ref[...], pb_ref[...] = pa.astype(pa_ref.dtype), pb.astype(pb_ref.dtype)
        m_ref[...] = (_sigmoid(ga_ref[...]) * pa + _sigmoid(gb_ref[...]) * pb).astype(m_ref.dtype)

    rows = pl.BlockSpec((tm, D), lambda i: (i, 0))
    whole = pl.BlockSpec((D, D), lambda i: (0, 0), pipeline_mode=pl.Buffered(1))
    return _call("branch_merge", body, grid=(T // tm,),
                 in_specs=[rows, rows, whole, whole, rows, pl.BlockSpec((tm, D), lambda i: (i, 1))],
                 out_specs=[rows] * 3,
                 out_shape=[jax.ShapeDtypeStruct((T, D), SAVED_DTYPE)] * 2 + [jax.ShapeDtypeStruct((T, D), MXU_DTYPE)],
                 args=[y_a, y_b, w_a, w_b, gates, gates], semantics=("parallel",), comm=comm)


def _merge_bwd(dh, w_out, w_a, w_b, p_a, p_b, gates, *, tm, d_in_width, first):
    T, D = dh.shape

    def body(dh_ref, wo_ref, wa_ref, wb_ref, pa_ref, pb_ref, ga_ref, gb_ref, dpa_ref, dpb_ref, dya_ref, dyb_ref,
             din_ref):
        dm = _nt(dh_ref[...], wo_ref[...])
        sa, sb = _sigmoid(ga_ref[...]), _sigmoid(gb_ref[...])
        dpa, dpb = (dm * sa).astype(dpa_ref.dtype), (dm * sb).astype(dpb_ref.dtype)
        dpa_ref[...], dpb_ref[...] = dpa, dpb
        din_ref[:, :D] = (dm * pa_ref[...].astype(F32) * sa * (1.0 - sa)).astype(din_ref.dtype)
        din_ref[:, D:] = (dm * pb_ref[...].astype(F32) * sb * (1.0 - sb)).astype(din_ref.dtype)
        dya_ref[...] = _nt(dpa, wa_ref[...]).astype(dya_ref.dtype)
        dyb_ref[...] = _nt(dpb, wb_ref[...]).astype(dyb_ref.dtype)

    rows = pl.BlockSpec((tm, D), lambda i: (i, 0))
    whole = pl.BlockSpec((D, D), lambda i: (0, 0), pipeline_mode=pl.Buffered(1))
    low = jax.ShapeDtypeStruct((T, D), MXU_DTYPE)
    return pl.pallas_call(
        body, name="d_branch_merge", grid=(T // tm,),
        in_specs=[rows, whole, whole, whole, rows, rows, rows, pl.BlockSpec((tm, D), lambda i: (i, 1))],
        out_specs=[rows] * 4 + [pl.BlockSpec((pl.Element(tm), pl.Element(2 * D)), lambda i: (
            pl.multiple_of(i * tm, ROW_ALIGN), first))],
        out_shape=[low] * 3 + [jax.ShapeDtypeStruct((T, D), F32), jax.ShapeDtypeStruct((T, d_in_width), MXU_DTYPE)],
        compiler_params=_params(("parallel",)),
    )(dh, w_out, w_a, w_b, p_a, p_b, gates, gates)


def _colsum_partials(p):
    return jnp.sum(p.reshape(-1, 8, p.shape[-1])[:, 0, :], axis=0, keepdims=True)


def _rmsnorm_bwd_vals(dy, xin, g):
    rstd = lax.rsqrt(jnp.mean(xin * xin, axis=-1, keepdims=True) + EPS)
    xhat = xin * rstd
    dg = jnp.sum(dy * xhat, axis=0, keepdims=True)
    dxh = dy * g
    dx = rstd * (dxh - xhat * jnp.mean(dxh * xhat, axis=-1, keepdims=True))
    return dx, dg


ATTN_SCALE = 1.0 / math.sqrt(HEAD_DIM)
GROUP_LANES = GROUP * ATTN_BLOCK
PAIR = 2 * HEAD_DIM


def _attn_mask():
    kj = lax.broadcasted_iota(jnp.int32, (ATTN_BLOCK, GROUP_LANES), 0)
    qi = lax.broadcasted_iota(jnp.int32, (ATTN_BLOCK, GROUP_LANES), 1) & (ATTN_BLOCK - 1)
    return kj <= qi


def _heads_transposed(ref, g, scale=None):
    parts = []
    for a in range(GROUP // 2):
        lo = (g * GROUP // 2 + a) * PAIR
        pair = ref[:, lo:lo + PAIR].astype(F32)
        pair = (pair if scale is None else pair * scale).T
        parts += [pair[:HEAD_DIM], pair[HEAD_DIM:]]
    return jnp.concatenate(parts, axis=1).astype(MXU_DTYPE)


def _heads_back(ref, g, vt):
    for a in range(GROUP // 2):
        lo = (g * GROUP // 2 + a) * PAIR
        pair = jnp.concatenate([vt[:, (2 * a) * ATTN_BLOCK:(2 * a + 1) * ATTN_BLOCK],
                                vt[:, (2 * a + 1) * ATTN_BLOCK:(2 * a + 2) * ATTN_BLOCK]], axis=0)
        ref[:, lo:lo + PAIR] = pair.T.astype(ref.dtype)


def _kv_parts(kv_ref, g):
    ks = slice(g * HEAD_DIM, (g + 1) * HEAD_DIM)
    vs = slice(KV_WIDTH + g * HEAD_DIM, KV_WIDTH + (g + 1) * HEAD_DIM)
    return kv_ref[:, ks].astype(MXU_DTYPE), kv_ref[:, vs].astype(MXU_DTYPE)


def _sink_rows(sinks):
    return jnp.repeat(sinks.reshape(KV_HEADS, GROUP), ATTN_BLOCK, axis=1)


def _attn_fwd(pq, pkv, sinks, comm=None):
    T = pq.shape[0]
    nb = T // ATTN_BLOCK

    def body(q_ref, kvc_ref, kvp_ref, s_ref, y_ref, lse_ref):
        mask_c = _attn_mask()
        has_prev = pl.program_id(0) > 0
        early = []
        for g in range(KV_HEADS):
            (kc, vc), (kp, vp) = _kv_parts(kvc_ref, g), _kv_parts(kvp_ref, g)
            qt = _heads_transposed(q_ref, g, ATTN_SCALE)
            early.append((vc, vp, jnp.where(mask_c, _nn(kc, qt), jnp.where(has_prev, _nn(kp, qt), NEG_INF))))
        for g, (vc, vp, s) in enumerate(early):
            sink = s_ref[g:g + 1, :]
            m = jnp.maximum(jnp.max(s, axis=0, keepdims=True), sink)
            p = jnp.exp(s - m)
            den = jnp.sum(p, axis=0, keepdims=True) + jnp.exp(sink - m)
            pc = jnp.where(mask_c, p, 0.0)
            _heads_back(y_ref, g, (_tn(vc, pc) + _tn(vp, p - pc)) / den)
            lse = m + jnp.log(den)
            for i in range(GROUP):
                lse_ref[g * GROUP + i:g * GROUP + i + 1, :] = lse[:, i * ATTN_BLOCK:(i + 1) * ATTN_BLOCK]

    return _call(
        "attn_fwd", body, grid=(nb,),
        in_specs=[pl.BlockSpec((ATTN_BLOCK, D_MODEL), lambda n: (n, 0)),
                  pl.BlockSpec((ATTN_BLOCK, 2 * KV_WIDTH), lambda n: (n, 0)),
                  pl.BlockSpec((ATTN_BLOCK, 2 * KV_WIDTH), lambda n: (jnp.maximum(n - 1, 0), 0)),
                  pl.BlockSpec((KV_HEADS, GROUP_LANES), lambda n: (0, 0))],
        out_specs=[pl.BlockSpec((ATTN_BLOCK, D_MODEL), lambda n: (n, 0)),
                   pl.BlockSpec((Q_HEADS, ATTN_BLOCK), lambda n: (0, n))],
        out_shape=[jax.ShapeDtypeStruct((T, D_MODEL), MXU_DTYPE), jax.ShapeDtypeStruct((Q_HEADS, T), F32)],
        args=[pq, pkv, pkv, _sink_rows(sinks)], semantics=("parallel",), comm=comm)


def _attn_bwd(pq, pkv, sinks, lse, dy, d_in, comm=None):
    T = pq.shape[0]
    nb = T // ATTN_BLOCK
    cur = lambda n: (jnp.minimum(n, nb - 1), 0)
    done = D_MODEL + 2 * KV_WIDTH

    def body(q_ref, kvc_ref, kvp_ref, s_ref, lse_ref, dy_ref, _, out_ref, ds_ref, carry, top, bot, dq_ref):
        n = pl.program_id(0)

        @pl.when(n == 0)
        def _():
            carry[...] = jnp.zeros_like(carry)
            dq_ref[...] = jnp.zeros_like(dq_ref)
            ds_ref[...] = jnp.zeros_like(ds_ref)

        out_ref[:, :D_MODEL] = dq_ref[...]

        @pl.when(n < nb)
        def _():
            mask_c = _attn_mask()
            valid = jnp.logical_or(mask_c, n > 0)
            early = []
            for g in range(KV_HEADS):
                (kc, vc), (kp, vp) = _kv_parts(kvc_ref, g), _kv_parts(kvp_ref, g)
                qt = _heads_transposed(q_ref, g, ATTN_SCALE)
                dot = _heads_transposed(dy_ref, g)
                early.append((kc, kp, qt, dot, jnp.where(mask_c, _nn(kc, qt), _nn(kp, qt)),
                              jnp.where(mask_c, _nn(vc, dot), _nn(vp, dot))))
            for g, (kc, kp, qt, dot, s, dp) in enumerate(early):
                ks = slice(g * HEAD_DIM, (g + 1) * HEAD_DIM)
                vs = slice(KV_WIDTH + g * HEAD_DIM, KV_WIDTH + (g + 1) * HEAD_DIM)
                lse = jnp.concatenate([lse_ref[g * GROUP + i:g * GROUP + i + 1, :] for i in range(GROUP)], axis=1)
                p = jnp.where(valid, jnp.exp(s - lse), 0.0)
                delta = jnp.sum(p * dp, axis=0, keepdims=True)
                ds = p * (dp - delta)
                ds_c, p_c = jnp.where(mask_c, ds, 0.0), jnp.where(mask_c, p, 0.0)
                ds_p, p_p = ds - ds_c, p - p_c
                _heads_back(dq_ref, g, (_tn(kc, ds_c) + _tn(kp, ds_p)) * ATTN_SCALE)
                bot[:, ks], bot[:, vs] = _nt(ds_c, qt), _nt(p_c, dot)
                top[:, ks], top[:, vs] = _nt(ds_p, qt), _nt(p_p, dot)
                ds_ref[g:g + 1, :] -= jnp.exp(s_ref[g:g + 1, :] - lse) * delta
            out_ref[:, D_MODEL:] = (carry[...] + top[...]).astype(out_ref.dtype)
            carry[...] = bot[...]

        @pl.when(n == nb)
        def _():
            out_ref[:, D_MODEL:] = carry[...].astype(out_ref.dtype)

    return _call(
        "attn_bwd", body, grid=(nb + 1,),
        in_specs=[pl.BlockSpec((ATTN_BLOCK, D_MODEL), cur),
                  pl.BlockSpec((ATTN_BLOCK, 2 * KV_WIDTH), cur),
                  pl.BlockSpec((ATTN_BLOCK, 2 * KV_WIDTH), lambda n: (jnp.maximum(jnp.minimum(n, nb - 1) - 1, 0), 0)),
                  pl.BlockSpec((KV_HEADS, GROUP_LANES), lambda n: (0, 0)),
                  pl.BlockSpec((Q_HEADS, ATTN_BLOCK), lambda n: (0, jnp.minimum(n, nb - 1))),
                  pl.BlockSpec((ATTN_BLOCK, D_MODEL), cur), HBM_SPEC],
        out_specs=[pl.BlockSpec((ATTN_BLOCK, done), lambda n: (jnp.maximum(n - 1, 0), 0)),
                   pl.BlockSpec((KV_HEADS, GROUP_LANES), lambda n: (0, 0))],
        out_shape=[jax.ShapeDtypeStruct(d_in.shape, d_in.dtype), jax.ShapeDtypeStruct((KV_HEADS, GROUP_LANES), F32)],
        scratch=[pltpu.VMEM((ATTN_BLOCK, 2 * KV_WIDTH), F32)] * 3 + [pltpu.VMEM((ATTN_BLOCK, D_MODEL), MXU_DTYPE)],
        args=[pq, pkv, pkv, _sink_rows(sinks), lse, dy, d_in], semantics=("arbitrary",), comm=comm, aliases={6: 0})


def _lower_bound(l):
    m = jnp.maximum(l[0:1], l[1:2])
    e0, e1 = jnp.exp(l[0:1] - m), jnp.exp(l[1:2] - m)
    return e0 / (e0 + e1)


def _tri(lower):
    r = lax.broadcasted_iota(jnp.int32, (CHUNK, CHUNK), 0)
    c = lax.broadcasted_iota(jnp.int32, (CHUNK, CHUNK), 1)
    return (r >= c) if lower else (c >= r)


def _chunk_sum(mask, v):
    ones = mask.astype(BF16)
    hi = v.astype(BF16)
    rest = v - hi.astype(F32)
    mid = rest.astype(BF16)
    lo = (rest - mid.astype(F32)).astype(BF16)
    part = lambda t: lax.dot_general(ones, t, (((1,), (0,)), ((), ())), preferred_element_type=F32)
    return part(hi) + part(mid) + part(lo)


def _hgrn_chunk_inputs(hq, hf, lb, causal):
    half_t = 0.5 * jnp.tanh(0.5 * hf)
    sg, sgn = 0.5 + half_t, 0.5 - half_t
    f = lb + (1.0 - lb) * sg
    kk = (1.0 - lb) * sgn
    sq = _sigmoid(hq)
    q = hq * sq
    b = _chunk_sum(causal, jnp.log(f))
    bm, bl = b[CHUNK // 2 - 1:CHUNK // 2, :], b[CHUNK - 1:CHUNK, :]
    e_qm, e_km = jnp.exp(b - bm), jnp.exp(bm - b)
    e_qs, e_kl = e_qm * jnp.exp(bm), e_km * jnp.exp(bl - bm)
    return dict(sg=sg, sgn=sgn, f=f, kk=kk, sq=sq, q=q, e_qm=e_qm, e_km=e_km, e_qs=e_qs, e_kl=e_kl,
                qm=q * e_qm, km=kk * e_km, qs=q * e_qs, kl=kk * e_kl, el=jnp.exp(bl))


def _hgrn_fwd(ph, lb_logits, norm_g, comm=None):
    T = ph.shape[0]
    nblk, cpb = T // HGRN_TOKENS, HGRN_TOKENS // CHUNK
    col = lambda c: pl.BlockSpec((HGRN_TOKENS, D_MODEL), functools.partial(lambda i, c: (i, c), c=c))

    def body(hq_ref, hf_ref, hi_ref, hg_ref, l_ref, ng_ref, y_ref, o_ref, st_ref, s_ref):
        @pl.when(pl.program_id(0) == 0)
        def _():
            s_ref[...] = jnp.zeros_like(s_ref)

        lb = _lower_bound(l_ref[...])
        causal = _tri(True)
        for c in range(cpb):
            rows = slice(c * CHUNK, (c + 1) * CHUNK)
            t = _hgrn_chunk_inputs(hq_ref[rows, :], hf_ref[rows, :], lb, causal)
            qm, km, qs, kl = (t[n].astype(MXU_DTYPE) for n in ("qm", "km", "qs", "kl"))
            v = hi_ref[rows, :].astype(MXU_DTYPE)
            heads = [slice(h * HGRN_K, (h + 1) * HGRN_K) for h in range(HGRN_HEADS)]
            a_all = [jnp.where(causal, _nt(qm[:, ls], km[:, ls]), 0.0).astype(MXU_DTYPE) for ls in heads]
            for h, ls in enumerate(heads):
                st = s_ref[h]
                st_ref[c, ls, :] = st
                o_ref[rows, ls] = _nn(a_all[h], v[:, ls]) + _nt(qs[:, ls], st)
                s_ref[h] = t["el"][:, ls] * st + _tn(v[:, ls], kl[:, ls])
        for h in range(HGRN_HEADS):
            ls = slice(h * HGRN_K, (h + 1) * HGRN_K)
            o = o_ref[:, ls]
            r = lax.rsqrt(jnp.mean(o * o, axis=-1, keepdims=True) + EPS)
            y_ref[:, ls] = (o * r * ng_ref[:, ls] * _sigmoid(hg_ref[:, ls])).astype(y_ref.dtype)

    return _call(
        "hgrn_fwd", body, grid=(nblk,),
        in_specs=[col(0), col(1), col(2), col(3),
                  pl.BlockSpec((2, D_MODEL), lambda i: (0, 0)), pl.BlockSpec((1, D_MODEL), lambda i: (0, 0))],
        out_specs=[pl.BlockSpec((HGRN_TOKENS, D_MODEL), lambda i: (i, 0)),
                   pl.BlockSpec((HGRN_TOKENS, D_MODEL), lambda i: (i, 0)),
                   pl.BlockSpec((cpb, D_MODEL, HGRN_K), lambda i: (i, 0, 0))],
        out_shape=[jax.ShapeDtypeStruct((T, D_MODEL), MXU_DTYPE), jax.ShapeDtypeStruct((T, D_MODEL), F32),
                   jax.ShapeDtypeStruct((T // CHUNK, D_MODEL, HGRN_K), F32)],
        scratch=[pltpu.VMEM((HGRN_HEADS, HGRN_K, HGRN_K), F32)],
        args=[ph, ph, ph, ph, lb_logits, norm_g], semantics=("arbitrary",), comm=comm)


def _hgrn_bwd(ph, o_raw, states, dy, lb_logits, norm_g, d_in, first, comm=None):
    T = ph.shape[0]
    nblk, cpb = T // HGRN_TOKENS, HGRN_TOKENS // CHUNK
    rev = lambda i: nblk - 1 - i
    col = lambda c: pl.BlockSpec((HGRN_TOKENS, D_MODEL), functools.partial(lambda i, c: (rev(i), c), c=c))
    tok = pl.BlockSpec((HGRN_TOKENS, D_MODEL), lambda i: (rev(i), 0))

    def body(hq_ref, hf_ref, hi_ref, hg_ref, o_ref, st_ref, dy_ref, l_ref, ng_ref, _,
             dph_ref, dng_ref, dl_ref, dst_ref, dlb_ref, do_s, dqm_s, dkm_s, dqs_s, dkl_s, dv_s, del_s):
        i = pl.program_id(0)

        @pl.when(i == 0)
        def _():
            dst_ref[...] = jnp.zeros_like(dst_ref)
            dlb_ref[...] = jnp.zeros_like(dlb_ref)
            dng_ref[...] = jnp.zeros_like(dng_ref)

        lb = _lower_bound(l_ref[...])
        causal, anti = _tri(True), _tri(False)
        row = lax.broadcasted_iota(jnp.int32, (CHUNK, D_MODEL), 0)
        for c in reversed(range(cpb)):
            rows = slice(c * CHUNK, (c + 1) * CHUNK)
            hq = hq_ref[rows, :]
            t = _hgrn_chunk_inputs(hq, hf_ref[rows, :], lb, causal)
            sgg = _sigmoid(hg_ref[rows, :])
            dyv = dy_ref[rows, :]
            for h in range(HGRN_HEADS):
                ls = slice(h * HGRN_K, (h + 1) * HGRN_K)
                o = o_ref[rows, ls]
                r = lax.rsqrt(jnp.mean(o * o, axis=-1, keepdims=True) + EPS)
                nrm = o * r
                g_h = sgg[:, ls]
                dph_ref[rows, 3 * D_MODEL + h * HGRN_K:3 * D_MODEL + (h + 1) * HGRN_K] = (
                    dyv[:, ls] * nrm * ng_ref[:, ls] * g_h * (1.0 - g_h)).astype(dph_ref.dtype)
                dyg = dyv[:, ls] * g_h
                dng_ref[:, ls] += jnp.sum(dyg * nrm, axis=0, keepdims=True)
                dn = dyg * ng_ref[:, ls]
                do_s[:, ls] = r * (dn - nrm * jnp.mean(dn * nrm, axis=-1, keepdims=True))
            qm, km, qs, kl = (t[n].astype(MXU_DTYPE) for n in ("qm", "km", "qs", "kl"))
            v = hi_ref[rows, :].astype(MXU_DTYPE)
            do = do_s[...].astype(MXU_DTYPE)
            heads = [slice(h * HGRN_K, (h + 1) * HGRN_K) for h in range(HGRN_HEADS)]
            a_all = [jnp.where(causal, _nt(qm[:, ls], km[:, ls]), 0.0).astype(MXU_DTYPE) for ls in heads]
            da_all = [jnp.where(causal, _nt(do[:, ls], v[:, ls]), 0.0).astype(MXU_DTYPE) for ls in heads]
            for h, ls in enumerate(heads):
                st = st_ref[c, ls, :]
                dst = dst_ref[h]
                a, da = a_all[h], da_all[h]
                dv_s[:, ls] = _tn(a, do[:, ls]) + _nt(kl[:, ls], dst)
                dkl_s[:, ls] = _nn(v[:, ls], dst)
                dqs_s[:, ls] = _nn(do[:, ls], st)
                del_s[:, ls] = jnp.sum(dst * st, axis=0, keepdims=True)
                dst_ref[h] = _tn(do[:, ls], qs[:, ls]) + t["el"][:, ls] * dst
                dqm_s[:, ls] = _nn(da, km[:, ls])
                dkm_s[:, ls] = _tn(da, qm[:, ls])
            dqm, dkm, dqs, dkl = dqm_s[...], dkm_s[...], dqs_s[...], dkl_s[...]
            dq = dqm * t["e_qm"] + dqs * t["e_qs"]
            dk = dkm * t["e_km"] + dkl * t["e_kl"]
            t_qm, t_km, t_kl = dqm * t["qm"], dkm * t["km"], dkl * t["kl"]
            db = t_qm - t_km + dqs * t["qs"] - t_kl
            db_mid = jnp.sum(t_km - t_qm, axis=0, keepdims=True)
            db_last = jnp.sum(t_kl, axis=0, keepdims=True) + del_s[...] * t["el"]
            db = db + jnp.where(row == CHUNK // 2 - 1, db_mid, 0.0) + jnp.where(row == CHUNK - 1, db_last, 0.0)
            dlogf = _chunk_sum(anti, db)
            sq, sg, sgn, f = t["sq"], t["sg"], t["sgn"], t["f"]
            dph_ref[rows, 0:D_MODEL] = (dq * (sq * (1.0 + hq * (1.0 - sq)))).astype(dph_ref.dtype)
            dph_ref[rows, D_MODEL:2 * D_MODEL] = (
                dlogf * (1.0 - lb) * sg * (1.0 - sg) / f - dk * (1.0 - lb) * sgn * (1.0 - sgn)).astype(dph_ref.dtype)
            dph_ref[rows, 2 * D_MODEL:3 * D_MODEL] = dv_s[...].astype(dph_ref.dtype)
            dlb_ref[...] += jnp.sum(dlogf * (1.0 - sg) / f - dk * sgn, axis=0, keepdims=True)

        @pl.when(i == nblk - 1)
        def _():
            dl0 = dlb_ref[...] * lb * (1.0 - lb)
            dl_ref[0:1, :] = dl0
            dl_ref[1:2, :] = -dl0

    wide = pltpu.VMEM((CHUNK, D_MODEL), F32)
    return _call(
        "hgrn_bwd", body, grid=(nblk,),
        in_specs=[col(0), col(1), col(2), col(3), tok,
                  pl.BlockSpec((cpb, D_MODEL, HGRN_K), lambda i: (rev(i), 0, 0)), tok,
                  pl.BlockSpec((2, D_MODEL), lambda i: (0, 0)), pl.BlockSpec((1, D_MODEL), lambda i: (0, 0)), HBM_SPEC],
        out_specs=[pl.BlockSpec((pl.Element(HGRN_TOKENS), pl.Element(4 * D_MODEL)), lambda i: (
                       pl.multiple_of(rev(i) * HGRN_TOKENS, ROW_ALIGN), first)),
                   pl.BlockSpec((1, D_MODEL), lambda i: (0, 0)), pl.BlockSpec((2, D_MODEL), lambda i: (0, 0))],
        out_shape=[jax.ShapeDtypeStruct(d_in.shape, d_in.dtype), jax.ShapeDtypeStruct((1, D_MODEL), F32),
                   jax.ShapeDtypeStruct((2, D_MODEL), F32)],
        scratch=[pltpu.VMEM((HGRN_HEADS, HGRN_K, HGRN_K), F32), pltpu.VMEM((1, D_MODEL), F32),
                 wide, wide, wide, wide, wide, wide, pltpu.VMEM((1, D_MODEL), F32)],
        args=[ph, ph, ph, ph, o_raw, states, dy, lb_logits, norm_g, d_in], semantics=("arbitrary",), comm=comm,
        aliases={9: 0})


def _local_step(x, target, vec, net):
    T, D = x.shape
    norm_mix_g, b_in, sinks, lb_logits = vec["norm_mix_g"], vec["b_in"], vec["attn_sinks"], vec["hgrn_lb_logits"]
    hgrn_norm_g, norm_ffn_g, norm_final_g = vec["hgrn_norm_g"], vec["norm_ffn_g"], vec["norm_final_g"]
    w_in = net.full("w_in")
    o_q, o_kv, o_h, o_g = (sum(IN_SPLITS[:i]) for i in range(4))
    TM = 512

    first = ("w_branch_attn", "w_branch_hgrn", "w_ffn_down")
    (u, pq, pkv, ph, pg), got = _in_proj(x, norm_mix_g, w_in, b_in, tm=256, comm=net.fetch(first))
    net.fetched(first, got)
    second = ("w_ffn_gate",)
    (y_attn, lse), got = _attn_fwd(pq, pkv, sinks, comm=_join(net.relay(first), net.fetch(second)))
    net.gathered(first, got[:len(first)])
    net.fetched(second, got[len(first):])
    third = ("w_ffn_up", "w_out")
    (y_hgrn, o_raw, states), got = _hgrn_fwd(ph, lb_logits, hgrn_norm_g,
                                             comm=_join(net.relay(second), net.fetch(third)))
    net.gathered(second, got[:len(second)])
    net.fetched(third, got[len(second):])
    w_ba, w_bh = net.full("w_branch_attn"), net.full("w_branch_hgrn")
    (ya, yb, merged), got = _merge_fwd(y_attn, y_hgrn, w_ba, w_bh, pg, tm=TM, comm=net.relay(third))
    net.gathered(third, got)
    w_gate, w_up, w_out = net.full("w_ffn_gate"), net.full("w_ffn_up"), net.full("w_out")

    (h1, u2, gpre, up, z), _ = _ffn_fwd(merged, w_out, x, norm_ffn_g, w_gate, w_up, tm=256)
    w_down = net.full("w_ffn_down")

    dh2, dh2b, dgp, dup, dgf_p, loss_p = _ffn_tail(z, w_down, h1, target, norm_final_g, gpre, up, tm=512)
    loss = jnp.sum(loss_p.reshape(-1, 8, D)[:, 0, 0])
    d_norm_final = _colsum_partials(dgf_p)
    d_w_down = _weight_grad("dw_down", z, dh2b, tm=DW_ROWS, tk=T)

    names, swap = ("w_ffn_down",), ()
    (dh1, dh1b, dg2_p), got = _ffn_in_bwd(dgp, dup, w_gate, w_up, h1, norm_ffn_g, dh2, tm=256,
                                          comm=net.exchange(dict(w_ffn_down=[d_w_down]), swap))
    net.received(names, swap, got)
    d_norm_ffn = _colsum_partials(dg2_p)
    d_w_gate = _weight_grad("dw_gate", dgp, u2, tm=DW_ROWS, tk=T)
    d_w_up = _weight_grad("dw_up", dup, u2, tm=DW_ROWS, tk=T)

    rows_in = sum(IN_SPLITS)
    dya, dyb, dy_attn, dy_hgrn, d_in = _merge_bwd(dh1b, w_out, w_ba, w_bh, ya, yb, pg, tm=TM, d_in_width=rows_in,
                                                  first=o_g)
    d_w_out = _weight_grad("dw_out", merged, dh1b, tm=1024, tk=1024)
    d_w_ba = _weight_grad("dw_branch_a", y_attn, dya, tm=1024, tk=1024)
    d_w_bh = _weight_grad("dw_branch_b", y_hgrn, dyb, tm=1024, tk=1024)

    names, swap = ("w_ffn_gate",), ("w_ffn_down",)
    (d_in, dsink), got = _attn_bwd(pq, pkv, sinks, lse, dy_attn, d_in,
                                   comm=net.exchange(dict(w_ffn_gate=[d_w_gate]), swap))
    net.received(names, swap, got)
    names, swap = ("w_ffn_up", "w_out"), ("w_ffn_gate",)
    (d_in, d_hgrn_norm, d_lb_logits), got = _hgrn_bwd(
        ph, o_raw, states, dy_hgrn, lb_logits, hgrn_norm_g, d_in, o_h,
        comm=net.exchange(dict(w_ffn_up=[d_w_up], w_out=[d_w_out]), swap))
    net.received(names, swap, got)

    names, swap = ("w_branch_attn", "w_branch_hgrn"), ("w_ffn_up", "w_out")
    (*d_w_in, d_b_in), got = _weight_grad(
        "dw_in", d_in, u, tm=DW_ROWS, tk=T, a_colsum=True, carrying=True,
        comm=net.exchange(dict(w_branch_attn=[d_w_ba], w_branch_hgrn=[d_w_bh]), swap))
    net.received(names, swap, got)
    d_w_in = [tuple(d_w_in)]

    first_level = net.presum_begin("w_in", d_w_in)
    halves = net.presum_end("w_in", [] if first_level is None else _copies_alone("presum_swap_w_in", first_level))
    names, swap = ("w_in",), ("w_branch_attn", "w_branch_hgrn")
    (dx, dg1_p), got = _in_proj_bwd([(d_in, 0)], w_in, x, norm_mix_g, dh1, tm=256,
                                    comm=_join(halves, net.swap(swap)))
    net.last = (names, swap, got)
    d_norm_mix = _colsum_partials(dg1_p)
    vecs = dict(norm_mix_g=d_norm_mix, b_in=d_b_in, attn_sinks=jnp.sum(dsink.reshape(Q_HEADS, ATTN_BLOCK), axis=1).reshape(1, Q_HEADS),
                hgrn_lb_logits=d_lb_logits,
                hgrn_norm_g=d_hgrn_norm, norm_ffn_g=d_norm_ffn, norm_final_g=d_norm_final)
    return loss, dx, vecs


def _place():
    return lax.axis_index("x"), lax.axis_index("y"), lax.axis_index("c")


def _other_chips(x, y):
    return [(1 - x, y), (x, 1 - y), (1 - x, 1 - y)]


def _y_first(copies):
    return [copies[3 * (i // 3) + (1, 0, 2)[i % 3]] for i in range(len(copies))]


def _gather_copies(shards):
    n = len(shards)

    def build(ins, outs, send_sems, recv_sems, local_sems):
        x, y, c = _place()
        mine = 2 * x + y
        local = [pltpu.make_async_copy(ins[w], outs[w].at[mine], local_sems.at[w]) for w in range(n)]
        sends, recvs = [], []
        for w in range(n):
            for k, (px, py) in enumerate(_other_chips(x, y)):
                sem = 3 * w + k
                sends.append(pltpu.make_async_remote_copy(
                    src_ref=ins[w], dst_ref=outs[w].at[mine], send_sem=send_sems.at[sem], recv_sem=recv_sems.at[sem],
                    device_id=(px, py, c), device_id_type=MESH_ID))
                recvs.append(pltpu.make_async_remote_copy(
                    src_ref=ins[w], dst_ref=outs[w].at[2 * px + py], send_sem=send_sems.at[sem],
                    recv_sem=recv_sems.at[sem], device_id=(px, py, c), device_id_type=MESH_ID))
        return sends, recvs, local, _y_first(sends)

    return _Carried(shards, [jax.ShapeDtypeStruct((N_CHIPS,) + s.shape, s.dtype) for s in shards], 3 * n, n, build)


def _fetch_copies(shards):
    n = len(shards)

    def build(ins, outs, send_sems, recv_sems, local_sems, *stages):
        x, y, c = _place()
        mine = 2 * x + y
        local = [(pltpu.make_async_copy(ins[w], stages[w], local_sems.at[2 * w]),
                  pltpu.make_async_copy(stages[w], outs[w].at[mine], local_sems.at[2 * w + 1])) for w in range(n)]
        sends, recvs = [], []
        for w in range(n):
            half = shards[w].shape[0] // 2
            rows = pl.ds(c * half, half)
            for k, (px, py) in enumerate(_other_chips(x, y)):
                sem = 3 * w + k
                sends.append(pltpu.make_async_remote_copy(
                    src_ref=ins[w].at[rows], dst_ref=outs[w].at[mine, rows], send_sem=send_sems.at[sem],
                    recv_sem=recv_sems.at[sem], device_id=(px, py, c), device_id_type=MESH_ID))
                recvs.append(pltpu.make_async_remote_copy(
                    src_ref=ins[w].at[rows], dst_ref=outs[w].at[2 * px + py, rows], send_sem=send_sems.at[sem],
                    recv_sem=recv_sems.at[sem], device_id=(px, py, c), device_id_type=MESH_ID))
        return sends, recvs, local, _y_first(sends)

    return _Carried(shards, [jax.ShapeDtypeStruct((N_CHIPS,) + s.shape, s.dtype) for s in shards], 3 * n, 2 * n, build,
                    stages=[pltpu.VMEM(s.shape, s.dtype) for s in shards])


def _relay_copies(fetched):
    n = len(fetched)

    def build(ins, outs, send_sems, recv_sems, local_sems):
        x, y, c = _place()
        sends, recvs = [], []
        for w in range(n):
            half = fetched[w].shape[1] // 2
            mine, theirs = pl.ds(c * half, half), pl.ds((1 - c) * half, half)
            for k, (px, py) in enumerate(_other_chips(x, y)):
                block, sem = 2 * px + py, 3 * w + k
                sends.append(pltpu.make_async_remote_copy(
                    src_ref=ins[w].at[block, mine], dst_ref=outs[w].at[block, mine], send_sem=send_sems.at[sem],
                    recv_sem=recv_sems.at[sem], device_id=(x, y, 1 - c), device_id_type=MESH_ID))
                recvs.append(pltpu.make_async_remote_copy(
                    src_ref=ins[w].at[block, theirs], dst_ref=outs[w].at[block, theirs], send_sem=send_sems.at[sem],
                    recv_sem=recv_sems.at[sem], device_id=(x, y, 1 - c), device_id_type=MESH_ID))
        return sends, recvs, []

    return _Carried(fetched, [jax.ShapeDtypeStruct(f.shape, f.dtype) for f in fetched], 3 * n, 0, build,
                    continued={w: w for w in range(n)})


def _grad_copies(stacked):
    n = len(stacked)

    def build(ins, outs, send_sems, recv_sems, local_sems):
        x, y, c = _place()
        sends = []
        for w in range(n):
            for k, (px, py) in enumerate(_other_chips(x, y)):
                sem = 3 * w + k
                sends.append(pltpu.make_async_remote_copy(
                    src_ref=ins[w].at[2 * px + py], dst_ref=outs[w].at[k], send_sem=send_sems.at[sem],
                    recv_sem=recv_sems.at[sem], device_id=(px, py, c), device_id_type=MESH_ID))
        return sends, sends, [], _y_first(sends)

    return _Carried(stacked, [jax.ShapeDtypeStruct((3,) + s.shape[1:], s.dtype) for s in stacked], 3 * n, 0, build)


def _small_copies(small):
    def build(ins, outs, send_sems, recv_sems, local_sems):
        small_ref, all_ref = ins[0], outs[0]
        x, y, c = _place()
        me = 4 * x + 2 * y + c
        sends, recvs = [], []
        for r in range(1, 8):
            px = 1 - x if r & 4 else x
            py = 1 - y if r & 2 else y
            pc = 1 - c if r & 1 else c
            sends.append(pltpu.make_async_remote_copy(
                src_ref=small_ref, dst_ref=all_ref.at[me], send_sem=send_sems.at[r - 1], recv_sem=recv_sems.at[r - 1],
                device_id=(px, py, pc), device_id_type=MESH_ID))
            recvs.append(pltpu.make_async_remote_copy(
                src_ref=small_ref, dst_ref=all_ref.at[4 * px + 2 * py + pc], send_sem=send_sems.at[r - 1],
                recv_sem=recv_sems.at[r - 1], device_id=(px, py, pc), device_id_type=MESH_ID))
        return sends, recvs, [pltpu.make_async_copy(small_ref, all_ref.at[me], local_sems.at[0])]

    return _Carried([small], [jax.ShapeDtypeStruct((8,) + small.shape, small.dtype)], 7, 1, build)


def _gather_by_neighbours(name, shard):
    half = shard.shape[0] // 2
    quarter = half // 2

    def body(in_ref, out_ref, send_sems, recv_sems, local_sem, stage_ref):
        for core in (0, 1):
            @pl.when(lax.axis_index("c") == core)
            def _():
                program(core, in_ref, out_ref, send_sems, recv_sems, local_sem, stage_ref)

    def program(c, in_ref, out_ref, send_sems, recv_sems, local_sem, stage_ref):
        x, y, _ = _place()
        chip = lambda px, py: 2 * px + py
        to_x, to_y, sibling = (1 - x, y, c), (x, 1 - y, c), (x, y, 1 - c)
        x_blk, y_blk, d_blk = chip(1 - x, y), chip(x, 1 - y), chip(1 - x, 1 - y)
        mine, theirs = c * half, (1 - c) * half

        def copy(sem, rows, block, to, src=None):
            place = out_ref.at[block, pl.ds(rows[0], rows[1])]
            return pltpu.make_async_remote_copy(
                src_ref=place if src is None else src, dst_ref=place, send_sem=send_sems.at[sem],
                recv_sem=recv_sems.at[sem], device_id=to, device_id_type=MESH_ID)

        stage = pltpu.make_async_copy(in_ref, stage_ref, local_sem.at[0])
        own = pltpu.make_async_copy(stage_ref, out_ref.at[chip(x, y)], local_sem.at[1])
        my_rows = in_ref.at[pl.ds(mine, half)]
        along_x = dict(send=copy(0, (mine, half), chip(x, y), to_x, src=my_rows),
                       landed=copy(0, (mine, half), x_blk, to_x),
                       onward=[copy(3, (mine + quarter, quarter), x_blk, to_y), copy(4, (mine, half), x_blk, sibling)],
                       diagonal=copy(2, (mine, quarter), d_blk, to_x))
        along_y = dict(send=copy(1, (mine, half), chip(x, y), to_y, src=my_rows),
                       landed=copy(1, (mine, half), y_blk, to_y),
                       onward=[copy(2, (mine, quarter), y_blk, to_x), copy(5, (mine, half), y_blk, sibling)],
                       diagonal=copy(3, (mine + quarter, quarter), d_blk, to_y))
        last = copy(6, (mine, half), d_blk, sibling)

        order = (along_x, along_y) if c == 0 else (along_y, along_x)
        for axis in order:
            axis["send"].start()
        stage.start()
        stage.wait()
        own.start()
        for axis in order:
            axis["landed"].wait_recv()
            for cp in axis["onward"]:
                cp.start()
        for axis in order:
            axis["diagonal"].wait_recv()
        last.start()
        for sem, block in ((4, x_blk), (5, y_blk), (6, d_blk)):
            copy(sem, (theirs, half), block, sibling).wait_recv()
        for cp in [along_x["send"], along_y["send"]] + along_x["onward"] + along_y["onward"] + [last]:
            cp.wait_send()
        own.wait()

    return pl.pallas_call(
        body, name=name, in_specs=[HBM_SPEC], out_specs=HBM_SPEC,
        out_shape=jax.ShapeDtypeStruct((N_CHIPS,) + shard.shape, shard.dtype),
        scratch_shapes=[pltpu.SemaphoreType.DMA((7,)), pltpu.SemaphoreType.DMA((7,)), pltpu.SemaphoreType.DMA((2,)),
                        pltpu.VMEM(shard.shape, shard.dtype)],
    )(shard)


def _copies_alone(name, comm):
    return _call(name, lambda: None, grid=(), in_specs=[], out_specs=[], out_shape=[], args=[], comm=comm)[1]


class _Net:
    def __init__(self, shards):
        self.shards = shards
        self.whole, self.partly, self.own, self.theirs, self.sums, self.other = {}, {}, {}, {}, {}, {}
        x, y, _ = _place()
        self.chip = 2 * x + y

    def fetch(self, names):
        return _fetch_copies([self.shards[n] for n in names])

    def fetched(self, names, got):
        self.partly.update(zip(names, got))

    def relay(self, names):
        return _relay_copies([self.partly[n] for n in names])

    def gathered(self, names, got):
        for n, g in zip(names, got):
            self.whole[n] = g.reshape(-1, g.shape[-1])

    def full(self, name):
        return self.whole[name]

    def exchange(self, grads, swap=()):
        stacked = []
        for n, pieces in grads.items():
            (keep, send), = pieces
            self.own[n] = keep
            stacked.append(send.reshape(N_CHIPS, keep.shape[0] // N_CHIPS, send.shape[-1]))
        return _join(_grad_copies(stacked), self.swap(swap))

    def swap(self, names):
        return _sibling_copies([self.sums[n] for n in names]) if names else None

    def presum_begin(self, name, pieces):
        keep = jnp.concatenate([p[0] for p in pieces], axis=0) if len(pieces) > 1 else pieces[0][0]
        send = jnp.concatenate([p[1] for p in pieces], axis=0) if len(pieces) > 1 else pieces[0][1]
        rows = keep.shape[0] // N_CHIPS
        self.held = keep.reshape(N_CHIPS, rows, keep.shape[-1])
        return _half_rows_copies(send.reshape(N_CHIPS, rows, send.shape[-1]))

    def presum_end(self, name, got):
        x, y, c = _place()
        to_send, self.own[name] = _pre_sum("presum_" + name, self.held, got[0], jnp.stack([c, self.chip]))
        return _grad_copies([to_send])

    def received(self, names, swap, got, carried=None):
        self.theirs.update(zip(names, got[:len(names)]))
        self.other.update(zip(swap, got[len(names):]))
        for n in names:
            (self.sums[n],), more = _partial_sum("sum_" + n, self.own[n], self.theirs[n], self.chip, comm=carried)
        return more


def _half_rows_copies(stacked):
    n, rows = stacked.shape[0], stacked.shape[1] // 2

    def build(ins, outs, send_sems, recv_sems, local_sems):
        x, y, c = _place()
        copies = [pltpu.make_async_remote_copy(
            src_ref=ins[0].at[s, pl.ds((1 - c) * rows, rows)], dst_ref=outs[0].at[s], send_sem=send_sems.at[s],
            recv_sem=recv_sems.at[s], device_id=(x, y, 1 - c), device_id_type=MESH_ID) for s in range(n)]
        return copies, copies, []

    return _Carried([stacked], [jax.ShapeDtypeStruct((n, rows, stacked.shape[2]), stacked.dtype)], n, 0, build)


def _pre_sum(name, held, theirs, core_and_chip):
    n, R, C = held.shape
    half = R // 2
    tr = _row_tile(half)
    per_half = half // tr

    def body(place_ref, h_ref, t_ref, send_ref, own_ref):
        total = h_ref[0] + t_ref[0].astype(F32)
        send_ref[0] = total.astype(send_ref.dtype)

        @pl.when(pl.program_id(1) == place_ref[1])
        def _():
            own_ref[...] = total

    return pl.pallas_call(
        body, name=name,
        grid_spec=pltpu.PrefetchScalarGridSpec(
            num_scalar_prefetch=1, grid=(per_half, n),
            in_specs=[pl.BlockSpec((1, tr, C), lambda i, s, place: (s, place[0] * per_half + i, 0)),
                      pl.BlockSpec((1, tr, C), lambda i, s, place: (s, i, 0))],
            out_specs=[pl.BlockSpec((1, tr, C), lambda i, s, place: (s, i, 0)),
                       pl.BlockSpec((tr, C), lambda i, s, place: (i, 0))]),
        out_shape=[jax.ShapeDtypeStruct((n, half, C), MXU_DTYPE), jax.ShapeDtypeStruct((half, C), F32)],
        compiler_params=_params(("arbitrary", "arbitrary")),
    )(core_and_chip, held, theirs)


def _sibling_copies(parts):
    n = len(parts)

    def build(ins, outs, send_sems, recv_sems, local_sems):
        x, y, c = _place()
        copies = [pltpu.make_async_remote_copy(
            src_ref=ins[w], dst_ref=outs[w], send_sem=send_sems.at[w], recv_sem=recv_sems.at[w],
            device_id=(x, y, 1 - c), device_id_type=MESH_ID) for w in range(n)]
        return copies, copies, []

    return _Carried(parts, [jax.ShapeDtypeStruct(p.shape, p.dtype) for p in parts], n, 0, build)


def _row_tile(rows, most=512, sublanes=16):
    return max(t for t in range(sublanes, min(most, rows // 2) + 1, sublanes) if rows % t == 0)


def _partial_sum(name, own, recv, chip, comm=None):
    _, R, C = recv.shape
    tr = _row_tile(R)

    def body(o_ref, r_ref, p_ref):
        p_ref[...] = ((o_ref[...] + r_ref[0].astype(F32)) + r_ref[1].astype(F32)) + r_ref[2].astype(F32)

    if own.shape[0] != R:
        assert comm is None and own.shape[0] == N_CHIPS * R
        total = pl.pallas_call(
            lambda chip_ref, *refs: body(*refs), name=name,
            grid_spec=pltpu.PrefetchScalarGridSpec(
                num_scalar_prefetch=1, grid=(R // tr,),
                in_specs=[pl.BlockSpec((tr, C), lambda i, chip_ref: (chip_ref[0] * (R // tr) + i, 0)),
                          pl.BlockSpec((3, tr, C), lambda i, chip_ref: (0, i, 0))],
                out_specs=pl.BlockSpec((tr, C), lambda i, chip_ref: (i, 0))),
            out_shape=jax.ShapeDtypeStruct((R, C), F32), compiler_params=_params(("parallel",)),
        )(chip.reshape(1), own, recv)
        return [total], []
    return _call(name, body, grid=(R // tr,),
                 in_specs=[pl.BlockSpec((tr, C), lambda i: (i, 0)), pl.BlockSpec((3, tr, C), lambda i: (0, i, 0))],
                 out_specs=[pl.BlockSpec((tr, C), lambda i: (i, 0))], out_shape=[jax.ShapeDtypeStruct((R, C), F32)],
                 args=[own, recv], semantics=("parallel",), comm=comm)


def _adam_vals(w, g, m, v):
    m = ADAM_B1 * m + (1.0 - ADAM_B1) * g
    v = ADAM_B2 * v + (1.0 - ADAM_B2) * (g * g)
    m_hat = m / (1.0 - ADAM_B1 ** ADAM_STEP)
    v_hat = v / (1.0 - ADAM_B2 ** ADAM_STEP)
    delta = -ADAM_LR * (m_hat / (jnp.sqrt(v_hat) + ADAM_EPS) + ADAM_WD * w)
    return delta, m, v


def _adamw(name, w, m, v, mine, other, comm=None):
    R, C = w.shape
    tr = _row_tile(R)

    def body(w_ref, m_ref, v_ref, s_ref, n_ref, g_ref, d_ref, nm_ref, nv_ref):
        g = s_ref[...] + n_ref[...]
        d, nm, nv = _adam_vals(w_ref[...], g, m_ref[...], v_ref[...])
        g_ref[...], d_ref[...], nm_ref[...], nv_ref[...] = g, d, nm, nv

    spec = pl.BlockSpec((tr, C), lambda i: (i, 0))
    return _call(name, body, grid=(R // tr,), in_specs=[spec] * 5, out_specs=[spec] * 4,
                 out_shape=[jax.ShapeDtypeStruct((R, C), F32)] * 4, args=[w, m, v, mine, other],
                 semantics=("parallel",), comm=comm)


def _adamw_by_halves(name, w, m, v, mine, other, core):
    R, C = w.shape
    tr = _row_tile(R // 2)
    per_half = R // 2 // tr

    def body(c_ref, w_ref, m_ref, v_ref, s_ref, n_ref, g_ref, d_ref, nm_ref, nv_ref):
        g = jnp.where(pl.program_id(0) // per_half == c_ref[0, 0], s_ref[...], n_ref[...])
        d, nm, nv = _adam_vals(w_ref[...], g, m_ref[...], v_ref[...])
        g_ref[...], d_ref[...], nm_ref[...], nv_ref[...] = g, d, nm, nv

    spec = pl.BlockSpec((tr, C), lambda i: (i, 0))
    part = pl.BlockSpec((tr, C), lambda i: (i % per_half, 0))
    return pl.pallas_call(
        body, name=name, grid=(R // tr,),
        in_specs=[pl.BlockSpec(memory_space=pltpu.SMEM), spec, spec, spec, part, part], out_specs=[spec] * 4,
        out_shape=[jax.ShapeDtypeStruct((R, C), F32)] * 4, compiler_params=_params(("parallel",)),
    )(core, w, m, v, mine, other)


SMALL_LAYOUT = dict(norm_mix_g=(0, 1, 1024), b_in=(1, 8, 7424), hgrn_norm_g=(9, 1, 1024), norm_ffn_g=(10, 1, 1024),
                    norm_final_g=(11, 1, 1024), hgrn_lb_logits=(12, 2, 2048), attn_sinks=(14, 1, 16))
SMALL_LOSS_ROW, SMALL_ROWS = 15, 16


def _pack_small(grads, loss):
    rows = [jnp.pad(grads[name].astype(F32).reshape(-1), (0, nrows * D_MODEL - n))
            for name, (_, nrows, n) in SMALL_LAYOUT.items()]
    rows.append(jnp.pad(loss.astype(F32).reshape(1), (0, D_MODEL - 1)))
    return jnp.concatenate(rows).reshape(SMALL_ROWS, D_MODEL)


def _adamw_small(w, m, v, g_all):
    names = list(SMALL_LAYOUT)
    n = len(names)

    def body(a_ref, *refs):
        ins, outs = refs[:3 * n], refs[3 * n:]
        g_all_rows = a_ref[0]
        for dev in range(1, 8):
            g_all_rows = g_all_rows + a_ref[dev]
        for i, name in enumerate(names):
            first, nrows, count = SMALL_LAYOUT[name]
            w_ref, m_ref, v_ref = ins[3 * i:3 * i + 3]
            if w_ref.shape[0] == nrows:
                g = g_all_rows[first:first + nrows, :w_ref.shape[1]]
            else:
                last = count - (nrows - 1) * D_MODEL
                g = jnp.concatenate([g_all_rows[r:r + 1, :] for r in range(first, first + nrows - 1)]
                                    + [g_all_rows[first + nrows - 1:first + nrows, :last]], axis=1)
            d, nm, nv = _adam_vals(w_ref[...], g, m_ref[...], v_ref[...])
            for o_ref, val in zip(outs[4 * i:4 * i + 4], (g, d, nm, nv)):
                o_ref[...] = val
        outs[4 * n][...] = g_all_rows[SMALL_LOSS_ROW:SMALL_LOSS_ROW + 1, 0:1]

    res = pl.pallas_call(
        body, name="adamw_small",
        out_shape=[jax.ShapeDtypeStruct(w[name].shape, F32) for name in names for _ in range(4)]
        + [jax.ShapeDtypeStruct((1, 1), F32)],
    )(g_all, *[t[name] for name in names for t in (w, m, v)])
    return {name: res[4 * i:4 * i + 4] for i, name in enumerate(names)}, res[4 * n]


MATRICES = ("w_in", "w_branch_attn", "w_branch_hgrn", "w_out", "w_ffn_gate", "w_ffn_up", "w_ffn_down")
COLUMN_SHARDED = ("w_in", "w_ffn_gate", "w_ffn_up")
WEIGHTS = ("norm_mix_g", "w_in", "b_in", "attn_sinks", "hgrn_lb_logits", "hgrn_norm_g", "w_branch_attn",
           "w_branch_hgrn", "w_out", "norm_ffn_g", "w_ffn_gate", "w_ffn_up", "w_ffn_down", "norm_final_g")


def kernel(x, norm_mix_g, w_in, b_in, attn_sinks, hgrn_lb_logits, hgrn_norm_g, w_branch_attn, w_branch_hgrn, w_out, norm_ffn_g, w_ffn_gate, w_ffn_up, w_ffn_down, norm_final_g, loss_target, m_norm_mix_g, m_w_in, m_b_in, m_attn_sinks, m_hgrn_lb_logits, m_hgrn_norm_g, m_w_branch_attn, m_w_branch_hgrn, m_w_out, m_norm_ffn_g, m_w_ffn_gate, m_w_ffn_up, m_w_ffn_down, m_norm_final_g, v_norm_mix_g, v_w_in, v_b_in, v_attn_sinks, v_hgrn_lb_logits, v_hgrn_norm_g, v_w_branch_attn, v_w_branch_hgrn, v_w_out, v_norm_ffn_g, v_w_ffn_gate, v_w_ffn_up, v_w_ffn_down, v_norm_final_g):
    given = dict(locals())
    w = {n: given[n] for n in WEIGHTS}
    m = {n: given["m_" + n] for n in WEIGHTS}
    v = {n: given["v_" + n] for n in WEIGHTS}

    block = lambda a, n: jnp.transpose(a[0]) if n in COLUMN_SHARDED else a[0]
    unblock = lambda a, n: (jnp.transpose(a) if n in COLUMN_SHARDED else a)[None]
    net = _Net({n: block(w[n], n).astype(MXU_DTYPE) for n in MATRICES})
    net.gathered(("w_in",), [_gather_by_neighbours("gather_w_in", net.shards["w_in"])])
    vec = dict(norm_mix_g=norm_mix_g, b_in=b_in, attn_sinks=attn_sinks, hgrn_lb_logits=hgrn_lb_logits,
               hgrn_norm_g=hgrn_norm_g, norm_ffn_g=norm_ffn_g, norm_final_g=norm_final_g.reshape(1, D_MODEL))
    loss_part, dx, d_vecs = _local_step(x[0], loss_target[0], vec, net)

    small_all, = net.received(*net.last, carried=_small_copies(_pack_small(d_vecs, loss_part)))
    grads, deltas, new_m, new_v = {}, {}, {}, {}
    for n in ("w_ffn_down", "w_ffn_gate", "w_ffn_up", "w_out", "w_branch_attn", "w_branch_hgrn"):
        res, got = _adamw("adamw_" + n, block(w[n], n), block(m[n], n), block(v[n], n), net.sums[n], net.other[n],
                          comm=net.swap(("w_in",)) if n == "w_ffn_down" else None)
        if n == "w_ffn_down":
            net.other["w_in"], = got
        grads[n], deltas[n], new_m[n], new_v[n] = (unblock(r, n) for r in res)
    n = "w_in"
    res = _adamw_by_halves("adamw_" + n, block(w[n], n), block(m[n], n), block(v[n], n), net.sums[n], net.other[n],
                           _place()[2].reshape(1, 1))
    grads[n], deltas[n], new_m[n], new_v[n] = (unblock(r, n) for r in res)
    rows = lambda t: {n: t[n].reshape(-1, t[n].shape[-1]) for n in SMALL_LAYOUT}
    res, loss = _adamw_small(rows(w), rows(m), rows(v), small_all)
    for n, four in res.items():
        grads[n], deltas[n], new_m[n], new_v[n] = (r.reshape(w[n].shape) for r in four)
    loss = loss.reshape(())
    return (loss, dx[None], *[grads[n] for n in WEIGHTS], *[deltas[n] for n in WEIGHTS],
            *[new_m[n] for n in WEIGHTS], *[new_v[n] for n in WEIGHTS])
```

```python
import collections
import functools
import math

import jax
import jax.numpy as jnp
from jax import lax
from jax.experimental import pallas as pl
from jax.experimental.pallas import tpu as pltpu

F32 = jnp.float32
BF16 = jnp.bfloat16
MXU_DTYPE = jnp.bfloat16
SAVED_DTYPE = jnp.bfloat16
MESH_ID = pl.DeviceIdType.MESH

D_MODEL = 1024
HEAD_DIM = 64
Q_HEADS = 16
KV_HEADS = 2
GROUP = Q_HEADS // KV_HEADS
KV_WIDTH = KV_HEADS * HEAD_DIM
ATTN_BLOCK = 128
HGRN_HEADS = 8
HGRN_K = 128
CHUNK = 64
HGRN_TOKENS = 256
FFN = 2816
IN_SPLITS = (1024, 256, 4096, 2048)
EPS = 1e-6
NEG_INF = -1e30
ADAM_LR, ADAM_B1, ADAM_B2, ADAM_EPS, ADAM_WD, ADAM_STEP = 0.001, 0.9, 0.999, 1e-08, 0.01, 10
N_CHIPS = 4
VMEM_LIMIT = 60 * 1024 * 1024
ROW_ALIGN = 16
DW_ROWS = 256


def _params(sem=None):
    return pltpu.CompilerParams(dimension_semantics=sem, vmem_limit_bytes=VMEM_LIMIT)


def _sigmoid(v):
    return 0.5 * jnp.tanh(0.5 * v) + 0.5


def _dot(a, b, dims):
    return lax.dot_general(a.astype(MXU_DTYPE), b.astype(MXU_DTYPE), (dims, ((), ())),
                           preferred_element_type=F32)


def _nn(a, b):
    return _dot(a, b, ((1,), (0,)))


def _nt(a, b):
    return _dot(a, b, ((1,), (1,)))


def _tn(a, b):
    return _dot(a, b, ((0,), (0,)))


HBM_SPEC = pl.BlockSpec(memory_space=pl.ANY)


class _Carried:
    def __init__(self, arrays, out_shapes, n_remote, n_local, build, continued=None, stages=()):
        self.continued = dict(continued or {})
        self.parts = [(len(arrays), len(out_shapes), 3 + len(stages), build)]
        self.arrays, self.out_shapes = list(arrays), list(out_shapes)
        self.scratch = [pltpu.SemaphoreType.DMA((n_remote,)), pltpu.SemaphoreType.DMA((n_remote,)),
                        pltpu.SemaphoreType.DMA((max(n_local, 1),))] + list(stages)

    def __add__(self, other):
        both = _Carried([], [], 1, 0, None)
        both.parts = self.parts + other.parts
        both.arrays, both.out_shapes = self.arrays + other.arrays, self.out_shapes + other.out_shapes
        both.scratch = self.scratch + other.scratch
        both.continued = dict(self.continued)
        both.continued.update({len(self.arrays) + i: len(self.out_shapes) + o for i, o in other.continued.items()})
        return both

    def _built(self, ins, outs, sems):
        for ni, no, ns, build in self.parts:
            yield build(ins[:ni], outs[:no], *sems[:ns])
            ins, outs, sems = ins[ni:], outs[no:], sems[ns:]

    def start(self, ins, outs, sems):
        core = lax.axis_index("c")
        for sends, _, local, *other_order in self._built(ins, outs, sems):
            for cp in local:
                if not isinstance(cp, tuple):
                    cp.start()
            if not other_order:
                for cp in sends:
                    cp.start()
            else:
                @pl.when(core == 0)
                def _():
                    for cp in sends:
                        cp.start()

                @pl.when(core == 1)
                def _():
                    for cp in other_order[0]:
                        cp.start()
            staged = [cp for cp in local if isinstance(cp, tuple)]
            for into, _ in staged:
                into.start()
            for into, out_of in staged:
                into.wait()
                out_of.start()

    def wait(self, ins, outs, sems):
        for sends, recvs, local, *_ in self._built(ins, outs, sems):
            for cp in recvs:
                cp.wait_recv()
            for cp in sends:
                cp.wait_send()
            for cp in local:
                (cp[1] if isinstance(cp, tuple) else cp).wait()


def _join(*comms):
    comms = [c for c in comms if c is not None]
    return functools.reduce(lambda a, b: a + b, comms) if comms else None


def _call(name, body, *, grid, in_specs, out_specs, out_shape, args, scratch=(), semantics=None, comm=None,
          aliases=None):
    n_in, n_out, n_scr = len(in_specs), len(out_specs), len(scratch)
    aliases = aliases or {}
    if comm is None:
        res = pl.pallas_call(body, name=name, grid=grid, in_specs=in_specs, out_specs=out_specs, out_shape=out_shape,
                             scratch_shapes=list(scratch), input_output_aliases=aliases,
                             compiler_params=_params(semantics))(*args)
        return list(res), []
    ci, co = len(comm.arrays), len(comm.out_shapes)
    aliases = dict(aliases)
    aliases.update({n_in + i: n_out + o for i, o in comm.continued.items()})

    def carrying(*refs):
        ins, refs = refs[:n_in], refs[n_in:]
        c_ins, refs = refs[:ci], refs[ci:]
        outs, refs = refs[:n_out], refs[n_out:]
        c_outs, refs = refs[:co], refs[co:]
        scr, sems = refs[:n_scr], refs[n_scr:]
        if not grid:
            comm.start(c_ins, c_outs, sems)
            body(*ins, *outs, *scr)
            comm.wait(c_ins, c_outs, sems)
            return
        first = functools.reduce(jnp.logical_and, [pl.program_id(a) == 0 for a in range(len(grid))])
        last = functools.reduce(jnp.logical_and, [pl.program_id(a) == g - 1 for a, g in enumerate(grid)])

        @pl.when(first)
        def _():
            comm.start(c_ins, c_outs, sems)

        body(*ins, *outs, *scr)

        @pl.when(last)
        def _():
            comm.wait(c_ins, c_outs, sems)

    res = pl.pallas_call(
        carrying, name=name, grid=grid, in_specs=list(in_specs) + [HBM_SPEC] * ci,
        out_specs=list(out_specs) + [HBM_SPEC] * co, out_shape=list(out_shape) + comm.out_shapes,
        scratch_shapes=list(scratch) + comm.scratch, input_output_aliases=aliases,
        compiler_params=_params(("arbitrary",) * len(grid) if grid else None),
    )(*args, *comm.arrays)
    return list(res[:n_out]), list(res[n_out:])


_Cols = collections.namedtuple("_Cols", "array first cols")
_Into = collections.namedtuple("_Into", "rows first held")


def _weight_grad(name, a, b, *, tm, tk, a_colsum=False, into=None, carrying=False, comm=None):
    cols = a if isinstance(a, _Cols) else _Cols(a, 0, a.shape[1])
    a = cols.array
    (T, N), M = b.shape, cols.cols
    tk = min(tk, T)
    assert cols.first % tm == 0 and M % tm == 0 and T % tk == 0, (name, cols.first, M, tm, T, tk)
    ni, nk, tile0 = M // tm, T // tk, cols.first // tm
    held = list(into.held) if into is not None and into.held is not None else []

    def body(a_ref, b_ref, *rest):
        keep_ref, send_ref = rest[len(held):len(held) + 2]
        sums_ref = rest[len(held) + 2] if a_colsum else None
        if nk == 1:
            acc = _tn(a_ref[...], b_ref[...])
            keep_ref[...], send_ref[...] = acc, acc.astype(send_ref.dtype)
            if a_colsum:
                sums_ref[...] = jnp.sum(a_ref[...].astype(F32), axis=0, keepdims=True)
            return
        acc_ref = rest[-1]
        k = pl.program_id(1)

        @pl.when(k == 0)
        def _():
            acc_ref[...] = jnp.zeros_like(acc_ref)
            if a_colsum:
                sums_ref[...] = jnp.zeros((1, tm), F32)

        if a_colsum:
            sums_ref[...] += jnp.sum(a_ref[...].astype(F32), axis=0, keepdims=True)
        acc_ref[...] += _tn(a_ref[...], b_ref[...])

        @pl.when(k == nk - 1)
        def _():
            keep_ref[...], send_ref[...] = acc_ref[...], acc_ref[...].astype(send_ref.dtype)

    if into is None:
        rows, out_spec = M, pl.BlockSpec((tm, N), lambda i, k: (i, 0))
    else:
        rows = into.rows
        out_spec = pl.BlockSpec((pl.Element(tm), pl.Element(N)),
                                lambda i, k: (pl.multiple_of(into.first + i * tm, ROW_ALIGN), 0))
    out_shape = [jax.ShapeDtypeStruct((rows, N), F32), jax.ShapeDtypeStruct((rows, N), MXU_DTYPE)]
    out_specs = [out_spec, out_spec]
    if a_colsum:
        out_shape.append(jax.ShapeDtypeStruct((1, M), F32))
        out_specs.append(pl.BlockSpec((1, tm), lambda i, k: (0, i)))
    res, got = _call(
        name, body, grid=(ni, nk),
        in_specs=[pl.BlockSpec((tk, tm), lambda i, k: (k, tile0 + i)), pl.BlockSpec((tk, N), lambda i, k: (k, 0))]
        + [HBM_SPEC] * len(held),
        out_specs=out_specs, out_shape=out_shape, scratch=[pltpu.VMEM((tm, N), F32)] if nk > 1 else [],
        args=[a, b] + held, aliases={2 + p: p for p in range(len(held))}, semantics=("parallel", "arbitrary"),
        comm=comm)
    return (res, got) if carrying else res


def _ffn_fwd(merged, w_out, x, gain, w_gate_t, w_up_t, *, tm, comm=None):
    (T, D), F = x.shape, w_gate_t.shape[0]

    def body(m_ref, wo_ref, x_ref, g_ref, wg_ref, wu_ref, h_ref, u_ref, gate_ref, up_ref, z_ref):
        h = x_ref[...] + _nn(m_ref[...], wo_ref[...])
        h_ref[...] = h
        u = (h * lax.rsqrt(jnp.mean(h * h, axis=-1, keepdims=True) + EPS) * g_ref[...]).astype(u_ref.dtype)
        u_ref[...] = u
        gate, up = _nt(u, wg_ref[...]), _nt(u, wu_ref[...])
        gate_ref[...], up_ref[...] = gate.astype(gate_ref.dtype), up.astype(up_ref.dtype)
        z_ref[...] = (gate * _sigmoid(gate) * up).astype(z_ref.dtype)

    rows = lambda n: pl.BlockSpec((tm, n), lambda i: (i, 0))
    fixed = _fixed_spec
    return _call("ffn_hidden", body, grid=(T // tm,),
                 in_specs=[rows(D), fixed(w_out), rows(D), fixed(gain), fixed(w_gate_t), fixed(w_up_t)],
                 out_specs=[rows(D), rows(D), rows(F), rows(F), rows(F)],
                 out_shape=[jax.ShapeDtypeStruct((T, D), F32), jax.ShapeDtypeStruct((T, D), MXU_DTYPE)]
                 + [jax.ShapeDtypeStruct((T, F), SAVED_DTYPE)] * 2 + [jax.ShapeDtypeStruct((T, F), MXU_DTYPE)],
                 args=[merged, w_out, x, gain, w_gate_t, w_up_t], semantics=("parallel",), comm=comm)


def _in_proj(x, gain, w_in_t, b_in, *, tm, comm=None):
    T, D = x.shape
    bounds = [sum(IN_SPLITS[:i]) for i in range(len(IN_SPLITS) + 1)]

    def body(x_ref, g_ref, w_ref, b_ref, u_ref, *piece_refs):
        xv = x_ref[...]
        r = lax.rsqrt(jnp.mean(xv * xv, axis=-1, keepdims=True) + EPS)
        u = (xv * r * g_ref[...]).astype(u_ref.dtype)
        u_ref[...] = u
        for o_ref, lo, hi in zip(piece_refs, bounds[:-1], bounds[1:]):
            o_ref[...] = (_nt(u, w_ref[lo:hi, :]) + b_ref[:, lo:hi]).astype(o_ref.dtype)

    rows = lambda n: pl.BlockSpec((tm, n), lambda i: (i, 0))
    fixed = _fixed_spec
    dtypes = (MXU_DTYPE, MXU_DTYPE, F32, F32)
    return _call("in_proj", body, grid=(T // tm,),
                 in_specs=[rows(D), fixed(gain), fixed(w_in_t), fixed(b_in)],
                 out_specs=[rows(D)] + [rows(n) for n in IN_SPLITS],
                 out_shape=[jax.ShapeDtypeStruct((T, D), MXU_DTYPE)]
                 + [jax.ShapeDtypeStruct((T, n), dt) for n, dt in zip(IN_SPLITS, dtypes)],
                 args=[x, gain, w_in_t, b_in], semantics=("parallel",), comm=comm)


def _row_spec(tm, n):
    return pl.BlockSpec((tm, n), lambda i: (i, 0))


def _fixed_spec(a):
    return pl.BlockSpec(a.shape, lambda i: (0,) * a.ndim, pipeline_mode=pl.Buffered(1))


def _partials_spec(n):
    return pl.BlockSpec((8, n), lambda i: (i, 0))


def _row_parts(tm, parts):
    assert tm % parts == 0, (tm, parts)
    return [slice(p * (tm // parts), (p + 1) * (tm // parts)) for p in range(parts)]


def _ffn_tail(z, w_down, h1, target, gain, gate, up, *, tm, parts=2):
    (T, F), D = z.shape, h1.shape[1]

    def body(z_ref, w_ref, h_ref, t_ref, g_ref, gate_ref, up_ref, dh_ref, dhb_ref, dgate_ref, dup_ref, dg_ref, l_ref):
        part, dgain = 0.0, 0.0
        pieces = _row_parts(tm, parts)
        h2s = [h_ref[rows, :] + _nn(z_ref[rows, :], w_ref[...]) for rows in pieces]
        for rows, h2 in zip(pieces, h2s):
            r = lax.rsqrt(jnp.mean(h2 * h2, axis=-1, keepdims=True) + EPS)
            xhat = h2 * r
            err = xhat * g_ref[...] - t_ref[rows, :]
            part += 0.5 * jnp.sum(jnp.sum(err * err, axis=-1, keepdims=True), axis=0, keepdims=True) / D
            dy = err / D
            dxh = dy * g_ref[...]
            dh2 = r * (dxh - xhat * jnp.mean(dxh * xhat, axis=-1, keepdims=True))
            dh_ref[rows, :] = dh2
            dhb = dh2.astype(dhb_ref.dtype)
            dhb_ref[rows, :] = dhb
            dgain += jnp.sum(dy * xhat, axis=0, keepdims=True)
            dz = _nt(dhb, w_ref[...])
            gv, upv = gate_ref[rows, :].astype(F32), up_ref[rows, :].astype(F32)
            s = _sigmoid(gv)
            dgate_ref[rows, :] = (dz * upv * (s * (1.0 + gv * (1.0 - s)))).astype(dgate_ref.dtype)
            dup_ref[rows, :] = (dz * (gv * s)).astype(dup_ref.dtype)
        dg_ref[...] = jnp.broadcast_to(dgain, dg_ref.shape)
        l_ref[...] = jnp.broadcast_to(part, l_ref.shape)

    low = lambda n: jax.ShapeDtypeStruct((T, n), MXU_DTYPE)
    part = jax.ShapeDtypeStruct((8 * (T // tm), D), F32)
    return pl.pallas_call(
        body, name="ffn_tail", grid=(T // tm,),
        in_specs=[_row_spec(tm, F), _fixed_spec(w_down), _row_spec(tm, D), _row_spec(tm, D), _fixed_spec(gain),
                  _row_spec(tm, F), _row_spec(tm, F)],
        out_specs=[_row_spec(tm, D), _row_spec(tm, D), _row_spec(tm, F), _row_spec(tm, F), _partials_spec(D),
                   _partials_spec(D)],
        out_shape=[jax.ShapeDtypeStruct((T, D), F32), low(D), low(F), low(F), part, part],
        compiler_params=_params(("parallel",)),
    )(z, w_down, h1, target, gain, gate, up)


def _ffn_in_bwd(dgate, dup, w_gate_t, w_up_t, h1, gain, dres, *, tm, comm=None):
    (T, F), D = dgate.shape, h1.shape[1]

    def body(dg_ref, du_ref, wg_ref, wu_ref, h_ref, g_ref, r_ref, dh_ref, dhb_ref, dgain_ref):
        d_u2 = _nn(dg_ref[...], wg_ref[...]) + _nn(du_ref[...], wu_ref[...])
        dx, dgain = _rmsnorm_bwd_vals(d_u2, h_ref[...], g_ref[...])
        dh = r_ref[...] + dx
        dh_ref[...] = dh
        dhb_ref[...] = dh.astype(dhb_ref.dtype)
        dgain_ref[...] = jnp.broadcast_to(dgain, dgain_ref.shape)

    return _call("d_ffn_in", body, grid=(T // tm,),
                 in_specs=[_row_spec(tm, F), _row_spec(tm, F), _fixed_spec(w_gate_t), _fixed_spec(w_up_t),
                           _row_spec(tm, D), _fixed_spec(gain), _row_spec(tm, D)],
                 out_specs=[_row_spec(tm, D), _row_spec(tm, D), _partials_spec(D)],
                 out_shape=[jax.ShapeDtypeStruct((T, D), F32), jax.ShapeDtypeStruct((T, D), MXU_DTYPE),
                            jax.ShapeDtypeStruct((8 * (T // tm), D), F32)],
                 args=[dgate, dup, w_gate_t, w_up_t, h1, gain, dres], semantics=("parallel",), comm=comm)


def _in_proj_bwd(pieces, w_in_t, x, gain, dres, *, tm, comm=None):
    T, D = x.shape
    n = len(pieces)

    def body(*refs):
        dps, (w_ref, x_ref, g_ref, r_ref, dx_ref, dgain_ref) = refs[:n], refs[n:]
        d_u = None
        for dp_ref, (dp, first) in zip(dps, pieces):
            term = _nn(dp_ref[...], w_ref[first:first + dp.shape[1], :])
            d_u = term if d_u is None else d_u + term
        dx, dgain = _rmsnorm_bwd_vals(d_u, x_ref[...], g_ref[...])
        dx_ref[...] = r_ref[...] + dx
        dgain_ref[...] = jnp.broadcast_to(dgain, dgain_ref.shape)

    return _call("d_u", body, grid=(T // tm,),
                 in_specs=[_row_spec(tm, dp.shape[1]) for dp, _ in pieces]
                 + [_fixed_spec(w_in_t), _row_spec(tm, D), _fixed_spec(gain), _row_spec(tm, D)],
                 out_specs=[_row_spec(tm, D), _partials_spec(D)],
                 out_shape=[jax.ShapeDtypeStruct((T, D), F32), jax.ShapeDtypeStruct((8 * (T // tm), D), F32)],
                 args=[dp for dp, _ in pieces] + [w_in_t, x, gain, dres], semantics=("parallel",), comm=comm)


def _merge_fwd(y_a, y_b, w_a, w_b, gates, *, tm, comm=None):
    T, D = y_a.shape

    def body(ya_ref, yb_ref, wa_ref, wb_ref, ga_ref, gb_ref, pa_ref, pb_ref, m_ref):
        pa, pb = _nn(ya_ref[...], wa_ref[...]), _nn(yb_ref[...], wb_ref[...])
        pa_ref[...], pb_ref[...] = pa.astype(pa_ref.dtype), pb.astype(pb_ref.dtype)
        m_ref[...] = (_sigmoid(ga_ref[...]) * pa + _sigmoid(gb_ref[...]) * pb).astype(m_ref.dtype)

    rows = pl.BlockSpec((tm, D), lambda i: (i, 0))
    whole = pl.BlockSpec((D, D), lambda i: (0, 0), pipeline_mode=pl.Buffered(1))
    return _call("branch_merge", body, grid=(T // tm,),
                 in_specs=[rows, rows, whole, whole, rows, pl.BlockSpec((tm, D), lambda i: (i, 1))],
                 out_specs=[rows] * 3,
                 out_shape=[jax.ShapeDtypeStruct((T, D), SAVED_DTYPE)] * 2 + [jax.ShapeDtypeStruct((T, D), MXU_DTYPE)],
                 args=[y_a, y_b, w_a, w_b, gates, gates], semantics=("parallel",), comm=comm)


def _merge_bwd(dh, w_out, w_a, w_b, p_a, p_b, gates, *, tm, d_in_width, first):
    T, D = dh.shape

    def body(dh_ref, wo_ref, wa_ref, wb_ref, pa_ref, pb_ref, ga_ref, gb_ref, dpa_ref, dpb_ref, dya_ref, dyb_ref,
             din_ref):
        dm = _nt(dh_ref[...], wo_ref[...])
        sa, sb = _sigmoid(ga_ref[...]), _sigmoid(gb_ref[...])
        dpa, dpb = (dm * sa).astype(dpa_ref.dtype), (dm * sb).astype(dpb_ref.dtype)
        dpa_ref[...], dpb_ref[...] = dpa, dpb
        din_ref[:, :D] = (dm * pa_ref[...].astype(F32) * sa * (1.0 - sa)).astype(din_ref.dtype)
        din_ref[:, D:] = (dm * pb_ref[...].astype(F32) * sb * (1.0 - sb)).astype(din_ref.dtype)
        dya_ref[...] = _nt(dpa, wa_ref[...]).astype(dya_ref.dtype)
        dyb_ref[...] = _nt(dpb, wb_ref[...]).astype(dyb_ref.dtype)

    rows = pl.BlockSpec((tm, D), lambda i: (i, 0))
    whole = pl.BlockSpec((D, D), lambda i: (0, 0), pipeline_mode=pl.Buffered(1))
    low = jax.ShapeDtypeStruct((T, D), MXU_DTYPE)
    return pl.pallas_call(
        body, name="d_branch_merge", grid=(T // tm,),
        in_specs=[rows, whole, whole, whole, rows, rows, rows, pl.BlockSpec((tm, D), lambda i: (i, 1))],
        out_specs=[rows] * 4 + [pl.BlockSpec((pl.Element(tm), pl.Element(2 * D)), lambda i: (
            pl.multiple_of(i * tm, ROW_ALIGN), first))],
        out_shape=[low] * 3 + [jax.ShapeDtypeStruct((T, D), F32), jax.ShapeDtypeStruct((T, d_in_width), MXU_DTYPE)],
        compiler_params=_params(("parallel",)),
    )(dh, w_out, w_a, w_b, p_a, p_b, gates, gates)


def _colsum_partials(p):
    return jnp.sum(p.reshape(-1, 8, p.shape[-1])[:, 0, :], axis=0, keepdims=True)


def _rmsnorm_bwd_vals(dy, xin, g):
    rstd = lax.rsqrt(jnp.mean(xin * xin, axis=-1, keepdims=True) + EPS)
    xhat = xin * rstd
    dg = jnp.sum(dy * xhat, axis=0, keepdims=True)
    dxh = dy * g
    dx = rstd * (dxh - xhat * jnp.mean(dxh * xhat, axis=-1, keepdims=True))
    return dx, dg


ATTN_SCALE = 1.0 / math.sqrt(HEAD_DIM)
GROUP_LANES = GROUP * ATTN_BLOCK
PAIR = 2 * HEAD_DIM


def _attn_mask():
    kj = lax.broadcasted_iota(jnp.int32, (ATTN_BLOCK, GROUP_LANES), 0)
    qi = lax.broadcasted_iota(jnp.int32, (ATTN_BLOCK, GROUP_LANES), 1) & (ATTN_BLOCK - 1)
    return kj <= qi


def _heads_transposed(ref, g, scale=None):
    parts = []
    for a in range(GROUP // 2):
        lo = (g * GROUP // 2 + a) * PAIR
        pair = ref[:, lo:lo + PAIR].astype(F32)
        pair = (pair if scale is None else pair * scale).T
        parts += [pair[:HEAD_DIM], pair[HEAD_DIM:]]
    return jnp.concatenate(parts, axis=1).astype(MXU_DTYPE)


def _heads_back(ref, g, vt):
    for a in range(GROUP // 2):
        lo = (g * GROUP // 2 + a) * PAIR
        pair = jnp.concatenate([vt[:, (2 * a) * ATTN_BLOCK:(2 * a + 1) * ATTN_BLOCK],
                                vt[:, (2 * a + 1) * ATTN_BLOCK:(2 * a + 2) * ATTN_BLOCK]], axis=0)
        ref[:, lo:lo + PAIR] = pair.T.astype(ref.dtype)


def _kv_parts(kv_ref, g):
    ks = slice(g * HEAD_DIM, (g + 1) * HEAD_DIM)
    vs = slice(KV_WIDTH + g * HEAD_DIM, KV_WIDTH + (g + 1) * HEAD_DIM)
    return kv_ref[:, ks].astype(MXU_DTYPE), kv_ref[:, vs].astype(MXU_DTYPE)


def _sink_rows(sinks):
    return jnp.repeat(sinks.reshape(KV_HEADS, GROUP), ATTN_BLOCK, axis=1)


def _attn_fwd(pq, pkv, sinks, comm=None):
    T = pq.shape[0]
    nb = T // ATTN_BLOCK

    def body(q_ref, kvc_ref, kvp_ref, s_ref, y_ref, lse_ref):
        mask_c = _attn_mask()
        has_prev = pl.program_id(0) > 0
        early = []
        for g in range(KV_HEADS):
            (kc, vc), (kp, vp) = _kv_parts(kvc_ref, g), _kv_parts(kvp_ref, g)
            qt = _heads_transposed(q_ref, g, ATTN_SCALE)
            early.append((vc, vp, jnp.where(mask_c, _nn(kc, qt), jnp.where(has_prev, _nn(kp, qt), NEG_INF))))
        for g, (vc, vp, s) in enumerate(early):
            sink = s_ref[g:g + 1, :]
            m = jnp.maximum(jnp.max(s, axis=0, keepdims=True), sink)
            p = jnp.exp(s - m)
            den = jnp.sum(p, axis=0, keepdims=True) + jnp.exp(sink - m)
            pc = jnp.where(mask_c, p, 0.0)
            _heads_back(y_ref, g, (_tn(vc, pc) + _tn(vp, p - pc)) / den)
            lse = m + jnp.log(den)
            for i in range(GROUP):
                lse_ref[g * GROUP + i:g * GROUP + i + 1, :] = lse[:, i * ATTN_BLOCK:(i + 1) * ATTN_BLOCK]

    return _call(
        "attn_fwd", body, grid=(nb,),
        in_specs=[pl.BlockSpec((ATTN_BLOCK, D_MODEL), lambda n: (n, 0)),
                  pl.BlockSpec((ATTN_BLOCK, 2 * KV_WIDTH), lambda n: (n, 0)),
                  pl.BlockSpec((ATTN_BLOCK, 2 * KV_WIDTH), lambda n: (jnp.maximum(n - 1, 0), 0)),
                  pl.BlockSpec((KV_HEADS, GROUP_LANES), lambda n: (0, 0))],
        out_specs=[pl.BlockSpec((ATTN_BLOCK, D_MODEL), lambda n: (n, 0)),
                   pl.BlockSpec((Q_HEADS, ATTN_BLOCK), lambda n: (0, n))],
        out_shape=[jax.ShapeDtypeStruct((T, D_MODEL), MXU_DTYPE), jax.ShapeDtypeStruct((Q_HEADS, T), F32)],
        args=[pq, pkv, pkv, _sink_rows(sinks)], semantics=("parallel",), comm=comm)


def _attn_bwd(pq, pkv, sinks, lse, dy, d_in, comm=None):
    T = pq.shape[0]
    nb = T // ATTN_BLOCK
    cur = lambda n: (jnp.minimum(n, nb - 1), 0)
    done = D_MODEL + 2 * KV_WIDTH

    def body(q_ref, kvc_ref, kvp_ref, s_ref, lse_ref, dy_ref, _, out_ref, ds_ref, carry, top, bot, dq_ref):
        n = pl.program_id(0)

        @pl.when(n == 0)
        def _():
            carry[...] = jnp.zeros_like(carry)
            dq_ref[...] = jnp.zeros_like(dq_ref)
            ds_ref[...] = jnp.zeros_like(ds_ref)

        out_ref[:, :D_MODEL] = dq_ref[...]

        @pl.when(n < nb)
        def _():
            mask_c = _attn_mask()
            valid = jnp.logical_or(mask_c, n > 0)
            early = []
            for g in range(KV_HEADS):
                (kc, vc), (kp, vp) = _kv_parts(kvc_ref, g), _kv_parts(kvp_ref, g)
                qt = _heads_transposed(q_ref, g, ATTN_SCALE)
                dot = _heads_transposed(dy_ref, g)
                early.append((kc, kp, qt, dot, jnp.where(mask_c, _nn(kc, qt), _nn(kp, qt)),
                              jnp.where(mask_c, _nn(vc, dot), _nn(vp, dot))))
            for g, (kc, kp, qt, dot, s, dp) in enumerate(early):
                ks = slice(g * HEAD_DIM, (g + 1) * HEAD_DIM)
                vs = slice(KV_WIDTH + g * HEAD_DIM, KV_WIDTH + (g + 1) * HEAD_DIM)
                lse = jnp.concatenate([lse_ref[g * GROUP + i:g * GROUP + i + 1, :] for i in range(GROUP)], axis=1)
                p = jnp.where(valid, jnp.exp(s - lse), 0.0)
                delta = jnp.sum(p * dp, axis=0, keepdims=True)
                ds = p * (dp - delta)
                ds_c, p_c = jnp.where(mask_c, ds, 0.0), jnp.where(mask_c, p, 0.0)
                ds_p, p_p = ds - ds_c, p - p_c
                _heads_back(dq_ref, g, (_tn(kc, ds_c) + _tn(kp, ds_p)) * ATTN_SCALE)
                bot[:, ks], bot[:, vs] = _nt(ds_c, qt), _nt(p_c, dot)
                top[:, ks], top[:, vs] = _nt(ds_p, qt), _nt(p_p, dot)
                ds_ref[g:g + 1, :] -= jnp.exp(s_ref[g:g + 1, :] - lse) * delta
            out_ref[:, D_MODEL:] = (carry[...] + top[...]).astype(out_ref.dtype)
            carry[...] = bot[...]

        @pl.when(n == nb)
        def _():
            out_ref[:, D_MODEL:] = carry[...].astype(out_ref.dtype)

    return _call(
        "attn_bwd", body, grid=(nb + 1,),
        in_specs=[pl.BlockSpec((ATTN_BLOCK, D_MODEL), cur),
                  pl.BlockSpec((ATTN_BLOCK, 2 * KV_WIDTH), cur),
                  pl.BlockSpec((ATTN_BLOCK, 2 * KV_WIDTH), lambda n: (jnp.maximum(jnp.minimum(n, nb - 1) - 1, 0), 0)),
                  pl.BlockSpec((KV_HEADS, GROUP_LANES), lambda n: (0, 0)),
                  pl.BlockSpec((Q_HEADS, ATTN_BLOCK), lambda n: (0, jnp.minimum(n, nb - 1))),
                  pl.BlockSpec((ATTN_BLOCK, D_MODEL), cur), HBM_SPEC],
        out_specs=[pl.BlockSpec((ATTN_BLOCK, done), lambda n: (jnp.maximum(n - 1, 0), 0)),
                   pl.BlockSpec((KV_HEADS, GROUP_LANES), lambda n: (0, 0))],
        out_shape=[jax.ShapeDtypeStruct(d_in.shape, d_in.dtype), jax.ShapeDtypeStruct((KV_HEADS, GROUP_LANES), F32)],
        scratch=[pltpu.VMEM((ATTN_BLOCK, 2 * KV_WIDTH), F32)] * 3 + [pltpu.VMEM((ATTN_BLOCK, D_MODEL), MXU_DTYPE)],
        args=[pq, pkv, pkv, _sink_rows(sinks), lse, dy, d_in], semantics=("arbitrary",), comm=comm, aliases={6: 0})


def _lower_bound(l):
    m = jnp.maximum(l[0:1], l[1:2])
    e0, e1 = jnp.exp(l[0:1] - m), jnp.exp(l[1:2] - m)
    return e0 / (e0 + e1)


def _tri(lower):
    r = lax.broadcasted_iota(jnp.int32, (CHUNK, CHUNK), 0)
    c = lax.broadcasted_iota(jnp.int32, (CHUNK, CHUNK), 1)
    return (r >= c) if lower else (c >= r)


def _chunk_sum(mask, v):
    ones = mask.astype(BF16)
    hi = v.astype(BF16)
    rest = v - hi.astype(F32)
    mid = rest.astype(BF16)
    lo = (rest - mid.astype(F32)).astype(BF16)
    part = lambda t: lax.dot_general(ones, t, (((1,), (0,)), ((), ())), preferred_element_type=F32)
    return part(hi) + part(mid) + part(lo)


def _hgrn_chunk_inputs(hq, hf, lb, causal):
    half_t = 0.5 * jnp.tanh(0.5 * hf)
    sg, sgn = 0.5 + half_t, 0.5 - half_t
    f = lb + (1.0 - lb) * sg
    kk = (1.0 - lb) * sgn
    sq = _sigmoid(hq)
    q = hq * sq
    b = _chunk_sum(causal, jnp.log(f))
    bm, bl = b[CHUNK // 2 - 1:CHUNK // 2, :], b[CHUNK - 1:CHUNK, :]
    e_qm, e_km = jnp.exp(b - bm), jnp.exp(bm - b)
    e_qs, e_kl = e_qm * jnp.exp(bm), e_km * jnp.exp(bl - bm)
    return dict(sg=sg, sgn=sgn, f=f, kk=kk, sq=sq, q=q, e_qm=e_qm, e_km=e_km, e_qs=e_qs, e_kl=e_kl,
                qm=q * e_qm, km=kk * e_km, qs=q * e_qs, kl=kk * e_kl, el=jnp.exp(bl))


def _hgrn_fwd(ph, lb_logits, norm_g, comm=None):
    T = ph.shape[0]
    nblk, cpb = T // HGRN_TOKENS, HGRN_TOKENS // CHUNK
    col = lambda c: pl.BlockSpec((HGRN_TOKENS, D_MODEL), functools.partial(lambda i, c: (i, c), c=c))

    def body(hq_ref, hf_ref, hi_ref, hg_ref, l_ref, ng_ref, y_ref, o_ref, st_ref, s_ref):
        @pl.when(pl.program_id(0) == 0)
        def _():
            s_ref[...] = jnp.zeros_like(s_ref)

        lb = _lower_bound(l_ref[...])
        causal = _tri(True)
        for c in range(cpb):
            rows = slice(c * CHUNK, (c + 1) * CHUNK)
            t = _hgrn_chunk_inputs(hq_ref[rows, :], hf_ref[rows, :], lb, causal)
            qm, km, qs, kl = (t[n].astype(MXU_DTYPE) for n in ("qm", "km", "qs", "kl"))
            v = hi_ref[rows, :].astype(MXU_DTYPE)
            heads = [slice(h * HGRN_K, (h + 1) * HGRN_K) for h in range(HGRN_HEADS)]
            a_all = [jnp.where(causal, _nt(qm[:, ls], km[:, ls]), 0.0).astype(MXU_DTYPE) for ls in heads]
            for h, ls in enumerate(heads):
                st = s_ref[h]
                st_ref[c, ls, :] = st
                o_ref[rows, ls] = _nn(a_all[h], v[:, ls]) + _nt(qs[:, ls], st)
                s_ref[h] = t["el"][:, ls] * st + _tn(v[:, ls], kl[:, ls])
        for h in range(HGRN_HEADS):
            ls = slice(h * HGRN_K, (h + 1) * HGRN_K)
            o = o_ref[:, ls]
            r = lax.rsqrt(jnp.mean(o * o, axis=-1, keepdims=True) + EPS)
            y_ref[:, ls] = (o * r * ng_ref[:, ls] * _sigmoid(hg_ref[:, ls])).astype(y_ref.dtype)

    return _call(
        "hgrn_fwd", body, grid=(nblk,),
        in_specs=[col(0), col(1), col(2), col(3),
                  pl.BlockSpec((2, D_MODEL), lambda i: (0, 0)), pl.BlockSpec((1, D_MODEL), lambda i: (0, 0))],
        out_specs=[pl.BlockSpec((HGRN_TOKENS, D_MODEL), lambda i: (i, 0)),
                   pl.BlockSpec((HGRN_TOKENS, D_MODEL), lambda i: (i, 0)),
                   pl.BlockSpec((cpb, D_MODEL, HGRN_K), lambda i: (i, 0, 0))],
        out_shape=[jax.ShapeDtypeStruct((T, D_MODEL), MXU_DTYPE), jax.ShapeDtypeStruct((T, D_MODEL), F32),
                   jax.ShapeDtypeStruct((T // CHUNK, D_MODEL, HGRN_K), F32)],
        scratch=[pltpu.VMEM((HGRN_HEADS, HGRN_K, HGRN_K), F32)],
        args=[ph, ph, ph, ph, lb_logits, norm_g], semantics=("arbitrary",), comm=comm)


def _hgrn_bwd(ph, o_raw, states, dy, lb_logits, norm_g, d_in, first, comm=None):
    T = ph.shape[0]
    nblk, cpb = T // HGRN_TOKENS, HGRN_TOKENS // CHUNK
    rev = lambda i: nblk - 1 - i
    col = lambda c: pl.BlockSpec((HGRN_TOKENS, D_MODEL), functools.partial(lambda i, c: (rev(i), c), c=c))
    tok = pl.BlockSpec((HGRN_TOKENS, D_MODEL), lambda i: (rev(i), 0))

    def body(hq_ref, hf_ref, hi_ref, hg_ref, o_ref, st_ref, dy_ref, l_ref, ng_ref, _,
             dph_ref, dng_ref, dl_ref, dst_ref, dlb_ref, do_s, dqm_s, dkm_s, dqs_s, dkl_s, dv_s, del_s):
        i = pl.program_id(0)

        @pl.when(i == 0)
        def _():
            dst_ref[...] = jnp.zeros_like(dst_ref)
            dlb_ref[...] = jnp.zeros_like(dlb_ref)
            dng_ref[...] = jnp.zeros_like(dng_ref)

        lb = _lower_bound(l_ref[...])
        causal, anti = _tri(True), _tri(False)
        row = lax.broadcasted_iota(jnp.int32, (CHUNK, D_MODEL), 0)
        for c in reversed(range(cpb)):
            rows = slice(c * CHUNK, (c + 1) * CHUNK)
            hq = hq_ref[rows, :]
            t = _hgrn_chunk_inputs(hq, hf_ref[rows, :], lb, causal)
            sgg = _sigmoid(hg_ref[rows, :])
            dyv = dy_ref[rows, :]
            for h in range(HGRN_HEADS):
                ls = slice(h * HGRN_K, (h + 1) * HGRN_K)
                o = o_ref[rows, ls]
                r = lax.rsqrt(jnp.mean(o * o, axis=-1, keepdims=True) + EPS)
                nrm = o * r
                g_h = sgg[:, ls]
                dph_ref[rows, 3 * D_MODEL + h * HGRN_K:3 * D_MODEL + (h + 1) * HGRN_K] = (
                    dyv[:, ls] * nrm * ng_ref[:, ls] * g_h * (1.0 - g_h)).astype(dph_ref.dtype)
                dyg = dyv[:, ls] * g_h
                dng_ref[:, ls] += jnp.sum(dyg * nrm, axis=0, keepdims=True)
                dn = dyg * ng_ref[:, ls]
                do_s[:, ls] = r * (dn - nrm * jnp.mean(dn * nrm, axis=-1, keepdims=True))
            qm, km, qs, kl = (t[n].astype(MXU_DTYPE) for n in ("qm", "km", "qs", "kl"))
            v = hi_ref[rows, :].astype(MXU_DTYPE)
            do = do_s[...].astype(MXU_DTYPE)
            heads = [slice(h * HGRN_K, (h + 1) * HGRN_K) for h in range(HGRN_HEADS)]
            a_all = [jnp.where(causal, _nt(qm[:, ls], km[:, ls]), 0.0).astype(MXU_DTYPE) for ls in heads]
            da_all = [jnp.where(causal, _nt(do[:, ls], v[:, ls]), 0.0).astype(MXU_DTYPE) for ls in heads]
            for h, ls in enumerate(heads):
                st = st_ref[c, ls, :]
                dst = dst_ref[h]
                a, da = a_all[h], da_all[h]
                dv_s[:, ls] = _tn(a, do[:, ls]) + _nt(kl[:, ls], dst)
                dkl_s[:, ls] = _nn(v[:, ls], dst)
                dqs_s[:, ls] = _nn(do[:, ls], st)
                del_s[:, ls] = jnp.sum(dst * st, axis=0, keepdims=True)
                dst_ref[h] = _tn(do[:, ls], qs[:, ls]) + t["el"][:, ls] * dst
                dqm_s[:, ls] = _nn(da, km[:, ls])
                dkm_s[:, ls] = _tn(da, qm[:, ls])
            dqm, dkm, dqs, dkl = dqm_s[...], dkm_s[...], dqs_s[...], dkl_s[...]
            dq = dqm * t["e_qm"] + dqs * t["e_qs"]
            dk = dkm * t["e_km"] + dkl * t["e_kl"]
            t_qm, t_km, t_kl = dqm * t["qm"], dkm * t["km"], dkl * t["kl"]
            db = t_qm - t_km + dqs * t["qs"] - t_kl
            db_mid = jnp.sum(t_km - t_qm, axis=0, keepdims=True)
            db_last = jnp.sum(t_kl, axis=0, keepdims=True) + del_s[...] * t["el"]
            db = db + jnp.where(row == CHUNK // 2 - 1, db_mid, 0.0) + jnp.where(row == CHUNK - 1, db_last, 0.0)
            dlogf = _chunk_sum(anti, db)
            sq, sg, sgn, f = t["sq"], t["sg"], t["sgn"], t["f"]
            dph_ref[rows, 0:D_MODEL] = (dq * (sq * (1.0 + hq * (1.0 - sq)))).astype(dph_ref.dtype)
            dph_ref[rows, D_MODEL:2 * D_MODEL] = (
                dlogf * (1.0 - lb) * sg * (1.0 - sg) / f - dk * (1.0 - lb) * sgn * (1.0 - sgn)).astype(dph_ref.dtype)
            dph_ref[rows, 2 * D_MODEL:3 * D_MODEL] = dv_s[...].astype(dph_ref.dtype)
            dlb_ref[...] += jnp.sum(dlogf * (1.0 - sg) / f - dk * sgn, axis=0, keepdims=True)

        @pl.when(i == nblk - 1)
        def _():
            dl0 = dlb_ref[...] * lb * (1.0 - lb)
            dl_ref[0:1, :] = dl0
            dl_ref[1:2, :] = -dl0

    wide = pltpu.VMEM((CHUNK, D_MODEL), F32)
    return _call(
        "hgrn_bwd", body, grid=(nblk,),
        in_specs=[col(0), col(1), col(2), col(3), tok,
                  pl.BlockSpec((cpb, D_MODEL, HGRN_K), lambda i: (rev(i), 0, 0)), tok,
                  pl.BlockSpec((2, D_MODEL), lambda i: (0, 0)), pl.BlockSpec((1, D_MODEL), lambda i: (0, 0)), HBM_SPEC],
        out_specs=[pl.BlockSpec((pl.Element(HGRN_TOKENS), pl.Element(4 * D_MODEL)), lambda i: (
                       pl.multiple_of(rev(i) * HGRN_TOKENS, ROW_ALIGN), first)),
                   pl.BlockSpec((1, D_MODEL), lambda i: (0, 0)), pl.BlockSpec((2, D_MODEL), lambda i: (0, 0))],
        out_shape=[jax.ShapeDtypeStruct(d_in.shape, d_in.dtype), jax.ShapeDtypeStruct((1, D_MODEL), F32),
                   jax.ShapeDtypeStruct((2, D_MODEL), F32)],
        scratch=[pltpu.VMEM((HGRN_HEADS, HGRN_K, HGRN_K), F32), pltpu.VMEM((1, D_MODEL), F32),
                 wide, wide, wide, wide, wide, wide, pltpu.VMEM((1, D_MODEL), F32)],
        args=[ph, ph, ph, ph, o_raw, states, dy, lb_logits, norm_g, d_in], semantics=("arbitrary",), comm=comm,
        aliases={9: 0})


def _local_step(x, target, vec, net):
    T, D = x.shape
    norm_mix_g, b_in, sinks, lb_logits = vec["norm_mix_g"], vec["b_in"], vec["attn_sinks"], vec["hgrn_lb_logits"]
    hgrn_norm_g, norm_ffn_g, norm_final_g = vec["hgrn_norm_g"], vec["norm_ffn_g"], vec["norm_final_g"]
    w_in = net.full("w_in")
    o_q, o_kv, o_h, o_g = (sum(IN_SPLITS[:i]) for i in range(4))
    TM = 512

    first = ("w_branch_attn", "w_branch_hgrn", "w_ffn_down")
    (u, pq, pkv, ph, pg), got = _in_proj(x, norm_mix_g, w_in, b_in, tm=256, comm=net.fetch(first))
    net.fetched(first, got)
    second = ("w_ffn_gate",)
    (y_attn, lse), got = _attn_fwd(pq, pkv, sinks, comm=_join(net.relay(first), net.fetch(second)))
    net.gathered(first, got[:len(first)])
    net.fetched(second, got[len(first):])
    third = ("w_ffn_up", "w_out")
    (y_hgrn, o_raw, states), got = _hgrn_fwd(ph, lb_logits, hgrn_norm_g,
                                             comm=_join(net.relay(second), net.fetch(third)))
    net.gathered(second, got[:len(second)])
    net.fetched(third, got[len(second):])
    w_ba, w_bh = net.full("w_branch_attn"), net.full("w_branch_hgrn")
    (ya, yb, merged), got = _merge_fwd(y_attn, y_hgrn, w_ba, w_bh, pg, tm=TM, comm=net.relay(third))
    net.gathered(third, got)
    w_gate, w_up, w_out = net.full("w_ffn_gate"), net.full("w_ffn_up"), net.full("w_out")

    (h1, u2, gpre, up, z), _ = _ffn_fwd(merged, w_out, x, norm_ffn_g, w_gate, w_up, tm=256)
    w_down = net.full("w_ffn_down")

    dh2, dh2b, dgp, dup, dgf_p, loss_p = _ffn_tail(z, w_down, h1, target, norm_final_g, gpre, up, tm=512)
    loss = jnp.sum(loss_p.reshape(-1, 8, D)[:, 0, 0])
    d_norm_final = _colsum_partials(dgf_p)
    d_w_down = _weight_grad("dw_down", z, dh2b, tm=DW_ROWS, tk=T)

    names, swap = ("w_ffn_down",), ()
    (dh1, dh1b, dg2_p), got = _ffn_in_bwd(dgp, dup, w_gate, w_up, h1, norm_ffn_g, dh2, tm=256,
                                          comm=net.exchange(dict(w_ffn_down=[d_w_down]), swap))
    net.received(names, swap, got)
    d_norm_ffn = _colsum_partials(dg2_p)
    d_w_gate = _weight_grad("dw_gate", dgp, u2, tm=DW_ROWS, tk=T)
    d_w_up = _weight_grad("dw_up", dup, u2, tm=DW_ROWS, tk=T)

    rows_in = sum(IN_SPLITS)
    dya, dyb, dy_attn, dy_hgrn, d_in = _merge_bwd(dh1b, w_out, w_ba, w_bh, ya, yb, pg, tm=TM, d_in_width=rows_in,
                                                  first=o_g)
    d_w_out = _weight_grad("dw_out", merged, dh1b, tm=1024, tk=1024)
    d_w_ba = _weight_grad("dw_branch_a", y_attn, dya, tm=1024, tk=1024)
    d_w_bh = _weight_grad("dw_branch_b", y_hgrn, dyb, tm=1024, tk=1024)

    names, swap = ("w_ffn_gate",), ("w_ffn_down",)
    (d_in, dsink), got = _attn_bwd(pq, pkv, sinks, lse, dy_attn, d_in,
                                   comm=net.exchange(dict(w_ffn_gate=[d_w_gate]), swap))
    net.received(names, swap, got)
    names, swap = ("w_ffn_up",), ("w_ffn_gate",)
    (d_in, d_hgrn_norm, d_lb_logits), got = _hgrn_bwd(
        ph, o_raw, states, dy_hgrn, lb_logits, hgrn_norm_g, d_in, o_h,
        comm=net.exchange(dict(w_ffn_up=[d_w_up]), swap))
    net.received(names, swap, got)

    names, swap = ("w_out", "w_branch_attn", "w_branch_hgrn"), ("w_ffn_up",)
    (*d_w_in, d_b_in), got = _weight_grad(
        "dw_in", d_in, u, tm=DW_ROWS, tk=T, a_colsum=True, carrying=True,
        comm=net.exchange(dict(w_out=[d_w_out], w_branch_attn=[d_w_ba], w_branch_hgrn=[d_w_bh]), swap))
    net.received(names, swap, got)
    d_w_in = [tuple(d_w_in)]

    first_level = net.presum_begin("w_in", d_w_in)
    halves = net.presum_end("w_in", [] if first_level is None else _copies_alone("presum_swap_w_in", first_level))
    names, swap = ("w_in",), ("w_out", "w_branch_attn", "w_branch_hgrn")
    (dx, dg1_p), got = _in_proj_bwd([(d_in, 0)], w_in, x, norm_mix_g, dh1, tm=256,
                                    comm=_join(halves, net.swap(swap)))
    net.last = (names, swap, got)
    d_norm_mix = _colsum_partials(dg1_p)
    vecs = dict(norm_mix_g=d_norm_mix, b_in=d_b_in, attn_sinks=jnp.sum(dsink.reshape(Q_HEADS, ATTN_BLOCK), axis=1).reshape(1, Q_HEADS),
                hgrn_lb_logits=d_lb_logits,
                hgrn_norm_g=d_hgrn_norm, norm_ffn_g=d_norm_ffn, norm_final_g=d_norm_final)
    return loss, dx, vecs


def _place():
    return lax.axis_index("x"), lax.axis_index("y"), lax.axis_index("c")


def _other_chips(x, y):
    return [(1 - x, y), (x, 1 - y), (1 - x, 1 - y)]


def _y_first(copies):
    return [copies[3 * (i // 3) + (1, 0, 2)[i % 3]] for i in range(len(copies))]


def _gather_copies(shards):
    n = len(shards)

    def build(ins, outs, send_sems, recv_sems, local_sems):
        x, y, c = _place()
        mine = 2 * x + y
        local = [pltpu.make_async_copy(ins[w], outs[w].at[mine], local_sems.at[w]) for w in range(n)]
        sends, recvs = [], []
        for w in range(n):
            for k, (px, py) in enumerate(_other_chips(x, y)):
                sem = 3 * w + k
                sends.append(pltpu.make_async_remote_copy(
                    src_ref=ins[w], dst_ref=outs[w].at[mine], send_sem=send_sems.at[sem], recv_sem=recv_sems.at[sem],
                    device_id=(px, py, c), device_id_type=MESH_ID))
                recvs.append(pltpu.make_async_remote_copy(
                    src_ref=ins[w], dst_ref=outs[w].at[2 * px + py], send_sem=send_sems.at[sem],
                    recv_sem=recv_sems.at[sem], device_id=(px, py, c), device_id_type=MESH_ID))
        return sends, recvs, local, _y_first(sends)

    return _Carried(shards, [jax.ShapeDtypeStruct((N_CHIPS,) + s.shape, s.dtype) for s in shards], 3 * n, n, build)


def _fetch_copies(shards):
    n = len(shards)

    def build(ins, outs, send_sems, recv_sems, local_sems, *stages):
        x, y, c = _place()
        mine = 2 * x + y
        local = [(pltpu.make_async_copy(ins[w], stages[w], local_sems.at[2 * w]),
                  pltpu.make_async_copy(stages[w], outs[w].at[mine], local_sems.at[2 * w + 1])) for w in range(n)]
        sends, recvs = [], []
        for w in range(n):
            half = shards[w].shape[0] // 2
            rows = pl.ds(c * half, half)
            for k, (px, py) in enumerate(_other_chips(x, y)):
                sem = 3 * w + k
                sends.append(pltpu.make_async_remote_copy(
                    src_ref=ins[w].at[rows], dst_ref=outs[w].at[mine, rows], send_sem=send_sems.at[sem],
                    recv_sem=recv_sems.at[sem], device_id=(px, py, c), device_id_type=MESH_ID))
                recvs.append(pltpu.make_async_remote_copy(
                    src_ref=ins[w].at[rows], dst_ref=outs[w].at[2 * px + py, rows], send_sem=send_sems.at[sem],
                    recv_sem=recv_sems.at[sem], device_id=(px, py, c), device_id_type=MESH_ID))
        return sends, recvs, local, _y_first(sends)

    return _Carried(shards, [jax.ShapeDtypeStruct((N_CHIPS,) + s.shape, s.dtype) for s in shards], 3 * n, 2 * n, build,
                    stages=[pltpu.VMEM(s.shape, s.dtype) for s in shards])


def _relay_copies(fetched):
    n = len(fetched)

    def build(ins, outs, send_sems, recv_sems, local_sems):
        x, y, c = _place()
        sends, recvs = [], []
        for w in range(n):
            half = fetched[w].shape[1] // 2
            mine, theirs = pl.ds(c * half, half), pl.ds((1 - c) * half, half)
            for k, (px, py) in enumerate(_other_chips(x, y)):
                block, sem = 2 * px + py, 3 * w + k
                sends.append(pltpu.make_async_remote_copy(
                    src_ref=ins[w].at[block, mine], dst_ref=outs[w].at[block, mine], send_sem=send_sems.at[sem],
                    recv_sem=recv_sems.at[sem], device_id=(x, y, 1 - c), device_id_type=MESH_ID))
                recvs.append(pltpu.make_async_remote_copy(
                    src_ref=ins[w].at[block, theirs], dst_ref=outs[w].at[block, theirs], send_sem=send_sems.at[sem],
                    recv_sem=recv_sems.at[sem], device_id=(x, y, 1 - c), device_id_type=MESH_ID))
        return sends, recvs, []

    return _Carried(fetched, [jax.ShapeDtypeStruct(f.shape, f.dtype) for f in fetched], 3 * n, 0, build,
                    continued={w: w for w in range(n)})


def _grad_copies(stacked):
    n = len(stacked)

    def build(ins, outs, send_sems, recv_sems, local_sems):
        x, y, c = _place()
        sends = []
        for w in range(n):
            for k, (px, py) in enumerate(_other_chips(x, y)):
                sem = 3 * w + k
                sends.append(pltpu.make_async_remote_copy(
                    src_ref=ins[w].at[2 * px + py], dst_ref=outs[w].at[k], send_sem=send_sems.at[sem],
                    recv_sem=recv_sems.at[sem], device_id=(px, py, c), device_id_type=MESH_ID))
        return sends, sends, [], _y_first(sends)

    return _Carried(stacked, [jax.ShapeDtypeStruct((3,) + s.shape[1:], s.dtype) for s in stacked], 3 * n, 0, build)


def _small_copies(small):
    def build(ins, outs, send_sems, recv_sems, local_sems):
        small_ref, all_ref = ins[0], outs[0]
        x, y, c = _place()
        me = 4 * x + 2 * y + c
        sends, recvs = [], []
        for r in range(1, 8):
            px = 1 - x if r & 4 else x
            py = 1 - y if r & 2 else y
            pc = 1 - c if r & 1 else c
            sends.append(pltpu.make_async_remote_copy(
                src_ref=small_ref, dst_ref=all_ref.at[me], send_sem=send_sems.at[r - 1], recv_sem=recv_sems.at[r - 1],
                device_id=(px, py, pc), device_id_type=MESH_ID))
            recvs.append(pltpu.make_async_remote_copy(
                src_ref=small_ref, dst_ref=all_ref.at[4 * px + 2 * py + pc], send_sem=send_sems.at[r - 1],
                recv_sem=recv_sems.at[r - 1], device_id=(px, py, pc), device_id_type=MESH_ID))
        return sends, recvs, [pltpu.make_async_copy(small_ref, all_ref.at[me], local_sems.at[0])]

    return _Carried([small], [jax.ShapeDtypeStruct((8,) + small.shape, small.dtype)], 7, 1, build)


def _gather_by_neighbours(name, shard):
    half = shard.shape[0] // 2
    quarter = half // 2

    def body(in_ref, out_ref, send_sems, recv_sems, local_sem, stage_ref):
        for core in (0, 1):
            @pl.when(lax.axis_index("c") == core)
            def _():
                program(core, in_ref, out_ref, send_sems, recv_sems, local_sem, stage_ref)

    def program(c, in_ref, out_ref, send_sems, recv_sems, local_sem, stage_ref):
        x, y, _ = _place()
        chip = lambda px, py: 2 * px + py
        to_x, to_y, sibling = (1 - x, y, c), (x, 1 - y, c), (x, y, 1 - c)
        x_blk, y_blk, d_blk = chip(1 - x, y), chip(x, 1 - y), chip(1 - x, 1 - y)
        mine, theirs = c * half, (1 - c) * half

        def copy(sem, rows, block, to, src=None):
            place = out_ref.at[block, pl.ds(rows[0], rows[1])]
            return pltpu.make_async_remote_copy(
                src_ref=place if src is None else src, dst_ref=place, send_sem=send_sems.at[sem],
                recv_sem=recv_sems.at[sem], device_id=to, device_id_type=MESH_ID)

        stage = pltpu.make_async_copy(in_ref, stage_ref, local_sem.at[0])
        own = pltpu.make_async_copy(stage_ref, out_ref.at[chip(x, y)], local_sem.at[1])
        my_rows = in_ref.at[pl.ds(mine, half)]
        along_x = dict(send=copy(0, (mine, half), chip(x, y), to_x, src=my_rows),
                       landed=copy(0, (mine, half), x_blk, to_x),
                       onward=[copy(3, (mine + quarter, quarter), x_blk, to_y), copy(4, (mine, half), x_blk, sibling)],
                       diagonal=copy(2, (mine, quarter), d_blk, to_x))
        along_y = dict(send=copy(1, (mine, half), chip(x, y), to_y, src=my_rows),
                       landed=copy(1, (mine, half), y_blk, to_y),
                       onward=[copy(2, (mine, quarter), y_blk, to_x), copy(5, (mine, half), y_blk, sibling)],
                       diagonal=copy(3, (mine + quarter, quarter), d_blk, to_y))
        last = copy(6, (mine, half), d_blk, sibling)

        order = (along_x, along_y) if c == 0 else (along_y, along_x)
        for axis in order:
            axis["send"].start()
        stage.start()
        stage.wait()
        own.start()
        for axis in order:
            axis["landed"].wait_recv()
            for cp in axis["onward"]:
                cp.start()
        for axis in order:
            axis["diagonal"].wait_recv()
        last.start()
        for sem, block in ((4, x_blk), (5, y_blk), (6, d_blk)):
            copy(sem, (theirs, half), block, sibling).wait_recv()
        for cp in [along_x["send"], along_y["send"]] + along_x["onward"] + along_y["onward"] + [last]:
            cp.wait_send()
        own.wait()

    return pl.pallas_call(
        body, name=name, in_specs=[HBM_SPEC], out_specs=HBM_SPEC,
        out_shape=jax.ShapeDtypeStruct((N_CHIPS,) + shard.shape, shard.dtype),
        scratch_shapes=[pltpu.SemaphoreType.DMA((7,)), pltpu.SemaphoreType.DMA((7,)), pltpu.SemaphoreType.DMA((2,)),
                        pltpu.VMEM(shard.shape, shard.dtype)],
    )(shard)


def _copies_alone(name, comm):
    return _call(name, lambda: None, grid=(), in_specs=[], out_specs=[], out_shape=[], args=[], comm=comm)[1]


class _Net:
    def __init__(self, shards):
        self.shards = shards
        self.whole, self.partly, self.own, self.theirs, self.sums, self.other = {}, {}, {}, {}, {}, {}
        x, y, _ = _place()
        self.chip = 2 * x + y

    def fetch(self, names):
        return _fetch_copies([self.shards[n] for n in names])

    def fetched(self, names, got):
        self.partly.update(zip(names, got))

    def relay(self, names):
        return _relay_copies([self.partly[n] for n in names])

    def gathered(self, names, got):
        for n, g in zip(names, got):
            self.whole[n] = g.reshape(-1, g.shape[-1])

    def full(self, name):
        return self.whole[name]

    def exchange(self, grads, swap=()):
        stacked = []
        for n, pieces in grads.items():
            (keep, send), = pieces
            self.own[n] = keep
            stacked.append(send.reshape(N_CHIPS, keep.shape[0] // N_CHIPS, send.shape[-1]))
        return _join(_grad_copies(stacked), self.swap(swap))

    def swap(self, names):
        return _sibling_copies([self.sums[n] for n in names]) if names else None

    def presum_begin(self, name, pieces):
        keep = jnp.concatenate([p[0] for p in pieces], axis=0) if len(pieces) > 1 else pieces[0][0]
        send = jnp.concatenate([p[1] for p in pieces], axis=0) if len(pieces) > 1 else pieces[0][1]
        rows = keep.shape[0] // N_CHIPS
        self.held = keep.reshape(N_CHIPS, rows, keep.shape[-1])
        return _half_rows_copies(send.reshape(N_CHIPS, rows, send.shape[-1]))

    def presum_end(self, name, got):
        x, y, c = _place()
        to_send, self.own[name] = _pre_sum("presum_" + name, self.held, got[0], jnp.stack([c, self.chip]))
        return _grad_copies([to_send])

    def received(self, names, swap, got, carried=None):
        self.theirs.update(zip(names, got[:len(names)]))
        self.other.update(zip(swap, got[len(names):]))
        for n in names:
            (self.sums[n],), more = _partial_sum("sum_" + n, self.own[n], self.theirs[n], self.chip, comm=carried)
        return more


def _half_rows_copies(stacked):
    n, rows = stacked.shape[0], stacked.shape[1] // 2

    def build(ins, outs, send_sems, recv_sems, local_sems):
        x, y, c = _place()
        copies = [pltpu.make_async_remote_copy(
            src_ref=ins[0].at[s, pl.ds((1 - c) * rows, rows)], dst_ref=outs[0].at[s], send_sem=send_sems.at[s],
            recv_sem=recv_sems.at[s], device_id=(x, y, 1 - c), device_id_type=MESH_ID) for s in range(n)]
        return copies, copies, []

    return _Carried([stacked], [jax.ShapeDtypeStruct((n, rows, stacked.shape[2]), stacked.dtype)], n, 0, build)


def _pre_sum(name, held, theirs, core_and_chip):
    n, R, C = held.shape
    half = R // 2
    tr = _row_tile(half)
    per_half = half // tr

    def body(place_ref, h_ref, t_ref, send_ref, own_ref):
        total = h_ref[0] + t_ref[0].astype(F32)
        send_ref[0] = total.astype(send_ref.dtype)

        @pl.when(pl.program_id(1) == place_ref[1])
        def _():
            own_ref[...] = total

    return pl.pallas_call(
        body, name=name,
        grid_spec=pltpu.PrefetchScalarGridSpec(
            num_scalar_prefetch=1, grid=(per_half, n),
            in_specs=[pl.BlockSpec((1, tr, C), lambda i, s, place: (s, place[0] * per_half + i, 0)),
                      pl.BlockSpec((1, tr, C), lambda i, s, place: (s, i, 0))],
            out_specs=[pl.BlockSpec((1, tr, C), lambda i, s, place: (s, i, 0)),
                       pl.BlockSpec((tr, C), lambda i, s, place: (i, 0))]),
        out_shape=[jax.ShapeDtypeStruct((n, half, C), MXU_DTYPE), jax.ShapeDtypeStruct((half, C), F32)],
        compiler_params=_params(("arbitrary", "arbitrary")),
    )(core_and_chip, held, theirs)


def _sibling_copies(parts):
    n = len(parts)

    def build(ins, outs, send_sems, recv_sems, local_sems):
        x, y, c = _place()
        copies = [pltpu.make_async_remote_copy(
            src_ref=ins[w], dst_ref=outs[w], send_sem=send_sems.at[w], recv_sem=recv_sems.at[w],
            device_id=(x, y, 1 - c), device_id_type=MESH_ID) for w in range(n)]
        return copies, copies, []

    return _Carried(parts, [jax.ShapeDtypeStruct(p.shape, p.dtype) for p in parts], n, 0, build)


def _row_tile(rows, most=512, sublanes=16):
    return max(t for t in range(sublanes, min(most, rows // 2) + 1, sublanes) if rows % t == 0)


def _partial_sum(name, own, recv, chip, comm=None):
    _, R, C = recv.shape
    tr = _row_tile(R)

    def body(o_ref, r_ref, p_ref):
        p_ref[...] = ((o_ref[...] + r_ref[0].astype(F32)) + r_ref[1].astype(F32)) + r_ref[2].astype(F32)

    if own.shape[0] != R:
        assert comm is None and own.shape[0] == N_CHIPS * R
        total = pl.pallas_call(
            lambda chip_ref, *refs: body(*refs), name=name,
            grid_spec=pltpu.PrefetchScalarGridSpec(
                num_scalar_prefetch=1, grid=(R // tr,),
                in_specs=[pl.BlockSpec((tr, C), lambda i, chip_ref: (chip_ref[0] * (R // tr) + i, 0)),
                          pl.BlockSpec((3, tr, C), lambda i, chip_ref: (0, i, 0))],
                out_specs=pl.BlockSpec((tr, C), lambda i, chip_ref: (i, 0))),
            out_shape=jax.ShapeDtypeStruct((R, C), F32), compiler_params=_params(("parallel",)),
        )(chip.reshape(1), own, recv)
        return [total], []
    return _call(name, body, grid=(R // tr,),
                 in_specs=[pl.BlockSpec((tr, C), lambda i: (i, 0)), pl.BlockSpec((3, tr, C), lambda i: (0, i, 0))],
                 out_specs=[pl.BlockSpec((tr, C), lambda i: (i, 0))], out_shape=[jax.ShapeDtypeStruct((R, C), F32)],
                 args=[own, recv], semantics=("parallel",), comm=comm)


def _adam_vals(w, g, m, v):
    m = ADAM_B1 * m + (1.0 - ADAM_B1) * g
    v = ADAM_B2 * v + (1.0 - ADAM_B2) * (g * g)
    m_hat = m / (1.0 - ADAM_B1 ** ADAM_STEP)
    v_hat = v / (1.0 - ADAM_B2 ** ADAM_STEP)
    delta = -ADAM_LR * (m_hat / (jnp.sqrt(v_hat) + ADAM_EPS) + ADAM_WD * w)
    return delta, m, v


def _adamw(name, w, m, v, mine, other, comm=None):
    R, C = w.shape
    tr = _row_tile(R)

    def body(w_ref, m_ref, v_ref, s_ref, n_ref, g_ref, d_ref, nm_ref, nv_ref):
        g = s_ref[...] + n_ref[...]
        d, nm, nv = _adam_vals(w_ref[...], g, m_ref[...], v_ref[...])
        g_ref[...], d_ref[...], nm_ref[...], nv_ref[...] = g, d, nm, nv

    spec = pl.BlockSpec((tr, C), lambda i: (i, 0))
    return _call(name, body, grid=(R // tr,), in_specs=[spec] * 5, out_specs=[spec] * 4,
                 out_shape=[jax.ShapeDtypeStruct((R, C), F32)] * 4, args=[w, m, v, mine, other],
                 semantics=("parallel",), comm=comm)


def _adamw_by_halves(name, w, m, v, mine, other, core):
    R, C = w.shape
    tr = _row_tile(R // 2)
    per_half = R // 2 // tr

    def body(c_ref, w_ref, m_ref, v_ref, s_ref, n_ref, g_ref, d_ref, nm_ref, nv_ref):
        g = jnp.where(pl.program_id(0) // per_half == c_ref[0, 0], s_ref[...], n_ref[...])
        d, nm, nv = _adam_vals(w_ref[...], g, m_ref[...], v_ref[...])
        g_ref[...], d_ref[...], nm_ref[...], nv_ref[...] = g, d, nm, nv

    spec = pl.BlockSpec((tr, C), lambda i: (i, 0))
    part = pl.BlockSpec((tr, C), lambda i: (i % per_half, 0))
    return pl.pallas_call(
        body, name=name, grid=(R // tr,),
        in_specs=[pl.BlockSpec(memory_space=pltpu.SMEM), spec, spec, spec, part, part], out_specs=[spec] * 4,
        out_shape=[jax.ShapeDtypeStruct((R, C), F32)] * 4, compiler_params=_params(("parallel",)),
    )(core, w, m, v, mine, other)


SMALL_LAYOUT = dict(norm_mix_g=(0, 1, 1024), b_in=(1, 8, 7424), hgrn_norm_g=(9, 1, 1024), norm_ffn_g=(10, 1, 1024),
                    norm_final_g=(11, 1, 1024), hgrn_lb_logits=(12, 2, 2048), attn_sinks=(14, 1, 16))
SMALL_LOSS_ROW, SMALL_ROWS = 15, 16


def _pack_small(grads, loss):
    rows = [jnp.pad(grads[name].astype(F32).reshape(-1), (0, nrows * D_MODEL - n))
            for name, (_, nrows, n) in SMALL_LAYOUT.items()]
    rows.append(jnp.pad(loss.astype(F32).reshape(1), (0, D_MODEL - 1)))
    return jnp.concatenate(rows).reshape(SMALL_ROWS, D_MODEL)


def _adamw_small(w, m, v, g_all):
    names = list(SMALL_LAYOUT)
    n = len(names)

    def body(a_ref, *refs):
        ins, outs = refs[:3 * n], refs[3 * n:]
        g_all_rows = a_ref[0]
        for dev in range(1, 8):
            g_all_rows = g_all_rows + a_ref[dev]
        for i, name in enumerate(names):
            first, nrows, count = SMALL_LAYOUT[name]
            w_ref, m_ref, v_ref = ins[3 * i:3 * i + 3]
            if w_ref.shape[0] == nrows:
                g = g_all_rows[first:first + nrows, :w_ref.shape[1]]
            else:
                last = count - (nrows - 1) * D_MODEL
                g = jnp.concatenate([g_all_rows[r:r + 1, :] for r in range(first, first + nrows - 1)]
                                    + [g_all_rows[first + nrows - 1:first + nrows, :last]], axis=1)
            d, nm, nv = _adam_vals(w_ref[...], g, m_ref[...], v_ref[...])
            for o_ref, val in zip(outs[4 * i:4 * i + 4], (g, d, nm, nv)):
                o_ref[...] = val
        outs[4 * n][...] = g_all_rows[SMALL_LOSS_ROW:SMALL_LOSS_ROW + 1, 0:1]

    res = pl.pallas_call(
        body, name="adamw_small",
        out_shape=[jax.ShapeDtypeStruct(w[name].shape, F32) for name in names for _ in range(4)]
        + [jax.ShapeDtypeStruct((1, 1), F32)],
    )(g_all, *[t[name] for name in names for t in (w, m, v)])
    return {name: res[4 * i:4 * i + 4] for i, name in enumerate(names)}, res[4 * n]


MATRICES = ("w_in", "w_branch_attn", "w_branch_hgrn", "w_out", "w_ffn_gate", "w_ffn_up", "w_ffn_down")
COLUMN_SHARDED = ("w_in", "w_ffn_gate", "w_ffn_up")
WEIGHTS = ("norm_mix_g", "w_in", "b_in", "attn_sinks", "hgrn_lb_logits", "hgrn_norm_g", "w_branch_attn",
           "w_branch_hgrn", "w_out", "norm_ffn_g", "w_ffn_gate", "w_ffn_up", "w_ffn_down", "norm_final_g")


def kernel(x, norm_mix_g, w_in, b_in, attn_sinks, hgrn_lb_logits, hgrn_norm_g, w_branch_attn, w_branch_hgrn, w_out, norm_ffn_g, w_ffn_gate, w_ffn_up, w_ffn_down, norm_final_g, loss_target, m_norm_mix_g, m_w_in, m_b_in, m_attn_sinks, m_hgrn_lb_logits, m_hgrn_norm_g, m_w_branch_attn, m_w_branch_hgrn, m_w_out, m_norm_ffn_g, m_w_ffn_gate, m_w_ffn_up, m_w_ffn_down, m_norm_final_g, v_norm_mix_g, v_w_in, v_b_in, v_attn_sinks, v_hgrn_lb_logits, v_hgrn_norm_g, v_w_branch_attn, v_w_branch_hgrn, v_w_out, v_norm_ffn_g, v_w_ffn_gate, v_w_ffn_up, v_w_ffn_down, v_norm_final_g):
    given = dict(locals())
    w = {n: given[n] for n in WEIGHTS}
    m = {n: given["m_" + n] for n in WEIGHTS}
    v = {n: given["v_" + n] for n in WEIGHTS}

    block = lambda a, n: jnp.transpose(a[0]) if n in COLUMN_SHARDED else a[0]
    unblock = lambda a, n: (jnp.transpose(a) if n in COLUMN_SHARDED else a)[None]
    net = _Net({n: block(w[n], n).astype(MXU_DTYPE) for n in MATRICES})
    net.gathered(("w_in",), [_gather_by_neighbours("gather_w_in", net.shards["w_in"])])
    vec = dict(norm_mix_g=norm_mix_g, b_in=b_in, attn_sinks=attn_sinks, hgrn_lb_logits=hgrn_lb_logits,
               hgrn_norm_g=hgrn_norm_g, norm_ffn_g=norm_ffn_g, norm_final_g=norm_final_g.reshape(1, D_MODEL))
    loss_part, dx, d_vecs = _local_step(x[0], loss_target[0], vec, net)

    small_all, = net.received(*net.last, carried=_small_copies(_pack_small(d_vecs, loss_part)))
    grads, deltas, new_m, new_v = {}, {}, {}, {}
    for n in ("w_ffn_down", "w_ffn_gate", "w_ffn_up", "w_out", "w_branch_attn", "w_branch_hgrn"):
        res, got = _adamw("adamw_" + n, block(w[n], n), block(m[n], n), block(v[n], n), net.sums[n], net.other[n],
                          comm=net.swap(("w_in",)) if n == "w_ffn_down" else None)
        if n == "w_ffn_down":
            net.other["w_in"], = got
        grads[n], deltas[n], new_m[n], new_v[n] = (unblock(r, n) for r in res)
    n = "w_in"
    res = _adamw_by_halves("adamw_" + n, block(w[n], n), block(m[n], n), block(v[n], n), net.sums[n], net.other[n],
                           _place()[2].reshape(1, 1))
    grads[n], deltas[n], new_m[n], new_v[n] = (unblock(r, n) for r in res)
    rows = lambda t: {n: t[n].reshape(-1, t[n].shape[-1]) for n in SMALL_LAYOUT}
    res, loss = _adamw_small(rows(w), rows(m), rows(v), small_all)
    for n, four in res.items():
        grads[n], deltas[n], new_m[n], new_v[n] = (r.reshape(w[n].shape) for r in four)
    loss = loss.reshape(())
    return (loss, dx[None], *[grads[n] for n in WEIGHTS], *[deltas[n] for n in WEIGHTS],
            *[new_m[n] for n in WEIGHTS], *[new_v[n] for n in WEIGHTS])
```

```python
import functools
import math

import jax
import jax.numpy as jnp
from jax import lax
from jax.experimental import pallas as pl
from jax.experimental.pallas import tpu as pltpu

F32 = jnp.float32
BF16 = jnp.bfloat16
MXU_DTYPE = jnp.bfloat16
SAVED_DTYPE = jnp.bfloat16
MESH_ID = pl.DeviceIdType.MESH

D_MODEL = 1024
HEAD_DIM = 64
Q_HEADS = 16
KV_HEADS = 2
GROUP = Q_HEADS // KV_HEADS
KV_WIDTH = KV_HEADS * HEAD_DIM
ATTN_BLOCK = 128
HGRN_HEADS = 8
HGRN_K = 128
CHUNK = 64
HGRN_TOKENS = 256
FFN = 2816
IN_SPLITS = (1024, 256, 4096, 2048)
EPS = 1e-6
NEG_INF = -1e30
ADAM_LR, ADAM_B1, ADAM_B2, ADAM_EPS, ADAM_WD, ADAM_STEP = 0.001, 0.9, 0.999, 1e-08, 0.01, 10
N_CHIPS = 4
VMEM_LIMIT = 60 * 1024 * 1024
ROW_ALIGN = 16
DW_ROWS = 256


def _params(sem=None):
    return pltpu.CompilerParams(dimension_semantics=sem, vmem_limit_bytes=VMEM_LIMIT)


def _sigmoid(v):
    return 0.5 * jnp.tanh(0.5 * v) + 0.5


def _dot(a, b, dims):
    return lax.dot_general(a.astype(MXU_DTYPE), b.astype(MXU_DTYPE), (dims, ((), ())),
                           preferred_element_type=F32)


def _nn(a, b):
    return _dot(a, b, ((1,), (0,)))


def _nt(a, b):
    return _dot(a, b, ((1,), (1,)))


def _tn(a, b):
    return _dot(a, b, ((0,), (0,)))


HBM_SPEC = pl.BlockSpec(memory_space=pl.ANY)


class _Carried:
    def __init__(self, arrays, out_shapes, n_remote, n_local, build, continued=None, stages=()):
        self.continued = dict(continued or {})
        self.parts = [(len(arrays), len(out_shapes), 3 + len(stages), build)]
        self.arrays, self.out_shapes = list(arrays), list(out_shapes)
        self.scratch = [pltpu.SemaphoreType.DMA((n_remote,)), pltpu.SemaphoreType.DMA((n_remote,)),
                        pltpu.SemaphoreType.DMA((max(n_local, 1),))] + list(stages)

    def __add__(self, other):
        both = _Carried([], [], 1, 0, None)
        both.parts = self.parts + other.parts
        both.arrays, both.out_shapes = self.arrays + other.arrays, self.out_shapes + other.out_shapes
        both.scratch = self.scratch + other.scratch
        both.continued = dict(self.continued)
        both.continued.update({len(self.arrays) + i: len(self.out_shapes) + o for i, o in other.continued.items()})
        return both

    def _built(self, ins, outs, sems):
        for ni, no, ns, build in self.parts:
            yield build(ins[:ni], outs[:no], *sems[:ns])
            ins, outs, sems = ins[ni:], outs[no:], sems[ns:]

    def start(self, ins, outs, sems):
        core = lax.axis_index("c")
        for sends, _, local, *other_order in self._built(ins, outs, sems):
            for cp in local:
                if not isinstance(cp, tuple):
                    cp.start()
            if not other_order:
                for cp in sends:
                    cp.start()
            else:
                @pl.when(core == 0)
                def _():
                    for cp in sends:
                        cp.start()

                @pl.when(core == 1)
                def _():
                    for cp in other_order[0]:
                        cp.start()
            staged = [cp for cp in local if isinstance(cp, tuple)]
            for into, _ in staged:
                into.start()
            for into, out_of in staged:
                into.wait()
                out_of.start()

    def wait(self, ins, outs, sems):
        for sends, recvs, local, *_ in self._built(ins, outs, sems):
            for cp in recvs:
                cp.wait_recv()
            for cp in sends:
                cp.wait_send()
            for cp in local:
                (cp[1] if isinstance(cp, tuple) else cp).wait()


def _join(*comms):
    comms = [c for c in comms if c is not None]
    return functools.reduce(lambda a, b: a + b, comms) if comms else None


def _call(name, body, *, grid, in_specs, out_specs, out_shape, args, scratch=(), semantics=None, comm=None,
          aliases=None):
    n_in, n_out, n_scr = len(in_specs), len(out_specs), len(scratch)
    aliases = aliases or {}
    if comm is None:
        res = pl.pallas_call(body, name=name, grid=grid, in_specs=in_specs, out_specs=out_specs, out_shape=out_shape,
                             scratch_shapes=list(scratch), input_output_aliases=aliases,
                             compiler_params=_params(semantics))(*args)
        return list(res), []
    ci, co = len(comm.arrays), len(comm.out_shapes)
    aliases = dict(aliases)
    aliases.update({n_in + i: n_out + o for i, o in comm.continued.items()})

    def carrying(*refs):
        ins, refs = refs[:n_in], refs[n_in:]
        c_ins, refs = refs[:ci], refs[ci:]
        outs, refs = refs[:n_out], refs[n_out:]
        c_outs, refs = refs[:co], refs[co:]
        scr, sems = refs[:n_scr], refs[n_scr:]
        if not grid:
            comm.start(c_ins, c_outs, sems)
            body(*ins, *outs, *scr)
            comm.wait(c_ins, c_outs, sems)
            return
        first = functools.reduce(jnp.logical_and, [pl.program_id(a) == 0 for a in range(len(grid))])
        last = functools.reduce(jnp.logical_and, [pl.program_id(a) == g - 1 for a, g in enumerate(grid)])

        @pl.when(first)
        def _():
            comm.start(c_ins, c_outs, sems)

        body(*ins, *outs, *scr)

        @pl.when(last)
        def _():
            comm.wait(c_ins, c_outs, sems)

    res = pl.pallas_call(
        carrying, name=name, grid=grid, in_specs=list(in_specs) + [HBM_SPEC] * ci,
        out_specs=list(out_specs) + [HBM_SPEC] * co, out_shape=list(out_shape) + comm.out_shapes,
        scratch_shapes=list(scratch) + comm.scratch, input_output_aliases=aliases,
        compiler_params=_params(("arbitrary",) * len(grid) if grid else None),
    )(*args, *comm.arrays)
    return list(res[:n_out]), list(res[n_out:])


def _weight_grad(name, a, b, *, tm, tk, a_colsum=False, carrying=False, comm=None):
    (T, N), M = b.shape, a.shape[1]
    tk = min(tk, T)
    assert M % tm == 0 and T % tk == 0, (name, M, tm, T, tk)
    ni, nk = M // tm, T // tk

    def body(a_ref, b_ref, *rest):
        keep_ref, send_ref = rest[:2]
        sums_ref = rest[2] if a_colsum else None
        if nk == 1:
            acc = _tn(a_ref[...], b_ref[...])
            keep_ref[...], send_ref[...] = acc, acc.astype(send_ref.dtype)
            if a_colsum:
                sums_ref[...] = jnp.sum(a_ref[...].astype(F32), axis=0, keepdims=True)
            return
        acc_ref = rest[-1]
        k = pl.program_id(1)

        @pl.when(k == 0)
        def _():
            acc_ref[...] = jnp.zeros_like(acc_ref)
            if a_colsum:
                sums_ref[...] = jnp.zeros((1, tm), F32)

        if a_colsum:
            sums_ref[...] += jnp.sum(a_ref[...].astype(F32), axis=0, keepdims=True)
        acc_ref[...] += _tn(a_ref[...], b_ref[...])

        @pl.when(k == nk - 1)
        def _():
            keep_ref[...], send_ref[...] = acc_ref[...], acc_ref[...].astype(send_ref.dtype)

    out_spec = pl.BlockSpec((tm, N), lambda i, k: (i, 0))
    out_shape = [jax.ShapeDtypeStruct((M, N), F32), jax.ShapeDtypeStruct((M, N), MXU_DTYPE)]
    out_specs = [out_spec, out_spec]
    if a_colsum:
        out_shape.append(jax.ShapeDtypeStruct((1, M), F32))
        out_specs.append(pl.BlockSpec((1, tm), lambda i, k: (0, i)))
    res, got = _call(
        name, body, grid=(ni, nk),
        in_specs=[pl.BlockSpec((tk, tm), lambda i, k: (k, i)), pl.BlockSpec((tk, N), lambda i, k: (k, 0))],
        out_specs=out_specs, out_shape=out_shape, scratch=[pltpu.VMEM((tm, N), F32)] if nk > 1 else [],
        args=[a, b], semantics=("parallel", "arbitrary"), comm=comm)
    return (res, got) if carrying else res


def _ffn_fwd(merged, w_out, x, gain, w_gate_t, w_up_t, *, tm, comm=None):
    (T, D), F = x.shape, w_gate_t.shape[0]

    def body(m_ref, wo_ref, x_ref, g_ref, wg_ref, wu_ref, h_ref, u_ref, gate_ref, up_ref, z_ref):
        h = x_ref[...] + _nn(m_ref[...], wo_ref[...])
        h_ref[...] = h
        u = (h * lax.rsqrt(jnp.mean(h * h, axis=-1, keepdims=True) + EPS) * g_ref[...]).astype(u_ref.dtype)
        u_ref[...] = u
        gate, up = _nt(u, wg_ref[...]), _nt(u, wu_ref[...])
        gate_ref[...], up_ref[...] = gate.astype(gate_ref.dtype), up.astype(up_ref.dtype)
        z_ref[...] = (gate * _sigmoid(gate) * up).astype(z_ref.dtype)

    rows = lambda n: pl.BlockSpec((tm, n), lambda i: (i, 0))
    fixed = _fixed_spec
    return _call("ffn_hidden", body, grid=(T // tm,),
                 in_specs=[rows(D), fixed(w_out), rows(D), fixed(gain), fixed(w_gate_t), fixed(w_up_t)],
                 out_specs=[rows(D), rows(D), rows(F), rows(F), rows(F)],
                 out_shape=[jax.ShapeDtypeStruct((T, D), F32), jax.ShapeDtypeStruct((T, D), MXU_DTYPE)]
                 + [jax.ShapeDtypeStruct((T, F), SAVED_DTYPE)] * 2 + [jax.ShapeDtypeStruct((T, F), MXU_DTYPE)],
                 args=[merged, w_out, x, gain, w_gate_t, w_up_t], semantics=("parallel",), comm=comm)


def _in_proj(x, gain, w_in_t, b_in, *, tm, comm=None):
    T, D = x.shape
    bounds = [sum(IN_SPLITS[:i]) for i in range(len(IN_SPLITS) + 1)]

    def body(x_ref, g_ref, w_ref, b_ref, u_ref, *piece_refs):
        xv = x_ref[...]
        r = lax.rsqrt(jnp.mean(xv * xv, axis=-1, keepdims=True) + EPS)
        u = (xv * r * g_ref[...]).astype(u_ref.dtype)
        u_ref[...] = u
        for o_ref, lo, hi in zip(piece_refs, bounds[:-1], bounds[1:]):
            o_ref[...] = (_nt(u, w_ref[lo:hi, :]) + b_ref[:, lo:hi]).astype(o_ref.dtype)

    rows = lambda n: pl.BlockSpec((tm, n), lambda i: (i, 0))
    fixed = _fixed_spec
    dtypes = (MXU_DTYPE, MXU_DTYPE, F32, F32)
    return _call("in_proj", body, grid=(T // tm,),
                 in_specs=[rows(D), fixed(gain), fixed(w_in_t), fixed(b_in)],
                 out_specs=[rows(D)] + [rows(n) for n in IN_SPLITS],
                 out_shape=[jax.ShapeDtypeStruct((T, D), MXU_DTYPE)]
                 + [jax.ShapeDtypeStruct((T, n), dt) for n, dt in zip(IN_SPLITS, dtypes)],
                 args=[x, gain, w_in_t, b_in], semantics=("parallel",), comm=comm)


def _row_spec(tm, n):
    return pl.BlockSpec((tm, n), lambda i: (i, 0))


def _fixed_spec(a):
    return pl.BlockSpec(a.shape, lambda i: (0,) * a.ndim, pipeline_mode=pl.Buffered(1))


def _partials_spec(n):
    return pl.BlockSpec((8, n), lambda i: (i, 0))


def _row_parts(tm, parts):
    assert tm % parts == 0, (tm, parts)
    return [slice(p * (tm // parts), (p + 1) * (tm // parts)) for p in range(parts)]


def _ffn_tail(z, w_down, h1, target, gain, gate, up, *, tm, parts=2):
    (T, F), D = z.shape, h1.shape[1]

    def body(z_ref, w_ref, h_ref, t_ref, g_ref, gate_ref, up_ref, dh_ref, dhb_ref, dgate_ref, dup_ref, dg_ref, l_ref):
        part, dgain = 0.0, 0.0
        pieces = _row_parts(tm, parts)
        h2s = [h_ref[rows, :] + _nn(z_ref[rows, :], w_ref[...]) for rows in pieces]
        for rows, h2 in zip(pieces, h2s):
            r = lax.rsqrt(jnp.mean(h2 * h2, axis=-1, keepdims=True) + EPS)
            xhat = h2 * r
            err = xhat * g_ref[...] - t_ref[rows, :]
            part += 0.5 * jnp.sum(jnp.sum(err * err, axis=-1, keepdims=True), axis=0, keepdims=True) / D
            dy = err / D
            dxh = dy * g_ref[...]
            dh2 = r * (dxh - xhat * jnp.mean(dxh * xhat, axis=-1, keepdims=True))
            dh_ref[rows, :] = dh2
            dhb = dh2.astype(dhb_ref.dtype)
            dhb_ref[rows, :] = dhb
            dgain += jnp.sum(dy * xhat, axis=0, keepdims=True)
            dz = _nt(dhb, w_ref[...])
            gv, upv = gate_ref[rows, :].astype(F32), up_ref[rows, :].astype(F32)
            s = _sigmoid(gv)
            dgate_ref[rows, :] = (dz * upv * (s * (1.0 + gv * (1.0 - s)))).astype(dgate_ref.dtype)
            dup_ref[rows, :] = (dz * (gv * s)).astype(dup_ref.dtype)
        dg_ref[...] = jnp.broadcast_to(dgain, dg_ref.shape)
        l_ref[...] = jnp.broadcast_to(part, l_ref.shape)

    low = lambda n: jax.ShapeDtypeStruct((T, n), MXU_DTYPE)
    part = jax.ShapeDtypeStruct((8 * (T // tm), D), F32)
    return pl.pallas_call(
        body, name="ffn_tail", grid=(T // tm,),
        in_specs=[_row_spec(tm, F), _fixed_spec(w_down), _row_spec(tm, D), _row_spec(tm, D), _fixed_spec(gain),
                  _row_spec(tm, F), _row_spec(tm, F)],
        out_specs=[_row_spec(tm, D), _row_spec(tm, D), _row_spec(tm, F), _row_spec(tm, F), _partials_spec(D),
                   _partials_spec(D)],
        out_shape=[jax.ShapeDtypeStruct((T, D), F32), low(D), low(F), low(F), part, part],
        compiler_params=_params(("parallel",)),
    )(z, w_down, h1, target, gain, gate, up)


def _ffn_in_bwd(dgate, dup, w_gate_t, w_up_t, h1, gain, dres, *, tm, comm=None):
    (T, F), D = dgate.shape, h1.shape[1]

    def body(dg_ref, du_ref, wg_ref, wu_ref, h_ref, g_ref, r_ref, dh_ref, dhb_ref, dgain_ref):
        d_u2 = _nn(dg_ref[...], wg_ref[...]) + _nn(du_ref[...], wu_ref[...])
        dx, dgain = _rmsnorm_bwd_vals(d_u2, h_ref[...], g_ref[...])
        dh = r_ref[...] + dx
        dh_ref[...] = dh
        dhb_ref[...] = dh.astype(dhb_ref.dtype)
        dgain_ref[...] = jnp.broadcast_to(dgain, dgain_ref.shape)

    return _call("d_ffn_in", body, grid=(T // tm,),
                 in_specs=[_row_spec(tm, F), _row_spec(tm, F), _fixed_spec(w_gate_t), _fixed_spec(w_up_t),
                           _row_spec(tm, D), _fixed_spec(gain), _row_spec(tm, D)],
                 out_specs=[_row_spec(tm, D), _row_spec(tm, D), _partials_spec(D)],
                 out_shape=[jax.ShapeDtypeStruct((T, D), F32), jax.ShapeDtypeStruct((T, D), MXU_DTYPE),
                            jax.ShapeDtypeStruct((8 * (T // tm), D), F32)],
                 args=[dgate, dup, w_gate_t, w_up_t, h1, gain, dres], semantics=("parallel",), comm=comm)


def _in_proj_bwd(pieces, w_in_t, x, gain, dres, *, tm, comm=None):
    T, D = x.shape
    n = len(pieces)

    def body(*refs):
        dps, (w_ref, x_ref, g_ref, r_ref, dx_ref, dgain_ref) = refs[:n], refs[n:]
        d_u = None
        for dp_ref, (dp, first) in zip(dps, pieces):
            term = _nn(dp_ref[...], w_ref[first:first + dp.shape[1], :])
            d_u = term if d_u is None else d_u + term
        dx, dgain = _rmsnorm_bwd_vals(d_u, x_ref[...], g_ref[...])
        dx_ref[...] = r_ref[...] + dx
        dgain_ref[...] = jnp.broadcast_to(dgain, dgain_ref.shape)

    return _call("d_u", body, grid=(T // tm,),
                 in_specs=[_row_spec(tm, dp.shape[1]) for dp, _ in pieces]
                 + [_fixed_spec(w_in_t), _row_spec(tm, D), _fixed_spec(gain), _row_spec(tm, D)],
                 out_specs=[_row_spec(tm, D), _partials_spec(D)],
                 out_shape=[jax.ShapeDtypeStruct((T, D), F32), jax.ShapeDtypeStruct((8 * (T // tm), D), F32)],
                 args=[dp for dp, _ in pieces] + [w_in_t, x, gain, dres], semantics=("parallel",), comm=comm)


def _merge_fwd(y_a, y_b, w_a, w_b, gates, *, tm, comm=None):
    T, D = y_a.shape

    def body(ya_ref, yb_ref, wa_ref, wb_ref, ga_ref, gb_ref, pa_ref, pb_ref, m_ref):
        pa, pb = _nn(ya_ref[...], wa_ref[...]), _nn(yb_ref[...], wb_ref[...])
        pa_ref[...], pb_ref[...] = pa.astype(pa_ref.dtype), pb.astype(pb_ref.dtype)
        m_ref[...] = (_sigmoid(ga_ref[...]) * pa + _sigmoid(gb_ref[...]) * pb).astype(m_ref.dtype)

    rows = pl.BlockSpec((tm, D), lambda i: (i, 0))
    whole = pl.BlockSpec((D, D), lambda i: (0, 0), pipeline_mode=pl.Buffered(1))
    return _call("branch_merge", body, grid=(T // tm,),
                 in_specs=[rows, rows, whole, whole, rows, pl.BlockSpec((tm, D), lambda i: (i, 1))],
                 out_specs=[rows] * 3,
                 out_shape=[jax.ShapeDtypeStruct((T, D), SAVED_DTYPE)] * 2 + [jax.ShapeDtypeStruct((T, D), MXU_DTYPE)],
                 args=[y_a, y_b, w_a, w_b, gates, gates], semantics=("parallel",), comm=comm)


def _merge_bwd(dh, w_out, w_a, w_b, p_a, p_b, gates, *, tm, d_in_width, first):
    T, D = dh.shape

    def body(dh_ref, wo_ref, wa_ref, wb_ref, pa_ref, pb_ref, ga_ref, gb_ref, dpa_ref, dpb_ref, dya_ref, dyb_ref,
             din_ref):
        dm = _nt(dh_ref[...], wo_ref[...])
        sa, sb = _sigmoid(ga_ref[...]), _sigmoid(gb_ref[...])
        dpa, dpb = (dm * sa).astype(dpa_ref.dtype), (dm * sb).astype(dpb_ref.dtype)
        dpa_ref[...], dpb_ref[...] = dpa, dpb
        din_ref[:, :D] = (dm * pa_ref[...].astype(F32) * sa * (1.0 - sa)).astype(din_ref.dtype)
        din_ref[:, D:] = (dm * pb_ref[...].astype(F32) * sb * (1.0 - sb)).astype(din_ref.dtype)
        dya_ref[...] = _nt(dpa, wa_ref[...]).astype(dya_ref.dtype)
        dyb_ref[...] = _nt(dpb, wb_ref[...]).astype(dyb_ref.dtype)

    rows = pl.BlockSpec((tm, D), lambda i: (i, 0))
    whole = pl.BlockSpec((D, D), lambda i: (0, 0), pipeline_mode=pl.Buffered(1))
    low = jax.ShapeDtypeStruct((T, D), MXU_DTYPE)
    return pl.pallas_call(
        body, name="d_branch_merge", grid=(T // tm,),
        in_specs=[rows, whole, whole, whole, rows, rows, rows, pl.BlockSpec((tm, D), lambda i: (i, 1))],
        out_specs=[rows] * 4 + [pl.BlockSpec((pl.Element(tm), pl.Element(2 * D)), lambda i: (
            pl.multiple_of(i * tm, ROW_ALIGN), first))],
        out_shape=[low] * 3 + [jax.ShapeDtypeStruct((T, D), F32), jax.ShapeDtypeStruct((T, d_in_width), MXU_DTYPE)],
        compiler_params=_params(("parallel",)),
    )(dh, w_out, w_a, w_b, p_a, p_b, gates, gates)


def _colsum_partials(p):
    return jnp.sum(p.reshape(-1, 8, p.shape[-1])[:, 0, :], axis=0, keepdims=True)


def _rmsnorm_bwd_vals(dy, xin, g):
    rstd = lax.rsqrt(jnp.mean(xin * xin, axis=-1, keepdims=True) + EPS)
    xhat = xin * rstd
    dg = jnp.sum(dy * xhat, axis=0, keepdims=True)
    dxh = dy * g
    dx = rstd * (dxh - xhat * jnp.mean(dxh * xhat, axis=-1, keepdims=True))
    return dx, dg


ATTN_SCALE = 1.0 / math.sqrt(HEAD_DIM)
GROUP_LANES = GROUP * ATTN_BLOCK
PAIR = 2 * HEAD_DIM


def _attn_mask():
    kj = lax.broadcasted_iota(jnp.int32, (ATTN_BLOCK, GROUP_LANES), 0)
    qi = lax.broadcasted_iota(jnp.int32, (ATTN_BLOCK, GROUP_LANES), 1) & (ATTN_BLOCK - 1)
    return kj <= qi


def _heads_transposed(ref, g, scale=None):
    parts = []
    for a in range(GROUP // 2):
        lo = (g * GROUP // 2 + a) * PAIR
        pair = ref[:, lo:lo + PAIR].astype(F32)
        pair = (pair if scale is None else pair * scale).T
        parts += [pair[:HEAD_DIM], pair[HEAD_DIM:]]
    return jnp.concatenate(parts, axis=1).astype(MXU_DTYPE)


def _heads_back(ref, g, vt):
    for a in range(GROUP // 2):
        lo = (g * GROUP // 2 + a) * PAIR
        pair = jnp.concatenate([vt[:, (2 * a) * ATTN_BLOCK:(2 * a + 1) * ATTN_BLOCK],
                                vt[:, (2 * a + 1) * ATTN_BLOCK:(2 * a + 2) * ATTN_BLOCK]], axis=0)
        ref[:, lo:lo + PAIR] = pair.T.astype(ref.dtype)


def _kv_parts(kv_ref, g):
    ks = slice(g * HEAD_DIM, (g + 1) * HEAD_DIM)
    vs = slice(KV_WIDTH + g * HEAD_DIM, KV_WIDTH + (g + 1) * HEAD_DIM)
    return kv_ref[:, ks].astype(MXU_DTYPE), kv_ref[:, vs].astype(MXU_DTYPE)


def _sink_rows(sinks):
    return jnp.repeat(sinks.reshape(KV_HEADS, GROUP), ATTN_BLOCK, axis=1)


def _attn_fwd(pq, pkv, sinks, comm=None):
    T = pq.shape[0]
    nb = T // ATTN_BLOCK

    def body(q_ref, kvc_ref, kvp_ref, s_ref, y_ref, lse_ref):
        mask_c = _attn_mask()
        has_prev = pl.program_id(0) > 0
        early = []
        for g in range(KV_HEADS):
            (kc, vc), (kp, vp) = _kv_parts(kvc_ref, g), _kv_parts(kvp_ref, g)
            qt = _heads_transposed(q_ref, g, ATTN_SCALE)
            early.append((vc, vp, jnp.where(mask_c, _nn(kc, qt), jnp.where(has_prev, _nn(kp, qt), NEG_INF))))
        for g, (vc, vp, s) in enumerate(early):
            sink = s_ref[g:g + 1, :]
            m = jnp.maximum(jnp.max(s, axis=0, keepdims=True), sink)
            p = jnp.exp(s - m)
            den = jnp.sum(p, axis=0, keepdims=True) + jnp.exp(sink - m)
            pc = jnp.where(mask_c, p, 0.0)
            _heads_back(y_ref, g, (_tn(vc, pc) + _tn(vp, p - pc)) / den)
            lse = m + jnp.log(den)
            for i in range(GROUP):
                lse_ref[g * GROUP + i:g * GROUP + i + 1, :] = lse[:, i * ATTN_BLOCK:(i + 1) * ATTN_BLOCK]

    return _call(
        "attn_fwd", body, grid=(nb,),
        in_specs=[pl.BlockSpec((ATTN_BLOCK, D_MODEL), lambda n: (n, 0)),
                  pl.BlockSpec((ATTN_BLOCK, 2 * KV_WIDTH), lambda n: (n, 0)),
                  pl.BlockSpec((ATTN_BLOCK, 2 * KV_WIDTH), lambda n: (jnp.maximum(n - 1, 0), 0)),
                  pl.BlockSpec((KV_HEADS, GROUP_LANES), lambda n: (0, 0))],
        out_specs=[pl.BlockSpec((ATTN_BLOCK, D_MODEL), lambda n: (n, 0)),
                   pl.BlockSpec((Q_HEADS, ATTN_BLOCK), lambda n: (0, n))],
        out_shape=[jax.ShapeDtypeStruct((T, D_MODEL), MXU_DTYPE), jax.ShapeDtypeStruct((Q_HEADS, T), F32)],
        args=[pq, pkv, pkv, _sink_rows(sinks)], semantics=("parallel",), comm=comm)


def _attn_bwd(pq, pkv, sinks, lse, dy, d_in, comm=None):
    T = pq.shape[0]
    nb = T // ATTN_BLOCK
    cur = lambda n: (jnp.minimum(n, nb - 1), 0)
    done = D_MODEL + 2 * KV_WIDTH

    def body(q_ref, kvc_ref, kvp_ref, s_ref, lse_ref, dy_ref, _, out_ref, ds_ref, carry, top, bot, dq_ref):
        n = pl.program_id(0)

        @pl.when(n == 0)
        def _():
            carry[...] = jnp.zeros_like(carry)
            dq_ref[...] = jnp.zeros_like(dq_ref)
            ds_ref[...] = jnp.zeros_like(ds_ref)

        out_ref[:, :D_MODEL] = dq_ref[...]

        @pl.when(n < nb)
        def _():
            mask_c = _attn_mask()
            valid = jnp.logical_or(mask_c, n > 0)
            early = []
            for g in range(KV_HEADS):
                (kc, vc), (kp, vp) = _kv_parts(kvc_ref, g), _kv_parts(kvp_ref, g)
                qt = _heads_transposed(q_ref, g, ATTN_SCALE)
                dot = _heads_transposed(dy_ref, g)
                early.append((kc, kp, qt, dot, jnp.where(mask_c, _nn(kc, qt), _nn(kp, qt)),
                              jnp.where(mask_c, _nn(vc, dot), _nn(vp, dot))))
            for g, (kc, kp, qt, dot, s, dp) in enumerate(early):
                ks = slice(g * HEAD_DIM, (g + 1) * HEAD_DIM)
                vs = slice(KV_WIDTH + g * HEAD_DIM, KV_WIDTH + (g + 1) * HEAD_DIM)
                lse = jnp.concatenate([lse_ref[g * GROUP + i:g * GROUP + i + 1, :] for i in range(GROUP)], axis=1)
                p = jnp.where(valid, jnp.exp(s - lse), 0.0)
                delta = jnp.sum(p * dp, axis=0, keepdims=True)
                ds = p * (dp - delta)
                ds_c, p_c = jnp.where(mask_c, ds, 0.0), jnp.where(mask_c, p, 0.0)
                ds_p, p_p = ds - ds_c, p - p_c
                _heads_back(dq_ref, g, (_tn(kc, ds_c) + _tn(kp, ds_p)) * ATTN_SCALE)
                bot[:, ks], bot[:, vs] = _nt(ds_c, qt), _nt(p_c, dot)
                top[:, ks], top[:, vs] = _nt(ds_p, qt), _nt(p_p, dot)
                ds_ref[g:g + 1, :] -= jnp.exp(s_ref[g:g + 1, :] - lse) * delta
            out_ref[:, D_MODEL:] = (carry[...] + top[...]).astype(out_ref.dtype)
            carry[...] = bot[...]

        @pl.when(n == nb)
        def _():
            out_ref[:, D_MODEL:] = carry[...].astype(out_ref.dtype)

    return _call(
        "attn_bwd", body, grid=(nb + 1,),
        in_specs=[pl.BlockSpec((ATTN_BLOCK, D_MODEL), cur),
                  pl.BlockSpec((ATTN_BLOCK, 2 * KV_WIDTH), cur),
                  pl.BlockSpec((ATTN_BLOCK, 2 * KV_WIDTH), lambda n: (jnp.maximum(jnp.minimum(n, nb - 1) - 1, 0), 0)),
                  pl.BlockSpec((KV_HEADS, GROUP_LANES), lambda n: (0, 0)),
                  pl.BlockSpec((Q_HEADS, ATTN_BLOCK), lambda n: (0, jnp.minimum(n, nb - 1))),
                  pl.BlockSpec((ATTN_BLOCK, D_MODEL), cur), HBM_SPEC],
        out_specs=[pl.BlockSpec((ATTN_BLOCK, done), lambda n: (jnp.maximum(n - 1, 0), 0)),
                   pl.BlockSpec((KV_HEADS, GROUP_LANES), lambda n: (0, 0))],
        out_shape=[jax.ShapeDtypeStruct(d_in.shape, d_in.dtype), jax.ShapeDtypeStruct((KV_HEADS, GROUP_LANES), F32)],
        scratch=[pltpu.VMEM((ATTN_BLOCK, 2 * KV_WIDTH), F32)] * 3 + [pltpu.VMEM((ATTN_BLOCK, D_MODEL), MXU_DTYPE)],
        args=[pq, pkv, pkv, _sink_rows(sinks), lse, dy, d_in], semantics=("arbitrary",), comm=comm, aliases={6: 0})


def _lower_bound(l):
    m = jnp.maximum(l[0:1], l[1:2])
    e0, e1 = jnp.exp(l[0:1] - m), jnp.exp(l[1:2] - m)
    return e0 / (e0 + e1)


def _tri(lower):
    r = lax.broadcasted_iota(jnp.int32, (CHUNK, CHUNK), 0)
    c = lax.broadcasted_iota(jnp.int32, (CHUNK, CHUNK), 1)
    return (r >= c) if lower else (c >= r)


def _chunk_sum(mask, v):
    ones = mask.astype(BF16)
    hi = v.astype(BF16)
    rest = v - hi.astype(F32)
    mid = rest.astype(BF16)
    lo = (rest - mid.astype(F32)).astype(BF16)
    part = lambda t: lax.dot_general(ones, t, (((1,), (0,)), ((), ())), preferred_element_type=F32)
    return part(hi) + part(mid) + part(lo)


def _hgrn_chunk_inputs(hq, hf, lb, causal):
    half_t = 0.5 * jnp.tanh(0.5 * hf)
    sg, sgn = 0.5 + half_t, 0.5 - half_t
    f = lb + (1.0 - lb) * sg
    kk = (1.0 - lb) * sgn
    sq = _sigmoid(hq)
    q = hq * sq
    b = _chunk_sum(causal, jnp.log(f))
    bm, bl = b[CHUNK // 2 - 1:CHUNK // 2, :], b[CHUNK - 1:CHUNK, :]
    e_qm, e_km = jnp.exp(b - bm), jnp.exp(bm - b)
    e_qs, e_kl = e_qm * jnp.exp(bm), e_km * jnp.exp(bl - bm)
    return dict(sg=sg, sgn=sgn, f=f, kk=kk, sq=sq, q=q, e_qm=e_qm, e_km=e_km, e_qs=e_qs, e_kl=e_kl,
                qm=q * e_qm, km=kk * e_km, qs=q * e_qs, kl=kk * e_kl, el=jnp.exp(bl))


def _hgrn_fwd(ph, lb_logits, norm_g, comm=None):
    T = ph.shape[0]
    nblk, cpb = T // HGRN_TOKENS, HGRN_TOKENS // CHUNK
    col = lambda c: pl.BlockSpec((HGRN_TOKENS, D_MODEL), functools.partial(lambda i, c: (i, c), c=c))

    def body(hq_ref, hf_ref, hi_ref, hg_ref, l_ref, ng_ref, y_ref, o_ref, st_ref, s_ref):
        @pl.when(pl.program_id(0) == 0)
        def _():
            s_ref[...] = jnp.zeros_like(s_ref)

        lb = _lower_bound(l_ref[...])
        causal = _tri(True)
        for c in range(cpb):
            rows = slice(c * CHUNK, (c + 1) * CHUNK)
            t = _hgrn_chunk_inputs(hq_ref[rows, :], hf_ref[rows, :], lb, causal)
            qm, km, qs, kl = (t[n].astype(MXU_DTYPE) for n in ("qm", "km", "qs", "kl"))
            v = hi_ref[rows, :].astype(MXU_DTYPE)
            heads = [slice(h * HGRN_K, (h + 1) * HGRN_K) for h in range(HGRN_HEADS)]
            a_all = [jnp.where(causal, _nt(qm[:, ls], km[:, ls]), 0.0).astype(MXU_DTYPE) for ls in heads]
            for h, ls in enumerate(heads):
                st = s_ref[h]
                st_ref[c, ls, :] = st
                o_ref[rows, ls] = _nn(a_all[h], v[:, ls]) + _nt(qs[:, ls], st)
                s_ref[h] = t["el"][:, ls] * st + _tn(v[:, ls], kl[:, ls])
        for h in range(HGRN_HEADS):
            ls = slice(h * HGRN_K, (h + 1) * HGRN_K)
            o = o_ref[:, ls]
            r = lax.rsqrt(jnp.mean(o * o, axis=-1, keepdims=True) + EPS)
            y_ref[:, ls] = (o * r * ng_ref[:, ls] * _sigmoid(hg_ref[:, ls])).astype(y_ref.dtype)

    return _call(
        "hgrn_fwd", body, grid=(nblk,),
        in_specs=[col(0), col(1), col(2), col(3),
                  pl.BlockSpec((2, D_MODEL), lambda i: (0, 0)), pl.BlockSpec((1, D_MODEL), lambda i: (0, 0))],
        out_specs=[pl.BlockSpec((HGRN_TOKENS, D_MODEL), lambda i: (i, 0)),
                   pl.BlockSpec((HGRN_TOKENS, D_MODEL), lambda i: (i, 0)),
                   pl.BlockSpec((cpb, D_MODEL, HGRN_K), lambda i: (i, 0, 0))],
        out_shape=[jax.ShapeDtypeStruct((T, D_MODEL), MXU_DTYPE), jax.ShapeDtypeStruct((T, D_MODEL), F32),
                   jax.ShapeDtypeStruct((T // CHUNK, D_MODEL, HGRN_K), F32)],
        scratch=[pltpu.VMEM((HGRN_HEADS, HGRN_K, HGRN_K), F32)],
        args=[ph, ph, ph, ph, lb_logits, norm_g], semantics=("arbitrary",), comm=comm)


def _hgrn_bwd(ph, o_raw, states, dy, lb_logits, norm_g, d_in, first, comm=None):
    T = ph.shape[0]
    nblk, cpb = T // HGRN_TOKENS, HGRN_TOKENS // CHUNK
    rev = lambda i: nblk - 1 - i
    col = lambda c: pl.BlockSpec((HGRN_TOKENS, D_MODEL), functools.partial(lambda i, c: (rev(i), c), c=c))
    tok = pl.BlockSpec((HGRN_TOKENS, D_MODEL), lambda i: (rev(i), 0))

    def body(hq_ref, hf_ref, hi_ref, hg_ref, o_ref, st_ref, dy_ref, l_ref, ng_ref, _,
             dph_ref, dng_ref, dl_ref, dst_ref, dlb_ref, do_s, dqm_s, dkm_s, dqs_s, dkl_s, dv_s, del_s):
        i = pl.program_id(0)

        @pl.when(i == 0)
        def _():
            dst_ref[...] = jnp.zeros_like(dst_ref)
            dlb_ref[...] = jnp.zeros_like(dlb_ref)
            dng_ref[...] = jnp.zeros_like(dng_ref)

        lb = _lower_bound(l_ref[...])
        causal, anti = _tri(True), _tri(False)
        row = lax.broadcasted_iota(jnp.int32, (CHUNK, D_MODEL), 0)
        for c in reversed(range(cpb)):
            rows = slice(c * CHUNK, (c + 1) * CHUNK)
            hq = hq_ref[rows, :]
            t = _hgrn_chunk_inputs(hq, hf_ref[rows, :], lb, causal)
            sgg = _sigmoid(hg_ref[rows, :])
            dyv = dy_ref[rows, :]
            for h in range(HGRN_HEADS):
                ls = slice(h * HGRN_K, (h + 1) * HGRN_K)
                o = o_ref[rows, ls]
                r = lax.rsqrt(jnp.mean(o * o, axis=-1, keepdims=True) + EPS)
                nrm = o * r
                g_h = sgg[:, ls]
                dph_ref[rows, 3 * D_MODEL + h * HGRN_K:3 * D_MODEL + (h + 1) * HGRN_K] = (
                    dyv[:, ls] * nrm * ng_ref[:, ls] * g_h * (1.0 - g_h)).astype(dph_ref.dtype)
                dyg = dyv[:, ls] * g_h
                dng_ref[:, ls] += jnp.sum(dyg * nrm, axis=0, keepdims=True)
                dn = dyg * ng_ref[:, ls]
                do_s[:, ls] = r * (dn - nrm * jnp.mean(dn * nrm, axis=-1, keepdims=True))
            qm, km, qs, kl = (t[n].astype(MXU_DTYPE) for n in ("qm", "km", "qs", "kl"))
            v = hi_ref[rows, :].astype(MXU_DTYPE)
            do = do_s[...].astype(MXU_DTYPE)
            heads = [slice(h * HGRN_K, (h + 1) * HGRN_K) for h in range(HGRN_HEADS)]
            a_all = [jnp.where(causal, _nt(qm[:, ls], km[:, ls]), 0.0).astype(MXU_DTYPE) for ls in heads]
            da_all = [jnp.where(causal, _nt(do[:, ls], v[:, ls]), 0.0).astype(MXU_DTYPE) for ls in heads]
            for h, ls in enumerate(heads):
                st = st_ref[c, ls, :]
                dst = dst_ref[h]
                a, da = a_all[h], da_all[h]
                dv_s[:, ls] = _tn(a, do[:, ls]) + _nt(kl[:, ls], dst)
                dkl_s[:, ls] = _nn(v[:, ls], dst)
                dqs_s[:, ls] = _nn(do[:, ls], st)
                del_s[:, ls] = jnp.sum(dst * st, axis=0, keepdims=True)
                dst_ref[h] = _tn(do[:, ls], qs[:, ls]) + t["el"][:, ls] * dst
                dqm_s[:, ls] = _nn(da, km[:, ls])
                dkm_s[:, ls] = _tn(da, qm[:, ls])
            dqm, dkm, dqs, dkl = dqm_s[...], dkm_s[...], dqs_s[...], dkl_s[...]
            dq = dqm * t["e_qm"] + dqs * t["e_qs"]
            dk = dkm * t["e_km"] + dkl * t["e_kl"]
            t_qm, t_km, t_kl = dqm * t["qm"], dkm * t["km"], dkl * t["kl"]
            db = t_qm - t_km + dqs * t["qs"] - t_kl
            db_mid = jnp.sum(t_km - t_qm, axis=0, keepdims=True)
            db_last = jnp.sum(t_kl, axis=0, keepdims=True) + del_s[...] * t["el"]
            db = db + jnp.where(row == CHUNK // 2 - 1, db_mid, 0.0) + jnp.where(row == CHUNK - 1, db_last, 0.0)
            dlogf = _chunk_sum(anti, db)
            sq, sg, sgn, f = t["sq"], t["sg"], t["sgn"], t["f"]
            dph_ref[rows, 0:D_MODEL] = (dq * (sq * (1.0 + hq * (1.0 - sq)))).astype(dph_ref.dtype)
            dph_ref[rows, D_MODEL:2 * D_MODEL] = (
                dlogf * (1.0 - lb) * sg * (1.0 - sg) / f - dk * (1.0 - lb) * sgn * (1.0 - sgn)).astype(dph_ref.dtype)
            dph_ref[rows, 2 * D_MODEL:3 * D_MODEL] = dv_s[...].astype(dph_ref.dtype)
            dlb_ref[...] += jnp.sum(dlogf * (1.0 - sg) / f - dk * sgn, axis=0, keepdims=True)

        @pl.when(i == nblk - 1)
        def _():
            dl0 = dlb_ref[...] * lb * (1.0 - lb)
            dl_ref[0:1, :] = dl0
            dl_ref[1:2, :] = -dl0

    wide = pltpu.VMEM((CHUNK, D_MODEL), F32)
    return _call(
        "hgrn_bwd", body, grid=(nblk,),
        in_specs=[col(0), col(1), col(2), col(3), tok,
                  pl.BlockSpec((cpb, D_MODEL, HGRN_K), lambda i: (rev(i), 0, 0)), tok,
                  pl.BlockSpec((2, D_MODEL), lambda i: (0, 0)), pl.BlockSpec((1, D_MODEL), lambda i: (0, 0)), HBM_SPEC],
        out_specs=[pl.BlockSpec((pl.Element(HGRN_TOKENS), pl.Element(4 * D_MODEL)), lambda i: (
                       pl.multiple_of(rev(i) * HGRN_TOKENS, ROW_ALIGN), first)),
                   pl.BlockSpec((1, D_MODEL), lambda i: (0, 0)), pl.BlockSpec((2, D_MODEL), lambda i: (0, 0))],
        out_shape=[jax.ShapeDtypeStruct(d_in.shape, d_in.dtype), jax.ShapeDtypeStruct((1, D_MODEL), F32),
                   jax.ShapeDtypeStruct((2, D_MODEL), F32)],
        scratch=[pltpu.VMEM((HGRN_HEADS, HGRN_K, HGRN_K), F32), pltpu.VMEM((1, D_MODEL), F32),
                 wide, wide, wide, wide, wide, wide, pltpu.VMEM((1, D_MODEL), F32)],
        args=[ph, ph, ph, ph, o_raw, states, dy, lb_logits, norm_g, d_in], semantics=("arbitrary",), comm=comm,
        aliases={9: 0})


def _local_step(x, target, vec, net):
    T, D = x.shape
    norm_mix_g, b_in, sinks, lb_logits = vec["norm_mix_g"], vec["b_in"], vec["attn_sinks"], vec["hgrn_lb_logits"]
    hgrn_norm_g, norm_ffn_g, norm_final_g = vec["hgrn_norm_g"], vec["norm_ffn_g"], vec["norm_final_g"]
    w_in = net.full("w_in")
    o_q, o_kv, o_h, o_g = (sum(IN_SPLITS[:i]) for i in range(4))
    TM = 512

    first = ("w_branch_attn", "w_branch_hgrn", "w_ffn_down")
    (u, pq, pkv, ph, pg), got = _in_proj(x, norm_mix_g, w_in, b_in, tm=256, comm=net.fetch(first))
    net.fetched(first, got)
    second = ("w_ffn_gate",)
    (y_attn, lse), got = _attn_fwd(pq, pkv, sinks, comm=_join(net.relay(first), net.fetch(second)))
    net.gathered(first, got[:len(first)])
    net.fetched(second, got[len(first):])
    third = ("w_ffn_up", "w_out")
    (y_hgrn, o_raw, states), got = _hgrn_fwd(ph, lb_logits, hgrn_norm_g,
                                             comm=_join(net.relay(second), net.fetch(third)))
    net.gathered(second, got[:len(second)])
    net.fetched(third, got[len(second):])
    w_ba, w_bh = net.full("w_branch_attn"), net.full("w_branch_hgrn")
    (ya, yb, merged), got = _merge_fwd(y_attn, y_hgrn, w_ba, w_bh, pg, tm=TM, comm=net.relay(third))
    net.gathered(third, got)
    w_gate, w_up, w_out = net.full("w_ffn_gate"), net.full("w_ffn_up"), net.full("w_out")

    (h1, u2, gpre, up, z), _ = _ffn_fwd(merged, w_out, x, norm_ffn_g, w_gate, w_up, tm=256)
    w_down = net.full("w_ffn_down")

    dh2, dh2b, dgp, dup, dgf_p, loss_p = _ffn_tail(z, w_down, h1, target, norm_final_g, gpre, up, tm=512)
    loss = jnp.sum(loss_p.reshape(-1, 8, D)[:, 0, 0])
    d_norm_final = _colsum_partials(dgf_p)
    d_w_down = _weight_grad("dw_down", z, dh2b, tm=DW_ROWS, tk=T)

    names, swap = ("w_ffn_down",), ()
    (dh1, dh1b, dg2_p), got = _ffn_in_bwd(dgp, dup, w_gate, w_up, h1, norm_ffn_g, dh2, tm=256,
                                          comm=net.exchange(dict(w_ffn_down=[d_w_down]), swap))
    net.received(names, swap, got)
    d_norm_ffn = _colsum_partials(dg2_p)
    d_w_gate = _weight_grad("dw_gate", dgp, u2, tm=DW_ROWS, tk=T)
    d_w_up = _weight_grad("dw_up", dup, u2, tm=DW_ROWS, tk=T)

    rows_in = sum(IN_SPLITS)
    dya, dyb, dy_attn, dy_hgrn, d_in = _merge_bwd(dh1b, w_out, w_ba, w_bh, ya, yb, pg, tm=TM, d_in_width=rows_in,
                                                  first=o_g)
    d_w_out = _weight_grad("dw_out", merged, dh1b, tm=1024, tk=1024)
    d_w_ba = _weight_grad("dw_branch_a", y_attn, dya, tm=1024, tk=1024)
    d_w_bh = _weight_grad("dw_branch_b", y_hgrn, dyb, tm=1024, tk=1024)

    names, swap = ("w_ffn_gate",), ("w_ffn_down",)
    (d_in, dsink), got = _attn_bwd(pq, pkv, sinks, lse, dy_attn, d_in,
                                   comm=net.exchange(dict(w_ffn_gate=[d_w_gate]), swap))
    net.received(names, swap, got)
    names, swap = ("w_ffn_up", "w_out"), ("w_ffn_gate",)
    (d_in, d_hgrn_norm, d_lb_logits), got = _hgrn_bwd(
        ph, o_raw, states, dy_hgrn, lb_logits, hgrn_norm_g, d_in, o_h,
        comm=net.exchange(dict(w_ffn_up=[d_w_up], w_out=[d_w_out]), swap))
    net.received(names, swap, got)

    names, swap = ("w_branch_attn", "w_branch_hgrn"), ("w_ffn_up", "w_out")
    (*d_w_in, d_b_in), got = _weight_grad(
        "dw_in", d_in, u, tm=DW_ROWS, tk=T, a_colsum=True, carrying=True,
        comm=net.exchange(dict(w_branch_attn=[d_w_ba], w_branch_hgrn=[d_w_bh]), swap))
    net.received(names, swap, got)
    d_w_in = [tuple(d_w_in)]

    first_level = net.presum_begin("w_in", d_w_in)
    halves = net.presum_end("w_in", [] if first_level is None else _copies_alone("presum_swap_w_in", first_level))
    names, swap = ("w_in",), ("w_branch_attn", "w_branch_hgrn")
    (dx, dg1_p), got = _in_proj_bwd([(d_in, 0)], w_in, x, norm_mix_g, dh1, tm=256,
                                    comm=_join(halves, net.swap(swap)))
    net.last = (names, swap, got)
    d_norm_mix = _colsum_partials(dg1_p)
    vecs = dict(norm_mix_g=d_norm_mix, b_in=d_b_in, attn_sinks=jnp.sum(dsink.reshape(Q_HEADS, ATTN_BLOCK), axis=1).reshape(1, Q_HEADS),
                hgrn_lb_logits=d_lb_logits,
                hgrn_norm_g=d_hgrn_norm, norm_ffn_g=d_norm_ffn, norm_final_g=d_norm_final)
    return loss, dx, vecs


def _place():
    return lax.axis_index("x"), lax.axis_index("y"), lax.axis_index("c")


def _other_chips(x, y):
    return [(1 - x, y), (x, 1 - y), (1 - x, 1 - y)]


def _y_first(copies):
    return [copies[3 * (i // 3) + (1, 0, 2)[i % 3]] for i in range(len(copies))]


def _fetch_copies(shards):
    n = len(shards)

    def build(ins, outs, send_sems, recv_sems, local_sems, *stages):
        x, y, c = _place()
        mine = 2 * x + y
        local = [(pltpu.make_async_copy(ins[w], stages[w], local_sems.at[2 * w]),
                  pltpu.make_async_copy(stages[w], outs[w].at[mine], local_sems.at[2 * w + 1])) for w in range(n)]
        sends, recvs = [], []
        for w in range(n):
            half = shards[w].shape[0] // 2
            rows = pl.ds(c * half, half)
            for k, (px, py) in enumerate(_other_chips(x, y)):
                sem = 3 * w + k
                sends.append(pltpu.make_async_remote_copy(
                    src_ref=ins[w].at[rows], dst_ref=outs[w].at[mine, rows], send_sem=send_sems.at[sem],
                    recv_sem=recv_sems.at[sem], device_id=(px, py, c), device_id_type=MESH_ID))
                recvs.append(pltpu.make_async_remote_copy(
                    src_ref=ins[w].at[rows], dst_ref=outs[w].at[2 * px + py, rows], send_sem=send_sems.at[sem],
                    recv_sem=recv_sems.at[sem], device_id=(px, py, c), device_id_type=MESH_ID))
        return sends, recvs, local, _y_first(sends)

    return _Carried(shards, [jax.ShapeDtypeStruct((N_CHIPS,) + s.shape, s.dtype) for s in shards], 3 * n, 2 * n, build,
                    stages=[pltpu.VMEM(s.shape, s.dtype) for s in shards])


def _relay_copies(fetched):
    n = len(fetched)

    def build(ins, outs, send_sems, recv_sems, local_sems):
        x, y, c = _place()
        sends, recvs = [], []
        for w in range(n):
            half = fetched[w].shape[1] // 2
            mine, theirs = pl.ds(c * half, half), pl.ds((1 - c) * half, half)
            for k, (px, py) in enumerate(_other_chips(x, y)):
                block, sem = 2 * px + py, 3 * w + k
                sends.append(pltpu.make_async_remote_copy(
                    src_ref=ins[w].at[block, mine], dst_ref=outs[w].at[block, mine], send_sem=send_sems.at[sem],
                    recv_sem=recv_sems.at[sem], device_id=(x, y, 1 - c), device_id_type=MESH_ID))
                recvs.append(pltpu.make_async_remote_copy(
                    src_ref=ins[w].at[block, theirs], dst_ref=outs[w].at[block, theirs], send_sem=send_sems.at[sem],
                    recv_sem=recv_sems.at[sem], device_id=(x, y, 1 - c), device_id_type=MESH_ID))
        return sends, recvs, []

    return _Carried(fetched, [jax.ShapeDtypeStruct(f.shape, f.dtype) for f in fetched], 3 * n, 0, build,
                    continued={w: w for w in range(n)})


def _grad_copies(stacked):
    n = len(stacked)

    def build(ins, outs, send_sems, recv_sems, local_sems):
        x, y, c = _place()
        sends = []
        for w in range(n):
            for k, (px, py) in enumerate(_other_chips(x, y)):
                sem = 3 * w + k
                sends.append(pltpu.make_async_remote_copy(
                    src_ref=ins[w].at[2 * px + py], dst_ref=outs[w].at[k], send_sem=send_sems.at[sem],
                    recv_sem=recv_sems.at[sem], device_id=(px, py, c), device_id_type=MESH_ID))
        return sends, sends, [], _y_first(sends)

    return _Carried(stacked, [jax.ShapeDtypeStruct((3,) + s.shape[1:], s.dtype) for s in stacked], 3 * n, 0, build)


def _small_copies(small):
    def build(ins, outs, send_sems, recv_sems, local_sems):
        small_ref, all_ref = ins[0], outs[0]
        x, y, c = _place()
        me = 4 * x + 2 * y + c
        sends, recvs = [], []
        for r in range(1, 8):
            px = 1 - x if r & 4 else x
            py = 1 - y if r & 2 else y
            pc = 1 - c if r & 1 else c
            sends.append(pltpu.make_async_remote_copy(
                src_ref=small_ref, dst_ref=all_ref.at[me], send_sem=send_sems.at[r - 1], recv_sem=recv_sems.at[r - 1],
                device_id=(px, py, pc), device_id_type=MESH_ID))
            recvs.append(pltpu.make_async_remote_copy(
                src_ref=small_ref, dst_ref=all_ref.at[4 * px + 2 * py + pc], send_sem=send_sems.at[r - 1],
                recv_sem=recv_sems.at[r - 1], device_id=(px, py, pc), device_id_type=MESH_ID))
        return sends, recvs, [pltpu.make_async_copy(small_ref, all_ref.at[me], local_sems.at[0])]

    return _Carried([small], [jax.ShapeDtypeStruct((8,) + small.shape, small.dtype)], 7, 1, build)


def _gather_by_neighbours(name, shard):
    half = shard.shape[0] // 2
    quarter = half // 2

    def body(in_ref, out_ref, send_sems, recv_sems, local_sem, stage_ref):
        for core in (0, 1):
            @pl.when(lax.axis_index("c") == core)
            def _():
                program(core, in_ref, out_ref, send_sems, recv_sems, local_sem, stage_ref)

    def program(c, in_ref, out_ref, send_sems, recv_sems, local_sem, stage_ref):
        x, y, _ = _place()
        chip = lambda px, py: 2 * px + py
        to_x, to_y, sibling = (1 - x, y, c), (x, 1 - y, c), (x, y, 1 - c)
        x_blk, y_blk, d_blk = chip(1 - x, y), chip(x, 1 - y), chip(1 - x, 1 - y)
        mine, theirs = c * half, (1 - c) * half

        def copy(sem, rows, block, to, src=None):
            place = out_ref.at[block, pl.ds(rows[0], rows[1])]
            return pltpu.make_async_remote_copy(
                src_ref=place if src is None else src, dst_ref=place, send_sem=send_sems.at[sem],
                recv_sem=recv_sems.at[sem], device_id=to, device_id_type=MESH_ID)

        stage = pltpu.make_async_copy(in_ref, stage_ref, local_sem.at[0])
        own = pltpu.make_async_copy(stage_ref, out_ref.at[chip(x, y)], local_sem.at[1])
        my_rows = in_ref.at[pl.ds(mine, half)]
        along_x = dict(send=copy(0, (mine, half), chip(x, y), to_x, src=my_rows),
                       landed=copy(0, (mine, half), x_blk, to_x),
                       onward=[copy(3, (mine + quarter, quarter), x_blk, to_y), copy(4, (mine, half), x_blk, sibling)],
                       diagonal=copy(2, (mine, quarter), d_blk, to_x))
        along_y = dict(send=copy(1, (mine, half), chip(x, y), to_y, src=my_rows),
                       landed=copy(1, (mine, half), y_blk, to_y),
                       onward=[copy(2, (mine, quarter), y_blk, to_x), copy(5, (mine, half), y_blk, sibling)],
                       diagonal=copy(3, (mine + quarter, quarter), d_blk, to_y))
        last = copy(6, (mine, half), d_blk, sibling)

        order = (along_x, along_y) if c == 0 else (along_y, along_x)
        for axis in order:
            axis["send"].start()
        stage.start()
        stage.wait()
        own.start()
        for axis in order:
            axis["landed"].wait_recv()
            for cp in axis["onward"]:
                cp.start()
        for axis in order:
            axis["diagonal"].wait_recv()
        last.start()
        for sem, block in ((4, x_blk), (5, y_blk), (6, d_blk)):
            copy(sem, (theirs, half), block, sibling).wait_recv()
        for cp in [along_x["send"], along_y["send"]] + along_x["onward"] + along_y["onward"] + [last]:
            cp.wait_send()
        own.wait()

    return pl.pallas_call(
        body, name=name, in_specs=[HBM_SPEC], out_specs=HBM_SPEC,
        out_shape=jax.ShapeDtypeStruct((N_CHIPS,) + shard.shape, shard.dtype),
        scratch_shapes=[pltpu.SemaphoreType.DMA((7,)), pltpu.SemaphoreType.DMA((7,)), pltpu.SemaphoreType.DMA((2,)),
                        pltpu.VMEM(shard.shape, shard.dtype)],
    )(shard)


def _copies_alone(name, comm):
    return _call(name, lambda: None, grid=(), in_specs=[], out_specs=[], out_shape=[], args=[], comm=comm)[1]


class _Net:
    def __init__(self, shards):
        self.shards = shards
        self.whole, self.partly, self.own, self.theirs, self.sums, self.other = {}, {}, {}, {}, {}, {}
        x, y, _ = _place()
        self.chip = 2 * x + y

    def fetch(self, names):
        return _fetch_copies([self.shards[n] for n in names])

    def fetched(self, names, got):
        self.partly.update(zip(names, got))

    def relay(self, names):
        return _relay_copies([self.partly[n] for n in names])

    def gathered(self, names, got):
        for n, g in zip(names, got):
            self.whole[n] = g.reshape(-1, g.shape[-1])

    def full(self, name):
        return self.whole[name]

    def exchange(self, grads, swap=()):
        stacked = []
        for n, pieces in grads.items():
            (keep, send), = pieces
            self.own[n] = keep
            stacked.append(send.reshape(N_CHIPS, keep.shape[0] // N_CHIPS, send.shape[-1]))
        return _join(_grad_copies(stacked), self.swap(swap))

    def swap(self, names):
        return _sibling_copies([self.sums[n] for n in names]) if names else None

    def presum_begin(self, name, pieces):
        keep = jnp.concatenate([p[0] for p in pieces], axis=0) if len(pieces) > 1 else pieces[0][0]
        send = jnp.concatenate([p[1] for p in pieces], axis=0) if len(pieces) > 1 else pieces[0][1]
        rows = keep.shape[0] // N_CHIPS
        self.held = keep.reshape(N_CHIPS, rows, keep.shape[-1])
        return _half_rows_copies(send.reshape(N_CHIPS, rows, send.shape[-1]))

    def presum_end(self, name, got):
        x, y, c = _place()
        to_send, self.own[name] = _pre_sum("presum_" + name, self.held, got[0], jnp.stack([c, self.chip]))
        return _grad_copies([to_send])

    def received(self, names, swap, got, carried=None):
        self.theirs.update(zip(names, got[:len(names)]))
        self.other.update(zip(swap, got[len(names):]))
        for n in names:
            (self.sums[n],), more = _partial_sum("sum_" + n, self.own[n], self.theirs[n], self.chip, comm=carried)
        return more


def _half_rows_copies(stacked):
    n, rows = stacked.shape[0], stacked.shape[1] // 2

    def build(ins, outs, send_sems, recv_sems, local_sems):
        x, y, c = _place()
        copies = [pltpu.make_async_remote_copy(
            src_ref=ins[0].at[s, pl.ds((1 - c) * rows, rows)], dst_ref=outs[0].at[s], send_sem=send_sems.at[s],
            recv_sem=recv_sems.at[s], device_id=(x, y, 1 - c), device_id_type=MESH_ID) for s in range(n)]
        return copies, copies, []

    return _Carried([stacked], [jax.ShapeDtypeStruct((n, rows, stacked.shape[2]), stacked.dtype)], n, 0, build)


def _pre_sum(name, held, theirs, core_and_chip):
    n, R, C = held.shape
    half = R // 2
    tr = _row_tile(half)
    per_half = half // tr

    def body(place_ref, h_ref, t_ref, send_ref, own_ref):
        total = h_ref[0] + t_ref[0].astype(F32)
        send_ref[0] = total.astype(send_ref.dtype)

        @pl.when(pl.program_id(1) == place_ref[1])
        def _():
            own_ref[...] = total

    return pl.pallas_call(
        body, name=name,
        grid_spec=pltpu.PrefetchScalarGridSpec(
            num_scalar_prefetch=1, grid=(per_half, n),
            in_specs=[pl.BlockSpec((1, tr, C), lambda i, s, place: (s, place[0] * per_half + i, 0)),
                      pl.BlockSpec((1, tr, C), lambda i, s, place: (s, i, 0))],
            out_specs=[pl.BlockSpec((1, tr, C), lambda i, s, place: (s, i, 0)),
                       pl.BlockSpec((tr, C), lambda i, s, place: (i, 0))]),
        out_shape=[jax.ShapeDtypeStruct((n, half, C), MXU_DTYPE), jax.ShapeDtypeStruct((half, C), F32)],
        compiler_params=_params(("arbitrary", "arbitrary")),
    )(core_and_chip, held, theirs)


def _sibling_copies(parts):
    n = len(parts)

    def build(ins, outs, send_sems, recv_sems, local_sems):
        x, y, c = _place()
        copies = [pltpu.make_async_remote_copy(
            src_ref=ins[w], dst_ref=outs[w], send_sem=send_sems.at[w], recv_sem=recv_sems.at[w],
            device_id=(x, y, 1 - c), device_id_type=MESH_ID) for w in range(n)]
        return copies, copies, []

    return _Carried(parts, [jax.ShapeDtypeStruct(p.shape, p.dtype) for p in parts], n, 0, build)


def _row_tile(rows, most=512, sublanes=16):
    return max(t for t in range(sublanes, min(most, rows // 2) + 1, sublanes) if rows % t == 0)


def _partial_sum(name, own, recv, chip, comm=None):
    _, R, C = recv.shape
    tr = _row_tile(R)

    def body(o_ref, r_ref, p_ref):
        p_ref[...] = ((o_ref[...] + r_ref[0].astype(F32)) + r_ref[1].astype(F32)) + r_ref[2].astype(F32)

    if own.shape[0] != R:
        assert comm is None and own.shape[0] == N_CHIPS * R
        total = pl.pallas_call(
            lambda chip_ref, *refs: body(*refs), name=name,
            grid_spec=pltpu.PrefetchScalarGridSpec(
                num_scalar_prefetch=1, grid=(R // tr,),
                in_specs=[pl.BlockSpec((tr, C), lambda i, chip_ref: (chip_ref[0] * (R // tr) + i, 0)),
                          pl.BlockSpec((3, tr, C), lambda i, chip_ref: (0, i, 0))],
                out_specs=pl.BlockSpec((tr, C), lambda i, chip_ref: (i, 0))),
            out_shape=jax.ShapeDtypeStruct((R, C), F32), compiler_params=_params(("parallel",)),
        )(chip.reshape(1), own, recv)
        return [total], []
    return _call(name, body, grid=(R // tr,),
                 in_specs=[pl.BlockSpec((tr, C), lambda i: (i, 0)), pl.BlockSpec((3, tr, C), lambda i: (0, i, 0))],
                 out_specs=[pl.BlockSpec((tr, C), lambda i: (i, 0))], out_shape=[jax.ShapeDtypeStruct((R, C), F32)],
                 args=[own, recv], semantics=("parallel",), comm=comm)


def _adam_vals(w, g, m, v):
    m = ADAM_B1 * m + (1.0 - ADAM_B1) * g
    v = ADAM_B2 * v + (1.0 - ADAM_B2) * (g * g)
    m_hat = m / (1.0 - ADAM_B1 ** ADAM_STEP)
    v_hat = v / (1.0 - ADAM_B2 ** ADAM_STEP)
    delta = -ADAM_LR * (m_hat / (jnp.sqrt(v_hat) + ADAM_EPS) + ADAM_WD * w)
    return delta, m, v


def _adamw(name, w, m, v, mine, other, comm=None):
    R, C = w.shape
    tr = _row_tile(R)

    def body(w_ref, m_ref, v_ref, s_ref, n_ref, g_ref, d_ref, nm_ref, nv_ref):
        g = s_ref[...] + n_ref[...]
        d, nm, nv = _adam_vals(w_ref[...], g, m_ref[...], v_ref[...])
        g_ref[...], d_ref[...], nm_ref[...], nv_ref[...] = g, d, nm, nv

    spec = pl.BlockSpec((tr, C), lambda i: (i, 0))
    return _call(name, body, grid=(R // tr,), in_specs=[spec] * 5, out_specs=[spec] * 4,
                 out_shape=[jax.ShapeDtypeStruct((R, C), F32)] * 4, args=[w, m, v, mine, other],
                 semantics=("parallel",), comm=comm)


def _adamw_by_halves(name, w, m, v, mine, other, core):
    R, C = w.shape
    tr = _row_tile(R // 2)
    per_half = R // 2 // tr

    def body(c_ref, w_ref, m_ref, v_ref, s_ref, n_ref, g_ref, d_ref, nm_ref, nv_ref):
        g = jnp.where(pl.program_id(0) // per_half == c_ref[0, 0], s_ref[...], n_ref[...])
        d, nm, nv = _adam_vals(w_ref[...], g, m_ref[...], v_ref[...])
        g_ref[...], d_ref[...], nm_ref[...], nv_ref[...] = g, d, nm, nv

    spec = pl.BlockSpec((tr, C), lambda i: (i, 0))
    part = pl.BlockSpec((tr, C), lambda i: (i % per_half, 0))
    return pl.pallas_call(
        body, name=name, grid=(R // tr,),
        in_specs=[pl.BlockSpec(memory_space=pltpu.SMEM), spec, spec, spec, part, part], out_specs=[spec] * 4,
        out_shape=[jax.ShapeDtypeStruct((R, C), F32)] * 4, compiler_params=_params(("parallel",)),
    )(core, w, m, v, mine, other)


SMALL_LAYOUT = dict(norm_mix_g=(0, 1, 1024), b_in=(1, 8, 7424), hgrn_norm_g=(9, 1, 1024), norm_ffn_g=(10, 1, 1024),
                    norm_final_g=(11, 1, 1024), hgrn_lb_logits=(12, 2, 2048), attn_sinks=(14, 1, 16))
SMALL_LOSS_ROW, SMALL_ROWS = 15, 16


def _pack_small(grads, loss):
    rows = [jnp.pad(grads[name].astype(F32).reshape(-1), (0, nrows * D_MODEL - n))
            for name, (_, nrows, n) in SMALL_LAYOUT.items()]
    rows.append(jnp.pad(loss.astype(F32).reshape(1), (0, D_MODEL - 1)))
    return jnp.concatenate(rows).reshape(SMALL_ROWS, D_MODEL)


def _adamw_small(w, m, v, g_all):
    names = list(SMALL_LAYOUT)
    n = len(names)

    def body(a_ref, *refs):
        ins, outs = refs[:3 * n], refs[3 * n:]
        g_all_rows = a_ref[0]
        for dev in range(1, 8):
            g_all_rows = g_all_rows + a_ref[dev]
        for i, name in enumerate(names):
            first, nrows, count = SMALL_LAYOUT[name]
            w_ref, m_ref, v_ref = ins[3 * i:3 * i + 3]
            if w_ref.shape[0] == nrows:
                g = g_all_rows[first:first + nrows, :w_ref.shape[1]]
            else:
                last = count - (nrows - 1) * D_MODEL
                g = jnp.concatenate([g_all_rows[r:r + 1, :] for r in range(first, first + nrows - 1)]
                                    + [g_all_rows[first + nrows - 1:first + nrows, :last]], axis=1)
            d, nm, nv = _adam_vals(w_ref[...], g, m_ref[...], v_ref[...])
            for o_ref, val in zip(outs[4 * i:4 * i + 4], (g, d, nm, nv)):
                o_ref[...] = val
        outs[4 * n][...] = g_all_rows[SMALL_LOSS_ROW:SMALL_LOSS_ROW + 1, 0:1]

    res = pl.pallas_call(
        body, name="adamw_small",
        out_shape=[jax.ShapeDtypeStruct(w[name].shape, F32) for name in names for _ in range(4)]
        + [jax.ShapeDtypeStruct((1, 1), F32)],
    )(g_all, *[t[name] for name in names for t in (w, m, v)])
    return {name: res[4 * i:4 * i + 4] for i, name in enumerate(names)}, res[4 * n]


MATRICES = ("w_in", "w_branch_attn", "w_branch_hgrn", "w_out", "w_ffn_gate", "w_ffn_up", "w_ffn_down")
COLUMN_SHARDED = ("w_in", "w_ffn_gate", "w_ffn_up")
WEIGHTS = ("norm_mix_g", "w_in", "b_in", "attn_sinks", "hgrn_lb_logits", "hgrn_norm_g", "w_branch_attn",
           "w_branch_hgrn", "w_out", "norm_ffn_g", "w_ffn_gate", "w_ffn_up", "w_ffn_down", "norm_final_g")


def kernel(x, norm_mix_g, w_in, b_in, attn_sinks, hgrn_lb_logits, hgrn_norm_g, w_branch_attn, w_branch_hgrn, w_out, norm_ffn_g, w_ffn_gate, w_ffn_up, w_ffn_down, norm_final_g, loss_target, m_norm_mix_g, m_w_in, m_b_in, m_attn_sinks, m_hgrn_lb_logits, m_hgrn_norm_g, m_w_branch_attn, m_w_branch_hgrn, m_w_out, m_norm_ffn_g, m_w_ffn_gate, m_w_ffn_up, m_w_ffn_down, m_norm_final_g, v_norm_mix_g, v_w_in, v_b_in, v_attn_sinks, v_hgrn_lb_logits, v_hgrn_norm_g, v_w_branch_attn, v_w_branch_hgrn, v_w_out, v_norm_ffn_g, v_w_ffn_gate, v_w_ffn_up, v_w_ffn_down, v_norm_final_g):
    given = dict(locals())
    w = {n: given[n] for n in WEIGHTS}
    m = {n: given["m_" + n] for n in WEIGHTS}
    v = {n: given["v_" + n] for n in WEIGHTS}

    block = lambda a, n: jnp.transpose(a[0]) if n in COLUMN_SHARDED else a[0]
    unblock = lambda a, n: (jnp.transpose(a) if n in COLUMN_SHARDED else a)[None]
    net = _Net({n: block(w[n], n).astype(MXU_DTYPE) for n in MATRICES})
    net.gathered(("w_in",), [_gather_by_neighbours("gather_w_in", net.shards["w_in"])])
    vec = dict(norm_mix_g=norm_mix_g, b_in=b_in, attn_sinks=attn_sinks, hgrn_lb_logits=hgrn_lb_logits,
               hgrn_norm_g=hgrn_norm_g, norm_ffn_g=norm_ffn_g, norm_final_g=norm_final_g.reshape(1, D_MODEL))
    loss_part, dx, d_vecs = _local_step(x[0], loss_target[0], vec, net)

    small_all, = net.received(*net.last, carried=_small_copies(_pack_small(d_vecs, loss_part)))
    grads, deltas, new_m, new_v = {}, {}, {}, {}
    for n in ("w_ffn_down", "w_ffn_gate", "w_ffn_up", "w_out", "w_branch_attn", "w_branch_hgrn"):
        res, got = _adamw("adamw_" + n, block(w[n], n), block(m[n], n), block(v[n], n), net.sums[n], net.other[n],
                          comm=net.swap(("w_in",)) if n == "w_ffn_down" else None)
        if n == "w_ffn_down":
            net.other["w_in"], = got
        grads[n], deltas[n], new_m[n], new_v[n] = (unblock(r, n) for r in res)
    n = "w_in"
    res = _adamw_by_halves("adamw_" + n, block(w[n], n), block(m[n], n), block(v[n], n), net.sums[n], net.other[n],
                           _place()[2].reshape(1, 1))
    grads[n], deltas[n], new_m[n], new_v[n] = (unblock(r, n) for r in res)
    rows = lambda t: {n: t[n].reshape(-1, t[n].shape[-1]) for n in SMALL_LAYOUT}
    res, loss = _adamw_small(rows(w), rows(m), rows(v), small_all)
    for n, four in res.items():
        grads[n], deltas[n], new_m[n], new_v[n] = (r.reshape(w[n].shape) for r in four)
    loss = loss.reshape(())
    return (loss, dx[None], *[grads[n] for n in WEIGHTS], *[deltas[n] for n in WEIGHTS],
            *[new_m[n] for n in WEIGHTS], *[new_v[n] for n in WEIGHTS])
```

```python
import functools
import math

import jax
import jax.numpy as jnp
from jax import lax
from jax.experimental import pallas as pl
from jax.experimental.pallas import tpu as pltpu

F32 = jnp.float32
BF16 = jnp.bfloat16
MXU_DTYPE = jnp.bfloat16
SAVED_DTYPE = jnp.bfloat16
MESH_ID = pl.DeviceIdType.MESH

D_MODEL = 1024
HEAD_DIM = 64
Q_HEADS = 16
KV_HEADS = 2
GROUP = Q_HEADS // KV_HEADS
KV_WIDTH = KV_HEADS * HEAD_DIM
ATTN_BLOCK = 128
HGRN_HEADS = 8
HGRN_K = 128
CHUNK = 64
HGRN_TOKENS = 256
FFN = 2816
IN_SPLITS = (1024, 256, 4096, 2048)
EPS = 1e-6
NEG_INF = -1e30
ADAM_LR, ADAM_B1, ADAM_B2, ADAM_EPS, ADAM_WD, ADAM_STEP = 0.001, 0.9, 0.999, 1e-08, 0.01, 10
N_CHIPS = 4
VMEM_LIMIT = 60 * 1024 * 1024
ROW_ALIGN = 16
DW_ROWS = 256


def _params(sem=None):
    return pltpu.CompilerParams(dimension_semantics=sem, vmem_limit_bytes=VMEM_LIMIT)


def _sigmoid(v):
    return 0.5 * jnp.tanh(0.5 * v) + 0.5


def _dot(a, b, dims):
    return lax.dot_general(a.astype(MXU_DTYPE), b.astype(MXU_DTYPE), (dims, ((), ())),
                           preferred_element_type=F32)


def _nn(a, b):
    return _dot(a, b, ((1,), (0,)))


def _nt(a, b):
    return _dot(a, b, ((1,), (1,)))


def _tn(a, b):
    return _dot(a, b, ((0,), (0,)))


HBM_SPEC = pl.BlockSpec(memory_space=pl.ANY)


class _Carried:
    def __init__(self, arrays, out_shapes, n_remote, n_local, build, continued=None, stages=()):
        self.continued = dict(continued or {})
        self.parts = [(len(arrays), len(out_shapes), 3 + len(stages), build)]
        self.arrays, self.out_shapes = list(arrays), list(out_shapes)
        self.scratch = [pltpu.SemaphoreType.DMA((n_remote,)), pltpu.SemaphoreType.DMA((n_remote,)),
                        pltpu.SemaphoreType.DMA((max(n_local, 1),))] + list(stages)

    def __add__(self, other):
        both = _Carried([], [], 1, 0, None)
        both.parts = self.parts + other.parts
        both.arrays, both.out_shapes = self.arrays + other.arrays, self.out_shapes + other.out_shapes
        both.scratch = self.scratch + other.scratch
        both.continued = dict(self.continued)
        both.continued.update({len(self.arrays) + i: len(self.out_shapes) + o for i, o in other.continued.items()})
        return both

    def _built(self, ins, outs, sems):
        for ni, no, ns, build in self.parts:
            yield build(ins[:ni], outs[:no], *sems[:ns])
            ins, outs, sems = ins[ni:], outs[no:], sems[ns:]

    def start(self, ins, outs, sems):
        core = lax.axis_index("c")
        for sends, _, local, *other_order in self._built(ins, outs, sems):
            for cp in local:
                if not isinstance(cp, tuple):
                    cp.start()
            if not other_order:
                for cp in sends:
                    cp.start()
            else:
                @pl.when(core == 0)
                def _():
                    for cp in sends:
                        cp.start()

                @pl.when(core == 1)
                def _():
                    for cp in other_order[0]:
                        cp.start()
            staged = [cp for cp in local if isinstance(cp, tuple)]
            for into, _ in staged:
                into.start()
            for into, out_of in staged:
                into.wait()
                out_of.start()

    def wait(self, ins, outs, sems):
        for sends, recvs, local, *_ in self._built(ins, outs, sems):
            for cp in recvs:
                cp.wait_recv()
            for cp in sends:
                cp.wait_send()
            for cp in local:
                (cp[1] if isinstance(cp, tuple) else cp).wait()


def _join(*comms):
    comms = [c for c in comms if c is not None]
    return functools.reduce(lambda a, b: a + b, comms) if comms else None


def _call(name, body, *, grid, in_specs, out_specs, out_shape, args, scratch=(), semantics=None, comm=None,
          aliases=None):
    n_in, n_out, n_scr = len(in_specs), len(out_specs), len(scratch)
    aliases = aliases or {}
    if comm is None:
        res = pl.pallas_call(body, name=name, grid=grid, in_specs=in_specs, out_specs=out_specs, out_shape=out_shape,
                             scratch_shapes=list(scratch), input_output_aliases=aliases,
                             compiler_params=_params(semantics))(*args)
        return list(res), []
    ci, co = len(comm.arrays), len(comm.out_shapes)
    aliases = dict(aliases)
    aliases.update({n_in + i: n_out + o for i, o in comm.continued.items()})

    def carrying(*refs):
        ins, refs = refs[:n_in], refs[n_in:]
        c_ins, refs = refs[:ci], refs[ci:]
        outs, refs = refs[:n_out], refs[n_out:]
        c_outs, refs = refs[:co], refs[co:]
        scr, sems = refs[:n_scr], refs[n_scr:]
        if not grid:
            comm.start(c_ins, c_outs, sems)
            body(*ins, *outs, *scr)
            comm.wait(c_ins, c_outs, sems)
            return
        first = functools.reduce(jnp.logical_and, [pl.program_id(a) == 0 for a in range(len(grid))])
        last = functools.reduce(jnp.logical_and, [pl.program_id(a) == g - 1 for a, g in enumerate(grid)])

        @pl.when(first)
        def _():
            comm.start(c_ins, c_outs, sems)

        body(*ins, *outs, *scr)

        @pl.when(last)
        def _():
            comm.wait(c_ins, c_outs, sems)

    res = pl.pallas_call(
        carrying, name=name, grid=grid, in_specs=list(in_specs) + [HBM_SPEC] * ci,
        out_specs=list(out_specs) + [HBM_SPEC] * co, out_shape=list(out_shape) + comm.out_shapes,
        scratch_shapes=list(scratch) + comm.scratch, input_output_aliases=aliases,
        compiler_params=_params(("arbitrary",) * len(grid) if grid else None),
    )(*args, *comm.arrays)
    return list(res[:n_out]), list(res[n_out:])


def _weight_grad(name, a, b, *, tm, tk, a_colsum=False, carrying=False, comm=None):
    (T, N), M = b.shape, a.shape[1]
    tk = min(tk, T)
    assert M % tm == 0 and T % tk == 0, (name, M, tm, T, tk)
    ni, nk = M // tm, T // tk

    def body(a_ref, b_ref, *rest):
        keep_ref, send_ref = rest[:2]
        sums_ref = rest[2] if a_colsum else None
        if nk == 1:
            acc = _tn(a_ref[...], b_ref[...])
            keep_ref[...], send_ref[...] = acc, acc.astype(send_ref.dtype)
            if a_colsum:
                sums_ref[...] = jnp.sum(a_ref[...].astype(F32), axis=0, keepdims=True)
            return
        acc_ref = rest[-1]
        k = pl.program_id(1)

        @pl.when(k == 0)
        def _():
            acc_ref[...] = jnp.zeros_like(acc_ref)
            if a_colsum:
                sums_ref[...] = jnp.zeros((1, tm), F32)

        if a_colsum:
            sums_ref[...] += jnp.sum(a_ref[...].astype(F32), axis=0, keepdims=True)
        acc_ref[...] += _tn(a_ref[...], b_ref[...])

        @pl.when(k == nk - 1)
        def _():
            keep_ref[...], send_ref[...] = acc_ref[...], acc_ref[...].astype(send_ref.dtype)

    out_spec = pl.BlockSpec((tm, N), lambda i, k: (i, 0))
    out_shape = [jax.ShapeDtypeStruct((M, N), F32), jax.ShapeDtypeStruct((M, N), MXU_DTYPE)]
    out_specs = [out_spec, out_spec]
    if a_colsum:
        out_shape.append(jax.ShapeDtypeStruct((1, M), F32))
        out_specs.append(pl.BlockSpec((1, tm), lambda i, k: (0, i)))
    res, got = _call(
        name, body, grid=(ni, nk),
        in_specs=[pl.BlockSpec((tk, tm), lambda i, k: (k, i)), pl.BlockSpec((tk, N), lambda i, k: (k, 0))],
        out_specs=out_specs, out_shape=out_shape, scratch=[pltpu.VMEM((tm, N), F32)] if nk > 1 else [],
        args=[a, b], semantics=("parallel", "arbitrary"), comm=comm)
    return (res, got) if carrying else res


def _ffn_fwd(merged, w_out, x, gain, w_gate_t, w_up_t, *, tm, comm=None):
    (T, D), F = x.shape, w_gate_t.shape[0]

    def body(m_ref, wo_ref, x_ref, g_ref, wg_ref, wu_ref, h_ref, u_ref, gate_ref, up_ref, z_ref):
        h = x_ref[...] + _nn(m_ref[...], wo_ref[...])
        h_ref[...] = h
        u = (h * lax.rsqrt(jnp.mean(h * h, axis=-1, keepdims=True) + EPS) * g_ref[...]).astype(u_ref.dtype)
        u_ref[...] = u
        gate, up = _nt(u, wg_ref[...]), _nt(u, wu_ref[...])
        gate_ref[...], up_ref[...] = gate.astype(gate_ref.dtype), up.astype(up_ref.dtype)
        z_ref[...] = (gate * _sigmoid(gate) * up).astype(z_ref.dtype)

    rows = lambda n: pl.BlockSpec((tm, n), lambda i: (i, 0))
    fixed = _fixed_spec
    return _call("ffn_hidden", body, grid=(T // tm,),
                 in_specs=[rows(D), fixed(w_out), rows(D), fixed(gain), fixed(w_gate_t), fixed(w_up_t)],
                 out_specs=[rows(D), rows(D), rows(F), rows(F), rows(F)],
                 out_shape=[jax.ShapeDtypeStruct((T, D), F32), jax.ShapeDtypeStruct((T, D), MXU_DTYPE)]
                 + [jax.ShapeDtypeStruct((T, F), SAVED_DTYPE)] * 2 + [jax.ShapeDtypeStruct((T, F), MXU_DTYPE)],
                 args=[merged, w_out, x, gain, w_gate_t, w_up_t], semantics=("parallel",), comm=comm)


def _in_proj(x, gain, w_in_t, b_in, *, tm, comm=None):
    T, D = x.shape
    bounds = [sum(IN_SPLITS[:i]) for i in range(len(IN_SPLITS) + 1)]

    def body(x_ref, g_ref, w_ref, b_ref, u_ref, *piece_refs):
        xv = x_ref[...]
        r = lax.rsqrt(jnp.mean(xv * xv, axis=-1, keepdims=True) + EPS)
        u = (xv * r * g_ref[...]).astype(u_ref.dtype)
        u_ref[...] = u
        for o_ref, lo, hi in zip(piece_refs, bounds[:-1], bounds[1:]):
            o_ref[...] = (_nt(u, w_ref[lo:hi, :]) + b_ref[:, lo:hi]).astype(o_ref.dtype)

    rows = lambda n: pl.BlockSpec((tm, n), lambda i: (i, 0))
    fixed = _fixed_spec
    dtypes = (MXU_DTYPE, MXU_DTYPE, F32, F32)
    return _call("in_proj", body, grid=(T // tm,),
                 in_specs=[rows(D), fixed(gain), fixed(w_in_t), fixed(b_in)],
                 out_specs=[rows(D)] + [rows(n) for n in IN_SPLITS],
                 out_shape=[jax.ShapeDtypeStruct((T, D), MXU_DTYPE)]
                 + [jax.ShapeDtypeStruct((T, n), dt) for n, dt in zip(IN_SPLITS, dtypes)],
                 args=[x, gain, w_in_t, b_in], semantics=("parallel",), comm=comm)


def _row_spec(tm, n):
    return pl.BlockSpec((tm, n), lambda i: (i, 0))


def _fixed_spec(a):
    return pl.BlockSpec(a.shape, lambda i: (0,) * a.ndim, pipeline_mode=pl.Buffered(1))


def _partials_spec(n):
    return pl.BlockSpec((8, n), lambda i: (i, 0))


def _row_parts(tm, parts):
    assert tm % parts == 0, (tm, parts)
    return [slice(p * (tm // parts), (p + 1) * (tm // parts)) for p in range(parts)]


def _ffn_tail(z, w_down, h1, target, gain, gate, up, *, tm, parts=2):
    (T, F), D = z.shape, h1.shape[1]

    def body(z_ref, w_ref, h_ref, t_ref, g_ref, gate_ref, up_ref, dh_ref, dhb_ref, dgate_ref, dup_ref, dg_ref, l_ref):
        part, dgain = 0.0, 0.0
        pieces = _row_parts(tm, parts)
        h2s = [h_ref[rows, :] + _nn(z_ref[rows, :], w_ref[...]) for rows in pieces]
        for rows, h2 in zip(pieces, h2s):
            r = lax.rsqrt(jnp.mean(h2 * h2, axis=-1, keepdims=True) + EPS)
            xhat = h2 * r
            err = xhat * g_ref[...] - t_ref[rows, :]
            part += 0.5 * jnp.sum(jnp.sum(err * err, axis=-1, keepdims=True), axis=0, keepdims=True) / D
            dy = err / D
            dxh = dy * g_ref[...]
            dh2 = r * (dxh - xhat * jnp.mean(dxh * xhat, axis=-1, keepdims=True))
            dh_ref[rows, :] = dh2
            dhb = dh2.astype(dhb_ref.dtype)
            dhb_ref[rows, :] = dhb
            dgain += jnp.sum(dy * xhat, axis=0, keepdims=True)
            dz = _nt(dhb, w_ref[...])
            gv, upv = gate_ref[rows, :].astype(F32), up_ref[rows, :].astype(F32)
            s = _sigmoid(gv)
            dgate_ref[rows, :] = (dz * upv * (s * (1.0 + gv * (1.0 - s)))).astype(dgate_ref.dtype)
            dup_ref[rows, :] = (dz * (gv * s)).astype(dup_ref.dtype)
        dg_ref[...] = jnp.broadcast_to(dgain, dg_ref.shape)
        l_ref[...] = jnp.broadcast_to(part, l_ref.shape)

    low = lambda n: jax.ShapeDtypeStruct((T, n), MXU_DTYPE)
    part = jax.ShapeDtypeStruct((8 * (T // tm), D), F32)
    return pl.pallas_call(
        body, name="ffn_tail", grid=(T // tm,),
        in_specs=[_row_spec(tm, F), _fixed_spec(w_down), _row_spec(tm, D), _row_spec(tm, D), _fixed_spec(gain),
                  _row_spec(tm, F), _row_spec(tm, F)],
        out_specs=[_row_spec(tm, D), _row_spec(tm, D), _row_spec(tm, F), _row_spec(tm, F), _partials_spec(D),
                   _partials_spec(D)],
        out_shape=[jax.ShapeDtypeStruct((T, D), F32), low(D), low(F), low(F), part, part],
        compiler_params=_params(("parallel",)),
    )(z, w_down, h1, target, gain, gate, up)


def _ffn_in_bwd(dgate, dup, w_gate_t, w_up_t, h1, gain, dres, *, tm, comm=None):
    (T, F), D = dgate.shape, h1.shape[1]

    def body(dg_ref, du_ref, wg_ref, wu_ref, h_ref, g_ref, r_ref, dh_ref, dhb_ref, dgain_ref):
        d_u2 = _nn(dg_ref[...], wg_ref[...]) + _nn(du_ref[...], wu_ref[...])
        dx, dgain = _rmsnorm_bwd_vals(d_u2, h_ref[...], g_ref[...])
        dh = r_ref[...] + dx
        dh_ref[...] = dh
        dhb_ref[...] = dh.astype(dhb_ref.dtype)
        dgain_ref[...] = jnp.broadcast_to(dgain, dgain_ref.shape)

    return _call("d_ffn_in", body, grid=(T // tm,),
                 in_specs=[_row_spec(tm, F), _row_spec(tm, F), _fixed_spec(w_gate_t), _fixed_spec(w_up_t),
                           _row_spec(tm, D), _fixed_spec(gain), _row_spec(tm, D)],
                 out_specs=[_row_spec(tm, D), _row_spec(tm, D), _partials_spec(D)],
                 out_shape=[jax.ShapeDtypeStruct((T, D), F32), jax.ShapeDtypeStruct((T, D), MXU_DTYPE),
                            jax.ShapeDtypeStruct((8 * (T // tm), D), F32)],
                 args=[dgate, dup, w_gate_t, w_up_t, h1, gain, dres], semantics=("parallel",), comm=comm)


def _in_proj_bwd(pieces, w_in_t, x, gain, dres, *, tm, comm=None):
    T, D = x.shape
    n = len(pieces)

    def body(*refs):
        dps, (w_ref, x_ref, g_ref, r_ref, dx_ref, dgain_ref) = refs[:n], refs[n:]
        d_u = None
        for dp_ref, (dp, first) in zip(dps, pieces):
            term = _nn(dp_ref[...], w_ref[first:first + dp.shape[1], :])
            d_u = term if d_u is None else d_u + term
        dx, dgain = _rmsnorm_bwd_vals(d_u, x_ref[...], g_ref[...])
        dx_ref[...] = r_ref[...] + dx
        dgain_ref[...] = jnp.broadcast_to(dgain, dgain_ref.shape)

    return _call("d_u", body, grid=(T // tm,),
                 in_specs=[_row_spec(tm, dp.shape[1]) for dp, _ in pieces]
                 + [_fixed_spec(w_in_t), _row_spec(tm, D), _fixed_spec(gain), _row_spec(tm, D)],
                 out_specs=[_row_spec(tm, D), _partials_spec(D)],
                 out_shape=[jax.ShapeDtypeStruct((T, D), F32), jax.ShapeDtypeStruct((8 * (T // tm), D), F32)],
                 args=[dp for dp, _ in pieces] + [w_in_t, x, gain, dres], semantics=("parallel",), comm=comm)


def _merge_fwd(y_a, y_b, w_a, w_b, gates, *, tm, comm=None):
    T, D = y_a.shape

    def body(ya_ref, yb_ref, wa_ref, wb_ref, ga_ref, gb_ref, pa_ref, pb_ref, m_ref):
        pa, pb = _nn(ya_ref[...], wa_ref[...]), _nn(yb_ref[...], wb_ref[...])
        pa_ref[...], pb_ref[...] = pa.astype(pa_ref.dtype), pb.astype(pb_ref.dtype)
        m_ref[...] = (_sigmoid(ga_ref[...]) * pa + _sigmoid(gb_ref[...]) * pb).astype(m_ref.dtype)

    rows = pl.BlockSpec((tm, D), lambda i: (i, 0))
    whole = pl.BlockSpec((D, D), lambda i: (0, 0), pipeline_mode=pl.Buffered(1))
    return _call("branch_merge", body, grid=(T // tm,),
                 in_specs=[rows, rows, whole, whole, rows, pl.BlockSpec((tm, D), lambda i: (i, 1))],
                 out_specs=[rows] * 3,
                 out_shape=[jax.ShapeDtypeStruct((T, D), SAVED_DTYPE)] * 2 + [jax.ShapeDtypeStruct((T, D), MXU_DTYPE)],
                 args=[y_a, y_b, w_a, w_b, gates, gates], semantics=("parallel",), comm=comm)


def _merge_bwd(dh, w_out, w_a, w_b, p_a, p_b, gates, *, tm, d_in_width, first):
    T, D = dh.shape

    def body(dh_ref, wo_ref, wa_ref, wb_ref, pa_ref, pb_ref, ga_ref, gb_ref, dpa_ref, dpb_ref, dya_ref, dyb_ref,
             din_ref):
        dm = _nt(dh_ref[...], wo_ref[...])
        sa, sb = _sigmoid(ga_ref[...]), _sigmoid(gb_ref[...])
        dpa, dpb = (dm * sa).astype(dpa_ref.dtype), (dm * sb).astype(dpb_ref.dtype)
        dpa_ref[...], dpb_ref[...] = dpa, dpb
        din_ref[:, :D] = (dm * pa_ref[...].astype(F32) * sa * (1.0 - sa)).astype(din_ref.dtype)
        din_ref[:, D:] = (dm * pb_ref[...].astype(F32) * sb * (1.0 - sb)).astype(din_ref.dtype)
        dya_ref[...] = _nt(dpa, wa_ref[...]).astype(dya_ref.dtype)
        dyb_ref[...] = _nt(dpb, wb_ref[...]).astype(dyb_ref.dtype)

    rows = pl.BlockSpec((tm, D), lambda i: (i, 0))
    whole = pl.BlockSpec((D, D), lambda i: (0, 0), pipeline_mode=pl.Buffered(1))
    low = jax.ShapeDtypeStruct((T, D), MXU_DTYPE)
    return pl.pallas_call(
        body, name="d_branch_merge", grid=(T // tm,),
        in_specs=[rows, whole, whole, whole, rows, rows, rows, pl.BlockSpec((tm, D), lambda i: (i, 1))],
        out_specs=[rows] * 4 + [pl.BlockSpec((pl.Element(tm), pl.Element(2 * D)), lambda i: (
            pl.multiple_of(i * tm, ROW_ALIGN), first))],
        out_shape=[low] * 3 + [jax.ShapeDtypeStruct((T, D), F32), jax.ShapeDtypeStruct((T, d_in_width), MXU_DTYPE)],
        compiler_params=_params(("parallel",)),
    )(dh, w_out, w_a, w_b, p_a, p_b, gates, gates)


def _colsum_partials(p):
    return jnp.sum(p.reshape(-1, 8, p.shape[-1])[:, 0, :], axis=0, keepdims=True)


def _rmsnorm_bwd_vals(dy, xin, g):
    rstd = lax.rsqrt(jnp.mean(xin * xin, axis=-1, keepdims=True) + EPS)
    xhat = xin * rstd
    dg = jnp.sum(dy * xhat, axis=0, keepdims=True)
    dxh = dy * g
    dx = rstd * (dxh - xhat * jnp.mean(dxh * xhat, axis=-1, keepdims=True))
    return dx, dg


ATTN_SCALE = 1.0 / math.sqrt(HEAD_DIM)
GROUP_LANES = GROUP * ATTN_BLOCK
PAIR = 2 * HEAD_DIM


def _attn_mask():
    kj = lax.broadcasted_iota(jnp.int32, (ATTN_BLOCK, GROUP_LANES), 0)
    qi = lax.broadcasted_iota(jnp.int32, (ATTN_BLOCK, GROUP_LANES), 1) & (ATTN_BLOCK - 1)
    return kj <= qi


def _heads_transposed(ref, g, scale=None):
    parts = []
    for a in range(GROUP // 2):
        lo = (g * GROUP // 2 + a) * PAIR
        pair = ref[:, lo:lo + PAIR].astype(F32)
        pair = (pair if scale is None else pair * scale).T
        parts += [pair[:HEAD_DIM], pair[HEAD_DIM:]]
    return jnp.concatenate(parts, axis=1).astype(MXU_DTYPE)


def _heads_back(ref, g, vt):
    for a in range(GROUP // 2):
        lo = (g * GROUP // 2 + a) * PAIR
        pair = jnp.concatenate([vt[:, (2 * a) * ATTN_BLOCK:(2 * a + 1) * ATTN_BLOCK],
                                vt[:, (2 * a + 1) * ATTN_BLOCK:(2 * a + 2) * ATTN_BLOCK]], axis=0)
        ref[:, lo:lo + PAIR] = pair.T.astype(ref.dtype)


def _kv_parts(kv_ref, g):
    ks = slice(g * HEAD_DIM, (g + 1) * HEAD_DIM)
    vs = slice(KV_WIDTH + g * HEAD_DIM, KV_WIDTH + (g + 1) * HEAD_DIM)
    return kv_ref[:, ks].astype(MXU_DTYPE), kv_ref[:, vs].astype(MXU_DTYPE)


def _sink_rows(sinks):
    return jnp.repeat(sinks.reshape(KV_HEADS, GROUP), ATTN_BLOCK, axis=1)


def _attn_fwd(pq, pkv, sinks, comm=None):
    T = pq.shape[0]
    nb = T // ATTN_BLOCK

    def body(q_ref, kvc_ref, kvp_ref, s_ref, y_ref, lse_ref):
        mask_c = _attn_mask()
        has_prev = pl.program_id(0) > 0
        early = []
        for g in range(KV_HEADS):
            (kc, vc), (kp, vp) = _kv_parts(kvc_ref, g), _kv_parts(kvp_ref, g)
            qt = _heads_transposed(q_ref, g, ATTN_SCALE)
            early.append((vc, vp, jnp.where(mask_c, _nn(kc, qt), jnp.where(has_prev, _nn(kp, qt), NEG_INF))))
        for g, (vc, vp, s) in enumerate(early):
            sink = s_ref[g:g + 1, :]
            m = jnp.maximum(jnp.max(s, axis=0, keepdims=True), sink)
            p = jnp.exp(s - m)
            den = jnp.sum(p, axis=0, keepdims=True) + jnp.exp(sink - m)
            pc = jnp.where(mask_c, p, 0.0)
            _heads_back(y_ref, g, (_tn(vc, pc) + _tn(vp, p - pc)) / den)
            lse = m + jnp.log(den)
            for i in range(GROUP):
                lse_ref[g * GROUP + i:g * GROUP + i + 1, :] = lse[:, i * ATTN_BLOCK:(i + 1) * ATTN_BLOCK]

    return _call(
        "attn_fwd", body, grid=(nb,),
        in_specs=[pl.BlockSpec((ATTN_BLOCK, D_MODEL), lambda n: (n, 0)),
                  pl.BlockSpec((ATTN_BLOCK, 2 * KV_WIDTH), lambda n: (n, 0)),
                  pl.BlockSpec((ATTN_BLOCK, 2 * KV_WIDTH), lambda n: (jnp.maximum(n - 1, 0), 0)),
                  pl.BlockSpec((KV_HEADS, GROUP_LANES), lambda n: (0, 0))],
        out_specs=[pl.BlockSpec((ATTN_BLOCK, D_MODEL), lambda n: (n, 0)),
                   pl.BlockSpec((Q_HEADS, ATTN_BLOCK), lambda n: (0, n))],
        out_shape=[jax.ShapeDtypeStruct((T, D_MODEL), MXU_DTYPE), jax.ShapeDtypeStruct((Q_HEADS, T), F32)],
        args=[pq, pkv, pkv, _sink_rows(sinks)], semantics=("parallel",), comm=comm)


def _attn_bwd(pq, pkv, sinks, lse, dy, d_in, comm=None):
    T = pq.shape[0]
    nb = T // ATTN_BLOCK
    cur = lambda n: (jnp.minimum(n, nb - 1), 0)
    done = D_MODEL + 2 * KV_WIDTH

    def body(q_ref, kvc_ref, kvp_ref, s_ref, lse_ref, dy_ref, _, out_ref, ds_ref, carry, top, bot, dq_ref):
        n = pl.program_id(0)

        @pl.when(n == 0)
        def _():
            carry[...] = jnp.zeros_like(carry)
            dq_ref[...] = jnp.zeros_like(dq_ref)
            ds_ref[...] = jnp.zeros_like(ds_ref)

        out_ref[:, :D_MODEL] = dq_ref[...]

        @pl.when(n < nb)
        def _():
            mask_c = _attn_mask()
            valid = jnp.logical_or(mask_c, n > 0)
            early = []
            for g in range(KV_HEADS):
                (kc, vc), (kp, vp) = _kv_parts(kvc_ref, g), _kv_parts(kvp_ref, g)
                qt = _heads_transposed(q_ref, g, ATTN_SCALE)
                dot = _heads_transposed(dy_ref, g)
                early.append((kc, kp, qt, dot, jnp.where(mask_c, _nn(kc, qt), _nn(kp, qt)),
                              jnp.where(mask_c, _nn(vc, dot), _nn(vp, dot))))
            for g, (kc, kp, qt, dot, s, dp) in enumerate(early):
                ks = slice(g * HEAD_DIM, (g + 1) * HEAD_DIM)
                vs = slice(KV_WIDTH + g * HEAD_DIM, KV_WIDTH + (g + 1) * HEAD_DIM)
                lse = jnp.concatenate([lse_ref[g * GROUP + i:g * GROUP + i + 1, :] for i in range(GROUP)], axis=1)
                p = jnp.where(valid, jnp.exp(s - lse), 0.0)
                delta = jnp.sum(p * dp, axis=0, keepdims=True)
                ds = p * (dp - delta)
                ds_c, p_c = jnp.where(mask_c, ds, 0.0), jnp.where(mask_c, p, 0.0)
                ds_p, p_p = ds - ds_c, p - p_c
                _heads_back(dq_ref, g, (_tn(kc, ds_c) + _tn(kp, ds_p)) * ATTN_SCALE)
                bot[:, ks], bot[:, vs] = _nt(ds_c, qt), _nt(p_c, dot)
                top[:, ks], top[:, vs] = _nt(ds_p, qt), _nt(p_p, dot)
                ds_ref[g:g + 1, :] -= jnp.exp(s_ref[g:g + 1, :] - lse) * delta
            out_ref[:, D_MODEL:] = (carry[...] + top[...]).astype(out_ref.dtype)
            carry[...] = bot[...]

        @pl.when(n == nb)
        def _():
            out_ref[:, D_MODEL:] = carry[...].astype(out_ref.dtype)

    return _call(
        "attn_bwd", body, grid=(nb + 1,),
        in_specs=[pl.BlockSpec((ATTN_BLOCK, D_MODEL), cur),
                  pl.BlockSpec((ATTN_BLOCK, 2 * KV_WIDTH), cur),
                  pl.BlockSpec((ATTN_BLOCK, 2 * KV_WIDTH), lambda n: (jnp.maximum(jnp.minimum(n, nb - 1) - 1, 0), 0)),
                  pl.BlockSpec((KV_HEADS, GROUP_LANES), lambda n: (0, 0)),
                  pl.BlockSpec((Q_HEADS, ATTN_BLOCK), lambda n: (0, jnp.minimum(n, nb - 1))),
                  pl.BlockSpec((ATTN_BLOCK, D_MODEL), cur), HBM_SPEC],
        out_specs=[pl.BlockSpec((ATTN_BLOCK, done), lambda n: (jnp.maximum(n - 1, 0), 0)),
                   pl.BlockSpec((KV_HEADS, GROUP_LANES), lambda n: (0, 0))],
        out_shape=[jax.ShapeDtypeStruct(d_in.shape, d_in.dtype), jax.ShapeDtypeStruct((KV_HEADS, GROUP_LANES), F32)],
        scratch=[pltpu.VMEM((ATTN_BLOCK, 2 * KV_WIDTH), F32)] * 3 + [pltpu.VMEM((ATTN_BLOCK, D_MODEL), MXU_DTYPE)],
        args=[pq, pkv, pkv, _sink_rows(sinks), lse, dy, d_in], semantics=("arbitrary",), comm=comm, aliases={6: 0})


def _lower_bound(l):
    m = jnp.maximum(l[0:1], l[1:2])
    e0, e1 = jnp.exp(l[0:1] - m), jnp.exp(l[1:2] - m)
    return e0 / (e0 + e1)


def _tri(lower):
    r = lax.broadcasted_iota(jnp.int32, (CHUNK, CHUNK), 0)
    c = lax.broadcasted_iota(jnp.int32, (CHUNK, CHUNK), 1)
    return (r >= c) if lower else (c >= r)


def _chunk_sum(mask, v):
    ones = mask.astype(BF16)
    hi = v.astype(BF16)
    rest = v - hi.astype(F32)
    mid = rest.astype(BF16)
    lo = (rest - mid.astype(F32)).astype(BF16)
    part = lambda t: lax.dot_general(ones, t, (((1,), (0,)), ((), ())), preferred_element_type=F32)
    return part(hi) + part(mid) + part(lo)


def _hgrn_chunk_inputs(hq, hf, lb, causal):
    half_t = 0.5 * jnp.tanh(0.5 * hf)
    sg, sgn = 0.5 + half_t, 0.5 - half_t
    f = lb + (1.0 - lb) * sg
    kk = (1.0 - lb) * sgn
    sq = _sigmoid(hq)
    q = hq * sq
    b = _chunk_sum(causal, jnp.log(f))
    bm, bl = b[CHUNK // 2 - 1:CHUNK // 2, :], b[CHUNK - 1:CHUNK, :]
    e_qm, e_km = jnp.exp(b - bm), jnp.exp(bm - b)
    e_qs, e_kl = e_qm * jnp.exp(bm), e_km * jnp.exp(bl - bm)
    return dict(sg=sg, sgn=sgn, f=f, kk=kk, sq=sq, q=q, e_qm=e_qm, e_km=e_km, e_qs=e_qs, e_kl=e_kl,
                qm=q * e_qm, km=kk * e_km, qs=q * e_qs, kl=kk * e_kl, el=jnp.exp(bl))


def _hgrn_fwd(ph, lb_logits, norm_g, comm=None):
    T = ph.shape[0]
    nblk, cpb = T // HGRN_TOKENS, HGRN_TOKENS // CHUNK
    col = lambda c: pl.BlockSpec((HGRN_TOKENS, D_MODEL), functools.partial(lambda i, c: (i, c), c=c))

    def body(hq_ref, hf_ref, hi_ref, hg_ref, l_ref, ng_ref, y_ref, o_ref, st_ref, s_ref):
        @pl.when(pl.program_id(0) == 0)
        def _():
            s_ref[...] = jnp.zeros_like(s_ref)

        lb = _lower_bound(l_ref[...])
        causal = _tri(True)
        for c in range(cpb):
            rows = slice(c * CHUNK, (c + 1) * CHUNK)
            t = _hgrn_chunk_inputs(hq_ref[rows, :], hf_ref[rows, :], lb, causal)
            qm, km, qs, kl = (t[n].astype(MXU_DTYPE) for n in ("qm", "km", "qs", "kl"))
            v = hi_ref[rows, :].astype(MXU_DTYPE)
            heads = [slice(h * HGRN_K, (h + 1) * HGRN_K) for h in range(HGRN_HEADS)]
            a_all = [jnp.where(causal, _nt(qm[:, ls], km[:, ls]), 0.0).astype(MXU_DTYPE) for ls in heads]
            for h, ls in enumerate(heads):
                st = s_ref[h]
                st_ref[c, ls, :] = st
                o_ref[rows, ls] = _nn(a_all[h], v[:, ls]) + _nt(qs[:, ls], st)
                s_ref[h] = t["el"][:, ls] * st + _tn(v[:, ls], kl[:, ls])
        for h in range(HGRN_HEADS):
            ls = slice(h * HGRN_K, (h + 1) * HGRN_K)
            o = o_ref[:, ls]
            r = lax.rsqrt(jnp.mean(o * o, axis=-1, keepdims=True) + EPS)
            y_ref[:, ls] = (o * r * ng_ref[:, ls] * _sigmoid(hg_ref[:, ls])).astype(y_ref.dtype)

    return _call(
        "hgrn_fwd", body, grid=(nblk,),
        in_specs=[col(0), col(1), col(2), col(3),
                  pl.BlockSpec((2, D_MODEL), lambda i: (0, 0)), pl.BlockSpec((1, D_MODEL), lambda i: (0, 0))],
        out_specs=[pl.BlockSpec((HGRN_TOKENS, D_MODEL), lambda i: (i, 0)),
                   pl.BlockSpec((HGRN_TOKENS, D_MODEL), lambda i: (i, 0)),
                   pl.BlockSpec((cpb, D_MODEL, HGRN_K), lambda i: (i, 0, 0))],
        out_shape=[jax.ShapeDtypeStruct((T, D_MODEL), MXU_DTYPE), jax.ShapeDtypeStruct((T, D_MODEL), F32),
                   jax.ShapeDtypeStruct((T // CHUNK, D_MODEL, HGRN_K), F32)],
        scratch=[pltpu.VMEM((HGRN_HEADS, HGRN_K, HGRN_K), F32)],
        args=[ph, ph, ph, ph, lb_logits, norm_g], semantics=("arbitrary",), comm=comm)


def _hgrn_bwd(ph, o_raw, states, dy, lb_logits, norm_g, d_in, first, comm=None):
    T = ph.shape[0]
    nblk, cpb = T // HGRN_TOKENS, HGRN_TOKENS // CHUNK
    rev = lambda i: nblk - 1 - i
    col = lambda c: pl.BlockSpec((HGRN_TOKENS, D_MODEL), functools.partial(lambda i, c: (rev(i), c), c=c))
    tok = pl.BlockSpec((HGRN_TOKENS, D_MODEL), lambda i: (rev(i), 0))

    def body(hq_ref, hf_ref, hi_ref, hg_ref, o_ref, st_ref, dy_ref, l_ref, ng_ref, _,
             dph_ref, dng_ref, dl_ref, dst_ref, dlb_ref, do_s, dqm_s, dkm_s, dqs_s, dkl_s, dv_s, del_s):
        i = pl.program_id(0)

        @pl.when(i == 0)
        def _():
            dst_ref[...] = jnp.zeros_like(dst_ref)
            dlb_ref[...] = jnp.zeros_like(dlb_ref)
            dng_ref[...] = jnp.zeros_like(dng_ref)

        lb = _lower_bound(l_ref[...])
        causal, anti = _tri(True), _tri(False)
        row = lax.broadcasted_iota(jnp.int32, (CHUNK, D_MODEL), 0)
        for c in reversed(range(cpb)):
            rows = slice(c * CHUNK, (c + 1) * CHUNK)
            hq = hq_ref[rows, :]
            t = _hgrn_chunk_inputs(hq, hf_ref[rows, :], lb, causal)
            sgg = _sigmoid(hg_ref[rows, :])
            dyv = dy_ref[rows, :]
            for h in range(HGRN_HEADS):
                ls = slice(h * HGRN_K, (h + 1) * HGRN_K)
                o = o_ref[rows, ls]
                r = lax.rsqrt(jnp.mean(o * o, axis=-1, keepdims=True) + EPS)
                nrm = o * r
                g_h = sgg[:, ls]
                dph_ref[rows, 3 * D_MODEL + h * HGRN_K:3 * D_MODEL + (h + 1) * HGRN_K] = (
                    dyv[:, ls] * nrm * ng_ref[:, ls] * g_h * (1.0 - g_h)).astype(dph_ref.dtype)
                dyg = dyv[:, ls] * g_h
                dng_ref[:, ls] += jnp.sum(dyg * nrm, axis=0, keepdims=True)
                dn = dyg * ng_ref[:, ls]
                do_s[:, ls] = r * (dn - nrm * jnp.mean(dn * nrm, axis=-1, keepdims=True))
            qm, km, qs, kl = (t[n].astype(MXU_DTYPE) for n in ("qm", "km", "qs", "kl"))
            v = hi_ref[rows, :].astype(MXU_DTYPE)
            do = do_s[...].astype(MXU_DTYPE)
            heads = [slice(h * HGRN_K, (h + 1) * HGRN_K) for h in range(HGRN_HEADS)]
            a_all = [jnp.where(causal, _nt(qm[:, ls], km[:, ls]), 0.0).astype(MXU_DTYPE) for ls in heads]
            da_all = [jnp.where(causal, _nt(do[:, ls], v[:, ls]), 0.0).astype(MXU_DTYPE) for ls in heads]
            for h, ls in enumerate(heads):
                st = st_ref[c, ls, :]
                dst = dst_ref[h]
                a, da = a_all[h], da_all[h]
                dv_s[:, ls] = _tn(a, do[:, ls]) + _nt(kl[:, ls], dst)
                dkl_s[:, ls] = _nn(v[:, ls], dst)
                dqs_s[:, ls] = _nn(do[:, ls], st)
                del_s[:, ls] = jnp.sum(dst * st, axis=0, keepdims=True)
                dst_ref[h] = _tn(do[:, ls], qs[:, ls]) + t["el"][:, ls] * dst
                dqm_s[:, ls] = _nn(da, km[:, ls])
                dkm_s[:, ls] = _tn(da, qm[:, ls])
            dqm, dkm, dqs, dkl = dqm_s[...], dkm_s[...], dqs_s[...], dkl_s[...]
            dq = dqm * t["e_qm"] + dqs * t["e_qs"]
            dk = dkm * t["e_km"] + dkl * t["e_kl"]
            t_qm, t_km, t_kl = dqm * t["qm"], dkm * t["km"], dkl * t["kl"]
            db = t_qm - t_km + dqs * t["qs"] - t_kl
            db_mid = jnp.sum(t_km - t_qm, axis=0, keepdims=True)
            db_last = jnp.sum(t_kl, axis=0, keepdims=True) + del_s[...] * t["el"]
            db = db + jnp.where(row == CHUNK // 2 - 1, db_mid, 0.0) + jnp.where(row == CHUNK - 1, db_last, 0.0)
            dlogf = _chunk_sum(anti, db)
            sq, sg, sgn, f = t["sq"], t["sg"], t["sgn"], t["f"]
            dph_ref[rows, 0:D_MODEL] = (dq * (sq * (1.0 + hq * (1.0 - sq)))).astype(dph_ref.dtype)
            dph_ref[rows, D_MODEL:2 * D_MODEL] = (
                dlogf * (1.0 - lb) * sg * (1.0 - sg) / f - dk * (1.0 - lb) * sgn * (1.0 - sgn)).astype(dph_ref.dtype)
            dph_ref[rows, 2 * D_MODEL:3 * D_MODEL] = dv_s[...].astype(dph_ref.dtype)
            dlb_ref[...] += jnp.sum(dlogf * (1.0 - sg) / f - dk * sgn, axis=0, keepdims=True)

        @pl.when(i == nblk - 1)
        def _():
            dl0 = dlb_ref[...] * lb * (1.0 - lb)
            dl_ref[0:1, :] = dl0
            dl_ref[1:2, :] = -dl0

    wide = pltpu.VMEM((CHUNK, D_MODEL), F32)
    return _call(
        "hgrn_bwd", body, grid=(nblk,),
        in_specs=[col(0), col(1), col(2), col(3), tok,
                  pl.BlockSpec((cpb, D_MODEL, HGRN_K), lambda i: (rev(i), 0, 0)), tok,
                  pl.BlockSpec((2, D_MODEL), lambda i: (0, 0)), pl.BlockSpec((1, D_MODEL), lambda i: (0, 0)), HBM_SPEC],
        out_specs=[pl.BlockSpec((pl.Element(HGRN_TOKENS), pl.Element(4 * D_MODEL)), lambda i: (
                       pl.multiple_of(rev(i) * HGRN_TOKENS, ROW_ALIGN), first)),
                   pl.BlockSpec((1, D_MODEL), lambda i: (0, 0)), pl.BlockSpec((2, D_MODEL), lambda i: (0, 0))],
        out_shape=[jax.ShapeDtypeStruct(d_in.shape, d_in.dtype), jax.ShapeDtypeStruct((1, D_MODEL), F32),
                   jax.ShapeDtypeStruct((2, D_MODEL), F32)],
        scratch=[pltpu.VMEM((HGRN_HEADS, HGRN_K, HGRN_K), F32), pltpu.VMEM((1, D_MODEL), F32),
                 wide, wide, wide, wide, wide, wide, pltpu.VMEM((1, D_MODEL), F32)],
        args=[ph, ph, ph, ph, o_raw, states, dy, lb_logits, norm_g, d_in], semantics=("arbitrary",), comm=comm,
        aliases={9: 0})


def _local_step(x, target, vec, net):
    T, D = x.shape
    norm_mix_g, b_in, sinks, lb_logits = vec["norm_mix_g"], vec["b_in"], vec["attn_sinks"], vec["hgrn_lb_logits"]
    hgrn_norm_g, norm_ffn_g, norm_final_g = vec["hgrn_norm_g"], vec["norm_ffn_g"], vec["norm_final_g"]
    w_in = net.full("w_in")
    o_q, o_kv, o_h, o_g = (sum(IN_SPLITS[:i]) for i in range(4))
    TM = 512

    first = ("w_branch_attn", "w_branch_hgrn", "w_ffn_down")
    (u, pq, pkv, ph, pg), got = _in_proj(x, norm_mix_g, w_in, b_in, tm=256, comm=net.fetch(first))
    net.fetched(first, got)
    second = ("w_ffn_gate", "w_out")
    (y_attn, lse), got = _attn_fwd(pq, pkv, sinks, comm=_join(net.relay(first), net.fetch(second)))
    net.gathered(first, got[:len(first)])
    net.fetched(second, got[len(first):])
    third = ("w_ffn_up",)
    (y_hgrn, o_raw, states), got = _hgrn_fwd(ph, lb_logits, hgrn_norm_g,
                                             comm=_join(net.relay(second), net.fetch(third)))
    net.gathered(second, got[:len(second)])
    net.fetched(third, got[len(second):])
    w_ba, w_bh = net.full("w_branch_attn"), net.full("w_branch_hgrn")
    (ya, yb, merged), got = _merge_fwd(y_attn, y_hgrn, w_ba, w_bh, pg, tm=TM, comm=net.relay(third))
    net.gathered(third, got)
    w_gate, w_up, w_out = net.full("w_ffn_gate"), net.full("w_ffn_up"), net.full("w_out")

    (h1, u2, gpre, up, z), _ = _ffn_fwd(merged, w_out, x, norm_ffn_g, w_gate, w_up, tm=256)
    w_down = net.full("w_ffn_down")

    dh2, dh2b, dgp, dup, dgf_p, loss_p = _ffn_tail(z, w_down, h1, target, norm_final_g, gpre, up, tm=512)
    loss = jnp.sum(loss_p.reshape(-1, 8, D)[:, 0, 0])
    d_norm_final = _colsum_partials(dgf_p)
    d_w_down = _weight_grad("dw_down", z, dh2b, tm=DW_ROWS, tk=T)

    names, swap = ("w_ffn_down",), ()
    (dh1, dh1b, dg2_p), got = _ffn_in_bwd(dgp, dup, w_gate, w_up, h1, norm_ffn_g, dh2, tm=256,
                                          comm=net.exchange(dict(w_ffn_down=[d_w_down]), swap))
    net.received(names, swap, got)
    d_norm_ffn = _colsum_partials(dg2_p)
    d_w_gate = _weight_grad("dw_gate", dgp, u2, tm=DW_ROWS, tk=T)
    d_w_up = _weight_grad("dw_up", dup, u2, tm=DW_ROWS, tk=T)

    rows_in = sum(IN_SPLITS)
    dya, dyb, dy_attn, dy_hgrn, d_in = _merge_bwd(dh1b, w_out, w_ba, w_bh, ya, yb, pg, tm=TM, d_in_width=rows_in,
                                                  first=o_g)
    d_w_out = _weight_grad("dw_out", merged, dh1b, tm=1024, tk=1024)
    d_w_ba = _weight_grad("dw_branch_a", y_attn, dya, tm=1024, tk=1024)
    d_w_bh = _weight_grad("dw_branch_b", y_hgrn, dyb, tm=1024, tk=1024)

    names, swap = ("w_ffn_gate",), ("w_ffn_down",)
    (d_in, dsink), got = _attn_bwd(pq, pkv, sinks, lse, dy_attn, d_in,
                                   comm=net.exchange(dict(w_ffn_gate=[d_w_gate]), swap))
    net.received(names, swap, got)
    names, swap = ("w_ffn_up", "w_out"), ("w_ffn_gate",)
    (d_in, d_hgrn_norm, d_lb_logits), got = _hgrn_bwd(
        ph, o_raw, states, dy_hgrn, lb_logits, hgrn_norm_g, d_in, o_h,
        comm=net.exchange(dict(w_ffn_up=[d_w_up], w_out=[d_w_out]), swap))
    net.received(names, swap, got)

    names, swap = ("w_branch_attn", "w_branch_hgrn"), ("w_ffn_up", "w_out")
    (*d_w_in, d_b_in), got = _weight_grad(
        "dw_in", d_in, u, tm=DW_ROWS, tk=T, a_colsum=True, carrying=True,
        comm=net.exchange(dict(w_branch_attn=[d_w_ba], w_branch_hgrn=[d_w_bh]), swap))
    net.received(names, swap, got)
    d_w_in = [tuple(d_w_in)]

    first_level = net.presum_begin("w_in", d_w_in)
    halves = net.presum_end("w_in", [] if first_level is None else _copies_alone("presum_swap_w_in", first_level))
    names, swap = ("w_in",), ("w_branch_attn", "w_branch_hgrn")
    (dx, dg1_p), got = _in_proj_bwd([(d_in, 0)], w_in, x, norm_mix_g, dh1, tm=256,
                                    comm=_join(halves, net.swap(swap)))
    net.last = (names, swap, got)
    d_norm_mix = _colsum_partials(dg1_p)
    vecs = dict(norm_mix_g=d_norm_mix, b_in=d_b_in, attn_sinks=jnp.sum(dsink.reshape(Q_HEADS, ATTN_BLOCK), axis=1).reshape(1, Q_HEADS),
                hgrn_lb_logits=d_lb_logits,
                hgrn_norm_g=d_hgrn_norm, norm_ffn_g=d_norm_ffn, norm_final_g=d_norm_final)
    return loss, dx, vecs


def _place():
    return lax.axis_index("x"), lax.axis_index("y"), lax.axis_index("c")


def _other_chips(x, y):
    return [(1 - x, y), (x, 1 - y), (1 - x, 1 - y)]


def _y_first(copies):
    return [copies[3 * (i // 3) + (1, 0, 2)[i % 3]] for i in range(len(copies))]


def _fetch_copies(shards):
    n = len(shards)

    def build(ins, outs, send_sems, recv_sems, local_sems, *stages):
        x, y, c = _place()
        mine = 2 * x + y
        local = [(pltpu.make_async_copy(ins[w], stages[w], local_sems.at[2 * w]),
                  pltpu.make_async_copy(stages[w], outs[w].at[mine], local_sems.at[2 * w + 1])) for w in range(n)]
        sends, recvs = [], []
        for w in range(n):
            half = shards[w].shape[0] // 2
            rows = pl.ds(c * half, half)
            for k, (px, py) in enumerate(_other_chips(x, y)):
                sem = 3 * w + k
                sends.append(pltpu.make_async_remote_copy(
                    src_ref=ins[w].at[rows], dst_ref=outs[w].at[mine, rows], send_sem=send_sems.at[sem],
                    recv_sem=recv_sems.at[sem], device_id=(px, py, c), device_id_type=MESH_ID))
                recvs.append(pltpu.make_async_remote_copy(
                    src_ref=ins[w].at[rows], dst_ref=outs[w].at[2 * px + py, rows], send_sem=send_sems.at[sem],
                    recv_sem=recv_sems.at[sem], device_id=(px, py, c), device_id_type=MESH_ID))
        return sends, recvs, local, _y_first(sends)

    return _Carried(shards, [jax.ShapeDtypeStruct((N_CHIPS,) + s.shape, s.dtype) for s in shards], 3 * n, 2 * n, build,
                    stages=[pltpu.VMEM(s.shape, s.dtype) for s in shards])


def _relay_copies(fetched):
    n = len(fetched)

    def build(ins, outs, send_sems, recv_sems, local_sems):
        x, y, c = _place()
        sends, recvs = [], []
        for w in range(n):
            half = fetched[w].shape[1] // 2
            mine, theirs = pl.ds(c * half, half), pl.ds((1 - c) * half, half)
            for k, (px, py) in enumerate(_other_chips(x, y)):
                block, sem = 2 * px + py, 3 * w + k
                sends.append(pltpu.make_async_remote_copy(
                    src_ref=ins[w].at[block, mine], dst_ref=outs[w].at[block, mine], send_sem=send_sems.at[sem],
                    recv_sem=recv_sems.at[sem], device_id=(x, y, 1 - c), device_id_type=MESH_ID))
                recvs.append(pltpu.make_async_remote_copy(
                    src_ref=ins[w].at[block, theirs], dst_ref=outs[w].at[block, theirs], send_sem=send_sems.at[sem],
                    recv_sem=recv_sems.at[sem], device_id=(x, y, 1 - c), device_id_type=MESH_ID))
        return sends, recvs, []

    return _Carried(fetched, [jax.ShapeDtypeStruct(f.shape, f.dtype) for f in fetched], 3 * n, 0, build,
                    continued={w: w for w in range(n)})


def _grad_copies(stacked):
    n = len(stacked)

    def build(ins, outs, send_sems, recv_sems, local_sems):
        x, y, c = _place()
        sends = []
        for w in range(n):
            for k, (px, py) in enumerate(_other_chips(x, y)):
                sem = 3 * w + k
                sends.append(pltpu.make_async_remote_copy(
                    src_ref=ins[w].at[2 * px + py], dst_ref=outs[w].at[k], send_sem=send_sems.at[sem],
                    recv_sem=recv_sems.at[sem], device_id=(px, py, c), device_id_type=MESH_ID))
        return sends, sends, [], _y_first(sends)

    return _Carried(stacked, [jax.ShapeDtypeStruct((3,) + s.shape[1:], s.dtype) for s in stacked], 3 * n, 0, build)


def _small_copies(small):
    def build(ins, outs, send_sems, recv_sems, local_sems):
        small_ref, all_ref = ins[0], outs[0]
        x, y, c = _place()
        me = 4 * x + 2 * y + c
        sends, recvs = [], []
        for r in range(1, 8):
            px = 1 - x if r & 4 else x
            py = 1 - y if r & 2 else y
            pc = 1 - c if r & 1 else c
            sends.append(pltpu.make_async_remote_copy(
                src_ref=small_ref, dst_ref=all_ref.at[me], send_sem=send_sems.at[r - 1], recv_sem=recv_sems.at[r - 1],
                device_id=(px, py, pc), device_id_type=MESH_ID))
            recvs.append(pltpu.make_async_remote_copy(
                src_ref=small_ref, dst_ref=all_ref.at[4 * px + 2 * py + pc], send_sem=send_sems.at[r - 1],
                recv_sem=recv_sems.at[r - 1], device_id=(px, py, pc), device_id_type=MESH_ID))
        return sends, recvs, [pltpu.make_async_copy(small_ref, all_ref.at[me], local_sems.at[0])]

    return _Carried([small], [jax.ShapeDtypeStruct((8,) + small.shape, small.dtype)], 7, 1, build)


def _gather_by_neighbours(name, shard):
    half = shard.shape[0] // 2
    quarter = half // 2

    def body(in_ref, out_ref, send_sems, recv_sems, local_sem, stage_ref):
        for core in (0, 1):
            @pl.when(lax.axis_index("c") == core)
            def _():
                program(core, in_ref, out_ref, send_sems, recv_sems, local_sem, stage_ref)

    def program(c, in_ref, out_ref, send_sems, recv_sems, local_sem, stage_ref):
        x, y, _ = _place()
        chip = lambda px, py: 2 * px + py
        to_x, to_y, sibling = (1 - x, y, c), (x, 1 - y, c), (x, y, 1 - c)
        x_blk, y_blk, d_blk = chip(1 - x, y), chip(x, 1 - y), chip(1 - x, 1 - y)
        mine, theirs = c * half, (1 - c) * half

        def copy(sem, rows, block, to, src=None):
            place = out_ref.at[block, pl.ds(rows[0], rows[1])]
            return pltpu.make_async_remote_copy(
                src_ref=place if src is None else src, dst_ref=place, send_sem=send_sems.at[sem],
                recv_sem=recv_sems.at[sem], device_id=to, device_id_type=MESH_ID)

        stage = pltpu.make_async_copy(in_ref, stage_ref, local_sem.at[0])
        own = pltpu.make_async_copy(stage_ref, out_ref.at[chip(x, y)], local_sem.at[1])
        my_rows = in_ref.at[pl.ds(mine, half)]
        along_x = dict(send=copy(0, (mine, half), chip(x, y), to_x, src=my_rows),
                       landed=copy(0, (mine, half), x_blk, to_x),
                       onward=[copy(3, (mine + quarter, quarter), x_blk, to_y), copy(4, (mine, half), x_blk, sibling)],
                       diagonal=copy(2, (mine, quarter), d_blk, to_x))
        along_y = dict(send=copy(1, (mine, half), chip(x, y), to_y, src=my_rows),
                       landed=copy(1, (mine, half), y_blk, to_y),
                       onward=[copy(2, (mine, quarter), y_blk, to_x), copy(5, (mine, half), y_blk, sibling)],
                       diagonal=copy(3, (mine + quarter, quarter), d_blk, to_y))
        last = copy(6, (mine, half), d_blk, sibling)

        order = (along_x, along_y) if c == 0 else (along_y, along_x)
        for axis in order:
            axis["send"].start()
        stage.start()
        stage.wait()
        own.start()
        for axis in order:
            axis["landed"].wait_recv()
            for cp in axis["onward"]:
                cp.start()
        for axis in order:
            axis["diagonal"].wait_recv()
        last.start()
        for sem, block in ((4, x_blk), (5, y_blk), (6, d_blk)):
            copy(sem, (theirs, half), block, sibling).wait_recv()
        for cp in [along_x["send"], along_y["send"]] + along_x["onward"] + along_y["onward"] + [last]:
            cp.wait_send()
        own.wait()

    return pl.pallas_call(
        body, name=name, in_specs=[HBM_SPEC], out_specs=HBM_SPEC,
        out_shape=jax.ShapeDtypeStruct((N_CHIPS,) + shard.shape, shard.dtype),
        scratch_shapes=[pltpu.SemaphoreType.DMA((7,)), pltpu.SemaphoreType.DMA((7,)), pltpu.SemaphoreType.DMA((2,)),
                        pltpu.VMEM(shard.shape, shard.dtype)],
    )(shard)


def _copies_alone(name, comm):
    return _call(name, lambda: None, grid=(), in_specs=[], out_specs=[], out_shape=[], args=[], comm=comm)[1]


class _Net:
    def __init__(self, shards):
        self.shards = shards
        self.whole, self.partly, self.own, self.theirs, self.sums, self.other = {}, {}, {}, {}, {}, {}
        x, y, _ = _place()
        self.chip = 2 * x + y

    def fetch(self, names):
        return _fetch_copies([self.shards[n] for n in names])

    def fetched(self, names, got):
        self.partly.update(zip(names, got))

    def relay(self, names):
        return _relay_copies([self.partly[n] for n in names])

    def gathered(self, names, got):
        for n, g in zip(names, got):
            self.whole[n] = g.reshape(-1, g.shape[-1])

    def full(self, name):
        return self.whole[name]

    def exchange(self, grads, swap=()):
        stacked = []
        for n, pieces in grads.items():
            (keep, send), = pieces
            self.own[n] = keep
            stacked.append(send.reshape(N_CHIPS, keep.shape[0] // N_CHIPS, send.shape[-1]))
        return _join(_grad_copies(stacked), self.swap(swap))

    def swap(self, names):
        return _sibling_copies([self.sums[n] for n in names]) if names else None

    def presum_begin(self, name, pieces):
        keep = jnp.concatenate([p[0] for p in pieces], axis=0) if len(pieces) > 1 else pieces[0][0]
        send = jnp.concatenate([p[1] for p in pieces], axis=0) if len(pieces) > 1 else pieces[0][1]
        rows = keep.shape[0] // N_CHIPS
        self.held = keep.reshape(N_CHIPS, rows, keep.shape[-1])
        return _half_rows_copies(send.reshape(N_CHIPS, rows, send.shape[-1]))

    def presum_end(self, name, got):
        x, y, c = _place()
        to_send, self.own[name] = _pre_sum("presum_" + name, self.held, got[0], jnp.stack([c, self.chip]))
        return _grad_copies([to_send])

    def received(self, names, swap, got, carried=None):
        self.theirs.update(zip(names, got[:len(names)]))
        self.other.update(zip(swap, got[len(names):]))
        for n in names:
            (self.sums[n],), more = _partial_sum("sum_" + n, self.own[n], self.theirs[n], self.chip, comm=carried)
        return more


def _half_rows_copies(stacked):
    n, rows = stacked.shape[0], stacked.shape[1] // 2

    def build(ins, outs, send_sems, recv_sems, local_sems):
        x, y, c = _place()
        copies = [pltpu.make_async_remote_copy(
            src_ref=ins[0].at[s, pl.ds((1 - c) * rows, rows)], dst_ref=outs[0].at[s], send_sem=send_sems.at[s],
            recv_sem=recv_sems.at[s], device_id=(x, y, 1 - c), device_id_type=MESH_ID) for s in range(n)]
        return copies, copies, []

    return _Carried([stacked], [jax.ShapeDtypeStruct((n, rows, stacked.shape[2]), stacked.dtype)], n, 0, build)


def _pre_sum(name, held, theirs, core_and_chip):
    n, R, C = held.shape
    half = R // 2
    tr = _row_tile(half)
    per_half = half // tr

    def body(place_ref, h_ref, t_ref, send_ref, own_ref):
        total = h_ref[0] + t_ref[0].astype(F32)
        send_ref[0] = total.astype(send_ref.dtype)

        @pl.when(pl.program_id(1) == place_ref[1])
        def _():
            own_ref[...] = total

    return pl.pallas_call(
        body, name=name,
        grid_spec=pltpu.PrefetchScalarGridSpec(
            num_scalar_prefetch=1, grid=(per_half, n),
            in_specs=[pl.BlockSpec((1, tr, C), lambda i, s, place: (s, place[0] * per_half + i, 0)),
                      pl.BlockSpec((1, tr, C), lambda i, s, place: (s, i, 0))],
            out_specs=[pl.BlockSpec((1, tr, C), lambda i, s, place: (s, i, 0)),
                       pl.BlockSpec((tr, C), lambda i, s, place: (i, 0))]),
        out_shape=[jax.ShapeDtypeStruct((n, half, C), MXU_DTYPE), jax.ShapeDtypeStruct((half, C), F32)],
        compiler_params=_params(("arbitrary", "arbitrary")),
    )(core_and_chip, held, theirs)


def _sibling_copies(parts):
    n = len(parts)

    def build(ins, outs, send_sems, recv_sems, local_sems):
        x, y, c = _place()
        copies = [pltpu.make_async_remote_copy(
            src_ref=ins[w], dst_ref=outs[w], send_sem=send_sems.at[w], recv_sem=recv_sems.at[w],
            device_id=(x, y, 1 - c), device_id_type=MESH_ID) for w in range(n)]
        return copies, copies, []

    return _Carried(parts, [jax.ShapeDtypeStruct(p.shape, p.dtype) for p in parts], n, 0, build)


def _row_tile(rows, most=512, sublanes=16):
    return max(t for t in range(sublanes, min(most, rows // 2) + 1, sublanes) if rows % t == 0)


def _partial_sum(name, own, recv, chip, comm=None):
    _, R, C = recv.shape
    tr = _row_tile(R)

    def body(o_ref, r_ref, p_ref):
        p_ref[...] = ((o_ref[...] + r_ref[0].astype(F32)) + r_ref[1].astype(F32)) + r_ref[2].astype(F32)

    if own.shape[0] != R:
        assert comm is None and own.shape[0] == N_CHIPS * R
        total = pl.pallas_call(
            lambda chip_ref, *refs: body(*refs), name=name,
            grid_spec=pltpu.PrefetchScalarGridSpec(
                num_scalar_prefetch=1, grid=(R // tr,),
                in_specs=[pl.BlockSpec((tr, C), lambda i, chip_ref: (chip_ref[0] * (R // tr) + i, 0)),
                          pl.BlockSpec((3, tr, C), lambda i, chip_ref: (0, i, 0))],
                out_specs=pl.BlockSpec((tr, C), lambda i, chip_ref: (i, 0))),
            out_shape=jax.ShapeDtypeStruct((R, C), F32), compiler_params=_params(("parallel",)),
        )(chip.reshape(1), own, recv)
        return [total], []
    return _call(name, body, grid=(R // tr,),
                 in_specs=[pl.BlockSpec((tr, C), lambda i: (i, 0)), pl.BlockSpec((3, tr, C), lambda i: (0, i, 0))],
                 out_specs=[pl.BlockSpec((tr, C), lambda i: (i, 0))], out_shape=[jax.ShapeDtypeStruct((R, C), F32)],
                 args=[own, recv], semantics=("parallel",), comm=comm)


def _adam_vals(w, g, m, v):
    m = ADAM_B1 * m + (1.0 - ADAM_B1) * g
    v = ADAM_B2 * v + (1.0 - ADAM_B2) * (g * g)
    m_hat = m / (1.0 - ADAM_B1 ** ADAM_STEP)
    v_hat = v / (1.0 - ADAM_B2 ** ADAM_STEP)
    delta = -ADAM_LR * (m_hat / (jnp.sqrt(v_hat) + ADAM_EPS) + ADAM_WD * w)
    return delta, m, v


def _adamw(name, w, m, v, mine, other, comm=None):
    R, C = w.shape
    tr = _row_tile(R)

    def body(w_ref, m_ref, v_ref, s_ref, n_ref, g_ref, d_ref, nm_ref, nv_ref):
        g = s_ref[...] + n_ref[...]
        d, nm, nv = _adam_vals(w_ref[...], g, m_ref[...], v_ref[...])
        g_ref[...], d_ref[...], nm_ref[...], nv_ref[...] = g, d, nm, nv

    spec = pl.BlockSpec((tr, C), lambda i: (i, 0))
    return _call(name, body, grid=(R // tr,), in_specs=[spec] * 5, out_specs=[spec] * 4,
                 out_shape=[jax.ShapeDtypeStruct((R, C), F32)] * 4, args=[w, m, v, mine, other],
                 semantics=("parallel",), comm=comm)


def _adamw_by_halves(name, w, m, v, mine, other, core):
    R, C = w.shape
    tr = _row_tile(R // 2)
    per_half = R // 2 // tr

    def body(c_ref, w_ref, m_ref, v_ref, s_ref, n_ref, g_ref, d_ref, nm_ref, nv_ref):
        g = jnp.where(pl.program_id(0) // per_half == c_ref[0, 0], s_ref[...], n_ref[...])
        d, nm, nv = _adam_vals(w_ref[...], g, m_ref[...], v_ref[...])
        g_ref[...], d_ref[...], nm_ref[...], nv_ref[...] = g, d, nm, nv

    spec = pl.BlockSpec((tr, C), lambda i: (i, 0))
    part = pl.BlockSpec((tr, C), lambda i: (i % per_half, 0))
    return pl.pallas_call(
        body, name=name, grid=(R // tr,),
        in_specs=[pl.BlockSpec(memory_space=pltpu.SMEM), spec, spec, spec, part, part], out_specs=[spec] * 4,
        out_shape=[jax.ShapeDtypeStruct((R, C), F32)] * 4, compiler_params=_params(("parallel",)),
    )(core, w, m, v, mine, other)


SMALL_LAYOUT = dict(norm_mix_g=(0, 1, 1024), b_in=(1, 8, 7424), hgrn_norm_g=(9, 1, 1024), norm_ffn_g=(10, 1, 1024),
                    norm_final_g=(11, 1, 1024), hgrn_lb_logits=(12, 2, 2048), attn_sinks=(14, 1, 16))
SMALL_LOSS_ROW, SMALL_ROWS = 15, 16


def _pack_small(grads, loss):
    rows = [jnp.pad(grads[name].astype(F32).reshape(-1), (0, nrows * D_MODEL - n))
            for name, (_, nrows, n) in SMALL_LAYOUT.items()]
    rows.append(jnp.pad(loss.astype(F32).reshape(1), (0, D_MODEL - 1)))
    return jnp.concatenate(rows).reshape(SMALL_ROWS, D_MODEL)


def _adamw_small(w, m, v, g_all):
    names = list(SMALL_LAYOUT)
    n = len(names)

    def body(a_ref, *refs):
        ins, outs = refs[:3 * n], refs[3 * n:]
        g_all_rows = a_ref[0]
        for dev in range(1, 8):
            g_all_rows = g_all_rows + a_ref[dev]
        for i, name in enumerate(names):
            first, nrows, count = SMALL_LAYOUT[name]
            w_ref, m_ref, v_ref = ins[3 * i:3 * i + 3]
            if w_ref.shape[0] == nrows:
                g = g_all_rows[first:first + nrows, :w_ref.shape[1]]
            else:
                last = count - (nrows - 1) * D_MODEL
                g = jnp.concatenate([g_all_rows[r:r + 1, :] for r in range(first, first + nrows - 1)]
                                    + [g_all_rows[first + nrows - 1:first + nrows, :last]], axis=1)
            d, nm, nv = _adam_vals(w_ref[...], g, m_ref[...], v_ref[...])
            for o_ref, val in zip(outs[4 * i:4 * i + 4], (g, d, nm, nv)):
                o_ref[...] = val
        outs[4 * n][...] = g_all_rows[SMALL_LOSS_ROW:SMALL_LOSS_ROW + 1, 0:1]

    res = pl.pallas_call(
        body, name="adamw_small",
        out_shape=[jax.ShapeDtypeStruct(w[name].shape, F32) for name in names for _ in range(4)]
        + [jax.ShapeDtypeStruct((1, 1), F32)],
    )(g_all, *[t[name] for name in names for t in (w, m, v)])
    return {name: res[4 * i:4 * i + 4] for i, name in enumerate(names)}, res[4 * n]


MATRICES = ("w_in", "w_branch_attn", "w_branch_hgrn", "w_out", "w_ffn_gate", "w_ffn_up", "w_ffn_down")
COLUMN_SHARDED = ("w_in", "w_ffn_gate", "w_ffn_up")
WEIGHTS = ("norm_mix_g", "w_in", "b_in", "attn_sinks", "hgrn_lb_logits", "hgrn_norm_g", "w_branch_attn",
           "w_branch_hgrn", "w_out", "norm_ffn_g", "w_ffn_gate", "w_ffn_up", "w_ffn_down", "norm_final_g")


def kernel(x, norm_mix_g, w_in, b_in, attn_sinks, hgrn_lb_logits, hgrn_norm_g, w_branch_attn, w_branch_hgrn, w_out, norm_ffn_g, w_ffn_gate, w_ffn_up, w_ffn_down, norm_final_g, loss_target, m_norm_mix_g, m_w_in, m_b_in, m_attn_sinks, m_hgrn_lb_logits, m_hgrn_norm_g, m_w_branch_attn, m_w_branch_hgrn, m_w_out, m_norm_ffn_g, m_w_ffn_gate, m_w_ffn_up, m_w_ffn_down, m_norm_final_g, v_norm_mix_g, v_w_in, v_b_in, v_attn_sinks, v_hgrn_lb_logits, v_hgrn_norm_g, v_w_branch_attn, v_w_branch_hgrn, v_w_out, v_norm_ffn_g, v_w_ffn_gate, v_w_ffn_up, v_w_ffn_down, v_norm_final_g):
    given = dict(locals())
    w = {n: given[n] for n in WEIGHTS}
    m = {n: given["m_" + n] for n in WEIGHTS}
    v = {n: given["v_" + n] for n in WEIGHTS}

    block = lambda a, n: jnp.transpose(a[0]) if n in COLUMN_SHARDED else a[0]
    unblock = lambda a, n: (jnp.transpose(a) if n in COLUMN_SHARDED else a)[None]
    net = _Net({n: block(w[n], n).astype(MXU_DTYPE) for n in MATRICES})
    net.gathered(("w_in",), [_gather_by_neighbours("gather_w_in", net.shards["w_in"])])
    vec = dict(norm_mix_g=norm_mix_g, b_in=b_in, attn_sinks=attn_sinks, hgrn_lb_logits=hgrn_lb_logits,
               hgrn_norm_g=hgrn_norm_g, norm_ffn_g=norm_ffn_g, norm_final_g=norm_final_g.reshape(1, D_MODEL))
    loss_part, dx, d_vecs = _local_step(x[0], loss_target[0], vec, net)

    small_all, = net.received(*net.last, carried=_small_copies(_pack_small(d_vecs, loss_part)))
    grads, deltas, new_m, new_v = {}, {}, {}, {}
    for n in ("w_ffn_down", "w_ffn_gate", "w_ffn_up", "w_out", "w_branch_attn", "w_branch_hgrn"):
        res, got = _adamw("adamw_" + n, block(w[n], n), block(m[n], n), block(v[n], n), net.sums[n], net.other[n],
                          comm=net.swap(("w_in",)) if n == "w_ffn_down" else None)
        if n == "w_ffn_down":
            net.other["w_in"], = got
        grads[n], deltas[n], new_m[n], new_v[n] = (unblock(r, n) for r in res)
    n = "w_in"
    res = _adamw_by_halves("adamw_" + n, block(w[n], n), block(m[n], n), block(v[n], n), net.sums[n], net.other[n],
                           _place()[2].reshape(1, 1))
    grads[n], deltas[n], new_m[n], new_v[n] = (unblock(r, n) for r in res)
    rows = lambda t: {n: t[n].reshape(-1, t[n].shape[-1]) for n in SMALL_LAYOUT}
    res, loss = _adamw_small(rows(w), rows(m), rows(v), small_all)
    for n, four in res.items():
        grads[n], deltas[n], new_m[n], new_v[n] = (r.reshape(w[n].shape) for r in four)
    loss = loss.reshape(())
    return (loss, dx[None], *[grads[n] for n in WEIGHTS], *[deltas[n] for n in WEIGHTS],
            *[new_m[n] for n in WEIGHTS], *[new_v[n] for n in WEIGHTS])
```

```python
import functools
import math

import jax
import jax.numpy as jnp
from jax import lax
from jax.experimental import pallas as pl
from jax.experimental.pallas import tpu as pltpu

F32 = jnp.float32
BF16 = jnp.bfloat16
MXU_DTYPE = jnp.bfloat16
SAVED_DTYPE = jnp.bfloat16
MESH_ID = pl.DeviceIdType.MESH

D_MODEL = 1024
HEAD_DIM = 64
Q_HEADS = 16
KV_HEADS = 2
GROUP = Q_HEADS // KV_HEADS
KV_WIDTH = KV_HEADS * HEAD_DIM
ATTN_BLOCK = 128
HGRN_HEADS = 8
HGRN_K = 128
CHUNK = 64
HGRN_TOKENS = 256
FFN = 2816
IN_SPLITS = (1024, 256, 4096, 2048)
EPS = 1e-6
NEG_INF = -1e30
ADAM_LR, ADAM_B1, ADAM_B2, ADAM_EPS, ADAM_WD, ADAM_STEP = 0.001, 0.9, 0.999, 1e-08, 0.01, 10
N_CHIPS = 4
VMEM_LIMIT = 60 * 1024 * 1024
ROW_ALIGN = 16
DW_ROWS = 256


def _params(sem=None):
    return pltpu.CompilerParams(dimension_semantics=sem, vmem_limit_bytes=VMEM_LIMIT)


def _sigmoid(v):
    return 0.5 * jnp.tanh(0.5 * v) + 0.5


def _dot(a, b, dims):
    return lax.dot_general(a.astype(MXU_DTYPE), b.astype(MXU_DTYPE), (dims, ((), ())),
                           preferred_element_type=F32)


def _nn(a, b):
    return _dot(a, b, ((1,), (0,)))


def _nt(a, b):
    return _dot(a, b, ((1,), (1,)))


def _tn(a, b):
    return _dot(a, b, ((0,), (0,)))


HBM_SPEC = pl.BlockSpec(memory_space=pl.ANY)


class _Carried:
    def __init__(self, arrays, out_shapes, n_remote, n_local, build, continued=None, stages=()):
        self.continued = dict(continued or {})
        self.parts = [(len(arrays), len(out_shapes), 3 + len(stages), build)]
        self.arrays, self.out_shapes = list(arrays), list(out_shapes)
        self.scratch = [pltpu.SemaphoreType.DMA((n_remote,)), pltpu.SemaphoreType.DMA((n_remote,)),
                        pltpu.SemaphoreType.DMA((max(n_local, 1),))] + list(stages)

    def __add__(self, other):
        both = _Carried([], [], 1, 0, None)
        both.parts = self.parts + other.parts
        both.arrays, both.out_shapes = self.arrays + other.arrays, self.out_shapes + other.out_shapes
        both.scratch = self.scratch + other.scratch
        both.continued = dict(self.continued)
        both.continued.update({len(self.arrays) + i: len(self.out_shapes) + o for i, o in other.continued.items()})
        return both

    def _built(self, ins, outs, sems):
        for ni, no, ns, build in self.parts:
            yield build(ins[:ni], outs[:no], *sems[:ns])
            ins, outs, sems = ins[ni:], outs[no:], sems[ns:]

    def start(self, ins, outs, sems):
        core = lax.axis_index("c")
        for sends, _, local, *other_order in self._built(ins, outs, sems):
            for cp in local:
                if not isinstance(cp, tuple):
                    cp.start()
            if not other_order:
                for cp in sends:
                    cp.start()
            else:
                @pl.when(core == 0)
                def _():
                    for cp in sends:
                        cp.start()

                @pl.when(core == 1)
                def _():
                    for cp in other_order[0]:
                        cp.start()
            staged = [cp for cp in local if isinstance(cp, tuple)]
            for into, _ in staged:
                into.start()
            for into, out_of in staged:
                into.wait()
                out_of.start()

    def wait(self, ins, outs, sems):
        for sends, recvs, local, *_ in self._built(ins, outs, sems):
            for cp in recvs:
                cp.wait_recv()
            for cp in sends:
                cp.wait_send()
            for cp in local:
                (cp[1] if isinstance(cp, tuple) else cp).wait()


def _join(*comms):
    comms = [c for c in comms if c is not None]
    return functools.reduce(lambda a, b: a + b, comms) if comms else None


def _call(name, body, *, grid, in_specs, out_specs, out_shape, args, scratch=(), semantics=None, comm=None,
          aliases=None):
    n_in, n_out, n_scr = len(in_specs), len(out_specs), len(scratch)
    aliases = aliases or {}
    if comm is None:
        res = pl.pallas_call(body, name=name, grid=grid, in_specs=in_specs, out_specs=out_specs, out_shape=out_shape,
                             scratch_shapes=list(scratch), input_output_aliases=aliases,
                             compiler_params=_params(semantics))(*args)
        return list(res), []
    ci, co = len(comm.arrays), len(comm.out_shapes)
    aliases = dict(aliases)
    aliases.update({n_in + i: n_out + o for i, o in comm.continued.items()})

    def carrying(*refs):
        ins, refs = refs[:n_in], refs[n_in:]
        c_ins, refs = refs[:ci], refs[ci:]
        outs, refs = refs[:n_out], refs[n_out:]
        c_outs, refs = refs[:co], refs[co:]
        scr, sems = refs[:n_scr], refs[n_scr:]
        if not grid:
            comm.start(c_ins, c_outs, sems)
            body(*ins, *outs, *scr)
            comm.wait(c_ins, c_outs, sems)
            return
        first = functools.reduce(jnp.logical_and, [pl.program_id(a) == 0 for a in range(len(grid))])
        last = functools.reduce(jnp.logical_and, [pl.program_id(a) == g - 1 for a, g in enumerate(grid)])

        @pl.when(first)
        def _():
            comm.start(c_ins, c_outs, sems)

        body(*ins, *outs, *scr)

        @pl.when(last)
        def _():
            comm.wait(c_ins, c_outs, sems)

    res = pl.pallas_call(
        carrying, name=name, grid=grid, in_specs=list(in_specs) + [HBM_SPEC] * ci,
        out_specs=list(out_specs) + [HBM_SPEC] * co, out_shape=list(out_shape) + comm.out_shapes,
        scratch_shapes=list(scratch) + comm.scratch, input_output_aliases=aliases,
        compiler_params=_params(("arbitrary",) * len(grid) if grid else None),
    )(*args, *comm.arrays)
    return list(res[:n_out]), list(res[n_out:])


def _weight_grad(name, a, b, *, tm, tk, a_colsum=False, carrying=False, comm=None):
    (T, N), M = b.shape, a.shape[1]
    tk = min(tk, T)
    assert M % tm == 0 and T % tk == 0, (name, M, tm, T, tk)
    ni, nk = M // tm, T // tk

    def body(a_ref, b_ref, *rest):
        keep_ref, send_ref = rest[:2]
        sums_ref = rest[2] if a_colsum else None
        if nk == 1:
            acc = _tn(a_ref[...], b_ref[...])
            keep_ref[...], send_ref[...] = acc, acc.astype(send_ref.dtype)
            if a_colsum:
                sums_ref[...] = jnp.sum(a_ref[...].astype(F32), axis=0, keepdims=True)
            return
        acc_ref = rest[-1]
        k = pl.program_id(1)

        @pl.when(k == 0)
        def _():
            acc_ref[...] = jnp.zeros_like(acc_ref)
            if a_colsum:
                sums_ref[...] = jnp.zeros((1, tm), F32)

        if a_colsum:
            sums_ref[...] += jnp.sum(a_ref[...].astype(F32), axis=0, keepdims=True)
        acc_ref[...] += _tn(a_ref[...], b_ref[...])

        @pl.when(k == nk - 1)
        def _():
            keep_ref[...], send_ref[...] = acc_ref[...], acc_ref[...].astype(send_ref.dtype)

    out_spec = pl.BlockSpec((tm, N), lambda i, k: (i, 0))
    out_shape = [jax.ShapeDtypeStruct((M, N), F32), jax.ShapeDtypeStruct((M, N), MXU_DTYPE)]
    out_specs = [out_spec, out_spec]
    if a_colsum:
        out_shape.append(jax.ShapeDtypeStruct((1, M), F32))
        out_specs.append(pl.BlockSpec((1, tm), lambda i, k: (0, i)))
    res, got = _call(
        name, body, grid=(ni, nk),
        in_specs=[pl.BlockSpec((tk, tm), lambda i, k: (k, i)), pl.BlockSpec((tk, N), lambda i, k: (k, 0))],
        out_specs=out_specs, out_shape=out_shape, scratch=[pltpu.VMEM((tm, N), F32)] if nk > 1 else [],
        args=[a, b], semantics=("parallel", "arbitrary"), comm=comm)
    return (res, got) if carrying else res


def _ffn_fwd(merged, w_out, x, gain, w_gate_t, w_up_t, *, tm, comm=None):
    (T, D), F = x.shape, w_gate_t.shape[0]

    def body(m_ref, wo_ref, x_ref, g_ref, wg_ref, wu_ref, h_ref, u_ref, gate_ref, up_ref, z_ref):
        h = x_ref[...] + _nn(m_ref[...], wo_ref[...])
        h_ref[...] = h
        u = (h * lax.rsqrt(jnp.mean(h * h, axis=-1, keepdims=True) + EPS) * g_ref[...]).astype(u_ref.dtype)
        u_ref[...] = u
        gate, up = _nt(u, wg_ref[...]), _nt(u, wu_ref[...])
        gate_ref[...], up_ref[...] = gate.astype(gate_ref.dtype), up.astype(up_ref.dtype)
        z_ref[...] = (gate * _sigmoid(gate) * up).astype(z_ref.dtype)

    rows = lambda n: pl.BlockSpec((tm, n), lambda i: (i, 0))
    fixed = _fixed_spec
    return _call("ffn_hidden", body, grid=(T // tm,),
                 in_specs=[rows(D), fixed(w_out), rows(D), fixed(gain), fixed(w_gate_t), fixed(w_up_t)],
                 out_specs=[rows(D), rows(D), rows(F), rows(F), rows(F)],
                 out_shape=[jax.ShapeDtypeStruct((T, D), F32), jax.ShapeDtypeStruct((T, D), MXU_DTYPE)]
                 + [jax.ShapeDtypeStruct((T, F), SAVED_DTYPE)] * 2 + [jax.ShapeDtypeStruct((T, F), MXU_DTYPE)],
                 args=[merged, w_out, x, gain, w_gate_t, w_up_t], semantics=("parallel",), comm=comm)


def _in_proj(x, gain, w_in_t, b_in, *, tm, comm=None):
    T, D = x.shape
    bounds = [sum(IN_SPLITS[:i]) for i in range(len(IN_SPLITS) + 1)]

    def body(x_ref, g_ref, w_ref, b_ref, u_ref, *piece_refs):
        xv = x_ref[...]
        r = lax.rsqrt(jnp.mean(xv * xv, axis=-1, keepdims=True) + EPS)
        u = (xv * r * g_ref[...]).astype(u_ref.dtype)
        u_ref[...] = u
        for o_ref, lo, hi in zip(piece_refs, bounds[:-1], bounds[1:]):
            o_ref[...] = (_nt(u, w_ref[lo:hi, :]) + b_ref[:, lo:hi]).astype(o_ref.dtype)

    rows = lambda n: pl.BlockSpec((tm, n), lambda i: (i, 0))
    fixed = _fixed_spec
    dtypes = (MXU_DTYPE, MXU_DTYPE, F32, F32)
    return _call("in_proj", body, grid=(T // tm,),
                 in_specs=[rows(D), fixed(gain), fixed(w_in_t), fixed(b_in)],
                 out_specs=[rows(D)] + [rows(n) for n in IN_SPLITS],
                 out_shape=[jax.ShapeDtypeStruct((T, D), MXU_DTYPE)]
                 + [jax.ShapeDtypeStruct((T, n), dt) for n, dt in zip(IN_SPLITS, dtypes)],
                 args=[x, gain, w_in_t, b_in], semantics=("parallel",), comm=comm)


def _row_spec(tm, n):
    return pl.BlockSpec((tm, n), lambda i: (i, 0))


def _fixed_spec(a):
    return pl.BlockSpec(a.shape, lambda i: (0,) * a.ndim, pipeline_mode=pl.Buffered(1))


def _partials_spec(n):
    return pl.BlockSpec((8, n), lambda i: (i, 0))


def _row_parts(tm, parts):
    assert tm % parts == 0, (tm, parts)
    return [slice(p * (tm // parts), (p + 1) * (tm // parts)) for p in range(parts)]


def _ffn_tail(z, w_down, h1, target, gain, gate, up, *, tm, parts=2):
    (T, F), D = z.shape, h1.shape[1]

    def body(z_ref, w_ref, h_ref, t_ref, g_ref, gate_ref, up_ref, dh_ref, dhb_ref, dgate_ref, dup_ref, dg_ref, l_ref):
        part, dgain = 0.0, 0.0
        pieces = _row_parts(tm, parts)
        h2s = [h_ref[rows, :] + _nn(z_ref[rows, :], w_ref[...]) for rows in pieces]
        for rows, h2 in zip(pieces, h2s):
            r = lax.rsqrt(jnp.mean(h2 * h2, axis=-1, keepdims=True) + EPS)
            xhat = h2 * r
            err = xhat * g_ref[...] - t_ref[rows, :]
            part += 0.5 * jnp.sum(jnp.sum(err * err, axis=-1, keepdims=True), axis=0, keepdims=True) / D
            dy = err / D
            dxh = dy * g_ref[...]
            dh2 = r * (dxh - xhat * jnp.mean(dxh * xhat, axis=-1, keepdims=True))
            dh_ref[rows, :] = dh2
            dhb = dh2.astype(dhb_ref.dtype)
            dhb_ref[rows, :] = dhb
            dgain += jnp.sum(dy * xhat, axis=0, keepdims=True)
            dz = _nt(dhb, w_ref[...])
            gv, upv = gate_ref[rows, :].astype(F32), up_ref[rows, :].astype(F32)
            s = _sigmoid(gv)
            dgate_ref[rows, :] = (dz * upv * (s * (1.0 + gv * (1.0 - s)))).astype(dgate_ref.dtype)
            dup_ref[rows, :] = (dz * (gv * s)).astype(dup_ref.dtype)
        dg_ref[...] = jnp.broadcast_to(dgain, dg_ref.shape)
        l_ref[...] = jnp.broadcast_to(part, l_ref.shape)

    low = lambda n: jax.ShapeDtypeStruct((T, n), MXU_DTYPE)
    part = jax.ShapeDtypeStruct((8 * (T // tm), D), F32)
    return pl.pallas_call(
        body, name="ffn_tail", grid=(T // tm,),
        in_specs=[_row_spec(tm, F), _fixed_spec(w_down), _row_spec(tm, D), _row_spec(tm, D), _fixed_spec(gain),
                  _row_spec(tm, F), _row_spec(tm, F)],
        out_specs=[_row_spec(tm, D), _row_spec(tm, D), _row_spec(tm, F), _row_spec(tm, F), _partials_spec(D),
                   _partials_spec(D)],
        out_shape=[jax.ShapeDtypeStruct((T, D), F32), low(D), low(F), low(F), part, part],
        compiler_params=_params(("parallel",)),
    )(z, w_down, h1, target, gain, gate, up)


def _ffn_in_bwd(dgate, dup, w_gate_t, w_up_t, h1, gain, dres, *, tm, comm=None):
    (T, F), D = dgate.shape, h1.shape[1]

    def body(dg_ref, du_ref, wg_ref, wu_ref, h_ref, g_ref, r_ref, dh_ref, dhb_ref, dgain_ref):
        d_u2 = _nn(dg_ref[...], wg_ref[...]) + _nn(du_ref[...], wu_ref[...])
        dx, dgain = _rmsnorm_bwd_vals(d_u2, h_ref[...], g_ref[...])
        dh = r_ref[...] + dx
        dh_ref[...] = dh
        dhb_ref[...] = dh.astype(dhb_ref.dtype)
        dgain_ref[...] = jnp.broadcast_to(dgain, dgain_ref.shape)

    return _call("d_ffn_in", body, grid=(T // tm,),
                 in_specs=[_row_spec(tm, F), _row_spec(tm, F), _fixed_spec(w_gate_t), _fixed_spec(w_up_t),
                           _row_spec(tm, D), _fixed_spec(gain), _row_spec(tm, D)],
                 out_specs=[_row_spec(tm, D), _row_spec(tm, D), _partials_spec(D)],
                 out_shape=[jax.ShapeDtypeStruct((T, D), F32), jax.ShapeDtypeStruct((T, D), MXU_DTYPE),
                            jax.ShapeDtypeStruct((8 * (T // tm), D), F32)],
                 args=[dgate, dup, w_gate_t, w_up_t, h1, gain, dres], semantics=("parallel",), comm=comm)


def _in_proj_bwd(pieces, w_in_t, x, gain, dres, *, tm, comm=None):
    T, D = x.shape
    n = len(pieces)

    def body(*refs):
        dps, (w_ref, x_ref, g_ref, r_ref, dx_ref, dgain_ref) = refs[:n], refs[n:]
        d_u = None
        for dp_ref, (dp, first) in zip(dps, pieces):
            term = _nn(dp_ref[...], w_ref[first:first + dp.shape[1], :])
            d_u = term if d_u is None else d_u + term
        dx, dgain = _rmsnorm_bwd_vals(d_u, x_ref[...], g_ref[...])
        dx_ref[...] = r_ref[...] + dx
        dgain_ref[...] = jnp.broadcast_to(dgain, dgain_ref.shape)

    return _call("d_u", body, grid=(T // tm,),
                 in_specs=[_row_spec(tm, dp.shape[1]) for dp, _ in pieces]
                 + [_fixed_spec(w_in_t), _row_spec(tm, D), _fixed_spec(gain), _row_spec(tm, D)],
                 out_specs=[_row_spec(tm, D), _partials_spec(D)],
                 out_shape=[jax.ShapeDtypeStruct((T, D), F32), jax.ShapeDtypeStruct((8 * (T // tm), D), F32)],
                 args=[dp for dp, _ in pieces] + [w_in_t, x, gain, dres], semantics=("parallel",), comm=comm)


def _merge_fwd(y_a, y_b, w_a, w_b, gates, *, tm, comm=None):
    T, D = y_a.shape

    def body(ya_ref, yb_ref, wa_ref, wb_ref, ga_ref, gb_ref, pa_ref, pb_ref, m_ref):
        pa, pb = _nn(ya_ref[...], wa_ref[...]), _nn(yb_ref[...], wb_ref[...])
        pa_ref[...], pb_ref[...] = pa.astype(pa_ref.dtype), pb.astype(pb_ref.dtype)
        m_ref[...] = (_sigmoid(ga_ref[...]) * pa + _sigmoid(gb_ref[...]) * pb).astype(m_ref.dtype)

    rows = pl.BlockSpec((tm, D), lambda i: (i, 0))
    whole = pl.BlockSpec((D, D), lambda i: (0, 0), pipeline_mode=pl.Buffered(1))
    return _call("branch_merge", body, grid=(T // tm,),
                 in_specs=[rows, rows, whole, whole, rows, pl.BlockSpec((tm, D), lambda i: (i, 1))],
                 out_specs=[rows] * 3,
                 out_shape=[jax.ShapeDtypeStruct((T, D), SAVED_DTYPE)] * 2 + [jax.ShapeDtypeStruct((T, D), MXU_DTYPE)],
                 args=[y_a, y_b, w_a, w_b, gates, gates], semantics=("parallel",), comm=comm)


def _merge_bwd(dh, w_out, w_a, w_b, p_a, p_b, gates, *, tm, d_in_width, first):
    T, D = dh.shape

    def body(dh_ref, wo_ref, wa_ref, wb_ref, pa_ref, pb_ref, ga_ref, gb_ref, dpa_ref, dpb_ref, dya_ref, dyb_ref,
             din_ref):
        dm = _nt(dh_ref[...], wo_ref[...])
        sa, sb = _sigmoid(ga_ref[...]), _sigmoid(gb_ref[...])
        dpa, dpb = (dm * sa).astype(dpa_ref.dtype), (dm * sb).astype(dpb_ref.dtype)
        dpa_ref[...], dpb_ref[...] = dpa, dpb
        din_ref[:, :D] = (dm * pa_ref[...].astype(F32) * sa * (1.0 - sa)).astype(din_ref.dtype)
        din_ref[:, D:] = (dm * pb_ref[...].astype(F32) * sb * (1.0 - sb)).astype(din_ref.dtype)
        dya_ref[...] = _nt(dpa, wa_ref[...]).astype(dya_ref.dtype)
        dyb_ref[...] = _nt(dpb, wb_ref[...]).astype(dyb_ref.dtype)

    rows = pl.BlockSpec((tm, D), lambda i: (i, 0))
    whole = pl.BlockSpec((D, D), lambda i: (0, 0), pipeline_mode=pl.Buffered(1))
    low = jax.ShapeDtypeStruct((T, D), MXU_DTYPE)
    return pl.pallas_call(
        body, name="d_branch_merge", grid=(T // tm,),
        in_specs=[rows, whole, whole, whole, rows, rows, rows, pl.BlockSpec((tm, D), lambda i: (i, 1))],
        out_specs=[rows] * 4 + [pl.BlockSpec((pl.Element(tm), pl.Element(2 * D)), lambda i: (
            pl.multiple_of(i * tm, ROW_ALIGN), first))],
        out_shape=[low] * 3 + [jax.ShapeDtypeStruct((T, D), F32), jax.ShapeDtypeStruct((T, d_in_width), MXU_DTYPE)],
        compiler_params=_params(("parallel",)),
    )(dh, w_out, w_a, w_b, p_a, p_b, gates, gates)


def _colsum_partials(p):
    return jnp.sum(p.reshape(-1, 8, p.shape[-1])[:, 0, :], axis=0, keepdims=True)


def _rmsnorm_bwd_vals(dy, xin, g):
    rstd = lax.rsqrt(jnp.mean(xin * xin, axis=-1, keepdims=True) + EPS)
    xhat = xin * rstd
    dg = jnp.sum(dy * xhat, axis=0, keepdims=True)
    dxh = dy * g
    dx = rstd * (dxh - xhat * jnp.mean(dxh * xhat, axis=-1, keepdims=True))
    return dx, dg


ATTN_SCALE = 1.0 / math.sqrt(HEAD_DIM)
GROUP_LANES = GROUP * ATTN_BLOCK
PAIR = 2 * HEAD_DIM


def _attn_mask():
    kj = lax.broadcasted_iota(jnp.int32, (ATTN_BLOCK, GROUP_LANES), 0)
    qi = lax.broadcasted_iota(jnp.int32, (ATTN_BLOCK, GROUP_LANES), 1) & (ATTN_BLOCK - 1)
    return kj <= qi


def _heads_transposed(ref, g, scale=None):
    parts = []
    for a in range(GROUP // 2):
        lo = (g * GROUP // 2 + a) * PAIR
        pair = ref[:, lo:lo + PAIR].astype(F32)
        pair = (pair if scale is None else pair * scale).T
        parts += [pair[:HEAD_DIM], pair[HEAD_DIM:]]
    return jnp.concatenate(parts, axis=1).astype(MXU_DTYPE)


def _heads_back(ref, g, vt):
    for a in range(GROUP // 2):
        lo = (g * GROUP // 2 + a) * PAIR
        pair = jnp.concatenate([vt[:, (2 * a) * ATTN_BLOCK:(2 * a + 1) * ATTN_BLOCK],
                                vt[:, (2 * a + 1) * ATTN_BLOCK:(2 * a + 2) * ATTN_BLOCK]], axis=0)
        ref[:, lo:lo + PAIR] = pair.T.astype(ref.dtype)


def _kv_parts(kv_ref, g):
    ks = slice(g * HEAD_DIM, (g + 1) * HEAD_DIM)
    vs = slice(KV_WIDTH + g * HEAD_DIM, KV_WIDTH + (g + 1) * HEAD_DIM)
    return kv_ref[:, ks].astype(MXU_DTYPE), kv_ref[:, vs].astype(MXU_DTYPE)


def _sink_rows(sinks):
    return jnp.repeat(sinks.reshape(KV_HEADS, GROUP), ATTN_BLOCK, axis=1)


def _attn_fwd(pq, pkv, sinks, comm=None):
    T = pq.shape[0]
    nb = T // ATTN_BLOCK

    def body(q_ref, kvc_ref, kvp_ref, s_ref, y_ref, lse_ref):
        mask_c = _attn_mask()
        has_prev = pl.program_id(0) > 0
        early = []
        for g in range(KV_HEADS):
            (kc, vc), (kp, vp) = _kv_parts(kvc_ref, g), _kv_parts(kvp_ref, g)
            qt = _heads_transposed(q_ref, g, ATTN_SCALE)
            early.append((vc, vp, jnp.where(mask_c, _nn(kc, qt), jnp.where(has_prev, _nn(kp, qt), NEG_INF))))
        for g, (vc, vp, s) in enumerate(early):
            sink = s_ref[g:g + 1, :]
            m = jnp.maximum(jnp.max(s, axis=0, keepdims=True), sink)
            p = jnp.exp(s - m)
            den = jnp.sum(p, axis=0, keepdims=True) + jnp.exp(sink - m)
            pc = jnp.where(mask_c, p, 0.0)
            _heads_back(y_ref, g, (_tn(vc, pc) + _tn(vp, p - pc)) / den)
            lse = m + jnp.log(den)
            for i in range(GROUP):
                lse_ref[g * GROUP + i:g * GROUP + i + 1, :] = lse[:, i * ATTN_BLOCK:(i + 1) * ATTN_BLOCK]

    return _call(
        "attn_fwd", body, grid=(nb,),
        in_specs=[pl.BlockSpec((ATTN_BLOCK, D_MODEL), lambda n: (n, 0)),
                  pl.BlockSpec((ATTN_BLOCK, 2 * KV_WIDTH), lambda n: (n, 0)),
                  pl.BlockSpec((ATTN_BLOCK, 2 * KV_WIDTH), lambda n: (jnp.maximum(n - 1, 0), 0)),
                  pl.BlockSpec((KV_HEADS, GROUP_LANES), lambda n: (0, 0))],
        out_specs=[pl.BlockSpec((ATTN_BLOCK, D_MODEL), lambda n: (n, 0)),
                   pl.BlockSpec((Q_HEADS, ATTN_BLOCK), lambda n: (0, n))],
        out_shape=[jax.ShapeDtypeStruct((T, D_MODEL), MXU_DTYPE), jax.ShapeDtypeStruct((Q_HEADS, T), F32)],
        args=[pq, pkv, pkv, _sink_rows(sinks)], semantics=("parallel",), comm=comm)


def _attn_bwd(pq, pkv, sinks, lse, dy, d_in, comm=None):
    T = pq.shape[0]
    nb = T // ATTN_BLOCK
    cur = lambda n: (jnp.minimum(n, nb - 1), 0)
    done = D_MODEL + 2 * KV_WIDTH

    def body(q_ref, kvc_ref, kvp_ref, s_ref, lse_ref, dy_ref, _, out_ref, ds_ref, carry, top, bot, dq_ref):
        n = pl.program_id(0)

        @pl.when(n == 0)
        def _():
            carry[...] = jnp.zeros_like(carry)
            dq_ref[...] = jnp.zeros_like(dq_ref)
            ds_ref[...] = jnp.zeros_like(ds_ref)

        out_ref[:, :D_MODEL] = dq_ref[...]

        @pl.when(n < nb)
        def _():
            mask_c = _attn_mask()
            valid = jnp.logical_or(mask_c, n > 0)
            early = []
            for g in range(KV_HEADS):
                (kc, vc), (kp, vp) = _kv_parts(kvc_ref, g), _kv_parts(kvp_ref, g)
                qt = _heads_transposed(q_ref, g, ATTN_SCALE)
                dot = _heads_transposed(dy_ref, g)
                early.append((kc, kp, qt, dot, jnp.where(mask_c, _nn(kc, qt), _nn(kp, qt)),
                              jnp.where(mask_c, _nn(vc, dot), _nn(vp, dot))))
            for g, (kc, kp, qt, dot, s, dp) in enumerate(early):
                ks = slice(g * HEAD_DIM, (g + 1) * HEAD_DIM)
                vs = slice(KV_WIDTH + g * HEAD_DIM, KV_WIDTH + (g + 1) * HEAD_DIM)
                lse = jnp.concatenate([lse_ref[g * GROUP + i:g * GROUP + i + 1, :] for i in range(GROUP)], axis=1)
                p = jnp.where(valid, jnp.exp(s - lse), 0.0)
                delta = jnp.sum(p * dp, axis=0, keepdims=True)
                ds = p * (dp - delta)
                ds_c, p_c = jnp.where(mask_c, ds, 0.0), jnp.where(mask_c, p, 0.0)
                ds_p, p_p = ds - ds_c, p - p_c
                _heads_back(dq_ref, g, (_tn(kc, ds_c) + _tn(kp, ds_p)) * ATTN_SCALE)
                bot[:, ks], bot[:, vs] = _nt(ds_c, qt), _nt(p_c, dot)
                top[:, ks], top[:, vs] = _nt(ds_p, qt), _nt(p_p, dot)
                ds_ref[g:g + 1, :] -= jnp.exp(s_ref[g:g + 1, :] - lse) * delta
            out_ref[:, D_MODEL:] = (carry[...] + top[...]).astype(out_ref.dtype)
            carry[...] = bot[...]

        @pl.when(n == nb)
        def _():
            out_ref[:, D_MODEL:] = carry[...].astype(out_ref.dtype)

    return _call(
        "attn_bwd", body, grid=(nb + 1,),
        in_specs=[pl.BlockSpec((ATTN_BLOCK, D_MODEL), cur),
                  pl.BlockSpec((ATTN_BLOCK, 2 * KV_WIDTH), cur),
                  pl.BlockSpec((ATTN_BLOCK, 2 * KV_WIDTH), lambda n: (jnp.maximum(jnp.minimum(n, nb - 1) - 1, 0), 0)),
                  pl.BlockSpec((KV_HEADS, GROUP_LANES), lambda n: (0, 0)),
                  pl.BlockSpec((Q_HEADS, ATTN_BLOCK), lambda n: (0, jnp.minimum(n, nb - 1))),
                  pl.BlockSpec((ATTN_BLOCK, D_MODEL), cur), HBM_SPEC],
        out_specs=[pl.BlockSpec((ATTN_BLOCK, done), lambda n: (jnp.maximum(n - 1, 0), 0)),
                   pl.BlockSpec((KV_HEADS, GROUP_LANES), lambda n: (0, 0))],
        out_shape=[jax.ShapeDtypeStruct(d_in.shape, d_in.dtype), jax.ShapeDtypeStruct((KV_HEADS, GROUP_LANES), F32)],
        scratch=[pltpu.VMEM((ATTN_BLOCK, 2 * KV_WIDTH), F32)] * 3 + [pltpu.VMEM((ATTN_BLOCK, D_MODEL), MXU_DTYPE)],
        args=[pq, pkv, pkv, _sink_rows(sinks), lse, dy, d_in], semantics=("arbitrary",), comm=comm, aliases={6: 0})


def _lower_bound(l):
    m = jnp.maximum(l[0:1], l[1:2])
    e0, e1 = jnp.exp(l[0:1] - m), jnp.exp(l[1:2] - m)
    return e0 / (e0 + e1)


def _tri(lower):
    r = lax.broadcasted_iota(jnp.int32, (CHUNK, CHUNK), 0)
    c = lax.broadcasted_iota(jnp.int32, (CHUNK, CHUNK), 1)
    return (r >= c) if lower else (c >= r)


def _chunk_sum(mask, v):
    ones = mask.astype(BF16)
    hi = v.astype(BF16)
    rest = v - hi.astype(F32)
    mid = rest.astype(BF16)
    lo = (rest - mid.astype(F32)).astype(BF16)
    part = lambda t: lax.dot_general(ones, t, (((1,), (0,)), ((), ())), preferred_element_type=F32)
    return part(hi) + part(mid) + part(lo)


def _hgrn_chunk_inputs(hq, hf, lb, causal):
    half_t = 0.5 * jnp.tanh(0.5 * hf)
    sg, sgn = 0.5 + half_t, 0.5 - half_t
    f = lb + (1.0 - lb) * sg
    kk = (1.0 - lb) * sgn
    sq = _sigmoid(hq)
    q = hq * sq
    b = _chunk_sum(causal, jnp.log(f))
    bm, bl = b[CHUNK // 2 - 1:CHUNK // 2, :], b[CHUNK - 1:CHUNK, :]
    e_qm, e_km = jnp.exp(b - bm), jnp.exp(bm - b)
    e_qs, e_kl = e_qm * jnp.exp(bm), e_km * jnp.exp(bl - bm)
    return dict(sg=sg, sgn=sgn, f=f, kk=kk, sq=sq, q=q, e_qm=e_qm, e_km=e_km, e_qs=e_qs, e_kl=e_kl,
                qm=q * e_qm, km=kk * e_km, qs=q * e_qs, kl=kk * e_kl, el=jnp.exp(bl))


def _hgrn_fwd(ph, lb_logits, norm_g, comm=None):
    T = ph.shape[0]
    nblk, cpb = T // HGRN_TOKENS, HGRN_TOKENS // CHUNK
    col = lambda c: pl.BlockSpec((HGRN_TOKENS, D_MODEL), functools.partial(lambda i, c: (i, c), c=c))

    def body(hq_ref, hf_ref, hi_ref, hg_ref, l_ref, ng_ref, y_ref, o_ref, st_ref, s_ref):
        @pl.when(pl.program_id(0) == 0)
        def _():
            s_ref[...] = jnp.zeros_like(s_ref)

        lb = _lower_bound(l_ref[...])
        causal = _tri(True)
        for c in range(cpb):
            rows = slice(c * CHUNK, (c + 1) * CHUNK)
            t = _hgrn_chunk_inputs(hq_ref[rows, :], hf_ref[rows, :], lb, causal)
            qm, km, qs, kl = (t[n].astype(MXU_DTYPE) for n in ("qm", "km", "qs", "kl"))
            v = hi_ref[rows, :].astype(MXU_DTYPE)
            heads = [slice(h * HGRN_K, (h + 1) * HGRN_K) for h in range(HGRN_HEADS)]
            a_all = [jnp.where(causal, _nt(qm[:, ls], km[:, ls]), 0.0).astype(MXU_DTYPE) for ls in heads]
            for h, ls in enumerate(heads):
                st = s_ref[h]
                st_ref[c, ls, :] = st
                o_ref[rows, ls] = _nn(a_all[h], v[:, ls]) + _nt(qs[:, ls], st)
                s_ref[h] = t["el"][:, ls] * st + _tn(v[:, ls], kl[:, ls])
        for h in range(HGRN_HEADS):
            ls = slice(h * HGRN_K, (h + 1) * HGRN_K)
            o = o_ref[:, ls]
            r = lax.rsqrt(jnp.mean(o * o, axis=-1, keepdims=True) + EPS)
            y_ref[:, ls] = (o * r * ng_ref[:, ls] * _sigmoid(hg_ref[:, ls])).astype(y_ref.dtype)

    return _call(
        "hgrn_fwd", body, grid=(nblk,),
        in_specs=[col(0), col(1), col(2), col(3),
                  pl.BlockSpec((2, D_MODEL), lambda i: (0, 0)), pl.BlockSpec((1, D_MODEL), lambda i: (0, 0))],
        out_specs=[pl.BlockSpec((HGRN_TOKENS, D_MODEL), lambda i: (i, 0)),
                   pl.BlockSpec((HGRN_TOKENS, D_MODEL), lambda i: (i, 0)),
                   pl.BlockSpec((cpb, D_MODEL, HGRN_K), lambda i: (i, 0, 0))],
        out_shape=[jax.ShapeDtypeStruct((T, D_MODEL), MXU_DTYPE), jax.ShapeDtypeStruct((T, D_MODEL), F32),
                   jax.ShapeDtypeStruct((T // CHUNK, D_MODEL, HGRN_K), F32)],
        scratch=[pltpu.VMEM((HGRN_HEADS, HGRN_K, HGRN_K), F32)],
        args=[ph, ph, ph, ph, lb_logits, norm_g], semantics=("arbitrary",), comm=comm)


def _hgrn_bwd(ph, o_raw, states, dy, lb_logits, norm_g, d_in, first, comm=None):
    T = ph.shape[0]
    nblk, cpb = T // HGRN_TOKENS, HGRN_TOKENS // CHUNK
    rev = lambda i: nblk - 1 - i
    col = lambda c: pl.BlockSpec((HGRN_TOKENS, D_MODEL), functools.partial(lambda i, c: (rev(i), c), c=c))
    tok = pl.BlockSpec((HGRN_TOKENS, D_MODEL), lambda i: (rev(i), 0))

    def body(hq_ref, hf_ref, hi_ref, hg_ref, o_ref, st_ref, dy_ref, l_ref, ng_ref, _,
             dph_ref, dng_ref, dl_ref, dst_ref, dlb_ref, do_s, dqm_s, dkm_s, dqs_s, dkl_s, dv_s, del_s):
        i = pl.program_id(0)

        @pl.when(i == 0)
        def _():
            dst_ref[...] = jnp.zeros_like(dst_ref)
            dlb_ref[...] = jnp.zeros_like(dlb_ref)
            dng_ref[...] = jnp.zeros_like(dng_ref)

        lb = _lower_bound(l_ref[...])
        causal, anti = _tri(True), _tri(False)
        row = lax.broadcasted_iota(jnp.int32, (CHUNK, D_MODEL), 0)
        for c in reversed(range(cpb)):
            rows = slice(c * CHUNK, (c + 1) * CHUNK)
            hq = hq_ref[rows, :]
            t = _hgrn_chunk_inputs(hq, hf_ref[rows, :], lb, causal)
            sgg = _sigmoid(hg_ref[rows, :])
            dyv = dy_ref[rows, :]
            for h in range(HGRN_HEADS):
                ls = slice(h * HGRN_K, (h + 1) * HGRN_K)
                o = o_ref[rows, ls]
                r = lax.rsqrt(jnp.mean(o * o, axis=-1, keepdims=True) + EPS)
                nrm = o * r
                g_h = sgg[:, ls]
                dph_ref[rows, 3 * D_MODEL + h * HGRN_K:3 * D_MODEL + (h + 1) * HGRN_K] = (
                    dyv[:, ls] * nrm * ng_ref[:, ls] * g_h * (1.0 - g_h)).astype(dph_ref.dtype)
                dyg = dyv[:, ls] * g_h
                dng_ref[:, ls] += jnp.sum(dyg * nrm, axis=0, keepdims=True)
                dn = dyg * ng_ref[:, ls]
                do_s[:, ls] = r * (dn - nrm * jnp.mean(dn * nrm, axis=-1, keepdims=True))
            qm, km, qs, kl = (t[n].astype(MXU_DTYPE) for n in ("qm", "km", "qs", "kl"))
            v = hi_ref[rows, :].astype(MXU_DTYPE)
            do = do_s[...].astype(MXU_DTYPE)
            heads = [slice(h * HGRN_K, (h + 1) * HGRN_K) for h in range(HGRN_HEADS)]
            a_all = [jnp.where(causal, _nt(qm[:, ls], km[:, ls]), 0.0).astype(MXU_DTYPE) for ls in heads]
            da_all = [jnp.where(causal, _nt(do[:, ls], v[:, ls]), 0.0).astype(MXU_DTYPE) for ls in heads]
            for h, ls in enumerate(heads):
                st = st_ref[c, ls, :]
                dst = dst_ref[h]
                a, da = a_all[h], da_all[h]
                dv_s[:, ls] = _tn(a, do[:, ls]) + _nt(kl[:, ls], dst)
                dkl_s[:, ls] = _nn(v[:, ls], dst)
                dqs_s[:, ls] = _nn(do[:, ls], st)
                del_s[:, ls] = jnp.sum(dst * st, axis=0, keepdims=True)
                dst_ref[h] = _tn(do[:, ls], qs[:, ls]) + t["el"][:, ls] * dst
                dqm_s[:, ls] = _nn(da, km[:, ls])
                dkm_s[:, ls] = _tn(da, qm[:, ls])
            dqm, dkm, dqs, dkl = dqm_s[...], dkm_s[...], dqs_s[...], dkl_s[...]
            dq = dqm * t["e_qm"] + dqs * t["e_qs"]
            dk = dkm * t["e_km"] + dkl * t["e_kl"]
            t_qm, t_km, t_kl = dqm * t["qm"], dkm * t["km"], dkl * t["kl"]
            db = t_qm - t_km + dqs * t["qs"] - t_kl
            db_mid = jnp.sum(t_km - t_qm, axis=0, keepdims=True)
            db_last = jnp.sum(t_kl, axis=0, keepdims=True) + del_s[...] * t["el"]
            db = db + jnp.where(row == CHUNK // 2 - 1, db_mid, 0.0) + jnp.where(row == CHUNK - 1, db_last, 0.0)
            dlogf = _chunk_sum(anti, db)
            sq, sg, sgn, f = t["sq"], t["sg"], t["sgn"], t["f"]
            dph_ref[rows, 0:D_MODEL] = (dq * (sq * (1.0 + hq * (1.0 - sq)))).astype(dph_ref.dtype)
            dph_ref[rows, D_MODEL:2 * D_MODEL] = (
                dlogf * (1.0 - lb) * sg * (1.0 - sg) / f - dk * (1.0 - lb) * sgn * (1.0 - sgn)).astype(dph_ref.dtype)
            dph_ref[rows, 2 * D_MODEL:3 * D_MODEL] = dv_s[...].astype(dph_ref.dtype)
            dlb_ref[...] += jnp.sum(dlogf * (1.0 - sg) / f - dk * sgn, axis=0, keepdims=True)

        @pl.when(i == nblk - 1)
        def _():
            dl0 = dlb_ref[...] * lb * (1.0 - lb)
            dl_ref[0:1, :] = dl0
            dl_ref[1:2, :] = -dl0

    wide = pltpu.VMEM((CHUNK, D_MODEL), F32)
    return _call(
        "hgrn_bwd", body, grid=(nblk,),
        in_specs=[col(0), col(1), col(2), col(3), tok,
                  pl.BlockSpec((cpb, D_MODEL, HGRN_K), lambda i: (rev(i), 0, 0)), tok,
                  pl.BlockSpec((2, D_MODEL), lambda i: (0, 0)), pl.BlockSpec((1, D_MODEL), lambda i: (0, 0)), HBM_SPEC],
        out_specs=[pl.BlockSpec((pl.Element(HGRN_TOKENS), pl.Element(4 * D_MODEL)), lambda i: (
                       pl.multiple_of(rev(i) * HGRN_TOKENS, ROW_ALIGN), first)),
                   pl.BlockSpec((1, D_MODEL), lambda i: (0, 0)), pl.BlockSpec((2, D_MODEL), lambda i: (0, 0))],
        out_shape=[jax.ShapeDtypeStruct(d_in.shape, d_in.dtype), jax.ShapeDtypeStruct((1, D_MODEL), F32),
                   jax.ShapeDtypeStruct((2, D_MODEL), F32)],
        scratch=[pltpu.VMEM((HGRN_HEADS, HGRN_K, HGRN_K), F32), pltpu.VMEM((1, D_MODEL), F32),
                 wide, wide, wide, wide, wide, wide, pltpu.VMEM((1, D_MODEL), F32)],
        args=[ph, ph, ph, ph, o_raw, states, dy, lb_logits, norm_g, d_in], semantics=("arbitrary",), comm=comm,
        aliases={9: 0})


def _local_step(x, target, vec, net):
    T, D = x.shape
    norm_mix_g, b_in, sinks, lb_logits = vec["norm_mix_g"], vec["b_in"], vec["attn_sinks"], vec["hgrn_lb_logits"]
    hgrn_norm_g, norm_ffn_g, norm_final_g = vec["hgrn_norm_g"], vec["norm_ffn_g"], vec["norm_final_g"]
    w_in = net.full("w_in")
    o_q, o_kv, o_h, o_g = (sum(IN_SPLITS[:i]) for i in range(4))
    TM = 512

    first = ("w_branch_attn", "w_branch_hgrn", "w_ffn_down")
    (u, pq, pkv, ph, pg), got = _in_proj(x, norm_mix_g, w_in, b_in, tm=256, comm=net.fetch(first))
    net.fetched(first, got)
    second = ("w_ffn_gate", "w_out")
    (y_attn, lse), got = _attn_fwd(pq, pkv, sinks, comm=_join(net.relay(first), net.fetch(second)))
    net.gathered(first, got[:len(first)])
    net.fetched(second, got[len(first):])
    third = ("w_ffn_up",)
    (y_hgrn, o_raw, states), got = _hgrn_fwd(ph, lb_logits, hgrn_norm_g,
                                             comm=_join(net.relay(second), net.fetch(third)))
    net.gathered(second, got[:len(second)])
    net.fetched(third, got[len(second):])
    w_ba, w_bh = net.full("w_branch_attn"), net.full("w_branch_hgrn")
    (ya, yb, merged), got = _merge_fwd(y_attn, y_hgrn, w_ba, w_bh, pg, tm=TM, comm=net.relay(third))
    net.gathered(third, got)
    w_gate, w_up, w_out = net.full("w_ffn_gate"), net.full("w_ffn_up"), net.full("w_out")

    (h1, u2, gpre, up, z), _ = _ffn_fwd(merged, w_out, x, norm_ffn_g, w_gate, w_up, tm=512)
    w_down = net.full("w_ffn_down")

    dh2, dh2b, dgp, dup, dgf_p, loss_p = _ffn_tail(z, w_down, h1, target, norm_final_g, gpre, up, tm=512)
    loss = jnp.sum(loss_p.reshape(-1, 8, D)[:, 0, 0])
    d_norm_final = _colsum_partials(dgf_p)
    d_w_down = _weight_grad("dw_down", z, dh2b, tm=DW_ROWS, tk=T)

    names, swap = ("w_ffn_down",), ()
    (dh1, dh1b, dg2_p), got = _ffn_in_bwd(dgp, dup, w_gate, w_up, h1, norm_ffn_g, dh2, tm=256,
                                          comm=net.exchange(dict(w_ffn_down=[d_w_down]), swap))
    net.received(names, swap, got)
    d_norm_ffn = _colsum_partials(dg2_p)
    d_w_gate = _weight_grad("dw_gate", dgp, u2, tm=DW_ROWS, tk=T)
    d_w_up = _weight_grad("dw_up", dup, u2, tm=DW_ROWS, tk=T)

    rows_in = sum(IN_SPLITS)
    dya, dyb, dy_attn, dy_hgrn, d_in = _merge_bwd(dh1b, w_out, w_ba, w_bh, ya, yb, pg, tm=TM, d_in_width=rows_in,
                                                  first=o_g)
    d_w_out = _weight_grad("dw_out", merged, dh1b, tm=1024, tk=1024)
    d_w_ba = _weight_grad("dw_branch_a", y_attn, dya, tm=1024, tk=1024)
    d_w_bh = _weight_grad("dw_branch_b", y_hgrn, dyb, tm=1024, tk=1024)

    names, swap = ("w_ffn_gate",), ("w_ffn_down",)
    (d_in, dsink), got = _attn_bwd(pq, pkv, sinks, lse, dy_attn, d_in,
                                   comm=net.exchange(dict(w_ffn_gate=[d_w_gate]), swap))
    net.received(names, swap, got)
    names, swap = ("w_ffn_up", "w_out"), ("w_ffn_gate",)
    (d_in, d_hgrn_norm, d_lb_logits), got = _hgrn_bwd(
        ph, o_raw, states, dy_hgrn, lb_logits, hgrn_norm_g, d_in, o_h,
        comm=net.exchange(dict(w_ffn_up=[d_w_up], w_out=[d_w_out]), swap))
    net.received(names, swap, got)

    names, swap = ("w_branch_attn", "w_branch_hgrn"), ("w_ffn_up", "w_out")
    (*d_w_in, d_b_in), got = _weight_grad(
        "dw_in", d_in, u, tm=DW_ROWS, tk=T, a_colsum=True, carrying=True,
        comm=net.exchange(dict(w_branch_attn=[d_w_ba], w_branch_hgrn=[d_w_bh]), swap))
    net.received(names, swap, got)
    d_w_in = [tuple(d_w_in)]

    first_level = net.presum_begin("w_in", d_w_in)
    halves = net.presum_end("w_in", [] if first_level is None else _copies_alone("presum_swap_w_in", first_level))
    names, swap = ("w_in",), ("w_branch_attn", "w_branch_hgrn")
    (dx, dg1_p), got = _in_proj_bwd([(d_in, 0)], w_in, x, norm_mix_g, dh1, tm=256,
                                    comm=_join(halves, net.swap(swap)))
    net.last = (names, swap, got)
    d_norm_mix = _colsum_partials(dg1_p)
    vecs = dict(norm_mix_g=d_norm_mix, b_in=d_b_in, attn_sinks=jnp.sum(dsink.reshape(Q_HEADS, ATTN_BLOCK), axis=1).reshape(1, Q_HEADS),
                hgrn_lb_logits=d_lb_logits,
                hgrn_norm_g=d_hgrn_norm, norm_ffn_g=d_norm_ffn, norm_final_g=d_norm_final)
    return loss, dx, vecs


def _place():
    return lax.axis_index("x"), lax.axis_index("y"), lax.axis_index("c")


def _other_chips(x, y):
    return [(1 - x, y), (x, 1 - y), (1 - x, 1 - y)]


def _y_first(copies):
    return [copies[3 * (i // 3) + (1, 0, 2)[i % 3]] for i in range(len(copies))]


def _fetch_copies(shards):
    n = len(shards)

    def build(ins, outs, send_sems, recv_sems, local_sems, *stages):
        x, y, c = _place()
        mine = 2 * x + y
        local = [(pltpu.make_async_copy(ins[w], stages[w], local_sems.at[2 * w]),
                  pltpu.make_async_copy(stages[w], outs[w].at[mine], local_sems.at[2 * w + 1])) for w in range(n)]
        sends, recvs = [], []
        for w in range(n):
            half = shards[w].shape[0] // 2
            rows = pl.ds(c * half, half)
            for k, (px, py) in enumerate(_other_chips(x, y)):
                sem = 3 * w + k
                sends.append(pltpu.make_async_remote_copy(
                    src_ref=ins[w].at[rows], dst_ref=outs[w].at[mine, rows], send_sem=send_sems.at[sem],
                    recv_sem=recv_sems.at[sem], device_id=(px, py, c), device_id_type=MESH_ID))
                recvs.append(pltpu.make_async_remote_copy(
                    src_ref=ins[w].at[rows], dst_ref=outs[w].at[2 * px + py, rows], send_sem=send_sems.at[sem],
                    recv_sem=recv_sems.at[sem], device_id=(px, py, c), device_id_type=MESH_ID))
        return sends, recvs, local, _y_first(sends)

    return _Carried(shards, [jax.ShapeDtypeStruct((N_CHIPS,) + s.shape, s.dtype) for s in shards], 3 * n, 2 * n, build,
                    stages=[pltpu.VMEM(s.shape, s.dtype) for s in shards])


def _relay_copies(fetched):
    n = len(fetched)

    def build(ins, outs, send_sems, recv_sems, local_sems):
        x, y, c = _place()
        sends, recvs = [], []
        for w in range(n):
            half = fetched[w].shape[1] // 2
            mine, theirs = pl.ds(c * half, half), pl.ds((1 - c) * half, half)
            for k, (px, py) in enumerate(_other_chips(x, y)):
                block, sem = 2 * px + py, 3 * w + k
                sends.append(pltpu.make_async_remote_copy(
                    src_ref=ins[w].at[block, mine], dst_ref=outs[w].at[block, mine], send_sem=send_sems.at[sem],
                    recv_sem=recv_sems.at[sem], device_id=(x, y, 1 - c), device_id_type=MESH_ID))
                recvs.append(pltpu.make_async_remote_copy(
                    src_ref=ins[w].at[block, theirs], dst_ref=outs[w].at[block, theirs], send_sem=send_sems.at[sem],
                    recv_sem=recv_sems.at[sem], device_id=(x, y, 1 - c), device_id_type=MESH_ID))
        return sends, recvs, []

    return _Carried(fetched, [jax.ShapeDtypeStruct(f.shape, f.dtype) for f in fetched], 3 * n, 0, build,
                    continued={w: w for w in range(n)})


def _grad_copies(stacked):
    n = len(stacked)

    def build(ins, outs, send_sems, recv_sems, local_sems):
        x, y, c = _place()
        sends = []
        for w in range(n):
            for k, (px, py) in enumerate(_other_chips(x, y)):
                sem = 3 * w + k
                sends.append(pltpu.make_async_remote_copy(
                    src_ref=ins[w].at[2 * px + py], dst_ref=outs[w].at[k], send_sem=send_sems.at[sem],
                    recv_sem=recv_sems.at[sem], device_id=(px, py, c), device_id_type=MESH_ID))
        return sends, sends, [], _y_first(sends)

    return _Carried(stacked, [jax.ShapeDtypeStruct((3,) + s.shape[1:], s.dtype) for s in stacked], 3 * n, 0, build)


def _small_copies(small):
    def build(ins, outs, send_sems, recv_sems, local_sems):
        small_ref, all_ref = ins[0], outs[0]
        x, y, c = _place()
        me = 4 * x + 2 * y + c
        sends, recvs = [], []
        for r in range(1, 8):
            px = 1 - x if r & 4 else x
            py = 1 - y if r & 2 else y
            pc = 1 - c if r & 1 else c
            sends.append(pltpu.make_async_remote_copy(
                src_ref=small_ref, dst_ref=all_ref.at[me], send_sem=send_sems.at[r - 1], recv_sem=recv_sems.at[r - 1],
                device_id=(px, py, pc), device_id_type=MESH_ID))
            recvs.append(pltpu.make_async_remote_copy(
                src_ref=small_ref, dst_ref=all_ref.at[4 * px + 2 * py + pc], send_sem=send_sems.at[r - 1],
                recv_sem=recv_sems.at[r - 1], device_id=(px, py, pc), device_id_type=MESH_ID))
        return sends, recvs, [pltpu.make_async_copy(small_ref, all_ref.at[me], local_sems.at[0])]

    return _Carried([small], [jax.ShapeDtypeStruct((8,) + small.shape, small.dtype)], 7, 1, build)


def _gather_by_neighbours(name, shard):
    half = shard.shape[0] // 2
    quarter = half // 2

    def body(in_ref, out_ref, send_sems, recv_sems, local_sem, stage_ref):
        for core in (0, 1):
            @pl.when(lax.axis_index("c") == core)
            def _():
                program(core, in_ref, out_ref, send_sems, recv_sems, local_sem, stage_ref)

    def program(c, in_ref, out_ref, send_sems, recv_sems, local_sem, stage_ref):
        x, y, _ = _place()
        chip = lambda px, py: 2 * px + py
        to_x, to_y, sibling = (1 - x, y, c), (x, 1 - y, c), (x, y, 1 - c)
        x_blk, y_blk, d_blk = chip(1 - x, y), chip(x, 1 - y), chip(1 - x, 1 - y)
        mine, theirs = c * half, (1 - c) * half

        def copy(sem, rows, block, to, src=None):
            place = out_ref.at[block, pl.ds(rows[0], rows[1])]
            return pltpu.make_async_remote_copy(
                src_ref=place if src is None else src, dst_ref=place, send_sem=send_sems.at[sem],
                recv_sem=recv_sems.at[sem], device_id=to, device_id_type=MESH_ID)

        stage = pltpu.make_async_copy(in_ref, stage_ref, local_sem.at[0])
        own = pltpu.make_async_copy(stage_ref, out_ref.at[chip(x, y)], local_sem.at[1])
        my_rows = in_ref.at[pl.ds(mine, half)]
        along_x = dict(send=copy(0, (mine, half), chip(x, y), to_x, src=my_rows),
                       landed=copy(0, (mine, half), x_blk, to_x),
                       onward=[copy(3, (mine + quarter, quarter), x_blk, to_y), copy(4, (mine, half), x_blk, sibling)],
                       diagonal=copy(2, (mine, quarter), d_blk, to_x))
        along_y = dict(send=copy(1, (mine, half), chip(x, y), to_y, src=my_rows),
                       landed=copy(1, (mine, half), y_blk, to_y),
                       onward=[copy(2, (mine, quarter), y_blk, to_x), copy(5, (mine, half), y_blk, sibling)],
                       diagonal=copy(3, (mine + quarter, quarter), d_blk, to_y))
        last = copy(6, (mine, half), d_blk, sibling)

        order = (along_x, along_y) if c == 0 else (along_y, along_x)
        for axis in order:
            axis["send"].start()
        stage.start()
        stage.wait()
        own.start()
        for axis in order:
            axis["landed"].wait_recv()
            for cp in axis["onward"]:
                cp.start()
        for axis in order:
            axis["diagonal"].wait_recv()
        last.start()
        for sem, block in ((4, x_blk), (5, y_blk), (6, d_blk)):
            copy(sem, (theirs, half), block, sibling).wait_recv()
        for cp in [along_x["send"], along_y["send"]] + along_x["onward"] + along_y["onward"] + [last]:
            cp.wait_send()
        own.wait()

    return pl.pallas_call(
        body, name=name, in_specs=[HBM_SPEC], out_specs=HBM_SPEC,
        out_shape=jax.ShapeDtypeStruct((N_CHIPS,) + shard.shape, shard.dtype),
        scratch_shapes=[pltpu.SemaphoreType.DMA((7,)), pltpu.SemaphoreType.DMA((7,)), pltpu.SemaphoreType.DMA((2,)),
                        pltpu.VMEM(shard.shape, shard.dtype)],
    )(shard)


def _copies_alone(name, comm):
    return _call(name, lambda: None, grid=(), in_specs=[], out_specs=[], out_shape=[], args=[], comm=comm)[1]


class _Net:
    def __init__(self, shards):
        self.shards = shards
        self.whole, self.partly, self.own, self.theirs, self.sums, self.other = {}, {}, {}, {}, {}, {}
        x, y, _ = _place()
        self.chip = 2 * x + y

    def fetch(self, names):
        return _fetch_copies([self.shards[n] for n in names])

    def fetched(self, names, got):
        self.partly.update(zip(names, got))

    def relay(self, names):
        return _relay_copies([self.partly[n] for n in names])

    def gathered(self, names, got):
        for n, g in zip(names, got):
            self.whole[n] = g.reshape(-1, g.shape[-1])

    def full(self, name):
        return self.whole[name]

    def exchange(self, grads, swap=()):
        stacked = []
        for n, pieces in grads.items():
            (keep, send), = pieces
            self.own[n] = keep
            stacked.append(send.reshape(N_CHIPS, keep.shape[0] // N_CHIPS, send.shape[-1]))
        return _join(_grad_copies(stacked), self.swap(swap))

    def swap(self, names):
        return _sibling_copies([self.sums[n] for n in names]) if names else None

    def presum_begin(self, name, pieces):
        keep = jnp.concatenate([p[0] for p in pieces], axis=0) if len(pieces) > 1 else pieces[0][0]
        send = jnp.concatenate([p[1] for p in pieces], axis=0) if len(pieces) > 1 else pieces[0][1]
        rows = keep.shape[0] // N_CHIPS
        self.held = keep.reshape(N_CHIPS, rows, keep.shape[-1])
        return _half_rows_copies(send.reshape(N_CHIPS, rows, send.shape[-1]))

    def presum_end(self, name, got):
        x, y, c = _place()
        to_send, self.own[name] = _pre_sum("presum_" + name, self.held, got[0], jnp.stack([c, self.chip]))
        return _grad_copies([to_send])

    def received(self, names, swap, got, carried=None):
        self.theirs.update(zip(names, got[:len(names)]))
        self.other.update(zip(swap, got[len(names):]))
        for n in names:
            (self.sums[n],), more = _partial_sum("sum_" + n, self.own[n], self.theirs[n], self.chip, comm=carried)
        return more


def _half_rows_copies(stacked):
    n, rows = stacked.shape[0], stacked.shape[1] // 2

    def build(ins, outs, send_sems, recv_sems, local_sems):
        x, y, c = _place()
        copies = [pltpu.make_async_remote_copy(
            src_ref=ins[0].at[s, pl.ds((1 - c) * rows, rows)], dst_ref=outs[0].at[s], send_sem=send_sems.at[s],
            recv_sem=recv_sems.at[s], device_id=(x, y, 1 - c), device_id_type=MESH_ID) for s in range(n)]
        return copies, copies, []

    return _Carried([stacked], [jax.ShapeDtypeStruct((n, rows, stacked.shape[2]), stacked.dtype)], n, 0, build)


def _pre_sum(name, held, theirs, core_and_chip):
    n, R, C = held.shape
    half = R // 2
    tr = _row_tile(half)
    per_half = half // tr

    def body(place_ref, h_ref, t_ref, send_ref, own_ref):
        total = h_ref[0] + t_ref[0].astype(F32)
        send_ref[0] = total.astype(send_ref.dtype)

        @pl.when(pl.program_id(1) == place_ref[1])
        def _():
            own_ref[...] = total

    return pl.pallas_call(
        body, name=name,
        grid_spec=pltpu.PrefetchScalarGridSpec(
            num_scalar_prefetch=1, grid=(per_half, n),
            in_specs=[pl.BlockSpec((1, tr, C), lambda i, s, place: (s, place[0] * per_half + i, 0)),
                      pl.BlockSpec((1, tr, C), lambda i, s, place: (s, i, 0))],
            out_specs=[pl.BlockSpec((1, tr, C), lambda i, s, place: (s, i, 0)),
                       pl.BlockSpec((tr, C), lambda i, s, place: (i, 0))]),
        out_shape=[jax.ShapeDtypeStruct((n, half, C), MXU_DTYPE), jax.ShapeDtypeStruct((half, C), F32)],
        compiler_params=_params(("arbitrary", "arbitrary")),
    )(core_and_chip, held, theirs)


def _sibling_copies(parts):
    n = len(parts)

    def build(ins, outs, send_sems, recv_sems, local_sems):
        x, y, c = _place()
        copies = [pltpu.make_async_remote_copy(
            src_ref=ins[w], dst_ref=outs[w], send_sem=send_sems.at[w], recv_sem=recv_sems.at[w],
            device_id=(x, y, 1 - c), device_id_type=MESH_ID) for w in range(n)]
        return copies, copies, []

    return _Carried(parts, [jax.ShapeDtypeStruct(p.shape, p.dtype) for p in parts], n, 0, build)


def _row_tile(rows, most=512, sublanes=16):
    return max(t for t in range(sublanes, min(most, rows // 2) + 1, sublanes) if rows % t == 0)


def _partial_sum(name, own, recv, chip, comm=None):
    _, R, C = recv.shape
    tr = _row_tile(R)

    def body(o_ref, r_ref, p_ref):
        p_ref[...] = ((o_ref[...] + r_ref[0].astype(F32)) + r_ref[1].astype(F32)) + r_ref[2].astype(F32)

    if own.shape[0] != R:
        assert comm is None and own.shape[0] == N_CHIPS * R
        total = pl.pallas_call(
            lambda chip_ref, *refs: body(*refs), name=name,
            grid_spec=pltpu.PrefetchScalarGridSpec(
                num_scalar_prefetch=1, grid=(R // tr,),
                in_specs=[pl.BlockSpec((tr, C), lambda i, chip_ref: (chip_ref[0] * (R // tr) + i, 0)),
                          pl.BlockSpec((3, tr, C), lambda i, chip_ref: (0, i, 0))],
                out_specs=pl.BlockSpec((tr, C), lambda i, chip_ref: (i, 0))),
            out_shape=jax.ShapeDtypeStruct((R, C), F32), compiler_params=_params(("parallel",)),
        )(chip.reshape(1), own, recv)
        return [total], []
    return _call(name, body, grid=(R // tr,),
                 in_specs=[pl.BlockSpec((tr, C), lambda i: (i, 0)), pl.BlockSpec((3, tr, C), lambda i: (0, i, 0))],
                 out_specs=[pl.BlockSpec((tr, C), lambda i: (i, 0))], out_shape=[jax.ShapeDtypeStruct((R, C), F32)],
                 args=[own, recv], semantics=("parallel",), comm=comm)


def _adam_vals(w, g, m, v):
    m = ADAM_B1 * m + (1.0 - ADAM_B1) * g
    v = ADAM_B2 * v + (1.0 - ADAM_B2) * (g * g)
    m_hat = m / (1.0 - ADAM_B1 ** ADAM_STEP)
    v_hat = v / (1.0 - ADAM_B2 ** ADAM_STEP)
    delta = -ADAM_LR * (m_hat / (jnp.sqrt(v_hat) + ADAM_EPS) + ADAM_WD * w)
    return delta, m, v


def _adamw(name, w, m, v, mine, other, comm=None):
    R, C = w.shape
    tr = _row_tile(R)

    def body(w_ref, m_ref, v_ref, s_ref, n_ref, g_ref, d_ref, nm_ref, nv_ref):
        g = s_ref[...] + n_ref[...]
        d, nm, nv = _adam_vals(w_ref[...], g, m_ref[...], v_ref[...])
        g_ref[...], d_ref[...], nm_ref[...], nv_ref[...] = g, d, nm, nv

    spec = pl.BlockSpec((tr, C), lambda i: (i, 0))
    return _call(name, body, grid=(R // tr,), in_specs=[spec] * 5, out_specs=[spec] * 4,
                 out_shape=[jax.ShapeDtypeStruct((R, C), F32)] * 4, args=[w, m, v, mine, other],
                 semantics=("parallel",), comm=comm)


def _adamw_by_halves(name, w, m, v, mine, other, core):
    R, C = w.shape
    tr = _row_tile(R // 2)
    per_half = R // 2 // tr

    def body(c_ref, w_ref, m_ref, v_ref, s_ref, n_ref, g_ref, d_ref, nm_ref, nv_ref):
        g = jnp.where(pl.program_id(0) // per_half == c_ref[0, 0], s_ref[...], n_ref[...])
        d, nm, nv = _adam_vals(w_ref[...], g, m_ref[...], v_ref[...])
        g_ref[...], d_ref[...], nm_ref[...], nv_ref[...] = g, d, nm, nv

    spec = pl.BlockSpec((tr, C), lambda i: (i, 0))
    part = pl.BlockSpec((tr, C), lambda i: (i % per_half, 0))
    return pl.pallas_call(
        body, name=name, grid=(R // tr,),
        in_specs=[pl.BlockSpec(memory_space=pltpu.SMEM), spec, spec, spec, part, part], out_specs=[spec] * 4,
        out_shape=[jax.ShapeDtypeStruct((R, C), F32)] * 4, compiler_params=_params(("parallel",)),
    )(core, w, m, v, mine, other)


SMALL_LAYOUT = dict(norm_mix_g=(0, 1, 1024), b_in=(1, 8, 7424), hgrn_norm_g=(9, 1, 1024), norm_ffn_g=(10, 1, 1024),
                    norm_final_g=(11, 1, 1024), hgrn_lb_logits=(12, 2, 2048), attn_sinks=(14, 1, 16))
SMALL_LOSS_ROW, SMALL_ROWS = 15, 16


def _pack_small(grads, loss):
    rows = [jnp.pad(grads[name].astype(F32).reshape(-1), (0, nrows * D_MODEL - n))
            for name, (_, nrows, n) in SMALL_LAYOUT.items()]
    rows.append(jnp.pad(loss.astype(F32).reshape(1), (0, D_MODEL - 1)))
    return jnp.concatenate(rows).reshape(SMALL_ROWS, D_MODEL)


def _adamw_small(w, m, v, g_all):
    names = list(SMALL_LAYOUT)
    n = len(names)

    def body(a_ref, *refs):
        ins, outs = refs[:3 * n], refs[3 * n:]
        g_all_rows = a_ref[0]
        for dev in range(1, 8):
            g_all_rows = g_all_rows + a_ref[dev]
        for i, name in enumerate(names):
            first, nrows, count = SMALL_LAYOUT[name]
            w_ref, m_ref, v_ref = ins[3 * i:3 * i + 3]
            if w_ref.shape[0] == nrows:
                g = g_all_rows[first:first + nrows, :w_ref.shape[1]]
            else:
                last = count - (nrows - 1) * D_MODEL
                g = jnp.concatenate([g_all_rows[r:r + 1, :] for r in range(first, first + nrows - 1)]
                                    + [g_all_rows[first + nrows - 1:first + nrows, :last]], axis=1)
            d, nm, nv = _adam_vals(w_ref[...], g, m_ref[...], v_ref[...])
            for o_ref, val in zip(outs[4 * i:4 * i + 4], (g, d, nm, nv)):
                o_ref[...] = val
        outs[4 * n][...] = g_all_rows[SMALL_LOSS_ROW:SMALL_LOSS_ROW + 1, 0:1]

    res = pl.pallas_call(
        body, name="adamw_small",
        out_shape=[jax.ShapeDtypeStruct(w[name].shape, F32) for name in names for _ in range(4)]
        + [jax.ShapeDtypeStruct((1, 1), F32)],
    )(g_all, *[t[name] for name in names for t in (w, m, v)])
    return {name: res[4 * i:4 * i + 4] for i, name in enumerate(names)}, res[4 * n]


MATRICES = ("w_in", "w_branch_attn", "w_branch_hgrn", "w_out", "w_ffn_gate", "w_ffn_up", "w_ffn_down")
COLUMN_SHARDED = ("w_in", "w_ffn_gate", "w_ffn_up")
WEIGHTS = ("norm_mix_g", "w_in", "b_in", "attn_sinks", "hgrn_lb_logits", "hgrn_norm_g", "w_branch_attn",
           "w_branch_hgrn", "w_out", "norm_ffn_g", "w_ffn_gate", "w_ffn_up", "w_ffn_down", "norm_final_g")


def kernel(x, norm_mix_g, w_in, b_in, attn_sinks, hgrn_lb_logits, hgrn_norm_g, w_branch_attn, w_branch_hgrn, w_out, norm_ffn_g, w_ffn_gate, w_ffn_up, w_ffn_down, norm_final_g, loss_target, m_norm_mix_g, m_w_in, m_b_in, m_attn_sinks, m_hgrn_lb_logits, m_hgrn_norm_g, m_w_branch_attn, m_w_branch_hgrn, m_w_out, m_norm_ffn_g, m_w_ffn_gate, m_w_ffn_up, m_w_ffn_down, m_norm_final_g, v_norm_mix_g, v_w_in, v_b_in, v_attn_sinks, v_hgrn_lb_logits, v_hgrn_norm_g, v_w_branch_attn, v_w_branch_hgrn, v_w_out, v_norm_ffn_g, v_w_ffn_gate, v_w_ffn_up, v_w_ffn_down, v_norm_final_g):
    given = dict(locals())
    w = {n: given[n] for n in WEIGHTS}
    m = {n: given["m_" + n] for n in WEIGHTS}
    v = {n: given["v_" + n] for n in WEIGHTS}

    block = lambda a, n: jnp.transpose(a[0]) if n in COLUMN_SHARDED else a[0]
    unblock = lambda a, n: (jnp.transpose(a) if n in COLUMN_SHARDED else a)[None]
    net = _Net({n: block(w[n], n).astype(MXU_DTYPE) for n in MATRICES})
    net.gathered(("w_in",), [_gather_by_neighbours("gather_w_in", net.shards["w_in"])])
    vec = dict(norm_mix_g=norm_mix_g, b_in=b_in, attn_sinks=attn_sinks, hgrn_lb_logits=hgrn_lb_logits,
               hgrn_norm_g=hgrn_norm_g, norm_ffn_g=norm_ffn_g, norm_final_g=norm_final_g.reshape(1, D_MODEL))
    loss_part, dx, d_vecs = _local_step(x[0], loss_target[0], vec, net)

    small_all, = net.received(*net.last, carried=_small_copies(_pack_small(d_vecs, loss_part)))
    grads, deltas, new_m, new_v = {}, {}, {}, {}
    for n in ("w_ffn_down", "w_ffn_gate", "w_ffn_up", "w_out", "w_branch_attn", "w_branch_hgrn"):
        res, got = _adamw("adamw_" + n, block(w[n], n), block(m[n], n), block(v[n], n), net.sums[n], net.other[n],
                          comm=net.swap(("w_in",)) if n == "w_ffn_down" else None)
        if n == "w_ffn_down":
            net.other["w_in"], = got
        grads[n], deltas[n], new_m[n], new_v[n] = (unblock(r, n) for r in res)
    n = "w_in"
    res = _adamw_by_halves("adamw_" + n, block(w[n], n), block(m[n], n), block(v[n], n), net.sums[n], net.other[n],
                           _place()[2].reshape(1, 1))
    grads[n], deltas[n], new_m[n], new_v[n] = (unblock(r, n) for r in res)
    rows = lambda t: {n: t[n].reshape(-1, t[n].shape[-1]) for n in SMALL_LAYOUT}
    res, loss = _adamw_small(rows(w), rows(m), rows(v), small_all)
    for n, four in res.items():
        grads[n], deltas[n], new_m[n], new_v[n] = (r.reshape(w[n].shape) for r in four)
    loss = loss.reshape(())
    return (loss, dx[None], *[grads[n] for n in WEIGHTS], *[deltas[n] for n in WEIGHTS],
            *[new_m[n] for n in WEIGHTS], *[new_v[n] for n in WEIGHTS])
```

```python
import functools
import math

import jax
import jax.numpy as jnp
from jax import lax
from jax.experimental import pallas as pl
from jax.experimental.pallas import tpu as pltpu

F32 = jnp.float32
BF16 = jnp.bfloat16
MXU_DTYPE = jnp.bfloat16
SAVED_DTYPE = jnp.bfloat16
MESH_ID = pl.DeviceIdType.MESH

D_MODEL = 1024
HEAD_DIM = 64
Q_HEADS = 16
KV_HEADS = 2
GROUP = Q_HEADS // KV_HEADS
KV_WIDTH = KV_HEADS * HEAD_DIM
ATTN_BLOCK = 128
HGRN_HEADS = 8
HGRN_K = 128
CHUNK = 64
HGRN_TOKENS = 256
FFN = 2816
IN_SPLITS = (1024, 256, 4096, 2048)
EPS = 1e-6
NEG_INF = -1e30
ADAM_LR, ADAM_B1, ADAM_B2, ADAM_EPS, ADAM_WD, ADAM_STEP = 0.001, 0.9, 0.999, 1e-08, 0.01, 10
N_CHIPS = 4
VMEM_LIMIT = 60 * 1024 * 1024
ROW_ALIGN = 16
DW_ROWS = 256


def _params(sem=None):
    return pltpu.CompilerParams(dimension_semantics=sem, vmem_limit_bytes=VMEM_LIMIT)


def _sigmoid(v):
    return 0.5 * jnp.tanh(0.5 * v) + 0.5


def _dot(a, b, dims):
    return lax.dot_general(a.astype(MXU_DTYPE), b.astype(MXU_DTYPE), (dims, ((), ())),
                           preferred_element_type=F32)


def _nn(a, b):
    return _dot(a, b, ((1,), (0,)))


def _nt(a, b):
    return _dot(a, b, ((1,), (1,)))


def _tn(a, b):
    return _dot(a, b, ((0,), (0,)))


HBM_SPEC = pl.BlockSpec(memory_space=pl.ANY)


class _Carried:
    def __init__(self, arrays, out_shapes, n_remote, n_local, build, continued=None, stages=()):
        self.continued = dict(continued or {})
        self.parts = [(len(arrays), len(out_shapes), 3 + len(stages), build)]
        self.arrays, self.out_shapes = list(arrays), list(out_shapes)
        self.scratch = [pltpu.SemaphoreType.DMA((n_remote,)), pltpu.SemaphoreType.DMA((n_remote,)),
                        pltpu.SemaphoreType.DMA((max(n_local, 1),))] + list(stages)

    def __add__(self, other):
        both = _Carried([], [], 1, 0, None)
        both.parts = self.parts + other.parts
        both.arrays, both.out_shapes = self.arrays + other.arrays, self.out_shapes + other.out_shapes
        both.scratch = self.scratch + other.scratch
        both.continued = dict(self.continued)
        both.continued.update({len(self.arrays) + i: len(self.out_shapes) + o for i, o in other.continued.items()})
        return both

    def _built(self, ins, outs, sems):
        for ni, no, ns, build in self.parts:
            yield build(ins[:ni], outs[:no], *sems[:ns])
            ins, outs, sems = ins[ni:], outs[no:], sems[ns:]

    def start(self, ins, outs, sems):
        core = lax.axis_index("c")
        for sends, _, local, *other_order in self._built(ins, outs, sems):
            for cp in local:
                if not isinstance(cp, tuple):
                    cp.start()
            if not other_order:
                for cp in sends:
                    cp.start()
            else:
                @pl.when(core == 0)
                def _():
                    for cp in sends:
                        cp.start()

                @pl.when(core == 1)
                def _():
                    for cp in other_order[0]:
                        cp.start()
            staged = [cp for cp in local if isinstance(cp, tuple)]
            for into, _ in staged:
                into.start()
            for into, out_of in staged:
                into.wait()
                out_of.start()

    def wait(self, ins, outs, sems):
        for sends, recvs, local, *_ in self._built(ins, outs, sems):
            for cp in recvs:
                cp.wait_recv()
            for cp in sends:
                cp.wait_send()
            for cp in local:
                (cp[1] if isinstance(cp, tuple) else cp).wait()


def _join(*comms):
    comms = [c for c in comms if c is not None]
    return functools.reduce(lambda a, b: a + b, comms) if comms else None


def _call(name, body, *, grid, in_specs, out_specs, out_shape, args, scratch=(), semantics=None, comm=None,
          aliases=None):
    n_in, n_out, n_scr = len(in_specs), len(out_specs), len(scratch)
    aliases = aliases or {}
    if comm is None:
        res = pl.pallas_call(body, name=name, grid=grid, in_specs=in_specs, out_specs=out_specs, out_shape=out_shape,
                             scratch_shapes=list(scratch), input_output_aliases=aliases,
                             compiler_params=_params(semantics))(*args)
        return list(res), []
    ci, co = len(comm.arrays), len(comm.out_shapes)
    aliases = dict(aliases)
    aliases.update({n_in + i: n_out + o for i, o in comm.continued.items()})

    def carrying(*refs):
        ins, refs = refs[:n_in], refs[n_in:]
        c_ins, refs = refs[:ci], refs[ci:]
        outs, refs = refs[:n_out], refs[n_out:]
        c_outs, refs = refs[:co], refs[co:]
        scr, sems = refs[:n_scr], refs[n_scr:]
        if not grid:
            comm.start(c_ins, c_outs, sems)
            body(*ins, *outs, *scr)
            comm.wait(c_ins, c_outs, sems)
            return
        first = functools.reduce(jnp.logical_and, [pl.program_id(a) == 0 for a in range(len(grid))])
        last = functools.reduce(jnp.logical_and, [pl.program_id(a) == g - 1 for a, g in enumerate(grid)])

        @pl.when(first)
        def _():
            comm.start(c_ins, c_outs, sems)

        body(*ins, *outs, *scr)

        @pl.when(last)
        def _():
            comm.wait(c_ins, c_outs, sems)

    res = pl.pallas_call(
        carrying, name=name, grid=grid, in_specs=list(in_specs) + [HBM_SPEC] * ci,
        out_specs=list(out_specs) + [HBM_SPEC] * co, out_shape=list(out_shape) + comm.out_shapes,
        scratch_shapes=list(scratch) + comm.scratch, input_output_aliases=aliases,
        compiler_params=_params(("arbitrary",) * len(grid) if grid else None),
    )(*args, *comm.arrays)
    return list(res[:n_out]), list(res[n_out:])


def _weight_grad(name, a, b, *, tm, tk, a_colsum=False, carrying=False, comm=None):
    (T, N), M = b.shape, a.shape[1]
    tk = min(tk, T)
    assert M % tm == 0 and T % tk == 0, (name, M, tm, T, tk)
    ni, nk = M // tm, T // tk

    def body(a_ref, b_ref, *rest):
        keep_ref, send_ref = rest[:2]
        sums_ref = rest[2] if a_colsum else None
        if nk == 1:
            acc = _tn(a_ref[...], b_ref[...])
            keep_ref[...], send_ref[...] = acc, acc.astype(send_ref.dtype)
            if a_colsum:
                sums_ref[...] = jnp.sum(a_ref[...].astype(F32), axis=0, keepdims=True)
            return
        acc_ref = rest[-1]
        k = pl.program_id(1)

        @pl.when(k == 0)
        def _():
            acc_ref[...] = jnp.zeros_like(acc_ref)
            if a_colsum:
                sums_ref[...] = jnp.zeros((1, tm), F32)

        if a_colsum:
            sums_ref[...] += jnp.sum(a_ref[...].astype(F32), axis=0, keepdims=True)
        acc_ref[...] += _tn(a_ref[...], b_ref[...])

        @pl.when(k == nk - 1)
        def _():
            keep_ref[...], send_ref[...] = acc_ref[...], acc_ref[...].astype(send_ref.dtype)

    out_spec = pl.BlockSpec((tm, N), lambda i, k: (i, 0))
    out_shape = [jax.ShapeDtypeStruct((M, N), F32), jax.ShapeDtypeStruct((M, N), MXU_DTYPE)]
    out_specs = [out_spec, out_spec]
    if a_colsum:
        out_shape.append(jax.ShapeDtypeStruct((1, M), F32))
        out_specs.append(pl.BlockSpec((1, tm), lambda i, k: (0, i)))
    res, got = _call(
        name, body, grid=(ni, nk),
        in_specs=[pl.BlockSpec((tk, tm), lambda i, k: (k, i)), pl.BlockSpec((tk, N), lambda i, k: (k, 0))],
        out_specs=out_specs, out_shape=out_shape, scratch=[pltpu.VMEM((tm, N), F32)] if nk > 1 else [],
        args=[a, b], semantics=("parallel", "arbitrary"), comm=comm)
    return (res, got) if carrying else res


def _ffn_fwd(merged, w_out, x, gain, w_gate_t, w_up_t, *, tm, comm=None):
    (T, D), F = x.shape, w_gate_t.shape[0]

    def body(m_ref, wo_ref, x_ref, g_ref, wg_ref, wu_ref, h_ref, u_ref, gate_ref, up_ref, z_ref):
        h = x_ref[...] + _nn(m_ref[...], wo_ref[...])
        h_ref[...] = h
        u = (h * lax.rsqrt(jnp.mean(h * h, axis=-1, keepdims=True) + EPS) * g_ref[...]).astype(u_ref.dtype)
        u_ref[...] = u
        gate, up = _nt(u, wg_ref[...]), _nt(u, wu_ref[...])
        gate_ref[...], up_ref[...] = gate.astype(gate_ref.dtype), up.astype(up_ref.dtype)
        z_ref[...] = (gate * _sigmoid(gate) * up).astype(z_ref.dtype)

    rows = lambda n: pl.BlockSpec((tm, n), lambda i: (i, 0))
    fixed = _fixed_spec
    return _call("ffn_hidden", body, grid=(T // tm,),
                 in_specs=[rows(D), fixed(w_out), rows(D), fixed(gain), fixed(w_gate_t), fixed(w_up_t)],
                 out_specs=[rows(D), rows(D), rows(F), rows(F), rows(F)],
                 out_shape=[jax.ShapeDtypeStruct((T, D), F32), jax.ShapeDtypeStruct((T, D), MXU_DTYPE)]
                 + [jax.ShapeDtypeStruct((T, F), SAVED_DTYPE)] * 2 + [jax.ShapeDtypeStruct((T, F), MXU_DTYPE)],
                 args=[merged, w_out, x, gain, w_gate_t, w_up_t], semantics=("parallel",), comm=comm)


def _in_proj(x, gain, w_in_t, b_in, *, tm, comm=None):
    T, D = x.shape
    bounds = [sum(IN_SPLITS[:i]) for i in range(len(IN_SPLITS) + 1)]

    def body(x_ref, g_ref, w_ref, b_ref, u_ref, *piece_refs):
        xv = x_ref[...]
        r = lax.rsqrt(jnp.mean(xv * xv, axis=-1, keepdims=True) + EPS)
        u = (xv * r * g_ref[...]).astype(u_ref.dtype)
        u_ref[...] = u
        for o_ref, lo, hi in zip(piece_refs, bounds[:-1], bounds[1:]):
            o_ref[...] = (_nt(u, w_ref[lo:hi, :]) + b_ref[:, lo:hi]).astype(o_ref.dtype)

    rows = lambda n: pl.BlockSpec((tm, n), lambda i: (i, 0))
    fixed = _fixed_spec
    dtypes = (MXU_DTYPE, MXU_DTYPE, F32, F32)
    return _call("in_proj", body, grid=(T // tm,),
                 in_specs=[rows(D), fixed(gain), fixed(w_in_t), fixed(b_in)],
                 out_specs=[rows(D)] + [rows(n) for n in IN_SPLITS],
                 out_shape=[jax.ShapeDtypeStruct((T, D), MXU_DTYPE)]
                 + [jax.ShapeDtypeStruct((T, n), dt) for n, dt in zip(IN_SPLITS, dtypes)],
                 args=[x, gain, w_in_t, b_in], semantics=("parallel",), comm=comm)


def _row_spec(tm, n):
    return pl.BlockSpec((tm, n), lambda i: (i, 0))


def _fixed_spec(a):
    return pl.BlockSpec(a.shape, lambda i: (0,) * a.ndim, pipeline_mode=pl.Buffered(1))


def _partials_spec(n):
    return pl.BlockSpec((8, n), lambda i: (i, 0))


def _row_parts(tm, parts):
    assert tm % parts == 0, (tm, parts)
    return [slice(p * (tm // parts), (p + 1) * (tm // parts)) for p in range(parts)]


def _ffn_tail(z, w_down, h1, target, gain, gate, up, *, tm, parts=2):
    (T, F), D = z.shape, h1.shape[1]

    def body(z_ref, w_ref, h_ref, t_ref, g_ref, gate_ref, up_ref, dh_ref, dhb_ref, dgate_ref, dup_ref, dg_ref, l_ref):
        part, dgain = 0.0, 0.0
        pieces = _row_parts(tm, parts)
        h2s = [h_ref[rows, :] + _nn(z_ref[rows, :], w_ref[...]) for rows in pieces]
        for rows, h2 in zip(pieces, h2s):
            r = lax.rsqrt(jnp.mean(h2 * h2, axis=-1, keepdims=True) + EPS)
            xhat = h2 * r
            err = xhat * g_ref[...] - t_ref[rows, :]
            part += 0.5 * jnp.sum(jnp.sum(err * err, axis=-1, keepdims=True), axis=0, keepdims=True) / D
            dy = err / D
            dxh = dy * g_ref[...]
            dh2 = r * (dxh - xhat * jnp.mean(dxh * xhat, axis=-1, keepdims=True))
            dh_ref[rows, :] = dh2
            dhb = dh2.astype(dhb_ref.dtype)
            dhb_ref[rows, :] = dhb
            dgain += jnp.sum(dy * xhat, axis=0, keepdims=True)
            dz = _nt(dhb, w_ref[...])
            gv, upv = gate_ref[rows, :].astype(F32), up_ref[rows, :].astype(F32)
            s = _sigmoid(gv)
            dgate_ref[rows, :] = (dz * upv * (s * (1.0 + gv * (1.0 - s)))).astype(dgate_ref.dtype)
            dup_ref[rows, :] = (dz * (gv * s)).astype(dup_ref.dtype)
        dg_ref[...] = jnp.broadcast_to(dgain, dg_ref.shape)
        l_ref[...] = jnp.broadcast_to(part, l_ref.shape)

    low = lambda n: jax.ShapeDtypeStruct((T, n), MXU_DTYPE)
    part = jax.ShapeDtypeStruct((8 * (T // tm), D), F32)
    return pl.pallas_call(
        body, name="ffn_tail", grid=(T // tm,),
        in_specs=[_row_spec(tm, F), _fixed_spec(w_down), _row_spec(tm, D), _row_spec(tm, D), _fixed_spec(gain),
                  _row_spec(tm, F), _row_spec(tm, F)],
        out_specs=[_row_spec(tm, D), _row_spec(tm, D), _row_spec(tm, F), _row_spec(tm, F), _partials_spec(D),
                   _partials_spec(D)],
        out_shape=[jax.ShapeDtypeStruct((T, D), F32), low(D), low(F), low(F), part, part],
        compiler_params=_params(("parallel",)),
    )(z, w_down, h1, target, gain, gate, up)


def _ffn_in_bwd(dgate, dup, w_gate_t, w_up_t, h1, gain, dres, *, tm, comm=None):
    (T, F), D = dgate.shape, h1.shape[1]

    def body(dg_ref, du_ref, wg_ref, wu_ref, h_ref, g_ref, r_ref, dh_ref, dhb_ref, dgain_ref):
        d_u2 = _nn(dg_ref[...], wg_ref[...]) + _nn(du_ref[...], wu_ref[...])
        dx, dgain = _rmsnorm_bwd_vals(d_u2, h_ref[...], g_ref[...])
        dh = r_ref[...] + dx
        dh_ref[...] = dh
        dhb_ref[...] = dh.astype(dhb_ref.dtype)
        dgain_ref[...] = jnp.broadcast_to(dgain, dgain_ref.shape)

    return _call("d_ffn_in", body, grid=(T // tm,),
                 in_specs=[_row_spec(tm, F), _row_spec(tm, F), _fixed_spec(w_gate_t), _fixed_spec(w_up_t),
                           _row_spec(tm, D), _fixed_spec(gain), _row_spec(tm, D)],
                 out_specs=[_row_spec(tm, D), _row_spec(tm, D), _partials_spec(D)],
                 out_shape=[jax.ShapeDtypeStruct((T, D), F32), jax.ShapeDtypeStruct((T, D), MXU_DTYPE),
                            jax.ShapeDtypeStruct((8 * (T // tm), D), F32)],
                 args=[dgate, dup, w_gate_t, w_up_t, h1, gain, dres], semantics=("parallel",), comm=comm)


def _in_proj_bwd(pieces, w_in_t, x, gain, dres, *, tm, comm=None):
    T, D = x.shape
    n = len(pieces)

    def body(*refs):
        dps, (w_ref, x_ref, g_ref, r_ref, dx_ref, dgain_ref) = refs[:n], refs[n:]
        d_u = None
        for dp_ref, (dp, first) in zip(dps, pieces):
            term = _nn(dp_ref[...], w_ref[first:first + dp.shape[1], :])
            d_u = term if d_u is None else d_u + term
        dx, dgain = _rmsnorm_bwd_vals(d_u, x_ref[...], g_ref[...])
        dx_ref[...] = r_ref[...] + dx
        dgain_ref[...] = jnp.broadcast_to(dgain, dgain_ref.shape)

    return _call("d_u", body, grid=(T // tm,),
                 in_specs=[_row_spec(tm, dp.shape[1]) for dp, _ in pieces]
                 + [_fixed_spec(w_in_t), _row_spec(tm, D), _fixed_spec(gain), _row_spec(tm, D)],
                 out_specs=[_row_spec(tm, D), _partials_spec(D)],
                 out_shape=[jax.ShapeDtypeStruct((T, D), F32), jax.ShapeDtypeStruct((8 * (T // tm), D), F32)],
                 args=[dp for dp, _ in pieces] + [w_in_t, x, gain, dres], semantics=("parallel",), comm=comm)


def _merge_fwd(y_a, y_b, w_a, w_b, gates, *, tm, comm=None):
    T, D = y_a.shape

    def body(ya_ref, yb_ref, wa_ref, wb_ref, ga_ref, gb_ref, pa_ref, pb_ref, m_ref):
        pa, pb = _nn(ya_ref[...], wa_ref[...]), _nn(yb_ref[...], wb_ref[...])
        pa_ref[...], pb_ref[...] = pa.astype(pa_ref.dtype), pb.astype(pb_ref.dtype)
        m_ref[...] = (_sigmoid(ga_ref[...]) * pa + _sigmoid(gb_ref[...]) * pb).astype(m_ref.dtype)

    rows = pl.BlockSpec((tm, D), lambda i: (i, 0))
    whole = pl.BlockSpec((D, D), lambda i: (0, 0), pipeline_mode=pl.Buffered(1))
    return _call("branch_merge", body, grid=(T // tm,),
                 in_specs=[rows, rows, whole, whole, rows, pl.BlockSpec((tm, D), lambda i: (i, 1))],
                 out_specs=[rows] * 3,
                 out_shape=[jax.ShapeDtypeStruct((T, D), SAVED_DTYPE)] * 2 + [jax.ShapeDtypeStruct((T, D), MXU_DTYPE)],
                 args=[y_a, y_b, w_a, w_b, gates, gates], semantics=("parallel",), comm=comm)


def _merge_bwd(dh, w_out, w_a, w_b, p_a, p_b, gates, *, tm, d_in_width, first):
    T, D = dh.shape

    def body(dh_ref, wo_ref, wa_ref, wb_ref, pa_ref, pb_ref, ga_ref, gb_ref, dpa_ref, dpb_ref, dya_ref, dyb_ref,
             din_ref):
        dm = _nt(dh_ref[...], wo_ref[...])
        sa, sb = _sigmoid(ga_ref[...]), _sigmoid(gb_ref[...])
        dpa, dpb = (dm * sa).astype(dpa_ref.dtype), (dm * sb).astype(dpb_ref.dtype)
        dpa_ref[...], dpb_ref[...] = dpa, dpb
        din_ref[:, :D] = (dm * pa_ref[...].astype(F32) * sa * (1.0 - sa)).astype(din_ref.dtype)
        din_ref[:, D:] = (dm * pb_ref[...].astype(F32) * sb * (1.0 - sb)).astype(din_ref.dtype)
        dya_ref[...] = _nt(dpa, wa_ref[...]).astype(dya_ref.dtype)
        dyb_ref[...] = _nt(dpb, wb_ref[...]).astype(dyb_ref.dtype)

    rows = pl.BlockSpec((tm, D), lambda i: (i, 0))
    whole = pl.BlockSpec((D, D), lambda i: (0, 0), pipeline_mode=pl.Buffered(1))
    low = jax.ShapeDtypeStruct((T, D), MXU_DTYPE)
    return pl.pallas_call(
        body, name="d_branch_merge", grid=(T // tm,),
        in_specs=[rows, whole, whole, whole, rows, rows, rows, pl.BlockSpec((tm, D), lambda i: (i, 1))],
        out_specs=[rows] * 4 + [pl.BlockSpec((pl.Element(tm), pl.Element(2 * D)), lambda i: (
            pl.multiple_of(i * tm, ROW_ALIGN), first))],
        out_shape=[low] * 3 + [jax.ShapeDtypeStruct((T, D), F32), jax.ShapeDtypeStruct((T, d_in_width), MXU_DTYPE)],
        compiler_params=_params(("parallel",)),
    )(dh, w_out, w_a, w_b, p_a, p_b, gates, gates)


def _colsum_partials(p):
    return jnp.sum(p.reshape(-1, 8, p.shape[-1])[:, 0, :], axis=0, keepdims=True)


def _rmsnorm_bwd_vals(dy, xin, g):
    rstd = lax.rsqrt(jnp.mean(xin * xin, axis=-1, keepdims=True) + EPS)
    xhat = xin * rstd
    dg = jnp.sum(dy * xhat, axis=0, keepdims=True)
    dxh = dy * g
    dx = rstd * (dxh - xhat * jnp.mean(dxh * xhat, axis=-1, keepdims=True))
    return dx, dg


ATTN_SCALE = 1.0 / math.sqrt(HEAD_DIM)
GROUP_LANES = GROUP * ATTN_BLOCK
PAIR = 2 * HEAD_DIM


def _attn_mask():
    kj = lax.broadcasted_iota(jnp.int32, (ATTN_BLOCK, GROUP_LANES), 0)
    qi = lax.broadcasted_iota(jnp.int32, (ATTN_BLOCK, GROUP_LANES), 1) & (ATTN_BLOCK - 1)
    return kj <= qi


def _heads_transposed(ref, g, scale=None):
    parts = []
    for a in range(GROUP // 2):
        lo = (g * GROUP // 2 + a) * PAIR
        pair = ref[:, lo:lo + PAIR].astype(F32)
        pair = (pair if scale is None else pair * scale).T
        parts += [pair[:HEAD_DIM], pair[HEAD_DIM:]]
    return jnp.concatenate(parts, axis=1).astype(MXU_DTYPE)


def _heads_back(ref, g, vt):
    for a in range(GROUP // 2):
        lo = (g * GROUP // 2 + a) * PAIR
        pair = jnp.concatenate([vt[:, (2 * a) * ATTN_BLOCK:(2 * a + 1) * ATTN_BLOCK],
                                vt[:, (2 * a + 1) * ATTN_BLOCK:(2 * a + 2) * ATTN_BLOCK]], axis=0)
        ref[:, lo:lo + PAIR] = pair.T.astype(ref.dtype)


def _kv_parts(kv_ref, g):
    ks = slice(g * HEAD_DIM, (g + 1) * HEAD_DIM)
    vs = slice(KV_WIDTH + g * HEAD_DIM, KV_WIDTH + (g + 1) * HEAD_DIM)
    return kv_ref[:, ks].astype(MXU_DTYPE), kv_ref[:, vs].astype(MXU_DTYPE)


def _sink_rows(sinks):
    return jnp.repeat(sinks.reshape(KV_HEADS, GROUP), ATTN_BLOCK, axis=1)


def _attn_fwd(pq, pkv, sinks, comm=None):
    T = pq.shape[0]
    nb = T // ATTN_BLOCK

    def body(q_ref, kvc_ref, kvp_ref, s_ref, y_ref, lse_ref):
        mask_c = _attn_mask()
        has_prev = pl.program_id(0) > 0
        early = []
        for g in range(KV_HEADS):
            (kc, vc), (kp, vp) = _kv_parts(kvc_ref, g), _kv_parts(kvp_ref, g)
            qt = _heads_transposed(q_ref, g, ATTN_SCALE)
            early.append((vc, vp, jnp.where(mask_c, _nn(kc, qt), jnp.where(has_prev, _nn(kp, qt), NEG_INF))))
        for g, (vc, vp, s) in enumerate(early):
            sink = s_ref[g:g + 1, :]
            m = jnp.maximum(jnp.max(s, axis=0, keepdims=True), sink)
            p = jnp.exp(s - m)
            den = jnp.sum(p, axis=0, keepdims=True) + jnp.exp(sink - m)
            pc = jnp.where(mask_c, p, 0.0)
            _heads_back(y_ref, g, (_tn(vc, pc) + _tn(vp, p - pc)) / den)
            lse = m + jnp.log(den)
            for i in range(GROUP):
                lse_ref[g * GROUP + i:g * GROUP + i + 1, :] = lse[:, i * ATTN_BLOCK:(i + 1) * ATTN_BLOCK]

    return _call(
        "attn_fwd", body, grid=(nb,),
        in_specs=[pl.BlockSpec((ATTN_BLOCK, D_MODEL), lambda n: (n, 0)),
                  pl.BlockSpec((ATTN_BLOCK, 2 * KV_WIDTH), lambda n: (n, 0)),
                  pl.BlockSpec((ATTN_BLOCK, 2 * KV_WIDTH), lambda n: (jnp.maximum(n - 1, 0), 0)),
                  pl.BlockSpec((KV_HEADS, GROUP_LANES), lambda n: (0, 0))],
        out_specs=[pl.BlockSpec((ATTN_BLOCK, D_MODEL), lambda n: (n, 0)),
                   pl.BlockSpec((Q_HEADS, ATTN_BLOCK), lambda n: (0, n))],
        out_shape=[jax.ShapeDtypeStruct((T, D_MODEL), MXU_DTYPE), jax.ShapeDtypeStruct((Q_HEADS, T), F32)],
        args=[pq, pkv, pkv, _sink_rows(sinks)], semantics=("parallel",), comm=comm)


def _attn_bwd(pq, pkv, sinks, lse, dy, d_in, comm=None):
    T = pq.shape[0]
    nb = T // ATTN_BLOCK
    cur = lambda n: (jnp.minimum(n, nb - 1), 0)
    done = D_MODEL + 2 * KV_WIDTH

    def body(q_ref, kvc_ref, kvp_ref, s_ref, lse_ref, dy_ref, _, out_ref, ds_ref, carry, top, bot, dq_ref):
        n = pl.program_id(0)

        @pl.when(n == 0)
        def _():
            carry[...] = jnp.zeros_like(carry)
            dq_ref[...] = jnp.zeros_like(dq_ref)
            ds_ref[...] = jnp.zeros_like(ds_ref)

        out_ref[:, :D_MODEL] = dq_ref[...]

        @pl.when(n < nb)
        def _():
            mask_c = _attn_mask()
            valid = jnp.logical_or(mask_c, n > 0)
            early = []
            for g in range(KV_HEADS):
                (kc, vc), (kp, vp) = _kv_parts(kvc_ref, g), _kv_parts(kvp_ref, g)
                qt = _heads_transposed(q_ref, g, ATTN_SCALE)
                dot = _heads_transposed(dy_ref, g)
                early.append((kc, kp, qt, dot, jnp.where(mask_c, _nn(kc, qt), _nn(kp, qt)),
                              jnp.where(mask_c, _nn(vc, dot), _nn(vp, dot))))
            for g, (kc, kp, qt, dot, s, dp) in enumerate(early):
                ks = slice(g * HEAD_DIM, (g + 1) * HEAD_DIM)
                vs = slice(KV_WIDTH + g * HEAD_DIM, KV_WIDTH + (g + 1) * HEAD_DIM)
                lse = jnp.concatenate([lse_ref[g * GROUP + i:g * GROUP + i + 1, :] for i in range(GROUP)], axis=1)
                p = jnp.where(valid, jnp.exp(s - lse), 0.0)
                delta = jnp.sum(p * dp, axis=0, keepdims=True)
                ds = p * (dp - delta)
                ds_c, p_c = jnp.where(mask_c, ds, 0.0), jnp.where(mask_c, p, 0.0)
                ds_p, p_p = ds - ds_c, p - p_c
                _heads_back(dq_ref, g, (_tn(kc, ds_c) + _tn(kp, ds_p)) * ATTN_SCALE)
                bot[:, ks], bot[:, vs] = _nt(ds_c, qt), _nt(p_c, dot)
                top[:, ks], top[:, vs] = _nt(ds_p, qt), _nt(p_p, dot)
                ds_ref[g:g + 1, :] -= jnp.exp(s_ref[g:g + 1, :] - lse) * delta
            out_ref[:, D_MODEL:] = (carry[...] + top[...]).astype(out_ref.dtype)
            carry[...] = bot[...]

        @pl.when(n == nb)
        def _():
            out_ref[:, D_MODEL:] = carry[...].astype(out_ref.dtype)

    return _call(
        "attn_bwd", body, grid=(nb + 1,),
        in_specs=[pl.BlockSpec((ATTN_BLOCK, D_MODEL), cur),
                  pl.BlockSpec((ATTN_BLOCK, 2 * KV_WIDTH), cur),
                  pl.BlockSpec((ATTN_BLOCK, 2 * KV_WIDTH), lambda n: (jnp.maximum(jnp.minimum(n, nb - 1) - 1, 0), 0)),
                  pl.BlockSpec((KV_HEADS, GROUP_LANES), lambda n: (0, 0)),
                  pl.BlockSpec((Q_HEADS, ATTN_BLOCK), lambda n: (0, jnp.minimum(n, nb - 1))),
                  pl.BlockSpec((ATTN_BLOCK, D_MODEL), cur), HBM_SPEC],
        out_specs=[pl.BlockSpec((ATTN_BLOCK, done), lambda n: (jnp.maximum(n - 1, 0), 0)),
                   pl.BlockSpec((KV_HEADS, GROUP_LANES), lambda n: (0, 0))],
        out_shape=[jax.ShapeDtypeStruct(d_in.shape, d_in.dtype), jax.ShapeDtypeStruct((KV_HEADS, GROUP_LANES), F32)],
        scratch=[pltpu.VMEM((ATTN_BLOCK, 2 * KV_WIDTH), F32)] * 3 + [pltpu.VMEM((ATTN_BLOCK, D_MODEL), MXU_DTYPE)],
        args=[pq, pkv, pkv, _sink_rows(sinks), lse, dy, d_in], semantics=("arbitrary",), comm=comm, aliases={6: 0})


def _lower_bound(l):
    m = jnp.maximum(l[0:1], l[1:2])
    e0, e1 = jnp.exp(l[0:1] - m), jnp.exp(l[1:2] - m)
    return e0 / (e0 + e1)


def _tri(lower):
    r = lax.broadcasted_iota(jnp.int32, (CHUNK, CHUNK), 0)
    c = lax.broadcasted_iota(jnp.int32, (CHUNK, CHUNK), 1)
    return (r >= c) if lower else (c >= r)


def _chunk_sum(mask, v):
    ones = mask.astype(BF16)
    hi = v.astype(BF16)
    rest = v - hi.astype(F32)
    mid = rest.astype(BF16)
    lo = (rest - mid.astype(F32)).astype(BF16)
    part = lambda t: lax.dot_general(ones, t, (((1,), (0,)), ((), ())), preferred_element_type=F32)
    return part(hi) + part(mid) + part(lo)


def _hgrn_chunk_inputs(hq, hf, lb, causal):
    half_t = 0.5 * jnp.tanh(0.5 * hf)
    sg, sgn = 0.5 + half_t, 0.5 - half_t
    f = lb + (1.0 - lb) * sg
    kk = (1.0 - lb) * sgn
    sq = _sigmoid(hq)
    q = hq * sq
    b = _chunk_sum(causal, jnp.log(f))
    bm, bl = b[CHUNK // 2 - 1:CHUNK // 2, :], b[CHUNK - 1:CHUNK, :]
    e_qm, e_km = jnp.exp(b - bm), jnp.exp(bm - b)
    e_qs, e_kl = e_qm * jnp.exp(bm), e_km * jnp.exp(bl - bm)
    return dict(sg=sg, sgn=sgn, f=f, kk=kk, sq=sq, q=q, e_qm=e_qm, e_km=e_km, e_qs=e_qs, e_kl=e_kl,
                qm=q * e_qm, km=kk * e_km, qs=q * e_qs, kl=kk * e_kl, el=jnp.exp(bl))


def _hgrn_fwd(ph, lb_logits, norm_g, comm=None):
    T = ph.shape[0]
    nblk, cpb = T // HGRN_TOKENS, HGRN_TOKENS // CHUNK
    col = lambda c: pl.BlockSpec((HGRN_TOKENS, D_MODEL), functools.partial(lambda i, c: (i, c), c=c))

    def body(hq_ref, hf_ref, hi_ref, hg_ref, l_ref, ng_ref, y_ref, o_ref, st_ref, s_ref):
        @pl.when(pl.program_id(0) == 0)
        def _():
            s_ref[...] = jnp.zeros_like(s_ref)

        lb = _lower_bound(l_ref[...])
        causal = _tri(True)
        for c in range(cpb):
            rows = slice(c * CHUNK, (c + 1) * CHUNK)
            t = _hgrn_chunk_inputs(hq_ref[rows, :], hf_ref[rows, :], lb, causal)
            qm, km, qs, kl = (t[n].astype(MXU_DTYPE) for n in ("qm", "km", "qs", "kl"))
            v = hi_ref[rows, :].astype(MXU_DTYPE)
            heads = [slice(h * HGRN_K, (h + 1) * HGRN_K) for h in range(HGRN_HEADS)]
            a_all = [jnp.where(causal, _nt(qm[:, ls], km[:, ls]), 0.0).astype(MXU_DTYPE) for ls in heads]
            for h, ls in enumerate(heads):
                st = s_ref[h]
                st_ref[c, ls, :] = st
                o_ref[rows, ls] = _nn(a_all[h], v[:, ls]) + _nt(qs[:, ls], st)
                s_ref[h] = t["el"][:, ls] * st + _tn(v[:, ls], kl[:, ls])
        for h in range(HGRN_HEADS):
            ls = slice(h * HGRN_K, (h + 1) * HGRN_K)
            o = o_ref[:, ls]
            r = lax.rsqrt(jnp.mean(o * o, axis=-1, keepdims=True) + EPS)
            y_ref[:, ls] = (o * r * ng_ref[:, ls] * _sigmoid(hg_ref[:, ls])).astype(y_ref.dtype)

    return _call(
        "hgrn_fwd", body, grid=(nblk,),
        in_specs=[col(0), col(1), col(2), col(3),
                  pl.BlockSpec((2, D_MODEL), lambda i: (0, 0)), pl.BlockSpec((1, D_MODEL), lambda i: (0, 0))],
        out_specs=[pl.BlockSpec((HGRN_TOKENS, D_MODEL), lambda i: (i, 0)),
                   pl.BlockSpec((HGRN_TOKENS, D_MODEL), lambda i: (i, 0)),
                   pl.BlockSpec((cpb, D_MODEL, HGRN_K), lambda i: (i, 0, 0))],
        out_shape=[jax.ShapeDtypeStruct((T, D_MODEL), MXU_DTYPE), jax.ShapeDtypeStruct((T, D_MODEL), F32),
                   jax.ShapeDtypeStruct((T // CHUNK, D_MODEL, HGRN_K), F32)],
        scratch=[pltpu.VMEM((HGRN_HEADS, HGRN_K, HGRN_K), F32)],
        args=[ph, ph, ph, ph, lb_logits, norm_g], semantics=("arbitrary",), comm=comm)


def _hgrn_bwd(ph, o_raw, states, dy, lb_logits, norm_g, d_in, first, comm=None):
    T = ph.shape[0]
    nblk, cpb = T // HGRN_TOKENS, HGRN_TOKENS // CHUNK
    rev = lambda i: nblk - 1 - i
    col = lambda c: pl.BlockSpec((HGRN_TOKENS, D_MODEL), functools.partial(lambda i, c: (rev(i), c), c=c))
    tok = pl.BlockSpec((HGRN_TOKENS, D_MODEL), lambda i: (rev(i), 0))

    def body(hq_ref, hf_ref, hi_ref, hg_ref, o_ref, st_ref, dy_ref, l_ref, ng_ref, _,
             dph_ref, dng_ref, dl_ref, dst_ref, dlb_ref, do_s, dqm_s, dkm_s, dqs_s, dkl_s, dv_s, del_s):
        i = pl.program_id(0)

        @pl.when(i == 0)
        def _():
            dst_ref[...] = jnp.zeros_like(dst_ref)
            dlb_ref[...] = jnp.zeros_like(dlb_ref)
            dng_ref[...] = jnp.zeros_like(dng_ref)

        lb = _lower_bound(l_ref[...])
        causal, anti = _tri(True), _tri(False)
        row = lax.broadcasted_iota(jnp.int32, (CHUNK, D_MODEL), 0)
        for c in reversed(range(cpb)):
            rows = slice(c * CHUNK, (c + 1) * CHUNK)
            hq = hq_ref[rows, :]
            t = _hgrn_chunk_inputs(hq, hf_ref[rows, :], lb, causal)
            sgg = _sigmoid(hg_ref[rows, :])
            dyv = dy_ref[rows, :]
            for h in range(HGRN_HEADS):
                ls = slice(h * HGRN_K, (h + 1) * HGRN_K)
                o = o_ref[rows, ls]
                r = lax.rsqrt(jnp.mean(o * o, axis=-1, keepdims=True) + EPS)
                nrm = o * r
                g_h = sgg[:, ls]
                dph_ref[rows, 3 * D_MODEL + h * HGRN_K:3 * D_MODEL + (h + 1) * HGRN_K] = (
                    dyv[:, ls] * nrm * ng_ref[:, ls] * g_h * (1.0 - g_h)).astype(dph_ref.dtype)
                dyg = dyv[:, ls] * g_h
                dng_ref[:, ls] += jnp.sum(dyg * nrm, axis=0, keepdims=True)
                dn = dyg * ng_ref[:, ls]
                do_s[:, ls] = r * (dn - nrm * jnp.mean(dn * nrm, axis=-1, keepdims=True))
            qm, km, qs, kl = (t[n].astype(MXU_DTYPE) for n in ("qm", "km", "qs", "kl"))
            v = hi_ref[rows, :].astype(MXU_DTYPE)
            do = do_s[...].astype(MXU_DTYPE)
            heads = [slice(h * HGRN_K, (h + 1) * HGRN_K) for h in range(HGRN_HEADS)]
            a_all = [jnp.where(causal, _nt(qm[:, ls], km[:, ls]), 0.0).astype(MXU_DTYPE) for ls in heads]
            da_all = [jnp.where(causal, _nt(do[:, ls], v[:, ls]), 0.0).astype(MXU_DTYPE) for ls in heads]
            for h, ls in enumerate(heads):
                st = st_ref[c, ls, :]
                dst = dst_ref[h]
                a, da = a_all[h], da_all[h]
                dv_s[:, ls] = _tn(a, do[:, ls]) + _nt(kl[:, ls], dst)
                dkl_s[:, ls] = _nn(v[:, ls], dst)
                dqs_s[:, ls] = _nn(do[:, ls], st)
                del_s[:, ls] = jnp.sum(dst * st, axis=0, keepdims=True)
                dst_ref[h] = _tn(do[:, ls], qs[:, ls]) + t["el"][:, ls] * dst
                dqm_s[:, ls] = _nn(da, km[:, ls])
                dkm_s[:, ls] = _tn(da, qm[:, ls])
            dqm, dkm, dqs, dkl = dqm_s[...], dkm_s[...], dqs_s[...], dkl_s[...]
            dq = dqm * t["e_qm"] + dqs * t["e_qs"]
            dk = dkm * t["e_km"] + dkl * t["e_kl"]
            t_qm, t_km, t_kl = dqm * t["qm"], dkm * t["km"], dkl * t["kl"]
            db = t_qm - t_km + dqs * t["qs"] - t_kl
            db_mid = jnp.sum(t_km - t_qm, axis=0, keepdims=True)
            db_last = jnp.sum(t_kl, axis=0, keepdims=True) + del_s[...] * t["el"]
            db = db + jnp.where(row == CHUNK // 2 - 1, db_mid, 0.0) + jnp.where(row == CHUNK - 1, db_last, 0.0)
            dlogf = _chunk_sum(anti, db)
            sq, sg, sgn, f = t["sq"], t["sg"], t["sgn"], t["f"]
            dph_ref[rows, 0:D_MODEL] = (dq * (sq * (1.0 + hq * (1.0 - sq)))).astype(dph_ref.dtype)
            dph_ref[rows, D_MODEL:2 * D_MODEL] = (
                dlogf * (1.0 - lb) * sg * (1.0 - sg) / f - dk * (1.0 - lb) * sgn * (1.0 - sgn)).astype(dph_ref.dtype)
            dph_ref[rows, 2 * D_MODEL:3 * D_MODEL] = dv_s[...].astype(dph_ref.dtype)
            dlb_ref[...] += jnp.sum(dlogf * (1.0 - sg) / f - dk * sgn, axis=0, keepdims=True)

        @pl.when(i == nblk - 1)
        def _():
            dl0 = dlb_ref[...] * lb * (1.0 - lb)
            dl_ref[0:1, :] = dl0
            dl_ref[1:2, :] = -dl0

    wide = pltpu.VMEM((CHUNK, D_MODEL), F32)
    return _call(
        "hgrn_bwd", body, grid=(nblk,),
        in_specs=[col(0), col(1), col(2), col(3), tok,
                  pl.BlockSpec((cpb, D_MODEL, HGRN_K), lambda i: (rev(i), 0, 0)), tok,
                  pl.BlockSpec((2, D_MODEL), lambda i: (0, 0)), pl.BlockSpec((1, D_MODEL), lambda i: (0, 0)), HBM_SPEC],
        out_specs=[pl.BlockSpec((pl.Element(HGRN_TOKENS), pl.Element(4 * D_MODEL)), lambda i: (
                       pl.multiple_of(rev(i) * HGRN_TOKENS, ROW_ALIGN), first)),
                   pl.BlockSpec((1, D_MODEL), lambda i: (0, 0)), pl.BlockSpec((2, D_MODEL), lambda i: (0, 0))],
        out_shape=[jax.ShapeDtypeStruct(d_in.shape, d_in.dtype), jax.ShapeDtypeStruct((1, D_MODEL), F32),
                   jax.ShapeDtypeStruct((2, D_MODEL), F32)],
        scratch=[pltpu.VMEM((HGRN_HEADS, HGRN_K, HGRN_K), F32), pltpu.VMEM((1, D_MODEL), F32),
                 wide, wide, wide, wide, wide, wide, pltpu.VMEM((1, D_MODEL), F32)],
        args=[ph, ph, ph, ph, o_raw, states, dy, lb_logits, norm_g, d_in], semantics=("arbitrary",), comm=comm,
        aliases={9: 0})


def _local_step(x, target, vec, net):
    T, D = x.shape
    norm_mix_g, b_in, sinks, lb_logits = vec["norm_mix_g"], vec["b_in"], vec["attn_sinks"], vec["hgrn_lb_logits"]
    hgrn_norm_g, norm_ffn_g, norm_final_g = vec["hgrn_norm_g"], vec["norm_ffn_g"], vec["norm_final_g"]
    w_in = net.full("w_in")
    o_q, o_kv, o_h, o_g = (sum(IN_SPLITS[:i]) for i in range(4))
    TM = 512

    first = ("w_branch_attn", "w_branch_hgrn", "w_ffn_down")
    (u, pq, pkv, ph, pg), got = _in_proj(x, norm_mix_g, w_in, b_in, tm=512, comm=net.fetch(first))
    net.fetched(first, got)
    second = ("w_ffn_gate", "w_out")
    (y_attn, lse), got = _attn_fwd(pq, pkv, sinks, comm=_join(net.relay(first), net.fetch(second)))
    net.gathered(first, got[:len(first)])
    net.fetched(second, got[len(first):])
    third = ("w_ffn_up",)
    (y_hgrn, o_raw, states), got = _hgrn_fwd(ph, lb_logits, hgrn_norm_g,
                                             comm=_join(net.relay(second), net.fetch(third)))
    net.gathered(second, got[:len(second)])
    net.fetched(third, got[len(second):])
    w_ba, w_bh = net.full("w_branch_attn"), net.full("w_branch_hgrn")
    (ya, yb, merged), got = _merge_fwd(y_attn, y_hgrn, w_ba, w_bh, pg, tm=TM, comm=net.relay(third))
    net.gathered(third, got)
    w_gate, w_up, w_out = net.full("w_ffn_gate"), net.full("w_ffn_up"), net.full("w_out")

    (h1, u2, gpre, up, z), _ = _ffn_fwd(merged, w_out, x, norm_ffn_g, w_gate, w_up, tm=512)
    w_down = net.full("w_ffn_down")

    dh2, dh2b, dgp, dup, dgf_p, loss_p = _ffn_tail(z, w_down, h1, target, norm_final_g, gpre, up, tm=512)
    loss = jnp.sum(loss_p.reshape(-1, 8, D)[:, 0, 0])
    d_norm_final = _colsum_partials(dgf_p)
    d_w_down = _weight_grad("dw_down", z, dh2b, tm=DW_ROWS, tk=T)

    names, swap = ("w_ffn_down",), ()
    (dh1, dh1b, dg2_p), got = _ffn_in_bwd(dgp, dup, w_gate, w_up, h1, norm_ffn_g, dh2, tm=256,
                                          comm=net.exchange(dict(w_ffn_down=[d_w_down]), swap))
    net.received(names, swap, got)
    d_norm_ffn = _colsum_partials(dg2_p)
    d_w_gate = _weight_grad("dw_gate", dgp, u2, tm=DW_ROWS, tk=T)
    d_w_up = _weight_grad("dw_up", dup, u2, tm=DW_ROWS, tk=T)

    rows_in = sum(IN_SPLITS)
    dya, dyb, dy_attn, dy_hgrn, d_in = _merge_bwd(dh1b, w_out, w_ba, w_bh, ya, yb, pg, tm=TM, d_in_width=rows_in,
                                                  first=o_g)
    d_w_out = _weight_grad("dw_out", merged, dh1b, tm=1024, tk=1024)
    d_w_ba = _weight_grad("dw_branch_a", y_attn, dya, tm=1024, tk=1024)
    d_w_bh = _weight_grad("dw_branch_b", y_hgrn, dyb, tm=1024, tk=1024)

    names, swap = ("w_ffn_gate",), ("w_ffn_down",)
    (d_in, dsink), got = _attn_bwd(pq, pkv, sinks, lse, dy_attn, d_in,
                                   comm=net.exchange(dict(w_ffn_gate=[d_w_gate]), swap))
    net.received(names, swap, got)
    names, swap = ("w_ffn_up", "w_out"), ("w_ffn_gate",)
    (d_in, d_hgrn_norm, d_lb_logits), got = _hgrn_bwd(
        ph, o_raw, states, dy_hgrn, lb_logits, hgrn_norm_g, d_in, o_h,
        comm=net.exchange(dict(w_ffn_up=[d_w_up], w_out=[d_w_out]), swap))
    net.received(names, swap, got)

    names, swap = ("w_branch_attn", "w_branch_hgrn"), ("w_ffn_up", "w_out")
    (*d_w_in, d_b_in), got = _weight_grad(
        "dw_in", d_in, u, tm=DW_ROWS, tk=T, a_colsum=True, carrying=True,
        comm=net.exchange(dict(w_branch_attn=[d_w_ba], w_branch_hgrn=[d_w_bh]), swap))
    net.received(names, swap, got)
    d_w_in = [tuple(d_w_in)]

    first_level = net.presum_begin("w_in", d_w_in)
    halves = net.presum_end("w_in", [] if first_level is None else _copies_alone("presum_swap_w_in", first_level))
    names, swap = ("w_in",), ("w_branch_attn", "w_branch_hgrn")
    (dx, dg1_p), got = _in_proj_bwd([(d_in, 0)], w_in, x, norm_mix_g, dh1, tm=256,
                                    comm=_join(halves, net.swap(swap)))
    net.last = (names, swap, got)
    d_norm_mix = _colsum_partials(dg1_p)
    vecs = dict(norm_mix_g=d_norm_mix, b_in=d_b_in, attn_sinks=jnp.sum(dsink.reshape(Q_HEADS, ATTN_BLOCK), axis=1).reshape(1, Q_HEADS),
                hgrn_lb_logits=d_lb_logits,
                hgrn_norm_g=d_hgrn_norm, norm_ffn_g=d_norm_ffn, norm_final_g=d_norm_final)
    return loss, dx, vecs


def _place():
    return lax.axis_index("x"), lax.axis_index("y"), lax.axis_index("c")


def _other_chips(x, y):
    return [(1 - x, y), (x, 1 - y), (1 - x, 1 - y)]


def _y_first(copies):
    return [copies[3 * (i // 3) + (1, 0, 2)[i % 3]] for i in range(len(copies))]


def _fetch_copies(shards):
    n = len(shards)

    def build(ins, outs, send_sems, recv_sems, local_sems, *stages):
        x, y, c = _place()
        mine = 2 * x + y
        local = [(pltpu.make_async_copy(ins[w], stages[w], local_sems.at[2 * w]),
                  pltpu.make_async_copy(stages[w], outs[w].at[mine], local_sems.at[2 * w + 1])) for w in range(n)]
        sends, recvs = [], []
        for w in range(n):
            half = shards[w].shape[0] // 2
            rows = pl.ds(c * half, half)
            for k, (px, py) in enumerate(_other_chips(x, y)):
                sem = 3 * w + k
                sends.append(pltpu.make_async_remote_copy(
                    src_ref=ins[w].at[rows], dst_ref=outs[w].at[mine, rows], send_sem=send_sems.at[sem],
                    recv_sem=recv_sems.at[sem], device_id=(px, py, c), device_id_type=MESH_ID))
                recvs.append(pltpu.make_async_remote_copy(
                    src_ref=ins[w].at[rows], dst_ref=outs[w].at[2 * px + py, rows], send_sem=send_sems.at[sem],
                    recv_sem=recv_sems.at[sem], device_id=(px, py, c), device_id_type=MESH_ID))
        return sends, recvs, local, _y_first(sends)

    return _Carried(shards, [jax.ShapeDtypeStruct((N_CHIPS,) + s.shape, s.dtype) for s in shards], 3 * n, 2 * n, build,
                    stages=[pltpu.VMEM(s.shape, s.dtype) for s in shards])


def _relay_copies(fetched):
    n = len(fetched)

    def build(ins, outs, send_sems, recv_sems, local_sems):
        x, y, c = _place()
        sends, recvs = [], []
        for w in range(n):
            half = fetched[w].shape[1] // 2
            mine, theirs = pl.ds(c * half, half), pl.ds((1 - c) * half, half)
            for k, (px, py) in enumerate(_other_chips(x, y)):
                block, sem = 2 * px + py, 3 * w + k
                sends.append(pltpu.make_async_remote_copy(
                    src_ref=ins[w].at[block, mine], dst_ref=outs[w].at[block, mine], send_sem=send_sems.at[sem],
                    recv_sem=recv_sems.at[sem], device_id=(x, y, 1 - c), device_id_type=MESH_ID))
                recvs.append(pltpu.make_async_remote_copy(
                    src_ref=ins[w].at[block, theirs], dst_ref=outs[w].at[block, theirs], send_sem=send_sems.at[sem],
                    recv_sem=recv_sems.at[sem], device_id=(x, y, 1 - c), device_id_type=MESH_ID))
        return sends, recvs, []

    return _Carried(fetched, [jax.ShapeDtypeStruct(f.shape, f.dtype) for f in fetched], 3 * n, 0, build,
                    continued={w: w for w in range(n)})


def _grad_copies(stacked):
    n = len(stacked)

    def build(ins, outs, send_sems, recv_sems, local_sems):
        x, y, c = _place()
        sends = []
        for w in range(n):
            for k, (px, py) in enumerate(_other_chips(x, y)):
                sem = 3 * w + k
                sends.append(pltpu.make_async_remote_copy(
                    src_ref=ins[w].at[2 * px + py], dst_ref=outs[w].at[k], send_sem=send_sems.at[sem],
                    recv_sem=recv_sems.at[sem], device_id=(px, py, c), device_id_type=MESH_ID))
        return sends, sends, [], _y_first(sends)

    return _Carried(stacked, [jax.ShapeDtypeStruct((3,) + s.shape[1:], s.dtype) for s in stacked], 3 * n, 0, build)


def _small_copies(small):
    def build(ins, outs, send_sems, recv_sems, local_sems):
        small_ref, all_ref = ins[0], outs[0]
        x, y, c = _place()
        me = 4 * x + 2 * y + c
        sends, recvs = [], []
        for r in range(1, 8):
            px = 1 - x if r & 4 else x
            py = 1 - y if r & 2 else y
            pc = 1 - c if r & 1 else c
            sends.append(pltpu.make_async_remote_copy(
                src_ref=small_ref, dst_ref=all_ref.at[me], send_sem=send_sems.at[r - 1], recv_sem=recv_sems.at[r - 1],
                device_id=(px, py, pc), device_id_type=MESH_ID))
            recvs.append(pltpu.make_async_remote_copy(
                src_ref=small_ref, dst_ref=all_ref.at[4 * px + 2 * py + pc], send_sem=send_sems.at[r - 1],
                recv_sem=recv_sems.at[r - 1], device_id=(px, py, pc), device_id_type=MESH_ID))
        return sends, recvs, [pltpu.make_async_copy(small_ref, all_ref.at[me], local_sems.at[0])]

    return _Carried([small], [jax.ShapeDtypeStruct((8,) + small.shape, small.dtype)], 7, 1, build)


def _gather_by_neighbours(name, shard):
    half = shard.shape[0] // 2
    quarter = half // 2

    def body(in_ref, out_ref, send_sems, recv_sems, local_sem, stage_ref):
        for core in (0, 1):
            @pl.when(lax.axis_index("c") == core)
            def _():
                program(core, in_ref, out_ref, send_sems, recv_sems, local_sem, stage_ref)

    def program(c, in_ref, out_ref, send_sems, recv_sems, local_sem, stage_ref):
        x, y, _ = _place()
        chip = lambda px, py: 2 * px + py
        to_x, to_y, sibling = (1 - x, y, c), (x, 1 - y, c), (x, y, 1 - c)
        x_blk, y_blk, d_blk = chip(1 - x, y), chip(x, 1 - y), chip(1 - x, 1 - y)
        mine, theirs = c * half, (1 - c) * half

        def copy(sem, rows, block, to, src=None):
            place = out_ref.at[block, pl.ds(rows[0], rows[1])]
            return pltpu.make_async_remote_copy(
                src_ref=place if src is None else src, dst_ref=place, send_sem=send_sems.at[sem],
                recv_sem=recv_sems.at[sem], device_id=to, device_id_type=MESH_ID)

        stage = pltpu.make_async_copy(in_ref, stage_ref, local_sem.at[0])
        own = pltpu.make_async_copy(stage_ref, out_ref.at[chip(x, y)], local_sem.at[1])
        my_rows = in_ref.at[pl.ds(mine, half)]
        along_x = dict(send=copy(0, (mine, half), chip(x, y), to_x, src=my_rows),
                       landed=copy(0, (mine, half), x_blk, to_x),
                       onward=[copy(3, (mine + quarter, quarter), x_blk, to_y), copy(4, (mine, half), x_blk, sibling)],
                       diagonal=copy(2, (mine, quarter), d_blk, to_x))
        along_y = dict(send=copy(1, (mine, half), chip(x, y), to_y, src=my_rows),
                       landed=copy(1, (mine, half), y_blk, to_y),
                       onward=[copy(2, (mine, quarter), y_blk, to_x), copy(5, (mine, half), y_blk, sibling)],
                       diagonal=copy(3, (mine + quarter, quarter), d_blk, to_y))
        last = copy(6, (mine, half), d_blk, sibling)

        order = (along_x, along_y) if c == 0 else (along_y, along_x)
        for axis in order:
            axis["send"].start()
        stage.start()
        stage.wait()
        own.start()
        for axis in order:
            axis["landed"].wait_recv()
            for cp in axis["onward"]:
                cp.start()
        for axis in order:
            axis["diagonal"].wait_recv()
        last.start()
        for sem, block in ((4, x_blk), (5, y_blk), (6, d_blk)):
            copy(sem, (theirs, half), block, sibling).wait_recv()
        for cp in [along_x["send"], along_y["send"]] + along_x["onward"] + along_y["onward"] + [last]:
            cp.wait_send()
        own.wait()

    return pl.pallas_call(
        body, name=name, in_specs=[HBM_SPEC], out_specs=HBM_SPEC,
        out_shape=jax.ShapeDtypeStruct((N_CHIPS,) + shard.shape, shard.dtype),
        scratch_shapes=[pltpu.SemaphoreType.DMA((7,)), pltpu.SemaphoreType.DMA((7,)), pltpu.SemaphoreType.DMA((2,)),
                        pltpu.VMEM(shard.shape, shard.dtype)],
    )(shard)


def _copies_alone(name, comm):
    return _call(name, lambda: None, grid=(), in_specs=[], out_specs=[], out_shape=[], args=[], comm=comm)[1]


class _Net:
    def __init__(self, shards):
        self.shards = shards
        self.whole, self.partly, self.own, self.theirs, self.sums, self.other = {}, {}, {}, {}, {}, {}
        x, y, _ = _place()
        self.chip = 2 * x + y

    def fetch(self, names):
        return _fetch_copies([self.shards[n] for n in names])

    def fetched(self, names, got):
        self.partly.update(zip(names, got))

    def relay(self, names):
        return _relay_copies([self.partly[n] for n in names])

    def gathered(self, names, got):
        for n, g in zip(names, got):
            self.whole[n] = g.reshape(-1, g.shape[-1])

    def full(self, name):
        return self.whole[name]

    def exchange(self, grads, swap=()):
        stacked = []
        for n, pieces in grads.items():
            (keep, send), = pieces
            self.own[n] = keep
            stacked.append(send.reshape(N_CHIPS, keep.shape[0] // N_CHIPS, send.shape[-1]))
        return _join(_grad_copies(stacked), self.swap(swap))

    def swap(self, names):
        return _sibling_copies([self.sums[n] for n in names]) if names else None

    def presum_begin(self, name, pieces):
        keep = jnp.concatenate([p[0] for p in pieces], axis=0) if len(pieces) > 1 else pieces[0][0]
        send = jnp.concatenate([p[1] for p in pieces], axis=0) if len(pieces) > 1 else pieces[0][1]
        rows = keep.shape[0] // N_CHIPS
        self.held = keep.reshape(N_CHIPS, rows, keep.shape[-1])
        return _half_rows_copies(send.reshape(N_CHIPS, rows, send.shape[-1]))

    def presum_end(self, name, got):
        x, y, c = _place()
        to_send, self.own[name] = _pre_sum("presum_" + name, self.held, got[0], jnp.stack([c, self.chip]))
        return _grad_copies([to_send])

    def received(self, names, swap, got, carried=None):
        self.theirs.update(zip(names, got[:len(names)]))
        self.other.update(zip(swap, got[len(names):]))
        for n in names:
            (self.sums[n],), more = _partial_sum("sum_" + n, self.own[n], self.theirs[n], self.chip, comm=carried)
        return more


def _half_rows_copies(stacked):
    n, rows = stacked.shape[0], stacked.shape[1] // 2

    def build(ins, outs, send_sems, recv_sems, local_sems):
        x, y, c = _place()
        copies = [pltpu.make_async_remote_copy(
            src_ref=ins[0].at[s, pl.ds((1 - c) * rows, rows)], dst_ref=outs[0].at[s], send_sem=send_sems.at[s],
            recv_sem=recv_sems.at[s], device_id=(x, y, 1 - c), device_id_type=MESH_ID) for s in range(n)]
        return copies, copies, []

    return _Carried([stacked], [jax.ShapeDtypeStruct((n, rows, stacked.shape[2]), stacked.dtype)], n, 0, build)


def _pre_sum(name, held, theirs, core_and_chip):
    n, R, C = held.shape
    half = R // 2
    tr = _row_tile(half)
    per_half = half // tr

    def body(place_ref, h_ref, t_ref, send_ref, own_ref):
        total = h_ref[0] + t_ref[0].astype(F32)
        send_ref[0] = total.astype(send_ref.dtype)

        @pl.when(pl.program_id(1) == place_ref[1])
        def _():
            own_ref[...] = total

    return pl.pallas_call(
        body, name=name,
        grid_spec=pltpu.PrefetchScalarGridSpec(
            num_scalar_prefetch=1, grid=(per_half, n),
            in_specs=[pl.BlockSpec((1, tr, C), lambda i, s, place: (s, place[0] * per_half + i, 0)),
                      pl.BlockSpec((1, tr, C), lambda i, s, place: (s, i, 0))],
            out_specs=[pl.BlockSpec((1, tr, C), lambda i, s, place: (s, i, 0)),
                       pl.BlockSpec((tr, C), lambda i, s, place: (i, 0))]),
        out_shape=[jax.ShapeDtypeStruct((n, half, C), MXU_DTYPE), jax.ShapeDtypeStruct((half, C), F32)],
        compiler_params=_params(("arbitrary", "arbitrary")),
    )(core_and_chip, held, theirs)


def _sibling_copies(parts):
    n = len(parts)

    def build(ins, outs, send_sems, recv_sems, local_sems):
        x, y, c = _place()
        copies = [pltpu.make_async_remote_copy(
            src_ref=ins[w], dst_ref=outs[w], send_sem=send_sems.at[w], recv_sem=recv_sems.at[w],
            device_id=(x, y, 1 - c), device_id_type=MESH_ID) for w in range(n)]
        return copies, copies, []

    return _Carried(parts, [jax.ShapeDtypeStruct(p.shape, p.dtype) for p in parts], n, 0, build)


def _row_tile(rows, most=512, sublanes=16):
    return max(t for t in range(sublanes, min(most, rows // 2) + 1, sublanes) if rows % t == 0)


def _partial_sum(name, own, recv, chip, comm=None):
    _, R, C = recv.shape
    tr = _row_tile(R)

    def body(o_ref, r_ref, p_ref):
        p_ref[...] = ((o_ref[...] + r_ref[0].astype(F32)) + r_ref[1].astype(F32)) + r_ref[2].astype(F32)

    if own.shape[0] != R:
        assert comm is None and own.shape[0] == N_CHIPS * R
        total = pl.pallas_call(
            lambda chip_ref, *refs: body(*refs), name=name,
            grid_spec=pltpu.PrefetchScalarGridSpec(
                num_scalar_prefetch=1, grid=(R // tr,),
                in_specs=[pl.BlockSpec((tr, C), lambda i, chip_ref: (chip_ref[0] * (R // tr) + i, 0)),
                          pl.BlockSpec((3, tr, C), lambda i, chip_ref: (0, i, 0))],
                out_specs=pl.BlockSpec((tr, C), lambda i, chip_ref: (i, 0))),
            out_shape=jax.ShapeDtypeStruct((R, C), F32), compiler_params=_params(("parallel",)),
        )(chip.reshape(1), own, recv)
        return [total], []
    return _call(name, body, grid=(R // tr,),
                 in_specs=[pl.BlockSpec((tr, C), lambda i: (i, 0)), pl.BlockSpec((3, tr, C), lambda i: (0, i, 0))],
                 out_specs=[pl.BlockSpec((tr, C), lambda i: (i, 0))], out_shape=[jax.ShapeDtypeStruct((R, C), F32)],
                 args=[own, recv], semantics=("parallel",), comm=comm)


def _adam_vals(w, g, m, v):
    m = ADAM_B1 * m + (1.0 - ADAM_B1) * g
    v = ADAM_B2 * v + (1.0 - ADAM_B2) * (g * g)
    m_hat = m / (1.0 - ADAM_B1 ** ADAM_STEP)
    v_hat = v / (1.0 - ADAM_B2 ** ADAM_STEP)
    delta = -ADAM_LR * (m_hat / (jnp.sqrt(v_hat) + ADAM_EPS) + ADAM_WD * w)
    return delta, m, v


def _adamw(name, w, m, v, mine, other, comm=None):
    R, C = w.shape
    tr = _row_tile(R)

    def body(w_ref, m_ref, v_ref, s_ref, n_ref, g_ref, d_ref, nm_ref, nv_ref):
        g = s_ref[...] + n_ref[...]
        d, nm, nv = _adam_vals(w_ref[...], g, m_ref[...], v_ref[...])
        g_ref[...], d_ref[...], nm_ref[...], nv_ref[...] = g, d, nm, nv

    spec = pl.BlockSpec((tr, C), lambda i: (i, 0))
    return _call(name, body, grid=(R // tr,), in_specs=[spec] * 5, out_specs=[spec] * 4,
                 out_shape=[jax.ShapeDtypeStruct((R, C), F32)] * 4, args=[w, m, v, mine, other],
                 semantics=("parallel",), comm=comm)


def _adamw_by_halves(name, w, m, v, mine, other, core):
    R, C = w.shape
    tr = _row_tile(R // 2)
    per_half = R // 2 // tr

    def body(c_ref, w_ref, m_ref, v_ref, s_ref, n_ref, g_ref, d_ref, nm_ref, nv_ref):
        g = jnp.where(pl.program_id(0) // per_half == c_ref[0, 0], s_ref[...], n_ref[...])
        d, nm, nv = _adam_vals(w_ref[...], g, m_ref[...], v_ref[...])
        g_ref[...], d_ref[...], nm_ref[...], nv_ref[...] = g, d, nm, nv

    spec = pl.BlockSpec((tr, C), lambda i: (i, 0))
    part = pl.BlockSpec((tr, C), lambda i: (i % per_half, 0))
    return pl.pallas_call(
        body, name=name, grid=(R // tr,),
        in_specs=[pl.BlockSpec(memory_space=pltpu.SMEM), spec, spec, spec, part, part], out_specs=[spec] * 4,
        out_shape=[jax.ShapeDtypeStruct((R, C), F32)] * 4, compiler_params=_params(("parallel",)),
    )(core, w, m, v, mine, other)


SMALL_LAYOUT = dict(norm_mix_g=(0, 1, 1024), b_in=(1, 8, 7424), hgrn_norm_g=(9, 1, 1024), norm_ffn_g=(10, 1, 1024),
                    norm_final_g=(11, 1, 1024), hgrn_lb_logits=(12, 2, 2048), attn_sinks=(14, 1, 16))
SMALL_LOSS_ROW, SMALL_ROWS = 15, 16


def _pack_small(grads, loss):
    rows = [jnp.pad(grads[name].astype(F32).reshape(-1), (0, nrows * D_MODEL - n))
            for name, (_, nrows, n) in SMALL_LAYOUT.items()]
    rows.append(jnp.pad(loss.astype(F32).reshape(1), (0, D_MODEL - 1)))
    return jnp.concatenate(rows).reshape(SMALL_ROWS, D_MODEL)


def _adamw_small(w, m, v, g_all):
    names = list(SMALL_LAYOUT)
    n = len(names)

    def body(a_ref, *refs):
        ins, outs = refs[:3 * n], refs[3 * n:]
        g_all_rows = a_ref[0]
        for dev in range(1, 8):
            g_all_rows = g_all_rows + a_ref[dev]
        for i, name in enumerate(names):
            first, nrows, count = SMALL_LAYOUT[name]
            w_ref, m_ref, v_ref = ins[3 * i:3 * i + 3]
            if w_ref.shape[0] == nrows:
                g = g_all_rows[first:first + nrows, :w_ref.shape[1]]
            else:
                last = count - (nrows - 1) * D_MODEL
                g = jnp.concatenate([g_all_rows[r:r + 1, :] for r in range(first, first + nrows - 1)]
                                    + [g_all_rows[first + nrows - 1:first + nrows, :last]], axis=1)
            d, nm, nv = _adam_vals(w_ref[...], g, m_ref[...], v_ref[...])
            for o_ref, val in zip(outs[4 * i:4 * i + 4], (g, d, nm, nv)):
                o_ref[...] = val
        outs[4 * n][...] = g_all_rows[SMALL_LOSS_ROW:SMALL_LOSS_ROW + 1, 0:1]

    res = pl.pallas_call(
        body, name="adamw_small",
        out_shape=[jax.ShapeDtypeStruct(w[name].shape, F32) for name in names for _ in range(4)]
        + [jax.ShapeDtypeStruct((1, 1), F32)],
    )(g_all, *[t[name] for name in names for t in (w, m, v)])
    return {name: res[4 * i:4 * i + 4] for i, name in enumerate(names)}, res[4 * n]


MATRICES = ("w_in", "w_branch_attn", "w_branch_hgrn", "w_out", "w_ffn_gate", "w_ffn_up", "w_ffn_down")
COLUMN_SHARDED = ("w_in", "w_ffn_gate", "w_ffn_up")
WEIGHTS = ("norm_mix_g", "w_in", "b_in", "attn_sinks", "hgrn_lb_logits", "hgrn_norm_g", "w_branch_attn",
           "w_branch_hgrn", "w_out", "norm_ffn_g", "w_ffn_gate", "w_ffn_up", "w_ffn_down", "norm_final_g")


def kernel(x, norm_mix_g, w_in, b_in, attn_sinks, hgrn_lb_logits, hgrn_norm_g, w_branch_attn, w_branch_hgrn, w_out, norm_ffn_g, w_ffn_gate, w_ffn_up, w_ffn_down, norm_final_g, loss_target, m_norm_mix_g, m_w_in, m_b_in, m_attn_sinks, m_hgrn_lb_logits, m_hgrn_norm_g, m_w_branch_attn, m_w_branch_hgrn, m_w_out, m_norm_ffn_g, m_w_ffn_gate, m_w_ffn_up, m_w_ffn_down, m_norm_final_g, v_norm_mix_g, v_w_in, v_b_in, v_attn_sinks, v_hgrn_lb_logits, v_hgrn_norm_g, v_w_branch_attn, v_w_branch_hgrn, v_w_out, v_norm_ffn_g, v_w_ffn_gate, v_w_ffn_up, v_w_ffn_down, v_norm_final_g):
    given = dict(locals())
    w = {n: given[n] for n in WEIGHTS}
    m = {n: given["m_" + n] for n in WEIGHTS}
    v = {n: given["v_" + n] for n in WEIGHTS}

    block = lambda a, n: jnp.transpose(a[0]) if n in COLUMN_SHARDED else a[0]
    unblock = lambda a, n: (jnp.transpose(a) if n in COLUMN_SHARDED else a)[None]
    net = _Net({n: block(w[n], n).astype(MXU_DTYPE) for n in MATRICES})
    net.gathered(("w_in",), [_gather_by_neighbours("gather_w_in", net.shards["w_in"])])
    vec = dict(norm_mix_g=norm_mix_g, b_in=b_in, attn_sinks=attn_sinks, hgrn_lb_logits=hgrn_lb_logits,
               hgrn_norm_g=hgrn_norm_g, norm_ffn_g=norm_ffn_g, norm_final_g=norm_final_g.reshape(1, D_MODEL))
    loss_part, dx, d_vecs = _local_step(x[0], loss_target[0], vec, net)

    small_all, = net.received(*net.last, carried=_small_copies(_pack_small(d_vecs, loss_part)))
    grads, deltas, new_m, new_v = {}, {}, {}, {}
    for n in ("w_ffn_down", "w_ffn_gate", "w_ffn_up", "w_out", "w_branch_attn", "w_branch_hgrn"):
        res, got = _adamw("adamw_" + n, block(w[n], n), block(m[n], n), block(v[n], n), net.sums[n], net.other[n],
                          comm=net.swap(("w_in",)) if n == "w_ffn_down" else None)
        if n == "w_ffn_down":
            net.other["w_in"], = got
        grads[n], deltas[n], new_m[n], new_v[n] = (unblock(r, n) for r in res)
    n = "w_in"
    res = _adamw_by_halves("adamw_" + n, block(w[n], n), block(m[n], n), block(v[n], n), net.sums[n], net.other[n],
                           _place()[2].reshape(1, 1))
    grads[n], deltas[n], new_m[n], new_v[n] = (unblock(r, n) for r in res)
    rows = lambda t: {n: t[n].reshape(-1, t[n].shape[-1]) for n in SMALL_LAYOUT}
    res, loss = _adamw_small(rows(w), rows(m), rows(v), small_all)
    for n, four in res.items():
        grads[n], deltas[n], new_m[n], new_v[n] = (r.reshape(w[n].shape) for r in four)
    loss = loss.reshape(())
    return (loss, dx[None], *[grads[n] for n in WEIGHTS], *[deltas[n] for n in WEIGHTS],
            *[new_m[n] for n in WEIGHTS], *[new_v[n] for n in WEIGHTS])
```

```python
import functools
import math

import jax
import jax.numpy as jnp
from jax import lax
from jax.experimental import pallas as pl
from jax.experimental.pallas import tpu as pltpu

F32 = jnp.float32
BF16 = jnp.bfloat16
MXU_DTYPE = jnp.bfloat16
SAVED_DTYPE = jnp.bfloat16
MESH_ID = pl.DeviceIdType.MESH

D_MODEL = 1024
HEAD_DIM = 64
Q_HEADS = 16
KV_HEADS = 2
GROUP = Q_HEADS // KV_HEADS
KV_WIDTH = KV_HEADS * HEAD_DIM
ATTN_BLOCK = 128
HGRN_HEADS = 8
HGRN_K = 128
CHUNK = 64
HGRN_TOKENS = 256
FFN = 2816
IN_SPLITS = (1024, 256, 4096, 2048)
EPS = 1e-6
NEG_INF = -1e30
ADAM_LR, ADAM_B1, ADAM_B2, ADAM_EPS, ADAM_WD, ADAM_STEP = 0.001, 0.9, 0.999, 1e-08, 0.01, 10
N_CHIPS = 4
VMEM_LIMIT = 60 * 1024 * 1024
ROW_ALIGN = 16
DW_ROWS = 256


def _params(sem=None):
    return pltpu.CompilerParams(dimension_semantics=sem, vmem_limit_bytes=VMEM_LIMIT)


def _sigmoid(v):
    return 0.5 * jnp.tanh(0.5 * v) + 0.5


def _dot(a, b, dims):
    return lax.dot_general(a.astype(MXU_DTYPE), b.astype(MXU_DTYPE), (dims, ((), ())),
                           preferred_element_type=F32)


def _nn(a, b):
    return _dot(a, b, ((1,), (0,)))


def _nt(a, b):
    return _dot(a, b, ((1,), (1,)))


def _tn(a, b):
    return _dot(a, b, ((0,), (0,)))


HBM_SPEC = pl.BlockSpec(memory_space=pl.ANY)


class _Carried:
    def __init__(self, arrays, out_shapes, n_remote, n_local, build, continued=None, stages=()):
        self.continued = dict(continued or {})
        self.parts = [(len(arrays), len(out_shapes), 3 + len(stages), build)]
        self.arrays, self.out_shapes = list(arrays), list(out_shapes)
        self.scratch = [pltpu.SemaphoreType.DMA((n_remote,)), pltpu.SemaphoreType.DMA((n_remote,)),
                        pltpu.SemaphoreType.DMA((max(n_local, 1),))] + list(stages)

    def __add__(self, other):
        both = _Carried([], [], 1, 0, None)
        both.parts = self.parts + other.parts
        both.arrays, both.out_shapes = self.arrays + other.arrays, self.out_shapes + other.out_shapes
        both.scratch = self.scratch + other.scratch
        both.continued = dict(self.continued)
        both.continued.update({len(self.arrays) + i: len(self.out_shapes) + o for i, o in other.continued.items()})
        return both

    def _built(self, ins, outs, sems):
        for ni, no, ns, build in self.parts:
            yield build(ins[:ni], outs[:no], *sems[:ns])
            ins, outs, sems = ins[ni:], outs[no:], sems[ns:]

    def start(self, ins, outs, sems):
        core = lax.axis_index("c")
        for sends, _, local, *other_order in self._built(ins, outs, sems):
            for cp in local:
                if not isinstance(cp, tuple):
                    cp.start()
            if not other_order:
                for cp in sends:
                    cp.start()
            else:
                @pl.when(core == 0)
                def _():
                    for cp in sends:
                        cp.start()

                @pl.when(core == 1)
                def _():
                    for cp in other_order[0]:
                        cp.start()
            staged = [cp for cp in local if isinstance(cp, tuple)]
            for into, _ in staged:
                into.start()
            for into, out_of in staged:
                into.wait()
                out_of.start()

    def wait(self, ins, outs, sems):
        for sends, recvs, local, *_ in self._built(ins, outs, sems):
            for cp in recvs:
                cp.wait_recv()
            for cp in sends:
                cp.wait_send()
            for cp in local:
                (cp[1] if isinstance(cp, tuple) else cp).wait()


def _join(*comms):
    comms = [c for c in comms if c is not None]
    return functools.reduce(lambda a, b: a + b, comms) if comms else None


def _call(name, body, *, grid, in_specs, out_specs, out_shape, args, scratch=(), semantics=None, comm=None,
          aliases=None):
    n_in, n_out, n_scr = len(in_specs), len(out_specs), len(scratch)
    aliases = aliases or {}
    if comm is None:
        res = pl.pallas_call(body, name=name, grid=grid, in_specs=in_specs, out_specs=out_specs, out_shape=out_shape,
                             scratch_shapes=list(scratch), input_output_aliases=aliases,
                             compiler_params=_params(semantics))(*args)
        return list(res), []
    ci, co = len(comm.arrays), len(comm.out_shapes)
    aliases = dict(aliases)
    aliases.update({n_in + i: n_out + o for i, o in comm.continued.items()})

    def carrying(*refs):
        ins, refs = refs[:n_in], refs[n_in:]
        c_ins, refs = refs[:ci], refs[ci:]
        outs, refs = refs[:n_out], refs[n_out:]
        c_outs, refs = refs[:co], refs[co:]
        scr, sems = refs[:n_scr], refs[n_scr:]
        if not grid:
            comm.start(c_ins, c_outs, sems)
            body(*ins, *outs, *scr)
            comm.wait(c_ins, c_outs, sems)
            return
        first = functools.reduce(jnp.logical_and, [pl.program_id(a) == 0 for a in range(len(grid))])
        last = functools.reduce(jnp.logical_and, [pl.program_id(a) == g - 1 for a, g in enumerate(grid)])

        @pl.when(first)
        def _():
            comm.start(c_ins, c_outs, sems)

        body(*ins, *outs, *scr)

        @pl.when(last)
        def _():
            comm.wait(c_ins, c_outs, sems)

    res = pl.pallas_call(
        carrying, name=name, grid=grid, in_specs=list(in_specs) + [HBM_SPEC] * ci,
        out_specs=list(out_specs) + [HBM_SPEC] * co, out_shape=list(out_shape) + comm.out_shapes,
        scratch_shapes=list(scratch) + comm.scratch, input_output_aliases=aliases,
        compiler_params=_params(("arbitrary",) * len(grid) if grid else None),
    )(*args, *comm.arrays)
    return list(res[:n_out]), list(res[n_out:])


def _weight_grad(name, a, b, *, tm, tk, a_colsum=False, carrying=False, comm=None):
    (T, N), M = b.shape, a.shape[1]
    tk = min(tk, T)
    assert M % tm == 0 and T % tk == 0, (name, M, tm, T, tk)
    ni, nk = M // tm, T // tk

    def body(a_ref, b_ref, *rest):
        keep_ref, send_ref = rest[:2]
        sums_ref = rest[2] if a_colsum else None
        if nk == 1:
            acc = _tn(a_ref[...], b_ref[...])
            keep_ref[...], send_ref[...] = acc, acc.astype(send_ref.dtype)
            if a_colsum:
                sums_ref[...] = jnp.sum(a_ref[...].astype(F32), axis=0, keepdims=True)
            return
        acc_ref = rest[-1]
        k = pl.program_id(1)

        @pl.when(k == 0)
        def _():
            acc_ref[...] = jnp.zeros_like(acc_ref)
            if a_colsum:
                sums_ref[...] = jnp.zeros((1, tm), F32)

        if a_colsum:
            sums_ref[...] += jnp.sum(a_ref[...].astype(F32), axis=0, keepdims=True)
        acc_ref[...] += _tn(a_ref[...], b_ref[...])

        @pl.when(k == nk - 1)
        def _():
            keep_ref[...], send_ref[...] = acc_ref[...], acc_ref[...].astype(send_ref.dtype)

    out_spec = pl.BlockSpec((tm, N), lambda i, k: (i, 0))
    out_shape = [jax.ShapeDtypeStruct((M, N), F32), jax.ShapeDtypeStruct((M, N), MXU_DTYPE)]
    out_specs = [out_spec, out_spec]
    if a_colsum:
        out_shape.append(jax.ShapeDtypeStruct((1, M), F32))
        out_specs.append(pl.BlockSpec((1, tm), lambda i, k: (0, i)))
    res, got = _call(
        name, body, grid=(ni, nk),
        in_specs=[pl.BlockSpec((tk, tm), lambda i, k: (k, i)), pl.BlockSpec((tk, N), lambda i, k: (k, 0))],
        out_specs=out_specs, out_shape=out_shape, scratch=[pltpu.VMEM((tm, N), F32)] if nk > 1 else [],
        args=[a, b], semantics=("parallel", "arbitrary"), comm=comm)
    return (res, got) if carrying else res


def _ffn_fwd(merged, w_out, x, gain, w_gate_t, w_up_t, *, tm, comm=None):
    (T, D), F = x.shape, w_gate_t.shape[0]

    def body(m_ref, wo_ref, x_ref, g_ref, wg_ref, wu_ref, h_ref, u_ref, gate_ref, up_ref, z_ref):
        h = x_ref[...] + _nn(m_ref[...], wo_ref[...])
        h_ref[...] = h
        u = (h * lax.rsqrt(jnp.mean(h * h, axis=-1, keepdims=True) + EPS) * g_ref[...]).astype(u_ref.dtype)
        u_ref[...] = u
        gate, up = _nt(u, wg_ref[...]), _nt(u, wu_ref[...])
        gate_ref[...], up_ref[...] = gate.astype(gate_ref.dtype), up.astype(up_ref.dtype)
        z_ref[...] = (gate * _sigmoid(gate) * up).astype(z_ref.dtype)

    rows = lambda n: pl.BlockSpec((tm, n), lambda i: (i, 0))
    fixed = _fixed_spec
    return _call("ffn_hidden", body, grid=(T // tm,),
                 in_specs=[rows(D), fixed(w_out), rows(D), fixed(gain), fixed(w_gate_t), fixed(w_up_t)],
                 out_specs=[rows(D), rows(D), rows(F), rows(F), rows(F)],
                 out_shape=[jax.ShapeDtypeStruct((T, D), F32), jax.ShapeDtypeStruct((T, D), MXU_DTYPE)]
                 + [jax.ShapeDtypeStruct((T, F), SAVED_DTYPE)] * 2 + [jax.ShapeDtypeStruct((T, F), MXU_DTYPE)],
                 args=[merged, w_out, x, gain, w_gate_t, w_up_t], semantics=("parallel",), comm=comm)


def _in_proj(x, gain, w_in_t, b_in, *, tm, comm=None):
    T, D = x.shape
    bounds = [sum(IN_SPLITS[:i]) for i in range(len(IN_SPLITS) + 1)]

    def body(x_ref, g_ref, w_ref, b_ref, u_ref, *piece_refs):
        xv = x_ref[...]
        r = lax.rsqrt(jnp.mean(xv * xv, axis=-1, keepdims=True) + EPS)
        u = (xv * r * g_ref[...]).astype(u_ref.dtype)
        u_ref[...] = u
        for o_ref, lo, hi in zip(piece_refs, bounds[:-1], bounds[1:]):
            o_ref[...] = (_nt(u, w_ref[lo:hi, :]) + b_ref[:, lo:hi]).astype(o_ref.dtype)

    rows = lambda n: pl.BlockSpec((tm, n), lambda i: (i, 0))
    fixed = _fixed_spec
    dtypes = (MXU_DTYPE, MXU_DTYPE, F32, F32)
    return _call("in_proj", body, grid=(T // tm,),
                 in_specs=[rows(D), fixed(gain), fixed(w_in_t), fixed(b_in)],
                 out_specs=[rows(D)] + [rows(n) for n in IN_SPLITS],
                 out_shape=[jax.ShapeDtypeStruct((T, D), MXU_DTYPE)]
                 + [jax.ShapeDtypeStruct((T, n), dt) for n, dt in zip(IN_SPLITS, dtypes)],
                 args=[x, gain, w_in_t, b_in], semantics=("parallel",), comm=comm)


def _row_spec(tm, n):
    return pl.BlockSpec((tm, n), lambda i: (i, 0))


def _fixed_spec(a):
    return pl.BlockSpec(a.shape, lambda i: (0,) * a.ndim, pipeline_mode=pl.Buffered(1))


def _partials_spec(n):
    return pl.BlockSpec((8, n), lambda i: (i, 0))


def _row_parts(tm, parts):
    assert tm % parts == 0, (tm, parts)
    return [slice(p * (tm // parts), (p + 1) * (tm // parts)) for p in range(parts)]


def _ffn_tail(z, w_down, h1, target, gain, gate, up, *, tm, parts=2):
    (T, F), D = z.shape, h1.shape[1]

    def body(z_ref, w_ref, h_ref, t_ref, g_ref, gate_ref, up_ref, dh_ref, dhb_ref, dgate_ref, dup_ref, dg_ref, l_ref):
        part, dgain = 0.0, 0.0
        pieces = _row_parts(tm, parts)
        h2s = [h_ref[rows, :] + _nn(z_ref[rows, :], w_ref[...]) for rows in pieces]
        for rows, h2 in zip(pieces, h2s):
            r = lax.rsqrt(jnp.mean(h2 * h2, axis=-1, keepdims=True) + EPS)
            xhat = h2 * r
            err = xhat * g_ref[...] - t_ref[rows, :]
            part += 0.5 * jnp.sum(jnp.sum(err * err, axis=-1, keepdims=True), axis=0, keepdims=True) / D
            dy = err / D
            dxh = dy * g_ref[...]
            dh2 = r * (dxh - xhat * jnp.mean(dxh * xhat, axis=-1, keepdims=True))
            dh_ref[rows, :] = dh2
            dhb = dh2.astype(dhb_ref.dtype)
            dhb_ref[rows, :] = dhb
            dgain += jnp.sum(dy * xhat, axis=0, keepdims=True)
            dz = _nt(dhb, w_ref[...])
            gv, upv = gate_ref[rows, :].astype(F32), up_ref[rows, :].astype(F32)
            s = _sigmoid(gv)
            dgate_ref[rows, :] = (dz * upv * (s * (1.0 + gv * (1.0 - s)))).astype(dgate_ref.dtype)
            dup_ref[rows, :] = (dz * (gv * s)).astype(dup_ref.dtype)
        dg_ref[...] = jnp.broadcast_to(dgain, dg_ref.shape)
        l_ref[...] = jnp.broadcast_to(part, l_ref.shape)

    low = lambda n: jax.ShapeDtypeStruct((T, n), MXU_DTYPE)
    part = jax.ShapeDtypeStruct((8 * (T // tm), D), F32)
    return pl.pallas_call(
        body, name="ffn_tail", grid=(T // tm,),
        in_specs=[_row_spec(tm, F), _fixed_spec(w_down), _row_spec(tm, D), _row_spec(tm, D), _fixed_spec(gain),
                  _row_spec(tm, F), _row_spec(tm, F)],
        out_specs=[_row_spec(tm, D), _row_spec(tm, D), _row_spec(tm, F), _row_spec(tm, F), _partials_spec(D),
                   _partials_spec(D)],
        out_shape=[jax.ShapeDtypeStruct((T, D), F32), low(D), low(F), low(F), part, part],
        compiler_params=_params(("parallel",)),
    )(z, w_down, h1, target, gain, gate, up)


def _ffn_in_bwd(dgate, dup, w_gate_t, w_up_t, h1, gain, dres, *, tm, comm=None):
    (T, F), D = dgate.shape, h1.shape[1]

    def body(dg_ref, du_ref, wg_ref, wu_ref, h_ref, g_ref, r_ref, dh_ref, dhb_ref, dgain_ref):
        d_u2 = _nn(dg_ref[...], wg_ref[...]) + _nn(du_ref[...], wu_ref[...])
        dx, dgain = _rmsnorm_bwd_vals(d_u2, h_ref[...], g_ref[...])
        dh = r_ref[...] + dx
        dh_ref[...] = dh
        dhb_ref[...] = dh.astype(dhb_ref.dtype)
        dgain_ref[...] = jnp.broadcast_to(dgain, dgain_ref.shape)

    return _call("d_ffn_in", body, grid=(T // tm,),
                 in_specs=[_row_spec(tm, F), _row_spec(tm, F), _fixed_spec(w_gate_t), _fixed_spec(w_up_t),
                           _row_spec(tm, D), _fixed_spec(gain), _row_spec(tm, D)],
                 out_specs=[_row_spec(tm, D), _row_spec(tm, D), _partials_spec(D)],
                 out_shape=[jax.ShapeDtypeStruct((T, D), F32), jax.ShapeDtypeStruct((T, D), MXU_DTYPE),
                            jax.ShapeDtypeStruct((8 * (T // tm), D), F32)],
                 args=[dgate, dup, w_gate_t, w_up_t, h1, gain, dres], semantics=("parallel",), comm=comm)


def _in_proj_bwd(pieces, w_in_t, x, gain, dres, *, tm, comm=None):
    T, D = x.shape
    n = len(pieces)

    def body(*refs):
        dps, (w_ref, x_ref, g_ref, r_ref, dx_ref, dgain_ref) = refs[:n], refs[n:]
        d_u = None
        for dp_ref, (dp, first) in zip(dps, pieces):
            term = _nn(dp_ref[...], w_ref[first:first + dp.shape[1], :])
            d_u = term if d_u is None else d_u + term
        dx, dgain = _rmsnorm_bwd_vals(d_u, x_ref[...], g_ref[...])
        dx_ref[...] = r_ref[...] + dx
        dgain_ref[...] = jnp.broadcast_to(dgain, dgain_ref.shape)

    return _call("d_u", body, grid=(T // tm,),
                 in_specs=[_row_spec(tm, dp.shape[1]) for dp, _ in pieces]
                 + [_fixed_spec(w_in_t), _row_spec(tm, D), _fixed_spec(gain), _row_spec(tm, D)],
                 out_specs=[_row_spec(tm, D), _partials_spec(D)],
                 out_shape=[jax.ShapeDtypeStruct((T, D), F32), jax.ShapeDtypeStruct((8 * (T // tm), D), F32)],
                 args=[dp for dp, _ in pieces] + [w_in_t, x, gain, dres], semantics=("parallel",), comm=comm)


def _merge_fwd(y_a, y_b, w_a, w_b, gates, *, tm, comm=None):
    T, D = y_a.shape

    def body(ya_ref, yb_ref, wa_ref, wb_ref, ga_ref, gb_ref, pa_ref, pb_ref, m_ref):
        pa, pb = _nn(ya_ref[...], wa_ref[...]), _nn(yb_ref[...], wb_ref[...])
        pa_ref[...], pb_ref[...] = pa.astype(pa_ref.dtype), pb.astype(pb_ref.dtype)
        m_ref[...] = (_sigmoid(ga_ref[...]) * pa + _sigmoid(gb_ref[...]) * pb).astype(m_ref.dtype)

    rows = pl.BlockSpec((tm, D), lambda i: (i, 0))
    whole = pl.BlockSpec((D, D), lambda i: (0, 0), pipeline_mode=pl.Buffered(1))
    return _call("branch_merge", body, grid=(T // tm,),
                 in_specs=[rows, rows, whole, whole, rows, pl.BlockSpec((tm, D), lambda i: (i, 1))],
                 out_specs=[rows] * 3,
                 out_shape=[jax.ShapeDtypeStruct((T, D), SAVED_DTYPE)] * 2 + [jax.ShapeDtypeStruct((T, D), MXU_DTYPE)],
                 args=[y_a, y_b, w_a, w_b, gates, gates], semantics=("parallel",), comm=comm)


def _merge_bwd(dh, w_out, w_a, w_b, p_a, p_b, gates, *, tm, d_in_width, first):
    T, D = dh.shape

    def body(dh_ref, wo_ref, wa_ref, wb_ref, pa_ref, pb_ref, ga_ref, gb_ref, dpa_ref, dpb_ref, dya_ref, dyb_ref,
             din_ref):
        dm = _nt(dh_ref[...], wo_ref[...])
        sa, sb = _sigmoid(ga_ref[...]), _sigmoid(gb_ref[...])
        dpa, dpb = (dm * sa).astype(dpa_ref.dtype), (dm * sb).astype(dpb_ref.dtype)
        dpa_ref[...], dpb_ref[...] = dpa, dpb
        din_ref[:, :D] = (dm * pa_ref[...].astype(F32) * sa * (1.0 - sa)).astype(din_ref.dtype)
        din_ref[:, D:] = (dm * pb_ref[...].astype(F32) * sb * (1.0 - sb)).astype(din_ref.dtype)
        dya_ref[...] = _nt(dpa, wa_ref[...]).astype(dya_ref.dtype)
        dyb_ref[...] = _nt(dpb, wb_ref[...]).astype(dyb_ref.dtype)

    rows = pl.BlockSpec((tm, D), lambda i: (i, 0))
    whole = pl.BlockSpec((D, D), lambda i: (0, 0), pipeline_mode=pl.Buffered(1))
    low = jax.ShapeDtypeStruct((T, D), MXU_DTYPE)
    return pl.pallas_call(
        body, name="d_branch_merge", grid=(T // tm,),
        in_specs=[rows, whole, whole, whole, rows, rows, rows, pl.BlockSpec((tm, D), lambda i: (i, 1))],
        out_specs=[rows] * 4 + [pl.BlockSpec((pl.Element(tm), pl.Element(2 * D)), lambda i: (
            pl.multiple_of(i * tm, ROW_ALIGN), first))],
        out_shape=[low] * 3 + [jax.ShapeDtypeStruct((T, D), F32), jax.ShapeDtypeStruct((T, d_in_width), MXU_DTYPE)],
        compiler_params=_params(("parallel",)),
    )(dh, w_out, w_a, w_b, p_a, p_b, gates, gates)


def _colsum_partials(p):
    return jnp.sum(p.reshape(-1, 8, p.shape[-1])[:, 0, :], axis=0, keepdims=True)


def _rmsnorm_bwd_vals(dy, xin, g):
    rstd = lax.rsqrt(jnp.mean(xin * xin, axis=-1, keepdims=True) + EPS)
    xhat = xin * rstd
    dg = jnp.sum(dy * xhat, axis=0, keepdims=True)
    dxh = dy * g
    dx = rstd * (dxh - xhat * jnp.mean(dxh * xhat, axis=-1, keepdims=True))
    return dx, dg


ATTN_SCALE = 1.0 / math.sqrt(HEAD_DIM)
GROUP_LANES = GROUP * ATTN_BLOCK
PAIR = 2 * HEAD_DIM


def _attn_mask():
    kj = lax.broadcasted_iota(jnp.int32, (ATTN_BLOCK, GROUP_LANES), 0)
    qi = lax.broadcasted_iota(jnp.int32, (ATTN_BLOCK, GROUP_LANES), 1) & (ATTN_BLOCK - 1)
    return kj <= qi


def _heads_transposed(ref, g, scale=None):
    parts = []
    for a in range(GROUP // 2):
        lo = (g * GROUP // 2 + a) * PAIR
        pair = ref[:, lo:lo + PAIR].astype(F32)
        pair = (pair if scale is None else pair * scale).T
        parts += [pair[:HEAD_DIM], pair[HEAD_DIM:]]
    return jnp.concatenate(parts, axis=1).astype(MXU_DTYPE)


def _heads_back(ref, g, vt):
    for a in range(GROUP // 2):
        lo = (g * GROUP // 2 + a) * PAIR
        pair = jnp.concatenate([vt[:, (2 * a) * ATTN_BLOCK:(2 * a + 1) * ATTN_BLOCK],
                                vt[:, (2 * a + 1) * ATTN_BLOCK:(2 * a + 2) * ATTN_BLOCK]], axis=0)
        ref[:, lo:lo + PAIR] = pair.T.astype(ref.dtype)


def _kv_parts(kv_ref, g):
    ks = slice(g * HEAD_DIM, (g + 1) * HEAD_DIM)
    vs = slice(KV_WIDTH + g * HEAD_DIM, KV_WIDTH + (g + 1) * HEAD_DIM)
    return kv_ref[:, ks].astype(MXU_DTYPE), kv_ref[:, vs].astype(MXU_DTYPE)


def _sink_rows(sinks):
    return jnp.repeat(sinks.reshape(KV_HEADS, GROUP), ATTN_BLOCK, axis=1)


def _attn_fwd(pq, pkv, sinks, comm=None):
    T = pq.shape[0]
    nb = T // ATTN_BLOCK

    def body(q_ref, kvc_ref, kvp_ref, s_ref, y_ref, lse_ref):
        mask_c = _attn_mask()
        has_prev = pl.program_id(0) > 0
        early = []
        for g in range(KV_HEADS):
            (kc, vc), (kp, vp) = _kv_parts(kvc_ref, g), _kv_parts(kvp_ref, g)
            qt = _heads_transposed(q_ref, g, ATTN_SCALE)
            early.append((vc, vp, jnp.where(mask_c, _nn(kc, qt), jnp.where(has_prev, _nn(kp, qt), NEG_INF))))
        for g, (vc, vp, s) in enumerate(early):
            sink = s_ref[g:g + 1, :]
            m = jnp.maximum(jnp.max(s, axis=0, keepdims=True), sink)
            p = jnp.exp(s - m)
            den = jnp.sum(p, axis=0, keepdims=True) + jnp.exp(sink - m)
            pc = jnp.where(mask_c, p, 0.0)
            _heads_back(y_ref, g, (_tn(vc, pc) + _tn(vp, p - pc)) / den)
            lse = m + jnp.log(den)
            for i in range(GROUP):
                lse_ref[g * GROUP + i:g * GROUP + i + 1, :] = lse[:, i * ATTN_BLOCK:(i + 1) * ATTN_BLOCK]

    return _call(
        "attn_fwd", body, grid=(nb,),
        in_specs=[pl.BlockSpec((ATTN_BLOCK, D_MODEL), lambda n: (n, 0)),
                  pl.BlockSpec((ATTN_BLOCK, 2 * KV_WIDTH), lambda n: (n, 0)),
                  pl.BlockSpec((ATTN_BLOCK, 2 * KV_WIDTH), lambda n: (jnp.maximum(n - 1, 0), 0)),
                  pl.BlockSpec((KV_HEADS, GROUP_LANES), lambda n: (0, 0))],
        out_specs=[pl.BlockSpec((ATTN_BLOCK, D_MODEL), lambda n: (n, 0)),
                   pl.BlockSpec((Q_HEADS, ATTN_BLOCK), lambda n: (0, n))],
        out_shape=[jax.ShapeDtypeStruct((T, D_MODEL), MXU_DTYPE), jax.ShapeDtypeStruct((Q_HEADS, T), F32)],
        args=[pq, pkv, pkv, _sink_rows(sinks)], semantics=("parallel",), comm=comm)


def _attn_bwd(pq, pkv, sinks, lse, dy, d_in, comm=None):
    T = pq.shape[0]
    nb = T // ATTN_BLOCK
    cur = lambda n: (jnp.minimum(n, nb - 1), 0)
    done = D_MODEL + 2 * KV_WIDTH

    def body(q_ref, kvc_ref, kvp_ref, s_ref, lse_ref, dy_ref, _, out_ref, ds_ref, carry, top, bot, dq_ref):
        n = pl.program_id(0)

        @pl.when(n == 0)
        def _():
            carry[...] = jnp.zeros_like(carry)
            dq_ref[...] = jnp.zeros_like(dq_ref)
            ds_ref[...] = jnp.zeros_like(ds_ref)

        out_ref[:, :D_MODEL] = dq_ref[...]

        @pl.when(n < nb)
        def _():
            mask_c = _attn_mask()
            valid = jnp.logical_or(mask_c, n > 0)
            early = []
            for g in range(KV_HEADS):
                (kc, vc), (kp, vp) = _kv_parts(kvc_ref, g), _kv_parts(kvp_ref, g)
                qt = _heads_transposed(q_ref, g, ATTN_SCALE)
                dot = _heads_transposed(dy_ref, g)
                early.append((kc, kp, qt, dot, jnp.where(mask_c, _nn(kc, qt), _nn(kp, qt)),
                              jnp.where(mask_c, _nn(vc, dot), _nn(vp, dot))))
            for g, (kc, kp, qt, dot, s, dp) in enumerate(early):
                ks = slice(g * HEAD_DIM, (g + 1) * HEAD_DIM)
                vs = slice(KV_WIDTH + g * HEAD_DIM, KV_WIDTH + (g + 1) * HEAD_DIM)
                lse = jnp.concatenate([lse_ref[g * GROUP + i:g * GROUP + i + 1, :] for i in range(GROUP)], axis=1)
                p = jnp.where(valid, jnp.exp(s - lse), 0.0)
                delta = jnp.sum(p * dp, axis=0, keepdims=True)
                ds = p * (dp - delta)
                ds_c, p_c = jnp.where(mask_c, ds, 0.0), jnp.where(mask_c, p, 0.0)
                ds_p, p_p = ds - ds_c, p - p_c
                _heads_back(dq_ref, g, (_tn(kc, ds_c) + _tn(kp, ds_p)) * ATTN_SCALE)
                bot[:, ks], bot[:, vs] = _nt(ds_c, qt), _nt(p_c, dot)
                top[:, ks], top[:, vs] = _nt(ds_p, qt), _nt(p_p, dot)
                ds_ref[g:g + 1, :] -= jnp.exp(s_ref[g:g + 1, :] - lse) * delta
            out_ref[:, D_MODEL:] = (carry[...] + top[...]).astype(out_ref.dtype)
            carry[...] = bot[...]

        @pl.when(n == nb)
        def _():
            out_ref[:, D_MODEL:] = carry[...].astype(out_ref.dtype)

    return _call(
        "attn_bwd", body, grid=(nb + 1,),
        in_specs=[pl.BlockSpec((ATTN_BLOCK, D_MODEL), cur),
                  pl.BlockSpec((ATTN_BLOCK, 2 * KV_WIDTH), cur),
                  pl.BlockSpec((ATTN_BLOCK, 2 * KV_WIDTH), lambda n: (jnp.maximum(jnp.minimum(n, nb - 1) - 1, 0), 0)),
                  pl.BlockSpec((KV_HEADS, GROUP_LANES), lambda n: (0, 0)),
                  pl.BlockSpec((Q_HEADS, ATTN_BLOCK), lambda n: (0, jnp.minimum(n, nb - 1))),
                  pl.BlockSpec((ATTN_BLOCK, D_MODEL), cur), HBM_SPEC],
        out_specs=[pl.BlockSpec((ATTN_BLOCK, done), lambda n: (jnp.maximum(n - 1, 0), 0)),
                   pl.BlockSpec((KV_HEADS, GROUP_LANES), lambda n: (0, 0))],
        out_shape=[jax.ShapeDtypeStruct(d_in.shape, d_in.dtype), jax.ShapeDtypeStruct((KV_HEADS, GROUP_LANES), F32)],
        scratch=[pltpu.VMEM((ATTN_BLOCK, 2 * KV_WIDTH), F32)] * 3 + [pltpu.VMEM((ATTN_BLOCK, D_MODEL), MXU_DTYPE)],
        args=[pq, pkv, pkv, _sink_rows(sinks), lse, dy, d_in], semantics=("arbitrary",), comm=comm, aliases={6: 0})


def _lower_bound(l):
    m = jnp.maximum(l[0:1], l[1:2])
    e0, e1 = jnp.exp(l[0:1] - m), jnp.exp(l[1:2] - m)
    return e0 / (e0 + e1)


def _tri(lower):
    r = lax.broadcasted_iota(jnp.int32, (CHUNK, CHUNK), 0)
    c = lax.broadcasted_iota(jnp.int32, (CHUNK, CHUNK), 1)
    return (r >= c) if lower else (c >= r)


def _chunk_sum(mask, v):
    ones = mask.astype(BF16)
    hi = v.astype(BF16)
    rest = v - hi.astype(F32)
    mid = rest.astype(BF16)
    lo = (rest - mid.astype(F32)).astype(BF16)
    part = lambda t: lax.dot_general(ones, t, (((1,), (0,)), ((), ())), preferred_element_type=F32)
    return part(hi) + part(mid) + part(lo)


def _hgrn_chunk_inputs(hq, hf, lb, causal):
    half_t = 0.5 * jnp.tanh(0.5 * hf)
    sg, sgn = 0.5 + half_t, 0.5 - half_t
    f = lb + (1.0 - lb) * sg
    kk = (1.0 - lb) * sgn
    sq = _sigmoid(hq)
    q = hq * sq
    b = _chunk_sum(causal, jnp.log(f))
    bm, bl = b[CHUNK // 2 - 1:CHUNK // 2, :], b[CHUNK - 1:CHUNK, :]
    e_qm, e_km = jnp.exp(b - bm), jnp.exp(bm - b)
    e_qs, e_kl = e_qm * jnp.exp(bm), e_km * jnp.exp(bl - bm)
    return dict(sg=sg, sgn=sgn, f=f, kk=kk, sq=sq, q=q, e_qm=e_qm, e_km=e_km, e_qs=e_qs, e_kl=e_kl,
                qm=q * e_qm, km=kk * e_km, qs=q * e_qs, kl=kk * e_kl, el=jnp.exp(bl))


def _hgrn_fwd(ph, lb_logits, norm_g, comm=None):
    T = ph.shape[0]
    nblk, cpb = T // HGRN_TOKENS, HGRN_TOKENS // CHUNK
    col = lambda c: pl.BlockSpec((HGRN_TOKENS, D_MODEL), functools.partial(lambda i, c: (i, c), c=c))

    def body(hq_ref, hf_ref, hi_ref, hg_ref, l_ref, ng_ref, y_ref, o_ref, st_ref, s_ref):
        @pl.when(pl.program_id(0) == 0)
        def _():
            s_ref[...] = jnp.zeros_like(s_ref)

        lb = _lower_bound(l_ref[...])
        causal = _tri(True)
        for c in range(cpb):
            rows = slice(c * CHUNK, (c + 1) * CHUNK)
            t = _hgrn_chunk_inputs(hq_ref[rows, :], hf_ref[rows, :], lb, causal)
            qm, km, qs, kl = (t[n].astype(MXU_DTYPE) for n in ("qm", "km", "qs", "kl"))
            v = hi_ref[rows, :].astype(MXU_DTYPE)
            heads = [slice(h * HGRN_K, (h + 1) * HGRN_K) for h in range(HGRN_HEADS)]
            a_all = [jnp.where(causal, _nt(qm[:, ls], km[:, ls]), 0.0).astype(MXU_DTYPE) for ls in heads]
            for h, ls in enumerate(heads):
                st = s_ref[h]
                st_ref[c, ls, :] = st
                o_ref[rows, ls] = _nn(a_all[h], v[:, ls]) + _nt(qs[:, ls], st)
                s_ref[h] = t["el"][:, ls] * st + _tn(v[:, ls], kl[:, ls])
        for h in range(HGRN_HEADS):
            ls = slice(h * HGRN_K, (h + 1) * HGRN_K)
            o = o_ref[:, ls]
            r = lax.rsqrt(jnp.mean(o * o, axis=-1, keepdims=True) + EPS)
            y_ref[:, ls] = (o * r * ng_ref[:, ls] * _sigmoid(hg_ref[:, ls])).astype(y_ref.dtype)

    return _call(
        "hgrn_fwd", body, grid=(nblk,),
        in_specs=[col(0), col(1), col(2), col(3),
                  pl.BlockSpec((2, D_MODEL), lambda i: (0, 0)), pl.BlockSpec((1, D_MODEL), lambda i: (0, 0))],
        out_specs=[pl.BlockSpec((HGRN_TOKENS, D_MODEL), lambda i: (i, 0)),
                   pl.BlockSpec((HGRN_TOKENS, D_MODEL), lambda i: (i, 0)),
                   pl.BlockSpec((cpb, D_MODEL, HGRN_K), lambda i: (i, 0, 0))],
        out_shape=[jax.ShapeDtypeStruct((T, D_MODEL), MXU_DTYPE), jax.ShapeDtypeStruct((T, D_MODEL), F32),
                   jax.ShapeDtypeStruct((T // CHUNK, D_MODEL, HGRN_K), F32)],
        scratch=[pltpu.VMEM((HGRN_HEADS, HGRN_K, HGRN_K), F32)],
        args=[ph, ph, ph, ph, lb_logits, norm_g], semantics=("arbitrary",), comm=comm)


def _hgrn_bwd(ph, o_raw, states, dy, lb_logits, norm_g, d_in, first, comm=None):
    T = ph.shape[0]
    nblk, cpb = T // HGRN_TOKENS, HGRN_TOKENS // CHUNK
    rev = lambda i: nblk - 1 - i
    col = lambda c: pl.BlockSpec((HGRN_TOKENS, D_MODEL), functools.partial(lambda i, c: (rev(i), c), c=c))
    tok = pl.BlockSpec((HGRN_TOKENS, D_MODEL), lambda i: (rev(i), 0))

    def body(hq_ref, hf_ref, hi_ref, hg_ref, o_ref, st_ref, dy_ref, l_ref, ng_ref, _,
             dph_ref, dng_ref, dl_ref, dst_ref, dlb_ref, do_s, dqm_s, dkm_s, dqs_s, dkl_s, dv_s, del_s):
        i = pl.program_id(0)

        @pl.when(i == 0)
        def _():
            dst_ref[...] = jnp.zeros_like(dst_ref)
            dlb_ref[...] = jnp.zeros_like(dlb_ref)
            dng_ref[...] = jnp.zeros_like(dng_ref)

        lb = _lower_bound(l_ref[...])
        causal, anti = _tri(True), _tri(False)
        row = lax.broadcasted_iota(jnp.int32, (CHUNK, D_MODEL), 0)
        for c in reversed(range(cpb)):
            rows = slice(c * CHUNK, (c + 1) * CHUNK)
            hq = hq_ref[rows, :]
            t = _hgrn_chunk_inputs(hq, hf_ref[rows, :], lb, causal)
            sgg = _sigmoid(hg_ref[rows, :])
            dyv = dy_ref[rows, :]
            for h in range(HGRN_HEADS):
                ls = slice(h * HGRN_K, (h + 1) * HGRN_K)
                o = o_ref[rows, ls]
                r = lax.rsqrt(jnp.mean(o * o, axis=-1, keepdims=True) + EPS)
                nrm = o * r
                g_h = sgg[:, ls]
                dph_ref[rows, 3 * D_MODEL + h * HGRN_K:3 * D_MODEL + (h + 1) * HGRN_K] = (
                    dyv[:, ls] * nrm * ng_ref[:, ls] * g_h * (1.0 - g_h)).astype(dph_ref.dtype)
                dyg = dyv[:, ls] * g_h
                dng_ref[:, ls] += jnp.sum(dyg * nrm, axis=0, keepdims=True)
                dn = dyg * ng_ref[:, ls]
                do_s[:, ls] = r * (dn - nrm * jnp.mean(dn * nrm, axis=-1, keepdims=True))
            qm, km, qs, kl = (t[n].astype(MXU_DTYPE) for n in ("qm", "km", "qs", "kl"))
            v = hi_ref[rows, :].astype(MXU_DTYPE)
            do = do_s[...].astype(MXU_DTYPE)
            heads = [slice(h * HGRN_K, (h + 1) * HGRN_K) for h in range(HGRN_HEADS)]
            a_all = [jnp.where(causal, _nt(qm[:, ls], km[:, ls]), 0.0).astype(MXU_DTYPE) for ls in heads]
            da_all = [jnp.where(causal, _nt(do[:, ls], v[:, ls]), 0.0).astype(MXU_DTYPE) for ls in heads]
            for h, ls in enumerate(heads):
                st = st_ref[c, ls, :]
                dst = dst_ref[h]
                a, da = a_all[h], da_all[h]
                dv_s[:, ls] = _tn(a, do[:, ls]) + _nt(kl[:, ls], dst)
                dkl_s[:, ls] = _nn(v[:, ls], dst)
                dqs_s[:, ls] = _nn(do[:, ls], st)
                del_s[:, ls] = jnp.sum(dst * st, axis=0, keepdims=True)
                dst_ref[h] = _tn(do[:, ls], qs[:, ls]) + t["el"][:, ls] * dst
                dqm_s[:, ls] = _nn(da, km[:, ls])
                dkm_s[:, ls] = _tn(da, qm[:, ls])
            dqm, dkm, dqs, dkl = dqm_s[...], dkm_s[...], dqs_s[...], dkl_s[...]
            dq = dqm * t["e_qm"] + dqs * t["e_qs"]
            dk = dkm * t["e_km"] + dkl * t["e_kl"]
            t_qm, t_km, t_kl = dqm * t["qm"], dkm * t["km"], dkl * t["kl"]
            db = t_qm - t_km + dqs * t["qs"] - t_kl
            db_mid = jnp.sum(t_km - t_qm, axis=0, keepdims=True)
            db_last = jnp.sum(t_kl, axis=0, keepdims=True) + del_s[...] * t["el"]
            db = db + jnp.where(row == CHUNK // 2 - 1, db_mid, 0.0) + jnp.where(row == CHUNK - 1, db_last, 0.0)
            dlogf = _chunk_sum(anti, db)
            sq, sg, sgn, f = t["sq"], t["sg"], t["sgn"], t["f"]
            dph_ref[rows, 0:D_MODEL] = (dq * (sq * (1.0 + hq * (1.0 - sq)))).astype(dph_ref.dtype)
            dph_ref[rows, D_MODEL:2 * D_MODEL] = (
                dlogf * (1.0 - lb) * sg * (1.0 - sg) / f - dk * (1.0 - lb) * sgn * (1.0 - sgn)).astype(dph_ref.dtype)
            dph_ref[rows, 2 * D_MODEL:3 * D_MODEL] = dv_s[...].astype(dph_ref.dtype)
            dlb_ref[...] += jnp.sum(dlogf * (1.0 - sg) / f - dk * sgn, axis=0, keepdims=True)

        @pl.when(i == nblk - 1)
        def _():
            dl0 = dlb_ref[...] * lb * (1.0 - lb)
            dl_ref[0:1, :] = dl0
            dl_ref[1:2, :] = -dl0

    wide = pltpu.VMEM((CHUNK, D_MODEL), F32)
    return _call(
        "hgrn_bwd", body, grid=(nblk,),
        in_specs=[col(0), col(1), col(2), col(3), tok,
                  pl.BlockSpec((cpb, D_MODEL, HGRN_K), lambda i: (rev(i), 0, 0)), tok,
                  pl.BlockSpec((2, D_MODEL), lambda i: (0, 0)), pl.BlockSpec((1, D_MODEL), lambda i: (0, 0)), HBM_SPEC],
        out_specs=[pl.BlockSpec((pl.Element(HGRN_TOKENS), pl.Element(4 * D_MODEL)), lambda i: (
                       pl.multiple_of(rev(i) * HGRN_TOKENS, ROW_ALIGN), first)),
                   pl.BlockSpec((1, D_MODEL), lambda i: (0, 0)), pl.BlockSpec((2, D_MODEL), lambda i: (0, 0))],
        out_shape=[jax.ShapeDtypeStruct(d_in.shape, d_in.dtype), jax.ShapeDtypeStruct((1, D_MODEL), F32),
                   jax.ShapeDtypeStruct((2, D_MODEL), F32)],
        scratch=[pltpu.VMEM((HGRN_HEADS, HGRN_K, HGRN_K), F32), pltpu.VMEM((1, D_MODEL), F32),
                 wide, wide, wide, wide, wide, wide, pltpu.VMEM((1, D_MODEL), F32)],
        args=[ph, ph, ph, ph, o_raw, states, dy, lb_logits, norm_g, d_in], semantics=("arbitrary",), comm=comm,
        aliases={9: 0})


def _local_step(x, target, vec, net):
    T, D = x.shape
    norm_mix_g, b_in, sinks, lb_logits = vec["norm_mix_g"], vec["b_in"], vec["attn_sinks"], vec["hgrn_lb_logits"]
    hgrn_norm_g, norm_ffn_g, norm_final_g = vec["hgrn_norm_g"], vec["norm_ffn_g"], vec["norm_final_g"]
    w_in = net.full("w_in")
    o_q, o_kv, o_h, o_g = (sum(IN_SPLITS[:i]) for i in range(4))
    TM = 512

    first = ("w_branch_attn", "w_branch_hgrn", "w_ffn_down")
    (u, pq, pkv, ph, pg), got = _in_proj(x, norm_mix_g, w_in, b_in, tm=512, comm=net.fetch(first))
    net.fetched(first, got)
    second = ("w_ffn_gate", "w_out")
    (y_attn, lse), got = _attn_fwd(pq, pkv, sinks, comm=_join(net.relay(first), net.fetch(second)))
    net.gathered(first, got[:len(first)])
    net.fetched(second, got[len(first):])
    third = ("w_ffn_up",)
    (y_hgrn, o_raw, states), got = _hgrn_fwd(ph, lb_logits, hgrn_norm_g,
                                             comm=_join(net.relay(second), net.fetch(third)))
    net.gathered(second, got[:len(second)])
    net.fetched(third, got[len(second):])
    w_ba, w_bh = net.full("w_branch_attn"), net.full("w_branch_hgrn")
    (ya, yb, merged), got = _merge_fwd(y_attn, y_hgrn, w_ba, w_bh, pg, tm=TM, comm=net.relay(third))
    net.gathered(third, got)
    w_gate, w_up, w_out = net.full("w_ffn_gate"), net.full("w_ffn_up"), net.full("w_out")

    (h1, u2, gpre, up, z), _ = _ffn_fwd(merged, w_out, x, norm_ffn_g, w_gate, w_up, tm=512)
    w_down = net.full("w_ffn_down")

    dh2, dh2b, dgp, dup, dgf_p, loss_p = _ffn_tail(z, w_down, h1, target, norm_final_g, gpre, up, tm=512)
    loss = jnp.sum(loss_p.reshape(-1, 8, D)[:, 0, 0])
    d_norm_final = _colsum_partials(dgf_p)
    d_w_down = _weight_grad("dw_down", z, dh2b, tm=DW_ROWS, tk=T)

    names, swap = ("w_ffn_down",), ()
    (dh1, dh1b, dg2_p), got = _ffn_in_bwd(dgp, dup, w_gate, w_up, h1, norm_ffn_g, dh2, tm=256,
                                          comm=net.exchange(dict(w_ffn_down=[d_w_down]), swap))
    net.received(names, swap, got)
    d_norm_ffn = _colsum_partials(dg2_p)
    d_w_gate = _weight_grad("dw_gate", dgp, u2, tm=DW_ROWS, tk=T)
    d_w_up = _weight_grad("dw_up", dup, u2, tm=DW_ROWS, tk=T)

    rows_in = sum(IN_SPLITS)
    dya, dyb, dy_attn, dy_hgrn, d_in = _merge_bwd(dh1b, w_out, w_ba, w_bh, ya, yb, pg, tm=TM, d_in_width=rows_in,
                                                  first=o_g)
    d_w_out = _weight_grad("dw_out", merged, dh1b, tm=1024, tk=1024)
    d_w_ba = _weight_grad("dw_branch_a", y_attn, dya, tm=1024, tk=1024)
    d_w_bh = _weight_grad("dw_branch_b", y_hgrn, dyb, tm=1024, tk=1024)

    names, swap = ("w_ffn_gate",), ("w_ffn_down",)
    (d_in, dsink), got = _attn_bwd(pq, pkv, sinks, lse, dy_attn, d_in,
                                   comm=net.exchange(dict(w_ffn_gate=[d_w_gate]), swap))
    net.received(names, swap, got)
    names, swap = ("w_ffn_up", "w_out"), ("w_ffn_gate",)
    (d_in, d_hgrn_norm, d_lb_logits), got = _hgrn_bwd(
        ph, o_raw, states, dy_hgrn, lb_logits, hgrn_norm_g, d_in, o_h,
        comm=net.exchange(dict(w_ffn_up=[d_w_up], w_out=[d_w_out]), swap))
    net.received(names, swap, got)

    names, swap = ("w_branch_attn", "w_branch_hgrn"), ("w_ffn_up", "w_out")
    (*d_w_in, d_b_in), got = _weight_grad(
        "dw_in", d_in, u, tm=DW_ROWS, tk=T, a_colsum=True, carrying=True,
        comm=net.exchange(dict(w_branch_attn=[d_w_ba], w_branch_hgrn=[d_w_bh]), swap))
    net.received(names, swap, got)
    d_w_in = [tuple(d_w_in)]

    first_level = net.presum_begin("w_in", d_w_in)
    halves = net.presum_end("w_in", [] if first_level is None else _copies_alone("presum_swap_w_in", first_level))
    names, swap = ("w_in",), ()
    (dx, dg1_p), got = _in_proj_bwd([(d_in, 0)], w_in, x, norm_mix_g, dh1, tm=256, comm=halves)
    net.last = (names, swap, got)
    d_norm_mix = _colsum_partials(dg1_p)
    vecs = dict(norm_mix_g=d_norm_mix, b_in=d_b_in, attn_sinks=jnp.sum(dsink.reshape(Q_HEADS, ATTN_BLOCK), axis=1).reshape(1, Q_HEADS),
                hgrn_lb_logits=d_lb_logits,
                hgrn_norm_g=d_hgrn_norm, norm_ffn_g=d_norm_ffn, norm_final_g=d_norm_final)
    return loss, dx, vecs


def _place():
    return lax.axis_index("x"), lax.axis_index("y"), lax.axis_index("c")


def _other_chips(x, y):
    return [(1 - x, y), (x, 1 - y), (1 - x, 1 - y)]


def _y_first(copies):
    return [copies[3 * (i // 3) + (1, 0, 2)[i % 3]] for i in range(len(copies))]


def _fetch_copies(shards):
    n = len(shards)

    def build(ins, outs, send_sems, recv_sems, local_sems, *stages):
        x, y, c = _place()
        mine = 2 * x + y
        local = [(pltpu.make_async_copy(ins[w], stages[w], local_sems.at[2 * w]),
                  pltpu.make_async_copy(stages[w], outs[w].at[mine], local_sems.at[2 * w + 1])) for w in range(n)]
        sends, recvs = [], []
        for w in range(n):
            half = shards[w].shape[0] // 2
            rows = pl.ds(c * half, half)
            for k, (px, py) in enumerate(_other_chips(x, y)):
                sem = 3 * w + k
                sends.append(pltpu.make_async_remote_copy(
                    src_ref=ins[w].at[rows], dst_ref=outs[w].at[mine, rows], send_sem=send_sems.at[sem],
                    recv_sem=recv_sems.at[sem], device_id=(px, py, c), device_id_type=MESH_ID))
                recvs.append(pltpu.make_async_remote_copy(
                    src_ref=ins[w].at[rows], dst_ref=outs[w].at[2 * px + py, rows], send_sem=send_sems.at[sem],
                    recv_sem=recv_sems.at[sem], device_id=(px, py, c), device_id_type=MESH_ID))
        return sends, recvs, local, _y_first(sends)

    return _Carried(shards, [jax.ShapeDtypeStruct((N_CHIPS,) + s.shape, s.dtype) for s in shards], 3 * n, 2 * n, build,
                    stages=[pltpu.VMEM(s.shape, s.dtype) for s in shards])


def _relay_copies(fetched):
    n = len(fetched)

    def build(ins, outs, send_sems, recv_sems, local_sems):
        x, y, c = _place()
        sends, recvs = [], []
        for w in range(n):
            half = fetched[w].shape[1] // 2
            mine, theirs = pl.ds(c * half, half), pl.ds((1 - c) * half, half)
            for k, (px, py) in enumerate(_other_chips(x, y)):
                block, sem = 2 * px + py, 3 * w + k
                sends.append(pltpu.make_async_remote_copy(
                    src_ref=ins[w].at[block, mine], dst_ref=outs[w].at[block, mine], send_sem=send_sems.at[sem],
                    recv_sem=recv_sems.at[sem], device_id=(x, y, 1 - c), device_id_type=MESH_ID))
                recvs.append(pltpu.make_async_remote_copy(
                    src_ref=ins[w].at[block, theirs], dst_ref=outs[w].at[block, theirs], send_sem=send_sems.at[sem],
                    recv_sem=recv_sems.at[sem], device_id=(x, y, 1 - c), device_id_type=MESH_ID))
        return sends, recvs, []

    return _Carried(fetched, [jax.ShapeDtypeStruct(f.shape, f.dtype) for f in fetched], 3 * n, 0, build,
                    continued={w: w for w in range(n)})


def _grad_copies(stacked):
    n = len(stacked)

    def build(ins, outs, send_sems, recv_sems, local_sems):
        x, y, c = _place()
        sends = []
        for w in range(n):
            for k, (px, py) in enumerate(_other_chips(x, y)):
                sem = 3 * w + k
                sends.append(pltpu.make_async_remote_copy(
                    src_ref=ins[w].at[2 * px + py], dst_ref=outs[w].at[k], send_sem=send_sems.at[sem],
                    recv_sem=recv_sems.at[sem], device_id=(px, py, c), device_id_type=MESH_ID))
        return sends, sends, [], _y_first(sends)

    return _Carried(stacked, [jax.ShapeDtypeStruct((3,) + s.shape[1:], s.dtype) for s in stacked], 3 * n, 0, build)


def _small_copies(small):
    def build(ins, outs, send_sems, recv_sems, local_sems):
        small_ref, all_ref = ins[0], outs[0]
        x, y, c = _place()
        me = 4 * x + 2 * y + c
        sends, recvs = [], []
        for r in range(1, 8):
            px = 1 - x if r & 4 else x
            py = 1 - y if r & 2 else y
            pc = 1 - c if r & 1 else c
            sends.append(pltpu.make_async_remote_copy(
                src_ref=small_ref, dst_ref=all_ref.at[me], send_sem=send_sems.at[r - 1], recv_sem=recv_sems.at[r - 1],
                device_id=(px, py, pc), device_id_type=MESH_ID))
            recvs.append(pltpu.make_async_remote_copy(
                src_ref=small_ref, dst_ref=all_ref.at[4 * px + 2 * py + pc], send_sem=send_sems.at[r - 1],
                recv_sem=recv_sems.at[r - 1], device_id=(px, py, pc), device_id_type=MESH_ID))
        return sends, recvs, [pltpu.make_async_copy(small_ref, all_ref.at[me], local_sems.at[0])]

    return _Carried([small], [jax.ShapeDtypeStruct((8,) + small.shape, small.dtype)], 7, 1, build)


def _gather_by_neighbours(name, shard):
    half = shard.shape[0] // 2
    quarter = half // 2

    def body(in_ref, out_ref, send_sems, recv_sems, local_sem, stage_ref):
        for core in (0, 1):
            @pl.when(lax.axis_index("c") == core)
            def _():
                program(core, in_ref, out_ref, send_sems, recv_sems, local_sem, stage_ref)

    def program(c, in_ref, out_ref, send_sems, recv_sems, local_sem, stage_ref):
        x, y, _ = _place()
        chip = lambda px, py: 2 * px + py
        to_x, to_y, sibling = (1 - x, y, c), (x, 1 - y, c), (x, y, 1 - c)
        x_blk, y_blk, d_blk = chip(1 - x, y), chip(x, 1 - y), chip(1 - x, 1 - y)
        mine, theirs = c * half, (1 - c) * half

        def copy(sem, rows, block, to, src=None):
            place = out_ref.at[block, pl.ds(rows[0], rows[1])]
            return pltpu.make_async_remote_copy(
                src_ref=place if src is None else src, dst_ref=place, send_sem=send_sems.at[sem],
                recv_sem=recv_sems.at[sem], device_id=to, device_id_type=MESH_ID)

        stage = pltpu.make_async_copy(in_ref, stage_ref, local_sem.at[0])
        own = pltpu.make_async_copy(stage_ref, out_ref.at[chip(x, y)], local_sem.at[1])
        my_rows = in_ref.at[pl.ds(mine, half)]
        along_x = dict(send=copy(0, (mine, half), chip(x, y), to_x, src=my_rows),
                       landed=copy(0, (mine, half), x_blk, to_x),
                       onward=[copy(3, (mine + quarter, quarter), x_blk, to_y), copy(4, (mine, half), x_blk, sibling)],
                       diagonal=copy(2, (mine, quarter), d_blk, to_x))
        along_y = dict(send=copy(1, (mine, half), chip(x, y), to_y, src=my_rows),
                       landed=copy(1, (mine, half), y_blk, to_y),
                       onward=[copy(2, (mine, quarter), y_blk, to_x), copy(5, (mine, half), y_blk, sibling)],
                       diagonal=copy(3, (mine + quarter, quarter), d_blk, to_y))
        last = copy(6, (mine, half), d_blk, sibling)

        order = (along_x, along_y) if c == 0 else (along_y, along_x)
        for axis in order:
            axis["send"].start()
        stage.start()
        stage.wait()
        own.start()
        for axis in order:
            axis["landed"].wait_recv()
            for cp in axis["onward"]:
                cp.start()
        for axis in order:
            axis["diagonal"].wait_recv()
        last.start()
        for sem, block in ((4, x_blk), (5, y_blk), (6, d_blk)):
            copy(sem, (theirs, half), block, sibling).wait_recv()
        for cp in [along_x["send"], along_y["send"]] + along_x["onward"] + along_y["onward"] + [last]:
            cp.wait_send()
        own.wait()

    return pl.pallas_call(
        body, name=name, in_specs=[HBM_SPEC], out_specs=HBM_SPEC,
        out_shape=jax.ShapeDtypeStruct((N_CHIPS,) + shard.shape, shard.dtype),
        scratch_shapes=[pltpu.SemaphoreType.DMA((7,)), pltpu.SemaphoreType.DMA((7,)), pltpu.SemaphoreType.DMA((2,)),
                        pltpu.VMEM(shard.shape, shard.dtype)],
    )(shard)


def _copies_alone(name, comm):
    return _call(name, lambda: None, grid=(), in_specs=[], out_specs=[], out_shape=[], args=[], comm=comm)[1]


class _Net:
    def __init__(self, shards):
        self.shards = shards
        self.whole, self.partly, self.own, self.theirs, self.sums, self.other = {}, {}, {}, {}, {}, {}
        x, y, _ = _place()
        self.chip = 2 * x + y

    def fetch(self, names):
        return _fetch_copies([self.shards[n] for n in names])

    def fetched(self, names, got):
        self.partly.update(zip(names, got))

    def relay(self, names):
        return _relay_copies([self.partly[n] for n in names])

    def gathered(self, names, got):
        for n, g in zip(names, got):
            self.whole[n] = g.reshape(-1, g.shape[-1])

    def full(self, name):
        return self.whole[name]

    def exchange(self, grads, swap=()):
        stacked = []
        for n, pieces in grads.items():
            (keep, send), = pieces
            self.own[n] = keep
            stacked.append(send.reshape(N_CHIPS, keep.shape[0] // N_CHIPS, send.shape[-1]))
        return _join(_grad_copies(stacked), self.swap(swap))

    def swap(self, names):
        return _sibling_copies([self.sums[n] for n in names]) if names else None

    def presum_begin(self, name, pieces):
        keep = jnp.concatenate([p[0] for p in pieces], axis=0) if len(pieces) > 1 else pieces[0][0]
        send = jnp.concatenate([p[1] for p in pieces], axis=0) if len(pieces) > 1 else pieces[0][1]
        rows = keep.shape[0] // N_CHIPS
        self.held = keep.reshape(N_CHIPS, rows, keep.shape[-1])
        return _half_rows_copies(send.reshape(N_CHIPS, rows, send.shape[-1]))

    def presum_end(self, name, got):
        x, y, c = _place()
        to_send, self.own[name] = _pre_sum("presum_" + name, self.held, got[0], jnp.stack([c, self.chip]))
        return _grad_copies([to_send])

    def received(self, names, swap, got, carried=None):
        self.theirs.update(zip(names, got[:len(names)]))
        self.other.update(zip(swap, got[len(names):]))
        for n in names:
            (self.sums[n],), more = _partial_sum("sum_" + n, self.own[n], self.theirs[n], self.chip, comm=carried)
        return more


def _half_rows_copies(stacked):
    n, rows = stacked.shape[0], stacked.shape[1] // 2

    def build(ins, outs, send_sems, recv_sems, local_sems):
        x, y, c = _place()
        copies = [pltpu.make_async_remote_copy(
            src_ref=ins[0].at[s, pl.ds((1 - c) * rows, rows)], dst_ref=outs[0].at[s], send_sem=send_sems.at[s],
            recv_sem=recv_sems.at[s], device_id=(x, y, 1 - c), device_id_type=MESH_ID) for s in range(n)]
        return copies, copies, []

    return _Carried([stacked], [jax.ShapeDtypeStruct((n, rows, stacked.shape[2]), stacked.dtype)], n, 0, build)


def _pre_sum(name, held, theirs, core_and_chip):
    n, R, C = held.shape
    half = R // 2
    tr = _row_tile(half)
    per_half = half // tr

    def body(place_ref, h_ref, t_ref, send_ref, own_ref):
        total = h_ref[0] + t_ref[0].astype(F32)
        send_ref[0] = total.astype(send_ref.dtype)

        @pl.when(pl.program_id(1) == place_ref[1])
        def _():
            own_ref[...] = total

    return pl.pallas_call(
        body, name=name,
        grid_spec=pltpu.PrefetchScalarGridSpec(
            num_scalar_prefetch=1, grid=(per_half, n),
            in_specs=[pl.BlockSpec((1, tr, C), lambda i, s, place: (s, place[0] * per_half + i, 0)),
                      pl.BlockSpec((1, tr, C), lambda i, s, place: (s, i, 0))],
            out_specs=[pl.BlockSpec((1, tr, C), lambda i, s, place: (s, i, 0)),
                       pl.BlockSpec((tr, C), lambda i, s, place: (i, 0))]),
        out_shape=[jax.ShapeDtypeStruct((n, half, C), MXU_DTYPE), jax.ShapeDtypeStruct((half, C), F32)],
        compiler_params=_params(("arbitrary", "arbitrary")),
    )(core_and_chip, held, theirs)


def _sibling_copies(parts):
    n = len(parts)

    def build(ins, outs, send_sems, recv_sems, local_sems):
        x, y, c = _place()
        copies = [pltpu.make_async_remote_copy(
            src_ref=ins[w], dst_ref=outs[w], send_sem=send_sems.at[w], recv_sem=recv_sems.at[w],
            device_id=(x, y, 1 - c), device_id_type=MESH_ID) for w in range(n)]
        return copies, copies, []

    return _Carried(parts, [jax.ShapeDtypeStruct(p.shape, p.dtype) for p in parts], n, 0, build)


def _row_tile(rows, most=512, sublanes=16):
    return max(t for t in range(sublanes, min(most, rows // 2) + 1, sublanes) if rows % t == 0)


def _partial_sum(name, own, recv, chip, comm=None):
    _, R, C = recv.shape
    tr = _row_tile(R)

    def body(o_ref, r_ref, p_ref):
        p_ref[...] = ((o_ref[...] + r_ref[0].astype(F32)) + r_ref[1].astype(F32)) + r_ref[2].astype(F32)

    if own.shape[0] != R:
        assert comm is None and own.shape[0] == N_CHIPS * R
        total = pl.pallas_call(
            lambda chip_ref, *refs: body(*refs), name=name,
            grid_spec=pltpu.PrefetchScalarGridSpec(
                num_scalar_prefetch=1, grid=(R // tr,),
                in_specs=[pl.BlockSpec((tr, C), lambda i, chip_ref: (chip_ref[0] * (R // tr) + i, 0)),
                          pl.BlockSpec((3, tr, C), lambda i, chip_ref: (0, i, 0))],
                out_specs=pl.BlockSpec((tr, C), lambda i, chip_ref: (i, 0))),
            out_shape=jax.ShapeDtypeStruct((R, C), F32), compiler_params=_params(("parallel",)),
        )(chip.reshape(1), own, recv)
        return [total], []
    return _call(name, body, grid=(R // tr,),
                 in_specs=[pl.BlockSpec((tr, C), lambda i: (i, 0)), pl.BlockSpec((3, tr, C), lambda i: (0, i, 0))],
                 out_specs=[pl.BlockSpec((tr, C), lambda i: (i, 0))], out_shape=[jax.ShapeDtypeStruct((R, C), F32)],
                 args=[own, recv], semantics=("parallel",), comm=comm)


def _adam_vals(w, g, m, v):
    m = ADAM_B1 * m + (1.0 - ADAM_B1) * g
    v = ADAM_B2 * v + (1.0 - ADAM_B2) * (g * g)
    m_hat = m / (1.0 - ADAM_B1 ** ADAM_STEP)
    v_hat = v / (1.0 - ADAM_B2 ** ADAM_STEP)
    delta = -ADAM_LR * (m_hat / (jnp.sqrt(v_hat) + ADAM_EPS) + ADAM_WD * w)
    return delta, m, v


def _adamw(name, w, m, v, mine, other, comm=None):
    R, C = w.shape
    tr = _row_tile(R)

    def body(w_ref, m_ref, v_ref, s_ref, n_ref, g_ref, d_ref, nm_ref, nv_ref):
        g = s_ref[...] + n_ref[...]
        d, nm, nv = _adam_vals(w_ref[...], g, m_ref[...], v_ref[...])
        g_ref[...], d_ref[...], nm_ref[...], nv_ref[...] = g, d, nm, nv

    spec = pl.BlockSpec((tr, C), lambda i: (i, 0))
    return _call(name, body, grid=(R // tr,), in_specs=[spec] * 5, out_specs=[spec] * 4,
                 out_shape=[jax.ShapeDtypeStruct((R, C), F32)] * 4, args=[w, m, v, mine, other],
                 semantics=("parallel",), comm=comm)


def _adamw_by_halves(name, w, m, v, mine, other, core):
    R, C = w.shape
    tr = _row_tile(R // 2)
    per_half = R // 2 // tr

    def body(c_ref, w_ref, m_ref, v_ref, s_ref, n_ref, g_ref, d_ref, nm_ref, nv_ref):
        g = jnp.where(pl.program_id(0) // per_half == c_ref[0, 0], s_ref[...], n_ref[...])
        d, nm, nv = _adam_vals(w_ref[...], g, m_ref[...], v_ref[...])
        g_ref[...], d_ref[...], nm_ref[...], nv_ref[...] = g, d, nm, nv

    spec = pl.BlockSpec((tr, C), lambda i: (i, 0))
    part = pl.BlockSpec((tr, C), lambda i: (i % per_half, 0))
    return pl.pallas_call(
        body, name=name, grid=(R // tr,),
        in_specs=[pl.BlockSpec(memory_space=pltpu.SMEM), spec, spec, spec, part, part], out_specs=[spec] * 4,
        out_shape=[jax.ShapeDtypeStruct((R, C), F32)] * 4, compiler_params=_params(("parallel",)),
    )(core, w, m, v, mine, other)


SMALL_LAYOUT = dict(norm_mix_g=(0, 1, 1024), b_in=(1, 8, 7424), hgrn_norm_g=(9, 1, 1024), norm_ffn_g=(10, 1, 1024),
                    norm_final_g=(11, 1, 1024), hgrn_lb_logits=(12, 2, 2048), attn_sinks=(14, 1, 16))
SMALL_LOSS_ROW, SMALL_ROWS = 15, 16


def _pack_small(grads, loss):
    rows = [jnp.pad(grads[name].astype(F32).reshape(-1), (0, nrows * D_MODEL - n))
            for name, (_, nrows, n) in SMALL_LAYOUT.items()]
    rows.append(jnp.pad(loss.astype(F32).reshape(1), (0, D_MODEL - 1)))
    return jnp.concatenate(rows).reshape(SMALL_ROWS, D_MODEL)


def _adamw_small(w, m, v, g_all):
    names = list(SMALL_LAYOUT)
    n = len(names)

    def body(a_ref, *refs):
        ins, outs = refs[:3 * n], refs[3 * n:]
        g_all_rows = a_ref[0]
        for dev in range(1, 8):
            g_all_rows = g_all_rows + a_ref[dev]
        for i, name in enumerate(names):
            first, nrows, count = SMALL_LAYOUT[name]
            w_ref, m_ref, v_ref = ins[3 * i:3 * i + 3]
            if w_ref.shape[0] == nrows:
                g = g_all_rows[first:first + nrows, :w_ref.shape[1]]
            else:
                last = count - (nrows - 1) * D_MODEL
                g = jnp.concatenate([g_all_rows[r:r + 1, :] for r in range(first, first + nrows - 1)]
                                    + [g_all_rows[first + nrows - 1:first + nrows, :last]], axis=1)
            d, nm, nv = _adam_vals(w_ref[...], g, m_ref[...], v_ref[...])
            for o_ref, val in zip(outs[4 * i:4 * i + 4], (g, d, nm, nv)):
                o_ref[...] = val
        outs[4 * n][...] = g_all_rows[SMALL_LOSS_ROW:SMALL_LOSS_ROW + 1, 0:1]

    res = pl.pallas_call(
        body, name="adamw_small",
        out_shape=[jax.ShapeDtypeStruct(w[name].shape, F32) for name in names for _ in range(4)]
        + [jax.ShapeDtypeStruct((1, 1), F32)],
    )(g_all, *[t[name] for name in names for t in (w, m, v)])
    return {name: res[4 * i:4 * i + 4] for i, name in enumerate(names)}, res[4 * n]


MATRICES = ("w_in", "w_branch_attn", "w_branch_hgrn", "w_out", "w_ffn_gate", "w_ffn_up", "w_ffn_down")
COLUMN_SHARDED = ("w_in", "w_ffn_gate", "w_ffn_up")
WEIGHTS = ("norm_mix_g", "w_in", "b_in", "attn_sinks", "hgrn_lb_logits", "hgrn_norm_g", "w_branch_attn",
           "w_branch_hgrn", "w_out", "norm_ffn_g", "w_ffn_gate", "w_ffn_up", "w_ffn_down", "norm_final_g")


def kernel(x, norm_mix_g, w_in, b_in, attn_sinks, hgrn_lb_logits, hgrn_norm_g, w_branch_attn, w_branch_hgrn, w_out, norm_ffn_g, w_ffn_gate, w_ffn_up, w_ffn_down, norm_final_g, loss_target, m_norm_mix_g, m_w_in, m_b_in, m_attn_sinks, m_hgrn_lb_logits, m_hgrn_norm_g, m_w_branch_attn, m_w_branch_hgrn, m_w_out, m_norm_ffn_g, m_w_ffn_gate, m_w_ffn_up, m_w_ffn_down, m_norm_final_g, v_norm_mix_g, v_w_in, v_b_in, v_attn_sinks, v_hgrn_lb_logits, v_hgrn_norm_g, v_w_branch_attn, v_w_branch_hgrn, v_w_out, v_norm_ffn_g, v_w_ffn_gate, v_w_ffn_up, v_w_ffn_down, v_norm_final_g):
    given = dict(locals())
    w = {n: given[n] for n in WEIGHTS}
    m = {n: given["m_" + n] for n in WEIGHTS}
    v = {n: given["v_" + n] for n in WEIGHTS}

    block = lambda a, n: jnp.transpose(a[0]) if n in COLUMN_SHARDED else a[0]
    unblock = lambda a, n: (jnp.transpose(a) if n in COLUMN_SHARDED else a)[None]
    net = _Net({n: block(w[n], n).astype(MXU_DTYPE) for n in MATRICES})
    net.gathered(("w_in",), [_gather_by_neighbours("gather_w_in", net.shards["w_in"])])
    vec = dict(norm_mix_g=norm_mix_g, b_in=b_in, attn_sinks=attn_sinks, hgrn_lb_logits=hgrn_lb_logits,
               hgrn_norm_g=hgrn_norm_g, norm_ffn_g=norm_ffn_g, norm_final_g=norm_final_g.reshape(1, D_MODEL))
    loss_part, dx, d_vecs = _local_step(x[0], loss_target[0], vec, net)

    late = ("w_branch_attn", "w_branch_hgrn")
    small_all, *others = net.received(*net.last, carried=_join(_small_copies(_pack_small(d_vecs, loss_part)),
                                                               net.swap(late)))
    net.other.update(zip(late, others))
    grads, deltas, new_m, new_v = {}, {}, {}, {}
    for n in ("w_ffn_down", "w_ffn_gate", "w_ffn_up", "w_out", "w_branch_attn", "w_branch_hgrn"):
        res, got = _adamw("adamw_" + n, block(w[n], n), block(m[n], n), block(v[n], n), net.sums[n], net.other[n],
                          comm=net.swap(("w_in",)) if n == "w_ffn_down" else None)
        if n == "w_ffn_down":
            net.other["w_in"], = got
        grads[n], deltas[n], new_m[n], new_v[n] = (unblock(r, n) for r in res)
    n = "w_in"
    res = _adamw_by_halves("adamw_" + n, block(w[n], n), block(m[n], n), block(v[n], n), net.sums[n], net.other[n],
                           _place()[2].reshape(1, 1))
    grads[n], deltas[n], new_m[n], new_v[n] = (unblock(r, n) for r in res)
    rows = lambda t: {n: t[n].reshape(-1, t[n].shape[-1]) for n in SMALL_LAYOUT}
    res, loss = _adamw_small(rows(w), rows(m), rows(v), small_all)
    for n, four in res.items():
        grads[n], deltas[n], new_m[n], new_v[n] = (r.reshape(w[n].shape) for r in four)
    loss = loss.reshape(())
    return (loss, dx[None], *[grads[n] for n in WEIGHTS], *[deltas[n] for n in WEIGHTS],
            *[new_m[n] for n in WEIGHTS], *[new_v[n] for n in WEIGHTS])
```
